```python
import math
import jax, jax.numpy as jnp
from jax import lax
import numpy as np

D_MODEL = 2048
BATCH = 8
SEQ = 2048
DEPTH = 1

MIX_WIDTH = D_MODEL
SSM_WIDTH = MIX_WIDTH // 2
SSM_GROUP = 16
SSM_GROUPS = SSM_WIDTH // SSM_GROUP
SSM_STATE = 64
DT_MIN = 1e-3
DT_MAX = 1e-1
QK_NOPE_DIM = 128
QK_ROPE_DIM = 64
V_HEAD_DIM = 128
MLA_WIDTH = MIX_WIDTH - SSM_WIDTH
MLA_HEADS = MLA_WIDTH // V_HEAD_DIM
Q_LORA_RANK = D_MODEL // 4
KV_LORA_RANK = D_MODEL // 8
ROPE_THETA = 10000.0
Q_BLOCK = 128
IN_WIDTH = SSM_WIDTH + Q_LORA_RANK + KV_LORA_RANK + QK_ROPE_DIM
D_FF = ((8 * D_MODEL // 3 + 255) // 256) * 256
CONV_WIDTH = 3
RMS_EPS = 1e-6

kernel_name = 'hybrid_s5_mla_convffn_layer'


def _rmsnorm(x, w):
    xf = x.astype(jnp.float32)
    y = xf * lax.rsqrt(jnp.mean(xf * xf, axis=-1, keepdims=True) + RMS_EPS)
    return (y * w.astype(jnp.float32)).astype(x.dtype)


def _rope_tables(positions, dtype):
    inv_freq = ROPE_THETA ** (-jnp.arange(0, QK_ROPE_DIM, 2, dtype=jnp.float32) / QK_ROPE_DIM)
    ang = positions.astype(jnp.float32)[..., None] * inv_freq
    return jnp.cos(ang).astype(dtype), jnp.sin(ang).astype(dtype)


def _rope(x, cos, sin):
    x1, x2 = jnp.split(x, 2, axis=-1)
    return jnp.concatenate([x1 * cos - x2 * sin, x1 * sin + x2 * cos], axis=-1)


def _ssm_combine(left, right):
    ar1, ai1, br1, bi1 = left
    ar2, ai2, br2, bi2 = right
    ar = ar2 * ar1 - ai2 * ai1
    ai = ar2 * ai1 + ai2 * ar1
    br = ar2 * br1 - ai2 * bi1 + br2
    bi = ar2 * bi1 + ai2 * br1 + bi2
    return ar, ai, br, bi


def _s5_group(u, lam_re, lam_im, log_dt, b_re, b_im, c_re, c_im, d_skip, w_glu, b_glu):
    bsz, seq, _ = u.shape
    ug = u.astype(jnp.float32).reshape(bsz, seq, SSM_GROUPS, SSM_GROUP)
    lr = lam_re.astype(jnp.float32)
    li = lam_im.astype(jnp.float32)
    dt = jnp.exp(log_dt.astype(jnp.float32))[:, None]
    mag = jnp.exp(lr * dt)
    abar_re = mag * jnp.cos(li * dt)
    abar_im = mag * jnp.sin(li * dt)
    nr, ni = abar_re - 1.0, abar_im
    den = lr * lr + li * li
    zr = (nr * lr + ni * li) / den
    zi = (ni * lr - nr * li) / den
    bre = b_re.astype(jnp.float32)
    bim = b_im.astype(jnp.float32)
    bbar_re = zr[..., None] * bre - zi[..., None] * bim
    bbar_im = zr[..., None] * bim + zi[..., None] * bre
    bu_re = jnp.einsum('blgh,gph->lbgp', ug, bbar_re)
    bu_im = jnp.einsum('blgh,gph->lbgp', ug, bbar_im)
    a_re = jnp.broadcast_to(abar_re, (seq, 1, SSM_GROUPS, SSM_STATE))
    a_im = jnp.broadcast_to(abar_im, (seq, 1, SSM_GROUPS, SSM_STATE))
    _, _, s_re, s_im = lax.associative_scan(_ssm_combine, (a_re, a_im, bu_re, bu_im), axis=0)
    y = (jnp.einsum('lbgp,ghp->blgh', s_re, c_re.astype(jnp.float32))
         - jnp.einsum('lbgp,ghp->blgh', s_im, c_im.astype(jnp.float32))
         + d_skip.astype(jnp.float32).reshape(SSM_GROUPS, SSM_GROUP) * ug)
    y = jax.nn.gelu(y.reshape(bsz, seq, SSM_WIDTH)).astype(u.dtype)
    return y * jax.nn.sigmoid(y @ w_glu + b_glu)


def _mla_group(c_q, c_kv, k_pe, positions, q_norm_w, w_uq, kv_norm_w, w_ukv):
    bsz, seq, _ = c_q.shape
    q = (_rmsnorm(c_q, q_norm_w) @ w_uq).reshape(bsz, seq, MLA_HEADS, QK_NOPE_DIM + QK_ROPE_DIM)
    q_nope, q_pe = q[..., :QK_NOPE_DIM], q[..., QK_NOPE_DIM:]
    kv = (_rmsnorm(c_kv, kv_norm_w) @ w_ukv).reshape(bsz, seq, MLA_HEADS, QK_NOPE_DIM + V_HEAD_DIM)
    k_nope, v = kv[..., :QK_NOPE_DIM], kv[..., QK_NOPE_DIM:]
    cos, sin = _rope_tables(positions, q.dtype)
    q_pe = _rope(q_pe, cos[:, :, None, :], sin[:, :, None, :])
    k_pe = _rope(k_pe, cos, sin)
    scale = (QK_NOPE_DIM + QK_ROPE_DIM) ** -0.5
    neg = jnp.finfo(jnp.float32).min
    outs = []
    for blk in range(seq // Q_BLOCK):
        q0 = blk * Q_BLOCK
        kend = q0 + Q_BLOCK
        s = (jnp.einsum('bqhd,bkhd->bhqk', q_nope[:, q0:kend], k_nope[:, :kend])
             + jnp.einsum('bqhr,bkr->bhqk', q_pe[:, q0:kend], k_pe[:, :kend]))
        s = s.astype(jnp.float32) * scale
        causal = jnp.arange(kend)[None, :] <= (q0 + jnp.arange(Q_BLOCK))[:, None]
        s = jnp.where(causal, s, neg)
        p = jax.nn.softmax(s, axis=-1).astype(v.dtype)
        outs.append(jnp.einsum('bhqk,bkhd->bqhd', p, v[:, :kend]))
    o = jnp.concatenate(outs, axis=1)
    return o.reshape(bsz, seq, MLA_WIDTH)


def _conv_ffn(h, w_up, conv_w, conv_b, w_down):
    a = h @ w_up
    a = lax.conv_general_dilated(a, conv_w[:, None, :], window_strides=(1,),
                                 padding=[(CONV_WIDTH - 1, 0)],
                                 dimension_numbers=('NWC', 'WIO', 'NWC'),
                                 feature_group_count=2 * D_FF) + conv_b
    gate, val = jnp.split(a, 2, axis=-1)
    return (jax.nn.silu(gate) * val) @ w_down


def _fwd_setup_inputs(seed: int = 0) -> dict:
    key = jax.random.key(seed)
    ks = jax.random.split(key, 32)
    f32 = jnp.float32

    def nrm(k, shape, scale):
        return jax.random.normal(k, (DEPTH,) + shape, f32) * scale

    def gain(k, n):
        return 1.0 + 0.02 * jax.random.normal(k, (DEPTH, n), f32)

    x = jax.random.normal(ks[0], (BATCH, SEQ, D_MODEL), f32)
    offs = jax.random.randint(ks[1], (BATCH, 1), 0, 1024, dtype=jnp.int32)
    positions = offs + jnp.arange(SEQ, dtype=jnp.int32)[None, :]
    lam_re = -0.5 + 0.01 * jax.random.normal(ks[4], (DEPTH, SSM_GROUPS, SSM_STATE), f32)
    lam_im = (math.pi * jnp.arange(SSM_STATE, dtype=f32))[None, None, :] + 0.01 * jax.random.normal(ks[5], (DEPTH, SSM_GROUPS, SSM_STATE), f32)
    log_dt = jax.random.uniform(ks[6], (DEPTH, SSM_GROUPS), f32, math.log(DT_MIN), math.log(DT_MAX))
    return {
        'x': x,
        'positions': positions,
        'attn_norm_w': gain(ks[2], D_MODEL),
        'w_in': nrm(ks[3], (D_MODEL, IN_WIDTH), D_MODEL ** -0.5),
        'ssm_lambda_re': lam_re,
        'ssm_lambda_im': lam_im,
        'ssm_log_dt': log_dt,
        'ssm_b_re': nrm(ks[7], (SSM_GROUPS, SSM_STATE, SSM_GROUP), (2 * SSM_GROUP) ** -0.5),
        'ssm_b_im': nrm(ks[8], (SSM_GROUPS, SSM_STATE, SSM_GROUP), (2 * SSM_GROUP) ** -0.5),
        'ssm_c_re': nrm(ks[9], (SSM_GROUPS, SSM_GROUP, SSM_STATE), (2 * SSM_STATE) ** -0.5),
        'ssm_c_im': nrm(ks[10], (SSM_GROUPS, SSM_GROUP, SSM_STATE), (2 * SSM_STATE) ** -0.5),
        'ssm_d': nrm(ks[11], (SSM_WIDTH,), 1.0),
        'ssm_w_glu': nrm(ks[12], (SSM_WIDTH, SSM_WIDTH), SSM_WIDTH ** -0.5),
        'ssm_b_glu': nrm(ks[13], (SSM_WIDTH,), 0.01),
        'mla_q_norm_w': gain(ks[14], Q_LORA_RANK),
        'mla_w_uq': nrm(ks[15], (Q_LORA_RANK, MLA_HEADS * (QK_NOPE_DIM + QK_ROPE_DIM)), Q_LORA_RANK ** -0.5),
        'mla_kv_norm_w': gain(ks[16], KV_LORA_RANK),
        'mla_w_ukv': nrm(ks[17], (KV_LORA_RANK, MLA_HEADS * (QK_NOPE_DIM + V_HEAD_DIM)), KV_LORA_RANK ** -0.5),
        'ssm_out_norm_w': gain(ks[18], SSM_WIDTH),
        'mla_out_norm_w': gain(ks[19], MLA_WIDTH),
        'w_out': nrm(ks[20], (MIX_WIDTH, D_MODEL), MIX_WIDTH ** -0.5),
        'ffn_norm_w': gain(ks[21], D_MODEL),
        'ffn_w_up': nrm(ks[22], (D_MODEL, 2 * D_FF), D_MODEL ** -0.5),
        'ffn_conv_w': nrm(ks[23], (CONV_WIDTH, 2 * D_FF), CONV_WIDTH ** -0.5),
        'ffn_conv_b': nrm(ks[24], (2 * D_FF,), 0.01),
        'ffn_w_down': nrm(ks[25], (D_FF, D_MODEL), D_FF ** -0.5),
        'final_norm_w': 1.0 + 0.02 * jax.random.normal(ks[26], (D_MODEL,), f32),
    }


def _fwd_reference(x, positions, attn_norm_w, w_in, ssm_lambda_re, ssm_lambda_im, ssm_log_dt,
              ssm_b_re, ssm_b_im, ssm_c_re, ssm_c_im, ssm_d, ssm_w_glu, ssm_b_glu,
              mla_q_norm_w, mla_w_uq, mla_kv_norm_w, mla_w_ukv, ssm_out_norm_w,
              mla_out_norm_w, w_out, ffn_norm_w, ffn_w_up, ffn_conv_w, ffn_conv_b,
              ffn_w_down, final_norm_w):
    split_at = [SSM_WIDTH, SSM_WIDTH + Q_LORA_RANK, SSM_WIDTH + Q_LORA_RANK + KV_LORA_RANK]
    h = x
    for l in range(DEPTH):
        hn = _rmsnorm(h, attn_norm_w[l])
        proj = hn @ w_in[l]
        u, c_q, c_kv, k_pe = jnp.split(proj, split_at, axis=-1)
        y_ssm = _s5_group(u, ssm_lambda_re[l], ssm_lambda_im[l], ssm_log_dt[l],
                          ssm_b_re[l], ssm_b_im[l], ssm_c_re[l], ssm_c_im[l],
                          ssm_d[l], ssm_w_glu[l], ssm_b_glu[l])
        y_mla = _mla_group(c_q, c_kv, k_pe, positions, mla_q_norm_w[l], mla_w_uq[l],
                           mla_kv_norm_w[l], mla_w_ukv[l])
        y = jnp.concatenate([_rmsnorm(y_ssm, ssm_out_norm_w[l]),
                             _rmsnorm(y_mla, mla_out_norm_w[l])], axis=-1)
        h = h + y @ w_out[l]
        h = h + _conv_ffn(_rmsnorm(h, ffn_norm_w[l]), ffn_w_up[l], ffn_conv_w[l],
                          ffn_conv_b[l], ffn_w_down[l])
    return _rmsnorm(h, final_norm_w)


import jax as _jax
import jax.numpy as _jnp

TWIN_FORMAT = 'train_step'
FWD_PARAMS = ['x', 'positions', 'attn_norm_w', 'w_in', 'ssm_lambda_re', 'ssm_lambda_im', 'ssm_log_dt', 'ssm_b_re', 'ssm_b_im', 'ssm_c_re', 'ssm_c_im', 'ssm_d', 'ssm_w_glu', 'ssm_b_glu', 'mla_q_norm_w', 'mla_w_uq', 'mla_kv_norm_w', 'mla_w_ukv', 'ssm_out_norm_w', 'mla_out_norm_w', 'w_out', 'ffn_norm_w', 'ffn_w_up', 'ffn_conv_w', 'ffn_conv_b', 'ffn_w_down', 'final_norm_w']
TWIN_WEIGHTS = ['attn_norm_w', 'w_in', 'ssm_lambda_re', 'ssm_lambda_im', 'ssm_log_dt', 'ssm_b_re', 'ssm_b_im', 'ssm_c_re', 'ssm_c_im', 'ssm_d', 'ssm_w_glu', 'ssm_b_glu', 'mla_q_norm_w', 'mla_w_uq', 'mla_kv_norm_w', 'mla_w_ukv', 'ssm_out_norm_w', 'mla_out_norm_w', 'w_out', 'ffn_norm_w', 'ffn_w_up', 'ffn_conv_w', 'ffn_conv_b', 'ffn_w_down', 'final_norm_w']
TWIN_DIFF_INPUT = 'x'
TWIN_INPUTS = ['x', 'positions', 'attn_norm_w', 'w_in', 'ssm_lambda_re', 'ssm_lambda_im', 'ssm_log_dt', 'ssm_b_re', 'ssm_b_im', 'ssm_c_re', 'ssm_c_im', 'ssm_d', 'ssm_w_glu', 'ssm_b_glu', 'mla_q_norm_w', 'mla_w_uq', 'mla_kv_norm_w', 'mla_w_ukv', 'ssm_out_norm_w', 'mla_out_norm_w', 'w_out', 'ffn_norm_w', 'ffn_w_up', 'ffn_conv_w', 'ffn_conv_b', 'ffn_w_down', 'final_norm_w', 'loss_target', 'm_attn_norm_w', 'm_w_in', 'm_ssm_lambda_re', 'm_ssm_lambda_im', 'm_ssm_log_dt', 'm_ssm_b_re', 'm_ssm_b_im', 'm_ssm_c_re', 'm_ssm_c_im', 'm_ssm_d', 'm_ssm_w_glu', 'm_ssm_b_glu', 'm_mla_q_norm_w', 'm_mla_w_uq', 'm_mla_kv_norm_w', 'm_mla_w_ukv', 'm_ssm_out_norm_w', 'm_mla_out_norm_w', 'm_w_out', 'm_ffn_norm_w', 'm_ffn_w_up', 'm_ffn_conv_w', 'm_ffn_conv_b', 'm_ffn_w_down', 'm_final_norm_w', 'v_attn_norm_w', 'v_w_in', 'v_ssm_lambda_re', 'v_ssm_lambda_im', 'v_ssm_log_dt', 'v_ssm_b_re', 'v_ssm_b_im', 'v_ssm_c_re', 'v_ssm_c_im', 'v_ssm_d', 'v_ssm_w_glu', 'v_ssm_b_glu', 'v_mla_q_norm_w', 'v_mla_w_uq', 'v_mla_kv_norm_w', 'v_mla_w_ukv', 'v_ssm_out_norm_w', 'v_mla_out_norm_w', 'v_w_out', 'v_ffn_norm_w', 'v_ffn_w_up', 'v_ffn_conv_w', 'v_ffn_conv_b', 'v_ffn_w_down', 'v_final_norm_w']
TWIN_OUTPUTS = ['loss', 'grad_x', 'grad_attn_norm_w', 'grad_w_in', 'grad_ssm_lambda_re', 'grad_ssm_lambda_im', 'grad_ssm_log_dt', 'grad_ssm_b_re', 'grad_ssm_b_im', 'grad_ssm_c_re', 'grad_ssm_c_im', 'grad_ssm_d', 'grad_ssm_w_glu', 'grad_ssm_b_glu', 'grad_mla_q_norm_w', 'grad_mla_w_uq', 'grad_mla_kv_norm_w', 'grad_mla_w_ukv', 'grad_ssm_out_norm_w', 'grad_mla_out_norm_w', 'grad_w_out', 'grad_ffn_norm_w', 'grad_ffn_w_up', 'grad_ffn_conv_w', 'grad_ffn_conv_b', 'grad_ffn_w_down', 'grad_final_norm_w', 'delta_attn_norm_w', 'delta_w_in', 'delta_ssm_lambda_re', 'delta_ssm_lambda_im', 'delta_ssm_log_dt', 'delta_ssm_b_re', 'delta_ssm_b_im', 'delta_ssm_c_re', 'delta_ssm_c_im', 'delta_ssm_d', 'delta_ssm_w_glu', 'delta_ssm_b_glu', 'delta_mla_q_norm_w', 'delta_mla_w_uq', 'delta_mla_kv_norm_w', 'delta_mla_w_ukv', 'delta_ssm_out_norm_w', 'delta_mla_out_norm_w', 'delta_w_out', 'delta_ffn_norm_w', 'delta_ffn_w_up', 'delta_ffn_conv_w', 'delta_ffn_conv_b', 'delta_ffn_w_down', 'delta_final_norm_w', 'new_m_attn_norm_w', 'new_m_w_in', 'new_m_ssm_lambda_re', 'new_m_ssm_lambda_im', 'new_m_ssm_log_dt', 'new_m_ssm_b_re', 'new_m_ssm_b_im', 'new_m_ssm_c_re', 'new_m_ssm_c_im', 'new_m_ssm_d', 'new_m_ssm_w_glu', 'new_m_ssm_b_glu', 'new_m_mla_q_norm_w', 'new_m_mla_w_uq', 'new_m_mla_kv_norm_w', 'new_m_mla_w_ukv', 'new_m_ssm_out_norm_w', 'new_m_mla_out_norm_w', 'new_m_w_out', 'new_m_ffn_norm_w', 'new_m_ffn_w_up', 'new_m_ffn_conv_w', 'new_m_ffn_conv_b', 'new_m_ffn_w_down', 'new_m_final_norm_w', 'new_v_attn_norm_w', 'new_v_w_in', 'new_v_ssm_lambda_re', 'new_v_ssm_lambda_im', 'new_v_ssm_log_dt', 'new_v_ssm_b_re', 'new_v_ssm_b_im', 'new_v_ssm_c_re', 'new_v_ssm_c_im', 'new_v_ssm_d', 'new_v_ssm_w_glu', 'new_v_ssm_b_glu', 'new_v_mla_q_norm_w', 'new_v_mla_w_uq', 'new_v_mla_kv_norm_w', 'new_v_mla_w_ukv', 'new_v_ssm_out_norm_w', 'new_v_mla_out_norm_w', 'new_v_w_out', 'new_v_ffn_norm_w', 'new_v_ffn_w_up', 'new_v_ffn_conv_w', 'new_v_ffn_conv_b', 'new_v_ffn_w_down', 'new_v_final_norm_w']
TWIN_LEAF_KINDS = {'loss': 'loss', 'grad_x': 'grad_x', 'grad_attn_norm_w': 'grad_w', 'grad_w_in': 'grad_w', 'grad_ssm_lambda_re': 'grad_w', 'grad_ssm_lambda_im': 'grad_w', 'grad_ssm_log_dt': 'grad_w', 'grad_ssm_b_re': 'grad_w', 'grad_ssm_b_im': 'grad_w', 'grad_ssm_c_re': 'grad_w', 'grad_ssm_c_im': 'grad_w', 'grad_ssm_d': 'grad_w', 'grad_ssm_w_glu': 'grad_w', 'grad_ssm_b_glu': 'grad_w', 'grad_mla_q_norm_w': 'grad_w', 'grad_mla_w_uq': 'grad_w', 'grad_mla_kv_norm_w': 'grad_w', 'grad_mla_w_ukv': 'grad_w', 'grad_ssm_out_norm_w': 'grad_w', 'grad_mla_out_norm_w': 'grad_w', 'grad_w_out': 'grad_w', 'grad_ffn_norm_w': 'grad_w', 'grad_ffn_w_up': 'grad_w', 'grad_ffn_conv_w': 'grad_w', 'grad_ffn_conv_b': 'grad_w', 'grad_ffn_w_down': 'grad_w', 'grad_final_norm_w': 'grad_w', 'delta_attn_norm_w': 'delta_w', 'delta_w_in': 'delta_w', 'delta_ssm_lambda_re': 'delta_w', 'delta_ssm_lambda_im': 'delta_w', 'delta_ssm_log_dt': 'delta_w', 'delta_ssm_b_re': 'delta_w', 'delta_ssm_b_im': 'delta_w', 'delta_ssm_c_re': 'delta_w', 'delta_ssm_c_im': 'delta_w', 'delta_ssm_d': 'delta_w', 'delta_ssm_w_glu': 'delta_w', 'delta_ssm_b_glu': 'delta_w', 'delta_mla_q_norm_w': 'delta_w', 'delta_mla_w_uq': 'delta_w', 'delta_mla_kv_norm_w': 'delta_w', 'delta_mla_w_ukv': 'delta_w', 'delta_ssm_out_norm_w': 'delta_w', 'delta_mla_out_norm_w': 'delta_w', 'delta_w_out': 'delta_w', 'delta_ffn_norm_w': 'delta_w', 'delta_ffn_w_up': 'delta_w', 'delta_ffn_conv_w': 'delta_w', 'delta_ffn_conv_b': 'delta_w', 'delta_ffn_w_down': 'delta_w', 'delta_final_norm_w': 'delta_w', 'new_m_attn_norm_w': 'new_m', 'new_m_w_in': 'new_m', 'new_m_ssm_lambda_re': 'new_m', 'new_m_ssm_lambda_im': 'new_m', 'new_m_ssm_log_dt': 'new_m', 'new_m_ssm_b_re': 'new_m', 'new_m_ssm_b_im': 'new_m', 'new_m_ssm_c_re': 'new_m', 'new_m_ssm_c_im': 'new_m', 'new_m_ssm_d': 'new_m', 'new_m_ssm_w_glu': 'new_m', 'new_m_ssm_b_glu': 'new_m', 'new_m_mla_q_norm_w': 'new_m', 'new_m_mla_w_uq': 'new_m', 'new_m_mla_kv_norm_w': 'new_m', 'new_m_mla_w_ukv': 'new_m', 'new_m_ssm_out_norm_w': 'new_m', 'new_m_mla_out_norm_w': 'new_m', 'new_m_w_out': 'new_m', 'new_m_ffn_norm_w': 'new_m', 'new_m_ffn_w_up': 'new_m', 'new_m_ffn_conv_w': 'new_m', 'new_m_ffn_conv_b': 'new_m', 'new_m_ffn_w_down': 'new_m', 'new_m_final_norm_w': 'new_m', 'new_v_attn_norm_w': 'new_v', 'new_v_w_in': 'new_v', 'new_v_ssm_lambda_re': 'new_v', 'new_v_ssm_lambda_im': 'new_v', 'new_v_ssm_log_dt': 'new_v', 'new_v_ssm_b_re': 'new_v', 'new_v_ssm_b_im': 'new_v', 'new_v_ssm_c_re': 'new_v', 'new_v_ssm_c_im': 'new_v', 'new_v_ssm_d': 'new_v', 'new_v_ssm_w_glu': 'new_v', 'new_v_ssm_b_glu': 'new_v', 'new_v_mla_q_norm_w': 'new_v', 'new_v_mla_w_uq': 'new_v', 'new_v_mla_kv_norm_w': 'new_v', 'new_v_mla_w_ukv': 'new_v', 'new_v_ssm_out_norm_w': 'new_v', 'new_v_mla_out_norm_w': 'new_v', 'new_v_w_out': 'new_v', 'new_v_ffn_norm_w': 'new_v', 'new_v_ffn_w_up': 'new_v', 'new_v_ffn_conv_w': 'new_v', 'new_v_ffn_conv_b': 'new_v', 'new_v_ffn_w_down': 'new_v', 'new_v_final_norm_w': 'new_v'}


def _forward(args):
    return _fwd_reference(*[args[k] for k in FWD_PARAMS])


def _output_shape():
    out = _jax.eval_shape(lambda: _forward(_fwd_setup_inputs(0)))
    return out.shape, out.dtype

N_MICROBATCH = 1
ADAM_LR = 0.001
ADAM_B1 = 0.9
ADAM_B2 = 0.999
ADAM_EPS = 1e-08
ADAM_WD = 0.01
ADAM_STEP = 10
PER_EXAMPLE_BATCH_AXIS = {'x': 0, 'positions': 0, 'loss_target': 0}
SHARED_INPUTS = []
_WEIGHT_DTYPES = {'attn_norm_w': _jnp.float32, 'w_in': _jnp.float32, 'ssm_lambda_re': _jnp.float32, 'ssm_lambda_im': _jnp.float32, 'ssm_log_dt': _jnp.float32, 'ssm_b_re': _jnp.float32, 'ssm_b_im': _jnp.float32, 'ssm_c_re': _jnp.float32, 'ssm_c_im': _jnp.float32, 'ssm_d': _jnp.float32, 'ssm_w_glu': _jnp.float32, 'ssm_b_glu': _jnp.float32, 'mla_q_norm_w': _jnp.float32, 'mla_w_uq': _jnp.float32, 'mla_kv_norm_w': _jnp.float32, 'mla_w_ukv': _jnp.float32, 'ssm_out_norm_w': _jnp.float32, 'mla_out_norm_w': _jnp.float32, 'w_out': _jnp.float32, 'ffn_norm_w': _jnp.float32, 'ffn_w_up': _jnp.float32, 'ffn_conv_w': _jnp.float32, 'ffn_conv_b': _jnp.float32, 'ffn_w_down': _jnp.float32, 'final_norm_w': _jnp.float32}
MOMENT_SCALE = {'attn_norm_w': 6.473288e-02, 'w_in': 6.915649e-02, 'ssm_lambda_re': 2.270050e-03, 'ssm_lambda_im': 2.649025e-03, 'ssm_log_dt': 2.495059e+00, 'ssm_b_re': 1.634341e-03, 'ssm_b_im': 1.658852e-03, 'ssm_c_re': 3.239219e-03, 'ssm_c_im': 3.283425e-03, 'ssm_d': 5.364586e-02, 'ssm_w_glu': 1.422623e-02, 'ssm_b_glu': 2.277446e-02, 'mla_q_norm_w': 7.089274e-02, 'mla_w_uq': 3.999904e-02, 'mla_kv_norm_w': 1.618976e-01, 'mla_w_ukv': 4.663686e-02, 'ssm_out_norm_w': 5.490191e-02, 'mla_out_norm_w': 5.062263e-02, 'w_out': 5.004621e-02, 'ffn_norm_w': 3.558215e-02, 'ffn_w_up': 1.540771e-02, 'ffn_conv_w': 1.564158e-02, 'ffn_conv_b': 1.620507e-02, 'ffn_w_down': 2.513951e-02, 'final_norm_w': 8.018657e+00}


def _to_microbatches(a, axis):
    t = _jnp.moveaxis(a, axis, 0)
    t = t.reshape((N_MICROBATCH, t.shape[0] // N_MICROBATCH) + t.shape[1:])
    return _jnp.moveaxis(t, 1, axis + 1)


def setup_inputs(seed: int = 0) -> dict:
    inp = _fwd_setup_inputs(seed)
    key = _jax.random.fold_in(_jax.random.key(seed), 7919)
    shape, _ = _output_shape()
    out = dict(inp)
    out["loss_target"] = _jax.random.normal(_jax.random.fold_in(key, 0), shape, _jnp.float32)
    for i, name in enumerate(TWIN_WEIGHTS):
        w = inp[name].astype(_jnp.float32)
        if MOMENT_SCALE is None:
            s = _jnp.sqrt(_jnp.mean(_jnp.square(w)) + 1e-30)
        else:
            s = MOMENT_SCALE[name]
        km, kv = _jax.random.split(_jax.random.fold_in(key, i + 1))
        out[name] = w
        out["m_" + name] = s * _jax.random.normal(km, w.shape, _jnp.float32)
        out["v_" + name] = (s * s) * _jax.random.uniform(kv, w.shape, _jnp.float32, 0.5, 1.5)
    if N_MICROBATCH > 1:
        for name, axis in PER_EXAMPLE_BATCH_AXIS.items():
            out[name] = _to_microbatches(out[name], axis)
    return {'x': out['x'], 'positions': out['positions'], 'attn_norm_w': out['attn_norm_w'], 'w_in': out['w_in'], 'ssm_lambda_re': out['ssm_lambda_re'], 'ssm_lambda_im': out['ssm_lambda_im'], 'ssm_log_dt': out['ssm_log_dt'], 'ssm_b_re': out['ssm_b_re'], 'ssm_b_im': out['ssm_b_im'], 'ssm_c_re': out['ssm_c_re'], 'ssm_c_im': out['ssm_c_im'], 'ssm_d': out['ssm_d'], 'ssm_w_glu': out['ssm_w_glu'], 'ssm_b_glu': out['ssm_b_glu'], 'mla_q_norm_w': out['mla_q_norm_w'], 'mla_w_uq': out['mla_w_uq'], 'mla_kv_norm_w': out['mla_kv_norm_w'], 'mla_w_ukv': out['mla_w_ukv'], 'ssm_out_norm_w': out['ssm_out_norm_w'], 'mla_out_norm_w': out['mla_out_norm_w'], 'w_out': out['w_out'], 'ffn_norm_w': out['ffn_norm_w'], 'ffn_w_up': out['ffn_w_up'], 'ffn_conv_w': out['ffn_conv_w'], 'ffn_conv_b': out['ffn_conv_b'], 'ffn_w_down': out['ffn_w_down'], 'final_norm_w': out['final_norm_w'], 'loss_target': out['loss_target'], 'm_attn_norm_w': out['m_attn_norm_w'], 'm_w_in': out['m_w_in'], 'm_ssm_lambda_re': out['m_ssm_lambda_re'], 'm_ssm_lambda_im': out['m_ssm_lambda_im'], 'm_ssm_log_dt': out['m_ssm_log_dt'], 'm_ssm_b_re': out['m_ssm_b_re'], 'm_ssm_b_im': out['m_ssm_b_im'], 'm_ssm_c_re': out['m_ssm_c_re'], 'm_ssm_c_im': out['m_ssm_c_im'], 'm_ssm_d': out['m_ssm_d'], 'm_ssm_w_glu': out['m_ssm_w_glu'], 'm_ssm_b_glu': out['m_ssm_b_glu'], 'm_mla_q_norm_w': out['m_mla_q_norm_w'], 'm_mla_w_uq': out['m_mla_w_uq'], 'm_mla_kv_norm_w': out['m_mla_kv_norm_w'], 'm_mla_w_ukv': out['m_mla_w_ukv'], 'm_ssm_out_norm_w': out['m_ssm_out_norm_w'], 'm_mla_out_norm_w': out['m_mla_out_norm_w'], 'm_w_out': out['m_w_out'], 'm_ffn_norm_w': out['m_ffn_norm_w'], 'm_ffn_w_up': out['m_ffn_w_up'], 'm_ffn_conv_w': out['m_ffn_conv_w'], 'm_ffn_conv_b': out['m_ffn_conv_b'], 'm_ffn_w_down': out['m_ffn_w_down'], 'm_final_norm_w': out['m_final_norm_w'], 'v_attn_norm_w': out['v_attn_norm_w'], 'v_w_in': out['v_w_in'], 'v_ssm_lambda_re': out['v_ssm_lambda_re'], 'v_ssm_lambda_im': out['v_ssm_lambda_im'], 'v_ssm_log_dt': out['v_ssm_log_dt'], 'v_ssm_b_re': out['v_ssm_b_re'], 'v_ssm_b_im': out['v_ssm_b_im'], 'v_ssm_c_re': out['v_ssm_c_re'], 'v_ssm_c_im': out['v_ssm_c_im'], 'v_ssm_d': out['v_ssm_d'], 'v_ssm_w_glu': out['v_ssm_w_glu'], 'v_ssm_b_glu': out['v_ssm_b_glu'], 'v_mla_q_norm_w': out['v_mla_q_norm_w'], 'v_mla_w_uq': out['v_mla_w_uq'], 'v_mla_kv_norm_w': out['v_mla_kv_norm_w'], 'v_mla_w_ukv': out['v_mla_w_ukv'], 'v_ssm_out_norm_w': out['v_ssm_out_norm_w'], 'v_mla_out_norm_w': out['v_mla_out_norm_w'], 'v_w_out': out['v_w_out'], 'v_ffn_norm_w': out['v_ffn_norm_w'], 'v_ffn_w_up': out['v_ffn_w_up'], 'v_ffn_conv_w': out['v_ffn_conv_w'], 'v_ffn_conv_b': out['v_ffn_conv_b'], 'v_ffn_w_down': out['v_ffn_w_down'], 'v_final_norm_w': out['v_final_norm_w']}


def _loss(weights, diff, rest, loss_target):
    with _jax.named_scope("forward"):
        args = {**rest, TWIN_DIFF_INPUT: diff, **{k: w.astype(_WEIGHT_DTYPES[k]) for k, w in weights.items()}}
        y = _forward(args)
    with _jax.named_scope("loss_head"):
        err = _jnp.square(y.astype(_jnp.float32) - loss_target)
        return 0.5 * _jnp.sum(_jnp.mean(err, axis=-1)) if err.ndim else 0.5 * err


def _adamw(w, g, m, v):
    m = ADAM_B1 * m + (1.0 - ADAM_B1) * g
    v = ADAM_B2 * v + (1.0 - ADAM_B2) * _jnp.square(g)
    m_hat = m / (1.0 - ADAM_B1 ** ADAM_STEP)
    v_hat = v / (1.0 - ADAM_B2 ** ADAM_STEP)
    delta = -ADAM_LR * (m_hat / (_jnp.sqrt(v_hat) + ADAM_EPS) + ADAM_WD * w)
    return delta, m, v


def reference(x, positions, attn_norm_w, w_in, ssm_lambda_re, ssm_lambda_im, ssm_log_dt, ssm_b_re, ssm_b_im, ssm_c_re, ssm_c_im, ssm_d, ssm_w_glu, ssm_b_glu, mla_q_norm_w, mla_w_uq, mla_kv_norm_w, mla_w_ukv, ssm_out_norm_w, mla_out_norm_w, w_out, ffn_norm_w, ffn_w_up, ffn_conv_w, ffn_conv_b, ffn_w_down, final_norm_w, loss_target, m_attn_norm_w, m_w_in, m_ssm_lambda_re, m_ssm_lambda_im, m_ssm_log_dt, m_ssm_b_re, m_ssm_b_im, m_ssm_c_re, m_ssm_c_im, m_ssm_d, m_ssm_w_glu, m_ssm_b_glu, m_mla_q_norm_w, m_mla_w_uq, m_mla_kv_norm_w, m_mla_w_ukv, m_ssm_out_norm_w, m_mla_out_norm_w, m_w_out, m_ffn_norm_w, m_ffn_w_up, m_ffn_conv_w, m_ffn_conv_b, m_ffn_w_down, m_final_norm_w, v_attn_norm_w, v_w_in, v_ssm_lambda_re, v_ssm_lambda_im, v_ssm_log_dt, v_ssm_b_re, v_ssm_b_im, v_ssm_c_re, v_ssm_c_im, v_ssm_d, v_ssm_w_glu, v_ssm_b_glu, v_mla_q_norm_w, v_mla_w_uq, v_mla_kv_norm_w, v_mla_w_ukv, v_ssm_out_norm_w, v_mla_out_norm_w, v_w_out, v_ffn_norm_w, v_ffn_w_up, v_ffn_conv_w, v_ffn_conv_b, v_ffn_w_down, v_final_norm_w):
    given = dict(x=x, positions=positions, attn_norm_w=attn_norm_w, w_in=w_in, ssm_lambda_re=ssm_lambda_re, ssm_lambda_im=ssm_lambda_im, ssm_log_dt=ssm_log_dt, ssm_b_re=ssm_b_re, ssm_b_im=ssm_b_im, ssm_c_re=ssm_c_re, ssm_c_im=ssm_c_im, ssm_d=ssm_d, ssm_w_glu=ssm_w_glu, ssm_b_glu=ssm_b_glu, mla_q_norm_w=mla_q_norm_w, mla_w_uq=mla_w_uq, mla_kv_norm_w=mla_kv_norm_w, mla_w_ukv=mla_w_ukv, ssm_out_norm_w=ssm_out_norm_w, mla_out_norm_w=mla_out_norm_w, w_out=w_out, ffn_norm_w=ffn_norm_w, ffn_w_up=ffn_w_up, ffn_conv_w=ffn_conv_w, ffn_conv_b=ffn_conv_b, ffn_w_down=ffn_w_down, final_norm_w=final_norm_w, loss_target=loss_target, m_attn_norm_w=m_attn_norm_w, m_w_in=m_w_in, m_ssm_lambda_re=m_ssm_lambda_re, m_ssm_lambda_im=m_ssm_lambda_im, m_ssm_log_dt=m_ssm_log_dt, m_ssm_b_re=m_ssm_b_re, m_ssm_b_im=m_ssm_b_im, m_ssm_c_re=m_ssm_c_re, m_ssm_c_im=m_ssm_c_im, m_ssm_d=m_ssm_d, m_ssm_w_glu=m_ssm_w_glu, m_ssm_b_glu=m_ssm_b_glu, m_mla_q_norm_w=m_mla_q_norm_w, m_mla_w_uq=m_mla_w_uq, m_mla_kv_norm_w=m_mla_kv_norm_w, m_mla_w_ukv=m_mla_w_ukv, m_ssm_out_norm_w=m_ssm_out_norm_w, m_mla_out_norm_w=m_mla_out_norm_w, m_w_out=m_w_out, m_ffn_norm_w=m_ffn_norm_w, m_ffn_w_up=m_ffn_w_up, m_ffn_conv_w=m_ffn_conv_w, m_ffn_conv_b=m_ffn_conv_b, m_ffn_w_down=m_ffn_w_down, m_final_norm_w=m_final_norm_w, v_attn_norm_w=v_attn_norm_w, v_w_in=v_w_in, v_ssm_lambda_re=v_ssm_lambda_re, v_ssm_lambda_im=v_ssm_lambda_im, v_ssm_log_dt=v_ssm_log_dt, v_ssm_b_re=v_ssm_b_re, v_ssm_b_im=v_ssm_b_im, v_ssm_c_re=v_ssm_c_re, v_ssm_c_im=v_ssm_c_im, v_ssm_d=v_ssm_d, v_ssm_w_glu=v_ssm_w_glu, v_ssm_b_glu=v_ssm_b_glu, v_mla_q_norm_w=v_mla_q_norm_w, v_mla_w_uq=v_mla_w_uq, v_mla_kv_norm_w=v_mla_kv_norm_w, v_mla_w_ukv=v_mla_w_ukv, v_ssm_out_norm_w=v_ssm_out_norm_w, v_mla_out_norm_w=v_mla_out_norm_w, v_w_out=v_w_out, v_ffn_norm_w=v_ffn_norm_w, v_ffn_w_up=v_ffn_w_up, v_ffn_conv_w=v_ffn_conv_w, v_ffn_conv_b=v_ffn_conv_b, v_ffn_w_down=v_ffn_w_down, v_final_norm_w=v_final_norm_w)
    weights = {n: given[n] for n in TWIN_WEIGHTS}
    shared = {n: given[n] for n in SHARED_INPUTS}
    per_example = {n: given[n] for n in ['x', 'positions']}
    grad_fn = _jax.value_and_grad(_loss, argnums=(0, 1))

    def one_microbatch(ex, loss_target):
        ex = dict(ex)
        diff = ex.pop(TWIN_DIFF_INPUT)
        return grad_fn(weights, diff, {**shared, **ex}, loss_target)

    if N_MICROBATCH == 1:
        loss, (grad_w, grad_x) = one_microbatch(per_example, given["loss_target"])
    else:
        def body(carry, xs):
            loss_sum, grad_sum = carry
            l_k, (gw_k, gx_k) = one_microbatch(xs[0], xs[1])
            with _jax.named_scope("update"):
                return (loss_sum + l_k, _jax.tree.map(_jnp.add, grad_sum, gw_k)), gx_k

        init = (_jnp.zeros((), _jnp.float32), _jax.tree.map(_jnp.zeros_like, weights))
        (loss, grad_w), grad_x = _jax.lax.scan(body, init, (per_example, given["loss_target"]))
    with _jax.named_scope("update"):
        delta_w, new_m, new_v = {}, {}, {}
        for n in TWIN_WEIGHTS:
            delta_w[n], new_m[n], new_v[n] = _adamw(weights[n], grad_w[n], given["m_" + n], given["v_" + n])
    return (loss, grad_x, *[grad_w[n] for n in TWIN_WEIGHTS], *[delta_w[n] for n in TWIN_WEIGHTS],
            *[new_m[n] for n in TWIN_WEIGHTS], *[new_v[n] for n in TWIN_WEIGHTS])
```

```python
import functools
import math

import jax
import jax.numpy as jnp
from jax import lax
from jax.experimental import pallas as pl
from jax.experimental.pallas import tpu as pltpu

F32 = jnp.float32
BF16 = jnp.bfloat16
MESH = pl.DeviceIdType.MESH

N_DEV = 8
LANES = 128
SUBLANES = 8
VMEM_LIMIT = 48 * 1024 * 1024

SSM_GROUP = 16
SSM_STATE = 64
GROUPS_PER_BLOCK = LANES // SSM_GROUP
STATE_BLOCK = GROUPS_PER_BLOCK * SSM_STATE
QK_NOPE = 128
QK_ROPE = 64
V_DIM = 128
ROPE_THETA = 10000.0
RMS_EPS = 1e-6

ADAM_LR = 0.001
ADAM_B1 = 0.9
ADAM_B2 = 0.999
ADAM_EPS = 1e-08
ADAM_WD = 0.01
ADAM_STEP = 10

NN = ((1,), (0,))
NT = ((1,), (1,))
TN = ((0,), (0,))


def _cparams():
    return pltpu.CompilerParams(vmem_limit_bytes=VMEM_LIMIT)


def _tile(n, want):
    if n <= want:
        return n
    t = (want // LANES) * LANES
    while t >= LANES:
        if n % t == 0:
            return t
        t -= LANES
    return n


def _mm(name, a, b, *, grid, a_spec, b_spec, o_spec, out_shape, out_dtype, contract=NN,
        res=None, res_spec=None):
    nk = grid[-1]
    kaxis = len(grid) - 1
    acc_shape = tuple(d for d in o_spec.block_shape if d is not None)

    def body(*refs):
        if res is None:
            a_ref, b_ref, o_ref, acc = refs
            r_ref = None
        else:
            a_ref, b_ref, r_ref, o_ref, acc = refs
        k = pl.program_id(kaxis)

        @pl.when(k == 0)
        def _():
            acc[...] = jnp.zeros_like(acc)

        acc[...] += lax.dot_general(a_ref[...].astype(BF16), b_ref[...].astype(BF16),
                                    (contract, ((), ())), preferred_element_type=F32)

        @pl.when(k == nk - 1)
        def _():
            r = acc[...]
            if r_ref is not None:
                r = r + r_ref[...].astype(F32)
            o_ref[...] = r.astype(o_ref.dtype)

    ins = [a, b] + ([] if res is None else [res])
    in_specs = [a_spec, b_spec] + ([] if res is None else [res_spec])
    return pl.pallas_call(
        body, name=name, grid=grid, in_specs=in_specs, out_specs=o_spec,
        out_shape=jax.ShapeDtypeStruct(out_shape, out_dtype),
        scratch_shapes=[pltpu.VMEM(acc_shape, F32)], compiler_params=_cparams(),
    )(*ins)


def _mm2d(name, a, b, contract, out_dtype, tm=512, tn=512, tk=512, res=None):
    if contract == NN:
        (m, kk), n = a.shape, b.shape[1]
    elif contract == NT:
        (m, kk), n = a.shape, b.shape[0]
    else:
        (kk, m), n = a.shape, b.shape[1]
    tm, tn, tk = _tile(m, tm), _tile(n, tn), _tile(kk, tk)
    grid = (m // tm, n // tn, kk // tk)
    if contract == TN:
        a_spec = pl.BlockSpec((tk, tm), lambda i, j, k: (k, i))
    else:
        a_spec = pl.BlockSpec((tm, tk), lambda i, j, k: (i, k))
    if contract == NT:
        b_spec = pl.BlockSpec((tn, tk), lambda i, j, k: (j, k))
    else:
        b_spec = pl.BlockSpec((tk, tn), lambda i, j, k: (k, j))
    o_spec = pl.BlockSpec((tm, tn), lambda i, j, k: (i, j))
    res_spec = None
    if res is not None:
        if res.shape[0] == 1:
            res_spec = pl.BlockSpec((1, tn), lambda i, j, k: (0, j))
        else:
            res_spec = pl.BlockSpec((tm, tn), lambda i, j, k: (i, j))
    return _mm(name, a, b, grid=grid, a_spec=a_spec, b_spec=b_spec, o_spec=o_spec,
               out_shape=(m, n), out_dtype=out_dtype, contract=contract, res=res, res_spec=res_spec)


def _blockwise(name, fn, ins, in_specs, outs, out_specs, grid, n_acc=0, acc_all=True):
    n_in, n_out = len(ins), len(outs)
    n_plain = n_out - n_acc

    def body(*refs):
        vals = fn(*[r[...] for r in refs[:n_in]])
        if not isinstance(vals, (tuple, list)):
            vals = (vals,)
        o_refs = refs[n_in:n_in + n_out]
        for r, v in zip(o_refs[:n_plain], vals[:n_plain]):
            r[...] = v.astype(r.dtype)
        if n_acc:
            if acc_all:
                first = functools.reduce(jnp.logical_and, [pl.program_id(d) == 0 for d in range(len(grid))])
            else:
                first = pl.program_id(len(grid) - 1) == 0

            @pl.when(first)
            def _():
                for r, v in zip(o_refs[n_plain:], vals[n_plain:]):
                    r[...] = v.astype(r.dtype)

            @pl.when(jnp.logical_not(first))
            def _():
                for r, v in zip(o_refs[n_plain:], vals[n_plain:]):
                    r[...] += v.astype(r.dtype)

    return pl.pallas_call(
        body, name=name, grid=grid, in_specs=in_specs, out_specs=out_specs,
        out_shape=[jax.ShapeDtypeStruct(s, d) for s, d in outs], compiler_params=_cparams(),
    )(*ins)


def _row_spec(t, c):
    return pl.BlockSpec((t, c), lambda i: (i, 0))


def _full_spec(shape):
    nd = len(shape)
    return pl.BlockSpec(tuple(shape), lambda *g: (0,) * nd)


def _rms(xf, w):
    return xf * lax.rsqrt(jnp.mean(xf * xf, axis=-1, keepdims=True) + RMS_EPS) * w


def _rms_bwd(xf, w, dy):
    _, vjp = jax.vjp(_rms, xf, w)
    return vjp(dy)


def _s5_disc(lr, li, ldt, bre, bim):
    dt = jnp.exp(ldt)
    mag = jnp.exp(lr * dt)
    ar = mag * jnp.cos(li * dt)
    ai = mag * jnp.sin(li * dt)
    nr, ni = ar - 1.0, ai
    den = lr * lr + li * li
    zr = (nr * lr + ni * li) / den
    zi = (ni * lr - nr * li) / den
    return ar, ai, zr * bre - zi * bim, zr * bim + zi * bre


def _s5_prep(lr, li, ldt, bre, bim):
    def body(lr_r, li_r, ldt_r, bre_r, bim_r, ar_r, ai_r, br_r, bi_r):
        ar, ai, br, bi = _s5_disc(lr_r[...], li_r[...], ldt_r[...], bre_r[...], bim_r[...])
        ar_r[...] = ar
        ai_r[...] = ai
        br_r[...] = br
        bi_r[...] = bi

    sd = jax.ShapeDtypeStruct
    return pl.pallas_call(
        body, name="s5_prep",
        out_shape=[sd(lr.shape, F32), sd(lr.shape, F32), sd(bre.shape, F32), sd(bre.shape, F32)],
        compiler_params=_cparams(),
    )(lr, li, ldt, bre, bim)


def _s5_prep_bwd(lr, li, ldt, bre, bim, dar, dai, dbr, dbi):
    def body(lr_r, li_r, ldt_r, bre_r, bim_r, dar_r, dai_r, dbr_r, dbi_r, o0, o1, o2, o3, o4):
        _, vjp = jax.vjp(_s5_disc, lr_r[...], li_r[...], ldt_r[...], bre_r[...], bim_r[...])
        g = vjp((dar_r[...], dai_r[...], dbr_r[...], dbi_r[...]))
        for o, v in zip((o0, o1, o2, o3, o4), g):
            o[...] = v

    sd = jax.ShapeDtypeStruct
    return pl.pallas_call(
        body, name="s5_prep_bwd",
        out_shape=[sd(lr.shape, F32), sd(li.shape, F32), sd(ldt.shape, F32), sd(bre.shape, F32), sd(bim.shape, F32)],
        compiler_params=_cparams(),
    )(lr, li, ldt, bre, bim, dar, dai, dbr, dbi)


SCAN_T = 256


def _s5_scan(x, a):
    seq, width = x.shape
    w2 = 2 * STATE_BLOCK
    nj = width // w2
    t_blk = min(SCAN_T, seq)
    hb = STATE_BLOCK

    def body(x_ref, a_ref, o_ref, st):
        @pl.when(pl.program_id(1) == 0)
        def _():
            st[...] = jnp.zeros_like(st)

        ar = a_ref[:, :hb]
        ai = a_ref[:, hb:]

        def step(t, carry):
            sr, si = carry
            xr = x_ref[pl.ds(t, 1), :hb]
            xi = x_ref[pl.ds(t, 1), hb:]
            nr = ar * sr - ai * si + xr
            ni = ar * si + ai * sr + xi
            o_ref[pl.ds(t, 1), :hb] = nr
            o_ref[pl.ds(t, 1), hb:] = ni
            return nr, ni

        sr, si = lax.fori_loop(0, t_blk, step, (st[0:1, :], st[1:2, :]))
        st[0:1, :] = sr
        st[1:2, :] = si

    return pl.pallas_call(
        body, name="s5_scan", grid=(nj, seq // t_blk),
        in_specs=[pl.BlockSpec((t_blk, w2), lambda j, i: (i, j)), pl.BlockSpec((1, w2), lambda j, i: (0, j))],
        out_specs=pl.BlockSpec((t_blk, w2), lambda j, i: (i, j)),
        out_shape=jax.ShapeDtypeStruct(x.shape, F32),
        scratch_shapes=[pltpu.VMEM((SUBLANES, hb), F32)], compiler_params=_cparams(),
    )(x, a)


def _s5_scan_bwd(ds, s, a):
    seq, width = ds.shape
    w2 = 2 * STATE_BLOCK
    nj = width // w2
    t_blk = min(SCAN_T, seq)
    nb = seq // t_blk
    hb = STATE_BLOCK
    per8 = t_blk // SUBLANES

    def body(d_ref, s_ref, sprev_ref, a_ref, o_ref, da_ref, st):
        ib = pl.program_id(1)

        @pl.when(ib == 0)
        def _():
            st[...] = jnp.zeros_like(st)

        ar = a_ref[:, :hb]
        ai = a_ref[:, hb:]

        def step(tt, carry):
            lr, li = carry
            t = t_blk - 1 - tt
            dr = d_ref[pl.ds(t, 1), :hb]
            di = d_ref[pl.ds(t, 1), hb:]
            nr = ar * lr + ai * li + dr
            ni = ar * li - ai * lr + di
            o_ref[pl.ds(t, 1), :hb] = nr
            o_ref[pl.ds(t, 1), hb:] = ni
            return nr, ni

        lr, li = lax.fori_loop(0, t_blk, step, (st[0:1, :], st[1:2, :]))
        st[0:1, :] = lr
        st[1:2, :] = li

        lam = o_ref[...]
        sv = s_ref[...]
        rows = lax.broadcasted_iota(jnp.int32, sv.shape, 0)
        prev_last = sprev_ref[SUBLANES - 1:SUBLANES, :]
        prev_last = jnp.where(ib == nb - 1, jnp.zeros_like(prev_last), prev_last)
        s_sh = jnp.where(rows >= 1, pltpu.roll(sv, 1, 0), prev_last)
        lam_r, lam_i = lam[:, :hb], lam[:, hb:]
        sr_, si_ = s_sh[:, :hb], s_sh[:, hb:]
        dar = jnp.sum(lam_r * sr_ + lam_i * si_, axis=0, keepdims=True)
        dai = jnp.sum(lam_i * sr_ - lam_r * si_, axis=0, keepdims=True)
        contrib = jnp.concatenate([dar, dai], axis=1)

        @pl.when(ib == 0)
        def _():
            da_ref[...] = contrib

        @pl.when(ib != 0)
        def _():
            da_ref[...] += contrib

    blk = lambda j, i: (nb - 1 - i, j)
    prev_blk = lambda j, i: (jnp.maximum((nb - 1 - i) * per8 - 1, 0), j)
    return pl.pallas_call(
        body, name="s5_scan_bwd", grid=(nj, nb),
        in_specs=[pl.BlockSpec((t_blk, w2), blk), pl.BlockSpec((t_blk, w2), blk),
                  pl.BlockSpec((SUBLANES, w2), prev_blk), pl.BlockSpec((1, w2), lambda j, i: (0, j))],
        out_specs=[pl.BlockSpec((t_blk, w2), blk), pl.BlockSpec((1, w2), lambda j, i: (0, j))],
        out_shape=[jax.ShapeDtypeStruct(ds.shape, F32), jax.ShapeDtypeStruct((1, width), F32)],
        scratch_shapes=[pltpu.VMEM((SUBLANES, hb), F32)], compiler_params=_cparams(),
    )(ds, s, s, a)


def _rope128(x, cos, sa, sb):
    return x * cos + pltpu.roll(x, 96, 1) * sa + pltpu.roll(x, 32, 1) * sb


def _rope128_t(dy, cos, sa, sb):
    return dy * cos + pltpu.roll(dy * sa, 32, 1) + pltpu.roll(dy * sb, 96, 1)


ATT_BQ = 256


def _scores(q, kn, kp, i, bq, scale):
    s = lax.dot_general(q[:, :QK_NOPE], kn, (NT, ((), ())), preferred_element_type=F32)
    s = s + lax.dot_general(q[:, QK_NOPE:], kp, (NT, ((), ())), preferred_element_type=F32)
    s = s * scale
    row = i * bq + lax.broadcasted_iota(jnp.int32, s.shape, 0)
    col = lax.broadcasted_iota(jnp.int32, s.shape, 1)
    s = jnp.where(col <= row, s, jnp.finfo(F32).min)
    m = jnp.max(s, axis=-1, keepdims=True)
    e = jnp.exp(s - m)
    return e / jnp.sum(e, axis=-1, keepdims=True)


def _attn_fwd(q, kv, kpe):
    nh, seq, _ = q.shape
    bq = min(ATT_BQ, seq)
    scale = (QK_NOPE + QK_ROPE) ** -0.5

    def body(q_ref, kn_ref, v_ref, kp_ref, o_ref):
        p = _scores(q_ref[...], kn_ref[...], kp_ref[...], pl.program_id(1), bq, scale)
        o_ref[...] = jnp.dot(p.astype(BF16), v_ref[...], preferred_element_type=F32)

    return pl.pallas_call(
        body, name="attn_fwd", grid=(nh, seq // bq),
        in_specs=[pl.BlockSpec((None, bq, 256), lambda h, i: (h, i, 0)),
                  pl.BlockSpec((None, seq, 128), lambda h, i: (h, 0, 0)),
                  pl.BlockSpec((None, seq, 128), lambda h, i: (h, 0, 1)),
                  pl.BlockSpec((seq, 128), lambda h, i: (0, 0))],
        out_specs=pl.BlockSpec((bq, V_DIM), lambda h, i: (i, h)),
        out_shape=jax.ShapeDtypeStruct((seq, nh * V_DIM), F32), compiler_params=_cparams(),
    )(q, kv, kv, kpe)


def _attn_bwd(q, kv, kpe, do):
    nh, seq, _ = q.shape
    bq = min(ATT_BQ, seq)
    scale = (QK_NOPE + QK_ROPE) ** -0.5

    def body(q_ref, kn_ref, v_ref, kp_ref, do_ref, dq_ref, dkv_ref, dkp_ref):
        i = pl.program_id(1)
        q = q_ref[...]
        kn, v, kp = kn_ref[...], v_ref[...], kp_ref[...]
        p = _scores(q, kn, kp, i, bq, scale)
        dob = do_ref[...].astype(BF16)
        dp = lax.dot_general(dob, v, (NT, ((), ())), preferred_element_type=F32)
        ds = p * (dp - jnp.sum(p * dp, axis=-1, keepdims=True)) * scale
        dsb = ds.astype(BF16)
        pb = p.astype(BF16)
        dq_ref[:, :QK_NOPE] = jnp.dot(dsb, kn, preferred_element_type=F32)
        dq_ref[:, QK_NOPE:] = jnp.dot(dsb, kp, preferred_element_type=F32)
        dkn = lax.dot_general(dsb, q[:, :QK_NOPE], (TN, ((), ())), preferred_element_type=F32)
        dkp = lax.dot_general(dsb, q[:, QK_NOPE:], (TN, ((), ())), preferred_element_type=F32)
        dv = lax.dot_general(pb, dob, (TN, ((), ())), preferred_element_type=F32)

        @pl.when(i == 0)
        def _():
            dkv_ref[:, :QK_NOPE] = dkn
            dkv_ref[:, QK_NOPE:] = dv
            dkp_ref[...] = dkp

        @pl.when(i != 0)
        def _():
            dkv_ref[:, :QK_NOPE] += dkn
            dkv_ref[:, QK_NOPE:] += dv
            dkp_ref[...] += dkp

    sd = jax.ShapeDtypeStruct
    return pl.pallas_call(
        body, name="attn_bwd", grid=(nh, seq // bq),
        in_specs=[pl.BlockSpec((None, bq, 256), lambda h, i: (h, i, 0)),
                  pl.BlockSpec((None, seq, 128), lambda h, i: (h, 0, 0)),
                  pl.BlockSpec((None, seq, 128), lambda h, i: (h, 0, 1)),
                  pl.BlockSpec((seq, 128), lambda h, i: (0, 0)),
                  pl.BlockSpec((bq, V_DIM), lambda h, i: (i, h))],
        out_specs=[pl.BlockSpec((None, bq, 256), lambda h, i: (h, i, 0)),
                   pl.BlockSpec((None, seq, 256), lambda h, i: (h, 0, 0)),
                   pl.BlockSpec((None, seq, 128), lambda h, i: (h, 0, 0))],
        out_shape=[sd((nh, seq, 256), F32), sd((nh, seq, 256), F32), sd((nh, seq, 128), F32)],
        compiler_params=_cparams(),
    )(q, kv, kv, kpe, do)


def _conv3(a, w, b):
    rows = lax.broadcasted_iota(jnp.int32, a.shape, 0)
    a1 = jnp.where(rows >= 1, pltpu.roll(a, 1, 0), 0.0)
    a2 = jnp.where(rows >= 2, pltpu.roll(a, 2, 0), 0.0)
    return w[2:3] * a + w[1:2] * a1 + w[0:1] * a2 + b, a1, a2


def _conv_gate_fwd(a, cw, cb):
    ns, seq, c = a.shape
    half = ns // 2
    nc = c // LANES

    def fn(ga, va, wg, wv, bg, bv):
        gc, _, _ = _conv3(ga, wg, bg)
        vc, _, _ = _conv3(va, wv, bv)
        return gc * jax.nn.sigmoid(gc) * vc

    def a_spec(off):
        return pl.BlockSpec((None, seq, LANES), lambda k, j: (k + off, 0, j))

    def w_spec(off, r):
        return pl.BlockSpec((None, r, LANES), lambda k, j: (k + off, 0, j))

    return _blockwise(
        "conv_gate_fwd", fn, [a, a, cw, cw, cb, cb],
        [a_spec(0), a_spec(half), w_spec(0, 3), w_spec(half, 3), w_spec(0, 1), w_spec(half, 1)],
        [((half, seq, c), BF16)], [pl.BlockSpec((None, seq, LANES), lambda k, j: (k, 0, j))],
        grid=(half, nc))[0]


def _conv_gate_bwd(a, cw, cb, dm):
    ns, seq, c = a.shape
    half = ns // 2
    nc = c // LANES

    def body(own_ref, oth_ref, wo_ref, wt_ref, bo_ref, bt_ref, dm_ref, da_ref, dw_ref, db_ref):
        k = pl.program_id(0)
        own, wo = own_ref[...], wo_ref[...]
        co, a1, a2 = _conv3(own, wo, bo_ref[...])
        ct, _, _ = _conv3(oth_ref[...], wt_ref[...], bt_ref[...])
        dmv = dm_ref[...]
        so = jax.nn.sigmoid(co)
        st = jax.nn.sigmoid(ct)
        d_gate = dmv * ct * (so * (1.0 + co * (1.0 - so)))
        d_val = dmv * ct * st
        dc = jnp.where(k < half, d_gate, d_val)
        rows = lax.broadcasted_iota(jnp.int32, dc.shape, 0)
        up1 = jnp.where(rows < seq - 1, pltpu.roll(dc, seq - 1, 0), 0.0)
        up2 = jnp.where(rows < seq - 2, pltpu.roll(dc, seq - 2, 0), 0.0)
        da_ref[...] = (wo[2:3] * dc + wo[1:2] * up1 + wo[0:1] * up2).astype(da_ref.dtype)
        dw_ref[0:1, :] = jnp.sum(dc * a2, axis=0, keepdims=True)
        dw_ref[1:2, :] = jnp.sum(dc * a1, axis=0, keepdims=True)
        dw_ref[2:3, :] = jnp.sum(dc * own, axis=0, keepdims=True)
        db_ref[...] = jnp.sum(dc, axis=0, keepdims=True)

    def blk(r, other=False):
        if other:
            return pl.BlockSpec((None, r, LANES), lambda k, j: ((k + half) % ns, 0, j))
        return pl.BlockSpec((None, r, LANES), lambda k, j: (k, 0, j))

    sd = jax.ShapeDtypeStruct
    return pl.pallas_call(
        body, name="conv_gate_bwd", grid=(ns, nc),
        in_specs=[blk(seq), blk(seq, True), blk(3), blk(3, True), blk(1), blk(1, True),
                  pl.BlockSpec((None, seq, LANES), lambda k, j: (k % half, 0, j))],
        out_specs=[blk(seq), blk(3), blk(1)],
        out_shape=[sd((ns, seq, c), BF16), sd((ns, 3, c), F32), sd((ns, 1, c), F32)],
        compiler_params=_cparams(),
    )(a, a, cw, cw, cb, cb, dm)


ROW_T = 256


def _local_step(x, positions, target, w):
    seq, d = x.shape
    t_row = min(ROW_T, seq)
    nrow = seq // t_row
    ssm_w = d // 2
    nj = ssm_w // LANES
    n_groups = ssm_w // SSM_GROUP
    nh = w["wuq"].shape[0]
    q_rank = w["wuq"].shape[1]
    kv_rank = w["wukv"].shape[1]
    ns = w["wup"].shape[0]
    c_ff = w["wup"].shape[2]
    in_pad = w["win"].shape[1]
    tm = min(512, seq)
    nm = seq // tm
    sw = 2 * STATE_BLOCK
    g1 = (nrow,)

    lr3 = w["lam_re"].reshape(n_groups, 1, SSM_STATE)
    li3 = w["lam_im"].reshape(n_groups, 1, SSM_STATE)
    ldt3 = w["log_dt"].reshape(n_groups, 1, 1)
    bt_re = jnp.swapaxes(w["b_re"].reshape(n_groups, SSM_STATE, SSM_GROUP), 1, 2)
    bt_im = jnp.swapaxes(w["b_im"].reshape(n_groups, SSM_STATE, SSM_GROUP), 1, 2)
    abar_re, abar_im, bbt_re, bbt_im = _s5_prep(lr3, li3, ldt3, bt_re, bt_im)
    eye = jnp.eye(GROUPS_PER_BLOCK, dtype=F32)

    def blockdiag_in(bb):
        t = bb.reshape(nj, GROUPS_PER_BLOCK, SSM_GROUP, SSM_STATE)
        return jnp.einsum("jghp,gk->jghkp", t, eye).reshape(nj, LANES, STATE_BLOCK)

    def blockdiag_in_t(dwb):
        t = dwb.reshape(nj, GROUPS_PER_BLOCK, SSM_GROUP, GROUPS_PER_BLOCK, SSM_STATE)
        return jnp.einsum("jghkp,gk->jghp", t, eye).reshape(n_groups, SSM_GROUP, SSM_STATE)

    def blockdiag_out(cc):
        t = cc.reshape(nj, GROUPS_PER_BLOCK, SSM_GROUP, SSM_STATE)
        return jnp.einsum("jghp,gk->jkpgh", t, eye).reshape(nj, STATE_BLOCK, LANES)

    def blockdiag_out_t(dwc):
        t = dwc.reshape(nj, GROUPS_PER_BLOCK, SSM_STATE, GROUPS_PER_BLOCK, SSM_GROUP)
        return jnp.einsum("jkpgh,gk->jghp", t, eye).reshape(n_groups, SSM_GROUP, SSM_STATE)

    c_re = w["c_re"].reshape(n_groups, SSM_GROUP, SSM_STATE)
    c_im = w["c_im"].reshape(n_groups, SSM_GROUP, SSM_STATE)
    wb = jnp.concatenate([blockdiag_in(bbt_re), blockdiag_in(bbt_im)], axis=2).astype(BF16)
    wc = jnp.concatenate([blockdiag_out(c_re), -blockdiag_out(c_im)], axis=1).astype(BF16)
    a_lay = jnp.concatenate([abar_re.reshape(nj, 1, STATE_BLOCK), abar_im.reshape(nj, 1, STATE_BLOCK)],
                            axis=1).reshape(1, nj * sw)

    attn_w = w["attn_norm"]
    hn = _blockwise("norm1", lambda xb, wv: _rms(xb, wv), [x, attn_w], [_row_spec(t_row, d), _full_spec((1, d))],
                    [((seq, d), BF16)], [_row_spec(t_row, d)], g1)[0]
    proj = _mm2d("proj", hn, w["win"], NN, F32, tn=640, tk=1024)

    s0 = _mm("ssm_bu", proj, wb, grid=(nm, nj, 1),
             a_spec=pl.BlockSpec((tm, LANES), lambda i, j, k: (i, j)),
             b_spec=pl.BlockSpec((None, LANES, sw), lambda i, j, k: (j, 0, 0)),
             o_spec=pl.BlockSpec((tm, sw), lambda i, j, k: (i, j)),
             out_shape=(seq, nj * sw), out_dtype=F32)
    s_all = _s5_scan(s0, a_lay)
    ylin = _mm("ssm_cy", s_all, wc, grid=(nm, nj, 1),
               a_spec=pl.BlockSpec((tm, sw), lambda i, j, k: (i, j)),
               b_spec=pl.BlockSpec((None, sw, LANES), lambda i, j, k: (j, 0, 0)),
               o_spec=pl.BlockSpec((tm, LANES), lambda i, j, k: (i, j)),
               out_shape=(seq, ssm_w), out_dtype=F32)
    u_spec = pl.BlockSpec((t_row, ssm_w), lambda i: (i, 0))

    def ypre_fn(yl, ub, dsk):
        yp = yl + dsk * ub
        return yp, jax.nn.gelu(yp)

    y_pre, yg = _blockwise("ssm_gelu", ypre_fn, [ylin, proj, w["ssm_d"]],
                           [_row_spec(t_row, ssm_w), u_spec, _full_spec((1, ssm_w))],
                           [((seq, ssm_w), F32), ((seq, ssm_w), BF16)],
                           [_row_spec(t_row, ssm_w)] * 2, g1)
    z = _mm2d("ssm_glu", yg, w["wglu"], NN, F32, tk=1024, res=w["b_glu"])
    y_ssm = _blockwise("ssm_gate", lambda yp, zb: jax.nn.gelu(yp) * jax.nn.sigmoid(zb), [y_pre, z],
                       [_row_spec(t_row, ssm_w)] * 2, [((seq, ssm_w), F32)], [_row_spec(t_row, ssm_w)], g1)[0]

    cq_off, ckv_off, kpe_off = ssm_w, ssm_w + q_rank, ssm_w + q_rank + kv_rank
    c_q = proj[:, cq_off:ckv_off]
    c_kv = proj[:, ckv_off:kpe_off]
    kpe_raw = proj[:, kpe_off:kpe_off + LANES]
    pos_b = jnp.broadcast_to(positions.astype(F32)[:, None], (seq, LANES))
    inv_freq = ROPE_THETA ** (-jnp.arange(0, QK_ROPE, 2, dtype=F32) / QK_ROPE)
    inv128 = jnp.tile(inv_freq, 4).reshape(1, LANES)

    def mla_prep_fn(cq, ckv, kp, pb, inv, wq, wkv):
        ang = pb * inv
        lane = lax.broadcasted_iota(jnp.int32, ang.shape, 1)
        cs, sn = jnp.cos(ang), jnp.sin(ang)
        cos = jnp.where(lane < QK_ROPE, cs, 0.0)
        sa = jnp.where(lane < QK_ROPE // 2, -sn, 0.0)
        sb = jnp.where(jnp.logical_and(lane >= QK_ROPE // 2, lane < QK_ROPE), sn, 0.0)
        return _rms(cq, wq), _rms(ckv, wkv), _rope128(kp, cos, sa, sb), cos, sa, sb

    qn, kvn, kpe, cos_t, sa_t, sb_t = _blockwise(
        "mla_prep", mla_prep_fn, [c_q, c_kv, kpe_raw, pos_b, inv128, w["q_norm"], w["kv_norm"]],
        [_row_spec(t_row, q_rank), _row_spec(t_row, kv_rank), _row_spec(t_row, LANES), _row_spec(t_row, LANES),
         _full_spec((1, LANES)), _full_spec((1, q_rank)), _full_spec((1, kv_rank))],
        [((seq, q_rank), BF16), ((seq, kv_rank), BF16), ((seq, LANES), BF16)] + [((seq, LANES), F32)] * 3,
        [_row_spec(t_row, q_rank), _row_spec(t_row, kv_rank)] + [_row_spec(t_row, LANES)] * 4, g1)

    def head_mm(name, act, wh, out_dtype):
        kdim, ndim = wh.shape[1], wh.shape[2]
        return _mm(name, act, wh, grid=(nh, nm, 1),
                   a_spec=pl.BlockSpec((tm, kdim), lambda h, i, k: (i, 0)),
                   b_spec=pl.BlockSpec((None, kdim, ndim), lambda h, i, k: (h, 0, 0)),
                   o_spec=pl.BlockSpec((None, tm, ndim), lambda h, i, k: (h, i, 0)),
                   out_shape=(nh, seq, ndim), out_dtype=out_dtype)

    q_raw = head_mm("mla_q", qn, w["wuq"], F32)
    kv = head_mm("mla_kv", kvn, w["wukv"], BF16)
    hrow = pl.BlockSpec((None, t_row, 256), lambda h, i: (h, i, 0))
    tab = pl.BlockSpec((t_row, LANES), lambda h, i: (i, 0))

    def q_rope_fn(qb, cos, sa, sb):
        return jnp.concatenate([qb[:, :QK_NOPE], _rope128(qb[:, QK_NOPE:], cos, sa, sb)], axis=1)

    q = _blockwise("mla_q_rope", q_rope_fn, [q_raw, cos_t, sa_t, sb_t], [hrow, tab, tab, tab],
                   [((nh, seq, 256), BF16)], [hrow], (nh, nrow))[0]
    y_mla = _attn_fwd(q, kv, kpe)
    mla_w = nh * V_DIM

    def outnorm_fn(ys, ym, ws, wm):
        return jnp.concatenate([_rms(ys, ws), _rms(ym, wm)], axis=1)

    ycat = _blockwise("out_norm", outnorm_fn, [y_ssm, y_mla, w["son"], w["mon"]],
                      [_row_spec(t_row, ssm_w), _row_spec(t_row, mla_w), _full_spec((1, ssm_w)), _full_spec((1, mla_w))],
                      [((seq, d), BF16)], [_row_spec(t_row, d)], g1)[0]
    h1 = _mm2d("out_proj", ycat, w["wout"], NN, F32, tk=1024, res=x)

    hn2 = _blockwise("norm2", lambda hb, wv: _rms(hb, wv), [h1, w["ffn_norm"]],
                     [_row_spec(t_row, d), _full_spec((1, d))], [((seq, d), BF16)], [_row_spec(t_row, d)], g1)[0]
    tku = _tile(d, 512)
    a_ff = _mm("ffn_up", hn2, w["wup"], grid=(ns, nm, d // tku),
               a_spec=pl.BlockSpec((tm, tku), lambda s, i, k: (i, k)),
               b_spec=pl.BlockSpec((None, tku, c_ff), lambda s, i, k: (s, k, 0)),
               o_spec=pl.BlockSpec((None, tm, c_ff), lambda s, i, k: (s, i, 0)),
               out_shape=(ns, seq, c_ff), out_dtype=F32)
    cb3 = w["conv_b"].reshape(ns, 1, c_ff)
    m_ff = _conv_gate_fwd(a_ff, w["conv_w"], cb3)
    half = ns // 2
    wd4 = w["wdown"]
    tnd = _tile(d, 512)
    h2 = _mm("ffn_down", m_ff, wd4, grid=(nm, d // tnd, half),
             a_spec=pl.BlockSpec((None, tm, c_ff), lambda i, j, s: (s, i, 0)),
             b_spec=pl.BlockSpec((None, c_ff, tnd), lambda i, j, s: (s, 0, j)),
             o_spec=pl.BlockSpec((tm, tnd), lambda i, j, s: (i, j)),
             out_shape=(seq, d), out_dtype=F32,
             res=h1, res_spec=pl.BlockSpec((tm, tnd), lambda i, j, s: (i, j)))

    def loss_fn(hb, tb, wv):
        def f(hh, ww):
            err = _rms(hh, ww) - tb
            return 0.5 * jnp.sum(jnp.mean(err * err, axis=-1))

        lossv, (dh, dw) = jax.value_and_grad(f, argnums=(0, 1))(hb, wv)
        return dh, dh, jnp.full((1, LANES), lossv, F32), dw

    fin_w = w["final_norm"].reshape(1, d)
    dh2, dh2b, loss_acc, g_final = _blockwise(
        "loss_head", loss_fn, [h2, target, fin_w], [_row_spec(t_row, d), _row_spec(t_row, d), _full_spec((1, d))],
        [((seq, d), F32), ((seq, d), BF16), ((1, LANES), F32), ((1, d), F32)],
        [_row_spec(t_row, d), _row_spec(t_row, d), _full_spec((1, LANES)), _full_spec((1, d))], g1, n_acc=2)
    loss = loss_acc[0, 0]

    dm = _mm("ffn_down_dx", dh2b, wd4, grid=(half, nm, d // tku), contract=NT,
             a_spec=pl.BlockSpec((tm, tku), lambda s, i, k: (i, k)),
             b_spec=pl.BlockSpec((None, c_ff, tku), lambda s, i, k: (s, 0, k)),
             o_spec=pl.BlockSpec((None, tm, c_ff), lambda s, i, k: (s, i, 0)),
             out_shape=(half, seq, c_ff), out_dtype=F32)
    tks = _tile(seq, 512)
    g_wdown = _mm("ffn_down_dw", m_ff, dh2b, grid=(half, d // tnd, seq // tks), contract=TN,
                  a_spec=pl.BlockSpec((None, tks, c_ff), lambda s, j, k: (s, k, 0)),
                  b_spec=pl.BlockSpec((tks, tnd), lambda s, j, k: (k, j)),
                  o_spec=pl.BlockSpec((None, c_ff, tnd), lambda s, j, k: (s, 0, j)),
                  out_shape=(half, c_ff, d), out_dtype=BF16)
    da_ff, g_convw, g_convb = _conv_gate_bwd(a_ff, w["conv_w"], cb3, dm)
    dhn2 = _mm("ffn_up_dx", da_ff, w["wup"], grid=(nm, d // tnd, ns), contract=NT,
               a_spec=pl.BlockSpec((None, tm, c_ff), lambda i, j, s: (s, i, 0)),
               b_spec=pl.BlockSpec((None, tnd, c_ff), lambda i, j, s: (s, j, 0)),
               o_spec=pl.BlockSpec((tm, tnd), lambda i, j, s: (i, j)),
               out_shape=(seq, d), out_dtype=F32)
    g_wup = _mm("ffn_up_dw", hn2, da_ff, grid=(ns, d // tnd, seq // tks), contract=TN,
                a_spec=pl.BlockSpec((tks, tnd), lambda s, j, k: (k, j)),
                b_spec=pl.BlockSpec((None, tks, c_ff), lambda s, j, k: (s, k, 0)),
                o_spec=pl.BlockSpec((None, tnd, c_ff), lambda s, j, k: (s, j, 0)),
                out_shape=(ns, d, c_ff), out_dtype=BF16)

    def norm_bwd_fn(hb, dres, dn, wv):
        dx_, dw_ = _rms_bwd(hb, wv, dn)
        dtot = dres + dx_
        return dtot, dtot, dw_

    dh1, dh1b, g_ffn_norm = _blockwise(
        "norm2_bwd", norm_bwd_fn, [h1, dh2, dhn2, w["ffn_norm"]],
        [_row_spec(t_row, d)] * 3 + [_full_spec((1, d))],
        [((seq, d), F32), ((seq, d), BF16), ((1, d), F32)],
        [_row_spec(t_row, d), _row_spec(t_row, d), _full_spec((1, d))], g1, n_acc=1)

    dycat = _mm2d("out_proj_dx", dh1b, w["wout"], NT, F32, tk=1024)
    g_wout = _mm2d("out_proj_dw", ycat, dh1b, TN, BF16)

    def outnorm_bwd_fn(ys, ym, dyc, ws, wm):
        dys, dws = _rms_bwd(ys, ws, dyc[:, :ssm_w])
        dym, dwm = _rms_bwd(ym, wm, dyc[:, ssm_w:])
        return dys, dym, dws, dwm

    dy_ssm, dy_mla, g_son, g_mon = _blockwise(
        "out_norm_bwd", outnorm_bwd_fn, [y_ssm, y_mla, dycat, w["son"], w["mon"]],
        [_row_spec(t_row, ssm_w), _row_spec(t_row, mla_w), _row_spec(t_row, d), _full_spec((1, ssm_w)),
         _full_spec((1, mla_w))],
        [((seq, ssm_w), F32), ((seq, mla_w), F32), ((1, ssm_w), F32), ((1, mla_w), F32)],
        [_row_spec(t_row, ssm_w), _row_spec(t_row, mla_w), _full_spec((1, ssm_w)), _full_spec((1, mla_w))],
        g1, n_acc=2)

    def gate_bwd1_fn(dy, yp, zb):
        ygv = jax.nn.gelu(yp)
        sg = jax.nn.sigmoid(zb)
        dz = dy * ygv * sg * (1.0 - sg)
        return dz, jnp.sum(dz, axis=0, keepdims=True)

    dz, g_bglu = _blockwise("ssm_gate_bwd", gate_bwd1_fn, [dy_ssm, y_pre, z], [_row_spec(t_row, ssm_w)] * 3,
                            [((seq, ssm_w), BF16), ((1, ssm_w), F32)],
                            [_row_spec(t_row, ssm_w), _full_spec((1, ssm_w))], g1, n_acc=1)
    dyg2 = _mm2d("ssm_glu_dx", dz, w["wglu"], NT, F32, tk=1024)
    g_wglu = _mm2d("ssm_glu_dw", yg, dz, TN, BF16)

    def gelu_bwd_fn(dy, yp, zb, dg2, ub, dsk):
        dyg = dy * jax.nn.sigmoid(zb) + dg2
        _, vjp = jax.vjp(jax.nn.gelu, yp)
        dyp = vjp(dyg)[0]
        return dyp, dyp * dsk, jnp.sum(dyp * ub, axis=0, keepdims=True)

    dy_pre, du1, g_ssmd = _blockwise(
        "ssm_gelu_bwd", gelu_bwd_fn, [dy_ssm, y_pre, z, dyg2, proj, w["ssm_d"]],
        [_row_spec(t_row, ssm_w)] * 4 + [u_spec, _full_spec((1, ssm_w))],
        [((seq, ssm_w), BF16), ((seq, ssm_w), F32), ((1, ssm_w), F32)],
        [_row_spec(t_row, ssm_w), _row_spec(t_row, ssm_w), _full_spec((1, ssm_w))], g1, n_acc=1)
    ds_all = _mm("ssm_cy_dx", dy_pre, wc, grid=(nm, nj, 1), contract=NT,
                 a_spec=pl.BlockSpec((tm, LANES), lambda i, j, k: (i, j)),
                 b_spec=pl.BlockSpec((None, sw, LANES), lambda i, j, k: (j, 0, 0)),
                 o_spec=pl.BlockSpec((tm, sw), lambda i, j, k: (i, j)),
                 out_shape=(seq, nj * sw), out_dtype=F32)
    dwc = _mm("ssm_cy_dw", s_all, dy_pre, grid=(nj, 1, seq // tks), contract=TN,
              a_spec=pl.BlockSpec((tks, sw), lambda j, n, k: (k, j)),
              b_spec=pl.BlockSpec((tks, LANES), lambda j, n, k: (k, j)),
              o_spec=pl.BlockSpec((None, sw, LANES), lambda j, n, k: (j, 0, 0)),
              out_shape=(nj, sw, LANES), out_dtype=F32)
    lam, da_lay = _s5_scan_bwd(ds_all, s_all, a_lay)
    du = _mm("ssm_bu_dx", lam, wb, grid=(nm, nj, 1), contract=NT,
             a_spec=pl.BlockSpec((tm, sw), lambda i, j, k: (i, j)),
             b_spec=pl.BlockSpec((None, LANES, sw), lambda i, j, k: (j, 0, 0)),
             o_spec=pl.BlockSpec((tm, LANES), lambda i, j, k: (i, j)),
             out_shape=(seq, ssm_w), out_dtype=BF16,
             res=du1, res_spec=pl.BlockSpec((tm, LANES), lambda i, j, k: (i, j)))
    dwb = _mm("ssm_bu_dw", proj, lam, grid=(nj, 1, seq // tks), contract=TN,
              a_spec=pl.BlockSpec((tks, LANES), lambda j, n, k: (k, j)),
              b_spec=pl.BlockSpec((tks, sw), lambda j, n, k: (k, j)),
              o_spec=pl.BlockSpec((None, LANES, sw), lambda j, n, k: (j, 0, 0)),
              out_shape=(nj, LANES, sw), out_dtype=F32)
    g_c_re = blockdiag_out_t(dwc[:, :STATE_BLOCK, :])
    g_c_im = -blockdiag_out_t(dwc[:, STATE_BLOCK:, :])
    dbbt_re = blockdiag_in_t(dwb[:, :, :STATE_BLOCK])
    dbbt_im = blockdiag_in_t(dwb[:, :, STATE_BLOCK:])
    da3 = da_lay.reshape(nj, 2, STATE_BLOCK)
    dabar_re = da3[:, 0, :].reshape(n_groups, 1, SSM_STATE)
    dabar_im = da3[:, 1, :].reshape(n_groups, 1, SSM_STATE)
    g_lr3, g_li3, g_ldt3, g_bt_re, g_bt_im = _s5_prep_bwd(lr3, li3, ldt3, bt_re, bt_im,
                                                           dabar_re, dabar_im, dbbt_re, dbbt_im)

    dq, dkv, dkp_h = _attn_bwd(q, kv, kpe, dy_mla)

    def q_rope_bwd_fn(dqb, cos, sa, sb):
        return jnp.concatenate([dqb[:, :QK_NOPE], _rope128_t(dqb[:, QK_NOPE:], cos, sa, sb)], axis=1)

    dq_raw = _blockwise("mla_q_rope_bwd", q_rope_bwd_fn, [dq, cos_t, sa_t, sb_t], [hrow, tab, tab, tab],
                        [((nh, seq, 256), BF16)], [hrow], (nh, nrow))[0]

    def head_mm_dx(name, dact, wh):
        kdim, ndim = wh.shape[1], wh.shape[2]
        return _mm(name, dact, wh, grid=(nm, 1, nh), contract=NT,
                   a_spec=pl.BlockSpec((None, tm, ndim), lambda i, j, h: (h, i, 0)),
                   b_spec=pl.BlockSpec((None, kdim, ndim), lambda i, j, h: (h, 0, 0)),
                   o_spec=pl.BlockSpec((tm, kdim), lambda i, j, h: (i, 0)),
                   out_shape=(seq, kdim), out_dtype=F32)

    def head_mm_dw(name, act, dact):
        kdim, ndim = act.shape[1], dact.shape[2]
        return _mm(name, act, dact, grid=(nh, 1, seq // tks), contract=TN,
                   a_spec=pl.BlockSpec((tks, kdim), lambda h, j, k: (k, 0)),
                   b_spec=pl.BlockSpec((None, tks, ndim), lambda h, j, k: (h, k, 0)),
                   o_spec=pl.BlockSpec((None, kdim, ndim), lambda h, j, k: (h, 0, 0)),
                   out_shape=(nh, kdim, ndim), out_dtype=BF16)

    dqn = head_mm_dx("mla_q_dx", dq_raw, w["wuq"])
    g_wuq = head_mm_dw("mla_q_dw", qn, dq_raw)
    dkvn = head_mm_dx("mla_kv_dx", dkv, w["wukv"])
    g_wukv = head_mm_dw("mla_kv_dw", kvn, dkv)

    def mla_prep_bwd_fn(cq, ckv, dqn_b, dkvn_b, dkp_b, cos, sa, sb, wq, wkv):
        dcq, dwq = _rms_bwd(cq, wq, dqn_b)
        dckv, dwkv = _rms_bwd(ckv, wkv, dkvn_b)
        dkp_sum = dkp_b[0]
        for h in range(1, nh):
            dkp_sum = dkp_sum + dkp_b[h]
        return dcq, dckv, _rope128_t(dkp_sum, cos, sa, sb), dwq, dwkv

    dc_q, dc_kv, dkpe_raw, g_qnorm, g_kvnorm = _blockwise(
        "mla_prep_bwd", mla_prep_bwd_fn, [c_q, c_kv, dqn, dkvn, dkp_h, cos_t, sa_t, sb_t, w["q_norm"], w["kv_norm"]],
        [_row_spec(t_row, q_rank), _row_spec(t_row, kv_rank), _row_spec(t_row, q_rank), _row_spec(t_row, kv_rank),
         pl.BlockSpec((nh, t_row, LANES), lambda i: (0, i, 0))] + [_row_spec(t_row, LANES)] * 3
        + [_full_spec((1, q_rank)), _full_spec((1, kv_rank))],
        [((seq, q_rank), BF16), ((seq, kv_rank), BF16), ((seq, LANES), BF16), ((1, q_rank), F32), ((1, kv_rank), F32)],
        [_row_spec(t_row, q_rank), _row_spec(t_row, kv_rank), _row_spec(t_row, LANES), _full_spec((1, q_rank)),
         _full_spec((1, kv_rank))], g1, n_acc=2)

    dproj = jnp.concatenate([du, dc_q, dc_kv, dkpe_raw], axis=1)
    dhn = _mm2d("proj_dx", dproj, w["win"], NT, F32, tk=640)
    g_win = _mm2d("proj_dw", hn, dproj, TN, BF16, tn=640)

    def norm1_bwd_fn(xb, dres, dn, wv):
        dx_, dw_ = _rms_bwd(xb, wv, dn)
        return dres + dx_, dw_

    grad_x, g_attn_norm = _blockwise(
        "norm1_bwd", norm1_bwd_fn, [x, dh1, dhn, attn_w], [_row_spec(t_row, d)] * 3 + [_full_spec((1, d))],
        [((seq, d), F32), ((1, d), F32)], [_row_spec(t_row, d), _full_spec((1, d))], g1, n_acc=1)

    grads = dict(
        attn_norm=g_attn_norm, win=g_win, lam_re=g_lr3, lam_im=g_li3, log_dt=g_ldt3,
        b_re=jnp.swapaxes(g_bt_re, 1, 2), b_im=jnp.swapaxes(g_bt_im, 1, 2), c_re=g_c_re, c_im=g_c_im,
        ssm_d=g_ssmd, wglu=g_wglu, b_glu=g_bglu, q_norm=g_qnorm, wuq=g_wuq, kv_norm=g_kvnorm, wukv=g_wukv,
        son=g_son, mon=g_mon, wout=g_wout, ffn_norm=g_ffn_norm, wup=g_wup, conv_w=g_convw, conv_b=g_convb,
        wdown=g_wdown, final_norm=g_final)
    return loss, grad_x, grads


def _mesh_pos():
    return lax.axis_index("x"), lax.axis_index("y"), lax.axis_index("c")


def _all_gather(name, xs):
    n = len(xs)

    def body(*refs):
        x_refs, o_refs = refs[:n], refs[n:2 * n]
        send_sems, recv_sems, local_sems = refs[2 * n:]
        x, y, c = _mesh_pos()
        me, sibling = (x, y, c), (x, y, 1 - c)
        chips = [(1 - x, y), (x, 1 - y), (1 - x, 1 - y)]

        def slot(o_ref, px, py, pc):
            return o_ref.at[4 * px + 2 * py + pc]

        def copy(t, k, block, to, src=None):
            dst = slot(o_refs[t], *block)
            return pltpu.make_async_remote_copy(
                src_ref=dst if src is None else src, dst_ref=dst,
                send_sem=send_sems.at[7 * t + k], recv_sem=recv_sems.at[7 * t + k],
                device_id=to, device_id_type=MESH)

        started = []
        for t in range(n):
            mine = pltpu.make_async_copy(x_refs[t], slot(o_refs[t], *me), local_sems.at[t])
            mine.start()
            started.append(mine)
        first = []
        for t in range(n):
            first.append(copy(t, 0, me, sibling, src=x_refs[t]))
            first += [copy(t, 1 + j, me, (*chip, c), src=x_refs[t]) for j, chip in enumerate(chips)]
        for cp in first:
            cp.start()
        passed = []
        for j, chip in enumerate(chips):
            for t in range(n):
                copy(t, 1 + j, (*chip, c), me).wait_recv()
                fwd = copy(t, 4 + j, (*chip, c), sibling)
                fwd.start()
                passed.append(fwd)
        for t in range(n):
            copy(t, 0, sibling, me).wait_recv()
            for j, chip in enumerate(chips):
                copy(t, 4 + j, (*chip, 1 - c), me).wait_recv()
        for cp in first + passed:
            cp.wait_send()
        for mine in started:
            mine.wait()

    any_spec = pl.BlockSpec(memory_space=pl.ANY)
    return pl.pallas_call(
        body, name=name,
        out_shape=[jax.ShapeDtypeStruct((N_DEV,) + v.shape, v.dtype) for v in xs],
        in_specs=[any_spec] * n, out_specs=[any_spec] * n,
        scratch_shapes=[pltpu.SemaphoreType.DMA((7 * n,)), pltpu.SemaphoreType.DMA((7 * n,)),
                        pltpu.SemaphoreType.DMA((n,))],
    )(*xs)


def _exchange_partials(name, gs):
    n = len(gs)

    def body(*refs):
        g_refs, o_refs = refs[:n], refs[n:2 * n]
        send_sems, recv_sems, local_sems = refs[2 * n:]
        x, y, c = _mesh_pos()
        me_idx = 4 * x + 2 * y + c
        copies = []
        for t in range(n):
            mine = pltpu.make_async_copy(g_refs[t].at[me_idx], o_refs[t].at[me_idx], local_sems.at[t])
            mine.start()
            copies.append(mine)
        remote = []
        for k in range(1, N_DEV):
            px = 1 - x if k & 4 else x
            py = 1 - y if k & 2 else y
            pc = 1 - c if k & 1 else c
            p_idx = 4 * px + 2 * py + pc
            for t in range(n):
                cp = pltpu.make_async_remote_copy(
                    src_ref=g_refs[t].at[p_idx], dst_ref=o_refs[t].at[me_idx],
                    send_sem=send_sems.at[7 * t + k - 1], recv_sem=recv_sems.at[7 * t + k - 1],
                    device_id=(px, py, pc), device_id_type=MESH)
                cp.start()
                landing = pltpu.make_async_remote_copy(
                    src_ref=g_refs[t].at[p_idx], dst_ref=o_refs[t].at[p_idx],
                    send_sem=send_sems.at[7 * t + k - 1], recv_sem=recv_sems.at[7 * t + k - 1],
                    device_id=(px, py, pc), device_id_type=MESH)
                remote.append((cp, landing))
        for cp, landing in remote:
            landing.wait_recv()
        for cp, landing in remote:
            cp.wait_send()
        for mine in copies:
            mine.wait()

    any_spec = pl.BlockSpec(memory_space=pl.ANY)
    return pl.pallas_call(
        body, name=name,
        out_shape=[jax.ShapeDtypeStruct(v.shape, v.dtype) for v in gs],
        in_specs=[any_spec] * n, out_specs=[any_spec] * n,
        scratch_shapes=[pltpu.SemaphoreType.DMA((7 * n,)), pltpu.SemaphoreType.DMA((7 * n,)),
                        pltpu.SemaphoreType.DMA((n,))],
    )(*gs)


ADAM_BLOCK_ELEMS = 128 * 1024


def _adamw_sum(name, parts, wv, mv, vv):
    npart, r, c = parts.shape
    tr = r
    if r * c > ADAM_BLOCK_ELEMS and r % SUBLANES == 0:
        tr = SUBLANES
        while r % (tr * 2) == 0 and tr * 2 * c <= ADAM_BLOCK_ELEMS:
            tr *= 2
    bc1 = 1.0 - ADAM_B1 ** ADAM_STEP
    bc2 = 1.0 - ADAM_B2 ** ADAM_STEP

    def fn(pb, wb_, mb, vb):
        g = pb[0].astype(F32)
        for j in range(1, npart):
            g = g + pb[j].astype(F32)
        m_new = ADAM_B1 * mb + (1.0 - ADAM_B1) * g
        v_new = ADAM_B2 * vb + (1.0 - ADAM_B2) * (g * g)
        m_hat = m_new / bc1
        v_hat = v_new / bc2
        delta = -ADAM_LR * (m_hat / (jnp.sqrt(v_hat) + ADAM_EPS) + ADAM_WD * wb_)
        return g, delta, m_new, v_new

    row = pl.BlockSpec((tr, c), lambda i: (i, 0))
    return _blockwise(name, fn, [parts, wv, mv, vv],
                      [pl.BlockSpec((npart, tr, c), lambda i: (0, i, 0)), row, row, row],
                      [((r, c), F32)] * 4, [row] * 4, (r // tr,))


_SMALL = ["attn_norm", "lam_re", "lam_im", "log_dt", "b_re", "b_im", "c_re", "c_im", "ssm_d", "b_glu",
          "q_norm", "kv_norm", "son", "mon", "ffn_norm", "conv_b", "final_norm"]
_BIG = ["win", "wglu", "wuq", "wukv", "wout", "wup", "wdown", "conv_w"]
_ORDER = ["attn_norm", "win", "lam_re", "lam_im", "log_dt", "b_re", "b_im", "c_re", "c_im", "ssm_d", "wglu",
          "b_glu", "q_norm", "wuq", "kv_norm", "wukv", "son", "mon", "wout", "ffn_norm", "wup", "conv_w",
          "conv_b", "wdown", "final_norm"]


def _pack(arrs):
    flat = jnp.concatenate([a.reshape(-1).astype(F32) for a in arrs])
    pad = (-flat.shape[0]) % (SUBLANES * LANES)
    return jnp.pad(flat, (0, pad)).reshape(-1, LANES)


def _unpack(packed, shapes):
    flat = packed.reshape(-1)
    out, off = [], 0
    for s in shapes:
        n = math.prod(s)
        out.append(flat[off:off + n].reshape(s))
        off += n
    return out


def kernel(x, positions, attn_norm_w, w_in, ssm_lambda_re, ssm_lambda_im, ssm_log_dt, ssm_b_re, ssm_b_im, ssm_c_re, ssm_c_im, ssm_d, ssm_w_glu, ssm_b_glu, mla_q_norm_w, mla_w_uq, mla_kv_norm_w, mla_w_ukv, ssm_out_norm_w, mla_out_norm_w, w_out, ffn_norm_w, ffn_w_up, ffn_conv_w, ffn_conv_b, ffn_w_down, final_norm_w, loss_target, m_attn_norm_w, m_w_in, m_ssm_lambda_re, m_ssm_lambda_im, m_ssm_log_dt, m_ssm_b_re, m_ssm_b_im, m_ssm_c_re, m_ssm_c_im, m_ssm_d, m_ssm_w_glu, m_ssm_b_glu, m_mla_q_norm_w, m_mla_w_uq, m_mla_kv_norm_w, m_mla_w_ukv, m_ssm_out_norm_w, m_mla_out_norm_w, m_w_out, m_ffn_norm_w, m_ffn_w_up, m_ffn_conv_w, m_ffn_conv_b, m_ffn_w_down, m_final_norm_w, v_attn_norm_w, v_w_in, v_ssm_lambda_re, v_ssm_lambda_im, v_ssm_log_dt, v_ssm_b_re, v_ssm_b_im, v_ssm_c_re, v_ssm_c_im, v_ssm_d, v_ssm_w_glu, v_ssm_b_glu, v_mla_q_norm_w, v_mla_w_uq, v_mla_kv_norm_w, v_mla_w_ukv, v_ssm_out_norm_w, v_mla_out_norm_w, v_w_out, v_ffn_norm_w, v_ffn_w_up, v_ffn_conv_w, v_ffn_conv_b, v_ffn_w_down, v_final_norm_w):
    wts = dict(attn_norm=attn_norm_w, win=w_in, lam_re=ssm_lambda_re, lam_im=ssm_lambda_im, log_dt=ssm_log_dt,
               b_re=ssm_b_re, b_im=ssm_b_im, c_re=ssm_c_re, c_im=ssm_c_im, ssm_d=ssm_d, wglu=ssm_w_glu,
               b_glu=ssm_b_glu, q_norm=mla_q_norm_w, wuq=mla_w_uq, kv_norm=mla_kv_norm_w, wukv=mla_w_ukv,
               son=ssm_out_norm_w, mon=mla_out_norm_w, wout=w_out, ffn_norm=ffn_norm_w, wup=ffn_w_up,
               conv_w=ffn_conv_w, conv_b=ffn_conv_b, wdown=ffn_w_down, final_norm=final_norm_w)
    moms = dict(zip(_ORDER, [m_attn_norm_w, m_w_in, m_ssm_lambda_re, m_ssm_lambda_im, m_ssm_log_dt, m_ssm_b_re,
                             m_ssm_b_im, m_ssm_c_re, m_ssm_c_im, m_ssm_d, m_ssm_w_glu, m_ssm_b_glu, m_mla_q_norm_w,
                             m_mla_w_uq, m_mla_kv_norm_w, m_mla_w_ukv, m_ssm_out_norm_w, m_mla_out_norm_w, m_w_out,
                             m_ffn_norm_w, m_ffn_w_up, m_ffn_conv_w, m_ffn_conv_b, m_ffn_w_down, m_final_norm_w]))
    vels = dict(zip(_ORDER, [v_attn_norm_w, v_w_in, v_ssm_lambda_re, v_ssm_lambda_im, v_ssm_log_dt, v_ssm_b_re,
                             v_ssm_b_im, v_ssm_c_re, v_ssm_c_im, v_ssm_d, v_ssm_w_glu, v_ssm_b_glu, v_mla_q_norm_w,
                             v_mla_w_uq, v_mla_kv_norm_w, v_mla_w_ukv, v_ssm_out_norm_w, v_mla_out_norm_w, v_w_out,
                             v_ffn_norm_w, v_ffn_w_up, v_ffn_conv_w, v_ffn_conv_b, v_ffn_w_down, v_final_norm_w]))
    seq, d = x.shape[1], x.shape[2]
    in_width = w_in.shape[2]
    in_pad = -(-in_width // LANES) * LANES
    q_cols = mla_w_uq.shape[2]
    q_pad = 2 * LANES

    local = [
        jnp.pad(w_in[0], ((0, 0), (0, in_pad - in_width))).astype(BF16),
        ssm_w_glu[0].astype(BF16),
        jnp.pad(mla_w_uq[0], ((0, 0), (0, q_pad - q_cols))).astype(BF16),
        mla_w_ukv[0].astype(BF16),
        w_out[0].astype(BF16),
        ffn_w_up[0].astype(BF16),
        ffn_w_down[0].astype(BF16),
        ffn_conv_w[0],
    ]
    win_g, wglu_g, wuq_g, wukv_g, wout_g, wup_g, wdown_g, convw_g = _all_gather("gather_weights", local)
    ns = N_DEV
    c_ff = wup_g.shape[2]
    w = dict(
        attn_norm=attn_norm_w, win=win_g.reshape(d, in_pad), lam_re=ssm_lambda_re, lam_im=ssm_lambda_im,
        log_dt=ssm_log_dt, b_re=ssm_b_re, b_im=ssm_b_im, c_re=ssm_c_re, c_im=ssm_c_im, ssm_d=ssm_d,
        wglu=wglu_g.reshape(d // 2, d // 2), b_glu=ssm_b_glu, q_norm=mla_q_norm_w, wuq=wuq_g,
        kv_norm=mla_kv_norm_w, wukv=wukv_g, son=ssm_out_norm_w, mon=mla_out_norm_w, wout=wout_g.reshape(d, d),
        ffn_norm=ffn_norm_w, wup=wup_g, conv_w=convw_g, conv_b=ffn_conv_b,
        wdown=wdown_g.reshape(ns // 2, c_ff, d), final_norm=final_norm_w)

    loss_part, grad_x, g = _local_step(x[0], positions[0], loss_target[0], w)
    loss = lax.psum(loss_part, ("x", "y", "c"))

    big_parts = [
        g["win"][:, :in_width].reshape(N_DEV, d // N_DEV, in_width),
        g["wglu"].reshape(N_DEV, d // 2 // N_DEV, d // 2),
        g["wuq"][:, :, :q_cols],
        g["wukv"],
        g["wout"].reshape(N_DEV, d // N_DEV, d),
        g["wup"],
        g["wdown"].reshape(N_DEV, c_ff // 2, d),
        g["conv_w"],
    ]
    big_recv = _exchange_partials("exchange_grads", big_parts)
    small_shapes = [wts[k].shape for k in _SMALL]
    small_part = _pack([g[k] for k in _SMALL])
    small_all = _all_gather("gather_small_grads", [small_part])[0]

    out = {}
    for k, parts in zip(_BIG, big_recv):
        shp = wts[k].shape
        r, c = shp[-2], shp[-1]
        res = _adamw_sum("adamw_" + k, parts.reshape(N_DEV, r, c), wts[k].reshape(r, c),
                         moms[k].reshape(r, c), vels[k].reshape(r, c))
        out[k] = [a.reshape(shp) for a in res]
    sw_ = _pack([wts[k] for k in _SMALL])
    sm_ = _pack([moms[k] for k in _SMALL])
    sv_ = _pack([vels[k] for k in _SMALL])
    res = _adamw_sum("adamw_small", small_all, sw_, sm_, sv_)
    unpacked = [_unpack(a, small_shapes) for a in res]
    for i, k in enumerate(_SMALL):
        out[k] = [u[i] for u in unpacked]

    grad_x = grad_x.reshape(x.shape)
    return (loss, grad_x, *[out[k][0] for k in _ORDER], *[out[k][1] for k in _ORDER],
            *[out[k][2] for k in _ORDER], *[out[k][3] for k in _ORDER])
```

```python
import functools
import math

import jax
import jax.numpy as jnp
from jax import lax
from jax.experimental import pallas as pl
from jax.experimental.pallas import tpu as pltpu
from jax.experimental.pallas import tpu_sc as plsc

F32 = jnp.float32
BF16 = jnp.bfloat16
MESH = pl.DeviceIdType.MESH

N_DEV = 8
LANES = 128
SUBLANES = 8
VMEM_LIMIT = 48 * 1024 * 1024

SSM_GROUP = 16
SSM_STATE = 64
GROUPS_PER_BLOCK = LANES // SSM_GROUP
STATE_BLOCK = GROUPS_PER_BLOCK * SSM_STATE
QK_NOPE = 128
QK_ROPE = 64
V_DIM = 128
ROPE_THETA = 10000.0
RMS_EPS = 1e-6

ADAM_LR = 0.001
ADAM_B1 = 0.9
ADAM_B2 = 0.999
ADAM_EPS = 1e-08
ADAM_WD = 0.01
ADAM_STEP = 10

NN = ((1,), (0,))
NT = ((1,), (1,))
TN = ((0,), (0,))


def _cparams():
    return pltpu.CompilerParams(vmem_limit_bytes=VMEM_LIMIT)


def _tile(n, want):
    if n <= want:
        return n
    t = (want // LANES) * LANES
    while t >= LANES:
        if n % t == 0:
            return t
        t -= LANES
    return n


def _mm(name, a, b, *, grid, a_spec, b_spec, o_spec, out_shape, out_dtype, contract=NN,
        res=None, res_spec=None):
    nk = grid[-1]
    kaxis = len(grid) - 1
    acc_shape = tuple(d for d in o_spec.block_shape if d is not None)

    def body(*refs):
        if res is None:
            a_ref, b_ref, o_ref, acc = refs
            r_ref = None
        else:
            a_ref, b_ref, r_ref, o_ref, acc = refs
        k = pl.program_id(kaxis)

        @pl.when(k == 0)
        def _():
            acc[...] = jnp.zeros_like(acc)

        acc[...] += lax.dot_general(a_ref[...].astype(BF16), b_ref[...].astype(BF16),
                                    (contract, ((), ())), preferred_element_type=F32)

        @pl.when(k == nk - 1)
        def _():
            r = acc[...]
            if r_ref is not None:
                r = r + r_ref[...].astype(F32)
            o_ref[...] = r.astype(o_ref.dtype)

    ins = [a, b] + ([] if res is None else [res])
    in_specs = [a_spec, b_spec] + ([] if res is None else [res_spec])
    return pl.pallas_call(
        body, name=name, grid=grid, in_specs=in_specs, out_specs=o_spec,
        out_shape=jax.ShapeDtypeStruct(out_shape, out_dtype),
        scratch_shapes=[pltpu.VMEM(acc_shape, F32)], compiler_params=_cparams(),
    )(*ins)


def _mm2d(name, a, b, contract, out_dtype, tm=512, tn=512, tk=512, res=None):
    if contract == NN:
        (m, kk), n = a.shape, b.shape[1]
    elif contract == NT:
        (m, kk), n = a.shape, b.shape[0]
    else:
        (kk, m), n = a.shape, b.shape[1]
    tm, tn, tk = _tile(m, tm), _tile(n, tn), _tile(kk, tk)
    grid = (m // tm, n // tn, kk // tk)
    if contract == TN:
        a_spec = pl.BlockSpec((tk, tm), lambda i, j, k: (k, i))
    else:
        a_spec = pl.BlockSpec((tm, tk), lambda i, j, k: (i, k))
    if contract == NT:
        b_spec = pl.BlockSpec((tn, tk), lambda i, j, k: (j, k))
    else:
        b_spec = pl.BlockSpec((tk, tn), lambda i, j, k: (k, j))
    o_spec = pl.BlockSpec((tm, tn), lambda i, j, k: (i, j))
    res_spec = None
    if res is not None:
        if res.shape[0] == 1:
            res_spec = pl.BlockSpec((1, tn), lambda i, j, k: (0, j))
        else:
            res_spec = pl.BlockSpec((tm, tn), lambda i, j, k: (i, j))
    return _mm(name, a, b, grid=grid, a_spec=a_spec, b_spec=b_spec, o_spec=o_spec,
               out_shape=(m, n), out_dtype=out_dtype, contract=contract, res=res, res_spec=res_spec)


def _blockwise(name, fn, ins, in_specs, outs, out_specs, grid, n_acc=0, acc_all=True):
    n_in, n_out = len(ins), len(outs)
    n_plain = n_out - n_acc

    def body(*refs):
        vals = fn(*[r[...] for r in refs[:n_in]])
        if not isinstance(vals, (tuple, list)):
            vals = (vals,)
        o_refs = refs[n_in:n_in + n_out]
        for r, v in zip(o_refs[:n_plain], vals[:n_plain]):
            r[...] = v.astype(r.dtype)
        if n_acc:
            if acc_all:
                first = functools.reduce(jnp.logical_and, [pl.program_id(d) == 0 for d in range(len(grid))])
            else:
                first = pl.program_id(len(grid) - 1) == 0

            @pl.when(first)
            def _():
                for r, v in zip(o_refs[n_plain:], vals[n_plain:]):
                    r[...] = v.astype(r.dtype)

            @pl.when(jnp.logical_not(first))
            def _():
                for r, v in zip(o_refs[n_plain:], vals[n_plain:]):
                    r[...] += v.astype(r.dtype)

    return pl.pallas_call(
        body, name=name, grid=grid, in_specs=in_specs, out_specs=out_specs,
        out_shape=[jax.ShapeDtypeStruct(s, d) for s, d in outs], compiler_params=_cparams(),
    )(*ins)


def _row_spec(t, c):
    return pl.BlockSpec((t, c), lambda i: (i, 0))


def _full_spec(shape):
    nd = len(shape)
    return pl.BlockSpec(tuple(shape), lambda *g: (0,) * nd)


def _rms(xf, w):
    return xf * lax.rsqrt(jnp.mean(xf * xf, axis=-1, keepdims=True) + RMS_EPS) * w


def _rms_bwd(xf, w, dy):
    _, vjp = jax.vjp(_rms, xf, w)
    return vjp(dy)


def _s5_disc(lr, li, ldt, bre, bim):
    dt = jnp.exp(ldt)
    mag = jnp.exp(lr * dt)
    ar = mag * jnp.cos(li * dt)
    ai = mag * jnp.sin(li * dt)
    nr, ni = ar - 1.0, ai
    den = lr * lr + li * li
    zr = (nr * lr + ni * li) / den
    zi = (ni * lr - nr * li) / den
    return ar, ai, zr * bre - zi * bim, zr * bim + zi * bre


def _s5_prep(lr, li, ldt, bre, bim):
    def body(lr_r, li_r, ldt_r, bre_r, bim_r, ar_r, ai_r, br_r, bi_r):
        ar, ai, br, bi = _s5_disc(lr_r[...], li_r[...], ldt_r[...], bre_r[...], bim_r[...])
        ar_r[...] = ar
        ai_r[...] = ai
        br_r[...] = br
        bi_r[...] = bi

    sd = jax.ShapeDtypeStruct
    return pl.pallas_call(
        body, name="s5_prep",
        out_shape=[sd(lr.shape, F32), sd(lr.shape, F32), sd(bre.shape, F32), sd(bre.shape, F32)],
        compiler_params=_cparams(),
    )(lr, li, ldt, bre, bim)


def _s5_prep_bwd(lr, li, ldt, bre, bim, dar, dai, dbr, dbi):
    def body(lr_r, li_r, ldt_r, bre_r, bim_r, dar_r, dai_r, dbr_r, dbi_r, o0, o1, o2, o3, o4):
        _, vjp = jax.vjp(_s5_disc, lr_r[...], li_r[...], ldt_r[...], bre_r[...], bim_r[...])
        g = vjp((dar_r[...], dai_r[...], dbr_r[...], dbi_r[...]))
        for o, v in zip((o0, o1, o2, o3, o4), g):
            o[...] = v

    sd = jax.ShapeDtypeStruct
    return pl.pallas_call(
        body, name="s5_prep_bwd",
        out_shape=[sd(lr.shape, F32), sd(li.shape, F32), sd(ldt.shape, F32), sd(bre.shape, F32), sd(bim.shape, F32)],
        compiler_params=_cparams(),
    )(lr, li, ldt, bre, bim, dar, dai, dbr, dbi)


SCAN_T = 256


def _s5_scan(x, a):
    seq, width = x.shape
    w2 = 2 * STATE_BLOCK
    nj = width // w2
    t_blk = min(SCAN_T, seq)
    hb = STATE_BLOCK

    def body(x_ref, a_ref, o_ref, st):
        @pl.when(pl.program_id(1) == 0)
        def _():
            st[...] = jnp.zeros_like(st)

        ar = a_ref[:, :hb]
        ai = a_ref[:, hb:]

        def step(t, carry):
            sr, si = carry
            xr = x_ref[pl.ds(t, 1), :hb]
            xi = x_ref[pl.ds(t, 1), hb:]
            nr = ar * sr - ai * si + xr
            ni = ar * si + ai * sr + xi
            o_ref[pl.ds(t, 1), :hb] = nr
            o_ref[pl.ds(t, 1), hb:] = ni
            return nr, ni

        sr, si = lax.fori_loop(0, t_blk, step, (st[0:1, :], st[1:2, :]))
        st[0:1, :] = sr
        st[1:2, :] = si

    return pl.pallas_call(
        body, name="s5_scan", grid=(nj, seq // t_blk),
        in_specs=[pl.BlockSpec((t_blk, w2), lambda j, i: (i, j)), pl.BlockSpec((1, w2), lambda j, i: (0, j))],
        out_specs=pl.BlockSpec((t_blk, w2), lambda j, i: (i, j)),
        out_shape=jax.ShapeDtypeStruct(x.shape, F32),
        scratch_shapes=[pltpu.VMEM((SUBLANES, hb), F32)], compiler_params=_cparams(),
    )(x, a)


def _s5_scan_bwd(ds, s, a):
    seq, width = ds.shape
    w2 = 2 * STATE_BLOCK
    nj = width // w2
    t_blk = min(SCAN_T, seq)
    nb = seq // t_blk
    hb = STATE_BLOCK
    per8 = t_blk // SUBLANES

    def body(d_ref, s_ref, sprev_ref, a_ref, o_ref, da_ref, st):
        ib = pl.program_id(1)

        @pl.when(ib == 0)
        def _():
            st[...] = jnp.zeros_like(st)

        ar = a_ref[:, :hb]
        ai = a_ref[:, hb:]

        def step(tt, carry):
            lr, li = carry
            t = t_blk - 1 - tt
            dr = d_ref[pl.ds(t, 1), :hb]
            di = d_ref[pl.ds(t, 1), hb:]
            nr = ar * lr + ai * li + dr
            ni = ar * li - ai * lr + di
            o_ref[pl.ds(t, 1), :hb] = nr
            o_ref[pl.ds(t, 1), hb:] = ni
            return nr, ni

        lr, li = lax.fori_loop(0, t_blk, step, (st[0:1, :], st[1:2, :]))
        st[0:1, :] = lr
        st[1:2, :] = li

        lam = o_ref[...]
        sv = s_ref[...]
        rows = lax.broadcasted_iota(jnp.int32, sv.shape, 0)
        prev_last = sprev_ref[SUBLANES - 1:SUBLANES, :]
        prev_last = jnp.where(ib == nb - 1, jnp.zeros_like(prev_last), prev_last)
        s_sh = jnp.where(rows >= 1, pltpu.roll(sv, 1, 0), prev_last)
        lam_r, lam_i = lam[:, :hb], lam[:, hb:]
        sr_, si_ = s_sh[:, :hb], s_sh[:, hb:]
        dar = jnp.sum(lam_r * sr_ + lam_i * si_, axis=0, keepdims=True)
        dai = jnp.sum(lam_i * sr_ - lam_r * si_, axis=0, keepdims=True)
        contrib = jnp.concatenate([dar, dai], axis=1)

        @pl.when(ib == 0)
        def _():
            da_ref[...] = contrib

        @pl.when(ib != 0)
        def _():
            da_ref[...] += contrib

    blk = lambda j, i: (nb - 1 - i, j)
    prev_blk = lambda j, i: (jnp.maximum((nb - 1 - i) * per8 - 1, 0), j)
    return pl.pallas_call(
        body, name="s5_scan_bwd", grid=(nj, nb),
        in_specs=[pl.BlockSpec((t_blk, w2), blk), pl.BlockSpec((t_blk, w2), blk),
                  pl.BlockSpec((SUBLANES, w2), prev_blk), pl.BlockSpec((1, w2), lambda j, i: (0, j))],
        out_specs=[pl.BlockSpec((t_blk, w2), blk), pl.BlockSpec((1, w2), lambda j, i: (0, j))],
        out_shape=[jax.ShapeDtypeStruct(ds.shape, F32), jax.ShapeDtypeStruct((1, width), F32)],
        scratch_shapes=[pltpu.VMEM((SUBLANES, hb), F32)], compiler_params=_cparams(),
    )(ds, s, s, a)


def _rope128(x, cos, sa, sb):
    return x * cos + pltpu.roll(x, 96, 1) * sa + pltpu.roll(x, 32, 1) * sb


def _rope128_t(dy, cos, sa, sb):
    return dy * cos + pltpu.roll(dy * sa, 32, 1) + pltpu.roll(dy * sb, 96, 1)


ATT_BQ = 256


def _scores(q, kn, kp, i, bq, scale):
    s = lax.dot_general(q[:, :QK_NOPE], kn, (NT, ((), ())), preferred_element_type=F32)
    s = s + lax.dot_general(q[:, QK_NOPE:], kp, (NT, ((), ())), preferred_element_type=F32)
    s = s * scale
    row = i * bq + lax.broadcasted_iota(jnp.int32, s.shape, 0)
    col = lax.broadcasted_iota(jnp.int32, s.shape, 1)
    s = jnp.where(col <= row, s, jnp.finfo(F32).min)
    m = jnp.max(s, axis=-1, keepdims=True)
    e = jnp.exp(s - m)
    return e / jnp.sum(e, axis=-1, keepdims=True)


def _attn_fwd(q, kv, kpe):
    nh, seq, _ = q.shape
    bq = min(ATT_BQ, seq)
    scale = (QK_NOPE + QK_ROPE) ** -0.5

    def body(q_ref, kn_ref, v_ref, kp_ref, o_ref):
        p = _scores(q_ref[...], kn_ref[...], kp_ref[...], pl.program_id(1), bq, scale)
        o_ref[...] = jnp.dot(p.astype(BF16), v_ref[...], preferred_element_type=F32)

    return pl.pallas_call(
        body, name="attn_fwd", grid=(nh, seq // bq),
        in_specs=[pl.BlockSpec((None, bq, 256), lambda h, i: (h, i, 0)),
                  pl.BlockSpec((None, seq, 128), lambda h, i: (h, 0, 0)),
                  pl.BlockSpec((None, seq, 128), lambda h, i: (h, 0, 1)),
                  pl.BlockSpec((seq, 128), lambda h, i: (0, 0))],
        out_specs=pl.BlockSpec((bq, V_DIM), lambda h, i: (i, h)),
        out_shape=jax.ShapeDtypeStruct((seq, nh * V_DIM), F32), compiler_params=_cparams(),
    )(q, kv, kv, kpe)


def _attn_bwd(q, kv, kpe, do):
    nh, seq, _ = q.shape
    bq = min(ATT_BQ, seq)
    scale = (QK_NOPE + QK_ROPE) ** -0.5

    def body(q_ref, kn_ref, v_ref, kp_ref, do_ref, dq_ref, dkv_ref, dkp_ref):
        i = pl.program_id(1)
        q = q_ref[...]
        kn, v, kp = kn_ref[...], v_ref[...], kp_ref[...]
        p = _scores(q, kn, kp, i, bq, scale)
        dob = do_ref[...].astype(BF16)
        dp = lax.dot_general(dob, v, (NT, ((), ())), preferred_element_type=F32)
        ds = p * (dp - jnp.sum(p * dp, axis=-1, keepdims=True)) * scale
        dsb = ds.astype(BF16)
        pb = p.astype(BF16)
        dq_ref[:, :QK_NOPE] = jnp.dot(dsb, kn, preferred_element_type=F32)
        dq_ref[:, QK_NOPE:] = jnp.dot(dsb, kp, preferred_element_type=F32)
        dkn = lax.dot_general(dsb, q[:, :QK_NOPE], (TN, ((), ())), preferred_element_type=F32)
        dkp = lax.dot_general(dsb, q[:, QK_NOPE:], (TN, ((), ())), preferred_element_type=F32)
        dv = lax.dot_general(pb, dob, (TN, ((), ())), preferred_element_type=F32)

        @pl.when(i == 0)
        def _():
            dkv_ref[:, :QK_NOPE] = dkn
            dkv_ref[:, QK_NOPE:] = dv
            dkp_ref[...] = dkp

        @pl.when(i != 0)
        def _():
            dkv_ref[:, :QK_NOPE] += dkn
            dkv_ref[:, QK_NOPE:] += dv
            dkp_ref[...] += dkp

    sd = jax.ShapeDtypeStruct
    return pl.pallas_call(
        body, name="attn_bwd", grid=(nh, seq // bq),
        in_specs=[pl.BlockSpec((None, bq, 256), lambda h, i: (h, i, 0)),
                  pl.BlockSpec((None, seq, 128), lambda h, i: (h, 0, 0)),
                  pl.BlockSpec((None, seq, 128), lambda h, i: (h, 0, 1)),
                  pl.BlockSpec((seq, 128), lambda h, i: (0, 0)),
                  pl.BlockSpec((bq, V_DIM), lambda h, i: (i, h))],
        out_specs=[pl.BlockSpec((None, bq, 256), lambda h, i: (h, i, 0)),
                   pl.BlockSpec((None, seq, 256), lambda h, i: (h, 0, 0)),
                   pl.BlockSpec((None, seq, 128), lambda h, i: (h, 0, 0))],
        out_shape=[sd((nh, seq, 256), F32), sd((nh, seq, 256), F32), sd((nh, seq, 128), F32)],
        compiler_params=_cparams(),
    )(q, kv, kv, kpe, do)


def _conv3(a, w, b):
    rows = lax.broadcasted_iota(jnp.int32, a.shape, 0)
    a1 = jnp.where(rows >= 1, pltpu.roll(a, 1, 0), 0.0)
    a2 = jnp.where(rows >= 2, pltpu.roll(a, 2, 0), 0.0)
    return w[2:3] * a + w[1:2] * a1 + w[0:1] * a2 + b, a1, a2


def _conv_gate_fwd(a, cw, cb):
    ns, seq, c = a.shape
    half = ns // 2
    nc = c // LANES

    def fn(ga, va, wg, wv, bg, bv):
        gc, _, _ = _conv3(ga, wg, bg)
        vc, _, _ = _conv3(va, wv, bv)
        return gc * jax.nn.sigmoid(gc) * vc

    def a_spec(off):
        return pl.BlockSpec((None, seq, LANES), lambda k, j: (k + off, 0, j))

    def w_spec(off, r):
        return pl.BlockSpec((None, r, LANES), lambda k, j: (k + off, 0, j))

    return _blockwise(
        "conv_gate_fwd", fn, [a, a, cw, cw, cb, cb],
        [a_spec(0), a_spec(half), w_spec(0, 3), w_spec(half, 3), w_spec(0, 1), w_spec(half, 1)],
        [((half, seq, c), BF16)], [pl.BlockSpec((None, seq, LANES), lambda k, j: (k, 0, j))],
        grid=(half, nc))[0]


def _conv_gate_bwd(a, cw, cb, dm):
    ns, seq, c = a.shape
    half = ns // 2
    nc = c // LANES

    def body(own_ref, oth_ref, wo_ref, wt_ref, bo_ref, bt_ref, dm_ref, da_ref, dw_ref, db_ref):
        k = pl.program_id(0)
        own, wo = own_ref[...], wo_ref[...]
        co, a1, a2 = _conv3(own, wo, bo_ref[...])
        ct, _, _ = _conv3(oth_ref[...], wt_ref[...], bt_ref[...])
        dmv = dm_ref[...]
        so = jax.nn.sigmoid(co)
        st = jax.nn.sigmoid(ct)
        d_gate = dmv * ct * (so * (1.0 + co * (1.0 - so)))
        d_val = dmv * ct * st
        dc = jnp.where(k < half, d_gate, d_val)
        rows = lax.broadcasted_iota(jnp.int32, dc.shape, 0)
        up1 = jnp.where(rows < seq - 1, pltpu.roll(dc, seq - 1, 0), 0.0)
        up2 = jnp.where(rows < seq - 2, pltpu.roll(dc, seq - 2, 0), 0.0)
        da_ref[...] = (wo[2:3] * dc + wo[1:2] * up1 + wo[0:1] * up2).astype(da_ref.dtype)
        dw_ref[0:1, :] = jnp.sum(dc * a2, axis=0, keepdims=True)
        dw_ref[1:2, :] = jnp.sum(dc * a1, axis=0, keepdims=True)
        dw_ref[2:3, :] = jnp.sum(dc * own, axis=0, keepdims=True)
        db_ref[...] = jnp.sum(dc, axis=0, keepdims=True)

    def blk(r, other=False):
        if other:
            return pl.BlockSpec((None, r, LANES), lambda k, j: ((k + half) % ns, 0, j))
        return pl.BlockSpec((None, r, LANES), lambda k, j: (k, 0, j))

    sd = jax.ShapeDtypeStruct
    return pl.pallas_call(
        body, name="conv_gate_bwd", grid=(ns, nc),
        in_specs=[blk(seq), blk(seq, True), blk(3), blk(3, True), blk(1), blk(1, True),
                  pl.BlockSpec((None, seq, LANES), lambda k, j: (k % half, 0, j))],
        out_specs=[blk(seq), blk(3), blk(1)],
        out_shape=[sd((ns, seq, c), BF16), sd((ns, 3, c), F32), sd((ns, 1, c), F32)],
        compiler_params=_cparams(),
    )(a, a, cw, cw, cb, cb, dm)


ROW_T = 256


def _local_step(x, positions, target, w, emit=lambda **grads: None):
    seq, d = x.shape
    t_row = min(ROW_T, seq)
    nrow = seq // t_row
    ssm_w = d // 2
    nj = ssm_w // LANES
    n_groups = ssm_w // SSM_GROUP
    nh = w["wuq"].shape[0]
    q_rank = w["wuq"].shape[1]
    kv_rank = w["wukv"].shape[1]
    ns = w["wup"].shape[0]
    c_ff = w["wup"].shape[2]
    in_pad = w["win"].shape[1]
    tm = min(512, seq)
    nm = seq // tm
    sw = 2 * STATE_BLOCK
    g1 = (nrow,)

    lr3 = w["lam_re"].reshape(n_groups, 1, SSM_STATE)
    li3 = w["lam_im"].reshape(n_groups, 1, SSM_STATE)
    ldt3 = w["log_dt"].reshape(n_groups, 1, 1)
    bt_re = jnp.swapaxes(w["b_re"].reshape(n_groups, SSM_STATE, SSM_GROUP), 1, 2)
    bt_im = jnp.swapaxes(w["b_im"].reshape(n_groups, SSM_STATE, SSM_GROUP), 1, 2)
    abar_re, abar_im, bbt_re, bbt_im = _s5_prep(lr3, li3, ldt3, bt_re, bt_im)
    eye = jnp.eye(GROUPS_PER_BLOCK, dtype=F32)

    def blockdiag_in(bb):
        t = bb.reshape(nj, GROUPS_PER_BLOCK, SSM_GROUP, SSM_STATE)
        return jnp.einsum("jghp,gk->jghkp", t, eye).reshape(nj, LANES, STATE_BLOCK)

    def blockdiag_in_t(dwb):
        t = dwb.reshape(nj, GROUPS_PER_BLOCK, SSM_GROUP, GROUPS_PER_BLOCK, SSM_STATE)
        return jnp.einsum("jghkp,gk->jghp", t, eye).reshape(n_groups, SSM_GROUP, SSM_STATE)

    def blockdiag_out(cc):
        t = cc.reshape(nj, GROUPS_PER_BLOCK, SSM_GROUP, SSM_STATE)
        return jnp.einsum("jghp,gk->jkpgh", t, eye).reshape(nj, STATE_BLOCK, LANES)

    def blockdiag_out_t(dwc):
        t = dwc.reshape(nj, GROUPS_PER_BLOCK, SSM_STATE, GROUPS_PER_BLOCK, SSM_GROUP)
        return jnp.einsum("jkpgh,gk->jghp", t, eye).reshape(n_groups, SSM_GROUP, SSM_STATE)

    c_re = w["c_re"].reshape(n_groups, SSM_GROUP, SSM_STATE)
    c_im = w["c_im"].reshape(n_groups, SSM_GROUP, SSM_STATE)
    wb = jnp.concatenate([blockdiag_in(bbt_re), blockdiag_in(bbt_im)], axis=2).astype(BF16)
    wc = jnp.concatenate([blockdiag_out(c_re), -blockdiag_out(c_im)], axis=1).astype(BF16)
    a_lay = jnp.concatenate([abar_re.reshape(nj, 1, STATE_BLOCK), abar_im.reshape(nj, 1, STATE_BLOCK)],
                            axis=1).reshape(1, nj * sw)

    attn_w = w["attn_norm"]
    hn = _blockwise("norm1", lambda xb, wv: _rms(xb, wv), [x, attn_w], [_row_spec(t_row, d), _full_spec((1, d))],
                    [((seq, d), BF16)], [_row_spec(t_row, d)], g1)[0]
    proj = _mm2d("proj", hn, w["win"], NN, F32, tn=640, tk=1024)

    s0 = _mm("ssm_bu", proj, wb, grid=(nm, nj, 1),
             a_spec=pl.BlockSpec((tm, LANES), lambda i, j, k: (i, j)),
             b_spec=pl.BlockSpec((None, LANES, sw), lambda i, j, k: (j, 0, 0)),
             o_spec=pl.BlockSpec((tm, sw), lambda i, j, k: (i, j)),
             out_shape=(seq, nj * sw), out_dtype=F32)
    s_all = _s5_scan(s0, a_lay)
    ylin = _mm("ssm_cy", s_all, wc, grid=(nm, nj, 1),
               a_spec=pl.BlockSpec((tm, sw), lambda i, j, k: (i, j)),
               b_spec=pl.BlockSpec((None, sw, LANES), lambda i, j, k: (j, 0, 0)),
               o_spec=pl.BlockSpec((tm, LANES), lambda i, j, k: (i, j)),
               out_shape=(seq, ssm_w), out_dtype=F32)
    u_spec = pl.BlockSpec((t_row, ssm_w), lambda i: (i, 0))

    def ypre_fn(yl, ub, dsk):
        yp = yl + dsk * ub
        return yp, jax.nn.gelu(yp)

    y_pre, yg = _blockwise("ssm_gelu", ypre_fn, [ylin, proj, w["ssm_d"]],
                           [_row_spec(t_row, ssm_w), u_spec, _full_spec((1, ssm_w))],
                           [((seq, ssm_w), F32), ((seq, ssm_w), BF16)],
                           [_row_spec(t_row, ssm_w)] * 2, g1)
    z = _mm2d("ssm_glu", yg, w["wglu"], NN, F32, tk=1024, res=w["b_glu"])
    y_ssm = _blockwise("ssm_gate", lambda yp, zb: jax.nn.gelu(yp) * jax.nn.sigmoid(zb), [y_pre, z],
                       [_row_spec(t_row, ssm_w)] * 2, [((seq, ssm_w), F32)], [_row_spec(t_row, ssm_w)], g1)[0]

    cq_off, ckv_off, kpe_off = ssm_w, ssm_w + q_rank, ssm_w + q_rank + kv_rank
    c_q = proj[:, cq_off:ckv_off]
    c_kv = proj[:, ckv_off:kpe_off]
    kpe_raw = proj[:, kpe_off:kpe_off + LANES]
    pos_b = jnp.broadcast_to(positions.astype(F32)[:, None], (seq, LANES))
    inv_freq = ROPE_THETA ** (-jnp.arange(0, QK_ROPE, 2, dtype=F32) / QK_ROPE)
    inv128 = jnp.tile(inv_freq, 4).reshape(1, LANES)

    def mla_prep_fn(cq, ckv, kp, pb, inv, wq, wkv):
        ang = pb * inv
        lane = lax.broadcasted_iota(jnp.int32, ang.shape, 1)
        cs, sn = jnp.cos(ang), jnp.sin(ang)
        cos = jnp.where(lane < QK_ROPE, cs, 0.0)
        sa = jnp.where(lane < QK_ROPE // 2, -sn, 0.0)
        sb = jnp.where(jnp.logical_and(lane >= QK_ROPE // 2, lane < QK_ROPE), sn, 0.0)
        return _rms(cq, wq), _rms(ckv, wkv), _rope128(kp, cos, sa, sb), cos, sa, sb

    qn, kvn, kpe, cos_t, sa_t, sb_t = _blockwise(
        "mla_prep", mla_prep_fn, [c_q, c_kv, kpe_raw, pos_b, inv128, w["q_norm"], w["kv_norm"]],
        [_row_spec(t_row, q_rank), _row_spec(t_row, kv_rank), _row_spec(t_row, LANES), _row_spec(t_row, LANES),
         _full_spec((1, LANES)), _full_spec((1, q_rank)), _full_spec((1, kv_rank))],
        [((seq, q_rank), BF16), ((seq, kv_rank), BF16), ((seq, LANES), BF16)] + [((seq, LANES), F32)] * 3,
        [_row_spec(t_row, q_rank), _row_spec(t_row, kv_rank)] + [_row_spec(t_row, LANES)] * 4, g1)

    def head_mm(name, act, wh, out_dtype):
        kdim, ndim = wh.shape[1], wh.shape[2]
        return _mm(name, act, wh, grid=(nh, nm, 1),
                   a_spec=pl.BlockSpec((tm, kdim), lambda h, i, k: (i, 0)),
                   b_spec=pl.BlockSpec((None, kdim, ndim), lambda h, i, k: (h, 0, 0)),
                   o_spec=pl.BlockSpec((None, tm, ndim), lambda h, i, k: (h, i, 0)),
                   out_shape=(nh, seq, ndim), out_dtype=out_dtype)

    q_raw = head_mm("mla_q", qn, w["wuq"], F32)
    kv = head_mm("mla_kv", kvn, w["wukv"], BF16)
    hrow = pl.BlockSpec((None, t_row, 256), lambda h, i: (h, i, 0))
    tab = pl.BlockSpec((t_row, LANES), lambda h, i: (i, 0))

    def q_rope_fn(qb, cos, sa, sb):
        return jnp.concatenate([qb[:, :QK_NOPE], _rope128(qb[:, QK_NOPE:], cos, sa, sb)], axis=1)

    q = _blockwise("mla_q_rope", q_rope_fn, [q_raw, cos_t, sa_t, sb_t], [hrow, tab, tab, tab],
                   [((nh, seq, 256), BF16)], [hrow], (nh, nrow))[0]
    y_mla = _attn_fwd(q, kv, kpe)
    mla_w = nh * V_DIM

    def outnorm_fn(ys, ym, ws, wm):
        return jnp.concatenate([_rms(ys, ws), _rms(ym, wm)], axis=1)

    ycat = _blockwise("out_norm", outnorm_fn, [y_ssm, y_mla, w["son"], w["mon"]],
                      [_row_spec(t_row, ssm_w), _row_spec(t_row, mla_w), _full_spec((1, ssm_w)), _full_spec((1, mla_w))],
                      [((seq, d), BF16)], [_row_spec(t_row, d)], g1)[0]
    h1 = _mm2d("out_proj", ycat, w["wout"], NN, F32, tk=1024, res=x)

    hn2 = _blockwise("norm2", lambda hb, wv: _rms(hb, wv), [h1, w["ffn_norm"]],
                     [_row_spec(t_row, d), _full_spec((1, d))], [((seq, d), BF16)], [_row_spec(t_row, d)], g1)[0]
    tku = _tile(d, 512)
    a_ff = _mm("ffn_up", hn2, w["wup"], grid=(ns, nm, d // tku),
               a_spec=pl.BlockSpec((tm, tku), lambda s, i, k: (i, k)),
               b_spec=pl.BlockSpec((None, tku, c_ff), lambda s, i, k: (s, k, 0)),
               o_spec=pl.BlockSpec((None, tm, c_ff), lambda s, i, k: (s, i, 0)),
               out_shape=(ns, seq, c_ff), out_dtype=F32)
    cb3 = w["conv_b"].reshape(ns, 1, c_ff)
    m_ff = _conv_gate_fwd(a_ff, w["conv_w"], cb3)
    half = ns // 2
    wd4 = w["wdown"]
    tnd = _tile(d, 512)
    h2 = _mm("ffn_down", m_ff, wd4, grid=(nm, d // tnd, half),
             a_spec=pl.BlockSpec((None, tm, c_ff), lambda i, j, s: (s, i, 0)),
             b_spec=pl.BlockSpec((None, c_ff, tnd), lambda i, j, s: (s, 0, j)),
             o_spec=pl.BlockSpec((tm, tnd), lambda i, j, s: (i, j)),
             out_shape=(seq, d), out_dtype=F32,
             res=h1, res_spec=pl.BlockSpec((tm, tnd), lambda i, j, s: (i, j)))

    def loss_fn(hb, tb, wv):
        def f(hh, ww):
            err = _rms(hh, ww) - tb
            return 0.5 * jnp.sum(jnp.mean(err * err, axis=-1))

        lossv, (dh, dw) = jax.value_and_grad(f, argnums=(0, 1))(hb, wv)
        return dh, dh, jnp.full((1, LANES), lossv, F32), dw

    fin_w = w["final_norm"].reshape(1, d)
    dh2, dh2b, loss_acc, g_final = _blockwise(
        "loss_head", loss_fn, [h2, target, fin_w], [_row_spec(t_row, d), _row_spec(t_row, d), _full_spec((1, d))],
        [((seq, d), F32), ((seq, d), BF16), ((1, LANES), F32), ((1, d), F32)],
        [_row_spec(t_row, d), _row_spec(t_row, d), _full_spec((1, LANES)), _full_spec((1, d))], g1, n_acc=2)
    loss = loss_acc[0, 0]

    dm = _mm("ffn_down_dx", dh2b, wd4, grid=(half, nm, d // tku), contract=NT,
             a_spec=pl.BlockSpec((tm, tku), lambda s, i, k: (i, k)),
             b_spec=pl.BlockSpec((None, c_ff, tku), lambda s, i, k: (s, 0, k)),
             o_spec=pl.BlockSpec((None, tm, c_ff), lambda s, i, k: (s, i, 0)),
             out_shape=(half, seq, c_ff), out_dtype=F32)
    tks = _tile(seq, 512)
    g_wdown = _mm("ffn_down_dw", m_ff, dh2b, grid=(half, d // tnd, seq // tks), contract=TN,
                  a_spec=pl.BlockSpec((None, tks, c_ff), lambda s, j, k: (s, k, 0)),
                  b_spec=pl.BlockSpec((tks, tnd), lambda s, j, k: (k, j)),
                  o_spec=pl.BlockSpec((None, c_ff, tnd), lambda s, j, k: (s, 0, j)),
                  out_shape=(half, c_ff, d), out_dtype=BF16)
    emit(wdown=g_wdown)
    da_ff, g_convw, g_convb = _conv_gate_bwd(a_ff, w["conv_w"], cb3, dm)
    g_wup = _mm("ffn_up_dw", hn2, da_ff, grid=(ns, d // tnd, seq // tks), contract=TN,
                a_spec=pl.BlockSpec((tks, tnd), lambda s, j, k: (k, j)),
                b_spec=pl.BlockSpec((None, tks, c_ff), lambda s, j, k: (s, k, 0)),
                o_spec=pl.BlockSpec((None, tnd, c_ff), lambda s, j, k: (s, j, 0)),
                out_shape=(ns, d, c_ff), out_dtype=BF16)
    emit(wup=g_wup, conv_w=g_convw)
    dhn2 = _mm("ffn_up_dx", da_ff, w["wup"], grid=(nm, d // tnd, ns), contract=NT,
               a_spec=pl.BlockSpec((None, tm, c_ff), lambda i, j, s: (s, i, 0)),
               b_spec=pl.BlockSpec((None, tnd, c_ff), lambda i, j, s: (s, j, 0)),
               o_spec=pl.BlockSpec((tm, tnd), lambda i, j, s: (i, j)),
               out_shape=(seq, d), out_dtype=F32)

    def norm_bwd_fn(hb, dres, dn, wv):
        dx_, dw_ = _rms_bwd(hb, wv, dn)
        dtot = dres + dx_
        return dtot, dtot, dw_

    dh1, dh1b, g_ffn_norm = _blockwise(
        "norm2_bwd", norm_bwd_fn, [h1, dh2, dhn2, w["ffn_norm"]],
        [_row_spec(t_row, d)] * 3 + [_full_spec((1, d))],
        [((seq, d), F32), ((seq, d), BF16), ((1, d), F32)],
        [_row_spec(t_row, d), _row_spec(t_row, d), _full_spec((1, d))], g1, n_acc=1)

    dycat = _mm2d("out_proj_dx", dh1b, w["wout"], NT, F32, tk=1024)
    g_wout = _mm2d("out_proj_dw", ycat, dh1b, TN, BF16)

    def outnorm_bwd_fn(ys, ym, dyc, ws, wm):
        dys, dws = _rms_bwd(ys, ws, dyc[:, :ssm_w])
        dym, dwm = _rms_bwd(ym, wm, dyc[:, ssm_w:])
        return dys, dym, dws, dwm

    dy_ssm, dy_mla, g_son, g_mon = _blockwise(
        "out_norm_bwd", outnorm_bwd_fn, [y_ssm, y_mla, dycat, w["son"], w["mon"]],
        [_row_spec(t_row, ssm_w), _row_spec(t_row, mla_w), _row_spec(t_row, d), _full_spec((1, ssm_w)),
         _full_spec((1, mla_w))],
        [((seq, ssm_w), F32), ((seq, mla_w), F32), ((1, ssm_w), F32), ((1, mla_w), F32)],
        [_row_spec(t_row, ssm_w), _row_spec(t_row, mla_w), _full_spec((1, ssm_w)), _full_spec((1, mla_w))],
        g1, n_acc=2)

    def gate_bwd1_fn(dy, yp, zb):
        ygv = jax.nn.gelu(yp)
        sg = jax.nn.sigmoid(zb)
        dz = dy * ygv * sg * (1.0 - sg)
        return dz, jnp.sum(dz, axis=0, keepdims=True)

    dz, g_bglu = _blockwise("ssm_gate_bwd", gate_bwd1_fn, [dy_ssm, y_pre, z], [_row_spec(t_row, ssm_w)] * 3,
                            [((seq, ssm_w), BF16), ((1, ssm_w), F32)],
                            [_row_spec(t_row, ssm_w), _full_spec((1, ssm_w))], g1, n_acc=1)
    dyg2 = _mm2d("ssm_glu_dx", dz, w["wglu"], NT, F32, tk=1024)
    g_wglu = _mm2d("ssm_glu_dw", yg, dz, TN, BF16)

    def gelu_bwd_fn(dy, yp, zb, dg2, ub, dsk):
        dyg = dy * jax.nn.sigmoid(zb) + dg2
        _, vjp = jax.vjp(jax.nn.gelu, yp)
        dyp = vjp(dyg)[0]
        return dyp, dyp * dsk, jnp.sum(dyp * ub, axis=0, keepdims=True)

    dy_pre, du1, g_ssmd = _blockwise(
        "ssm_gelu_bwd", gelu_bwd_fn, [dy_ssm, y_pre, z, dyg2, proj, w["ssm_d"]],
        [_row_spec(t_row, ssm_w)] * 4 + [u_spec, _full_spec((1, ssm_w))],
        [((seq, ssm_w), BF16), ((seq, ssm_w), F32), ((1, ssm_w), F32)],
        [_row_spec(t_row, ssm_w), _row_spec(t_row, ssm_w), _full_spec((1, ssm_w))], g1, n_acc=1)
    ds_all = _mm("ssm_cy_dx", dy_pre, wc, grid=(nm, nj, 1), contract=NT,
                 a_spec=pl.BlockSpec((tm, LANES), lambda i, j, k: (i, j)),
                 b_spec=pl.BlockSpec((None, sw, LANES), lambda i, j, k: (j, 0, 0)),
                 o_spec=pl.BlockSpec((tm, sw), lambda i, j, k: (i, j)),
                 out_shape=(seq, nj * sw), out_dtype=F32)
    dwc = _mm("ssm_cy_dw", s_all, dy_pre, grid=(nj, 1, seq // tks), contract=TN,
              a_spec=pl.BlockSpec((tks, sw), lambda j, n, k: (k, j)),
              b_spec=pl.BlockSpec((tks, LANES), lambda j, n, k: (k, j)),
              o_spec=pl.BlockSpec((None, sw, LANES), lambda j, n, k: (j, 0, 0)),
              out_shape=(nj, sw, LANES), out_dtype=F32)
    lam, da_lay = _s5_scan_bwd(ds_all, s_all, a_lay)
    du = _mm("ssm_bu_dx", lam, wb, grid=(nm, nj, 1), contract=NT,
             a_spec=pl.BlockSpec((tm, sw), lambda i, j, k: (i, j)),
             b_spec=pl.BlockSpec((None, LANES, sw), lambda i, j, k: (j, 0, 0)),
             o_spec=pl.BlockSpec((tm, LANES), lambda i, j, k: (i, j)),
             out_shape=(seq, ssm_w), out_dtype=BF16,
             res=du1, res_spec=pl.BlockSpec((tm, LANES), lambda i, j, k: (i, j)))
    dwb = _mm("ssm_bu_dw", proj, lam, grid=(nj, 1, seq // tks), contract=TN,
              a_spec=pl.BlockSpec((tks, LANES), lambda j, n, k: (k, j)),
              b_spec=pl.BlockSpec((tks, sw), lambda j, n, k: (k, j)),
              o_spec=pl.BlockSpec((None, LANES, sw), lambda j, n, k: (j, 0, 0)),
              out_shape=(nj, LANES, sw), out_dtype=F32)
    g_c_re = blockdiag_out_t(dwc[:, :STATE_BLOCK, :])
    g_c_im = -blockdiag_out_t(dwc[:, STATE_BLOCK:, :])
    dbbt_re = blockdiag_in_t(dwb[:, :, :STATE_BLOCK])
    dbbt_im = blockdiag_in_t(dwb[:, :, STATE_BLOCK:])
    da3 = da_lay.reshape(nj, 2, STATE_BLOCK)
    dabar_re = da3[:, 0, :].reshape(n_groups, 1, SSM_STATE)
    dabar_im = da3[:, 1, :].reshape(n_groups, 1, SSM_STATE)
    g_lr3, g_li3, g_ldt3, g_bt_re, g_bt_im = _s5_prep_bwd(lr3, li3, ldt3, bt_re, bt_im,
                                                           dabar_re, dabar_im, dbbt_re, dbbt_im)

    dq, dkv, dkp_h = _attn_bwd(q, kv, kpe, dy_mla)

    def q_rope_bwd_fn(dqb, cos, sa, sb):
        return jnp.concatenate([dqb[:, :QK_NOPE], _rope128_t(dqb[:, QK_NOPE:], cos, sa, sb)], axis=1)

    dq_raw = _blockwise("mla_q_rope_bwd", q_rope_bwd_fn, [dq, cos_t, sa_t, sb_t], [hrow, tab, tab, tab],
                        [((nh, seq, 256), BF16)], [hrow], (nh, nrow))[0]

    def head_mm_dx(name, dact, wh):
        kdim, ndim = wh.shape[1], wh.shape[2]
        return _mm(name, dact, wh, grid=(nm, 1, nh), contract=NT,
                   a_spec=pl.BlockSpec((None, tm, ndim), lambda i, j, h: (h, i, 0)),
                   b_spec=pl.BlockSpec((None, kdim, ndim), lambda i, j, h: (h, 0, 0)),
                   o_spec=pl.BlockSpec((tm, kdim), lambda i, j, h: (i, 0)),
                   out_shape=(seq, kdim), out_dtype=F32)

    def head_mm_dw(name, act, dact):
        kdim, ndim = act.shape[1], dact.shape[2]
        return _mm(name, act, dact, grid=(nh, 1, seq // tks), contract=TN,
                   a_spec=pl.BlockSpec((tks, kdim), lambda h, j, k: (k, 0)),
                   b_spec=pl.BlockSpec((None, tks, ndim), lambda h, j, k: (h, k, 0)),
                   o_spec=pl.BlockSpec((None, kdim, ndim), lambda h, j, k: (h, 0, 0)),
                   out_shape=(nh, kdim, ndim), out_dtype=BF16)

    dqn = head_mm_dx("mla_q_dx", dq_raw, w["wuq"])
    g_wuq = head_mm_dw("mla_q_dw", qn, dq_raw)
    dkvn = head_mm_dx("mla_kv_dx", dkv, w["wukv"])
    g_wukv = head_mm_dw("mla_kv_dw", kvn, dkv)
    emit(wout=g_wout, wglu=g_wglu, wuq=g_wuq, wukv=g_wukv)

    def mla_prep_bwd_fn(cq, ckv, dqn_b, dkvn_b, dkp_b, cos, sa, sb, wq, wkv):
        dcq, dwq = _rms_bwd(cq, wq, dqn_b)
        dckv, dwkv = _rms_bwd(ckv, wkv, dkvn_b)
        dkp_sum = dkp_b[0]
        for h in range(1, nh):
            dkp_sum = dkp_sum + dkp_b[h]
        return dcq, dckv, _rope128_t(dkp_sum, cos, sa, sb), dwq, dwkv

    dc_q, dc_kv, dkpe_raw, g_qnorm, g_kvnorm = _blockwise(
        "mla_prep_bwd", mla_prep_bwd_fn, [c_q, c_kv, dqn, dkvn, dkp_h, cos_t, sa_t, sb_t, w["q_norm"], w["kv_norm"]],
        [_row_spec(t_row, q_rank), _row_spec(t_row, kv_rank), _row_spec(t_row, q_rank), _row_spec(t_row, kv_rank),
         pl.BlockSpec((nh, t_row, LANES), lambda i: (0, i, 0))] + [_row_spec(t_row, LANES)] * 3
        + [_full_spec((1, q_rank)), _full_spec((1, kv_rank))],
        [((seq, q_rank), BF16), ((seq, kv_rank), BF16), ((seq, LANES), BF16), ((1, q_rank), F32), ((1, kv_rank), F32)],
        [_row_spec(t_row, q_rank), _row_spec(t_row, kv_rank), _row_spec(t_row, LANES), _full_spec((1, q_rank)),
         _full_spec((1, kv_rank))], g1, n_acc=2)

    dproj = jnp.concatenate([du, dc_q, dc_kv, dkpe_raw], axis=1)
    dhn = _mm2d("proj_dx", dproj, w["win"], NT, F32, tk=640)
    g_win = _mm2d("proj_dw", hn, dproj, TN, BF16, tn=640)

    def norm1_bwd_fn(xb, dres, dn, wv):
        dx_, dw_ = _rms_bwd(xb, wv, dn)
        return dres + dx_, dw_

    grad_x, g_attn_norm = _blockwise(
        "norm1_bwd", norm1_bwd_fn, [x, dh1, dhn, attn_w], [_row_spec(t_row, d)] * 3 + [_full_spec((1, d))],
        [((seq, d), F32), ((1, d), F32)], [_row_spec(t_row, d), _full_spec((1, d))], g1, n_acc=1)

    grads = dict(
        attn_norm=g_attn_norm, win=g_win, lam_re=g_lr3, lam_im=g_li3, log_dt=g_ldt3,
        b_re=jnp.swapaxes(g_bt_re, 1, 2), b_im=jnp.swapaxes(g_bt_im, 1, 2), c_re=g_c_re, c_im=g_c_im,
        ssm_d=g_ssmd, wglu=g_wglu, b_glu=g_bglu, q_norm=g_qnorm, wuq=g_wuq, kv_norm=g_kvnorm, wukv=g_wukv,
        son=g_son, mon=g_mon, wout=g_wout, ffn_norm=g_ffn_norm, wup=g_wup, conv_w=g_convw, conv_b=g_convb,
        wdown=g_wdown, final_norm=g_final)
    return loss, grad_x, grads


def _mesh_pos():
    return lax.axis_index("x"), lax.axis_index("y"), lax.axis_index("c")


def _handshake_all():
    x, y, c = _mesh_pos()
    barrier = pltpu.get_barrier_semaphore()
    for k in range(1, N_DEV):
        peer = (1 - x if k & 4 else x, 1 - y if k & 2 else y, 1 - c if k & 1 else c)
        pl.semaphore_signal(barrier, inc=1, device_id=peer, device_id_type=MESH)
    pl.semaphore_wait(barrier, N_DEV - 1)


def _comm_call(name, body, n, out_shape, ins, collective_id):
    sems = [pltpu.SemaphoreType.DMA((7 * n,)), pltpu.SemaphoreType.DMA((7 * n,)), pltpu.SemaphoreType.DMA((n,))]
    if collective_id is None:
        any_spec = pl.BlockSpec(memory_space=pl.ANY)
        return pl.pallas_call(body, name=name, out_shape=out_shape, in_specs=[any_spec] * n,
                              out_specs=[any_spec] * n, scratch_shapes=sems)(*ins)
    return pl.kernel(body, name=name, out_type=out_shape,
                     mesh=plsc.ScalarSubcoreMesh(axis_name="seq", num_cores=1), scratch_types=sems,
                     compiler_params=pltpu.CompilerParams(collective_id=collective_id))(*ins)


def _all_gather(name, xs, collective_id=None):
    n = len(xs)

    def body(*refs):
        x_refs, o_refs = refs[:n], refs[n:2 * n]
        send_sems, recv_sems, local_sems = refs[2 * n:]
        if collective_id is not None:
            _handshake_all()
        x, y, c = _mesh_pos()
        me, sibling = (x, y, c), (x, y, 1 - c)
        chips = [(1 - x, y), (x, 1 - y), (1 - x, 1 - y)]

        def slot(o_ref, px, py, pc):
            return o_ref.at[4 * px + 2 * py + pc]

        def copy(t, k, block, to, src=None):
            dst = slot(o_refs[t], *block)
            return pltpu.make_async_remote_copy(
                src_ref=dst if src is None else src, dst_ref=dst,
                send_sem=send_sems.at[7 * t + k], recv_sem=recv_sems.at[7 * t + k],
                device_id=to, device_id_type=MESH)

        started = []
        for t in range(n):
            mine = pltpu.make_async_copy(x_refs[t], slot(o_refs[t], *me), local_sems.at[t])
            mine.start()
            started.append(mine)
        first = []
        for t in range(n):
            first.append(copy(t, 0, me, sibling, src=x_refs[t]))
            first += [copy(t, 1 + j, me, (*chip, c), src=x_refs[t]) for j, chip in enumerate(chips)]
        for cp in first:
            cp.start()
        passed = []
        for j, chip in enumerate(chips):
            for t in range(n):
                copy(t, 1 + j, (*chip, c), me).wait_recv()
                fwd = copy(t, 4 + j, (*chip, c), sibling)
                fwd.start()
                passed.append(fwd)
        for t in range(n):
            copy(t, 0, sibling, me).wait_recv()
            for j, chip in enumerate(chips):
                copy(t, 4 + j, (*chip, 1 - c), me).wait_recv()
        for cp in first + passed:
            cp.wait_send()
        for mine in started:
            mine.wait()

    out_shape = [jax.ShapeDtypeStruct((N_DEV,) + v.shape, v.dtype) for v in xs]
    return _comm_call(name, body, n, out_shape, xs, collective_id)


def _exchange_partials(name, gs, collective_id=None):
    n = len(gs)

    def body(*refs):
        g_refs, o_refs = refs[:n], refs[n:2 * n]
        send_sems, recv_sems, local_sems = refs[2 * n:]
        if collective_id is not None:
            _handshake_all()
        x, y, c = _mesh_pos()
        me_idx = 4 * x + 2 * y + c
        copies = []
        for t in range(n):
            mine = pltpu.make_async_copy(g_refs[t].at[me_idx], o_refs[t].at[me_idx], local_sems.at[t])
            mine.start()
            copies.append(mine)
        remote = []
        for k in range(1, N_DEV):
            px = 1 - x if k & 4 else x
            py = 1 - y if k & 2 else y
            pc = 1 - c if k & 1 else c
            p_idx = 4 * px + 2 * py + pc
            for t in range(n):
                cp = pltpu.make_async_remote_copy(
                    src_ref=g_refs[t].at[p_idx], dst_ref=o_refs[t].at[me_idx],
                    send_sem=send_sems.at[7 * t + k - 1], recv_sem=recv_sems.at[7 * t + k - 1],
                    device_id=(px, py, pc), device_id_type=MESH)
                cp.start()
                landing = pltpu.make_async_remote_copy(
                    src_ref=g_refs[t].at[p_idx], dst_ref=o_refs[t].at[p_idx],
                    send_sem=send_sems.at[7 * t + k - 1], recv_sem=recv_sems.at[7 * t + k - 1],
                    device_id=(px, py, pc), device_id_type=MESH)
                remote.append((cp, landing))
        for cp, landing in remote:
            landing.wait_recv()
        for cp, landing in remote:
            cp.wait_send()
        for mine in copies:
            mine.wait()

    out_shape = [jax.ShapeDtypeStruct(v.shape, v.dtype) for v in gs]
    return _comm_call(name, body, n, out_shape, gs, collective_id)


ADAM_BLOCK_ELEMS = 128 * 1024


def _adamw_sum(name, parts, wv, mv, vv):
    npart, r, c = parts.shape
    tr = r
    if r * c > ADAM_BLOCK_ELEMS and r % SUBLANES == 0:
        tr = SUBLANES
        while r % (tr * 2) == 0 and tr * 2 * c <= ADAM_BLOCK_ELEMS:
            tr *= 2
    bc1 = 1.0 - ADAM_B1 ** ADAM_STEP
    bc2 = 1.0 - ADAM_B2 ** ADAM_STEP

    def fn(pb, wb_, mb, vb):
        g = pb[0].astype(F32)
        for j in range(1, npart):
            g = g + pb[j].astype(F32)
        m_new = ADAM_B1 * mb + (1.0 - ADAM_B1) * g
        v_new = ADAM_B2 * vb + (1.0 - ADAM_B2) * (g * g)
        m_hat = m_new / bc1
        v_hat = v_new / bc2
        delta = -ADAM_LR * (m_hat / (jnp.sqrt(v_hat) + ADAM_EPS) + ADAM_WD * wb_)
        return g, delta, m_new, v_new

    row = pl.BlockSpec((tr, c), lambda i: (i, 0))
    return _blockwise(name, fn, [parts, wv, mv, vv],
                      [pl.BlockSpec((npart, tr, c), lambda i: (0, i, 0)), row, row, row],
                      [((r, c), F32)] * 4, [row] * 4, (r // tr,))


_SMALL = ["attn_norm", "lam_re", "lam_im", "log_dt", "b_re", "b_im", "c_re", "c_im", "ssm_d", "b_glu",
          "q_norm", "kv_norm", "son", "mon", "ffn_norm", "conv_b", "final_norm"]
_BIG = ["win", "wglu", "wuq", "wukv", "wout", "wup", "wdown", "conv_w"]
_ORDER = ["attn_norm", "win", "lam_re", "lam_im", "log_dt", "b_re", "b_im", "c_re", "c_im", "ssm_d", "wglu",
          "b_glu", "q_norm", "wuq", "kv_norm", "wukv", "son", "mon", "wout", "ffn_norm", "wup", "conv_w",
          "conv_b", "wdown", "final_norm"]


def _pack(arrs):
    flat = jnp.concatenate([a.reshape(-1).astype(F32) for a in arrs])
    pad = (-flat.shape[0]) % (LANES * LANES)
    return jnp.pad(flat, (0, pad)).reshape(-1, LANES)


def _unpack(packed, shapes):
    flat = packed.reshape(-1)
    out, off = [], 0
    for s in shapes:
        n = math.prod(s)
        out.append(flat[off:off + n].reshape(s))
        off += n
    return out


def kernel(x, positions, attn_norm_w, w_in, ssm_lambda_re, ssm_lambda_im, ssm_log_dt, ssm_b_re, ssm_b_im, ssm_c_re, ssm_c_im, ssm_d, ssm_w_glu, ssm_b_glu, mla_q_norm_w, mla_w_uq, mla_kv_norm_w, mla_w_ukv, ssm_out_norm_w, mla_out_norm_w, w_out, ffn_norm_w, ffn_w_up, ffn_conv_w, ffn_conv_b, ffn_w_down, final_norm_w, loss_target, m_attn_norm_w, m_w_in, m_ssm_lambda_re, m_ssm_lambda_im, m_ssm_log_dt, m_ssm_b_re, m_ssm_b_im, m_ssm_c_re, m_ssm_c_im, m_ssm_d, m_ssm_w_glu, m_ssm_b_glu, m_mla_q_norm_w, m_mla_w_uq, m_mla_kv_norm_w, m_mla_w_ukv, m_ssm_out_norm_w, m_mla_out_norm_w, m_w_out, m_ffn_norm_w, m_ffn_w_up, m_ffn_conv_w, m_ffn_conv_b, m_ffn_w_down, m_final_norm_w, v_attn_norm_w, v_w_in, v_ssm_lambda_re, v_ssm_lambda_im, v_ssm_log_dt, v_ssm_b_re, v_ssm_b_im, v_ssm_c_re, v_ssm_c_im, v_ssm_d, v_ssm_w_glu, v_ssm_b_glu, v_mla_q_norm_w, v_mla_w_uq, v_mla_kv_norm_w, v_mla_w_ukv, v_ssm_out_norm_w, v_mla_out_norm_w, v_w_out, v_ffn_norm_w, v_ffn_w_up, v_ffn_conv_w, v_ffn_conv_b, v_ffn_w_down, v_final_norm_w):
    wts = dict(attn_norm=attn_norm_w, win=w_in, lam_re=ssm_lambda_re, lam_im=ssm_lambda_im, log_dt=ssm_log_dt,
               b_re=ssm_b_re, b_im=ssm_b_im, c_re=ssm_c_re, c_im=ssm_c_im, ssm_d=ssm_d, wglu=ssm_w_glu,
               b_glu=ssm_b_glu, q_norm=mla_q_norm_w, wuq=mla_w_uq, kv_norm=mla_kv_norm_w, wukv=mla_w_ukv,
               son=ssm_out_norm_w, mon=mla_out_norm_w, wout=w_out, ffn_norm=ffn_norm_w, wup=ffn_w_up,
               conv_w=ffn_conv_w, conv_b=ffn_conv_b, wdown=ffn_w_down, final_norm=final_norm_w)
    moms = dict(zip(_ORDER, [m_attn_norm_w, m_w_in, m_ssm_lambda_re, m_ssm_lambda_im, m_ssm_log_dt, m_ssm_b_re,
                             m_ssm_b_im, m_ssm_c_re, m_ssm_c_im, m_ssm_d, m_ssm_w_glu, m_ssm_b_glu, m_mla_q_norm_w,
                             m_mla_w_uq, m_mla_kv_norm_w, m_mla_w_ukv, m_ssm_out_norm_w, m_mla_out_norm_w, m_w_out,
                             m_ffn_norm_w, m_ffn_w_up, m_ffn_conv_w, m_ffn_conv_b, m_ffn_w_down, m_final_norm_w]))
    vels = dict(zip(_ORDER, [v_attn_norm_w, v_w_in, v_ssm_lambda_re, v_ssm_lambda_im, v_ssm_log_dt, v_ssm_b_re,
                             v_ssm_b_im, v_ssm_c_re, v_ssm_c_im, v_ssm_d, v_ssm_w_glu, v_ssm_b_glu, v_mla_q_norm_w,
                             v_mla_w_uq, v_mla_kv_norm_w, v_mla_w_ukv, v_ssm_out_norm_w, v_mla_out_norm_w, v_w_out,
                             v_ffn_norm_w, v_ffn_w_up, v_ffn_conv_w, v_ffn_conv_b, v_ffn_w_down, v_final_norm_w]))
    seq, d = x.shape[1], x.shape[2]
    in_width = w_in.shape[2]
    in_pad = -(-in_width // LANES) * LANES
    q_cols = mla_w_uq.shape[2]
    q_pad = 2 * LANES

    (win_g,) = _all_gather("gather_w_in", [jnp.pad(w_in[0], ((0, 0), (0, in_pad - in_width))).astype(BF16)])
    wglu_g, wuq_g, wukv_g, wout_g, convw_g = _all_gather(
        "gather_mix", [ssm_w_glu[0].astype(BF16), jnp.pad(mla_w_uq[0], ((0, 0), (0, q_pad - q_cols))).astype(BF16),
                       mla_w_ukv[0].astype(BF16), w_out[0].astype(BF16), ffn_conv_w[0]], collective_id=0)
    (wup_g,) = _all_gather("gather_ffn_up", [ffn_w_up[0].astype(BF16)], collective_id=1)
    (wdown_g,) = _all_gather("gather_ffn_down", [ffn_w_down[0].astype(BF16)], collective_id=2)
    ns = N_DEV
    c_ff = wup_g.shape[2]
    w = dict(
        attn_norm=attn_norm_w, win=win_g.reshape(d, in_pad), lam_re=ssm_lambda_re, lam_im=ssm_lambda_im,
        log_dt=ssm_log_dt, b_re=ssm_b_re, b_im=ssm_b_im, c_re=ssm_c_re, c_im=ssm_c_im, ssm_d=ssm_d,
        wglu=wglu_g.reshape(d // 2, d // 2), b_glu=ssm_b_glu, q_norm=mla_q_norm_w, wuq=wuq_g,
        kv_norm=mla_kv_norm_w, wukv=wukv_g, son=ssm_out_norm_w, mon=mla_out_norm_w, wout=wout_g.reshape(d, d),
        ffn_norm=ffn_norm_w, wup=wup_g, conv_w=convw_g, conv_b=ffn_conv_b,
        wdown=wdown_g.reshape(ns // 2, c_ff, d), final_norm=final_norm_w)

    shard_layout = dict(
        win=lambda a: a[:, :in_width].reshape(N_DEV, d // N_DEV, in_width),
        wglu=lambda a: a.reshape(N_DEV, d // 2 // N_DEV, d // 2),
        wuq=lambda a: a[:, :, :q_cols], wukv=lambda a: a, wout=lambda a: a.reshape(N_DEV, d // N_DEV, d),
        wup=lambda a: a, wdown=lambda a: a.reshape(N_DEV, c_ff // 2, d), conv_w=lambda a: a)
    recv = {}
    next_id = [3]

    def exchange(sequencer=True, **grads):
        collective_id = None
        if sequencer:
            collective_id = next_id[0]
            next_id[0] += 1
        names = list(grads)
        got = _exchange_partials("exchange_" + "_".join(names), [shard_layout[k](grads[k]) for k in names],
                                 collective_id=collective_id)
        recv.update(zip(names, got))

    loss_part, grad_x, g = _local_step(x[0], positions[0], loss_target[0], w, emit=exchange)
    loss = lax.psum(loss_part, ("x", "y", "c"))
    exchange(sequencer=False, win=g["win"])
    small_shapes = [wts[k].shape for k in _SMALL]
    small_part = _pack([g[k] for k in _SMALL])
    small_all = _all_gather("gather_small_grads", [small_part])[0]

    out = {}
    for k in _BIG:
        shp = wts[k].shape
        r, c = shp[-2], shp[-1]
        res = _adamw_sum("adamw_" + k, recv[k].reshape(N_DEV, r, c), wts[k].reshape(r, c),
                         moms[k].reshape(r, c), vels[k].reshape(r, c))
        out[k] = [a.reshape(shp) for a in res]
    sw_ = _pack([wts[k] for k in _SMALL])
    sm_ = _pack([moms[k] for k in _SMALL])
    sv_ = _pack([vels[k] for k in _SMALL])
    res = _adamw_sum("adamw_small", small_all, sw_, sm_, sv_)
    unpacked = [_unpack(a, small_shapes) for a in res]
    for i, k in enumerate(_SMALL):
        out[k] = [u[i] for u in unpacked]

    grad_x = grad_x.reshape(x.shape)
    return (loss, grad_x, *[out[k][0] for k in _ORDER], *[out[k][1] for k in _ORDER],
            *[out[k][2] for k in _ORDER], *[out[k][3] for k in _ORDER])
```

```python
import functools
import math

import jax
import jax.numpy as jnp
from jax import lax
from jax.experimental import pallas as pl
from jax.experimental.pallas import tpu as pltpu
from jax.experimental.pallas import tpu_sc as plsc

F32 = jnp.float32
BF16 = jnp.bfloat16
MESH = pl.DeviceIdType.MESH

N_DEV = 8
LANES = 128
SUBLANES = 8
VMEM_LIMIT = 48 * 1024 * 1024

SSM_GROUP = 16
SSM_STATE = 64
GROUPS_PER_BLOCK = LANES // SSM_GROUP
STATE_BLOCK = GROUPS_PER_BLOCK * SSM_STATE
QK_NOPE = 128
QK_ROPE = 64
V_DIM = 128
ROPE_THETA = 10000.0
RMS_EPS = 1e-6

ADAM_LR = 0.001
ADAM_B1 = 0.9
ADAM_B2 = 0.999
ADAM_EPS = 1e-08
ADAM_WD = 0.01
ADAM_STEP = 10

NN = ((1,), (0,))
NT = ((1,), (1,))
TN = ((0,), (0,))


def _cparams():
    return pltpu.CompilerParams(vmem_limit_bytes=VMEM_LIMIT)


def _tile(n, want):
    if n <= want:
        return n
    t = (want // LANES) * LANES
    while t >= LANES:
        if n % t == 0:
            return t
        t -= LANES
    return n


def _mm(name, a, b, *, grid, a_spec, b_spec, o_spec, out_shape, out_dtype, contract=NN,
        res=None, res_spec=None):
    nk = grid[-1]
    kaxis = len(grid) - 1
    acc_shape = tuple(d for d in o_spec.block_shape if d is not None)

    def body(*refs):
        a_ref, b_ref = refs[:2]
        r_ref = None if res is None else refs[2]
        o_ref = refs[2 if res is None else 3]
        part = lax.dot_general(a_ref[...].astype(BF16), b_ref[...].astype(BF16),
                               (contract, ((), ())), preferred_element_type=F32)
        if nk == 1:
            if r_ref is not None:
                part = part + r_ref[...].astype(F32)
            o_ref[...] = part.astype(o_ref.dtype)
            return
        acc = refs[-1]
        k = pl.program_id(kaxis)

        @pl.when(k == 0)
        def _():
            acc[...] = part

        @pl.when(k != 0)
        def _():
            acc[...] += part

        @pl.when(k == nk - 1)
        def _():
            r = acc[...]
            if r_ref is not None:
                r = r + r_ref[...].astype(F32)
            o_ref[...] = r.astype(o_ref.dtype)

    ins = [a, b] + ([] if res is None else [res])
    in_specs = [a_spec, b_spec] + ([] if res is None else [res_spec])
    return pl.pallas_call(
        body, name=name, grid=grid, in_specs=in_specs, out_specs=o_spec,
        out_shape=jax.ShapeDtypeStruct(out_shape, out_dtype),
        scratch_shapes=[pltpu.VMEM(acc_shape, F32)] if nk > 1 else [], compiler_params=_cparams(),
    )(*ins)


def _mm2d(name, a, b, contract, out_dtype, tm=512, tn=512, tk=2048, res=None):
    if contract == NN:
        (m, kk), n = a.shape, b.shape[1]
    elif contract == NT:
        (m, kk), n = a.shape, b.shape[0]
    else:
        (kk, m), n = a.shape, b.shape[1]
    tm, tn, tk = _tile(m, tm), _tile(n, tn), _tile(kk, tk)
    grid = (m // tm, n // tn, kk // tk)
    if contract == TN:
        a_spec = pl.BlockSpec((tk, tm), lambda i, j, k: (k, i))
    else:
        a_spec = pl.BlockSpec((tm, tk), lambda i, j, k: (i, k))
    if contract == NT:
        b_spec = pl.BlockSpec((tn, tk), lambda i, j, k: (j, k))
    else:
        b_spec = pl.BlockSpec((tk, tn), lambda i, j, k: (k, j))
    o_spec = pl.BlockSpec((tm, tn), lambda i, j, k: (i, j))
    res_spec = None
    if res is not None:
        if res.shape[0] == 1:
            res_spec = pl.BlockSpec((1, tn), lambda i, j, k: (0, j))
        else:
            res_spec = pl.BlockSpec((tm, tn), lambda i, j, k: (i, j))
    return _mm(name, a, b, grid=grid, a_spec=a_spec, b_spec=b_spec, o_spec=o_spec,
               out_shape=(m, n), out_dtype=out_dtype, contract=contract, res=res, res_spec=res_spec)


def _blockwise(name, fn, ins, in_specs, outs, out_specs, grid, n_acc=0, acc_all=True):
    n_in, n_out = len(ins), len(outs)
    n_plain = n_out - n_acc

    def body(*refs):
        vals = fn(*[r[...] for r in refs[:n_in]])
        if not isinstance(vals, (tuple, list)):
            vals = (vals,)
        o_refs = refs[n_in:n_in + n_out]
        for r, v in zip(o_refs[:n_plain], vals[:n_plain]):
            r[...] = v.astype(r.dtype)
        if n_acc:
            if acc_all:
                first = functools.reduce(jnp.logical_and, [pl.program_id(d) == 0 for d in range(len(grid))])
            else:
                first = pl.program_id(len(grid) - 1) == 0

            @pl.when(first)
            def _():
                for r, v in zip(o_refs[n_plain:], vals[n_plain:]):
                    r[...] = v.astype(r.dtype)

            @pl.when(jnp.logical_not(first))
            def _():
                for r, v in zip(o_refs[n_plain:], vals[n_plain:]):
                    r[...] += v.astype(r.dtype)

    return pl.pallas_call(
        body, name=name, grid=grid, in_specs=in_specs, out_specs=out_specs,
        out_shape=[jax.ShapeDtypeStruct(s, d) for s, d in outs], compiler_params=_cparams(),
    )(*ins)


def _row_spec(t, c):
    return pl.BlockSpec((t, c), lambda i: (i, 0))


def _full_spec(shape):
    nd = len(shape)
    return pl.BlockSpec(tuple(shape), lambda *g: (0,) * nd)


def _rms(xf, w):
    return xf * lax.rsqrt(jnp.mean(xf * xf, axis=-1, keepdims=True) + RMS_EPS) * w


def _rms_bwd(xf, w, dy):
    _, vjp = jax.vjp(_rms, xf, w)
    return vjp(dy)


def _s5_disc(lr, li, ldt, bre, bim):
    dt = jnp.exp(ldt)
    mag = jnp.exp(lr * dt)
    ar = mag * jnp.cos(li * dt)
    ai = mag * jnp.sin(li * dt)
    nr, ni = ar - 1.0, ai
    den = lr * lr + li * li
    zr = (nr * lr + ni * li) / den
    zi = (ni * lr - nr * li) / den
    return ar, ai, zr * bre - zi * bim, zr * bim + zi * bre


def _s5_prep(lr, li, ldt, bre, bim):
    def body(lr_r, li_r, ldt_r, bre_r, bim_r, ar_r, ai_r, br_r, bi_r):
        ar, ai, br, bi = _s5_disc(lr_r[...], li_r[...], ldt_r[...], bre_r[...], bim_r[...])
        ar_r[...] = ar
        ai_r[...] = ai
        br_r[...] = br
        bi_r[...] = bi

    sd = jax.ShapeDtypeStruct
    return pl.pallas_call(
        body, name="s5_prep",
        out_shape=[sd(lr.shape, F32), sd(lr.shape, F32), sd(bre.shape, F32), sd(bre.shape, F32)],
        compiler_params=_cparams(),
    )(lr, li, ldt, bre, bim)


def _s5_prep_bwd(lr, li, ldt, bre, bim, dar, dai, dbr, dbi):
    def body(lr_r, li_r, ldt_r, bre_r, bim_r, dar_r, dai_r, dbr_r, dbi_r, o0, o1, o2, o3, o4):
        _, vjp = jax.vjp(_s5_disc, lr_r[...], li_r[...], ldt_r[...], bre_r[...], bim_r[...])
        g = vjp((dar_r[...], dai_r[...], dbr_r[...], dbi_r[...]))
        for o, v in zip((o0, o1, o2, o3, o4), g):
            o[...] = v

    sd = jax.ShapeDtypeStruct
    return pl.pallas_call(
        body, name="s5_prep_bwd",
        out_shape=[sd(lr.shape, F32), sd(li.shape, F32), sd(ldt.shape, F32), sd(bre.shape, F32), sd(bim.shape, F32)],
        compiler_params=_cparams(),
    )(lr, li, ldt, bre, bim, dar, dai, dbr, dbi)


SCAN_T = 256


def _s5_scan(x, a):
    seq, width = x.shape
    w2 = 2 * STATE_BLOCK
    nj = width // w2
    t_blk = min(SCAN_T, seq)
    hb = STATE_BLOCK

    def body(x_ref, a_ref, o_ref, st):
        @pl.when(pl.program_id(1) == 0)
        def _():
            st[...] = jnp.zeros_like(st)

        ar = a_ref[:, :hb]
        ai = a_ref[:, hb:]

        def step(t, carry):
            sr, si = carry
            xr = x_ref[pl.ds(t, 1), :hb]
            xi = x_ref[pl.ds(t, 1), hb:]
            nr = ar * sr - ai * si + xr
            ni = ar * si + ai * sr + xi
            o_ref[pl.ds(t, 1), :hb] = nr
            o_ref[pl.ds(t, 1), hb:] = ni
            return nr, ni

        sr, si = lax.fori_loop(0, t_blk, step, (st[0:1, :], st[1:2, :]))
        st[0:1, :] = sr
        st[1:2, :] = si

    return pl.pallas_call(
        body, name="s5_scan", grid=(nj, seq // t_blk),
        in_specs=[pl.BlockSpec((t_blk, w2), lambda j, i: (i, j)), pl.BlockSpec((1, w2), lambda j, i: (0, j))],
        out_specs=pl.BlockSpec((t_blk, w2), lambda j, i: (i, j)),
        out_shape=jax.ShapeDtypeStruct(x.shape, F32),
        scratch_shapes=[pltpu.VMEM((SUBLANES, hb), F32)], compiler_params=_cparams(),
    )(x, a)


def _s5_scan_bwd(ds, s, a):
    seq, width = ds.shape
    w2 = 2 * STATE_BLOCK
    nj = width // w2
    t_blk = min(SCAN_T, seq)
    nb = seq // t_blk
    hb = STATE_BLOCK
    per8 = t_blk // SUBLANES

    def body(d_ref, s_ref, sprev_ref, a_ref, o_ref, da_ref, st):
        ib = pl.program_id(1)

        @pl.when(ib == 0)
        def _():
            st[...] = jnp.zeros_like(st)

        ar = a_ref[:, :hb]
        ai = a_ref[:, hb:]

        def step(tt, carry):
            lr, li = carry
            t = t_blk - 1 - tt
            dr = d_ref[pl.ds(t, 1), :hb]
            di = d_ref[pl.ds(t, 1), hb:]
            nr = ar * lr + ai * li + dr
            ni = ar * li - ai * lr + di
            o_ref[pl.ds(t, 1), :hb] = nr
            o_ref[pl.ds(t, 1), hb:] = ni
            return nr, ni

        lr, li = lax.fori_loop(0, t_blk, step, (st[0:1, :], st[1:2, :]))
        st[0:1, :] = lr
        st[1:2, :] = li

        lam = o_ref[...]
        sv = s_ref[...]
        rows = lax.broadcasted_iota(jnp.int32, sv.shape, 0)
        prev_last = sprev_ref[SUBLANES - 1:SUBLANES, :]
        prev_last = jnp.where(ib == nb - 1, jnp.zeros_like(prev_last), prev_last)
        s_sh = jnp.where(rows >= 1, pltpu.roll(sv, 1, 0), prev_last)
        lam_r, lam_i = lam[:, :hb], lam[:, hb:]
        sr_, si_ = s_sh[:, :hb], s_sh[:, hb:]
        dar = jnp.sum(lam_r * sr_ + lam_i * si_, axis=0, keepdims=True)
        dai = jnp.sum(lam_i * sr_ - lam_r * si_, axis=0, keepdims=True)
        contrib = jnp.concatenate([dar, dai], axis=1)

        @pl.when(ib == 0)
        def _():
            da_ref[...] = contrib

        @pl.when(ib != 0)
        def _():
            da_ref[...] += contrib

    blk = lambda j, i: (nb - 1 - i, j)
    prev_blk = lambda j, i: (jnp.maximum((nb - 1 - i) * per8 - 1, 0), j)
    return pl.pallas_call(
        body, name="s5_scan_bwd", grid=(nj, nb),
        in_specs=[pl.BlockSpec((t_blk, w2), blk), pl.BlockSpec((t_blk, w2), blk),
                  pl.BlockSpec((SUBLANES, w2), prev_blk), pl.BlockSpec((1, w2), lambda j, i: (0, j))],
        out_specs=[pl.BlockSpec((t_blk, w2), blk), pl.BlockSpec((1, w2), lambda j, i: (0, j))],
        out_shape=[jax.ShapeDtypeStruct(ds.shape, F32), jax.ShapeDtypeStruct((1, width), F32)],
        scratch_shapes=[pltpu.VMEM((SUBLANES, hb), F32)], compiler_params=_cparams(),
    )(ds, s, s, a)


def _rope128(x, cos, sa, sb):
    return x * cos + pltpu.roll(x, 96, 1) * sa + pltpu.roll(x, 32, 1) * sb


def _rope128_t(dy, cos, sa, sb):
    return dy * cos + pltpu.roll(dy * sa, 32, 1) + pltpu.roll(dy * sb, 96, 1)


ATT_BQ = 256


def _scores(q, kn, kp, i, bq, scale):
    s = lax.dot_general(q[:, :QK_NOPE], kn, (NT, ((), ())), preferred_element_type=F32)
    s = s + lax.dot_general(q[:, QK_NOPE:], kp, (NT, ((), ())), preferred_element_type=F32)
    s = s * scale
    row = i * bq + lax.broadcasted_iota(jnp.int32, s.shape, 0)
    col = lax.broadcasted_iota(jnp.int32, s.shape, 1)
    s = jnp.where(col <= row, s, jnp.finfo(F32).min)
    m = jnp.max(s, axis=-1, keepdims=True)
    e = jnp.exp(s - m)
    return e / jnp.sum(e, axis=-1, keepdims=True)


def _attn_fwd(q, kv, kpe):
    nh, seq, _ = q.shape
    bq = min(ATT_BQ, seq)
    scale = (QK_NOPE + QK_ROPE) ** -0.5

    def body(q_ref, kn_ref, v_ref, kp_ref, o_ref):
        p = _scores(q_ref[...], kn_ref[...], kp_ref[...], pl.program_id(1), bq, scale)
        o_ref[...] = jnp.dot(p.astype(BF16), v_ref[...], preferred_element_type=F32)

    return pl.pallas_call(
        body, name="attn_fwd", grid=(nh, seq // bq),
        in_specs=[pl.BlockSpec((None, bq, 256), lambda h, i: (h, i, 0)),
                  pl.BlockSpec((None, seq, 128), lambda h, i: (h, 0, 0)),
                  pl.BlockSpec((None, seq, 128), lambda h, i: (h, 0, 1)),
                  pl.BlockSpec((seq, 128), lambda h, i: (0, 0))],
        out_specs=pl.BlockSpec((bq, V_DIM), lambda h, i: (i, h)),
        out_shape=jax.ShapeDtypeStruct((seq, nh * V_DIM), F32), compiler_params=_cparams(),
    )(q, kv, kv, kpe)


def _attn_bwd(q, kv, kpe, do):
    nh, seq, _ = q.shape
    bq = min(ATT_BQ, seq)
    scale = (QK_NOPE + QK_ROPE) ** -0.5

    def body(q_ref, kn_ref, v_ref, kp_ref, do_ref, dq_ref, dkv_ref, dkp_ref):
        i = pl.program_id(1)
        q = q_ref[...]
        kn, v, kp = kn_ref[...], v_ref[...], kp_ref[...]
        p = _scores(q, kn, kp, i, bq, scale)
        dob = do_ref[...].astype(BF16)
        dp = lax.dot_general(dob, v, (NT, ((), ())), preferred_element_type=F32)
        ds = p * (dp - jnp.sum(p * dp, axis=-1, keepdims=True)) * scale
        dsb = ds.astype(BF16)
        pb = p.astype(BF16)
        dq_ref[:, :QK_NOPE] = jnp.dot(dsb, kn, preferred_element_type=F32)
        dq_ref[:, QK_NOPE:] = jnp.dot(dsb, kp, preferred_element_type=F32)
        dkn = lax.dot_general(dsb, q[:, :QK_NOPE], (TN, ((), ())), preferred_element_type=F32)
        dkp = lax.dot_general(dsb, q[:, QK_NOPE:], (TN, ((), ())), preferred_element_type=F32)
        dv = lax.dot_general(pb, dob, (TN, ((), ())), preferred_element_type=F32)

        @pl.when(i == 0)
        def _():
            dkv_ref[:, :QK_NOPE] = dkn
            dkv_ref[:, QK_NOPE:] = dv
            dkp_ref[...] = dkp

        @pl.when(i != 0)
        def _():
            dkv_ref[:, :QK_NOPE] += dkn
            dkv_ref[:, QK_NOPE:] += dv
            dkp_ref[...] += dkp

    sd = jax.ShapeDtypeStruct
    return pl.pallas_call(
        body, name="attn_bwd", grid=(nh, seq // bq),
        in_specs=[pl.BlockSpec((None, bq, 256), lambda h, i: (h, i, 0)),
                  pl.BlockSpec((None, seq, 128), lambda h, i: (h, 0, 0)),
                  pl.BlockSpec((None, seq, 128), lambda h, i: (h, 0, 1)),
                  pl.BlockSpec((seq, 128), lambda h, i: (0, 0)),
                  pl.BlockSpec((bq, V_DIM), lambda h, i: (i, h))],
        out_specs=[pl.BlockSpec((None, bq, 256), lambda h, i: (h, i, 0)),
                   pl.BlockSpec((None, seq, 256), lambda h, i: (h, 0, 0)),
                   pl.BlockSpec((None, seq, 128), lambda h, i: (h, 0, 0))],
        out_shape=[sd((nh, seq, 256), F32), sd((nh, seq, 256), F32), sd((nh, seq, 128), F32)],
        compiler_params=_cparams(),
    )(q, kv, kv, kpe, do)


def _conv3(a, w, b):
    rows = lax.broadcasted_iota(jnp.int32, a.shape, 0)
    a1 = jnp.where(rows >= 1, pltpu.roll(a, 1, 0), 0.0)
    a2 = jnp.where(rows >= 2, pltpu.roll(a, 2, 0), 0.0)
    return w[2:3] * a + w[1:2] * a1 + w[0:1] * a2 + b, a1, a2


def _conv_gate_fwd(a, cw, cb):
    ns, seq, c = a.shape
    half = ns // 2
    nc = c // LANES

    def fn(ga, va, wg, wv, bg, bv):
        gc, _, _ = _conv3(ga, wg, bg)
        vc, _, _ = _conv3(va, wv, bv)
        return gc * jax.nn.sigmoid(gc) * vc

    def a_spec(off):
        return pl.BlockSpec((None, seq, LANES), lambda k, j: (k + off, 0, j))

    def w_spec(off, r):
        return pl.BlockSpec((None, r, LANES), lambda k, j: (k + off, 0, j))

    return _blockwise(
        "conv_gate_fwd", fn, [a, a, cw, cw, cb, cb],
        [a_spec(0), a_spec(half), w_spec(0, 3), w_spec(half, 3), w_spec(0, 1), w_spec(half, 1)],
        [((half, seq, c), BF16)], [pl.BlockSpec((None, seq, LANES), lambda k, j: (k, 0, j))],
        grid=(half, nc))[0]


def _conv_gate_bwd(a, cw, cb, dm):
    ns, seq, c = a.shape
    half = ns // 2
    nc = c // LANES

    def body(own_ref, oth_ref, wo_ref, wt_ref, bo_ref, bt_ref, dm_ref, da_ref, dw_ref, db_ref):
        k = pl.program_id(0)
        own, wo = own_ref[...], wo_ref[...]
        co, a1, a2 = _conv3(own, wo, bo_ref[...])
        ct, _, _ = _conv3(oth_ref[...], wt_ref[...], bt_ref[...])
        dmv = dm_ref[...]
        so = jax.nn.sigmoid(co)
        st = jax.nn.sigmoid(ct)
        d_gate = dmv * ct * (so * (1.0 + co * (1.0 - so)))
        d_val = dmv * ct * st
        dc = jnp.where(k < half, d_gate, d_val)
        rows = lax.broadcasted_iota(jnp.int32, dc.shape, 0)
        up1 = jnp.where(rows < seq - 1, pltpu.roll(dc, seq - 1, 0), 0.0)
        up2 = jnp.where(rows < seq - 2, pltpu.roll(dc, seq - 2, 0), 0.0)
        da_ref[...] = (wo[2:3] * dc + wo[1:2] * up1 + wo[0:1] * up2).astype(da_ref.dtype)
        dw_ref[0:1, :] = jnp.sum(dc * a2, axis=0, keepdims=True)
        dw_ref[1:2, :] = jnp.sum(dc * a1, axis=0, keepdims=True)
        dw_ref[2:3, :] = jnp.sum(dc * own, axis=0, keepdims=True)
        db_ref[...] = jnp.sum(dc, axis=0, keepdims=True)

    def blk(r, other=False):
        if other:
            return pl.BlockSpec((None, r, LANES), lambda k, j: ((k + half) % ns, 0, j))
        return pl.BlockSpec((None, r, LANES), lambda k, j: (k, 0, j))

    sd = jax.ShapeDtypeStruct
    return pl.pallas_call(
        body, name="conv_gate_bwd", grid=(ns, nc),
        in_specs=[blk(seq), blk(seq, True), blk(3), blk(3, True), blk(1), blk(1, True),
                  pl.BlockSpec((None, seq, LANES), lambda k, j: (k % half, 0, j))],
        out_specs=[blk(seq), blk(3), blk(1)],
        out_shape=[sd((ns, seq, c), BF16), sd((ns, 3, c), F32), sd((ns, 1, c), F32)],
        compiler_params=_cparams(),
    )(a, a, cw, cw, cb, cb, dm)


ROW_T = 256


def _local_step(x, positions, target, w, emit=lambda **grads: None):
    seq, d = x.shape
    t_row = min(ROW_T, seq)
    nrow = seq // t_row
    ssm_w = d // 2
    nj = ssm_w // LANES
    n_groups = ssm_w // SSM_GROUP
    nh = w["wuq"].shape[0]
    q_rank = w["wuq"].shape[1]
    kv_rank = w["wukv"].shape[1]
    ns = w["wup"].shape[0]
    c_ff = w["wup"].shape[2]
    in_pad = w["win"].shape[1]
    tm = min(512, seq)
    nm = seq // tm
    sw = 2 * STATE_BLOCK
    g1 = (nrow,)

    lr3 = w["lam_re"].reshape(n_groups, 1, SSM_STATE)
    li3 = w["lam_im"].reshape(n_groups, 1, SSM_STATE)
    ldt3 = w["log_dt"].reshape(n_groups, 1, 1)
    bt_re = jnp.swapaxes(w["b_re"].reshape(n_groups, SSM_STATE, SSM_GROUP), 1, 2)
    bt_im = jnp.swapaxes(w["b_im"].reshape(n_groups, SSM_STATE, SSM_GROUP), 1, 2)
    abar_re, abar_im, bbt_re, bbt_im = _s5_prep(lr3, li3, ldt3, bt_re, bt_im)
    eye = jnp.eye(GROUPS_PER_BLOCK, dtype=F32)

    def blockdiag_in(bb):
        t = bb.reshape(nj, GROUPS_PER_BLOCK, SSM_GROUP, SSM_STATE)
        return jnp.einsum("jghp,gk->jghkp", t, eye).reshape(nj, LANES, STATE_BLOCK)

    def blockdiag_in_t(dwb):
        t = dwb.reshape(nj, GROUPS_PER_BLOCK, SSM_GROUP, GROUPS_PER_BLOCK, SSM_STATE)
        return jnp.einsum("jghkp,gk->jghp", t, eye).reshape(n_groups, SSM_GROUP, SSM_STATE)

    def blockdiag_out(cc):
        t = cc.reshape(nj, GROUPS_PER_BLOCK, SSM_GROUP, SSM_STATE)
        return jnp.einsum("jghp,gk->jkpgh", t, eye).reshape(nj, STATE_BLOCK, LANES)

    def blockdiag_out_t(dwc):
        t = dwc.reshape(nj, GROUPS_PER_BLOCK, SSM_STATE, GROUPS_PER_BLOCK, SSM_GROUP)
        return jnp.einsum("jkpgh,gk->jghp", t, eye).reshape(n_groups, SSM_GROUP, SSM_STATE)

    c_re = w["c_re"].reshape(n_groups, SSM_GROUP, SSM_STATE)
    c_im = w["c_im"].reshape(n_groups, SSM_GROUP, SSM_STATE)
    wb = jnp.concatenate([blockdiag_in(bbt_re), blockdiag_in(bbt_im)], axis=2).astype(BF16)
    wc = jnp.concatenate([blockdiag_out(c_re), -blockdiag_out(c_im)], axis=1).astype(BF16)
    a_lay = jnp.concatenate([abar_re.reshape(nj, 1, STATE_BLOCK), abar_im.reshape(nj, 1, STATE_BLOCK)],
                            axis=1).reshape(1, nj * sw)

    attn_w = w["attn_norm"]
    hn = _blockwise("norm1", lambda xb, wv: _rms(xb, wv), [x, attn_w], [_row_spec(t_row, d), _full_spec((1, d))],
                    [((seq, d), BF16)], [_row_spec(t_row, d)], g1)[0]
    proj = _mm2d("proj", hn, w["win"], NN, F32, tn=640)

    s0 = _mm("ssm_bu", proj, wb, grid=(nm, nj, 1),
             a_spec=pl.BlockSpec((tm, LANES), lambda i, j, k: (i, j)),
             b_spec=pl.BlockSpec((None, LANES, sw), lambda i, j, k: (j, 0, 0)),
             o_spec=pl.BlockSpec((tm, sw), lambda i, j, k: (i, j)),
             out_shape=(seq, nj * sw), out_dtype=F32)
    s_all = _s5_scan(s0, a_lay)
    ylin = _mm("ssm_cy", s_all, wc, grid=(nm, nj, 1),
               a_spec=pl.BlockSpec((tm, sw), lambda i, j, k: (i, j)),
               b_spec=pl.BlockSpec((None, sw, LANES), lambda i, j, k: (j, 0, 0)),
               o_spec=pl.BlockSpec((tm, LANES), lambda i, j, k: (i, j)),
               out_shape=(seq, ssm_w), out_dtype=F32)
    u_spec = pl.BlockSpec((t_row, ssm_w), lambda i: (i, 0))

    def ypre_fn(yl, ub, dsk):
        yp = yl + dsk * ub
        return yp, jax.nn.gelu(yp)

    y_pre, yg = _blockwise("ssm_gelu", ypre_fn, [ylin, proj, w["ssm_d"]],
                           [_row_spec(t_row, ssm_w), u_spec, _full_spec((1, ssm_w))],
                           [((seq, ssm_w), F32), ((seq, ssm_w), BF16)],
                           [_row_spec(t_row, ssm_w)] * 2, g1)
    z = _mm2d("ssm_glu", yg, w["wglu"], NN, F32, res=w["b_glu"])
    y_ssm = _blockwise("ssm_gate", lambda yp, zb: jax.nn.gelu(yp) * jax.nn.sigmoid(zb), [y_pre, z],
                       [_row_spec(t_row, ssm_w)] * 2, [((seq, ssm_w), F32)], [_row_spec(t_row, ssm_w)], g1)[0]

    cq_off, ckv_off, kpe_off = ssm_w, ssm_w + q_rank, ssm_w + q_rank + kv_rank
    c_q = proj[:, cq_off:ckv_off]
    c_kv = proj[:, ckv_off:kpe_off]
    kpe_raw = proj[:, kpe_off:kpe_off + LANES]
    pos_b = jnp.broadcast_to(positions.astype(F32)[:, None], (seq, LANES))
    inv_freq = ROPE_THETA ** (-jnp.arange(0, QK_ROPE, 2, dtype=F32) / QK_ROPE)
    inv128 = jnp.tile(inv_freq, 4).reshape(1, LANES)

    def mla_prep_fn(cq, ckv, kp, pb, inv, wq, wkv):
        ang = pb * inv
        lane = lax.broadcasted_iota(jnp.int32, ang.shape, 1)
        cs, sn = jnp.cos(ang), jnp.sin(ang)
        cos = jnp.where(lane < QK_ROPE, cs, 0.0)
        sa = jnp.where(lane < QK_ROPE // 2, -sn, 0.0)
        sb = jnp.where(jnp.logical_and(lane >= QK_ROPE // 2, lane < QK_ROPE), sn, 0.0)
        return _rms(cq, wq), _rms(ckv, wkv), _rope128(kp, cos, sa, sb), cos, sa, sb

    qn, kvn, kpe, cos_t, sa_t, sb_t = _blockwise(
        "mla_prep", mla_prep_fn, [c_q, c_kv, kpe_raw, pos_b, inv128, w["q_norm"], w["kv_norm"]],
        [_row_spec(t_row, q_rank), _row_spec(t_row, kv_rank), _row_spec(t_row, LANES), _row_spec(t_row, LANES),
         _full_spec((1, LANES)), _full_spec((1, q_rank)), _full_spec((1, kv_rank))],
        [((seq, q_rank), BF16), ((seq, kv_rank), BF16), ((seq, LANES), BF16)] + [((seq, LANES), F32)] * 3,
        [_row_spec(t_row, q_rank), _row_spec(t_row, kv_rank)] + [_row_spec(t_row, LANES)] * 4, g1)

    def head_mm(name, act, wh, out_dtype):
        kdim, ndim = wh.shape[1], wh.shape[2]
        return _mm(name, act, wh, grid=(nh, nm, 1),
                   a_spec=pl.BlockSpec((tm, kdim), lambda h, i, k: (i, 0)),
                   b_spec=pl.BlockSpec((None, kdim, ndim), lambda h, i, k: (h, 0, 0)),
                   o_spec=pl.BlockSpec((None, tm, ndim), lambda h, i, k: (h, i, 0)),
                   out_shape=(nh, seq, ndim), out_dtype=out_dtype)

    q_raw = head_mm("mla_q", qn, w["wuq"], F32)
    kv = head_mm("mla_kv", kvn, w["wukv"], BF16)
    hrow = pl.BlockSpec((None, t_row, 256), lambda h, i: (h, i, 0))
    tab = pl.BlockSpec((t_row, LANES), lambda h, i: (i, 0))

    def q_rope_fn(qb, cos, sa, sb):
        return jnp.concatenate([qb[:, :QK_NOPE], _rope128(qb[:, QK_NOPE:], cos, sa, sb)], axis=1)

    q = _blockwise("mla_q_rope", q_rope_fn, [q_raw, cos_t, sa_t, sb_t], [hrow, tab, tab, tab],
                   [((nh, seq, 256), BF16)], [hrow], (nh, nrow))[0]
    y_mla = _attn_fwd(q, kv, kpe)
    mla_w = nh * V_DIM

    def outnorm_fn(ys, ym, ws, wm):
        return jnp.concatenate([_rms(ys, ws), _rms(ym, wm)], axis=1)

    ycat = _blockwise("out_norm", outnorm_fn, [y_ssm, y_mla, w["son"], w["mon"]],
                      [_row_spec(t_row, ssm_w), _row_spec(t_row, mla_w), _full_spec((1, ssm_w)), _full_spec((1, mla_w))],
                      [((seq, d), BF16)], [_row_spec(t_row, d)], g1)[0]
    h1 = _mm2d("out_proj", ycat, w["wout"], NN, F32, res=x)

    hn2 = _blockwise("norm2", lambda hb, wv: _rms(hb, wv), [h1, w["ffn_norm"]],
                     [_row_spec(t_row, d), _full_spec((1, d))], [((seq, d), BF16)], [_row_spec(t_row, d)], g1)[0]
    tku = d
    a_ff = _mm("ffn_up", hn2, w["wup"], grid=(ns, nm, d // tku),
               a_spec=pl.BlockSpec((tm, tku), lambda s, i, k: (i, k)),
               b_spec=pl.BlockSpec((None, tku, c_ff), lambda s, i, k: (s, k, 0)),
               o_spec=pl.BlockSpec((None, tm, c_ff), lambda s, i, k: (s, i, 0)),
               out_shape=(ns, seq, c_ff), out_dtype=F32)
    cb3 = w["conv_b"].reshape(ns, 1, c_ff)
    m_ff = _conv_gate_fwd(a_ff, w["conv_w"], cb3)
    half = ns // 2
    wd4 = w["wdown"]
    tnd = _tile(d, 512)
    tmx, tnx = min(1024, seq), _tile(d, 1024)
    h2 = _mm("ffn_down", m_ff, wd4, grid=(seq // tmx, d // tnx, half),
             a_spec=pl.BlockSpec((None, tmx, c_ff), lambda i, j, s: (s, i, 0)),
             b_spec=pl.BlockSpec((None, c_ff, tnx), lambda i, j, s: (s, 0, j)),
             o_spec=pl.BlockSpec((tmx, tnx), lambda i, j, s: (i, j)),
             out_shape=(seq, d), out_dtype=F32,
             res=h1, res_spec=pl.BlockSpec((tmx, tnx), lambda i, j, s: (i, j)))

    def loss_fn(hb, tb, wv):
        def f(hh, ww):
            err = _rms(hh, ww) - tb
            return 0.5 * jnp.sum(jnp.mean(err * err, axis=-1))

        lossv, (dh, dw) = jax.value_and_grad(f, argnums=(0, 1))(hb, wv)
        return dh, dh, jnp.full((1, LANES), lossv, F32), dw

    fin_w = w["final_norm"].reshape(1, d)
    dh2, dh2b, loss_acc, g_final = _blockwise(
        "loss_head", loss_fn, [h2, target, fin_w], [_row_spec(t_row, d), _row_spec(t_row, d), _full_spec((1, d))],
        [((seq, d), F32), ((seq, d), BF16), ((1, LANES), F32), ((1, d), F32)],
        [_row_spec(t_row, d), _row_spec(t_row, d), _full_spec((1, LANES)), _full_spec((1, d))], g1, n_acc=2)
    loss = loss_acc[0, 0]

    dm = _mm("ffn_down_dx", dh2b, wd4, grid=(half, nm, d // tku), contract=NT,
             a_spec=pl.BlockSpec((tm, tku), lambda s, i, k: (i, k)),
             b_spec=pl.BlockSpec((None, c_ff, tku), lambda s, i, k: (s, 0, k)),
             o_spec=pl.BlockSpec((None, tm, c_ff), lambda s, i, k: (s, i, 0)),
             out_shape=(half, seq, c_ff), out_dtype=F32)
    tks = seq
    g_wdown = _mm("ffn_down_dw", m_ff, dh2b, grid=(half, d // tnd, seq // tks), contract=TN,
                  a_spec=pl.BlockSpec((None, tks, c_ff), lambda s, j, k: (s, k, 0)),
                  b_spec=pl.BlockSpec((tks, tnd), lambda s, j, k: (k, j)),
                  o_spec=pl.BlockSpec((None, c_ff, tnd), lambda s, j, k: (s, 0, j)),
                  out_shape=(half, c_ff, d), out_dtype=BF16)
    emit(wdown=g_wdown)
    da_ff, g_convw, g_convb = _conv_gate_bwd(a_ff, w["conv_w"], cb3, dm)
    g_wup = _mm("ffn_up_dw", hn2, da_ff, grid=(ns, d // tnd, seq // tks), contract=TN,
                a_spec=pl.BlockSpec((tks, tnd), lambda s, j, k: (k, j)),
                b_spec=pl.BlockSpec((None, tks, c_ff), lambda s, j, k: (s, k, 0)),
                o_spec=pl.BlockSpec((None, tnd, c_ff), lambda s, j, k: (s, j, 0)),
                out_shape=(ns, d, c_ff), out_dtype=BF16)
    emit(wup=g_wup, conv_w=g_convw)
    dhn2 = _mm("ffn_up_dx", da_ff, w["wup"], grid=(seq // tmx, d // tnx, ns), contract=NT,
               a_spec=pl.BlockSpec((None, tmx, c_ff), lambda i, j, s: (s, i, 0)),
               b_spec=pl.BlockSpec((None, tnx, c_ff), lambda i, j, s: (s, j, 0)),
               o_spec=pl.BlockSpec((tmx, tnx), lambda i, j, s: (i, j)),
               out_shape=(seq, d), out_dtype=F32)

    def norm_bwd_fn(hb, dres, dn, wv):
        dx_, dw_ = _rms_bwd(hb, wv, dn)
        dtot = dres + dx_
        return dtot, dtot, dw_

    dh1, dh1b, g_ffn_norm = _blockwise(
        "norm2_bwd", norm_bwd_fn, [h1, dh2, dhn2, w["ffn_norm"]],
        [_row_spec(t_row, d)] * 3 + [_full_spec((1, d))],
        [((seq, d), F32), ((seq, d), BF16), ((1, d), F32)],
        [_row_spec(t_row, d), _row_spec(t_row, d), _full_spec((1, d))], g1, n_acc=1)

    dycat = _mm2d("out_proj_dx", dh1b, w["wout"], NT, F32)
    g_wout = _mm2d("out_proj_dw", ycat, dh1b, TN, BF16)

    def outnorm_bwd_fn(ys, ym, dyc, ws, wm):
        dys, dws = _rms_bwd(ys, ws, dyc[:, :ssm_w])
        dym, dwm = _rms_bwd(ym, wm, dyc[:, ssm_w:])
        return dys, dym, dws, dwm

    dy_ssm, dy_mla, g_son, g_mon = _blockwise(
        "out_norm_bwd", outnorm_bwd_fn, [y_ssm, y_mla, dycat, w["son"], w["mon"]],
        [_row_spec(t_row, ssm_w), _row_spec(t_row, mla_w), _row_spec(t_row, d), _full_spec((1, ssm_w)),
         _full_spec((1, mla_w))],
        [((seq, ssm_w), F32), ((seq, mla_w), F32), ((1, ssm_w), F32), ((1, mla_w), F32)],
        [_row_spec(t_row, ssm_w), _row_spec(t_row, mla_w), _full_spec((1, ssm_w)), _full_spec((1, mla_w))],
        g1, n_acc=2)

    def gate_bwd1_fn(dy, yp, zb):
        ygv = jax.nn.gelu(yp)
        sg = jax.nn.sigmoid(zb)
        dz = dy * ygv * sg * (1.0 - sg)
        return dz, jnp.sum(dz, axis=0, keepdims=True)

    dz, g_bglu = _blockwise("ssm_gate_bwd", gate_bwd1_fn, [dy_ssm, y_pre, z], [_row_spec(t_row, ssm_w)] * 3,
                            [((seq, ssm_w), BF16), ((1, ssm_w), F32)],
                            [_row_spec(t_row, ssm_w), _full_spec((1, ssm_w))], g1, n_acc=1)
    dyg2 = _mm2d("ssm_glu_dx", dz, w["wglu"], NT, F32)
    g_wglu = _mm2d("ssm_glu_dw", yg, dz, TN, BF16)

    def gelu_bwd_fn(dy, yp, zb, dg2, ub, dsk):
        dyg = dy * jax.nn.sigmoid(zb) + dg2
        _, vjp = jax.vjp(jax.nn.gelu, yp)
        dyp = vjp(dyg)[0]
        return dyp, dyp * dsk, jnp.sum(dyp * ub, axis=0, keepdims=True)

    dy_pre, du1, g_ssmd = _blockwise(
        "ssm_gelu_bwd", gelu_bwd_fn, [dy_ssm, y_pre, z, dyg2, proj, w["ssm_d"]],
        [_row_spec(t_row, ssm_w)] * 4 + [u_spec, _full_spec((1, ssm_w))],
        [((seq, ssm_w), BF16), ((seq, ssm_w), F32), ((1, ssm_w), F32)],
        [_row_spec(t_row, ssm_w), _row_spec(t_row, ssm_w), _full_spec((1, ssm_w))], g1, n_acc=1)
    ds_all = _mm("ssm_cy_dx", dy_pre, wc, grid=(nm, nj, 1), contract=NT,
                 a_spec=pl.BlockSpec((tm, LANES), lambda i, j, k: (i, j)),
                 b_spec=pl.BlockSpec((None, sw, LANES), lambda i, j, k: (j, 0, 0)),
                 o_spec=pl.BlockSpec((tm, sw), lambda i, j, k: (i, j)),
                 out_shape=(seq, nj * sw), out_dtype=F32)
    dwc = _mm("ssm_cy_dw", s_all, dy_pre, grid=(nj, 1, seq // tks), contract=TN,
              a_spec=pl.BlockSpec((tks, sw), lambda j, n, k: (k, j)),
              b_spec=pl.BlockSpec((tks, LANES), lambda j, n, k: (k, j)),
              o_spec=pl.BlockSpec((None, sw, LANES), lambda j, n, k: (j, 0, 0)),
              out_shape=(nj, sw, LANES), out_dtype=F32)
    lam, da_lay = _s5_scan_bwd(ds_all, s_all, a_lay)
    du = _mm("ssm_bu_dx", lam, wb, grid=(nm, nj, 1), contract=NT,
             a_spec=pl.BlockSpec((tm, sw), lambda i, j, k: (i, j)),
             b_spec=pl.BlockSpec((None, LANES, sw), lambda i, j, k: (j, 0, 0)),
             o_spec=pl.BlockSpec((tm, LANES), lambda i, j, k: (i, j)),
             out_shape=(seq, ssm_w), out_dtype=BF16,
             res=du1, res_spec=pl.BlockSpec((tm, LANES), lambda i, j, k: (i, j)))
    dwb = _mm("ssm_bu_dw", proj, lam, grid=(nj, 1, seq // tks), contract=TN,
              a_spec=pl.BlockSpec((tks, LANES), lambda j, n, k: (k, j)),
              b_spec=pl.BlockSpec((tks, sw), lambda j, n, k: (k, j)),
              o_spec=pl.BlockSpec((None, LANES, sw), lambda j, n, k: (j, 0, 0)),
              out_shape=(nj, LANES, sw), out_dtype=F32)
    g_c_re = blockdiag_out_t(dwc[:, :STATE_BLOCK, :])
    g_c_im = -blockdiag_out_t(dwc[:, STATE_BLOCK:, :])
    dbbt_re = blockdiag_in_t(dwb[:, :, :STATE_BLOCK])
    dbbt_im = blockdiag_in_t(dwb[:, :, STATE_BLOCK:])
    da3 = da_lay.reshape(nj, 2, STATE_BLOCK)
    dabar_re = da3[:, 0, :].reshape(n_groups, 1, SSM_STATE)
    dabar_im = da3[:, 1, :].reshape(n_groups, 1, SSM_STATE)
    g_lr3, g_li3, g_ldt3, g_bt_re, g_bt_im = _s5_prep_bwd(lr3, li3, ldt3, bt_re, bt_im,
                                                           dabar_re, dabar_im, dbbt_re, dbbt_im)

    dq, dkv, dkp_h = _attn_bwd(q, kv, kpe, dy_mla)

    def q_rope_bwd_fn(dqb, cos, sa, sb):
        return jnp.concatenate([dqb[:, :QK_NOPE], _rope128_t(dqb[:, QK_NOPE:], cos, sa, sb)], axis=1)

    dq_raw = _blockwise("mla_q_rope_bwd", q_rope_bwd_fn, [dq, cos_t, sa_t, sb_t], [hrow, tab, tab, tab],
                        [((nh, seq, 256), BF16)], [hrow], (nh, nrow))[0]

    def head_mm_dx(name, dact, wh):
        kdim, ndim = wh.shape[1], wh.shape[2]
        return _mm(name, dact, wh, grid=(nm, 1, nh), contract=NT,
                   a_spec=pl.BlockSpec((None, tm, ndim), lambda i, j, h: (h, i, 0)),
                   b_spec=pl.BlockSpec((None, kdim, ndim), lambda i, j, h: (h, 0, 0)),
                   o_spec=pl.BlockSpec((tm, kdim), lambda i, j, h: (i, 0)),
                   out_shape=(seq, kdim), out_dtype=F32)

    def head_mm_dw(name, act, dact):
        kdim, ndim = act.shape[1], dact.shape[2]
        return _mm(name, act, dact, grid=(nh, 1, seq // tks), contract=TN,
                   a_spec=pl.BlockSpec((tks, kdim), lambda h, j, k: (k, 0)),
                   b_spec=pl.BlockSpec((None, tks, ndim), lambda h, j, k: (h, k, 0)),
                   o_spec=pl.BlockSpec((None, kdim, ndim), lambda h, j, k: (h, 0, 0)),
                   out_shape=(nh, kdim, ndim), out_dtype=BF16)

    dqn = head_mm_dx("mla_q_dx", dq_raw, w["wuq"])
    g_wuq = head_mm_dw("mla_q_dw", qn, dq_raw)
    dkvn = head_mm_dx("mla_kv_dx", dkv, w["wukv"])
    g_wukv = head_mm_dw("mla_kv_dw", kvn, dkv)
    emit(wout=g_wout, wglu=g_wglu, wuq=g_wuq, wukv=g_wukv)

    def mla_prep_bwd_fn(cq, ckv, dqn_b, dkvn_b, dkp_b, cos, sa, sb, wq, wkv):
        dcq, dwq = _rms_bwd(cq, wq, dqn_b)
        dckv, dwkv = _rms_bwd(ckv, wkv, dkvn_b)
        dkp_sum = dkp_b[0]
        for h in range(1, nh):
            dkp_sum = dkp_sum + dkp_b[h]
        return dcq, dckv, _rope128_t(dkp_sum, cos, sa, sb), dwq, dwkv

    dc_q, dc_kv, dkpe_raw, g_qnorm, g_kvnorm = _blockwise(
        "mla_prep_bwd", mla_prep_bwd_fn, [c_q, c_kv, dqn, dkvn, dkp_h, cos_t, sa_t, sb_t, w["q_norm"], w["kv_norm"]],
        [_row_spec(t_row, q_rank), _row_spec(t_row, kv_rank), _row_spec(t_row, q_rank), _row_spec(t_row, kv_rank),
         pl.BlockSpec((nh, t_row, LANES), lambda i: (0, i, 0))] + [_row_spec(t_row, LANES)] * 3
        + [_full_spec((1, q_rank)), _full_spec((1, kv_rank))],
        [((seq, q_rank), BF16), ((seq, kv_rank), BF16), ((seq, LANES), BF16), ((1, q_rank), F32), ((1, kv_rank), F32)],
        [_row_spec(t_row, q_rank), _row_spec(t_row, kv_rank), _row_spec(t_row, LANES), _full_spec((1, q_rank)),
         _full_spec((1, kv_rank))], g1, n_acc=2)

    dproj = jnp.concatenate([du, dc_q, dc_kv, dkpe_raw], axis=1)
    dhn = _mm2d("proj_dx", dproj, w["win"], NT, F32)
    g_win = _mm2d("proj_dw", hn, dproj, TN, BF16, tn=640)

    def norm1_bwd_fn(xb, dres, dn, wv):
        dx_, dw_ = _rms_bwd(xb, wv, dn)
        return dres + dx_, dw_

    grad_x, g_attn_norm = _blockwise(
        "norm1_bwd", norm1_bwd_fn, [x, dh1, dhn, attn_w], [_row_spec(t_row, d)] * 3 + [_full_spec((1, d))],
        [((seq, d), F32), ((1, d), F32)], [_row_spec(t_row, d), _full_spec((1, d))], g1, n_acc=1)

    grads = dict(
        attn_norm=g_attn_norm, win=g_win, lam_re=g_lr3, lam_im=g_li3, log_dt=g_ldt3,
        b_re=jnp.swapaxes(g_bt_re, 1, 2), b_im=jnp.swapaxes(g_bt_im, 1, 2), c_re=g_c_re, c_im=g_c_im,
        ssm_d=g_ssmd, wglu=g_wglu, b_glu=g_bglu, q_norm=g_qnorm, wuq=g_wuq, kv_norm=g_kvnorm, wukv=g_wukv,
        son=g_son, mon=g_mon, wout=g_wout, ffn_norm=g_ffn_norm, wup=g_wup, conv_w=g_convw, conv_b=g_convb,
        wdown=g_wdown, final_norm=g_final)
    return loss, grad_x, grads


def _mesh_pos():
    return lax.axis_index("x"), lax.axis_index("y"), lax.axis_index("c")


def _handshake_all():
    x, y, c = _mesh_pos()
    barrier = pltpu.get_barrier_semaphore()
    for k in range(1, N_DEV):
        peer = (1 - x if k & 4 else x, 1 - y if k & 2 else y, 1 - c if k & 1 else c)
        pl.semaphore_signal(barrier, inc=1, device_id=peer, device_id_type=MESH)
    pl.semaphore_wait(barrier, N_DEV - 1)


def _comm_call(name, body, n, out_shape, ins, collective_id):
    sems = [pltpu.SemaphoreType.DMA((7 * n,)), pltpu.SemaphoreType.DMA((7 * n,)), pltpu.SemaphoreType.DMA((n,))]
    if collective_id is None:
        any_spec = pl.BlockSpec(memory_space=pl.ANY)
        return pl.pallas_call(body, name=name, out_shape=out_shape, in_specs=[any_spec] * n,
                              out_specs=[any_spec] * n, scratch_shapes=sems)(*ins)
    return pl.kernel(body, name=name, out_type=out_shape,
                     mesh=plsc.ScalarSubcoreMesh(axis_name="seq", num_cores=1), scratch_types=sems,
                     compiler_params=pltpu.CompilerParams(collective_id=collective_id))(*ins)


def _all_gather(name, xs, collective_id=None):
    n = len(xs)

    def body(*refs):
        x_refs, o_refs = refs[:n], refs[n:2 * n]
        send_sems, recv_sems, local_sems = refs[2 * n:]
        if collective_id is not None:
            _handshake_all()
        x, y, c = _mesh_pos()
        me, sibling = (x, y, c), (x, y, 1 - c)
        chips = [(1 - x, y), (x, 1 - y), (1 - x, 1 - y)]

        def slot(o_ref, px, py, pc):
            return o_ref.at[4 * px + 2 * py + pc]

        def copy(t, k, block, to, src=None):
            dst = slot(o_refs[t], *block)
            return pltpu.make_async_remote_copy(
                src_ref=dst if src is None else src, dst_ref=dst,
                send_sem=send_sems.at[7 * t + k], recv_sem=recv_sems.at[7 * t + k],
                device_id=to, device_id_type=MESH)

        started = []
        for t in range(n):
            mine = pltpu.make_async_copy(x_refs[t], slot(o_refs[t], *me), local_sems.at[t])
            mine.start()
            started.append(mine)
        first = []
        for t in range(n):
            first.append(copy(t, 0, me, sibling, src=x_refs[t]))
            first += [copy(t, 1 + j, me, (*chip, c), src=x_refs[t]) for j, chip in enumerate(chips)]
        for cp in first:
            cp.start()
        passed = []
        for j, chip in enumerate(chips):
            for t in range(n):
                copy(t, 1 + j, (*chip, c), me).wait_recv()
                fwd = copy(t, 4 + j, (*chip, c), sibling)
                fwd.start()
                passed.append(fwd)
        for t in range(n):
            copy(t, 0, sibling, me).wait_recv()
            for j, chip in enumerate(chips):
                copy(t, 4 + j, (*chip, 1 - c), me).wait_recv()
        for cp in first + passed:
            cp.wait_send()
        for mine in started:
            mine.wait()

    out_shape = [jax.ShapeDtypeStruct((N_DEV,) + v.shape, v.dtype) for v in xs]
    return _comm_call(name, body, n, out_shape, xs, collective_id)


def _exchange_partials(name, gs, collective_id=None):
    n = len(gs)

    def body(*refs):
        g_refs, o_refs = refs[:n], refs[n:2 * n]
        send_sems, recv_sems, local_sems = refs[2 * n:]
        if collective_id is not None:
            _handshake_all()
        x, y, c = _mesh_pos()
        me_idx = 4 * x + 2 * y + c
        copies = []
        for t in range(n):
            mine = pltpu.make_async_copy(g_refs[t].at[me_idx], o_refs[t].at[me_idx], local_sems.at[t])
            mine.start()
            copies.append(mine)
        remote = []
        for k in range(1, N_DEV):
            px = 1 - x if k & 4 else x
            py = 1 - y if k & 2 else y
            pc = 1 - c if k & 1 else c
            p_idx = 4 * px + 2 * py + pc
            for t in range(n):
                cp = pltpu.make_async_remote_copy(
                    src_ref=g_refs[t].at[p_idx], dst_ref=o_refs[t].at[me_idx],
                    send_sem=send_sems.at[7 * t + k - 1], recv_sem=recv_sems.at[7 * t + k - 1],
                    device_id=(px, py, pc), device_id_type=MESH)
                cp.start()
                landing = pltpu.make_async_remote_copy(
                    src_ref=g_refs[t].at[p_idx], dst_ref=o_refs[t].at[p_idx],
                    send_sem=send_sems.at[7 * t + k - 1], recv_sem=recv_sems.at[7 * t + k - 1],
                    device_id=(px, py, pc), device_id_type=MESH)
                remote.append((cp, landing))
        for cp, landing in remote:
            landing.wait_recv()
        for cp, landing in remote:
            cp.wait_send()
        for mine in copies:
            mine.wait()

    out_shape = [jax.ShapeDtypeStruct(v.shape, v.dtype) for v in gs]
    return _comm_call(name, body, n, out_shape, gs, collective_id)


ADAM_BLOCK_ELEMS = 128 * 1024


def _adamw_sum(name, parts, wv, mv, vv):
    npart, r, c = parts.shape
    tr = r
    if r * c > ADAM_BLOCK_ELEMS and r % SUBLANES == 0:
        tr = SUBLANES
        while r % (tr * 2) == 0 and tr * 2 * c <= ADAM_BLOCK_ELEMS:
            tr *= 2
    bc1 = 1.0 - ADAM_B1 ** ADAM_STEP
    bc2 = 1.0 - ADAM_B2 ** ADAM_STEP

    def fn(pb, wb_, mb, vb):
        g = pb[0].astype(F32)
        for j in range(1, npart):
            g = g + pb[j].astype(F32)
        m_new = ADAM_B1 * mb + (1.0 - ADAM_B1) * g
        v_new = ADAM_B2 * vb + (1.0 - ADAM_B2) * (g * g)
        m_hat = m_new / bc1
        v_hat = v_new / bc2
        delta = -ADAM_LR * (m_hat / (jnp.sqrt(v_hat) + ADAM_EPS) + ADAM_WD * wb_)
        return g, delta, m_new, v_new

    row = pl.BlockSpec((tr, c), lambda i: (i, 0))
    return _blockwise(name, fn, [parts, wv, mv, vv],
                      [pl.BlockSpec((npart, tr, c), lambda i: (0, i, 0)), row, row, row],
                      [((r, c), F32)] * 4, [row] * 4, (r // tr,))


_SMALL = ["attn_norm", "lam_re", "lam_im", "log_dt", "b_re", "b_im", "c_re", "c_im", "ssm_d", "b_glu",
          "q_norm", "kv_norm", "son", "mon", "ffn_norm", "conv_b", "final_norm"]
_BIG = ["win", "wglu", "wuq", "wukv", "wout", "wup", "wdown", "conv_w"]
_ORDER = ["attn_norm", "win", "lam_re", "lam_im", "log_dt", "b_re", "b_im", "c_re", "c_im", "ssm_d", "wglu",
          "b_glu", "q_norm", "wuq", "kv_norm", "wukv", "son", "mon", "wout", "ffn_norm", "wup", "conv_w",
          "conv_b", "wdown", "final_norm"]


def _pack(arrs):
    flat = jnp.concatenate([a.reshape(-1).astype(F32) for a in arrs])
    pad = (-flat.shape[0]) % (LANES * LANES)
    return jnp.pad(flat, (0, pad)).reshape(-1, LANES)


def _unpack(packed, shapes):
    flat = packed.reshape(-1)
    out, off = [], 0
    for s in shapes:
        n = math.prod(s)
        out.append(flat[off:off + n].reshape(s))
        off += n
    return out


def kernel(x, positions, attn_norm_w, w_in, ssm_lambda_re, ssm_lambda_im, ssm_log_dt, ssm_b_re, ssm_b_im, ssm_c_re, ssm_c_im, ssm_d, ssm_w_glu, ssm_b_glu, mla_q_norm_w, mla_w_uq, mla_kv_norm_w, mla_w_ukv, ssm_out_norm_w, mla_out_norm_w, w_out, ffn_norm_w, ffn_w_up, ffn_conv_w, ffn_conv_b, ffn_w_down, final_norm_w, loss_target, m_attn_norm_w, m_w_in, m_ssm_lambda_re, m_ssm_lambda_im, m_ssm_log_dt, m_ssm_b_re, m_ssm_b_im, m_ssm_c_re, m_ssm_c_im, m_ssm_d, m_ssm_w_glu, m_ssm_b_glu, m_mla_q_norm_w, m_mla_w_uq, m_mla_kv_norm_w, m_mla_w_ukv, m_ssm_out_norm_w, m_mla_out_norm_w, m_w_out, m_ffn_norm_w, m_ffn_w_up, m_ffn_conv_w, m_ffn_conv_b, m_ffn_w_down, m_final_norm_w, v_attn_norm_w, v_w_in, v_ssm_lambda_re, v_ssm_lambda_im, v_ssm_log_dt, v_ssm_b_re, v_ssm_b_im, v_ssm_c_re, v_ssm_c_im, v_ssm_d, v_ssm_w_glu, v_ssm_b_glu, v_mla_q_norm_w, v_mla_w_uq, v_mla_kv_norm_w, v_mla_w_ukv, v_ssm_out_norm_w, v_mla_out_norm_w, v_w_out, v_ffn_norm_w, v_ffn_w_up, v_ffn_conv_w, v_ffn_conv_b, v_ffn_w_down, v_final_norm_w):
    wts = dict(attn_norm=attn_norm_w, win=w_in, lam_re=ssm_lambda_re, lam_im=ssm_lambda_im, log_dt=ssm_log_dt,
               b_re=ssm_b_re, b_im=ssm_b_im, c_re=ssm_c_re, c_im=ssm_c_im, ssm_d=ssm_d, wglu=ssm_w_glu,
               b_glu=ssm_b_glu, q_norm=mla_q_norm_w, wuq=mla_w_uq, kv_norm=mla_kv_norm_w, wukv=mla_w_ukv,
               son=ssm_out_norm_w, mon=mla_out_norm_w, wout=w_out, ffn_norm=ffn_norm_w, wup=ffn_w_up,
               conv_w=ffn_conv_w, conv_b=ffn_conv_b, wdown=ffn_w_down, final_norm=final_norm_w)
    moms = dict(zip(_ORDER, [m_attn_norm_w, m_w_in, m_ssm_lambda_re, m_ssm_lambda_im, m_ssm_log_dt, m_ssm_b_re,
                             m_ssm_b_im, m_ssm_c_re, m_ssm_c_im, m_ssm_d, m_ssm_w_glu, m_ssm_b_glu, m_mla_q_norm_w,
                             m_mla_w_uq, m_mla_kv_norm_w, m_mla_w_ukv, m_ssm_out_norm_w, m_mla_out_norm_w, m_w_out,
                             m_ffn_norm_w, m_ffn_w_up, m_ffn_conv_w, m_ffn_conv_b, m_ffn_w_down, m_final_norm_w]))
    vels = dict(zip(_ORDER, [v_attn_norm_w, v_w_in, v_ssm_lambda_re, v_ssm_lambda_im, v_ssm_log_dt, v_ssm_b_re,
                             v_ssm_b_im, v_ssm_c_re, v_ssm_c_im, v_ssm_d, v_ssm_w_glu, v_ssm_b_glu, v_mla_q_norm_w,
                             v_mla_w_uq, v_mla_kv_norm_w, v_mla_w_ukv, v_ssm_out_norm_w, v_mla_out_norm_w, v_w_out,
                             v_ffn_norm_w, v_ffn_w_up, v_ffn_conv_w, v_ffn_conv_b, v_ffn_w_down, v_final_norm_w]))
    seq, d = x.shape[1], x.shape[2]
    in_width = w_in.shape[2]
    in_pad = -(-in_width // LANES) * LANES
    q_cols = mla_w_uq.shape[2]
    q_pad = 2 * LANES

    (win_g,) = _all_gather("gather_w_in", [jnp.pad(w_in[0], ((0, 0), (0, in_pad - in_width))).astype(BF16)])
    wglu_g, wuq_g, wukv_g, wout_g, convw_g = _all_gather(
        "gather_mix", [ssm_w_glu[0].astype(BF16), jnp.pad(mla_w_uq[0], ((0, 0), (0, q_pad - q_cols))).astype(BF16),
                       mla_w_ukv[0].astype(BF16), w_out[0].astype(BF16), ffn_conv_w[0]], collective_id=0)
    (wup_g,) = _all_gather("gather_ffn_up", [ffn_w_up[0].astype(BF16)], collective_id=1)
    (wdown_g,) = _all_gather("gather_ffn_down", [ffn_w_down[0].astype(BF16)], collective_id=2)
    ns = N_DEV
    c_ff = wup_g.shape[2]
    w = dict(
        attn_norm=attn_norm_w, win=win_g.reshape(d, in_pad), lam_re=ssm_lambda_re, lam_im=ssm_lambda_im,
        log_dt=ssm_log_dt, b_re=ssm_b_re, b_im=ssm_b_im, c_re=ssm_c_re, c_im=ssm_c_im, ssm_d=ssm_d,
        wglu=wglu_g.reshape(d // 2, d // 2), b_glu=ssm_b_glu, q_norm=mla_q_norm_w, wuq=wuq_g,
        kv_norm=mla_kv_norm_w, wukv=wukv_g, son=ssm_out_norm_w, mon=mla_out_norm_w, wout=wout_g.reshape(d, d),
        ffn_norm=ffn_norm_w, wup=wup_g, conv_w=convw_g, conv_b=ffn_conv_b,
        wdown=wdown_g.reshape(ns // 2, c_ff, d), final_norm=final_norm_w)

    shard_layout = dict(
        win=lambda a: a[:, :in_width].reshape(N_DEV, d // N_DEV, in_width),
        wglu=lambda a: a.reshape(N_DEV, d // 2 // N_DEV, d // 2),
        wuq=lambda a: a[:, :, :q_cols], wukv=lambda a: a, wout=lambda a: a.reshape(N_DEV, d // N_DEV, d),
        wup=lambda a: a, wdown=lambda a: a.reshape(N_DEV, c_ff // 2, d), conv_w=lambda a: a)
    recv = {}
    next_id = [3]

    def exchange(sequencer=True, **grads):
        collective_id = None
        if sequencer:
            collective_id = next_id[0]
            next_id[0] += 1
        names = list(grads)
        got = _exchange_partials("exchange_" + "_".join(names), [shard_layout[k](grads[k]) for k in names],
                                 collective_id=collective_id)
        recv.update(zip(names, got))

    loss_part, grad_x, g = _local_step(x[0], positions[0], loss_target[0], w, emit=exchange)
    loss = lax.psum(loss_part, ("x", "y", "c"))
    exchange(win=g["win"])
    small_shapes = [wts[k].shape for k in _SMALL]
    small_part = _pack([g[k] for k in _SMALL])
    small_all = _all_gather("gather_small_grads", [small_part])[0]

    out = {}
    for k in _BIG:
        shp = wts[k].shape
        r, c = shp[-2], shp[-1]
        res = _adamw_sum("adamw_" + k, recv[k].reshape(N_DEV, r, c), wts[k].reshape(r, c),
                         moms[k].reshape(r, c), vels[k].reshape(r, c))
        out[k] = [a.reshape(shp) for a in res]
    sw_ = _pack([wts[k] for k in _SMALL])
    sm_ = _pack([moms[k] for k in _SMALL])
    sv_ = _pack([vels[k] for k in _SMALL])
    res = _adamw_sum("adamw_small", small_all, sw_, sm_, sv_)
    unpacked = [_unpack(a, small_shapes) for a in res]
    for i, k in enumerate(_SMALL):
        out[k] = [u[i] for u in unpacked]

    grad_x = grad_x.reshape(x.shape)
    return (loss, grad_x, *[out[k][0] for k in _ORDER], *[out[k][1] for k in _ORDER],
            *[out[k][2] for k in _ORDER], *[out[k][3] for k in _ORDER])
```

```python
import functools
import math

import jax
import jax.numpy as jnp
from jax import lax
from jax.experimental import pallas as pl
from jax.experimental.pallas import tpu as pltpu
from jax.experimental.pallas import tpu_sc as plsc

F32 = jnp.float32
BF16 = jnp.bfloat16
MESH = pl.DeviceIdType.MESH

N_DEV = 8
LANES = 128
SUBLANES = 8
VMEM_LIMIT = 48 * 1024 * 1024

SSM_GROUP = 16
SSM_STATE = 64
GROUPS_PER_BLOCK = LANES // SSM_GROUP
STATE_BLOCK = GROUPS_PER_BLOCK * SSM_STATE
QK_NOPE = 128
QK_ROPE = 64
V_DIM = 128
ROPE_THETA = 10000.0
RMS_EPS = 1e-6

ADAM_LR = 0.001
ADAM_B1 = 0.9
ADAM_B2 = 0.999
ADAM_EPS = 1e-08
ADAM_WD = 0.01
ADAM_STEP = 10

NN = ((1,), (0,))
NT = ((1,), (1,))
TN = ((0,), (0,))


def _cparams():
    return pltpu.CompilerParams(vmem_limit_bytes=VMEM_LIMIT)


def _tile(n, want):
    if n <= want:
        return n
    t = (want // LANES) * LANES
    while t >= LANES:
        if n % t == 0:
            return t
        t -= LANES
    return n


def _mm(name, a, b, *, grid, a_spec, b_spec, o_spec, out_shape, out_dtype, contract=NN,
        res=None, res_spec=None):
    nk = grid[-1]
    kaxis = len(grid) - 1
    acc_shape = tuple(d for d in o_spec.block_shape if d is not None)

    def body(*refs):
        a_ref, b_ref = refs[:2]
        r_ref = None if res is None else refs[2]
        o_ref = refs[2 if res is None else 3]
        part = lax.dot_general(a_ref[...].astype(BF16), b_ref[...].astype(BF16),
                               (contract, ((), ())), preferred_element_type=F32)
        if nk == 1:
            if r_ref is not None:
                part = part + r_ref[...].astype(F32)
            o_ref[...] = part.astype(o_ref.dtype)
            return
        acc = refs[-1]
        k = pl.program_id(kaxis)

        @pl.when(k == 0)
        def _():
            acc[...] = part

        @pl.when(k != 0)
        def _():
            acc[...] += part

        @pl.when(k == nk - 1)
        def _():
            r = acc[...]
            if r_ref is not None:
                r = r + r_ref[...].astype(F32)
            o_ref[...] = r.astype(o_ref.dtype)

    ins = [a, b] + ([] if res is None else [res])
    in_specs = [a_spec, b_spec] + ([] if res is None else [res_spec])
    return pl.pallas_call(
        body, name=name, grid=grid, in_specs=in_specs, out_specs=o_spec,
        out_shape=jax.ShapeDtypeStruct(out_shape, out_dtype),
        scratch_shapes=[pltpu.VMEM(acc_shape, F32)] if nk > 1 else [], compiler_params=_cparams(),
    )(*ins)


def _mm2d(name, a, b, contract, out_dtype, tm=512, tn=512, tk=2048, res=None):
    if contract == NN:
        (m, kk), n = a.shape, b.shape[1]
    elif contract == NT:
        (m, kk), n = a.shape, b.shape[0]
    else:
        (kk, m), n = a.shape, b.shape[1]
    tm, tn, tk = _tile(m, tm), _tile(n, tn), _tile(kk, tk)
    grid = (m // tm, n // tn, kk // tk)
    if contract == TN:
        a_spec = pl.BlockSpec((tk, tm), lambda i, j, k: (k, i))
    else:
        a_spec = pl.BlockSpec((tm, tk), lambda i, j, k: (i, k))
    if contract == NT:
        b_spec = pl.BlockSpec((tn, tk), lambda i, j, k: (j, k))
    else:
        b_spec = pl.BlockSpec((tk, tn), lambda i, j, k: (k, j))
    o_spec = pl.BlockSpec((tm, tn), lambda i, j, k: (i, j))
    res_spec = None
    if res is not None:
        if res.shape[0] == 1:
            res_spec = pl.BlockSpec((1, tn), lambda i, j, k: (0, j))
        else:
            res_spec = pl.BlockSpec((tm, tn), lambda i, j, k: (i, j))
    return _mm(name, a, b, grid=grid, a_spec=a_spec, b_spec=b_spec, o_spec=o_spec,
               out_shape=(m, n), out_dtype=out_dtype, contract=contract, res=res, res_spec=res_spec)


def _blockwise(name, fn, ins, in_specs, outs, out_specs, grid, n_acc=0, acc_all=True):
    n_in, n_out = len(ins), len(outs)
    n_plain = n_out - n_acc

    def body(*refs):
        vals = fn(*[r[...] for r in refs[:n_in]])
        if not isinstance(vals, (tuple, list)):
            vals = (vals,)
        o_refs = refs[n_in:n_in + n_out]
        for r, v in zip(o_refs[:n_plain], vals[:n_plain]):
            r[...] = v.astype(r.dtype)
        if n_acc:
            if acc_all:
                first = functools.reduce(jnp.logical_and, [pl.program_id(d) == 0 for d in range(len(grid))])
            else:
                first = pl.program_id(len(grid) - 1) == 0

            @pl.when(first)
            def _():
                for r, v in zip(o_refs[n_plain:], vals[n_plain:]):
                    r[...] = v.astype(r.dtype)

            @pl.when(jnp.logical_not(first))
            def _():
                for r, v in zip(o_refs[n_plain:], vals[n_plain:]):
                    r[...] += v.astype(r.dtype)

    return pl.pallas_call(
        body, name=name, grid=grid, in_specs=in_specs, out_specs=out_specs,
        out_shape=[jax.ShapeDtypeStruct(s, d) for s, d in outs], compiler_params=_cparams(),
    )(*ins)


def _row_spec(t, c):
    return pl.BlockSpec((t, c), lambda i: (i, 0))


def _full_spec(shape):
    nd = len(shape)
    return pl.BlockSpec(tuple(shape), lambda *g: (0,) * nd)


def _rms(xf, w):
    return xf * lax.rsqrt(jnp.mean(xf * xf, axis=-1, keepdims=True) + RMS_EPS) * w


def _rms_bwd(xf, w, dy):
    _, vjp = jax.vjp(_rms, xf, w)
    return vjp(dy)


def _s5_disc(lr, li, ldt, bre, bim):
    dt = jnp.exp(ldt)
    mag = jnp.exp(lr * dt)
    ar = mag * jnp.cos(li * dt)
    ai = mag * jnp.sin(li * dt)
    nr, ni = ar - 1.0, ai
    den = lr * lr + li * li
    zr = (nr * lr + ni * li) / den
    zi = (ni * lr - nr * li) / den
    return ar, ai, zr * bre - zi * bim, zr * bim + zi * bre


def _s5_prep(lr, li, ldt, bre, bim):
    def body(lr_r, li_r, ldt_r, bre_r, bim_r, ar_r, ai_r, br_r, bi_r):
        ar, ai, br, bi = _s5_disc(lr_r[...], li_r[...], ldt_r[...], bre_r[...], bim_r[...])
        ar_r[...] = ar
        ai_r[...] = ai
        br_r[...] = br
        bi_r[...] = bi

    sd = jax.ShapeDtypeStruct
    return pl.pallas_call(
        body, name="s5_prep",
        out_shape=[sd(lr.shape, F32), sd(lr.shape, F32), sd(bre.shape, F32), sd(bre.shape, F32)],
        compiler_params=_cparams(),
    )(lr, li, ldt, bre, bim)


def _s5_prep_bwd(lr, li, ldt, bre, bim, dar, dai, dbr, dbi):
    def body(lr_r, li_r, ldt_r, bre_r, bim_r, dar_r, dai_r, dbr_r, dbi_r, o0, o1, o2, o3, o4):
        _, vjp = jax.vjp(_s5_disc, lr_r[...], li_r[...], ldt_r[...], bre_r[...], bim_r[...])
        g = vjp((dar_r[...], dai_r[...], dbr_r[...], dbi_r[...]))
        for o, v in zip((o0, o1, o2, o3, o4), g):
            o[...] = v

    sd = jax.ShapeDtypeStruct
    return pl.pallas_call(
        body, name="s5_prep_bwd",
        out_shape=[sd(lr.shape, F32), sd(li.shape, F32), sd(ldt.shape, F32), sd(bre.shape, F32), sd(bim.shape, F32)],
        compiler_params=_cparams(),
    )(lr, li, ldt, bre, bim, dar, dai, dbr, dbi)


SCAN_T = 256


def _s5_scan(x, a):
    seq, width = x.shape
    w2 = 2 * STATE_BLOCK
    nj = width // w2
    t_blk = min(SCAN_T, seq)
    hb = STATE_BLOCK

    def body(x_ref, a_ref, o_ref, st):
        @pl.when(pl.program_id(1) == 0)
        def _():
            st[...] = jnp.zeros_like(st)

        ar = a_ref[:, :hb]
        ai = a_ref[:, hb:]

        def step(t, carry):
            sr, si = carry
            xr = x_ref[pl.ds(t, 1), :hb]
            xi = x_ref[pl.ds(t, 1), hb:]
            nr = ar * sr - ai * si + xr
            ni = ar * si + ai * sr + xi
            o_ref[pl.ds(t, 1), :hb] = nr
            o_ref[pl.ds(t, 1), hb:] = ni
            return nr, ni

        sr, si = lax.fori_loop(0, t_blk, step, (st[0:1, :], st[1:2, :]))
        st[0:1, :] = sr
        st[1:2, :] = si

    return pl.pallas_call(
        body, name="s5_scan", grid=(nj, seq // t_blk),
        in_specs=[pl.BlockSpec((t_blk, w2), lambda j, i: (i, j)), pl.BlockSpec((1, w2), lambda j, i: (0, j))],
        out_specs=pl.BlockSpec((t_blk, w2), lambda j, i: (i, j)),
        out_shape=jax.ShapeDtypeStruct(x.shape, F32),
        scratch_shapes=[pltpu.VMEM((SUBLANES, hb), F32)], compiler_params=_cparams(),
    )(x, a)


def _s5_scan_bwd(ds, s, a):
    seq, width = ds.shape
    w2 = 2 * STATE_BLOCK
    nj = width // w2
    t_blk = min(SCAN_T, seq)
    nb = seq // t_blk
    hb = STATE_BLOCK
    per8 = t_blk // SUBLANES

    def body(d_ref, s_ref, sprev_ref, a_ref, o_ref, da_ref, st):
        ib = pl.program_id(1)

        @pl.when(ib == 0)
        def _():
            st[...] = jnp.zeros_like(st)

        ar = a_ref[:, :hb]
        ai = a_ref[:, hb:]

        def step(tt, carry):
            lr, li = carry
            t = t_blk - 1 - tt
            dr = d_ref[pl.ds(t, 1), :hb]
            di = d_ref[pl.ds(t, 1), hb:]
            nr = ar * lr + ai * li + dr
            ni = ar * li - ai * lr + di
            o_ref[pl.ds(t, 1), :hb] = nr
            o_ref[pl.ds(t, 1), hb:] = ni
            return nr, ni

        lr, li = lax.fori_loop(0, t_blk, step, (st[0:1, :], st[1:2, :]))
        st[0:1, :] = lr
        st[1:2, :] = li

        lam = o_ref[...]
        sv = s_ref[...]
        rows = lax.broadcasted_iota(jnp.int32, sv.shape, 0)
        prev_last = sprev_ref[SUBLANES - 1:SUBLANES, :]
        prev_last = jnp.where(ib == nb - 1, jnp.zeros_like(prev_last), prev_last)
        s_sh = jnp.where(rows >= 1, pltpu.roll(sv, 1, 0), prev_last)
        lam_r, lam_i = lam[:, :hb], lam[:, hb:]
        sr_, si_ = s_sh[:, :hb], s_sh[:, hb:]
        dar = jnp.sum(lam_r * sr_ + lam_i * si_, axis=0, keepdims=True)
        dai = jnp.sum(lam_i * sr_ - lam_r * si_, axis=0, keepdims=True)
        contrib = jnp.concatenate([dar, dai], axis=1)

        @pl.when(ib == 0)
        def _():
            da_ref[...] = contrib

        @pl.when(ib != 0)
        def _():
            da_ref[...] += contrib

    blk = lambda j, i: (nb - 1 - i, j)
    prev_blk = lambda j, i: (jnp.maximum((nb - 1 - i) * per8 - 1, 0), j)
    return pl.pallas_call(
        body, name="s5_scan_bwd", grid=(nj, nb),
        in_specs=[pl.BlockSpec((t_blk, w2), blk), pl.BlockSpec((t_blk, w2), blk),
                  pl.BlockSpec((SUBLANES, w2), prev_blk), pl.BlockSpec((1, w2), lambda j, i: (0, j))],
        out_specs=[pl.BlockSpec((t_blk, w2), blk), pl.BlockSpec((1, w2), lambda j, i: (0, j))],
        out_shape=[jax.ShapeDtypeStruct(ds.shape, F32), jax.ShapeDtypeStruct((1, width), F32)],
        scratch_shapes=[pltpu.VMEM((SUBLANES, hb), F32)], compiler_params=_cparams(),
    )(ds, s, s, a)


def _rope128(x, cos, sa, sb):
    return x * cos + pltpu.roll(x, 96, 1) * sa + pltpu.roll(x, 32, 1) * sb


def _rope128_t(dy, cos, sa, sb):
    return dy * cos + pltpu.roll(dy * sa, 32, 1) + pltpu.roll(dy * sb, 96, 1)


ATT_BQ = 256


def _probs(qn, qp, kn, kp, r0, scale):
    s = lax.dot_general(qn, kn, (NT, ((), ())), preferred_element_type=F32)
    s = s + lax.dot_general(qp, kp, (NT, ((), ())), preferred_element_type=F32)
    s = s * scale
    row = r0 + lax.broadcasted_iota(jnp.int32, s.shape, 0)
    col = lax.broadcasted_iota(jnp.int32, s.shape, 1)
    s = jnp.where(col <= row, s, jnp.finfo(F32).min)
    m = jnp.max(s, axis=-1, keepdims=True)
    e = jnp.exp(s - m)
    return e / jnp.sum(e, axis=-1, keepdims=True)


def _attn_specs(seq):
    tab = pl.BlockSpec((seq, LANES), lambda h: (0, 0))
    return [pl.BlockSpec((None, seq, 256), lambda h: (h, 0, 0)), pl.BlockSpec((None, seq, 128), lambda h: (h, 0, 0)),
            pl.BlockSpec((None, seq, 128), lambda h: (h, 0, 1)), tab, tab, tab, tab]


def _attn_fwd(q_raw, kv, kpe, cos, sa, sb):
    nh, seq, _ = q_raw.shape
    bq = min(ATT_BQ, seq)
    scale = (QK_NOPE + QK_ROPE) ** -0.5

    def body(q_ref, kn_ref, v_ref, kp_ref, cos_ref, sa_ref, sb_ref, o_ref):
        for r0 in range(0, seq, bq):
            rows, kend = pl.ds(r0, bq), r0 + bq
            qn = q_ref[rows, :QK_NOPE].astype(BF16)
            qp = _rope128(q_ref[rows, QK_NOPE:], cos_ref[rows, :], sa_ref[rows, :], sb_ref[rows, :]).astype(BF16)
            p = _probs(qn, qp, kn_ref[:kend, :], kp_ref[:kend, :], r0, scale)
            o_ref[rows, :] = jnp.dot(p.astype(BF16), v_ref[:kend, :], preferred_element_type=F32)

    return pl.pallas_call(
        body, name="attn_fwd", grid=(nh,), in_specs=_attn_specs(seq),
        out_specs=pl.BlockSpec((seq, V_DIM), lambda h: (0, h)),
        out_shape=jax.ShapeDtypeStruct((seq, nh * V_DIM), F32), compiler_params=_cparams(),
    )(q_raw, kv, kv, kpe, cos, sa, sb)


def _attn_bwd(q_raw, kv, kpe, cos, sa, sb, do):
    nh, seq, _ = q_raw.shape
    bq = min(ATT_BQ, seq)
    scale = (QK_NOPE + QK_ROPE) ** -0.5

    def body(q_ref, kn_ref, v_ref, kp_ref, cos_ref, sa_ref, sb_ref, do_ref, dq_ref, dkv_ref, dkp_ref):
        dkv_ref[...] = jnp.zeros_like(dkv_ref)
        dkp_ref[...] = jnp.zeros_like(dkp_ref)
        for r0 in range(0, seq, bq):
            rows, kend = pl.ds(r0, bq), r0 + bq
            cos_b, sa_b, sb_b = cos_ref[rows, :], sa_ref[rows, :], sb_ref[rows, :]
            qn = q_ref[rows, :QK_NOPE].astype(BF16)
            qp = _rope128(q_ref[rows, QK_NOPE:], cos_b, sa_b, sb_b).astype(BF16)
            kn, v, kp = kn_ref[:kend, :], v_ref[:kend, :], kp_ref[:kend, :]
            p = _probs(qn, qp, kn, kp, r0, scale)
            dob = do_ref[rows, :].astype(BF16)
            dp = lax.dot_general(dob, v, (NT, ((), ())), preferred_element_type=F32)
            ds = p * (dp - jnp.sum(p * dp, axis=-1, keepdims=True)) * scale
            dsb = ds.astype(BF16)
            pb = p.astype(BF16)
            dq_ref[rows, :QK_NOPE] = jnp.dot(dsb, kn, preferred_element_type=F32).astype(dq_ref.dtype)
            dqp = jnp.dot(dsb, kp, preferred_element_type=F32)
            dq_ref[rows, QK_NOPE:] = _rope128_t(dqp, cos_b, sa_b, sb_b).astype(dq_ref.dtype)
            dkv_ref[:kend, :QK_NOPE] += lax.dot_general(dsb, qn, (TN, ((), ())), preferred_element_type=F32)
            dkv_ref[:kend, QK_NOPE:] += lax.dot_general(pb, dob, (TN, ((), ())), preferred_element_type=F32)
            dkp_ref[:kend, :] += lax.dot_general(dsb, qp, (TN, ((), ())), preferred_element_type=F32)

    sd = jax.ShapeDtypeStruct
    return pl.pallas_call(
        body, name="attn_bwd", grid=(nh,),
        in_specs=_attn_specs(seq) + [pl.BlockSpec((seq, V_DIM), lambda h: (0, h))],
        out_specs=[pl.BlockSpec((None, seq, 256), lambda h: (h, 0, 0)),
                   pl.BlockSpec((None, seq, 256), lambda h: (h, 0, 0)),
                   pl.BlockSpec((None, seq, 128), lambda h: (h, 0, 0))],
        out_shape=[sd((nh, seq, 256), BF16), sd((nh, seq, 256), F32), sd((nh, seq, 128), F32)],
        compiler_params=_cparams(),
    )(q_raw, kv, kv, kpe, cos, sa, sb, do)


def _conv3(a, w, b):
    rows = lax.broadcasted_iota(jnp.int32, a.shape, 0)
    a1 = jnp.where(rows >= 1, pltpu.roll(a, 1, 0), 0.0)
    a2 = jnp.where(rows >= 2, pltpu.roll(a, 2, 0), 0.0)
    return w[2:3] * a + w[1:2] * a1 + w[0:1] * a2 + b, a1, a2


def _conv_gate_fwd(a, cw, cb):
    half, _, seq, c = a.shape
    nc = c // LANES

    def fn(pair, wg, wv, bg, bv):
        gc, _, _ = _conv3(pair[0], wg, bg)
        vc, _, _ = _conv3(pair[1], wv, bv)
        return gc * jax.nn.sigmoid(gc) * vc

    def w_spec(off, r):
        return pl.BlockSpec((None, r, LANES), lambda k, j: (k + off, 0, j))

    return _blockwise(
        "conv_gate_fwd", fn, [a, cw, cw, cb, cb],
        [pl.BlockSpec((None, 2, seq, LANES), lambda k, j: (k, 0, 0, j)),
         w_spec(0, 3), w_spec(half, 3), w_spec(0, 1), w_spec(half, 1)],
        [((seq, half * c), BF16)], [pl.BlockSpec((seq, LANES), lambda k, j: (0, k * nc + j))],
        grid=(half, nc))[0]


def _conv_gate_bwd(a, cw, cb, dm):
    half, _, seq, c = a.shape
    nc = c // LANES

    def body(a_ref, wg_ref, wv_ref, bg_ref, bv_ref, dm_ref, da_ref, dw_ref, db_ref):
        dmv = dm_ref[...]
        rows = lax.broadcasted_iota(jnp.int32, dmv.shape, 0)
        ga, wg = a_ref[0], wg_ref[...]
        va, wv = a_ref[1], wv_ref[...]
        gc, g1, g2 = _conv3(ga, wg, bg_ref[...])
        vc, v1, v2 = _conv3(va, wv, bv_ref[...])
        sg = jax.nn.sigmoid(gc)
        dms = dmv * sg
        d_val = dms * gc
        d_gate = dms * vc * (1.0 + gc * (1.0 - sg))

        def back(r, dc, own, a1, a2, w):
            up1 = jnp.where(rows < seq - 1, pltpu.roll(dc, seq - 1, 0), 0.0)
            up2 = jnp.where(rows < seq - 2, pltpu.roll(dc, seq - 2, 0), 0.0)
            da_ref[r] = (w[2:3] * dc + w[1:2] * up1 + w[0:1] * up2).astype(da_ref.dtype)
            dw_ref[r, 0:1, :] = jnp.sum(dc * a2, axis=0, keepdims=True)
            dw_ref[r, 1:2, :] = jnp.sum(dc * a1, axis=0, keepdims=True)
            dw_ref[r, 2:3, :] = jnp.sum(dc * own, axis=0, keepdims=True)
            db_ref[r] = jnp.sum(dc, axis=0, keepdims=True)

        back(0, d_gate, ga, g1, g2, wg)
        back(1, d_val, va, v1, v2, wv)

    def w_spec(off, r):
        return pl.BlockSpec((None, r, LANES), lambda k, j: (k + off, 0, j))

    def pair_spec(r):
        return pl.BlockSpec((None, 2, r, LANES), lambda k, j: (k, 0, 0, j))

    sd = jax.ShapeDtypeStruct
    return pl.pallas_call(
        body, name="conv_gate_bwd", grid=(half, nc),
        in_specs=[pair_spec(seq), w_spec(0, 3), w_spec(half, 3), w_spec(0, 1), w_spec(half, 1),
                  pl.BlockSpec((seq, LANES), lambda k, j: (0, k * nc + j))],
        out_specs=[pair_spec(seq), pair_spec(3), pair_spec(1)],
        out_shape=[sd((half, 2, seq, c), BF16), sd((half, 2, 3, c), F32), sd((half, 2, 1, c), F32)],
        compiler_params=_cparams(),
    )(a, cw, cw, cb, cb, dm)


ROW_T = 256


def _local_step(x, positions, target, w, emit=lambda **grads: None):
    seq, d = x.shape
    t_row = min(ROW_T, seq)
    nrow = seq // t_row
    ssm_w = d // 2
    nj = ssm_w // LANES
    n_groups = ssm_w // SSM_GROUP
    nh = w["wuq"].shape[0]
    q_rank = w["wuq"].shape[1]
    kv_rank = w["wukv"].shape[1]
    ns = w["wup"].shape[0]
    c_ff = w["wup"].shape[2]
    in_pad = w["win"].shape[1]
    tm = min(512, seq)
    nm = seq // tm
    sw = 2 * STATE_BLOCK
    g1 = (nrow,)

    lr3 = w["lam_re"].reshape(n_groups, 1, SSM_STATE)
    li3 = w["lam_im"].reshape(n_groups, 1, SSM_STATE)
    ldt3 = w["log_dt"].reshape(n_groups, 1, 1)
    bt_re = jnp.swapaxes(w["b_re"].reshape(n_groups, SSM_STATE, SSM_GROUP), 1, 2)
    bt_im = jnp.swapaxes(w["b_im"].reshape(n_groups, SSM_STATE, SSM_GROUP), 1, 2)
    abar_re, abar_im, bbt_re, bbt_im = _s5_prep(lr3, li3, ldt3, bt_re, bt_im)
    eye = jnp.eye(GROUPS_PER_BLOCK, dtype=F32)

    def blockdiag_in(bb):
        t = bb.reshape(nj, GROUPS_PER_BLOCK, SSM_GROUP, SSM_STATE)
        return jnp.einsum("jghp,gk->jghkp", t, eye).reshape(nj, LANES, STATE_BLOCK)

    def blockdiag_in_t(dwb):
        t = dwb.reshape(nj, GROUPS_PER_BLOCK, SSM_GROUP, GROUPS_PER_BLOCK, SSM_STATE)
        return jnp.einsum("jghkp,gk->jghp", t, eye).reshape(n_groups, SSM_GROUP, SSM_STATE)

    def blockdiag_out(cc):
        t = cc.reshape(nj, GROUPS_PER_BLOCK, SSM_GROUP, SSM_STATE)
        return jnp.einsum("jghp,gk->jkpgh", t, eye).reshape(nj, STATE_BLOCK, LANES)

    def blockdiag_out_t(dwc):
        t = dwc.reshape(nj, GROUPS_PER_BLOCK, SSM_STATE, GROUPS_PER_BLOCK, SSM_GROUP)
        return jnp.einsum("jkpgh,gk->jghp", t, eye).reshape(n_groups, SSM_GROUP, SSM_STATE)

    c_re = w["c_re"].reshape(n_groups, SSM_GROUP, SSM_STATE)
    c_im = w["c_im"].reshape(n_groups, SSM_GROUP, SSM_STATE)
    wb = jnp.concatenate([blockdiag_in(bbt_re), blockdiag_in(bbt_im)], axis=2).astype(BF16)
    wc = jnp.concatenate([blockdiag_out(c_re), -blockdiag_out(c_im)], axis=1).astype(BF16)
    a_lay = jnp.concatenate([abar_re.reshape(nj, 1, STATE_BLOCK), abar_im.reshape(nj, 1, STATE_BLOCK)],
                            axis=1).reshape(1, nj * sw)

    attn_w = w["attn_norm"]
    hn = _blockwise("norm1", lambda xb, wv: _rms(xb, wv), [x, attn_w], [_row_spec(t_row, d), _full_spec((1, d))],
                    [((seq, d), BF16)], [_row_spec(t_row, d)], g1)[0]
    proj = _mm2d("proj", hn, w["win"], NN, F32, tn=640)

    s0 = _mm("ssm_bu", proj, wb, grid=(nm, nj, 1),
             a_spec=pl.BlockSpec((tm, LANES), lambda i, j, k: (i, j)),
             b_spec=pl.BlockSpec((None, LANES, sw), lambda i, j, k: (j, 0, 0)),
             o_spec=pl.BlockSpec((tm, sw), lambda i, j, k: (i, j)),
             out_shape=(seq, nj * sw), out_dtype=F32)
    s_all = _s5_scan(s0, a_lay)
    ylin = _mm("ssm_cy", s_all, wc, grid=(nm, nj, 1),
               a_spec=pl.BlockSpec((tm, sw), lambda i, j, k: (i, j)),
               b_spec=pl.BlockSpec((None, sw, LANES), lambda i, j, k: (j, 0, 0)),
               o_spec=pl.BlockSpec((tm, LANES), lambda i, j, k: (i, j)),
               out_shape=(seq, ssm_w), out_dtype=F32)
    u_spec = pl.BlockSpec((t_row, ssm_w), lambda i: (i, 0))

    def ypre_fn(yl, ub, dsk):
        yp = yl + dsk * ub
        return yp, jax.nn.gelu(yp)

    y_pre, yg = _blockwise("ssm_gelu", ypre_fn, [ylin, proj, w["ssm_d"]],
                           [_row_spec(t_row, ssm_w), u_spec, _full_spec((1, ssm_w))],
                           [((seq, ssm_w), F32), ((seq, ssm_w), BF16)],
                           [_row_spec(t_row, ssm_w)] * 2, g1)
    z = _mm2d("ssm_glu", yg, w["wglu"], NN, F32, res=w["b_glu"])
    y_ssm = _blockwise("ssm_gate", lambda yp, zb: jax.nn.gelu(yp) * jax.nn.sigmoid(zb), [y_pre, z],
                       [_row_spec(t_row, ssm_w)] * 2, [((seq, ssm_w), F32)], [_row_spec(t_row, ssm_w)], g1)[0]

    cq_off, ckv_off, kpe_off = ssm_w, ssm_w + q_rank, ssm_w + q_rank + kv_rank
    c_q = proj[:, cq_off:ckv_off]
    c_kv = proj[:, ckv_off:kpe_off]
    kpe_raw = proj[:, kpe_off:kpe_off + LANES]
    pos_b = jnp.broadcast_to(positions.astype(F32)[:, None], (seq, LANES))
    inv_freq = ROPE_THETA ** (-jnp.arange(0, QK_ROPE, 2, dtype=F32) / QK_ROPE)
    inv128 = jnp.tile(inv_freq, 4).reshape(1, LANES)

    def mla_prep_fn(cq, ckv, kp, pb, inv, wq, wkv):
        ang = pb * inv
        lane = lax.broadcasted_iota(jnp.int32, ang.shape, 1)
        cs, sn = jnp.cos(ang), jnp.sin(ang)
        cos = jnp.where(lane < QK_ROPE, cs, 0.0)
        sa = jnp.where(lane < QK_ROPE // 2, -sn, 0.0)
        sb = jnp.where(jnp.logical_and(lane >= QK_ROPE // 2, lane < QK_ROPE), sn, 0.0)
        return _rms(cq, wq), _rms(ckv, wkv), _rope128(kp, cos, sa, sb), cos, sa, sb

    qn, kvn, kpe, cos_t, sa_t, sb_t = _blockwise(
        "mla_prep", mla_prep_fn, [c_q, c_kv, kpe_raw, pos_b, inv128, w["q_norm"], w["kv_norm"]],
        [_row_spec(t_row, q_rank), _row_spec(t_row, kv_rank), _row_spec(t_row, LANES), _row_spec(t_row, LANES),
         _full_spec((1, LANES)), _full_spec((1, q_rank)), _full_spec((1, kv_rank))],
        [((seq, q_rank), BF16), ((seq, kv_rank), BF16), ((seq, LANES), BF16)] + [((seq, LANES), F32)] * 3,
        [_row_spec(t_row, q_rank), _row_spec(t_row, kv_rank)] + [_row_spec(t_row, LANES)] * 4, g1)

    def head_mm(name, act, wh, out_dtype):
        kdim, ndim = wh.shape[1], wh.shape[2]
        return _mm(name, act, wh, grid=(nh, nm, 1),
                   a_spec=pl.BlockSpec((tm, kdim), lambda h, i, k: (i, 0)),
                   b_spec=pl.BlockSpec((None, kdim, ndim), lambda h, i, k: (h, 0, 0)),
                   o_spec=pl.BlockSpec((None, tm, ndim), lambda h, i, k: (h, i, 0)),
                   out_shape=(nh, seq, ndim), out_dtype=out_dtype)

    q_raw = head_mm("mla_q", qn, w["wuq"], F32)
    kv = head_mm("mla_kv", kvn, w["wukv"], BF16)
    y_mla = _attn_fwd(q_raw, kv, kpe, cos_t, sa_t, sb_t)
    mla_w = nh * V_DIM

    def outnorm_fn(ys, ym, ws, wm):
        return jnp.concatenate([_rms(ys, ws), _rms(ym, wm)], axis=1)

    ycat = _blockwise("out_norm", outnorm_fn, [y_ssm, y_mla, w["son"], w["mon"]],
                      [_row_spec(t_row, ssm_w), _row_spec(t_row, mla_w), _full_spec((1, ssm_w)), _full_spec((1, mla_w))],
                      [((seq, d), BF16)], [_row_spec(t_row, d)], g1)[0]
    h1 = _mm2d("out_proj", ycat, w["wout"], NN, F32, res=x)

    hn2 = _blockwise("norm2", lambda hb, wv: _rms(hb, wv), [h1, w["ffn_norm"]],
                     [_row_spec(t_row, d), _full_spec((1, d))], [((seq, d), BF16)], [_row_spec(t_row, d)], g1)[0]
    tku = d
    half = ns // 2
    a_ff = _mm("ffn_up", hn2, w["wup"], grid=(ns, nm, d // tku),
               a_spec=pl.BlockSpec((tm, tku), lambda s, i, k: (i, k)),
               b_spec=pl.BlockSpec((None, tku, c_ff), lambda s, i, k: (s, k, 0)),
               o_spec=pl.BlockSpec((None, None, tm, c_ff), lambda s, i, k: (s % half, s // half, i, 0)),
               out_shape=(half, 2, seq, c_ff), out_dtype=F32)
    cb3 = w["conv_b"].reshape(ns, 1, c_ff)
    m_ff = _conv_gate_fwd(a_ff, w["conv_w"], cb3)
    d_ff = half * c_ff
    wdn = w["wdown"]
    tnd = _tile(d, 512)
    tmx, tnx = min(1024, seq), _tile(d, 1024)
    h2 = _mm2d("ffn_down", m_ff, wdn, NN, F32, tk=d_ff, res=h1)

    def loss_fn(hb, tb, wv):
        def f(hh, ww):
            err = _rms(hh, ww) - tb
            return 0.5 * jnp.sum(jnp.mean(err * err, axis=-1))

        lossv, (dh, dw) = jax.value_and_grad(f, argnums=(0, 1))(hb, wv)
        return dh, dh, jnp.full((1, LANES), lossv, F32), dw

    fin_w = w["final_norm"].reshape(1, d)
    dh2, dh2b, loss_acc, g_final = _blockwise(
        "loss_head", loss_fn, [h2, target, fin_w], [_row_spec(t_row, d), _row_spec(t_row, d), _full_spec((1, d))],
        [((seq, d), F32), ((seq, d), BF16), ((1, LANES), F32), ((1, d), F32)],
        [_row_spec(t_row, d), _row_spec(t_row, d), _full_spec((1, LANES)), _full_spec((1, d))], g1, n_acc=2)
    loss = loss_acc[0, 0]

    dm = _mm2d("ffn_down_dx", dh2b, wdn, NT, F32, tn=c_ff)
    tks = seq
    g_wdown = _mm2d("ffn_down_dw", m_ff, dh2b, TN, BF16, tm=c_ff)
    emit(wdown=g_wdown)
    da_ff, g_convw2, g_convb2 = _conv_gate_bwd(a_ff, w["conv_w"], cb3, dm)
    g_convw = jnp.swapaxes(g_convw2, 0, 1).reshape(ns, 3, c_ff)
    g_convb = jnp.swapaxes(g_convb2, 0, 1).reshape(ns, 1, c_ff)
    g_wup = _mm("ffn_up_dw", hn2, da_ff, grid=(ns, d // tnd, seq // tks), contract=TN,
                a_spec=pl.BlockSpec((tks, tnd), lambda s, j, k: (k, j)),
                b_spec=pl.BlockSpec((None, None, tks, c_ff), lambda s, j, k: (s % half, s // half, k, 0)),
                o_spec=pl.BlockSpec((None, tnd, c_ff), lambda s, j, k: (s, j, 0)),
                out_shape=(ns, d, c_ff), out_dtype=BF16)
    emit(wup=g_wup, conv_w=g_convw)
    dhn2 = _mm("ffn_up_dx", da_ff, w["wup"], grid=(seq // tmx, d // tnx, ns), contract=NT,
               a_spec=pl.BlockSpec((None, None, tmx, c_ff), lambda i, j, s: (s % half, s // half, i, 0)),
               b_spec=pl.BlockSpec((None, tnx, c_ff), lambda i, j, s: (s, j, 0)),
               o_spec=pl.BlockSpec((tmx, tnx), lambda i, j, s: (i, j)),
               out_shape=(seq, d), out_dtype=F32)

    def norm_bwd_fn(hb, dres, dn, wv):
        dx_, dw_ = _rms_bwd(hb, wv, dn)
        dtot = dres + dx_
        return dtot, dtot, dw_

    dh1, dh1b, g_ffn_norm = _blockwise(
        "norm2_bwd", norm_bwd_fn, [h1, dh2, dhn2, w["ffn_norm"]],
        [_row_spec(t_row, d)] * 3 + [_full_spec((1, d))],
        [((seq, d), F32), ((seq, d), BF16), ((1, d), F32)],
        [_row_spec(t_row, d), _row_spec(t_row, d), _full_spec((1, d))], g1, n_acc=1)

    dycat = _mm2d("out_proj_dx", dh1b, w["wout"], NT, F32)
    g_wout = _mm2d("out_proj_dw", ycat, dh1b, TN, BF16)

    def outnorm_bwd_fn(ys, ym, dyc, ws, wm):
        dys, dws = _rms_bwd(ys, ws, dyc[:, :ssm_w])
        dym, dwm = _rms_bwd(ym, wm, dyc[:, ssm_w:])
        return dys, dym, dws, dwm

    dy_ssm, dy_mla, g_son, g_mon = _blockwise(
        "out_norm_bwd", outnorm_bwd_fn, [y_ssm, y_mla, dycat, w["son"], w["mon"]],
        [_row_spec(t_row, ssm_w), _row_spec(t_row, mla_w), _row_spec(t_row, d), _full_spec((1, ssm_w)),
         _full_spec((1, mla_w))],
        [((seq, ssm_w), F32), ((seq, mla_w), F32), ((1, ssm_w), F32), ((1, mla_w), F32)],
        [_row_spec(t_row, ssm_w), _row_spec(t_row, mla_w), _full_spec((1, ssm_w)), _full_spec((1, mla_w))],
        g1, n_acc=2)

    def gate_bwd1_fn(dy, yp, zb):
        ygv = jax.nn.gelu(yp)
        sg = jax.nn.sigmoid(zb)
        dz = dy * ygv * sg * (1.0 - sg)
        return dz, jnp.sum(dz, axis=0, keepdims=True)

    dz, g_bglu = _blockwise("ssm_gate_bwd", gate_bwd1_fn, [dy_ssm, y_pre, z], [_row_spec(t_row, ssm_w)] * 3,
                            [((seq, ssm_w), BF16), ((1, ssm_w), F32)],
                            [_row_spec(t_row, ssm_w), _full_spec((1, ssm_w))], g1, n_acc=1)
    dyg2 = _mm2d("ssm_glu_dx", dz, w["wglu"], NT, F32)
    g_wglu = _mm2d("ssm_glu_dw", yg, dz, TN, BF16)

    def gelu_bwd_fn(dy, yp, zb, dg2, ub, dsk):
        dyg = dy * jax.nn.sigmoid(zb) + dg2
        _, vjp = jax.vjp(jax.nn.gelu, yp)
        dyp = vjp(dyg)[0]
        return dyp, dyp * dsk, jnp.sum(dyp * ub, axis=0, keepdims=True)

    dy_pre, du1, g_ssmd = _blockwise(
        "ssm_gelu_bwd", gelu_bwd_fn, [dy_ssm, y_pre, z, dyg2, proj, w["ssm_d"]],
        [_row_spec(t_row, ssm_w)] * 4 + [u_spec, _full_spec((1, ssm_w))],
        [((seq, ssm_w), BF16), ((seq, ssm_w), F32), ((1, ssm_w), F32)],
        [_row_spec(t_row, ssm_w), _row_spec(t_row, ssm_w), _full_spec((1, ssm_w))], g1, n_acc=1)
    ds_all = _mm("ssm_cy_dx", dy_pre, wc, grid=(nm, nj, 1), contract=NT,
                 a_spec=pl.BlockSpec((tm, LANES), lambda i, j, k: (i, j)),
                 b_spec=pl.BlockSpec((None, sw, LANES), lambda i, j, k: (j, 0, 0)),
                 o_spec=pl.BlockSpec((tm, sw), lambda i, j, k: (i, j)),
                 out_shape=(seq, nj * sw), out_dtype=F32)
    dwc = _mm("ssm_cy_dw", s_all, dy_pre, grid=(nj, 1, seq // tks), contract=TN,
              a_spec=pl.BlockSpec((tks, sw), lambda j, n, k: (k, j)),
              b_spec=pl.BlockSpec((tks, LANES), lambda j, n, k: (k, j)),
              o_spec=pl.BlockSpec((None, sw, LANES), lambda j, n, k: (j, 0, 0)),
              out_shape=(nj, sw, LANES), out_dtype=F32)
    lam, da_lay = _s5_scan_bwd(ds_all, s_all, a_lay)
    du = _mm("ssm_bu_dx", lam, wb, grid=(nm, nj, 1), contract=NT,
             a_spec=pl.BlockSpec((tm, sw), lambda i, j, k: (i, j)),
             b_spec=pl.BlockSpec((None, LANES, sw), lambda i, j, k: (j, 0, 0)),
             o_spec=pl.BlockSpec((tm, LANES), lambda i, j, k: (i, j)),
             out_shape=(seq, ssm_w), out_dtype=BF16,
             res=du1, res_spec=pl.BlockSpec((tm, LANES), lambda i, j, k: (i, j)))
    dwb = _mm("ssm_bu_dw", proj, lam, grid=(nj, 1, seq // tks), contract=TN,
              a_spec=pl.BlockSpec((tks, LANES), lambda j, n, k: (k, j)),
              b_spec=pl.BlockSpec((tks, sw), lambda j, n, k: (k, j)),
              o_spec=pl.BlockSpec((None, LANES, sw), lambda j, n, k: (j, 0, 0)),
              out_shape=(nj, LANES, sw), out_dtype=F32)
    g_c_re = blockdiag_out_t(dwc[:, :STATE_BLOCK, :])
    g_c_im = -blockdiag_out_t(dwc[:, STATE_BLOCK:, :])
    dbbt_re = blockdiag_in_t(dwb[:, :, :STATE_BLOCK])
    dbbt_im = blockdiag_in_t(dwb[:, :, STATE_BLOCK:])
    da3 = da_lay.reshape(nj, 2, STATE_BLOCK)
    dabar_re = da3[:, 0, :].reshape(n_groups, 1, SSM_STATE)
    dabar_im = da3[:, 1, :].reshape(n_groups, 1, SSM_STATE)
    g_lr3, g_li3, g_ldt3, g_bt_re, g_bt_im = _s5_prep_bwd(lr3, li3, ldt3, bt_re, bt_im,
                                                           dabar_re, dabar_im, dbbt_re, dbbt_im)

    dq_raw, dkv, dkp_h = _attn_bwd(q_raw, kv, kpe, cos_t, sa_t, sb_t, dy_mla)

    def head_mm_dx(name, dact, wh):
        kdim, ndim = wh.shape[1], wh.shape[2]
        return _mm(name, dact, wh, grid=(nm, 1, nh), contract=NT,
                   a_spec=pl.BlockSpec((None, tm, ndim), lambda i, j, h: (h, i, 0)),
                   b_spec=pl.BlockSpec((None, kdim, ndim), lambda i, j, h: (h, 0, 0)),
                   o_spec=pl.BlockSpec((tm, kdim), lambda i, j, h: (i, 0)),
                   out_shape=(seq, kdim), out_dtype=F32)

    def head_mm_dw(name, act, dact):
        kdim, ndim = act.shape[1], dact.shape[2]
        return _mm(name, act, dact, grid=(nh, 1, seq // tks), contract=TN,
                   a_spec=pl.BlockSpec((tks, kdim), lambda h, j, k: (k, 0)),
                   b_spec=pl.BlockSpec((None, tks, ndim), lambda h, j, k: (h, k, 0)),
                   o_spec=pl.BlockSpec((None, kdim, ndim), lambda h, j, k: (h, 0, 0)),
                   out_shape=(nh, kdim, ndim), out_dtype=BF16)

    dqn = head_mm_dx("mla_q_dx", dq_raw, w["wuq"])
    g_wuq = head_mm_dw("mla_q_dw", qn, dq_raw)
    dkvn = head_mm_dx("mla_kv_dx", dkv, w["wukv"])
    g_wukv = head_mm_dw("mla_kv_dw", kvn, dkv)
    emit(wout=g_wout, wglu=g_wglu, wuq=g_wuq, wukv=g_wukv)

    def mla_prep_bwd_fn(cq, ckv, dqn_b, dkvn_b, dkp_b, cos, sa, sb, wq, wkv):
        dcq, dwq = _rms_bwd(cq, wq, dqn_b)
        dckv, dwkv = _rms_bwd(ckv, wkv, dkvn_b)
        dkp_sum = dkp_b[0]
        for h in range(1, nh):
            dkp_sum = dkp_sum + dkp_b[h]
        return dcq, dckv, _rope128_t(dkp_sum, cos, sa, sb), dwq, dwkv

    dc_q, dc_kv, dkpe_raw, g_qnorm, g_kvnorm = _blockwise(
        "mla_prep_bwd", mla_prep_bwd_fn, [c_q, c_kv, dqn, dkvn, dkp_h, cos_t, sa_t, sb_t, w["q_norm"], w["kv_norm"]],
        [_row_spec(t_row, q_rank), _row_spec(t_row, kv_rank), _row_spec(t_row, q_rank), _row_spec(t_row, kv_rank),
         pl.BlockSpec((nh, t_row, LANES), lambda i: (0, i, 0))] + [_row_spec(t_row, LANES)] * 3
        + [_full_spec((1, q_rank)), _full_spec((1, kv_rank))],
        [((seq, q_rank), BF16), ((seq, kv_rank), BF16), ((seq, LANES), BF16), ((1, q_rank), F32), ((1, kv_rank), F32)],
        [_row_spec(t_row, q_rank), _row_spec(t_row, kv_rank), _row_spec(t_row, LANES), _full_spec((1, q_rank)),
         _full_spec((1, kv_rank))], g1, n_acc=2)

    dproj = jnp.concatenate([du, dc_q, dc_kv, dkpe_raw], axis=1)
    g_win = _mm2d("proj_dw", hn, dproj, TN, BF16, tn=640)
    emit(win=g_win)
    dhn = _mm2d("proj_dx", dproj, w["win"], NT, F32)

    def norm1_bwd_fn(xb, dres, dn, wv):
        dx_, dw_ = _rms_bwd(xb, wv, dn)
        return dres + dx_, dw_

    grad_x, g_attn_norm = _blockwise(
        "norm1_bwd", norm1_bwd_fn, [x, dh1, dhn, attn_w], [_row_spec(t_row, d)] * 3 + [_full_spec((1, d))],
        [((seq, d), F32), ((1, d), F32)], [_row_spec(t_row, d), _full_spec((1, d))], g1, n_acc=1)

    grads = dict(
        attn_norm=g_attn_norm, win=g_win, lam_re=g_lr3, lam_im=g_li3, log_dt=g_ldt3,
        b_re=jnp.swapaxes(g_bt_re, 1, 2), b_im=jnp.swapaxes(g_bt_im, 1, 2), c_re=g_c_re, c_im=g_c_im,
        ssm_d=g_ssmd, wglu=g_wglu, b_glu=g_bglu, q_norm=g_qnorm, wuq=g_wuq, kv_norm=g_kvnorm, wukv=g_wukv,
        son=g_son, mon=g_mon, wout=g_wout, ffn_norm=g_ffn_norm, wup=g_wup, conv_w=g_convw, conv_b=g_convb,
        wdown=g_wdown, final_norm=g_final)
    return loss, grad_x, grads


def _mesh_pos():
    return lax.axis_index("x"), lax.axis_index("y"), lax.axis_index("c")


def _handshake_all():
    x, y, c = _mesh_pos()
    barrier = pltpu.get_barrier_semaphore()
    for k in range(1, N_DEV):
        peer = (1 - x if k & 4 else x, 1 - y if k & 2 else y, 1 - c if k & 1 else c)
        pl.semaphore_signal(barrier, inc=1, device_id=peer, device_id_type=MESH)
    pl.semaphore_wait(barrier, N_DEV - 1)


def _comm_call(name, body, n, out_shape, ins, collective_id):
    sems = [pltpu.SemaphoreType.DMA((7 * n,)), pltpu.SemaphoreType.DMA((7 * n,)), pltpu.SemaphoreType.DMA((n,))]
    if collective_id is None:
        any_spec = pl.BlockSpec(memory_space=pl.ANY)
        return pl.pallas_call(body, name=name, out_shape=out_shape, in_specs=[any_spec] * n,
                              out_specs=[any_spec] * n, scratch_shapes=sems)(*ins)
    return pl.kernel(body, name=name, out_type=out_shape,
                     mesh=plsc.ScalarSubcoreMesh(axis_name="seq", num_cores=1), scratch_types=sems,
                     compiler_params=pltpu.CompilerParams(collective_id=collective_id))(*ins)


def _all_gather(name, xs, collective_id=None):
    n = len(xs)

    def body(*refs):
        x_refs, o_refs = refs[:n], refs[n:2 * n]
        send_sems, recv_sems, local_sems = refs[2 * n:]
        if collective_id is not None:
            _handshake_all()
        x, y, c = _mesh_pos()
        me, sibling = (x, y, c), (x, y, 1 - c)
        chips = [(1 - x, y), (x, 1 - y), (1 - x, 1 - y)]

        def slot(o_ref, px, py, pc):
            return o_ref.at[4 * px + 2 * py + pc]

        def copy(t, k, block, to, src=None):
            dst = slot(o_refs[t], *block)
            return pltpu.make_async_remote_copy(
                src_ref=dst if src is None else src, dst_ref=dst,
                send_sem=send_sems.at[7 * t + k], recv_sem=recv_sems.at[7 * t + k],
                device_id=to, device_id_type=MESH)

        started = []
        for t in range(n):
            mine = pltpu.make_async_copy(x_refs[t], slot(o_refs[t], *me), local_sems.at[t])
            mine.start()
            started.append(mine)
        first = []
        for t in range(n):
            first.append(copy(t, 0, me, sibling, src=x_refs[t]))
            first += [copy(t, 1 + j, me, (*chip, c), src=x_refs[t]) for j, chip in enumerate(chips)]
        for cp in first:
            cp.start()
        passed = []
        for j, chip in enumerate(chips):
            for t in range(n):
                copy(t, 1 + j, (*chip, c), me).wait_recv()
                fwd = copy(t, 4 + j, (*chip, c), sibling)
                fwd.start()
                passed.append(fwd)
        for t in range(n):
            copy(t, 0, sibling, me).wait_recv()
            for j, chip in enumerate(chips):
                copy(t, 4 + j, (*chip, 1 - c), me).wait_recv()
        for cp in first + passed:
            cp.wait_send()
        for mine in started:
            mine.wait()

    out_shape = [jax.ShapeDtypeStruct((N_DEV,) + v.shape, v.dtype) for v in xs]
    return _comm_call(name, body, n, out_shape, xs, collective_id)


def _exchange_partials(name, gs, collective_id=None):
    n = len(gs)

    def body(*refs):
        g_refs, o_refs = refs[:n], refs[n:2 * n]
        send_sems, recv_sems, local_sems = refs[2 * n:]
        if collective_id is not None:
            _handshake_all()
        x, y, c = _mesh_pos()
        me_idx = 4 * x + 2 * y + c
        copies = []
        for t in range(n):
            mine = pltpu.make_async_copy(g_refs[t].at[me_idx], o_refs[t].at[me_idx], local_sems.at[t])
            mine.start()
            copies.append(mine)
        remote = []
        for k in range(1, N_DEV):
            px = 1 - x if k & 4 else x
            py = 1 - y if k & 2 else y
            pc = 1 - c if k & 1 else c
            p_idx = 4 * px + 2 * py + pc
            for t in range(n):
                cp = pltpu.make_async_remote_copy(
                    src_ref=g_refs[t].at[p_idx], dst_ref=o_refs[t].at[me_idx],
                    send_sem=send_sems.at[7 * t + k - 1], recv_sem=recv_sems.at[7 * t + k - 1],
                    device_id=(px, py, pc), device_id_type=MESH)
                cp.start()
                landing = pltpu.make_async_remote_copy(
                    src_ref=g_refs[t].at[p_idx], dst_ref=o_refs[t].at[p_idx],
                    send_sem=send_sems.at[7 * t + k - 1], recv_sem=recv_sems.at[7 * t + k - 1],
                    device_id=(px, py, pc), device_id_type=MESH)
                remote.append((cp, landing))
        for cp, landing in remote:
            landing.wait_recv()
        for cp, landing in remote:
            cp.wait_send()
        for mine in copies:
            mine.wait()

    out_shape = [jax.ShapeDtypeStruct(v.shape, v.dtype) for v in gs]
    return _comm_call(name, body, n, out_shape, gs, collective_id)


ADAM_BLOCK_ELEMS = 128 * 1024


def _adamw_sum(name, parts, wv, mv, vv):
    npart, r, c = parts.shape
    tr = r
    if r * c > ADAM_BLOCK_ELEMS and r % SUBLANES == 0:
        tr = SUBLANES
        while r % (tr * 2) == 0 and tr * 2 * c <= ADAM_BLOCK_ELEMS:
            tr *= 2
    bc1 = 1.0 - ADAM_B1 ** ADAM_STEP
    bc2 = 1.0 - ADAM_B2 ** ADAM_STEP

    def fn(pb, wb_, mb, vb):
        g = pb[0].astype(F32)
        for j in range(1, npart):
            g = g + pb[j].astype(F32)
        m_new = ADAM_B1 * mb + (1.0 - ADAM_B1) * g
        v_new = ADAM_B2 * vb + (1.0 - ADAM_B2) * (g * g)
        m_hat = m_new / bc1
        v_hat = v_new / bc2
        delta = -ADAM_LR * (m_hat / (jnp.sqrt(v_hat) + ADAM_EPS) + ADAM_WD * wb_)
        return g, delta, m_new, v_new

    row = pl.BlockSpec((tr, c), lambda i: (i, 0))
    return _blockwise(name, fn, [parts, wv, mv, vv],
                      [pl.BlockSpec((npart, tr, c), lambda i: (0, i, 0)), row, row, row],
                      [((r, c), F32)] * 4, [row] * 4, (r // tr,))


_SMALL = ["attn_norm", "lam_re", "lam_im", "log_dt", "b_re", "b_im", "c_re", "c_im", "ssm_d", "b_glu",
          "q_norm", "kv_norm", "son", "mon", "ffn_norm", "conv_b", "final_norm"]
_BIG = ["win", "wglu", "wuq", "wukv", "wout", "wup", "wdown", "conv_w"]
_ORDER = ["attn_norm", "win", "lam_re", "lam_im", "log_dt", "b_re", "b_im", "c_re", "c_im", "ssm_d", "wglu",
          "b_glu", "q_norm", "wuq", "kv_norm", "wukv", "son", "mon", "wout", "ffn_norm", "wup", "conv_w",
          "conv_b", "wdown", "final_norm"]


def _pack(arrs):
    flat = jnp.concatenate([a.reshape(-1).astype(F32) for a in arrs])
    pad = (-flat.shape[0]) % (LANES * LANES)
    return jnp.pad(flat, (0, pad)).reshape(-1, LANES)


def _unpack(packed, shapes):
    flat = packed.reshape(-1)
    out, off = [], 0
    for s in shapes:
        n = math.prod(s)
        out.append(flat[off:off + n].reshape(s))
        off += n
    return out


def kernel(x, positions, attn_norm_w, w_in, ssm_lambda_re, ssm_lambda_im, ssm_log_dt, ssm_b_re, ssm_b_im, ssm_c_re, ssm_c_im, ssm_d, ssm_w_glu, ssm_b_glu, mla_q_norm_w, mla_w_uq, mla_kv_norm_w, mla_w_ukv, ssm_out_norm_w, mla_out_norm_w, w_out, ffn_norm_w, ffn_w_up, ffn_conv_w, ffn_conv_b, ffn_w_down, final_norm_w, loss_target, m_attn_norm_w, m_w_in, m_ssm_lambda_re, m_ssm_lambda_im, m_ssm_log_dt, m_ssm_b_re, m_ssm_b_im, m_ssm_c_re, m_ssm_c_im, m_ssm_d, m_ssm_w_glu, m_ssm_b_glu, m_mla_q_norm_w, m_mla_w_uq, m_mla_kv_norm_w, m_mla_w_ukv, m_ssm_out_norm_w, m_mla_out_norm_w, m_w_out, m_ffn_norm_w, m_ffn_w_up, m_ffn_conv_w, m_ffn_conv_b, m_ffn_w_down, m_final_norm_w, v_attn_norm_w, v_w_in, v_ssm_lambda_re, v_ssm_lambda_im, v_ssm_log_dt, v_ssm_b_re, v_ssm_b_im, v_ssm_c_re, v_ssm_c_im, v_ssm_d, v_ssm_w_glu, v_ssm_b_glu, v_mla_q_norm_w, v_mla_w_uq, v_mla_kv_norm_w, v_mla_w_ukv, v_ssm_out_norm_w, v_mla_out_norm_w, v_w_out, v_ffn_norm_w, v_ffn_w_up, v_ffn_conv_w, v_ffn_conv_b, v_ffn_w_down, v_final_norm_w):
    wts = dict(attn_norm=attn_norm_w, win=w_in, lam_re=ssm_lambda_re, lam_im=ssm_lambda_im, log_dt=ssm_log_dt,
               b_re=ssm_b_re, b_im=ssm_b_im, c_re=ssm_c_re, c_im=ssm_c_im, ssm_d=ssm_d, wglu=ssm_w_glu,
               b_glu=ssm_b_glu, q_norm=mla_q_norm_w, wuq=mla_w_uq, kv_norm=mla_kv_norm_w, wukv=mla_w_ukv,
               son=ssm_out_norm_w, mon=mla_out_norm_w, wout=w_out, ffn_norm=ffn_norm_w, wup=ffn_w_up,
               conv_w=ffn_conv_w, conv_b=ffn_conv_b, wdown=ffn_w_down, final_norm=final_norm_w)
    moms = dict(zip(_ORDER, [m_attn_norm_w, m_w_in, m_ssm_lambda_re, m_ssm_lambda_im, m_ssm_log_dt, m_ssm_b_re,
                             m_ssm_b_im, m_ssm_c_re, m_ssm_c_im, m_ssm_d, m_ssm_w_glu, m_ssm_b_glu, m_mla_q_norm_w,
                             m_mla_w_uq, m_mla_kv_norm_w, m_mla_w_ukv, m_ssm_out_norm_w, m_mla_out_norm_w, m_w_out,
                             m_ffn_norm_w, m_ffn_w_up, m_ffn_conv_w, m_ffn_conv_b, m_ffn_w_down, m_final_norm_w]))
    vels = dict(zip(_ORDER, [v_attn_norm_w, v_w_in, v_ssm_lambda_re, v_ssm_lambda_im, v_ssm_log_dt, v_ssm_b_re,
                             v_ssm_b_im, v_ssm_c_re, v_ssm_c_im, v_ssm_d, v_ssm_w_glu, v_ssm_b_glu, v_mla_q_norm_w,
                             v_mla_w_uq, v_mla_kv_norm_w, v_mla_w_ukv, v_ssm_out_norm_w, v_mla_out_norm_w, v_w_out,
                             v_ffn_norm_w, v_ffn_w_up, v_ffn_conv_w, v_ffn_conv_b, v_ffn_w_down, v_final_norm_w]))
    seq, d = x.shape[1], x.shape[2]
    in_width = w_in.shape[2]
    in_pad = -(-in_width // LANES) * LANES
    q_cols = mla_w_uq.shape[2]
    q_pad = 2 * LANES

    (win_g,) = _all_gather("gather_w_in", [jnp.pad(w_in[0], ((0, 0), (0, in_pad - in_width))).astype(BF16)])
    wglu_g, wuq_g, wukv_g, wout_g, convw_g = _all_gather(
        "gather_mix", [ssm_w_glu[0].astype(BF16), jnp.pad(mla_w_uq[0], ((0, 0), (0, q_pad - q_cols))).astype(BF16),
                       mla_w_ukv[0].astype(BF16), w_out[0].astype(BF16), ffn_conv_w[0]], collective_id=0)
    (wup_g,) = _all_gather("gather_ffn_up", [ffn_w_up[0].astype(BF16)], collective_id=1)
    (wdown_g,) = _all_gather("gather_ffn_down", [ffn_w_down[0].astype(BF16)], collective_id=2)
    ns = N_DEV
    c_ff = wup_g.shape[2]
    w = dict(
        attn_norm=attn_norm_w, win=win_g.reshape(d, in_pad), lam_re=ssm_lambda_re, lam_im=ssm_lambda_im,
        log_dt=ssm_log_dt, b_re=ssm_b_re, b_im=ssm_b_im, c_re=ssm_c_re, c_im=ssm_c_im, ssm_d=ssm_d,
        wglu=wglu_g.reshape(d // 2, d // 2), b_glu=ssm_b_glu, q_norm=mla_q_norm_w, wuq=wuq_g,
        kv_norm=mla_kv_norm_w, wukv=wukv_g, son=ssm_out_norm_w, mon=mla_out_norm_w, wout=wout_g.reshape(d, d),
        ffn_norm=ffn_norm_w, wup=wup_g, conv_w=convw_g, conv_b=ffn_conv_b,
        wdown=wdown_g.reshape(ns // 2 * c_ff, d), final_norm=final_norm_w)

    shard_layout = dict(
        win=lambda a: a[:, :in_width].reshape(N_DEV, d // N_DEV, in_width),
        wglu=lambda a: a.reshape(N_DEV, d // 2 // N_DEV, d // 2),
        wuq=lambda a: a[:, :, :q_cols], wukv=lambda a: a, wout=lambda a: a.reshape(N_DEV, d // N_DEV, d),
        wup=lambda a: a, wdown=lambda a: a.reshape(N_DEV, c_ff // 2, d), conv_w=lambda a: a)
    recv = {}
    next_id = [3]

    def exchange(sequencer=True, **grads):
        collective_id = None
        if sequencer:
            collective_id = next_id[0]
            next_id[0] += 1
        names = list(grads)
        got = _exchange_partials("exchange_" + "_".join(names), [shard_layout[k](grads[k]) for k in names],
                                 collective_id=collective_id)
        recv.update(zip(names, got))

    loss_part, grad_x, g = _local_step(x[0], positions[0], loss_target[0], w, emit=exchange)
    loss = lax.psum(loss_part, ("x", "y", "c"))
    small_shapes = [wts[k].shape for k in _SMALL]
    small_part = _pack([g[k] for k in _SMALL])
    small_all = _all_gather("gather_small_grads", [small_part], collective_id=next_id[0])[0]

    out = {}
    for k in _BIG:
        shp = wts[k].shape
        r, c = shp[-2], shp[-1]
        res = _adamw_sum("adamw_" + k, recv[k].reshape(N_DEV, r, c), wts[k].reshape(r, c),
                         moms[k].reshape(r, c), vels[k].reshape(r, c))
        out[k] = [a.reshape(shp) for a in res]
    sw_ = _pack([wts[k] for k in _SMALL])
    sm_ = _pack([moms[k] for k in _SMALL])
    sv_ = _pack([vels[k] for k in _SMALL])
    res = _adamw_sum("adamw_small", small_all, sw_, sm_, sv_)
    unpacked = [_unpack(a, small_shapes) for a in res]
    for i, k in enumerate(_SMALL):
        out[k] = [u[i] for u in unpacked]

    grad_x = grad_x.reshape(x.shape)
    return (loss, grad_x, *[out[k][0] for k in _ORDER], *[out[k][1] for k in _ORDER],
            *[out[k][2] for k in _ORDER], *[out[k][3] for k in _ORDER])
```

```python
import functools
import math

import jax
import jax.numpy as jnp
from jax import lax
from jax.experimental import pallas as pl
from jax.experimental.pallas import tpu as pltpu
from jax.experimental.pallas import tpu_sc as plsc

F32 = jnp.float32
BF16 = jnp.bfloat16
MESH = pl.DeviceIdType.MESH

N_DEV = 8
LANES = 128
SUBLANES = 8
VMEM_LIMIT = 48 * 1024 * 1024

SSM_GROUP = 16
SSM_STATE = 64
GROUPS_PER_BLOCK = LANES // SSM_GROUP
STATE_BLOCK = GROUPS_PER_BLOCK * SSM_STATE
QK_NOPE = 128
QK_ROPE = 64
V_DIM = 128
ROPE_THETA = 10000.0
RMS_EPS = 1e-6

ADAM_LR = 0.001
ADAM_B1 = 0.9
ADAM_B2 = 0.999
ADAM_EPS = 1e-08
ADAM_WD = 0.01
ADAM_STEP = 10

NN = ((1,), (0,))
NT = ((1,), (1,))
TN = ((0,), (0,))


def _cparams():
    return pltpu.CompilerParams(vmem_limit_bytes=VMEM_LIMIT)


def _tile(n, want):
    if n <= want:
        return n
    t = (want // LANES) * LANES
    while t >= LANES:
        if n % t == 0:
            return t
        t -= LANES
    return n


def _mm(name, a, b, *, grid, a_spec, b_spec, o_spec, out_shape, out_dtype, contract=NN,
        res=None, res_spec=None):
    nk = grid[-1]
    kaxis = len(grid) - 1
    acc_shape = tuple(d for d in o_spec.block_shape if d is not None)

    def body(*refs):
        a_ref, b_ref = refs[:2]
        r_ref = None if res is None else refs[2]
        o_ref = refs[2 if res is None else 3]
        part = lax.dot_general(a_ref[...].astype(BF16), b_ref[...].astype(BF16),
                               (contract, ((), ())), preferred_element_type=F32)
        if nk == 1:
            if r_ref is not None:
                part = part + r_ref[...].astype(F32)
            o_ref[...] = part.astype(o_ref.dtype)
            return
        acc = refs[-1]
        k = pl.program_id(kaxis)

        @pl.when(k == 0)
        def _():
            acc[...] = part

        @pl.when(k != 0)
        def _():
            acc[...] += part

        @pl.when(k == nk - 1)
        def _():
            r = acc[...]
            if r_ref is not None:
                r = r + r_ref[...].astype(F32)
            o_ref[...] = r.astype(o_ref.dtype)

    ins = [a, b] + ([] if res is None else [res])
    in_specs = [a_spec, b_spec] + ([] if res is None else [res_spec])
    return pl.pallas_call(
        body, name=name, grid=grid, in_specs=in_specs, out_specs=o_spec,
        out_shape=jax.ShapeDtypeStruct(out_shape, out_dtype),
        scratch_shapes=[pltpu.VMEM(acc_shape, F32)] if nk > 1 else [], compiler_params=_cparams(),
    )(*ins)


def _mm2d(name, a, b, contract, out_dtype, tm=512, tn=512, tk=2048, res=None):
    if contract == NN:
        (m, kk), n = a.shape, b.shape[1]
    elif contract == NT:
        (m, kk), n = a.shape, b.shape[0]
    else:
        (kk, m), n = a.shape, b.shape[1]
    tm, tn, tk = _tile(m, tm), _tile(n, tn), _tile(kk, tk)
    grid = (m // tm, n // tn, kk // tk)
    if contract == TN:
        a_spec = pl.BlockSpec((tk, tm), lambda i, j, k: (k, i))
    else:
        a_spec = pl.BlockSpec((tm, tk), lambda i, j, k: (i, k))
    if contract == NT:
        b_spec = pl.BlockSpec((tn, tk), lambda i, j, k: (j, k))
    else:
        b_spec = pl.BlockSpec((tk, tn), lambda i, j, k: (k, j))
    o_spec = pl.BlockSpec((tm, tn), lambda i, j, k: (i, j))
    res_spec = None
    if res is not None:
        if res.shape[0] == 1:
            res_spec = pl.BlockSpec((1, tn), lambda i, j, k: (0, j))
        else:
            res_spec = pl.BlockSpec((tm, tn), lambda i, j, k: (i, j))
    return _mm(name, a, b, grid=grid, a_spec=a_spec, b_spec=b_spec, o_spec=o_spec,
               out_shape=(m, n), out_dtype=out_dtype, contract=contract, res=res, res_spec=res_spec)


def _blockwise(name, fn, ins, in_specs, outs, out_specs, grid, n_acc=0, acc_all=True):
    n_in, n_out = len(ins), len(outs)
    n_plain = n_out - n_acc

    def body(*refs):
        vals = fn(*[r[...] for r in refs[:n_in]])
        if not isinstance(vals, (tuple, list)):
            vals = (vals,)
        o_refs = refs[n_in:n_in + n_out]
        for r, v in zip(o_refs[:n_plain], vals[:n_plain]):
            r[...] = v.astype(r.dtype)
        if n_acc:
            if acc_all:
                first = functools.reduce(jnp.logical_and, [pl.program_id(d) == 0 for d in range(len(grid))])
            else:
                first = pl.program_id(len(grid) - 1) == 0

            @pl.when(first)
            def _():
                for r, v in zip(o_refs[n_plain:], vals[n_plain:]):
                    r[...] = v.astype(r.dtype)

            @pl.when(jnp.logical_not(first))
            def _():
                for r, v in zip(o_refs[n_plain:], vals[n_plain:]):
                    r[...] += v.astype(r.dtype)

    return pl.pallas_call(
        body, name=name, grid=grid, in_specs=in_specs, out_specs=out_specs,
        out_shape=[jax.ShapeDtypeStruct(s, d) for s, d in outs], compiler_params=_cparams(),
    )(*ins)


def _row_spec(t, c):
    return pl.BlockSpec((t, c), lambda i: (i, 0))


def _full_spec(shape):
    nd = len(shape)
    return pl.BlockSpec(tuple(shape), lambda *g: (0,) * nd)


def _rms(xf, w):
    return xf * lax.rsqrt(jnp.mean(xf * xf, axis=-1, keepdims=True) + RMS_EPS) * w


def _rms_bwd(xf, w, dy):
    _, vjp = jax.vjp(_rms, xf, w)
    return vjp(dy)


def _s5_disc(lr, li, ldt, bre, bim):
    dt = jnp.exp(ldt)
    mag = jnp.exp(lr * dt)
    ar = mag * jnp.cos(li * dt)
    ai = mag * jnp.sin(li * dt)
    nr, ni = ar - 1.0, ai
    den = lr * lr + li * li
    zr = (nr * lr + ni * li) / den
    zi = (ni * lr - nr * li) / den
    return ar, ai, zr * bre - zi * bim, zr * bim + zi * bre


def _s5_prep(lr, li, ldt, bre, bim):
    def body(lr_r, li_r, ldt_r, bre_r, bim_r, ar_r, ai_r, br_r, bi_r):
        ar, ai, br, bi = _s5_disc(lr_r[...], li_r[...], ldt_r[...], bre_r[...], bim_r[...])
        ar_r[...] = ar
        ai_r[...] = ai
        br_r[...] = br
        bi_r[...] = bi

    sd = jax.ShapeDtypeStruct
    return pl.pallas_call(
        body, name="s5_prep",
        out_shape=[sd(lr.shape, F32), sd(lr.shape, F32), sd(bre.shape, F32), sd(bre.shape, F32)],
        compiler_params=_cparams(),
    )(lr, li, ldt, bre, bim)


def _s5_prep_bwd(lr, li, ldt, bre, bim, dar, dai, dbr, dbi):
    def body(lr_r, li_r, ldt_r, bre_r, bim_r, dar_r, dai_r, dbr_r, dbi_r, o0, o1, o2, o3, o4):
        _, vjp = jax.vjp(_s5_disc, lr_r[...], li_r[...], ldt_r[...], bre_r[...], bim_r[...])
        g = vjp((dar_r[...], dai_r[...], dbr_r[...], dbi_r[...]))
        for o, v in zip((o0, o1, o2, o3, o4), g):
            o[...] = v

    sd = jax.ShapeDtypeStruct
    return pl.pallas_call(
        body, name="s5_prep_bwd",
        out_shape=[sd(lr.shape, F32), sd(li.shape, F32), sd(ldt.shape, F32), sd(bre.shape, F32), sd(bim.shape, F32)],
        compiler_params=_cparams(),
    )(lr, li, ldt, bre, bim, dar, dai, dbr, dbi)


SCAN_T = 256


def _s5_scan(x, a):
    seq, width = x.shape
    w2 = 2 * STATE_BLOCK
    nj = width // w2
    t_blk = min(SCAN_T, seq)
    hb = STATE_BLOCK

    def body(x_ref, a_ref, o_ref, st):
        @pl.when(pl.program_id(1) == 0)
        def _():
            st[...] = jnp.zeros_like(st)

        ar = a_ref[:, :hb]
        ai = a_ref[:, hb:]

        def step(t, carry):
            sr, si = carry
            xr = x_ref[pl.ds(t, 1), :hb]
            xi = x_ref[pl.ds(t, 1), hb:]
            nr = ar * sr - ai * si + xr
            ni = ar * si + ai * sr + xi
            o_ref[pl.ds(t, 1), :hb] = nr
            o_ref[pl.ds(t, 1), hb:] = ni
            return nr, ni

        sr, si = lax.fori_loop(0, t_blk, step, (st[0:1, :], st[1:2, :]))
        st[0:1, :] = sr
        st[1:2, :] = si

    return pl.pallas_call(
        body, name="s5_scan", grid=(nj, seq // t_blk),
        in_specs=[pl.BlockSpec((t_blk, w2), lambda j, i: (i, j)), pl.BlockSpec((1, w2), lambda j, i: (0, j))],
        out_specs=pl.BlockSpec((t_blk, w2), lambda j, i: (i, j)),
        out_shape=jax.ShapeDtypeStruct(x.shape, F32),
        scratch_shapes=[pltpu.VMEM((SUBLANES, hb), F32)], compiler_params=_cparams(),
    )(x, a)


def _s5_scan_bwd(ds, s, a):
    seq, width = ds.shape
    w2 = 2 * STATE_BLOCK
    nj = width // w2
    t_blk = min(SCAN_T, seq)
    nb = seq // t_blk
    hb = STATE_BLOCK
    per8 = t_blk // SUBLANES

    def body(d_ref, s_ref, sprev_ref, a_ref, o_ref, da_ref, st):
        ib = pl.program_id(1)

        @pl.when(ib == 0)
        def _():
            st[...] = jnp.zeros_like(st)

        ar = a_ref[:, :hb]
        ai = a_ref[:, hb:]

        def step(tt, carry):
            lr, li = carry
            t = t_blk - 1 - tt
            dr = d_ref[pl.ds(t, 1), :hb]
            di = d_ref[pl.ds(t, 1), hb:]
            nr = ar * lr + ai * li + dr
            ni = ar * li - ai * lr + di
            o_ref[pl.ds(t, 1), :hb] = nr
            o_ref[pl.ds(t, 1), hb:] = ni
            return nr, ni

        lr, li = lax.fori_loop(0, t_blk, step, (st[0:1, :], st[1:2, :]))
        st[0:1, :] = lr
        st[1:2, :] = li

        lam = o_ref[...]
        sv = s_ref[...]
        rows = lax.broadcasted_iota(jnp.int32, sv.shape, 0)
        prev_last = sprev_ref[SUBLANES - 1:SUBLANES, :]
        prev_last = jnp.where(ib == nb - 1, jnp.zeros_like(prev_last), prev_last)
        s_sh = jnp.where(rows >= 1, pltpu.roll(sv, 1, 0), prev_last)
        lam_r, lam_i = lam[:, :hb], lam[:, hb:]
        sr_, si_ = s_sh[:, :hb], s_sh[:, hb:]
        dar = jnp.sum(lam_r * sr_ + lam_i * si_, axis=0, keepdims=True)
        dai = jnp.sum(lam_i * sr_ - lam_r * si_, axis=0, keepdims=True)
        contrib = jnp.concatenate([dar, dai], axis=1)

        @pl.when(ib == 0)
        def _():
            da_ref[...] = contrib

        @pl.when(ib != 0)
        def _():
            da_ref[...] += contrib

    blk = lambda j, i: (nb - 1 - i, j)
    prev_blk = lambda j, i: (jnp.maximum((nb - 1 - i) * per8 - 1, 0), j)
    return pl.pallas_call(
        body, name="s5_scan_bwd", grid=(nj, nb),
        in_specs=[pl.BlockSpec((t_blk, w2), blk), pl.BlockSpec((t_blk, w2), blk),
                  pl.BlockSpec((SUBLANES, w2), prev_blk), pl.BlockSpec((1, w2), lambda j, i: (0, j))],
        out_specs=[pl.BlockSpec((t_blk, w2), blk), pl.BlockSpec((1, w2), lambda j, i: (0, j))],
        out_shape=[jax.ShapeDtypeStruct(ds.shape, F32), jax.ShapeDtypeStruct((1, width), F32)],
        scratch_shapes=[pltpu.VMEM((SUBLANES, hb), F32)], compiler_params=_cparams(),
    )(ds, s, s, a)


def _rope128(x, cos, sa, sb):
    return x * cos + pltpu.roll(x, 96, 1) * sa + pltpu.roll(x, 32, 1) * sb


def _rope128_t(dy, cos, sa, sb):
    return dy * cos + pltpu.roll(dy * sa, 32, 1) + pltpu.roll(dy * sb, 96, 1)


ATT_BQ = 256


def _probs(qn, qp, kn, kp, r0, scale):
    s = lax.dot_general(qn, kn, (NT, ((), ())), preferred_element_type=F32)
    s = s + lax.dot_general(qp, kp, (NT, ((), ())), preferred_element_type=F32)
    s = s * scale
    row = r0 + lax.broadcasted_iota(jnp.int32, s.shape, 0)
    col = lax.broadcasted_iota(jnp.int32, s.shape, 1)
    s = jnp.where(col <= row, s, jnp.finfo(F32).min)
    m = jnp.max(s, axis=-1, keepdims=True)
    e = jnp.exp(s - m)
    return e / jnp.sum(e, axis=-1, keepdims=True)


def _attn_specs(seq):
    tab = pl.BlockSpec((seq, LANES), lambda h: (0, 0))
    return [pl.BlockSpec((None, seq, 256), lambda h: (h, 0, 0)), pl.BlockSpec((None, seq, 128), lambda h: (h, 0, 0)),
            pl.BlockSpec((None, seq, 128), lambda h: (h, 0, 1)), tab, tab, tab, tab]


def _attn_fwd(q_raw, kv, kpe, cos, sa, sb):
    nh, seq, _ = q_raw.shape
    bq = min(ATT_BQ, seq)
    scale = (QK_NOPE + QK_ROPE) ** -0.5

    def body(q_ref, kn_ref, v_ref, kp_ref, cos_ref, sa_ref, sb_ref, o_ref):
        for r0 in range(0, seq, bq):
            rows, kend = pl.ds(r0, bq), r0 + bq
            qn = q_ref[rows, :QK_NOPE].astype(BF16)
            qp = _rope128(q_ref[rows, QK_NOPE:], cos_ref[rows, :], sa_ref[rows, :], sb_ref[rows, :]).astype(BF16)
            p = _probs(qn, qp, kn_ref[:kend, :], kp_ref[:kend, :], r0, scale)
            o_ref[rows, :] = jnp.dot(p.astype(BF16), v_ref[:kend, :], preferred_element_type=F32)

    return pl.pallas_call(
        body, name="attn_fwd", grid=(nh,), in_specs=_attn_specs(seq),
        out_specs=pl.BlockSpec((seq, V_DIM), lambda h: (0, h)),
        out_shape=jax.ShapeDtypeStruct((seq, nh * V_DIM), F32), compiler_params=_cparams(),
    )(q_raw, kv, kv, kpe, cos, sa, sb)


def _attn_bwd(q_raw, kv, kpe, cos, sa, sb, do):
    nh, seq, _ = q_raw.shape
    bq = min(ATT_BQ, seq)
    scale = (QK_NOPE + QK_ROPE) ** -0.5

    def body(q_ref, kn_ref, v_ref, kp_ref, cos_ref, sa_ref, sb_ref, do_ref, dq_ref, dkv_ref, dkp_ref):
        dkv_ref[...] = jnp.zeros_like(dkv_ref)
        dkp_ref[...] = jnp.zeros_like(dkp_ref)
        for r0 in range(0, seq, bq):
            rows, kend = pl.ds(r0, bq), r0 + bq
            cos_b, sa_b, sb_b = cos_ref[rows, :], sa_ref[rows, :], sb_ref[rows, :]
            qn = q_ref[rows, :QK_NOPE].astype(BF16)
            qp = _rope128(q_ref[rows, QK_NOPE:], cos_b, sa_b, sb_b).astype(BF16)
            kn, v, kp = kn_ref[:kend, :], v_ref[:kend, :], kp_ref[:kend, :]
            p = _probs(qn, qp, kn, kp, r0, scale)
            dob = do_ref[rows, :].astype(BF16)
            dp = lax.dot_general(dob, v, (NT, ((), ())), preferred_element_type=F32)
            ds = p * (dp - jnp.sum(p * dp, axis=-1, keepdims=True)) * scale
            dsb = ds.astype(BF16)
            pb = p.astype(BF16)
            dq_ref[rows, :QK_NOPE] = jnp.dot(dsb, kn, preferred_element_type=F32).astype(dq_ref.dtype)
            dqp = jnp.dot(dsb, kp, preferred_element_type=F32)
            dq_ref[rows, QK_NOPE:] = _rope128_t(dqp, cos_b, sa_b, sb_b).astype(dq_ref.dtype)
            dkv_ref[:kend, :QK_NOPE] += lax.dot_general(dsb, qn, (TN, ((), ())), preferred_element_type=F32)
            dkv_ref[:kend, QK_NOPE:] += lax.dot_general(pb, dob, (TN, ((), ())), preferred_element_type=F32)
            dkp_ref[:kend, :] += lax.dot_general(dsb, qp, (TN, ((), ())), preferred_element_type=F32)

    sd = jax.ShapeDtypeStruct
    return pl.pallas_call(
        body, name="attn_bwd", grid=(nh,),
        in_specs=_attn_specs(seq) + [pl.BlockSpec((seq, V_DIM), lambda h: (0, h))],
        out_specs=[pl.BlockSpec((None, seq, 256), lambda h: (h, 0, 0)),
                   pl.BlockSpec((None, seq, 256), lambda h: (h, 0, 0)),
                   pl.BlockSpec((None, seq, 128), lambda h: (h, 0, 0))],
        out_shape=[sd((nh, seq, 256), BF16), sd((nh, seq, 256), F32), sd((nh, seq, 128), F32)],
        compiler_params=_cparams(),
    )(q_raw, kv, kv, kpe, cos, sa, sb, do)


def _conv3(a, w, b):
    rows = lax.broadcasted_iota(jnp.int32, a.shape, 0)
    a1 = jnp.where(rows >= 1, pltpu.roll(a, 1, 0), 0.0)
    a2 = jnp.where(rows >= 2, pltpu.roll(a, 2, 0), 0.0)
    return w[2:3] * a + w[1:2] * a1 + w[0:1] * a2 + b, a1, a2


def _conv_gate_fwd(a, cw, cb):
    half, _, seq, c = a.shape
    nc = c // LANES

    def fn(pair, wg, wv, bg, bv):
        gc, _, _ = _conv3(pair[0], wg, bg)
        vc, _, _ = _conv3(pair[1], wv, bv)
        return gc * jax.nn.sigmoid(gc) * vc

    def w_spec(off, r):
        return pl.BlockSpec((None, r, LANES), lambda k, j: (k + off, 0, j))

    return _blockwise(
        "conv_gate_fwd", fn, [a, cw, cw, cb, cb],
        [pl.BlockSpec((None, 2, seq, LANES), lambda k, j: (k, 0, 0, j)),
         w_spec(0, 3), w_spec(half, 3), w_spec(0, 1), w_spec(half, 1)],
        [((seq, half * c), BF16)], [pl.BlockSpec((seq, LANES), lambda k, j: (0, k * nc + j))],
        grid=(half, nc))[0]


def _conv_gate_bwd(a, cw, cb, dm):
    half, _, seq, c = a.shape
    nc = c // LANES

    def body(a_ref, wg_ref, wv_ref, bg_ref, bv_ref, dm_ref, da_ref, dw_ref, db_ref):
        dmv = dm_ref[...]
        rows = lax.broadcasted_iota(jnp.int32, dmv.shape, 0)
        ga, wg = a_ref[0], wg_ref[...]
        va, wv = a_ref[1], wv_ref[...]
        gc, g1, g2 = _conv3(ga, wg, bg_ref[...])
        vc, v1, v2 = _conv3(va, wv, bv_ref[...])
        sg = jax.nn.sigmoid(gc)
        dms = dmv * sg
        d_val = dms * gc
        d_gate = dms * vc * (1.0 + gc * (1.0 - sg))

        def back(r, dc, own, a1, a2, w):
            up1 = jnp.where(rows < seq - 1, pltpu.roll(dc, seq - 1, 0), 0.0)
            up2 = jnp.where(rows < seq - 2, pltpu.roll(dc, seq - 2, 0), 0.0)
            da_ref[r] = (w[2:3] * dc + w[1:2] * up1 + w[0:1] * up2).astype(da_ref.dtype)
            dw_ref[r, 0:1, :] = jnp.sum(dc * a2, axis=0, keepdims=True)
            dw_ref[r, 1:2, :] = jnp.sum(dc * a1, axis=0, keepdims=True)
            dw_ref[r, 2:3, :] = jnp.sum(dc * own, axis=0, keepdims=True)
            db_ref[r] = jnp.sum(dc, axis=0, keepdims=True)

        back(0, d_gate, ga, g1, g2, wg)
        back(1, d_val, va, v1, v2, wv)

    def w_spec(off, r):
        return pl.BlockSpec((None, r, LANES), lambda k, j: (k + off, 0, j))

    def pair_spec(r):
        return pl.BlockSpec((None, 2, r, LANES), lambda k, j: (k, 0, 0, j))

    sd = jax.ShapeDtypeStruct
    return pl.pallas_call(
        body, name="conv_gate_bwd", grid=(half, nc),
        in_specs=[pair_spec(seq), w_spec(0, 3), w_spec(half, 3), w_spec(0, 1), w_spec(half, 1),
                  pl.BlockSpec((seq, LANES), lambda k, j: (0, k * nc + j))],
        out_specs=[pair_spec(seq), pair_spec(3), pair_spec(1)],
        out_shape=[sd((half, 2, seq, c), BF16), sd((half, 2, 3, c), F32), sd((half, 2, 1, c), F32)],
        compiler_params=_cparams(),
    )(a, cw, cw, cb, cb, dm)


ROW_T = 256


def _local_step(x, positions, target, w, emit=lambda **grads: None):
    seq, d = x.shape
    t_row = min(ROW_T, seq)
    nrow = seq // t_row
    ssm_w = d // 2
    nj = ssm_w // LANES
    n_groups = ssm_w // SSM_GROUP
    nh = w["wuq"].shape[0]
    q_rank = w["wuq"].shape[1]
    kv_rank = w["wukv"].shape[1]
    ns = w["wup"].shape[0]
    c_ff = w["wup"].shape[2]
    in_pad = w["win"].shape[1]
    tm = min(512, seq)
    nm = seq // tm
    sw = 2 * STATE_BLOCK
    g1 = (nrow,)

    lr3 = w["lam_re"].reshape(n_groups, 1, SSM_STATE)
    li3 = w["lam_im"].reshape(n_groups, 1, SSM_STATE)
    ldt3 = w["log_dt"].reshape(n_groups, 1, 1)
    bt_re = jnp.swapaxes(w["b_re"].reshape(n_groups, SSM_STATE, SSM_GROUP), 1, 2)
    bt_im = jnp.swapaxes(w["b_im"].reshape(n_groups, SSM_STATE, SSM_GROUP), 1, 2)
    abar_re, abar_im, bbt_re, bbt_im = _s5_prep(lr3, li3, ldt3, bt_re, bt_im)
    eye = jnp.eye(GROUPS_PER_BLOCK, dtype=F32)

    def blockdiag_in(bb):
        t = bb.reshape(nj, GROUPS_PER_BLOCK, SSM_GROUP, SSM_STATE)
        return jnp.einsum("jghp,gk->jghkp", t, eye).reshape(nj, LANES, STATE_BLOCK)

    def blockdiag_in_t(dwb):
        t = dwb.reshape(nj, GROUPS_PER_BLOCK, SSM_GROUP, GROUPS_PER_BLOCK, SSM_STATE)
        return jnp.einsum("jghkp,gk->jghp", t, eye).reshape(n_groups, SSM_GROUP, SSM_STATE)

    def blockdiag_out(cc):
        t = cc.reshape(nj, GROUPS_PER_BLOCK, SSM_GROUP, SSM_STATE)
        return jnp.einsum("jghp,gk->jkpgh", t, eye).reshape(nj, STATE_BLOCK, LANES)

    def blockdiag_out_t(dwc):
        t = dwc.reshape(nj, GROUPS_PER_BLOCK, SSM_STATE, GROUPS_PER_BLOCK, SSM_GROUP)
        return jnp.einsum("jkpgh,gk->jghp", t, eye).reshape(n_groups, SSM_GROUP, SSM_STATE)

    c_re = w["c_re"].reshape(n_groups, SSM_GROUP, SSM_STATE)
    c_im = w["c_im"].reshape(n_groups, SSM_GROUP, SSM_STATE)
    wb = jnp.concatenate([blockdiag_in(bbt_re), blockdiag_in(bbt_im)], axis=2).astype(BF16)
    wc = jnp.concatenate([blockdiag_out(c_re), -blockdiag_out(c_im)], axis=1).astype(BF16)
    a_lay = jnp.concatenate([abar_re.reshape(nj, 1, STATE_BLOCK), abar_im.reshape(nj, 1, STATE_BLOCK)],
                            axis=1).reshape(1, nj * sw)

    attn_w = w["attn_norm"]
    hn = _blockwise("norm1", lambda xb, wv: _rms(xb, wv), [x, attn_w], [_row_spec(t_row, d), _full_spec((1, d))],
                    [((seq, d), BF16)], [_row_spec(t_row, d)], g1)[0]
    proj = _mm2d("proj", hn, w["win"], NN, F32, tn=640)

    s0 = _mm("ssm_bu", proj, wb, grid=(nm, nj, 1),
             a_spec=pl.BlockSpec((tm, LANES), lambda i, j, k: (i, j)),
             b_spec=pl.BlockSpec((None, LANES, sw), lambda i, j, k: (j, 0, 0)),
             o_spec=pl.BlockSpec((tm, sw), lambda i, j, k: (i, j)),
             out_shape=(seq, nj * sw), out_dtype=F32)
    s_all = _s5_scan(s0, a_lay)
    ylin = _mm("ssm_cy", s_all, wc, grid=(nm, nj, 1),
               a_spec=pl.BlockSpec((tm, sw), lambda i, j, k: (i, j)),
               b_spec=pl.BlockSpec((None, sw, LANES), lambda i, j, k: (j, 0, 0)),
               o_spec=pl.BlockSpec((tm, LANES), lambda i, j, k: (i, j)),
               out_shape=(seq, ssm_w), out_dtype=F32)
    u_spec = pl.BlockSpec((t_row, ssm_w), lambda i: (i, 0))

    def ypre_fn(yl, ub, dsk):
        yp = yl + dsk * ub
        return yp, jax.nn.gelu(yp)

    y_pre, yg = _blockwise("ssm_gelu", ypre_fn, [ylin, proj, w["ssm_d"]],
                           [_row_spec(t_row, ssm_w), u_spec, _full_spec((1, ssm_w))],
                           [((seq, ssm_w), F32), ((seq, ssm_w), BF16)],
                           [_row_spec(t_row, ssm_w)] * 2, g1)
    z = _mm2d("ssm_glu", yg, w["wglu"], NN, F32, res=w["b_glu"])
    y_ssm = _blockwise("ssm_gate", lambda yp, zb: jax.nn.gelu(yp) * jax.nn.sigmoid(zb), [y_pre, z],
                       [_row_spec(t_row, ssm_w)] * 2, [((seq, ssm_w), F32)], [_row_spec(t_row, ssm_w)], g1)[0]

    cq_off, ckv_off, kpe_off = ssm_w, ssm_w + q_rank, ssm_w + q_rank + kv_rank
    c_q = proj[:, cq_off:ckv_off]
    c_kv = proj[:, ckv_off:kpe_off]
    kpe_raw = proj[:, kpe_off:kpe_off + LANES]
    pos_b = jnp.broadcast_to(positions.astype(F32)[:, None], (seq, LANES))
    inv_freq = ROPE_THETA ** (-jnp.arange(0, QK_ROPE, 2, dtype=F32) / QK_ROPE)
    inv128 = jnp.tile(inv_freq, 4).reshape(1, LANES)

    def mla_prep_fn(cq, ckv, kp, pb, inv, wq, wkv):
        ang = pb * inv
        lane = lax.broadcasted_iota(jnp.int32, ang.shape, 1)
        cs, sn = jnp.cos(ang), jnp.sin(ang)
        cos = jnp.where(lane < QK_ROPE, cs, 0.0)
        sa = jnp.where(lane < QK_ROPE // 2, -sn, 0.0)
        sb = jnp.where(jnp.logical_and(lane >= QK_ROPE // 2, lane < QK_ROPE), sn, 0.0)
        return _rms(cq, wq), _rms(ckv, wkv), _rope128(kp, cos, sa, sb), cos, sa, sb

    qn, kvn, kpe, cos_t, sa_t, sb_t = _blockwise(
        "mla_prep", mla_prep_fn, [c_q, c_kv, kpe_raw, pos_b, inv128, w["q_norm"], w["kv_norm"]],
        [_row_spec(t_row, q_rank), _row_spec(t_row, kv_rank), _row_spec(t_row, LANES), _row_spec(t_row, LANES),
         _full_spec((1, LANES)), _full_spec((1, q_rank)), _full_spec((1, kv_rank))],
        [((seq, q_rank), BF16), ((seq, kv_rank), BF16), ((seq, LANES), BF16)] + [((seq, LANES), F32)] * 3,
        [_row_spec(t_row, q_rank), _row_spec(t_row, kv_rank)] + [_row_spec(t_row, LANES)] * 4, g1)

    def head_mm(name, act, wh, out_dtype):
        kdim, ndim = wh.shape[1], wh.shape[2]
        return _mm(name, act, wh, grid=(nh, nm, 1),
                   a_spec=pl.BlockSpec((tm, kdim), lambda h, i, k: (i, 0)),
                   b_spec=pl.BlockSpec((None, kdim, ndim), lambda h, i, k: (h, 0, 0)),
                   o_spec=pl.BlockSpec((None, tm, ndim), lambda h, i, k: (h, i, 0)),
                   out_shape=(nh, seq, ndim), out_dtype=out_dtype)

    q_raw = head_mm("mla_q", qn, w["wuq"], F32)
    kv = head_mm("mla_kv", kvn, w["wukv"], BF16)
    y_mla = _attn_fwd(q_raw, kv, kpe, cos_t, sa_t, sb_t)
    mla_w = nh * V_DIM

    def outnorm_fn(ys, ym, ws, wm):
        return jnp.concatenate([_rms(ys, ws), _rms(ym, wm)], axis=1)

    ycat = _blockwise("out_norm", outnorm_fn, [y_ssm, y_mla, w["son"], w["mon"]],
                      [_row_spec(t_row, ssm_w), _row_spec(t_row, mla_w), _full_spec((1, ssm_w)), _full_spec((1, mla_w))],
                      [((seq, d), BF16)], [_row_spec(t_row, d)], g1)[0]
    h1 = _mm2d("out_proj", ycat, w["wout"], NN, F32, res=x)

    hn2 = _blockwise("norm2", lambda hb, wv: _rms(hb, wv), [h1, w["ffn_norm"]],
                     [_row_spec(t_row, d), _full_spec((1, d))], [((seq, d), BF16)], [_row_spec(t_row, d)], g1)[0]
    tku = d
    half = ns // 2
    a_ff = _mm("ffn_up", hn2, w["wup"], grid=(ns, nm, d // tku),
               a_spec=pl.BlockSpec((tm, tku), lambda s, i, k: (i, k)),
               b_spec=pl.BlockSpec((None, tku, c_ff), lambda s, i, k: (s, k, 0)),
               o_spec=pl.BlockSpec((None, None, tm, c_ff), lambda s, i, k: (s % half, s // half, i, 0)),
               out_shape=(half, 2, seq, c_ff), out_dtype=F32)
    cb3 = w["conv_b"].reshape(ns, 1, c_ff)
    m_ff = _conv_gate_fwd(a_ff, w["conv_w"], cb3)
    d_ff = half * c_ff
    wdn = w["wdown"]
    tnd = _tile(d, 512)
    tmx, tnx = min(1024, seq), _tile(d, 1024)
    h2 = _mm2d("ffn_down", m_ff, wdn, NN, F32, tk=d_ff, res=h1)

    def loss_fn(hb, tb, wv):
        def f(hh, ww):
            err = _rms(hh, ww) - tb
            return 0.5 * jnp.sum(jnp.mean(err * err, axis=-1))

        lossv, (dh, dw) = jax.value_and_grad(f, argnums=(0, 1))(hb, wv)
        return dh, dh, jnp.full((1, LANES), lossv, F32), dw

    fin_w = w["final_norm"].reshape(1, d)
    dh2, dh2b, loss_acc, g_final = _blockwise(
        "loss_head", loss_fn, [h2, target, fin_w], [_row_spec(t_row, d), _row_spec(t_row, d), _full_spec((1, d))],
        [((seq, d), F32), ((seq, d), BF16), ((1, LANES), F32), ((1, d), F32)],
        [_row_spec(t_row, d), _row_spec(t_row, d), _full_spec((1, LANES)), _full_spec((1, d))], g1, n_acc=2)
    loss = loss_acc[0, 0]

    dm = _mm2d("ffn_down_dx", dh2b, wdn, NT, F32, tn=c_ff)
    tks = seq
    g_wdown = _mm2d("ffn_down_dw", m_ff, dh2b, TN, BF16, tm=c_ff)
    emit(wdown=g_wdown)
    da_ff, g_convw2, g_convb2 = _conv_gate_bwd(a_ff, w["conv_w"], cb3, dm)
    g_convw = jnp.swapaxes(g_convw2, 0, 1).reshape(ns, 3, c_ff)
    g_convb = jnp.swapaxes(g_convb2, 0, 1).reshape(ns, 1, c_ff)
    g_wup = _mm("ffn_up_dw", hn2, da_ff, grid=(ns, d // tnd, seq // tks), contract=TN,
                a_spec=pl.BlockSpec((tks, tnd), lambda s, j, k: (k, j)),
                b_spec=pl.BlockSpec((None, None, tks, c_ff), lambda s, j, k: (s % half, s // half, k, 0)),
                o_spec=pl.BlockSpec((None, tnd, c_ff), lambda s, j, k: (s, j, 0)),
                out_shape=(ns, d, c_ff), out_dtype=BF16)
    emit(wup=g_wup, conv_w=g_convw)
    dhn2 = _mm("ffn_up_dx", da_ff, w["wup"], grid=(seq // tmx, d // tnx, ns), contract=NT,
               a_spec=pl.BlockSpec((None, None, tmx, c_ff), lambda i, j, s: (s % half, s // half, i, 0)),
               b_spec=pl.BlockSpec((None, tnx, c_ff), lambda i, j, s: (s, j, 0)),
               o_spec=pl.BlockSpec((tmx, tnx), lambda i, j, s: (i, j)),
               out_shape=(seq, d), out_dtype=F32)

    def norm_bwd_fn(hb, dres, dn, wv):
        dx_, dw_ = _rms_bwd(hb, wv, dn)
        dtot = dres + dx_
        return dtot, dtot, dw_

    dh1, dh1b, g_ffn_norm = _blockwise(
        "norm2_bwd", norm_bwd_fn, [h1, dh2, dhn2, w["ffn_norm"]],
        [_row_spec(t_row, d)] * 3 + [_full_spec((1, d))],
        [((seq, d), F32), ((seq, d), BF16), ((1, d), F32)],
        [_row_spec(t_row, d), _row_spec(t_row, d), _full_spec((1, d))], g1, n_acc=1)

    dycat = _mm2d("out_proj_dx", dh1b, w["wout"], NT, F32)
    g_wout = _mm2d("out_proj_dw", ycat, dh1b, TN, BF16)

    def outnorm_bwd_fn(ys, ym, dyc, ws, wm):
        dys, dws = _rms_bwd(ys, ws, dyc[:, :ssm_w])
        dym, dwm = _rms_bwd(ym, wm, dyc[:, ssm_w:])
        return dys, dym, dws, dwm

    dy_ssm, dy_mla, g_son, g_mon = _blockwise(
        "out_norm_bwd", outnorm_bwd_fn, [y_ssm, y_mla, dycat, w["son"], w["mon"]],
        [_row_spec(t_row, ssm_w), _row_spec(t_row, mla_w), _row_spec(t_row, d), _full_spec((1, ssm_w)),
         _full_spec((1, mla_w))],
        [((seq, ssm_w), F32), ((seq, mla_w), F32), ((1, ssm_w), F32), ((1, mla_w), F32)],
        [_row_spec(t_row, ssm_w), _row_spec(t_row, mla_w), _full_spec((1, ssm_w)), _full_spec((1, mla_w))],
        g1, n_acc=2)

    def gate_bwd1_fn(dy, yp, zb):
        ygv = jax.nn.gelu(yp)
        sg = jax.nn.sigmoid(zb)
        dz = dy * ygv * sg * (1.0 - sg)
        return dz, jnp.sum(dz, axis=0, keepdims=True)

    dz, g_bglu = _blockwise("ssm_gate_bwd", gate_bwd1_fn, [dy_ssm, y_pre, z], [_row_spec(t_row, ssm_w)] * 3,
                            [((seq, ssm_w), BF16), ((1, ssm_w), F32)],
                            [_row_spec(t_row, ssm_w), _full_spec((1, ssm_w))], g1, n_acc=1)
    dyg2 = _mm2d("ssm_glu_dx", dz, w["wglu"], NT, F32)
    g_wglu = _mm2d("ssm_glu_dw", yg, dz, TN, BF16)

    def gelu_bwd_fn(dy, yp, zb, dg2, ub, dsk):
        dyg = dy * jax.nn.sigmoid(zb) + dg2
        _, vjp = jax.vjp(jax.nn.gelu, yp)
        dyp = vjp(dyg)[0]
        return dyp, dyp * dsk, jnp.sum(dyp * ub, axis=0, keepdims=True)

    dy_pre, du1, g_ssmd = _blockwise(
        "ssm_gelu_bwd", gelu_bwd_fn, [dy_ssm, y_pre, z, dyg2, proj, w["ssm_d"]],
        [_row_spec(t_row, ssm_w)] * 4 + [u_spec, _full_spec((1, ssm_w))],
        [((seq, ssm_w), BF16), ((seq, ssm_w), F32), ((1, ssm_w), F32)],
        [_row_spec(t_row, ssm_w), _row_spec(t_row, ssm_w), _full_spec((1, ssm_w))], g1, n_acc=1)
    dq_raw, dkv, dkp_h = _attn_bwd(q_raw, kv, kpe, cos_t, sa_t, sb_t, dy_mla)

    def head_mm_dx(name, dact, wh):
        kdim, ndim = wh.shape[1], wh.shape[2]
        return _mm(name, dact, wh, grid=(nm, 1, nh), contract=NT,
                   a_spec=pl.BlockSpec((None, tm, ndim), lambda i, j, h: (h, i, 0)),
                   b_spec=pl.BlockSpec((None, kdim, ndim), lambda i, j, h: (h, 0, 0)),
                   o_spec=pl.BlockSpec((tm, kdim), lambda i, j, h: (i, 0)),
                   out_shape=(seq, kdim), out_dtype=F32)

    def head_mm_dw(name, act, dact):
        kdim, ndim = act.shape[1], dact.shape[2]
        return _mm(name, act, dact, grid=(nh, 1, seq // tks), contract=TN,
                   a_spec=pl.BlockSpec((tks, kdim), lambda h, j, k: (k, 0)),
                   b_spec=pl.BlockSpec((None, tks, ndim), lambda h, j, k: (h, k, 0)),
                   o_spec=pl.BlockSpec((None, kdim, ndim), lambda h, j, k: (h, 0, 0)),
                   out_shape=(nh, kdim, ndim), out_dtype=BF16)

    g_wuq = head_mm_dw("mla_q_dw", qn, dq_raw)
    g_wukv = head_mm_dw("mla_kv_dw", kvn, dkv)
    emit(wout=g_wout, wuq=g_wuq, wukv=g_wukv, wglu=g_wglu)
    dqn = head_mm_dx("mla_q_dx", dq_raw, w["wuq"])
    dkvn = head_mm_dx("mla_kv_dx", dkv, w["wukv"])

    ds_all = _mm("ssm_cy_dx", dy_pre, wc, grid=(nm, nj, 1), contract=NT,
                 a_spec=pl.BlockSpec((tm, LANES), lambda i, j, k: (i, j)),
                 b_spec=pl.BlockSpec((None, sw, LANES), lambda i, j, k: (j, 0, 0)),
                 o_spec=pl.BlockSpec((tm, sw), lambda i, j, k: (i, j)),
                 out_shape=(seq, nj * sw), out_dtype=F32)
    dwc = _mm("ssm_cy_dw", s_all, dy_pre, grid=(nj, 1, seq // tks), contract=TN,
              a_spec=pl.BlockSpec((tks, sw), lambda j, n, k: (k, j)),
              b_spec=pl.BlockSpec((tks, LANES), lambda j, n, k: (k, j)),
              o_spec=pl.BlockSpec((None, sw, LANES), lambda j, n, k: (j, 0, 0)),
              out_shape=(nj, sw, LANES), out_dtype=F32)
    lam, da_lay = _s5_scan_bwd(ds_all, s_all, a_lay)
    du = _mm("ssm_bu_dx", lam, wb, grid=(nm, nj, 1), contract=NT,
             a_spec=pl.BlockSpec((tm, sw), lambda i, j, k: (i, j)),
             b_spec=pl.BlockSpec((None, LANES, sw), lambda i, j, k: (j, 0, 0)),
             o_spec=pl.BlockSpec((tm, LANES), lambda i, j, k: (i, j)),
             out_shape=(seq, ssm_w), out_dtype=BF16,
             res=du1, res_spec=pl.BlockSpec((tm, LANES), lambda i, j, k: (i, j)))
    dwb = _mm("ssm_bu_dw", proj, lam, grid=(nj, 1, seq // tks), contract=TN,
              a_spec=pl.BlockSpec((tks, LANES), lambda j, n, k: (k, j)),
              b_spec=pl.BlockSpec((tks, sw), lambda j, n, k: (k, j)),
              o_spec=pl.BlockSpec((None, LANES, sw), lambda j, n, k: (j, 0, 0)),
              out_shape=(nj, LANES, sw), out_dtype=F32)
    g_c_re = blockdiag_out_t(dwc[:, :STATE_BLOCK, :])
    g_c_im = -blockdiag_out_t(dwc[:, STATE_BLOCK:, :])
    dbbt_re = blockdiag_in_t(dwb[:, :, :STATE_BLOCK])
    dbbt_im = blockdiag_in_t(dwb[:, :, STATE_BLOCK:])
    da3 = da_lay.reshape(nj, 2, STATE_BLOCK)
    dabar_re = da3[:, 0, :].reshape(n_groups, 1, SSM_STATE)
    dabar_im = da3[:, 1, :].reshape(n_groups, 1, SSM_STATE)
    g_lr3, g_li3, g_ldt3, g_bt_re, g_bt_im = _s5_prep_bwd(lr3, li3, ldt3, bt_re, bt_im,
                                                           dabar_re, dabar_im, dbbt_re, dbbt_im)

    def mla_prep_bwd_fn(cq, ckv, dqn_b, dkvn_b, dkp_b, cos, sa, sb, wq, wkv):
        dcq, dwq = _rms_bwd(cq, wq, dqn_b)
        dckv, dwkv = _rms_bwd(ckv, wkv, dkvn_b)
        dkp_sum = dkp_b[0]
        for h in range(1, nh):
            dkp_sum = dkp_sum + dkp_b[h]
        return dcq, dckv, _rope128_t(dkp_sum, cos, sa, sb), dwq, dwkv

    dc_q, dc_kv, dkpe_raw, g_qnorm, g_kvnorm = _blockwise(
        "mla_prep_bwd", mla_prep_bwd_fn, [c_q, c_kv, dqn, dkvn, dkp_h, cos_t, sa_t, sb_t, w["q_norm"], w["kv_norm"]],
        [_row_spec(t_row, q_rank), _row_spec(t_row, kv_rank), _row_spec(t_row, q_rank), _row_spec(t_row, kv_rank),
         pl.BlockSpec((nh, t_row, LANES), lambda i: (0, i, 0))] + [_row_spec(t_row, LANES)] * 3
        + [_full_spec((1, q_rank)), _full_spec((1, kv_rank))],
        [((seq, q_rank), BF16), ((seq, kv_rank), BF16), ((seq, LANES), BF16), ((1, q_rank), F32), ((1, kv_rank), F32)],
        [_row_spec(t_row, q_rank), _row_spec(t_row, kv_rank), _row_spec(t_row, LANES), _full_spec((1, q_rank)),
         _full_spec((1, kv_rank))], g1, n_acc=2)

    dproj = jnp.concatenate([du, dc_q, dc_kv, dkpe_raw], axis=1)
    g_win = _mm2d("proj_dw", hn, dproj, TN, BF16, tn=640)
    emit(win=g_win)
    dhn = _mm2d("proj_dx", dproj, w["win"], NT, F32)

    def norm1_bwd_fn(xb, dres, dn, wv):
        dx_, dw_ = _rms_bwd(xb, wv, dn)
        return dres + dx_, dw_

    grad_x, g_attn_norm = _blockwise(
        "norm1_bwd", norm1_bwd_fn, [x, dh1, dhn, attn_w], [_row_spec(t_row, d)] * 3 + [_full_spec((1, d))],
        [((seq, d), F32), ((1, d), F32)], [_row_spec(t_row, d), _full_spec((1, d))], g1, n_acc=1)

    grads = dict(
        attn_norm=g_attn_norm, win=g_win, lam_re=g_lr3, lam_im=g_li3, log_dt=g_ldt3,
        b_re=jnp.swapaxes(g_bt_re, 1, 2), b_im=jnp.swapaxes(g_bt_im, 1, 2), c_re=g_c_re, c_im=g_c_im,
        ssm_d=g_ssmd, wglu=g_wglu, b_glu=g_bglu, q_norm=g_qnorm, wuq=g_wuq, kv_norm=g_kvnorm, wukv=g_wukv,
        son=g_son, mon=g_mon, wout=g_wout, ffn_norm=g_ffn_norm, wup=g_wup, conv_w=g_convw, conv_b=g_convb,
        wdown=g_wdown, final_norm=g_final)
    return loss, grad_x, grads


def _mesh_pos():
    return lax.axis_index("x"), lax.axis_index("y"), lax.axis_index("c")


def _handshake_all():
    x, y, c = _mesh_pos()
    barrier = pltpu.get_barrier_semaphore()
    for k in range(1, N_DEV):
        peer = (1 - x if k & 4 else x, 1 - y if k & 2 else y, 1 - c if k & 1 else c)
        pl.semaphore_signal(barrier, inc=1, device_id=peer, device_id_type=MESH)
    pl.semaphore_wait(barrier, N_DEV - 1)


def _comm_call(name, body, n, out_shape, ins, collective_id, after=None):
    sems = [pltpu.SemaphoreType.DMA((7 * n,)), pltpu.SemaphoreType.DMA((7 * n,)), pltpu.SemaphoreType.DMA((n,))]
    if collective_id is None:
        any_spec = pl.BlockSpec(memory_space=pl.ANY)
        return pl.pallas_call(body, name=name, out_shape=out_shape, in_specs=[any_spec] * n,
                              out_specs=[any_spec] * n, scratch_shapes=sems)(*ins)
    seq_body = body
    if after is not None:
        ins = list(ins) + [after]

        def seq_body(*refs):
            body(*refs[:n], *refs[n + 1:])

    return pl.kernel(seq_body, name=name, out_type=out_shape,
                     mesh=plsc.ScalarSubcoreMesh(axis_name="seq", num_cores=1), scratch_types=sems,
                     compiler_params=pltpu.CompilerParams(collective_id=collective_id))(*ins)


def _all_gather(name, xs, collective_id=None, after=None):
    n = len(xs)

    def body(*refs):
        x_refs, o_refs = refs[:n], refs[n:2 * n]
        send_sems, recv_sems, local_sems = refs[2 * n:]
        if collective_id is not None:
            _handshake_all()
        x, y, c = _mesh_pos()
        me, sibling = (x, y, c), (x, y, 1 - c)
        chips = [(1 - x, y), (x, 1 - y), (1 - x, 1 - y)]

        def slot(o_ref, px, py, pc):
            return o_ref.at[4 * px + 2 * py + pc]

        def copy(t, k, block, to, src=None):
            dst = slot(o_refs[t], *block)
            return pltpu.make_async_remote_copy(
                src_ref=dst if src is None else src, dst_ref=dst,
                send_sem=send_sems.at[7 * t + k], recv_sem=recv_sems.at[7 * t + k],
                device_id=to, device_id_type=MESH)

        started = []
        for t in range(n):
            mine = pltpu.make_async_copy(x_refs[t], slot(o_refs[t], *me), local_sems.at[t])
            mine.start()
            started.append(mine)
        first = []
        for t in range(n):
            first.append(copy(t, 0, me, sibling, src=x_refs[t]))
            first += [copy(t, 1 + j, me, (*chip, c), src=x_refs[t]) for j, chip in enumerate(chips)]
        for cp in first:
            cp.start()
        passed = []
        for j, chip in enumerate(chips):
            for t in range(n):
                copy(t, 1 + j, (*chip, c), me).wait_recv()
                fwd = copy(t, 4 + j, (*chip, c), sibling)
                fwd.start()
                passed.append(fwd)
        for t in range(n):
            copy(t, 0, sibling, me).wait_recv()
            for j, chip in enumerate(chips):
                copy(t, 4 + j, (*chip, 1 - c), me).wait_recv()
        for cp in first + passed:
            cp.wait_send()
        for mine in started:
            mine.wait()

    out_shape = [jax.ShapeDtypeStruct((N_DEV,) + v.shape, v.dtype) for v in xs]
    return _comm_call(name, body, n, out_shape, xs, collective_id, after)


def _exchange_partials(name, gs, collective_id=None, after=None):
    n = len(gs)

    def body(*refs):
        g_refs, o_refs = refs[:n], refs[n:2 * n]
        send_sems, recv_sems, local_sems = refs[2 * n:]
        if collective_id is not None:
            _handshake_all()
        x, y, c = _mesh_pos()
        me_idx = 4 * x + 2 * y + c
        copies = []
        for t in range(n):
            mine = pltpu.make_async_copy(g_refs[t].at[me_idx], o_refs[t].at[me_idx], local_sems.at[t])
            mine.start()
            copies.append(mine)
        remote = []
        for k in range(1, N_DEV):
            px = 1 - x if k & 4 else x
            py = 1 - y if k & 2 else y
            pc = 1 - c if k & 1 else c
            p_idx = 4 * px + 2 * py + pc
            for t in range(n):
                cp = pltpu.make_async_remote_copy(
                    src_ref=g_refs[t].at[p_idx], dst_ref=o_refs[t].at[me_idx],
                    send_sem=send_sems.at[7 * t + k - 1], recv_sem=recv_sems.at[7 * t + k - 1],
                    device_id=(px, py, pc), device_id_type=MESH)
                cp.start()
                landing = pltpu.make_async_remote_copy(
                    src_ref=g_refs[t].at[p_idx], dst_ref=o_refs[t].at[p_idx],
                    send_sem=send_sems.at[7 * t + k - 1], recv_sem=recv_sems.at[7 * t + k - 1],
                    device_id=(px, py, pc), device_id_type=MESH)
                remote.append((cp, landing))
        for cp, landing in remote:
            landing.wait_recv()
        for cp, landing in remote:
            cp.wait_send()
        for mine in copies:
            mine.wait()

    out_shape = [jax.ShapeDtypeStruct(v.shape, v.dtype) for v in gs]
    return _comm_call(name, body, n, out_shape, gs, collective_id, after)


ADAM_BLOCK_ELEMS = 128 * 1024


def _adamw_sum(name, parts, wv, mv, vv):
    npart, r, c = parts.shape
    tr = r
    if r * c > ADAM_BLOCK_ELEMS and r % SUBLANES == 0:
        tr = SUBLANES
        while r % (tr * 2) == 0 and tr * 2 * c <= ADAM_BLOCK_ELEMS:
            tr *= 2
    bc1 = 1.0 - ADAM_B1 ** ADAM_STEP
    bc2 = 1.0 - ADAM_B2 ** ADAM_STEP

    def fn(pb, wb_, mb, vb):
        g = pb[0].astype(F32)
        for j in range(1, npart):
            g = g + pb[j].astype(F32)
        m_new = ADAM_B1 * mb + (1.0 - ADAM_B1) * g
        v_new = ADAM_B2 * vb + (1.0 - ADAM_B2) * (g * g)
        m_hat = m_new / bc1
        v_hat = v_new / bc2
        delta = -ADAM_LR * (m_hat / (jnp.sqrt(v_hat) + ADAM_EPS) + ADAM_WD * wb_)
        return g, delta, m_new, v_new

    row = pl.BlockSpec((tr, c), lambda i: (i, 0))
    return _blockwise(name, fn, [parts, wv, mv, vv],
                      [pl.BlockSpec((npart, tr, c), lambda i: (0, i, 0)), row, row, row],
                      [((r, c), F32)] * 4, [row] * 4, (r // tr,))


_SMALL = ["attn_norm", "lam_re", "lam_im", "log_dt", "b_re", "b_im", "c_re", "c_im", "ssm_d", "b_glu",
          "q_norm", "kv_norm", "son", "mon", "ffn_norm", "conv_b", "final_norm"]
_BIG = ["win", "wglu", "wuq", "wukv", "wout", "wup", "wdown", "conv_w"]
_ORDER = ["attn_norm", "win", "lam_re", "lam_im", "log_dt", "b_re", "b_im", "c_re", "c_im", "ssm_d", "wglu",
          "b_glu", "q_norm", "wuq", "kv_norm", "wukv", "son", "mon", "wout", "ffn_norm", "wup", "conv_w",
          "conv_b", "wdown", "final_norm"]


def _pack(arrs):
    flat = jnp.concatenate([a.reshape(-1).astype(F32) for a in arrs])
    pad = (-flat.shape[0]) % (LANES * LANES)
    return jnp.pad(flat, (0, pad)).reshape(-1, LANES)


def _unpack(packed, shapes):
    flat = packed.reshape(-1)
    out, off = [], 0
    for s in shapes:
        n = math.prod(s)
        out.append(flat[off:off + n].reshape(s))
        off += n
    return out


def kernel(x, positions, attn_norm_w, w_in, ssm_lambda_re, ssm_lambda_im, ssm_log_dt, ssm_b_re, ssm_b_im, ssm_c_re, ssm_c_im, ssm_d, ssm_w_glu, ssm_b_glu, mla_q_norm_w, mla_w_uq, mla_kv_norm_w, mla_w_ukv, ssm_out_norm_w, mla_out_norm_w, w_out, ffn_norm_w, ffn_w_up, ffn_conv_w, ffn_conv_b, ffn_w_down, final_norm_w, loss_target, m_attn_norm_w, m_w_in, m_ssm_lambda_re, m_ssm_lambda_im, m_ssm_log_dt, m_ssm_b_re, m_ssm_b_im, m_ssm_c_re, m_ssm_c_im, m_ssm_d, m_ssm_w_glu, m_ssm_b_glu, m_mla_q_norm_w, m_mla_w_uq, m_mla_kv_norm_w, m_mla_w_ukv, m_ssm_out_norm_w, m_mla_out_norm_w, m_w_out, m_ffn_norm_w, m_ffn_w_up, m_ffn_conv_w, m_ffn_conv_b, m_ffn_w_down, m_final_norm_w, v_attn_norm_w, v_w_in, v_ssm_lambda_re, v_ssm_lambda_im, v_ssm_log_dt, v_ssm_b_re, v_ssm_b_im, v_ssm_c_re, v_ssm_c_im, v_ssm_d, v_ssm_w_glu, v_ssm_b_glu, v_mla_q_norm_w, v_mla_w_uq, v_mla_kv_norm_w, v_mla_w_ukv, v_ssm_out_norm_w, v_mla_out_norm_w, v_w_out, v_ffn_norm_w, v_ffn_w_up, v_ffn_conv_w, v_ffn_conv_b, v_ffn_w_down, v_final_norm_w):
    wts = dict(attn_norm=attn_norm_w, win=w_in, lam_re=ssm_lambda_re, lam_im=ssm_lambda_im, log_dt=ssm_log_dt,
               b_re=ssm_b_re, b_im=ssm_b_im, c_re=ssm_c_re, c_im=ssm_c_im, ssm_d=ssm_d, wglu=ssm_w_glu,
               b_glu=ssm_b_glu, q_norm=mla_q_norm_w, wuq=mla_w_uq, kv_norm=mla_kv_norm_w, wukv=mla_w_ukv,
               son=ssm_out_norm_w, mon=mla_out_norm_w, wout=w_out, ffn_norm=ffn_norm_w, wup=ffn_w_up,
               conv_w=ffn_conv_w, conv_b=ffn_conv_b, wdown=ffn_w_down, final_norm=final_norm_w)
    moms = dict(zip(_ORDER, [m_attn_norm_w, m_w_in, m_ssm_lambda_re, m_ssm_lambda_im, m_ssm_log_dt, m_ssm_b_re,
                             m_ssm_b_im, m_ssm_c_re, m_ssm_c_im, m_ssm_d, m_ssm_w_glu, m_ssm_b_glu, m_mla_q_norm_w,
                             m_mla_w_uq, m_mla_kv_norm_w, m_mla_w_ukv, m_ssm_out_norm_w, m_mla_out_norm_w, m_w_out,
                             m_ffn_norm_w, m_ffn_w_up, m_ffn_conv_w, m_ffn_conv_b, m_ffn_w_down, m_final_norm_w]))
    vels = dict(zip(_ORDER, [v_attn_norm_w, v_w_in, v_ssm_lambda_re, v_ssm_lambda_im, v_ssm_log_dt, v_ssm_b_re,
                             v_ssm_b_im, v_ssm_c_re, v_ssm_c_im, v_ssm_d, v_ssm_w_glu, v_ssm_b_glu, v_mla_q_norm_w,
                             v_mla_w_uq, v_mla_kv_norm_w, v_mla_w_ukv, v_ssm_out_norm_w, v_mla_out_norm_w, v_w_out,
                             v_ffn_norm_w, v_ffn_w_up, v_ffn_conv_w, v_ffn_conv_b, v_ffn_w_down, v_final_norm_w]))
    seq, d = x.shape[1], x.shape[2]
    in_width = w_in.shape[2]
    in_pad = -(-in_width // LANES) * LANES
    q_cols = mla_w_uq.shape[2]
    q_pad = 2 * LANES

    (win_g,) = _all_gather("gather_w_in", [jnp.pad(w_in[0], ((0, 0), (0, in_pad - in_width))).astype(BF16)])
    wglu_g, wuq_g, wukv_g, wout_g, convw_g = _all_gather(
        "gather_mix", [ssm_w_glu[0].astype(BF16), jnp.pad(mla_w_uq[0], ((0, 0), (0, q_pad - q_cols))).astype(BF16),
                       mla_w_ukv[0].astype(BF16), w_out[0].astype(BF16), ffn_conv_w[0]], collective_id=0)
    (wup_g,) = _all_gather("gather_ffn_up", [ffn_w_up[0].astype(BF16)], collective_id=1)
    (wdown_g,) = _all_gather("gather_ffn_down", [ffn_w_down[0].astype(BF16)], collective_id=2)
    ns = N_DEV
    c_ff = wup_g.shape[2]
    w = dict(
        attn_norm=attn_norm_w, win=win_g.reshape(d, in_pad), lam_re=ssm_lambda_re, lam_im=ssm_lambda_im,
        log_dt=ssm_log_dt, b_re=ssm_b_re, b_im=ssm_b_im, c_re=ssm_c_re, c_im=ssm_c_im, ssm_d=ssm_d,
        wglu=wglu_g.reshape(d // 2, d // 2), b_glu=ssm_b_glu, q_norm=mla_q_norm_w, wuq=wuq_g,
        kv_norm=mla_kv_norm_w, wukv=wukv_g, son=ssm_out_norm_w, mon=mla_out_norm_w, wout=wout_g.reshape(d, d),
        ffn_norm=ffn_norm_w, wup=wup_g, conv_w=convw_g, conv_b=ffn_conv_b,
        wdown=wdown_g.reshape(ns // 2 * c_ff, d), final_norm=final_norm_w)

    shard_layout = dict(
        win=lambda a: a[:, :in_width].reshape(N_DEV, d // N_DEV, in_width),
        wglu=lambda a: a.reshape(N_DEV, d // 2 // N_DEV, d // 2),
        wuq=lambda a: a[:, :, :q_cols], wukv=lambda a: a, wout=lambda a: a.reshape(N_DEV, d // N_DEV, d),
        wup=lambda a: a, wdown=lambda a: a.reshape(N_DEV, c_ff // 2, d), conv_w=lambda a: a)
    recv = {}
    next_id = [3]

    last = [None]

    def exchange(**grads):
        names = list(grads)
        got = _exchange_partials("exchange_" + "_".join(names), [shard_layout[k](grads[k]) for k in names],
                                 collective_id=next_id[0], after=last[0])
        next_id[0] += 1
        last[0] = got[-1]
        recv.update(zip(names, got))

    loss_part, grad_x, g = _local_step(x[0], positions[0], loss_target[0], w, emit=exchange)
    loss = lax.psum(loss_part, ("x", "y", "c"))
    small_shapes = [wts[k].shape for k in _SMALL]
    small_part = _pack([g[k] for k in _SMALL])
    small_all = _all_gather("gather_small_grads", [small_part], collective_id=next_id[0], after=last[0])[0]

    out = {}
    for k in _BIG:
        shp = wts[k].shape
        r, c = shp[-2], shp[-1]
        res = _adamw_sum("adamw_" + k, recv[k].reshape(N_DEV, r, c), wts[k].reshape(r, c),
                         moms[k].reshape(r, c), vels[k].reshape(r, c))
        out[k] = [a.reshape(shp) for a in res]
    sw_ = _pack([wts[k] for k in _SMALL])
    sm_ = _pack([moms[k] for k in _SMALL])
    sv_ = _pack([vels[k] for k in _SMALL])
    res = _adamw_sum("adamw_small", small_all, sw_, sm_, sv_)
    unpacked = [_unpack(a, small_shapes) for a in res]
    for i, k in enumerate(_SMALL):
        out[k] = [u[i] for u in unpacked]

    grad_x = grad_x.reshape(x.shape)
    return (loss, grad_x, *[out[k][0] for k in _ORDER], *[out[k][1] for k in _ORDER],
            *[out[k][2] for k in _ORDER], *[out[k][3] for k in _ORDER])
```

```python
import functools
import math

import jax
import jax.numpy as jnp
from jax import lax
from jax.experimental import pallas as pl
from jax.experimental.pallas import tpu as pltpu
from jax.experimental.pallas import tpu_sc as plsc

F32 = jnp.float32
BF16 = jnp.bfloat16
MESH = pl.DeviceIdType.MESH

N_DEV = 8
LANES = 128
SUBLANES = 8
VMEM_LIMIT = 48 * 1024 * 1024

SSM_GROUP = 16
SSM_STATE = 64
GROUPS_PER_BLOCK = LANES // SSM_GROUP
STATE_BLOCK = GROUPS_PER_BLOCK * SSM_STATE
QK_NOPE = 128
QK_ROPE = 64
V_DIM = 128
ROPE_THETA = 10000.0
RMS_EPS = 1e-6

ADAM_LR = 0.001
ADAM_B1 = 0.9
ADAM_B2 = 0.999
ADAM_EPS = 1e-08
ADAM_WD = 0.01
ADAM_STEP = 10

NN = ((1,), (0,))
NT = ((1,), (1,))
TN = ((0,), (0,))


def _cparams():
    return pltpu.CompilerParams(vmem_limit_bytes=VMEM_LIMIT)


def _tile(n, want):
    if n <= want:
        return n
    t = (want // LANES) * LANES
    while t >= LANES:
        if n % t == 0:
            return t
        t -= LANES
    return n


def _mm(name, a, b, *, grid, a_spec, b_spec, o_spec, out_shape, out_dtype, contract=NN,
        res=None, res_spec=None):
    nk = grid[-1]
    kaxis = len(grid) - 1
    acc_shape = tuple(d for d in o_spec.block_shape if d is not None)

    def body(*refs):
        a_ref, b_ref = refs[:2]
        r_ref = None if res is None else refs[2]
        o_ref = refs[2 if res is None else 3]
        part = lax.dot_general(a_ref[...].astype(BF16), b_ref[...].astype(BF16),
                               (contract, ((), ())), preferred_element_type=F32)
        if nk == 1:
            if r_ref is not None:
                part = part + r_ref[...].astype(F32)
            o_ref[...] = part.astype(o_ref.dtype)
            return
        acc = refs[-1]
        k = pl.program_id(kaxis)

        @pl.when(k == 0)
        def _():
            acc[...] = part

        @pl.when(k != 0)
        def _():
            acc[...] += part

        @pl.when(k == nk - 1)
        def _():
            r = acc[...]
            if r_ref is not None:
                r = r + r_ref[...].astype(F32)
            o_ref[...] = r.astype(o_ref.dtype)

    ins = [a, b] + ([] if res is None else [res])
    in_specs = [a_spec, b_spec] + ([] if res is None else [res_spec])
    return pl.pallas_call(
        body, name=name, grid=grid, in_specs=in_specs, out_specs=o_spec,
        out_shape=jax.ShapeDtypeStruct(out_shape, out_dtype),
        scratch_shapes=[pltpu.VMEM(acc_shape, F32)] if nk > 1 else [], compiler_params=_cparams(),
    )(*ins)


def _mm2d(name, a, b, contract, out_dtype, tm=512, tn=512, tk=2048, res=None):
    if contract == NN:
        (m, kk), n = a.shape, b.shape[1]
    elif contract == NT:
        (m, kk), n = a.shape, b.shape[0]
    else:
        (kk, m), n = a.shape, b.shape[1]
    tm, tn, tk = _tile(m, tm), _tile(n, tn), _tile(kk, tk)
    grid = (m // tm, n // tn, kk // tk)
    if contract == TN:
        a_spec = pl.BlockSpec((tk, tm), lambda i, j, k: (k, i))
    else:
        a_spec = pl.BlockSpec((tm, tk), lambda i, j, k: (i, k))
    if contract == NT:
        b_spec = pl.BlockSpec((tn, tk), lambda i, j, k: (j, k))
    else:
        b_spec = pl.BlockSpec((tk, tn), lambda i, j, k: (k, j))
    o_spec = pl.BlockSpec((tm, tn), lambda i, j, k: (i, j))
    res_spec = None
    if res is not None:
        if res.shape[0] == 1:
            res_spec = pl.BlockSpec((1, tn), lambda i, j, k: (0, j))
        else:
            res_spec = pl.BlockSpec((tm, tn), lambda i, j, k: (i, j))
    return _mm(name, a, b, grid=grid, a_spec=a_spec, b_spec=b_spec, o_spec=o_spec,
               out_shape=(m, n), out_dtype=out_dtype, contract=contract, res=res, res_spec=res_spec)


def _blockwise(name, fn, ins, in_specs, outs, out_specs, grid, n_acc=0, acc_all=True):
    n_in, n_out = len(ins), len(outs)
    n_plain = n_out - n_acc

    def body(*refs):
        vals = fn(*[r[...] for r in refs[:n_in]])
        if not isinstance(vals, (tuple, list)):
            vals = (vals,)
        o_refs = refs[n_in:n_in + n_out]
        for r, v in zip(o_refs[:n_plain], vals[:n_plain]):
            r[...] = v.astype(r.dtype)
        if n_acc:
            if acc_all:
                first = functools.reduce(jnp.logical_and, [pl.program_id(d) == 0 for d in range(len(grid))])
            else:
                first = pl.program_id(len(grid) - 1) == 0

            @pl.when(first)
            def _():
                for r, v in zip(o_refs[n_plain:], vals[n_plain:]):
                    r[...] = v.astype(r.dtype)

            @pl.when(jnp.logical_not(first))
            def _():
                for r, v in zip(o_refs[n_plain:], vals[n_plain:]):
                    r[...] += v.astype(r.dtype)

    return pl.pallas_call(
        body, name=name, grid=grid, in_specs=in_specs, out_specs=out_specs,
        out_shape=[jax.ShapeDtypeStruct(s, d) for s, d in outs], compiler_params=_cparams(),
    )(*ins)


def _row_spec(t, c):
    return pl.BlockSpec((t, c), lambda i: (i, 0))


def _full_spec(shape):
    nd = len(shape)
    return pl.BlockSpec(tuple(shape), lambda *g: (0,) * nd)


def _rms(xf, w):
    return xf * lax.rsqrt(jnp.mean(xf * xf, axis=-1, keepdims=True) + RMS_EPS) * w


def _rms_bwd(xf, w, dy):
    _, vjp = jax.vjp(_rms, xf, w)
    return vjp(dy)


def _s5_disc(lr, li, ldt, bre, bim):
    dt = jnp.exp(ldt)
    mag = jnp.exp(lr * dt)
    ar = mag * jnp.cos(li * dt)
    ai = mag * jnp.sin(li * dt)
    nr, ni = ar - 1.0, ai
    den = lr * lr + li * li
    zr = (nr * lr + ni * li) / den
    zi = (ni * lr - nr * li) / den
    return ar, ai, zr * bre - zi * bim, zr * bim + zi * bre


def _s5_prep(lr, li, ldt, bre, bim):
    def body(lr_r, li_r, ldt_r, bre_r, bim_r, ar_r, ai_r, br_r, bi_r):
        ar, ai, br, bi = _s5_disc(lr_r[...], li_r[...], ldt_r[...], bre_r[...], bim_r[...])
        ar_r[...] = ar
        ai_r[...] = ai
        br_r[...] = br
        bi_r[...] = bi

    sd = jax.ShapeDtypeStruct
    return pl.pallas_call(
        body, name="s5_prep",
        out_shape=[sd(lr.shape, F32), sd(lr.shape, F32), sd(bre.shape, F32), sd(bre.shape, F32)],
        compiler_params=_cparams(),
    )(lr, li, ldt, bre, bim)


def _s5_prep_bwd(lr, li, ldt, bre, bim, dar, dai, dbr, dbi):
    def body(lr_r, li_r, ldt_r, bre_r, bim_r, dar_r, dai_r, dbr_r, dbi_r, o0, o1, o2, o3, o4):
        _, vjp = jax.vjp(_s5_disc, lr_r[...], li_r[...], ldt_r[...], bre_r[...], bim_r[...])
        g = vjp((dar_r[...], dai_r[...], dbr_r[...], dbi_r[...]))
        for o, v in zip((o0, o1, o2, o3, o4), g):
            o[...] = v

    sd = jax.ShapeDtypeStruct
    return pl.pallas_call(
        body, name="s5_prep_bwd",
        out_shape=[sd(lr.shape, F32), sd(li.shape, F32), sd(ldt.shape, F32), sd(bre.shape, F32), sd(bim.shape, F32)],
        compiler_params=_cparams(),
    )(lr, li, ldt, bre, bim, dar, dai, dbr, dbi)


SCAN_T = 256


def _scan_tables(ar, ai, tab_r, tab_i, sub, reverse):
    pr, pi = ar, ai
    for k in range(sub):
        row = sub - 1 - k if reverse else k
        tab_r[row:row + 1, :] = pr
        tab_i[row:row + 1, :] = pi
        pr, pi = ar * pr - ai * pi, ar * pi + ai * pr


SCAN_PIECES = STATE_BLOCK // LANES


def _scan_block(x, xs, os_, ar, ai, st, tab_r, tab_i, sub, reverse):
    npc = SCAN_PIECES
    hb = npc * LANES
    for k in range(2 * npc):
        xs[k][...] = x[:, k * LANES:(k + 1) * LANES]
    a8r = jnp.broadcast_to(ar, (SUBLANES, hb))
    a8i = jnp.broadcast_to(ai, (SUBLANES, hb))
    lane = lambda v, k: v[:, k * LANES:(k + 1) * LANES]
    sr = [jnp.zeros((SUBLANES, LANES), F32)] * npc
    si = [jnp.zeros((SUBLANES, LANES), F32)] * npc
    for t in (range(sub - 1, -1, -1) if reverse else range(sub)):
        rows = pl.ds(t, SUBLANES, stride=sub)
        for k in range(npc):
            kr, ki = lane(a8r, k), lane(a8i, k)
            sr[k], si[k] = (kr * sr[k] - ki * si[k] + xs[k][rows, :], kr * si[k] + ki * sr[k] + xs[npc + k][rows, :])
            os_[k][rows, :] = sr[k]
            os_[npc + k][rows, :] = si[k]
    cr, ci = st[0:1, :], st[1:2, :]
    tr, ti = tab_r[...], tab_i[...]
    far = 0 if reverse else sub - 1
    fr, fi = tab_r[far:far + 1, :], tab_i[far:far + 1, :]
    last_r = jnp.concatenate(sr, axis=1)
    last_i = jnp.concatenate(si, axis=1)
    for c in (range(SUBLANES - 1, -1, -1) if reverse else range(SUBLANES)):
        rows = pl.ds(c * sub, sub)
        add_r = tr * cr - ti * ci
        add_i = tr * ci + ti * cr
        for k in range(npc):
            os_[k][rows, :] = os_[k][rows, :] + lane(add_r, k)
            os_[npc + k][rows, :] = os_[npc + k][rows, :] + lane(add_i, k)
        cr, ci = last_r[c:c + 1, :] + (fr * cr - fi * ci), last_i[c:c + 1, :] + (fr * ci + fi * cr)
    st[0:1, :] = cr
    st[1:2, :] = ci
    return jnp.concatenate([o[...] for o in os_], axis=1)


def _scan_scratch(t_blk, sub, hb):
    pieces = [pltpu.VMEM((t_blk, LANES), F32)] * (4 * SCAN_PIECES)
    return [pltpu.VMEM((SUBLANES, hb), F32), pltpu.VMEM((sub, hb), F32), pltpu.VMEM((sub, hb), F32)] + pieces


def _ssm_fwd(proj, wb, wc, a):
    seq = proj.shape[0]
    nj = wb.shape[0]
    w2 = 2 * STATE_BLOCK
    hb = STATE_BLOCK
    t_blk = min(SCAN_T, seq)
    sub = t_blk // SUBLANES
    npc2 = 2 * SCAN_PIECES

    def body(u_ref, wb_ref, wc_ref, a_ref, s_ref, y_ref, st, tab_r, tab_i, *pieces):
        xs, os_ = pieces[:npc2], pieces[npc2:]
        ar = a_ref[:, :hb]
        ai = a_ref[:, hb:]

        @pl.when(pl.program_id(1) == 0)
        def _():
            st[...] = jnp.zeros_like(st)
            _scan_tables(ar, ai, tab_r, tab_i, sub, False)

        bu = jnp.dot(u_ref[...].astype(BF16), wb_ref[...], preferred_element_type=F32)
        s = _scan_block(bu, xs, os_, ar, ai, st, tab_r, tab_i, sub, False)
        s_ref[...] = s
        y_ref[...] = jnp.dot(s.astype(BF16), wc_ref[...], preferred_element_type=F32)

    sd = jax.ShapeDtypeStruct
    return pl.pallas_call(
        body, name="ssm_fwd", grid=(nj, seq // t_blk),
        in_specs=[pl.BlockSpec((t_blk, LANES), lambda j, i: (i, j)),
                  pl.BlockSpec((None, LANES, w2), lambda j, i: (j, 0, 0)),
                  pl.BlockSpec((None, w2, LANES), lambda j, i: (j, 0, 0)),
                  pl.BlockSpec((1, w2), lambda j, i: (0, j))],
        out_specs=[pl.BlockSpec((t_blk, w2), lambda j, i: (i, j)), pl.BlockSpec((t_blk, LANES), lambda j, i: (i, j))],
        out_shape=[sd((seq, nj * w2), F32), sd((seq, nj * LANES), F32)],
        scratch_shapes=_scan_scratch(t_blk, sub, hb), compiler_params=_cparams(),
    )(proj, wb, wc, a)


def _ssm_bwd(dy, s, proj, du1, wb, wc, a):
    seq = dy.shape[0]
    nj = wb.shape[0]
    w2 = 2 * STATE_BLOCK
    hb = STATE_BLOCK
    t_blk = min(SCAN_T, seq)
    sub = t_blk // SUBLANES
    nb = seq // t_blk
    npc2 = 2 * SCAN_PIECES

    def body(dy_ref, s_ref, sprev_ref, u_ref, du1_ref, wb_ref, wc_ref, a_ref,
             du_ref, dwb_ref, dwc_ref, da_ref, st, tab_r, tab_i, *pieces):
        xs, os_ = pieces[:npc2], pieces[npc2:]
        ib = pl.program_id(1)
        ar = a_ref[:, :hb]
        ai = -a_ref[:, hb:]

        @pl.when(ib == 0)
        def _():
            st[...] = jnp.zeros_like(st)
            _scan_tables(ar, ai, tab_r, tab_i, sub, True)

        dyb = dy_ref[...]
        ds = lax.dot_general(dyb, wc_ref[...], (NT, ((), ())), preferred_element_type=F32)
        lam = _scan_block(ds, xs, os_, ar, ai, st, tab_r, tab_i, sub, True)
        lamb = lam.astype(BF16)
        du = lax.dot_general(lamb, wb_ref[...], (NT, ((), ())), preferred_element_type=F32)
        du_ref[...] = (du + du1_ref[...]).astype(du_ref.dtype)
        sv = s_ref[...]
        dwb = lax.dot_general(u_ref[...].astype(BF16), lamb, (TN, ((), ())), preferred_element_type=F32)
        dwc = lax.dot_general(sv.astype(BF16), dyb, (TN, ((), ())), preferred_element_type=F32)

        rows = lax.broadcasted_iota(jnp.int32, sv.shape, 0)
        prev_last = sprev_ref[SUBLANES - 1:SUBLANES, :]
        prev_last = jnp.where(ib == nb - 1, jnp.zeros_like(prev_last), prev_last)
        s_sh = jnp.where(rows >= 1, pltpu.roll(sv, 1, 0), prev_last)
        lam_r, lam_i = lam[:, :hb], lam[:, hb:]
        sr_, si_ = s_sh[:, :hb], s_sh[:, hb:]
        dar = jnp.sum(lam_r * sr_ + lam_i * si_, axis=0, keepdims=True)
        dai = jnp.sum(lam_i * sr_ - lam_r * si_, axis=0, keepdims=True)
        contrib = jnp.concatenate([dar, dai], axis=1)

        @pl.when(ib == 0)
        def _():
            da_ref[...] = contrib
            dwb_ref[...] = dwb
            dwc_ref[...] = dwc

        @pl.when(ib != 0)
        def _():
            da_ref[...] += contrib
            dwb_ref[...] += dwb
            dwc_ref[...] += dwc

    blk = lambda j, i: (nb - 1 - i, j)
    prev_blk = lambda j, i: (jnp.maximum((nb - 1 - i) * sub - 1, 0), j)
    sd = jax.ShapeDtypeStruct
    return pl.pallas_call(
        body, name="ssm_bwd", grid=(nj, nb),
        in_specs=[pl.BlockSpec((t_blk, LANES), blk), pl.BlockSpec((t_blk, w2), blk),
                  pl.BlockSpec((SUBLANES, w2), prev_blk), pl.BlockSpec((t_blk, LANES), blk),
                  pl.BlockSpec((t_blk, LANES), blk),
                  pl.BlockSpec((None, LANES, w2), lambda j, i: (j, 0, 0)),
                  pl.BlockSpec((None, w2, LANES), lambda j, i: (j, 0, 0)),
                  pl.BlockSpec((1, w2), lambda j, i: (0, j))],
        out_specs=[pl.BlockSpec((t_blk, LANES), blk),
                   pl.BlockSpec((None, LANES, w2), lambda j, i: (j, 0, 0)),
                   pl.BlockSpec((None, w2, LANES), lambda j, i: (j, 0, 0)),
                   pl.BlockSpec((1, w2), lambda j, i: (0, j))],
        out_shape=[sd((seq, nj * LANES), BF16), sd((nj, LANES, w2), F32), sd((nj, w2, LANES), F32),
                   sd((1, nj * w2), F32)],
        scratch_shapes=_scan_scratch(t_blk, sub, hb), compiler_params=_cparams(),
    )(dy, s, s, proj, du1, wb, wc, a)


def _rope128(x, cos, sa, sb):
    return x * cos + pltpu.roll(x, 96, 1) * sa + pltpu.roll(x, 32, 1) * sb


def _rope128_t(dy, cos, sa, sb):
    return dy * cos + pltpu.roll(dy * sa, 32, 1) + pltpu.roll(dy * sb, 96, 1)


ATT_BQ = 256


def _probs(qn, qp, kn, kp, r0, scale):
    s = lax.dot_general(qn, kn, (NT, ((), ())), preferred_element_type=F32)
    s = s + lax.dot_general(qp, kp, (NT, ((), ())), preferred_element_type=F32)
    s = s * scale
    row = r0 + lax.broadcasted_iota(jnp.int32, s.shape, 0)
    col = lax.broadcasted_iota(jnp.int32, s.shape, 1)
    s = jnp.where(col <= row, s, jnp.finfo(F32).min)
    m = jnp.max(s, axis=-1, keepdims=True)
    e = jnp.exp(s - m)
    return e / jnp.sum(e, axis=-1, keepdims=True)


def _attn_specs(seq):
    tab = pl.BlockSpec((seq, LANES), lambda h: (0, 0))
    return [pl.BlockSpec((None, seq, 256), lambda h: (h, 0, 0)), pl.BlockSpec((None, seq, 128), lambda h: (h, 0, 0)),
            pl.BlockSpec((None, seq, 128), lambda h: (h, 0, 1)), tab, tab, tab, tab]


def _attn_fwd(q_raw, kv, kpe, cos, sa, sb):
    nh, seq, _ = q_raw.shape
    bq = min(ATT_BQ, seq)
    scale = (QK_NOPE + QK_ROPE) ** -0.5

    def body(q_ref, kn_ref, v_ref, kp_ref, cos_ref, sa_ref, sb_ref, o_ref):
        for r0 in range(0, seq, bq):
            rows, kend = pl.ds(r0, bq), r0 + bq
            qn = q_ref[rows, :QK_NOPE].astype(BF16)
            qp = _rope128(q_ref[rows, QK_NOPE:], cos_ref[rows, :], sa_ref[rows, :], sb_ref[rows, :]).astype(BF16)
            p = _probs(qn, qp, kn_ref[:kend, :], kp_ref[:kend, :], r0, scale)
            o_ref[rows, :] = jnp.dot(p.astype(BF16), v_ref[:kend, :], preferred_element_type=F32)

    return pl.pallas_call(
        body, name="attn_fwd", grid=(nh,), in_specs=_attn_specs(seq),
        out_specs=pl.BlockSpec((seq, V_DIM), lambda h: (0, h)),
        out_shape=jax.ShapeDtypeStruct((seq, nh * V_DIM), F32), compiler_params=_cparams(),
    )(q_raw, kv, kv, kpe, cos, sa, sb)


def _attn_bwd(q_raw, kv, kpe, cos, sa, sb, do):
    nh, seq, _ = q_raw.shape
    bq = min(ATT_BQ, seq)
    scale = (QK_NOPE + QK_ROPE) ** -0.5

    def body(q_ref, kn_ref, v_ref, kp_ref, cos_ref, sa_ref, sb_ref, do_ref, dq_ref, dkv_ref, dkp_ref):
        dkv_ref[...] = jnp.zeros_like(dkv_ref)
        dkp_ref[...] = jnp.zeros_like(dkp_ref)
        for r0 in range(0, seq, bq):
            rows, kend = pl.ds(r0, bq), r0 + bq
            cos_b, sa_b, sb_b = cos_ref[rows, :], sa_ref[rows, :], sb_ref[rows, :]
            qn = q_ref[rows, :QK_NOPE].astype(BF16)
            qp = _rope128(q_ref[rows, QK_NOPE:], cos_b, sa_b, sb_b).astype(BF16)
            kn, v, kp = kn_ref[:kend, :], v_ref[:kend, :], kp_ref[:kend, :]
            p = _probs(qn, qp, kn, kp, r0, scale)
            dob = do_ref[rows, :].astype(BF16)
            dp = lax.dot_general(dob, v, (NT, ((), ())), preferred_element_type=F32)
            ds = p * (dp - jnp.sum(p * dp, axis=-1, keepdims=True)) * scale
            dsb = ds.astype(BF16)
            pb = p.astype(BF16)
            dq_ref[rows, :QK_NOPE] = jnp.dot(dsb, kn, preferred_element_type=F32).astype(dq_ref.dtype)
            dqp = jnp.dot(dsb, kp, preferred_element_type=F32)
            dq_ref[rows, QK_NOPE:] = _rope128_t(dqp, cos_b, sa_b, sb_b).astype(dq_ref.dtype)
            dkv_ref[:kend, :QK_NOPE] += lax.dot_general(dsb, qn, (TN, ((), ())), preferred_element_type=F32)
            dkv_ref[:kend, QK_NOPE:] += lax.dot_general(pb, dob, (TN, ((), ())), preferred_element_type=F32)
            dkp_ref[:kend, :] += lax.dot_general(dsb, qp, (TN, ((), ())), preferred_element_type=F32)

    sd = jax.ShapeDtypeStruct
    return pl.pallas_call(
        body, name="attn_bwd", grid=(nh,),
        in_specs=_attn_specs(seq) + [pl.BlockSpec((seq, V_DIM), lambda h: (0, h))],
        out_specs=[pl.BlockSpec((None, seq, 256), lambda h: (h, 0, 0)),
                   pl.BlockSpec((None, seq, 256), lambda h: (h, 0, 0)),
                   pl.BlockSpec((None, seq, 128), lambda h: (h, 0, 0))],
        out_shape=[sd((nh, seq, 256), BF16), sd((nh, seq, 256), F32), sd((nh, seq, 128), F32)],
        compiler_params=_cparams(),
    )(q_raw, kv, kv, kpe, cos, sa, sb, do)


def _conv3(a, w, b):
    rows = lax.broadcasted_iota(jnp.int32, a.shape, 0)
    a1 = jnp.where(rows >= 1, pltpu.roll(a, 1, 0), 0.0)
    a2 = jnp.where(rows >= 2, pltpu.roll(a, 2, 0), 0.0)
    return w[2:3] * a + w[1:2] * a1 + w[0:1] * a2 + b, a1, a2


def _conv_gate_fwd(a, cw, cb):
    half, _, seq, c = a.shape
    nc = c // LANES

    def fn(pair, wg, wv, bg, bv):
        gc, _, _ = _conv3(pair[0], wg, bg)
        vc, _, _ = _conv3(pair[1], wv, bv)
        return gc * jax.nn.sigmoid(gc) * vc

    def w_spec(off, r):
        return pl.BlockSpec((None, r, LANES), lambda k, j: (k + off, 0, j))

    return _blockwise(
        "conv_gate_fwd", fn, [a, cw, cw, cb, cb],
        [pl.BlockSpec((None, 2, seq, LANES), lambda k, j: (k, 0, 0, j)),
         w_spec(0, 3), w_spec(half, 3), w_spec(0, 1), w_spec(half, 1)],
        [((seq, half * c), BF16)], [pl.BlockSpec((seq, LANES), lambda k, j: (0, k * nc + j))],
        grid=(half, nc))[0]


def _conv_gate_bwd(a, cw, cb, dm):
    half, _, seq, c = a.shape
    nc = c // LANES

    def body(a_ref, wg_ref, wv_ref, bg_ref, bv_ref, dm_ref, da_ref, dw_ref, db_ref):
        dmv = dm_ref[...]
        rows = lax.broadcasted_iota(jnp.int32, dmv.shape, 0)
        ga, wg = a_ref[0], wg_ref[...]
        va, wv = a_ref[1], wv_ref[...]
        gc, g1, g2 = _conv3(ga, wg, bg_ref[...])
        vc, v1, v2 = _conv3(va, wv, bv_ref[...])
        sg = jax.nn.sigmoid(gc)
        dms = dmv * sg
        d_val = dms * gc
        d_gate = dms * vc * (1.0 + gc * (1.0 - sg))

        def back(r, dc, own, a1, a2, w):
            up1 = jnp.where(rows < seq - 1, pltpu.roll(dc, seq - 1, 0), 0.0)
            up2 = jnp.where(rows < seq - 2, pltpu.roll(dc, seq - 2, 0), 0.0)
            da_ref[r] = (w[2:3] * dc + w[1:2] * up1 + w[0:1] * up2).astype(da_ref.dtype)
            dw_ref[r, 0:1, :] = jnp.sum(dc * a2, axis=0, keepdims=True)
            dw_ref[r, 1:2, :] = jnp.sum(dc * a1, axis=0, keepdims=True)
            dw_ref[r, 2:3, :] = jnp.sum(dc * own, axis=0, keepdims=True)
            db_ref[r] = jnp.sum(dc, axis=0, keepdims=True)

        back(0, d_gate, ga, g1, g2, wg)
        back(1, d_val, va, v1, v2, wv)

    def w_spec(off, r):
        return pl.BlockSpec((None, r, LANES), lambda k, j: (k + off, 0, j))

    def pair_spec(r):
        return pl.BlockSpec((None, 2, r, LANES), lambda k, j: (k, 0, 0, j))

    sd = jax.ShapeDtypeStruct
    return pl.pallas_call(
        body, name="conv_gate_bwd", grid=(half, nc),
        in_specs=[pair_spec(seq), w_spec(0, 3), w_spec(half, 3), w_spec(0, 1), w_spec(half, 1),
                  pl.BlockSpec((seq, LANES), lambda k, j: (0, k * nc + j))],
        out_specs=[pair_spec(seq), pair_spec(3), pair_spec(1)],
        out_shape=[sd((half, 2, seq, c), BF16), sd((half, 2, 3, c), F32), sd((half, 2, 1, c), F32)],
        compiler_params=_cparams(),
    )(a, cw, cw, cb, cb, dm)


ROW_T = 256


def _local_step(x, positions, target, w, emit=lambda **grads: None):
    seq, d = x.shape
    t_row = min(ROW_T, seq)
    nrow = seq // t_row
    ssm_w = d // 2
    nj = ssm_w // LANES
    n_groups = ssm_w // SSM_GROUP
    nh = w["wuq"].shape[0]
    q_rank = w["wuq"].shape[1]
    kv_rank = w["wukv"].shape[1]
    ns = w["wup"].shape[0]
    c_ff = w["wup"].shape[2]
    in_pad = w["win"].shape[1]
    tm = min(512, seq)
    nm = seq // tm
    sw = 2 * STATE_BLOCK
    g1 = (nrow,)

    lr3 = w["lam_re"].reshape(n_groups, 1, SSM_STATE)
    li3 = w["lam_im"].reshape(n_groups, 1, SSM_STATE)
    ldt3 = w["log_dt"].reshape(n_groups, 1, 1)
    bt_re = jnp.swapaxes(w["b_re"].reshape(n_groups, SSM_STATE, SSM_GROUP), 1, 2)
    bt_im = jnp.swapaxes(w["b_im"].reshape(n_groups, SSM_STATE, SSM_GROUP), 1, 2)
    abar_re, abar_im, bbt_re, bbt_im = _s5_prep(lr3, li3, ldt3, bt_re, bt_im)
    eye = jnp.eye(GROUPS_PER_BLOCK, dtype=F32)

    def blockdiag_in(bb):
        t = bb.reshape(nj, GROUPS_PER_BLOCK, SSM_GROUP, SSM_STATE)
        return jnp.einsum("jghp,gk->jghkp", t, eye).reshape(nj, LANES, STATE_BLOCK)

    def blockdiag_in_t(dwb):
        t = dwb.reshape(nj, GROUPS_PER_BLOCK, SSM_GROUP, GROUPS_PER_BLOCK, SSM_STATE)
        return jnp.einsum("jghkp,gk->jghp", t, eye).reshape(n_groups, SSM_GROUP, SSM_STATE)

    def blockdiag_out(cc):
        t = cc.reshape(nj, GROUPS_PER_BLOCK, SSM_GROUP, SSM_STATE)
        return jnp.einsum("jghp,gk->jkpgh", t, eye).reshape(nj, STATE_BLOCK, LANES)

    def blockdiag_out_t(dwc):
        t = dwc.reshape(nj, GROUPS_PER_BLOCK, SSM_STATE, GROUPS_PER_BLOCK, SSM_GROUP)
        return jnp.einsum("jkpgh,gk->jghp", t, eye).reshape(n_groups, SSM_GROUP, SSM_STATE)

    c_re = w["c_re"].reshape(n_groups, SSM_GROUP, SSM_STATE)
    c_im = w["c_im"].reshape(n_groups, SSM_GROUP, SSM_STATE)
    wb = jnp.concatenate([blockdiag_in(bbt_re), blockdiag_in(bbt_im)], axis=2).astype(BF16)
    wc = jnp.concatenate([blockdiag_out(c_re), -blockdiag_out(c_im)], axis=1).astype(BF16)
    a_lay = jnp.concatenate([abar_re.reshape(nj, 1, STATE_BLOCK), abar_im.reshape(nj, 1, STATE_BLOCK)],
                            axis=1).reshape(1, nj * sw)

    attn_w = w["attn_norm"]
    hn = _blockwise("norm1", lambda xb, wv: _rms(xb, wv), [x, attn_w], [_row_spec(t_row, d), _full_spec((1, d))],
                    [((seq, d), BF16)], [_row_spec(t_row, d)], g1)[0]
    proj = _mm2d("proj", hn, w["win"], NN, F32, tn=640)

    s_all, ylin = _ssm_fwd(proj, wb, wc, a_lay)
    u_spec = pl.BlockSpec((t_row, ssm_w), lambda i: (i, 0))

    def ypre_fn(yl, ub, dsk):
        yp = yl + dsk * ub
        return yp, jax.nn.gelu(yp)

    y_pre, yg = _blockwise("ssm_gelu", ypre_fn, [ylin, proj, w["ssm_d"]],
                           [_row_spec(t_row, ssm_w), u_spec, _full_spec((1, ssm_w))],
                           [((seq, ssm_w), F32), ((seq, ssm_w), BF16)],
                           [_row_spec(t_row, ssm_w)] * 2, g1)
    z = _mm2d("ssm_glu", yg, w["wglu"], NN, F32, res=w["b_glu"])
    y_ssm = _blockwise("ssm_gate", lambda yp, zb: jax.nn.gelu(yp) * jax.nn.sigmoid(zb), [y_pre, z],
                       [_row_spec(t_row, ssm_w)] * 2, [((seq, ssm_w), F32)], [_row_spec(t_row, ssm_w)], g1)[0]

    cq_off, ckv_off, kpe_off = ssm_w, ssm_w + q_rank, ssm_w + q_rank + kv_rank
    c_q = proj[:, cq_off:ckv_off]
    c_kv = proj[:, ckv_off:kpe_off]
    kpe_raw = proj[:, kpe_off:kpe_off + LANES]
    pos_b = jnp.broadcast_to(positions.astype(F32)[:, None], (seq, LANES))
    inv_freq = ROPE_THETA ** (-jnp.arange(0, QK_ROPE, 2, dtype=F32) / QK_ROPE)
    inv128 = jnp.tile(inv_freq, 4).reshape(1, LANES)

    def mla_prep_fn(cq, ckv, kp, pb, inv, wq, wkv):
        ang = pb * inv
        lane = lax.broadcasted_iota(jnp.int32, ang.shape, 1)
        cs, sn = jnp.cos(ang), jnp.sin(ang)
        cos = jnp.where(lane < QK_ROPE, cs, 0.0)
        sa = jnp.where(lane < QK_ROPE // 2, -sn, 0.0)
        sb = jnp.where(jnp.logical_and(lane >= QK_ROPE // 2, lane < QK_ROPE), sn, 0.0)
        return _rms(cq, wq), _rms(ckv, wkv), _rope128(kp, cos, sa, sb), cos, sa, sb

    qn, kvn, kpe, cos_t, sa_t, sb_t = _blockwise(
        "mla_prep", mla_prep_fn, [c_q, c_kv, kpe_raw, pos_b, inv128, w["q_norm"], w["kv_norm"]],
        [_row_spec(t_row, q_rank), _row_spec(t_row, kv_rank), _row_spec(t_row, LANES), _row_spec(t_row, LANES),
         _full_spec((1, LANES)), _full_spec((1, q_rank)), _full_spec((1, kv_rank))],
        [((seq, q_rank), BF16), ((seq, kv_rank), BF16), ((seq, LANES), BF16)] + [((seq, LANES), F32)] * 3,
        [_row_spec(t_row, q_rank), _row_spec(t_row, kv_rank)] + [_row_spec(t_row, LANES)] * 4, g1)

    def head_mm(name, act, wh, out_dtype):
        kdim, ndim = wh.shape[1], wh.shape[2]
        return _mm(name, act, wh, grid=(nh, nm, 1),
                   a_spec=pl.BlockSpec((tm, kdim), lambda h, i, k: (i, 0)),
                   b_spec=pl.BlockSpec((None, kdim, ndim), lambda h, i, k: (h, 0, 0)),
                   o_spec=pl.BlockSpec((None, tm, ndim), lambda h, i, k: (h, i, 0)),
                   out_shape=(nh, seq, ndim), out_dtype=out_dtype)

    q_raw = head_mm("mla_q", qn, w["wuq"], F32)
    kv = head_mm("mla_kv", kvn, w["wukv"], BF16)
    y_mla = _attn_fwd(q_raw, kv, kpe, cos_t, sa_t, sb_t)
    mla_w = nh * V_DIM

    def outnorm_fn(ys, ym, ws, wm):
        return jnp.concatenate([_rms(ys, ws), _rms(ym, wm)], axis=1)

    ycat = _blockwise("out_norm", outnorm_fn, [y_ssm, y_mla, w["son"], w["mon"]],
                      [_row_spec(t_row, ssm_w), _row_spec(t_row, mla_w), _full_spec((1, ssm_w)), _full_spec((1, mla_w))],
                      [((seq, d), BF16)], [_row_spec(t_row, d)], g1)[0]
    h1 = _mm2d("out_proj", ycat, w["wout"], NN, F32, res=x)

    hn2 = _blockwise("norm2", lambda hb, wv: _rms(hb, wv), [h1, w["ffn_norm"]],
                     [_row_spec(t_row, d), _full_spec((1, d))], [((seq, d), BF16)], [_row_spec(t_row, d)], g1)[0]
    tku = d
    half = ns // 2
    a_ff = _mm("ffn_up", hn2, w["wup"], grid=(ns, nm, d // tku),
               a_spec=pl.BlockSpec((tm, tku), lambda s, i, k: (i, k)),
               b_spec=pl.BlockSpec((None, tku, c_ff), lambda s, i, k: (s, k, 0)),
               o_spec=pl.BlockSpec((None, None, tm, c_ff), lambda s, i, k: (s % half, s // half, i, 0)),
               out_shape=(half, 2, seq, c_ff), out_dtype=F32)
    cb3 = w["conv_b"].reshape(ns, 1, c_ff)
    m_ff = _conv_gate_fwd(a_ff, w["conv_w"], cb3)
    d_ff = half * c_ff
    wdn = w["wdown"]
    tnd = _tile(d, 512)
    tmx, tnx = min(1024, seq), _tile(d, 1024)
    h2 = _mm2d("ffn_down", m_ff, wdn, NN, F32, tk=d_ff, res=h1)

    def loss_fn(hb, tb, wv):
        def f(hh, ww):
            err = _rms(hh, ww) - tb
            return 0.5 * jnp.sum(jnp.mean(err * err, axis=-1))

        lossv, (dh, dw) = jax.value_and_grad(f, argnums=(0, 1))(hb, wv)
        return dh, dh, jnp.full((1, LANES), lossv, F32), dw

    fin_w = w["final_norm"].reshape(1, d)
    dh2, dh2b, loss_acc, g_final = _blockwise(
        "loss_head", loss_fn, [h2, target, fin_w], [_row_spec(t_row, d), _row_spec(t_row, d), _full_spec((1, d))],
        [((seq, d), F32), ((seq, d), BF16), ((1, LANES), F32), ((1, d), F32)],
        [_row_spec(t_row, d), _row_spec(t_row, d), _full_spec((1, LANES)), _full_spec((1, d))], g1, n_acc=2)
    loss = loss_acc[0, 0]

    dm = _mm2d("ffn_down_dx", dh2b, wdn, NT, F32, tn=c_ff)
    tks = seq
    g_wdown = _mm2d("ffn_down_dw", m_ff, dh2b, TN, BF16, tm=c_ff)
    emit(wdown=g_wdown)
    da_ff, g_convw2, g_convb2 = _conv_gate_bwd(a_ff, w["conv_w"], cb3, dm)
    g_convw = jnp.swapaxes(g_convw2, 0, 1).reshape(ns, 3, c_ff)
    g_convb = jnp.swapaxes(g_convb2, 0, 1).reshape(ns, 1, c_ff)
    g_wup = _mm("ffn_up_dw", hn2, da_ff, grid=(ns, d // tnd, seq // tks), contract=TN,
                a_spec=pl.BlockSpec((tks, tnd), lambda s, j, k: (k, j)),
                b_spec=pl.BlockSpec((None, None, tks, c_ff), lambda s, j, k: (s % half, s // half, k, 0)),
                o_spec=pl.BlockSpec((None, tnd, c_ff), lambda s, j, k: (s, j, 0)),
                out_shape=(ns, d, c_ff), out_dtype=BF16)
    emit(wup=g_wup, conv_w=g_convw)
    dhn2 = _mm("ffn_up_dx", da_ff, w["wup"], grid=(seq // tmx, d // tnx, ns), contract=NT,
               a_spec=pl.BlockSpec((None, None, tmx, c_ff), lambda i, j, s: (s % half, s // half, i, 0)),
               b_spec=pl.BlockSpec((None, tnx, c_ff), lambda i, j, s: (s, j, 0)),
               o_spec=pl.BlockSpec((tmx, tnx), lambda i, j, s: (i, j)),
               out_shape=(seq, d), out_dtype=F32)

    def norm_bwd_fn(hb, dres, dn, wv):
        dx_, dw_ = _rms_bwd(hb, wv, dn)
        dtot = dres + dx_
        return dtot, dtot, dw_

    dh1, dh1b, g_ffn_norm = _blockwise(
        "norm2_bwd", norm_bwd_fn, [h1, dh2, dhn2, w["ffn_norm"]],
        [_row_spec(t_row, d)] * 3 + [_full_spec((1, d))],
        [((seq, d), F32), ((seq, d), BF16), ((1, d), F32)],
        [_row_spec(t_row, d), _row_spec(t_row, d), _full_spec((1, d))], g1, n_acc=1)

    dycat = _mm2d("out_proj_dx", dh1b, w["wout"], NT, F32)
    g_wout = _mm2d("out_proj_dw", ycat, dh1b, TN, BF16)

    def outnorm_bwd_fn(ys, ym, dyc, ws, wm):
        dys, dws = _rms_bwd(ys, ws, dyc[:, :ssm_w])
        dym, dwm = _rms_bwd(ym, wm, dyc[:, ssm_w:])
        return dys, dym, dws, dwm

    dy_ssm, dy_mla, g_son, g_mon = _blockwise(
        "out_norm_bwd", outnorm_bwd_fn, [y_ssm, y_mla, dycat, w["son"], w["mon"]],
        [_row_spec(t_row, ssm_w), _row_spec(t_row, mla_w), _row_spec(t_row, d), _full_spec((1, ssm_w)),
         _full_spec((1, mla_w))],
        [((seq, ssm_w), F32), ((seq, mla_w), F32), ((1, ssm_w), F32), ((1, mla_w), F32)],
        [_row_spec(t_row, ssm_w), _row_spec(t_row, mla_w), _full_spec((1, ssm_w)), _full_spec((1, mla_w))],
        g1, n_acc=2)

    def gate_bwd1_fn(dy, yp, zb):
        ygv = jax.nn.gelu(yp)
        sg = jax.nn.sigmoid(zb)
        dz = dy * ygv * sg * (1.0 - sg)
        return dz, jnp.sum(dz, axis=0, keepdims=True)

    dz, g_bglu = _blockwise("ssm_gate_bwd", gate_bwd1_fn, [dy_ssm, y_pre, z], [_row_spec(t_row, ssm_w)] * 3,
                            [((seq, ssm_w), BF16), ((1, ssm_w), F32)],
                            [_row_spec(t_row, ssm_w), _full_spec((1, ssm_w))], g1, n_acc=1)
    dyg2 = _mm2d("ssm_glu_dx", dz, w["wglu"], NT, F32)
    g_wglu = _mm2d("ssm_glu_dw", yg, dz, TN, BF16)

    def gelu_bwd_fn(dy, yp, zb, dg2, ub, dsk):
        dyg = dy * jax.nn.sigmoid(zb) + dg2
        _, vjp = jax.vjp(jax.nn.gelu, yp)
        dyp = vjp(dyg)[0]
        return dyp, dyp * dsk, jnp.sum(dyp * ub, axis=0, keepdims=True)

    dy_pre, du1, g_ssmd = _blockwise(
        "ssm_gelu_bwd", gelu_bwd_fn, [dy_ssm, y_pre, z, dyg2, proj, w["ssm_d"]],
        [_row_spec(t_row, ssm_w)] * 4 + [u_spec, _full_spec((1, ssm_w))],
        [((seq, ssm_w), BF16), ((seq, ssm_w), F32), ((1, ssm_w), F32)],
        [_row_spec(t_row, ssm_w), _row_spec(t_row, ssm_w), _full_spec((1, ssm_w))], g1, n_acc=1)
    dq_raw, dkv, dkp_h = _attn_bwd(q_raw, kv, kpe, cos_t, sa_t, sb_t, dy_mla)

    def head_mm_dx(name, dact, wh):
        kdim, ndim = wh.shape[1], wh.shape[2]
        return _mm(name, dact, wh, grid=(nm, 1, nh), contract=NT,
                   a_spec=pl.BlockSpec((None, tm, ndim), lambda i, j, h: (h, i, 0)),
                   b_spec=pl.BlockSpec((None, kdim, ndim), lambda i, j, h: (h, 0, 0)),
                   o_spec=pl.BlockSpec((tm, kdim), lambda i, j, h: (i, 0)),
                   out_shape=(seq, kdim), out_dtype=F32)

    def head_mm_dw(name, act, dact):
        kdim, ndim = act.shape[1], dact.shape[2]
        return _mm(name, act, dact, grid=(nh, 1, seq // tks), contract=TN,
                   a_spec=pl.BlockSpec((tks, kdim), lambda h, j, k: (k, 0)),
                   b_spec=pl.BlockSpec((None, tks, ndim), lambda h, j, k: (h, k, 0)),
                   o_spec=pl.BlockSpec((None, kdim, ndim), lambda h, j, k: (h, 0, 0)),
                   out_shape=(nh, kdim, ndim), out_dtype=BF16)

    g_wuq = head_mm_dw("mla_q_dw", qn, dq_raw)
    g_wukv = head_mm_dw("mla_kv_dw", kvn, dkv)
    emit(wout=g_wout, wuq=g_wuq, wukv=g_wukv, wglu=g_wglu)
    dqn = head_mm_dx("mla_q_dx", dq_raw, w["wuq"])
    dkvn = head_mm_dx("mla_kv_dx", dkv, w["wukv"])

    du, dwb, dwc, da_lay = _ssm_bwd(dy_pre, s_all, proj, du1, wb, wc, a_lay)
    g_c_re = blockdiag_out_t(dwc[:, :STATE_BLOCK, :])
    g_c_im = -blockdiag_out_t(dwc[:, STATE_BLOCK:, :])
    dbbt_re = blockdiag_in_t(dwb[:, :, :STATE_BLOCK])
    dbbt_im = blockdiag_in_t(dwb[:, :, STATE_BLOCK:])
    da3 = da_lay.reshape(nj, 2, STATE_BLOCK)
    dabar_re = da3[:, 0, :].reshape(n_groups, 1, SSM_STATE)
    dabar_im = da3[:, 1, :].reshape(n_groups, 1, SSM_STATE)
    g_lr3, g_li3, g_ldt3, g_bt_re, g_bt_im = _s5_prep_bwd(lr3, li3, ldt3, bt_re, bt_im,
                                                           dabar_re, dabar_im, dbbt_re, dbbt_im)

    def mla_prep_bwd_fn(cq, ckv, dqn_b, dkvn_b, dkp_b, cos, sa, sb, wq, wkv):
        dcq, dwq = _rms_bwd(cq, wq, dqn_b)
        dckv, dwkv = _rms_bwd(ckv, wkv, dkvn_b)
        dkp_sum = dkp_b[0]
        for h in range(1, nh):
            dkp_sum = dkp_sum + dkp_b[h]
        return dcq, dckv, _rope128_t(dkp_sum, cos, sa, sb), dwq, dwkv

    dc_q, dc_kv, dkpe_raw, g_qnorm, g_kvnorm = _blockwise(
        "mla_prep_bwd", mla_prep_bwd_fn, [c_q, c_kv, dqn, dkvn, dkp_h, cos_t, sa_t, sb_t, w["q_norm"], w["kv_norm"]],
        [_row_spec(t_row, q_rank), _row_spec(t_row, kv_rank), _row_spec(t_row, q_rank), _row_spec(t_row, kv_rank),
         pl.BlockSpec((nh, t_row, LANES), lambda i: (0, i, 0))] + [_row_spec(t_row, LANES)] * 3
        + [_full_spec((1, q_rank)), _full_spec((1, kv_rank))],
        [((seq, q_rank), BF16), ((seq, kv_rank), BF16), ((seq, LANES), BF16), ((1, q_rank), F32), ((1, kv_rank), F32)],
        [_row_spec(t_row, q_rank), _row_spec(t_row, kv_rank), _row_spec(t_row, LANES), _full_spec((1, q_rank)),
         _full_spec((1, kv_rank))], g1, n_acc=2)

    dproj = jnp.concatenate([du, dc_q, dc_kv, dkpe_raw], axis=1)
    g_win = _mm2d("proj_dw", hn, dproj, TN, BF16, tn=640)
    emit(win=g_win)
    dhn = _mm2d("proj_dx", dproj, w["win"], NT, F32)

    def norm1_bwd_fn(xb, dres, dn, wv):
        dx_, dw_ = _rms_bwd(xb, wv, dn)
        return dres + dx_, dw_

    grad_x, g_attn_norm = _blockwise(
        "norm1_bwd", norm1_bwd_fn, [x, dh1, dhn, attn_w], [_row_spec(t_row, d)] * 3 + [_full_spec((1, d))],
        [((seq, d), F32), ((1, d), F32)], [_row_spec(t_row, d), _full_spec((1, d))], g1, n_acc=1)

    grads = dict(
        attn_norm=g_attn_norm, win=g_win, lam_re=g_lr3, lam_im=g_li3, log_dt=g_ldt3,
        b_re=jnp.swapaxes(g_bt_re, 1, 2), b_im=jnp.swapaxes(g_bt_im, 1, 2), c_re=g_c_re, c_im=g_c_im,
        ssm_d=g_ssmd, wglu=g_wglu, b_glu=g_bglu, q_norm=g_qnorm, wuq=g_wuq, kv_norm=g_kvnorm, wukv=g_wukv,
        son=g_son, mon=g_mon, wout=g_wout, ffn_norm=g_ffn_norm, wup=g_wup, conv_w=g_convw, conv_b=g_convb,
        wdown=g_wdown, final_norm=g_final)
    return loss, grad_x, grads


def _mesh_pos():
    return lax.axis_index("x"), lax.axis_index("y"), lax.axis_index("c")


def _handshake_all():
    x, y, c = _mesh_pos()
    barrier = pltpu.get_barrier_semaphore()
    for k in range(1, N_DEV):
        peer = (1 - x if k & 4 else x, 1 - y if k & 2 else y, 1 - c if k & 1 else c)
        pl.semaphore_signal(barrier, inc=1, device_id=peer, device_id_type=MESH)
    pl.semaphore_wait(barrier, N_DEV - 1)


def _comm_call(name, body, n, out_shape, ins, collective_id, after=None):
    sems = [pltpu.SemaphoreType.DMA((7 * n,)), pltpu.SemaphoreType.DMA((7 * n,)), pltpu.SemaphoreType.DMA((n,))]
    if collective_id is None:
        any_spec = pl.BlockSpec(memory_space=pl.ANY)
        return pl.pallas_call(body, name=name, out_shape=out_shape, in_specs=[any_spec] * n,
                              out_specs=[any_spec] * n, scratch_shapes=sems)(*ins)
    seq_body = body
    if after is not None:
        ins = list(ins) + [after]

        def seq_body(*refs):
            body(*refs[:n], *refs[n + 1:])

    return pl.kernel(seq_body, name=name, out_type=out_shape,
                     mesh=plsc.ScalarSubcoreMesh(axis_name="seq", num_cores=1), scratch_types=sems,
                     compiler_params=pltpu.CompilerParams(collective_id=collective_id))(*ins)


def _all_gather(name, xs, collective_id=None, after=None):
    n = len(xs)

    def body(*refs):
        x_refs, o_refs = refs[:n], refs[n:2 * n]
        send_sems, recv_sems, local_sems = refs[2 * n:]
        if collective_id is not None:
            _handshake_all()
        x, y, c = _mesh_pos()
        me, sibling = (x, y, c), (x, y, 1 - c)
        chips = [(1 - x, y), (x, 1 - y), (1 - x, 1 - y)]

        def slot(o_ref, px, py, pc):
            return o_ref.at[4 * px + 2 * py + pc]

        def copy(t, k, block, to, src=None):
            dst = slot(o_refs[t], *block)
            return pltpu.make_async_remote_copy(
                src_ref=dst if src is None else src, dst_ref=dst,
                send_sem=send_sems.at[7 * t + k], recv_sem=recv_sems.at[7 * t + k],
                device_id=to, device_id_type=MESH)

        started = []
        for t in range(n):
            mine = pltpu.make_async_copy(x_refs[t], slot(o_refs[t], *me), local_sems.at[t])
            mine.start()
            started.append(mine)
        first = []
        for t in range(n):
            first.append(copy(t, 0, me, sibling, src=x_refs[t]))
            first += [copy(t, 1 + j, me, (*chip, c), src=x_refs[t]) for j, chip in enumerate(chips)]
        for cp in first:
            cp.start()
        passed = []
        for j, chip in enumerate(chips):
            for t in range(n):
                copy(t, 1 + j, (*chip, c), me).wait_recv()
                fwd = copy(t, 4 + j, (*chip, c), sibling)
                fwd.start()
                passed.append(fwd)
        for t in range(n):
            copy(t, 0, sibling, me).wait_recv()
            for j, chip in enumerate(chips):
                copy(t, 4 + j, (*chip, 1 - c), me).wait_recv()
        for cp in first + passed:
            cp.wait_send()
        for mine in started:
            mine.wait()

    out_shape = [jax.ShapeDtypeStruct((N_DEV,) + v.shape, v.dtype) for v in xs]
    return _comm_call(name, body, n, out_shape, xs, collective_id, after)


def _exchange_partials(name, gs, collective_id=None, after=None):
    n = len(gs)

    def body(*refs):
        g_refs, o_refs = refs[:n], refs[n:2 * n]
        send_sems, recv_sems, local_sems = refs[2 * n:]
        if collective_id is not None:
            _handshake_all()
        x, y, c = _mesh_pos()
        me_idx = 4 * x + 2 * y + c
        copies = []
        for t in range(n):
            mine = pltpu.make_async_copy(g_refs[t].at[me_idx], o_refs[t].at[me_idx], local_sems.at[t])
            mine.start()
            copies.append(mine)
        remote = []
        for k in range(1, N_DEV):
            px = 1 - x if k & 4 else x
            py = 1 - y if k & 2 else y
            pc = 1 - c if k & 1 else c
            p_idx = 4 * px + 2 * py + pc
            for t in range(n):
                cp = pltpu.make_async_remote_copy(
                    src_ref=g_refs[t].at[p_idx], dst_ref=o_refs[t].at[me_idx],
                    send_sem=send_sems.at[7 * t + k - 1], recv_sem=recv_sems.at[7 * t + k - 1],
                    device_id=(px, py, pc), device_id_type=MESH)
                cp.start()
                landing = pltpu.make_async_remote_copy(
                    src_ref=g_refs[t].at[p_idx], dst_ref=o_refs[t].at[p_idx],
                    send_sem=send_sems.at[7 * t + k - 1], recv_sem=recv_sems.at[7 * t + k - 1],
                    device_id=(px, py, pc), device_id_type=MESH)
                remote.append((cp, landing))
        for cp, landing in remote:
            landing.wait_recv()
        for cp, landing in remote:
            cp.wait_send()
        for mine in copies:
            mine.wait()

    out_shape = [jax.ShapeDtypeStruct(v.shape, v.dtype) for v in gs]
    return _comm_call(name, body, n, out_shape, gs, collective_id, after)


ADAM_BLOCK_ELEMS = 128 * 1024


def _adamw_sum(name, parts, wv, mv, vv):
    npart, r, c = parts.shape
    tr = r
    if r * c > ADAM_BLOCK_ELEMS and r % SUBLANES == 0:
        tr = SUBLANES
        while r % (tr * 2) == 0 and tr * 2 * c <= ADAM_BLOCK_ELEMS:
            tr *= 2
    bc1 = 1.0 - ADAM_B1 ** ADAM_STEP
    bc2 = 1.0 - ADAM_B2 ** ADAM_STEP

    def fn(pb, wb_, mb, vb):
        g = pb[0].astype(F32)
        for j in range(1, npart):
            g = g + pb[j].astype(F32)
        m_new = ADAM_B1 * mb + (1.0 - ADAM_B1) * g
        v_new = ADAM_B2 * vb + (1.0 - ADAM_B2) * (g * g)
        m_hat = m_new / bc1
        v_hat = v_new / bc2
        delta = -ADAM_LR * (m_hat / (jnp.sqrt(v_hat) + ADAM_EPS) + ADAM_WD * wb_)
        return g, delta, m_new, v_new

    row = pl.BlockSpec((tr, c), lambda i: (i, 0))
    return _blockwise(name, fn, [parts, wv, mv, vv],
                      [pl.BlockSpec((npart, tr, c), lambda i: (0, i, 0)), row, row, row],
                      [((r, c), F32)] * 4, [row] * 4, (r // tr,))


_SMALL = ["attn_norm", "lam_re", "lam_im", "log_dt", "b_re", "b_im", "c_re", "c_im", "ssm_d", "b_glu",
          "q_norm", "kv_norm", "son", "mon", "ffn_norm", "conv_b", "final_norm"]
_BIG = ["win", "wglu", "wuq", "wukv", "wout", "wup", "wdown", "conv_w"]
_ORDER = ["attn_norm", "win", "lam_re", "lam_im", "log_dt", "b_re", "b_im", "c_re", "c_im", "ssm_d", "wglu",
          "b_glu", "q_norm", "wuq", "kv_norm", "wukv", "son", "mon", "wout", "ffn_norm", "wup", "conv_w",
          "conv_b", "wdown", "final_norm"]


def _pack(arrs):
    flat = jnp.concatenate([a.reshape(-1).astype(F32) for a in arrs])
    pad = (-flat.shape[0]) % (LANES * LANES)
    return jnp.pad(flat, (0, pad)).reshape(-1, LANES)


def _unpack(packed, shapes):
    flat = packed.reshape(-1)
    out, off = [], 0
    for s in shapes:
        n = math.prod(s)
        out.append(flat[off:off + n].reshape(s))
        off += n
    return out


def kernel(x, positions, attn_norm_w, w_in, ssm_lambda_re, ssm_lambda_im, ssm_log_dt, ssm_b_re, ssm_b_im, ssm_c_re, ssm_c_im, ssm_d, ssm_w_glu, ssm_b_glu, mla_q_norm_w, mla_w_uq, mla_kv_norm_w, mla_w_ukv, ssm_out_norm_w, mla_out_norm_w, w_out, ffn_norm_w, ffn_w_up, ffn_conv_w, ffn_conv_b, ffn_w_down, final_norm_w, loss_target, m_attn_norm_w, m_w_in, m_ssm_lambda_re, m_ssm_lambda_im, m_ssm_log_dt, m_ssm_b_re, m_ssm_b_im, m_ssm_c_re, m_ssm_c_im, m_ssm_d, m_ssm_w_glu, m_ssm_b_glu, m_mla_q_norm_w, m_mla_w_uq, m_mla_kv_norm_w, m_mla_w_ukv, m_ssm_out_norm_w, m_mla_out_norm_w, m_w_out, m_ffn_norm_w, m_ffn_w_up, m_ffn_conv_w, m_ffn_conv_b, m_ffn_w_down, m_final_norm_w, v_attn_norm_w, v_w_in, v_ssm_lambda_re, v_ssm_lambda_im, v_ssm_log_dt, v_ssm_b_re, v_ssm_b_im, v_ssm_c_re, v_ssm_c_im, v_ssm_d, v_ssm_w_glu, v_ssm_b_glu, v_mla_q_norm_w, v_mla_w_uq, v_mla_kv_norm_w, v_mla_w_ukv, v_ssm_out_norm_w, v_mla_out_norm_w, v_w_out, v_ffn_norm_w, v_ffn_w_up, v_ffn_conv_w, v_ffn_conv_b, v_ffn_w_down, v_final_norm_w):
    wts = dict(attn_norm=attn_norm_w, win=w_in, lam_re=ssm_lambda_re, lam_im=ssm_lambda_im, log_dt=ssm_log_dt,
               b_re=ssm_b_re, b_im=ssm_b_im, c_re=ssm_c_re, c_im=ssm_c_im, ssm_d=ssm_d, wglu=ssm_w_glu,
               b_glu=ssm_b_glu, q_norm=mla_q_norm_w, wuq=mla_w_uq, kv_norm=mla_kv_norm_w, wukv=mla_w_ukv,
               son=ssm_out_norm_w, mon=mla_out_norm_w, wout=w_out, ffn_norm=ffn_norm_w, wup=ffn_w_up,
               conv_w=ffn_conv_w, conv_b=ffn_conv_b, wdown=ffn_w_down, final_norm=final_norm_w)
    moms = dict(zip(_ORDER, [m_attn_norm_w, m_w_in, m_ssm_lambda_re, m_ssm_lambda_im, m_ssm_log_dt, m_ssm_b_re,
                             m_ssm_b_im, m_ssm_c_re, m_ssm_c_im, m_ssm_d, m_ssm_w_glu, m_ssm_b_glu, m_mla_q_norm_w,
                             m_mla_w_uq, m_mla_kv_norm_w, m_mla_w_ukv, m_ssm_out_norm_w, m_mla_out_norm_w, m_w_out,
                             m_ffn_norm_w, m_ffn_w_up, m_ffn_conv_w, m_ffn_conv_b, m_ffn_w_down, m_final_norm_w]))
    vels = dict(zip(_ORDER, [v_attn_norm_w, v_w_in, v_ssm_lambda_re, v_ssm_lambda_im, v_ssm_log_dt, v_ssm_b_re,
                             v_ssm_b_im, v_ssm_c_re, v_ssm_c_im, v_ssm_d, v_ssm_w_glu, v_ssm_b_glu, v_mla_q_norm_w,
                             v_mla_w_uq, v_mla_kv_norm_w, v_mla_w_ukv, v_ssm_out_norm_w, v_mla_out_norm_w, v_w_out,
                             v_ffn_norm_w, v_ffn_w_up, v_ffn_conv_w, v_ffn_conv_b, v_ffn_w_down, v_final_norm_w]))
    seq, d = x.shape[1], x.shape[2]
    in_width = w_in.shape[2]
    in_pad = -(-in_width // LANES) * LANES
    q_cols = mla_w_uq.shape[2]
    q_pad = 2 * LANES

    (win_g,) = _all_gather("gather_w_in", [jnp.pad(w_in[0], ((0, 0), (0, in_pad - in_width))).astype(BF16)])
    wglu_g, wuq_g, wukv_g, wout_g, convw_g = _all_gather(
        "gather_mix", [ssm_w_glu[0].astype(BF16), jnp.pad(mla_w_uq[0], ((0, 0), (0, q_pad - q_cols))).astype(BF16),
                       mla_w_ukv[0].astype(BF16), w_out[0].astype(BF16), ffn_conv_w[0]], collective_id=0)
    (wup_g,) = _all_gather("gather_ffn_up", [ffn_w_up[0].astype(BF16)], collective_id=1)
    (wdown_g,) = _all_gather("gather_ffn_down", [ffn_w_down[0].astype(BF16)], collective_id=2)
    ns = N_DEV
    c_ff = wup_g.shape[2]
    w = dict(
        attn_norm=attn_norm_w, win=win_g.reshape(d, in_pad), lam_re=ssm_lambda_re, lam_im=ssm_lambda_im,
        log_dt=ssm_log_dt, b_re=ssm_b_re, b_im=ssm_b_im, c_re=ssm_c_re, c_im=ssm_c_im, ssm_d=ssm_d,
        wglu=wglu_g.reshape(d // 2, d // 2), b_glu=ssm_b_glu, q_norm=mla_q_norm_w, wuq=wuq_g,
        kv_norm=mla_kv_norm_w, wukv=wukv_g, son=ssm_out_norm_w, mon=mla_out_norm_w, wout=wout_g.reshape(d, d),
        ffn_norm=ffn_norm_w, wup=wup_g, conv_w=convw_g, conv_b=ffn_conv_b,
        wdown=wdown_g.reshape(ns // 2 * c_ff, d), final_norm=final_norm_w)

    shard_layout = dict(
        win=lambda a: a[:, :in_width].reshape(N_DEV, d // N_DEV, in_width),
        wglu=lambda a: a.reshape(N_DEV, d // 2 // N_DEV, d // 2),
        wuq=lambda a: a[:, :, :q_cols], wukv=lambda a: a, wout=lambda a: a.reshape(N_DEV, d // N_DEV, d),
        wup=lambda a: a, wdown=lambda a: a.reshape(N_DEV, c_ff // 2, d), conv_w=lambda a: a)
    recv = {}
    next_id = [3]

    last = [None]

    def exchange(**grads):
        names = list(grads)
        got = _exchange_partials("exchange_" + "_".join(names), [shard_layout[k](grads[k]) for k in names],
                                 collective_id=next_id[0], after=last[0])
        next_id[0] += 1
        last[0] = got[-1]
        recv.update(zip(names, got))

    loss_part, grad_x, g = _local_step(x[0], positions[0], loss_target[0], w, emit=exchange)
    loss = lax.psum(loss_part, ("x", "y", "c"))
    small_shapes = [wts[k].shape for k in _SMALL]
    small_part = _pack([g[k] for k in _SMALL])
    small_all = _all_gather("gather_small_grads", [small_part], collective_id=next_id[0], after=last[0])[0]

    out = {}
    for k in _BIG:
        shp = wts[k].shape
        r, c = shp[-2], shp[-1]
        res = _adamw_sum("adamw_" + k, recv[k].reshape(N_DEV, r, c), wts[k].reshape(r, c),
                         moms[k].reshape(r, c), vels[k].reshape(r, c))
        out[k] = [a.reshape(shp) for a in res]
    sw_ = _pack([wts[k] for k in _SMALL])
    sm_ = _pack([moms[k] for k in _SMALL])
    sv_ = _pack([vels[k] for k in _SMALL])
    res = _adamw_sum("adamw_small", small_all, sw_, sm_, sv_)
    unpacked = [_unpack(a, small_shapes) for a in res]
    for i, k in enumerate(_SMALL):
        out[k] = [u[i] for u in unpacked]

    grad_x = grad_x.reshape(x.shape)
    return (loss, grad_x, *[out[k][0] for k in _ORDER], *[out[k][1] for k in _ORDER],
            *[out[k][2] for k in _ORDER], *[out[k][3] for k in _ORDER])
```

```python
import functools
import math

import jax
import jax.numpy as jnp
from jax import lax
from jax.experimental import pallas as pl
from jax.experimental.pallas import tpu as pltpu
from jax.experimental.pallas import tpu_sc as plsc

F32 = jnp.float32
BF16 = jnp.bfloat16
MESH = pl.DeviceIdType.MESH

N_DEV = 8
LANES = 128
SUBLANES = 8
VMEM_LIMIT = 48 * 1024 * 1024

SSM_GROUP = 16
SSM_STATE = 64
GROUPS_PER_BLOCK = LANES // SSM_GROUP
STATE_BLOCK = GROUPS_PER_BLOCK * SSM_STATE
QK_NOPE = 128
QK_ROPE = 64
V_DIM = 128
ROPE_THETA = 10000.0
RMS_EPS = 1e-6

ADAM_LR = 0.001
ADAM_B1 = 0.9
ADAM_B2 = 0.999
ADAM_EPS = 1e-08
ADAM_WD = 0.01
ADAM_STEP = 10

NN = ((1,), (0,))
NT = ((1,), (1,))
TN = ((0,), (0,))


def _cparams():
    return pltpu.CompilerParams(vmem_limit_bytes=VMEM_LIMIT)


def _tile(n, want):
    if n <= want:
        return n
    t = (want // LANES) * LANES
    while t >= LANES:
        if n % t == 0:
            return t
        t -= LANES
    return n


def _mm(name, a, b, *, grid, a_spec, b_spec, o_spec, out_shape, out_dtype, contract=NN,
        res=None, res_spec=None):
    nk = grid[-1]
    kaxis = len(grid) - 1
    acc_shape = tuple(d for d in o_spec.block_shape if d is not None)

    def body(*refs):
        a_ref, b_ref = refs[:2]
        r_ref = None if res is None else refs[2]
        o_ref = refs[2 if res is None else 3]
        part = lax.dot_general(a_ref[...].astype(BF16), b_ref[...].astype(BF16),
                               (contract, ((), ())), preferred_element_type=F32)
        if nk == 1:
            if r_ref is not None:
                part = part + r_ref[...].astype(F32)
            o_ref[...] = part.astype(o_ref.dtype)
            return
        acc = refs[-1]
        k = pl.program_id(kaxis)

        @pl.when(k == 0)
        def _():
            acc[...] = part

        @pl.when(k != 0)
        def _():
            acc[...] += part

        @pl.when(k == nk - 1)
        def _():
            r = acc[...]
            if r_ref is not None:
                r = r + r_ref[...].astype(F32)
            o_ref[...] = r.astype(o_ref.dtype)

    ins = [a, b] + ([] if res is None else [res])
    in_specs = [a_spec, b_spec] + ([] if res is None else [res_spec])
    return pl.pallas_call(
        body, name=name, grid=grid, in_specs=in_specs, out_specs=o_spec,
        out_shape=jax.ShapeDtypeStruct(out_shape, out_dtype),
        scratch_shapes=[pltpu.VMEM(acc_shape, F32)] if nk > 1 else [], compiler_params=_cparams(),
    )(*ins)


def _mm2d(name, a, b, contract, out_dtype, tm=512, tn=512, tk=2048, res=None):
    if contract == NN:
        (m, kk), n = a.shape, b.shape[1]
    elif contract == NT:
        (m, kk), n = a.shape, b.shape[0]
    else:
        (kk, m), n = a.shape, b.shape[1]
    tm, tn, tk = _tile(m, tm), _tile(n, tn), _tile(kk, tk)
    grid = (m // tm, n // tn, kk // tk)
    if contract == TN:
        a_spec = pl.BlockSpec((tk, tm), lambda i, j, k: (k, i))
    else:
        a_spec = pl.BlockSpec((tm, tk), lambda i, j, k: (i, k))
    if contract == NT:
        b_spec = pl.BlockSpec((tn, tk), lambda i, j, k: (j, k))
    else:
        b_spec = pl.BlockSpec((tk, tn), lambda i, j, k: (k, j))
    o_spec = pl.BlockSpec((tm, tn), lambda i, j, k: (i, j))
    res_spec = None
    if res is not None:
        if res.shape[0] == 1:
            res_spec = pl.BlockSpec((1, tn), lambda i, j, k: (0, j))
        else:
            res_spec = pl.BlockSpec((tm, tn), lambda i, j, k: (i, j))
    return _mm(name, a, b, grid=grid, a_spec=a_spec, b_spec=b_spec, o_spec=o_spec,
               out_shape=(m, n), out_dtype=out_dtype, contract=contract, res=res, res_spec=res_spec)


def _blockwise(name, fn, ins, in_specs, outs, out_specs, grid, n_acc=0, acc_all=True):
    n_in, n_out = len(ins), len(outs)
    n_plain = n_out - n_acc

    def body(*refs):
        vals = fn(*[r[...] for r in refs[:n_in]])
        if not isinstance(vals, (tuple, list)):
            vals = (vals,)
        o_refs = refs[n_in:n_in + n_out]
        for r, v in zip(o_refs[:n_plain], vals[:n_plain]):
            r[...] = v.astype(r.dtype)
        if n_acc:
            if acc_all:
                first = functools.reduce(jnp.logical_and, [pl.program_id(d) == 0 for d in range(len(grid))])
            else:
                first = pl.program_id(len(grid) - 1) == 0

            @pl.when(first)
            def _():
                for r, v in zip(o_refs[n_plain:], vals[n_plain:]):
                    r[...] = v.astype(r.dtype)

            @pl.when(jnp.logical_not(first))
            def _():
                for r, v in zip(o_refs[n_plain:], vals[n_plain:]):
                    r[...] += v.astype(r.dtype)

    return pl.pallas_call(
        body, name=name, grid=grid, in_specs=in_specs, out_specs=out_specs,
        out_shape=[jax.ShapeDtypeStruct(s, d) for s, d in outs], compiler_params=_cparams(),
    )(*ins)


def _row_spec(t, c):
    return pl.BlockSpec((t, c), lambda i: (i, 0))


def _full_spec(shape):
    nd = len(shape)
    return pl.BlockSpec(tuple(shape), lambda *g: (0,) * nd)


def _rms(xf, w):
    return xf * lax.rsqrt(jnp.mean(xf * xf, axis=-1, keepdims=True) + RMS_EPS) * w


def _rms_bwd(xf, w, dy):
    _, vjp = jax.vjp(_rms, xf, w)
    return vjp(dy)


def _s5_disc(lr, li, ldt, bre, bim):
    dt = jnp.exp(ldt)
    mag = jnp.exp(lr * dt)
    ar = mag * jnp.cos(li * dt)
    ai = mag * jnp.sin(li * dt)
    nr, ni = ar - 1.0, ai
    den = lr * lr + li * li
    zr = (nr * lr + ni * li) / den
    zi = (ni * lr - nr * li) / den
    return ar, ai, zr * bre - zi * bim, zr * bim + zi * bre


def _s5_prep(lr, li, ldt, bre, bim):
    def body(lr_r, li_r, ldt_r, bre_r, bim_r, ar_r, ai_r, br_r, bi_r):
        ar, ai, br, bi = _s5_disc(lr_r[...], li_r[...], ldt_r[...], bre_r[...], bim_r[...])
        ar_r[...] = ar
        ai_r[...] = ai
        br_r[...] = br
        bi_r[...] = bi

    sd = jax.ShapeDtypeStruct
    return pl.pallas_call(
        body, name="s5_prep",
        out_shape=[sd(lr.shape, F32), sd(lr.shape, F32), sd(bre.shape, F32), sd(bre.shape, F32)],
        compiler_params=_cparams(),
    )(lr, li, ldt, bre, bim)


def _s5_prep_bwd(lr, li, ldt, bre, bim, dar, dai, dbr, dbi):
    def body(lr_r, li_r, ldt_r, bre_r, bim_r, dar_r, dai_r, dbr_r, dbi_r, o0, o1, o2, o3, o4):
        _, vjp = jax.vjp(_s5_disc, lr_r[...], li_r[...], ldt_r[...], bre_r[...], bim_r[...])
        g = vjp((dar_r[...], dai_r[...], dbr_r[...], dbi_r[...]))
        for o, v in zip((o0, o1, o2, o3, o4), g):
            o[...] = v

    sd = jax.ShapeDtypeStruct
    return pl.pallas_call(
        body, name="s5_prep_bwd",
        out_shape=[sd(lr.shape, F32), sd(li.shape, F32), sd(ldt.shape, F32), sd(bre.shape, F32), sd(bim.shape, F32)],
        compiler_params=_cparams(),
    )(lr, li, ldt, bre, bim, dar, dai, dbr, dbi)


SCAN_T = 256


def _scan_tables(ar, ai, tab_r, tab_i, sub, reverse):
    pr, pi = ar, ai
    for k in range(sub):
        row = sub - 1 - k if reverse else k
        tab_r[row:row + 1, :] = pr
        tab_i[row:row + 1, :] = pi
        pr, pi = ar * pr - ai * pi, ar * pi + ai * pr


def _pack_matrix(t_blk, dtype):
    sub = t_blk // SUBLANES
    dst = jnp.arange(t_blk)
    src = (dst % SUBLANES) * sub + dst // SUBLANES
    return (src[:, None] == jnp.arange(t_blk)[None, :]).astype(dtype)


def _permute_rows_f32(pm, x):
    hi = x.astype(BF16)
    r1 = x - hi.astype(F32)
    mid = r1.astype(BF16)
    lo = (r1 - mid.astype(F32)).astype(BF16)
    dot = lambda v: jnp.dot(pm, v, preferred_element_type=F32)
    return dot(hi) + dot(mid) + dot(lo)


def _scan_block(x, loc, ar, ai, st, tab_r, tab_i, sub, reverse):
    hb = STATE_BLOCK
    a8r = jnp.broadcast_to(ar, (SUBLANES, hb))
    a8i = jnp.broadcast_to(ai, (SUBLANES, hb))
    sr = jnp.zeros((SUBLANES, hb), F32)
    si = jnp.zeros((SUBLANES, hb), F32)
    steps = range(sub - 1, -1, -1) if reverse else range(sub)
    for t in steps:
        rows = slice(t * SUBLANES, (t + 1) * SUBLANES)
        sr, si = a8r * sr - a8i * si + x[rows, :hb], a8r * si + a8i * sr + x[rows, hb:]
        loc[rows, :hb] = sr
        loc[rows, hb:] = si
    cr, ci = st[0:1, :], st[1:2, :]
    far = 0 if reverse else sub - 1
    fr, fi = tab_r[far:far + 1, :], tab_i[far:far + 1, :]
    ent_r, ent_i = [None] * SUBLANES, [None] * SUBLANES
    for c in (range(SUBLANES - 1, -1, -1) if reverse else range(SUBLANES)):
        ent_r[c], ent_i[c] = cr, ci
        cr, ci = sr[c:c + 1, :] + (fr * cr - fi * ci), si[c:c + 1, :] + (fr * ci + fi * cr)
    st[0:1, :] = cr
    st[1:2, :] = ci
    c8r = jnp.concatenate(ent_r, axis=0)
    c8i = jnp.concatenate(ent_i, axis=0)
    out = []
    for t in range(sub):
        rows = slice(t * SUBLANES, (t + 1) * SUBLANES)
        tr, ti = tab_r[t:t + 1, :], tab_i[t:t + 1, :]
        out.append(jnp.concatenate([loc[rows, :hb] + (tr * c8r - ti * c8i), loc[rows, hb:] + (tr * c8i + ti * c8r)],
                                   axis=1))
    return jnp.concatenate(out, axis=0)


def _scan_scratch(t_blk, sub, hb):
    return [pltpu.VMEM((SUBLANES, hb), F32), pltpu.VMEM((sub, hb), F32), pltpu.VMEM((sub, hb), F32),
            pltpu.VMEM((t_blk, 2 * hb), F32)]


def _ssm_fwd(proj, wb, wc, a):
    seq = proj.shape[0]
    nj = wb.shape[0]
    w2 = 2 * STATE_BLOCK
    hb = STATE_BLOCK
    t_blk = min(SCAN_T, seq)
    sub = t_blk // SUBLANES
    pm = _pack_matrix(t_blk, BF16)

    def body(u_ref, wb_ref, wc_ref, a_ref, pm_ref, pmt_ref, s_ref, y_ref, st, tab_r, tab_i, loc):
        ar = a_ref[:, :hb]
        ai = a_ref[:, hb:]

        @pl.when(pl.program_id(1) == 0)
        def _():
            st[...] = jnp.zeros_like(st)
            _scan_tables(ar, ai, tab_r, tab_i, sub, False)

        up = jnp.dot(pm_ref[...], u_ref[...].astype(BF16), preferred_element_type=F32).astype(BF16)
        bu = jnp.dot(up, wb_ref[...], preferred_element_type=F32)
        s = _scan_block(bu, loc, ar, ai, st, tab_r, tab_i, sub, False)
        s_ref[...] = s
        yp = jnp.dot(s.astype(BF16), wc_ref[...], preferred_element_type=F32)
        y_ref[...] = _permute_rows_f32(pmt_ref[...], yp)

    sd = jax.ShapeDtypeStruct
    return pl.pallas_call(
        body, name="ssm_fwd", grid=(nj, seq // t_blk),
        in_specs=[pl.BlockSpec((t_blk, LANES), lambda j, i: (i, j)),
                  pl.BlockSpec((None, LANES, w2), lambda j, i: (j, 0, 0)),
                  pl.BlockSpec((None, w2, LANES), lambda j, i: (j, 0, 0)),
                  pl.BlockSpec((1, w2), lambda j, i: (0, j)),
                  _full_spec((t_blk, t_blk)), _full_spec((t_blk, t_blk))],
        out_specs=[pl.BlockSpec((t_blk, w2), lambda j, i: (i, j)), pl.BlockSpec((t_blk, LANES), lambda j, i: (i, j))],
        out_shape=[sd((seq, nj * w2), F32), sd((seq, nj * LANES), F32)],
        scratch_shapes=_scan_scratch(t_blk, sub, hb), compiler_params=_cparams(),
    )(proj, wb, wc, a, pm, pm.T)


def _ssm_bwd(dy, s, proj, du1, wb, wc, a):
    seq = dy.shape[0]
    nj = wb.shape[0]
    w2 = 2 * STATE_BLOCK
    hb = STATE_BLOCK
    t_blk = min(SCAN_T, seq)
    sub = t_blk // SUBLANES
    nb = seq // t_blk
    pm = _pack_matrix(t_blk, BF16)

    def body(dy_ref, s_ref, sprev_ref, u_ref, du1_ref, wb_ref, wc_ref, a_ref, pm_ref, pmt_ref,
             du_ref, dwb_ref, dwc_ref, da_ref, st, tab_r, tab_i, loc):
        ib = pl.program_id(1)
        ar = a_ref[:, :hb]
        ai = -a_ref[:, hb:]

        @pl.when(ib == 0)
        def _():
            st[...] = jnp.zeros_like(st)
            _scan_tables(ar, ai, tab_r, tab_i, sub, True)

        pmv = pm_ref[...]
        dyp = jnp.dot(pmv, dy_ref[...], preferred_element_type=F32).astype(BF16)
        up = jnp.dot(pmv, u_ref[...].astype(BF16), preferred_element_type=F32).astype(BF16)
        ds = lax.dot_general(dyp, wc_ref[...], (NT, ((), ())), preferred_element_type=F32)
        lam = _scan_block(ds, loc, ar, ai, st, tab_r, tab_i, sub, True)
        lamb = lam.astype(BF16)
        du = lax.dot_general(lamb, wb_ref[...], (NT, ((), ())), preferred_element_type=F32)
        du_ref[...] = (_permute_rows_f32(pmt_ref[...], du) + du1_ref[...]).astype(du_ref.dtype)
        sv = s_ref[...]
        dwb = lax.dot_general(up, lamb, (TN, ((), ())), preferred_element_type=F32)
        dwc = lax.dot_general(sv.astype(BF16), dyp, (TN, ((), ())), preferred_element_type=F32)

        prev_last = sprev_ref[SUBLANES - 1:SUBLANES, :]
        prev_last = jnp.where(ib == nb - 1, jnp.zeros_like(prev_last), prev_last)
        tail = sv[t_blk - SUBLANES:, :]
        sl = lax.broadcasted_iota(jnp.int32, tail.shape, 0)
        head = jnp.where(sl >= 1, pltpu.roll(tail, 1, 0), prev_last)
        s_sh = jnp.concatenate([head, sv[:t_blk - SUBLANES, :]], axis=0)
        lam_r, lam_i = lam[:, :hb], lam[:, hb:]
        sr_, si_ = s_sh[:, :hb], s_sh[:, hb:]
        dar = jnp.sum(lam_r * sr_ + lam_i * si_, axis=0, keepdims=True)
        dai = jnp.sum(lam_i * sr_ - lam_r * si_, axis=0, keepdims=True)
        contrib = jnp.concatenate([dar, dai], axis=1)

        @pl.when(ib == 0)
        def _():
            da_ref[...] = contrib
            dwb_ref[...] = dwb
            dwc_ref[...] = dwc

        @pl.when(ib != 0)
        def _():
            da_ref[...] += contrib
            dwb_ref[...] += dwb
            dwc_ref[...] += dwc

    blk = lambda j, i: (nb - 1 - i, j)
    prev_blk = lambda j, i: (jnp.maximum((nb - 1 - i) * sub - 1, 0), j)
    sd = jax.ShapeDtypeStruct
    return pl.pallas_call(
        body, name="ssm_bwd", grid=(nj, nb),
        in_specs=[pl.BlockSpec((t_blk, LANES), blk), pl.BlockSpec((t_blk, w2), blk),
                  pl.BlockSpec((SUBLANES, w2), prev_blk), pl.BlockSpec((t_blk, LANES), blk),
                  pl.BlockSpec((t_blk, LANES), blk),
                  pl.BlockSpec((None, LANES, w2), lambda j, i: (j, 0, 0)),
                  pl.BlockSpec((None, w2, LANES), lambda j, i: (j, 0, 0)),
                  pl.BlockSpec((1, w2), lambda j, i: (0, j)),
                  _full_spec((t_blk, t_blk)), _full_spec((t_blk, t_blk))],
        out_specs=[pl.BlockSpec((t_blk, LANES), blk),
                   pl.BlockSpec((None, LANES, w2), lambda j, i: (j, 0, 0)),
                   pl.BlockSpec((None, w2, LANES), lambda j, i: (j, 0, 0)),
                   pl.BlockSpec((1, w2), lambda j, i: (0, j))],
        out_shape=[sd((seq, nj * LANES), BF16), sd((nj, LANES, w2), F32), sd((nj, w2, LANES), F32),
                   sd((1, nj * w2), F32)],
        scratch_shapes=_scan_scratch(t_blk, sub, hb), compiler_params=_cparams(),
    )(dy, s, s, proj, du1, wb, wc, a, pm, pm.T)


def _rope128(x, cos, sa, sb):
    return x * cos + pltpu.roll(x, 96, 1) * sa + pltpu.roll(x, 32, 1) * sb


def _rope128_t(dy, cos, sa, sb):
    return dy * cos + pltpu.roll(dy * sa, 32, 1) + pltpu.roll(dy * sb, 96, 1)


ATT_BQ = 256


def _probs(qn, qp, kn, kp, r0, scale):
    s = lax.dot_general(qn, kn, (NT, ((), ())), preferred_element_type=F32)
    s = s + lax.dot_general(qp, kp, (NT, ((), ())), preferred_element_type=F32)
    s = s * scale
    row = r0 + lax.broadcasted_iota(jnp.int32, s.shape, 0)
    col = lax.broadcasted_iota(jnp.int32, s.shape, 1)
    s = jnp.where(col <= row, s, jnp.finfo(F32).min)
    m = jnp.max(s, axis=-1, keepdims=True)
    e = jnp.exp(s - m)
    return e / jnp.sum(e, axis=-1, keepdims=True)


def _attn_specs(seq):
    tab = pl.BlockSpec((seq, LANES), lambda h: (0, 0))
    return [pl.BlockSpec((None, seq, 256), lambda h: (h, 0, 0)), pl.BlockSpec((None, seq, 128), lambda h: (h, 0, 0)),
            pl.BlockSpec((None, seq, 128), lambda h: (h, 0, 1)), tab, tab, tab, tab]


def _attn_fwd(q_raw, kv, kpe, cos, sa, sb):
    nh, seq, _ = q_raw.shape
    bq = min(ATT_BQ, seq)
    scale = (QK_NOPE + QK_ROPE) ** -0.5

    def body(q_ref, kn_ref, v_ref, kp_ref, cos_ref, sa_ref, sb_ref, o_ref):
        for r0 in range(0, seq, bq):
            rows, kend = pl.ds(r0, bq), r0 + bq
            qn = q_ref[rows, :QK_NOPE].astype(BF16)
            qp = _rope128(q_ref[rows, QK_NOPE:], cos_ref[rows, :], sa_ref[rows, :], sb_ref[rows, :]).astype(BF16)
            p = _probs(qn, qp, kn_ref[:kend, :], kp_ref[:kend, :], r0, scale)
            o_ref[rows, :] = jnp.dot(p.astype(BF16), v_ref[:kend, :], preferred_element_type=F32)

    return pl.pallas_call(
        body, name="attn_fwd", grid=(nh,), in_specs=_attn_specs(seq),
        out_specs=pl.BlockSpec((seq, V_DIM), lambda h: (0, h)),
        out_shape=jax.ShapeDtypeStruct((seq, nh * V_DIM), F32), compiler_params=_cparams(),
    )(q_raw, kv, kv, kpe, cos, sa, sb)


def _attn_bwd(q_raw, kv, kpe, cos, sa, sb, do):
    nh, seq, _ = q_raw.shape
    bq = min(ATT_BQ, seq)
    scale = (QK_NOPE + QK_ROPE) ** -0.5

    def body(q_ref, kn_ref, v_ref, kp_ref, cos_ref, sa_ref, sb_ref, do_ref, dq_ref, dkv_ref, dkp_ref):
        dkv_ref[...] = jnp.zeros_like(dkv_ref)
        dkp_ref[...] = jnp.zeros_like(dkp_ref)
        for r0 in range(0, seq, bq):
            rows, kend = pl.ds(r0, bq), r0 + bq
            cos_b, sa_b, sb_b = cos_ref[rows, :], sa_ref[rows, :], sb_ref[rows, :]
            qn = q_ref[rows, :QK_NOPE].astype(BF16)
            qp = _rope128(q_ref[rows, QK_NOPE:], cos_b, sa_b, sb_b).astype(BF16)
            kn, v, kp = kn_ref[:kend, :], v_ref[:kend, :], kp_ref[:kend, :]
            p = _probs(qn, qp, kn, kp, r0, scale)
            dob = do_ref[rows, :].astype(BF16)
            dp = lax.dot_general(dob, v, (NT, ((), ())), preferred_element_type=F32)
            ds = p * (dp - jnp.sum(p * dp, axis=-1, keepdims=True)) * scale
            dsb = ds.astype(BF16)
            pb = p.astype(BF16)
            dq_ref[rows, :QK_NOPE] = jnp.dot(dsb, kn, preferred_element_type=F32).astype(dq_ref.dtype)
            dqp = jnp.dot(dsb, kp, preferred_element_type=F32)
            dq_ref[rows, QK_NOPE:] = _rope128_t(dqp, cos_b, sa_b, sb_b).astype(dq_ref.dtype)
            dkv_ref[:kend, :QK_NOPE] += lax.dot_general(dsb, qn, (TN, ((), ())), preferred_element_type=F32)
            dkv_ref[:kend, QK_NOPE:] += lax.dot_general(pb, dob, (TN, ((), ())), preferred_element_type=F32)
            dkp_ref[:kend, :] += lax.dot_general(dsb, qp, (TN, ((), ())), preferred_element_type=F32)

    sd = jax.ShapeDtypeStruct
    return pl.pallas_call(
        body, name="attn_bwd", grid=(nh,),
        in_specs=_attn_specs(seq) + [pl.BlockSpec((seq, V_DIM), lambda h: (0, h))],
        out_specs=[pl.BlockSpec((None, seq, 256), lambda h: (h, 0, 0)),
                   pl.BlockSpec((None, seq, 256), lambda h: (h, 0, 0)),
                   pl.BlockSpec((None, seq, 128), lambda h: (h, 0, 0))],
        out_shape=[sd((nh, seq, 256), BF16), sd((nh, seq, 256), F32), sd((nh, seq, 128), F32)],
        compiler_params=_cparams(),
    )(q_raw, kv, kv, kpe, cos, sa, sb, do)


def _conv3(a, w, b):
    rows = lax.broadcasted_iota(jnp.int32, a.shape, 0)
    a1 = jnp.where(rows >= 1, pltpu.roll(a, 1, 0), 0.0)
    a2 = jnp.where(rows >= 2, pltpu.roll(a, 2, 0), 0.0)
    return w[2:3] * a + w[1:2] * a1 + w[0:1] * a2 + b, a1, a2


def _conv_gate_fwd(a, cw, cb):
    half, _, seq, c = a.shape
    nc = c // LANES

    def fn(pair, wg, wv, bg, bv):
        gc, _, _ = _conv3(pair[0], wg, bg)
        vc, _, _ = _conv3(pair[1], wv, bv)
        return gc * jax.nn.sigmoid(gc) * vc

    def w_spec(off, r):
        return pl.BlockSpec((None, r, LANES), lambda k, j: (k + off, 0, j))

    return _blockwise(
        "conv_gate_fwd", fn, [a, cw, cw, cb, cb],
        [pl.BlockSpec((None, 2, seq, LANES), lambda k, j: (k, 0, 0, j)),
         w_spec(0, 3), w_spec(half, 3), w_spec(0, 1), w_spec(half, 1)],
        [((seq, half * c), BF16)], [pl.BlockSpec((seq, LANES), lambda k, j: (0, k * nc + j))],
        grid=(half, nc))[0]


def _conv_gate_bwd(a, cw, cb, dm):
    half, _, seq, c = a.shape
    nc = c // LANES

    def body(a_ref, wg_ref, wv_ref, bg_ref, bv_ref, dm_ref, da_ref, dw_ref, db_ref):
        dmv = dm_ref[...]
        rows = lax.broadcasted_iota(jnp.int32, dmv.shape, 0)
        ga, wg = a_ref[0], wg_ref[...]
        va, wv = a_ref[1], wv_ref[...]
        gc, g1, g2 = _conv3(ga, wg, bg_ref[...])
        vc, v1, v2 = _conv3(va, wv, bv_ref[...])
        sg = jax.nn.sigmoid(gc)
        dms = dmv * sg
        d_val = dms * gc
        d_gate = dms * vc * (1.0 + gc * (1.0 - sg))

        def back(r, dc, own, a1, a2, w):
            up1 = jnp.where(rows < seq - 1, pltpu.roll(dc, seq - 1, 0), 0.0)
            up2 = jnp.where(rows < seq - 2, pltpu.roll(dc, seq - 2, 0), 0.0)
            da_ref[r] = (w[2:3] * dc + w[1:2] * up1 + w[0:1] * up2).astype(da_ref.dtype)
            dw_ref[r, 0:1, :] = jnp.sum(dc * a2, axis=0, keepdims=True)
            dw_ref[r, 1:2, :] = jnp.sum(dc * a1, axis=0, keepdims=True)
            dw_ref[r, 2:3, :] = jnp.sum(dc * own, axis=0, keepdims=True)
            db_ref[r] = jnp.sum(dc, axis=0, keepdims=True)

        back(0, d_gate, ga, g1, g2, wg)
        back(1, d_val, va, v1, v2, wv)

    def w_spec(off, r):
        return pl.BlockSpec((None, r, LANES), lambda k, j: (k + off, 0, j))

    def pair_spec(r):
        return pl.BlockSpec((None, 2, r, LANES), lambda k, j: (k, 0, 0, j))

    sd = jax.ShapeDtypeStruct
    return pl.pallas_call(
        body, name="conv_gate_bwd", grid=(half, nc),
        in_specs=[pair_spec(seq), w_spec(0, 3), w_spec(half, 3), w_spec(0, 1), w_spec(half, 1),
                  pl.BlockSpec((seq, LANES), lambda k, j: (0, k * nc + j))],
        out_specs=[pair_spec(seq), pair_spec(3), pair_spec(1)],
        out_shape=[sd((half, 2, seq, c), BF16), sd((half, 2, 3, c), F32), sd((half, 2, 1, c), F32)],
        compiler_params=_cparams(),
    )(a, cw, cw, cb, cb, dm)


ROW_T = 256


def _local_step(x, positions, target, w, emit=lambda **grads: None):
    seq, d = x.shape
    t_row = min(ROW_T, seq)
    nrow = seq // t_row
    ssm_w = d // 2
    nj = ssm_w // LANES
    n_groups = ssm_w // SSM_GROUP
    nh = w["wuq"].shape[0]
    q_rank = w["wuq"].shape[1]
    kv_rank = w["wukv"].shape[1]
    ns = w["wup"].shape[0]
    c_ff = w["wup"].shape[2]
    in_pad = w["win"].shape[1]
    tm = min(512, seq)
    nm = seq // tm
    sw = 2 * STATE_BLOCK
    g1 = (nrow,)

    lr3 = w["lam_re"].reshape(n_groups, 1, SSM_STATE)
    li3 = w["lam_im"].reshape(n_groups, 1, SSM_STATE)
    ldt3 = w["log_dt"].reshape(n_groups, 1, 1)
    bt_re = jnp.swapaxes(w["b_re"].reshape(n_groups, SSM_STATE, SSM_GROUP), 1, 2)
    bt_im = jnp.swapaxes(w["b_im"].reshape(n_groups, SSM_STATE, SSM_GROUP), 1, 2)
    abar_re, abar_im, bbt_re, bbt_im = _s5_prep(lr3, li3, ldt3, bt_re, bt_im)
    eye = jnp.eye(GROUPS_PER_BLOCK, dtype=F32)

    def blockdiag_in(bb):
        t = bb.reshape(nj, GROUPS_PER_BLOCK, SSM_GROUP, SSM_STATE)
        return jnp.einsum("jghp,gk->jghkp", t, eye).reshape(nj, LANES, STATE_BLOCK)

    def blockdiag_in_t(dwb):
        t = dwb.reshape(nj, GROUPS_PER_BLOCK, SSM_GROUP, GROUPS_PER_BLOCK, SSM_STATE)
        return jnp.einsum("jghkp,gk->jghp", t, eye).reshape(n_groups, SSM_GROUP, SSM_STATE)

    def blockdiag_out(cc):
        t = cc.reshape(nj, GROUPS_PER_BLOCK, SSM_GROUP, SSM_STATE)
        return jnp.einsum("jghp,gk->jkpgh", t, eye).reshape(nj, STATE_BLOCK, LANES)

    def blockdiag_out_t(dwc):
        t = dwc.reshape(nj, GROUPS_PER_BLOCK, SSM_STATE, GROUPS_PER_BLOCK, SSM_GROUP)
        return jnp.einsum("jkpgh,gk->jghp", t, eye).reshape(n_groups, SSM_GROUP, SSM_STATE)

    c_re = w["c_re"].reshape(n_groups, SSM_GROUP, SSM_STATE)
    c_im = w["c_im"].reshape(n_groups, SSM_GROUP, SSM_STATE)
    wb = jnp.concatenate([blockdiag_in(bbt_re), blockdiag_in(bbt_im)], axis=2).astype(BF16)
    wc = jnp.concatenate([blockdiag_out(c_re), -blockdiag_out(c_im)], axis=1).astype(BF16)
    a_lay = jnp.concatenate([abar_re.reshape(nj, 1, STATE_BLOCK), abar_im.reshape(nj, 1, STATE_BLOCK)],
                            axis=1).reshape(1, nj * sw)

    attn_w = w["attn_norm"]
    hn = _blockwise("norm1", lambda xb, wv: _rms(xb, wv), [x, attn_w], [_row_spec(t_row, d), _full_spec((1, d))],
                    [((seq, d), BF16)], [_row_spec(t_row, d)], g1)[0]
    proj = _mm2d("proj", hn, w["win"], NN, F32, tn=640)

    s_all, ylin = _ssm_fwd(proj, wb, wc, a_lay)
    u_spec = pl.BlockSpec((t_row, ssm_w), lambda i: (i, 0))

    def ypre_fn(yl, ub, dsk):
        yp = yl + dsk * ub
        return yp, jax.nn.gelu(yp)

    y_pre, yg = _blockwise("ssm_gelu", ypre_fn, [ylin, proj, w["ssm_d"]],
                           [_row_spec(t_row, ssm_w), u_spec, _full_spec((1, ssm_w))],
                           [((seq, ssm_w), F32), ((seq, ssm_w), BF16)],
                           [_row_spec(t_row, ssm_w)] * 2, g1)
    z = _mm2d("ssm_glu", yg, w["wglu"], NN, F32, res=w["b_glu"])
    y_ssm = _blockwise("ssm_gate", lambda yp, zb: jax.nn.gelu(yp) * jax.nn.sigmoid(zb), [y_pre, z],
                       [_row_spec(t_row, ssm_w)] * 2, [((seq, ssm_w), F32)], [_row_spec(t_row, ssm_w)], g1)[0]

    cq_off, ckv_off, kpe_off = ssm_w, ssm_w + q_rank, ssm_w + q_rank + kv_rank
    c_q = proj[:, cq_off:ckv_off]
    c_kv = proj[:, ckv_off:kpe_off]
    kpe_raw = proj[:, kpe_off:kpe_off + LANES]
    pos_b = jnp.broadcast_to(positions.astype(F32)[:, None], (seq, LANES))
    inv_freq = ROPE_THETA ** (-jnp.arange(0, QK_ROPE, 2, dtype=F32) / QK_ROPE)
    inv128 = jnp.tile(inv_freq, 4).reshape(1, LANES)

    def mla_prep_fn(cq, ckv, kp, pb, inv, wq, wkv):
        ang = pb * inv
        lane = lax.broadcasted_iota(jnp.int32, ang.shape, 1)
        cs, sn = jnp.cos(ang), jnp.sin(ang)
        cos = jnp.where(lane < QK_ROPE, cs, 0.0)
        sa = jnp.where(lane < QK_ROPE // 2, -sn, 0.0)
        sb = jnp.where(jnp.logical_and(lane >= QK_ROPE // 2, lane < QK_ROPE), sn, 0.0)
        return _rms(cq, wq), _rms(ckv, wkv), _rope128(kp, cos, sa, sb), cos, sa, sb

    qn, kvn, kpe, cos_t, sa_t, sb_t = _blockwise(
        "mla_prep", mla_prep_fn, [c_q, c_kv, kpe_raw, pos_b, inv128, w["q_norm"], w["kv_norm"]],
        [_row_spec(t_row, q_rank), _row_spec(t_row, kv_rank), _row_spec(t_row, LANES), _row_spec(t_row, LANES),
         _full_spec((1, LANES)), _full_spec((1, q_rank)), _full_spec((1, kv_rank))],
        [((seq, q_rank), BF16), ((seq, kv_rank), BF16), ((seq, LANES), BF16)] + [((seq, LANES), F32)] * 3,
        [_row_spec(t_row, q_rank), _row_spec(t_row, kv_rank)] + [_row_spec(t_row, LANES)] * 4, g1)

    def head_mm(name, act, wh, out_dtype):
        kdim, ndim = wh.shape[1], wh.shape[2]
        return _mm(name, act, wh, grid=(nh, nm, 1),
                   a_spec=pl.BlockSpec((tm, kdim), lambda h, i, k: (i, 0)),
                   b_spec=pl.BlockSpec((None, kdim, ndim), lambda h, i, k: (h, 0, 0)),
                   o_spec=pl.BlockSpec((None, tm, ndim), lambda h, i, k: (h, i, 0)),
                   out_shape=(nh, seq, ndim), out_dtype=out_dtype)

    q_raw = head_mm("mla_q", qn, w["wuq"], F32)
    kv = head_mm("mla_kv", kvn, w["wukv"], BF16)
    y_mla = _attn_fwd(q_raw, kv, kpe, cos_t, sa_t, sb_t)
    mla_w = nh * V_DIM

    def outnorm_fn(ys, ym, ws, wm):
        return jnp.concatenate([_rms(ys, ws), _rms(ym, wm)], axis=1)

    ycat = _blockwise("out_norm", outnorm_fn, [y_ssm, y_mla, w["son"], w["mon"]],
                      [_row_spec(t_row, ssm_w), _row_spec(t_row, mla_w), _full_spec((1, ssm_w)), _full_spec((1, mla_w))],
                      [((seq, d), BF16)], [_row_spec(t_row, d)], g1)[0]
    h1 = _mm2d("out_proj", ycat, w["wout"], NN, F32, res=x)

    hn2 = _blockwise("norm2", lambda hb, wv: _rms(hb, wv), [h1, w["ffn_norm"]],
                     [_row_spec(t_row, d), _full_spec((1, d))], [((seq, d), BF16)], [_row_spec(t_row, d)], g1)[0]
    tku = d
    half = ns // 2
    a_ff = _mm("ffn_up", hn2, w["wup"], grid=(ns, nm, d // tku),
               a_spec=pl.BlockSpec((tm, tku), lambda s, i, k: (i, k)),
               b_spec=pl.BlockSpec((None, tku, c_ff), lambda s, i, k: (s, k, 0)),
               o_spec=pl.BlockSpec((None, None, tm, c_ff), lambda s, i, k: (s % half, s // half, i, 0)),
               out_shape=(half, 2, seq, c_ff), out_dtype=F32)
    cb3 = w["conv_b"].reshape(ns, 1, c_ff)
    m_ff = _conv_gate_fwd(a_ff, w["conv_w"], cb3)
    d_ff = half * c_ff
    wdn = w["wdown"]
    tnd = _tile(d, 512)
    tmx, tnx = min(1024, seq), _tile(d, 1024)
    h2 = _mm2d("ffn_down", m_ff, wdn, NN, F32, tk=d_ff, res=h1)

    def loss_fn(hb, tb, wv):
        def f(hh, ww):
            err = _rms(hh, ww) - tb
            return 0.5 * jnp.sum(jnp.mean(err * err, axis=-1))

        lossv, (dh, dw) = jax.value_and_grad(f, argnums=(0, 1))(hb, wv)
        return dh, dh, jnp.full((1, LANES), lossv, F32), dw

    fin_w = w["final_norm"].reshape(1, d)
    dh2, dh2b, loss_acc, g_final = _blockwise(
        "loss_head", loss_fn, [h2, target, fin_w], [_row_spec(t_row, d), _row_spec(t_row, d), _full_spec((1, d))],
        [((seq, d), F32), ((seq, d), BF16), ((1, LANES), F32), ((1, d), F32)],
        [_row_spec(t_row, d), _row_spec(t_row, d), _full_spec((1, LANES)), _full_spec((1, d))], g1, n_acc=2)
    loss = loss_acc[0, 0]

    dm = _mm2d("ffn_down_dx", dh2b, wdn, NT, F32, tn=c_ff)
    tks = seq
    g_wdown = _mm2d("ffn_down_dw", m_ff, dh2b, TN, BF16, tm=c_ff)
    emit(wdown=g_wdown)
    da_ff, g_convw2, g_convb2 = _conv_gate_bwd(a_ff, w["conv_w"], cb3, dm)
    g_convw = jnp.swapaxes(g_convw2, 0, 1).reshape(ns, 3, c_ff)
    g_convb = jnp.swapaxes(g_convb2, 0, 1).reshape(ns, 1, c_ff)
    g_wup = _mm("ffn_up_dw", hn2, da_ff, grid=(ns, d // tnd, seq // tks), contract=TN,
                a_spec=pl.BlockSpec((tks, tnd), lambda s, j, k: (k, j)),
                b_spec=pl.BlockSpec((None, None, tks, c_ff), lambda s, j, k: (s % half, s // half, k, 0)),
                o_spec=pl.BlockSpec((None, tnd, c_ff), lambda s, j, k: (s, j, 0)),
                out_shape=(ns, d, c_ff), out_dtype=BF16)
    emit(wup=g_wup, conv_w=g_convw)
    dhn2 = _mm("ffn_up_dx", da_ff, w["wup"], grid=(seq // tmx, d // tnx, ns), contract=NT,
               a_spec=pl.BlockSpec((None, None, tmx, c_ff), lambda i, j, s: (s % half, s // half, i, 0)),
               b_spec=pl.BlockSpec((None, tnx, c_ff), lambda i, j, s: (s, j, 0)),
               o_spec=pl.BlockSpec((tmx, tnx), lambda i, j, s: (i, j)),
               out_shape=(seq, d), out_dtype=F32)

    def norm_bwd_fn(hb, dres, dn, wv):
        dx_, dw_ = _rms_bwd(hb, wv, dn)
        dtot = dres + dx_
        return dtot, dtot, dw_

    dh1, dh1b, g_ffn_norm = _blockwise(
        "norm2_bwd", norm_bwd_fn, [h1, dh2, dhn2, w["ffn_norm"]],
        [_row_spec(t_row, d)] * 3 + [_full_spec((1, d))],
        [((seq, d), F32), ((seq, d), BF16), ((1, d), F32)],
        [_row_spec(t_row, d), _row_spec(t_row, d), _full_spec((1, d))], g1, n_acc=1)

    dycat = _mm2d("out_proj_dx", dh1b, w["wout"], NT, F32)
    g_wout = _mm2d("out_proj_dw", ycat, dh1b, TN, BF16)

    def outnorm_bwd_fn(ys, ym, dyc, ws, wm):
        dys, dws = _rms_bwd(ys, ws, dyc[:, :ssm_w])
        dym, dwm = _rms_bwd(ym, wm, dyc[:, ssm_w:])
        return dys, dym, dws, dwm

    dy_ssm, dy_mla, g_son, g_mon = _blockwise(
        "out_norm_bwd", outnorm_bwd_fn, [y_ssm, y_mla, dycat, w["son"], w["mon"]],
        [_row_spec(t_row, ssm_w), _row_spec(t_row, mla_w), _row_spec(t_row, d), _full_spec((1, ssm_w)),
         _full_spec((1, mla_w))],
        [((seq, ssm_w), F32), ((seq, mla_w), F32), ((1, ssm_w), F32), ((1, mla_w), F32)],
        [_row_spec(t_row, ssm_w), _row_spec(t_row, mla_w), _full_spec((1, ssm_w)), _full_spec((1, mla_w))],
        g1, n_acc=2)

    def gate_bwd1_fn(dy, yp, zb):
        ygv = jax.nn.gelu(yp)
        sg = jax.nn.sigmoid(zb)
        dz = dy * ygv * sg * (1.0 - sg)
        return dz, jnp.sum(dz, axis=0, keepdims=True)

    dz, g_bglu = _blockwise("ssm_gate_bwd", gate_bwd1_fn, [dy_ssm, y_pre, z], [_row_spec(t_row, ssm_w)] * 3,
                            [((seq, ssm_w), BF16), ((1, ssm_w), F32)],
                            [_row_spec(t_row, ssm_w), _full_spec((1, ssm_w))], g1, n_acc=1)
    dyg2 = _mm2d("ssm_glu_dx", dz, w["wglu"], NT, F32)
    g_wglu = _mm2d("ssm_glu_dw", yg, dz, TN, BF16)

    def gelu_bwd_fn(dy, yp, zb, dg2, ub, dsk):
        dyg = dy * jax.nn.sigmoid(zb) + dg2
        _, vjp = jax.vjp(jax.nn.gelu, yp)
        dyp = vjp(dyg)[0]
        return dyp, dyp * dsk, jnp.sum(dyp * ub, axis=0, keepdims=True)

    dy_pre, du1, g_ssmd = _blockwise(
        "ssm_gelu_bwd", gelu_bwd_fn, [dy_ssm, y_pre, z, dyg2, proj, w["ssm_d"]],
        [_row_spec(t_row, ssm_w)] * 4 + [u_spec, _full_spec((1, ssm_w))],
        [((seq, ssm_w), BF16), ((seq, ssm_w), F32), ((1, ssm_w), F32)],
        [_row_spec(t_row, ssm_w), _row_spec(t_row, ssm_w), _full_spec((1, ssm_w))], g1, n_acc=1)
    dq_raw, dkv, dkp_h = _attn_bwd(q_raw, kv, kpe, cos_t, sa_t, sb_t, dy_mla)

    def head_mm_dx(name, dact, wh):
        kdim, ndim = wh.shape[1], wh.shape[2]
        return _mm(name, dact, wh, grid=(nm, 1, nh), contract=NT,
                   a_spec=pl.BlockSpec((None, tm, ndim), lambda i, j, h: (h, i, 0)),
                   b_spec=pl.BlockSpec((None, kdim, ndim), lambda i, j, h: (h, 0, 0)),
                   o_spec=pl.BlockSpec((tm, kdim), lambda i, j, h: (i, 0)),
                   out_shape=(seq, kdim), out_dtype=F32)

    def head_mm_dw(name, act, dact):
        kdim, ndim = act.shape[1], dact.shape[2]
        return _mm(name, act, dact, grid=(nh, 1, seq // tks), contract=TN,
                   a_spec=pl.BlockSpec((tks, kdim), lambda h, j, k: (k, 0)),
                   b_spec=pl.BlockSpec((None, tks, ndim), lambda h, j, k: (h, k, 0)),
                   o_spec=pl.BlockSpec((None, kdim, ndim), lambda h, j, k: (h, 0, 0)),
                   out_shape=(nh, kdim, ndim), out_dtype=BF16)

    g_wuq = head_mm_dw("mla_q_dw", qn, dq_raw)
    g_wukv = head_mm_dw("mla_kv_dw", kvn, dkv)
    emit(wout=g_wout, wuq=g_wuq, wukv=g_wukv, wglu=g_wglu)
    dqn = head_mm_dx("mla_q_dx", dq_raw, w["wuq"])
    dkvn = head_mm_dx("mla_kv_dx", dkv, w["wukv"])

    du, dwb, dwc, da_lay = _ssm_bwd(dy_pre, s_all, proj, du1, wb, wc, a_lay)
    g_c_re = blockdiag_out_t(dwc[:, :STATE_BLOCK, :])
    g_c_im = -blockdiag_out_t(dwc[:, STATE_BLOCK:, :])
    dbbt_re = blockdiag_in_t(dwb[:, :, :STATE_BLOCK])
    dbbt_im = blockdiag_in_t(dwb[:, :, STATE_BLOCK:])
    da3 = da_lay.reshape(nj, 2, STATE_BLOCK)
    dabar_re = da3[:, 0, :].reshape(n_groups, 1, SSM_STATE)
    dabar_im = da3[:, 1, :].reshape(n_groups, 1, SSM_STATE)
    g_lr3, g_li3, g_ldt3, g_bt_re, g_bt_im = _s5_prep_bwd(lr3, li3, ldt3, bt_re, bt_im,
                                                           dabar_re, dabar_im, dbbt_re, dbbt_im)

    def mla_prep_bwd_fn(cq, ckv, dqn_b, dkvn_b, dkp_b, cos, sa, sb, wq, wkv):
        dcq, dwq = _rms_bwd(cq, wq, dqn_b)
        dckv, dwkv = _rms_bwd(ckv, wkv, dkvn_b)
        dkp_sum = dkp_b[0]
        for h in range(1, nh):
            dkp_sum = dkp_sum + dkp_b[h]
        return dcq, dckv, _rope128_t(dkp_sum, cos, sa, sb), dwq, dwkv

    dc_q, dc_kv, dkpe_raw, g_qnorm, g_kvnorm = _blockwise(
        "mla_prep_bwd", mla_prep_bwd_fn, [c_q, c_kv, dqn, dkvn, dkp_h, cos_t, sa_t, sb_t, w["q_norm"], w["kv_norm"]],
        [_row_spec(t_row, q_rank), _row_spec(t_row, kv_rank), _row_spec(t_row, q_rank), _row_spec(t_row, kv_rank),
         pl.BlockSpec((nh, t_row, LANES), lambda i: (0, i, 0))] + [_row_spec(t_row, LANES)] * 3
        + [_full_spec((1, q_rank)), _full_spec((1, kv_rank))],
        [((seq, q_rank), BF16), ((seq, kv_rank), BF16), ((seq, LANES), BF16), ((1, q_rank), F32), ((1, kv_rank), F32)],
        [_row_spec(t_row, q_rank), _row_spec(t_row, kv_rank), _row_spec(t_row, LANES), _full_spec((1, q_rank)),
         _full_spec((1, kv_rank))], g1, n_acc=2)

    dproj = jnp.concatenate([du, dc_q, dc_kv, dkpe_raw], axis=1)
    g_win = _mm2d("proj_dw", hn, dproj, TN, BF16, tn=640)
    emit(win=g_win)
    dhn = _mm2d("proj_dx", dproj, w["win"], NT, F32)

    def norm1_bwd_fn(xb, dres, dn, wv):
        dx_, dw_ = _rms_bwd(xb, wv, dn)
        return dres + dx_, dw_

    grad_x, g_attn_norm = _blockwise(
        "norm1_bwd", norm1_bwd_fn, [x, dh1, dhn, attn_w], [_row_spec(t_row, d)] * 3 + [_full_spec((1, d))],
        [((seq, d), F32), ((1, d), F32)], [_row_spec(t_row, d), _full_spec((1, d))], g1, n_acc=1)

    grads = dict(
        attn_norm=g_attn_norm, win=g_win, lam_re=g_lr3, lam_im=g_li3, log_dt=g_ldt3,
        b_re=jnp.swapaxes(g_bt_re, 1, 2), b_im=jnp.swapaxes(g_bt_im, 1, 2), c_re=g_c_re, c_im=g_c_im,
        ssm_d=g_ssmd, wglu=g_wglu, b_glu=g_bglu, q_norm=g_qnorm, wuq=g_wuq, kv_norm=g_kvnorm, wukv=g_wukv,
        son=g_son, mon=g_mon, wout=g_wout, ffn_norm=g_ffn_norm, wup=g_wup, conv_w=g_convw, conv_b=g_convb,
        wdown=g_wdown, final_norm=g_final)
    return loss, grad_x, grads


def _mesh_pos():
    return lax.axis_index("x"), lax.axis_index("y"), lax.axis_index("c")


def _handshake_all():
    x, y, c = _mesh_pos()
    barrier = pltpu.get_barrier_semaphore()
    for k in range(1, N_DEV):
        peer = (1 - x if k & 4 else x, 1 - y if k & 2 else y, 1 - c if k & 1 else c)
        pl.semaphore_signal(barrier, inc=1, device_id=peer, device_id_type=MESH)
    pl.semaphore_wait(barrier, N_DEV - 1)


def _comm_call(name, body, n, out_shape, ins, collective_id, after=None):
    sems = [pltpu.SemaphoreType.DMA((7 * n,)), pltpu.SemaphoreType.DMA((7 * n,)), pltpu.SemaphoreType.DMA((n,))]
    if collective_id is None:
        any_spec = pl.BlockSpec(memory_space=pl.ANY)
        return pl.pallas_call(body, name=name, out_shape=out_shape, in_specs=[any_spec] * n,
                              out_specs=[any_spec] * n, scratch_shapes=sems)(*ins)
    seq_body = body
    if after is not None:
        ins = list(ins) + [after]

        def seq_body(*refs):
            body(*refs[:n], *refs[n + 1:])

    return pl.kernel(seq_body, name=name, out_type=out_shape,
                     mesh=plsc.ScalarSubcoreMesh(axis_name="seq", num_cores=1), scratch_types=sems,
                     compiler_params=pltpu.CompilerParams(collective_id=collective_id))(*ins)


def _all_gather(name, xs, collective_id=None, after=None):
    n = len(xs)

    def body(*refs):
        x_refs, o_refs = refs[:n], refs[n:2 * n]
        send_sems, recv_sems, local_sems = refs[2 * n:]
        if collective_id is not None:
            _handshake_all()
        x, y, c = _mesh_pos()
        me, sibling = (x, y, c), (x, y, 1 - c)
        chips = [(1 - x, y), (x, 1 - y), (1 - x, 1 - y)]

        def slot(o_ref, px, py, pc):
            return o_ref.at[4 * px + 2 * py + pc]

        def copy(t, k, block, to, src=None):
            dst = slot(o_refs[t], *block)
            return pltpu.make_async_remote_copy(
                src_ref=dst if src is None else src, dst_ref=dst,
                send_sem=send_sems.at[7 * t + k], recv_sem=recv_sems.at[7 * t + k],
                device_id=to, device_id_type=MESH)

        started = []
        for t in range(n):
            mine = pltpu.make_async_copy(x_refs[t], slot(o_refs[t], *me), local_sems.at[t])
            mine.start()
            started.append(mine)
        first = []
        for t in range(n):
            first.append(copy(t, 0, me, sibling, src=x_refs[t]))
            first += [copy(t, 1 + j, me, (*chip, c), src=x_refs[t]) for j, chip in enumerate(chips)]
        for cp in first:
            cp.start()
        passed = []
        for j, chip in enumerate(chips):
            for t in range(n):
                copy(t, 1 + j, (*chip, c), me).wait_recv()
                fwd = copy(t, 4 + j, (*chip, c), sibling)
                fwd.start()
                passed.append(fwd)
        for t in range(n):
            copy(t, 0, sibling, me).wait_recv()
            for j, chip in enumerate(chips):
                copy(t, 4 + j, (*chip, 1 - c), me).wait_recv()
        for cp in first + passed:
            cp.wait_send()
        for mine in started:
            mine.wait()

    out_shape = [jax.ShapeDtypeStruct((N_DEV,) + v.shape, v.dtype) for v in xs]
    return _comm_call(name, body, n, out_shape, xs, collective_id, after)


def _exchange_partials(name, gs, collective_id=None, after=None):
    n = len(gs)

    def body(*refs):
        g_refs, o_refs = refs[:n], refs[n:2 * n]
        send_sems, recv_sems, local_sems = refs[2 * n:]
        if collective_id is not None:
            _handshake_all()
        x, y, c = _mesh_pos()
        me_idx = 4 * x + 2 * y + c
        copies = []
        for t in range(n):
            mine = pltpu.make_async_copy(g_refs[t].at[me_idx], o_refs[t].at[me_idx], local_sems.at[t])
            mine.start()
            copies.append(mine)
        remote = []
        for k in range(1, N_DEV):
            px = 1 - x if k & 4 else x
            py = 1 - y if k & 2 else y
            pc = 1 - c if k & 1 else c
            p_idx = 4 * px + 2 * py + pc
            for t in range(n):
                cp = pltpu.make_async_remote_copy(
                    src_ref=g_refs[t].at[p_idx], dst_ref=o_refs[t].at[me_idx],
                    send_sem=send_sems.at[7 * t + k - 1], recv_sem=recv_sems.at[7 * t + k - 1],
                    device_id=(px, py, pc), device_id_type=MESH)
                cp.start()
                landing = pltpu.make_async_remote_copy(
                    src_ref=g_refs[t].at[p_idx], dst_ref=o_refs[t].at[p_idx],
                    send_sem=send_sems.at[7 * t + k - 1], recv_sem=recv_sems.at[7 * t + k - 1],
                    device_id=(px, py, pc), device_id_type=MESH)
                remote.append((cp, landing))
        for cp, landing in remote:
            landing.wait_recv()
        for cp, landing in remote:
            cp.wait_send()
        for mine in copies:
            mine.wait()

    out_shape = [jax.ShapeDtypeStruct(v.shape, v.dtype) for v in gs]
    return _comm_call(name, body, n, out_shape, gs, collective_id, after)


ADAM_BLOCK_ELEMS = 128 * 1024


def _adamw_sum(name, parts, wv, mv, vv):
    npart, r, c = parts.shape
    tr = r
    if r * c > ADAM_BLOCK_ELEMS and r % SUBLANES == 0:
        tr = SUBLANES
        while r % (tr * 2) == 0 and tr * 2 * c <= ADAM_BLOCK_ELEMS:
            tr *= 2
    bc1 = 1.0 - ADAM_B1 ** ADAM_STEP
    bc2 = 1.0 - ADAM_B2 ** ADAM_STEP

    def fn(pb, wb_, mb, vb):
        g = pb[0].astype(F32)
        for j in range(1, npart):
            g = g + pb[j].astype(F32)
        m_new = ADAM_B1 * mb + (1.0 - ADAM_B1) * g
        v_new = ADAM_B2 * vb + (1.0 - ADAM_B2) * (g * g)
        m_hat = m_new / bc1
        v_hat = v_new / bc2
        delta = -ADAM_LR * (m_hat / (jnp.sqrt(v_hat) + ADAM_EPS) + ADAM_WD * wb_)
        return g, delta, m_new, v_new

    row = pl.BlockSpec((tr, c), lambda i: (i, 0))
    return _blockwise(name, fn, [parts, wv, mv, vv],
                      [pl.BlockSpec((npart, tr, c), lambda i: (0, i, 0)), row, row, row],
                      [((r, c), F32)] * 4, [row] * 4, (r // tr,))


_SMALL = ["attn_norm", "lam_re", "lam_im", "log_dt", "b_re", "b_im", "c_re", "c_im", "ssm_d", "b_glu",
          "q_norm", "kv_norm", "son", "mon", "ffn_norm", "conv_b", "final_norm"]
_BIG = ["win", "wglu", "wuq", "wukv", "wout", "wup", "wdown", "conv_w"]
_ORDER = ["attn_norm", "win", "lam_re", "lam_im", "log_dt", "b_re", "b_im", "c_re", "c_im", "ssm_d", "wglu",
          "b_glu", "q_norm", "wuq", "kv_norm", "wukv", "son", "mon", "wout", "ffn_norm", "wup", "conv_w",
          "conv_b", "wdown", "final_norm"]


def _pack(arrs):
    flat = jnp.concatenate([a.reshape(-1).astype(F32) for a in arrs])
    pad = (-flat.shape[0]) % (LANES * LANES)
    return jnp.pad(flat, (0, pad)).reshape(-1, LANES)


def _unpack(packed, shapes):
    flat = packed.reshape(-1)
    out, off = [], 0
    for s in shapes:
        n = math.prod(s)
        out.append(flat[off:off + n].reshape(s))
        off += n
    return out


def kernel(x, positions, attn_norm_w, w_in, ssm_lambda_re, ssm_lambda_im, ssm_log_dt, ssm_b_re, ssm_b_im, ssm_c_re, ssm_c_im, ssm_d, ssm_w_glu, ssm_b_glu, mla_q_norm_w, mla_w_uq, mla_kv_norm_w, mla_w_ukv, ssm_out_norm_w, mla_out_norm_w, w_out, ffn_norm_w, ffn_w_up, ffn_conv_w, ffn_conv_b, ffn_w_down, final_norm_w, loss_target, m_attn_norm_w, m_w_in, m_ssm_lambda_re, m_ssm_lambda_im, m_ssm_log_dt, m_ssm_b_re, m_ssm_b_im, m_ssm_c_re, m_ssm_c_im, m_ssm_d, m_ssm_w_glu, m_ssm_b_glu, m_mla_q_norm_w, m_mla_w_uq, m_mla_kv_norm_w, m_mla_w_ukv, m_ssm_out_norm_w, m_mla_out_norm_w, m_w_out, m_ffn_norm_w, m_ffn_w_up, m_ffn_conv_w, m_ffn_conv_b, m_ffn_w_down, m_final_norm_w, v_attn_norm_w, v_w_in, v_ssm_lambda_re, v_ssm_lambda_im, v_ssm_log_dt, v_ssm_b_re, v_ssm_b_im, v_ssm_c_re, v_ssm_c_im, v_ssm_d, v_ssm_w_glu, v_ssm_b_glu, v_mla_q_norm_w, v_mla_w_uq, v_mla_kv_norm_w, v_mla_w_ukv, v_ssm_out_norm_w, v_mla_out_norm_w, v_w_out, v_ffn_norm_w, v_ffn_w_up, v_ffn_conv_w, v_ffn_conv_b, v_ffn_w_down, v_final_norm_w):
    wts = dict(attn_norm=attn_norm_w, win=w_in, lam_re=ssm_lambda_re, lam_im=ssm_lambda_im, log_dt=ssm_log_dt,
               b_re=ssm_b_re, b_im=ssm_b_im, c_re=ssm_c_re, c_im=ssm_c_im, ssm_d=ssm_d, wglu=ssm_w_glu,
               b_glu=ssm_b_glu, q_norm=mla_q_norm_w, wuq=mla_w_uq, kv_norm=mla_kv_norm_w, wukv=mla_w_ukv,
               son=ssm_out_norm_w, mon=mla_out_norm_w, wout=w_out, ffn_norm=ffn_norm_w, wup=ffn_w_up,
               conv_w=ffn_conv_w, conv_b=ffn_conv_b, wdown=ffn_w_down, final_norm=final_norm_w)
    moms = dict(zip(_ORDER, [m_attn_norm_w, m_w_in, m_ssm_lambda_re, m_ssm_lambda_im, m_ssm_log_dt, m_ssm_b_re,
                             m_ssm_b_im, m_ssm_c_re, m_ssm_c_im, m_ssm_d, m_ssm_w_glu, m_ssm_b_glu, m_mla_q_norm_w,
                             m_mla_w_uq, m_mla_kv_norm_w, m_mla_w_ukv, m_ssm_out_norm_w, m_mla_out_norm_w, m_w_out,
                             m_ffn_norm_w, m_ffn_w_up, m_ffn_conv_w, m_ffn_conv_b, m_ffn_w_down, m_final_norm_w]))
    vels = dict(zip(_ORDER, [v_attn_norm_w, v_w_in, v_ssm_lambda_re, v_ssm_lambda_im, v_ssm_log_dt, v_ssm_b_re,
                             v_ssm_b_im, v_ssm_c_re, v_ssm_c_im, v_ssm_d, v_ssm_w_glu, v_ssm_b_glu, v_mla_q_norm_w,
                             v_mla_w_uq, v_mla_kv_norm_w, v_mla_w_ukv, v_ssm_out_norm_w, v_mla_out_norm_w, v_w_out,
                             v_ffn_norm_w, v_ffn_w_up, v_ffn_conv_w, v_ffn_conv_b, v_ffn_w_down, v_final_norm_w]))
    seq, d = x.shape[1], x.shape[2]
    in_width = w_in.shape[2]
    in_pad = -(-in_width // LANES) * LANES
    q_cols = mla_w_uq.shape[2]
    q_pad = 2 * LANES

    (win_g,) = _all_gather("gather_w_in", [jnp.pad(w_in[0], ((0, 0), (0, in_pad - in_width))).astype(BF16)])
    wglu_g, wuq_g, wukv_g, wout_g, convw_g = _all_gather(
        "gather_mix", [ssm_w_glu[0].astype(BF16), jnp.pad(mla_w_uq[0], ((0, 0), (0, q_pad - q_cols))).astype(BF16),
                       mla_w_ukv[0].astype(BF16), w_out[0].astype(BF16), ffn_conv_w[0]], collective_id=0)
    (wup_g,) = _all_gather("gather_ffn_up", [ffn_w_up[0].astype(BF16)], collective_id=1)
    (wdown_g,) = _all_gather("gather_ffn_down", [ffn_w_down[0].astype(BF16)], collective_id=2)
    ns = N_DEV
    c_ff = wup_g.shape[2]
    w = dict(
        attn_norm=attn_norm_w, win=win_g.reshape(d, in_pad), lam_re=ssm_lambda_re, lam_im=ssm_lambda_im,
        log_dt=ssm_log_dt, b_re=ssm_b_re, b_im=ssm_b_im, c_re=ssm_c_re, c_im=ssm_c_im, ssm_d=ssm_d,
        wglu=wglu_g.reshape(d // 2, d // 2), b_glu=ssm_b_glu, q_norm=mla_q_norm_w, wuq=wuq_g,
        kv_norm=mla_kv_norm_w, wukv=wukv_g, son=ssm_out_norm_w, mon=mla_out_norm_w, wout=wout_g.reshape(d, d),
        ffn_norm=ffn_norm_w, wup=wup_g, conv_w=convw_g, conv_b=ffn_conv_b,
        wdown=wdown_g.reshape(ns // 2 * c_ff, d), final_norm=final_norm_w)

    shard_layout = dict(
        win=lambda a: a[:, :in_width].reshape(N_DEV, d // N_DEV, in_width),
        wglu=lambda a: a.reshape(N_DEV, d // 2 // N_DEV, d // 2),
        wuq=lambda a: a[:, :, :q_cols], wukv=lambda a: a, wout=lambda a: a.reshape(N_DEV, d // N_DEV, d),
        wup=lambda a: a, wdown=lambda a: a.reshape(N_DEV, c_ff // 2, d), conv_w=lambda a: a)
    recv = {}
    next_id = [3]

    last = [None]

    def exchange(**grads):
        names = list(grads)
        got = _exchange_partials("exchange_" + "_".join(names), [shard_layout[k](grads[k]) for k in names],
                                 collective_id=next_id[0], after=last[0])
        next_id[0] += 1
        last[0] = got[-1]
        recv.update(zip(names, got))

    loss_part, grad_x, g = _local_step(x[0], positions[0], loss_target[0], w, emit=exchange)
    loss = lax.psum(loss_part, ("x", "y", "c"))
    small_shapes = [wts[k].shape for k in _SMALL]
    small_part = _pack([g[k] for k in _SMALL])
    small_all = _all_gather("gather_small_grads", [small_part], collective_id=next_id[0], after=last[0])[0]

    out = {}
    for k in _BIG:
        shp = wts[k].shape
        r, c = shp[-2], shp[-1]
        res = _adamw_sum("adamw_" + k, recv[k].reshape(N_DEV, r, c), wts[k].reshape(r, c),
                         moms[k].reshape(r, c), vels[k].reshape(r, c))
        out[k] = [a.reshape(shp) for a in res]
    sw_ = _pack([wts[k] for k in _SMALL])
    sm_ = _pack([moms[k] for k in _SMALL])
    sv_ = _pack([vels[k] for k in _SMALL])
    res = _adamw_sum("adamw_small", small_all, sw_, sm_, sv_)
    unpacked = [_unpack(a, small_shapes) for a in res]
    for i, k in enumerate(_SMALL):
        out[k] = [u[i] for u in unpacked]

    grad_x = grad_x.reshape(x.shape)
    return (loss, grad_x, *[out[k][0] for k in _ORDER], *[out[k][1] for k in _ORDER],
            *[out[k][2] for k in _ORDER], *[out[k][3] for k in _ORDER])
```

```python
import functools
import math

import jax
import jax.numpy as jnp
from jax import lax
from jax.experimental import pallas as pl
from jax.experimental.pallas import tpu as pltpu
from jax.experimental.pallas import tpu_sc as plsc

F32 = jnp.float32
BF16 = jnp.bfloat16
MESH = pl.DeviceIdType.MESH

N_DEV = 8
LANES = 128
SUBLANES = 8
VMEM_LIMIT = 48 * 1024 * 1024

SSM_GROUP = 16
SSM_STATE = 64
GROUPS_PER_BLOCK = LANES // SSM_GROUP
STATE_BLOCK = GROUPS_PER_BLOCK * SSM_STATE
QK_NOPE = 128
QK_ROPE = 64
V_DIM = 128
ROPE_THETA = 10000.0
RMS_EPS = 1e-6

ADAM_LR = 0.001
ADAM_B1 = 0.9
ADAM_B2 = 0.999
ADAM_EPS = 1e-08
ADAM_WD = 0.01
ADAM_STEP = 10

NN = ((1,), (0,))
NT = ((1,), (1,))
TN = ((0,), (0,))


def _cparams():
    return pltpu.CompilerParams(vmem_limit_bytes=VMEM_LIMIT)


def _tile(n, want):
    if n <= want:
        return n
    t = (want // LANES) * LANES
    while t >= LANES:
        if n % t == 0:
            return t
        t -= LANES
    return n


def _mm(name, a, b, *, grid, a_spec, b_spec, o_spec, out_shape, out_dtype, contract=NN,
        res=None, res_spec=None):
    nk = grid[-1]
    kaxis = len(grid) - 1
    acc_shape = tuple(d for d in o_spec.block_shape if d is not None)

    def body(*refs):
        a_ref, b_ref = refs[:2]
        r_ref = None if res is None else refs[2]
        o_ref = refs[2 if res is None else 3]
        part = lax.dot_general(a_ref[...].astype(BF16), b_ref[...].astype(BF16),
                               (contract, ((), ())), preferred_element_type=F32)
        if nk == 1:
            if r_ref is not None:
                part = part + r_ref[...].astype(F32)
            o_ref[...] = part.astype(o_ref.dtype)
            return
        acc = refs[-1]
        k = pl.program_id(kaxis)

        @pl.when(k == 0)
        def _():
            acc[...] = part

        @pl.when(k != 0)
        def _():
            acc[...] += part

        @pl.when(k == nk - 1)
        def _():
            r = acc[...]
            if r_ref is not None:
                r = r + r_ref[...].astype(F32)
            o_ref[...] = r.astype(o_ref.dtype)

    ins = [a, b] + ([] if res is None else [res])
    in_specs = [a_spec, b_spec] + ([] if res is None else [res_spec])
    return pl.pallas_call(
        body, name=name, grid=grid, in_specs=in_specs, out_specs=o_spec,
        out_shape=jax.ShapeDtypeStruct(out_shape, out_dtype),
        scratch_shapes=[pltpu.VMEM(acc_shape, F32)] if nk > 1 else [], compiler_params=_cparams(),
    )(*ins)


def _mm2d(name, a, b, contract, out_dtype, tm=1024, tn=1024, tk=2048, res=None):
    if contract == NN:
        (m, kk), n = a.shape, b.shape[1]
    elif contract == NT:
        (m, kk), n = a.shape, b.shape[0]
    else:
        (kk, m), n = a.shape, b.shape[1]
    tm, tn, tk = _tile(m, tm), _tile(n, tn), _tile(kk, tk)
    grid = (m // tm, n // tn, kk // tk)
    if contract == TN:
        a_spec = pl.BlockSpec((tk, tm), lambda i, j, k: (k, i))
    else:
        a_spec = pl.BlockSpec((tm, tk), lambda i, j, k: (i, k))
    if contract == NT:
        b_spec = pl.BlockSpec((tn, tk), lambda i, j, k: (j, k))
    else:
        b_spec = pl.BlockSpec((tk, tn), lambda i, j, k: (k, j))
    o_spec = pl.BlockSpec((tm, tn), lambda i, j, k: (i, j))
    res_spec = None
    if res is not None:
        if res.shape[0] == 1:
            res_spec = pl.BlockSpec((1, tn), lambda i, j, k: (0, j))
        else:
            res_spec = pl.BlockSpec((tm, tn), lambda i, j, k: (i, j))
    return _mm(name, a, b, grid=grid, a_spec=a_spec, b_spec=b_spec, o_spec=o_spec,
               out_shape=(m, n), out_dtype=out_dtype, contract=contract, res=res, res_spec=res_spec)


def _blockwise(name, fn, ins, in_specs, outs, out_specs, grid, n_acc=0, acc_all=True):
    n_in, n_out = len(ins), len(outs)
    n_plain = n_out - n_acc

    def body(*refs):
        vals = fn(*[r[...] for r in refs[:n_in]])
        if not isinstance(vals, (tuple, list)):
            vals = (vals,)
        o_refs = refs[n_in:n_in + n_out]
        for r, v in zip(o_refs[:n_plain], vals[:n_plain]):
            r[...] = v.astype(r.dtype)
        if n_acc:
            if acc_all:
                first = functools.reduce(jnp.logical_and, [pl.program_id(d) == 0 for d in range(len(grid))])
            else:
                first = pl.program_id(len(grid) - 1) == 0

            @pl.when(first)
            def _():
                for r, v in zip(o_refs[n_plain:], vals[n_plain:]):
                    r[...] = v.astype(r.dtype)

            @pl.when(jnp.logical_not(first))
            def _():
                for r, v in zip(o_refs[n_plain:], vals[n_plain:]):
                    r[...] += v.astype(r.dtype)

    return pl.pallas_call(
        body, name=name, grid=grid, in_specs=in_specs, out_specs=out_specs,
        out_shape=[jax.ShapeDtypeStruct(s, d) for s, d in outs], compiler_params=_cparams(),
    )(*ins)


def _row_spec(t, c):
    return pl.BlockSpec((t, c), lambda i: (i, 0))


def _full_spec(shape):
    nd = len(shape)
    return pl.BlockSpec(tuple(shape), lambda *g: (0,) * nd)


def _rms(xf, w):
    return xf * lax.rsqrt(jnp.mean(xf * xf, axis=-1, keepdims=True) + RMS_EPS) * w


def _rms_bwd(xf, w, dy):
    _, vjp = jax.vjp(_rms, xf, w)
    return vjp(dy)


def _s5_disc(lr, li, ldt, bre, bim):
    dt = jnp.exp(ldt)
    mag = jnp.exp(lr * dt)
    ar = mag * jnp.cos(li * dt)
    ai = mag * jnp.sin(li * dt)
    nr, ni = ar - 1.0, ai
    den = lr * lr + li * li
    zr = (nr * lr + ni * li) / den
    zi = (ni * lr - nr * li) / den
    return ar, ai, zr * bre - zi * bim, zr * bim + zi * bre


def _s5_prep(lr, li, ldt, bre, bim):
    def body(lr_r, li_r, ldt_r, bre_r, bim_r, ar_r, ai_r, br_r, bi_r):
        ar, ai, br, bi = _s5_disc(lr_r[...], li_r[...], ldt_r[...], bre_r[...], bim_r[...])
        ar_r[...] = ar
        ai_r[...] = ai
        br_r[...] = br
        bi_r[...] = bi

    sd = jax.ShapeDtypeStruct
    return pl.pallas_call(
        body, name="s5_prep",
        out_shape=[sd(lr.shape, F32), sd(lr.shape, F32), sd(bre.shape, F32), sd(bre.shape, F32)],
        compiler_params=_cparams(),
    )(lr, li, ldt, bre, bim)


def _s5_prep_bwd(lr, li, ldt, bre, bim, dar, dai, dbr, dbi):
    def body(lr_r, li_r, ldt_r, bre_r, bim_r, dar_r, dai_r, dbr_r, dbi_r, o0, o1, o2, o3, o4):
        _, vjp = jax.vjp(_s5_disc, lr_r[...], li_r[...], ldt_r[...], bre_r[...], bim_r[...])
        g = vjp((dar_r[...], dai_r[...], dbr_r[...], dbi_r[...]))
        for o, v in zip((o0, o1, o2, o3, o4), g):
            o[...] = v

    sd = jax.ShapeDtypeStruct
    return pl.pallas_call(
        body, name="s5_prep_bwd",
        out_shape=[sd(lr.shape, F32), sd(li.shape, F32), sd(ldt.shape, F32), sd(bre.shape, F32), sd(bim.shape, F32)],
        compiler_params=_cparams(),
    )(lr, li, ldt, bre, bim, dar, dai, dbr, dbi)


SCAN_T = 256


def _scan_tables(ar, ai, tab_r, tab_i, sub, reverse):
    pr, pi = ar, ai
    for k in range(sub):
        row = sub - 1 - k if reverse else k
        tab_r[row:row + 1, :] = pr
        tab_i[row:row + 1, :] = pi
        pr, pi = ar * pr - ai * pi, ar * pi + ai * pr


def _pack_matrix(t_blk, dtype):
    sub = t_blk // SUBLANES
    dst = jnp.arange(t_blk)
    src = (dst % SUBLANES) * sub + dst // SUBLANES
    return (src[:, None] == jnp.arange(t_blk)[None, :]).astype(dtype)


def _permute_rows_f32(pm, x):
    hi = x.astype(BF16)
    r1 = x - hi.astype(F32)
    mid = r1.astype(BF16)
    lo = (r1 - mid.astype(F32)).astype(BF16)
    dot = lambda v: jnp.dot(pm, v, preferred_element_type=F32)
    return dot(hi) + dot(mid) + dot(lo)


def _scan_block(x, loc, ar, ai, st, tab_r, tab_i, sub, reverse):
    hb = STATE_BLOCK
    a8r = jnp.broadcast_to(ar, (SUBLANES, hb))
    a8i = jnp.broadcast_to(ai, (SUBLANES, hb))
    sr = jnp.zeros((SUBLANES, hb), F32)
    si = jnp.zeros((SUBLANES, hb), F32)
    steps = range(sub - 1, -1, -1) if reverse else range(sub)
    for t in steps:
        rows = slice(t * SUBLANES, (t + 1) * SUBLANES)
        sr, si = a8r * sr - a8i * si + x[rows, :hb], a8r * si + a8i * sr + x[rows, hb:]
        loc[rows, :hb] = sr
        loc[rows, hb:] = si
    cr, ci = st[0:1, :], st[1:2, :]
    far = 0 if reverse else sub - 1
    fr, fi = tab_r[far:far + 1, :], tab_i[far:far + 1, :]
    ent_r, ent_i = [None] * SUBLANES, [None] * SUBLANES
    for c in (range(SUBLANES - 1, -1, -1) if reverse else range(SUBLANES)):
        ent_r[c], ent_i[c] = cr, ci
        cr, ci = sr[c:c + 1, :] + (fr * cr - fi * ci), si[c:c + 1, :] + (fr * ci + fi * cr)
    st[0:1, :] = cr
    st[1:2, :] = ci
    c8r = jnp.concatenate(ent_r, axis=0)
    c8i = jnp.concatenate(ent_i, axis=0)
    out = []
    for t in range(sub):
        rows = slice(t * SUBLANES, (t + 1) * SUBLANES)
        tr, ti = tab_r[t:t + 1, :], tab_i[t:t + 1, :]
        out.append(jnp.concatenate([loc[rows, :hb] + (tr * c8r - ti * c8i), loc[rows, hb:] + (tr * c8i + ti * c8r)],
                                   axis=1))
    return jnp.concatenate(out, axis=0)


def _scan_scratch(t_blk, sub, hb):
    return [pltpu.VMEM((SUBLANES, hb), F32), pltpu.VMEM((sub, hb), F32), pltpu.VMEM((sub, hb), F32),
            pltpu.VMEM((t_blk, 2 * hb), F32)]


def _ssm_fwd(proj, wb, wc, a):
    seq = proj.shape[0]
    nj = wb.shape[0]
    w2 = 2 * STATE_BLOCK
    hb = STATE_BLOCK
    t_blk = min(SCAN_T, seq)
    sub = t_blk // SUBLANES
    pm = _pack_matrix(t_blk, BF16)

    def body(u_ref, wb_ref, wc_ref, a_ref, pm_ref, pmt_ref, s_ref, y_ref, st, tab_r, tab_i, loc):
        ar = a_ref[:, :hb]
        ai = a_ref[:, hb:]

        @pl.when(pl.program_id(1) == 0)
        def _():
            st[...] = jnp.zeros_like(st)
            _scan_tables(ar, ai, tab_r, tab_i, sub, False)

        up = jnp.dot(pm_ref[...], u_ref[...].astype(BF16), preferred_element_type=F32).astype(BF16)
        bu = jnp.dot(up, wb_ref[...], preferred_element_type=F32)
        s = _scan_block(bu, loc, ar, ai, st, tab_r, tab_i, sub, False)
        s_ref[...] = s
        yp = jnp.dot(s.astype(BF16), wc_ref[...], preferred_element_type=F32)
        y_ref[...] = _permute_rows_f32(pmt_ref[...], yp)

    sd = jax.ShapeDtypeStruct
    return pl.pallas_call(
        body, name="ssm_fwd", grid=(nj, seq // t_blk),
        in_specs=[pl.BlockSpec((t_blk, LANES), lambda j, i: (i, j)),
                  pl.BlockSpec((None, LANES, w2), lambda j, i: (j, 0, 0)),
                  pl.BlockSpec((None, w2, LANES), lambda j, i: (j, 0, 0)),
                  pl.BlockSpec((1, w2), lambda j, i: (0, j)),
                  _full_spec((t_blk, t_blk)), _full_spec((t_blk, t_blk))],
        out_specs=[pl.BlockSpec((t_blk, w2), lambda j, i: (i, j)), pl.BlockSpec((t_blk, LANES), lambda j, i: (i, j))],
        out_shape=[sd((seq, nj * w2), F32), sd((seq, nj * LANES), F32)],
        scratch_shapes=_scan_scratch(t_blk, sub, hb), compiler_params=_cparams(),
    )(proj, wb, wc, a, pm, pm.T)


def _ssm_bwd(dy, s, proj, du1, wb, wc, a):
    seq = dy.shape[0]
    nj = wb.shape[0]
    w2 = 2 * STATE_BLOCK
    hb = STATE_BLOCK
    t_blk = min(SCAN_T, seq)
    sub = t_blk // SUBLANES
    nb = seq // t_blk
    pm = _pack_matrix(t_blk, BF16)

    def body(dy_ref, s_ref, sprev_ref, u_ref, du1_ref, wb_ref, wc_ref, a_ref, pm_ref, pmt_ref,
             du_ref, dwb_ref, dwc_ref, da_ref, st, tab_r, tab_i, loc):
        ib = pl.program_id(1)
        ar = a_ref[:, :hb]
        ai = -a_ref[:, hb:]

        @pl.when(ib == 0)
        def _():
            st[...] = jnp.zeros_like(st)
            _scan_tables(ar, ai, tab_r, tab_i, sub, True)

        pmv = pm_ref[...]
        dyp = jnp.dot(pmv, dy_ref[...], preferred_element_type=F32).astype(BF16)
        up = jnp.dot(pmv, u_ref[...].astype(BF16), preferred_element_type=F32).astype(BF16)
        ds = lax.dot_general(dyp, wc_ref[...], (NT, ((), ())), preferred_element_type=F32)
        lam = _scan_block(ds, loc, ar, ai, st, tab_r, tab_i, sub, True)
        lamb = lam.astype(BF16)
        du = lax.dot_general(lamb, wb_ref[...], (NT, ((), ())), preferred_element_type=F32)
        du_ref[...] = (_permute_rows_f32(pmt_ref[...], du) + du1_ref[...]).astype(du_ref.dtype)
        sv = s_ref[...]
        dwb = lax.dot_general(up, lamb, (TN, ((), ())), preferred_element_type=F32)
        dwc = lax.dot_general(sv.astype(BF16), dyp, (TN, ((), ())), preferred_element_type=F32)

        prev_last = sprev_ref[SUBLANES - 1:SUBLANES, :]
        prev_last = jnp.where(ib == nb - 1, jnp.zeros_like(prev_last), prev_last)
        tail = sv[t_blk - SUBLANES:, :]
        sl = lax.broadcasted_iota(jnp.int32, tail.shape, 0)
        head = jnp.where(sl >= 1, pltpu.roll(tail, 1, 0), prev_last)
        s_sh = jnp.concatenate([head, sv[:t_blk - SUBLANES, :]], axis=0)
        lam_r, lam_i = lam[:, :hb], lam[:, hb:]
        sr_, si_ = s_sh[:, :hb], s_sh[:, hb:]
        dar = jnp.sum(lam_r * sr_ + lam_i * si_, axis=0, keepdims=True)
        dai = jnp.sum(lam_i * sr_ - lam_r * si_, axis=0, keepdims=True)
        contrib = jnp.concatenate([dar, dai], axis=1)

        @pl.when(ib == 0)
        def _():
            da_ref[...] = contrib
            dwb_ref[...] = dwb
            dwc_ref[...] = dwc

        @pl.when(ib != 0)
        def _():
            da_ref[...] += contrib
            dwb_ref[...] += dwb
            dwc_ref[...] += dwc

    blk = lambda j, i: (nb - 1 - i, j)
    prev_blk = lambda j, i: (jnp.maximum((nb - 1 - i) * sub - 1, 0), j)
    sd = jax.ShapeDtypeStruct
    return pl.pallas_call(
        body, name="ssm_bwd", grid=(nj, nb),
        in_specs=[pl.BlockSpec((t_blk, LANES), blk), pl.BlockSpec((t_blk, w2), blk),
                  pl.BlockSpec((SUBLANES, w2), prev_blk), pl.BlockSpec((t_blk, LANES), blk),
                  pl.BlockSpec((t_blk, LANES), blk),
                  pl.BlockSpec((None, LANES, w2), lambda j, i: (j, 0, 0)),
                  pl.BlockSpec((None, w2, LANES), lambda j, i: (j, 0, 0)),
                  pl.BlockSpec((1, w2), lambda j, i: (0, j)),
                  _full_spec((t_blk, t_blk)), _full_spec((t_blk, t_blk))],
        out_specs=[pl.BlockSpec((t_blk, LANES), blk),
                   pl.BlockSpec((None, LANES, w2), lambda j, i: (j, 0, 0)),
                   pl.BlockSpec((None, w2, LANES), lambda j, i: (j, 0, 0)),
                   pl.BlockSpec((1, w2), lambda j, i: (0, j))],
        out_shape=[sd((seq, nj * LANES), BF16), sd((nj, LANES, w2), F32), sd((nj, w2, LANES), F32),
                   sd((1, nj * w2), F32)],
        scratch_shapes=_scan_scratch(t_blk, sub, hb), compiler_params=_cparams(),
    )(dy, s, s, proj, du1, wb, wc, a, pm, pm.T)


def _rope128(x, cos, sa, sb):
    return x * cos + pltpu.roll(x, 96, 1) * sa + pltpu.roll(x, 32, 1) * sb


def _rope128_t(dy, cos, sa, sb):
    return dy * cos + pltpu.roll(dy * sa, 32, 1) + pltpu.roll(dy * sb, 96, 1)


ATT_BQ = 256


def _probs(qn, qp, kn, kp, r0, scale):
    s = lax.dot_general(qn, kn, (NT, ((), ())), preferred_element_type=F32)
    s = s + lax.dot_general(qp, kp, (NT, ((), ())), preferred_element_type=F32)
    s = s * scale
    diag = s[:, r0:]
    row = lax.broadcasted_iota(jnp.int32, diag.shape, 0)
    col = lax.broadcasted_iota(jnp.int32, diag.shape, 1)
    diag = jnp.where(col <= row, diag, jnp.finfo(F32).min)
    s = diag if r0 == 0 else jnp.concatenate([s[:, :r0], diag], axis=1)
    m = jnp.max(s, axis=-1, keepdims=True)
    e = jnp.exp(s - m)
    return e / jnp.sum(e, axis=-1, keepdims=True)


def _attn_specs(seq):
    tab = pl.BlockSpec((seq, LANES), lambda h: (0, 0))
    return [pl.BlockSpec((None, seq, 256), lambda h: (h, 0, 0)), pl.BlockSpec((None, seq, 128), lambda h: (h, 0, 0)),
            pl.BlockSpec((None, seq, 128), lambda h: (h, 0, 1)), tab, tab, tab, tab]


def _attn_fwd(q_raw, kv, kpe, cos, sa, sb):
    nh, seq, _ = q_raw.shape
    bq = min(ATT_BQ, seq)
    scale = (QK_NOPE + QK_ROPE) ** -0.5

    def body(q_ref, kn_ref, v_ref, kp_ref, cos_ref, sa_ref, sb_ref, o_ref):
        for r0 in range(0, seq, bq):
            rows, kend = pl.ds(r0, bq), r0 + bq
            qn = q_ref[rows, :QK_NOPE].astype(BF16)
            qp = _rope128(q_ref[rows, QK_NOPE:], cos_ref[rows, :], sa_ref[rows, :], sb_ref[rows, :]).astype(BF16)
            p = _probs(qn, qp, kn_ref[:kend, :], kp_ref[:kend, :], r0, scale)
            o_ref[rows, :] = jnp.dot(p.astype(BF16), v_ref[:kend, :], preferred_element_type=F32)

    return pl.pallas_call(
        body, name="attn_fwd", grid=(nh,), in_specs=_attn_specs(seq),
        out_specs=pl.BlockSpec((seq, V_DIM), lambda h: (0, h)),
        out_shape=jax.ShapeDtypeStruct((seq, nh * V_DIM), F32), compiler_params=_cparams(),
    )(q_raw, kv, kv, kpe, cos, sa, sb)


def _attn_bwd(q_raw, kv, kpe, cos, sa, sb, do):
    nh, seq, _ = q_raw.shape
    bq = min(ATT_BQ, seq)
    scale = (QK_NOPE + QK_ROPE) ** -0.5

    def body(q_ref, kn_ref, v_ref, kp_ref, cos_ref, sa_ref, sb_ref, do_ref, dq_ref, dkv_ref, dkp_ref):
        dkv_ref[...] = jnp.zeros_like(dkv_ref)
        dkp_ref[...] = jnp.zeros_like(dkp_ref)
        for r0 in range(0, seq, bq):
            rows, kend = pl.ds(r0, bq), r0 + bq
            cos_b, sa_b, sb_b = cos_ref[rows, :], sa_ref[rows, :], sb_ref[rows, :]
            qn = q_ref[rows, :QK_NOPE].astype(BF16)
            qp = _rope128(q_ref[rows, QK_NOPE:], cos_b, sa_b, sb_b).astype(BF16)
            kn, v, kp = kn_ref[:kend, :], v_ref[:kend, :], kp_ref[:kend, :]
            p = _probs(qn, qp, kn, kp, r0, scale)
            dob = do_ref[rows, :].astype(BF16)
            dp = lax.dot_general(dob, v, (NT, ((), ())), preferred_element_type=F32)
            ds = p * (dp - jnp.sum(p * dp, axis=-1, keepdims=True)) * scale
            dsb = ds.astype(BF16)
            pb = p.astype(BF16)
            dq_ref[rows, :QK_NOPE] = jnp.dot(dsb, kn, preferred_element_type=F32).astype(dq_ref.dtype)
            dqp = jnp.dot(dsb, kp, preferred_element_type=F32)
            dq_ref[rows, QK_NOPE:] = _rope128_t(dqp, cos_b, sa_b, sb_b).astype(dq_ref.dtype)
            dkv_ref[:kend, :QK_NOPE] += lax.dot_general(dsb, qn, (TN, ((), ())), preferred_element_type=F32)
            dkv_ref[:kend, QK_NOPE:] += lax.dot_general(pb, dob, (TN, ((), ())), preferred_element_type=F32)
            dkp_ref[:kend, :] += lax.dot_general(dsb, qp, (TN, ((), ())), preferred_element_type=F32)

    sd = jax.ShapeDtypeStruct
    return pl.pallas_call(
        body, name="attn_bwd", grid=(nh,),
        in_specs=_attn_specs(seq) + [pl.BlockSpec((seq, V_DIM), lambda h: (0, h))],
        out_specs=[pl.BlockSpec((None, seq, 256), lambda h: (h, 0, 0)),
                   pl.BlockSpec((None, seq, 256), lambda h: (h, 0, 0)),
                   pl.BlockSpec((None, seq, 128), lambda h: (h, 0, 0))],
        out_shape=[sd((nh, seq, 256), BF16), sd((nh, seq, 256), F32), sd((nh, seq, 128), F32)],
        compiler_params=_cparams(),
    )(q_raw, kv, kv, kpe, cos, sa, sb, do)


def _conv3(a, w, b):
    rows = lax.broadcasted_iota(jnp.int32, a.shape, 0)
    a1 = jnp.where(rows >= 1, pltpu.roll(a, 1, 0), 0.0)
    a2 = jnp.where(rows >= 2, pltpu.roll(a, 2, 0), 0.0)
    return w[2:3] * a + w[1:2] * a1 + w[0:1] * a2 + b, a1, a2


def _conv_gate_fwd(a, cw, cb):
    half, _, seq, c = a.shape
    nc = c // LANES

    def fn(pair, wg, wv, bg, bv):
        gc, _, _ = _conv3(pair[0], wg, bg)
        vc, _, _ = _conv3(pair[1], wv, bv)
        return gc * jax.nn.sigmoid(gc) * vc

    def w_spec(off, r):
        return pl.BlockSpec((None, r, LANES), lambda k, j: (k + off, 0, j))

    return _blockwise(
        "conv_gate_fwd", fn, [a, cw, cw, cb, cb],
        [pl.BlockSpec((None, 2, seq, LANES), lambda k, j: (k, 0, 0, j)),
         w_spec(0, 3), w_spec(half, 3), w_spec(0, 1), w_spec(half, 1)],
        [((seq, half * c), BF16)], [pl.BlockSpec((seq, LANES), lambda k, j: (0, k * nc + j))],
        grid=(half, nc))[0]


def _conv_gate_bwd(a, cw, cb, dm):
    half, _, seq, c = a.shape
    nc = c // LANES

    def body(a_ref, wg_ref, wv_ref, bg_ref, bv_ref, dm_ref, da_ref, dw_ref, db_ref):
        dmv = dm_ref[...]
        rows = lax.broadcasted_iota(jnp.int32, dmv.shape, 0)
        ga, wg = a_ref[0], wg_ref[...]
        va, wv = a_ref[1], wv_ref[...]
        gc, g1, g2 = _conv3(ga, wg, bg_ref[...])
        vc, v1, v2 = _conv3(va, wv, bv_ref[...])
        sg = jax.nn.sigmoid(gc)
        dms = dmv * sg
        d_val = dms * gc
        d_gate = dms * vc * (1.0 + gc * (1.0 - sg))

        def back(r, dc, own, a1, a2, w):
            up1 = jnp.where(rows < seq - 1, pltpu.roll(dc, seq - 1, 0), 0.0)
            up2 = jnp.where(rows < seq - 2, pltpu.roll(dc, seq - 2, 0), 0.0)
            da_ref[r] = (w[2:3] * dc + w[1:2] * up1 + w[0:1] * up2).astype(da_ref.dtype)
            dw_ref[r, 0:1, :] = jnp.sum(dc * a2, axis=0, keepdims=True)
            dw_ref[r, 1:2, :] = jnp.sum(dc * a1, axis=0, keepdims=True)
            dw_ref[r, 2:3, :] = jnp.sum(dc * own, axis=0, keepdims=True)
            db_ref[r] = jnp.sum(dc, axis=0, keepdims=True)

        back(0, d_gate, ga, g1, g2, wg)
        back(1, d_val, va, v1, v2, wv)

    def w_spec(off, r):
        return pl.BlockSpec((None, r, LANES), lambda k, j: (k + off, 0, j))

    def pair_spec(r):
        return pl.BlockSpec((None, 2, r, LANES), lambda k, j: (k, 0, 0, j))

    sd = jax.ShapeDtypeStruct
    return pl.pallas_call(
        body, name="conv_gate_bwd", grid=(half, nc),
        in_specs=[pair_spec(seq), w_spec(0, 3), w_spec(half, 3), w_spec(0, 1), w_spec(half, 1),
                  pl.BlockSpec((seq, LANES), lambda k, j: (0, k * nc + j))],
        out_specs=[pair_spec(seq), pair_spec(3), pair_spec(1)],
        out_shape=[sd((half, 2, seq, c), BF16), sd((half, 2, 3, c), F32), sd((half, 2, 1, c), F32)],
        compiler_params=_cparams(),
    )(a, cw, cw, cb, cb, dm)


ROW_T = 256


def _local_step(x, positions, target, w, emit=lambda **grads: None):
    seq, d = x.shape
    t_row = min(ROW_T, seq)
    nrow = seq // t_row
    ssm_w = d // 2
    nj = ssm_w // LANES
    n_groups = ssm_w // SSM_GROUP
    nh = w["wuq"].shape[0]
    q_rank = w["wuq"].shape[1]
    kv_rank = w["wukv"].shape[1]
    ns = w["wup"].shape[0]
    c_ff = w["wup"].shape[2]
    in_pad = w["win"].shape[1]
    tm = min(1024, seq)
    nm = seq // tm
    sw = 2 * STATE_BLOCK
    g1 = (nrow,)

    lr3 = w["lam_re"].reshape(n_groups, 1, SSM_STATE)
    li3 = w["lam_im"].reshape(n_groups, 1, SSM_STATE)
    ldt3 = w["log_dt"].reshape(n_groups, 1, 1)
    bt_re = jnp.swapaxes(w["b_re"].reshape(n_groups, SSM_STATE, SSM_GROUP), 1, 2)
    bt_im = jnp.swapaxes(w["b_im"].reshape(n_groups, SSM_STATE, SSM_GROUP), 1, 2)
    abar_re, abar_im, bbt_re, bbt_im = _s5_prep(lr3, li3, ldt3, bt_re, bt_im)
    eye = jnp.eye(GROUPS_PER_BLOCK, dtype=F32)

    def blockdiag_in(bb):
        t = bb.reshape(nj, GROUPS_PER_BLOCK, SSM_GROUP, SSM_STATE)
        return jnp.einsum("jghp,gk->jghkp", t, eye).reshape(nj, LANES, STATE_BLOCK)

    def blockdiag_in_t(dwb):
        t = dwb.reshape(nj, GROUPS_PER_BLOCK, SSM_GROUP, GROUPS_PER_BLOCK, SSM_STATE)
        return jnp.einsum("jghkp,gk->jghp", t, eye).reshape(n_groups, SSM_GROUP, SSM_STATE)

    def blockdiag_out(cc):
        t = cc.reshape(nj, GROUPS_PER_BLOCK, SSM_GROUP, SSM_STATE)
        return jnp.einsum("jghp,gk->jkpgh", t, eye).reshape(nj, STATE_BLOCK, LANES)

    def blockdiag_out_t(dwc):
        t = dwc.reshape(nj, GROUPS_PER_BLOCK, SSM_STATE, GROUPS_PER_BLOCK, SSM_GROUP)
        return jnp.einsum("jkpgh,gk->jghp", t, eye).reshape(n_groups, SSM_GROUP, SSM_STATE)

    c_re = w["c_re"].reshape(n_groups, SSM_GROUP, SSM_STATE)
    c_im = w["c_im"].reshape(n_groups, SSM_GROUP, SSM_STATE)
    wb = jnp.concatenate([blockdiag_in(bbt_re), blockdiag_in(bbt_im)], axis=2).astype(BF16)
    wc = jnp.concatenate([blockdiag_out(c_re), -blockdiag_out(c_im)], axis=1).astype(BF16)
    a_lay = jnp.concatenate([abar_re.reshape(nj, 1, STATE_BLOCK), abar_im.reshape(nj, 1, STATE_BLOCK)],
                            axis=1).reshape(1, nj * sw)

    attn_w = w["attn_norm"]
    hn = _blockwise("norm1", lambda xb, wv: _rms(xb, wv), [x, attn_w], [_row_spec(t_row, d), _full_spec((1, d))],
                    [((seq, d), BF16)], [_row_spec(t_row, d)], g1)[0]
    proj = _mm2d("proj", hn, w["win"], NN, F32, tn=640)

    s_all, ylin = _ssm_fwd(proj, wb, wc, a_lay)
    u_spec = pl.BlockSpec((t_row, ssm_w), lambda i: (i, 0))

    def ypre_fn(yl, ub, dsk):
        yp = yl + dsk * ub
        return yp, jax.nn.gelu(yp)

    y_pre, yg = _blockwise("ssm_gelu", ypre_fn, [ylin, proj, w["ssm_d"]],
                           [_row_spec(t_row, ssm_w), u_spec, _full_spec((1, ssm_w))],
                           [((seq, ssm_w), F32), ((seq, ssm_w), BF16)],
                           [_row_spec(t_row, ssm_w)] * 2, g1)
    z = _mm2d("ssm_glu", yg, w["wglu"], NN, F32, res=w["b_glu"])
    y_ssm = _blockwise("ssm_gate", lambda yp, zb: jax.nn.gelu(yp) * jax.nn.sigmoid(zb), [y_pre, z],
                       [_row_spec(t_row, ssm_w)] * 2, [((seq, ssm_w), F32)], [_row_spec(t_row, ssm_w)], g1)[0]

    cq_off, ckv_off, kpe_off = ssm_w, ssm_w + q_rank, ssm_w + q_rank + kv_rank
    c_q = proj[:, cq_off:ckv_off]
    c_kv = proj[:, ckv_off:kpe_off]
    kpe_raw = proj[:, kpe_off:kpe_off + LANES]
    pos_b = jnp.broadcast_to(positions.astype(F32)[:, None], (seq, LANES))
    inv_freq = ROPE_THETA ** (-jnp.arange(0, QK_ROPE, 2, dtype=F32) / QK_ROPE)
    inv128 = jnp.tile(inv_freq, 4).reshape(1, LANES)

    def mla_prep_fn(cq, ckv, kp, pb, inv, wq, wkv):
        ang = pb * inv
        lane = lax.broadcasted_iota(jnp.int32, ang.shape, 1)
        cs, sn = jnp.cos(ang), jnp.sin(ang)
        cos = jnp.where(lane < QK_ROPE, cs, 0.0)
        sa = jnp.where(lane < QK_ROPE // 2, -sn, 0.0)
        sb = jnp.where(jnp.logical_and(lane >= QK_ROPE // 2, lane < QK_ROPE), sn, 0.0)
        return _rms(cq, wq), _rms(ckv, wkv), _rope128(kp, cos, sa, sb), cos, sa, sb

    qn, kvn, kpe, cos_t, sa_t, sb_t = _blockwise(
        "mla_prep", mla_prep_fn, [c_q, c_kv, kpe_raw, pos_b, inv128, w["q_norm"], w["kv_norm"]],
        [_row_spec(t_row, q_rank), _row_spec(t_row, kv_rank), _row_spec(t_row, LANES), _row_spec(t_row, LANES),
         _full_spec((1, LANES)), _full_spec((1, q_rank)), _full_spec((1, kv_rank))],
        [((seq, q_rank), BF16), ((seq, kv_rank), BF16), ((seq, LANES), BF16)] + [((seq, LANES), F32)] * 3,
        [_row_spec(t_row, q_rank), _row_spec(t_row, kv_rank)] + [_row_spec(t_row, LANES)] * 4, g1)

    def head_mm(name, act, wh, out_dtype):
        kdim, ndim = wh.shape[1], wh.shape[2]
        return _mm(name, act, wh, grid=(nh, 1, 1),
                   a_spec=pl.BlockSpec((seq, kdim), lambda h, i, k: (i, 0)),
                   b_spec=pl.BlockSpec((None, kdim, ndim), lambda h, i, k: (h, 0, 0)),
                   o_spec=pl.BlockSpec((None, seq, ndim), lambda h, i, k: (h, i, 0)),
                   out_shape=(nh, seq, ndim), out_dtype=out_dtype)

    q_raw = head_mm("mla_q", qn, w["wuq"], F32)
    kv = head_mm("mla_kv", kvn, w["wukv"], BF16)
    y_mla = _attn_fwd(q_raw, kv, kpe, cos_t, sa_t, sb_t)
    mla_w = nh * V_DIM

    def outnorm_fn(ys, ym, ws, wm):
        return jnp.concatenate([_rms(ys, ws), _rms(ym, wm)], axis=1)

    ycat = _blockwise("out_norm", outnorm_fn, [y_ssm, y_mla, w["son"], w["mon"]],
                      [_row_spec(t_row, ssm_w), _row_spec(t_row, mla_w), _full_spec((1, ssm_w)), _full_spec((1, mla_w))],
                      [((seq, d), BF16)], [_row_spec(t_row, d)], g1)[0]
    h1 = _mm2d("out_proj", ycat, w["wout"], NN, F32, res=x)

    hn2 = _blockwise("norm2", lambda hb, wv: _rms(hb, wv), [h1, w["ffn_norm"]],
                     [_row_spec(t_row, d), _full_spec((1, d))], [((seq, d), BF16)], [_row_spec(t_row, d)], g1)[0]
    tku = d
    half = ns // 2
    a_ff = _mm("ffn_up", hn2, w["wup"], grid=(ns, nm, d // tku),
               a_spec=pl.BlockSpec((tm, tku), lambda s, i, k: (i, k)),
               b_spec=pl.BlockSpec((None, tku, c_ff), lambda s, i, k: (s, k, 0)),
               o_spec=pl.BlockSpec((None, None, tm, c_ff), lambda s, i, k: (s % half, s // half, i, 0)),
               out_shape=(half, 2, seq, c_ff), out_dtype=F32)
    cb3 = w["conv_b"].reshape(ns, 1, c_ff)
    m_ff = _conv_gate_fwd(a_ff, w["conv_w"], cb3)
    d_ff = half * c_ff
    wdn = w["wdown"]
    tnd = _tile(d, 1024)
    tmx, tnx = min(1024, seq), _tile(d, 1024)
    h2 = _mm2d("ffn_down", m_ff, wdn, NN, F32, tm=512, tn=512, tk=d_ff, res=h1)

    def loss_fn(hb, tb, wv):
        def f(hh, ww):
            err = _rms(hh, ww) - tb
            return 0.5 * jnp.sum(jnp.mean(err * err, axis=-1))

        lossv, (dh, dw) = jax.value_and_grad(f, argnums=(0, 1))(hb, wv)
        return dh, dh, jnp.full((1, LANES), lossv, F32), dw

    fin_w = w["final_norm"].reshape(1, d)
    dh2, dh2b, loss_acc, g_final = _blockwise(
        "loss_head", loss_fn, [h2, target, fin_w], [_row_spec(t_row, d), _row_spec(t_row, d), _full_spec((1, d))],
        [((seq, d), F32), ((seq, d), BF16), ((1, LANES), F32), ((1, d), F32)],
        [_row_spec(t_row, d), _row_spec(t_row, d), _full_spec((1, LANES)), _full_spec((1, d))], g1, n_acc=2)
    loss = loss_acc[0, 0]

    dm = _mm2d("ffn_down_dx", dh2b, wdn, NT, F32, tn=c_ff)
    tks = seq
    g_wdown = _mm2d("ffn_down_dw", m_ff, dh2b, TN, BF16, tm=c_ff)
    emit(wdown=g_wdown)
    da_ff, g_convw2, g_convb2 = _conv_gate_bwd(a_ff, w["conv_w"], cb3, dm)
    g_convw = jnp.swapaxes(g_convw2, 0, 1).reshape(ns, 3, c_ff)
    g_convb = jnp.swapaxes(g_convb2, 0, 1).reshape(ns, 1, c_ff)
    g_wup = _mm("ffn_up_dw", hn2, da_ff, grid=(ns, d // tnd, seq // tks), contract=TN,
                a_spec=pl.BlockSpec((tks, tnd), lambda s, j, k: (k, j)),
                b_spec=pl.BlockSpec((None, None, tks, c_ff), lambda s, j, k: (s % half, s // half, k, 0)),
                o_spec=pl.BlockSpec((None, tnd, c_ff), lambda s, j, k: (s, j, 0)),
                out_shape=(ns, d, c_ff), out_dtype=BF16)
    emit(wup=g_wup, conv_w=g_convw)
    dhn2 = _mm("ffn_up_dx", da_ff, w["wup"], grid=(seq // tmx, d // tnx, ns), contract=NT,
               a_spec=pl.BlockSpec((None, None, tmx, c_ff), lambda i, j, s: (s % half, s // half, i, 0)),
               b_spec=pl.BlockSpec((None, tnx, c_ff), lambda i, j, s: (s, j, 0)),
               o_spec=pl.BlockSpec((tmx, tnx), lambda i, j, s: (i, j)),
               out_shape=(seq, d), out_dtype=F32)

    def norm_bwd_fn(hb, dres, dn, wv):
        dx_, dw_ = _rms_bwd(hb, wv, dn)
        dtot = dres + dx_
        return dtot, dtot, dw_

    dh1, dh1b, g_ffn_norm = _blockwise(
        "norm2_bwd", norm_bwd_fn, [h1, dh2, dhn2, w["ffn_norm"]],
        [_row_spec(t_row, d)] * 3 + [_full_spec((1, d))],
        [((seq, d), F32), ((seq, d), BF16), ((1, d), F32)],
        [_row_spec(t_row, d), _row_spec(t_row, d), _full_spec((1, d))], g1, n_acc=1)

    dycat = _mm2d("out_proj_dx", dh1b, w["wout"], NT, F32)
    g_wout = _mm2d("out_proj_dw", ycat, dh1b, TN, BF16)

    def outnorm_bwd_fn(ys, ym, dyc, ws, wm):
        dys, dws = _rms_bwd(ys, ws, dyc[:, :ssm_w])
        dym, dwm = _rms_bwd(ym, wm, dyc[:, ssm_w:])
        return dys, dym, dws, dwm

    dy_ssm, dy_mla, g_son, g_mon = _blockwise(
        "out_norm_bwd", outnorm_bwd_fn, [y_ssm, y_mla, dycat, w["son"], w["mon"]],
        [_row_spec(t_row, ssm_w), _row_spec(t_row, mla_w), _row_spec(t_row, d), _full_spec((1, ssm_w)),
         _full_spec((1, mla_w))],
        [((seq, ssm_w), F32), ((seq, mla_w), F32), ((1, ssm_w), F32), ((1, mla_w), F32)],
        [_row_spec(t_row, ssm_w), _row_spec(t_row, mla_w), _full_spec((1, ssm_w)), _full_spec((1, mla_w))],
        g1, n_acc=2)

    def gate_bwd1_fn(dy, yp, zb):
        ygv = jax.nn.gelu(yp)
        sg = jax.nn.sigmoid(zb)
        dz = dy * ygv * sg * (1.0 - sg)
        return dz, jnp.sum(dz, axis=0, keepdims=True)

    dz, g_bglu = _blockwise("ssm_gate_bwd", gate_bwd1_fn, [dy_ssm, y_pre, z], [_row_spec(t_row, ssm_w)] * 3,
                            [((seq, ssm_w), BF16), ((1, ssm_w), F32)],
                            [_row_spec(t_row, ssm_w), _full_spec((1, ssm_w))], g1, n_acc=1)
    dyg2 = _mm2d("ssm_glu_dx", dz, w["wglu"], NT, F32)
    g_wglu = _mm2d("ssm_glu_dw", yg, dz, TN, BF16)

    def gelu_bwd_fn(dy, yp, zb, dg2, ub, dsk):
        dyg = dy * jax.nn.sigmoid(zb) + dg2
        _, vjp = jax.vjp(jax.nn.gelu, yp)
        dyp = vjp(dyg)[0]
        return dyp, dyp * dsk, jnp.sum(dyp * ub, axis=0, keepdims=True)

    dy_pre, du1, g_ssmd = _blockwise(
        "ssm_gelu_bwd", gelu_bwd_fn, [dy_ssm, y_pre, z, dyg2, proj, w["ssm_d"]],
        [_row_spec(t_row, ssm_w)] * 4 + [u_spec, _full_spec((1, ssm_w))],
        [((seq, ssm_w), BF16), ((seq, ssm_w), F32), ((1, ssm_w), F32)],
        [_row_spec(t_row, ssm_w), _row_spec(t_row, ssm_w), _full_spec((1, ssm_w))], g1, n_acc=1)
    dq_raw, dkv, dkp_h = _attn_bwd(q_raw, kv, kpe, cos_t, sa_t, sb_t, dy_mla)

    def head_mm_dx(name, dact, wh):
        kdim, ndim = wh.shape[1], wh.shape[2]
        return _mm(name, dact, wh, grid=(1, 1, nh), contract=NT,
                   a_spec=pl.BlockSpec((None, seq, ndim), lambda i, j, h: (h, i, 0)),
                   b_spec=pl.BlockSpec((None, kdim, ndim), lambda i, j, h: (h, 0, 0)),
                   o_spec=pl.BlockSpec((seq, kdim), lambda i, j, h: (i, 0)),
                   out_shape=(seq, kdim), out_dtype=F32)

    def head_mm_dw(name, act, dact):
        kdim, ndim = act.shape[1], dact.shape[2]
        return _mm(name, act, dact, grid=(nh, 1, seq // tks), contract=TN,
                   a_spec=pl.BlockSpec((tks, kdim), lambda h, j, k: (k, 0)),
                   b_spec=pl.BlockSpec((None, tks, ndim), lambda h, j, k: (h, k, 0)),
                   o_spec=pl.BlockSpec((None, kdim, ndim), lambda h, j, k: (h, 0, 0)),
                   out_shape=(nh, kdim, ndim), out_dtype=BF16)

    g_wuq = head_mm_dw("mla_q_dw", qn, dq_raw)
    g_wukv = head_mm_dw("mla_kv_dw", kvn, dkv)
    dqn = head_mm_dx("mla_q_dx", dq_raw, w["wuq"])
    dkvn = head_mm_dx("mla_kv_dx", dkv, w["wukv"])
    emit(not_before=(dqn, dkvn, dy_pre), wout=g_wout, wuq=g_wuq, wukv=g_wukv, wglu=g_wglu)

    du, dwb, dwc, da_lay = _ssm_bwd(dy_pre, s_all, proj, du1, wb, wc, a_lay)
    g_c_re = blockdiag_out_t(dwc[:, :STATE_BLOCK, :])
    g_c_im = -blockdiag_out_t(dwc[:, STATE_BLOCK:, :])
    dbbt_re = blockdiag_in_t(dwb[:, :, :STATE_BLOCK])
    dbbt_im = blockdiag_in_t(dwb[:, :, STATE_BLOCK:])
    da3 = da_lay.reshape(nj, 2, STATE_BLOCK)
    dabar_re = da3[:, 0, :].reshape(n_groups, 1, SSM_STATE)
    dabar_im = da3[:, 1, :].reshape(n_groups, 1, SSM_STATE)
    g_lr3, g_li3, g_ldt3, g_bt_re, g_bt_im = _s5_prep_bwd(lr3, li3, ldt3, bt_re, bt_im,
                                                           dabar_re, dabar_im, dbbt_re, dbbt_im)

    def mla_prep_bwd_fn(cq, ckv, dqn_b, dkvn_b, dkp_b, cos, sa, sb, wq, wkv):
        dcq, dwq = _rms_bwd(cq, wq, dqn_b)
        dckv, dwkv = _rms_bwd(ckv, wkv, dkvn_b)
        dkp_sum = dkp_b[0]
        for h in range(1, nh):
            dkp_sum = dkp_sum + dkp_b[h]
        return dcq, dckv, _rope128_t(dkp_sum, cos, sa, sb), dwq, dwkv

    dc_q, dc_kv, dkpe_raw, g_qnorm, g_kvnorm = _blockwise(
        "mla_prep_bwd", mla_prep_bwd_fn, [c_q, c_kv, dqn, dkvn, dkp_h, cos_t, sa_t, sb_t, w["q_norm"], w["kv_norm"]],
        [_row_spec(t_row, q_rank), _row_spec(t_row, kv_rank), _row_spec(t_row, q_rank), _row_spec(t_row, kv_rank),
         pl.BlockSpec((nh, t_row, LANES), lambda i: (0, i, 0))] + [_row_spec(t_row, LANES)] * 3
        + [_full_spec((1, q_rank)), _full_spec((1, kv_rank))],
        [((seq, q_rank), BF16), ((seq, kv_rank), BF16), ((seq, LANES), BF16), ((1, q_rank), F32), ((1, kv_rank), F32)],
        [_row_spec(t_row, q_rank), _row_spec(t_row, kv_rank), _row_spec(t_row, LANES), _full_spec((1, q_rank)),
         _full_spec((1, kv_rank))], g1, n_acc=2)

    dproj = jnp.concatenate([du, dc_q, dc_kv, dkpe_raw], axis=1)
    g_win = _mm2d("proj_dw", hn, dproj, TN, BF16, tn=640)
    emit(win=g_win)
    dhn = _mm2d("proj_dx", dproj, w["win"], NT, F32)

    def norm1_bwd_fn(xb, dres, dn, wv):
        dx_, dw_ = _rms_bwd(xb, wv, dn)
        return dres + dx_, dw_

    grad_x, g_attn_norm = _blockwise(
        "norm1_bwd", norm1_bwd_fn, [x, dh1, dhn, attn_w], [_row_spec(t_row, d)] * 3 + [_full_spec((1, d))],
        [((seq, d), F32), ((1, d), F32)], [_row_spec(t_row, d), _full_spec((1, d))], g1, n_acc=1)

    grads = dict(
        attn_norm=g_attn_norm, win=g_win, lam_re=g_lr3, lam_im=g_li3, log_dt=g_ldt3,
        b_re=jnp.swapaxes(g_bt_re, 1, 2), b_im=jnp.swapaxes(g_bt_im, 1, 2), c_re=g_c_re, c_im=g_c_im,
        ssm_d=g_ssmd, wglu=g_wglu, b_glu=g_bglu, q_norm=g_qnorm, wuq=g_wuq, kv_norm=g_kvnorm, wukv=g_wukv,
        son=g_son, mon=g_mon, wout=g_wout, ffn_norm=g_ffn_norm, wup=g_wup, conv_w=g_convw, conv_b=g_convb,
        wdown=g_wdown, final_norm=g_final)
    return loss, grad_x, grads


def _mesh_pos():
    return lax.axis_index("x"), lax.axis_index("y"), lax.axis_index("c")


def _handshake_all():
    x, y, c = _mesh_pos()
    barrier = pltpu.get_barrier_semaphore()
    for k in range(1, N_DEV):
        peer = (1 - x if k & 4 else x, 1 - y if k & 2 else y, 1 - c if k & 1 else c)
        pl.semaphore_signal(barrier, inc=1, device_id=peer, device_id_type=MESH)
    pl.semaphore_wait(barrier, N_DEV - 1)


def _comm_call(name, body, n, out_shape, ins, collective_id, after=None):
    sems = [pltpu.SemaphoreType.DMA((7 * n,)), pltpu.SemaphoreType.DMA((7 * n,)), pltpu.SemaphoreType.DMA((n,))]
    if collective_id is None:
        any_spec = pl.BlockSpec(memory_space=pl.ANY)
        return pl.pallas_call(body, name=name, out_shape=out_shape, in_specs=[any_spec] * n,
                              out_specs=[any_spec] * n, scratch_shapes=sems)(*ins)
    seq_body = body
    if after:
        n_after = len(after)
        ins = list(ins) + list(after)

        def seq_body(*refs):
            body(*refs[:n], *refs[n + n_after:])

    return pl.kernel(seq_body, name=name, out_type=out_shape,
                     mesh=plsc.ScalarSubcoreMesh(axis_name="seq", num_cores=1), scratch_types=sems,
                     compiler_params=pltpu.CompilerParams(collective_id=collective_id))(*ins)


def _all_gather(name, xs, collective_id=None, after=None):
    n = len(xs)

    def body(*refs):
        x_refs, o_refs = refs[:n], refs[n:2 * n]
        send_sems, recv_sems, local_sems = refs[2 * n:]
        if collective_id is not None:
            _handshake_all()
        x, y, c = _mesh_pos()
        me, sibling = (x, y, c), (x, y, 1 - c)
        chips = [(1 - x, y), (x, 1 - y), (1 - x, 1 - y)]

        def slot(o_ref, px, py, pc):
            return o_ref.at[4 * px + 2 * py + pc]

        def copy(t, k, block, to, src=None):
            dst = slot(o_refs[t], *block)
            return pltpu.make_async_remote_copy(
                src_ref=dst if src is None else src, dst_ref=dst,
                send_sem=send_sems.at[7 * t + k], recv_sem=recv_sems.at[7 * t + k],
                device_id=to, device_id_type=MESH)

        started = []
        for t in range(n):
            mine = pltpu.make_async_copy(x_refs[t], slot(o_refs[t], *me), local_sems.at[t])
            mine.start()
            started.append(mine)
        first = []
        for t in range(n):
            first.append(copy(t, 0, me, sibling, src=x_refs[t]))
            first += [copy(t, 1 + j, me, (*chip, c), src=x_refs[t]) for j, chip in enumerate(chips)]
        for cp in first:
            cp.start()
        passed = []
        for j, chip in enumerate(chips):
            for t in range(n):
                copy(t, 1 + j, (*chip, c), me).wait_recv()
                fwd = copy(t, 4 + j, (*chip, c), sibling)
                fwd.start()
                passed.append(fwd)
        for t in range(n):
            copy(t, 0, sibling, me).wait_recv()
            for j, chip in enumerate(chips):
                copy(t, 4 + j, (*chip, 1 - c), me).wait_recv()
        for cp in first + passed:
            cp.wait_send()
        for mine in started:
            mine.wait()

    out_shape = [jax.ShapeDtypeStruct((N_DEV,) + v.shape, v.dtype) for v in xs]
    return _comm_call(name, body, n, out_shape, xs, collective_id, after)


def _exchange_partials(name, gs, collective_id=None, after=None):
    n = len(gs)

    def body(*refs):
        g_refs, o_refs = refs[:n], refs[n:2 * n]
        send_sems, recv_sems, local_sems = refs[2 * n:]
        if collective_id is not None:
            _handshake_all()
        x, y, c = _mesh_pos()
        me_idx = 4 * x + 2 * y + c
        copies = []
        for t in range(n):
            mine = pltpu.make_async_copy(g_refs[t].at[me_idx], o_refs[t].at[me_idx], local_sems.at[t])
            mine.start()
            copies.append(mine)
        remote = []
        for k in range(1, N_DEV):
            px = 1 - x if k & 4 else x
            py = 1 - y if k & 2 else y
            pc = 1 - c if k & 1 else c
            p_idx = 4 * px + 2 * py + pc
            for t in range(n):
                cp = pltpu.make_async_remote_copy(
                    src_ref=g_refs[t].at[p_idx], dst_ref=o_refs[t].at[me_idx],
                    send_sem=send_sems.at[7 * t + k - 1], recv_sem=recv_sems.at[7 * t + k - 1],
                    device_id=(px, py, pc), device_id_type=MESH)
                cp.start()
                landing = pltpu.make_async_remote_copy(
                    src_ref=g_refs[t].at[p_idx], dst_ref=o_refs[t].at[p_idx],
                    send_sem=send_sems.at[7 * t + k - 1], recv_sem=recv_sems.at[7 * t + k - 1],
                    device_id=(px, py, pc), device_id_type=MESH)
                remote.append((cp, landing))
        for cp, landing in remote:
            landing.wait_recv()
        for cp, landing in remote:
            cp.wait_send()
        for mine in copies:
            mine.wait()

    out_shape = [jax.ShapeDtypeStruct(v.shape, v.dtype) for v in gs]
    return _comm_call(name, body, n, out_shape, gs, collective_id, after)


ADAM_BLOCK_ELEMS = 128 * 1024


def _adamw_sum(name, parts, wv, mv, vv):
    npart, r, c = parts.shape
    tr = r
    if r * c > ADAM_BLOCK_ELEMS and r % SUBLANES == 0:
        tr = SUBLANES
        while r % (tr * 2) == 0 and tr * 2 * c <= ADAM_BLOCK_ELEMS:
            tr *= 2
    bc1 = 1.0 - ADAM_B1 ** ADAM_STEP
    bc2 = 1.0 - ADAM_B2 ** ADAM_STEP

    def fn(pb, wb_, mb, vb):
        g = pb[0].astype(F32)
        for j in range(1, npart):
            g = g + pb[j].astype(F32)
        m_new = ADAM_B1 * mb + (1.0 - ADAM_B1) * g
        v_new = ADAM_B2 * vb + (1.0 - ADAM_B2) * (g * g)
        m_hat = m_new / bc1
        v_hat = v_new / bc2
        delta = -ADAM_LR * (m_hat / (jnp.sqrt(v_hat) + ADAM_EPS) + ADAM_WD * wb_)
        return g, delta, m_new, v_new

    row = pl.BlockSpec((tr, c), lambda i: (i, 0))
    return _blockwise(name, fn, [parts, wv, mv, vv],
                      [pl.BlockSpec((npart, tr, c), lambda i: (0, i, 0)), row, row, row],
                      [((r, c), F32)] * 4, [row] * 4, (r // tr,))


_SMALL = ["attn_norm", "lam_re", "lam_im", "log_dt", "b_re", "b_im", "c_re", "c_im", "ssm_d", "b_glu",
          "q_norm", "kv_norm", "son", "mon", "ffn_norm", "conv_b", "final_norm"]
_BIG = ["win", "wglu", "wuq", "wukv", "wout", "wup", "wdown", "conv_w"]
_ORDER = ["attn_norm", "win", "lam_re", "lam_im", "log_dt", "b_re", "b_im", "c_re", "c_im", "ssm_d", "wglu",
          "b_glu", "q_norm", "wuq", "kv_norm", "wukv", "son", "mon", "wout", "ffn_norm", "wup", "conv_w",
          "conv_b", "wdown", "final_norm"]


def _pack(arrs):
    flat = jnp.concatenate([a.reshape(-1).astype(F32) for a in arrs])
    pad = (-flat.shape[0]) % (LANES * LANES)
    return jnp.pad(flat, (0, pad)).reshape(-1, LANES)


def _unpack(packed, shapes):
    flat = packed.reshape(-1)
    out, off = [], 0
    for s in shapes:
        n = math.prod(s)
        out.append(flat[off:off + n].reshape(s))
        off += n
    return out


def kernel(x, positions, attn_norm_w, w_in, ssm_lambda_re, ssm_lambda_im, ssm_log_dt, ssm_b_re, ssm_b_im, ssm_c_re, ssm_c_im, ssm_d, ssm_w_glu, ssm_b_glu, mla_q_norm_w, mla_w_uq, mla_kv_norm_w, mla_w_ukv, ssm_out_norm_w, mla_out_norm_w, w_out, ffn_norm_w, ffn_w_up, ffn_conv_w, ffn_conv_b, ffn_w_down, final_norm_w, loss_target, m_attn_norm_w, m_w_in, m_ssm_lambda_re, m_ssm_lambda_im, m_ssm_log_dt, m_ssm_b_re, m_ssm_b_im, m_ssm_c_re, m_ssm_c_im, m_ssm_d, m_ssm_w_glu, m_ssm_b_glu, m_mla_q_norm_w, m_mla_w_uq, m_mla_kv_norm_w, m_mla_w_ukv, m_ssm_out_norm_w, m_mla_out_norm_w, m_w_out, m_ffn_norm_w, m_ffn_w_up, m_ffn_conv_w, m_ffn_conv_b, m_ffn_w_down, m_final_norm_w, v_attn_norm_w, v_w_in, v_ssm_lambda_re, v_ssm_lambda_im, v_ssm_log_dt, v_ssm_b_re, v_ssm_b_im, v_ssm_c_re, v_ssm_c_im, v_ssm_d, v_ssm_w_glu, v_ssm_b_glu, v_mla_q_norm_w, v_mla_w_uq, v_mla_kv_norm_w, v_mla_w_ukv, v_ssm_out_norm_w, v_mla_out_norm_w, v_w_out, v_ffn_norm_w, v_ffn_w_up, v_ffn_conv_w, v_ffn_conv_b, v_ffn_w_down, v_final_norm_w):
    wts = dict(attn_norm=attn_norm_w, win=w_in, lam_re=ssm_lambda_re, lam_im=ssm_lambda_im, log_dt=ssm_log_dt,
               b_re=ssm_b_re, b_im=ssm_b_im, c_re=ssm_c_re, c_im=ssm_c_im, ssm_d=ssm_d, wglu=ssm_w_glu,
               b_glu=ssm_b_glu, q_norm=mla_q_norm_w, wuq=mla_w_uq, kv_norm=mla_kv_norm_w, wukv=mla_w_ukv,
               son=ssm_out_norm_w, mon=mla_out_norm_w, wout=w_out, ffn_norm=ffn_norm_w, wup=ffn_w_up,
               conv_w=ffn_conv_w, conv_b=ffn_conv_b, wdown=ffn_w_down, final_norm=final_norm_w)
    moms = dict(zip(_ORDER, [m_attn_norm_w, m_w_in, m_ssm_lambda_re, m_ssm_lambda_im, m_ssm_log_dt, m_ssm_b_re,
                             m_ssm_b_im, m_ssm_c_re, m_ssm_c_im, m_ssm_d, m_ssm_w_glu, m_ssm_b_glu, m_mla_q_norm_w,
                             m_mla_w_uq, m_mla_kv_norm_w, m_mla_w_ukv, m_ssm_out_norm_w, m_mla_out_norm_w, m_w_out,
                             m_ffn_norm_w, m_ffn_w_up, m_ffn_conv_w, m_ffn_conv_b, m_ffn_w_down, m_final_norm_w]))
    vels = dict(zip(_ORDER, [v_attn_norm_w, v_w_in, v_ssm_lambda_re, v_ssm_lambda_im, v_ssm_log_dt, v_ssm_b_re,
                             v_ssm_b_im, v_ssm_c_re, v_ssm_c_im, v_ssm_d, v_ssm_w_glu, v_ssm_b_glu, v_mla_q_norm_w,
                             v_mla_w_uq, v_mla_kv_norm_w, v_mla_w_ukv, v_ssm_out_norm_w, v_mla_out_norm_w, v_w_out,
                             v_ffn_norm_w, v_ffn_w_up, v_ffn_conv_w, v_ffn_conv_b, v_ffn_w_down, v_final_norm_w]))
    seq, d = x.shape[1], x.shape[2]
    in_width = w_in.shape[2]
    in_pad = -(-in_width // LANES) * LANES
    q_cols = mla_w_uq.shape[2]
    q_pad = 2 * LANES

    (win_g,) = _all_gather("gather_w_in", [jnp.pad(w_in[0], ((0, 0), (0, in_pad - in_width))).astype(BF16)])
    wglu_g, wuq_g, wukv_g, wout_g, convw_g = _all_gather(
        "gather_mix", [ssm_w_glu[0].astype(BF16), jnp.pad(mla_w_uq[0], ((0, 0), (0, q_pad - q_cols))).astype(BF16),
                       mla_w_ukv[0].astype(BF16), w_out[0].astype(BF16), ffn_conv_w[0]], collective_id=0)
    (wup_g,) = _all_gather("gather_ffn_up", [ffn_w_up[0].astype(BF16)], collective_id=1)
    (wdown_g,) = _all_gather("gather_ffn_down", [ffn_w_down[0].astype(BF16)], collective_id=2)
    ns = N_DEV
    c_ff = wup_g.shape[2]
    w = dict(
        attn_norm=attn_norm_w, win=win_g.reshape(d, in_pad), lam_re=ssm_lambda_re, lam_im=ssm_lambda_im,
        log_dt=ssm_log_dt, b_re=ssm_b_re, b_im=ssm_b_im, c_re=ssm_c_re, c_im=ssm_c_im, ssm_d=ssm_d,
        wglu=wglu_g.reshape(d // 2, d // 2), b_glu=ssm_b_glu, q_norm=mla_q_norm_w, wuq=wuq_g,
        kv_norm=mla_kv_norm_w, wukv=wukv_g, son=ssm_out_norm_w, mon=mla_out_norm_w, wout=wout_g.reshape(d, d),
        ffn_norm=ffn_norm_w, wup=wup_g, conv_w=convw_g, conv_b=ffn_conv_b,
        wdown=wdown_g.reshape(ns // 2 * c_ff, d), final_norm=final_norm_w)

    shard_layout = dict(
        win=lambda a: a[:, :in_width].reshape(N_DEV, d // N_DEV, in_width),
        wglu=lambda a: a.reshape(N_DEV, d // 2 // N_DEV, d // 2),
        wuq=lambda a: a[:, :, :q_cols], wukv=lambda a: a, wout=lambda a: a.reshape(N_DEV, d // N_DEV, d),
        wup=lambda a: a, wdown=lambda a: a.reshape(N_DEV, c_ff // 2, d), conv_w=lambda a: a)
    recv = {}
    next_id = [3]

    last = [None]

    def exchange(not_before=(), **grads):
        names = list(grads)
        got = _exchange_partials("exchange_" + "_".join(names), [shard_layout[k](grads[k]) for k in names],
                                 collective_id=next_id[0], after=[a for a in (last[0], *not_before) if a is not None])
        next_id[0] += 1
        last[0] = got[-1]
        recv.update(zip(names, got))

    loss_part, grad_x, g = _local_step(x[0], positions[0], loss_target[0], w, emit=exchange)
    loss = lax.psum(loss_part, ("x", "y", "c"))
    small_shapes = [wts[k].shape for k in _SMALL]
    small_part = _pack([g[k] for k in _SMALL])
    small_all = _all_gather("gather_small_grads", [small_part], collective_id=next_id[0], after=[last[0]])[0]

    out = {}
    for k in _BIG:
        shp = wts[k].shape
        r, c = shp[-2], shp[-1]
        res = _adamw_sum("adamw_" + k, recv[k].reshape(N_DEV, r, c), wts[k].reshape(r, c),
                         moms[k].reshape(r, c), vels[k].reshape(r, c))
        out[k] = [a.reshape(shp) for a in res]
    sw_ = _pack([wts[k] for k in _SMALL])
    sm_ = _pack([moms[k] for k in _SMALL])
    sv_ = _pack([vels[k] for k in _SMALL])
    res = _adamw_sum("adamw_small", small_all, sw_, sm_, sv_)
    unpacked = [_unpack(a, small_shapes) for a in res]
    for i, k in enumerate(_SMALL):
        out[k] = [u[i] for u in unpacked]

    grad_x = grad_x.reshape(x.shape)
    return (loss, grad_x, *[out[k][0] for k in _ORDER], *[out[k][1] for k in _ORDER],
            *[out[k][2] for k in _ORDER], *[out[k][3] for k in _ORDER])
```

```python
import functools
import math

import jax
import jax.numpy as jnp
from jax import lax
from jax.experimental import pallas as pl
from jax.experimental.pallas import tpu as pltpu
from jax.experimental.pallas import tpu_sc as plsc

F32 = jnp.float32
BF16 = jnp.bfloat16
MESH = pl.DeviceIdType.MESH

N_DEV = 8
LANES = 128
SUBLANES = 8
VMEM_LIMIT = 48 * 1024 * 1024

SSM_GROUP = 16
SSM_STATE = 64
GROUPS_PER_BLOCK = LANES // SSM_GROUP
STATE_BLOCK = GROUPS_PER_BLOCK * SSM_STATE
QK_NOPE = 128
QK_ROPE = 64
V_DIM = 128
ROPE_THETA = 10000.0
RMS_EPS = 1e-6

ADAM_LR = 0.001
ADAM_B1 = 0.9
ADAM_B2 = 0.999
ADAM_EPS = 1e-08
ADAM_WD = 0.01
ADAM_STEP = 10

NN = ((1,), (0,))
NT = ((1,), (1,))
TN = ((0,), (0,))


def _cparams():
    return pltpu.CompilerParams(vmem_limit_bytes=VMEM_LIMIT)


def _tile(n, want):
    if n <= want:
        return n
    t = (want // LANES) * LANES
    while t >= LANES:
        if n % t == 0:
            return t
        t -= LANES
    return n


def _mm(name, a, b, *, grid, a_spec, b_spec, o_spec, out_shape, out_dtype, contract=NN,
        res=None, res_spec=None):
    nk = grid[-1]
    kaxis = len(grid) - 1
    acc_shape = tuple(d for d in o_spec.block_shape if d is not None)

    def body(*refs):
        a_ref, b_ref = refs[:2]
        r_ref = None if res is None else refs[2]
        o_ref = refs[2 if res is None else 3]
        part = lax.dot_general(a_ref[...].astype(BF16), b_ref[...].astype(BF16),
                               (contract, ((), ())), preferred_element_type=F32)
        if nk == 1:
            if r_ref is not None:
                part = part + r_ref[...].astype(F32)
            o_ref[...] = part.astype(o_ref.dtype)
            return
        acc = refs[-1]
        k = pl.program_id(kaxis)

        @pl.when(k == 0)
        def _():
            acc[...] = part

        @pl.when(k != 0)
        def _():
            acc[...] += part

        @pl.when(k == nk - 1)
        def _():
            r = acc[...]
            if r_ref is not None:
                r = r + r_ref[...].astype(F32)
            o_ref[...] = r.astype(o_ref.dtype)

    ins = [a, b] + ([] if res is None else [res])
    in_specs = [a_spec, b_spec] + ([] if res is None else [res_spec])
    return pl.pallas_call(
        body, name=name, grid=grid, in_specs=in_specs, out_specs=o_spec,
        out_shape=jax.ShapeDtypeStruct(out_shape, out_dtype),
        scratch_shapes=[pltpu.VMEM(acc_shape, F32)] if nk > 1 else [], compiler_params=_cparams(),
    )(*ins)


def _mm2d(name, a, b, contract, out_dtype, tm=1024, tn=1024, tk=2048, res=None):
    if contract == NN:
        (m, kk), n = a.shape, b.shape[1]
    elif contract == NT:
        (m, kk), n = a.shape, b.shape[0]
    else:
        (kk, m), n = a.shape, b.shape[1]
    tm, tn, tk = _tile(m, tm), _tile(n, tn), _tile(kk, tk)
    grid = (m // tm, n // tn, kk // tk)
    if contract == TN:
        a_spec = pl.BlockSpec((tk, tm), lambda i, j, k: (k, i))
    else:
        a_spec = pl.BlockSpec((tm, tk), lambda i, j, k: (i, k))
    if contract == NT:
        b_spec = pl.BlockSpec((tn, tk), lambda i, j, k: (j, k))
    else:
        b_spec = pl.BlockSpec((tk, tn), lambda i, j, k: (k, j))
    o_spec = pl.BlockSpec((tm, tn), lambda i, j, k: (i, j))
    res_spec = None
    if res is not None:
        if res.shape[0] == 1:
            res_spec = pl.BlockSpec((1, tn), lambda i, j, k: (0, j))
        else:
            res_spec = pl.BlockSpec((tm, tn), lambda i, j, k: (i, j))
    return _mm(name, a, b, grid=grid, a_spec=a_spec, b_spec=b_spec, o_spec=o_spec,
               out_shape=(m, n), out_dtype=out_dtype, contract=contract, res=res, res_spec=res_spec)


def _blockwise(name, fn, ins, in_specs, outs, out_specs, grid, n_acc=0, acc_all=True):
    n_in, n_out = len(ins), len(outs)
    n_plain = n_out - n_acc

    def body(*refs):
        vals = fn(*[r[...] for r in refs[:n_in]])
        if not isinstance(vals, (tuple, list)):
            vals = (vals,)
        o_refs = refs[n_in:n_in + n_out]
        for r, v in zip(o_refs[:n_plain], vals[:n_plain]):
            r[...] = v.astype(r.dtype)
        if n_acc:
            if acc_all:
                first = functools.reduce(jnp.logical_and, [pl.program_id(d) == 0 for d in range(len(grid))])
            else:
                first = pl.program_id(len(grid) - 1) == 0

            @pl.when(first)
            def _():
                for r, v in zip(o_refs[n_plain:], vals[n_plain:]):
                    r[...] = v.astype(r.dtype)

            @pl.when(jnp.logical_not(first))
            def _():
                for r, v in zip(o_refs[n_plain:], vals[n_plain:]):
                    r[...] += v.astype(r.dtype)

    return pl.pallas_call(
        body, name=name, grid=grid, in_specs=in_specs, out_specs=out_specs,
        out_shape=[jax.ShapeDtypeStruct(s, d) for s, d in outs], compiler_params=_cparams(),
    )(*ins)


def _row_spec(t, c):
    return pl.BlockSpec((t, c), lambda i: (i, 0))


def _full_spec(shape):
    nd = len(shape)
    return pl.BlockSpec(tuple(shape), lambda *g: (0,) * nd)


def _rms(xf, w):
    return xf * lax.rsqrt(jnp.mean(xf * xf, axis=-1, keepdims=True) + RMS_EPS) * w


def _rms_bwd(xf, w, dy):
    _, vjp = jax.vjp(_rms, xf, w)
    return vjp(dy)


def _s5_disc(lr, li, ldt, bre, bim):
    dt = jnp.exp(ldt)
    mag = jnp.exp(lr * dt)
    ar = mag * jnp.cos(li * dt)
    ai = mag * jnp.sin(li * dt)
    nr, ni = ar - 1.0, ai
    den = lr * lr + li * li
    zr = (nr * lr + ni * li) / den
    zi = (ni * lr - nr * li) / den
    return ar, ai, zr * bre - zi * bim, zr * bim + zi * bre


def _s5_prep(lr, li, ldt, bre, bim):
    def body(lr_r, li_r, ldt_r, bre_r, bim_r, ar_r, ai_r, br_r, bi_r):
        ar, ai, br, bi = _s5_disc(lr_r[...], li_r[...], ldt_r[...], bre_r[...], bim_r[...])
        ar_r[...] = ar
        ai_r[...] = ai
        br_r[...] = br
        bi_r[...] = bi

    sd = jax.ShapeDtypeStruct
    return pl.pallas_call(
        body, name="s5_prep",
        out_shape=[sd(lr.shape, F32), sd(lr.shape, F32), sd(bre.shape, F32), sd(bre.shape, F32)],
        compiler_params=_cparams(),
    )(lr, li, ldt, bre, bim)


def _s5_prep_bwd(lr, li, ldt, bre, bim, dar, dai, dbr, dbi):
    def body(lr_r, li_r, ldt_r, bre_r, bim_r, dar_r, dai_r, dbr_r, dbi_r, o0, o1, o2, o3, o4):
        _, vjp = jax.vjp(_s5_disc, lr_r[...], li_r[...], ldt_r[...], bre_r[...], bim_r[...])
        g = vjp((dar_r[...], dai_r[...], dbr_r[...], dbi_r[...]))
        for o, v in zip((o0, o1, o2, o3, o4), g):
            o[...] = v

    sd = jax.ShapeDtypeStruct
    return pl.pallas_call(
        body, name="s5_prep_bwd",
        out_shape=[sd(lr.shape, F32), sd(li.shape, F32), sd(ldt.shape, F32), sd(bre.shape, F32), sd(bim.shape, F32)],
        compiler_params=_cparams(),
    )(lr, li, ldt, bre, bim, dar, dai, dbr, dbi)


SCAN_T = 256


def _scan_tables(ar, ai, tab_r, tab_i, sub, reverse):
    pr, pi = ar, ai
    for k in range(sub):
        row = sub - 1 - k if reverse else k
        tab_r[row:row + 1, :] = pr
        tab_i[row:row + 1, :] = pi
        pr, pi = ar * pr - ai * pi, ar * pi + ai * pr


def _pack_matrix(t_blk, dtype):
    sub = t_blk // SUBLANES
    dst = jnp.arange(t_blk)
    src = (dst % SUBLANES) * sub + dst // SUBLANES
    return (src[:, None] == jnp.arange(t_blk)[None, :]).astype(dtype)


def _permute_rows_f32(pm, x):
    hi = x.astype(BF16)
    r1 = x - hi.astype(F32)
    mid = r1.astype(BF16)
    lo = (r1 - mid.astype(F32)).astype(BF16)
    dot = lambda v: jnp.dot(pm, v, preferred_element_type=F32)
    return dot(hi) + dot(mid) + dot(lo)


def _scan_block(x, loc, ar, ai, st, tab_r, tab_i, sub, reverse):
    hb = STATE_BLOCK
    a8r = jnp.broadcast_to(ar, (SUBLANES, hb))
    a8i = jnp.broadcast_to(ai, (SUBLANES, hb))
    sr = jnp.zeros((SUBLANES, hb), F32)
    si = jnp.zeros((SUBLANES, hb), F32)
    steps = range(sub - 1, -1, -1) if reverse else range(sub)
    for t in steps:
        rows = slice(t * SUBLANES, (t + 1) * SUBLANES)
        sr, si = a8r * sr - a8i * si + x[rows, :hb], a8r * si + a8i * sr + x[rows, hb:]
        loc[rows, :hb] = sr
        loc[rows, hb:] = si
    cr, ci = st[0:1, :], st[1:2, :]
    far = 0 if reverse else sub - 1
    fr, fi = tab_r[far:far + 1, :], tab_i[far:far + 1, :]
    ent_r, ent_i = [None] * SUBLANES, [None] * SUBLANES
    for c in (range(SUBLANES - 1, -1, -1) if reverse else range(SUBLANES)):
        ent_r[c], ent_i[c] = cr, ci
        cr, ci = sr[c:c + 1, :] + (fr * cr - fi * ci), si[c:c + 1, :] + (fr * ci + fi * cr)
    st[0:1, :] = cr
    st[1:2, :] = ci
    c8r = jnp.concatenate(ent_r, axis=0)
    c8i = jnp.concatenate(ent_i, axis=0)
    out = []
    for t in range(sub):
        rows = slice(t * SUBLANES, (t + 1) * SUBLANES)
        tr, ti = tab_r[t:t + 1, :], tab_i[t:t + 1, :]
        out.append(jnp.concatenate([loc[rows, :hb] + (tr * c8r - ti * c8i), loc[rows, hb:] + (tr * c8i + ti * c8r)],
                                   axis=1))
    return jnp.concatenate(out, axis=0)


def _scan_scratch(t_blk, sub, hb):
    return [pltpu.VMEM((SUBLANES, hb), F32), pltpu.VMEM((sub, hb), F32), pltpu.VMEM((sub, hb), F32),
            pltpu.VMEM((t_blk, 2 * hb), F32)]


def _ssm_fwd(proj, wb, wc, a):
    seq = proj.shape[0]
    nj = wb.shape[0]
    w2 = 2 * STATE_BLOCK
    hb = STATE_BLOCK
    t_blk = min(SCAN_T, seq)
    sub = t_blk // SUBLANES
    pm = _pack_matrix(t_blk, BF16)

    def body(u_ref, wb_ref, wc_ref, a_ref, pm_ref, pmt_ref, s_ref, y_ref, st, tab_r, tab_i, loc):
        ar = a_ref[:, :hb]
        ai = a_ref[:, hb:]

        @pl.when(pl.program_id(1) == 0)
        def _():
            st[...] = jnp.zeros_like(st)
            _scan_tables(ar, ai, tab_r, tab_i, sub, False)

        up = jnp.dot(pm_ref[...], u_ref[...].astype(BF16), preferred_element_type=F32).astype(BF16)
        bu = jnp.dot(up, wb_ref[...], preferred_element_type=F32)
        s = _scan_block(bu, loc, ar, ai, st, tab_r, tab_i, sub, False)
        s_ref[...] = s
        yp = jnp.dot(s.astype(BF16), wc_ref[...], preferred_element_type=F32)
        y_ref[...] = _permute_rows_f32(pmt_ref[...], yp)

    sd = jax.ShapeDtypeStruct
    return pl.pallas_call(
        body, name="ssm_fwd", grid=(nj, seq // t_blk),
        in_specs=[pl.BlockSpec((t_blk, LANES), lambda j, i: (i, j)),
                  pl.BlockSpec((None, LANES, w2), lambda j, i: (j, 0, 0)),
                  pl.BlockSpec((None, w2, LANES), lambda j, i: (j, 0, 0)),
                  pl.BlockSpec((1, w2), lambda j, i: (0, j)),
                  _full_spec((t_blk, t_blk)), _full_spec((t_blk, t_blk))],
        out_specs=[pl.BlockSpec((t_blk, w2), lambda j, i: (i, j)), pl.BlockSpec((t_blk, LANES), lambda j, i: (i, j))],
        out_shape=[sd((seq, nj * w2), F32), sd((seq, nj * LANES), F32)],
        scratch_shapes=_scan_scratch(t_blk, sub, hb), compiler_params=_cparams(),
    )(proj, wb, wc, a, pm, pm.T)


def _ssm_bwd(dy, s, proj, du1, wb, wc, a):
    seq = dy.shape[0]
    nj = wb.shape[0]
    w2 = 2 * STATE_BLOCK
    hb = STATE_BLOCK
    t_blk = min(SCAN_T, seq)
    sub = t_blk // SUBLANES
    nb = seq // t_blk
    pm = _pack_matrix(t_blk, BF16)

    def body(dy_ref, s_ref, sprev_ref, u_ref, du1_ref, wb_ref, wc_ref, a_ref, pm_ref, pmt_ref,
             du_ref, dwb_ref, dwc_ref, da_ref, st, tab_r, tab_i, loc):
        ib = pl.program_id(1)
        ar = a_ref[:, :hb]
        ai = -a_ref[:, hb:]

        @pl.when(ib == 0)
        def _():
            st[...] = jnp.zeros_like(st)
            _scan_tables(ar, ai, tab_r, tab_i, sub, True)

        pmv = pm_ref[...]
        dyp = jnp.dot(pmv, dy_ref[...], preferred_element_type=F32).astype(BF16)
        up = jnp.dot(pmv, u_ref[...].astype(BF16), preferred_element_type=F32).astype(BF16)
        ds = lax.dot_general(dyp, wc_ref[...], (NT, ((), ())), preferred_element_type=F32)
        lam = _scan_block(ds, loc, ar, ai, st, tab_r, tab_i, sub, True)
        lamb = lam.astype(BF16)
        du = lax.dot_general(lamb, wb_ref[...], (NT, ((), ())), preferred_element_type=F32)
        du_ref[...] = (_permute_rows_f32(pmt_ref[...], du) + du1_ref[...]).astype(du_ref.dtype)
        sv = s_ref[...]
        dwb = lax.dot_general(up, lamb, (TN, ((), ())), preferred_element_type=F32)
        dwc = lax.dot_general(sv.astype(BF16), dyp, (TN, ((), ())), preferred_element_type=F32)

        prev_last = sprev_ref[SUBLANES - 1:SUBLANES, :]
        prev_last = jnp.where(ib == nb - 1, jnp.zeros_like(prev_last), prev_last)
        tail = sv[t_blk - SUBLANES:, :]
        sl = lax.broadcasted_iota(jnp.int32, tail.shape, 0)
        head = jnp.where(sl >= 1, pltpu.roll(tail, 1, 0), prev_last)
        s_sh = jnp.concatenate([head, sv[:t_blk - SUBLANES, :]], axis=0)
        lam_r, lam_i = lam[:, :hb], lam[:, hb:]
        sr_, si_ = s_sh[:, :hb], s_sh[:, hb:]
        dar = jnp.sum(lam_r * sr_ + lam_i * si_, axis=0, keepdims=True)
        dai = jnp.sum(lam_i * sr_ - lam_r * si_, axis=0, keepdims=True)
        contrib = jnp.concatenate([dar, dai], axis=1)

        @pl.when(ib == 0)
        def _():
            da_ref[...] = contrib
            dwb_ref[...] = dwb
            dwc_ref[...] = dwc

        @pl.when(ib != 0)
        def _():
            da_ref[...] += contrib
            dwb_ref[...] += dwb
            dwc_ref[...] += dwc

    blk = lambda j, i: (nb - 1 - i, j)
    prev_blk = lambda j, i: (jnp.maximum((nb - 1 - i) * sub - 1, 0), j)
    sd = jax.ShapeDtypeStruct
    return pl.pallas_call(
        body, name="ssm_bwd", grid=(nj, nb),
        in_specs=[pl.BlockSpec((t_blk, LANES), blk), pl.BlockSpec((t_blk, w2), blk),
                  pl.BlockSpec((SUBLANES, w2), prev_blk), pl.BlockSpec((t_blk, LANES), blk),
                  pl.BlockSpec((t_blk, LANES), blk),
                  pl.BlockSpec((None, LANES, w2), lambda j, i: (j, 0, 0)),
                  pl.BlockSpec((None, w2, LANES), lambda j, i: (j, 0, 0)),
                  pl.BlockSpec((1, w2), lambda j, i: (0, j)),
                  _full_spec((t_blk, t_blk)), _full_spec((t_blk, t_blk))],
        out_specs=[pl.BlockSpec((t_blk, LANES), blk),
                   pl.BlockSpec((None, LANES, w2), lambda j, i: (j, 0, 0)),
                   pl.BlockSpec((None, w2, LANES), lambda j, i: (j, 0, 0)),
                   pl.BlockSpec((1, w2), lambda j, i: (0, j))],
        out_shape=[sd((seq, nj * LANES), BF16), sd((nj, LANES, w2), F32), sd((nj, w2, LANES), F32),
                   sd((1, nj * w2), F32)],
        scratch_shapes=_scan_scratch(t_blk, sub, hb), compiler_params=_cparams(),
    )(dy, s, s, proj, du1, wb, wc, a, pm, pm.T)


def _rope128(x, cos, sa, sb):
    return x * cos + pltpu.roll(x, 96, 1) * sa + pltpu.roll(x, 32, 1) * sb


def _rope128_t(dy, cos, sa, sb):
    return dy * cos + pltpu.roll(dy * sa, 32, 1) + pltpu.roll(dy * sb, 96, 1)


ATT_BQ = 256


ATT_STRIP = 16


def _scores_to(s_scr, qn, qp, kn, kp, kend):
    s_scr[:, :kend] = (lax.dot_general(qn, kn, (NT, ((), ())), preferred_element_type=F32)
                       + lax.dot_general(qp, kp, (NT, ((), ())), preferred_element_type=F32))


def _softmax_rows(s, row0, r0, scale):
    s = s * scale
    diag = s[:, r0:]
    row = row0 + lax.broadcasted_iota(jnp.int32, diag.shape, 0)
    col = lax.broadcasted_iota(jnp.int32, diag.shape, 1)
    diag = jnp.where(col <= row, diag, jnp.finfo(F32).min)
    s = diag if r0 == 0 else jnp.concatenate([s[:, :r0], diag], axis=1)
    m = jnp.max(s, axis=-1, keepdims=True)
    e = jnp.exp(s - m)
    return e / jnp.sum(e, axis=-1, keepdims=True)


def _attn_specs(seq):
    tab = pl.BlockSpec((seq, LANES), lambda h: (0, 0))
    return [pl.BlockSpec((None, seq, 256), lambda h: (h, 0, 0)), pl.BlockSpec((None, seq, 128), lambda h: (h, 0, 0)),
            pl.BlockSpec((None, seq, 128), lambda h: (h, 0, 1)), tab, tab, tab, tab]


def _attn_fwd(q_raw, kv, kpe, cos, sa, sb):
    nh, seq, _ = q_raw.shape
    bq = min(ATT_BQ, seq)
    scale = (QK_NOPE + QK_ROPE) ** -0.5

    def body(q_ref, kn_ref, v_ref, kp_ref, cos_ref, sa_ref, sb_ref, o_ref, s_scr, p_scr):
        for r0 in range(0, seq, bq):
            rows, kend = pl.ds(r0, bq), r0 + bq
            qn = q_ref[rows, :QK_NOPE].astype(BF16)
            qp = _rope128(q_ref[rows, QK_NOPE:], cos_ref[rows, :], sa_ref[rows, :], sb_ref[rows, :]).astype(BF16)
            _scores_to(s_scr, qn, qp, kn_ref[:kend, :], kp_ref[:kend, :], kend)

            def strip(i, carry, kend=kend, r0=r0):
                st = pl.ds(pl.multiple_of(i * ATT_STRIP, ATT_STRIP), ATT_STRIP)
                p = _softmax_rows(s_scr[st, :kend], i * ATT_STRIP, r0, scale)
                p_scr[st, :kend] = p.astype(BF16)
                return carry

            lax.fori_loop(0, bq // ATT_STRIP, strip, 0)
            o_ref[rows, :] = jnp.dot(p_scr[:, :kend], v_ref[:kend, :], preferred_element_type=F32)

    return pl.pallas_call(
        body, name="attn_fwd", grid=(nh,), in_specs=_attn_specs(seq),
        out_specs=pl.BlockSpec((seq, V_DIM), lambda h: (0, h)),
        out_shape=jax.ShapeDtypeStruct((seq, nh * V_DIM), F32),
        scratch_shapes=[pltpu.VMEM((bq, seq), F32), pltpu.VMEM((bq, seq), BF16)], compiler_params=_cparams(),
    )(q_raw, kv, kv, kpe, cos, sa, sb)


def _attn_bwd(q_raw, kv, kpe, cos, sa, sb, do):
    nh, seq, _ = q_raw.shape
    bq = min(ATT_BQ, seq)
    scale = (QK_NOPE + QK_ROPE) ** -0.5

    def body(q_ref, kn_ref, v_ref, kp_ref, cos_ref, sa_ref, sb_ref, do_ref, dq_ref, dkv_ref, dkp_ref,
             s_scr, dp_scr, p_scr, ds_scr):
        dkv_ref[...] = jnp.zeros_like(dkv_ref)
        dkp_ref[...] = jnp.zeros_like(dkp_ref)
        for r0 in range(0, seq, bq):
            rows, kend = pl.ds(r0, bq), r0 + bq
            cos_b, sa_b, sb_b = cos_ref[rows, :], sa_ref[rows, :], sb_ref[rows, :]
            qn = q_ref[rows, :QK_NOPE].astype(BF16)
            qp = _rope128(q_ref[rows, QK_NOPE:], cos_b, sa_b, sb_b).astype(BF16)
            kn, v, kp = kn_ref[:kend, :], v_ref[:kend, :], kp_ref[:kend, :]
            dob = do_ref[rows, :].astype(BF16)
            _scores_to(s_scr, qn, qp, kn, kp, kend)
            dp_scr[:, :kend] = lax.dot_general(dob, v, (NT, ((), ())), preferred_element_type=F32)

            def strip(i, carry, kend=kend, r0=r0):
                st = pl.ds(pl.multiple_of(i * ATT_STRIP, ATT_STRIP), ATT_STRIP)
                p = _softmax_rows(s_scr[st, :kend], i * ATT_STRIP, r0, scale)
                dp = dp_scr[st, :kend]
                ds = p * (dp - jnp.sum(p * dp, axis=-1, keepdims=True)) * scale
                p_scr[st, :kend] = p.astype(BF16)
                ds_scr[st, :kend] = ds.astype(BF16)
                return carry

            lax.fori_loop(0, bq // ATT_STRIP, strip, 0)
            dsb = ds_scr[:, :kend]
            dq_ref[rows, :QK_NOPE] = jnp.dot(dsb, kn, preferred_element_type=F32).astype(dq_ref.dtype)
            dqp = jnp.dot(dsb, kp, preferred_element_type=F32)
            dq_ref[rows, QK_NOPE:] = _rope128_t(dqp, cos_b, sa_b, sb_b).astype(dq_ref.dtype)
            dkv_ref[:kend, :QK_NOPE] += lax.dot_general(dsb, qn, (TN, ((), ())), preferred_element_type=F32)
            dkv_ref[:kend, QK_NOPE:] += lax.dot_general(p_scr[:, :kend], dob, (TN, ((), ())),
                                                        preferred_element_type=F32)
            dkp_ref[:kend, :] += lax.dot_general(dsb, qp, (TN, ((), ())), preferred_element_type=F32)

    sd = jax.ShapeDtypeStruct
    return pl.pallas_call(
        body, name="attn_bwd", grid=(nh,),
        in_specs=_attn_specs(seq) + [pl.BlockSpec((seq, V_DIM), lambda h: (0, h))],
        out_specs=[pl.BlockSpec((None, seq, 256), lambda h: (h, 0, 0)),
                   pl.BlockSpec((None, seq, 256), lambda h: (h, 0, 0)),
                   pl.BlockSpec((None, seq, 128), lambda h: (h, 0, 0))],
        out_shape=[sd((nh, seq, 256), BF16), sd((nh, seq, 256), F32), sd((nh, seq, 128), F32)],
        scratch_shapes=[pltpu.VMEM((bq, seq), F32), pltpu.VMEM((bq, seq), F32), pltpu.VMEM((bq, seq), BF16),
                        pltpu.VMEM((bq, seq), BF16)],
        compiler_params=_cparams(),
    )(q_raw, kv, kv, kpe, cos, sa, sb, do)


def _conv3(a, w, b):
    rows = lax.broadcasted_iota(jnp.int32, a.shape, 0)
    a1 = jnp.where(rows >= 1, pltpu.roll(a, 1, 0), 0.0)
    a2 = jnp.where(rows >= 2, pltpu.roll(a, 2, 0), 0.0)
    return w[2:3] * a + w[1:2] * a1 + w[0:1] * a2 + b, a1, a2


def _conv_gate_fwd(a, cw, cb):
    half, _, seq, c = a.shape
    nc = c // LANES

    def fn(pair, wg, wv, bg, bv):
        gc, _, _ = _conv3(pair[0], wg, bg)
        vc, _, _ = _conv3(pair[1], wv, bv)
        return gc * jax.nn.sigmoid(gc) * vc

    def w_spec(off, r):
        return pl.BlockSpec((None, r, LANES), lambda k, j: (k + off, 0, j))

    return _blockwise(
        "conv_gate_fwd", fn, [a, cw, cw, cb, cb],
        [pl.BlockSpec((None, 2, seq, LANES), lambda k, j: (k, 0, 0, j)),
         w_spec(0, 3), w_spec(half, 3), w_spec(0, 1), w_spec(half, 1)],
        [((seq, half * c), BF16)], [pl.BlockSpec((seq, LANES), lambda k, j: (0, k * nc + j))],
        grid=(half, nc))[0]


def _conv_gate_bwd(a, cw, cb, dm):
    half, _, seq, c = a.shape
    nc = c // LANES

    def body(a_ref, wg_ref, wv_ref, bg_ref, bv_ref, dm_ref, da_ref, dw_ref, db_ref):
        dmv = dm_ref[...]
        rows = lax.broadcasted_iota(jnp.int32, dmv.shape, 0)
        ga, wg = a_ref[0], wg_ref[...]
        va, wv = a_ref[1], wv_ref[...]
        gc, g1, g2 = _conv3(ga, wg, bg_ref[...])
        vc, v1, v2 = _conv3(va, wv, bv_ref[...])
        sg = jax.nn.sigmoid(gc)
        dms = dmv * sg
        d_val = dms * gc
        d_gate = dms * vc * (1.0 + gc * (1.0 - sg))

        def back(r, dc, own, a1, a2, w):
            up1 = jnp.where(rows < seq - 1, pltpu.roll(dc, seq - 1, 0), 0.0)
            up2 = jnp.where(rows < seq - 2, pltpu.roll(dc, seq - 2, 0), 0.0)
            da_ref[r] = (w[2:3] * dc + w[1:2] * up1 + w[0:1] * up2).astype(da_ref.dtype)
            dw_ref[r, 0:1, :] = jnp.sum(dc * a2, axis=0, keepdims=True)
            dw_ref[r, 1:2, :] = jnp.sum(dc * a1, axis=0, keepdims=True)
            dw_ref[r, 2:3, :] = jnp.sum(dc * own, axis=0, keepdims=True)
            db_ref[r] = jnp.sum(dc, axis=0, keepdims=True)

        back(0, d_gate, ga, g1, g2, wg)
        back(1, d_val, va, v1, v2, wv)

    def w_spec(off, r):
        return pl.BlockSpec((None, r, LANES), lambda k, j: (k + off, 0, j))

    def pair_spec(r):
        return pl.BlockSpec((None, 2, r, LANES), lambda k, j: (k, 0, 0, j))

    sd = jax.ShapeDtypeStruct
    return pl.pallas_call(
        body, name="conv_gate_bwd", grid=(half, nc),
        in_specs=[pair_spec(seq), w_spec(0, 3), w_spec(half, 3), w_spec(0, 1), w_spec(half, 1),
                  pl.BlockSpec((seq, LANES), lambda k, j: (0, k * nc + j))],
        out_specs=[pair_spec(seq), pair_spec(3), pair_spec(1)],
        out_shape=[sd((half, 2, seq, c), BF16), sd((half, 2, 3, c), F32), sd((half, 2, 1, c), F32)],
        compiler_params=_cparams(),
    )(a, cw, cw, cb, cb, dm)


ROW_T = 256


def _local_step(x, positions, target, w, emit=lambda **grads: None):
    seq, d = x.shape
    t_row = min(ROW_T, seq)
    nrow = seq // t_row
    ssm_w = d // 2
    nj = ssm_w // LANES
    n_groups = ssm_w // SSM_GROUP
    nh = w["wuq"].shape[0]
    q_rank = w["wuq"].shape[1]
    kv_rank = w["wukv"].shape[1]
    ns = w["wup"].shape[0]
    c_ff = w["wup"].shape[2]
    in_pad = w["win"].shape[1]
    tm = min(1024, seq)
    nm = seq // tm
    sw = 2 * STATE_BLOCK
    g1 = (nrow,)

    lr3 = w["lam_re"].reshape(n_groups, 1, SSM_STATE)
    li3 = w["lam_im"].reshape(n_groups, 1, SSM_STATE)
    ldt3 = w["log_dt"].reshape(n_groups, 1, 1)
    bt_re = jnp.swapaxes(w["b_re"].reshape(n_groups, SSM_STATE, SSM_GROUP), 1, 2)
    bt_im = jnp.swapaxes(w["b_im"].reshape(n_groups, SSM_STATE, SSM_GROUP), 1, 2)
    abar_re, abar_im, bbt_re, bbt_im = _s5_prep(lr3, li3, ldt3, bt_re, bt_im)
    eye = jnp.eye(GROUPS_PER_BLOCK, dtype=F32)

    def blockdiag_in(bb):
        t = bb.reshape(nj, GROUPS_PER_BLOCK, SSM_GROUP, SSM_STATE)
        return jnp.einsum("jghp,gk->jghkp", t, eye).reshape(nj, LANES, STATE_BLOCK)

    def blockdiag_in_t(dwb):
        t = dwb.reshape(nj, GROUPS_PER_BLOCK, SSM_GROUP, GROUPS_PER_BLOCK, SSM_STATE)
        return jnp.einsum("jghkp,gk->jghp", t, eye).reshape(n_groups, SSM_GROUP, SSM_STATE)

    def blockdiag_out(cc):
        t = cc.reshape(nj, GROUPS_PER_BLOCK, SSM_GROUP, SSM_STATE)
        return jnp.einsum("jghp,gk->jkpgh", t, eye).reshape(nj, STATE_BLOCK, LANES)

    def blockdiag_out_t(dwc):
        t = dwc.reshape(nj, GROUPS_PER_BLOCK, SSM_STATE, GROUPS_PER_BLOCK, SSM_GROUP)
        return jnp.einsum("jkpgh,gk->jghp", t, eye).reshape(n_groups, SSM_GROUP, SSM_STATE)

    c_re = w["c_re"].reshape(n_groups, SSM_GROUP, SSM_STATE)
    c_im = w["c_im"].reshape(n_groups, SSM_GROUP, SSM_STATE)
    wb = jnp.concatenate([blockdiag_in(bbt_re), blockdiag_in(bbt_im)], axis=2).astype(BF16)
    wc = jnp.concatenate([blockdiag_out(c_re), -blockdiag_out(c_im)], axis=1).astype(BF16)
    a_lay = jnp.concatenate([abar_re.reshape(nj, 1, STATE_BLOCK), abar_im.reshape(nj, 1, STATE_BLOCK)],
                            axis=1).reshape(1, nj * sw)

    attn_w = w["attn_norm"]
    hn = _blockwise("norm1", lambda xb, wv: _rms(xb, wv), [x, attn_w], [_row_spec(t_row, d), _full_spec((1, d))],
                    [((seq, d), BF16)], [_row_spec(t_row, d)], g1)[0]
    proj = _mm2d("proj", hn, w["win"], NN, F32, tn=640)

    s_all, ylin = _ssm_fwd(proj, wb, wc, a_lay)
    u_spec = pl.BlockSpec((t_row, ssm_w), lambda i: (i, 0))

    def ypre_fn(yl, ub, dsk):
        yp = yl + dsk * ub
        return yp, jax.nn.gelu(yp)

    y_pre, yg = _blockwise("ssm_gelu", ypre_fn, [ylin, proj, w["ssm_d"]],
                           [_row_spec(t_row, ssm_w), u_spec, _full_spec((1, ssm_w))],
                           [((seq, ssm_w), F32), ((seq, ssm_w), BF16)],
                           [_row_spec(t_row, ssm_w)] * 2, g1)
    z = _mm2d("ssm_glu", yg, w["wglu"], NN, F32, res=w["b_glu"])
    y_ssm = _blockwise("ssm_gate", lambda yp, zb: jax.nn.gelu(yp) * jax.nn.sigmoid(zb), [y_pre, z],
                       [_row_spec(t_row, ssm_w)] * 2, [((seq, ssm_w), F32)], [_row_spec(t_row, ssm_w)], g1)[0]

    cq_off, ckv_off, kpe_off = ssm_w, ssm_w + q_rank, ssm_w + q_rank + kv_rank
    c_q = proj[:, cq_off:ckv_off]
    c_kv = proj[:, ckv_off:kpe_off]
    kpe_raw = proj[:, kpe_off:kpe_off + LANES]
    pos_b = jnp.broadcast_to(positions.astype(F32)[:, None], (seq, LANES))
    inv_freq = ROPE_THETA ** (-jnp.arange(0, QK_ROPE, 2, dtype=F32) / QK_ROPE)
    inv128 = jnp.tile(inv_freq, 4).reshape(1, LANES)

    def mla_prep_fn(cq, ckv, kp, pb, inv, wq, wkv):
        ang = pb * inv
        lane = lax.broadcasted_iota(jnp.int32, ang.shape, 1)
        cs, sn = jnp.cos(ang), jnp.sin(ang)
        cos = jnp.where(lane < QK_ROPE, cs, 0.0)
        sa = jnp.where(lane < QK_ROPE // 2, -sn, 0.0)
        sb = jnp.where(jnp.logical_and(lane >= QK_ROPE // 2, lane < QK_ROPE), sn, 0.0)
        return _rms(cq, wq), _rms(ckv, wkv), _rope128(kp, cos, sa, sb), cos, sa, sb

    qn, kvn, kpe, cos_t, sa_t, sb_t = _blockwise(
        "mla_prep", mla_prep_fn, [c_q, c_kv, kpe_raw, pos_b, inv128, w["q_norm"], w["kv_norm"]],
        [_row_spec(t_row, q_rank), _row_spec(t_row, kv_rank), _row_spec(t_row, LANES), _row_spec(t_row, LANES),
         _full_spec((1, LANES)), _full_spec((1, q_rank)), _full_spec((1, kv_rank))],
        [((seq, q_rank), BF16), ((seq, kv_rank), BF16), ((seq, LANES), BF16)] + [((seq, LANES), F32)] * 3,
        [_row_spec(t_row, q_rank), _row_spec(t_row, kv_rank)] + [_row_spec(t_row, LANES)] * 4, g1)

    def head_mm(name, act, wh, out_dtype):
        kdim, ndim = wh.shape[1], wh.shape[2]
        return _mm(name, act, wh, grid=(nh, 1, 1),
                   a_spec=pl.BlockSpec((seq, kdim), lambda h, i, k: (i, 0)),
                   b_spec=pl.BlockSpec((None, kdim, ndim), lambda h, i, k: (h, 0, 0)),
                   o_spec=pl.BlockSpec((None, seq, ndim), lambda h, i, k: (h, i, 0)),
                   out_shape=(nh, seq, ndim), out_dtype=out_dtype)

    q_raw = head_mm("mla_q", qn, w["wuq"], F32)
    kv = head_mm("mla_kv", kvn, w["wukv"], BF16)
    y_mla = _attn_fwd(q_raw, kv, kpe, cos_t, sa_t, sb_t)
    mla_w = nh * V_DIM

    def outnorm_fn(ys, ym, ws, wm):
        return jnp.concatenate([_rms(ys, ws), _rms(ym, wm)], axis=1)

    ycat = _blockwise("out_norm", outnorm_fn, [y_ssm, y_mla, w["son"], w["mon"]],
                      [_row_spec(t_row, ssm_w), _row_spec(t_row, mla_w), _full_spec((1, ssm_w)), _full_spec((1, mla_w))],
                      [((seq, d), BF16)], [_row_spec(t_row, d)], g1)[0]
    h1 = _mm2d("out_proj", ycat, w["wout"], NN, F32, res=x)

    hn2 = _blockwise("norm2", lambda hb, wv: _rms(hb, wv), [h1, w["ffn_norm"]],
                     [_row_spec(t_row, d), _full_spec((1, d))], [((seq, d), BF16)], [_row_spec(t_row, d)], g1)[0]
    tku = d
    half = ns // 2
    a_ff = _mm("ffn_up", hn2, w["wup"], grid=(ns, nm, d // tku),
               a_spec=pl.BlockSpec((tm, tku), lambda s, i, k: (i, k)),
               b_spec=pl.BlockSpec((None, tku, c_ff), lambda s, i, k: (s, k, 0)),
               o_spec=pl.BlockSpec((None, None, tm, c_ff), lambda s, i, k: (s % half, s // half, i, 0)),
               out_shape=(half, 2, seq, c_ff), out_dtype=F32)
    cb3 = w["conv_b"].reshape(ns, 1, c_ff)
    m_ff = _conv_gate_fwd(a_ff, w["conv_w"], cb3)
    d_ff = half * c_ff
    wdn = w["wdown"]
    tnd = _tile(d, 1024)
    tmx, tnx = min(1024, seq), _tile(d, 1024)
    h2 = _mm2d("ffn_down", m_ff, wdn, NN, F32, tm=512, tn=512, tk=d_ff, res=h1)

    def loss_fn(hb, tb, wv):
        def f(hh, ww):
            err = _rms(hh, ww) - tb
            return 0.5 * jnp.sum(jnp.mean(err * err, axis=-1))

        lossv, (dh, dw) = jax.value_and_grad(f, argnums=(0, 1))(hb, wv)
        return dh, dh, jnp.full((1, LANES), lossv, F32), dw

    fin_w = w["final_norm"].reshape(1, d)
    dh2, dh2b, loss_acc, g_final = _blockwise(
        "loss_head", loss_fn, [h2, target, fin_w], [_row_spec(t_row, d), _row_spec(t_row, d), _full_spec((1, d))],
        [((seq, d), F32), ((seq, d), BF16), ((1, LANES), F32), ((1, d), F32)],
        [_row_spec(t_row, d), _row_spec(t_row, d), _full_spec((1, LANES)), _full_spec((1, d))], g1, n_acc=2)
    loss = loss_acc[0, 0]

    dm = _mm2d("ffn_down_dx", dh2b, wdn, NT, F32, tn=c_ff)
    tks = seq
    g_wdown = _mm2d("ffn_down_dw", m_ff, dh2b, TN, BF16, tm=c_ff)
    emit(wdown=g_wdown)
    da_ff, g_convw2, g_convb2 = _conv_gate_bwd(a_ff, w["conv_w"], cb3, dm)
    g_convw = jnp.swapaxes(g_convw2, 0, 1).reshape(ns, 3, c_ff)
    g_convb = jnp.swapaxes(g_convb2, 0, 1).reshape(ns, 1, c_ff)
    g_wup = _mm("ffn_up_dw", hn2, da_ff, grid=(ns, d // tnd, seq // tks), contract=TN,
                a_spec=pl.BlockSpec((tks, tnd), lambda s, j, k: (k, j)),
                b_spec=pl.BlockSpec((None, None, tks, c_ff), lambda s, j, k: (s % half, s // half, k, 0)),
                o_spec=pl.BlockSpec((None, tnd, c_ff), lambda s, j, k: (s, j, 0)),
                out_shape=(ns, d, c_ff), out_dtype=BF16)
    emit(wup=g_wup, conv_w=g_convw)
    dhn2 = _mm("ffn_up_dx", da_ff, w["wup"], grid=(seq // tmx, d // tnx, ns), contract=NT,
               a_spec=pl.BlockSpec((None, None, tmx, c_ff), lambda i, j, s: (s % half, s // half, i, 0)),
               b_spec=pl.BlockSpec((None, tnx, c_ff), lambda i, j, s: (s, j, 0)),
               o_spec=pl.BlockSpec((tmx, tnx), lambda i, j, s: (i, j)),
               out_shape=(seq, d), out_dtype=F32)

    def norm_bwd_fn(hb, dres, dn, wv):
        dx_, dw_ = _rms_bwd(hb, wv, dn)
        dtot = dres + dx_
        return dtot, dtot, dw_

    dh1, dh1b, g_ffn_norm = _blockwise(
        "norm2_bwd", norm_bwd_fn, [h1, dh2, dhn2, w["ffn_norm"]],
        [_row_spec(t_row, d)] * 3 + [_full_spec((1, d))],
        [((seq, d), F32), ((seq, d), BF16), ((1, d), F32)],
        [_row_spec(t_row, d), _row_spec(t_row, d), _full_spec((1, d))], g1, n_acc=1)

    dycat = _mm2d("out_proj_dx", dh1b, w["wout"], NT, F32)
    g_wout = _mm2d("out_proj_dw", ycat, dh1b, TN, BF16)

    def outnorm_bwd_fn(ys, ym, dyc, ws, wm):
        dys, dws = _rms_bwd(ys, ws, dyc[:, :ssm_w])
        dym, dwm = _rms_bwd(ym, wm, dyc[:, ssm_w:])
        return dys, dym, dws, dwm

    dy_ssm, dy_mla, g_son, g_mon = _blockwise(
        "out_norm_bwd", outnorm_bwd_fn, [y_ssm, y_mla, dycat, w["son"], w["mon"]],
        [_row_spec(t_row, ssm_w), _row_spec(t_row, mla_w), _row_spec(t_row, d), _full_spec((1, ssm_w)),
         _full_spec((1, mla_w))],
        [((seq, ssm_w), F32), ((seq, mla_w), F32), ((1, ssm_w), F32), ((1, mla_w), F32)],
        [_row_spec(t_row, ssm_w), _row_spec(t_row, mla_w), _full_spec((1, ssm_w)), _full_spec((1, mla_w))],
        g1, n_acc=2)

    def gate_bwd1_fn(dy, yp, zb):
        ygv = jax.nn.gelu(yp)
        sg = jax.nn.sigmoid(zb)
        dz = dy * ygv * sg * (1.0 - sg)
        return dz, jnp.sum(dz, axis=0, keepdims=True)

    dz, g_bglu = _blockwise("ssm_gate_bwd", gate_bwd1_fn, [dy_ssm, y_pre, z], [_row_spec(t_row, ssm_w)] * 3,
                            [((seq, ssm_w), BF16), ((1, ssm_w), F32)],
                            [_row_spec(t_row, ssm_w), _full_spec((1, ssm_w))], g1, n_acc=1)
    dyg2 = _mm2d("ssm_glu_dx", dz, w["wglu"], NT, F32)
    g_wglu = _mm2d("ssm_glu_dw", yg, dz, TN, BF16)

    def gelu_bwd_fn(dy, yp, zb, dg2, ub, dsk):
        dyg = dy * jax.nn.sigmoid(zb) + dg2
        _, vjp = jax.vjp(jax.nn.gelu, yp)
        dyp = vjp(dyg)[0]
        return dyp, dyp * dsk, jnp.sum(dyp * ub, axis=0, keepdims=True)

    dy_pre, du1, g_ssmd = _blockwise(
        "ssm_gelu_bwd", gelu_bwd_fn, [dy_ssm, y_pre, z, dyg2, proj, w["ssm_d"]],
        [_row_spec(t_row, ssm_w)] * 4 + [u_spec, _full_spec((1, ssm_w))],
        [((seq, ssm_w), BF16), ((seq, ssm_w), F32), ((1, ssm_w), F32)],
        [_row_spec(t_row, ssm_w), _row_spec(t_row, ssm_w), _full_spec((1, ssm_w))], g1, n_acc=1)
    dq_raw, dkv, dkp_h = _attn_bwd(q_raw, kv, kpe, cos_t, sa_t, sb_t, dy_mla)

    def head_mm_dx(name, dact, wh):
        kdim, ndim = wh.shape[1], wh.shape[2]
        return _mm(name, dact, wh, grid=(1, 1, nh), contract=NT,
                   a_spec=pl.BlockSpec((None, seq, ndim), lambda i, j, h: (h, i, 0)),
                   b_spec=pl.BlockSpec((None, kdim, ndim), lambda i, j, h: (h, 0, 0)),
                   o_spec=pl.BlockSpec((seq, kdim), lambda i, j, h: (i, 0)),
                   out_shape=(seq, kdim), out_dtype=F32)

    def head_mm_dw(name, act, dact):
        kdim, ndim = act.shape[1], dact.shape[2]
        return _mm(name, act, dact, grid=(nh, 1, seq // tks), contract=TN,
                   a_spec=pl.BlockSpec((tks, kdim), lambda h, j, k: (k, 0)),
                   b_spec=pl.BlockSpec((None, tks, ndim), lambda h, j, k: (h, k, 0)),
                   o_spec=pl.BlockSpec((None, kdim, ndim), lambda h, j, k: (h, 0, 0)),
                   out_shape=(nh, kdim, ndim), out_dtype=BF16)

    g_wuq = head_mm_dw("mla_q_dw", qn, dq_raw)
    g_wukv = head_mm_dw("mla_kv_dw", kvn, dkv)
    dqn = head_mm_dx("mla_q_dx", dq_raw, w["wuq"])
    dkvn = head_mm_dx("mla_kv_dx", dkv, w["wukv"])
    emit(not_before=(dqn, dkvn, dy_pre), wout=g_wout, wuq=g_wuq, wukv=g_wukv, wglu=g_wglu)

    du, dwb, dwc, da_lay = _ssm_bwd(dy_pre, s_all, proj, du1, wb, wc, a_lay)
    g_c_re = blockdiag_out_t(dwc[:, :STATE_BLOCK, :])
    g_c_im = -blockdiag_out_t(dwc[:, STATE_BLOCK:, :])
    dbbt_re = blockdiag_in_t(dwb[:, :, :STATE_BLOCK])
    dbbt_im = blockdiag_in_t(dwb[:, :, STATE_BLOCK:])
    da3 = da_lay.reshape(nj, 2, STATE_BLOCK)
    dabar_re = da3[:, 0, :].reshape(n_groups, 1, SSM_STATE)
    dabar_im = da3[:, 1, :].reshape(n_groups, 1, SSM_STATE)
    g_lr3, g_li3, g_ldt3, g_bt_re, g_bt_im = _s5_prep_bwd(lr3, li3, ldt3, bt_re, bt_im,
                                                           dabar_re, dabar_im, dbbt_re, dbbt_im)

    def mla_prep_bwd_fn(cq, ckv, dqn_b, dkvn_b, dkp_b, cos, sa, sb, wq, wkv):
        dcq, dwq = _rms_bwd(cq, wq, dqn_b)
        dckv, dwkv = _rms_bwd(ckv, wkv, dkvn_b)
        dkp_sum = dkp_b[0]
        for h in range(1, nh):
            dkp_sum = dkp_sum + dkp_b[h]
        return dcq, dckv, _rope128_t(dkp_sum, cos, sa, sb), dwq, dwkv

    dc_q, dc_kv, dkpe_raw, g_qnorm, g_kvnorm = _blockwise(
        "mla_prep_bwd", mla_prep_bwd_fn, [c_q, c_kv, dqn, dkvn, dkp_h, cos_t, sa_t, sb_t, w["q_norm"], w["kv_norm"]],
        [_row_spec(t_row, q_rank), _row_spec(t_row, kv_rank), _row_spec(t_row, q_rank), _row_spec(t_row, kv_rank),
         pl.BlockSpec((nh, t_row, LANES), lambda i: (0, i, 0))] + [_row_spec(t_row, LANES)] * 3
        + [_full_spec((1, q_rank)), _full_spec((1, kv_rank))],
        [((seq, q_rank), BF16), ((seq, kv_rank), BF16), ((seq, LANES), BF16), ((1, q_rank), F32), ((1, kv_rank), F32)],
        [_row_spec(t_row, q_rank), _row_spec(t_row, kv_rank), _row_spec(t_row, LANES), _full_spec((1, q_rank)),
         _full_spec((1, kv_rank))], g1, n_acc=2)

    dproj = jnp.concatenate([du, dc_q, dc_kv, dkpe_raw], axis=1)
    g_win = _mm2d("proj_dw", hn, dproj, TN, BF16, tn=640)
    emit(win=g_win)
    dhn = _mm2d("proj_dx", dproj, w["win"], NT, F32)

    def norm1_bwd_fn(xb, dres, dn, wv):
        dx_, dw_ = _rms_bwd(xb, wv, dn)
        return dres + dx_, dw_

    grad_x, g_attn_norm = _blockwise(
        "norm1_bwd", norm1_bwd_fn, [x, dh1, dhn, attn_w], [_row_spec(t_row, d)] * 3 + [_full_spec((1, d))],
        [((seq, d), F32), ((1, d), F32)], [_row_spec(t_row, d), _full_spec((1, d))], g1, n_acc=1)

    grads = dict(
        attn_norm=g_attn_norm, win=g_win, lam_re=g_lr3, lam_im=g_li3, log_dt=g_ldt3,
        b_re=jnp.swapaxes(g_bt_re, 1, 2), b_im=jnp.swapaxes(g_bt_im, 1, 2), c_re=g_c_re, c_im=g_c_im,
        ssm_d=g_ssmd, wglu=g_wglu, b_glu=g_bglu, q_norm=g_qnorm, wuq=g_wuq, kv_norm=g_kvnorm, wukv=g_wukv,
        son=g_son, mon=g_mon, wout=g_wout, ffn_norm=g_ffn_norm, wup=g_wup, conv_w=g_convw, conv_b=g_convb,
        wdown=g_wdown, final_norm=g_final)
    return loss, grad_x, grads


def _mesh_pos():
    return lax.axis_index("x"), lax.axis_index("y"), lax.axis_index("c")


def _handshake_all():
    x, y, c = _mesh_pos()
    barrier = pltpu.get_barrier_semaphore()
    for k in range(1, N_DEV):
        peer = (1 - x if k & 4 else x, 1 - y if k & 2 else y, 1 - c if k & 1 else c)
        pl.semaphore_signal(barrier, inc=1, device_id=peer, device_id_type=MESH)
    pl.semaphore_wait(barrier, N_DEV - 1)


def _comm_call(name, body, n, out_shape, ins, collective_id, after=None):
    sems = [pltpu.SemaphoreType.DMA((7 * n,)), pltpu.SemaphoreType.DMA((7 * n,)), pltpu.SemaphoreType.DMA((n,))]
    if collective_id is None:
        any_spec = pl.BlockSpec(memory_space=pl.ANY)
        return pl.pallas_call(body, name=name, out_shape=out_shape, in_specs=[any_spec] * n,
                              out_specs=[any_spec] * n, scratch_shapes=sems)(*ins)
    seq_body = body
    if after:
        n_after = len(after)
        ins = list(ins) + list(after)

        def seq_body(*refs):
            body(*refs[:n], *refs[n + n_after:])

    return pl.kernel(seq_body, name=name, out_type=out_shape,
                     mesh=plsc.ScalarSubcoreMesh(axis_name="seq", num_cores=1), scratch_types=sems,
                     compiler_params=pltpu.CompilerParams(collective_id=collective_id))(*ins)


def _all_gather(name, xs, collective_id=None, after=None):
    n = len(xs)

    def body(*refs):
        x_refs, o_refs = refs[:n], refs[n:2 * n]
        send_sems, recv_sems, local_sems = refs[2 * n:]
        if collective_id is not None:
            _handshake_all()
        x, y, c = _mesh_pos()
        me, sibling = (x, y, c), (x, y, 1 - c)
        chips = [(1 - x, y), (x, 1 - y), (1 - x, 1 - y)]

        def slot(o_ref, px, py, pc):
            return o_ref.at[4 * px + 2 * py + pc]

        def copy(t, k, block, to, src=None):
            dst = slot(o_refs[t], *block)
            return pltpu.make_async_remote_copy(
                src_ref=dst if src is None else src, dst_ref=dst,
                send_sem=send_sems.at[7 * t + k], recv_sem=recv_sems.at[7 * t + k],
                device_id=to, device_id_type=MESH)

        started = []
        for t in range(n):
            mine = pltpu.make_async_copy(x_refs[t], slot(o_refs[t], *me), local_sems.at[t])
            mine.start()
            started.append(mine)
        first = []
        for t in range(n):
            first.append(copy(t, 0, me, sibling, src=x_refs[t]))
            first += [copy(t, 1 + j, me, (*chip, c), src=x_refs[t]) for j, chip in enumerate(chips)]
        for cp in first:
            cp.start()
        passed = []
        for j, chip in enumerate(chips):
            for t in range(n):
                copy(t, 1 + j, (*chip, c), me).wait_recv()
                fwd = copy(t, 4 + j, (*chip, c), sibling)
                fwd.start()
                passed.append(fwd)
        for t in range(n):
            copy(t, 0, sibling, me).wait_recv()
            for j, chip in enumerate(chips):
                copy(t, 4 + j, (*chip, 1 - c), me).wait_recv()
        for cp in first + passed:
            cp.wait_send()
        for mine in started:
            mine.wait()

    out_shape = [jax.ShapeDtypeStruct((N_DEV,) + v.shape, v.dtype) for v in xs]
    return _comm_call(name, body, n, out_shape, xs, collective_id, after)


def _exchange_partials(name, gs, collective_id=None, after=None):
    n = len(gs)

    def body(*refs):
        g_refs, o_refs = refs[:n], refs[n:2 * n]
        send_sems, recv_sems, local_sems = refs[2 * n:]
        if collective_id is not None:
            _handshake_all()
        x, y, c = _mesh_pos()
        me_idx = 4 * x + 2 * y + c
        copies = []
        for t in range(n):
            mine = pltpu.make_async_copy(g_refs[t].at[me_idx], o_refs[t].at[me_idx], local_sems.at[t])
            mine.start()
            copies.append(mine)
        remote = []
        for k in range(1, N_DEV):
            px = 1 - x if k & 4 else x
            py = 1 - y if k & 2 else y
            pc = 1 - c if k & 1 else c
            p_idx = 4 * px + 2 * py + pc
            for t in range(n):
                cp = pltpu.make_async_remote_copy(
                    src_ref=g_refs[t].at[p_idx], dst_ref=o_refs[t].at[me_idx],
                    send_sem=send_sems.at[7 * t + k - 1], recv_sem=recv_sems.at[7 * t + k - 1],
                    device_id=(px, py, pc), device_id_type=MESH)
                cp.start()
                landing = pltpu.make_async_remote_copy(
                    src_ref=g_refs[t].at[p_idx], dst_ref=o_refs[t].at[p_idx],
                    send_sem=send_sems.at[7 * t + k - 1], recv_sem=recv_sems.at[7 * t + k - 1],
                    device_id=(px, py, pc), device_id_type=MESH)
                remote.append((cp, landing))
        for cp, landing in remote:
            landing.wait_recv()
        for cp, landing in remote:
            cp.wait_send()
        for mine in copies:
            mine.wait()

    out_shape = [jax.ShapeDtypeStruct(v.shape, v.dtype) for v in gs]
    return _comm_call(name, body, n, out_shape, gs, collective_id, after)


ADAM_BLOCK_ELEMS = 128 * 1024


def _adamw_sum(name, parts, wv, mv, vv):
    npart, r, c = parts.shape
    tr = r
    if r * c > ADAM_BLOCK_ELEMS and r % SUBLANES == 0:
        tr = SUBLANES
        while r % (tr * 2) == 0 and tr * 2 * c <= ADAM_BLOCK_ELEMS:
            tr *= 2
    bc1 = 1.0 - ADAM_B1 ** ADAM_STEP
    bc2 = 1.0 - ADAM_B2 ** ADAM_STEP

    def fn(pb, wb_, mb, vb):
        g = pb[0].astype(F32)
        for j in range(1, npart):
            g = g + pb[j].astype(F32)
        m_new = ADAM_B1 * mb + (1.0 - ADAM_B1) * g
        v_new = ADAM_B2 * vb + (1.0 - ADAM_B2) * (g * g)
        m_hat = m_new / bc1
        v_hat = v_new / bc2
        delta = -ADAM_LR * (m_hat / (jnp.sqrt(v_hat) + ADAM_EPS) + ADAM_WD * wb_)
        return g, delta, m_new, v_new

    row = pl.BlockSpec((tr, c), lambda i: (i, 0))
    return _blockwise(name, fn, [parts, wv, mv, vv],
                      [pl.BlockSpec((npart, tr, c), lambda i: (0, i, 0)), row, row, row],
                      [((r, c), F32)] * 4, [row] * 4, (r // tr,))


_SMALL = ["attn_norm", "lam_re", "lam_im", "log_dt", "b_re", "b_im", "c_re", "c_im", "ssm_d", "b_glu",
          "q_norm", "kv_norm", "son", "mon", "ffn_norm", "conv_b", "final_norm"]
_BIG = ["win", "wglu", "wuq", "wukv", "wout", "wup", "wdown", "conv_w"]
_ORDER = ["attn_norm", "win", "lam_re", "lam_im", "log_dt", "b_re", "b_im", "c_re", "c_im", "ssm_d", "wglu",
          "b_glu", "q_norm", "wuq", "kv_norm", "wukv", "son", "mon", "wout", "ffn_norm", "wup", "conv_w",
          "conv_b", "wdown", "final_norm"]


def _pack(arrs):
    flat = jnp.concatenate([a.reshape(-1).astype(F32) for a in arrs])
    pad = (-flat.shape[0]) % (LANES * LANES)
    return jnp.pad(flat, (0, pad)).reshape(-1, LANES)


def _unpack(packed, shapes):
    flat = packed.reshape(-1)
    out, off = [], 0
    for s in shapes:
        n = math.prod(s)
        out.append(flat[off:off + n].reshape(s))
        off += n
    return out


def kernel(x, positions, attn_norm_w, w_in, ssm_lambda_re, ssm_lambda_im, ssm_log_dt, ssm_b_re, ssm_b_im, ssm_c_re, ssm_c_im, ssm_d, ssm_w_glu, ssm_b_glu, mla_q_norm_w, mla_w_uq, mla_kv_norm_w, mla_w_ukv, ssm_out_norm_w, mla_out_norm_w, w_out, ffn_norm_w, ffn_w_up, ffn_conv_w, ffn_conv_b, ffn_w_down, final_norm_w, loss_target, m_attn_norm_w, m_w_in, m_ssm_lambda_re, m_ssm_lambda_im, m_ssm_log_dt, m_ssm_b_re, m_ssm_b_im, m_ssm_c_re, m_ssm_c_im, m_ssm_d, m_ssm_w_glu, m_ssm_b_glu, m_mla_q_norm_w, m_mla_w_uq, m_mla_kv_norm_w, m_mla_w_ukv, m_ssm_out_norm_w, m_mla_out_norm_w, m_w_out, m_ffn_norm_w, m_ffn_w_up, m_ffn_conv_w, m_ffn_conv_b, m_ffn_w_down, m_final_norm_w, v_attn_norm_w, v_w_in, v_ssm_lambda_re, v_ssm_lambda_im, v_ssm_log_dt, v_ssm_b_re, v_ssm_b_im, v_ssm_c_re, v_ssm_c_im, v_ssm_d, v_ssm_w_glu, v_ssm_b_glu, v_mla_q_norm_w, v_mla_w_uq, v_mla_kv_norm_w, v_mla_w_ukv, v_ssm_out_norm_w, v_mla_out_norm_w, v_w_out, v_ffn_norm_w, v_ffn_w_up, v_ffn_conv_w, v_ffn_conv_b, v_ffn_w_down, v_final_norm_w):
    wts = dict(attn_norm=attn_norm_w, win=w_in, lam_re=ssm_lambda_re, lam_im=ssm_lambda_im, log_dt=ssm_log_dt,
               b_re=ssm_b_re, b_im=ssm_b_im, c_re=ssm_c_re, c_im=ssm_c_im, ssm_d=ssm_d, wglu=ssm_w_glu,
               b_glu=ssm_b_glu, q_norm=mla_q_norm_w, wuq=mla_w_uq, kv_norm=mla_kv_norm_w, wukv=mla_w_ukv,
               son=ssm_out_norm_w, mon=mla_out_norm_w, wout=w_out, ffn_norm=ffn_norm_w, wup=ffn_w_up,
               conv_w=ffn_conv_w, conv_b=ffn_conv_b, wdown=ffn_w_down, final_norm=final_norm_w)
    moms = dict(zip(_ORDER, [m_attn_norm_w, m_w_in, m_ssm_lambda_re, m_ssm_lambda_im, m_ssm_log_dt, m_ssm_b_re,
                             m_ssm_b_im, m_ssm_c_re, m_ssm_c_im, m_ssm_d, m_ssm_w_glu, m_ssm_b_glu, m_mla_q_norm_w,
                             m_mla_w_uq, m_mla_kv_norm_w, m_mla_w_ukv, m_ssm_out_norm_w, m_mla_out_norm_w, m_w_out,
                             m_ffn_norm_w, m_ffn_w_up, m_ffn_conv_w, m_ffn_conv_b, m_ffn_w_down, m_final_norm_w]))
    vels = dict(zip(_ORDER, [v_attn_norm_w, v_w_in, v_ssm_lambda_re, v_ssm_lambda_im, v_ssm_log_dt, v_ssm_b_re,
                             v_ssm_b_im, v_ssm_c_re, v_ssm_c_im, v_ssm_d, v_ssm_w_glu, v_ssm_b_glu, v_mla_q_norm_w,
                             v_mla_w_uq, v_mla_kv_norm_w, v_mla_w_ukv, v_ssm_out_norm_w, v_mla_out_norm_w, v_w_out,
                             v_ffn_norm_w, v_ffn_w_up, v_ffn_conv_w, v_ffn_conv_b, v_ffn_w_down, v_final_norm_w]))
    seq, d = x.shape[1], x.shape[2]
    in_width = w_in.shape[2]
    in_pad = -(-in_width // LANES) * LANES
    q_cols = mla_w_uq.shape[2]
    q_pad = 2 * LANES

    (win_g,) = _all_gather("gather_w_in", [jnp.pad(w_in[0], ((0, 0), (0, in_pad - in_width))).astype(BF16)])
    wglu_g, wuq_g, wukv_g, wout_g, convw_g = _all_gather(
        "gather_mix", [ssm_w_glu[0].astype(BF16), jnp.pad(mla_w_uq[0], ((0, 0), (0, q_pad - q_cols))).astype(BF16),
                       mla_w_ukv[0].astype(BF16), w_out[0].astype(BF16), ffn_conv_w[0]], collective_id=0)
    (wup_g,) = _all_gather("gather_ffn_up", [ffn_w_up[0].astype(BF16)], collective_id=1)
    (wdown_g,) = _all_gather("gather_ffn_down", [ffn_w_down[0].astype(BF16)], collective_id=2)
    ns = N_DEV
    c_ff = wup_g.shape[2]
    w = dict(
        attn_norm=attn_norm_w, win=win_g.reshape(d, in_pad), lam_re=ssm_lambda_re, lam_im=ssm_lambda_im,
        log_dt=ssm_log_dt, b_re=ssm_b_re, b_im=ssm_b_im, c_re=ssm_c_re, c_im=ssm_c_im, ssm_d=ssm_d,
        wglu=wglu_g.reshape(d // 2, d // 2), b_glu=ssm_b_glu, q_norm=mla_q_norm_w, wuq=wuq_g,
        kv_norm=mla_kv_norm_w, wukv=wukv_g, son=ssm_out_norm_w, mon=mla_out_norm_w, wout=wout_g.reshape(d, d),
        ffn_norm=ffn_norm_w, wup=wup_g, conv_w=convw_g, conv_b=ffn_conv_b,
        wdown=wdown_g.reshape(ns // 2 * c_ff, d), final_norm=final_norm_w)

    shard_layout = dict(
        win=lambda a: a[:, :in_width].reshape(N_DEV, d // N_DEV, in_width),
        wglu=lambda a: a.reshape(N_DEV, d // 2 // N_DEV, d // 2),
        wuq=lambda a: a[:, :, :q_cols], wukv=lambda a: a, wout=lambda a: a.reshape(N_DEV, d // N_DEV, d),
        wup=lambda a: a, wdown=lambda a: a.reshape(N_DEV, c_ff // 2, d), conv_w=lambda a: a)
    recv = {}
    next_id = [3]

    last = [None]

    def exchange(not_before=(), **grads):
        names = list(grads)
        got = _exchange_partials("exchange_" + "_".join(names), [shard_layout[k](grads[k]) for k in names],
                                 collective_id=next_id[0], after=[a for a in (last[0], *not_before) if a is not None])
        next_id[0] += 1
        last[0] = got[-1]
        recv.update(zip(names, got))

    loss_part, grad_x, g = _local_step(x[0], positions[0], loss_target[0], w, emit=exchange)
    loss = lax.psum(loss_part, ("x", "y", "c"))
    small_shapes = [wts[k].shape for k in _SMALL]
    small_part = _pack([g[k] for k in _SMALL])
    small_all = _all_gather("gather_small_grads", [small_part], collective_id=next_id[0], after=[last[0]])[0]

    out = {}
    for k in _BIG:
        shp = wts[k].shape
        r, c = shp[-2], shp[-1]
        res = _adamw_sum("adamw_" + k, recv[k].reshape(N_DEV, r, c), wts[k].reshape(r, c),
                         moms[k].reshape(r, c), vels[k].reshape(r, c))
        out[k] = [a.reshape(shp) for a in res]
    sw_ = _pack([wts[k] for k in _SMALL])
    sm_ = _pack([moms[k] for k in _SMALL])
    sv_ = _pack([vels[k] for k in _SMALL])
    res = _adamw_sum("adamw_small", small_all, sw_, sm_, sv_)
    unpacked = [_unpack(a, small_shapes) for a in res]
    for i, k in enumerate(_SMALL):
        out[k] = [u[i] for u in unpacked]

    grad_x = grad_x.reshape(x.shape)
    return (loss, grad_x, *[out[k][0] for k in _ORDER], *[out[k][1] for k in _ORDER],
            *[out[k][2] for k in _ORDER], *[out[k][3] for k in _ORDER])
```

```python
import functools
import math

import jax
import jax.numpy as jnp
from jax import lax
from jax.experimental import pallas as pl
from jax.experimental.pallas import tpu as pltpu
from jax.experimental.pallas import tpu_sc as plsc

F32 = jnp.float32
BF16 = jnp.bfloat16
MESH = pl.DeviceIdType.MESH

N_DEV = 8
LANES = 128
SUBLANES = 8
VMEM_LIMIT = 48 * 1024 * 1024

SSM_GROUP = 16
SSM_STATE = 64
GROUPS_PER_BLOCK = LANES // SSM_GROUP
STATE_BLOCK = GROUPS_PER_BLOCK * SSM_STATE
QK_NOPE = 128
QK_ROPE = 64
V_DIM = 128
ROPE_THETA = 10000.0
RMS_EPS = 1e-6

ADAM_LR = 0.001
ADAM_B1 = 0.9
ADAM_B2 = 0.999
ADAM_EPS = 1e-08
ADAM_WD = 0.01
ADAM_STEP = 10

NN = ((1,), (0,))
NT = ((1,), (1,))
TN = ((0,), (0,))


def _cparams():
    return pltpu.CompilerParams(vmem_limit_bytes=VMEM_LIMIT)


def _tile(n, want):
    if n <= want:
        return n
    t = (want // LANES) * LANES
    while t >= LANES:
        if n % t == 0:
            return t
        t -= LANES
    return n


def _mm(name, a, b, *, grid, a_spec, b_spec, o_spec, out_shape, out_dtype, contract=NN,
        res=None, res_spec=None):
    nk = grid[-1]
    kaxis = len(grid) - 1
    acc_shape = tuple(d for d in o_spec.block_shape if d is not None)

    def body(*refs):
        a_ref, b_ref = refs[:2]
        r_ref = None if res is None else refs[2]
        o_ref = refs[2 if res is None else 3]
        part = lax.dot_general(a_ref[...].astype(BF16), b_ref[...].astype(BF16),
                               (contract, ((), ())), preferred_element_type=F32)
        if nk == 1:
            if r_ref is not None:
                part = part + r_ref[...].astype(F32)
            o_ref[...] = part.astype(o_ref.dtype)
            return
        acc = refs[-1]
        k = pl.program_id(kaxis)

        @pl.when(k == 0)
        def _():
            acc[...] = part

        @pl.when(k != 0)
        def _():
            acc[...] += part

        @pl.when(k == nk - 1)
        def _():
            r = acc[...]
            if r_ref is not None:
                r = r + r_ref[...].astype(F32)
            o_ref[...] = r.astype(o_ref.dtype)

    ins = [a, b] + ([] if res is None else [res])
    in_specs = [a_spec, b_spec] + ([] if res is None else [res_spec])
    return pl.pallas_call(
        body, name=name, grid=grid, in_specs=in_specs, out_specs=o_spec,
        out_shape=jax.ShapeDtypeStruct(out_shape, out_dtype),
        scratch_shapes=[pltpu.VMEM(acc_shape, F32)] if nk > 1 else [], compiler_params=_cparams(),
    )(*ins)


def _mm2d(name, a, b, contract, out_dtype, tm=1024, tn=1024, tk=2048, res=None):
    if contract == NN:
        (m, kk), n = a.shape, b.shape[1]
    elif contract == NT:
        (m, kk), n = a.shape, b.shape[0]
    else:
        (kk, m), n = a.shape, b.shape[1]
    tm, tn, tk = _tile(m, tm), _tile(n, tn), _tile(kk, tk)
    grid = (m // tm, n // tn, kk // tk)
    if contract == TN:
        a_spec = pl.BlockSpec((tk, tm), lambda i, j, k: (k, i))
    else:
        a_spec = pl.BlockSpec((tm, tk), lambda i, j, k: (i, k))
    if contract == NT:
        b_spec = pl.BlockSpec((tn, tk), lambda i, j, k: (j, k))
    else:
        b_spec = pl.BlockSpec((tk, tn), lambda i, j, k: (k, j))
    o_spec = pl.BlockSpec((tm, tn), lambda i, j, k: (i, j))
    res_spec = None
    if res is not None:
        if res.shape[0] == 1:
            res_spec = pl.BlockSpec((1, tn), lambda i, j, k: (0, j))
        else:
            res_spec = pl.BlockSpec((tm, tn), lambda i, j, k: (i, j))
    return _mm(name, a, b, grid=grid, a_spec=a_spec, b_spec=b_spec, o_spec=o_spec,
               out_shape=(m, n), out_dtype=out_dtype, contract=contract, res=res, res_spec=res_spec)


def _blockwise(name, fn, ins, in_specs, outs, out_specs, grid, n_acc=0, acc_all=True):
    n_in, n_out = len(ins), len(outs)
    n_plain = n_out - n_acc

    def body(*refs):
        vals = fn(*[r[...] for r in refs[:n_in]])
        if not isinstance(vals, (tuple, list)):
            vals = (vals,)
        o_refs = refs[n_in:n_in + n_out]
        for r, v in zip(o_refs[:n_plain], vals[:n_plain]):
            r[...] = v.astype(r.dtype)
        if n_acc:
            if acc_all:
                first = functools.reduce(jnp.logical_and, [pl.program_id(d) == 0 for d in range(len(grid))])
            else:
                first = pl.program_id(len(grid) - 1) == 0

            @pl.when(first)
            def _():
                for r, v in zip(o_refs[n_plain:], vals[n_plain:]):
                    r[...] = v.astype(r.dtype)

            @pl.when(jnp.logical_not(first))
            def _():
                for r, v in zip(o_refs[n_plain:], vals[n_plain:]):
                    r[...] += v.astype(r.dtype)

    return pl.pallas_call(
        body, name=name, grid=grid, in_specs=in_specs, out_specs=out_specs,
        out_shape=[jax.ShapeDtypeStruct(s, d) for s, d in outs], compiler_params=_cparams(),
    )(*ins)


def _row_spec(t, c):
    return pl.BlockSpec((t, c), lambda i: (i, 0))


def _full_spec(shape):
    nd = len(shape)
    return pl.BlockSpec(tuple(shape), lambda *g: (0,) * nd)


def _rms(xf, w):
    return xf * lax.rsqrt(jnp.mean(xf * xf, axis=-1, keepdims=True) + RMS_EPS) * w


def _rms_bwd(xf, w, dy):
    _, vjp = jax.vjp(_rms, xf, w)
    return vjp(dy)


def _s5_disc(lr, li, ldt, bre, bim):
    dt = jnp.exp(ldt)
    mag = jnp.exp(lr * dt)
    ar = mag * jnp.cos(li * dt)
    ai = mag * jnp.sin(li * dt)
    nr, ni = ar - 1.0, ai
    den = lr * lr + li * li
    zr = (nr * lr + ni * li) / den
    zi = (ni * lr - nr * li) / den
    return ar, ai, zr * bre - zi * bim, zr * bim + zi * bre


def _s5_prep(lr, li, ldt, bre, bim):
    def body(lr_r, li_r, ldt_r, bre_r, bim_r, ar_r, ai_r, br_r, bi_r):
        ar, ai, br, bi = _s5_disc(lr_r[...], li_r[...], ldt_r[...], bre_r[...], bim_r[...])
        ar_r[...] = ar
        ai_r[...] = ai
        br_r[...] = br
        bi_r[...] = bi

    sd = jax.ShapeDtypeStruct
    return pl.pallas_call(
        body, name="s5_prep",
        out_shape=[sd(lr.shape, F32), sd(lr.shape, F32), sd(bre.shape, F32), sd(bre.shape, F32)],
        compiler_params=_cparams(),
    )(lr, li, ldt, bre, bim)


def _s5_prep_bwd(lr, li, ldt, bre, bim, dar, dai, dbr, dbi):
    def body(lr_r, li_r, ldt_r, bre_r, bim_r, dar_r, dai_r, dbr_r, dbi_r, o0, o1, o2, o3, o4):
        _, vjp = jax.vjp(_s5_disc, lr_r[...], li_r[...], ldt_r[...], bre_r[...], bim_r[...])
        g = vjp((dar_r[...], dai_r[...], dbr_r[...], dbi_r[...]))
        for o, v in zip((o0, o1, o2, o3, o4), g):
            o[...] = v

    sd = jax.ShapeDtypeStruct
    return pl.pallas_call(
        body, name="s5_prep_bwd",
        out_shape=[sd(lr.shape, F32), sd(li.shape, F32), sd(ldt.shape, F32), sd(bre.shape, F32), sd(bim.shape, F32)],
        compiler_params=_cparams(),
    )(lr, li, ldt, bre, bim, dar, dai, dbr, dbi)


SCAN_T = 256


def _scan_tables(ar, ai, tab_r, tab_i, sub, reverse):
    pr, pi = ar, ai
    for k in range(sub):
        row = sub - 1 - k if reverse else k
        tab_r[row:row + 1, :] = pr
        tab_i[row:row + 1, :] = pi
        pr, pi = ar * pr - ai * pi, ar * pi + ai * pr


def _pack_matrix(t_blk, dtype):
    sub = t_blk // SUBLANES
    dst = jnp.arange(t_blk)
    src = (dst % SUBLANES) * sub + dst // SUBLANES
    return (src[:, None] == jnp.arange(t_blk)[None, :]).astype(dtype)


def _permute_rows_f32(pm, x):
    hi = x.astype(BF16)
    r1 = x - hi.astype(F32)
    mid = r1.astype(BF16)
    lo = (r1 - mid.astype(F32)).astype(BF16)
    dot = lambda v: jnp.dot(pm, v, preferred_element_type=F32)
    return dot(hi) + dot(mid) + dot(lo)


def _scan_block(x, loc, ar, ai, st, tab_r, tab_i, sub, reverse):
    hb = STATE_BLOCK
    a8r = jnp.broadcast_to(ar, (SUBLANES, hb))
    a8i = jnp.broadcast_to(ai, (SUBLANES, hb))
    sr = jnp.zeros((SUBLANES, hb), F32)
    si = jnp.zeros((SUBLANES, hb), F32)
    steps = range(sub - 1, -1, -1) if reverse else range(sub)
    for t in steps:
        rows = slice(t * SUBLANES, (t + 1) * SUBLANES)
        sr, si = a8r * sr - a8i * si + x[rows, :hb], a8r * si + a8i * sr + x[rows, hb:]
        loc[rows, :hb] = sr
        loc[rows, hb:] = si
    cr, ci = st[0:1, :], st[1:2, :]
    far = 0 if reverse else sub - 1
    fr, fi = tab_r[far:far + 1, :], tab_i[far:far + 1, :]
    ent_r, ent_i = [None] * SUBLANES, [None] * SUBLANES
    for c in (range(SUBLANES - 1, -1, -1) if reverse else range(SUBLANES)):
        ent_r[c], ent_i[c] = cr, ci
        cr, ci = sr[c:c + 1, :] + (fr * cr - fi * ci), si[c:c + 1, :] + (fr * ci + fi * cr)
    st[0:1, :] = cr
    st[1:2, :] = ci
    c8r = jnp.concatenate(ent_r, axis=0)
    c8i = jnp.concatenate(ent_i, axis=0)
    out = []
    for t in range(sub):
        rows = slice(t * SUBLANES, (t + 1) * SUBLANES)
        tr, ti = tab_r[t:t + 1, :], tab_i[t:t + 1, :]
        out.append(jnp.concatenate([loc[rows, :hb] + (tr * c8r - ti * c8i), loc[rows, hb:] + (tr * c8i + ti * c8r)],
                                   axis=1))
    return jnp.concatenate(out, axis=0)


SSM_BLOCKS_PER_STEP = 2


def _scan_scratch(nblk, t_blk, sub, hb):
    return [pltpu.VMEM((nblk, SUBLANES, hb), F32), pltpu.VMEM((nblk, sub, hb), F32), pltpu.VMEM((nblk, sub, hb), F32),
            pltpu.VMEM((nblk, t_blk, 2 * hb), F32)]


def _ssm_fwd(proj, wb, wc, a):
    seq = proj.shape[0]
    nj = wb.shape[0]
    w2 = 2 * STATE_BLOCK
    hb = STATE_BLOCK
    t_blk = min(SCAN_T, seq)
    sub = t_blk // SUBLANES
    pm = _pack_matrix(t_blk, BF16)

    npair = SSM_BLOCKS_PER_STEP

    def body(u_ref, wb_ref, wc_ref, a_ref, pm_ref, pmt_ref, s_ref, y_ref, st, tab_r, tab_i, loc):
        coef = [(a_ref[:, b * w2:b * w2 + hb], a_ref[:, b * w2 + hb:(b + 1) * w2]) for b in range(npair)]

        @pl.when(pl.program_id(1) == 0)
        def _():
            for b, (ar, ai) in enumerate(coef):
                st[b] = jnp.zeros((SUBLANES, hb), F32)
                _scan_tables(ar, ai, tab_r.at[b], tab_i.at[b], sub, False)

        for b, (ar, ai) in enumerate(coef):
            ub = u_ref[:, b * LANES:(b + 1) * LANES].astype(BF16)
            up = jnp.dot(pm_ref[...], ub, preferred_element_type=F32).astype(BF16)
            bu = jnp.dot(up, wb_ref[b], preferred_element_type=F32)
            s = _scan_block(bu, loc.at[b], ar, ai, st.at[b], tab_r.at[b], tab_i.at[b], sub, False)
            s_ref[:, b * w2:(b + 1) * w2] = s
            yp = jnp.dot(s.astype(BF16), wc_ref[b], preferred_element_type=F32)
            y_ref[:, b * LANES:(b + 1) * LANES] = _permute_rows_f32(pmt_ref[...], yp)

    sd = jax.ShapeDtypeStruct
    return pl.pallas_call(
        body, name="ssm_fwd", grid=(nj // npair, seq // t_blk),
        in_specs=[pl.BlockSpec((t_blk, npair * LANES), lambda j, i: (i, j)),
                  pl.BlockSpec((npair, LANES, w2), lambda j, i: (j, 0, 0)),
                  pl.BlockSpec((npair, w2, LANES), lambda j, i: (j, 0, 0)),
                  pl.BlockSpec((1, npair * w2), lambda j, i: (0, j)),
                  _full_spec((t_blk, t_blk)), _full_spec((t_blk, t_blk))],
        out_specs=[pl.BlockSpec((t_blk, npair * w2), lambda j, i: (i, j)),
                   pl.BlockSpec((t_blk, npair * LANES), lambda j, i: (i, j))],
        out_shape=[sd((seq, nj * w2), F32), sd((seq, nj * LANES), F32)],
        scratch_shapes=_scan_scratch(npair, t_blk, sub, hb), compiler_params=_cparams(),
    )(proj, wb, wc, a, pm, pm.T)


def _ssm_bwd(dy, s, proj, du1, wb, wc, a):
    seq = dy.shape[0]
    nj = wb.shape[0]
    w2 = 2 * STATE_BLOCK
    hb = STATE_BLOCK
    t_blk = min(SCAN_T, seq)
    sub = t_blk // SUBLANES
    nb = seq // t_blk
    pm = _pack_matrix(t_blk, BF16)

    npair = SSM_BLOCKS_PER_STEP

    def body(dy_ref, s_ref, sprev_ref, u_ref, du1_ref, wb_ref, wc_ref, a_ref, pm_ref, pmt_ref,
             du_ref, dwb_ref, dwc_ref, da_ref, st, tab_r, tab_i, loc):
        ib = pl.program_id(1)
        pmv = pm_ref[...]
        coef = [(a_ref[:, b * w2:b * w2 + hb], -a_ref[:, b * w2 + hb:(b + 1) * w2]) for b in range(npair)]

        @pl.when(ib == 0)
        def _():
            for b, (ar, ai) in enumerate(coef):
                st[b] = jnp.zeros((SUBLANES, hb), F32)
                _scan_tables(ar, ai, tab_r.at[b], tab_i.at[b], sub, True)

        sums = []
        for b, (ar, ai) in enumerate(coef):
            cols, wide = slice(b * LANES, (b + 1) * LANES), slice(b * w2, (b + 1) * w2)
            dyp = jnp.dot(pmv, dy_ref[:, cols], preferred_element_type=F32).astype(BF16)
            up = jnp.dot(pmv, u_ref[:, cols].astype(BF16), preferred_element_type=F32).astype(BF16)
            ds = lax.dot_general(dyp, wc_ref[b], (NT, ((), ())), preferred_element_type=F32)
            lam = _scan_block(ds, loc.at[b], ar, ai, st.at[b], tab_r.at[b], tab_i.at[b], sub, True)
            lamb = lam.astype(BF16)
            du = lax.dot_general(lamb, wb_ref[b], (NT, ((), ())), preferred_element_type=F32)
            du_ref[:, cols] = (_permute_rows_f32(pmt_ref[...], du) + du1_ref[:, cols]).astype(du_ref.dtype)
            sv = s_ref[:, wide]
            dwb = lax.dot_general(up, lamb, (TN, ((), ())), preferred_element_type=F32)
            dwc = lax.dot_general(sv.astype(BF16), dyp, (TN, ((), ())), preferred_element_type=F32)

            prev_last = sprev_ref[SUBLANES - 1:SUBLANES, wide]
            prev_last = jnp.where(ib == nb - 1, jnp.zeros_like(prev_last), prev_last)
            tail = sv[t_blk - SUBLANES:, :]
            sl = lax.broadcasted_iota(jnp.int32, tail.shape, 0)
            head = jnp.where(sl >= 1, pltpu.roll(tail, 1, 0), prev_last)
            s_sh = jnp.concatenate([head, sv[:t_blk - SUBLANES, :]], axis=0)
            lam_r, lam_i = lam[:, :hb], lam[:, hb:]
            sr_, si_ = s_sh[:, :hb], s_sh[:, hb:]
            dar = jnp.sum(lam_r * sr_ + lam_i * si_, axis=0, keepdims=True)
            dai = jnp.sum(lam_i * sr_ - lam_r * si_, axis=0, keepdims=True)
            sums.append((wide, jnp.concatenate([dar, dai], axis=1), dwb, dwc))

        @pl.when(ib == 0)
        def _():
            for b, (wide, contrib, dwb, dwc) in enumerate(sums):
                da_ref[:, wide] = contrib
                dwb_ref[b] = dwb
                dwc_ref[b] = dwc

        @pl.when(ib != 0)
        def _():
            for b, (wide, contrib, dwb, dwc) in enumerate(sums):
                da_ref[:, wide] += contrib
                dwb_ref[b] += dwb
                dwc_ref[b] += dwc

    blk = lambda j, i: (nb - 1 - i, j)
    prev_blk = lambda j, i: (jnp.maximum((nb - 1 - i) * sub - 1, 0), j)
    sd = jax.ShapeDtypeStruct
    return pl.pallas_call(
        body, name="ssm_bwd", grid=(nj // npair, nb),
        in_specs=[pl.BlockSpec((t_blk, npair * LANES), blk), pl.BlockSpec((t_blk, npair * w2), blk),
                  pl.BlockSpec((SUBLANES, npair * w2), prev_blk), pl.BlockSpec((t_blk, npair * LANES), blk),
                  pl.BlockSpec((t_blk, npair * LANES), blk),
                  pl.BlockSpec((npair, LANES, w2), lambda j, i: (j, 0, 0)),
                  pl.BlockSpec((npair, w2, LANES), lambda j, i: (j, 0, 0)),
                  pl.BlockSpec((1, npair * w2), lambda j, i: (0, j)),
                  _full_spec((t_blk, t_blk)), _full_spec((t_blk, t_blk))],
        out_specs=[pl.BlockSpec((t_blk, npair * LANES), blk),
                   pl.BlockSpec((npair, LANES, w2), lambda j, i: (j, 0, 0)),
                   pl.BlockSpec((npair, w2, LANES), lambda j, i: (j, 0, 0)),
                   pl.BlockSpec((1, npair * w2), lambda j, i: (0, j))],
        out_shape=[sd((seq, nj * LANES), BF16), sd((nj, LANES, w2), F32), sd((nj, w2, LANES), F32),
                   sd((1, nj * w2), F32)],
        scratch_shapes=_scan_scratch(npair, t_blk, sub, hb), compiler_params=_cparams(),
    )(dy, s, s, proj, du1, wb, wc, a, pm, pm.T)


def _rope128(x, cos, sa, sb):
    return x * cos + pltpu.roll(x, 96, 1) * sa + pltpu.roll(x, 32, 1) * sb


def _rope128_t(dy, cos, sa, sb):
    return dy * cos + pltpu.roll(dy * sa, 32, 1) + pltpu.roll(dy * sb, 96, 1)


ATT_BQ = 256


def _probs(qn, qp, kn, kp, r0, scale):
    s = lax.dot_general(qn, kn, (NT, ((), ())), preferred_element_type=F32)
    s = s + lax.dot_general(qp, kp, (NT, ((), ())), preferred_element_type=F32)
    s = s * scale
    diag = s[:, r0:]
    row = lax.broadcasted_iota(jnp.int32, diag.shape, 0)
    col = lax.broadcasted_iota(jnp.int32, diag.shape, 1)
    diag = jnp.where(col <= row, diag, jnp.finfo(F32).min)
    s = diag if r0 == 0 else jnp.concatenate([s[:, :r0], diag], axis=1)
    m = jnp.max(s, axis=-1, keepdims=True)
    e = jnp.exp(s - m)
    return e / jnp.sum(e, axis=-1, keepdims=True)


def _attn_specs(seq):
    tab = pl.BlockSpec((seq, LANES), lambda h: (0, 0))
    return [pl.BlockSpec((None, seq, 256), lambda h: (h, 0, 0)), pl.BlockSpec((None, seq, 128), lambda h: (h, 0, 0)),
            pl.BlockSpec((None, seq, 128), lambda h: (h, 0, 1)), tab, tab, tab, tab]


def _attn_fwd(q_raw, kv, kpe, cos, sa, sb):
    nh, seq, _ = q_raw.shape
    bq = min(ATT_BQ, seq)
    scale = (QK_NOPE + QK_ROPE) ** -0.5

    def body(q_ref, kn_ref, v_ref, kp_ref, cos_ref, sa_ref, sb_ref, o_ref):
        for r0 in range(0, seq, bq):
            rows, kend = pl.ds(r0, bq), r0 + bq
            qn = q_ref[rows, :QK_NOPE].astype(BF16)
            qp = _rope128(q_ref[rows, QK_NOPE:], cos_ref[rows, :], sa_ref[rows, :], sb_ref[rows, :]).astype(BF16)
            p = _probs(qn, qp, kn_ref[:kend, :], kp_ref[:kend, :], r0, scale)
            o_ref[rows, :] = jnp.dot(p.astype(BF16), v_ref[:kend, :], preferred_element_type=F32)

    return pl.pallas_call(
        body, name="attn_fwd", grid=(nh,), in_specs=_attn_specs(seq),
        out_specs=pl.BlockSpec((seq, V_DIM), lambda h: (0, h)),
        out_shape=jax.ShapeDtypeStruct((seq, nh * V_DIM), F32), compiler_params=_cparams(),
    )(q_raw, kv, kv, kpe, cos, sa, sb)


def _attn_bwd(q_raw, kv, kpe, cos, sa, sb, do):
    nh, seq, _ = q_raw.shape
    bq = min(ATT_BQ, seq)
    scale = (QK_NOPE + QK_ROPE) ** -0.5

    def body(q_ref, kn_ref, v_ref, kp_ref, cos_ref, sa_ref, sb_ref, do_ref, dq_ref, dkv_ref, dkp_ref):
        dkv_ref[...] = jnp.zeros_like(dkv_ref)
        dkp_ref[...] = jnp.zeros_like(dkp_ref)
        for r0 in range(0, seq, bq):
            rows, kend = pl.ds(r0, bq), r0 + bq
            cos_b, sa_b, sb_b = cos_ref[rows, :], sa_ref[rows, :], sb_ref[rows, :]
            qn = q_ref[rows, :QK_NOPE].astype(BF16)
            qp = _rope128(q_ref[rows, QK_NOPE:], cos_b, sa_b, sb_b).astype(BF16)
            kn, v, kp = kn_ref[:kend, :], v_ref[:kend, :], kp_ref[:kend, :]
            p = _probs(qn, qp, kn, kp, r0, scale)
            dob = do_ref[rows, :].astype(BF16)
            dp = lax.dot_general(dob, v, (NT, ((), ())), preferred_element_type=F32)
            ds = p * (dp - jnp.sum(p * dp, axis=-1, keepdims=True)) * scale
            dsb = ds.astype(BF16)
            pb = p.astype(BF16)
            dq_ref[rows, :QK_NOPE] = jnp.dot(dsb, kn, preferred_element_type=F32).astype(dq_ref.dtype)
            dqp = jnp.dot(dsb, kp, preferred_element_type=F32)
            dq_ref[rows, QK_NOPE:] = _rope128_t(dqp, cos_b, sa_b, sb_b).astype(dq_ref.dtype)
            dkv_ref[:kend, :QK_NOPE] += lax.dot_general(dsb, qn, (TN, ((), ())), preferred_element_type=F32)
            dkv_ref[:kend, QK_NOPE:] += lax.dot_general(pb, dob, (TN, ((), ())), preferred_element_type=F32)
            dkp_ref[:kend, :] += lax.dot_general(dsb, qp, (TN, ((), ())), preferred_element_type=F32)

    sd = jax.ShapeDtypeStruct
    return pl.pallas_call(
        body, name="attn_bwd", grid=(nh,),
        in_specs=_attn_specs(seq) + [pl.BlockSpec((seq, V_DIM), lambda h: (0, h))],
        out_specs=[pl.BlockSpec((None, seq, 256), lambda h: (h, 0, 0)),
                   pl.BlockSpec((None, seq, 256), lambda h: (h, 0, 0)),
                   pl.BlockSpec((None, seq, 128), lambda h: (h, 0, 0))],
        out_shape=[sd((nh, seq, 256), BF16), sd((nh, seq, 256), F32), sd((nh, seq, 128), F32)],
        compiler_params=_cparams(),
    )(q_raw, kv, kv, kpe, cos, sa, sb, do)


def _conv3(a, w, b):
    rows = lax.broadcasted_iota(jnp.int32, a.shape, 0)
    a1 = jnp.where(rows >= 1, pltpu.roll(a, 1, 0), 0.0)
    a2 = jnp.where(rows >= 2, pltpu.roll(a, 2, 0), 0.0)
    return w[2:3] * a + w[1:2] * a1 + w[0:1] * a2 + b, a1, a2


def _conv_gate_fwd(a, cw, cb):
    half, _, seq, c = a.shape
    nc = c // LANES

    def fn(pair, wg, wv, bg, bv):
        gc, _, _ = _conv3(pair[0], wg, bg)
        vc, _, _ = _conv3(pair[1], wv, bv)
        return gc * jax.nn.sigmoid(gc) * vc

    def w_spec(off, r):
        return pl.BlockSpec((None, r, LANES), lambda k, j: (k + off, 0, j))

    return _blockwise(
        "conv_gate_fwd", fn, [a, cw, cw, cb, cb],
        [pl.BlockSpec((None, 2, seq, LANES), lambda k, j: (k, 0, 0, j)),
         w_spec(0, 3), w_spec(half, 3), w_spec(0, 1), w_spec(half, 1)],
        [((seq, half * c), BF16)], [pl.BlockSpec((seq, LANES), lambda k, j: (0, k * nc + j))],
        grid=(half, nc))[0]


def _conv_gate_bwd(a, cw, cb, dm):
    half, _, seq, c = a.shape
    nc = c // LANES

    def body(a_ref, wg_ref, wv_ref, bg_ref, bv_ref, dm_ref, da_ref, dw_ref, db_ref):
        dmv = dm_ref[...]
        rows = lax.broadcasted_iota(jnp.int32, dmv.shape, 0)
        ga, wg = a_ref[0], wg_ref[...]
        va, wv = a_ref[1], wv_ref[...]
        gc, g1, g2 = _conv3(ga, wg, bg_ref[...])
        vc, v1, v2 = _conv3(va, wv, bv_ref[...])
        sg = jax.nn.sigmoid(gc)
        dms = dmv * sg
        d_val = dms * gc
        d_gate = dms * vc * (1.0 + gc * (1.0 - sg))

        def back(r, dc, own, a1, a2, w):
            up1 = jnp.where(rows < seq - 1, pltpu.roll(dc, seq - 1, 0), 0.0)
            up2 = jnp.where(rows < seq - 2, pltpu.roll(dc, seq - 2, 0), 0.0)
            da_ref[r] = (w[2:3] * dc + w[1:2] * up1 + w[0:1] * up2).astype(da_ref.dtype)
            dw_ref[r, 0:1, :] = jnp.sum(dc * a2, axis=0, keepdims=True)
            dw_ref[r, 1:2, :] = jnp.sum(dc * a1, axis=0, keepdims=True)
            dw_ref[r, 2:3, :] = jnp.sum(dc * own, axis=0, keepdims=True)
            db_ref[r] = jnp.sum(dc, axis=0, keepdims=True)

        back(0, d_gate, ga, g1, g2, wg)
        back(1, d_val, va, v1, v2, wv)

    def w_spec(off, r):
        return pl.BlockSpec((None, r, LANES), lambda k, j: (k + off, 0, j))

    def pair_spec(r):
        return pl.BlockSpec((None, 2, r, LANES), lambda k, j: (k, 0, 0, j))

    sd = jax.ShapeDtypeStruct
    return pl.pallas_call(
        body, name="conv_gate_bwd", grid=(half, nc),
        in_specs=[pair_spec(seq), w_spec(0, 3), w_spec(half, 3), w_spec(0, 1), w_spec(half, 1),
                  pl.BlockSpec((seq, LANES), lambda k, j: (0, k * nc + j))],
        out_specs=[pair_spec(seq), pair_spec(3), pair_spec(1)],
        out_shape=[sd((half, 2, seq, c), BF16), sd((half, 2, 3, c), F32), sd((half, 2, 1, c), F32)],
        compiler_params=_cparams(),
    )(a, cw, cw, cb, cb, dm)


ROW_T = 256


def _local_step(x, positions, target, w, emit=lambda **grads: None):
    seq, d = x.shape
    t_row = min(ROW_T, seq)
    nrow = seq // t_row
    ssm_w = d // 2
    nj = ssm_w // LANES
    n_groups = ssm_w // SSM_GROUP
    nh = w["wuq"].shape[0]
    q_rank = w["wuq"].shape[1]
    kv_rank = w["wukv"].shape[1]
    ns = w["wup"].shape[0]
    c_ff = w["wup"].shape[2]
    in_pad = w["win"].shape[1]
    tm = min(1024, seq)
    nm = seq // tm
    sw = 2 * STATE_BLOCK
    g1 = (nrow,)

    lr3 = w["lam_re"].reshape(n_groups, 1, SSM_STATE)
    li3 = w["lam_im"].reshape(n_groups, 1, SSM_STATE)
    ldt3 = w["log_dt"].reshape(n_groups, 1, 1)
    bt_re = jnp.swapaxes(w["b_re"].reshape(n_groups, SSM_STATE, SSM_GROUP), 1, 2)
    bt_im = jnp.swapaxes(w["b_im"].reshape(n_groups, SSM_STATE, SSM_GROUP), 1, 2)
    abar_re, abar_im, bbt_re, bbt_im = _s5_prep(lr3, li3, ldt3, bt_re, bt_im)
    eye = jnp.eye(GROUPS_PER_BLOCK, dtype=F32)

    def blockdiag_in(bb):
        t = bb.reshape(nj, GROUPS_PER_BLOCK, SSM_GROUP, SSM_STATE)
        return jnp.einsum("jghp,gk->jghkp", t, eye).reshape(nj, LANES, STATE_BLOCK)

    def blockdiag_in_t(dwb):
        t = dwb.reshape(nj, GROUPS_PER_BLOCK, SSM_GROUP, GROUPS_PER_BLOCK, SSM_STATE)
        return jnp.einsum("jghkp,gk->jghp", t, eye).reshape(n_groups, SSM_GROUP, SSM_STATE)

    def blockdiag_out(cc):
        t = cc.reshape(nj, GROUPS_PER_BLOCK, SSM_GROUP, SSM_STATE)
        return jnp.einsum("jghp,gk->jkpgh", t, eye).reshape(nj, STATE_BLOCK, LANES)

    def blockdiag_out_t(dwc):
        t = dwc.reshape(nj, GROUPS_PER_BLOCK, SSM_STATE, GROUPS_PER_BLOCK, SSM_GROUP)
        return jnp.einsum("jkpgh,gk->jghp", t, eye).reshape(n_groups, SSM_GROUP, SSM_STATE)

    c_re = w["c_re"].reshape(n_groups, SSM_GROUP, SSM_STATE)
    c_im = w["c_im"].reshape(n_groups, SSM_GROUP, SSM_STATE)
    wb = jnp.concatenate([blockdiag_in(bbt_re), blockdiag_in(bbt_im)], axis=2).astype(BF16)
    wc = jnp.concatenate([blockdiag_out(c_re), -blockdiag_out(c_im)], axis=1).astype(BF16)
    a_lay = jnp.concatenate([abar_re.reshape(nj, 1, STATE_BLOCK), abar_im.reshape(nj, 1, STATE_BLOCK)],
                            axis=1).reshape(1, nj * sw)

    attn_w = w["attn_norm"]
    hn = _blockwise("norm1", lambda xb, wv: _rms(xb, wv), [x, attn_w], [_row_spec(t_row, d), _full_spec((1, d))],
                    [((seq, d), BF16)], [_row_spec(t_row, d)], g1)[0]
    proj = _mm2d("proj", hn, w["win"], NN, F32, tn=640)

    s_all, ylin = _ssm_fwd(proj, wb, wc, a_lay)
    u_spec = pl.BlockSpec((t_row, ssm_w), lambda i: (i, 0))

    def ypre_fn(yl, ub, dsk):
        yp = yl + dsk * ub
        return yp, jax.nn.gelu(yp)

    y_pre, yg = _blockwise("ssm_gelu", ypre_fn, [ylin, proj, w["ssm_d"]],
                           [_row_spec(t_row, ssm_w), u_spec, _full_spec((1, ssm_w))],
                           [((seq, ssm_w), F32), ((seq, ssm_w), BF16)],
                           [_row_spec(t_row, ssm_w)] * 2, g1)
    z = _mm2d("ssm_glu", yg, w["wglu"], NN, F32, res=w["b_glu"])
    y_ssm = _blockwise("ssm_gate", lambda yp, zb: jax.nn.gelu(yp) * jax.nn.sigmoid(zb), [y_pre, z],
                       [_row_spec(t_row, ssm_w)] * 2, [((seq, ssm_w), F32)], [_row_spec(t_row, ssm_w)], g1)[0]

    cq_off, ckv_off, kpe_off = ssm_w, ssm_w + q_rank, ssm_w + q_rank + kv_rank
    c_q = proj[:, cq_off:ckv_off]
    c_kv = proj[:, ckv_off:kpe_off]
    kpe_raw = proj[:, kpe_off:kpe_off + LANES]
    pos_b = jnp.broadcast_to(positions.astype(F32)[:, None], (seq, LANES))
    inv_freq = ROPE_THETA ** (-jnp.arange(0, QK_ROPE, 2, dtype=F32) / QK_ROPE)
    inv128 = jnp.tile(inv_freq, 4).reshape(1, LANES)

    def mla_prep_fn(cq, ckv, kp, pb, inv, wq, wkv):
        ang = pb * inv
        lane = lax.broadcasted_iota(jnp.int32, ang.shape, 1)
        cs, sn = jnp.cos(ang), jnp.sin(ang)
        cos = jnp.where(lane < QK_ROPE, cs, 0.0)
        sa = jnp.where(lane < QK_ROPE // 2, -sn, 0.0)
        sb = jnp.where(jnp.logical_and(lane >= QK_ROPE // 2, lane < QK_ROPE), sn, 0.0)
        return _rms(cq, wq), _rms(ckv, wkv), _rope128(kp, cos, sa, sb), cos, sa, sb

    qn, kvn, kpe, cos_t, sa_t, sb_t = _blockwise(
        "mla_prep", mla_prep_fn, [c_q, c_kv, kpe_raw, pos_b, inv128, w["q_norm"], w["kv_norm"]],
        [_row_spec(t_row, q_rank), _row_spec(t_row, kv_rank), _row_spec(t_row, LANES), _row_spec(t_row, LANES),
         _full_spec((1, LANES)), _full_spec((1, q_rank)), _full_spec((1, kv_rank))],
        [((seq, q_rank), BF16), ((seq, kv_rank), BF16), ((seq, LANES), BF16)] + [((seq, LANES), F32)] * 3,
        [_row_spec(t_row, q_rank), _row_spec(t_row, kv_rank)] + [_row_spec(t_row, LANES)] * 4, g1)

    def head_mm(name, act, wh, out_dtype):
        kdim, ndim = wh.shape[1], wh.shape[2]
        return _mm(name, act, wh, grid=(nh, 1, 1),
                   a_spec=pl.BlockSpec((seq, kdim), lambda h, i, k: (i, 0)),
                   b_spec=pl.BlockSpec((None, kdim, ndim), lambda h, i, k: (h, 0, 0)),
                   o_spec=pl.BlockSpec((None, seq, ndim), lambda h, i, k: (h, i, 0)),
                   out_shape=(nh, seq, ndim), out_dtype=out_dtype)

    q_raw = head_mm("mla_q", qn, w["wuq"], F32)
    kv = head_mm("mla_kv", kvn, w["wukv"], BF16)
    y_mla = _attn_fwd(q_raw, kv, kpe, cos_t, sa_t, sb_t)
    mla_w = nh * V_DIM

    def outnorm_fn(ys, ym, ws, wm):
        return jnp.concatenate([_rms(ys, ws), _rms(ym, wm)], axis=1)

    ycat = _blockwise("out_norm", outnorm_fn, [y_ssm, y_mla, w["son"], w["mon"]],
                      [_row_spec(t_row, ssm_w), _row_spec(t_row, mla_w), _full_spec((1, ssm_w)), _full_spec((1, mla_w))],
                      [((seq, d), BF16)], [_row_spec(t_row, d)], g1)[0]
    h1 = _mm2d("out_proj", ycat, w["wout"], NN, F32, res=x)

    hn2 = _blockwise("norm2", lambda hb, wv: _rms(hb, wv), [h1, w["ffn_norm"]],
                     [_row_spec(t_row, d), _full_spec((1, d))], [((seq, d), BF16)], [_row_spec(t_row, d)], g1)[0]
    tku = d
    half = ns // 2
    a_ff = _mm("ffn_up", hn2, w["wup"], grid=(ns, nm, d // tku),
               a_spec=pl.BlockSpec((tm, tku), lambda s, i, k: (i, k)),
               b_spec=pl.BlockSpec((None, tku, c_ff), lambda s, i, k: (s, k, 0)),
               o_spec=pl.BlockSpec((None, None, tm, c_ff), lambda s, i, k: (s % half, s // half, i, 0)),
               out_shape=(half, 2, seq, c_ff), out_dtype=F32)
    cb3 = w["conv_b"].reshape(ns, 1, c_ff)
    m_ff = _conv_gate_fwd(a_ff, w["conv_w"], cb3)
    d_ff = half * c_ff
    wdn = w["wdown"]
    tnd = _tile(d, 1024)
    tmx, tnx = min(1024, seq), _tile(d, 1024)
    h2 = _mm2d("ffn_down", m_ff, wdn, NN, F32, tm=512, tn=512, tk=d_ff, res=h1)

    def loss_fn(hb, tb, wv):
        def f(hh, ww):
            err = _rms(hh, ww) - tb
            return 0.5 * jnp.sum(jnp.mean(err * err, axis=-1))

        lossv, (dh, dw) = jax.value_and_grad(f, argnums=(0, 1))(hb, wv)
        return dh, dh, jnp.full((1, LANES), lossv, F32), dw

    fin_w = w["final_norm"].reshape(1, d)
    dh2, dh2b, loss_acc, g_final = _blockwise(
        "loss_head", loss_fn, [h2, target, fin_w], [_row_spec(t_row, d), _row_spec(t_row, d), _full_spec((1, d))],
        [((seq, d), F32), ((seq, d), BF16), ((1, LANES), F32), ((1, d), F32)],
        [_row_spec(t_row, d), _row_spec(t_row, d), _full_spec((1, LANES)), _full_spec((1, d))], g1, n_acc=2)
    loss = loss_acc[0, 0]

    dm = _mm2d("ffn_down_dx", dh2b, wdn, NT, F32, tn=c_ff)
    tks = seq
    g_wdown = _mm2d("ffn_down_dw", m_ff, dh2b, TN, BF16, tm=c_ff)
    emit(wdown=g_wdown)
    da_ff, g_convw2, g_convb2 = _conv_gate_bwd(a_ff, w["conv_w"], cb3, dm)
    g_convw = jnp.swapaxes(g_convw2, 0, 1).reshape(ns, 3, c_ff)
    g_convb = jnp.swapaxes(g_convb2, 0, 1).reshape(ns, 1, c_ff)
    g_wup = _mm("ffn_up_dw", hn2, da_ff, grid=(ns, d // tnd, seq // tks), contract=TN,
                a_spec=pl.BlockSpec((tks, tnd), lambda s, j, k: (k, j)),
                b_spec=pl.BlockSpec((None, None, tks, c_ff), lambda s, j, k: (s % half, s // half, k, 0)),
                o_spec=pl.BlockSpec((None, tnd, c_ff), lambda s, j, k: (s, j, 0)),
                out_shape=(ns, d, c_ff), out_dtype=BF16)
    emit(wup=g_wup, conv_w=g_convw)
    dhn2 = _mm("ffn_up_dx", da_ff, w["wup"], grid=(seq // tmx, d // tnx, ns), contract=NT,
               a_spec=pl.BlockSpec((None, None, tmx, c_ff), lambda i, j, s: (s % half, s // half, i, 0)),
               b_spec=pl.BlockSpec((None, tnx, c_ff), lambda i, j, s: (s, j, 0)),
               o_spec=pl.BlockSpec((tmx, tnx), lambda i, j, s: (i, j)),
               out_shape=(seq, d), out_dtype=F32)

    def norm_bwd_fn(hb, dres, dn, wv):
        dx_, dw_ = _rms_bwd(hb, wv, dn)
        dtot = dres + dx_
        return dtot, dtot, dw_

    dh1, dh1b, g_ffn_norm = _blockwise(
        "norm2_bwd", norm_bwd_fn, [h1, dh2, dhn2, w["ffn_norm"]],
        [_row_spec(t_row, d)] * 3 + [_full_spec((1, d))],
        [((seq, d), F32), ((seq, d), BF16), ((1, d), F32)],
        [_row_spec(t_row, d), _row_spec(t_row, d), _full_spec((1, d))], g1, n_acc=1)

    dycat = _mm2d("out_proj_dx", dh1b, w["wout"], NT, F32)
    g_wout = _mm2d("out_proj_dw", ycat, dh1b, TN, BF16)

    def outnorm_bwd_fn(ys, ym, dyc, ws, wm):
        dys, dws = _rms_bwd(ys, ws, dyc[:, :ssm_w])
        dym, dwm = _rms_bwd(ym, wm, dyc[:, ssm_w:])
        return dys, dym, dws, dwm

    dy_ssm, dy_mla, g_son, g_mon = _blockwise(
        "out_norm_bwd", outnorm_bwd_fn, [y_ssm, y_mla, dycat, w["son"], w["mon"]],
        [_row_spec(t_row, ssm_w), _row_spec(t_row, mla_w), _row_spec(t_row, d), _full_spec((1, ssm_w)),
         _full_spec((1, mla_w))],
        [((seq, ssm_w), F32), ((seq, mla_w), F32), ((1, ssm_w), F32), ((1, mla_w), F32)],
        [_row_spec(t_row, ssm_w), _row_spec(t_row, mla_w), _full_spec((1, ssm_w)), _full_spec((1, mla_w))],
        g1, n_acc=2)

    def gate_bwd1_fn(dy, yp, zb):
        ygv = jax.nn.gelu(yp)
        sg = jax.nn.sigmoid(zb)
        dz = dy * ygv * sg * (1.0 - sg)
        return dz, jnp.sum(dz, axis=0, keepdims=True)

    dz, g_bglu = _blockwise("ssm_gate_bwd", gate_bwd1_fn, [dy_ssm, y_pre, z], [_row_spec(t_row, ssm_w)] * 3,
                            [((seq, ssm_w), BF16), ((1, ssm_w), F32)],
                            [_row_spec(t_row, ssm_w), _full_spec((1, ssm_w))], g1, n_acc=1)
    dyg2 = _mm2d("ssm_glu_dx", dz, w["wglu"], NT, F32)
    g_wglu = _mm2d("ssm_glu_dw", yg, dz, TN, BF16)

    def gelu_bwd_fn(dy, yp, zb, dg2, ub, dsk):
        dyg = dy * jax.nn.sigmoid(zb) + dg2
        _, vjp = jax.vjp(jax.nn.gelu, yp)
        dyp = vjp(dyg)[0]
        return dyp, dyp * dsk, jnp.sum(dyp * ub, axis=0, keepdims=True)

    dy_pre, du1, g_ssmd = _blockwise(
        "ssm_gelu_bwd", gelu_bwd_fn, [dy_ssm, y_pre, z, dyg2, proj, w["ssm_d"]],
        [_row_spec(t_row, ssm_w)] * 4 + [u_spec, _full_spec((1, ssm_w))],
        [((seq, ssm_w), BF16), ((seq, ssm_w), F32), ((1, ssm_w), F32)],
        [_row_spec(t_row, ssm_w), _row_spec(t_row, ssm_w), _full_spec((1, ssm_w))], g1, n_acc=1)
    dq_raw, dkv, dkp_h = _attn_bwd(q_raw, kv, kpe, cos_t, sa_t, sb_t, dy_mla)

    def head_mm_dx(name, dact, wh):
        kdim, ndim = wh.shape[1], wh.shape[2]
        return _mm(name, dact, wh, grid=(1, 1, nh), contract=NT,
                   a_spec=pl.BlockSpec((None, seq, ndim), lambda i, j, h: (h, i, 0)),
                   b_spec=pl.BlockSpec((None, kdim, ndim), lambda i, j, h: (h, 0, 0)),
                   o_spec=pl.BlockSpec((seq, kdim), lambda i, j, h: (i, 0)),
                   out_shape=(seq, kdim), out_dtype=F32)

    def head_mm_dw(name, act, dact):
        kdim, ndim = act.shape[1], dact.shape[2]
        return _mm(name, act, dact, grid=(nh, 1, seq // tks), contract=TN,
                   a_spec=pl.BlockSpec((tks, kdim), lambda h, j, k: (k, 0)),
                   b_spec=pl.BlockSpec((None, tks, ndim), lambda h, j, k: (h, k, 0)),
                   o_spec=pl.BlockSpec((None, kdim, ndim), lambda h, j, k: (h, 0, 0)),
                   out_shape=(nh, kdim, ndim), out_dtype=BF16)

    g_wuq = head_mm_dw("mla_q_dw", qn, dq_raw)
    g_wukv = head_mm_dw("mla_kv_dw", kvn, dkv)
    dqn = head_mm_dx("mla_q_dx", dq_raw, w["wuq"])
    dkvn = head_mm_dx("mla_kv_dx", dkv, w["wukv"])
    emit(not_before=(dqn, dkvn, dy_pre), wout=g_wout, wuq=g_wuq, wukv=g_wukv, wglu=g_wglu)

    du, dwb, dwc, da_lay = _ssm_bwd(dy_pre, s_all, proj, du1, wb, wc, a_lay)
    g_c_re = blockdiag_out_t(dwc[:, :STATE_BLOCK, :])
    g_c_im = -blockdiag_out_t(dwc[:, STATE_BLOCK:, :])
    dbbt_re = blockdiag_in_t(dwb[:, :, :STATE_BLOCK])
    dbbt_im = blockdiag_in_t(dwb[:, :, STATE_BLOCK:])
    da3 = da_lay.reshape(nj, 2, STATE_BLOCK)
    dabar_re = da3[:, 0, :].reshape(n_groups, 1, SSM_STATE)
    dabar_im = da3[:, 1, :].reshape(n_groups, 1, SSM_STATE)
    g_lr3, g_li3, g_ldt3, g_bt_re, g_bt_im = _s5_prep_bwd(lr3, li3, ldt3, bt_re, bt_im,
                                                           dabar_re, dabar_im, dbbt_re, dbbt_im)

    def mla_prep_bwd_fn(cq, ckv, dqn_b, dkvn_b, dkp_b, cos, sa, sb, wq, wkv):
        dcq, dwq = _rms_bwd(cq, wq, dqn_b)
        dckv, dwkv = _rms_bwd(ckv, wkv, dkvn_b)
        dkp_sum = dkp_b[0]
        for h in range(1, nh):
            dkp_sum = dkp_sum + dkp_b[h]
        return dcq, dckv, _rope128_t(dkp_sum, cos, sa, sb), dwq, dwkv

    dc_q, dc_kv, dkpe_raw, g_qnorm, g_kvnorm = _blockwise(
        "mla_prep_bwd", mla_prep_bwd_fn, [c_q, c_kv, dqn, dkvn, dkp_h, cos_t, sa_t, sb_t, w["q_norm"], w["kv_norm"]],
        [_row_spec(t_row, q_rank), _row_spec(t_row, kv_rank), _row_spec(t_row, q_rank), _row_spec(t_row, kv_rank),
         pl.BlockSpec((nh, t_row, LANES), lambda i: (0, i, 0))] + [_row_spec(t_row, LANES)] * 3
        + [_full_spec((1, q_rank)), _full_spec((1, kv_rank))],
        [((seq, q_rank), BF16), ((seq, kv_rank), BF16), ((seq, LANES), BF16), ((1, q_rank), F32), ((1, kv_rank), F32)],
        [_row_spec(t_row, q_rank), _row_spec(t_row, kv_rank), _row_spec(t_row, LANES), _full_spec((1, q_rank)),
         _full_spec((1, kv_rank))], g1, n_acc=2)

    dproj = jnp.concatenate([du, dc_q, dc_kv, dkpe_raw], axis=1)
    g_win = _mm2d("proj_dw", hn, dproj, TN, BF16, tn=640)
    emit(win=g_win)
    dhn = _mm2d("proj_dx", dproj, w["win"], NT, F32)

    def norm1_bwd_fn(xb, dres, dn, wv):
        dx_, dw_ = _rms_bwd(xb, wv, dn)
        return dres + dx_, dw_

    grad_x, g_attn_norm = _blockwise(
        "norm1_bwd", norm1_bwd_fn, [x, dh1, dhn, attn_w], [_row_spec(t_row, d)] * 3 + [_full_spec((1, d))],
        [((seq, d), F32), ((1, d), F32)], [_row_spec(t_row, d), _full_spec((1, d))], g1, n_acc=1)

    grads = dict(
        attn_norm=g_attn_norm, win=g_win, lam_re=g_lr3, lam_im=g_li3, log_dt=g_ldt3,
        bt_re=g_bt_re, bt_im=g_bt_im, c_re=g_c_re, c_im=g_c_im,
        ssm_d=g_ssmd, wglu=g_wglu, b_glu=g_bglu, q_norm=g_qnorm, wuq=g_wuq, kv_norm=g_kvnorm, wukv=g_wukv,
        son=g_son, mon=g_mon, wout=g_wout, ffn_norm=g_ffn_norm, wup=g_wup, conv_w=g_convw, conv_b=g_convb,
        wdown=g_wdown, final_norm=g_final)
    return loss, grad_x, grads


def _mesh_pos():
    return lax.axis_index("x"), lax.axis_index("y"), lax.axis_index("c")


def _handshake_all():
    x, y, c = _mesh_pos()
    barrier = pltpu.get_barrier_semaphore()
    for k in range(1, N_DEV):
        peer = (1 - x if k & 4 else x, 1 - y if k & 2 else y, 1 - c if k & 1 else c)
        pl.semaphore_signal(barrier, inc=1, device_id=peer, device_id_type=MESH)
    pl.semaphore_wait(barrier, N_DEV - 1)


def _comm_call(name, body, n, out_shape, ins, collective_id, after=None):
    sems = [pltpu.SemaphoreType.DMA((7 * n,)), pltpu.SemaphoreType.DMA((7 * n,)), pltpu.SemaphoreType.DMA((n,))]
    if collective_id is None:
        any_spec = pl.BlockSpec(memory_space=pl.ANY)
        return pl.pallas_call(body, name=name, out_shape=out_shape, in_specs=[any_spec] * n,
                              out_specs=[any_spec] * n, scratch_shapes=sems)(*ins)
    seq_body = body
    if after:
        n_after = len(after)
        ins = list(ins) + list(after)

        def seq_body(*refs):
            body(*refs[:n], *refs[n + n_after:])

    return pl.kernel(seq_body, name=name, out_type=out_shape,
                     mesh=plsc.ScalarSubcoreMesh(axis_name="seq", num_cores=1), scratch_types=sems,
                     compiler_params=pltpu.CompilerParams(collective_id=collective_id))(*ins)


def _all_gather(name, xs, collective_id=None, after=None):
    n = len(xs)

    def body(*refs):
        x_refs, o_refs = refs[:n], refs[n:2 * n]
        send_sems, recv_sems, local_sems = refs[2 * n:]
        if collective_id is not None:
            _handshake_all()
        x, y, c = _mesh_pos()
        me, sibling = (x, y, c), (x, y, 1 - c)
        chips = [(1 - x, y), (x, 1 - y), (1 - x, 1 - y)]

        def slot(o_ref, px, py, pc):
            return o_ref.at[4 * px + 2 * py + pc]

        def copy(t, k, block, to, src=None):
            dst = slot(o_refs[t], *block)
            return pltpu.make_async_remote_copy(
                src_ref=dst if src is None else src, dst_ref=dst,
                send_sem=send_sems.at[7 * t + k], recv_sem=recv_sems.at[7 * t + k],
                device_id=to, device_id_type=MESH)

        started = []
        for t in range(n):
            mine = pltpu.make_async_copy(x_refs[t], slot(o_refs[t], *me), local_sems.at[t])
            mine.start()
            started.append(mine)
        first = []
        for t in range(n):
            first.append(copy(t, 0, me, sibling, src=x_refs[t]))
            first += [copy(t, 1 + j, me, (*chip, c), src=x_refs[t]) for j, chip in enumerate(chips)]
        for cp in first:
            cp.start()
        passed = []
        for j, chip in enumerate(chips):
            for t in range(n):
                copy(t, 1 + j, (*chip, c), me).wait_recv()
                fwd = copy(t, 4 + j, (*chip, c), sibling)
                fwd.start()
                passed.append(fwd)
        for t in range(n):
            copy(t, 0, sibling, me).wait_recv()
            for j, chip in enumerate(chips):
                copy(t, 4 + j, (*chip, 1 - c), me).wait_recv()
        for cp in first + passed:
            cp.wait_send()
        for mine in started:
            mine.wait()

    out_shape = [jax.ShapeDtypeStruct((N_DEV,) + v.shape, v.dtype) for v in xs]
    return _comm_call(name, body, n, out_shape, xs, collective_id, after)


def _exchange_partials(name, gs, collective_id=None, after=None):
    n = len(gs)

    def body(*refs):
        g_refs, o_refs = refs[:n], refs[n:2 * n]
        send_sems, recv_sems, local_sems = refs[2 * n:]
        if collective_id is not None:
            _handshake_all()
        x, y, c = _mesh_pos()
        me_idx = 4 * x + 2 * y + c
        copies = []
        for t in range(n):
            mine = pltpu.make_async_copy(g_refs[t].at[me_idx], o_refs[t].at[me_idx], local_sems.at[t])
            mine.start()
            copies.append(mine)
        remote = []
        for k in range(1, N_DEV):
            px = 1 - x if k & 4 else x
            py = 1 - y if k & 2 else y
            pc = 1 - c if k & 1 else c
            p_idx = 4 * px + 2 * py + pc
            for t in range(n):
                cp = pltpu.make_async_remote_copy(
                    src_ref=g_refs[t].at[p_idx], dst_ref=o_refs[t].at[me_idx],
                    send_sem=send_sems.at[7 * t + k - 1], recv_sem=recv_sems.at[7 * t + k - 1],
                    device_id=(px, py, pc), device_id_type=MESH)
                cp.start()
                landing = pltpu.make_async_remote_copy(
                    src_ref=g_refs[t].at[p_idx], dst_ref=o_refs[t].at[p_idx],
                    send_sem=send_sems.at[7 * t + k - 1], recv_sem=recv_sems.at[7 * t + k - 1],
                    device_id=(px, py, pc), device_id_type=MESH)
                remote.append((cp, landing))
        for cp, landing in remote:
            landing.wait_recv()
        for cp, landing in remote:
            cp.wait_send()
        for mine in copies:
            mine.wait()

    out_shape = [jax.ShapeDtypeStruct(v.shape, v.dtype) for v in gs]
    return _comm_call(name, body, n, out_shape, gs, collective_id, after)


ADAM_BLOCK_ELEMS = 128 * 1024


def _sum_parts(pb):
    g = pb[0].astype(F32)
    for j in range(1, pb.shape[0]):
        g = g + pb[j].astype(F32)
    return g


def _adam_math(g, wb_, mb, vb):
    m_new = ADAM_B1 * mb + (1.0 - ADAM_B1) * g
    v_new = ADAM_B2 * vb + (1.0 - ADAM_B2) * (g * g)
    m_hat = m_new / (1.0 - ADAM_B1 ** ADAM_STEP)
    v_hat = v_new / (1.0 - ADAM_B2 ** ADAM_STEP)
    delta = -ADAM_LR * (m_hat / (jnp.sqrt(v_hat) + ADAM_EPS) + ADAM_WD * wb_)
    return g, delta, m_new, v_new


def _adamw_multi(name, items, nblk=1):
    n = len(items)

    def body(*refs):
        for t in range(n):
            pr, wr, mr, vr = refs[4 * t:4 * t + 4]
            res = _adam_math(_sum_parts(pr[...]), wr[...], mr[...], vr[...])
            for o, val in zip(refs[4 * n + 4 * t:4 * n + 4 * t + 4], res):
                o[...] = val

    def spec(shape, lead):
        blk = list(shape)
        blk[lead + 1] = shape[lead + 1] // nblk
        if nblk == 1:
            return pl.BlockSpec(tuple(blk), lambda i, nd=len(shape): (0,) * nd)
        return pl.BlockSpec(tuple(blk), lambda i, nd=len(shape), ax=lead + 1: (0,) * ax + (i,) + (0,) * (nd - ax - 1))

    in_specs, out_specs, out_shape, ins = [], [], [], []
    for parts, wv, mv, vv in items:
        assert parts.shape[1:] == wv.shape, (name, parts.shape, wv.shape)
        ins += [parts, wv, mv, vv]
        in_specs += [spec(parts.shape, 1)] + [spec(wv.shape, 0)] * 3
        out_specs += [spec(wv.shape, 0)] * 4
        out_shape += [jax.ShapeDtypeStruct(wv.shape, F32)] * 4
    res = pl.pallas_call(body, name=name, grid=(nblk,), in_specs=in_specs, out_specs=out_specs, out_shape=out_shape,
                         compiler_params=_cparams())(*ins)
    return [tuple(res[4 * t:4 * t + 4]) for t in range(n)]


def _sum_multi(name, parts_list):
    def body(*refs):
        for pr, o in zip(refs[:len(parts_list)], refs[len(parts_list):]):
            o[...] = _sum_parts(pr[...])

    return pl.pallas_call(body, name=name, out_shape=[jax.ShapeDtypeStruct(p.shape[1:], F32) for p in parts_list],
                          compiler_params=_cparams())(*parts_list)


def _adamw_sum(name, parts, wv, mv, vv):
    npart, r, c = parts.shape
    tr = r
    if r * c > ADAM_BLOCK_ELEMS and r % SUBLANES == 0:
        tr = SUBLANES
        while r % (tr * 2) == 0 and tr * 2 * c <= ADAM_BLOCK_ELEMS:
            tr *= 2

    def fn(pb, wb_, mb, vb):
        return _adam_math(_sum_parts(pb), wb_, mb, vb)

    row = pl.BlockSpec((tr, c), lambda i: (i, 0))
    return _blockwise(name, fn, [parts, wv, mv, vv],
                      [pl.BlockSpec((npart, tr, c), lambda i: (0, i, 0)), row, row, row],
                      [((r, c), F32)] * 4, [row] * 4, (r // tr,))


_VECTORS = ["attn_norm", "lam_re", "lam_im", "log_dt", "ssm_d", "b_glu", "q_norm", "kv_norm", "son", "mon",
            "ffn_norm", "conv_b", "final_norm"]
_BIG = ["win", "wglu", "wuq", "wukv", "wout", "wup", "wdown", "conv_w"]
_ORDER = ["attn_norm", "win", "lam_re", "lam_im", "log_dt", "b_re", "b_im", "c_re", "c_im", "ssm_d", "wglu",
          "b_glu", "q_norm", "wuq", "kv_norm", "wukv", "son", "mon", "wout", "ffn_norm", "wup", "conv_w",
          "conv_b", "wdown", "final_norm"]


def kernel(x, positions, attn_norm_w, w_in, ssm_lambda_re, ssm_lambda_im, ssm_log_dt, ssm_b_re, ssm_b_im, ssm_c_re, ssm_c_im, ssm_d, ssm_w_glu, ssm_b_glu, mla_q_norm_w, mla_w_uq, mla_kv_norm_w, mla_w_ukv, ssm_out_norm_w, mla_out_norm_w, w_out, ffn_norm_w, ffn_w_up, ffn_conv_w, ffn_conv_b, ffn_w_down, final_norm_w, loss_target, m_attn_norm_w, m_w_in, m_ssm_lambda_re, m_ssm_lambda_im, m_ssm_log_dt, m_ssm_b_re, m_ssm_b_im, m_ssm_c_re, m_ssm_c_im, m_ssm_d, m_ssm_w_glu, m_ssm_b_glu, m_mla_q_norm_w, m_mla_w_uq, m_mla_kv_norm_w, m_mla_w_ukv, m_ssm_out_norm_w, m_mla_out_norm_w, m_w_out, m_ffn_norm_w, m_ffn_w_up, m_ffn_conv_w, m_ffn_conv_b, m_ffn_w_down, m_final_norm_w, v_attn_norm_w, v_w_in, v_ssm_lambda_re, v_ssm_lambda_im, v_ssm_log_dt, v_ssm_b_re, v_ssm_b_im, v_ssm_c_re, v_ssm_c_im, v_ssm_d, v_ssm_w_glu, v_ssm_b_glu, v_mla_q_norm_w, v_mla_w_uq, v_mla_kv_norm_w, v_mla_w_ukv, v_ssm_out_norm_w, v_mla_out_norm_w, v_w_out, v_ffn_norm_w, v_ffn_w_up, v_ffn_conv_w, v_ffn_conv_b, v_ffn_w_down, v_final_norm_w):
    wts = dict(attn_norm=attn_norm_w, win=w_in, lam_re=ssm_lambda_re, lam_im=ssm_lambda_im, log_dt=ssm_log_dt,
               b_re=ssm_b_re, b_im=ssm_b_im, c_re=ssm_c_re, c_im=ssm_c_im, ssm_d=ssm_d, wglu=ssm_w_glu,
               b_glu=ssm_b_glu, q_norm=mla_q_norm_w, wuq=mla_w_uq, kv_norm=mla_kv_norm_w, wukv=mla_w_ukv,
               son=ssm_out_norm_w, mon=mla_out_norm_w, wout=w_out, ffn_norm=ffn_norm_w, wup=ffn_w_up,
               conv_w=ffn_conv_w, conv_b=ffn_conv_b, wdown=ffn_w_down, final_norm=final_norm_w)
    moms = dict(zip(_ORDER, [m_attn_norm_w, m_w_in, m_ssm_lambda_re, m_ssm_lambda_im, m_ssm_log_dt, m_ssm_b_re,
                             m_ssm_b_im, m_ssm_c_re, m_ssm_c_im, m_ssm_d, m_ssm_w_glu, m_ssm_b_glu, m_mla_q_norm_w,
                             m_mla_w_uq, m_mla_kv_norm_w, m_mla_w_ukv, m_ssm_out_norm_w, m_mla_out_norm_w, m_w_out,
                             m_ffn_norm_w, m_ffn_w_up, m_ffn_conv_w, m_ffn_conv_b, m_ffn_w_down, m_final_norm_w]))
    vels = dict(zip(_ORDER, [v_attn_norm_w, v_w_in, v_ssm_lambda_re, v_ssm_lambda_im, v_ssm_log_dt, v_ssm_b_re,
                             v_ssm_b_im, v_ssm_c_re, v_ssm_c_im, v_ssm_d, v_ssm_w_glu, v_ssm_b_glu, v_mla_q_norm_w,
                             v_mla_w_uq, v_mla_kv_norm_w, v_mla_w_ukv, v_ssm_out_norm_w, v_mla_out_norm_w, v_w_out,
                             v_ffn_norm_w, v_ffn_w_up, v_ffn_conv_w, v_ffn_conv_b, v_ffn_w_down, v_final_norm_w]))
    seq, d = x.shape[1], x.shape[2]
    in_width = w_in.shape[2]
    in_pad = -(-in_width // LANES) * LANES
    q_cols = mla_w_uq.shape[2]
    q_pad = 2 * LANES

    (win_g,) = _all_gather("gather_w_in", [jnp.pad(w_in[0], ((0, 0), (0, in_pad - in_width))).astype(BF16)])
    wglu_g, wuq_g, wukv_g, wout_g, convw_g = _all_gather(
        "gather_mix", [ssm_w_glu[0].astype(BF16), jnp.pad(mla_w_uq[0], ((0, 0), (0, q_pad - q_cols))).astype(BF16),
                       mla_w_ukv[0].astype(BF16), w_out[0].astype(BF16), ffn_conv_w[0]], collective_id=0)
    (wup_g,) = _all_gather("gather_ffn_up", [ffn_w_up[0].astype(BF16)], collective_id=1)
    (wdown_g,) = _all_gather("gather_ffn_down", [ffn_w_down[0].astype(BF16)], collective_id=2)
    ns = N_DEV
    c_ff = wup_g.shape[2]
    w = dict(
        attn_norm=attn_norm_w, win=win_g.reshape(d, in_pad), lam_re=ssm_lambda_re, lam_im=ssm_lambda_im,
        log_dt=ssm_log_dt, b_re=ssm_b_re, b_im=ssm_b_im, c_re=ssm_c_re, c_im=ssm_c_im, ssm_d=ssm_d,
        wglu=wglu_g.reshape(d // 2, d // 2), b_glu=ssm_b_glu, q_norm=mla_q_norm_w, wuq=wuq_g,
        kv_norm=mla_kv_norm_w, wukv=wukv_g, son=ssm_out_norm_w, mon=mla_out_norm_w, wout=wout_g.reshape(d, d),
        ffn_norm=ffn_norm_w, wup=wup_g, conv_w=convw_g, conv_b=ffn_conv_b,
        wdown=wdown_g.reshape(ns // 2 * c_ff, d), final_norm=final_norm_w)

    shard_layout = dict(
        win=lambda a: a[:, :in_width].reshape(N_DEV, d // N_DEV, in_width),
        wglu=lambda a: a.reshape(N_DEV, d // 2 // N_DEV, d // 2),
        wuq=lambda a: a[:, :, :q_cols], wukv=lambda a: a, wout=lambda a: a.reshape(N_DEV, d // N_DEV, d),
        wup=lambda a: a, wdown=lambda a: a.reshape(N_DEV, c_ff // 2, d), conv_w=lambda a: a)
    recv = {}
    next_id = [3]

    last = [None]

    out = {}

    def update(k):
        shp = wts[k].shape
        r, c = shp[-2], shp[-1]
        res = _adamw_sum("adamw_" + k, recv[k].reshape(N_DEV, r, c), wts[k].reshape(r, c),
                         moms[k].reshape(r, c), vels[k].reshape(r, c))
        out[k] = [a.reshape(shp) for a in res]
        return res[0]

    def exchange(not_before=(), **grads):
        names = list(grads)
        if "wout" in names:
            not_before = (*not_before, update("wdown"))
        got = _exchange_partials("exchange_" + "_".join(names), [shard_layout[k](grads[k]) for k in names],
                                 collective_id=next_id[0], after=[a for a in (last[0], *not_before) if a is not None])
        next_id[0] += 1
        last[0] = got[-1]
        recv.update(zip(names, got))

    loss_part, grad_x, g = _local_step(x[0], positions[0], loss_target[0], w, emit=exchange)
    loss = lax.psum(loss_part, ("x", "y", "c"))
    n_groups = ssm_lambda_re.shape[1]
    two_d = {"lam_re": (n_groups, -1), "lam_im": (n_groups, -1)}
    dense = {k: g[k].reshape(two_d.get(k, (1, -1))) for k in _VECTORS}
    dense.update(c_re=g["c_re"], c_im=g["c_im"], bt_re=g["bt_re"], bt_im=g["bt_im"])
    names = list(dense)
    gathered = dict(zip(names, _all_gather("gather_small_grads", [dense[k] for k in names],
                                           collective_id=next_id[0], after=[last[0]])))
    for k in _BIG:
        if k not in out:
            update(k)

    def finish(keys, results):
        for k, res in zip(keys, results):
            out[k] = [a.reshape(wts[k].shape) for a in res]

    view = lambda k, a: a.reshape(dense[k].shape)
    finish(_VECTORS, _adamw_multi("adamw_vectors", [(gathered[k], view(k, wts[k]), view(k, moms[k]), view(k, vels[k]))
                                                    for k in _VECTORS]))
    c_keys = ["c_re", "c_im"]
    finish(c_keys, _adamw_multi("adamw_ssm_c", [(gathered[k][:, None], wts[k], moms[k], vels[k]) for k in c_keys]))
    b_sums = _sum_multi("sum_ssm_b", [gathered["bt_re"], gathered["bt_im"]])
    b_keys = ["b_re", "b_im"]
    finish(b_keys, _adamw_multi("adamw_ssm_b", [(jnp.swapaxes(s, 1, 2)[None, None], wts[k], moms[k], vels[k])
                                                for k, s in zip(b_keys, b_sums)], nblk=SUBLANES))

    grad_x = grad_x.reshape(x.shape)
    return (loss, grad_x, *[out[k][0] for k in _ORDER], *[out[k][1] for k in _ORDER],
            *[out[k][2] for k in _ORDER], *[out[k][3] for k in _ORDER])
```

```python
import functools
import math

import jax
import jax.numpy as jnp
from jax import lax
from jax.experimental import pallas as pl
from jax.experimental.pallas import tpu as pltpu
from jax.experimental.pallas import tpu_sc as plsc

F32 = jnp.float32
BF16 = jnp.bfloat16
MESH = pl.DeviceIdType.MESH

N_DEV = 8
LANES = 128
SUBLANES = 8
VMEM_LIMIT = 48 * 1024 * 1024

SSM_GROUP = 16
SSM_STATE = 64
GROUPS_PER_BLOCK = LANES // SSM_GROUP
STATE_BLOCK = GROUPS_PER_BLOCK * SSM_STATE
QK_NOPE = 128
QK_ROPE = 64
V_DIM = 128
ROPE_THETA = 10000.0
RMS_EPS = 1e-6

ADAM_LR = 0.001
ADAM_B1 = 0.9
ADAM_B2 = 0.999
ADAM_EPS = 1e-08
ADAM_WD = 0.01
ADAM_STEP = 10

NN = ((1,), (0,))
NT = ((1,), (1,))
TN = ((0,), (0,))


def _cparams():
    return pltpu.CompilerParams(vmem_limit_bytes=VMEM_LIMIT)


def _tile(n, want):
    if n <= want:
        return n
    t = (want // LANES) * LANES
    while t >= LANES:
        if n % t == 0:
            return t
        t -= LANES
    return n


def _mm(name, a, b, *, grid, a_spec, b_spec, o_spec, out_shape, out_dtype, contract=NN,
        res=None, res_spec=None):
    nk = grid[-1]
    kaxis = len(grid) - 1
    acc_shape = tuple(d for d in o_spec.block_shape if d is not None)

    def body(*refs):
        a_ref, b_ref = refs[:2]
        r_ref = None if res is None else refs[2]
        o_ref = refs[2 if res is None else 3]
        part = lax.dot_general(a_ref[...].astype(BF16), b_ref[...].astype(BF16),
                               (contract, ((), ())), preferred_element_type=F32)
        if nk == 1:
            if r_ref is not None:
                part = part + r_ref[...].astype(F32)
            o_ref[...] = part.astype(o_ref.dtype)
            return
        acc = refs[-1]
        k = pl.program_id(kaxis)

        @pl.when(k == 0)
        def _():
            acc[...] = part

        @pl.when(k != 0)
        def _():
            acc[...] += part

        @pl.when(k == nk - 1)
        def _():
            r = acc[...]
            if r_ref is not None:
                r = r + r_ref[...].astype(F32)
            o_ref[...] = r.astype(o_ref.dtype)

    ins = [a, b] + ([] if res is None else [res])
    in_specs = [a_spec, b_spec] + ([] if res is None else [res_spec])
    return pl.pallas_call(
        body, name=name, grid=grid, in_specs=in_specs, out_specs=o_spec,
        out_shape=jax.ShapeDtypeStruct(out_shape, out_dtype),
        scratch_shapes=[pltpu.VMEM(acc_shape, F32)] if nk > 1 else [], compiler_params=_cparams(),
    )(*ins)


def _mm2d(name, a, b, contract, out_dtype, tm=1024, tn=1024, tk=2048, res=None):
    if contract == NN:
        (m, kk), n = a.shape, b.shape[1]
    elif contract == NT:
        (m, kk), n = a.shape, b.shape[0]
    else:
        (kk, m), n = a.shape, b.shape[1]
    tm, tn, tk = _tile(m, tm), _tile(n, tn), _tile(kk, tk)
    grid = (m // tm, n // tn, kk // tk)
    if contract == TN:
        a_spec = pl.BlockSpec((tk, tm), lambda i, j, k: (k, i))
    else:
        a_spec = pl.BlockSpec((tm, tk), lambda i, j, k: (i, k))
    if contract == NT:
        b_spec = pl.BlockSpec((tn, tk), lambda i, j, k: (j, k))
    else:
        b_spec = pl.BlockSpec((tk, tn), lambda i, j, k: (k, j))
    o_spec = pl.BlockSpec((tm, tn), lambda i, j, k: (i, j))
    res_spec = None
    if res is not None:
        if res.shape[0] == 1:
            res_spec = pl.BlockSpec((1, tn), lambda i, j, k: (0, j))
        else:
            res_spec = pl.BlockSpec((tm, tn), lambda i, j, k: (i, j))
    return _mm(name, a, b, grid=grid, a_spec=a_spec, b_spec=b_spec, o_spec=o_spec,
               out_shape=(m, n), out_dtype=out_dtype, contract=contract, res=res, res_spec=res_spec)


def _blockwise(name, fn, ins, in_specs, outs, out_specs, grid, n_acc=0, acc_all=True):
    n_in, n_out = len(ins), len(outs)
    n_plain = n_out - n_acc

    def body(*refs):
        vals = fn(*[r[...] for r in refs[:n_in]])
        if not isinstance(vals, (tuple, list)):
            vals = (vals,)
        o_refs = refs[n_in:n_in + n_out]
        for r, v in zip(o_refs[:n_plain], vals[:n_plain]):
            r[...] = v.astype(r.dtype)
        if n_acc:
            if acc_all:
                first = functools.reduce(jnp.logical_and, [pl.program_id(d) == 0 for d in range(len(grid))])
            else:
                first = pl.program_id(len(grid) - 1) == 0

            @pl.when(first)
            def _():
                for r, v in zip(o_refs[n_plain:], vals[n_plain:]):
                    r[...] = v.astype(r.dtype)

            @pl.when(jnp.logical_not(first))
            def _():
                for r, v in zip(o_refs[n_plain:], vals[n_plain:]):
                    r[...] += v.astype(r.dtype)

    return pl.pallas_call(
        body, name=name, grid=grid, in_specs=in_specs, out_specs=out_specs,
        out_shape=[jax.ShapeDtypeStruct(s, d) for s, d in outs], compiler_params=_cparams(),
    )(*ins)


def _row_spec(t, c):
    return pl.BlockSpec((t, c), lambda i: (i, 0))


def _full_spec(shape):
    nd = len(shape)
    return pl.BlockSpec(tuple(shape), lambda *g: (0,) * nd)


def _rms(xf, w):
    return xf * lax.rsqrt(jnp.mean(xf * xf, axis=-1, keepdims=True) + RMS_EPS) * w


def _rms_bwd(xf, w, dy):
    _, vjp = jax.vjp(_rms, xf, w)
    return vjp(dy)


def _s5_disc(lr, li, ldt, bre, bim):
    dt = jnp.exp(ldt)
    mag = jnp.exp(lr * dt)
    ar = mag * jnp.cos(li * dt)
    ai = mag * jnp.sin(li * dt)
    nr, ni = ar - 1.0, ai
    den = lr * lr + li * li
    zr = (nr * lr + ni * li) / den
    zi = (ni * lr - nr * li) / den
    return ar, ai, zr * bre - zi * bim, zr * bim + zi * bre


def _s5_prep(lr, li, ldt, bre, bim):
    def body(lr_r, li_r, ldt_r, bre_r, bim_r, ar_r, ai_r, br_r, bi_r):
        ar, ai, br, bi = _s5_disc(lr_r[...], li_r[...], ldt_r[...], bre_r[...], bim_r[...])
        ar_r[...] = ar
        ai_r[...] = ai
        br_r[...] = br
        bi_r[...] = bi

    sd = jax.ShapeDtypeStruct
    return pl.pallas_call(
        body, name="s5_prep",
        out_shape=[sd(lr.shape, F32), sd(lr.shape, F32), sd(bre.shape, F32), sd(bre.shape, F32)],
        compiler_params=_cparams(),
    )(lr, li, ldt, bre, bim)


def _s5_prep_bwd(lr, li, ldt, bre, bim, dar, dai, dbr, dbi):
    def body(lr_r, li_r, ldt_r, bre_r, bim_r, dar_r, dai_r, dbr_r, dbi_r, o0, o1, o2, o3, o4):
        _, vjp = jax.vjp(_s5_disc, lr_r[...], li_r[...], ldt_r[...], bre_r[...], bim_r[...])
        g = vjp((dar_r[...], dai_r[...], dbr_r[...], dbi_r[...]))
        for o, v in zip((o0, o1, o2, o3, o4), g):
            o[...] = v

    sd = jax.ShapeDtypeStruct
    return pl.pallas_call(
        body, name="s5_prep_bwd",
        out_shape=[sd(lr.shape, F32), sd(li.shape, F32), sd(ldt.shape, F32), sd(bre.shape, F32), sd(bim.shape, F32)],
        compiler_params=_cparams(),
    )(lr, li, ldt, bre, bim, dar, dai, dbr, dbi)


SCAN_T = 256


def _scan_tables(ar, ai, tab_r, tab_i, sub, reverse):
    pr, pi = ar, ai
    for k in range(sub):
        row = sub - 1 - k if reverse else k
        tab_r[row:row + 1, :] = pr
        tab_i[row:row + 1, :] = pi
        pr, pi = ar * pr - ai * pi, ar * pi + ai * pr


def _pack_matrix(t_blk, dtype):
    sub = t_blk // SUBLANES
    dst = jnp.arange(t_blk)
    src = (dst % SUBLANES) * sub + dst // SUBLANES
    return (src[:, None] == jnp.arange(t_blk)[None, :]).astype(dtype)


def _permute_rows_f32(pm, x):
    hi = x.astype(BF16)
    r1 = x - hi.astype(F32)
    mid = r1.astype(BF16)
    lo = (r1 - mid.astype(F32)).astype(BF16)
    dot = lambda v: jnp.dot(pm, v, preferred_element_type=F32)
    return dot(hi) + dot(mid) + dot(lo)


def _scan_block(x, loc, ar, ai, st, tab_r, tab_i, sub, reverse):
    hb = STATE_BLOCK
    a8r = jnp.broadcast_to(ar, (SUBLANES, hb))
    a8i = jnp.broadcast_to(ai, (SUBLANES, hb))
    sr = jnp.zeros((SUBLANES, hb), F32)
    si = jnp.zeros((SUBLANES, hb), F32)
    steps = range(sub - 1, -1, -1) if reverse else range(sub)
    for t in steps:
        rows = slice(t * SUBLANES, (t + 1) * SUBLANES)
        sr, si = a8r * sr - a8i * si + x[rows, :hb], a8r * si + a8i * sr + x[rows, hb:]
        loc[rows, :hb] = sr
        loc[rows, hb:] = si
    cr, ci = st[0:1, :], st[1:2, :]
    far = 0 if reverse else sub - 1
    fr, fi = tab_r[far:far + 1, :], tab_i[far:far + 1, :]
    ent_r, ent_i = [None] * SUBLANES, [None] * SUBLANES
    for c in (range(SUBLANES - 1, -1, -1) if reverse else range(SUBLANES)):
        ent_r[c], ent_i[c] = cr, ci
        cr, ci = sr[c:c + 1, :] + (fr * cr - fi * ci), si[c:c + 1, :] + (fr * ci + fi * cr)
    st[0:1, :] = cr
    st[1:2, :] = ci
    c8r = jnp.concatenate(ent_r, axis=0)
    c8i = jnp.concatenate(ent_i, axis=0)
    out = []
    for t in range(sub):
        rows = slice(t * SUBLANES, (t + 1) * SUBLANES)
        tr, ti = tab_r[t:t + 1, :], tab_i[t:t + 1, :]
        out.append(jnp.concatenate([loc[rows, :hb] + (tr * c8r - ti * c8i), loc[rows, hb:] + (tr * c8i + ti * c8r)],
                                   axis=1))
    return jnp.concatenate(out, axis=0)


SSM_BLOCKS_PER_STEP = 2


def _scan_scratch(nblk, t_blk, sub, hb):
    return [pltpu.VMEM((nblk, SUBLANES, hb), F32), pltpu.VMEM((nblk, sub, hb), F32), pltpu.VMEM((nblk, sub, hb), F32),
            pltpu.VMEM((nblk, t_blk, 2 * hb), F32)]


def _ssm_fwd(proj, wb, wc, a):
    seq = proj.shape[0]
    nj = wb.shape[0]
    w2 = 2 * STATE_BLOCK
    hb = STATE_BLOCK
    t_blk = min(SCAN_T, seq)
    sub = t_blk // SUBLANES
    pm = _pack_matrix(t_blk, BF16)

    npair = SSM_BLOCKS_PER_STEP

    def body(u_ref, wb_ref, wc_ref, a_ref, pm_ref, pmt_ref, s_ref, y_ref, st, tab_r, tab_i, loc):
        coef = [(a_ref[:, b * w2:b * w2 + hb], a_ref[:, b * w2 + hb:(b + 1) * w2]) for b in range(npair)]

        @pl.when(pl.program_id(1) == 0)
        def _():
            for b, (ar, ai) in enumerate(coef):
                st[b] = jnp.zeros((SUBLANES, hb), F32)
                _scan_tables(ar, ai, tab_r.at[b], tab_i.at[b], sub, False)

        for b, (ar, ai) in enumerate(coef):
            ub = u_ref[:, b * LANES:(b + 1) * LANES].astype(BF16)
            up = jnp.dot(pm_ref[...], ub, preferred_element_type=F32).astype(BF16)
            bu = jnp.dot(up, wb_ref[b], preferred_element_type=F32)
            s = _scan_block(bu, loc.at[b], ar, ai, st.at[b], tab_r.at[b], tab_i.at[b], sub, False)
            s_ref[:, b * w2:(b + 1) * w2] = s
            yp = jnp.dot(s.astype(BF16), wc_ref[b], preferred_element_type=F32)
            y_ref[:, b * LANES:(b + 1) * LANES] = _permute_rows_f32(pmt_ref[...], yp)

    sd = jax.ShapeDtypeStruct
    return pl.pallas_call(
        body, name="ssm_fwd", grid=(nj // npair, seq // t_blk),
        in_specs=[pl.BlockSpec((t_blk, npair * LANES), lambda j, i: (i, j)),
                  pl.BlockSpec((npair, LANES, w2), lambda j, i: (j, 0, 0)),
                  pl.BlockSpec((npair, w2, LANES), lambda j, i: (j, 0, 0)),
                  pl.BlockSpec((1, npair * w2), lambda j, i: (0, j)),
                  _full_spec((t_blk, t_blk)), _full_spec((t_blk, t_blk))],
        out_specs=[pl.BlockSpec((t_blk, npair * w2), lambda j, i: (i, j)),
                   pl.BlockSpec((t_blk, npair * LANES), lambda j, i: (i, j))],
        out_shape=[sd((seq, nj * w2), F32), sd((seq, nj * LANES), F32)],
        scratch_shapes=_scan_scratch(npair, t_blk, sub, hb), compiler_params=_cparams(),
    )(proj, wb, wc, a, pm, pm.T)


def _ssm_bwd(dy, s, proj, du1, wb, wc, a):
    seq = dy.shape[0]
    nj = wb.shape[0]
    w2 = 2 * STATE_BLOCK
    hb = STATE_BLOCK
    t_blk = min(SCAN_T, seq)
    sub = t_blk // SUBLANES
    nb = seq // t_blk
    pm = _pack_matrix(t_blk, BF16)

    npair = SSM_BLOCKS_PER_STEP

    def body(dy_ref, s_ref, sprev_ref, u_ref, du1_ref, wb_ref, wc_ref, a_ref, pm_ref, pmt_ref,
             du_ref, dwb_ref, dwc_ref, da_ref, st, tab_r, tab_i, loc):
        ib = pl.program_id(1)
        pmv = pm_ref[...]
        coef = [(a_ref[:, b * w2:b * w2 + hb], -a_ref[:, b * w2 + hb:(b + 1) * w2]) for b in range(npair)]

        @pl.when(ib == 0)
        def _():
            for b, (ar, ai) in enumerate(coef):
                st[b] = jnp.zeros((SUBLANES, hb), F32)
                _scan_tables(ar, ai, tab_r.at[b], tab_i.at[b], sub, True)

        sums = []
        for b, (ar, ai) in enumerate(coef):
            cols, wide = slice(b * LANES, (b + 1) * LANES), slice(b * w2, (b + 1) * w2)
            dyp = jnp.dot(pmv, dy_ref[:, cols], preferred_element_type=F32).astype(BF16)
            up = jnp.dot(pmv, u_ref[:, cols].astype(BF16), preferred_element_type=F32).astype(BF16)
            ds = lax.dot_general(dyp, wc_ref[b], (NT, ((), ())), preferred_element_type=F32)
            lam = _scan_block(ds, loc.at[b], ar, ai, st.at[b], tab_r.at[b], tab_i.at[b], sub, True)
            lamb = lam.astype(BF16)
            du = lax.dot_general(lamb, wb_ref[b], (NT, ((), ())), preferred_element_type=F32)
            du_ref[:, cols] = (_permute_rows_f32(pmt_ref[...], du) + du1_ref[:, cols]).astype(du_ref.dtype)
            sv = s_ref[:, wide]
            dwb = lax.dot_general(up, lamb, (TN, ((), ())), preferred_element_type=F32)
            dwc = lax.dot_general(sv.astype(BF16), dyp, (TN, ((), ())), preferred_element_type=F32)

            prev_last = sprev_ref[SUBLANES - 1:SUBLANES, wide]
            prev_last = jnp.where(ib == nb - 1, jnp.zeros_like(prev_last), prev_last)
            tail = sv[t_blk - SUBLANES:, :]
            sl = lax.broadcasted_iota(jnp.int32, tail.shape, 0)
            head = jnp.where(sl >= 1, pltpu.roll(tail, 1, 0), prev_last)
            s_sh = jnp.concatenate([head, sv[:t_blk - SUBLANES, :]], axis=0)
            lam_r, lam_i = lam[:, :hb], lam[:, hb:]
            sr_, si_ = s_sh[:, :hb], s_sh[:, hb:]
            dar = jnp.sum(lam_r * sr_ + lam_i * si_, axis=0, keepdims=True)
            dai = jnp.sum(lam_i * sr_ - lam_r * si_, axis=0, keepdims=True)
            sums.append((wide, jnp.concatenate([dar, dai], axis=1), dwb, dwc))

        @pl.when(ib == 0)
        def _():
            for b, (wide, contrib, dwb, dwc) in enumerate(sums):
                da_ref[:, wide] = contrib
                dwb_ref[b] = dwb
                dwc_ref[b] = dwc

        @pl.when(ib != 0)
        def _():
            for b, (wide, contrib, dwb, dwc) in enumerate(sums):
                da_ref[:, wide] += contrib
                dwb_ref[b] += dwb
                dwc_ref[b] += dwc

    blk = lambda j, i: (nb - 1 - i, j)
    prev_blk = lambda j, i: (jnp.maximum((nb - 1 - i) * sub - 1, 0), j)
    sd = jax.ShapeDtypeStruct
    return pl.pallas_call(
        body, name="ssm_bwd", grid=(nj // npair, nb),
        in_specs=[pl.BlockSpec((t_blk, npair * LANES), blk), pl.BlockSpec((t_blk, npair * w2), blk),
                  pl.BlockSpec((SUBLANES, npair * w2), prev_blk), pl.BlockSpec((t_blk, npair * LANES), blk),
                  pl.BlockSpec((t_blk, npair * LANES), blk),
                  pl.BlockSpec((npair, LANES, w2), lambda j, i: (j, 0, 0)),
                  pl.BlockSpec((npair, w2, LANES), lambda j, i: (j, 0, 0)),
                  pl.BlockSpec((1, npair * w2), lambda j, i: (0, j)),
                  _full_spec((t_blk, t_blk)), _full_spec((t_blk, t_blk))],
        out_specs=[pl.BlockSpec((t_blk, npair * LANES), blk),
                   pl.BlockSpec((npair, LANES, w2), lambda j, i: (j, 0, 0)),
                   pl.BlockSpec((npair, w2, LANES), lambda j, i: (j, 0, 0)),
                   pl.BlockSpec((1, npair * w2), lambda j, i: (0, j))],
        out_shape=[sd((seq, nj * LANES), BF16), sd((nj, LANES, w2), F32), sd((nj, w2, LANES), F32),
                   sd((1, nj * w2), F32)],
        scratch_shapes=_scan_scratch(npair, t_blk, sub, hb), compiler_params=_cparams(),
    )(dy, s, s, proj, du1, wb, wc, a, pm, pm.T)


def _rope128(x, cos, sa, sb):
    return x * cos + pltpu.roll(x, 96, 1) * sa + pltpu.roll(x, 32, 1) * sb


def _rope128_t(dy, cos, sa, sb):
    return dy * cos + pltpu.roll(dy * sa, 32, 1) + pltpu.roll(dy * sb, 96, 1)


ATT_BQ = 256


def _probs(qn, qp, kn, kp, r0, scale):
    s = lax.dot_general(qn, kn, (NT, ((), ())), preferred_element_type=F32)
    s = s + lax.dot_general(qp, kp, (NT, ((), ())), preferred_element_type=F32)
    s = s * scale
    diag = s[:, r0:]
    row = lax.broadcasted_iota(jnp.int32, diag.shape, 0)
    col = lax.broadcasted_iota(jnp.int32, diag.shape, 1)
    diag = jnp.where(col <= row, diag, jnp.finfo(F32).min)
    s = diag if r0 == 0 else jnp.concatenate([s[:, :r0], diag], axis=1)
    m = jnp.max(s, axis=-1, keepdims=True)
    e = jnp.exp(s - m)
    return e / jnp.sum(e, axis=-1, keepdims=True)


def _attn_specs(seq):
    tab = pl.BlockSpec((seq, LANES), lambda h: (0, 0))
    return [pl.BlockSpec((None, seq, 256), lambda h: (h, 0, 0)), pl.BlockSpec((None, seq, 128), lambda h: (h, 0, 0)),
            pl.BlockSpec((None, seq, 128), lambda h: (h, 0, 1)), tab, tab, tab, tab]


def _attn_fwd(q_raw, kv, kpe, cos, sa, sb):
    nh, seq, _ = q_raw.shape
    bq = min(ATT_BQ, seq)
    scale = (QK_NOPE + QK_ROPE) ** -0.5

    def body(q_ref, kn_ref, v_ref, kp_ref, cos_ref, sa_ref, sb_ref, o_ref):
        for r0 in range(0, seq, bq):
            rows, kend = pl.ds(r0, bq), r0 + bq
            qn = q_ref[rows, :QK_NOPE].astype(BF16)
            qp = _rope128(q_ref[rows, QK_NOPE:], cos_ref[rows, :], sa_ref[rows, :], sb_ref[rows, :]).astype(BF16)
            p = _probs(qn, qp, kn_ref[:kend, :], kp_ref[:kend, :], r0, scale)
            o_ref[rows, :] = jnp.dot(p.astype(BF16), v_ref[:kend, :], preferred_element_type=F32)

    return pl.pallas_call(
        body, name="attn_fwd", grid=(nh,), in_specs=_attn_specs(seq),
        out_specs=pl.BlockSpec((seq, V_DIM), lambda h: (0, h)),
        out_shape=jax.ShapeDtypeStruct((seq, nh * V_DIM), F32), compiler_params=_cparams(),
    )(q_raw, kv, kv, kpe, cos, sa, sb)


def _attn_bwd(q_raw, kv, kpe, cos, sa, sb, do):
    nh, seq, _ = q_raw.shape
    bq = min(ATT_BQ, seq)
    scale = (QK_NOPE + QK_ROPE) ** -0.5

    def body(q_ref, kn_ref, v_ref, kp_ref, cos_ref, sa_ref, sb_ref, do_ref, dq_ref, dkv_ref, dkp_ref):
        dkv_ref[...] = jnp.zeros_like(dkv_ref)
        dkp_ref[...] = jnp.zeros_like(dkp_ref)
        for r0 in range(0, seq, bq):
            rows, kend = pl.ds(r0, bq), r0 + bq
            cos_b, sa_b, sb_b = cos_ref[rows, :], sa_ref[rows, :], sb_ref[rows, :]
            qn = q_ref[rows, :QK_NOPE].astype(BF16)
            qp = _rope128(q_ref[rows, QK_NOPE:], cos_b, sa_b, sb_b).astype(BF16)
            kn, v, kp = kn_ref[:kend, :], v_ref[:kend, :], kp_ref[:kend, :]
            p = _probs(qn, qp, kn, kp, r0, scale)
            dob = do_ref[rows, :].astype(BF16)
            dp = lax.dot_general(dob, v, (NT, ((), ())), preferred_element_type=F32)
            ds = p * (dp - jnp.sum(p * dp, axis=-1, keepdims=True)) * scale
            dsb = ds.astype(BF16)
            pb = p.astype(BF16)
            dq_ref[rows, :QK_NOPE] = jnp.dot(dsb, kn, preferred_element_type=F32).astype(dq_ref.dtype)
            dqp = jnp.dot(dsb, kp, preferred_element_type=F32)
            dq_ref[rows, QK_NOPE:] = _rope128_t(dqp, cos_b, sa_b, sb_b).astype(dq_ref.dtype)
            dkv_ref[:kend, :QK_NOPE] += lax.dot_general(dsb, qn, (TN, ((), ())), preferred_element_type=F32)
            dkv_ref[:kend, QK_NOPE:] += lax.dot_general(pb, dob, (TN, ((), ())), preferred_element_type=F32)
            dkp_ref[:kend, :] += lax.dot_general(dsb, qp, (TN, ((), ())), preferred_element_type=F32)

    sd = jax.ShapeDtypeStruct
    return pl.pallas_call(
        body, name="attn_bwd", grid=(nh,),
        in_specs=_attn_specs(seq) + [pl.BlockSpec((seq, V_DIM), lambda h: (0, h))],
        out_specs=[pl.BlockSpec((None, seq, 256), lambda h: (h, 0, 0)),
                   pl.BlockSpec((None, seq, 256), lambda h: (h, 0, 0)),
                   pl.BlockSpec((None, seq, 128), lambda h: (h, 0, 0))],
        out_shape=[sd((nh, seq, 256), BF16), sd((nh, seq, 256), F32), sd((nh, seq, 128), F32)],
        compiler_params=_cparams(),
    )(q_raw, kv, kv, kpe, cos, sa, sb, do)


def _conv3(a, w, b):
    rows = lax.broadcasted_iota(jnp.int32, a.shape, 0)
    a1 = jnp.where(rows >= 1, pltpu.roll(a, 1, 0), 0.0)
    a2 = jnp.where(rows >= 2, pltpu.roll(a, 2, 0), 0.0)
    return w[2:3] * a + w[1:2] * a1 + w[0:1] * a2 + b, a1, a2


def _conv_gate_fwd(a, cw, cb):
    half, _, seq, c = a.shape
    nc = c // LANES

    def fn(pair, wg, wv, bg, bv):
        gc, _, _ = _conv3(pair[0], wg, bg)
        vc, _, _ = _conv3(pair[1], wv, bv)
        return gc * jax.nn.sigmoid(gc) * vc

    def w_spec(off, r):
        return pl.BlockSpec((None, r, LANES), lambda k, j: (k + off, 0, j))

    return _blockwise(
        "conv_gate_fwd", fn, [a, cw, cw, cb, cb],
        [pl.BlockSpec((None, 2, seq, LANES), lambda k, j: (k, 0, 0, j)),
         w_spec(0, 3), w_spec(half, 3), w_spec(0, 1), w_spec(half, 1)],
        [((seq, half * c), BF16)], [pl.BlockSpec((seq, LANES), lambda k, j: (0, k * nc + j))],
        grid=(half, nc))[0]


def _conv_gate_bwd(a, cw, cb, dm):
    half, _, seq, c = a.shape
    nc = c // LANES

    def body(a_ref, wg_ref, wv_ref, bg_ref, bv_ref, dm_ref, da_ref, dw_ref, db_ref):
        dmv = dm_ref[...]
        rows = lax.broadcasted_iota(jnp.int32, dmv.shape, 0)
        ga, wg = a_ref[0], wg_ref[...]
        va, wv = a_ref[1], wv_ref[...]
        gc, g1, g2 = _conv3(ga, wg, bg_ref[...])
        vc, v1, v2 = _conv3(va, wv, bv_ref[...])
        sg = jax.nn.sigmoid(gc)
        dms = dmv * sg
        d_val = dms * gc
        d_gate = dms * vc * (1.0 + gc * (1.0 - sg))

        def back(r, dc, own, a1, a2, w):
            up1 = jnp.where(rows < seq - 1, pltpu.roll(dc, seq - 1, 0), 0.0)
            up2 = jnp.where(rows < seq - 2, pltpu.roll(dc, seq - 2, 0), 0.0)
            da_ref[r] = (w[2:3] * dc + w[1:2] * up1 + w[0:1] * up2).astype(da_ref.dtype)
            dw_ref[r, 0:1, :] = jnp.sum(dc * a2, axis=0, keepdims=True)
            dw_ref[r, 1:2, :] = jnp.sum(dc * a1, axis=0, keepdims=True)
            dw_ref[r, 2:3, :] = jnp.sum(dc * own, axis=0, keepdims=True)
            db_ref[r] = jnp.sum(dc, axis=0, keepdims=True)

        back(0, d_gate, ga, g1, g2, wg)
        back(1, d_val, va, v1, v2, wv)

    def w_spec(off, r):
        return pl.BlockSpec((None, r, LANES), lambda k, j: (k + off, 0, j))

    def pair_spec(r):
        return pl.BlockSpec((None, 2, r, LANES), lambda k, j: (k, 0, 0, j))

    sd = jax.ShapeDtypeStruct
    return pl.pallas_call(
        body, name="conv_gate_bwd", grid=(half, nc),
        in_specs=[pair_spec(seq), w_spec(0, 3), w_spec(half, 3), w_spec(0, 1), w_spec(half, 1),
                  pl.BlockSpec((seq, LANES), lambda k, j: (0, k * nc + j))],
        out_specs=[pair_spec(seq), pair_spec(3), pair_spec(1)],
        out_shape=[sd((half, 2, seq, c), BF16), sd((half, 2, 3, c), F32), sd((half, 2, 1, c), F32)],
        compiler_params=_cparams(),
    )(a, cw, cw, cb, cb, dm)


ROW_T = 256


def _local_step(x, positions, target, w, emit=lambda **grads: None):
    seq, d = x.shape
    t_row = min(ROW_T, seq)
    nrow = seq // t_row
    ssm_w = d // 2
    nj = ssm_w // LANES
    n_groups = ssm_w // SSM_GROUP
    nh = w["wuq"].shape[0]
    q_rank = w["wuq"].shape[1]
    kv_rank = w["wukv"].shape[1]
    ns = w["wup"].shape[0]
    c_ff = w["wup"].shape[2]
    in_pad = w["win"].shape[1]
    tm = min(1024, seq)
    nm = seq // tm
    sw = 2 * STATE_BLOCK
    g1 = (nrow,)

    lr3 = w["lam_re"].reshape(n_groups, 1, SSM_STATE)
    li3 = w["lam_im"].reshape(n_groups, 1, SSM_STATE)
    ldt3 = w["log_dt"].reshape(n_groups, 1, 1)
    bt_re = jnp.swapaxes(w["b_re"].reshape(n_groups, SSM_STATE, SSM_GROUP), 1, 2)
    bt_im = jnp.swapaxes(w["b_im"].reshape(n_groups, SSM_STATE, SSM_GROUP), 1, 2)
    abar_re, abar_im, bbt_re, bbt_im = _s5_prep(lr3, li3, ldt3, bt_re, bt_im)
    eye = jnp.eye(GROUPS_PER_BLOCK, dtype=F32)

    def blockdiag_in(bb):
        t = bb.reshape(nj, GROUPS_PER_BLOCK, SSM_GROUP, SSM_STATE)
        return jnp.einsum("jghp,gk->jghkp", t, eye).reshape(nj, LANES, STATE_BLOCK)

    def blockdiag_in_t(dwb):
        t = dwb.reshape(nj, GROUPS_PER_BLOCK, SSM_GROUP, GROUPS_PER_BLOCK, SSM_STATE)
        return jnp.einsum("jghkp,gk->jghp", t, eye).reshape(n_groups, SSM_GROUP, SSM_STATE)

    def blockdiag_out(cc):
        t = cc.reshape(nj, GROUPS_PER_BLOCK, SSM_GROUP, SSM_STATE)
        return jnp.einsum("jghp,gk->jkpgh", t, eye).reshape(nj, STATE_BLOCK, LANES)

    def blockdiag_out_t(dwc):
        t = dwc.reshape(nj, GROUPS_PER_BLOCK, SSM_STATE, GROUPS_PER_BLOCK, SSM_GROUP)
        return jnp.einsum("jkpgh,gk->jghp", t, eye).reshape(n_groups, SSM_GROUP, SSM_STATE)

    c_re = w["c_re"].reshape(n_groups, SSM_GROUP, SSM_STATE)
    c_im = w["c_im"].reshape(n_groups, SSM_GROUP, SSM_STATE)
    wb = jnp.concatenate([blockdiag_in(bbt_re), blockdiag_in(bbt_im)], axis=2).astype(BF16)
    wc = jnp.concatenate([blockdiag_out(c_re), -blockdiag_out(c_im)], axis=1).astype(BF16)
    a_lay = jnp.concatenate([abar_re.reshape(nj, 1, STATE_BLOCK), abar_im.reshape(nj, 1, STATE_BLOCK)],
                            axis=1).reshape(1, nj * sw)

    attn_w = w["attn_norm"]
    hn = _blockwise("norm1", lambda xb, wv: _rms(xb, wv), [x, attn_w], [_row_spec(t_row, d), _full_spec((1, d))],
                    [((seq, d), BF16)], [_row_spec(t_row, d)], g1)[0]
    proj = _mm2d("proj", hn, w["win"], NN, F32, tn=640)

    s_all, ylin = _ssm_fwd(proj, wb, wc, a_lay)
    u_spec = pl.BlockSpec((t_row, ssm_w), lambda i: (i, 0))

    def ypre_fn(yl, ub, dsk):
        yp = yl + dsk * ub
        return yp, jax.nn.gelu(yp)

    y_pre, yg = _blockwise("ssm_gelu", ypre_fn, [ylin, proj, w["ssm_d"]],
                           [_row_spec(t_row, ssm_w), u_spec, _full_spec((1, ssm_w))],
                           [((seq, ssm_w), F32), ((seq, ssm_w), BF16)],
                           [_row_spec(t_row, ssm_w)] * 2, g1)
    z = _mm2d("ssm_glu", yg, w["wglu"], NN, F32, res=w["b_glu"])
    y_ssm = _blockwise("ssm_gate", lambda yp, zb: jax.nn.gelu(yp) * jax.nn.sigmoid(zb), [y_pre, z],
                       [_row_spec(t_row, ssm_w)] * 2, [((seq, ssm_w), F32)], [_row_spec(t_row, ssm_w)], g1)[0]

    cq_off, ckv_off, kpe_off = ssm_w, ssm_w + q_rank, ssm_w + q_rank + kv_rank
    c_q = proj[:, cq_off:ckv_off]
    c_kv = proj[:, ckv_off:kpe_off]
    kpe_raw = proj[:, kpe_off:kpe_off + LANES]
    pos_b = jnp.broadcast_to(positions.astype(F32)[:, None], (seq, LANES))
    inv_freq = ROPE_THETA ** (-jnp.arange(0, QK_ROPE, 2, dtype=F32) / QK_ROPE)
    inv128 = jnp.tile(inv_freq, 4).reshape(1, LANES)

    def mla_prep_fn(cq, ckv, kp, pb, inv, wq, wkv):
        ang = pb * inv
        lane = lax.broadcasted_iota(jnp.int32, ang.shape, 1)
        cs, sn = jnp.cos(ang), jnp.sin(ang)
        cos = jnp.where(lane < QK_ROPE, cs, 0.0)
        sa = jnp.where(lane < QK_ROPE // 2, -sn, 0.0)
        sb = jnp.where(jnp.logical_and(lane >= QK_ROPE // 2, lane < QK_ROPE), sn, 0.0)
        return _rms(cq, wq), _rms(ckv, wkv), _rope128(kp, cos, sa, sb), cos, sa, sb

    qn, kvn, kpe, cos_t, sa_t, sb_t = _blockwise(
        "mla_prep", mla_prep_fn, [c_q, c_kv, kpe_raw, pos_b, inv128, w["q_norm"], w["kv_norm"]],
        [_row_spec(t_row, q_rank), _row_spec(t_row, kv_rank), _row_spec(t_row, LANES), _row_spec(t_row, LANES),
         _full_spec((1, LANES)), _full_spec((1, q_rank)), _full_spec((1, kv_rank))],
        [((seq, q_rank), BF16), ((seq, kv_rank), BF16), ((seq, LANES), BF16)] + [((seq, LANES), F32)] * 3,
        [_row_spec(t_row, q_rank), _row_spec(t_row, kv_rank)] + [_row_spec(t_row, LANES)] * 4, g1)

    def head_mm(name, act, wh, out_dtype):
        kdim, ndim = wh.shape[1], wh.shape[2]
        return _mm(name, act, wh, grid=(nh, 1, 1),
                   a_spec=pl.BlockSpec((seq, kdim), lambda h, i, k: (i, 0)),
                   b_spec=pl.BlockSpec((None, kdim, ndim), lambda h, i, k: (h, 0, 0)),
                   o_spec=pl.BlockSpec((None, seq, ndim), lambda h, i, k: (h, i, 0)),
                   out_shape=(nh, seq, ndim), out_dtype=out_dtype)

    q_raw = head_mm("mla_q", qn, w["wuq"], F32)
    kv = head_mm("mla_kv", kvn, w["wukv"], BF16)
    y_mla = _attn_fwd(q_raw, kv, kpe, cos_t, sa_t, sb_t)
    mla_w = nh * V_DIM

    def outnorm_fn(ys, ym, ws, wm):
        return jnp.concatenate([_rms(ys, ws), _rms(ym, wm)], axis=1)

    ycat = _blockwise("out_norm", outnorm_fn, [y_ssm, y_mla, w["son"], w["mon"]],
                      [_row_spec(t_row, ssm_w), _row_spec(t_row, mla_w), _full_spec((1, ssm_w)), _full_spec((1, mla_w))],
                      [((seq, d), BF16)], [_row_spec(t_row, d)], g1)[0]
    h1 = _mm2d("out_proj", ycat, w["wout"], NN, F32, res=x)

    hn2 = _blockwise("norm2", lambda hb, wv: _rms(hb, wv), [h1, w["ffn_norm"]],
                     [_row_spec(t_row, d), _full_spec((1, d))], [((seq, d), BF16)], [_row_spec(t_row, d)], g1)[0]
    tku = d
    half = ns // 2
    a_ff = _mm("ffn_up", hn2, w["wup"], grid=(ns, nm, d // tku),
               a_spec=pl.BlockSpec((tm, tku), lambda s, i, k: (i, k)),
               b_spec=pl.BlockSpec((None, tku, c_ff), lambda s, i, k: (s, k, 0)),
               o_spec=pl.BlockSpec((None, None, tm, c_ff), lambda s, i, k: (s % half, s // half, i, 0)),
               out_shape=(half, 2, seq, c_ff), out_dtype=F32)
    cb3 = w["conv_b"].reshape(ns, 1, c_ff)
    m_ff = _conv_gate_fwd(a_ff, w["conv_w"], cb3)
    d_ff = half * c_ff
    wdn = w["wdown"]
    tnd = _tile(d, 1024)
    tmx, tnx = min(1024, seq), _tile(d, 1024)
    h2 = _mm2d("ffn_down", m_ff, wdn, NN, F32, tm=512, tn=512, tk=d_ff, res=h1)

    def loss_fn(hb, tb, wv):
        def f(hh, ww):
            err = _rms(hh, ww) - tb
            return 0.5 * jnp.sum(jnp.mean(err * err, axis=-1))

        lossv, (dh, dw) = jax.value_and_grad(f, argnums=(0, 1))(hb, wv)
        return dh, dh, jnp.full((1, LANES), lossv, F32), dw

    fin_w = w["final_norm"].reshape(1, d)
    dh2, dh2b, loss_acc, g_final = _blockwise(
        "loss_head", loss_fn, [h2, target, fin_w], [_row_spec(t_row, d), _row_spec(t_row, d), _full_spec((1, d))],
        [((seq, d), F32), ((seq, d), BF16), ((1, LANES), F32), ((1, d), F32)],
        [_row_spec(t_row, d), _row_spec(t_row, d), _full_spec((1, LANES)), _full_spec((1, d))], g1, n_acc=2)
    loss = loss_acc[0, 0]

    dm = _mm2d("ffn_down_dx", dh2b, wdn, NT, F32, tn=c_ff)
    tks = seq
    g_wdown = _mm2d("ffn_down_dw", m_ff, dh2b, TN, BF16, tm=c_ff)
    emit(wdown=g_wdown)
    da_ff, g_convw2, g_convb2 = _conv_gate_bwd(a_ff, w["conv_w"], cb3, dm)
    g_convw = jnp.swapaxes(g_convw2, 0, 1).reshape(ns, 3, c_ff)
    g_convb = jnp.swapaxes(g_convb2, 0, 1).reshape(ns, 1, c_ff)
    g_wup = _mm("ffn_up_dw", hn2, da_ff, grid=(ns, d // tnd, seq // tks), contract=TN,
                a_spec=pl.BlockSpec((tks, tnd), lambda s, j, k: (k, j)),
                b_spec=pl.BlockSpec((None, None, tks, c_ff), lambda s, j, k: (s % half, s // half, k, 0)),
                o_spec=pl.BlockSpec((None, tnd, c_ff), lambda s, j, k: (s, j, 0)),
                out_shape=(ns, d, c_ff), out_dtype=BF16)
    emit(wup=g_wup, conv_w=g_convw)
    dhn2 = _mm("ffn_up_dx", da_ff, w["wup"], grid=(seq // tmx, d // tnx, ns), contract=NT,
               a_spec=pl.BlockSpec((None, None, tmx, c_ff), lambda i, j, s: (s % half, s // half, i, 0)),
               b_spec=pl.BlockSpec((None, tnx, c_ff), lambda i, j, s: (s, j, 0)),
               o_spec=pl.BlockSpec((tmx, tnx), lambda i, j, s: (i, j)),
               out_shape=(seq, d), out_dtype=F32)

    def norm_bwd_fn(hb, dres, dn, wv):
        dx_, dw_ = _rms_bwd(hb, wv, dn)
        dtot = dres + dx_
        return dtot, dtot, dw_

    dh1, dh1b, g_ffn_norm = _blockwise(
        "norm2_bwd", norm_bwd_fn, [h1, dh2, dhn2, w["ffn_norm"]],
        [_row_spec(t_row, d)] * 3 + [_full_spec((1, d))],
        [((seq, d), F32), ((seq, d), BF16), ((1, d), F32)],
        [_row_spec(t_row, d), _row_spec(t_row, d), _full_spec((1, d))], g1, n_acc=1)

    g_wout = _mm2d("out_proj_dw", ycat, dh1b, TN, BF16)

    def outnorm_bwd_fn(dhb, wo, ys, ym, ws, wm):
        dyc = lax.dot_general(dhb, wo, (NT, ((), ())), preferred_element_type=F32)
        dys, dws = _rms_bwd(ys, ws, dyc[:, :ssm_w])
        dym, dwm = _rms_bwd(ym, wm, dyc[:, ssm_w:])
        return dys, dym, dws, dwm

    dy_ssm, dy_mla, g_son, g_mon = _blockwise(
        "out_proj_dx_norm_bwd", outnorm_bwd_fn, [dh1b, w["wout"], y_ssm, y_mla, w["son"], w["mon"]],
        [_row_spec(t_row, d), _full_spec((d, d)), _row_spec(t_row, ssm_w), _row_spec(t_row, mla_w),
         _full_spec((1, ssm_w)), _full_spec((1, mla_w))],
        [((seq, ssm_w), F32), ((seq, mla_w), F32), ((1, ssm_w), F32), ((1, mla_w), F32)],
        [_row_spec(t_row, ssm_w), _row_spec(t_row, mla_w), _full_spec((1, ssm_w)), _full_spec((1, mla_w))],
        g1, n_acc=2)

    def glu_bwd_fn(dy, yp, zb, ub, dsk, wg):
        ygv = jax.nn.gelu(yp)
        sg = jax.nn.sigmoid(zb)
        dz = dy * ygv * sg * (1.0 - sg)
        dzb = dz.astype(BF16)
        dyg = dy * sg + lax.dot_general(dzb, wg, (NT, ((), ())), preferred_element_type=F32)
        _, vjp = jax.vjp(jax.nn.gelu, yp)
        dyp = vjp(dyg)[0]
        return (dzb, dyp, dyp * dsk, jnp.sum(dz, axis=0, keepdims=True), jnp.sum(dyp * ub, axis=0, keepdims=True))

    dz, dy_pre, du1, g_bglu, g_ssmd = _blockwise(
        "ssm_glu_bwd", glu_bwd_fn, [dy_ssm, y_pre, z, proj, w["ssm_d"], w["wglu"]],
        [_row_spec(t_row, ssm_w)] * 3 + [u_spec, _full_spec((1, ssm_w)), _full_spec((ssm_w, ssm_w))],
        [((seq, ssm_w), BF16), ((seq, ssm_w), BF16), ((seq, ssm_w), F32), ((1, ssm_w), F32), ((1, ssm_w), F32)],
        [_row_spec(t_row, ssm_w)] * 3 + [_full_spec((1, ssm_w))] * 2, g1, n_acc=2)
    g_wglu = _mm2d("ssm_glu_dw", yg, dz, TN, BF16)
    dq_raw, dkv, dkp_h = _attn_bwd(q_raw, kv, kpe, cos_t, sa_t, sb_t, dy_mla)

    def head_mm_dx(name, dact, wh):
        kdim, ndim = wh.shape[1], wh.shape[2]
        return _mm(name, dact, wh, grid=(1, 1, nh), contract=NT,
                   a_spec=pl.BlockSpec((None, seq, ndim), lambda i, j, h: (h, i, 0)),
                   b_spec=pl.BlockSpec((None, kdim, ndim), lambda i, j, h: (h, 0, 0)),
                   o_spec=pl.BlockSpec((seq, kdim), lambda i, j, h: (i, 0)),
                   out_shape=(seq, kdim), out_dtype=F32)

    def head_mm_dw(name, act, dact):
        kdim, ndim = act.shape[1], dact.shape[2]
        return _mm(name, act, dact, grid=(nh, 1, seq // tks), contract=TN,
                   a_spec=pl.BlockSpec((tks, kdim), lambda h, j, k: (k, 0)),
                   b_spec=pl.BlockSpec((None, tks, ndim), lambda h, j, k: (h, k, 0)),
                   o_spec=pl.BlockSpec((None, kdim, ndim), lambda h, j, k: (h, 0, 0)),
                   out_shape=(nh, kdim, ndim), out_dtype=BF16)

    g_wuq = head_mm_dw("mla_q_dw", qn, dq_raw)
    g_wukv = head_mm_dw("mla_kv_dw", kvn, dkv)
    dqn = head_mm_dx("mla_q_dx", dq_raw, w["wuq"])
    dkvn = head_mm_dx("mla_kv_dx", dkv, w["wukv"])
    emit(not_before=(dqn, dkvn, dy_pre), wout=g_wout, wuq=g_wuq, wukv=g_wukv, wglu=g_wglu)

    du, dwb, dwc, da_lay = _ssm_bwd(dy_pre, s_all, proj, du1, wb, wc, a_lay)
    g_c_re = blockdiag_out_t(dwc[:, :STATE_BLOCK, :])
    g_c_im = -blockdiag_out_t(dwc[:, STATE_BLOCK:, :])
    dbbt_re = blockdiag_in_t(dwb[:, :, :STATE_BLOCK])
    dbbt_im = blockdiag_in_t(dwb[:, :, STATE_BLOCK:])
    da3 = da_lay.reshape(nj, 2, STATE_BLOCK)
    dabar_re = da3[:, 0, :].reshape(n_groups, 1, SSM_STATE)
    dabar_im = da3[:, 1, :].reshape(n_groups, 1, SSM_STATE)
    g_lr3, g_li3, g_ldt3, g_bt_re, g_bt_im = _s5_prep_bwd(lr3, li3, ldt3, bt_re, bt_im,
                                                           dabar_re, dabar_im, dbbt_re, dbbt_im)

    def mla_prep_bwd_fn(cq, ckv, dqn_b, dkvn_b, dkp_b, cos, sa, sb, wq, wkv):
        dcq, dwq = _rms_bwd(cq, wq, dqn_b)
        dckv, dwkv = _rms_bwd(ckv, wkv, dkvn_b)
        dkp_sum = dkp_b[0]
        for h in range(1, nh):
            dkp_sum = dkp_sum + dkp_b[h]
        return dcq, dckv, _rope128_t(dkp_sum, cos, sa, sb), dwq, dwkv

    dc_q, dc_kv, dkpe_raw, g_qnorm, g_kvnorm = _blockwise(
        "mla_prep_bwd", mla_prep_bwd_fn, [c_q, c_kv, dqn, dkvn, dkp_h, cos_t, sa_t, sb_t, w["q_norm"], w["kv_norm"]],
        [_row_spec(t_row, q_rank), _row_spec(t_row, kv_rank), _row_spec(t_row, q_rank), _row_spec(t_row, kv_rank),
         pl.BlockSpec((nh, t_row, LANES), lambda i: (0, i, 0))] + [_row_spec(t_row, LANES)] * 3
        + [_full_spec((1, q_rank)), _full_spec((1, kv_rank))],
        [((seq, q_rank), BF16), ((seq, kv_rank), BF16), ((seq, LANES), BF16), ((1, q_rank), F32), ((1, kv_rank), F32)],
        [_row_spec(t_row, q_rank), _row_spec(t_row, kv_rank), _row_spec(t_row, LANES), _full_spec((1, q_rank)),
         _full_spec((1, kv_rank))], g1, n_acc=2)

    dproj = jnp.concatenate([du, dc_q, dc_kv, dkpe_raw], axis=1)
    g_win = _mm2d("proj_dw", hn, dproj, TN, BF16, tn=640)
    emit(win=g_win)
    def norm1_bwd_fn(dpb, wi, xb, dres, wv):
        dn = lax.dot_general(dpb, wi, (NT, ((), ())), preferred_element_type=F32)
        dx_, dw_ = _rms_bwd(xb, wv, dn)
        return dres + dx_, dw_

    grad_x, g_attn_norm = _blockwise(
        "proj_dx_norm1_bwd", norm1_bwd_fn, [dproj, w["win"], x, dh1, attn_w],
        [_row_spec(t_row, in_pad), _full_spec((d, in_pad)), _row_spec(t_row, d), _row_spec(t_row, d), _full_spec((1, d))],
        [((seq, d), F32), ((1, d), F32)], [_row_spec(t_row, d), _full_spec((1, d))], g1, n_acc=1)

    grads = dict(
        attn_norm=g_attn_norm, win=g_win, lam_re=g_lr3, lam_im=g_li3, log_dt=g_ldt3,
        bt_re=g_bt_re, bt_im=g_bt_im, c_re=g_c_re, c_im=g_c_im,
        ssm_d=g_ssmd, wglu=g_wglu, b_glu=g_bglu, q_norm=g_qnorm, wuq=g_wuq, kv_norm=g_kvnorm, wukv=g_wukv,
        son=g_son, mon=g_mon, wout=g_wout, ffn_norm=g_ffn_norm, wup=g_wup, conv_w=g_convw, conv_b=g_convb,
        wdown=g_wdown, final_norm=g_final)
    return loss, grad_x, grads


def _mesh_pos():
    return lax.axis_index("x"), lax.axis_index("y"), lax.axis_index("c")


def _handshake_all():
    x, y, c = _mesh_pos()
    barrier = pltpu.get_barrier_semaphore()
    for k in range(1, N_DEV):
        peer = (1 - x if k & 4 else x, 1 - y if k & 2 else y, 1 - c if k & 1 else c)
        pl.semaphore_signal(barrier, inc=1, device_id=peer, device_id_type=MESH)
    pl.semaphore_wait(barrier, N_DEV - 1)


def _comm_call(name, body, n, out_shape, ins, collective_id, after=None):
    sems = [pltpu.SemaphoreType.DMA((7 * n,)), pltpu.SemaphoreType.DMA((7 * n,)), pltpu.SemaphoreType.DMA((n,))]
    if collective_id is None:
        any_spec = pl.BlockSpec(memory_space=pl.ANY)
        return pl.pallas_call(body, name=name, out_shape=out_shape, in_specs=[any_spec] * n,
                              out_specs=[any_spec] * n, scratch_shapes=sems)(*ins)
    seq_body = body
    if after:
        n_after = len(after)
        ins = list(ins) + list(after)

        def seq_body(*refs):
            body(*refs[:n], *refs[n + n_after:])

    return pl.kernel(seq_body, name=name, out_type=out_shape,
                     mesh=plsc.ScalarSubcoreMesh(axis_name="seq", num_cores=1), scratch_types=sems,
                     compiler_params=pltpu.CompilerParams(collective_id=collective_id))(*ins)


def _all_gather(name, xs, collective_id=None, after=None):
    n = len(xs)

    def body(*refs):
        x_refs, o_refs = refs[:n], refs[n:2 * n]
        send_sems, recv_sems, local_sems = refs[2 * n:]
        if collective_id is not None:
            _handshake_all()
        x, y, c = _mesh_pos()
        me, sibling = (x, y, c), (x, y, 1 - c)
        chips = [(1 - x, y), (x, 1 - y), (1 - x, 1 - y)]

        def slot(o_ref, px, py, pc):
            return o_ref.at[4 * px + 2 * py + pc]

        def copy(t, k, block, to, src=None):
            dst = slot(o_refs[t], *block)
            return pltpu.make_async_remote_copy(
                src_ref=dst if src is None else src, dst_ref=dst,
                send_sem=send_sems.at[7 * t + k], recv_sem=recv_sems.at[7 * t + k],
                device_id=to, device_id_type=MESH)

        started = []
        for t in range(n):
            mine = pltpu.make_async_copy(x_refs[t], slot(o_refs[t], *me), local_sems.at[t])
            mine.start()
            started.append(mine)
        first = []
        for t in range(n):
            first.append(copy(t, 0, me, sibling, src=x_refs[t]))
            first += [copy(t, 1 + j, me, (*chip, c), src=x_refs[t]) for j, chip in enumerate(chips)]
        for cp in first:
            cp.start()
        passed = []
        for j, chip in enumerate(chips):
            for t in range(n):
                copy(t, 1 + j, (*chip, c), me).wait_recv()
                fwd = copy(t, 4 + j, (*chip, c), sibling)
                fwd.start()
                passed.append(fwd)
        for t in range(n):
            copy(t, 0, sibling, me).wait_recv()
            for j, chip in enumerate(chips):
                copy(t, 4 + j, (*chip, 1 - c), me).wait_recv()
        for cp in first + passed:
            cp.wait_send()
        for mine in started:
            mine.wait()

    out_shape = [jax.ShapeDtypeStruct((N_DEV,) + v.shape, v.dtype) for v in xs]
    return _comm_call(name, body, n, out_shape, xs, collective_id, after)


def _exchange_partials(name, gs, collective_id=None, after=None):
    n = len(gs)

    def body(*refs):
        g_refs, o_refs = refs[:n], refs[n:2 * n]
        send_sems, recv_sems, local_sems = refs[2 * n:]
        if collective_id is not None:
            _handshake_all()
        x, y, c = _mesh_pos()
        me_idx = 4 * x + 2 * y + c
        copies = []
        for t in range(n):
            mine = pltpu.make_async_copy(g_refs[t].at[me_idx], o_refs[t].at[me_idx], local_sems.at[t])
            mine.start()
            copies.append(mine)
        remote = []
        for k in range(1, N_DEV):
            px = 1 - x if k & 4 else x
            py = 1 - y if k & 2 else y
            pc = 1 - c if k & 1 else c
            p_idx = 4 * px + 2 * py + pc
            for t in range(n):
                cp = pltpu.make_async_remote_copy(
                    src_ref=g_refs[t].at[p_idx], dst_ref=o_refs[t].at[me_idx],
                    send_sem=send_sems.at[7 * t + k - 1], recv_sem=recv_sems.at[7 * t + k - 1],
                    device_id=(px, py, pc), device_id_type=MESH)
                cp.start()
                landing = pltpu.make_async_remote_copy(
                    src_ref=g_refs[t].at[p_idx], dst_ref=o_refs[t].at[p_idx],
                    send_sem=send_sems.at[7 * t + k - 1], recv_sem=recv_sems.at[7 * t + k - 1],
                    device_id=(px, py, pc), device_id_type=MESH)
                remote.append((cp, landing))
        for cp, landing in remote:
            landing.wait_recv()
        for cp, landing in remote:
            cp.wait_send()
        for mine in copies:
            mine.wait()

    out_shape = [jax.ShapeDtypeStruct(v.shape, v.dtype) for v in gs]
    return _comm_call(name, body, n, out_shape, gs, collective_id, after)


ADAM_BLOCK_ELEMS = 128 * 1024


def _sum_parts(pb):
    g = pb[0].astype(F32)
    for j in range(1, pb.shape[0]):
        g = g + pb[j].astype(F32)
    return g


def _adam_math(g, wb_, mb, vb):
    m_new = ADAM_B1 * mb + (1.0 - ADAM_B1) * g
    v_new = ADAM_B2 * vb + (1.0 - ADAM_B2) * (g * g)
    m_hat = m_new / (1.0 - ADAM_B1 ** ADAM_STEP)
    v_hat = v_new / (1.0 - ADAM_B2 ** ADAM_STEP)
    delta = -ADAM_LR * (m_hat / (jnp.sqrt(v_hat) + ADAM_EPS) + ADAM_WD * wb_)
    return g, delta, m_new, v_new


def _adamw_multi(name, items, nblk=1):
    n = len(items)

    def body(*refs):
        for t in range(n):
            pr, wr, mr, vr = refs[4 * t:4 * t + 4]
            res = _adam_math(_sum_parts(pr[...]), wr[...], mr[...], vr[...])
            for o, val in zip(refs[4 * n + 4 * t:4 * n + 4 * t + 4], res):
                o[...] = val

    def spec(shape, lead):
        blk = list(shape)
        blk[lead + 1] = shape[lead + 1] // nblk
        if nblk == 1:
            return pl.BlockSpec(tuple(blk), lambda i, nd=len(shape): (0,) * nd)
        return pl.BlockSpec(tuple(blk), lambda i, nd=len(shape), ax=lead + 1: (0,) * ax + (i,) + (0,) * (nd - ax - 1))

    in_specs, out_specs, out_shape, ins = [], [], [], []
    for parts, wv, mv, vv in items:
        assert parts.shape[1:] == wv.shape, (name, parts.shape, wv.shape)
        ins += [parts, wv, mv, vv]
        in_specs += [spec(parts.shape, 1)] + [spec(wv.shape, 0)] * 3
        out_specs += [spec(wv.shape, 0)] * 4
        out_shape += [jax.ShapeDtypeStruct(wv.shape, F32)] * 4
    res = pl.pallas_call(body, name=name, grid=(nblk,), in_specs=in_specs, out_specs=out_specs, out_shape=out_shape,
                         compiler_params=_cparams())(*ins)
    return [tuple(res[4 * t:4 * t + 4]) for t in range(n)]


def _sum_multi(name, parts_list):
    def body(*refs):
        for pr, o in zip(refs[:len(parts_list)], refs[len(parts_list):]):
            o[...] = _sum_parts(pr[...])

    return pl.pallas_call(body, name=name, out_shape=[jax.ShapeDtypeStruct(p.shape[1:], F32) for p in parts_list],
                          compiler_params=_cparams())(*parts_list)


def _adamw_sum(name, parts, wv, mv, vv):
    npart, r, c = parts.shape
    tr = r
    if r * c > ADAM_BLOCK_ELEMS and r % SUBLANES == 0:
        tr = SUBLANES
        while r % (tr * 2) == 0 and tr * 2 * c <= ADAM_BLOCK_ELEMS:
            tr *= 2

    def fn(pb, wb_, mb, vb):
        return _adam_math(_sum_parts(pb), wb_, mb, vb)

    row = pl.BlockSpec((tr, c), lambda i: (i, 0))
    return _blockwise(name, fn, [parts, wv, mv, vv],
                      [pl.BlockSpec((npart, tr, c), lambda i: (0, i, 0)), row, row, row],
                      [((r, c), F32)] * 4, [row] * 4, (r // tr,))


_VECTORS = ["attn_norm", "lam_re", "lam_im", "log_dt", "ssm_d", "b_glu", "q_norm", "kv_norm", "son", "mon",
            "ffn_norm", "conv_b", "final_norm"]
_BIG = ["win", "wglu", "wuq", "wukv", "wout", "wup", "wdown", "conv_w"]
_ORDER = ["attn_norm", "win", "lam_re", "lam_im", "log_dt", "b_re", "b_im", "c_re", "c_im", "ssm_d", "wglu",
          "b_glu", "q_norm", "wuq", "kv_norm", "wukv", "son", "mon", "wout", "ffn_norm", "wup", "conv_w",
          "conv_b", "wdown", "final_norm"]


def kernel(x, positions, attn_norm_w, w_in, ssm_lambda_re, ssm_lambda_im, ssm_log_dt, ssm_b_re, ssm_b_im, ssm_c_re, ssm_c_im, ssm_d, ssm_w_glu, ssm_b_glu, mla_q_norm_w, mla_w_uq, mla_kv_norm_w, mla_w_ukv, ssm_out_norm_w, mla_out_norm_w, w_out, ffn_norm_w, ffn_w_up, ffn_conv_w, ffn_conv_b, ffn_w_down, final_norm_w, loss_target, m_attn_norm_w, m_w_in, m_ssm_lambda_re, m_ssm_lambda_im, m_ssm_log_dt, m_ssm_b_re, m_ssm_b_im, m_ssm_c_re, m_ssm_c_im, m_ssm_d, m_ssm_w_glu, m_ssm_b_glu, m_mla_q_norm_w, m_mla_w_uq, m_mla_kv_norm_w, m_mla_w_ukv, m_ssm_out_norm_w, m_mla_out_norm_w, m_w_out, m_ffn_norm_w, m_ffn_w_up, m_ffn_conv_w, m_ffn_conv_b, m_ffn_w_down, m_final_norm_w, v_attn_norm_w, v_w_in, v_ssm_lambda_re, v_ssm_lambda_im, v_ssm_log_dt, v_ssm_b_re, v_ssm_b_im, v_ssm_c_re, v_ssm_c_im, v_ssm_d, v_ssm_w_glu, v_ssm_b_glu, v_mla_q_norm_w, v_mla_w_uq, v_mla_kv_norm_w, v_mla_w_ukv, v_ssm_out_norm_w, v_mla_out_norm_w, v_w_out, v_ffn_norm_w, v_ffn_w_up, v_ffn_conv_w, v_ffn_conv_b, v_ffn_w_down, v_final_norm_w):
    wts = dict(attn_norm=attn_norm_w, win=w_in, lam_re=ssm_lambda_re, lam_im=ssm_lambda_im, log_dt=ssm_log_dt,
               b_re=ssm_b_re, b_im=ssm_b_im, c_re=ssm_c_re, c_im=ssm_c_im, ssm_d=ssm_d, wglu=ssm_w_glu,
               b_glu=ssm_b_glu, q_norm=mla_q_norm_w, wuq=mla_w_uq, kv_norm=mla_kv_norm_w, wukv=mla_w_ukv,
               son=ssm_out_norm_w, mon=mla_out_norm_w, wout=w_out, ffn_norm=ffn_norm_w, wup=ffn_w_up,
               conv_w=ffn_conv_w, conv_b=ffn_conv_b, wdown=ffn_w_down, final_norm=final_norm_w)
    moms = dict(zip(_ORDER, [m_attn_norm_w, m_w_in, m_ssm_lambda_re, m_ssm_lambda_im, m_ssm_log_dt, m_ssm_b_re,
                             m_ssm_b_im, m_ssm_c_re, m_ssm_c_im, m_ssm_d, m_ssm_w_glu, m_ssm_b_glu, m_mla_q_norm_w,
                             m_mla_w_uq, m_mla_kv_norm_w, m_mla_w_ukv, m_ssm_out_norm_w, m_mla_out_norm_w, m_w_out,
                             m_ffn_norm_w, m_ffn_w_up, m_ffn_conv_w, m_ffn_conv_b, m_ffn_w_down, m_final_norm_w]))
    vels = dict(zip(_ORDER, [v_attn_norm_w, v_w_in, v_ssm_lambda_re, v_ssm_lambda_im, v_ssm_log_dt, v_ssm_b_re,
                             v_ssm_b_im, v_ssm_c_re, v_ssm_c_im, v_ssm_d, v_ssm_w_glu, v_ssm_b_glu, v_mla_q_norm_w,
                             v_mla_w_uq, v_mla_kv_norm_w, v_mla_w_ukv, v_ssm_out_norm_w, v_mla_out_norm_w, v_w_out,
                             v_ffn_norm_w, v_ffn_w_up, v_ffn_conv_w, v_ffn_conv_b, v_ffn_w_down, v_final_norm_w]))
    seq, d = x.shape[1], x.shape[2]
    in_width = w_in.shape[2]
    in_pad = -(-in_width // LANES) * LANES
    q_cols = mla_w_uq.shape[2]
    q_pad = 2 * LANES

    (win_g,) = _all_gather("gather_w_in", [jnp.pad(w_in[0], ((0, 0), (0, in_pad - in_width))).astype(BF16)])
    wglu_g, wuq_g, wukv_g, wout_g, convw_g = _all_gather(
        "gather_mix", [ssm_w_glu[0].astype(BF16), jnp.pad(mla_w_uq[0], ((0, 0), (0, q_pad - q_cols))).astype(BF16),
                       mla_w_ukv[0].astype(BF16), w_out[0].astype(BF16), ffn_conv_w[0]], collective_id=0)
    (wup_g,) = _all_gather("gather_ffn_up", [ffn_w_up[0].astype(BF16)], collective_id=1)
    (wdown_g,) = _all_gather("gather_ffn_down", [ffn_w_down[0].astype(BF16)], collective_id=2)
    ns = N_DEV
    c_ff = wup_g.shape[2]
    w = dict(
        attn_norm=attn_norm_w, win=win_g.reshape(d, in_pad), lam_re=ssm_lambda_re, lam_im=ssm_lambda_im,
        log_dt=ssm_log_dt, b_re=ssm_b_re, b_im=ssm_b_im, c_re=ssm_c_re, c_im=ssm_c_im, ssm_d=ssm_d,
        wglu=wglu_g.reshape(d // 2, d // 2), b_glu=ssm_b_glu, q_norm=mla_q_norm_w, wuq=wuq_g,
        kv_norm=mla_kv_norm_w, wukv=wukv_g, son=ssm_out_norm_w, mon=mla_out_norm_w, wout=wout_g.reshape(d, d),
        ffn_norm=ffn_norm_w, wup=wup_g, conv_w=convw_g, conv_b=ffn_conv_b,
        wdown=wdown_g.reshape(ns // 2 * c_ff, d), final_norm=final_norm_w)

    shard_layout = dict(
        win=lambda a: a[:, :in_width].reshape(N_DEV, d // N_DEV, in_width),
        wglu=lambda a: a.reshape(N_DEV, d // 2 // N_DEV, d // 2),
        wuq=lambda a: a[:, :, :q_cols], wukv=lambda a: a, wout=lambda a: a.reshape(N_DEV, d // N_DEV, d),
        wup=lambda a: a, wdown=lambda a: a.reshape(N_DEV, c_ff // 2, d), conv_w=lambda a: a)
    recv = {}
    next_id = [3]

    last = [None]

    out = {}

    def update(k):
        shp = wts[k].shape
        r, c = shp[-2], shp[-1]
        res = _adamw_sum("adamw_" + k, recv[k].reshape(N_DEV, r, c), wts[k].reshape(r, c),
                         moms[k].reshape(r, c), vels[k].reshape(r, c))
        out[k] = [a.reshape(shp) for a in res]
        return res[0]

    def exchange(not_before=(), **grads):
        names = list(grads)
        if "wout" in names:
            not_before = (*not_before, update("wdown"))
        got = _exchange_partials("exchange_" + "_".join(names), [shard_layout[k](grads[k]) for k in names],
                                 collective_id=next_id[0], after=[a for a in (last[0], *not_before) if a is not None])
        next_id[0] += 1
        last[0] = got[-1]
        recv.update(zip(names, got))

    loss_part, grad_x, g = _local_step(x[0], positions[0], loss_target[0], w, emit=exchange)
    loss = lax.psum(loss_part, ("x", "y", "c"))
    n_groups = ssm_lambda_re.shape[1]
    two_d = {"lam_re": (n_groups, -1), "lam_im": (n_groups, -1)}
    dense = {k: g[k].reshape(two_d.get(k, (1, -1))) for k in _VECTORS}
    dense.update(c_re=g["c_re"], c_im=g["c_im"], bt_re=g["bt_re"], bt_im=g["bt_im"])
    names = list(dense)
    gathered = dict(zip(names, _all_gather("gather_small_grads", [dense[k] for k in names],
                                           collective_id=next_id[0], after=[last[0]])))
    for k in _BIG:
        if k not in out:
            update(k)

    def finish(keys, results):
        for k, res in zip(keys, results):
            out[k] = [a.reshape(wts[k].shape) for a in res]

    view = lambda k, a: a.reshape(dense[k].shape)
    finish(_VECTORS, _adamw_multi("adamw_vectors", [(gathered[k], view(k, wts[k]), view(k, moms[k]), view(k, vels[k]))
                                                    for k in _VECTORS]))
    c_keys = ["c_re", "c_im"]
    finish(c_keys, _adamw_multi("adamw_ssm_c", [(gathered[k][:, None], wts[k], moms[k], vels[k]) for k in c_keys]))
    b_sums = _sum_multi("sum_ssm_b", [gathered["bt_re"], gathered["bt_im"]])
    b_keys = ["b_re", "b_im"]
    finish(b_keys, _adamw_multi("adamw_ssm_b", [(jnp.swapaxes(s, 1, 2)[None, None], wts[k], moms[k], vels[k])
                                                for k, s in zip(b_keys, b_sums)], nblk=SUBLANES))

    grad_x = grad_x.reshape(x.shape)
    return (loss, grad_x, *[out[k][0] for k in _ORDER], *[out[k][1] for k in _ORDER],
            *[out[k][2] for k in _ORDER], *[out[k][3] for k in _ORDER])
```

```python
import functools
import math

import jax
import jax.numpy as jnp
from jax import lax
from jax.experimental import pallas as pl
from jax.experimental.pallas import tpu as pltpu
from jax.experimental.pallas import tpu_sc as plsc

F32 = jnp.float32
BF16 = jnp.bfloat16
MESH = pl.DeviceIdType.MESH

N_DEV = 8
LANES = 128
SUBLANES = 8
VMEM_LIMIT = 48 * 1024 * 1024

SSM_GROUP = 16
SSM_STATE = 64
GROUPS_PER_BLOCK = LANES // SSM_GROUP
STATE_BLOCK = GROUPS_PER_BLOCK * SSM_STATE
QK_NOPE = 128
QK_ROPE = 64
V_DIM = 128
ROPE_THETA = 10000.0
RMS_EPS = 1e-6

ADAM_LR = 0.001
ADAM_B1 = 0.9
ADAM_B2 = 0.999
ADAM_EPS = 1e-08
ADAM_WD = 0.01
ADAM_STEP = 10

NN = ((1,), (0,))
NT = ((1,), (1,))
TN = ((0,), (0,))


def _cparams():
    return pltpu.CompilerParams(vmem_limit_bytes=VMEM_LIMIT)


def _tile(n, want):
    if n <= want:
        return n
    t = (want // LANES) * LANES
    while t >= LANES:
        if n % t == 0:
            return t
        t -= LANES
    return n


def _mm(name, a, b, *, grid, a_spec, b_spec, o_spec, out_shape, out_dtype, contract=NN,
        res=None, res_spec=None):
    nk = grid[-1]
    kaxis = len(grid) - 1
    acc_shape = tuple(d for d in o_spec.block_shape if d is not None)

    def body(*refs):
        a_ref, b_ref = refs[:2]
        r_ref = None if res is None else refs[2]
        o_ref = refs[2 if res is None else 3]
        part = lax.dot_general(a_ref[...].astype(BF16), b_ref[...].astype(BF16),
                               (contract, ((), ())), preferred_element_type=F32)
        if nk == 1:
            if r_ref is not None:
                part = part + r_ref[...].astype(F32)
            o_ref[...] = part.astype(o_ref.dtype)
            return
        acc = refs[-1]
        k = pl.program_id(kaxis)

        @pl.when(k == 0)
        def _():
            acc[...] = part

        @pl.when(k != 0)
        def _():
            acc[...] += part

        @pl.when(k == nk - 1)
        def _():
            r = acc[...]
            if r_ref is not None:
                r = r + r_ref[...].astype(F32)
            o_ref[...] = r.astype(o_ref.dtype)

    ins = [a, b] + ([] if res is None else [res])
    in_specs = [a_spec, b_spec] + ([] if res is None else [res_spec])
    return pl.pallas_call(
        body, name=name, grid=grid, in_specs=in_specs, out_specs=o_spec,
        out_shape=jax.ShapeDtypeStruct(out_shape, out_dtype),
        scratch_shapes=[pltpu.VMEM(acc_shape, F32)] if nk > 1 else [], compiler_params=_cparams(),
    )(*ins)


def _mm2d(name, a, b, contract, out_dtype, tm=1024, tn=1024, tk=2048, res=None):
    if contract == NN:
        (m, kk), n = a.shape, b.shape[1]
    elif contract == NT:
        (m, kk), n = a.shape, b.shape[0]
    else:
        (kk, m), n = a.shape, b.shape[1]
    tm, tn, tk = _tile(m, tm), _tile(n, tn), _tile(kk, tk)
    grid = (m // tm, n // tn, kk // tk)
    if contract == TN:
        a_spec = pl.BlockSpec((tk, tm), lambda i, j, k: (k, i))
    else:
        a_spec = pl.BlockSpec((tm, tk), lambda i, j, k: (i, k))
    if contract == NT:
        b_spec = pl.BlockSpec((tn, tk), lambda i, j, k: (j, k))
    else:
        b_spec = pl.BlockSpec((tk, tn), lambda i, j, k: (k, j))
    o_spec = pl.BlockSpec((tm, tn), lambda i, j, k: (i, j))
    res_spec = None
    if res is not None:
        if res.shape[0] == 1:
            res_spec = pl.BlockSpec((1, tn), lambda i, j, k: (0, j))
        else:
            res_spec = pl.BlockSpec((tm, tn), lambda i, j, k: (i, j))
    return _mm(name, a, b, grid=grid, a_spec=a_spec, b_spec=b_spec, o_spec=o_spec,
               out_shape=(m, n), out_dtype=out_dtype, contract=contract, res=res, res_spec=res_spec)


def _blockwise(name, fn, ins, in_specs, outs, out_specs, grid, n_acc=0, acc_all=True):
    n_in, n_out = len(ins), len(outs)
    n_plain = n_out - n_acc

    def body(*refs):
        vals = fn(*[r[...] for r in refs[:n_in]])
        if not isinstance(vals, (tuple, list)):
            vals = (vals,)
        o_refs = refs[n_in:n_in + n_out]
        for r, v in zip(o_refs[:n_plain], vals[:n_plain]):
            r[...] = v.astype(r.dtype)
        if n_acc:
            if acc_all:
                first = functools.reduce(jnp.logical_and, [pl.program_id(d) == 0 for d in range(len(grid))])
            else:
                first = pl.program_id(len(grid) - 1) == 0

            @pl.when(first)
            def _():
                for r, v in zip(o_refs[n_plain:], vals[n_plain:]):
                    r[...] = v.astype(r.dtype)

            @pl.when(jnp.logical_not(first))
            def _():
                for r, v in zip(o_refs[n_plain:], vals[n_plain:]):
                    r[...] += v.astype(r.dtype)

    return pl.pallas_call(
        body, name=name, grid=grid, in_specs=in_specs, out_specs=out_specs,
        out_shape=[jax.ShapeDtypeStruct(s, d) for s, d in outs], compiler_params=_cparams(),
    )(*ins)


def _row_spec(t, c):
    return pl.BlockSpec((t, c), lambda i: (i, 0))


def _full_spec(shape):
    nd = len(shape)
    return pl.BlockSpec(tuple(shape), lambda *g: (0,) * nd)


def _rms(xf, w):
    return xf * lax.rsqrt(jnp.mean(xf * xf, axis=-1, keepdims=True) + RMS_EPS) * w


def _rms_bwd(xf, w, dy):
    _, vjp = jax.vjp(_rms, xf, w)
    return vjp(dy)


def _s5_disc(lr, li, ldt, bre, bim):
    dt = jnp.exp(ldt)
    mag = jnp.exp(lr * dt)
    ar = mag * jnp.cos(li * dt)
    ai = mag * jnp.sin(li * dt)
    nr, ni = ar - 1.0, ai
    den = lr * lr + li * li
    zr = (nr * lr + ni * li) / den
    zi = (ni * lr - nr * li) / den
    return ar, ai, zr * bre - zi * bim, zr * bim + zi * bre


def _s5_prep(lr, li, ldt, bre, bim):
    def body(lr_r, li_r, ldt_r, bre_r, bim_r, ar_r, ai_r, br_r, bi_r):
        ar, ai, br, bi = _s5_disc(lr_r[...], li_r[...], ldt_r[...], bre_r[...], bim_r[...])
        ar_r[...] = ar
        ai_r[...] = ai
        br_r[...] = br
        bi_r[...] = bi

    sd = jax.ShapeDtypeStruct
    return pl.pallas_call(
        body, name="s5_prep",
        out_shape=[sd(lr.shape, F32), sd(lr.shape, F32), sd(bre.shape, F32), sd(bre.shape, F32)],
        compiler_params=_cparams(),
    )(lr, li, ldt, bre, bim)


def _s5_prep_bwd(lr, li, ldt, bre, bim, dar, dai, dbr, dbi):
    def body(lr_r, li_r, ldt_r, bre_r, bim_r, dar_r, dai_r, dbr_r, dbi_r, o0, o1, o2, o3, o4):
        _, vjp = jax.vjp(_s5_disc, lr_r[...], li_r[...], ldt_r[...], bre_r[...], bim_r[...])
        g = vjp((dar_r[...], dai_r[...], dbr_r[...], dbi_r[...]))
        for o, v in zip((o0, o1, o2, o3, o4), g):
            o[...] = v

    sd = jax.ShapeDtypeStruct
    return pl.pallas_call(
        body, name="s5_prep_bwd",
        out_shape=[sd(lr.shape, F32), sd(li.shape, F32), sd(ldt.shape, F32), sd(bre.shape, F32), sd(bim.shape, F32)],
        compiler_params=_cparams(),
    )(lr, li, ldt, bre, bim, dar, dai, dbr, dbi)


SCAN_T = 256


def _scan_tables(ar, ai, tab_r, tab_i, sub, reverse):
    pr, pi = ar, ai
    for k in range(sub):
        row = sub - 1 - k if reverse else k
        tab_r[row:row + 1, :] = pr
        tab_i[row:row + 1, :] = pi
        pr, pi = ar * pr - ai * pi, ar * pi + ai * pr


def _pack_matrix(t_blk, dtype):
    sub = t_blk // SUBLANES
    dst = jnp.arange(t_blk)
    src = (dst % SUBLANES) * sub + dst // SUBLANES
    return (src[:, None] == jnp.arange(t_blk)[None, :]).astype(dtype)


def _permute_rows_f32(pm, x):
    hi = x.astype(BF16)
    r1 = x - hi.astype(F32)
    mid = r1.astype(BF16)
    lo = (r1 - mid.astype(F32)).astype(BF16)
    dot = lambda v: jnp.dot(pm, v, preferred_element_type=F32)
    return dot(hi) + dot(mid) + dot(lo)


def _scan_block(x, loc, ar, ai, st, tab_r, tab_i, sub, reverse):
    hb = STATE_BLOCK
    a8r = jnp.broadcast_to(ar, (SUBLANES, hb))
    a8i = jnp.broadcast_to(ai, (SUBLANES, hb))
    sr = jnp.zeros((SUBLANES, hb), F32)
    si = jnp.zeros((SUBLANES, hb), F32)
    steps = range(sub - 1, -1, -1) if reverse else range(sub)
    for t in steps:
        rows = slice(t * SUBLANES, (t + 1) * SUBLANES)
        sr, si = a8r * sr - a8i * si + x[rows, :hb], a8r * si + a8i * sr + x[rows, hb:]
        loc[rows, :hb] = sr
        loc[rows, hb:] = si
    cr, ci = st[0:1, :], st[1:2, :]
    far = 0 if reverse else sub - 1
    fr, fi = tab_r[far:far + 1, :], tab_i[far:far + 1, :]
    ent_r, ent_i = [None] * SUBLANES, [None] * SUBLANES
    for c in (range(SUBLANES - 1, -1, -1) if reverse else range(SUBLANES)):
        ent_r[c], ent_i[c] = cr, ci
        cr, ci = sr[c:c + 1, :] + (fr * cr - fi * ci), si[c:c + 1, :] + (fr * ci + fi * cr)
    st[0:1, :] = cr
    st[1:2, :] = ci
    c8r = jnp.concatenate(ent_r, axis=0)
    c8i = jnp.concatenate(ent_i, axis=0)
    out = []
    for t in range(sub):
        rows = slice(t * SUBLANES, (t + 1) * SUBLANES)
        tr, ti = tab_r[t:t + 1, :], tab_i[t:t + 1, :]
        out.append(jnp.concatenate([loc[rows, :hb] + (tr * c8r - ti * c8i), loc[rows, hb:] + (tr * c8i + ti * c8r)],
                                   axis=1))
    return jnp.concatenate(out, axis=0)


SSM_BLOCKS_PER_STEP = 2


def _scan_scratch(nblk, t_blk, sub, hb):
    return [pltpu.VMEM((nblk, SUBLANES, hb), F32), pltpu.VMEM((nblk, sub, hb), F32), pltpu.VMEM((nblk, sub, hb), F32),
            pltpu.VMEM((nblk, t_blk, 2 * hb), F32)]


def _ssm_fwd(proj, wb, wc, a):
    seq = proj.shape[0]
    nj = wb.shape[0]
    w2 = 2 * STATE_BLOCK
    hb = STATE_BLOCK
    t_blk = min(SCAN_T, seq)
    sub = t_blk // SUBLANES
    pm = _pack_matrix(t_blk, BF16)

    npair = SSM_BLOCKS_PER_STEP

    def body(u_ref, wb_ref, wc_ref, a_ref, pm_ref, pmt_ref, s_ref, y_ref, st, tab_r, tab_i, loc):
        coef = [(a_ref[:, b * w2:b * w2 + hb], a_ref[:, b * w2 + hb:(b + 1) * w2]) for b in range(npair)]

        @pl.when(pl.program_id(1) == 0)
        def _():
            for b, (ar, ai) in enumerate(coef):
                st[b] = jnp.zeros((SUBLANES, hb), F32)
                _scan_tables(ar, ai, tab_r.at[b], tab_i.at[b], sub, False)

        for b, (ar, ai) in enumerate(coef):
            ub = u_ref[:, b * LANES:(b + 1) * LANES].astype(BF16)
            up = jnp.dot(pm_ref[...], ub, preferred_element_type=F32).astype(BF16)
            bu = jnp.dot(up, wb_ref[b], preferred_element_type=F32)
            s = _scan_block(bu, loc.at[b], ar, ai, st.at[b], tab_r.at[b], tab_i.at[b], sub, False)
            s_ref[:, b * w2:(b + 1) * w2] = s
            yp = jnp.dot(s.astype(BF16), wc_ref[b], preferred_element_type=F32)
            y_ref[:, b * LANES:(b + 1) * LANES] = _permute_rows_f32(pmt_ref[...], yp)

    sd = jax.ShapeDtypeStruct
    return pl.pallas_call(
        body, name="ssm_fwd", grid=(nj // npair, seq // t_blk),
        in_specs=[pl.BlockSpec((t_blk, npair * LANES), lambda j, i: (i, j)),
                  pl.BlockSpec((npair, LANES, w2), lambda j, i: (j, 0, 0)),
                  pl.BlockSpec((npair, w2, LANES), lambda j, i: (j, 0, 0)),
                  pl.BlockSpec((1, npair * w2), lambda j, i: (0, j)),
                  _full_spec((t_blk, t_blk)), _full_spec((t_blk, t_blk))],
        out_specs=[pl.BlockSpec((t_blk, npair * w2), lambda j, i: (i, j)),
                   pl.BlockSpec((t_blk, npair * LANES), lambda j, i: (i, j))],
        out_shape=[sd((seq, nj * w2), F32), sd((seq, nj * LANES), F32)],
        scratch_shapes=_scan_scratch(npair, t_blk, sub, hb), compiler_params=_cparams(),
    )(proj, wb, wc, a, pm, pm.T)


def _ssm_bwd(dy, s, proj, du1, wb, wc, a):
    seq = dy.shape[0]
    nj = wb.shape[0]
    w2 = 2 * STATE_BLOCK
    hb = STATE_BLOCK
    t_blk = min(SCAN_T, seq)
    sub = t_blk // SUBLANES
    nb = seq // t_blk
    pm = _pack_matrix(t_blk, BF16)

    npair = SSM_BLOCKS_PER_STEP

    def body(dy_ref, s_ref, sprev_ref, u_ref, du1_ref, wb_ref, wc_ref, a_ref, pm_ref, pmt_ref,
             du_ref, dwb_ref, dwc_ref, da_ref, st, tab_r, tab_i, loc):
        ib = pl.program_id(1)
        pmv = pm_ref[...]
        coef = [(a_ref[:, b * w2:b * w2 + hb], -a_ref[:, b * w2 + hb:(b + 1) * w2]) for b in range(npair)]

        @pl.when(ib == 0)
        def _():
            for b, (ar, ai) in enumerate(coef):
                st[b] = jnp.zeros((SUBLANES, hb), F32)
                _scan_tables(ar, ai, tab_r.at[b], tab_i.at[b], sub, True)

        sums = []
        for b, (ar, ai) in enumerate(coef):
            cols, wide = slice(b * LANES, (b + 1) * LANES), slice(b * w2, (b + 1) * w2)
            dyp = jnp.dot(pmv, dy_ref[:, cols], preferred_element_type=F32).astype(BF16)
            up = jnp.dot(pmv, u_ref[:, cols].astype(BF16), preferred_element_type=F32).astype(BF16)
            ds = lax.dot_general(dyp, wc_ref[b], (NT, ((), ())), preferred_element_type=F32)
            lam = _scan_block(ds, loc.at[b], ar, ai, st.at[b], tab_r.at[b], tab_i.at[b], sub, True)
            lamb = lam.astype(BF16)
            du = lax.dot_general(lamb, wb_ref[b], (NT, ((), ())), preferred_element_type=F32)
            du_ref[:, cols] = (_permute_rows_f32(pmt_ref[...], du) + du1_ref[:, cols]).astype(du_ref.dtype)
            sv = s_ref[:, wide]
            dwb = lax.dot_general(up, lamb, (TN, ((), ())), preferred_element_type=F32)
            dwc = lax.dot_general(sv.astype(BF16), dyp, (TN, ((), ())), preferred_element_type=F32)

            prev_last = sprev_ref[SUBLANES - 1:SUBLANES, wide]
            prev_last = jnp.where(ib == nb - 1, jnp.zeros_like(prev_last), prev_last)
            tail = sv[t_blk - SUBLANES:, :]
            sl = lax.broadcasted_iota(jnp.int32, tail.shape, 0)
            head = jnp.where(sl >= 1, pltpu.roll(tail, 1, 0), prev_last)
            s_sh = jnp.concatenate([head, sv[:t_blk - SUBLANES, :]], axis=0)
            lam_r, lam_i = lam[:, :hb], lam[:, hb:]
            sr_, si_ = s_sh[:, :hb], s_sh[:, hb:]
            dar = jnp.sum(lam_r * sr_ + lam_i * si_, axis=0, keepdims=True)
            dai = jnp.sum(lam_i * sr_ - lam_r * si_, axis=0, keepdims=True)
            sums.append((wide, jnp.concatenate([dar, dai], axis=1), dwb, dwc))

        @pl.when(ib == 0)
        def _():
            for b, (wide, contrib, dwb, dwc) in enumerate(sums):
                da_ref[:, wide] = contrib
                dwb_ref[b] = dwb
                dwc_ref[b] = dwc

        @pl.when(ib != 0)
        def _():
            for b, (wide, contrib, dwb, dwc) in enumerate(sums):
                da_ref[:, wide] += contrib
                dwb_ref[b] += dwb
                dwc_ref[b] += dwc

    blk = lambda j, i: (nb - 1 - i, j)
    prev_blk = lambda j, i: (jnp.maximum((nb - 1 - i) * sub - 1, 0), j)
    sd = jax.ShapeDtypeStruct
    return pl.pallas_call(
        body, name="ssm_bwd", grid=(nj // npair, nb),
        in_specs=[pl.BlockSpec((t_blk, npair * LANES), blk), pl.BlockSpec((t_blk, npair * w2), blk),
                  pl.BlockSpec((SUBLANES, npair * w2), prev_blk), pl.BlockSpec((t_blk, npair * LANES), blk),
                  pl.BlockSpec((t_blk, npair * LANES), blk),
                  pl.BlockSpec((npair, LANES, w2), lambda j, i: (j, 0, 0)),
                  pl.BlockSpec((npair, w2, LANES), lambda j, i: (j, 0, 0)),
                  pl.BlockSpec((1, npair * w2), lambda j, i: (0, j)),
                  _full_spec((t_blk, t_blk)), _full_spec((t_blk, t_blk))],
        out_specs=[pl.BlockSpec((t_blk, npair * LANES), blk),
                   pl.BlockSpec((npair, LANES, w2), lambda j, i: (j, 0, 0)),
                   pl.BlockSpec((npair, w2, LANES), lambda j, i: (j, 0, 0)),
                   pl.BlockSpec((1, npair * w2), lambda j, i: (0, j))],
        out_shape=[sd((seq, nj * LANES), BF16), sd((nj, LANES, w2), F32), sd((nj, w2, LANES), F32),
                   sd((1, nj * w2), F32)],
        scratch_shapes=_scan_scratch(npair, t_blk, sub, hb), compiler_params=_cparams(),
    )(dy, s, s, proj, du1, wb, wc, a, pm, pm.T)


def _rope128(x, cos, sa, sb):
    return x * cos + pltpu.roll(x, 96, 1) * sa + pltpu.roll(x, 32, 1) * sb


def _rope128_t(dy, cos, sa, sb):
    return dy * cos + pltpu.roll(dy * sa, 32, 1) + pltpu.roll(dy * sb, 96, 1)


ATT_BQ = 256


def _probs(qn, qp, kn, kp, r0, scale):
    s = lax.dot_general(qn, kn, (NT, ((), ())), preferred_element_type=F32)
    s = s + lax.dot_general(qp, kp, (NT, ((), ())), preferred_element_type=F32)
    s = s * scale
    diag = s[:, r0:]
    row = lax.broadcasted_iota(jnp.int32, diag.shape, 0)
    col = lax.broadcasted_iota(jnp.int32, diag.shape, 1)
    diag = jnp.where(col <= row, diag, jnp.finfo(F32).min)
    s = diag if r0 == 0 else jnp.concatenate([s[:, :r0], diag], axis=1)
    m = jnp.max(s, axis=-1, keepdims=True)
    e = jnp.exp(s - m)
    return e / jnp.sum(e, axis=-1, keepdims=True)


def _attn_specs(seq):
    tab = pl.BlockSpec((seq, LANES), lambda h: (0, 0))
    return [pl.BlockSpec((None, seq, 256), lambda h: (h, 0, 0)), pl.BlockSpec((None, seq, 128), lambda h: (h, 0, 0)),
            pl.BlockSpec((None, seq, 128), lambda h: (h, 0, 1)), tab, tab, tab, tab]


def _attn_fwd(q_raw, kv, kpe, cos, sa, sb):
    nh, seq, _ = q_raw.shape
    bq = min(ATT_BQ, seq)
    scale = (QK_NOPE + QK_ROPE) ** -0.5

    def body(q_ref, kn_ref, v_ref, kp_ref, cos_ref, sa_ref, sb_ref, o_ref):
        for r0 in range(0, seq, bq):
            rows, kend = pl.ds(r0, bq), r0 + bq
            qn = q_ref[rows, :QK_NOPE].astype(BF16)
            qp = _rope128(q_ref[rows, QK_NOPE:], cos_ref[rows, :], sa_ref[rows, :], sb_ref[rows, :]).astype(BF16)
            p = _probs(qn, qp, kn_ref[:kend, :], kp_ref[:kend, :], r0, scale)
            o_ref[rows, :] = jnp.dot(p.astype(BF16), v_ref[:kend, :], preferred_element_type=F32)

    return pl.pallas_call(
        body, name="attn_fwd", grid=(nh,), in_specs=_attn_specs(seq),
        out_specs=pl.BlockSpec((seq, V_DIM), lambda h: (0, h)),
        out_shape=jax.ShapeDtypeStruct((seq, nh * V_DIM), F32), compiler_params=_cparams(),
    )(q_raw, kv, kv, kpe, cos, sa, sb)


def _attn_bwd(q_raw, kv, kpe, cos, sa, sb, do):
    nh, seq, _ = q_raw.shape
    bq = min(ATT_BQ, seq)
    scale = (QK_NOPE + QK_ROPE) ** -0.5

    def body(q_ref, kn_ref, v_ref, kp_ref, cos_ref, sa_ref, sb_ref, do_ref, dq_ref, dkv_ref, dkp_ref):
        dkv_ref[...] = jnp.zeros_like(dkv_ref)
        dkp_ref[...] = jnp.zeros_like(dkp_ref)
        for r0 in range(0, seq, bq):
            rows, kend = pl.ds(r0, bq), r0 + bq
            cos_b, sa_b, sb_b = cos_ref[rows, :], sa_ref[rows, :], sb_ref[rows, :]
            qn = q_ref[rows, :QK_NOPE].astype(BF16)
            qp = _rope128(q_ref[rows, QK_NOPE:], cos_b, sa_b, sb_b).astype(BF16)
            kn, v, kp = kn_ref[:kend, :], v_ref[:kend, :], kp_ref[:kend, :]
            p = _probs(qn, qp, kn, kp, r0, scale)
            dob = do_ref[rows, :].astype(BF16)
            dp = lax.dot_general(dob, v, (NT, ((), ())), preferred_element_type=F32)
            ds = p * (dp - jnp.sum(p * dp, axis=-1, keepdims=True)) * scale
            dsb = ds.astype(BF16)
            pb = p.astype(BF16)
            dq_ref[rows, :QK_NOPE] = jnp.dot(dsb, kn, preferred_element_type=F32).astype(dq_ref.dtype)
            dqp = jnp.dot(dsb, kp, preferred_element_type=F32)
            dq_ref[rows, QK_NOPE:] = _rope128_t(dqp, cos_b, sa_b, sb_b).astype(dq_ref.dtype)
            dkv_ref[:kend, :QK_NOPE] += lax.dot_general(dsb, qn, (TN, ((), ())), preferred_element_type=F32)
            dkv_ref[:kend, QK_NOPE:] += lax.dot_general(pb, dob, (TN, ((), ())), preferred_element_type=F32)
            dkp_ref[:kend, :] += lax.dot_general(dsb, qp, (TN, ((), ())), preferred_element_type=F32)

    sd = jax.ShapeDtypeStruct
    return pl.pallas_call(
        body, name="attn_bwd", grid=(nh,),
        in_specs=_attn_specs(seq) + [pl.BlockSpec((seq, V_DIM), lambda h: (0, h))],
        out_specs=[pl.BlockSpec((None, seq, 256), lambda h: (h, 0, 0)),
                   pl.BlockSpec((None, seq, 256), lambda h: (h, 0, 0)),
                   pl.BlockSpec((None, seq, 128), lambda h: (h, 0, 0))],
        out_shape=[sd((nh, seq, 256), BF16), sd((nh, seq, 256), F32), sd((nh, seq, 128), F32)],
        compiler_params=_cparams(),
    )(q_raw, kv, kv, kpe, cos, sa, sb, do)


def _conv3(a, w, b):
    rows = lax.broadcasted_iota(jnp.int32, a.shape, 0)
    a1 = jnp.where(rows >= 1, pltpu.roll(a, 1, 0), 0.0)
    a2 = jnp.where(rows >= 2, pltpu.roll(a, 2, 0), 0.0)
    return w[2:3] * a + w[1:2] * a1 + w[0:1] * a2 + b, a1, a2


def _conv_gate_fwd(a, cw, cb):
    half, _, seq, c = a.shape
    nc = c // LANES

    def fn(pair, wg, wv, bg, bv):
        gc, _, _ = _conv3(pair[0], wg, bg)
        vc, _, _ = _conv3(pair[1], wv, bv)
        return gc * jax.nn.sigmoid(gc) * vc

    def w_spec(off, r):
        return pl.BlockSpec((None, r, LANES), lambda k, j: (k + off, 0, j))

    return _blockwise(
        "conv_gate_fwd", fn, [a, cw, cw, cb, cb],
        [pl.BlockSpec((None, 2, seq, LANES), lambda k, j: (k, 0, 0, j)),
         w_spec(0, 3), w_spec(half, 3), w_spec(0, 1), w_spec(half, 1)],
        [((seq, half * c), BF16)], [pl.BlockSpec((seq, LANES), lambda k, j: (0, k * nc + j))],
        grid=(half, nc))[0]


def _conv_gate_bwd(a, cw, cb, dm):
    half, _, seq, c = a.shape
    nc = c // LANES

    def body(a_ref, wg_ref, wv_ref, bg_ref, bv_ref, dm_ref, da_ref, dw_ref, db_ref):
        dmv = dm_ref[...]
        rows = lax.broadcasted_iota(jnp.int32, dmv.shape, 0)
        ga, wg = a_ref[0], wg_ref[...]
        va, wv = a_ref[1], wv_ref[...]
        gc, g1, g2 = _conv3(ga, wg, bg_ref[...])
        vc, v1, v2 = _conv3(va, wv, bv_ref[...])
        sg = jax.nn.sigmoid(gc)
        dms = dmv * sg
        d_val = dms * gc
        d_gate = dms * vc * (1.0 + gc * (1.0 - sg))

        def back(r, dc, own, a1, a2, w):
            up1 = jnp.where(rows < seq - 1, pltpu.roll(dc, seq - 1, 0), 0.0)
            up2 = jnp.where(rows < seq - 2, pltpu.roll(dc, seq - 2, 0), 0.0)
            da_ref[r] = (w[2:3] * dc + w[1:2] * up1 + w[0:1] * up2).astype(da_ref.dtype)
            dw_ref[r, 0:1, :] = jnp.sum(dc * a2, axis=0, keepdims=True)
            dw_ref[r, 1:2, :] = jnp.sum(dc * a1, axis=0, keepdims=True)
            dw_ref[r, 2:3, :] = jnp.sum(dc * own, axis=0, keepdims=True)
            db_ref[r] = jnp.sum(dc, axis=0, keepdims=True)

        back(0, d_gate, ga, g1, g2, wg)
        back(1, d_val, va, v1, v2, wv)

    def w_spec(off, r):
        return pl.BlockSpec((None, r, LANES), lambda k, j: (k + off, 0, j))

    def pair_spec(r):
        return pl.BlockSpec((None, 2, r, LANES), lambda k, j: (k, 0, 0, j))

    sd = jax.ShapeDtypeStruct
    return pl.pallas_call(
        body, name="conv_gate_bwd", grid=(half, nc),
        in_specs=[pair_spec(seq), w_spec(0, 3), w_spec(half, 3), w_spec(0, 1), w_spec(half, 1),
                  pl.BlockSpec((seq, LANES), lambda k, j: (0, k * nc + j))],
        out_specs=[pair_spec(seq), pair_spec(3), pair_spec(1)],
        out_shape=[sd((half, 2, seq, c), BF16), sd((half, 2, 3, c), F32), sd((half, 2, 1, c), F32)],
        compiler_params=_cparams(),
    )(a, cw, cw, cb, cb, dm)


ROW_T = 256


def _local_step(x, positions, target, w, emit=lambda **grads: None):
    seq, d = x.shape
    t_row = min(ROW_T, seq)
    nrow = seq // t_row
    ssm_w = d // 2
    nj = ssm_w // LANES
    n_groups = ssm_w // SSM_GROUP
    nh = w["wuq"].shape[0]
    q_rank = w["wuq"].shape[1]
    kv_rank = w["wukv"].shape[1]
    ns = w["wup"].shape[0]
    c_ff = w["wup"].shape[2]
    in_pad = w["win"].shape[1]
    tm = min(1024, seq)
    nm = seq // tm
    sw = 2 * STATE_BLOCK
    g1 = (nrow,)

    lr3 = w["lam_re"].reshape(n_groups, 1, SSM_STATE)
    li3 = w["lam_im"].reshape(n_groups, 1, SSM_STATE)
    ldt3 = w["log_dt"].reshape(n_groups, 1, 1)
    bt_re = jnp.swapaxes(w["b_re"].reshape(n_groups, SSM_STATE, SSM_GROUP), 1, 2)
    bt_im = jnp.swapaxes(w["b_im"].reshape(n_groups, SSM_STATE, SSM_GROUP), 1, 2)
    abar_re, abar_im, bbt_re, bbt_im = _s5_prep(lr3, li3, ldt3, bt_re, bt_im)
    eye = jnp.eye(GROUPS_PER_BLOCK, dtype=F32)

    def blockdiag_in(bb):
        t = bb.reshape(nj, GROUPS_PER_BLOCK, SSM_GROUP, SSM_STATE)
        return jnp.einsum("jghp,gk->jghkp", t, eye).reshape(nj, LANES, STATE_BLOCK)

    def blockdiag_in_t(dwb):
        t = dwb.reshape(nj, GROUPS_PER_BLOCK, SSM_GROUP, GROUPS_PER_BLOCK, SSM_STATE)
        return jnp.einsum("jghkp,gk->jghp", t, eye).reshape(n_groups, SSM_GROUP, SSM_STATE)

    def blockdiag_out(cc):
        t = cc.reshape(nj, GROUPS_PER_BLOCK, SSM_GROUP, SSM_STATE)
        return jnp.einsum("jghp,gk->jkpgh", t, eye).reshape(nj, STATE_BLOCK, LANES)

    def blockdiag_out_t(dwc):
        t = dwc.reshape(nj, GROUPS_PER_BLOCK, SSM_STATE, GROUPS_PER_BLOCK, SSM_GROUP)
        return jnp.einsum("jkpgh,gk->jghp", t, eye).reshape(n_groups, SSM_GROUP, SSM_STATE)

    c_re = w["c_re"].reshape(n_groups, SSM_GROUP, SSM_STATE)
    c_im = w["c_im"].reshape(n_groups, SSM_GROUP, SSM_STATE)
    wb = jnp.concatenate([blockdiag_in(bbt_re), blockdiag_in(bbt_im)], axis=2).astype(BF16)
    wc = jnp.concatenate([blockdiag_out(c_re), -blockdiag_out(c_im)], axis=1).astype(BF16)
    a_lay = jnp.concatenate([abar_re.reshape(nj, 1, STATE_BLOCK), abar_im.reshape(nj, 1, STATE_BLOCK)],
                            axis=1).reshape(1, nj * sw)

    attn_w = w["attn_norm"]
    hn = _blockwise("norm1", lambda xb, wv: _rms(xb, wv), [x, attn_w], [_row_spec(t_row, d), _full_spec((1, d))],
                    [((seq, d), BF16)], [_row_spec(t_row, d)], g1)[0]
    proj = _mm2d("proj", hn, w["win"], NN, F32, tn=640)

    s_all, ylin = _ssm_fwd(proj, wb, wc, a_lay)
    u_spec = pl.BlockSpec((t_row, ssm_w), lambda i: (i, 0))

    def ypre_fn(yl, ub, dsk):
        yp = yl + dsk * ub
        return yp, jax.nn.gelu(yp)

    y_pre, yg = _blockwise("ssm_gelu", ypre_fn, [ylin, proj, w["ssm_d"]],
                           [_row_spec(t_row, ssm_w), u_spec, _full_spec((1, ssm_w))],
                           [((seq, ssm_w), F32), ((seq, ssm_w), BF16)],
                           [_row_spec(t_row, ssm_w)] * 2, g1)
    z = _mm2d("ssm_glu", yg, w["wglu"], NN, F32, res=w["b_glu"])
    y_ssm = _blockwise("ssm_gate", lambda yp, zb: jax.nn.gelu(yp) * jax.nn.sigmoid(zb), [y_pre, z],
                       [_row_spec(t_row, ssm_w)] * 2, [((seq, ssm_w), F32)], [_row_spec(t_row, ssm_w)], g1)[0]

    cq_off, ckv_off, kpe_off = ssm_w, ssm_w + q_rank, ssm_w + q_rank + kv_rank
    c_q = proj[:, cq_off:ckv_off]
    c_kv = proj[:, ckv_off:kpe_off]
    kpe_raw = proj[:, kpe_off:kpe_off + LANES]
    pos_b = jnp.broadcast_to(positions.astype(F32)[:, None], (seq, LANES))
    inv_freq = ROPE_THETA ** (-jnp.arange(0, QK_ROPE, 2, dtype=F32) / QK_ROPE)
    inv128 = jnp.tile(inv_freq, 4).reshape(1, LANES)

    def mla_prep_fn(cq, ckv, kp, pb, inv, wq, wkv):
        ang = pb * inv
        lane = lax.broadcasted_iota(jnp.int32, ang.shape, 1)
        cs, sn = jnp.cos(ang), jnp.sin(ang)
        cos = jnp.where(lane < QK_ROPE, cs, 0.0)
        sa = jnp.where(lane < QK_ROPE // 2, -sn, 0.0)
        sb = jnp.where(jnp.logical_and(lane >= QK_ROPE // 2, lane < QK_ROPE), sn, 0.0)
        return _rms(cq, wq), _rms(ckv, wkv), _rope128(kp, cos, sa, sb), cos, sa, sb

    qn, kvn, kpe, cos_t, sa_t, sb_t = _blockwise(
        "mla_prep", mla_prep_fn, [c_q, c_kv, kpe_raw, pos_b, inv128, w["q_norm"], w["kv_norm"]],
        [_row_spec(t_row, q_rank), _row_spec(t_row, kv_rank), _row_spec(t_row, LANES), _row_spec(t_row, LANES),
         _full_spec((1, LANES)), _full_spec((1, q_rank)), _full_spec((1, kv_rank))],
        [((seq, q_rank), BF16), ((seq, kv_rank), BF16), ((seq, LANES), BF16)] + [((seq, LANES), F32)] * 3,
        [_row_spec(t_row, q_rank), _row_spec(t_row, kv_rank)] + [_row_spec(t_row, LANES)] * 4, g1)

    def head_mm(name, act, wh, out_dtype):
        kdim, ndim = wh.shape[1], wh.shape[2]
        return _mm(name, act, wh, grid=(nh, 1, 1),
                   a_spec=pl.BlockSpec((seq, kdim), lambda h, i, k: (i, 0)),
                   b_spec=pl.BlockSpec((None, kdim, ndim), lambda h, i, k: (h, 0, 0)),
                   o_spec=pl.BlockSpec((None, seq, ndim), lambda h, i, k: (h, i, 0)),
                   out_shape=(nh, seq, ndim), out_dtype=out_dtype)

    q_raw = head_mm("mla_q", qn, w["wuq"], F32)
    kv = head_mm("mla_kv", kvn, w["wukv"], BF16)
    y_mla = _attn_fwd(q_raw, kv, kpe, cos_t, sa_t, sb_t)
    mla_w = nh * V_DIM

    def outnorm_fn(ys, ym, ws, wm):
        return jnp.concatenate([_rms(ys, ws), _rms(ym, wm)], axis=1)

    ycat = _blockwise("out_norm", outnorm_fn, [y_ssm, y_mla, w["son"], w["mon"]],
                      [_row_spec(t_row, ssm_w), _row_spec(t_row, mla_w), _full_spec((1, ssm_w)), _full_spec((1, mla_w))],
                      [((seq, d), BF16)], [_row_spec(t_row, d)], g1)[0]
    h1 = _mm2d("out_proj", ycat, w["wout"], NN, F32, res=x)

    hn2 = _blockwise("norm2", lambda hb, wv: _rms(hb, wv), [h1, w["ffn_norm"]],
                     [_row_spec(t_row, d), _full_spec((1, d))], [((seq, d), BF16)], [_row_spec(t_row, d)], g1)[0]
    tku = d
    half = ns // 2
    a_ff = _mm("ffn_up", hn2, w["wup"], grid=(ns, nm, d // tku),
               a_spec=pl.BlockSpec((tm, tku), lambda s, i, k: (i, k)),
               b_spec=pl.BlockSpec((None, tku, c_ff), lambda s, i, k: (s, k, 0)),
               o_spec=pl.BlockSpec((None, None, tm, c_ff), lambda s, i, k: (s % half, s // half, i, 0)),
               out_shape=(half, 2, seq, c_ff), out_dtype=F32)
    cb3 = w["conv_b"].reshape(ns, 1, c_ff)
    m_ff = _conv_gate_fwd(a_ff, w["conv_w"], cb3)
    d_ff = half * c_ff
    wdn = w["wdown"]
    tnd = _tile(d, 1024)
    tmx, tnx = min(1024, seq), _tile(d, 1024)
    h2 = _mm2d("ffn_down", m_ff, wdn, NN, F32, tm=512, tn=512, tk=d_ff, res=h1)

    def loss_fn(hb, tb, wv):
        def f(hh, ww):
            err = _rms(hh, ww) - tb
            return 0.5 * jnp.sum(jnp.mean(err * err, axis=-1))

        lossv, (dh, dw) = jax.value_and_grad(f, argnums=(0, 1))(hb, wv)
        return dh, dh, jnp.full((1, LANES), lossv, F32), dw

    fin_w = w["final_norm"].reshape(1, d)
    dh2, dh2b, loss_acc, g_final = _blockwise(
        "loss_head", loss_fn, [h2, target, fin_w], [_row_spec(t_row, d), _row_spec(t_row, d), _full_spec((1, d))],
        [((seq, d), F32), ((seq, d), BF16), ((1, LANES), F32), ((1, d), F32)],
        [_row_spec(t_row, d), _row_spec(t_row, d), _full_spec((1, LANES)), _full_spec((1, d))], g1, n_acc=2)
    loss = loss_acc[0, 0]

    dm = _mm2d("ffn_down_dx", dh2b, wdn, NT, F32, tn=c_ff)
    tks = seq
    g_wdown = _mm2d("ffn_down_dw", m_ff, dh2b, TN, BF16, tm=c_ff)
    emit(wdown=g_wdown)
    da_ff, g_convw2, g_convb2 = _conv_gate_bwd(a_ff, w["conv_w"], cb3, dm)
    g_convw = jnp.swapaxes(g_convw2, 0, 1).reshape(ns, 3, c_ff)
    g_convb = jnp.swapaxes(g_convb2, 0, 1).reshape(ns, 1, c_ff)
    g_wup = _mm("ffn_up_dw", hn2, da_ff, grid=(ns, d // tnd, seq // tks), contract=TN,
                a_spec=pl.BlockSpec((tks, tnd), lambda s, j, k: (k, j)),
                b_spec=pl.BlockSpec((None, None, tks, c_ff), lambda s, j, k: (s % half, s // half, k, 0)),
                o_spec=pl.BlockSpec((None, tnd, c_ff), lambda s, j, k: (s, j, 0)),
                out_shape=(ns, d, c_ff), out_dtype=BF16)
    emit(wup=g_wup)
    dhn2 = _mm("ffn_up_dx", da_ff, w["wup"], grid=(seq // tmx, d // tnx, ns), contract=NT,
               a_spec=pl.BlockSpec((None, None, tmx, c_ff), lambda i, j, s: (s % half, s // half, i, 0)),
               b_spec=pl.BlockSpec((None, tnx, c_ff), lambda i, j, s: (s, j, 0)),
               o_spec=pl.BlockSpec((tmx, tnx), lambda i, j, s: (i, j)),
               out_shape=(seq, d), out_dtype=F32)
    emit(wup_pair_sums_after=dhn2)

    def norm_bwd_fn(hb, dres, dn, wv):
        dx_, dw_ = _rms_bwd(hb, wv, dn)
        dtot = dres + dx_
        return dtot, dtot, dw_

    dh1, dh1b, g_ffn_norm = _blockwise(
        "norm2_bwd", norm_bwd_fn, [h1, dh2, dhn2, w["ffn_norm"]],
        [_row_spec(t_row, d)] * 3 + [_full_spec((1, d))],
        [((seq, d), F32), ((seq, d), BF16), ((1, d), F32)],
        [_row_spec(t_row, d), _row_spec(t_row, d), _full_spec((1, d))], g1, n_acc=1)

    g_wout = _mm2d("out_proj_dw", ycat, dh1b, TN, BF16)

    def outnorm_bwd_fn(dhb, wo, ys, ym, ws, wm):
        dyc = lax.dot_general(dhb, wo, (NT, ((), ())), preferred_element_type=F32)
        dys, dws = _rms_bwd(ys, ws, dyc[:, :ssm_w])
        dym, dwm = _rms_bwd(ym, wm, dyc[:, ssm_w:])
        return dys, dym, dws, dwm

    dy_ssm, dy_mla, g_son, g_mon = _blockwise(
        "out_proj_dx_norm_bwd", outnorm_bwd_fn, [dh1b, w["wout"], y_ssm, y_mla, w["son"], w["mon"]],
        [_row_spec(t_row, d), _full_spec((d, d)), _row_spec(t_row, ssm_w), _row_spec(t_row, mla_w),
         _full_spec((1, ssm_w)), _full_spec((1, mla_w))],
        [((seq, ssm_w), F32), ((seq, mla_w), F32), ((1, ssm_w), F32), ((1, mla_w), F32)],
        [_row_spec(t_row, ssm_w), _row_spec(t_row, mla_w), _full_spec((1, ssm_w)), _full_spec((1, mla_w))],
        g1, n_acc=2)

    def glu_bwd_fn(dy, yp, zb, ub, dsk, wg):
        ygv = jax.nn.gelu(yp)
        sg = jax.nn.sigmoid(zb)
        dz = dy * ygv * sg * (1.0 - sg)
        dzb = dz.astype(BF16)
        dyg = dy * sg + lax.dot_general(dzb, wg, (NT, ((), ())), preferred_element_type=F32)
        _, vjp = jax.vjp(jax.nn.gelu, yp)
        dyp = vjp(dyg)[0]
        return (dzb, dyp, dyp * dsk, jnp.sum(dz, axis=0, keepdims=True), jnp.sum(dyp * ub, axis=0, keepdims=True))

    dz, dy_pre, du1, g_bglu, g_ssmd = _blockwise(
        "ssm_glu_bwd", glu_bwd_fn, [dy_ssm, y_pre, z, proj, w["ssm_d"], w["wglu"]],
        [_row_spec(t_row, ssm_w)] * 3 + [u_spec, _full_spec((1, ssm_w)), _full_spec((ssm_w, ssm_w))],
        [((seq, ssm_w), BF16), ((seq, ssm_w), BF16), ((seq, ssm_w), F32), ((1, ssm_w), F32), ((1, ssm_w), F32)],
        [_row_spec(t_row, ssm_w)] * 3 + [_full_spec((1, ssm_w))] * 2, g1, n_acc=2)
    g_wglu = _mm2d("ssm_glu_dw", yg, dz, TN, BF16)
    dq_raw, dkv, dkp_h = _attn_bwd(q_raw, kv, kpe, cos_t, sa_t, sb_t, dy_mla)

    def head_mm_dx(name, dact, wh):
        kdim, ndim = wh.shape[1], wh.shape[2]
        return _mm(name, dact, wh, grid=(1, 1, nh), contract=NT,
                   a_spec=pl.BlockSpec((None, seq, ndim), lambda i, j, h: (h, i, 0)),
                   b_spec=pl.BlockSpec((None, kdim, ndim), lambda i, j, h: (h, 0, 0)),
                   o_spec=pl.BlockSpec((seq, kdim), lambda i, j, h: (i, 0)),
                   out_shape=(seq, kdim), out_dtype=F32)

    def head_mm_dw(name, act, dact):
        kdim, ndim = act.shape[1], dact.shape[2]
        return _mm(name, act, dact, grid=(nh, 1, seq // tks), contract=TN,
                   a_spec=pl.BlockSpec((tks, kdim), lambda h, j, k: (k, 0)),
                   b_spec=pl.BlockSpec((None, tks, ndim), lambda h, j, k: (h, k, 0)),
                   o_spec=pl.BlockSpec((None, kdim, ndim), lambda h, j, k: (h, 0, 0)),
                   out_shape=(nh, kdim, ndim), out_dtype=BF16)

    g_wuq = head_mm_dw("mla_q_dw", qn, dq_raw)
    g_wukv = head_mm_dw("mla_kv_dw", kvn, dkv)
    dqn = head_mm_dx("mla_q_dx", dq_raw, w["wuq"])
    dkvn = head_mm_dx("mla_kv_dx", dkv, w["wukv"])
    emit(not_before=(dqn, dkvn, dy_pre), wout=g_wout, wuq=g_wuq, wukv=g_wukv, wglu=g_wglu, conv_w=g_convw)

    du, dwb, dwc, da_lay = _ssm_bwd(dy_pre, s_all, proj, du1, wb, wc, a_lay)
    g_c_re = blockdiag_out_t(dwc[:, :STATE_BLOCK, :])
    g_c_im = -blockdiag_out_t(dwc[:, STATE_BLOCK:, :])
    dbbt_re = blockdiag_in_t(dwb[:, :, :STATE_BLOCK])
    dbbt_im = blockdiag_in_t(dwb[:, :, STATE_BLOCK:])
    da3 = da_lay.reshape(nj, 2, STATE_BLOCK)
    dabar_re = da3[:, 0, :].reshape(n_groups, 1, SSM_STATE)
    dabar_im = da3[:, 1, :].reshape(n_groups, 1, SSM_STATE)
    g_lr3, g_li3, g_ldt3, g_bt_re, g_bt_im = _s5_prep_bwd(lr3, li3, ldt3, bt_re, bt_im,
                                                           dabar_re, dabar_im, dbbt_re, dbbt_im)

    def mla_prep_bwd_fn(cq, ckv, dqn_b, dkvn_b, dkp_b, cos, sa, sb, wq, wkv):
        dcq, dwq = _rms_bwd(cq, wq, dqn_b)
        dckv, dwkv = _rms_bwd(ckv, wkv, dkvn_b)
        dkp_sum = dkp_b[0]
        for h in range(1, nh):
            dkp_sum = dkp_sum + dkp_b[h]
        return dcq, dckv, _rope128_t(dkp_sum, cos, sa, sb), dwq, dwkv

    dc_q, dc_kv, dkpe_raw, g_qnorm, g_kvnorm = _blockwise(
        "mla_prep_bwd", mla_prep_bwd_fn, [c_q, c_kv, dqn, dkvn, dkp_h, cos_t, sa_t, sb_t, w["q_norm"], w["kv_norm"]],
        [_row_spec(t_row, q_rank), _row_spec(t_row, kv_rank), _row_spec(t_row, q_rank), _row_spec(t_row, kv_rank),
         pl.BlockSpec((nh, t_row, LANES), lambda i: (0, i, 0))] + [_row_spec(t_row, LANES)] * 3
        + [_full_spec((1, q_rank)), _full_spec((1, kv_rank))],
        [((seq, q_rank), BF16), ((seq, kv_rank), BF16), ((seq, LANES), BF16), ((1, q_rank), F32), ((1, kv_rank), F32)],
        [_row_spec(t_row, q_rank), _row_spec(t_row, kv_rank), _row_spec(t_row, LANES), _full_spec((1, q_rank)),
         _full_spec((1, kv_rank))], g1, n_acc=2)

    dproj = jnp.concatenate([du, dc_q, dc_kv, dkpe_raw], axis=1)
    g_win = _mm2d("proj_dw", hn, dproj, TN, BF16, tn=640)
    emit(win=g_win)
    def norm1_bwd_fn(dpb, wi, xb, dres, wv):
        dn = lax.dot_general(dpb, wi, (NT, ((), ())), preferred_element_type=F32)
        dx_, dw_ = _rms_bwd(xb, wv, dn)
        return dres + dx_, dw_

    grad_x, g_attn_norm = _blockwise(
        "proj_dx_norm1_bwd", norm1_bwd_fn, [dproj, w["win"], x, dh1, attn_w],
        [_row_spec(t_row, in_pad), _full_spec((d, in_pad)), _row_spec(t_row, d), _row_spec(t_row, d), _full_spec((1, d))],
        [((seq, d), F32), ((1, d), F32)], [_row_spec(t_row, d), _full_spec((1, d))], g1, n_acc=1)

    grads = dict(
        attn_norm=g_attn_norm, win=g_win, lam_re=g_lr3, lam_im=g_li3, log_dt=g_ldt3,
        bt_re=g_bt_re, bt_im=g_bt_im, c_re=g_c_re, c_im=g_c_im,
        ssm_d=g_ssmd, wglu=g_wglu, b_glu=g_bglu, q_norm=g_qnorm, wuq=g_wuq, kv_norm=g_kvnorm, wukv=g_wukv,
        son=g_son, mon=g_mon, wout=g_wout, ffn_norm=g_ffn_norm, wup=g_wup, conv_w=g_convw, conv_b=g_convb,
        wdown=g_wdown, final_norm=g_final)
    return loss, grad_x, grads


def _mesh_pos():
    return lax.axis_index("x"), lax.axis_index("y"), lax.axis_index("c")


def _handshake_all():
    x, y, c = _mesh_pos()
    barrier = pltpu.get_barrier_semaphore()
    for k in range(1, N_DEV):
        peer = (1 - x if k & 4 else x, 1 - y if k & 2 else y, 1 - c if k & 1 else c)
        pl.semaphore_signal(barrier, inc=1, device_id=peer, device_id_type=MESH)
    pl.semaphore_wait(barrier, N_DEV - 1)


def _handshake(peers):
    barrier = pltpu.get_barrier_semaphore()
    for peer in peers:
        pl.semaphore_signal(barrier, inc=1, device_id=peer, device_id_type=MESH)
    pl.semaphore_wait(barrier, len(peers))


def _comm_call(name, body, n, out_shape, ins, collective_id, after=None, copies=7):
    sems = [pltpu.SemaphoreType.DMA((copies * n,)), pltpu.SemaphoreType.DMA((copies * n,)),
            pltpu.SemaphoreType.DMA((n,))]
    if collective_id is None:
        any_spec = pl.BlockSpec(memory_space=pl.ANY)
        return pl.pallas_call(body, name=name, out_shape=out_shape, in_specs=[any_spec] * n,
                              out_specs=[any_spec] * n, scratch_shapes=sems)(*ins)
    seq_body = body
    if after:
        n_after = len(after)
        ins = list(ins) + list(after)

        def seq_body(*refs):
            body(*refs[:n], *refs[n + n_after:])

    return pl.kernel(seq_body, name=name, out_type=out_shape,
                     mesh=plsc.ScalarSubcoreMesh(axis_name="seq", num_cores=1), scratch_types=sems,
                     compiler_params=pltpu.CompilerParams(collective_id=collective_id))(*ins)


def _all_gather(name, xs, collective_id=None, after=None):
    n = len(xs)

    def body(*refs):
        x_refs, o_refs = refs[:n], refs[n:2 * n]
        send_sems, recv_sems, local_sems = refs[2 * n:]
        if collective_id is not None:
            _handshake_all()
        x, y, c = _mesh_pos()
        me, sibling = (x, y, c), (x, y, 1 - c)
        chips = [(1 - x, y), (x, 1 - y), (1 - x, 1 - y)]

        def slot(o_ref, px, py, pc):
            return o_ref.at[4 * px + 2 * py + pc]

        def copy(t, k, block, to, src=None):
            dst = slot(o_refs[t], *block)
            return pltpu.make_async_remote_copy(
                src_ref=dst if src is None else src, dst_ref=dst,
                send_sem=send_sems.at[7 * t + k], recv_sem=recv_sems.at[7 * t + k],
                device_id=to, device_id_type=MESH)

        started = []
        for t in range(n):
            mine = pltpu.make_async_copy(x_refs[t], slot(o_refs[t], *me), local_sems.at[t])
            mine.start()
            started.append(mine)
        first = []
        for t in range(n):
            first.append(copy(t, 0, me, sibling, src=x_refs[t]))
            first += [copy(t, 1 + j, me, (*chip, c), src=x_refs[t]) for j, chip in enumerate(chips)]
        for cp in first:
            cp.start()
        passed = []
        for j, chip in enumerate(chips):
            for t in range(n):
                copy(t, 1 + j, (*chip, c), me).wait_recv()
                fwd = copy(t, 4 + j, (*chip, c), sibling)
                fwd.start()
                passed.append(fwd)
        for t in range(n):
            copy(t, 0, sibling, me).wait_recv()
            for j, chip in enumerate(chips):
                copy(t, 4 + j, (*chip, 1 - c), me).wait_recv()
        for cp in first + passed:
            cp.wait_send()
        for mine in started:
            mine.wait()

    out_shape = [jax.ShapeDtypeStruct((N_DEV,) + v.shape, v.dtype) for v in xs]
    return _comm_call(name, body, n, out_shape, xs, collective_id, after)


def _exchange_partials(name, gs, collective_id=None, after=None):
    n = len(gs)

    def body(*refs):
        g_refs, o_refs = refs[:n], refs[n:2 * n]
        send_sems, recv_sems, local_sems = refs[2 * n:]
        if collective_id is not None:
            _handshake_all()
        x, y, c = _mesh_pos()
        me_idx = 4 * x + 2 * y + c
        copies = []
        for t in range(n):
            mine = pltpu.make_async_copy(g_refs[t].at[me_idx], o_refs[t].at[me_idx], local_sems.at[t])
            mine.start()
            copies.append(mine)
        remote = []
        for k in range(1, N_DEV):
            px = 1 - x if k & 4 else x
            py = 1 - y if k & 2 else y
            pc = 1 - c if k & 1 else c
            p_idx = 4 * px + 2 * py + pc
            for t in range(n):
                cp = pltpu.make_async_remote_copy(
                    src_ref=g_refs[t].at[p_idx], dst_ref=o_refs[t].at[me_idx],
                    send_sem=send_sems.at[7 * t + k - 1], recv_sem=recv_sems.at[7 * t + k - 1],
                    device_id=(px, py, pc), device_id_type=MESH)
                cp.start()
                landing = pltpu.make_async_remote_copy(
                    src_ref=g_refs[t].at[p_idx], dst_ref=o_refs[t].at[p_idx],
                    send_sem=send_sems.at[7 * t + k - 1], recv_sem=recv_sems.at[7 * t + k - 1],
                    device_id=(px, py, pc), device_id_type=MESH)
                remote.append((cp, landing))
        for cp, landing in remote:
            landing.wait_recv()
        for cp, landing in remote:
            cp.wait_send()
        for mine in copies:
            mine.wait()

    out_shape = [jax.ShapeDtypeStruct(v.shape, v.dtype) for v in gs]
    return _comm_call(name, body, n, out_shape, gs, collective_id, after)


N_CHIP = N_DEV // 2


def _pair_swap(name, gs, collective_id, after=None):
    n = len(gs)

    def body(*refs):
        g_refs, o_refs = refs[:n], refs[n:2 * n]
        send_sems, recv_sems, _ = refs[2 * n:]
        x, y, c = _mesh_pos()
        sibling = (x, y, 1 - c)
        _handshake([sibling])
        copies = []
        for t in range(n):
            for k in range(N_CHIP):
                copies.append(pltpu.make_async_remote_copy(
                    src_ref=g_refs[t].at[2 * k + 1 - c], dst_ref=o_refs[t].at[k],
                    send_sem=send_sems.at[N_CHIP * t + k], recv_sem=recv_sems.at[N_CHIP * t + k],
                    device_id=sibling, device_id_type=MESH))
        for cp in copies:
            cp.start()
        for cp in copies:
            cp.wait_recv()
        for cp in copies:
            cp.wait_send()

    out_shape = [jax.ShapeDtypeStruct((N_CHIP,) + v.shape[1:], v.dtype) for v in gs]
    return _comm_call(name, body, n, out_shape, gs, collective_id, after, copies=N_CHIP)


def _pair_add(name, g, got):
    _, r, c = g.shape
    tr = r
    if r * c > ADAM_BLOCK_ELEMS and r % SUBLANES == 0:
        tr = SUBLANES
        while r % (tr * 2) == 0 and tr * 2 * c <= ADAM_BLOCK_ELEMS:
            tr *= 2

    def body(core_ref, g_ref, got_ref, o_ref):
        o_ref[...] = (g_ref[...].astype(F32) + got_ref[...].astype(F32)).astype(o_ref.dtype)

    grid_spec = pltpu.PrefetchScalarGridSpec(
        num_scalar_prefetch=1, grid=(N_CHIP, r // tr),
        in_specs=[pl.BlockSpec((None, None, tr, c), lambda k, i, core: (k, core[0], i, 0)),
                  pl.BlockSpec((None, tr, c), lambda k, i, core: (k, i, 0))],
        out_specs=pl.BlockSpec((None, tr, c), lambda k, i, core: (k, i, 0)))
    core = lax.axis_index("c").astype(jnp.int32).reshape(1)
    return pl.pallas_call(body, name=name, grid_spec=grid_spec, out_shape=jax.ShapeDtypeStruct((N_CHIP, r, c), g.dtype),
                          compiler_params=_cparams())(core, g.reshape(N_CHIP, 2, r, c), got)


def _chip_exchange(name, hs, collective_id, after=None):
    n = len(hs)
    per = N_CHIP - 1

    def body(*refs):
        h_refs, o_refs = refs[:n], refs[n:2 * n]
        send_sems, recv_sems, local_sems = refs[2 * n:]
        x, y, c = _mesh_pos()
        others = [(1 - x if k & 2 else x, 1 - y if k & 1 else y) for k in range(1, N_CHIP)]
        _handshake([(px, py, c) for px, py in others])
        my_chip = 2 * x + y
        local = []
        for t in range(n):
            mine = pltpu.make_async_copy(h_refs[t].at[my_chip], o_refs[t].at[my_chip], local_sems.at[t])
            mine.start()
            local.append(mine)
        remote = []
        for j, (px, py) in enumerate(others):
            chip = 2 * px + py
            for t in range(n):
                sems = dict(send_sem=send_sems.at[per * t + j], recv_sem=recv_sems.at[per * t + j],
                            device_id=(px, py, c), device_id_type=MESH)
                cp = pltpu.make_async_remote_copy(src_ref=h_refs[t].at[chip], dst_ref=o_refs[t].at[my_chip], **sems)
                cp.start()
                landing = pltpu.make_async_remote_copy(src_ref=h_refs[t].at[chip], dst_ref=o_refs[t].at[chip], **sems)
                remote.append((cp, landing))
        for cp, landing in remote:
            landing.wait_recv()
        for cp, landing in remote:
            cp.wait_send()
        for mine in local:
            mine.wait()

    out_shape = [jax.ShapeDtypeStruct(v.shape, v.dtype) for v in hs]
    return _comm_call(name, body, n, out_shape, hs, collective_id, after, copies=per)


ADAM_BLOCK_ELEMS = 128 * 1024


def _sum_parts(pb):
    g = pb[0].astype(F32)
    for j in range(1, pb.shape[0]):
        g = g + pb[j].astype(F32)
    return g


def _adam_math(g, wb_, mb, vb):
    m_new = ADAM_B1 * mb + (1.0 - ADAM_B1) * g
    v_new = ADAM_B2 * vb + (1.0 - ADAM_B2) * (g * g)
    m_hat = m_new / (1.0 - ADAM_B1 ** ADAM_STEP)
    v_hat = v_new / (1.0 - ADAM_B2 ** ADAM_STEP)
    delta = -ADAM_LR * (m_hat / (jnp.sqrt(v_hat) + ADAM_EPS) + ADAM_WD * wb_)
    return g, delta, m_new, v_new


def _adamw_multi(name, items, nblk=1):
    n = len(items)

    def body(*refs):
        for t in range(n):
            pr, wr, mr, vr = refs[4 * t:4 * t + 4]
            res = _adam_math(_sum_parts(pr[...]), wr[...], mr[...], vr[...])
            for o, val in zip(refs[4 * n + 4 * t:4 * n + 4 * t + 4], res):
                o[...] = val

    def spec(shape, lead):
        blk = list(shape)
        blk[lead + 1] = shape[lead + 1] // nblk
        if nblk == 1:
            return pl.BlockSpec(tuple(blk), lambda i, nd=len(shape): (0,) * nd)
        return pl.BlockSpec(tuple(blk), lambda i, nd=len(shape), ax=lead + 1: (0,) * ax + (i,) + (0,) * (nd - ax - 1))

    in_specs, out_specs, out_shape, ins = [], [], [], []
    for parts, wv, mv, vv in items:
        assert parts.shape[1:] == wv.shape, (name, parts.shape, wv.shape)
        ins += [parts, wv, mv, vv]
        in_specs += [spec(parts.shape, 1)] + [spec(wv.shape, 0)] * 3
        out_specs += [spec(wv.shape, 0)] * 4
        out_shape += [jax.ShapeDtypeStruct(wv.shape, F32)] * 4
    res = pl.pallas_call(body, name=name, grid=(nblk,), in_specs=in_specs, out_specs=out_specs, out_shape=out_shape,
                         compiler_params=_cparams())(*ins)
    return [tuple(res[4 * t:4 * t + 4]) for t in range(n)]


def _sum_multi(name, parts_list):
    def body(*refs):
        for pr, o in zip(refs[:len(parts_list)], refs[len(parts_list):]):
            o[...] = _sum_parts(pr[...])

    return pl.pallas_call(body, name=name, out_shape=[jax.ShapeDtypeStruct(p.shape[1:], F32) for p in parts_list],
                          compiler_params=_cparams())(*parts_list)


def _adamw_sum(name, parts, wv, mv, vv):
    npart, r, c = parts.shape
    tr = r
    if r * c > ADAM_BLOCK_ELEMS and r % SUBLANES == 0:
        tr = SUBLANES
        while r % (tr * 2) == 0 and tr * 2 * c <= ADAM_BLOCK_ELEMS:
            tr *= 2

    def fn(pb, wb_, mb, vb):
        return _adam_math(_sum_parts(pb), wb_, mb, vb)

    row = pl.BlockSpec((tr, c), lambda i: (i, 0))
    return _blockwise(name, fn, [parts, wv, mv, vv],
                      [pl.BlockSpec((npart, tr, c), lambda i: (0, i, 0)), row, row, row],
                      [((r, c), F32)] * 4, [row] * 4, (r // tr,))


_VECTORS = ["attn_norm", "lam_re", "lam_im", "log_dt", "ssm_d", "b_glu", "q_norm", "kv_norm", "son", "mon",
            "ffn_norm", "conv_b", "final_norm"]
_BIG = ["win", "wglu", "wuq", "wukv", "wout", "wup", "wdown", "conv_w"]
_ORDER = ["attn_norm", "win", "lam_re", "lam_im", "log_dt", "b_re", "b_im", "c_re", "c_im", "ssm_d", "wglu",
          "b_glu", "q_norm", "wuq", "kv_norm", "wukv", "son", "mon", "wout", "ffn_norm", "wup", "conv_w",
          "conv_b", "wdown", "final_norm"]


def kernel(x, positions, attn_norm_w, w_in, ssm_lambda_re, ssm_lambda_im, ssm_log_dt, ssm_b_re, ssm_b_im, ssm_c_re, ssm_c_im, ssm_d, ssm_w_glu, ssm_b_glu, mla_q_norm_w, mla_w_uq, mla_kv_norm_w, mla_w_ukv, ssm_out_norm_w, mla_out_norm_w, w_out, ffn_norm_w, ffn_w_up, ffn_conv_w, ffn_conv_b, ffn_w_down, final_norm_w, loss_target, m_attn_norm_w, m_w_in, m_ssm_lambda_re, m_ssm_lambda_im, m_ssm_log_dt, m_ssm_b_re, m_ssm_b_im, m_ssm_c_re, m_ssm_c_im, m_ssm_d, m_ssm_w_glu, m_ssm_b_glu, m_mla_q_norm_w, m_mla_w_uq, m_mla_kv_norm_w, m_mla_w_ukv, m_ssm_out_norm_w, m_mla_out_norm_w, m_w_out, m_ffn_norm_w, m_ffn_w_up, m_ffn_conv_w, m_ffn_conv_b, m_ffn_w_down, m_final_norm_w, v_attn_norm_w, v_w_in, v_ssm_lambda_re, v_ssm_lambda_im, v_ssm_log_dt, v_ssm_b_re, v_ssm_b_im, v_ssm_c_re, v_ssm_c_im, v_ssm_d, v_ssm_w_glu, v_ssm_b_glu, v_mla_q_norm_w, v_mla_w_uq, v_mla_kv_norm_w, v_mla_w_ukv, v_ssm_out_norm_w, v_mla_out_norm_w, v_w_out, v_ffn_norm_w, v_ffn_w_up, v_ffn_conv_w, v_ffn_conv_b, v_ffn_w_down, v_final_norm_w):
    wts = dict(attn_norm=attn_norm_w, win=w_in, lam_re=ssm_lambda_re, lam_im=ssm_lambda_im, log_dt=ssm_log_dt,
               b_re=ssm_b_re, b_im=ssm_b_im, c_re=ssm_c_re, c_im=ssm_c_im, ssm_d=ssm_d, wglu=ssm_w_glu,
               b_glu=ssm_b_glu, q_norm=mla_q_norm_w, wuq=mla_w_uq, kv_norm=mla_kv_norm_w, wukv=mla_w_ukv,
               son=ssm_out_norm_w, mon=mla_out_norm_w, wout=w_out, ffn_norm=ffn_norm_w, wup=ffn_w_up,
               conv_w=ffn_conv_w, conv_b=ffn_conv_b, wdown=ffn_w_down, final_norm=final_norm_w)
    moms = dict(zip(_ORDER, [m_attn_norm_w, m_w_in, m_ssm_lambda_re, m_ssm_lambda_im, m_ssm_log_dt, m_ssm_b_re,
                             m_ssm_b_im, m_ssm_c_re, m_ssm_c_im, m_ssm_d, m_ssm_w_glu, m_ssm_b_glu, m_mla_q_norm_w,
                             m_mla_w_uq, m_mla_kv_norm_w, m_mla_w_ukv, m_ssm_out_norm_w, m_mla_out_norm_w, m_w_out,
                             m_ffn_norm_w, m_ffn_w_up, m_ffn_conv_w, m_ffn_conv_b, m_ffn_w_down, m_final_norm_w]))
    vels = dict(zip(_ORDER, [v_attn_norm_w, v_w_in, v_ssm_lambda_re, v_ssm_lambda_im, v_ssm_log_dt, v_ssm_b_re,
                             v_ssm_b_im, v_ssm_c_re, v_ssm_c_im, v_ssm_d, v_ssm_w_glu, v_ssm_b_glu, v_mla_q_norm_w,
                             v_mla_w_uq, v_mla_kv_norm_w, v_mla_w_ukv, v_ssm_out_norm_w, v_mla_out_norm_w, v_w_out,
                             v_ffn_norm_w, v_ffn_w_up, v_ffn_conv_w, v_ffn_conv_b, v_ffn_w_down, v_final_norm_w]))
    seq, d = x.shape[1], x.shape[2]
    in_width = w_in.shape[2]
    in_pad = -(-in_width // LANES) * LANES
    q_cols = mla_w_uq.shape[2]
    q_pad = 2 * LANES

    (win_g,) = _all_gather("gather_w_in", [jnp.pad(w_in[0], ((0, 0), (0, in_pad - in_width))).astype(BF16)])
    wglu_g, wuq_g, wukv_g, wout_g, convw_g = _all_gather(
        "gather_mix", [ssm_w_glu[0].astype(BF16), jnp.pad(mla_w_uq[0], ((0, 0), (0, q_pad - q_cols))).astype(BF16),
                       mla_w_ukv[0].astype(BF16), w_out[0].astype(BF16), ffn_conv_w[0]], collective_id=0)
    (wup_g,) = _all_gather("gather_ffn_up", [ffn_w_up[0].astype(BF16)], collective_id=1)
    (wdown_g,) = _all_gather("gather_ffn_down", [ffn_w_down[0].astype(BF16)], collective_id=2)
    ns = N_DEV
    c_ff = wup_g.shape[2]
    w = dict(
        attn_norm=attn_norm_w, win=win_g.reshape(d, in_pad), lam_re=ssm_lambda_re, lam_im=ssm_lambda_im,
        log_dt=ssm_log_dt, b_re=ssm_b_re, b_im=ssm_b_im, c_re=ssm_c_re, c_im=ssm_c_im, ssm_d=ssm_d,
        wglu=wglu_g.reshape(d // 2, d // 2), b_glu=ssm_b_glu, q_norm=mla_q_norm_w, wuq=wuq_g,
        kv_norm=mla_kv_norm_w, wukv=wukv_g, son=ssm_out_norm_w, mon=mla_out_norm_w, wout=wout_g.reshape(d, d),
        ffn_norm=ffn_norm_w, wup=wup_g, conv_w=convw_g, conv_b=ffn_conv_b,
        wdown=wdown_g.reshape(ns // 2 * c_ff, d), final_norm=final_norm_w)

    shard_layout = dict(
        win=lambda a: a[:, :in_width].reshape(N_DEV, d // N_DEV, in_width),
        wglu=lambda a: a.reshape(N_DEV, d // 2 // N_DEV, d // 2),
        wuq=lambda a: a[:, :, :q_cols], wukv=lambda a: a, wout=lambda a: a.reshape(N_DEV, d // N_DEV, d),
        wup=lambda a: a, wdown=lambda a: a.reshape(N_DEV, c_ff // 2, d), conv_w=lambda a: a)
    recv = {}
    next_id = [3]

    last = [None]

    out = {}

    def update(k):
        shp = wts[k].shape
        r, c = shp[-2], shp[-1]
        res = _adamw_sum("adamw_" + k, recv[k].reshape(-1, r, c), wts[k].reshape(r, c),
                         moms[k].reshape(r, c), vels[k].reshape(r, c))
        out[k] = [a.reshape(shp) for a in res]
        return res[0]

    pending = {}

    def exchange(not_before=(), **grads):
        names = list(grads)
        if names == ["wup"]:
            pending["g"] = shard_layout["wup"](grads["wup"])
            pending["got"] = _pair_swap("swap_wup", [pending["g"]], collective_id=next_id[0], after=[last[0]])[0]
            next_id[0] += 1
            last[0] = pending["got"]
            return
        if names == ["wup_pair_sums_after"]:
            sums = _pair_add("pair_add_wup", pending["g"], pending["got"])
            recv["wup"] = _chip_exchange("exchange_wup", [sums], collective_id=next_id[0],
                                         after=[last[0], grads["wup_pair_sums_after"]])[0]
            next_id[0] += 1
            last[0] = recv["wup"]
            return
        if "wout" in names:
            not_before = (*not_before, update("wdown"))
        got = _exchange_partials("exchange_" + "_".join(names), [shard_layout[k](grads[k]) for k in names],
                                 collective_id=next_id[0], after=[a for a in (last[0], *not_before) if a is not None])
        next_id[0] += 1
        last[0] = got[-1]
        recv.update(zip(names, got))

    loss_part, grad_x, g = _local_step(x[0], positions[0], loss_target[0], w, emit=exchange)
    loss = lax.psum(loss_part, ("x", "y", "c"))
    n_groups = ssm_lambda_re.shape[1]
    two_d = {"lam_re": (n_groups, -1), "lam_im": (n_groups, -1)}
    dense = {k: g[k].reshape(two_d.get(k, (1, -1))) for k in _VECTORS}
    dense.update(c_re=g["c_re"], c_im=g["c_im"], bt_re=g["bt_re"], bt_im=g["bt_im"])
    names = list(dense)
    gathered = dict(zip(names, _all_gather("gather_small_grads", [dense[k] for k in names],
                                           collective_id=next_id[0], after=[last[0]])))
    for k in _BIG:
        if k not in out:
            update(k)

    def finish(keys, results):
        for k, res in zip(keys, results):
            out[k] = [a.reshape(wts[k].shape) for a in res]

    view = lambda k, a: a.reshape(dense[k].shape)
    finish(_VECTORS, _adamw_multi("adamw_vectors", [(gathered[k], view(k, wts[k]), view(k, moms[k]), view(k, vels[k]))
                                                    for k in _VECTORS]))
    c_keys = ["c_re", "c_im"]
    finish(c_keys, _adamw_multi("adamw_ssm_c", [(gathered[k][:, None], wts[k], moms[k], vels[k]) for k in c_keys]))
    b_sums = _sum_multi("sum_ssm_b", [gathered["bt_re"], gathered["bt_im"]])
    b_keys = ["b_re", "b_im"]
    finish(b_keys, _adamw_multi("adamw_ssm_b", [(jnp.swapaxes(s, 1, 2)[None, None], wts[k], moms[k], vels[k])
                                                for k, s in zip(b_keys, b_sums)], nblk=SUBLANES))

    grad_x = grad_x.reshape(x.shape)
    return (loss, grad_x, *[out[k][0] for k in _ORDER], *[out[k][1] for k in _ORDER],
            *[out[k][2] for k in _ORDER], *[out[k][3] for k in _ORDER])
```

```python
import functools
import math

import jax
import jax.numpy as jnp
from jax import lax
from jax.experimental import pallas as pl
from jax.experimental.pallas import tpu as pltpu
from jax.experimental.pallas import tpu_sc as plsc

F32 = jnp.float32
BF16 = jnp.bfloat16
MESH = pl.DeviceIdType.MESH

N_DEV = 8
LANES = 128
SUBLANES = 8
VMEM_LIMIT = 48 * 1024 * 1024

SSM_GROUP = 16
SSM_STATE = 64
GROUPS_PER_BLOCK = LANES // SSM_GROUP
STATE_BLOCK = GROUPS_PER_BLOCK * SSM_STATE
QK_NOPE = 128
QK_ROPE = 64
V_DIM = 128
ROPE_THETA = 10000.0
RMS_EPS = 1e-6

ADAM_LR = 0.001
ADAM_B1 = 0.9
ADAM_B2 = 0.999
ADAM_EPS = 1e-08
ADAM_WD = 0.01
ADAM_STEP = 10

NN = ((1,), (0,))
NT = ((1,), (1,))
TN = ((0,), (0,))


def _cparams():
    return pltpu.CompilerParams(vmem_limit_bytes=VMEM_LIMIT)


def _tile(n, want):
    if n <= want:
        return n
    t = (want // LANES) * LANES
    while t >= LANES:
        if n % t == 0:
            return t
        t -= LANES
    return n


def _mm(name, a, b, *, grid, a_spec, b_spec, o_spec, out_shape, out_dtype, contract=NN,
        res=None, res_spec=None):
    nk = grid[-1]
    kaxis = len(grid) - 1
    acc_shape = tuple(d for d in o_spec.block_shape if d is not None)

    def body(*refs):
        a_ref, b_ref = refs[:2]
        r_ref = None if res is None else refs[2]
        o_ref = refs[2 if res is None else 3]
        part = lax.dot_general(a_ref[...].astype(BF16), b_ref[...].astype(BF16),
                               (contract, ((), ())), preferred_element_type=F32)
        if nk == 1:
            if r_ref is not None:
                part = part + r_ref[...].astype(F32)
            o_ref[...] = part.astype(o_ref.dtype)
            return
        acc = refs[-1]
        k = pl.program_id(kaxis)

        @pl.when(k == 0)
        def _():
            acc[...] = part

        @pl.when(k != 0)
        def _():
            acc[...] += part

        @pl.when(k == nk - 1)
        def _():
            r = acc[...]
            if r_ref is not None:
                r = r + r_ref[...].astype(F32)
            o_ref[...] = r.astype(o_ref.dtype)

    ins = [a, b] + ([] if res is None else [res])
    in_specs = [a_spec, b_spec] + ([] if res is None else [res_spec])
    return pl.pallas_call(
        body, name=name, grid=grid, in_specs=in_specs, out_specs=o_spec,
        out_shape=jax.ShapeDtypeStruct(out_shape, out_dtype),
        scratch_shapes=[pltpu.VMEM(acc_shape, F32)] if nk > 1 else [], compiler_params=_cparams(),
    )(*ins)


def _mm2d(name, a, b, contract, out_dtype, tm=1024, tn=1024, tk=2048, res=None):
    if contract == NN:
        (m, kk), n = a.shape, b.shape[1]
    elif contract == NT:
        (m, kk), n = a.shape, b.shape[0]
    else:
        (kk, m), n = a.shape, b.shape[1]
    tm, tn, tk = _tile(m, tm), _tile(n, tn), _tile(kk, tk)
    grid = (m // tm, n // tn, kk // tk)
    if contract == TN:
        a_spec = pl.BlockSpec((tk, tm), lambda i, j, k: (k, i))
    else:
        a_spec = pl.BlockSpec((tm, tk), lambda i, j, k: (i, k))
    if contract == NT:
        b_spec = pl.BlockSpec((tn, tk), lambda i, j, k: (j, k))
    else:
        b_spec = pl.BlockSpec((tk, tn), lambda i, j, k: (k, j))
    o_spec = pl.BlockSpec((tm, tn), lambda i, j, k: (i, j))
    res_spec = None
    if res is not None:
        if res.shape[0] == 1:
            res_spec = pl.BlockSpec((1, tn), lambda i, j, k: (0, j))
        else:
            res_spec = pl.BlockSpec((tm, tn), lambda i, j, k: (i, j))
    return _mm(name, a, b, grid=grid, a_spec=a_spec, b_spec=b_spec, o_spec=o_spec,
               out_shape=(m, n), out_dtype=out_dtype, contract=contract, res=res, res_spec=res_spec)


def _blockwise(name, fn, ins, in_specs, outs, out_specs, grid, n_acc=0, acc_all=True):
    n_in, n_out = len(ins), len(outs)
    n_plain = n_out - n_acc

    def body(*refs):
        vals = fn(*[r[...] for r in refs[:n_in]])
        if not isinstance(vals, (tuple, list)):
            vals = (vals,)
        o_refs = refs[n_in:n_in + n_out]
        for r, v in zip(o_refs[:n_plain], vals[:n_plain]):
            r[...] = v.astype(r.dtype)
        if n_acc:
            if acc_all:
                first = functools.reduce(jnp.logical_and, [pl.program_id(d) == 0 for d in range(len(grid))])
            else:
                first = pl.program_id(len(grid) - 1) == 0

            @pl.when(first)
            def _():
                for r, v in zip(o_refs[n_plain:], vals[n_plain:]):
                    r[...] = v.astype(r.dtype)

            @pl.when(jnp.logical_not(first))
            def _():
                for r, v in zip(o_refs[n_plain:], vals[n_plain:]):
                    r[...] += v.astype(r.dtype)

    return pl.pallas_call(
        body, name=name, grid=grid, in_specs=in_specs, out_specs=out_specs,
        out_shape=[jax.ShapeDtypeStruct(s, d) for s, d in outs], compiler_params=_cparams(),
    )(*ins)


def _row_spec(t, c):
    return pl.BlockSpec((t, c), lambda i: (i, 0))


def _full_spec(shape):
    nd = len(shape)
    return pl.BlockSpec(tuple(shape), lambda *g: (0,) * nd)


def _rms(xf, w):
    return xf * lax.rsqrt(jnp.mean(xf * xf, axis=-1, keepdims=True) + RMS_EPS) * w


def _rms_bwd(xf, w, dy):
    _, vjp = jax.vjp(_rms, xf, w)
    return vjp(dy)


def _s5_disc(lr, li, ldt, bre, bim):
    dt = jnp.exp(ldt)
    mag = jnp.exp(lr * dt)
    ar = mag * jnp.cos(li * dt)
    ai = mag * jnp.sin(li * dt)
    nr, ni = ar - 1.0, ai
    den = lr * lr + li * li
    zr = (nr * lr + ni * li) / den
    zi = (ni * lr - nr * li) / den
    return ar, ai, zr * bre - zi * bim, zr * bim + zi * bre


def _s5_prep(lr, li, ldt, bre, bim):
    def body(lr_r, li_r, ldt_r, bre_r, bim_r, ar_r, ai_r, br_r, bi_r):
        ar, ai, br, bi = _s5_disc(lr_r[...], li_r[...], ldt_r[...], bre_r[...], bim_r[...])
        ar_r[...] = ar
        ai_r[...] = ai
        br_r[...] = br
        bi_r[...] = bi

    sd = jax.ShapeDtypeStruct
    return pl.pallas_call(
        body, name="s5_prep",
        out_shape=[sd(lr.shape, F32), sd(lr.shape, F32), sd(bre.shape, F32), sd(bre.shape, F32)],
        compiler_params=_cparams(),
    )(lr, li, ldt, bre, bim)


def _s5_prep_bwd(lr, li, ldt, bre, bim, dar, dai, dbr, dbi):
    def body(lr_r, li_r, ldt_r, bre_r, bim_r, dar_r, dai_r, dbr_r, dbi_r, o0, o1, o2, o3, o4):
        _, vjp = jax.vjp(_s5_disc, lr_r[...], li_r[...], ldt_r[...], bre_r[...], bim_r[...])
        g = vjp((dar_r[...], dai_r[...], dbr_r[...], dbi_r[...]))
        for o, v in zip((o0, o1, o2, o3, o4), g):
            o[...] = v

    sd = jax.ShapeDtypeStruct
    return pl.pallas_call(
        body, name="s5_prep_bwd",
        out_shape=[sd(lr.shape, F32), sd(li.shape, F32), sd(ldt.shape, F32), sd(bre.shape, F32), sd(bim.shape, F32)],
        compiler_params=_cparams(),
    )(lr, li, ldt, bre, bim, dar, dai, dbr, dbi)


SCAN_T = 256


def _scan_tables(ar, ai, tab_r, tab_i, sub, reverse):
    pr, pi = ar, ai
    for k in range(sub):
        row = sub - 1 - k if reverse else k
        tab_r[row:row + 1, :] = pr
        tab_i[row:row + 1, :] = pi
        pr, pi = ar * pr - ai * pi, ar * pi + ai * pr


def _pack_matrix(t_blk, dtype):
    sub = t_blk // SUBLANES
    dst = jnp.arange(t_blk)
    src = (dst % SUBLANES) * sub + dst // SUBLANES
    return (src[:, None] == jnp.arange(t_blk)[None, :]).astype(dtype)


def _permute_rows_f32(pm, x):
    hi = x.astype(BF16)
    r1 = x - hi.astype(F32)
    mid = r1.astype(BF16)
    lo = (r1 - mid.astype(F32)).astype(BF16)
    dot = lambda v: jnp.dot(pm, v, preferred_element_type=F32)
    return dot(hi) + dot(mid) + dot(lo)


def _scan_block(x, loc, ar, ai, st, tab_r, tab_i, sub, reverse):
    hb = STATE_BLOCK
    a8r = jnp.broadcast_to(ar, (SUBLANES, hb))
    a8i = jnp.broadcast_to(ai, (SUBLANES, hb))
    sr = jnp.zeros((SUBLANES, hb), F32)
    si = jnp.zeros((SUBLANES, hb), F32)
    steps = range(sub - 1, -1, -1) if reverse else range(sub)
    for t in steps:
        rows = slice(t * SUBLANES, (t + 1) * SUBLANES)
        sr, si = a8r * sr - a8i * si + x[rows, :hb], a8r * si + a8i * sr + x[rows, hb:]
        loc[rows, :hb] = sr
        loc[rows, hb:] = si
    cr, ci = st[0:1, :], st[1:2, :]
    far = 0 if reverse else sub - 1
    fr, fi = tab_r[far:far + 1, :], tab_i[far:far + 1, :]
    ent_r, ent_i = [None] * SUBLANES, [None] * SUBLANES
    for c in (range(SUBLANES - 1, -1, -1) if reverse else range(SUBLANES)):
        ent_r[c], ent_i[c] = cr, ci
        cr, ci = sr[c:c + 1, :] + (fr * cr - fi * ci), si[c:c + 1, :] + (fr * ci + fi * cr)
    st[0:1, :] = cr
    st[1:2, :] = ci
    c8r = jnp.concatenate(ent_r, axis=0)
    c8i = jnp.concatenate(ent_i, axis=0)
    out = []
    for t in range(sub):
        rows = slice(t * SUBLANES, (t + 1) * SUBLANES)
        tr, ti = tab_r[t:t + 1, :], tab_i[t:t + 1, :]
        out.append(jnp.concatenate([loc[rows, :hb] + (tr * c8r - ti * c8i), loc[rows, hb:] + (tr * c8i + ti * c8r)],
                                   axis=1))
    return jnp.concatenate(out, axis=0)


SSM_BLOCKS_PER_STEP = 2


def _scan_scratch(nblk, t_blk, sub, hb):
    return [pltpu.VMEM((nblk, SUBLANES, hb), F32), pltpu.VMEM((nblk, sub, hb), F32), pltpu.VMEM((nblk, sub, hb), F32),
            pltpu.VMEM((nblk, t_blk, 2 * hb), F32)]


def _ssm_fwd(proj, wb, wc, a):
    seq = proj.shape[0]
    nj = wb.shape[0]
    w2 = 2 * STATE_BLOCK
    hb = STATE_BLOCK
    t_blk = min(SCAN_T, seq)
    sub = t_blk // SUBLANES
    pm = _pack_matrix(t_blk, BF16)

    npair = SSM_BLOCKS_PER_STEP

    def body(u_ref, wb_ref, wc_ref, a_ref, pm_ref, pmt_ref, s_ref, y_ref, st, tab_r, tab_i, loc):
        coef = [(a_ref[:, b * w2:b * w2 + hb], a_ref[:, b * w2 + hb:(b + 1) * w2]) for b in range(npair)]

        @pl.when(pl.program_id(1) == 0)
        def _():
            for b, (ar, ai) in enumerate(coef):
                st[b] = jnp.zeros((SUBLANES, hb), F32)
                _scan_tables(ar, ai, tab_r.at[b], tab_i.at[b], sub, False)

        for b, (ar, ai) in enumerate(coef):
            ub = u_ref[:, b * LANES:(b + 1) * LANES].astype(BF16)
            up = jnp.dot(pm_ref[...], ub, preferred_element_type=F32).astype(BF16)
            bu = jnp.dot(up, wb_ref[b], preferred_element_type=F32)
            s = _scan_block(bu, loc.at[b], ar, ai, st.at[b], tab_r.at[b], tab_i.at[b], sub, False)
            s_ref[:, b * w2:(b + 1) * w2] = s
            yp = jnp.dot(s.astype(BF16), wc_ref[b], preferred_element_type=F32)
            y_ref[:, b * LANES:(b + 1) * LANES] = _permute_rows_f32(pmt_ref[...], yp)

    sd = jax.ShapeDtypeStruct
    return pl.pallas_call(
        body, name="ssm_fwd", grid=(nj // npair, seq // t_blk),
        in_specs=[pl.BlockSpec((t_blk, npair * LANES), lambda j, i: (i, j)),
                  pl.BlockSpec((npair, LANES, w2), lambda j, i: (j, 0, 0)),
                  pl.BlockSpec((npair, w2, LANES), lambda j, i: (j, 0, 0)),
                  pl.BlockSpec((1, npair * w2), lambda j, i: (0, j)),
                  _full_spec((t_blk, t_blk)), _full_spec((t_blk, t_blk))],
        out_specs=[pl.BlockSpec((t_blk, npair * w2), lambda j, i: (i, j)),
                   pl.BlockSpec((t_blk, npair * LANES), lambda j, i: (i, j))],
        out_shape=[sd((seq, nj * w2), F32), sd((seq, nj * LANES), F32)],
        scratch_shapes=_scan_scratch(npair, t_blk, sub, hb), compiler_params=_cparams(),
    )(proj, wb, wc, a, pm, pm.T)


def _ssm_bwd(dy, s, proj, du1, wb, wc, a):
    seq = dy.shape[0]
    nj = wb.shape[0]
    w2 = 2 * STATE_BLOCK
    hb = STATE_BLOCK
    t_blk = min(SCAN_T, seq)
    sub = t_blk // SUBLANES
    nb = seq // t_blk
    pm = _pack_matrix(t_blk, BF16)

    npair = SSM_BLOCKS_PER_STEP

    def body(dy_ref, s_ref, sprev_ref, u_ref, du1_ref, wb_ref, wc_ref, a_ref, pm_ref, pmt_ref,
             du_ref, dwb_ref, dwc_ref, da_ref, st, tab_r, tab_i, loc):
        ib = pl.program_id(1)
        pmv = pm_ref[...]
        coef = [(a_ref[:, b * w2:b * w2 + hb], -a_ref[:, b * w2 + hb:(b + 1) * w2]) for b in range(npair)]

        @pl.when(ib == 0)
        def _():
            for b, (ar, ai) in enumerate(coef):
                st[b] = jnp.zeros((SUBLANES, hb), F32)
                _scan_tables(ar, ai, tab_r.at[b], tab_i.at[b], sub, True)

        sums = []
        for b, (ar, ai) in enumerate(coef):
            cols, wide = slice(b * LANES, (b + 1) * LANES), slice(b * w2, (b + 1) * w2)
            dyp = jnp.dot(pmv, dy_ref[:, cols], preferred_element_type=F32).astype(BF16)
            up = jnp.dot(pmv, u_ref[:, cols].astype(BF16), preferred_element_type=F32).astype(BF16)
            ds = lax.dot_general(dyp, wc_ref[b], (NT, ((), ())), preferred_element_type=F32)
            lam = _scan_block(ds, loc.at[b], ar, ai, st.at[b], tab_r.at[b], tab_i.at[b], sub, True)
            lamb = lam.astype(BF16)
            du = lax.dot_general(lamb, wb_ref[b], (NT, ((), ())), preferred_element_type=F32)
            du_ref[:, cols] = (_permute_rows_f32(pmt_ref[...], du) + du1_ref[:, cols]).astype(du_ref.dtype)
            sv = s_ref[:, wide]
            dwb = lax.dot_general(up, lamb, (TN, ((), ())), preferred_element_type=F32)
            dwc = lax.dot_general(sv.astype(BF16), dyp, (TN, ((), ())), preferred_element_type=F32)

            prev_last = sprev_ref[SUBLANES - 1:SUBLANES, wide]
            prev_last = jnp.where(ib == nb - 1, jnp.zeros_like(prev_last), prev_last)
            tail = sv[t_blk - SUBLANES:, :]
            sl = lax.broadcasted_iota(jnp.int32, tail.shape, 0)
            head = jnp.where(sl >= 1, pltpu.roll(tail, 1, 0), prev_last)
            s_sh = jnp.concatenate([head, sv[:t_blk - SUBLANES, :]], axis=0)
            lam_r, lam_i = lam[:, :hb], lam[:, hb:]
            sr_, si_ = s_sh[:, :hb], s_sh[:, hb:]
            dar = jnp.sum(lam_r * sr_ + lam_i * si_, axis=0, keepdims=True)
            dai = jnp.sum(lam_i * sr_ - lam_r * si_, axis=0, keepdims=True)
            sums.append((wide, jnp.concatenate([dar, dai], axis=1), dwb, dwc))

        @pl.when(ib == 0)
        def _():
            for b, (wide, contrib, dwb, dwc) in enumerate(sums):
                da_ref[:, wide] = contrib
                dwb_ref[b] = dwb
                dwc_ref[b] = dwc

        @pl.when(ib != 0)
        def _():
            for b, (wide, contrib, dwb, dwc) in enumerate(sums):
                da_ref[:, wide] += contrib
                dwb_ref[b] += dwb
                dwc_ref[b] += dwc

    blk = lambda j, i: (nb - 1 - i, j)
    prev_blk = lambda j, i: (jnp.maximum((nb - 1 - i) * sub - 1, 0), j)
    sd = jax.ShapeDtypeStruct
    return pl.pallas_call(
        body, name="ssm_bwd", grid=(nj // npair, nb),
        in_specs=[pl.BlockSpec((t_blk, npair * LANES), blk), pl.BlockSpec((t_blk, npair * w2), blk),
                  pl.BlockSpec((SUBLANES, npair * w2), prev_blk), pl.BlockSpec((t_blk, npair * LANES), blk),
                  pl.BlockSpec((t_blk, npair * LANES), blk),
                  pl.BlockSpec((npair, LANES, w2), lambda j, i: (j, 0, 0)),
                  pl.BlockSpec((npair, w2, LANES), lambda j, i: (j, 0, 0)),
                  pl.BlockSpec((1, npair * w2), lambda j, i: (0, j)),
                  _full_spec((t_blk, t_blk)), _full_spec((t_blk, t_blk))],
        out_specs=[pl.BlockSpec((t_blk, npair * LANES), blk),
                   pl.BlockSpec((npair, LANES, w2), lambda j, i: (j, 0, 0)),
                   pl.BlockSpec((npair, w2, LANES), lambda j, i: (j, 0, 0)),
                   pl.BlockSpec((1, npair * w2), lambda j, i: (0, j))],
        out_shape=[sd((seq, nj * LANES), BF16), sd((nj, LANES, w2), F32), sd((nj, w2, LANES), F32),
                   sd((1, nj * w2), F32)],
        scratch_shapes=_scan_scratch(npair, t_blk, sub, hb), compiler_params=_cparams(),
    )(dy, s, s, proj, du1, wb, wc, a, pm, pm.T)


def _rope128(x, cos, sa, sb):
    return x * cos + pltpu.roll(x, 96, 1) * sa + pltpu.roll(x, 32, 1) * sb


def _rope128_t(dy, cos, sa, sb):
    return dy * cos + pltpu.roll(dy * sa, 32, 1) + pltpu.roll(dy * sb, 96, 1)


ATT_BQ = 256


def _probs(qn, qp, kn, kp, r0, scale):
    s = lax.dot_general(qn, kn, (NT, ((), ())), preferred_element_type=F32)
    s = s + lax.dot_general(qp, kp, (NT, ((), ())), preferred_element_type=F32)
    s = s * scale
    diag = s[:, r0:]
    row = lax.broadcasted_iota(jnp.int32, diag.shape, 0)
    col = lax.broadcasted_iota(jnp.int32, diag.shape, 1)
    diag = jnp.where(col <= row, diag, jnp.finfo(F32).min)
    s = diag if r0 == 0 else jnp.concatenate([s[:, :r0], diag], axis=1)
    m = jnp.max(s, axis=-1, keepdims=True)
    e = jnp.exp(s - m)
    return e / jnp.sum(e, axis=-1, keepdims=True)


def _attn_specs(seq):
    tab = pl.BlockSpec((seq, LANES), lambda h: (0, 0))
    return [pl.BlockSpec((None, seq, 256), lambda h: (h, 0, 0)), pl.BlockSpec((None, seq, 128), lambda h: (h, 0, 0)),
            pl.BlockSpec((None, seq, 128), lambda h: (h, 0, 1)), tab, tab, tab, tab]


def _attn_fwd(q_raw, kv, kpe, cos, sa, sb):
    nh, seq, _ = q_raw.shape
    bq = min(ATT_BQ, seq)
    scale = (QK_NOPE + QK_ROPE) ** -0.5

    def body(q_ref, kn_ref, v_ref, kp_ref, cos_ref, sa_ref, sb_ref, o_ref):
        for r0 in range(0, seq, bq):
            rows, kend = pl.ds(r0, bq), r0 + bq
            qn = q_ref[rows, :QK_NOPE].astype(BF16)
            qp = _rope128(q_ref[rows, QK_NOPE:], cos_ref[rows, :], sa_ref[rows, :], sb_ref[rows, :]).astype(BF16)
            p = _probs(qn, qp, kn_ref[:kend, :], kp_ref[:kend, :], r0, scale)
            o_ref[rows, :] = jnp.dot(p.astype(BF16), v_ref[:kend, :], preferred_element_type=F32)

    return pl.pallas_call(
        body, name="attn_fwd", grid=(nh,), in_specs=_attn_specs(seq),
        out_specs=pl.BlockSpec((seq, V_DIM), lambda h: (0, h)),
        out_shape=jax.ShapeDtypeStruct((seq, nh * V_DIM), F32), compiler_params=_cparams(),
    )(q_raw, kv, kv, kpe, cos, sa, sb)


def _attn_bwd(q_raw, kv, kpe, cos, sa, sb, do):
    nh, seq, _ = q_raw.shape
    bq = min(ATT_BQ, seq)
    scale = (QK_NOPE + QK_ROPE) ** -0.5

    def body(q_ref, kn_ref, v_ref, kp_ref, cos_ref, sa_ref, sb_ref, do_ref, dq_ref, dkv_ref, dkp_ref):
        dkv_ref[...] = jnp.zeros_like(dkv_ref)
        dkp_ref[...] = jnp.zeros_like(dkp_ref)
        for r0 in range(0, seq, bq):
            rows, kend = pl.ds(r0, bq), r0 + bq
            cos_b, sa_b, sb_b = cos_ref[rows, :], sa_ref[rows, :], sb_ref[rows, :]
            qn = q_ref[rows, :QK_NOPE].astype(BF16)
            qp = _rope128(q_ref[rows, QK_NOPE:], cos_b, sa_b, sb_b).astype(BF16)
            kn, v, kp = kn_ref[:kend, :], v_ref[:kend, :], kp_ref[:kend, :]
            p = _probs(qn, qp, kn, kp, r0, scale)
            dob = do_ref[rows, :].astype(BF16)
            dp = lax.dot_general(dob, v, (NT, ((), ())), preferred_element_type=F32)
            ds = p * (dp - jnp.sum(p * dp, axis=-1, keepdims=True)) * scale
            dsb = ds.astype(BF16)
            pb = p.astype(BF16)
            dq_ref[rows, :QK_NOPE] = jnp.dot(dsb, kn, preferred_element_type=F32).astype(dq_ref.dtype)
            dqp = jnp.dot(dsb, kp, preferred_element_type=F32)
            dq_ref[rows, QK_NOPE:] = _rope128_t(dqp, cos_b, sa_b, sb_b).astype(dq_ref.dtype)
            dkv_ref[:kend, :QK_NOPE] += lax.dot_general(dsb, qn, (TN, ((), ())), preferred_element_type=F32)
            dkv_ref[:kend, QK_NOPE:] += lax.dot_general(pb, dob, (TN, ((), ())), preferred_element_type=F32)
            dkp_ref[:kend, :] += lax.dot_general(dsb, qp, (TN, ((), ())), preferred_element_type=F32)

    sd = jax.ShapeDtypeStruct
    return pl.pallas_call(
        body, name="attn_bwd", grid=(nh,),
        in_specs=_attn_specs(seq) + [pl.BlockSpec((seq, V_DIM), lambda h: (0, h))],
        out_specs=[pl.BlockSpec((None, seq, 256), lambda h: (h, 0, 0)),
                   pl.BlockSpec((None, seq, 256), lambda h: (h, 0, 0)),
                   pl.BlockSpec((None, seq, 128), lambda h: (h, 0, 0))],
        out_shape=[sd((nh, seq, 256), BF16), sd((nh, seq, 256), F32), sd((nh, seq, 128), F32)],
        compiler_params=_cparams(),
    )(q_raw, kv, kv, kpe, cos, sa, sb, do)


def _conv3(a, w, b):
    rows = lax.broadcasted_iota(jnp.int32, a.shape, 0)
    a1 = jnp.where(rows >= 1, pltpu.roll(a, 1, 0), 0.0)
    a2 = jnp.where(rows >= 2, pltpu.roll(a, 2, 0), 0.0)
    return w[2:3] * a + w[1:2] * a1 + w[0:1] * a2 + b, a1, a2


def _conv_gate_fwd(a, cw, cb):
    half, _, seq, c = a.shape
    nc = c // LANES

    def fn(pair, wg, wv, bg, bv):
        gc, _, _ = _conv3(pair[0], wg, bg)
        vc, _, _ = _conv3(pair[1], wv, bv)
        return gc * jax.nn.sigmoid(gc) * vc

    def w_spec(off, r):
        return pl.BlockSpec((None, r, LANES), lambda k, j: (k + off, 0, j))

    return _blockwise(
        "conv_gate_fwd", fn, [a, cw, cw, cb, cb],
        [pl.BlockSpec((None, 2, seq, LANES), lambda k, j: (k, 0, 0, j)),
         w_spec(0, 3), w_spec(half, 3), w_spec(0, 1), w_spec(half, 1)],
        [((seq, half * c), BF16)], [pl.BlockSpec((seq, LANES), lambda k, j: (0, k * nc + j))],
        grid=(half, nc))[0]


def _conv_gate_bwd(a, cw, cb, dm):
    half, _, seq, c = a.shape
    nc = c // LANES

    def body(a_ref, wg_ref, wv_ref, bg_ref, bv_ref, dm_ref, da_ref, dw_ref, db_ref):
        dmv = dm_ref[...]
        rows = lax.broadcasted_iota(jnp.int32, dmv.shape, 0)
        ga, wg = a_ref[0], wg_ref[...]
        va, wv = a_ref[1], wv_ref[...]
        gc, g1, g2 = _conv3(ga, wg, bg_ref[...])
        vc, v1, v2 = _conv3(va, wv, bv_ref[...])
        sg = jax.nn.sigmoid(gc)
        dms = dmv * sg
        d_val = dms * gc
        d_gate = dms * vc * (1.0 + gc * (1.0 - sg))

        def back(r, dc, own, a1, a2, w):
            up1 = jnp.where(rows < seq - 1, pltpu.roll(dc, seq - 1, 0), 0.0)
            up2 = jnp.where(rows < seq - 2, pltpu.roll(dc, seq - 2, 0), 0.0)
            da_ref[r] = (w[2:3] * dc + w[1:2] * up1 + w[0:1] * up2).astype(da_ref.dtype)
            dw_ref[r, 0:1, :] = jnp.sum(dc * a2, axis=0, keepdims=True)
            dw_ref[r, 1:2, :] = jnp.sum(dc * a1, axis=0, keepdims=True)
            dw_ref[r, 2:3, :] = jnp.sum(dc * own, axis=0, keepdims=True)
            db_ref[r] = jnp.sum(dc, axis=0, keepdims=True)

        back(0, d_gate, ga, g1, g2, wg)
        back(1, d_val, va, v1, v2, wv)

    def w_spec(off, r):
        return pl.BlockSpec((None, r, LANES), lambda k, j: (k + off, 0, j))

    def pair_spec(r):
        return pl.BlockSpec((None, 2, r, LANES), lambda k, j: (k, 0, 0, j))

    sd = jax.ShapeDtypeStruct
    return pl.pallas_call(
        body, name="conv_gate_bwd", grid=(half, nc),
        in_specs=[pair_spec(seq), w_spec(0, 3), w_spec(half, 3), w_spec(0, 1), w_spec(half, 1),
                  pl.BlockSpec((seq, LANES), lambda k, j: (0, k * nc + j))],
        out_specs=[pair_spec(seq), pair_spec(3), pair_spec(1)],
        out_shape=[sd((half, 2, seq, c), BF16), sd((half, 2, 3, c), F32), sd((half, 2, 1, c), F32)],
        compiler_params=_cparams(),
    )(a, cw, cw, cb, cb, dm)


ROW_T = 256


def _local_step(x, positions, target, w, emit=lambda **grads: None):
    seq, d = x.shape
    t_row = min(ROW_T, seq)
    nrow = seq // t_row
    ssm_w = d // 2
    nj = ssm_w // LANES
    n_groups = ssm_w // SSM_GROUP
    nh = w["wuq"].shape[0]
    q_rank = w["wuq"].shape[1]
    kv_rank = w["wukv"].shape[1]
    ns = w["wup"].shape[0]
    c_ff = w["wup"].shape[2]
    in_pad = w["win"].shape[1]
    tm = min(1024, seq)
    nm = seq // tm
    sw = 2 * STATE_BLOCK
    g1 = (nrow,)

    lr3 = w["lam_re"].reshape(n_groups, 1, SSM_STATE)
    li3 = w["lam_im"].reshape(n_groups, 1, SSM_STATE)
    ldt3 = w["log_dt"].reshape(n_groups, 1, 1)
    bt_re = jnp.swapaxes(w["b_re"].reshape(n_groups, SSM_STATE, SSM_GROUP), 1, 2)
    bt_im = jnp.swapaxes(w["b_im"].reshape(n_groups, SSM_STATE, SSM_GROUP), 1, 2)
    abar_re, abar_im, bbt_re, bbt_im = _s5_prep(lr3, li3, ldt3, bt_re, bt_im)
    eye = jnp.eye(GROUPS_PER_BLOCK, dtype=F32)

    def blockdiag_in(bb):
        t = bb.reshape(nj, GROUPS_PER_BLOCK, SSM_GROUP, SSM_STATE)
        return jnp.einsum("jghp,gk->jghkp", t, eye).reshape(nj, LANES, STATE_BLOCK)

    def blockdiag_in_t(dwb):
        t = dwb.reshape(nj, GROUPS_PER_BLOCK, SSM_GROUP, GROUPS_PER_BLOCK, SSM_STATE)
        return jnp.einsum("jghkp,gk->jghp", t, eye).reshape(n_groups, SSM_GROUP, SSM_STATE)

    def blockdiag_out(cc):
        t = cc.reshape(nj, GROUPS_PER_BLOCK, SSM_GROUP, SSM_STATE)
        return jnp.einsum("jghp,gk->jkpgh", t, eye).reshape(nj, STATE_BLOCK, LANES)

    def blockdiag_out_t(dwc):
        t = dwc.reshape(nj, GROUPS_PER_BLOCK, SSM_STATE, GROUPS_PER_BLOCK, SSM_GROUP)
        return jnp.einsum("jkpgh,gk->jghp", t, eye).reshape(n_groups, SSM_GROUP, SSM_STATE)

    c_re = w["c_re"].reshape(n_groups, SSM_GROUP, SSM_STATE)
    c_im = w["c_im"].reshape(n_groups, SSM_GROUP, SSM_STATE)
    wb = jnp.concatenate([blockdiag_in(bbt_re), blockdiag_in(bbt_im)], axis=2).astype(BF16)
    wc = jnp.concatenate([blockdiag_out(c_re), -blockdiag_out(c_im)], axis=1).astype(BF16)
    a_lay = jnp.concatenate([abar_re.reshape(nj, 1, STATE_BLOCK), abar_im.reshape(nj, 1, STATE_BLOCK)],
                            axis=1).reshape(1, nj * sw)

    attn_w = w["attn_norm"]
    hn = _blockwise("norm1", lambda xb, wv: _rms(xb, wv), [x, attn_w], [_row_spec(t_row, d), _full_spec((1, d))],
                    [((seq, d), BF16)], [_row_spec(t_row, d)], g1)[0]
    proj = _mm2d("proj", hn, w["win"], NN, F32, tn=640)

    s_all, ylin = _ssm_fwd(proj, wb, wc, a_lay)
    u_spec = pl.BlockSpec((t_row, ssm_w), lambda i: (i, 0))

    def ypre_fn(yl, ub, dsk):
        yp = yl + dsk * ub
        return yp, jax.nn.gelu(yp)

    y_pre, yg = _blockwise("ssm_gelu", ypre_fn, [ylin, proj, w["ssm_d"]],
                           [_row_spec(t_row, ssm_w), u_spec, _full_spec((1, ssm_w))],
                           [((seq, ssm_w), F32), ((seq, ssm_w), BF16)],
                           [_row_spec(t_row, ssm_w)] * 2, g1)
    z = _mm2d("ssm_glu", yg, w["wglu"], NN, F32, res=w["b_glu"])
    y_ssm = _blockwise("ssm_gate", lambda yp, zb: jax.nn.gelu(yp) * jax.nn.sigmoid(zb), [y_pre, z],
                       [_row_spec(t_row, ssm_w)] * 2, [((seq, ssm_w), F32)], [_row_spec(t_row, ssm_w)], g1)[0]

    cq_off, ckv_off, kpe_off = ssm_w, ssm_w + q_rank, ssm_w + q_rank + kv_rank
    c_q = proj[:, cq_off:ckv_off]
    c_kv = proj[:, ckv_off:kpe_off]
    kpe_raw = proj[:, kpe_off:kpe_off + LANES]
    pos_b = jnp.broadcast_to(positions.astype(F32)[:, None], (seq, LANES))
    inv_freq = ROPE_THETA ** (-jnp.arange(0, QK_ROPE, 2, dtype=F32) / QK_ROPE)
    inv128 = jnp.tile(inv_freq, 4).reshape(1, LANES)

    def mla_prep_fn(cq, ckv, kp, pb, inv, wq, wkv):
        ang = pb * inv
        lane = lax.broadcasted_iota(jnp.int32, ang.shape, 1)
        cs, sn = jnp.cos(ang), jnp.sin(ang)
        cos = jnp.where(lane < QK_ROPE, cs, 0.0)
        sa = jnp.where(lane < QK_ROPE // 2, -sn, 0.0)
        sb = jnp.where(jnp.logical_and(lane >= QK_ROPE // 2, lane < QK_ROPE), sn, 0.0)
        return _rms(cq, wq), _rms(ckv, wkv), _rope128(kp, cos, sa, sb), cos, sa, sb

    qn, kvn, kpe, cos_t, sa_t, sb_t = _blockwise(
        "mla_prep", mla_prep_fn, [c_q, c_kv, kpe_raw, pos_b, inv128, w["q_norm"], w["kv_norm"]],
        [_row_spec(t_row, q_rank), _row_spec(t_row, kv_rank), _row_spec(t_row, LANES), _row_spec(t_row, LANES),
         _full_spec((1, LANES)), _full_spec((1, q_rank)), _full_spec((1, kv_rank))],
        [((seq, q_rank), BF16), ((seq, kv_rank), BF16), ((seq, LANES), BF16)] + [((seq, LANES), F32)] * 3,
        [_row_spec(t_row, q_rank), _row_spec(t_row, kv_rank)] + [_row_spec(t_row, LANES)] * 4, g1)

    def head_mm(name, act, wh, out_dtype):
        kdim, ndim = wh.shape[1], wh.shape[2]
        return _mm(name, act, wh, grid=(nh, 1, 1),
                   a_spec=pl.BlockSpec((seq, kdim), lambda h, i, k: (i, 0)),
                   b_spec=pl.BlockSpec((None, kdim, ndim), lambda h, i, k: (h, 0, 0)),
                   o_spec=pl.BlockSpec((None, seq, ndim), lambda h, i, k: (h, i, 0)),
                   out_shape=(nh, seq, ndim), out_dtype=out_dtype)

    q_raw = head_mm("mla_q", qn, w["wuq"], F32)
    kv = head_mm("mla_kv", kvn, w["wukv"], BF16)
    y_mla = _attn_fwd(q_raw, kv, kpe, cos_t, sa_t, sb_t)
    mla_w = nh * V_DIM

    def outnorm_fn(ys, ym, ws, wm):
        return jnp.concatenate([_rms(ys, ws), _rms(ym, wm)], axis=1)

    ycat = _blockwise("out_norm", outnorm_fn, [y_ssm, y_mla, w["son"], w["mon"]],
                      [_row_spec(t_row, ssm_w), _row_spec(t_row, mla_w), _full_spec((1, ssm_w)), _full_spec((1, mla_w))],
                      [((seq, d), BF16)], [_row_spec(t_row, d)], g1)[0]
    h1 = _mm2d("out_proj", ycat, w["wout"], NN, F32, res=x)

    hn2 = _blockwise("norm2", lambda hb, wv: _rms(hb, wv), [h1, w["ffn_norm"]],
                     [_row_spec(t_row, d), _full_spec((1, d))], [((seq, d), BF16)], [_row_spec(t_row, d)], g1)[0]
    tku = d
    half = ns // 2
    a_ff = _mm("ffn_up", hn2, w["wup"], grid=(ns, nm, d // tku),
               a_spec=pl.BlockSpec((tm, tku), lambda s, i, k: (i, k)),
               b_spec=pl.BlockSpec((None, tku, c_ff), lambda s, i, k: (s, k, 0)),
               o_spec=pl.BlockSpec((None, None, tm, c_ff), lambda s, i, k: (s % half, s // half, i, 0)),
               out_shape=(half, 2, seq, c_ff), out_dtype=F32)
    cb3 = w["conv_b"].reshape(ns, 1, c_ff)
    m_ff = _conv_gate_fwd(a_ff, w["conv_w"], cb3)
    d_ff = half * c_ff
    wdn = w["wdown"]
    tnd = _tile(d, 1024)
    tmx, tnx = min(1024, seq), _tile(d, 1024)
    h2 = _mm2d("ffn_down", m_ff, wdn, NN, F32, tm=512, tn=512, tk=d_ff, res=h1)

    def loss_fn(hb, tb, wv):
        def f(hh, ww):
            err = _rms(hh, ww) - tb
            return 0.5 * jnp.sum(jnp.mean(err * err, axis=-1))

        lossv, (dh, dw) = jax.value_and_grad(f, argnums=(0, 1))(hb, wv)
        return dh, dh, jnp.full((1, LANES), lossv, F32), dw

    fin_w = w["final_norm"].reshape(1, d)
    dh2, dh2b, loss_acc, g_final = _blockwise(
        "loss_head", loss_fn, [h2, target, fin_w], [_row_spec(t_row, d), _row_spec(t_row, d), _full_spec((1, d))],
        [((seq, d), F32), ((seq, d), BF16), ((1, LANES), F32), ((1, d), F32)],
        [_row_spec(t_row, d), _row_spec(t_row, d), _full_spec((1, LANES)), _full_spec((1, d))], g1, n_acc=2)
    loss = loss_acc[0, 0]

    dm = _mm2d("ffn_down_dx", dh2b, wdn, NT, F32, tn=c_ff)
    tks = seq
    g_wdown = _mm2d("ffn_down_dw", m_ff, dh2b, TN, BF16, tm=c_ff)
    emit(wdown=g_wdown)
    da_ff, g_convw2, g_convb2 = _conv_gate_bwd(a_ff, w["conv_w"], cb3, dm)
    g_convw = jnp.swapaxes(g_convw2, 0, 1).reshape(ns, 3, c_ff)
    g_convb = jnp.swapaxes(g_convb2, 0, 1).reshape(ns, 1, c_ff)
    g_wup = _mm("ffn_up_dw", hn2, da_ff, grid=(ns, d // tnd, seq // tks), contract=TN,
                a_spec=pl.BlockSpec((tks, tnd), lambda s, j, k: (k, j)),
                b_spec=pl.BlockSpec((None, None, tks, c_ff), lambda s, j, k: (s % half, s // half, k, 0)),
                o_spec=pl.BlockSpec((None, tnd, c_ff), lambda s, j, k: (s, j, 0)),
                out_shape=(ns, d, c_ff), out_dtype=BF16)
    emit(wup=g_wup)
    dhn2 = _mm("ffn_up_dx", da_ff, w["wup"], grid=(seq // tmx, d // tnx, ns), contract=NT,
               a_spec=pl.BlockSpec((None, None, tmx, c_ff), lambda i, j, s: (s % half, s // half, i, 0)),
               b_spec=pl.BlockSpec((None, tnx, c_ff), lambda i, j, s: (s, j, 0)),
               o_spec=pl.BlockSpec((tmx, tnx), lambda i, j, s: (i, j)),
               out_shape=(seq, d), out_dtype=F32)
    emit(wup_pair_sums_after=dhn2)

    def norm_bwd_fn(hb, dres, dn, wv):
        dx_, dw_ = _rms_bwd(hb, wv, dn)
        dtot = dres + dx_
        return dtot, dtot, dw_

    dh1, dh1b, g_ffn_norm = _blockwise(
        "norm2_bwd", norm_bwd_fn, [h1, dh2, dhn2, w["ffn_norm"]],
        [_row_spec(t_row, d)] * 3 + [_full_spec((1, d))],
        [((seq, d), F32), ((seq, d), BF16), ((1, d), F32)],
        [_row_spec(t_row, d), _row_spec(t_row, d), _full_spec((1, d))], g1, n_acc=1)

    g_wout = _mm2d("out_proj_dw", ycat, dh1b, TN, BF16)

    def outnorm_bwd_fn(dhb, wo, ys, ym, ws, wm):
        dyc = lax.dot_general(dhb, wo, (NT, ((), ())), preferred_element_type=F32)
        dys, dws = _rms_bwd(ys, ws, dyc[:, :ssm_w])
        dym, dwm = _rms_bwd(ym, wm, dyc[:, ssm_w:])
        return dys, dym, dws, dwm

    dy_ssm, dy_mla, g_son, g_mon = _blockwise(
        "out_proj_dx_norm_bwd", outnorm_bwd_fn, [dh1b, w["wout"], y_ssm, y_mla, w["son"], w["mon"]],
        [_row_spec(t_row, d), _full_spec((d, d)), _row_spec(t_row, ssm_w), _row_spec(t_row, mla_w),
         _full_spec((1, ssm_w)), _full_spec((1, mla_w))],
        [((seq, ssm_w), F32), ((seq, mla_w), F32), ((1, ssm_w), F32), ((1, mla_w), F32)],
        [_row_spec(t_row, ssm_w), _row_spec(t_row, mla_w), _full_spec((1, ssm_w)), _full_spec((1, mla_w))],
        g1, n_acc=2)

    def glu_bwd_fn(dy, yp, zb, ub, dsk, wg):
        ygv = jax.nn.gelu(yp)
        sg = jax.nn.sigmoid(zb)
        dz = dy * ygv * sg * (1.0 - sg)
        dzb = dz.astype(BF16)
        dyg = dy * sg + lax.dot_general(dzb, wg, (NT, ((), ())), preferred_element_type=F32)
        _, vjp = jax.vjp(jax.nn.gelu, yp)
        dyp = vjp(dyg)[0]
        return (dzb, dyp, dyp * dsk, jnp.sum(dz, axis=0, keepdims=True), jnp.sum(dyp * ub, axis=0, keepdims=True))

    dz, dy_pre, du1, g_bglu, g_ssmd = _blockwise(
        "ssm_glu_bwd", glu_bwd_fn, [dy_ssm, y_pre, z, proj, w["ssm_d"], w["wglu"]],
        [_row_spec(t_row, ssm_w)] * 3 + [u_spec, _full_spec((1, ssm_w)), _full_spec((ssm_w, ssm_w))],
        [((seq, ssm_w), BF16), ((seq, ssm_w), BF16), ((seq, ssm_w), F32), ((1, ssm_w), F32), ((1, ssm_w), F32)],
        [_row_spec(t_row, ssm_w)] * 3 + [_full_spec((1, ssm_w))] * 2, g1, n_acc=2)
    g_wglu = _mm2d("ssm_glu_dw", yg, dz, TN, BF16)
    dq_raw, dkv, dkp_h = _attn_bwd(q_raw, kv, kpe, cos_t, sa_t, sb_t, dy_mla)

    def head_mm_dx(name, dact, wh):
        kdim, ndim = wh.shape[1], wh.shape[2]
        return _mm(name, dact, wh, grid=(1, 1, nh), contract=NT,
                   a_spec=pl.BlockSpec((None, seq, ndim), lambda i, j, h: (h, i, 0)),
                   b_spec=pl.BlockSpec((None, kdim, ndim), lambda i, j, h: (h, 0, 0)),
                   o_spec=pl.BlockSpec((seq, kdim), lambda i, j, h: (i, 0)),
                   out_shape=(seq, kdim), out_dtype=F32)

    def head_mm_dw(name, act, dact):
        kdim, ndim = act.shape[1], dact.shape[2]
        return _mm(name, act, dact, grid=(nh, 1, seq // tks), contract=TN,
                   a_spec=pl.BlockSpec((tks, kdim), lambda h, j, k: (k, 0)),
                   b_spec=pl.BlockSpec((None, tks, ndim), lambda h, j, k: (h, k, 0)),
                   o_spec=pl.BlockSpec((None, kdim, ndim), lambda h, j, k: (h, 0, 0)),
                   out_shape=(nh, kdim, ndim), out_dtype=BF16)

    g_wuq = head_mm_dw("mla_q_dw", qn, dq_raw)
    g_wukv = head_mm_dw("mla_kv_dw", kvn, dkv)
    dqn = head_mm_dx("mla_q_dx", dq_raw, w["wuq"])
    dkvn = head_mm_dx("mla_kv_dx", dkv, w["wukv"])
    emit(not_before=(dqn, dkvn, dy_pre), wout=g_wout, wuq=g_wuq, wukv=g_wukv, wglu=g_wglu, conv_w=g_convw)

    du, dwb, dwc, da_lay = _ssm_bwd(dy_pre, s_all, proj, du1, wb, wc, a_lay)
    g_c_re = blockdiag_out_t(dwc[:, :STATE_BLOCK, :])
    g_c_im = -blockdiag_out_t(dwc[:, STATE_BLOCK:, :])
    dbbt_re = blockdiag_in_t(dwb[:, :, :STATE_BLOCK])
    dbbt_im = blockdiag_in_t(dwb[:, :, STATE_BLOCK:])
    da3 = da_lay.reshape(nj, 2, STATE_BLOCK)
    dabar_re = da3[:, 0, :].reshape(n_groups, 1, SSM_STATE)
    dabar_im = da3[:, 1, :].reshape(n_groups, 1, SSM_STATE)
    g_lr3, g_li3, g_ldt3, g_bt_re, g_bt_im = _s5_prep_bwd(lr3, li3, ldt3, bt_re, bt_im,
                                                           dabar_re, dabar_im, dbbt_re, dbbt_im)

    def mla_prep_bwd_fn(cq, ckv, dqn_b, dkvn_b, dkp_b, cos, sa, sb, wq, wkv):
        dcq, dwq = _rms_bwd(cq, wq, dqn_b)
        dckv, dwkv = _rms_bwd(ckv, wkv, dkvn_b)
        dkp_sum = dkp_b[0]
        for h in range(1, nh):
            dkp_sum = dkp_sum + dkp_b[h]
        return dcq, dckv, _rope128_t(dkp_sum, cos, sa, sb), dwq, dwkv

    dc_q, dc_kv, dkpe_raw, g_qnorm, g_kvnorm = _blockwise(
        "mla_prep_bwd", mla_prep_bwd_fn, [c_q, c_kv, dqn, dkvn, dkp_h, cos_t, sa_t, sb_t, w["q_norm"], w["kv_norm"]],
        [_row_spec(t_row, q_rank), _row_spec(t_row, kv_rank), _row_spec(t_row, q_rank), _row_spec(t_row, kv_rank),
         pl.BlockSpec((nh, t_row, LANES), lambda i: (0, i, 0))] + [_row_spec(t_row, LANES)] * 3
        + [_full_spec((1, q_rank)), _full_spec((1, kv_rank))],
        [((seq, q_rank), BF16), ((seq, kv_rank), BF16), ((seq, LANES), BF16), ((1, q_rank), F32), ((1, kv_rank), F32)],
        [_row_spec(t_row, q_rank), _row_spec(t_row, kv_rank), _row_spec(t_row, LANES), _full_spec((1, q_rank)),
         _full_spec((1, kv_rank))], g1, n_acc=2)

    dproj = jnp.concatenate([du, dc_q, dc_kv, dkpe_raw], axis=1)
    g_win = _mm2d("proj_dw", hn, dproj, TN, BF16, tn=640)
    emit(win=g_win)
    def norm1_bwd_fn(dpb, wi, xb, dres, wv):
        dn = lax.dot_general(dpb, wi, (NT, ((), ())), preferred_element_type=F32)
        dx_, dw_ = _rms_bwd(xb, wv, dn)
        return dres + dx_, dw_

    grad_x, g_attn_norm = _blockwise(
        "proj_dx_norm1_bwd", norm1_bwd_fn, [dproj, w["win"], x, dh1, attn_w],
        [_row_spec(t_row, in_pad), _full_spec((d, in_pad)), _row_spec(t_row, d), _row_spec(t_row, d), _full_spec((1, d))],
        [((seq, d), F32), ((1, d), F32)], [_row_spec(t_row, d), _full_spec((1, d))], g1, n_acc=1)
    emit(win_pair_sums_after=grad_x)

    grads = dict(
        attn_norm=g_attn_norm, win=g_win, lam_re=g_lr3, lam_im=g_li3, log_dt=g_ldt3,
        bt_re=g_bt_re, bt_im=g_bt_im, c_re=g_c_re, c_im=g_c_im,
        ssm_d=g_ssmd, wglu=g_wglu, b_glu=g_bglu, q_norm=g_qnorm, wuq=g_wuq, kv_norm=g_kvnorm, wukv=g_wukv,
        son=g_son, mon=g_mon, wout=g_wout, ffn_norm=g_ffn_norm, wup=g_wup, conv_w=g_convw, conv_b=g_convb,
        wdown=g_wdown, final_norm=g_final)
    return loss, grad_x, grads


def _mesh_pos():
    return lax.axis_index("x"), lax.axis_index("y"), lax.axis_index("c")


def _handshake_all():
    x, y, c = _mesh_pos()
    barrier = pltpu.get_barrier_semaphore()
    for k in range(1, N_DEV):
        peer = (1 - x if k & 4 else x, 1 - y if k & 2 else y, 1 - c if k & 1 else c)
        pl.semaphore_signal(barrier, inc=1, device_id=peer, device_id_type=MESH)
    pl.semaphore_wait(barrier, N_DEV - 1)


def _handshake(peers):
    barrier = pltpu.get_barrier_semaphore()
    for peer in peers:
        pl.semaphore_signal(barrier, inc=1, device_id=peer, device_id_type=MESH)
    pl.semaphore_wait(barrier, len(peers))


def _comm_call(name, body, n, out_shape, ins, collective_id, after=None, copies=7):
    sems = [pltpu.SemaphoreType.DMA((copies * n,)), pltpu.SemaphoreType.DMA((copies * n,)),
            pltpu.SemaphoreType.DMA((n,))]
    if collective_id is None:
        any_spec = pl.BlockSpec(memory_space=pl.ANY)
        return pl.pallas_call(body, name=name, out_shape=out_shape, in_specs=[any_spec] * n,
                              out_specs=[any_spec] * n, scratch_shapes=sems)(*ins)
    seq_body = body
    if after:
        n_after = len(after)
        ins = list(ins) + list(after)

        def seq_body(*refs):
            body(*refs[:n], *refs[n + n_after:])

    return pl.kernel(seq_body, name=name, out_type=out_shape,
                     mesh=plsc.ScalarSubcoreMesh(axis_name="seq", num_cores=1), scratch_types=sems,
                     compiler_params=pltpu.CompilerParams(collective_id=collective_id))(*ins)


def _all_gather(name, xs, collective_id=None, after=None):
    n = len(xs)

    def body(*refs):
        x_refs, o_refs = refs[:n], refs[n:2 * n]
        send_sems, recv_sems, local_sems = refs[2 * n:]
        if collective_id is not None:
            _handshake_all()
        x, y, c = _mesh_pos()
        me, sibling = (x, y, c), (x, y, 1 - c)
        chips = [(1 - x, y), (x, 1 - y), (1 - x, 1 - y)]

        def slot(o_ref, px, py, pc):
            return o_ref.at[4 * px + 2 * py + pc]

        def copy(t, k, block, to, src=None):
            dst = slot(o_refs[t], *block)
            return pltpu.make_async_remote_copy(
                src_ref=dst if src is None else src, dst_ref=dst,
                send_sem=send_sems.at[7 * t + k], recv_sem=recv_sems.at[7 * t + k],
                device_id=to, device_id_type=MESH)

        started = []
        for t in range(n):
            mine = pltpu.make_async_copy(x_refs[t], slot(o_refs[t], *me), local_sems.at[t])
            mine.start()
            started.append(mine)
        first = []
        for t in range(n):
            first.append(copy(t, 0, me, sibling, src=x_refs[t]))
            first += [copy(t, 1 + j, me, (*chip, c), src=x_refs[t]) for j, chip in enumerate(chips)]
        for cp in first:
            cp.start()
        passed = []
        for j, chip in enumerate(chips):
            for t in range(n):
                copy(t, 1 + j, (*chip, c), me).wait_recv()
                fwd = copy(t, 4 + j, (*chip, c), sibling)
                fwd.start()
                passed.append(fwd)
        for t in range(n):
            copy(t, 0, sibling, me).wait_recv()
            for j, chip in enumerate(chips):
                copy(t, 4 + j, (*chip, 1 - c), me).wait_recv()
        for cp in first + passed:
            cp.wait_send()
        for mine in started:
            mine.wait()

    out_shape = [jax.ShapeDtypeStruct((N_DEV,) + v.shape, v.dtype) for v in xs]
    return _comm_call(name, body, n, out_shape, xs, collective_id, after)


def _exchange_partials(name, gs, collective_id=None, after=None):
    n = len(gs)

    def body(*refs):
        g_refs, o_refs = refs[:n], refs[n:2 * n]
        send_sems, recv_sems, local_sems = refs[2 * n:]
        if collective_id is not None:
            _handshake_all()
        x, y, c = _mesh_pos()
        me_idx = 4 * x + 2 * y + c
        copies = []
        for t in range(n):
            mine = pltpu.make_async_copy(g_refs[t].at[me_idx], o_refs[t].at[me_idx], local_sems.at[t])
            mine.start()
            copies.append(mine)
        remote = []
        for k in range(1, N_DEV):
            px = 1 - x if k & 4 else x
            py = 1 - y if k & 2 else y
            pc = 1 - c if k & 1 else c
            p_idx = 4 * px + 2 * py + pc
            for t in range(n):
                cp = pltpu.make_async_remote_copy(
                    src_ref=g_refs[t].at[p_idx], dst_ref=o_refs[t].at[me_idx],
                    send_sem=send_sems.at[7 * t + k - 1], recv_sem=recv_sems.at[7 * t + k - 1],
                    device_id=(px, py, pc), device_id_type=MESH)
                cp.start()
                landing = pltpu.make_async_remote_copy(
                    src_ref=g_refs[t].at[p_idx], dst_ref=o_refs[t].at[p_idx],
                    send_sem=send_sems.at[7 * t + k - 1], recv_sem=recv_sems.at[7 * t + k - 1],
                    device_id=(px, py, pc), device_id_type=MESH)
                remote.append((cp, landing))
        for cp, landing in remote:
            landing.wait_recv()
        for cp, landing in remote:
            cp.wait_send()
        for mine in copies:
            mine.wait()

    out_shape = [jax.ShapeDtypeStruct(v.shape, v.dtype) for v in gs]
    return _comm_call(name, body, n, out_shape, gs, collective_id, after)


N_CHIP = N_DEV // 2
PAIR_ADD_BLOCK_ELEMS = 1024 * 1024


def _pair_swap(name, gs, collective_id, after=None):
    n = len(gs)

    def body(*refs):
        g_refs, o_refs = refs[:n], refs[n:2 * n]
        send_sems, recv_sems, _ = refs[2 * n:]
        x, y, c = _mesh_pos()
        sibling = (x, y, 1 - c)
        _handshake([sibling])
        copies = []
        for t in range(n):
            for k in range(N_CHIP):
                copies.append(pltpu.make_async_remote_copy(
                    src_ref=g_refs[t].at[2 * k + 1 - c], dst_ref=o_refs[t].at[k],
                    send_sem=send_sems.at[N_CHIP * t + k], recv_sem=recv_sems.at[N_CHIP * t + k],
                    device_id=sibling, device_id_type=MESH))
        for cp in copies:
            cp.start()
        for cp in copies:
            cp.wait_recv()
        for cp in copies:
            cp.wait_send()

    out_shape = [jax.ShapeDtypeStruct((N_CHIP,) + v.shape[1:], v.dtype) for v in gs]
    return _comm_call(name, body, n, out_shape, gs, collective_id, after, copies=N_CHIP)


def _pair_add(name, g, got):
    _, r, c = g.shape
    tr = r
    if r * c > PAIR_ADD_BLOCK_ELEMS and r % SUBLANES == 0:
        tr = SUBLANES
        while r % (tr * 2) == 0 and tr * 2 * c <= PAIR_ADD_BLOCK_ELEMS:
            tr *= 2

    def body(core_ref, g_ref, got_ref, o_ref):
        o_ref[...] = (g_ref[...].astype(F32) + got_ref[...].astype(F32)).astype(o_ref.dtype)

    grid_spec = pltpu.PrefetchScalarGridSpec(
        num_scalar_prefetch=1, grid=(N_CHIP, r // tr),
        in_specs=[pl.BlockSpec((None, None, tr, c), lambda k, i, core: (k, core[0], i, 0)),
                  pl.BlockSpec((None, tr, c), lambda k, i, core: (k, i, 0))],
        out_specs=pl.BlockSpec((None, tr, c), lambda k, i, core: (k, i, 0)))
    core = lax.axis_index("c").astype(jnp.int32).reshape(1)
    return pl.pallas_call(body, name=name, grid_spec=grid_spec, out_shape=jax.ShapeDtypeStruct((N_CHIP, r, c), g.dtype),
                          compiler_params=_cparams())(core, g.reshape(N_CHIP, 2, r, c), got)


def _chip_exchange(name, hs, collective_id, after=None):
    n = len(hs)
    per = N_CHIP - 1

    def body(*refs):
        h_refs, o_refs = refs[:n], refs[n:2 * n]
        send_sems, recv_sems, local_sems = refs[2 * n:]
        x, y, c = _mesh_pos()
        others = [(1 - x if k & 2 else x, 1 - y if k & 1 else y) for k in range(1, N_CHIP)]
        _handshake([(px, py, c) for px, py in others])
        my_chip = 2 * x + y
        local = []
        for t in range(n):
            mine = pltpu.make_async_copy(h_refs[t].at[my_chip], o_refs[t].at[my_chip], local_sems.at[t])
            mine.start()
            local.append(mine)
        remote = []
        for j, (px, py) in enumerate(others):
            chip = 2 * px + py
            for t in range(n):
                sems = dict(send_sem=send_sems.at[per * t + j], recv_sem=recv_sems.at[per * t + j],
                            device_id=(px, py, c), device_id_type=MESH)
                cp = pltpu.make_async_remote_copy(src_ref=h_refs[t].at[chip], dst_ref=o_refs[t].at[my_chip], **sems)
                cp.start()
                landing = pltpu.make_async_remote_copy(src_ref=h_refs[t].at[chip], dst_ref=o_refs[t].at[chip], **sems)
                remote.append((cp, landing))
        for cp, landing in remote:
            landing.wait_recv()
        for cp, landing in remote:
            cp.wait_send()
        for mine in local:
            mine.wait()

    out_shape = [jax.ShapeDtypeStruct(v.shape, v.dtype) for v in hs]
    return _comm_call(name, body, n, out_shape, hs, collective_id, after, copies=per)


ADAM_BLOCK_ELEMS = 128 * 1024


def _sum_parts(pb):
    g = pb[0].astype(F32)
    for j in range(1, pb.shape[0]):
        g = g + pb[j].astype(F32)
    return g


def _adam_math(g, wb_, mb, vb):
    m_new = ADAM_B1 * mb + (1.0 - ADAM_B1) * g
    v_new = ADAM_B2 * vb + (1.0 - ADAM_B2) * (g * g)
    m_hat = m_new / (1.0 - ADAM_B1 ** ADAM_STEP)
    v_hat = v_new / (1.0 - ADAM_B2 ** ADAM_STEP)
    delta = -ADAM_LR * (m_hat / (jnp.sqrt(v_hat) + ADAM_EPS) + ADAM_WD * wb_)
    return g, delta, m_new, v_new


def _adamw_multi(name, items, nblk=1):
    n = len(items)

    def body(*refs):
        for t in range(n):
            pr, wr, mr, vr = refs[4 * t:4 * t + 4]
            res = _adam_math(_sum_parts(pr[...]), wr[...], mr[...], vr[...])
            for o, val in zip(refs[4 * n + 4 * t:4 * n + 4 * t + 4], res):
                o[...] = val

    def spec(shape, lead):
        blk = list(shape)
        blk[lead + 1] = shape[lead + 1] // nblk
        if nblk == 1:
            return pl.BlockSpec(tuple(blk), lambda i, nd=len(shape): (0,) * nd)
        return pl.BlockSpec(tuple(blk), lambda i, nd=len(shape), ax=lead + 1: (0,) * ax + (i,) + (0,) * (nd - ax - 1))

    in_specs, out_specs, out_shape, ins = [], [], [], []
    for parts, wv, mv, vv in items:
        assert parts.shape[1:] == wv.shape, (name, parts.shape, wv.shape)
        ins += [parts, wv, mv, vv]
        in_specs += [spec(parts.shape, 1)] + [spec(wv.shape, 0)] * 3
        out_specs += [spec(wv.shape, 0)] * 4
        out_shape += [jax.ShapeDtypeStruct(wv.shape, F32)] * 4
    res = pl.pallas_call(body, name=name, grid=(nblk,), in_specs=in_specs, out_specs=out_specs, out_shape=out_shape,
                         compiler_params=_cparams())(*ins)
    return [tuple(res[4 * t:4 * t + 4]) for t in range(n)]


def _sum_multi(name, parts_list):
    def body(*refs):
        for pr, o in zip(refs[:len(parts_list)], refs[len(parts_list):]):
            o[...] = _sum_parts(pr[...])

    return pl.pallas_call(body, name=name, out_shape=[jax.ShapeDtypeStruct(p.shape[1:], F32) for p in parts_list],
                          compiler_params=_cparams())(*parts_list)


def _adamw_sum(name, parts, wv, mv, vv):
    npart, r, c = parts.shape
    tr = r
    if r * c > ADAM_BLOCK_ELEMS and r % SUBLANES == 0:
        tr = SUBLANES
        while r % (tr * 2) == 0 and tr * 2 * c <= ADAM_BLOCK_ELEMS:
            tr *= 2

    def fn(pb, wb_, mb, vb):
        return _adam_math(_sum_parts(pb), wb_, mb, vb)

    row = pl.BlockSpec((tr, c), lambda i: (i, 0))
    return _blockwise(name, fn, [parts, wv, mv, vv],
                      [pl.BlockSpec((npart, tr, c), lambda i: (0, i, 0)), row, row, row],
                      [((r, c), F32)] * 4, [row] * 4, (r // tr,))


_VECTORS = ["attn_norm", "lam_re", "lam_im", "log_dt", "ssm_d", "b_glu", "q_norm", "kv_norm", "son", "mon",
            "ffn_norm", "conv_b", "final_norm"]
_BIG = ["win", "wglu", "wuq", "wukv", "wout", "wup", "wdown", "conv_w"]
_TWO_LEVEL = ("wup", "win")
_AFTER = "_pair_sums_after"
_ORDER = ["attn_norm", "win", "lam_re", "lam_im", "log_dt", "b_re", "b_im", "c_re", "c_im", "ssm_d", "wglu",
          "b_glu", "q_norm", "wuq", "kv_norm", "wukv", "son", "mon", "wout", "ffn_norm", "wup", "conv_w",
          "conv_b", "wdown", "final_norm"]


def kernel(x, positions, attn_norm_w, w_in, ssm_lambda_re, ssm_lambda_im, ssm_log_dt, ssm_b_re, ssm_b_im, ssm_c_re, ssm_c_im, ssm_d, ssm_w_glu, ssm_b_glu, mla_q_norm_w, mla_w_uq, mla_kv_norm_w, mla_w_ukv, ssm_out_norm_w, mla_out_norm_w, w_out, ffn_norm_w, ffn_w_up, ffn_conv_w, ffn_conv_b, ffn_w_down, final_norm_w, loss_target, m_attn_norm_w, m_w_in, m_ssm_lambda_re, m_ssm_lambda_im, m_ssm_log_dt, m_ssm_b_re, m_ssm_b_im, m_ssm_c_re, m_ssm_c_im, m_ssm_d, m_ssm_w_glu, m_ssm_b_glu, m_mla_q_norm_w, m_mla_w_uq, m_mla_kv_norm_w, m_mla_w_ukv, m_ssm_out_norm_w, m_mla_out_norm_w, m_w_out, m_ffn_norm_w, m_ffn_w_up, m_ffn_conv_w, m_ffn_conv_b, m_ffn_w_down, m_final_norm_w, v_attn_norm_w, v_w_in, v_ssm_lambda_re, v_ssm_lambda_im, v_ssm_log_dt, v_ssm_b_re, v_ssm_b_im, v_ssm_c_re, v_ssm_c_im, v_ssm_d, v_ssm_w_glu, v_ssm_b_glu, v_mla_q_norm_w, v_mla_w_uq, v_mla_kv_norm_w, v_mla_w_ukv, v_ssm_out_norm_w, v_mla_out_norm_w, v_w_out, v_ffn_norm_w, v_ffn_w_up, v_ffn_conv_w, v_ffn_conv_b, v_ffn_w_down, v_final_norm_w):
    wts = dict(attn_norm=attn_norm_w, win=w_in, lam_re=ssm_lambda_re, lam_im=ssm_lambda_im, log_dt=ssm_log_dt,
               b_re=ssm_b_re, b_im=ssm_b_im, c_re=ssm_c_re, c_im=ssm_c_im, ssm_d=ssm_d, wglu=ssm_w_glu,
               b_glu=ssm_b_glu, q_norm=mla_q_norm_w, wuq=mla_w_uq, kv_norm=mla_kv_norm_w, wukv=mla_w_ukv,
               son=ssm_out_norm_w, mon=mla_out_norm_w, wout=w_out, ffn_norm=ffn_norm_w, wup=ffn_w_up,
               conv_w=ffn_conv_w, conv_b=ffn_conv_b, wdown=ffn_w_down, final_norm=final_norm_w)
    moms = dict(zip(_ORDER, [m_attn_norm_w, m_w_in, m_ssm_lambda_re, m_ssm_lambda_im, m_ssm_log_dt, m_ssm_b_re,
                             m_ssm_b_im, m_ssm_c_re, m_ssm_c_im, m_ssm_d, m_ssm_w_glu, m_ssm_b_glu, m_mla_q_norm_w,
                             m_mla_w_uq, m_mla_kv_norm_w, m_mla_w_ukv, m_ssm_out_norm_w, m_mla_out_norm_w, m_w_out,
                             m_ffn_norm_w, m_ffn_w_up, m_ffn_conv_w, m_ffn_conv_b, m_ffn_w_down, m_final_norm_w]))
    vels = dict(zip(_ORDER, [v_attn_norm_w, v_w_in, v_ssm_lambda_re, v_ssm_lambda_im, v_ssm_log_dt, v_ssm_b_re,
                             v_ssm_b_im, v_ssm_c_re, v_ssm_c_im, v_ssm_d, v_ssm_w_glu, v_ssm_b_glu, v_mla_q_norm_w,
                             v_mla_w_uq, v_mla_kv_norm_w, v_mla_w_ukv, v_ssm_out_norm_w, v_mla_out_norm_w, v_w_out,
                             v_ffn_norm_w, v_ffn_w_up, v_ffn_conv_w, v_ffn_conv_b, v_ffn_w_down, v_final_norm_w]))
    seq, d = x.shape[1], x.shape[2]
    in_width = w_in.shape[2]
    in_pad = -(-in_width // LANES) * LANES
    q_cols = mla_w_uq.shape[2]
    q_pad = 2 * LANES

    (win_g,) = _all_gather("gather_w_in", [jnp.pad(w_in[0], ((0, 0), (0, in_pad - in_width))).astype(BF16)])
    wglu_g, wuq_g, wukv_g, wout_g, convw_g = _all_gather(
        "gather_mix", [ssm_w_glu[0].astype(BF16), jnp.pad(mla_w_uq[0], ((0, 0), (0, q_pad - q_cols))).astype(BF16),
                       mla_w_ukv[0].astype(BF16), w_out[0].astype(BF16), ffn_conv_w[0]], collective_id=0)
    (wup_g,) = _all_gather("gather_ffn_up", [ffn_w_up[0].astype(BF16)], collective_id=1)
    (wdown_g,) = _all_gather("gather_ffn_down", [ffn_w_down[0].astype(BF16)], collective_id=2)
    ns = N_DEV
    c_ff = wup_g.shape[2]
    w = dict(
        attn_norm=attn_norm_w, win=win_g.reshape(d, in_pad), lam_re=ssm_lambda_re, lam_im=ssm_lambda_im,
        log_dt=ssm_log_dt, b_re=ssm_b_re, b_im=ssm_b_im, c_re=ssm_c_re, c_im=ssm_c_im, ssm_d=ssm_d,
        wglu=wglu_g.reshape(d // 2, d // 2), b_glu=ssm_b_glu, q_norm=mla_q_norm_w, wuq=wuq_g,
        kv_norm=mla_kv_norm_w, wukv=wukv_g, son=ssm_out_norm_w, mon=mla_out_norm_w, wout=wout_g.reshape(d, d),
        ffn_norm=ffn_norm_w, wup=wup_g, conv_w=convw_g, conv_b=ffn_conv_b,
        wdown=wdown_g.reshape(ns // 2 * c_ff, d), final_norm=final_norm_w)

    shard_layout = dict(
        win=lambda a: a[:, :in_width].reshape(N_DEV, d // N_DEV, in_width),
        wglu=lambda a: a.reshape(N_DEV, d // 2 // N_DEV, d // 2),
        wuq=lambda a: a[:, :, :q_cols], wukv=lambda a: a, wout=lambda a: a.reshape(N_DEV, d // N_DEV, d),
        wup=lambda a: a, wdown=lambda a: a.reshape(N_DEV, c_ff // 2, d), conv_w=lambda a: a)
    recv = {}
    next_id = [3]

    last = [None]

    out = {}

    def update(k):
        shp = wts[k].shape
        r, c = shp[-2], shp[-1]
        res = _adamw_sum("adamw_" + k, recv[k].reshape(-1, r, c), wts[k].reshape(r, c),
                         moms[k].reshape(r, c), vels[k].reshape(r, c))
        out[k] = [a.reshape(shp) for a in res]
        return res[0]

    pending = {}

    def exchange(not_before=(), **grads):
        names = list(grads)
        if len(names) == 1 and names[0] in _TWO_LEVEL:
            k = names[0]
            parts = shard_layout[k](grads[k])
            got = _pair_swap("swap_" + k, [parts], collective_id=next_id[0], after=[last[0]])[0]
            next_id[0] += 1
            pending[k] = (parts, got)
            last[0] = got
            return
        if len(names) == 1 and names[0].endswith(_AFTER):
            k = names[0][:-len(_AFTER)]
            sums = _pair_add("pair_add_" + k, *pending[k])
            recv[k] = _chip_exchange("exchange_" + k, [sums], collective_id=next_id[0],
                                     after=[last[0], grads[names[0]]])[0]
            next_id[0] += 1
            last[0] = recv[k]
            return
        if "wout" in names:
            not_before = (*not_before, update("wdown"))
        got = _exchange_partials("exchange_" + "_".join(names), [shard_layout[k](grads[k]) for k in names],
                                 collective_id=next_id[0], after=[a for a in (last[0], *not_before) if a is not None])
        next_id[0] += 1
        last[0] = got[-1]
        recv.update(zip(names, got))

    loss_part, grad_x, g = _local_step(x[0], positions[0], loss_target[0], w, emit=exchange)
    loss = lax.psum(loss_part, ("x", "y", "c"))
    n_groups = ssm_lambda_re.shape[1]
    two_d = {"lam_re": (n_groups, -1), "lam_im": (n_groups, -1)}
    dense = {k: g[k].reshape(two_d.get(k, (1, -1))) for k in _VECTORS}
    dense.update(c_re=g["c_re"], c_im=g["c_im"], bt_re=g["bt_re"], bt_im=g["bt_im"])
    names = list(dense)
    gathered = dict(zip(names, _all_gather("gather_small_grads", [dense[k] for k in names],
                                           collective_id=next_id[0], after=[last[0]])))
    for k in _BIG:
        if k not in out:
            update(k)

    def finish(keys, results):
        for k, res in zip(keys, results):
            out[k] = [a.reshape(wts[k].shape) for a in res]

    view = lambda k, a: a.reshape(dense[k].shape)
    finish(_VECTORS, _adamw_multi("adamw_vectors", [(gathered[k], view(k, wts[k]), view(k, moms[k]), view(k, vels[k]))
                                                    for k in _VECTORS]))
    c_keys = ["c_re", "c_im"]
    finish(c_keys, _adamw_multi("adamw_ssm_c", [(gathered[k][:, None], wts[k], moms[k], vels[k]) for k in c_keys]))
    b_sums = _sum_multi("sum_ssm_b", [gathered["bt_re"], gathered["bt_im"]])
    b_keys = ["b_re", "b_im"]
    finish(b_keys, _adamw_multi("adamw_ssm_b", [(jnp.swapaxes(s, 1, 2)[None, None], wts[k], moms[k], vels[k])
                                                for k, s in zip(b_keys, b_sums)], nblk=SUBLANES))

    grad_x = grad_x.reshape(x.shape)
    return (loss, grad_x, *[out[k][0] for k in _ORDER], *[out[k][1] for k in _ORDER],
            *[out[k][2] for k in _ORDER], *[out[k][3] for k in _ORDER])
```

```python
import functools
import math

import jax
import jax.numpy as jnp
from jax import lax
from jax.experimental import pallas as pl
from jax.experimental.pallas import tpu as pltpu
from jax.experimental.pallas import tpu_sc as plsc

F32 = jnp.float32
BF16 = jnp.bfloat16
MESH = pl.DeviceIdType.MESH

N_DEV = 8
LANES = 128
SUBLANES = 8
VMEM_LIMIT = 48 * 1024 * 1024

SSM_GROUP = 16
SSM_STATE = 64
GROUPS_PER_BLOCK = LANES // SSM_GROUP
STATE_BLOCK = GROUPS_PER_BLOCK * SSM_STATE
QK_NOPE = 128
QK_ROPE = 64
V_DIM = 128
ROPE_THETA = 10000.0
RMS_EPS = 1e-6

ADAM_LR = 0.001
ADAM_B1 = 0.9
ADAM_B2 = 0.999
ADAM_EPS = 1e-08
ADAM_WD = 0.01
ADAM_STEP = 10

NN = ((1,), (0,))
NT = ((1,), (1,))
TN = ((0,), (0,))


def _cparams():
    return pltpu.CompilerParams(vmem_limit_bytes=VMEM_LIMIT)


def _tile(n, want):
    if n <= want:
        return n
    t = (want // LANES) * LANES
    while t >= LANES:
        if n % t == 0:
            return t
        t -= LANES
    return n


def _mm(name, a, b, *, grid, a_spec, b_spec, o_spec, out_shape, out_dtype, contract=NN,
        res=None, res_spec=None):
    nk = grid[-1]
    kaxis = len(grid) - 1
    acc_shape = tuple(d for d in o_spec.block_shape if d is not None)

    def body(*refs):
        a_ref, b_ref = refs[:2]
        r_ref = None if res is None else refs[2]
        o_ref = refs[2 if res is None else 3]
        part = lax.dot_general(a_ref[...].astype(BF16), b_ref[...].astype(BF16),
                               (contract, ((), ())), preferred_element_type=F32)
        if nk == 1:
            if r_ref is not None:
                part = part + r_ref[...].astype(F32)
            o_ref[...] = part.astype(o_ref.dtype)
            return
        acc = refs[-1]
        k = pl.program_id(kaxis)

        @pl.when(k == 0)
        def _():
            acc[...] = part

        @pl.when(k != 0)
        def _():
            acc[...] += part

        @pl.when(k == nk - 1)
        def _():
            r = acc[...]
            if r_ref is not None:
                r = r + r_ref[...].astype(F32)
            o_ref[...] = r.astype(o_ref.dtype)

    ins = [a, b] + ([] if res is None else [res])
    in_specs = [a_spec, b_spec] + ([] if res is None else [res_spec])
    return pl.pallas_call(
        body, name=name, grid=grid, in_specs=in_specs, out_specs=o_spec,
        out_shape=jax.ShapeDtypeStruct(out_shape, out_dtype),
        scratch_shapes=[pltpu.VMEM(acc_shape, F32)] if nk > 1 else [], compiler_params=_cparams(),
    )(*ins)


def _mm2d(name, a, b, contract, out_dtype, tm=1024, tn=1024, tk=2048, res=None):
    if contract == NN:
        (m, kk), n = a.shape, b.shape[1]
    elif contract == NT:
        (m, kk), n = a.shape, b.shape[0]
    else:
        (kk, m), n = a.shape, b.shape[1]
    tm, tn, tk = _tile(m, tm), _tile(n, tn), _tile(kk, tk)
    grid = (m // tm, n // tn, kk // tk)
    if contract == TN:
        a_spec = pl.BlockSpec((tk, tm), lambda i, j, k: (k, i))
    else:
        a_spec = pl.BlockSpec((tm, tk), lambda i, j, k: (i, k))
    if contract == NT:
        b_spec = pl.BlockSpec((tn, tk), lambda i, j, k: (j, k))
    else:
        b_spec = pl.BlockSpec((tk, tn), lambda i, j, k: (k, j))
    o_spec = pl.BlockSpec((tm, tn), lambda i, j, k: (i, j))
    res_spec = None
    if res is not None:
        if res.shape[0] == 1:
            res_spec = pl.BlockSpec((1, tn), lambda i, j, k: (0, j))
        else:
            res_spec = pl.BlockSpec((tm, tn), lambda i, j, k: (i, j))
    return _mm(name, a, b, grid=grid, a_spec=a_spec, b_spec=b_spec, o_spec=o_spec,
               out_shape=(m, n), out_dtype=out_dtype, contract=contract, res=res, res_spec=res_spec)


def _blockwise(name, fn, ins, in_specs, outs, out_specs, grid, n_acc=0, acc_all=True):
    n_in, n_out = len(ins), len(outs)
    n_plain = n_out - n_acc

    def body(*refs):
        vals = fn(*[r[...] for r in refs[:n_in]])
        if not isinstance(vals, (tuple, list)):
            vals = (vals,)
        o_refs = refs[n_in:n_in + n_out]
        for r, v in zip(o_refs[:n_plain], vals[:n_plain]):
            r[...] = v.astype(r.dtype)
        if n_acc:
            if acc_all:
                first = functools.reduce(jnp.logical_and, [pl.program_id(d) == 0 for d in range(len(grid))])
            else:
                first = pl.program_id(len(grid) - 1) == 0

            @pl.when(first)
            def _():
                for r, v in zip(o_refs[n_plain:], vals[n_plain:]):
                    r[...] = v.astype(r.dtype)

            @pl.when(jnp.logical_not(first))
            def _():
                for r, v in zip(o_refs[n_plain:], vals[n_plain:]):
                    r[...] += v.astype(r.dtype)

    return pl.pallas_call(
        body, name=name, grid=grid, in_specs=in_specs, out_specs=out_specs,
        out_shape=[jax.ShapeDtypeStruct(s, d) for s, d in outs], compiler_params=_cparams(),
    )(*ins)


def _row_spec(t, c):
    return pl.BlockSpec((t, c), lambda i: (i, 0))


def _full_spec(shape):
    nd = len(shape)
    return pl.BlockSpec(tuple(shape), lambda *g: (0,) * nd)


def _rms(xf, w):
    return xf * lax.rsqrt(jnp.mean(xf * xf, axis=-1, keepdims=True) + RMS_EPS) * w


def _rms_bwd(xf, w, dy):
    _, vjp = jax.vjp(_rms, xf, w)
    return vjp(dy)


def _s5_disc(lr, li, ldt, bre, bim):
    dt = jnp.exp(ldt)
    mag = jnp.exp(lr * dt)
    ar = mag * jnp.cos(li * dt)
    ai = mag * jnp.sin(li * dt)
    nr, ni = ar - 1.0, ai
    den = lr * lr + li * li
    zr = (nr * lr + ni * li) / den
    zi = (ni * lr - nr * li) / den
    return ar, ai, zr * bre - zi * bim, zr * bim + zi * bre


def _s5_prep(lr, li, ldt, bre, bim):
    def body(lr_r, li_r, ldt_r, bre_r, bim_r, ar_r, ai_r, br_r, bi_r):
        ar, ai, br, bi = _s5_disc(lr_r[...], li_r[...], ldt_r[...], bre_r[...], bim_r[...])
        ar_r[...] = ar
        ai_r[...] = ai
        br_r[...] = br
        bi_r[...] = bi

    sd = jax.ShapeDtypeStruct
    return pl.pallas_call(
        body, name="s5_prep",
        out_shape=[sd(lr.shape, F32), sd(lr.shape, F32), sd(bre.shape, F32), sd(bre.shape, F32)],
        compiler_params=_cparams(),
    )(lr, li, ldt, bre, bim)


def _s5_prep_bwd(lr, li, ldt, bre, bim, dar, dai, dbr, dbi):
    def body(lr_r, li_r, ldt_r, bre_r, bim_r, dar_r, dai_r, dbr_r, dbi_r, o0, o1, o2, o3, o4):
        _, vjp = jax.vjp(_s5_disc, lr_r[...], li_r[...], ldt_r[...], bre_r[...], bim_r[...])
        g = vjp((dar_r[...], dai_r[...], dbr_r[...], dbi_r[...]))
        for o, v in zip((o0, o1, o2, o3, o4), g):
            o[...] = v

    sd = jax.ShapeDtypeStruct
    return pl.pallas_call(
        body, name="s5_prep_bwd",
        out_shape=[sd(lr.shape, F32), sd(li.shape, F32), sd(ldt.shape, F32), sd(bre.shape, F32), sd(bim.shape, F32)],
        compiler_params=_cparams(),
    )(lr, li, ldt, bre, bim, dar, dai, dbr, dbi)


SCAN_T = 256


def _scan_tables(ar, ai, tab_r, tab_i, sub, reverse):
    pr, pi = ar, ai
    for k in range(sub):
        row = sub - 1 - k if reverse else k
        tab_r[row:row + 1, :] = pr
        tab_i[row:row + 1, :] = pi
        pr, pi = ar * pr - ai * pi, ar * pi + ai * pr


def _pack_matrix(t_blk, dtype):
    sub = t_blk // SUBLANES
    dst = jnp.arange(t_blk)
    src = (dst % SUBLANES) * sub + dst // SUBLANES
    return (src[:, None] == jnp.arange(t_blk)[None, :]).astype(dtype)


def _permute_rows_f32(pm, x):
    hi = x.astype(BF16)
    r1 = x - hi.astype(F32)
    mid = r1.astype(BF16)
    lo = (r1 - mid.astype(F32)).astype(BF16)
    dot = lambda v: jnp.dot(pm, v, preferred_element_type=F32)
    return dot(hi) + dot(mid) + dot(lo)


def _scan_block(x, loc, ar, ai, st, tab_r, tab_i, sub, reverse):
    hb = STATE_BLOCK
    a8r = jnp.broadcast_to(ar, (SUBLANES, hb))
    a8i = jnp.broadcast_to(ai, (SUBLANES, hb))
    sr = jnp.zeros((SUBLANES, hb), F32)
    si = jnp.zeros((SUBLANES, hb), F32)
    steps = range(sub - 1, -1, -1) if reverse else range(sub)
    for t in steps:
        rows = slice(t * SUBLANES, (t + 1) * SUBLANES)
        sr, si = a8r * sr - a8i * si + x[rows, :hb], a8r * si + a8i * sr + x[rows, hb:]
        loc[rows, :hb] = sr
        loc[rows, hb:] = si
    cr, ci = st[0:1, :], st[1:2, :]
    far = 0 if reverse else sub - 1
    fr, fi = tab_r[far:far + 1, :], tab_i[far:far + 1, :]
    ent_r, ent_i = [None] * SUBLANES, [None] * SUBLANES
    for c in (range(SUBLANES - 1, -1, -1) if reverse else range(SUBLANES)):
        ent_r[c], ent_i[c] = cr, ci
        cr, ci = sr[c:c + 1, :] + (fr * cr - fi * ci), si[c:c + 1, :] + (fr * ci + fi * cr)
    st[0:1, :] = cr
    st[1:2, :] = ci
    c8r = jnp.concatenate(ent_r, axis=0)
    c8i = jnp.concatenate(ent_i, axis=0)
    out = []
    for t in range(sub):
        rows = slice(t * SUBLANES, (t + 1) * SUBLANES)
        tr, ti = tab_r[t:t + 1, :], tab_i[t:t + 1, :]
        out.append(jnp.concatenate([loc[rows, :hb] + (tr * c8r - ti * c8i), loc[rows, hb:] + (tr * c8i + ti * c8r)],
                                   axis=1))
    return jnp.concatenate(out, axis=0)


SSM_BLOCKS_PER_STEP = 2


def _scan_scratch(nblk, t_blk, sub, hb):
    return [pltpu.VMEM((nblk, SUBLANES, hb), F32), pltpu.VMEM((nblk, sub, hb), F32), pltpu.VMEM((nblk, sub, hb), F32),
            pltpu.VMEM((nblk, t_blk, 2 * hb), F32)]


def _ssm_fwd(proj, wb, wc, a):
    seq = proj.shape[0]
    nj = wb.shape[0]
    w2 = 2 * STATE_BLOCK
    hb = STATE_BLOCK
    t_blk = min(SCAN_T, seq)
    sub = t_blk // SUBLANES
    pm = _pack_matrix(t_blk, BF16)

    npair = SSM_BLOCKS_PER_STEP

    def body(u_ref, wb_ref, wc_ref, a_ref, pm_ref, pmt_ref, s_ref, y_ref, st, tab_r, tab_i, loc):
        coef = [(a_ref[:, b * w2:b * w2 + hb], a_ref[:, b * w2 + hb:(b + 1) * w2]) for b in range(npair)]

        @pl.when(pl.program_id(1) == 0)
        def _():
            for b, (ar, ai) in enumerate(coef):
                st[b] = jnp.zeros((SUBLANES, hb), F32)
                _scan_tables(ar, ai, tab_r.at[b], tab_i.at[b], sub, False)

        for b, (ar, ai) in enumerate(coef):
            ub = u_ref[:, b * LANES:(b + 1) * LANES].astype(BF16)
            up = jnp.dot(pm_ref[...], ub, preferred_element_type=F32).astype(BF16)
            bu = jnp.dot(up, wb_ref[b], preferred_element_type=F32)
            s = _scan_block(bu, loc.at[b], ar, ai, st.at[b], tab_r.at[b], tab_i.at[b], sub, False)
            s_ref[:, b * w2:(b + 1) * w2] = s
            yp = jnp.dot(s.astype(BF16), wc_ref[b], preferred_element_type=F32)
            y_ref[:, b * LANES:(b + 1) * LANES] = _permute_rows_f32(pmt_ref[...], yp)

    sd = jax.ShapeDtypeStruct
    return pl.pallas_call(
        body, name="ssm_fwd", grid=(nj // npair, seq // t_blk),
        in_specs=[pl.BlockSpec((t_blk, npair * LANES), lambda j, i: (i, j)),
                  pl.BlockSpec((npair, LANES, w2), lambda j, i: (j, 0, 0)),
                  pl.BlockSpec((npair, w2, LANES), lambda j, i: (j, 0, 0)),
                  pl.BlockSpec((1, npair * w2), lambda j, i: (0, j)),
                  _full_spec((t_blk, t_blk)), _full_spec((t_blk, t_blk))],
        out_specs=[pl.BlockSpec((t_blk, npair * w2), lambda j, i: (i, j)),
                   pl.BlockSpec((t_blk, npair * LANES), lambda j, i: (i, j))],
        out_shape=[sd((seq, nj * w2), F32), sd((seq, nj * LANES), F32)],
        scratch_shapes=_scan_scratch(npair, t_blk, sub, hb), compiler_params=_cparams(),
    )(proj, wb, wc, a, pm, pm.T)


def _ssm_bwd(dy, s, proj, du1, wb, wc, a):
    seq = dy.shape[0]
    nj = wb.shape[0]
    w2 = 2 * STATE_BLOCK
    hb = STATE_BLOCK
    t_blk = min(SCAN_T, seq)
    sub = t_blk // SUBLANES
    nb = seq // t_blk
    pm = _pack_matrix(t_blk, BF16)

    npair = SSM_BLOCKS_PER_STEP

    def body(dy_ref, s_ref, sprev_ref, u_ref, du1_ref, wb_ref, wc_ref, a_ref, pm_ref, pmt_ref,
             du_ref, dwb_ref, dwc_ref, da_ref, st, tab_r, tab_i, loc):
        ib = pl.program_id(1)
        pmv = pm_ref[...]
        coef = [(a_ref[:, b * w2:b * w2 + hb], -a_ref[:, b * w2 + hb:(b + 1) * w2]) for b in range(npair)]

        @pl.when(ib == 0)
        def _():
            for b, (ar, ai) in enumerate(coef):
                st[b] = jnp.zeros((SUBLANES, hb), F32)
                _scan_tables(ar, ai, tab_r.at[b], tab_i.at[b], sub, True)

        sums = []
        for b, (ar, ai) in enumerate(coef):
            cols, wide = slice(b * LANES, (b + 1) * LANES), slice(b * w2, (b + 1) * w2)
            dyp = jnp.dot(pmv, dy_ref[:, cols], preferred_element_type=F32).astype(BF16)
            up = jnp.dot(pmv, u_ref[:, cols].astype(BF16), preferred_element_type=F32).astype(BF16)
            ds = lax.dot_general(dyp, wc_ref[b], (NT, ((), ())), preferred_element_type=F32)
            lam = _scan_block(ds, loc.at[b], ar, ai, st.at[b], tab_r.at[b], tab_i.at[b], sub, True)
            lamb = lam.astype(BF16)
            du = lax.dot_general(lamb, wb_ref[b], (NT, ((), ())), preferred_element_type=F32)
            du_ref[:, cols] = (_permute_rows_f32(pmt_ref[...], du) + du1_ref[:, cols]).astype(du_ref.dtype)
            sv = s_ref[:, wide]
            dwb = lax.dot_general(up, lamb, (TN, ((), ())), preferred_element_type=F32)
            dwc = lax.dot_general(sv.astype(BF16), dyp, (TN, ((), ())), preferred_element_type=F32)

            prev_last = sprev_ref[SUBLANES - 1:SUBLANES, wide]
            prev_last = jnp.where(ib == nb - 1, jnp.zeros_like(prev_last), prev_last)
            tail = sv[t_blk - SUBLANES:, :]
            sl = lax.broadcasted_iota(jnp.int32, tail.shape, 0)
            head = jnp.where(sl >= 1, pltpu.roll(tail, 1, 0), prev_last)
            s_sh = jnp.concatenate([head, sv[:t_blk - SUBLANES, :]], axis=0)
            lam_r, lam_i = lam[:, :hb], lam[:, hb:]
            sr_, si_ = s_sh[:, :hb], s_sh[:, hb:]
            dar = jnp.sum(lam_r * sr_ + lam_i * si_, axis=0, keepdims=True)
            dai = jnp.sum(lam_i * sr_ - lam_r * si_, axis=0, keepdims=True)
            sums.append((wide, jnp.concatenate([dar, dai], axis=1), dwb, dwc))

        @pl.when(ib == 0)
        def _():
            for b, (wide, contrib, dwb, dwc) in enumerate(sums):
                da_ref[:, wide] = contrib
                dwb_ref[b] = dwb
                dwc_ref[b] = dwc

        @pl.when(ib != 0)
        def _():
            for b, (wide, contrib, dwb, dwc) in enumerate(sums):
                da_ref[:, wide] += contrib
                dwb_ref[b] += dwb
                dwc_ref[b] += dwc

    blk = lambda j, i: (nb - 1 - i, j)
    prev_blk = lambda j, i: (jnp.maximum((nb - 1 - i) * sub - 1, 0), j)
    sd = jax.ShapeDtypeStruct
    return pl.pallas_call(
        body, name="ssm_bwd", grid=(nj // npair, nb),
        in_specs=[pl.BlockSpec((t_blk, npair * LANES), blk), pl.BlockSpec((t_blk, npair * w2), blk),
                  pl.BlockSpec((SUBLANES, npair * w2), prev_blk), pl.BlockSpec((t_blk, npair * LANES), blk),
                  pl.BlockSpec((t_blk, npair * LANES), blk),
                  pl.BlockSpec((npair, LANES, w2), lambda j, i: (j, 0, 0)),
                  pl.BlockSpec((npair, w2, LANES), lambda j, i: (j, 0, 0)),
                  pl.BlockSpec((1, npair * w2), lambda j, i: (0, j)),
                  _full_spec((t_blk, t_blk)), _full_spec((t_blk, t_blk))],
        out_specs=[pl.BlockSpec((t_blk, npair * LANES), blk),
                   pl.BlockSpec((npair, LANES, w2), lambda j, i: (j, 0, 0)),
                   pl.BlockSpec((npair, w2, LANES), lambda j, i: (j, 0, 0)),
                   pl.BlockSpec((1, npair * w2), lambda j, i: (0, j))],
        out_shape=[sd((seq, nj * LANES), BF16), sd((nj, LANES, w2), F32), sd((nj, w2, LANES), F32),
                   sd((1, nj * w2), F32)],
        scratch_shapes=_scan_scratch(npair, t_blk, sub, hb), compiler_params=_cparams(),
    )(dy, s, s, proj, du1, wb, wc, a, pm, pm.T)


def _rope128(x, cos, sa, sb):
    return x * cos + pltpu.roll(x, 96, 1) * sa + pltpu.roll(x, 32, 1) * sb


def _rope128_t(dy, cos, sa, sb):
    return dy * cos + pltpu.roll(dy * sa, 32, 1) + pltpu.roll(dy * sb, 96, 1)


ATT_BQ = 256


def _probs(qn, qp, kn, kp, r0, scale):
    s = lax.dot_general(qn, kn, (NT, ((), ())), preferred_element_type=F32)
    s = s + lax.dot_general(qp, kp, (NT, ((), ())), preferred_element_type=F32)
    s = s * scale
    diag = s[:, r0:]
    row = lax.broadcasted_iota(jnp.int32, diag.shape, 0)
    col = lax.broadcasted_iota(jnp.int32, diag.shape, 1)
    diag = jnp.where(col <= row, diag, jnp.finfo(F32).min)
    s = diag if r0 == 0 else jnp.concatenate([s[:, :r0], diag], axis=1)
    m = jnp.max(s, axis=-1, keepdims=True)
    e = jnp.exp(s - m)
    return e / jnp.sum(e, axis=-1, keepdims=True)


def _attn_specs(seq):
    tab = pl.BlockSpec((seq, LANES), lambda h: (0, 0))
    return [pl.BlockSpec((None, seq, 256), lambda h: (h, 0, 0)), pl.BlockSpec((None, seq, 128), lambda h: (h, 0, 0)),
            pl.BlockSpec((None, seq, 128), lambda h: (h, 0, 1)), tab, tab, tab, tab]


def _attn_fwd(q_raw, kv, kpe, cos, sa, sb):
    nh, seq, _ = q_raw.shape
    bq = min(ATT_BQ, seq)
    scale = (QK_NOPE + QK_ROPE) ** -0.5

    def body(q_ref, kn_ref, v_ref, kp_ref, cos_ref, sa_ref, sb_ref, o_ref):
        for r0 in range(0, seq, bq):
            rows, kend = pl.ds(r0, bq), r0 + bq
            qn = q_ref[rows, :QK_NOPE].astype(BF16)
            qp = _rope128(q_ref[rows, QK_NOPE:], cos_ref[rows, :], sa_ref[rows, :], sb_ref[rows, :]).astype(BF16)
            p = _probs(qn, qp, kn_ref[:kend, :], kp_ref[:kend, :], r0, scale)
            o_ref[rows, :] = jnp.dot(p.astype(BF16), v_ref[:kend, :], preferred_element_type=F32)

    return pl.pallas_call(
        body, name="attn_fwd", grid=(nh,), in_specs=_attn_specs(seq),
        out_specs=pl.BlockSpec((seq, V_DIM), lambda h: (0, h)),
        out_shape=jax.ShapeDtypeStruct((seq, nh * V_DIM), F32), compiler_params=_cparams(),
    )(q_raw, kv, kv, kpe, cos, sa, sb)


def _attn_bwd(q_raw, kv, kpe, cos, sa, sb, do):
    nh, seq, _ = q_raw.shape
    bq = min(ATT_BQ, seq)
    scale = (QK_NOPE + QK_ROPE) ** -0.5

    def body(q_ref, kn_ref, v_ref, kp_ref, cos_ref, sa_ref, sb_ref, do_ref, dq_ref, dkv_ref, dkp_ref):
        dkv_ref[...] = jnp.zeros_like(dkv_ref)
        dkp_ref[...] = jnp.zeros_like(dkp_ref)
        for r0 in range(0, seq, bq):
            rows, kend = pl.ds(r0, bq), r0 + bq
            cos_b, sa_b, sb_b = cos_ref[rows, :], sa_ref[rows, :], sb_ref[rows, :]
            qn = q_ref[rows, :QK_NOPE].astype(BF16)
            qp = _rope128(q_ref[rows, QK_NOPE:], cos_b, sa_b, sb_b).astype(BF16)
            kn, v, kp = kn_ref[:kend, :], v_ref[:kend, :], kp_ref[:kend, :]
            p = _probs(qn, qp, kn, kp, r0, scale)
            dob = do_ref[rows, :].astype(BF16)
            dp = lax.dot_general(dob, v, (NT, ((), ())), preferred_element_type=F32)
            ds = p * (dp - jnp.sum(p * dp, axis=-1, keepdims=True)) * scale
            dsb = ds.astype(BF16)
            pb = p.astype(BF16)
            dq_ref[rows, :QK_NOPE] = jnp.dot(dsb, kn, preferred_element_type=F32).astype(dq_ref.dtype)
            dqp = jnp.dot(dsb, kp, preferred_element_type=F32)
            dq_ref[rows, QK_NOPE:] = _rope128_t(dqp, cos_b, sa_b, sb_b).astype(dq_ref.dtype)
            dkv_ref[:kend, :QK_NOPE] += lax.dot_general(dsb, qn, (TN, ((), ())), preferred_element_type=F32)
            dkv_ref[:kend, QK_NOPE:] += lax.dot_general(pb, dob, (TN, ((), ())), preferred_element_type=F32)
            dkp_ref[:kend, :] += lax.dot_general(dsb, qp, (TN, ((), ())), preferred_element_type=F32)

    sd = jax.ShapeDtypeStruct
    return pl.pallas_call(
        body, name="attn_bwd", grid=(nh,),
        in_specs=_attn_specs(seq) + [pl.BlockSpec((seq, V_DIM), lambda h: (0, h))],
        out_specs=[pl.BlockSpec((None, seq, 256), lambda h: (h, 0, 0)),
                   pl.BlockSpec((None, seq, 256), lambda h: (h, 0, 0)),
                   pl.BlockSpec((None, seq, 128), lambda h: (h, 0, 0))],
        out_shape=[sd((nh, seq, 256), BF16), sd((nh, seq, 256), F32), sd((nh, seq, 128), F32)],
        compiler_params=_cparams(),
    )(q_raw, kv, kv, kpe, cos, sa, sb, do)


def _conv3(a, w, b):
    rows = lax.broadcasted_iota(jnp.int32, a.shape, 0)
    a1 = jnp.where(rows >= 1, pltpu.roll(a, 1, 0), 0.0)
    a2 = jnp.where(rows >= 2, pltpu.roll(a, 2, 0), 0.0)
    return w[2:3] * a + w[1:2] * a1 + w[0:1] * a2 + b, a1, a2


def _conv_gate_fwd(a, cw, cb):
    half, _, seq, c = a.shape
    nc = c // LANES

    def fn(pair, wg, wv, bg, bv):
        gc, _, _ = _conv3(pair[0], wg, bg)
        vc, _, _ = _conv3(pair[1], wv, bv)
        return gc * jax.nn.sigmoid(gc) * vc

    def w_spec(off, r):
        return pl.BlockSpec((None, r, LANES), lambda k, j: (k + off, 0, j))

    return _blockwise(
        "conv_gate_fwd", fn, [a, cw, cw, cb, cb],
        [pl.BlockSpec((None, 2, seq, LANES), lambda k, j: (k, 0, 0, j)),
         w_spec(0, 3), w_spec(half, 3), w_spec(0, 1), w_spec(half, 1)],
        [((seq, half * c), BF16)], [pl.BlockSpec((seq, LANES), lambda k, j: (0, k * nc + j))],
        grid=(half, nc))[0]


def _conv_gate_bwd(a, cw, cb, dm):
    half, _, seq, c = a.shape
    nc = c // LANES

    def body(a_ref, wg_ref, wv_ref, bg_ref, bv_ref, dm_ref, da_ref, dw_ref, db_ref):
        dmv = dm_ref[...]
        rows = lax.broadcasted_iota(jnp.int32, dmv.shape, 0)
        ga, wg = a_ref[0], wg_ref[...]
        va, wv = a_ref[1], wv_ref[...]
        gc, g1, g2 = _conv3(ga, wg, bg_ref[...])
        vc, v1, v2 = _conv3(va, wv, bv_ref[...])
        sg = jax.nn.sigmoid(gc)
        dms = dmv * sg
        d_val = dms * gc
        d_gate = dms * vc * (1.0 + gc * (1.0 - sg))

        def back(r, dc, own, a1, a2, w):
            up1 = jnp.where(rows < seq - 1, pltpu.roll(dc, seq - 1, 0), 0.0)
            up2 = jnp.where(rows < seq - 2, pltpu.roll(dc, seq - 2, 0), 0.0)
            da_ref[r] = (w[2:3] * dc + w[1:2] * up1 + w[0:1] * up2).astype(da_ref.dtype)
            dw_ref[r, 0:1, :] = jnp.sum(dc * a2, axis=0, keepdims=True)
            dw_ref[r, 1:2, :] = jnp.sum(dc * a1, axis=0, keepdims=True)
            dw_ref[r, 2:3, :] = jnp.sum(dc * own, axis=0, keepdims=True)
            db_ref[r] = jnp.sum(dc, axis=0, keepdims=True)

        back(0, d_gate, ga, g1, g2, wg)
        back(1, d_val, va, v1, v2, wv)

    def w_spec(off, r):
        return pl.BlockSpec((None, r, LANES), lambda k, j: (k + off, 0, j))

    def pair_spec(r):
        return pl.BlockSpec((None, 2, r, LANES), lambda k, j: (k, 0, 0, j))

    sd = jax.ShapeDtypeStruct
    return pl.pallas_call(
        body, name="conv_gate_bwd", grid=(half, nc),
        in_specs=[pair_spec(seq), w_spec(0, 3), w_spec(half, 3), w_spec(0, 1), w_spec(half, 1),
                  pl.BlockSpec((seq, LANES), lambda k, j: (0, k * nc + j))],
        out_specs=[pair_spec(seq), pair_spec(3), pair_spec(1)],
        out_shape=[sd((half, 2, seq, c), BF16), sd((half, 2, 3, c), F32), sd((half, 2, 1, c), F32)],
        compiler_params=_cparams(),
    )(a, cw, cw, cb, cb, dm)


ROW_T = 256


def _local_step(x, positions, target, w, emit=lambda **grads: None):
    seq, d = x.shape
    t_row = min(ROW_T, seq)
    nrow = seq // t_row
    ssm_w = d // 2
    nj = ssm_w // LANES
    n_groups = ssm_w // SSM_GROUP
    nh = w["wuq"].shape[0]
    q_rank = w["wuq"].shape[1]
    kv_rank = w["wukv"].shape[1]
    ns = w["wup"].shape[0]
    c_ff = w["wup"].shape[2]
    in_pad = w["win"].shape[1]
    tm = min(1024, seq)
    nm = seq // tm
    sw = 2 * STATE_BLOCK
    g1 = (nrow,)

    lr3 = w["lam_re"].reshape(n_groups, 1, SSM_STATE)
    li3 = w["lam_im"].reshape(n_groups, 1, SSM_STATE)
    ldt3 = w["log_dt"].reshape(n_groups, 1, 1)
    bt_re = jnp.swapaxes(w["b_re"].reshape(n_groups, SSM_STATE, SSM_GROUP), 1, 2)
    bt_im = jnp.swapaxes(w["b_im"].reshape(n_groups, SSM_STATE, SSM_GROUP), 1, 2)
    abar_re, abar_im, bbt_re, bbt_im = _s5_prep(lr3, li3, ldt3, bt_re, bt_im)
    eye = jnp.eye(GROUPS_PER_BLOCK, dtype=F32)

    def blockdiag_in(bb):
        t = bb.reshape(nj, GROUPS_PER_BLOCK, SSM_GROUP, SSM_STATE)
        return jnp.einsum("jghp,gk->jghkp", t, eye).reshape(nj, LANES, STATE_BLOCK)

    def blockdiag_in_t(dwb):
        t = dwb.reshape(nj, GROUPS_PER_BLOCK, SSM_GROUP, GROUPS_PER_BLOCK, SSM_STATE)
        return jnp.einsum("jghkp,gk->jghp", t, eye).reshape(n_groups, SSM_GROUP, SSM_STATE)

    def blockdiag_out(cc):
        t = cc.reshape(nj, GROUPS_PER_BLOCK, SSM_GROUP, SSM_STATE)
        return jnp.einsum("jghp,gk->jkpgh", t, eye).reshape(nj, STATE_BLOCK, LANES)

    def blockdiag_out_t(dwc):
        t = dwc.reshape(nj, GROUPS_PER_BLOCK, SSM_STATE, GROUPS_PER_BLOCK, SSM_GROUP)
        return jnp.einsum("jkpgh,gk->jghp", t, eye).reshape(n_groups, SSM_GROUP, SSM_STATE)

    c_re = w["c_re"].reshape(n_groups, SSM_GROUP, SSM_STATE)
    c_im = w["c_im"].reshape(n_groups, SSM_GROUP, SSM_STATE)
    wb = jnp.concatenate([blockdiag_in(bbt_re), blockdiag_in(bbt_im)], axis=2).astype(BF16)
    wc = jnp.concatenate([blockdiag_out(c_re), -blockdiag_out(c_im)], axis=1).astype(BF16)
    a_lay = jnp.concatenate([abar_re.reshape(nj, 1, STATE_BLOCK), abar_im.reshape(nj, 1, STATE_BLOCK)],
                            axis=1).reshape(1, nj * sw)

    attn_w = w["attn_norm"]
    hn = _blockwise("norm1", lambda xb, wv: _rms(xb, wv), [x, attn_w], [_row_spec(t_row, d), _full_spec((1, d))],
                    [((seq, d), BF16)], [_row_spec(t_row, d)], g1)[0]
    proj = _mm2d("proj", hn, w["win"], NN, F32, tn=640)

    s_all, ylin = _ssm_fwd(proj, wb, wc, a_lay)
    u_spec = pl.BlockSpec((t_row, ssm_w), lambda i: (i, 0))

    def ypre_fn(yl, ub, dsk):
        yp = yl + dsk * ub
        return yp, jax.nn.gelu(yp)

    y_pre, yg = _blockwise("ssm_gelu", ypre_fn, [ylin, proj, w["ssm_d"]],
                           [_row_spec(t_row, ssm_w), u_spec, _full_spec((1, ssm_w))],
                           [((seq, ssm_w), F32), ((seq, ssm_w), BF16)],
                           [_row_spec(t_row, ssm_w)] * 2, g1)
    z = _mm2d("ssm_glu", yg, w["wglu"], NN, F32, res=w["b_glu"])
    y_ssm = _blockwise("ssm_gate", lambda yp, zb: jax.nn.gelu(yp) * jax.nn.sigmoid(zb), [y_pre, z],
                       [_row_spec(t_row, ssm_w)] * 2, [((seq, ssm_w), F32)], [_row_spec(t_row, ssm_w)], g1)[0]

    cq_off, ckv_off, kpe_off = ssm_w, ssm_w + q_rank, ssm_w + q_rank + kv_rank
    c_q = proj[:, cq_off:ckv_off]
    c_kv = proj[:, ckv_off:kpe_off]
    kpe_raw = proj[:, kpe_off:kpe_off + LANES]
    pos_b = jnp.broadcast_to(positions.astype(F32)[:, None], (seq, LANES))
    inv_freq = ROPE_THETA ** (-jnp.arange(0, QK_ROPE, 2, dtype=F32) / QK_ROPE)
    inv128 = jnp.tile(inv_freq, 4).reshape(1, LANES)

    def mla_prep_fn(cq, ckv, kp, pb, inv, wq, wkv):
        ang = pb * inv
        lane = lax.broadcasted_iota(jnp.int32, ang.shape, 1)
        cs, sn = jnp.cos(ang), jnp.sin(ang)
        cos = jnp.where(lane < QK_ROPE, cs, 0.0)
        sa = jnp.where(lane < QK_ROPE // 2, -sn, 0.0)
        sb = jnp.where(jnp.logical_and(lane >= QK_ROPE // 2, lane < QK_ROPE), sn, 0.0)
        return _rms(cq, wq), _rms(ckv, wkv), _rope128(kp, cos, sa, sb), cos, sa, sb

    qn, kvn, kpe, cos_t, sa_t, sb_t = _blockwise(
        "mla_prep", mla_prep_fn, [c_q, c_kv, kpe_raw, pos_b, inv128, w["q_norm"], w["kv_norm"]],
        [_row_spec(t_row, q_rank), _row_spec(t_row, kv_rank), _row_spec(t_row, LANES), _row_spec(t_row, LANES),
         _full_spec((1, LANES)), _full_spec((1, q_rank)), _full_spec((1, kv_rank))],
        [((seq, q_rank), BF16), ((seq, kv_rank), BF16), ((seq, LANES), BF16)] + [((seq, LANES), F32)] * 3,
        [_row_spec(t_row, q_rank), _row_spec(t_row, kv_rank)] + [_row_spec(t_row, LANES)] * 4, g1)

    def head_mm(name, act, wh, out_dtype):
        kdim, ndim = wh.shape[1], wh.shape[2]
        return _mm(name, act, wh, grid=(nh, 1, 1),
                   a_spec=pl.BlockSpec((seq, kdim), lambda h, i, k: (i, 0)),
                   b_spec=pl.BlockSpec((None, kdim, ndim), lambda h, i, k: (h, 0, 0)),
                   o_spec=pl.BlockSpec((None, seq, ndim), lambda h, i, k: (h, i, 0)),
                   out_shape=(nh, seq, ndim), out_dtype=out_dtype)

    q_raw = head_mm("mla_q", qn, w["wuq"], F32)
    kv = head_mm("mla_kv", kvn, w["wukv"], BF16)
    y_mla = _attn_fwd(q_raw, kv, kpe, cos_t, sa_t, sb_t)
    mla_w = nh * V_DIM

    def outnorm_fn(ys, ym, ws, wm):
        return jnp.concatenate([_rms(ys, ws), _rms(ym, wm)], axis=1)

    ycat = _blockwise("out_norm", outnorm_fn, [y_ssm, y_mla, w["son"], w["mon"]],
                      [_row_spec(t_row, ssm_w), _row_spec(t_row, mla_w), _full_spec((1, ssm_w)), _full_spec((1, mla_w))],
                      [((seq, d), BF16)], [_row_spec(t_row, d)], g1)[0]
    h1 = _mm2d("out_proj", ycat, w["wout"], NN, F32, res=x)

    hn2 = _blockwise("norm2", lambda hb, wv: _rms(hb, wv), [h1, w["ffn_norm"]],
                     [_row_spec(t_row, d), _full_spec((1, d))], [((seq, d), BF16)], [_row_spec(t_row, d)], g1)[0]
    tku = d
    half = ns // 2
    a_ff = _mm("ffn_up", hn2, w["wup"], grid=(ns, nm, d // tku),
               a_spec=pl.BlockSpec((tm, tku), lambda s, i, k: (i, k)),
               b_spec=pl.BlockSpec((None, tku, c_ff), lambda s, i, k: (s, k, 0)),
               o_spec=pl.BlockSpec((None, None, tm, c_ff), lambda s, i, k: (s % half, s // half, i, 0)),
               out_shape=(half, 2, seq, c_ff), out_dtype=F32)
    cb3 = w["conv_b"].reshape(ns, 1, c_ff)
    m_ff = _conv_gate_fwd(a_ff, w["conv_w"], cb3)
    d_ff = half * c_ff
    wdn = w["wdown"]
    tnd = _tile(d, 1024)
    tmx, tnx = min(1024, seq), _tile(d, 1024)
    h2 = _mm2d("ffn_down", m_ff, wdn, NN, F32, tm=512, tn=512, tk=d_ff, res=h1)

    def loss_fn(hb, tb, wv):
        def f(hh, ww):
            err = _rms(hh, ww) - tb
            return 0.5 * jnp.sum(jnp.mean(err * err, axis=-1))

        lossv, (dh, dw) = jax.value_and_grad(f, argnums=(0, 1))(hb, wv)
        return dh, dh, jnp.full((1, LANES), lossv, F32), dw

    fin_w = w["final_norm"].reshape(1, d)
    dh2, dh2b, loss_acc, g_final = _blockwise(
        "loss_head", loss_fn, [h2, target, fin_w], [_row_spec(t_row, d), _row_spec(t_row, d), _full_spec((1, d))],
        [((seq, d), F32), ((seq, d), BF16), ((1, LANES), F32), ((1, d), F32)],
        [_row_spec(t_row, d), _row_spec(t_row, d), _full_spec((1, LANES)), _full_spec((1, d))], g1, n_acc=2)
    loss = loss_acc[0, 0]

    dm = _mm2d("ffn_down_dx", dh2b, wdn, NT, F32, tn=c_ff)
    tks = seq
    g_wdown = _mm2d("ffn_down_dw", m_ff, dh2b, TN, BF16, tm=c_ff)
    emit(wdown=g_wdown)
    da_ff, g_convw2, g_convb2 = _conv_gate_bwd(a_ff, w["conv_w"], cb3, dm)
    g_convw = jnp.swapaxes(g_convw2, 0, 1).reshape(ns, 3, c_ff)
    g_convb = jnp.swapaxes(g_convb2, 0, 1).reshape(ns, 1, c_ff)
    g_wup = _mm("ffn_up_dw", hn2, da_ff, grid=(ns, d // tnd, seq // tks), contract=TN,
                a_spec=pl.BlockSpec((tks, tnd), lambda s, j, k: (k, j)),
                b_spec=pl.BlockSpec((None, None, tks, c_ff), lambda s, j, k: (s % half, s // half, k, 0)),
                o_spec=pl.BlockSpec((None, tnd, c_ff), lambda s, j, k: (s, j, 0)),
                out_shape=(ns, d, c_ff), out_dtype=BF16)
    emit(wup=g_wup)
    dhn2 = _mm("ffn_up_dx", da_ff, w["wup"], grid=(seq // tmx, d // tnx, ns), contract=NT,
               a_spec=pl.BlockSpec((None, None, tmx, c_ff), lambda i, j, s: (s % half, s // half, i, 0)),
               b_spec=pl.BlockSpec((None, tnx, c_ff), lambda i, j, s: (s, j, 0)),
               o_spec=pl.BlockSpec((tmx, tnx), lambda i, j, s: (i, j)),
               out_shape=(seq, d), out_dtype=F32)
    emit(wup_pair_sums_after=dhn2)

    def norm_bwd_fn(hb, dres, dn, wv):
        dx_, dw_ = _rms_bwd(hb, wv, dn)
        dtot = dres + dx_
        return dtot, dtot, dw_

    dh1, dh1b, g_ffn_norm = _blockwise(
        "norm2_bwd", norm_bwd_fn, [h1, dh2, dhn2, w["ffn_norm"]],
        [_row_spec(t_row, d)] * 3 + [_full_spec((1, d))],
        [((seq, d), F32), ((seq, d), BF16), ((1, d), F32)],
        [_row_spec(t_row, d), _row_spec(t_row, d), _full_spec((1, d))], g1, n_acc=1)

    g_wout = _mm2d("out_proj_dw", ycat, dh1b, TN, BF16)

    def outnorm_bwd_fn(dhb, wo, ys, ym, ws, wm):
        dyc = lax.dot_general(dhb, wo, (NT, ((), ())), preferred_element_type=F32)
        dys, dws = _rms_bwd(ys, ws, dyc[:, :ssm_w])
        dym, dwm = _rms_bwd(ym, wm, dyc[:, ssm_w:])
        return dys, dym, dws, dwm

    dy_ssm, dy_mla, g_son, g_mon = _blockwise(
        "out_proj_dx_norm_bwd", outnorm_bwd_fn, [dh1b, w["wout"], y_ssm, y_mla, w["son"], w["mon"]],
        [_row_spec(t_row, d), _full_spec((d, d)), _row_spec(t_row, ssm_w), _row_spec(t_row, mla_w),
         _full_spec((1, ssm_w)), _full_spec((1, mla_w))],
        [((seq, ssm_w), F32), ((seq, mla_w), F32), ((1, ssm_w), F32), ((1, mla_w), F32)],
        [_row_spec(t_row, ssm_w), _row_spec(t_row, mla_w), _full_spec((1, ssm_w)), _full_spec((1, mla_w))],
        g1, n_acc=2)

    def glu_bwd_fn(dy, yp, zb, ub, dsk, wg):
        ygv = jax.nn.gelu(yp)
        sg = jax.nn.sigmoid(zb)
        dz = dy * ygv * sg * (1.0 - sg)
        dzb = dz.astype(BF16)
        dyg = dy * sg + lax.dot_general(dzb, wg, (NT, ((), ())), preferred_element_type=F32)
        _, vjp = jax.vjp(jax.nn.gelu, yp)
        dyp = vjp(dyg)[0]
        return (dzb, dyp, dyp * dsk, jnp.sum(dz, axis=0, keepdims=True), jnp.sum(dyp * ub, axis=0, keepdims=True))

    dz, dy_pre, du1, g_bglu, g_ssmd = _blockwise(
        "ssm_glu_bwd", glu_bwd_fn, [dy_ssm, y_pre, z, proj, w["ssm_d"], w["wglu"]],
        [_row_spec(t_row, ssm_w)] * 3 + [u_spec, _full_spec((1, ssm_w)), _full_spec((ssm_w, ssm_w))],
        [((seq, ssm_w), BF16), ((seq, ssm_w), BF16), ((seq, ssm_w), F32), ((1, ssm_w), F32), ((1, ssm_w), F32)],
        [_row_spec(t_row, ssm_w)] * 3 + [_full_spec((1, ssm_w))] * 2, g1, n_acc=2)
    g_wglu = _mm2d("ssm_glu_dw", yg, dz, TN, BF16)
    dq_raw, dkv, dkp_h = _attn_bwd(q_raw, kv, kpe, cos_t, sa_t, sb_t, dy_mla)

    def head_mm_dx(name, dact, wh):
        kdim, ndim = wh.shape[1], wh.shape[2]
        return _mm(name, dact, wh, grid=(1, 1, nh), contract=NT,
                   a_spec=pl.BlockSpec((None, seq, ndim), lambda i, j, h: (h, i, 0)),
                   b_spec=pl.BlockSpec((None, kdim, ndim), lambda i, j, h: (h, 0, 0)),
                   o_spec=pl.BlockSpec((seq, kdim), lambda i, j, h: (i, 0)),
                   out_shape=(seq, kdim), out_dtype=F32)

    def head_mm_dw(name, act, dact):
        kdim, ndim = act.shape[1], dact.shape[2]
        return _mm(name, act, dact, grid=(nh, 1, seq // tks), contract=TN,
                   a_spec=pl.BlockSpec((tks, kdim), lambda h, j, k: (k, 0)),
                   b_spec=pl.BlockSpec((None, tks, ndim), lambda h, j, k: (h, k, 0)),
                   o_spec=pl.BlockSpec((None, kdim, ndim), lambda h, j, k: (h, 0, 0)),
                   out_shape=(nh, kdim, ndim), out_dtype=BF16)

    g_wuq = head_mm_dw("mla_q_dw", qn, dq_raw)
    g_wukv = head_mm_dw("mla_kv_dw", kvn, dkv)
    dqn = head_mm_dx("mla_q_dx", dq_raw, w["wuq"])
    dkvn = head_mm_dx("mla_kv_dx", dkv, w["wukv"])
    emit(not_before=(dqn, dkvn, dy_pre), wout=g_wout, wuq=g_wuq, wukv=g_wukv, wglu=g_wglu, conv_w=g_convw)

    du, dwb, dwc, da_lay = _ssm_bwd(dy_pre, s_all, proj, du1, wb, wc, a_lay)
    g_c_re = blockdiag_out_t(dwc[:, :STATE_BLOCK, :])
    g_c_im = -blockdiag_out_t(dwc[:, STATE_BLOCK:, :])
    dbbt_re = blockdiag_in_t(dwb[:, :, :STATE_BLOCK])
    dbbt_im = blockdiag_in_t(dwb[:, :, STATE_BLOCK:])
    da3 = da_lay.reshape(nj, 2, STATE_BLOCK)
    dabar_re = da3[:, 0, :].reshape(n_groups, 1, SSM_STATE)
    dabar_im = da3[:, 1, :].reshape(n_groups, 1, SSM_STATE)
    g_lr3, g_li3, g_ldt3, g_bt_re, g_bt_im = _s5_prep_bwd(lr3, li3, ldt3, bt_re, bt_im,
                                                           dabar_re, dabar_im, dbbt_re, dbbt_im)

    def mla_prep_bwd_fn(cq, ckv, dqn_b, dkvn_b, dkp_b, cos, sa, sb, wq, wkv):
        dcq, dwq = _rms_bwd(cq, wq, dqn_b)
        dckv, dwkv = _rms_bwd(ckv, wkv, dkvn_b)
        dkp_sum = dkp_b[0]
        for h in range(1, nh):
            dkp_sum = dkp_sum + dkp_b[h]
        return dcq, dckv, _rope128_t(dkp_sum, cos, sa, sb), dwq, dwkv

    dc_q, dc_kv, dkpe_raw, g_qnorm, g_kvnorm = _blockwise(
        "mla_prep_bwd", mla_prep_bwd_fn, [c_q, c_kv, dqn, dkvn, dkp_h, cos_t, sa_t, sb_t, w["q_norm"], w["kv_norm"]],
        [_row_spec(t_row, q_rank), _row_spec(t_row, kv_rank), _row_spec(t_row, q_rank), _row_spec(t_row, kv_rank),
         pl.BlockSpec((nh, t_row, LANES), lambda i: (0, i, 0))] + [_row_spec(t_row, LANES)] * 3
        + [_full_spec((1, q_rank)), _full_spec((1, kv_rank))],
        [((seq, q_rank), BF16), ((seq, kv_rank), BF16), ((seq, LANES), BF16), ((1, q_rank), F32), ((1, kv_rank), F32)],
        [_row_spec(t_row, q_rank), _row_spec(t_row, kv_rank), _row_spec(t_row, LANES), _full_spec((1, q_rank)),
         _full_spec((1, kv_rank))], g1, n_acc=2)

    dproj = jnp.concatenate([du, dc_q, dc_kv, dkpe_raw], axis=1)
    g_win = _mm2d("proj_dw", hn, dproj, TN, BF16, tn=640)
    emit(win=g_win)
    def norm1_bwd_fn(dpb, wi, xb, dres, wv):
        dn = lax.dot_general(dpb, wi, (NT, ((), ())), preferred_element_type=F32)
        dx_, dw_ = _rms_bwd(xb, wv, dn)
        return dres + dx_, dw_

    grad_x, g_attn_norm = _blockwise(
        "proj_dx_norm1_bwd", norm1_bwd_fn, [dproj, w["win"], x, dh1, attn_w],
        [_row_spec(t_row, in_pad), _full_spec((d, in_pad)), _row_spec(t_row, d), _row_spec(t_row, d), _full_spec((1, d))],
        [((seq, d), F32), ((1, d), F32)], [_row_spec(t_row, d), _full_spec((1, d))], g1, n_acc=1)
    emit(win_pair_sums_after=grad_x)

    grads = dict(
        attn_norm=g_attn_norm, win=g_win, lam_re=g_lr3, lam_im=g_li3, log_dt=g_ldt3,
        bt_re=g_bt_re, bt_im=g_bt_im, c_re=g_c_re, c_im=g_c_im,
        ssm_d=g_ssmd, wglu=g_wglu, b_glu=g_bglu, q_norm=g_qnorm, wuq=g_wuq, kv_norm=g_kvnorm, wukv=g_wukv,
        son=g_son, mon=g_mon, wout=g_wout, ffn_norm=g_ffn_norm, wup=g_wup, conv_w=g_convw, conv_b=g_convb,
        wdown=g_wdown, final_norm=g_final)
    return loss, grad_x, grads


def _mesh_pos():
    return lax.axis_index("x"), lax.axis_index("y"), lax.axis_index("c")


def _handshake_all():
    x, y, c = _mesh_pos()
    barrier = pltpu.get_barrier_semaphore()
    for k in range(1, N_DEV):
        peer = (1 - x if k & 4 else x, 1 - y if k & 2 else y, 1 - c if k & 1 else c)
        pl.semaphore_signal(barrier, inc=1, device_id=peer, device_id_type=MESH)
    pl.semaphore_wait(barrier, N_DEV - 1)


def _handshake(peers):
    barrier = pltpu.get_barrier_semaphore()
    for peer in peers:
        pl.semaphore_signal(barrier, inc=1, device_id=peer, device_id_type=MESH)
    pl.semaphore_wait(barrier, len(peers))


def _comm_call(name, body, n, out_shape, ins, collective_id, after=None, copies=7, n_remote=None, n_local=None):
    n_remote = copies * n if n_remote is None else n_remote
    sems = [pltpu.SemaphoreType.DMA((n_remote,)), pltpu.SemaphoreType.DMA((n_remote,)),
            pltpu.SemaphoreType.DMA((n if n_local is None else n_local,))]
    if collective_id is None:
        any_spec = pl.BlockSpec(memory_space=pl.ANY)
        return pl.pallas_call(body, name=name, out_shape=out_shape, in_specs=[any_spec] * n,
                              out_specs=[any_spec] * n, scratch_shapes=sems)(*ins)
    seq_body = body
    if after:
        n_after = len(after)
        ins = list(ins) + list(after)

        def seq_body(*refs):
            body(*refs[:n], *refs[n + n_after:])

    return pl.kernel(seq_body, name=name, out_type=out_shape,
                     mesh=plsc.ScalarSubcoreMesh(axis_name="seq", num_cores=1), scratch_types=sems,
                     compiler_params=pltpu.CompilerParams(collective_id=collective_id))(*ins)


def _all_gather(name, xs, collective_id=None, after=None, pair_sums=()):
    n = len(xs)
    nh = len(pair_sums)
    m = n + nh

    def body(*refs):
        x_refs, h_refs, o_refs, e_refs = refs[:n], refs[n:m], refs[m:m + n], refs[m + n:2 * m]
        send_sems, recv_sems, local_sems = refs[2 * m:]
        if collective_id is not None:
            _handshake_all()
        finish_pairs = _chip_copies(h_refs, e_refs, send_sems, recv_sems, local_sems, 7 * n, n) if nh else None
        x, y, c = _mesh_pos()
        me, sibling = (x, y, c), (x, y, 1 - c)
        chips = [(1 - x, y), (x, 1 - y), (1 - x, 1 - y)]

        def slot(o_ref, px, py, pc):
            return o_ref.at[4 * px + 2 * py + pc]

        def copy(t, k, block, to, src=None):
            dst = slot(o_refs[t], *block)
            return pltpu.make_async_remote_copy(
                src_ref=dst if src is None else src, dst_ref=dst,
                send_sem=send_sems.at[7 * t + k], recv_sem=recv_sems.at[7 * t + k],
                device_id=to, device_id_type=MESH)

        started = []
        for t in range(n):
            mine = pltpu.make_async_copy(x_refs[t], slot(o_refs[t], *me), local_sems.at[t])
            mine.start()
            started.append(mine)
        first = []
        for t in range(n):
            first.append(copy(t, 0, me, sibling, src=x_refs[t]))
            first += [copy(t, 1 + j, me, (*chip, c), src=x_refs[t]) for j, chip in enumerate(chips)]
        for cp in first:
            cp.start()
        passed = []
        for j, chip in enumerate(chips):
            for t in range(n):
                copy(t, 1 + j, (*chip, c), me).wait_recv()
                fwd = copy(t, 4 + j, (*chip, c), sibling)
                fwd.start()
                passed.append(fwd)
        for t in range(n):
            copy(t, 0, sibling, me).wait_recv()
            for j, chip in enumerate(chips):
                copy(t, 4 + j, (*chip, 1 - c), me).wait_recv()
        for cp in first + passed:
            cp.wait_send()
        for mine in started:
            mine.wait()
        if nh:
            finish_pairs()

    out_shape = ([jax.ShapeDtypeStruct((N_DEV,) + v.shape, v.dtype) for v in xs]
                 + [jax.ShapeDtypeStruct(v.shape, v.dtype) for v in pair_sums])
    return _comm_call(name, body, m, out_shape, list(xs) + list(pair_sums), collective_id, after,
                      n_remote=7 * n + (N_CHIP - 1) * nh, n_local=m)


def _exchange_partials(name, gs, collective_id=None, after=None):
    n = len(gs)

    def body(*refs):
        g_refs, o_refs = refs[:n], refs[n:2 * n]
        send_sems, recv_sems, local_sems = refs[2 * n:]
        if collective_id is not None:
            _handshake_all()
        x, y, c = _mesh_pos()
        me_idx = 4 * x + 2 * y + c
        copies = []
        for t in range(n):
            mine = pltpu.make_async_copy(g_refs[t].at[me_idx], o_refs[t].at[me_idx], local_sems.at[t])
            mine.start()
            copies.append(mine)
        remote = []
        for k in range(1, N_DEV):
            px = 1 - x if k & 4 else x
            py = 1 - y if k & 2 else y
            pc = 1 - c if k & 1 else c
            p_idx = 4 * px + 2 * py + pc
            for t in range(n):
                cp = pltpu.make_async_remote_copy(
                    src_ref=g_refs[t].at[p_idx], dst_ref=o_refs[t].at[me_idx],
                    send_sem=send_sems.at[7 * t + k - 1], recv_sem=recv_sems.at[7 * t + k - 1],
                    device_id=(px, py, pc), device_id_type=MESH)
                cp.start()
                landing = pltpu.make_async_remote_copy(
                    src_ref=g_refs[t].at[p_idx], dst_ref=o_refs[t].at[p_idx],
                    send_sem=send_sems.at[7 * t + k - 1], recv_sem=recv_sems.at[7 * t + k - 1],
                    device_id=(px, py, pc), device_id_type=MESH)
                remote.append((cp, landing))
        for cp, landing in remote:
            landing.wait_recv()
        for cp, landing in remote:
            cp.wait_send()
        for mine in copies:
            mine.wait()

    out_shape = [jax.ShapeDtypeStruct(v.shape, v.dtype) for v in gs]
    return _comm_call(name, body, n, out_shape, gs, collective_id, after)


N_CHIP = N_DEV // 2
PAIR_ADD_BLOCK_ELEMS = 1024 * 1024


def _pair_swap(name, gs, collective_id, after=None):
    n = len(gs)

    def body(*refs):
        g_refs, o_refs = refs[:n], refs[n:2 * n]
        send_sems, recv_sems, _ = refs[2 * n:]
        x, y, c = _mesh_pos()
        sibling = (x, y, 1 - c)
        _handshake([sibling])
        copies = []
        for t in range(n):
            for k in range(N_CHIP):
                copies.append(pltpu.make_async_remote_copy(
                    src_ref=g_refs[t].at[2 * k + 1 - c], dst_ref=o_refs[t].at[k],
                    send_sem=send_sems.at[N_CHIP * t + k], recv_sem=recv_sems.at[N_CHIP * t + k],
                    device_id=sibling, device_id_type=MESH))
        for cp in copies:
            cp.start()
        for cp in copies:
            cp.wait_recv()
        for cp in copies:
            cp.wait_send()

    out_shape = [jax.ShapeDtypeStruct((N_CHIP,) + v.shape[1:], v.dtype) for v in gs]
    return _comm_call(name, body, n, out_shape, gs, collective_id, after, copies=N_CHIP)


def _pair_add(name, g, got):
    _, r, c = g.shape
    tr = r
    if r * c > PAIR_ADD_BLOCK_ELEMS and r % SUBLANES == 0:
        tr = SUBLANES
        while r % (tr * 2) == 0 and tr * 2 * c <= PAIR_ADD_BLOCK_ELEMS:
            tr *= 2

    def body(core_ref, g_ref, got_ref, o_ref):
        o_ref[...] = (g_ref[...].astype(F32) + got_ref[...].astype(F32)).astype(o_ref.dtype)

    grid_spec = pltpu.PrefetchScalarGridSpec(
        num_scalar_prefetch=1, grid=(N_CHIP, r // tr),
        in_specs=[pl.BlockSpec((None, None, tr, c), lambda k, i, core: (k, core[0], i, 0)),
                  pl.BlockSpec((None, tr, c), lambda k, i, core: (k, i, 0))],
        out_specs=pl.BlockSpec((None, tr, c), lambda k, i, core: (k, i, 0)))
    core = lax.axis_index("c").astype(jnp.int32).reshape(1)
    return pl.pallas_call(body, name=name, grid_spec=grid_spec, out_shape=jax.ShapeDtypeStruct((N_CHIP, r, c), g.dtype),
                          compiler_params=_cparams())(core, g.reshape(N_CHIP, 2, r, c), got)


def _chip_copies(h_refs, o_refs, send_sems, recv_sems, local_sems, sem0, local0):
    n = len(h_refs)
    per = N_CHIP - 1
    x, y, c = _mesh_pos()
    others = [(1 - x if k & 2 else x, 1 - y if k & 1 else y) for k in range(1, N_CHIP)]
    my_chip = 2 * x + y
    local = []
    for t in range(n):
        mine = pltpu.make_async_copy(h_refs[t].at[my_chip], o_refs[t].at[my_chip], local_sems.at[local0 + t])
        mine.start()
        local.append(mine)
    remote = []
    for j, (px, py) in enumerate(others):
        chip = 2 * px + py
        for t in range(n):
            sems = dict(send_sem=send_sems.at[sem0 + per * t + j], recv_sem=recv_sems.at[sem0 + per * t + j],
                        device_id=(px, py, c), device_id_type=MESH)
            cp = pltpu.make_async_remote_copy(src_ref=h_refs[t].at[chip], dst_ref=o_refs[t].at[my_chip], **sems)
            cp.start()
            landing = pltpu.make_async_remote_copy(src_ref=h_refs[t].at[chip], dst_ref=o_refs[t].at[chip], **sems)
            remote.append((cp, landing))

    def finish():
        for cp, landing in remote:
            landing.wait_recv()
        for cp, landing in remote:
            cp.wait_send()
        for mine in local:
            mine.wait()

    return finish


def _chip_exchange(name, hs, collective_id, after=None):
    n = len(hs)
    per = N_CHIP - 1

    def body(*refs):
        h_refs, o_refs = refs[:n], refs[n:2 * n]
        send_sems, recv_sems, local_sems = refs[2 * n:]
        x, y, c = _mesh_pos()
        _handshake([(1 - x if k & 2 else x, 1 - y if k & 1 else y, c) for k in range(1, N_CHIP)])
        _chip_copies(h_refs, o_refs, send_sems, recv_sems, local_sems, 0, 0)()

    out_shape = [jax.ShapeDtypeStruct(v.shape, v.dtype) for v in hs]
    return _comm_call(name, body, n, out_shape, hs, collective_id, after, copies=per)


ADAM_BLOCK_ELEMS = 128 * 1024


def _sum_parts(pb):
    g = pb[0].astype(F32)
    for j in range(1, pb.shape[0]):
        g = g + pb[j].astype(F32)
    return g


def _adam_math(g, wb_, mb, vb):
    m_new = ADAM_B1 * mb + (1.0 - ADAM_B1) * g
    v_new = ADAM_B2 * vb + (1.0 - ADAM_B2) * (g * g)
    m_hat = m_new / (1.0 - ADAM_B1 ** ADAM_STEP)
    v_hat = v_new / (1.0 - ADAM_B2 ** ADAM_STEP)
    delta = -ADAM_LR * (m_hat / (jnp.sqrt(v_hat) + ADAM_EPS) + ADAM_WD * wb_)
    return g, delta, m_new, v_new


def _adamw_multi(name, items, nblk=1, packed=None):
    n = len(items)

    def spec(shape, lead):
        blk = list(shape)
        blk[lead + 1] = shape[lead + 1] // nblk
        if nblk == 1:
            return pl.BlockSpec(tuple(blk), lambda i, nd=len(shape): (0,) * nd)
        return pl.BlockSpec(tuple(blk), lambda i, nd=len(shape), ax=lead + 1: (0,) * ax + (i,) + (0,) * (nd - ax - 1))

    ins, in_specs, out_specs, out_shape, where = [], [], [], [], []
    if packed is not None:
        ins.append(packed)
        in_specs.append(spec(packed.shape, 1))
    for parts, wv, mv, vv in items:
        if isinstance(parts, int):
            where.append((0, parts, len(ins)))
        else:
            assert parts.shape[1:] == wv.shape, (name, parts.shape, wv.shape)
            where.append((len(ins), None, len(ins) + 1))
            ins.append(parts)
            in_specs.append(spec(parts.shape, 1))
        ins += [wv, mv, vv]
        in_specs += [spec(wv.shape, 0)] * 3
        out_specs += [spec(wv.shape, 0)] * 4
        out_shape += [jax.ShapeDtypeStruct(wv.shape, F32)] * 4
    n_in = len(ins)

    def body(*refs):
        for t, (ip, off, iw) in enumerate(where):
            wr, mr, vr = refs[iw:iw + 3]
            parts = refs[ip][...] if off is None else refs[ip][:, :, off:off + wr.shape[-1]]
            res = _adam_math(_sum_parts(parts), wr[...], mr[...], vr[...])
            for o, val in zip(refs[n_in + 4 * t:n_in + 4 * t + 4], res):
                o[...] = val

    res = pl.pallas_call(body, name=name, grid=(nblk,), in_specs=in_specs, out_specs=out_specs, out_shape=out_shape,
                         compiler_params=_cparams())(*ins)
    return [tuple(res[4 * t:4 * t + 4]) for t in range(n)]


def _sum_multi(name, parts_list):
    def body(*refs):
        for pr, o in zip(refs[:len(parts_list)], refs[len(parts_list):]):
            o[...] = _sum_parts(pr[...])

    return pl.pallas_call(body, name=name, out_shape=[jax.ShapeDtypeStruct(p.shape[1:], F32) for p in parts_list],
                          compiler_params=_cparams())(*parts_list)


def _adamw_sum(name, parts, wv, mv, vv):
    npart, r, c = parts.shape
    tr = r
    if r * c > ADAM_BLOCK_ELEMS and r % SUBLANES == 0:
        tr = SUBLANES
        while r % (tr * 2) == 0 and tr * 2 * c <= ADAM_BLOCK_ELEMS:
            tr *= 2

    def fn(pb, wb_, mb, vb):
        return _adam_math(_sum_parts(pb), wb_, mb, vb)

    row = pl.BlockSpec((tr, c), lambda i: (i, 0))
    return _blockwise(name, fn, [parts, wv, mv, vv],
                      [pl.BlockSpec((npart, tr, c), lambda i: (0, i, 0)), row, row, row],
                      [((r, c), F32)] * 4, [row] * 4, (r // tr,))


_VECTORS = ["attn_norm", "lam_re", "lam_im", "log_dt", "ssm_d", "b_glu", "q_norm", "kv_norm", "son", "mon",
            "ffn_norm", "conv_b", "final_norm"]
_PACKED = ["attn_norm", "ssm_d", "b_glu", "q_norm", "kv_norm", "son", "mon", "ffn_norm", "conv_b", "final_norm"]
_BIG = ["win", "wglu", "wuq", "wukv", "wout", "wup", "wdown", "conv_w"]
_TWO_LEVEL = ("wup", "win")
_AFTER = "_pair_sums_after"
_ORDER = ["attn_norm", "win", "lam_re", "lam_im", "log_dt", "b_re", "b_im", "c_re", "c_im", "ssm_d", "wglu",
          "b_glu", "q_norm", "wuq", "kv_norm", "wukv", "son", "mon", "wout", "ffn_norm", "wup", "conv_w",
          "conv_b", "wdown", "final_norm"]


def kernel(x, positions, attn_norm_w, w_in, ssm_lambda_re, ssm_lambda_im, ssm_log_dt, ssm_b_re, ssm_b_im, ssm_c_re, ssm_c_im, ssm_d, ssm_w_glu, ssm_b_glu, mla_q_norm_w, mla_w_uq, mla_kv_norm_w, mla_w_ukv, ssm_out_norm_w, mla_out_norm_w, w_out, ffn_norm_w, ffn_w_up, ffn_conv_w, ffn_conv_b, ffn_w_down, final_norm_w, loss_target, m_attn_norm_w, m_w_in, m_ssm_lambda_re, m_ssm_lambda_im, m_ssm_log_dt, m_ssm_b_re, m_ssm_b_im, m_ssm_c_re, m_ssm_c_im, m_ssm_d, m_ssm_w_glu, m_ssm_b_glu, m_mla_q_norm_w, m_mla_w_uq, m_mla_kv_norm_w, m_mla_w_ukv, m_ssm_out_norm_w, m_mla_out_norm_w, m_w_out, m_ffn_norm_w, m_ffn_w_up, m_ffn_conv_w, m_ffn_conv_b, m_ffn_w_down, m_final_norm_w, v_attn_norm_w, v_w_in, v_ssm_lambda_re, v_ssm_lambda_im, v_ssm_log_dt, v_ssm_b_re, v_ssm_b_im, v_ssm_c_re, v_ssm_c_im, v_ssm_d, v_ssm_w_glu, v_ssm_b_glu, v_mla_q_norm_w, v_mla_w_uq, v_mla_kv_norm_w, v_mla_w_ukv, v_ssm_out_norm_w, v_mla_out_norm_w, v_w_out, v_ffn_norm_w, v_ffn_w_up, v_ffn_conv_w, v_ffn_conv_b, v_ffn_w_down, v_final_norm_w):
    wts = dict(attn_norm=attn_norm_w, win=w_in, lam_re=ssm_lambda_re, lam_im=ssm_lambda_im, log_dt=ssm_log_dt,
               b_re=ssm_b_re, b_im=ssm_b_im, c_re=ssm_c_re, c_im=ssm_c_im, ssm_d=ssm_d, wglu=ssm_w_glu,
               b_glu=ssm_b_glu, q_norm=mla_q_norm_w, wuq=mla_w_uq, kv_norm=mla_kv_norm_w, wukv=mla_w_ukv,
               son=ssm_out_norm_w, mon=mla_out_norm_w, wout=w_out, ffn_norm=ffn_norm_w, wup=ffn_w_up,
               conv_w=ffn_conv_w, conv_b=ffn_conv_b, wdown=ffn_w_down, final_norm=final_norm_w)
    moms = dict(zip(_ORDER, [m_attn_norm_w, m_w_in, m_ssm_lambda_re, m_ssm_lambda_im, m_ssm_log_dt, m_ssm_b_re,
                             m_ssm_b_im, m_ssm_c_re, m_ssm_c_im, m_ssm_d, m_ssm_w_glu, m_ssm_b_glu, m_mla_q_norm_w,
                             m_mla_w_uq, m_mla_kv_norm_w, m_mla_w_ukv, m_ssm_out_norm_w, m_mla_out_norm_w, m_w_out,
                             m_ffn_norm_w, m_ffn_w_up, m_ffn_conv_w, m_ffn_conv_b, m_ffn_w_down, m_final_norm_w]))
    vels = dict(zip(_ORDER, [v_attn_norm_w, v_w_in, v_ssm_lambda_re, v_ssm_lambda_im, v_ssm_log_dt, v_ssm_b_re,
                             v_ssm_b_im, v_ssm_c_re, v_ssm_c_im, v_ssm_d, v_ssm_w_glu, v_ssm_b_glu, v_mla_q_norm_w,
                             v_mla_w_uq, v_mla_kv_norm_w, v_mla_w_ukv, v_ssm_out_norm_w, v_mla_out_norm_w, v_w_out,
                             v_ffn_norm_w, v_ffn_w_up, v_ffn_conv_w, v_ffn_conv_b, v_ffn_w_down, v_final_norm_w]))
    seq, d = x.shape[1], x.shape[2]
    in_width = w_in.shape[2]
    in_pad = -(-in_width // LANES) * LANES
    q_cols = mla_w_uq.shape[2]
    q_pad = 2 * LANES

    (win_g,) = _all_gather("gather_w_in", [jnp.pad(w_in[0], ((0, 0), (0, in_pad - in_width))).astype(BF16)])
    wglu_g, wuq_g, wukv_g, wout_g, convw_g = _all_gather(
        "gather_mix", [ssm_w_glu[0].astype(BF16), jnp.pad(mla_w_uq[0], ((0, 0), (0, q_pad - q_cols))).astype(BF16),
                       mla_w_ukv[0].astype(BF16), w_out[0].astype(BF16), ffn_conv_w[0]], collective_id=0)
    (wup_g,) = _all_gather("gather_ffn_up", [ffn_w_up[0].astype(BF16)], collective_id=1)
    (wdown_g,) = _all_gather("gather_ffn_down", [ffn_w_down[0].astype(BF16)], collective_id=2)
    ns = N_DEV
    c_ff = wup_g.shape[2]
    w = dict(
        attn_norm=attn_norm_w, win=win_g.reshape(d, in_pad), lam_re=ssm_lambda_re, lam_im=ssm_lambda_im,
        log_dt=ssm_log_dt, b_re=ssm_b_re, b_im=ssm_b_im, c_re=ssm_c_re, c_im=ssm_c_im, ssm_d=ssm_d,
        wglu=wglu_g.reshape(d // 2, d // 2), b_glu=ssm_b_glu, q_norm=mla_q_norm_w, wuq=wuq_g,
        kv_norm=mla_kv_norm_w, wukv=wukv_g, son=ssm_out_norm_w, mon=mla_out_norm_w, wout=wout_g.reshape(d, d),
        ffn_norm=ffn_norm_w, wup=wup_g, conv_w=convw_g, conv_b=ffn_conv_b,
        wdown=wdown_g.reshape(ns // 2 * c_ff, d), final_norm=final_norm_w)

    shard_layout = dict(
        win=lambda a: a[:, :in_width].reshape(N_DEV, d // N_DEV, in_width),
        wglu=lambda a: a.reshape(N_DEV, d // 2 // N_DEV, d // 2),
        wuq=lambda a: a[:, :, :q_cols], wukv=lambda a: a, wout=lambda a: a.reshape(N_DEV, d // N_DEV, d),
        wup=lambda a: a, wdown=lambda a: a.reshape(N_DEV, c_ff // 2, d), conv_w=lambda a: a)
    recv = {}
    next_id = [3]

    last = [None]

    out = {}

    def update(k):
        shp = wts[k].shape
        r, c = shp[-2], shp[-1]
        res = _adamw_sum("adamw_" + k, recv[k].reshape(-1, r, c), wts[k].reshape(r, c),
                         moms[k].reshape(r, c), vels[k].reshape(r, c))
        out[k] = [a.reshape(shp) for a in res]
        return res[0]

    pending = {}

    def exchange(not_before=(), **grads):
        names = list(grads)
        if len(names) == 1 and names[0] in _TWO_LEVEL:
            k = names[0]
            parts = shard_layout[k](grads[k])
            got = _pair_swap("swap_" + k, [parts], collective_id=next_id[0], after=[last[0]])[0]
            next_id[0] += 1
            pending[k] = (parts, got)
            last[0] = got
            return
        if len(names) == 1 and names[0].endswith(_AFTER):
            k = names[0][:-len(_AFTER)]
            sums = _pair_add("pair_add_" + k, *pending[k])
            if k == "win":
                pending["tail"] = sums
                return
            recv[k] = _chip_exchange("exchange_" + k, [sums], collective_id=next_id[0],
                                     after=[last[0], grads[names[0]]])[0]
            next_id[0] += 1
            last[0] = recv[k]
            return
        if "wout" in names:
            not_before = (*not_before, update("wdown"))
        got = _exchange_partials("exchange_" + "_".join(names), [shard_layout[k](grads[k]) for k in names],
                                 collective_id=next_id[0], after=[a for a in (last[0], *not_before) if a is not None])
        next_id[0] += 1
        last[0] = got[-1]
        recv.update(zip(names, got))

    loss_part, grad_x, g = _local_step(x[0], positions[0], loss_target[0], w, emit=exchange)
    loss = lax.psum(loss_part, ("x", "y", "c"))
    n_groups = ssm_lambda_re.shape[1]
    two_d = {"lam_re": (n_groups, -1), "lam_im": (n_groups, -1)}
    dense = {k: g[k].reshape(two_d.get(k, (1, -1))) for k in _VECTORS}
    offsets, width = {}, 0
    for k in _PACKED:
        offsets[k] = width
        width += dense[k].shape[1]
    sent = dict(packed=jnp.concatenate([dense[k] for k in _PACKED], axis=1),
                **{k: dense[k] for k in _VECTORS if k not in _PACKED},
                c_re=g["c_re"], c_im=g["c_im"], bt_re=g["bt_re"], bt_im=g["bt_im"])
    names = list(sent)
    got = _all_gather("gather_small_grads", [sent[k] for k in names], collective_id=next_id[0], after=[last[0]],
                      pair_sums=[pending["tail"]])
    gathered = dict(zip(names, got))
    recv["win"] = got[len(names)]
    for k in _BIG:
        if k not in out and k != "win":
            update(k)
    update("win")

    def finish(keys, results):
        for k, res in zip(keys, results):
            out[k] = [a.reshape(wts[k].shape) for a in res]

    view = lambda k, a: a.reshape(dense[k].shape)
    finish(_VECTORS, _adamw_multi("adamw_vectors", [(offsets.get(k, gathered.get(k)), view(k, wts[k]), view(k, moms[k]),
                                                     view(k, vels[k])) for k in _VECTORS], packed=gathered["packed"]))
    c_keys = ["c_re", "c_im"]
    finish(c_keys, _adamw_multi("adamw_ssm_c", [(gathered[k][:, None], wts[k], moms[k], vels[k]) for k in c_keys]))
    b_sums = _sum_multi("sum_ssm_b", [gathered["bt_re"], gathered["bt_im"]])
    b_keys = ["b_re", "b_im"]
    finish(b_keys, _adamw_multi("adamw_ssm_b", [(jnp.swapaxes(s, 1, 2)[None, None], wts[k], moms[k], vels[k])
                                                for k, s in zip(b_keys, b_sums)], nblk=SUBLANES))

    grad_x = grad_x.reshape(x.shape)
    return (loss, grad_x, *[out[k][0] for k in _ORDER], *[out[k][1] for k in _ORDER],
            *[out[k][2] for k in _ORDER], *[out[k][3] for k in _ORDER])
```

```python
import functools
import math

import jax
import jax.numpy as jnp
from jax import lax
from jax.experimental import pallas as pl
from jax.experimental.pallas import tpu as pltpu
from jax.experimental.pallas import tpu_sc as plsc

F32 = jnp.float32
BF16 = jnp.bfloat16
MESH = pl.DeviceIdType.MESH

N_DEV = 8
LANES = 128
SUBLANES = 8
VMEM_LIMIT = 48 * 1024 * 1024

SSM_GROUP = 16
SSM_STATE = 64
GROUPS_PER_BLOCK = LANES // SSM_GROUP
STATE_BLOCK = GROUPS_PER_BLOCK * SSM_STATE
QK_NOPE = 128
QK_ROPE = 64
V_DIM = 128
ROPE_THETA = 10000.0
RMS_EPS = 1e-6

ADAM_LR = 0.001
ADAM_B1 = 0.9
ADAM_B2 = 0.999
ADAM_EPS = 1e-08
ADAM_WD = 0.01
ADAM_STEP = 10

NN = ((1,), (0,))
NT = ((1,), (1,))
TN = ((0,), (0,))


def _cparams():
    return pltpu.CompilerParams(vmem_limit_bytes=VMEM_LIMIT)


def _tile(n, want):
    if n <= want:
        return n
    t = (want // LANES) * LANES
    while t >= LANES:
        if n % t == 0:
            return t
        t -= LANES
    return n


def _mm(name, a, b, *, grid, a_spec, b_spec, o_spec, out_shape, out_dtype, contract=NN,
        res=None, res_spec=None):
    nk = grid[-1]
    kaxis = len(grid) - 1
    acc_shape = tuple(d for d in o_spec.block_shape if d is not None)

    def body(*refs):
        a_ref, b_ref = refs[:2]
        r_ref = None if res is None else refs[2]
        o_ref = refs[2 if res is None else 3]
        part = lax.dot_general(a_ref[...].astype(BF16), b_ref[...].astype(BF16),
                               (contract, ((), ())), preferred_element_type=F32)
        if nk == 1:
            if r_ref is not None:
                part = part + r_ref[...].astype(F32)
            o_ref[...] = part.astype(o_ref.dtype)
            return
        acc = refs[-1]
        k = pl.program_id(kaxis)

        @pl.when(k == 0)
        def _():
            acc[...] = part

        @pl.when(k != 0)
        def _():
            acc[...] += part

        @pl.when(k == nk - 1)
        def _():
            r = acc[...]
            if r_ref is not None:
                r = r + r_ref[...].astype(F32)
            o_ref[...] = r.astype(o_ref.dtype)

    ins = [a, b] + ([] if res is None else [res])
    in_specs = [a_spec, b_spec] + ([] if res is None else [res_spec])
    return pl.pallas_call(
        body, name=name, grid=grid, in_specs=in_specs, out_specs=o_spec,
        out_shape=jax.ShapeDtypeStruct(out_shape, out_dtype),
        scratch_shapes=[pltpu.VMEM(acc_shape, F32)] if nk > 1 else [], compiler_params=_cparams(),
    )(*ins)


def _mm2d(name, a, b, contract, out_dtype, tm=1024, tn=1024, tk=2048, res=None):
    if contract == NN:
        (m, kk), n = a.shape, b.shape[1]
    elif contract == NT:
        (m, kk), n = a.shape, b.shape[0]
    else:
        (kk, m), n = a.shape, b.shape[1]
    tm, tn, tk = _tile(m, tm), _tile(n, tn), _tile(kk, tk)
    grid = (m // tm, n // tn, kk // tk)
    if contract == TN:
        a_spec = pl.BlockSpec((tk, tm), lambda i, j, k: (k, i))
    else:
        a_spec = pl.BlockSpec((tm, tk), lambda i, j, k: (i, k))
    if contract == NT:
        b_spec = pl.BlockSpec((tn, tk), lambda i, j, k: (j, k))
    else:
        b_spec = pl.BlockSpec((tk, tn), lambda i, j, k: (k, j))
    o_spec = pl.BlockSpec((tm, tn), lambda i, j, k: (i, j))
    res_spec = None
    if res is not None:
        if res.shape[0] == 1:
            res_spec = pl.BlockSpec((1, tn), lambda i, j, k: (0, j))
        else:
            res_spec = pl.BlockSpec((tm, tn), lambda i, j, k: (i, j))
    return _mm(name, a, b, grid=grid, a_spec=a_spec, b_spec=b_spec, o_spec=o_spec,
               out_shape=(m, n), out_dtype=out_dtype, contract=contract, res=res, res_spec=res_spec)


def _blockwise(name, fn, ins, in_specs, outs, out_specs, grid, n_acc=0, acc_all=True):
    n_in, n_out = len(ins), len(outs)
    n_plain = n_out - n_acc

    def body(*refs):
        vals = fn(*[r[...] for r in refs[:n_in]])
        if not isinstance(vals, (tuple, list)):
            vals = (vals,)
        o_refs = refs[n_in:n_in + n_out]
        for r, v in zip(o_refs[:n_plain], vals[:n_plain]):
            r[...] = v.astype(r.dtype)
        if n_acc:
            if acc_all:
                first = functools.reduce(jnp.logical_and, [pl.program_id(d) == 0 for d in range(len(grid))])
            else:
                first = pl.program_id(len(grid) - 1) == 0

            @pl.when(first)
            def _():
                for r, v in zip(o_refs[n_plain:], vals[n_plain:]):
                    r[...] = v.astype(r.dtype)

            @pl.when(jnp.logical_not(first))
            def _():
                for r, v in zip(o_refs[n_plain:], vals[n_plain:]):
                    r[...] += v.astype(r.dtype)

    return pl.pallas_call(
        body, name=name, grid=grid, in_specs=in_specs, out_specs=out_specs,
        out_shape=[jax.ShapeDtypeStruct(s, d) for s, d in outs], compiler_params=_cparams(),
    )(*ins)


def _row_spec(t, c):
    return pl.BlockSpec((t, c), lambda i: (i, 0))


def _full_spec(shape):
    nd = len(shape)
    return pl.BlockSpec(tuple(shape), lambda *g: (0,) * nd)


def _rms(xf, w):
    return xf * lax.rsqrt(jnp.mean(xf * xf, axis=-1, keepdims=True) + RMS_EPS) * w


def _rms_bwd(xf, w, dy):
    _, vjp = jax.vjp(_rms, xf, w)
    return vjp(dy)


def _s5_disc(lr, li, ldt, bre, bim):
    dt = jnp.exp(ldt)
    mag = jnp.exp(lr * dt)
    ar = mag * jnp.cos(li * dt)
    ai = mag * jnp.sin(li * dt)
    nr, ni = ar - 1.0, ai
    den = lr * lr + li * li
    zr = (nr * lr + ni * li) / den
    zi = (ni * lr - nr * li) / den
    return ar, ai, zr * bre - zi * bim, zr * bim + zi * bre


def _s5_prep(lr, li, ldt, bre, bim):
    def body(lr_r, li_r, ldt_r, bre_r, bim_r, ar_r, ai_r, br_r, bi_r):
        ar, ai, br, bi = _s5_disc(lr_r[...], li_r[...], ldt_r[...], bre_r[...], bim_r[...])
        ar_r[...] = ar
        ai_r[...] = ai
        br_r[...] = br
        bi_r[...] = bi

    sd = jax.ShapeDtypeStruct
    return pl.pallas_call(
        body, name="s5_prep",
        out_shape=[sd(lr.shape, F32), sd(lr.shape, F32), sd(bre.shape, F32), sd(bre.shape, F32)],
        compiler_params=_cparams(),
    )(lr, li, ldt, bre, bim)


def _s5_prep_bwd(lr, li, ldt, bre, bim, dar, dai, dbr, dbi):
    def body(lr_r, li_r, ldt_r, bre_r, bim_r, dar_r, dai_r, dbr_r, dbi_r, o0, o1, o2, o3, o4):
        _, vjp = jax.vjp(_s5_disc, lr_r[...], li_r[...], ldt_r[...], bre_r[...], bim_r[...])
        g = vjp((dar_r[...], dai_r[...], dbr_r[...], dbi_r[...]))
        for o, v in zip((o0, o1, o2, o3, o4), g):
            o[...] = v

    sd = jax.ShapeDtypeStruct
    return pl.pallas_call(
        body, name="s5_prep_bwd",
        out_shape=[sd(lr.shape, F32), sd(li.shape, F32), sd(ldt.shape, F32), sd(bre.shape, F32), sd(bim.shape, F32)],
        compiler_params=_cparams(),
    )(lr, li, ldt, bre, bim, dar, dai, dbr, dbi)


SCAN_T = 256


def _scan_tables(ar, ai, tab_r, tab_i, sub, reverse):
    pr, pi = ar, ai
    for k in range(sub):
        row = sub - 1 - k if reverse else k
        tab_r[row:row + 1, :] = pr
        tab_i[row:row + 1, :] = pi
        pr, pi = ar * pr - ai * pi, ar * pi + ai * pr


def _pack_matrix(t_blk, dtype):
    sub = t_blk // SUBLANES
    dst = jnp.arange(t_blk)
    src = (dst % SUBLANES) * sub + dst // SUBLANES
    return (src[:, None] == jnp.arange(t_blk)[None, :]).astype(dtype)


def _permute_rows_f32(pm, x):
    hi = x.astype(BF16)
    r1 = x - hi.astype(F32)
    mid = r1.astype(BF16)
    lo = (r1 - mid.astype(F32)).astype(BF16)
    dot = lambda v: jnp.dot(pm, v, preferred_element_type=F32)
    return dot(hi) + dot(mid) + dot(lo)


def _scan_block(x, loc, ar, ai, st, tab_r, tab_i, sub, reverse):
    hb = STATE_BLOCK
    a8r = jnp.broadcast_to(ar, (SUBLANES, hb))
    a8i = jnp.broadcast_to(ai, (SUBLANES, hb))
    sr = jnp.zeros((SUBLANES, hb), F32)
    si = jnp.zeros((SUBLANES, hb), F32)
    steps = range(sub - 1, -1, -1) if reverse else range(sub)
    for t in steps:
        rows = slice(t * SUBLANES, (t + 1) * SUBLANES)
        sr, si = a8r * sr - a8i * si + x[rows, :hb], a8r * si + a8i * sr + x[rows, hb:]
        loc[rows, :hb] = sr
        loc[rows, hb:] = si
    cr, ci = st[0:1, :], st[1:2, :]
    far = 0 if reverse else sub - 1
    fr, fi = tab_r[far:far + 1, :], tab_i[far:far + 1, :]
    ent_r, ent_i = [None] * SUBLANES, [None] * SUBLANES
    for c in (range(SUBLANES - 1, -1, -1) if reverse else range(SUBLANES)):
        ent_r[c], ent_i[c] = cr, ci
        cr, ci = sr[c:c + 1, :] + (fr * cr - fi * ci), si[c:c + 1, :] + (fr * ci + fi * cr)
    st[0:1, :] = cr
    st[1:2, :] = ci
    c8r = jnp.concatenate(ent_r, axis=0)
    c8i = jnp.concatenate(ent_i, axis=0)
    out = []
    for t in range(sub):
        rows = slice(t * SUBLANES, (t + 1) * SUBLANES)
        tr, ti = tab_r[t:t + 1, :], tab_i[t:t + 1, :]
        out.append(jnp.concatenate([loc[rows, :hb] + (tr * c8r - ti * c8i), loc[rows, hb:] + (tr * c8i + ti * c8r)],
                                   axis=1))
    return jnp.concatenate(out, axis=0)


SSM_BLOCKS_PER_STEP = 2


def _scan_scratch(nblk, t_blk, sub, hb):
    return [pltpu.VMEM((nblk, SUBLANES, hb), F32), pltpu.VMEM((nblk, sub, hb), F32), pltpu.VMEM((nblk, sub, hb), F32),
            pltpu.VMEM((nblk, t_blk, 2 * hb), F32)]


def _ssm_fwd(proj, wb, wc, a):
    seq = proj.shape[0]
    nj = wb.shape[0]
    w2 = 2 * STATE_BLOCK
    hb = STATE_BLOCK
    t_blk = min(SCAN_T, seq)
    sub = t_blk // SUBLANES
    pm = _pack_matrix(t_blk, BF16)

    npair = SSM_BLOCKS_PER_STEP

    def body(u_ref, wb_ref, wc_ref, a_ref, pm_ref, pmt_ref, s_ref, y_ref, st, tab_r, tab_i, loc):
        coef = [(a_ref[:, b * w2:b * w2 + hb], a_ref[:, b * w2 + hb:(b + 1) * w2]) for b in range(npair)]

        @pl.when(pl.program_id(1) == 0)
        def _():
            for b, (ar, ai) in enumerate(coef):
                st[b] = jnp.zeros((SUBLANES, hb), F32)
                _scan_tables(ar, ai, tab_r.at[b], tab_i.at[b], sub, False)

        for b, (ar, ai) in enumerate(coef):
            ub = u_ref[:, b * LANES:(b + 1) * LANES].astype(BF16)
            up = jnp.dot(pm_ref[...], ub, preferred_element_type=F32).astype(BF16)
            bu = jnp.dot(up, wb_ref[b], preferred_element_type=F32)
            s = _scan_block(bu, loc.at[b], ar, ai, st.at[b], tab_r.at[b], tab_i.at[b], sub, False)
            s_ref[:, b * w2:(b + 1) * w2] = s
            yp = jnp.dot(s.astype(BF16), wc_ref[b], preferred_element_type=F32)
            y_ref[:, b * LANES:(b + 1) * LANES] = _permute_rows_f32(pmt_ref[...], yp)

    sd = jax.ShapeDtypeStruct
    return pl.pallas_call(
        body, name="ssm_fwd", grid=(nj // npair, seq // t_blk),
        in_specs=[pl.BlockSpec((t_blk, npair * LANES), lambda j, i: (i, j)),
                  pl.BlockSpec((npair, LANES, w2), lambda j, i: (j, 0, 0)),
                  pl.BlockSpec((npair, w2, LANES), lambda j, i: (j, 0, 0)),
                  pl.BlockSpec((1, npair * w2), lambda j, i: (0, j)),
                  _full_spec((t_blk, t_blk)), _full_spec((t_blk, t_blk))],
        out_specs=[pl.BlockSpec((t_blk, npair * w2), lambda j, i: (i, j)),
                   pl.BlockSpec((t_blk, npair * LANES), lambda j, i: (i, j))],
        out_shape=[sd((seq, nj * w2), F32), sd((seq, nj * LANES), F32)],
        scratch_shapes=_scan_scratch(npair, t_blk, sub, hb), compiler_params=_cparams(),
    )(proj, wb, wc, a, pm, pm.T)


def _ssm_bwd(dy, s, proj, du1, wb, wc, a):
    seq = dy.shape[0]
    nj = wb.shape[0]
    w2 = 2 * STATE_BLOCK
    hb = STATE_BLOCK
    t_blk = min(SCAN_T, seq)
    sub = t_blk // SUBLANES
    nb = seq // t_blk
    pm = _pack_matrix(t_blk, BF16)

    npair = SSM_BLOCKS_PER_STEP

    def body(dy_ref, s_ref, sprev_ref, u_ref, du1_ref, wb_ref, wc_ref, a_ref, pm_ref, pmt_ref,
             du_ref, dwb_ref, dwc_ref, da_ref, st, tab_r, tab_i, loc):
        ib = pl.program_id(1)
        pmv = pm_ref[...]
        coef = [(a_ref[:, b * w2:b * w2 + hb], -a_ref[:, b * w2 + hb:(b + 1) * w2]) for b in range(npair)]

        @pl.when(ib == 0)
        def _():
            for b, (ar, ai) in enumerate(coef):
                st[b] = jnp.zeros((SUBLANES, hb), F32)
                _scan_tables(ar, ai, tab_r.at[b], tab_i.at[b], sub, True)

        sums = []
        for b, (ar, ai) in enumerate(coef):
            cols, wide = slice(b * LANES, (b + 1) * LANES), slice(b * w2, (b + 1) * w2)
            dyp = jnp.dot(pmv, dy_ref[:, cols], preferred_element_type=F32).astype(BF16)
            up = jnp.dot(pmv, u_ref[:, cols].astype(BF16), preferred_element_type=F32).astype(BF16)
            ds = lax.dot_general(dyp, wc_ref[b], (NT, ((), ())), preferred_element_type=F32)
            lam = _scan_block(ds, loc.at[b], ar, ai, st.at[b], tab_r.at[b], tab_i.at[b], sub, True)
            lamb = lam.astype(BF16)
            du = lax.dot_general(lamb, wb_ref[b], (NT, ((), ())), preferred_element_type=F32)
            du_ref[:, cols] = (_permute_rows_f32(pmt_ref[...], du) + du1_ref[:, cols]).astype(du_ref.dtype)
            sv = s_ref[:, wide]
            dwb = lax.dot_general(up, lamb, (TN, ((), ())), preferred_element_type=F32)
            dwc = lax.dot_general(sv.astype(BF16), dyp, (TN, ((), ())), preferred_element_type=F32)

            prev_last = sprev_ref[SUBLANES - 1:SUBLANES, wide]
            prev_last = jnp.where(ib == nb - 1, jnp.zeros_like(prev_last), prev_last)
            tail = sv[t_blk - SUBLANES:, :]
            sl = lax.broadcasted_iota(jnp.int32, tail.shape, 0)
            head = jnp.where(sl >= 1, pltpu.roll(tail, 1, 0), prev_last)
            s_sh = jnp.concatenate([head, sv[:t_blk - SUBLANES, :]], axis=0)
            lam_r, lam_i = lam[:, :hb], lam[:, hb:]
            sr_, si_ = s_sh[:, :hb], s_sh[:, hb:]
            dar = jnp.sum(lam_r * sr_ + lam_i * si_, axis=0, keepdims=True)
            dai = jnp.sum(lam_i * sr_ - lam_r * si_, axis=0, keepdims=True)
            sums.append((wide, jnp.concatenate([dar, dai], axis=1), dwb, dwc))

        @pl.when(ib == 0)
        def _():
            for b, (wide, contrib, dwb, dwc) in enumerate(sums):
                da_ref[:, wide] = contrib
                dwb_ref[b] = dwb
                dwc_ref[b] = dwc

        @pl.when(ib != 0)
        def _():
            for b, (wide, contrib, dwb, dwc) in enumerate(sums):
                da_ref[:, wide] += contrib
                dwb_ref[b] += dwb
                dwc_ref[b] += dwc

    blk = lambda j, i: (nb - 1 - i, j)
    prev_blk = lambda j, i: (jnp.maximum((nb - 1 - i) * sub - 1, 0), j)
    sd = jax.ShapeDtypeStruct
    return pl.pallas_call(
        body, name="ssm_bwd", grid=(nj // npair, nb),
        in_specs=[pl.BlockSpec((t_blk, npair * LANES), blk), pl.BlockSpec((t_blk, npair * w2), blk),
                  pl.BlockSpec((SUBLANES, npair * w2), prev_blk), pl.BlockSpec((t_blk, npair * LANES), blk),
                  pl.BlockSpec((t_blk, npair * LANES), blk),
                  pl.BlockSpec((npair, LANES, w2), lambda j, i: (j, 0, 0)),
                  pl.BlockSpec((npair, w2, LANES), lambda j, i: (j, 0, 0)),
                  pl.BlockSpec((1, npair * w2), lambda j, i: (0, j)),
                  _full_spec((t_blk, t_blk)), _full_spec((t_blk, t_blk))],
        out_specs=[pl.BlockSpec((t_blk, npair * LANES), blk),
                   pl.BlockSpec((npair, LANES, w2), lambda j, i: (j, 0, 0)),
                   pl.BlockSpec((npair, w2, LANES), lambda j, i: (j, 0, 0)),
                   pl.BlockSpec((1, npair * w2), lambda j, i: (0, j))],
        out_shape=[sd((seq, nj * LANES), BF16), sd((nj, LANES, w2), F32), sd((nj, w2, LANES), F32),
                   sd((1, nj * w2), F32)],
        scratch_shapes=_scan_scratch(npair, t_blk, sub, hb), compiler_params=_cparams(),
    )(dy, s, s, proj, du1, wb, wc, a, pm, pm.T)


def _rope128(x, cos, sa, sb):
    return x * cos + pltpu.roll(x, 96, 1) * sa + pltpu.roll(x, 32, 1) * sb


def _rope128_t(dy, cos, sa, sb):
    return dy * cos + pltpu.roll(dy * sa, 32, 1) + pltpu.roll(dy * sb, 96, 1)


ATT_BQ = 256


def _probs(qn, qp, kn, kp, r0, scale):
    s = lax.dot_general(qn, kn, (NT, ((), ())), preferred_element_type=F32)
    s = s + lax.dot_general(qp, kp, (NT, ((), ())), preferred_element_type=F32)
    s = s * scale
    diag = s[:, r0:]
    row = lax.broadcasted_iota(jnp.int32, diag.shape, 0)
    col = lax.broadcasted_iota(jnp.int32, diag.shape, 1)
    diag = jnp.where(col <= row, diag, jnp.finfo(F32).min)
    s = diag if r0 == 0 else jnp.concatenate([s[:, :r0], diag], axis=1)
    m = jnp.max(s, axis=-1, keepdims=True)
    e = jnp.exp(s - m)
    return e / jnp.sum(e, axis=-1, keepdims=True)


def _attn_specs(seq):
    tab = pl.BlockSpec((seq, LANES), lambda h: (0, 0))
    return [pl.BlockSpec((None, seq, 256), lambda h: (h, 0, 0)), pl.BlockSpec((None, seq, 128), lambda h: (h, 0, 0)),
            pl.BlockSpec((None, seq, 128), lambda h: (h, 0, 1)), tab, tab, tab, tab]


def _attn_fwd(q_raw, kv, kpe, cos, sa, sb):
    nh, seq, _ = q_raw.shape
    bq = min(ATT_BQ, seq)
    scale = (QK_NOPE + QK_ROPE) ** -0.5

    def body(q_ref, kn_ref, v_ref, kp_ref, cos_ref, sa_ref, sb_ref, o_ref):
        for r0 in range(0, seq, bq):
            rows, kend = pl.ds(r0, bq), r0 + bq
            qn = q_ref[rows, :QK_NOPE].astype(BF16)
            qp = _rope128(q_ref[rows, QK_NOPE:], cos_ref[rows, :], sa_ref[rows, :], sb_ref[rows, :]).astype(BF16)
            p = _probs(qn, qp, kn_ref[:kend, :], kp_ref[:kend, :], r0, scale)
            o_ref[rows, :] = jnp.dot(p.astype(BF16), v_ref[:kend, :], preferred_element_type=F32)

    return pl.pallas_call(
        body, name="attn_fwd", grid=(nh,), in_specs=_attn_specs(seq),
        out_specs=pl.BlockSpec((seq, V_DIM), lambda h: (0, h)),
        out_shape=jax.ShapeDtypeStruct((seq, nh * V_DIM), F32), compiler_params=_cparams(),
    )(q_raw, kv, kv, kpe, cos, sa, sb)


def _attn_bwd(q_raw, kv, kpe, cos, sa, sb, do):
    nh, seq, _ = q_raw.shape
    bq = min(ATT_BQ, seq)
    scale = (QK_NOPE + QK_ROPE) ** -0.5

    def body(q_ref, kn_ref, v_ref, kp_ref, cos_ref, sa_ref, sb_ref, do_ref, dq_ref, dkv_ref, dkp_ref):
        dkv_ref[...] = jnp.zeros_like(dkv_ref)
        dkp_ref[...] = jnp.zeros_like(dkp_ref)
        for r0 in range(0, seq, bq):
            rows, kend = pl.ds(r0, bq), r0 + bq
            cos_b, sa_b, sb_b = cos_ref[rows, :], sa_ref[rows, :], sb_ref[rows, :]
            qn = q_ref[rows, :QK_NOPE].astype(BF16)
            qp = _rope128(q_ref[rows, QK_NOPE:], cos_b, sa_b, sb_b).astype(BF16)
            kn, v, kp = kn_ref[:kend, :], v_ref[:kend, :], kp_ref[:kend, :]
            p = _probs(qn, qp, kn, kp, r0, scale)
            dob = do_ref[rows, :].astype(BF16)
            dp = lax.dot_general(dob, v, (NT, ((), ())), preferred_element_type=F32)
            ds = p * (dp - jnp.sum(p * dp, axis=-1, keepdims=True)) * scale
            dsb = ds.astype(BF16)
            pb = p.astype(BF16)
            dq_ref[rows, :QK_NOPE] = jnp.dot(dsb, kn, preferred_element_type=F32).astype(dq_ref.dtype)
            dqp = jnp.dot(dsb, kp, preferred_element_type=F32)
            dq_ref[rows, QK_NOPE:] = _rope128_t(dqp, cos_b, sa_b, sb_b).astype(dq_ref.dtype)
            dkv_ref[:kend, :QK_NOPE] += lax.dot_general(dsb, qn, (TN, ((), ())), preferred_element_type=F32)
            dkv_ref[:kend, QK_NOPE:] += lax.dot_general(pb, dob, (TN, ((), ())), preferred_element_type=F32)
            dkp_ref[:kend, :] += lax.dot_general(dsb, qp, (TN, ((), ())), preferred_element_type=F32)

    sd = jax.ShapeDtypeStruct
    return pl.pallas_call(
        body, name="attn_bwd", grid=(nh,),
        in_specs=_attn_specs(seq) + [pl.BlockSpec((seq, V_DIM), lambda h: (0, h))],
        out_specs=[pl.BlockSpec((None, seq, 256), lambda h: (h, 0, 0)),
                   pl.BlockSpec((None, seq, 256), lambda h: (h, 0, 0)),
                   pl.BlockSpec((None, seq, 128), lambda h: (h, 0, 0))],
        out_shape=[sd((nh, seq, 256), BF16), sd((nh, seq, 256), F32), sd((nh, seq, 128), F32)],
        compiler_params=_cparams(),
    )(q_raw, kv, kv, kpe, cos, sa, sb, do)


def _conv3(a, w, b):
    rows = lax.broadcasted_iota(jnp.int32, a.shape, 0)
    a1 = jnp.where(rows >= 1, pltpu.roll(a, 1, 0), 0.0)
    a2 = jnp.where(rows >= 2, pltpu.roll(a, 2, 0), 0.0)
    return w[2:3] * a + w[1:2] * a1 + w[0:1] * a2 + b, a1, a2


def _conv_gate_fwd(a, cw, cb):
    half, _, seq, c = a.shape
    nc = c // LANES

    def fn(pair, wg, wv, bg, bv):
        gc, _, _ = _conv3(pair[0], wg, bg)
        vc, _, _ = _conv3(pair[1], wv, bv)
        return gc * jax.nn.sigmoid(gc) * vc

    def w_spec(off, r):
        return pl.BlockSpec((None, r, LANES), lambda k, j: (k + off, 0, j))

    return _blockwise(
        "conv_gate_fwd", fn, [a, cw, cw, cb, cb],
        [pl.BlockSpec((None, 2, seq, LANES), lambda k, j: (k, 0, 0, j)),
         w_spec(0, 3), w_spec(half, 3), w_spec(0, 1), w_spec(half, 1)],
        [((seq, half * c), BF16)], [pl.BlockSpec((seq, LANES), lambda k, j: (0, k * nc + j))],
        grid=(half, nc))[0]


def _conv_gate_bwd(a, cw, cb, dm):
    half, _, seq, c = a.shape
    nc = c // LANES

    def body(a_ref, wg_ref, wv_ref, bg_ref, bv_ref, dm_ref, da_ref, dw_ref, db_ref):
        dmv = dm_ref[...]
        rows = lax.broadcasted_iota(jnp.int32, dmv.shape, 0)
        ga, wg = a_ref[0], wg_ref[...]
        va, wv = a_ref[1], wv_ref[...]
        gc, g1, g2 = _conv3(ga, wg, bg_ref[...])
        vc, v1, v2 = _conv3(va, wv, bv_ref[...])
        sg = jax.nn.sigmoid(gc)
        dms = dmv * sg
        d_val = dms * gc
        d_gate = dms * vc * (1.0 + gc * (1.0 - sg))

        def back(r, dc, own, a1, a2, w):
            up1 = jnp.where(rows < seq - 1, pltpu.roll(dc, seq - 1, 0), 0.0)
            up2 = jnp.where(rows < seq - 2, pltpu.roll(dc, seq - 2, 0), 0.0)
            da_ref[r] = (w[2:3] * dc + w[1:2] * up1 + w[0:1] * up2).astype(da_ref.dtype)
            dw_ref[r, 0:1, :] = jnp.sum(dc * a2, axis=0, keepdims=True)
            dw_ref[r, 1:2, :] = jnp.sum(dc * a1, axis=0, keepdims=True)
            dw_ref[r, 2:3, :] = jnp.sum(dc * own, axis=0, keepdims=True)
            db_ref[r] = jnp.sum(dc, axis=0, keepdims=True)

        back(0, d_gate, ga, g1, g2, wg)
        back(1, d_val, va, v1, v2, wv)

    def w_spec(off, r):
        return pl.BlockSpec((None, r, LANES), lambda k, j: (k + off, 0, j))

    def pair_spec(r):
        return pl.BlockSpec((None, 2, r, LANES), lambda k, j: (k, 0, 0, j))

    sd = jax.ShapeDtypeStruct
    return pl.pallas_call(
        body, name="conv_gate_bwd", grid=(half, nc),
        in_specs=[pair_spec(seq), w_spec(0, 3), w_spec(half, 3), w_spec(0, 1), w_spec(half, 1),
                  pl.BlockSpec((seq, LANES), lambda k, j: (0, k * nc + j))],
        out_specs=[pair_spec(seq), pair_spec(3), pair_spec(1)],
        out_shape=[sd((half, 2, seq, c), BF16), sd((half, 2, 3, c), F32), sd((half, 2, 1, c), F32)],
        compiler_params=_cparams(),
    )(a, cw, cw, cb, cb, dm)


ROW_T = 256


def _local_step(x, positions, target, w, emit=lambda **grads: None):
    seq, d = x.shape
    t_row = min(ROW_T, seq)
    nrow = seq // t_row
    ssm_w = d // 2
    nj = ssm_w // LANES
    n_groups = ssm_w // SSM_GROUP
    nh = w["wuq"].shape[0]
    q_rank = w["wuq"].shape[1]
    kv_rank = w["wukv"].shape[1]
    ns = w["wup"].shape[0]
    c_ff = w["wup"].shape[2]
    in_pad = w["win"].shape[1]
    tm = min(1024, seq)
    nm = seq // tm
    sw = 2 * STATE_BLOCK
    g1 = (nrow,)

    lr3 = w["lam_re"].reshape(n_groups, 1, SSM_STATE)
    li3 = w["lam_im"].reshape(n_groups, 1, SSM_STATE)
    ldt3 = w["log_dt"].reshape(n_groups, 1, 1)
    bt_re = jnp.swapaxes(w["b_re"].reshape(n_groups, SSM_STATE, SSM_GROUP), 1, 2)
    bt_im = jnp.swapaxes(w["b_im"].reshape(n_groups, SSM_STATE, SSM_GROUP), 1, 2)
    abar_re, abar_im, bbt_re, bbt_im = _s5_prep(lr3, li3, ldt3, bt_re, bt_im)
    eye = jnp.eye(GROUPS_PER_BLOCK, dtype=F32)

    def blockdiag_in(bb):
        t = bb.reshape(nj, GROUPS_PER_BLOCK, SSM_GROUP, SSM_STATE)
        return jnp.einsum("jghp,gk->jghkp", t, eye).reshape(nj, LANES, STATE_BLOCK)

    def blockdiag_in_t(dwb):
        t = dwb.reshape(nj, GROUPS_PER_BLOCK, SSM_GROUP, GROUPS_PER_BLOCK, SSM_STATE)
        return jnp.einsum("jghkp,gk->jghp", t, eye).reshape(n_groups, SSM_GROUP, SSM_STATE)

    def blockdiag_out(cc):
        t = cc.reshape(nj, GROUPS_PER_BLOCK, SSM_GROUP, SSM_STATE)
        return jnp.einsum("jghp,gk->jkpgh", t, eye).reshape(nj, STATE_BLOCK, LANES)

    def blockdiag_out_t(dwc):
        t = dwc.reshape(nj, GROUPS_PER_BLOCK, SSM_STATE, GROUPS_PER_BLOCK, SSM_GROUP)
        return jnp.einsum("jkpgh,gk->jghp", t, eye).reshape(n_groups, SSM_GROUP, SSM_STATE)

    c_re = w["c_re"].reshape(n_groups, SSM_GROUP, SSM_STATE)
    c_im = w["c_im"].reshape(n_groups, SSM_GROUP, SSM_STATE)
    wb = jnp.concatenate([blockdiag_in(bbt_re), blockdiag_in(bbt_im)], axis=2).astype(BF16)
    wc = jnp.concatenate([blockdiag_out(c_re), -blockdiag_out(c_im)], axis=1).astype(BF16)
    a_lay = jnp.concatenate([abar_re.reshape(nj, 1, STATE_BLOCK), abar_im.reshape(nj, 1, STATE_BLOCK)],
                            axis=1).reshape(1, nj * sw)

    attn_w = w["attn_norm"]
    hn = _blockwise("norm1", lambda xb, wv: _rms(xb, wv), [x, attn_w], [_row_spec(t_row, d), _full_spec((1, d))],
                    [((seq, d), BF16)], [_row_spec(t_row, d)], g1)[0]
    proj = _mm2d("proj", hn, w["win"], NN, F32, tn=640)

    s_all, ylin = _ssm_fwd(proj, wb, wc, a_lay)
    u_spec = pl.BlockSpec((t_row, ssm_w), lambda i: (i, 0))

    def ypre_fn(yl, ub, dsk):
        yp = yl + dsk * ub
        return yp, jax.nn.gelu(yp)

    y_pre, yg = _blockwise("ssm_gelu", ypre_fn, [ylin, proj, w["ssm_d"]],
                           [_row_spec(t_row, ssm_w), u_spec, _full_spec((1, ssm_w))],
                           [((seq, ssm_w), F32), ((seq, ssm_w), BF16)],
                           [_row_spec(t_row, ssm_w)] * 2, g1)
    z = _mm2d("ssm_glu", yg, w["wglu"], NN, F32, res=w["b_glu"])
    y_ssm = _blockwise("ssm_gate", lambda yp, zb: jax.nn.gelu(yp) * jax.nn.sigmoid(zb), [y_pre, z],
                       [_row_spec(t_row, ssm_w)] * 2, [((seq, ssm_w), F32)], [_row_spec(t_row, ssm_w)], g1)[0]

    cq_off, ckv_off, kpe_off = ssm_w, ssm_w + q_rank, ssm_w + q_rank + kv_rank
    c_q = proj[:, cq_off:ckv_off]
    c_kv = proj[:, ckv_off:kpe_off]
    kpe_raw = proj[:, kpe_off:kpe_off + LANES]
    pos_b = jnp.broadcast_to(positions.astype(F32)[:, None], (seq, LANES))
    inv_freq = ROPE_THETA ** (-jnp.arange(0, QK_ROPE, 2, dtype=F32) / QK_ROPE)
    inv128 = jnp.tile(inv_freq, 4).reshape(1, LANES)

    def mla_prep_fn(cq, ckv, kp, pb, inv, wq, wkv):
        ang = pb * inv
        lane = lax.broadcasted_iota(jnp.int32, ang.shape, 1)
        cs, sn = jnp.cos(ang), jnp.sin(ang)
        cos = jnp.where(lane < QK_ROPE, cs, 0.0)
        sa = jnp.where(lane < QK_ROPE // 2, -sn, 0.0)
        sb = jnp.where(jnp.logical_and(lane >= QK_ROPE // 2, lane < QK_ROPE), sn, 0.0)
        return _rms(cq, wq), _rms(ckv, wkv), _rope128(kp, cos, sa, sb), cos, sa, sb

    qn, kvn, kpe, cos_t, sa_t, sb_t = _blockwise(
        "mla_prep", mla_prep_fn, [c_q, c_kv, kpe_raw, pos_b, inv128, w["q_norm"], w["kv_norm"]],
        [_row_spec(t_row, q_rank), _row_spec(t_row, kv_rank), _row_spec(t_row, LANES), _row_spec(t_row, LANES),
         _full_spec((1, LANES)), _full_spec((1, q_rank)), _full_spec((1, kv_rank))],
        [((seq, q_rank), BF16), ((seq, kv_rank), BF16), ((seq, LANES), BF16)] + [((seq, LANES), F32)] * 3,
        [_row_spec(t_row, q_rank), _row_spec(t_row, kv_rank)] + [_row_spec(t_row, LANES)] * 4, g1)

    def head_mm(name, act, wh, out_dtype):
        kdim, ndim = wh.shape[1], wh.shape[2]
        return _mm(name, act, wh, grid=(nh, 1, 1),
                   a_spec=pl.BlockSpec((seq, kdim), lambda h, i, k: (i, 0)),
                   b_spec=pl.BlockSpec((None, kdim, ndim), lambda h, i, k: (h, 0, 0)),
                   o_spec=pl.BlockSpec((None, seq, ndim), lambda h, i, k: (h, i, 0)),
                   out_shape=(nh, seq, ndim), out_dtype=out_dtype)

    q_raw = head_mm("mla_q", qn, w["wuq"], F32)
    kv = head_mm("mla_kv", kvn, w["wukv"], BF16)
    y_mla = _attn_fwd(q_raw, kv, kpe, cos_t, sa_t, sb_t)
    mla_w = nh * V_DIM

    def outnorm_fn(ys, ym, ws, wm):
        return jnp.concatenate([_rms(ys, ws), _rms(ym, wm)], axis=1)

    ycat = _blockwise("out_norm", outnorm_fn, [y_ssm, y_mla, w["son"], w["mon"]],
                      [_row_spec(t_row, ssm_w), _row_spec(t_row, mla_w), _full_spec((1, ssm_w)), _full_spec((1, mla_w))],
                      [((seq, d), BF16)], [_row_spec(t_row, d)], g1)[0]
    h1 = _mm2d("out_proj", ycat, w["wout"], NN, F32, res=x)

    hn2 = _blockwise("norm2", lambda hb, wv: _rms(hb, wv), [h1, w["ffn_norm"]],
                     [_row_spec(t_row, d), _full_spec((1, d))], [((seq, d), BF16)], [_row_spec(t_row, d)], g1)[0]
    tku = d
    half = ns // 2
    a_ff = _mm("ffn_up", hn2, w["wup"], grid=(ns, nm, d // tku),
               a_spec=pl.BlockSpec((tm, tku), lambda s, i, k: (i, k)),
               b_spec=pl.BlockSpec((None, tku, c_ff), lambda s, i, k: (s, k, 0)),
               o_spec=pl.BlockSpec((None, None, tm, c_ff), lambda s, i, k: (s % half, s // half, i, 0)),
               out_shape=(half, 2, seq, c_ff), out_dtype=F32)
    cb3 = w["conv_b"].reshape(ns, 1, c_ff)
    m_ff = _conv_gate_fwd(a_ff, w["conv_w"], cb3)
    d_ff = half * c_ff
    wdn = w["wdown"]
    tnd = _tile(d, 1024)
    tmx, tnx = min(1024, seq), _tile(d, 1024)
    h2 = _mm2d("ffn_down", m_ff, wdn, NN, F32, tm=512, tn=512, tk=d_ff, res=h1)

    def loss_fn(hb, tb, wv):
        def f(hh, ww):
            err = _rms(hh, ww) - tb
            return 0.5 * jnp.sum(jnp.mean(err * err, axis=-1))

        lossv, (dh, dw) = jax.value_and_grad(f, argnums=(0, 1))(hb, wv)
        return dh, dh, jnp.full((1, LANES), lossv, F32), dw

    fin_w = w["final_norm"].reshape(1, d)
    dh2, dh2b, loss_acc, g_final = _blockwise(
        "loss_head", loss_fn, [h2, target, fin_w], [_row_spec(t_row, d), _row_spec(t_row, d), _full_spec((1, d))],
        [((seq, d), F32), ((seq, d), BF16), ((1, LANES), F32), ((1, d), F32)],
        [_row_spec(t_row, d), _row_spec(t_row, d), _full_spec((1, LANES)), _full_spec((1, d))], g1, n_acc=2)
    loss = loss_acc[0, 0]

    dm = _mm2d("ffn_down_dx", dh2b, wdn, NT, F32, tn=c_ff)
    tks = seq
    g_wdown = _mm2d("ffn_down_dw", m_ff, dh2b, TN, BF16, tm=c_ff)
    emit(wdown=g_wdown)
    da_ff, g_convw2, g_convb2 = _conv_gate_bwd(a_ff, w["conv_w"], cb3, dm)
    g_convw = jnp.swapaxes(g_convw2, 0, 1).reshape(ns, 3, c_ff)
    g_convb = jnp.swapaxes(g_convb2, 0, 1).reshape(ns, 1, c_ff)
    g_wup = _mm("ffn_up_dw", hn2, da_ff, grid=(ns, d // tnd, seq // tks), contract=TN,
                a_spec=pl.BlockSpec((tks, tnd), lambda s, j, k: (k, j)),
                b_spec=pl.BlockSpec((None, None, tks, c_ff), lambda s, j, k: (s % half, s // half, k, 0)),
                o_spec=pl.BlockSpec((None, tnd, c_ff), lambda s, j, k: (s, j, 0)),
                out_shape=(ns, d, c_ff), out_dtype=BF16)
    emit(wup=g_wup)
    dhn2 = _mm("ffn_up_dx", da_ff, w["wup"], grid=(seq // tmx, d // tnx, ns), contract=NT,
               a_spec=pl.BlockSpec((None, None, tmx, c_ff), lambda i, j, s: (s % half, s // half, i, 0)),
               b_spec=pl.BlockSpec((None, tnx, c_ff), lambda i, j, s: (s, j, 0)),
               o_spec=pl.BlockSpec((tmx, tnx), lambda i, j, s: (i, j)),
               out_shape=(seq, d), out_dtype=F32)
    emit(wup_pair_sums_after=dhn2)

    def norm_bwd_fn(hb, dres, dn, wv):
        dx_, dw_ = _rms_bwd(hb, wv, dn)
        dtot = dres + dx_
        return dtot, dtot, dw_

    dh1, dh1b, g_ffn_norm = _blockwise(
        "norm2_bwd", norm_bwd_fn, [h1, dh2, dhn2, w["ffn_norm"]],
        [_row_spec(t_row, d)] * 3 + [_full_spec((1, d))],
        [((seq, d), F32), ((seq, d), BF16), ((1, d), F32)],
        [_row_spec(t_row, d), _row_spec(t_row, d), _full_spec((1, d))], g1, n_acc=1)

    g_wout = _mm2d("out_proj_dw", ycat, dh1b, TN, BF16)

    def outnorm_bwd_fn(dhb, wo, ys, ym, ws, wm):
        dyc = lax.dot_general(dhb, wo, (NT, ((), ())), preferred_element_type=F32)
        dys, dws = _rms_bwd(ys, ws, dyc[:, :ssm_w])
        dym, dwm = _rms_bwd(ym, wm, dyc[:, ssm_w:])
        return dys, dym, dws, dwm

    dy_ssm, dy_mla, g_son, g_mon = _blockwise(
        "out_proj_dx_norm_bwd", outnorm_bwd_fn, [dh1b, w["wout"], y_ssm, y_mla, w["son"], w["mon"]],
        [_row_spec(t_row, d), _full_spec((d, d)), _row_spec(t_row, ssm_w), _row_spec(t_row, mla_w),
         _full_spec((1, ssm_w)), _full_spec((1, mla_w))],
        [((seq, ssm_w), F32), ((seq, mla_w), F32), ((1, ssm_w), F32), ((1, mla_w), F32)],
        [_row_spec(t_row, ssm_w), _row_spec(t_row, mla_w), _full_spec((1, ssm_w)), _full_spec((1, mla_w))],
        g1, n_acc=2)

    def glu_bwd_fn(dy, yp, zb, ub, dsk, wg):
        ygv = jax.nn.gelu(yp)
        sg = jax.nn.sigmoid(zb)
        dz = dy * ygv * sg * (1.0 - sg)
        dzb = dz.astype(BF16)
        dyg = dy * sg + lax.dot_general(dzb, wg, (NT, ((), ())), preferred_element_type=F32)
        _, vjp = jax.vjp(jax.nn.gelu, yp)
        dyp = vjp(dyg)[0]
        return (dzb, dyp, dyp * dsk, jnp.sum(dz, axis=0, keepdims=True), jnp.sum(dyp * ub, axis=0, keepdims=True))

    dz, dy_pre, du1, g_bglu, g_ssmd = _blockwise(
        "ssm_glu_bwd", glu_bwd_fn, [dy_ssm, y_pre, z, proj, w["ssm_d"], w["wglu"]],
        [_row_spec(t_row, ssm_w)] * 3 + [u_spec, _full_spec((1, ssm_w)), _full_spec((ssm_w, ssm_w))],
        [((seq, ssm_w), BF16), ((seq, ssm_w), BF16), ((seq, ssm_w), F32), ((1, ssm_w), F32), ((1, ssm_w), F32)],
        [_row_spec(t_row, ssm_w)] * 3 + [_full_spec((1, ssm_w))] * 2, g1, n_acc=2)
    g_wglu = _mm2d("ssm_glu_dw", yg, dz, TN, BF16)
    dq_raw, dkv, dkp_h = _attn_bwd(q_raw, kv, kpe, cos_t, sa_t, sb_t, dy_mla)

    def head_mm_dx(name, dact, wh):
        kdim, ndim = wh.shape[1], wh.shape[2]
        return _mm(name, dact, wh, grid=(1, 1, nh), contract=NT,
                   a_spec=pl.BlockSpec((None, seq, ndim), lambda i, j, h: (h, i, 0)),
                   b_spec=pl.BlockSpec((None, kdim, ndim), lambda i, j, h: (h, 0, 0)),
                   o_spec=pl.BlockSpec((seq, kdim), lambda i, j, h: (i, 0)),
                   out_shape=(seq, kdim), out_dtype=F32)

    def head_mm_dw(name, act, dact):
        kdim, ndim = act.shape[1], dact.shape[2]
        return _mm(name, act, dact, grid=(nh, 1, seq // tks), contract=TN,
                   a_spec=pl.BlockSpec((tks, kdim), lambda h, j, k: (k, 0)),
                   b_spec=pl.BlockSpec((None, tks, ndim), lambda h, j, k: (h, k, 0)),
                   o_spec=pl.BlockSpec((None, kdim, ndim), lambda h, j, k: (h, 0, 0)),
                   out_shape=(nh, kdim, ndim), out_dtype=BF16)

    g_wuq = head_mm_dw("mla_q_dw", qn, dq_raw)
    g_wukv = head_mm_dw("mla_kv_dw", kvn, dkv)
    dqn = head_mm_dx("mla_q_dx", dq_raw, w["wuq"])
    dkvn = head_mm_dx("mla_kv_dx", dkv, w["wukv"])
    emit(not_before=(dqn, dkvn, dy_pre), wout=g_wout, wuq=g_wuq, wukv=g_wukv, wglu=g_wglu, conv_w=g_convw)

    du, dwb, dwc, da_lay = _ssm_bwd(dy_pre, s_all, proj, du1, wb, wc, a_lay)
    g_c_re = blockdiag_out_t(dwc[:, :STATE_BLOCK, :])
    g_c_im = -blockdiag_out_t(dwc[:, STATE_BLOCK:, :])
    dbbt_re = blockdiag_in_t(dwb[:, :, :STATE_BLOCK])
    dbbt_im = blockdiag_in_t(dwb[:, :, STATE_BLOCK:])
    da3 = da_lay.reshape(nj, 2, STATE_BLOCK)
    dabar_re = da3[:, 0, :].reshape(n_groups, 1, SSM_STATE)
    dabar_im = da3[:, 1, :].reshape(n_groups, 1, SSM_STATE)
    g_lr3, g_li3, g_ldt3, g_bt_re, g_bt_im = _s5_prep_bwd(lr3, li3, ldt3, bt_re, bt_im,
                                                           dabar_re, dabar_im, dbbt_re, dbbt_im)

    def mla_prep_bwd_fn(cq, ckv, dqn_b, dkvn_b, dkp_b, cos, sa, sb, wq, wkv):
        dcq, dwq = _rms_bwd(cq, wq, dqn_b)
        dckv, dwkv = _rms_bwd(ckv, wkv, dkvn_b)
        dkp_sum = dkp_b[0]
        for h in range(1, nh):
            dkp_sum = dkp_sum + dkp_b[h]
        return dcq, dckv, _rope128_t(dkp_sum, cos, sa, sb), dwq, dwkv

    dc_q, dc_kv, dkpe_raw, g_qnorm, g_kvnorm = _blockwise(
        "mla_prep_bwd", mla_prep_bwd_fn, [c_q, c_kv, dqn, dkvn, dkp_h, cos_t, sa_t, sb_t, w["q_norm"], w["kv_norm"]],
        [_row_spec(t_row, q_rank), _row_spec(t_row, kv_rank), _row_spec(t_row, q_rank), _row_spec(t_row, kv_rank),
         pl.BlockSpec((nh, t_row, LANES), lambda i: (0, i, 0))] + [_row_spec(t_row, LANES)] * 3
        + [_full_spec((1, q_rank)), _full_spec((1, kv_rank))],
        [((seq, q_rank), BF16), ((seq, kv_rank), BF16), ((seq, LANES), BF16), ((1, q_rank), F32), ((1, kv_rank), F32)],
        [_row_spec(t_row, q_rank), _row_spec(t_row, kv_rank), _row_spec(t_row, LANES), _full_spec((1, q_rank)),
         _full_spec((1, kv_rank))], g1, n_acc=2)

    dproj = jnp.concatenate([du, dc_q, dc_kv, dkpe_raw], axis=1)
    g_win = _mm2d("proj_dw", hn, dproj, TN, BF16, tn=640)
    emit(win=g_win)
    def norm1_bwd_fn(dpb, wi, xb, dres, wv):
        dn = lax.dot_general(dpb, wi, (NT, ((), ())), preferred_element_type=F32)
        dx_, dw_ = _rms_bwd(xb, wv, dn)
        return dres + dx_, dw_

    grad_x, g_attn_norm = _blockwise(
        "proj_dx_norm1_bwd", norm1_bwd_fn, [dproj, w["win"], x, dh1, attn_w],
        [_row_spec(t_row, in_pad), _full_spec((d, in_pad)), _row_spec(t_row, d), _row_spec(t_row, d), _full_spec((1, d))],
        [((seq, d), F32), ((1, d), F32)], [_row_spec(t_row, d), _full_spec((1, d))], g1, n_acc=1)
    emit(win_pair_sums_after=grad_x)

    grads = dict(
        attn_norm=g_attn_norm, win=g_win, lam_re=g_lr3, lam_im=g_li3, log_dt=g_ldt3,
        bt_re=g_bt_re, bt_im=g_bt_im, c_re=g_c_re, c_im=g_c_im,
        ssm_d=g_ssmd, wglu=g_wglu, b_glu=g_bglu, q_norm=g_qnorm, wuq=g_wuq, kv_norm=g_kvnorm, wukv=g_wukv,
        son=g_son, mon=g_mon, wout=g_wout, ffn_norm=g_ffn_norm, wup=g_wup, conv_w=g_convw, conv_b=g_convb,
        wdown=g_wdown, final_norm=g_final)
    return loss, grad_x, grads


def _mesh_pos():
    return lax.axis_index("x"), lax.axis_index("y"), lax.axis_index("c")


def _handshake_all():
    x, y, c = _mesh_pos()
    barrier = pltpu.get_barrier_semaphore()
    for k in range(1, N_DEV):
        peer = (1 - x if k & 4 else x, 1 - y if k & 2 else y, 1 - c if k & 1 else c)
        pl.semaphore_signal(barrier, inc=1, device_id=peer, device_id_type=MESH)
    pl.semaphore_wait(barrier, N_DEV - 1)


def _handshake(peers):
    barrier = pltpu.get_barrier_semaphore()
    for peer in peers:
        pl.semaphore_signal(barrier, inc=1, device_id=peer, device_id_type=MESH)
    pl.semaphore_wait(barrier, len(peers))


def _comm_call(name, body, n, out_shape, ins, collective_id, after=None, copies=7, n_remote=None, n_local=None):
    n_remote = copies * n if n_remote is None else n_remote
    sems = [pltpu.SemaphoreType.DMA((n_remote,)), pltpu.SemaphoreType.DMA((n_remote,)),
            pltpu.SemaphoreType.DMA((n if n_local is None else n_local,))]
    if collective_id is None:
        any_spec = pl.BlockSpec(memory_space=pl.ANY)
        return pl.pallas_call(body, name=name, out_shape=out_shape, in_specs=[any_spec] * n,
                              out_specs=[any_spec] * n, scratch_shapes=sems)(*ins)
    seq_body = body
    if after:
        n_after = len(after)
        ins = list(ins) + list(after)

        def seq_body(*refs):
            body(*refs[:n], *refs[n + n_after:])

    return pl.kernel(seq_body, name=name, out_type=out_shape,
                     mesh=plsc.ScalarSubcoreMesh(axis_name="seq", num_cores=1), scratch_types=sems,
                     compiler_params=pltpu.CompilerParams(collective_id=collective_id))(*ins)


def _all_gather(name, xs, collective_id=None, after=None, pair_sums=()):
    n = len(xs)
    nh = len(pair_sums)
    m = n + nh

    def body(*refs):
        x_refs, h_refs, o_refs, e_refs = refs[:n], refs[n:m], refs[m:m + n], refs[m + n:2 * m]
        send_sems, recv_sems, local_sems = refs[2 * m:]
        if collective_id is not None:
            _handshake_all()
        finish_pairs = _chip_copies(h_refs, e_refs, send_sems, recv_sems, local_sems, 7 * n, n) if nh else None
        x, y, c = _mesh_pos()
        me, sibling = (x, y, c), (x, y, 1 - c)
        chips = [(1 - x, y), (x, 1 - y), (1 - x, 1 - y)]

        def slot(o_ref, px, py, pc):
            return o_ref.at[4 * px + 2 * py + pc]

        def copy(t, k, block, to, src=None):
            dst = slot(o_refs[t], *block)
            return pltpu.make_async_remote_copy(
                src_ref=dst if src is None else src, dst_ref=dst,
                send_sem=send_sems.at[7 * t + k], recv_sem=recv_sems.at[7 * t + k],
                device_id=to, device_id_type=MESH)

        started = []
        for t in range(n):
            mine = pltpu.make_async_copy(x_refs[t], slot(o_refs[t], *me), local_sems.at[t])
            mine.start()
            started.append(mine)
        first = []
        for t in range(n):
            first.append(copy(t, 0, me, sibling, src=x_refs[t]))
            first += [copy(t, 1 + j, me, (*chip, c), src=x_refs[t]) for j, chip in enumerate(chips)]
        for cp in first:
            cp.start()
        passed = []
        for j, chip in enumerate(chips):
            for t in range(n):
                copy(t, 1 + j, (*chip, c), me).wait_recv()
                fwd = copy(t, 4 + j, (*chip, c), sibling)
                fwd.start()
                passed.append(fwd)
        for t in range(n):
            copy(t, 0, sibling, me).wait_recv()
            for j, chip in enumerate(chips):
                copy(t, 4 + j, (*chip, 1 - c), me).wait_recv()
        for cp in first + passed:
            cp.wait_send()
        for mine in started:
            mine.wait()
        if nh:
            finish_pairs()

    out_shape = ([jax.ShapeDtypeStruct((N_DEV,) + v.shape, v.dtype) for v in xs]
                 + [jax.ShapeDtypeStruct(v.shape, v.dtype) for v in pair_sums])
    return _comm_call(name, body, m, out_shape, list(xs) + list(pair_sums), collective_id, after,
                      n_remote=7 * n + (N_CHIP - 1) * nh, n_local=m)


def _exchange_partials(name, gs, collective_id=None, after=None):
    n = len(gs)

    def body(*refs):
        g_refs, o_refs = refs[:n], refs[n:2 * n]
        send_sems, recv_sems, local_sems = refs[2 * n:]
        if collective_id is not None:
            _handshake_all()
        x, y, c = _mesh_pos()
        me_idx = 4 * x + 2 * y + c
        copies = []
        for t in range(n):
            mine = pltpu.make_async_copy(g_refs[t].at[me_idx], o_refs[t].at[me_idx], local_sems.at[t])
            mine.start()
            copies.append(mine)
        remote = []
        for k in range(1, N_DEV):
            px = 1 - x if k & 4 else x
            py = 1 - y if k & 2 else y
            pc = 1 - c if k & 1 else c
            p_idx = 4 * px + 2 * py + pc
            for t in range(n):
                cp = pltpu.make_async_remote_copy(
                    src_ref=g_refs[t].at[p_idx], dst_ref=o_refs[t].at[me_idx],
                    send_sem=send_sems.at[7 * t + k - 1], recv_sem=recv_sems.at[7 * t + k - 1],
                    device_id=(px, py, pc), device_id_type=MESH)
                cp.start()
                landing = pltpu.make_async_remote_copy(
                    src_ref=g_refs[t].at[p_idx], dst_ref=o_refs[t].at[p_idx],
                    send_sem=send_sems.at[7 * t + k - 1], recv_sem=recv_sems.at[7 * t + k - 1],
                    device_id=(px, py, pc), device_id_type=MESH)
                remote.append((cp, landing))
        for cp, landing in remote:
            landing.wait_recv()
        for cp, landing in remote:
            cp.wait_send()
        for mine in copies:
            mine.wait()

    out_shape = [jax.ShapeDtypeStruct(v.shape, v.dtype) for v in gs]
    return _comm_call(name, body, n, out_shape, gs, collective_id, after)


N_CHIP = N_DEV // 2
PAIR_ADD_BLOCK_ELEMS = 1024 * 1024


def _pair_swap(name, gs, collective_id, after=None):
    n = len(gs)

    def body(*refs):
        g_refs, o_refs = refs[:n], refs[n:2 * n]
        send_sems, recv_sems, _ = refs[2 * n:]
        x, y, c = _mesh_pos()
        sibling = (x, y, 1 - c)
        _handshake([sibling])
        copies = []
        for t in range(n):
            for k in range(N_CHIP):
                copies.append(pltpu.make_async_remote_copy(
                    src_ref=g_refs[t].at[2 * k + 1 - c], dst_ref=o_refs[t].at[k],
                    send_sem=send_sems.at[N_CHIP * t + k], recv_sem=recv_sems.at[N_CHIP * t + k],
                    device_id=sibling, device_id_type=MESH))
        for cp in copies:
            cp.start()
        for cp in copies:
            cp.wait_recv()
        for cp in copies:
            cp.wait_send()

    out_shape = [jax.ShapeDtypeStruct((N_CHIP,) + v.shape[1:], v.dtype) for v in gs]
    return _comm_call(name, body, n, out_shape, gs, collective_id, after, copies=N_CHIP)


def _pair_add(name, g, got):
    _, r, c = g.shape
    tr = r
    if r * c > PAIR_ADD_BLOCK_ELEMS and r % SUBLANES == 0:
        tr = SUBLANES
        while r % (tr * 2) == 0 and tr * 2 * c <= PAIR_ADD_BLOCK_ELEMS:
            tr *= 2

    def body(core_ref, g_ref, got_ref, o_ref):
        o_ref[...] = (g_ref[...].astype(F32) + got_ref[...].astype(F32)).astype(o_ref.dtype)

    grid_spec = pltpu.PrefetchScalarGridSpec(
        num_scalar_prefetch=1, grid=(N_CHIP, r // tr),
        in_specs=[pl.BlockSpec((None, None, tr, c), lambda k, i, core: (k, core[0], i, 0)),
                  pl.BlockSpec((None, tr, c), lambda k, i, core: (k, i, 0))],
        out_specs=pl.BlockSpec((None, tr, c), lambda k, i, core: (k, i, 0)))
    core = lax.axis_index("c").astype(jnp.int32).reshape(1)
    return pl.pallas_call(body, name=name, grid_spec=grid_spec, out_shape=jax.ShapeDtypeStruct((N_CHIP, r, c), g.dtype),
                          compiler_params=_cparams())(core, g.reshape(N_CHIP, 2, r, c), got)


def _chip_copies(h_refs, o_refs, send_sems, recv_sems, local_sems, sem0, local0):
    n = len(h_refs)
    per = N_CHIP - 1
    x, y, c = _mesh_pos()
    others = [(1 - x if k & 2 else x, 1 - y if k & 1 else y) for k in range(1, N_CHIP)]
    my_chip = 2 * x + y
    local = []
    for t in range(n):
        mine = pltpu.make_async_copy(h_refs[t].at[my_chip], o_refs[t].at[my_chip], local_sems.at[local0 + t])
        mine.start()
        local.append(mine)
    remote = []
    for j, (px, py) in enumerate(others):
        chip = 2 * px + py
        for t in range(n):
            sems = dict(send_sem=send_sems.at[sem0 + per * t + j], recv_sem=recv_sems.at[sem0 + per * t + j],
                        device_id=(px, py, c), device_id_type=MESH)
            cp = pltpu.make_async_remote_copy(src_ref=h_refs[t].at[chip], dst_ref=o_refs[t].at[my_chip], **sems)
            cp.start()
            landing = pltpu.make_async_remote_copy(src_ref=h_refs[t].at[chip], dst_ref=o_refs[t].at[chip], **sems)
            remote.append((cp, landing))

    def finish():
        for cp, landing in remote:
            landing.wait_recv()
        for cp, landing in remote:
            cp.wait_send()
        for mine in local:
            mine.wait()

    return finish


def _chip_exchange(name, hs, collective_id, after=None):
    n = len(hs)
    per = N_CHIP - 1

    def body(*refs):
        h_refs, o_refs = refs[:n], refs[n:2 * n]
        send_sems, recv_sems, local_sems = refs[2 * n:]
        x, y, c = _mesh_pos()
        _handshake([(1 - x if k & 2 else x, 1 - y if k & 1 else y, c) for k in range(1, N_CHIP)])
        _chip_copies(h_refs, o_refs, send_sems, recv_sems, local_sems, 0, 0)()

    out_shape = [jax.ShapeDtypeStruct(v.shape, v.dtype) for v in hs]
    return _comm_call(name, body, n, out_shape, hs, collective_id, after, copies=per)


ADAM_BLOCK_ELEMS = 128 * 1024


def _sum_parts(pb):
    g = pb[0].astype(F32)
    for j in range(1, pb.shape[0]):
        g = g + pb[j].astype(F32)
    return g


def _adam_math(g, wb_, mb, vb):
    m_new = ADAM_B1 * mb + (1.0 - ADAM_B1) * g
    v_new = ADAM_B2 * vb + (1.0 - ADAM_B2) * (g * g)
    m_hat = m_new / (1.0 - ADAM_B1 ** ADAM_STEP)
    v_hat = v_new / (1.0 - ADAM_B2 ** ADAM_STEP)
    delta = -ADAM_LR * (m_hat / (jnp.sqrt(v_hat) + ADAM_EPS) + ADAM_WD * wb_)
    return g, delta, m_new, v_new


def _adamw_multi(name, items, nblk=1, packed=None):
    n = len(items)

    def spec(shape, lead):
        blk = list(shape)
        blk[lead + 1] = shape[lead + 1] // nblk
        if nblk == 1:
            return pl.BlockSpec(tuple(blk), lambda i, nd=len(shape): (0,) * nd)
        return pl.BlockSpec(tuple(blk), lambda i, nd=len(shape), ax=lead + 1: (0,) * ax + (i,) + (0,) * (nd - ax - 1))

    ins, in_specs, out_specs, out_shape, where = [], [], [], [], []
    if packed is not None:
        ins.append(packed)
        in_specs.append(spec(packed.shape, 1))
    for parts, wv, mv, vv in items:
        if isinstance(parts, int):
            where.append((0, parts, len(ins)))
        else:
            assert parts.shape[1:] == wv.shape, (name, parts.shape, wv.shape)
            where.append((len(ins), None, len(ins) + 1))
            ins.append(parts)
            in_specs.append(spec(parts.shape, 1))
        ins += [wv, mv, vv]
        in_specs += [spec(wv.shape, 0)] * 3
        out_specs += [spec(wv.shape, 0)] * 4
        out_shape += [jax.ShapeDtypeStruct(wv.shape, F32)] * 4
    n_in = len(ins)

    def body(*refs):
        for t, (ip, off, iw) in enumerate(where):
            wr, mr, vr = refs[iw:iw + 3]
            parts = refs[ip][...] if off is None else refs[ip][:, :, off:off + wr.shape[-1]]
            res = _adam_math(_sum_parts(parts), wr[...], mr[...], vr[...])
            for o, val in zip(refs[n_in + 4 * t:n_in + 4 * t + 4], res):
                o[...] = val

    res = pl.pallas_call(body, name=name, grid=(nblk,), in_specs=in_specs, out_specs=out_specs, out_shape=out_shape,
                         compiler_params=_cparams())(*ins)
    return [tuple(res[4 * t:4 * t + 4]) for t in range(n)]


def _sum_multi(name, parts_list):
    def body(*refs):
        for pr, o in zip(refs[:len(parts_list)], refs[len(parts_list):]):
            o[...] = _sum_parts(pr[...])

    return pl.pallas_call(body, name=name, out_shape=[jax.ShapeDtypeStruct(p.shape[1:], F32) for p in parts_list],
                          compiler_params=_cparams())(*parts_list)


def _adamw_sum(name, parts, wv, mv, vv):
    npart, r, c = parts.shape
    tr = r
    if r * c > ADAM_BLOCK_ELEMS and r % SUBLANES == 0:
        tr = SUBLANES
        while r % (tr * 2) == 0 and tr * 2 * c <= ADAM_BLOCK_ELEMS:
            tr *= 2

    def fn(pb, wb_, mb, vb):
        return _adam_math(_sum_parts(pb), wb_, mb, vb)

    row = pl.BlockSpec((tr, c), lambda i: (i, 0))
    return _blockwise(name, fn, [parts, wv, mv, vv],
                      [pl.BlockSpec((npart, tr, c), lambda i: (0, i, 0)), row, row, row],
                      [((r, c), F32)] * 4, [row] * 4, (r // tr,))


_VECTORS = ["attn_norm", "lam_re", "lam_im", "log_dt", "ssm_d", "b_glu", "q_norm", "kv_norm", "son", "mon",
            "ffn_norm", "conv_b", "final_norm"]
_GHP = ["c_re", "c_im", "bt_re", "bt_im"]
_PACKED = ["attn_norm", "ssm_d", "b_glu", "q_norm", "kv_norm", "son", "mon", "ffn_norm", "conv_b", "final_norm"]
_BIG = ["win", "wglu", "wuq", "wukv", "wout", "wup", "wdown", "conv_w"]
_TWO_LEVEL = ("wup", "win")
_AFTER = "_pair_sums_after"
_ORDER = ["attn_norm", "win", "lam_re", "lam_im", "log_dt", "b_re", "b_im", "c_re", "c_im", "ssm_d", "wglu",
          "b_glu", "q_norm", "wuq", "kv_norm", "wukv", "son", "mon", "wout", "ffn_norm", "wup", "conv_w",
          "conv_b", "wdown", "final_norm"]


def kernel(x, positions, attn_norm_w, w_in, ssm_lambda_re, ssm_lambda_im, ssm_log_dt, ssm_b_re, ssm_b_im, ssm_c_re, ssm_c_im, ssm_d, ssm_w_glu, ssm_b_glu, mla_q_norm_w, mla_w_uq, mla_kv_norm_w, mla_w_ukv, ssm_out_norm_w, mla_out_norm_w, w_out, ffn_norm_w, ffn_w_up, ffn_conv_w, ffn_conv_b, ffn_w_down, final_norm_w, loss_target, m_attn_norm_w, m_w_in, m_ssm_lambda_re, m_ssm_lambda_im, m_ssm_log_dt, m_ssm_b_re, m_ssm_b_im, m_ssm_c_re, m_ssm_c_im, m_ssm_d, m_ssm_w_glu, m_ssm_b_glu, m_mla_q_norm_w, m_mla_w_uq, m_mla_kv_norm_w, m_mla_w_ukv, m_ssm_out_norm_w, m_mla_out_norm_w, m_w_out, m_ffn_norm_w, m_ffn_w_up, m_ffn_conv_w, m_ffn_conv_b, m_ffn_w_down, m_final_norm_w, v_attn_norm_w, v_w_in, v_ssm_lambda_re, v_ssm_lambda_im, v_ssm_log_dt, v_ssm_b_re, v_ssm_b_im, v_ssm_c_re, v_ssm_c_im, v_ssm_d, v_ssm_w_glu, v_ssm_b_glu, v_mla_q_norm_w, v_mla_w_uq, v_mla_kv_norm_w, v_mla_w_ukv, v_ssm_out_norm_w, v_mla_out_norm_w, v_w_out, v_ffn_norm_w, v_ffn_w_up, v_ffn_conv_w, v_ffn_conv_b, v_ffn_w_down, v_final_norm_w):
    wts = dict(attn_norm=attn_norm_w, win=w_in, lam_re=ssm_lambda_re, lam_im=ssm_lambda_im, log_dt=ssm_log_dt,
               b_re=ssm_b_re, b_im=ssm_b_im, c_re=ssm_c_re, c_im=ssm_c_im, ssm_d=ssm_d, wglu=ssm_w_glu,
               b_glu=ssm_b_glu, q_norm=mla_q_norm_w, wuq=mla_w_uq, kv_norm=mla_kv_norm_w, wukv=mla_w_ukv,
               son=ssm_out_norm_w, mon=mla_out_norm_w, wout=w_out, ffn_norm=ffn_norm_w, wup=ffn_w_up,
               conv_w=ffn_conv_w, conv_b=ffn_conv_b, wdown=ffn_w_down, final_norm=final_norm_w)
    moms = dict(zip(_ORDER, [m_attn_norm_w, m_w_in, m_ssm_lambda_re, m_ssm_lambda_im, m_ssm_log_dt, m_ssm_b_re,
                             m_ssm_b_im, m_ssm_c_re, m_ssm_c_im, m_ssm_d, m_ssm_w_glu, m_ssm_b_glu, m_mla_q_norm_w,
                             m_mla_w_uq, m_mla_kv_norm_w, m_mla_w_ukv, m_ssm_out_norm_w, m_mla_out_norm_w, m_w_out,
                             m_ffn_norm_w, m_ffn_w_up, m_ffn_conv_w, m_ffn_conv_b, m_ffn_w_down, m_final_norm_w]))
    vels = dict(zip(_ORDER, [v_attn_norm_w, v_w_in, v_ssm_lambda_re, v_ssm_lambda_im, v_ssm_log_dt, v_ssm_b_re,
                             v_ssm_b_im, v_ssm_c_re, v_ssm_c_im, v_ssm_d, v_ssm_w_glu, v_ssm_b_glu, v_mla_q_norm_w,
                             v_mla_w_uq, v_mla_kv_norm_w, v_mla_w_ukv, v_ssm_out_norm_w, v_mla_out_norm_w, v_w_out,
                             v_ffn_norm_w, v_ffn_w_up, v_ffn_conv_w, v_ffn_conv_b, v_ffn_w_down, v_final_norm_w]))
    seq, d = x.shape[1], x.shape[2]
    in_width = w_in.shape[2]
    in_pad = -(-in_width // LANES) * LANES
    q_cols = mla_w_uq.shape[2]
    q_pad = 2 * LANES

    (win_g,) = _all_gather("gather_w_in", [jnp.pad(w_in[0], ((0, 0), (0, in_pad - in_width))).astype(BF16)])
    wglu_g, wuq_g, wukv_g, wout_g, convw_g = _all_gather(
        "gather_mix", [ssm_w_glu[0].astype(BF16), jnp.pad(mla_w_uq[0], ((0, 0), (0, q_pad - q_cols))).astype(BF16),
                       mla_w_ukv[0].astype(BF16), w_out[0].astype(BF16), ffn_conv_w[0]], collective_id=0)
    (wup_g,) = _all_gather("gather_ffn_up", [ffn_w_up[0].astype(BF16)], collective_id=1)
    (wdown_g,) = _all_gather("gather_ffn_down", [ffn_w_down[0].astype(BF16)], collective_id=2)
    ns = N_DEV
    c_ff = wup_g.shape[2]
    w = dict(
        attn_norm=attn_norm_w, win=win_g.reshape(d, in_pad), lam_re=ssm_lambda_re, lam_im=ssm_lambda_im,
        log_dt=ssm_log_dt, b_re=ssm_b_re, b_im=ssm_b_im, c_re=ssm_c_re, c_im=ssm_c_im, ssm_d=ssm_d,
        wglu=wglu_g.reshape(d // 2, d // 2), b_glu=ssm_b_glu, q_norm=mla_q_norm_w, wuq=wuq_g,
        kv_norm=mla_kv_norm_w, wukv=wukv_g, son=ssm_out_norm_w, mon=mla_out_norm_w, wout=wout_g.reshape(d, d),
        ffn_norm=ffn_norm_w, wup=wup_g, conv_w=convw_g, conv_b=ffn_conv_b,
        wdown=wdown_g.reshape(ns // 2 * c_ff, d), final_norm=final_norm_w)

    shard_layout = dict(
        win=lambda a: a[:, :in_width].reshape(N_DEV, d // N_DEV, in_width),
        wglu=lambda a: a.reshape(N_DEV, d // 2 // N_DEV, d // 2),
        wuq=lambda a: a[:, :, :q_cols], wukv=lambda a: a, wout=lambda a: a.reshape(N_DEV, d // N_DEV, d),
        wup=lambda a: a, wdown=lambda a: a.reshape(N_DEV, c_ff // 2, d), conv_w=lambda a: a)
    recv = {}
    next_id = [3]

    last = [None]

    out = {}

    def update(k):
        shp = wts[k].shape
        r, c = shp[-2], shp[-1]
        res = _adamw_sum("adamw_" + k, recv[k].reshape(-1, r, c), wts[k].reshape(r, c),
                         moms[k].reshape(r, c), vels[k].reshape(r, c))
        out[k] = [a.reshape(shp) for a in res]
        return res[0]

    pending = {}

    def exchange(not_before=(), **grads):
        names = list(grads)
        if len(names) == 1 and names[0] in _TWO_LEVEL:
            k = names[0]
            parts = shard_layout[k](grads[k])
            got = _pair_swap("swap_" + k, [parts], collective_id=next_id[0], after=[last[0]])[0]
            next_id[0] += 1
            pending[k] = (parts, got)
            last[0] = got
            return
        if len(names) == 1 and names[0].endswith(_AFTER):
            k = names[0][:-len(_AFTER)]
            sums = _pair_add("pair_add_" + k, *pending[k])
            if k == "win":
                pending["tail"] = sums
                return
            recv[k] = _chip_exchange("exchange_" + k, [sums], collective_id=next_id[0],
                                     after=[last[0], grads[names[0]]])[0]
            next_id[0] += 1
            last[0] = recv[k]
            return
        got = _exchange_partials("exchange_" + "_".join(names), [shard_layout[k](grads[k]) for k in names],
                                 collective_id=next_id[0], after=[a for a in (last[0], *not_before) if a is not None])
        next_id[0] += 1
        last[0] = got[-1]
        recv.update(zip(names, got))

    loss_part, grad_x, g = _local_step(x[0], positions[0], loss_target[0], w, emit=exchange)
    loss = lax.psum(loss_part, ("x", "y", "c"))
    n_groups = ssm_lambda_re.shape[1]
    two_d = {"lam_re": (n_groups, -1), "lam_im": (n_groups, -1)}
    dense = {k: g[k].reshape(two_d.get(k, (1, -1))) for k in _VECTORS}
    offsets, width = {}, 0
    for k in _PACKED:
        offsets[k] = width
        width += dense[k].shape[1]
    sent = dict(packed=jnp.concatenate([dense[k] for k in _PACKED], axis=1),
                **{k: dense[k] for k in _VECTORS if k not in _PACKED},
                **{k: g[k].reshape(n_groups, -1) for k in _GHP})
    names = list(sent)
    got = _all_gather("gather_small_grads", [sent[k] for k in names], collective_id=next_id[0], after=[last[0]],
                      pair_sums=[pending["tail"]])
    gathered = dict(zip(names, got))
    recv["win"] = got[len(names)]
    for k in _BIG:
        if k not in out and k != "win":
            update(k)
    update("win")

    def finish(keys, results):
        for k, res in zip(keys, results):
            out[k] = [a.reshape(wts[k].shape) for a in res]

    view = lambda k, a: a.reshape(dense[k].shape)
    finish(_VECTORS, _adamw_multi("adamw_vectors", [(offsets.get(k, gathered.get(k)), view(k, wts[k]), view(k, moms[k]),
                                                     view(k, vels[k])) for k in _VECTORS], packed=gathered["packed"]))
    sums = dict(zip(_GHP, _sum_multi("sum_ssm_bc", [gathered[k] for k in _GHP])))
    ghp = lambda k: sums[k].reshape(g[k].shape)
    c_keys = ["c_re", "c_im"]
    finish(c_keys, _adamw_multi("adamw_ssm_c", [(ghp(k)[None, None], wts[k], moms[k], vels[k]) for k in c_keys]))
    b_keys = ["b_re", "b_im"]
    finish(b_keys, _adamw_multi("adamw_ssm_b", [(jnp.swapaxes(ghp(t), 1, 2)[None, None], wts[k], moms[k], vels[k])
                                                for k, t in zip(b_keys, ("bt_re", "bt_im"))], nblk=SUBLANES))

    grad_x = grad_x.reshape(x.shape)
    return (loss, grad_x, *[out[k][0] for k in _ORDER], *[out[k][1] for k in _ORDER],
            *[out[k][2] for k in _ORDER], *[out[k][3] for k in _ORDER])
```

```python
import functools
import math

import jax
import jax.numpy as jnp
from jax import lax
from jax.experimental import pallas as pl
from jax.experimental.pallas import tpu as pltpu
from jax.experimental.pallas import tpu_sc as plsc

F32 = jnp.float32
BF16 = jnp.bfloat16
MESH = pl.DeviceIdType.MESH

N_DEV = 8
LANES = 128
SUBLANES = 8
VMEM_LIMIT = 48 * 1024 * 1024

SSM_GROUP = 16
SSM_STATE = 64
GROUPS_PER_BLOCK = LANES // SSM_GROUP
STATE_BLOCK = GROUPS_PER_BLOCK * SSM_STATE
QK_NOPE = 128
QK_ROPE = 64
V_DIM = 128
ROPE_THETA = 10000.0
RMS_EPS = 1e-6

ADAM_LR = 0.001
ADAM_B1 = 0.9
ADAM_B2 = 0.999
ADAM_EPS = 1e-08
ADAM_WD = 0.01
ADAM_STEP = 10

NN = ((1,), (0,))
NT = ((1,), (1,))
TN = ((0,), (0,))


def _cparams():
    return pltpu.CompilerParams(vmem_limit_bytes=VMEM_LIMIT)


def _tile(n, want):
    if n <= want:
        return n
    t = (want // LANES) * LANES
    while t >= LANES:
        if n % t == 0:
            return t
        t -= LANES
    return n


def _mm(name, a, b, *, grid, a_spec, b_spec, o_spec, out_shape, out_dtype, contract=NN,
        res=None, res_spec=None):
    nk = grid[-1]
    kaxis = len(grid) - 1
    acc_shape = tuple(d for d in o_spec.block_shape if d is not None)

    def body(*refs):
        a_ref, b_ref = refs[:2]
        r_ref = None if res is None else refs[2]
        o_ref = refs[2 if res is None else 3]
        part = lax.dot_general(a_ref[...].astype(BF16), b_ref[...].astype(BF16),
                               (contract, ((), ())), preferred_element_type=F32)
        if nk == 1:
            if r_ref is not None:
                part = part + r_ref[...].astype(F32)
            o_ref[...] = part.astype(o_ref.dtype)
            return
        acc = refs[-1]
        k = pl.program_id(kaxis)

        @pl.when(k == 0)
        def _():
            acc[...] = part

        @pl.when(k != 0)
        def _():
            acc[...] += part

        @pl.when(k == nk - 1)
        def _():
            r = acc[...]
            if r_ref is not None:
                r = r + r_ref[...].astype(F32)
            o_ref[...] = r.astype(o_ref.dtype)

    ins = [a, b] + ([] if res is None else [res])
    in_specs = [a_spec, b_spec] + ([] if res is None else [res_spec])
    return pl.pallas_call(
        body, name=name, grid=grid, in_specs=in_specs, out_specs=o_spec,
        out_shape=jax.ShapeDtypeStruct(out_shape, out_dtype),
        scratch_shapes=[pltpu.VMEM(acc_shape, F32)] if nk > 1 else [], compiler_params=_cparams(),
    )(*ins)


def _mm2d(name, a, b, contract, out_dtype, tm=1024, tn=1024, tk=2048, res=None):
    if contract == NN:
        (m, kk), n = a.shape, b.shape[1]
    elif contract == NT:
        (m, kk), n = a.shape, b.shape[0]
    else:
        (kk, m), n = a.shape, b.shape[1]
    tm, tn, tk = _tile(m, tm), _tile(n, tn), _tile(kk, tk)
    grid = (m // tm, n // tn, kk // tk)
    if contract == TN:
        a_spec = pl.BlockSpec((tk, tm), lambda i, j, k: (k, i))
    else:
        a_spec = pl.BlockSpec((tm, tk), lambda i, j, k: (i, k))
    if contract == NT:
        b_spec = pl.BlockSpec((tn, tk), lambda i, j, k: (j, k))
    else:
        b_spec = pl.BlockSpec((tk, tn), lambda i, j, k: (k, j))
    o_spec = pl.BlockSpec((tm, tn), lambda i, j, k: (i, j))
    res_spec = None
    if res is not None:
        if res.shape[0] == 1:
            res_spec = pl.BlockSpec((1, tn), lambda i, j, k: (0, j))
        else:
            res_spec = pl.BlockSpec((tm, tn), lambda i, j, k: (i, j))
    return _mm(name, a, b, grid=grid, a_spec=a_spec, b_spec=b_spec, o_spec=o_spec,
               out_shape=(m, n), out_dtype=out_dtype, contract=contract, res=res, res_spec=res_spec)


def _blockwise(name, fn, ins, in_specs, outs, out_specs, grid, n_acc=0, acc_all=True):
    n_in, n_out = len(ins), len(outs)
    n_plain = n_out - n_acc

    def body(*refs):
        vals = fn(*[r[...] for r in refs[:n_in]])
        if not isinstance(vals, (tuple, list)):
            vals = (vals,)
        o_refs = refs[n_in:n_in + n_out]
        for r, v in zip(o_refs[:n_plain], vals[:n_plain]):
            r[...] = v.astype(r.dtype)
        if n_acc:
            if acc_all:
                first = functools.reduce(jnp.logical_and, [pl.program_id(d) == 0 for d in range(len(grid))])
            else:
                first = pl.program_id(len(grid) - 1) == 0

            @pl.when(first)
            def _():
                for r, v in zip(o_refs[n_plain:], vals[n_plain:]):
                    r[...] = v.astype(r.dtype)

            @pl.when(jnp.logical_not(first))
            def _():
                for r, v in zip(o_refs[n_plain:], vals[n_plain:]):
                    r[...] += v.astype(r.dtype)

    return pl.pallas_call(
        body, name=name, grid=grid, in_specs=in_specs, out_specs=out_specs,
        out_shape=[jax.ShapeDtypeStruct(s, d) for s, d in outs], compiler_params=_cparams(),
    )(*ins)


def _row_spec(t, c):
    return pl.BlockSpec((t, c), lambda i: (i, 0))


def _full_spec(shape):
    nd = len(shape)
    return pl.BlockSpec(tuple(shape), lambda *g: (0,) * nd)


def _rms(xf, w):
    return xf * lax.rsqrt(jnp.mean(xf * xf, axis=-1, keepdims=True) + RMS_EPS) * w


def _rms_bwd(xf, w, dy):
    _, vjp = jax.vjp(_rms, xf, w)
    return vjp(dy)


def _s5_disc(lr, li, ldt, bre, bim):
    dt = jnp.exp(ldt)
    mag = jnp.exp(lr * dt)
    ar = mag * jnp.cos(li * dt)
    ai = mag * jnp.sin(li * dt)
    nr, ni = ar - 1.0, ai
    den = lr * lr + li * li
    zr = (nr * lr + ni * li) / den
    zi = (ni * lr - nr * li) / den
    return ar, ai, zr * bre - zi * bim, zr * bim + zi * bre


def _s5_prep(lr, li, ldt, bre, bim):
    def body(lr_r, li_r, ldt_r, bre_r, bim_r, ar_r, ai_r, br_r, bi_r):
        ar, ai, br, bi = _s5_disc(lr_r[...], li_r[...], ldt_r[...], bre_r[...], bim_r[...])
        ar_r[...] = ar
        ai_r[...] = ai
        br_r[...] = br
        bi_r[...] = bi

    sd = jax.ShapeDtypeStruct
    return pl.pallas_call(
        body, name="s5_prep",
        out_shape=[sd(lr.shape, F32), sd(lr.shape, F32), sd(bre.shape, F32), sd(bre.shape, F32)],
        compiler_params=_cparams(),
    )(lr, li, ldt, bre, bim)


def _s5_prep_bwd(lr, li, ldt, bre, bim, dar, dai, dbr, dbi):
    def body(lr_r, li_r, ldt_r, bre_r, bim_r, dar_r, dai_r, dbr_r, dbi_r, o0, o1, o2, o3, o4):
        _, vjp = jax.vjp(_s5_disc, lr_r[...], li_r[...], ldt_r[...], bre_r[...], bim_r[...])
        g = vjp((dar_r[...], dai_r[...], dbr_r[...], dbi_r[...]))
        for o, v in zip((o0, o1, o2, o3, o4), g):
            o[...] = v

    sd = jax.ShapeDtypeStruct
    return pl.pallas_call(
        body, name="s5_prep_bwd",
        out_shape=[sd(lr.shape, F32), sd(li.shape, F32), sd(ldt.shape, F32), sd(bre.shape, F32), sd(bim.shape, F32)],
        compiler_params=_cparams(),
    )(lr, li, ldt, bre, bim, dar, dai, dbr, dbi)


SCAN_T = 256


def _scan_tables(ar, ai, tab_r, tab_i, sub, reverse):
    pr, pi = ar, ai
    for k in range(sub):
        row = sub - 1 - k if reverse else k
        tab_r[row:row + 1, :] = pr
        tab_i[row:row + 1, :] = pi
        pr, pi = ar * pr - ai * pi, ar * pi + ai * pr


def _pack_matrix(t_blk, dtype):
    sub = t_blk // SUBLANES
    dst = jnp.arange(t_blk)
    src = (dst % SUBLANES) * sub + dst // SUBLANES
    return (src[:, None] == jnp.arange(t_blk)[None, :]).astype(dtype)


def _permute_rows_f32(pm, x):
    hi = x.astype(BF16)
    r1 = x - hi.astype(F32)
    mid = r1.astype(BF16)
    lo = (r1 - mid.astype(F32)).astype(BF16)
    dot = lambda v: jnp.dot(pm, v, preferred_element_type=F32)
    return dot(hi) + dot(mid) + dot(lo)


def _scan_block(x, loc, ar, ai, st, tab_r, tab_i, sub, reverse):
    hb = STATE_BLOCK
    a8r = jnp.broadcast_to(ar, (SUBLANES, hb))
    a8i = jnp.broadcast_to(ai, (SUBLANES, hb))
    sr = jnp.zeros((SUBLANES, hb), F32)
    si = jnp.zeros((SUBLANES, hb), F32)
    steps = range(sub - 1, -1, -1) if reverse else range(sub)
    for t in steps:
        rows = slice(t * SUBLANES, (t + 1) * SUBLANES)
        sr, si = a8r * sr - a8i * si + x[rows, :hb], a8r * si + a8i * sr + x[rows, hb:]
        loc[rows, :hb] = sr
        loc[rows, hb:] = si
    cr, ci = st[0:1, :], st[1:2, :]
    far = 0 if reverse else sub - 1
    fr, fi = tab_r[far:far + 1, :], tab_i[far:far + 1, :]
    ent_r, ent_i = [None] * SUBLANES, [None] * SUBLANES
    for c in (range(SUBLANES - 1, -1, -1) if reverse else range(SUBLANES)):
        ent_r[c], ent_i[c] = cr, ci
        cr, ci = sr[c:c + 1, :] + (fr * cr - fi * ci), si[c:c + 1, :] + (fr * ci + fi * cr)
    st[0:1, :] = cr
    st[1:2, :] = ci
    c8r = jnp.concatenate(ent_r, axis=0)
    c8i = jnp.concatenate(ent_i, axis=0)
    out = []
    for t in range(sub):
        rows = slice(t * SUBLANES, (t + 1) * SUBLANES)
        tr, ti = tab_r[t:t + 1, :], tab_i[t:t + 1, :]
        out.append(jnp.concatenate([loc[rows, :hb] + (tr * c8r - ti * c8i), loc[rows, hb:] + (tr * c8i + ti * c8r)],
                                   axis=1))
    return jnp.concatenate(out, axis=0)


SSM_BLOCKS_PER_STEP = 2


def _scan_scratch(nblk, t_blk, sub, hb):
    return [pltpu.VMEM((nblk, SUBLANES, hb), F32), pltpu.VMEM((nblk, sub, hb), F32), pltpu.VMEM((nblk, sub, hb), F32),
            pltpu.VMEM((nblk, t_blk, 2 * hb), F32)]


def _ssm_fwd(proj, wb, wc, a):
    seq = proj.shape[0]
    nj = wb.shape[0]
    w2 = 2 * STATE_BLOCK
    hb = STATE_BLOCK
    t_blk = min(SCAN_T, seq)
    sub = t_blk // SUBLANES
    pm = _pack_matrix(t_blk, BF16)

    npair = SSM_BLOCKS_PER_STEP

    def body(u_ref, wb_ref, wc_ref, a_ref, pm_ref, pmt_ref, s_ref, y_ref, st, tab_r, tab_i, loc):
        coef = [(a_ref[:, b * w2:b * w2 + hb], a_ref[:, b * w2 + hb:(b + 1) * w2]) for b in range(npair)]

        @pl.when(pl.program_id(1) == 0)
        def _():
            for b, (ar, ai) in enumerate(coef):
                st[b] = jnp.zeros((SUBLANES, hb), F32)
                _scan_tables(ar, ai, tab_r.at[b], tab_i.at[b], sub, False)

        for b, (ar, ai) in enumerate(coef):
            ub = u_ref[:, b * LANES:(b + 1) * LANES].astype(BF16)
            up = jnp.dot(pm_ref[...], ub, preferred_element_type=F32).astype(BF16)
            bu = jnp.dot(up, wb_ref[b], preferred_element_type=F32)
            s = _scan_block(bu, loc.at[b], ar, ai, st.at[b], tab_r.at[b], tab_i.at[b], sub, False)
            s_ref[:, b * w2:(b + 1) * w2] = s
            yp = jnp.dot(s.astype(BF16), wc_ref[b], preferred_element_type=F32)
            y_ref[:, b * LANES:(b + 1) * LANES] = _permute_rows_f32(pmt_ref[...], yp)

    sd = jax.ShapeDtypeStruct
    return pl.pallas_call(
        body, name="ssm_fwd", grid=(nj // npair, seq // t_blk),
        in_specs=[pl.BlockSpec((t_blk, npair * LANES), lambda j, i: (i, j)),
                  pl.BlockSpec((npair, LANES, w2), lambda j, i: (j, 0, 0)),
                  pl.BlockSpec((npair, w2, LANES), lambda j, i: (j, 0, 0)),
                  pl.BlockSpec((1, npair * w2), lambda j, i: (0, j)),
                  _full_spec((t_blk, t_blk)), _full_spec((t_blk, t_blk))],
        out_specs=[pl.BlockSpec((t_blk, npair * w2), lambda j, i: (i, j)),
                   pl.BlockSpec((t_blk, npair * LANES), lambda j, i: (i, j))],
        out_shape=[sd((seq, nj * w2), F32), sd((seq, nj * LANES), F32)],
        scratch_shapes=_scan_scratch(npair, t_blk, sub, hb), compiler_params=_cparams(),
    )(proj, wb, wc, a, pm, pm.T)


def _ssm_bwd(dy, s, proj, du1, wb, wc, a):
    seq = dy.shape[0]
    nj = wb.shape[0]
    w2 = 2 * STATE_BLOCK
    hb = STATE_BLOCK
    t_blk = min(SCAN_T, seq)
    sub = t_blk // SUBLANES
    nb = seq // t_blk
    pm = _pack_matrix(t_blk, BF16)

    npair = SSM_BLOCKS_PER_STEP

    def body(dy_ref, s_ref, sprev_ref, u_ref, du1_ref, wb_ref, wc_ref, a_ref, pm_ref, pmt_ref,
             du_ref, dwb_ref, dwc_ref, da_ref, st, tab_r, tab_i, loc):
        ib = pl.program_id(1)
        pmv = pm_ref[...]
        coef = [(a_ref[:, b * w2:b * w2 + hb], -a_ref[:, b * w2 + hb:(b + 1) * w2]) for b in range(npair)]

        @pl.when(ib == 0)
        def _():
            for b, (ar, ai) in enumerate(coef):
                st[b] = jnp.zeros((SUBLANES, hb), F32)
                _scan_tables(ar, ai, tab_r.at[b], tab_i.at[b], sub, True)

        sums = []
        for b, (ar, ai) in enumerate(coef):
            cols, wide = slice(b * LANES, (b + 1) * LANES), slice(b * w2, (b + 1) * w2)
            dyp = jnp.dot(pmv, dy_ref[:, cols], preferred_element_type=F32).astype(BF16)
            up = jnp.dot(pmv, u_ref[:, cols].astype(BF16), preferred_element_type=F32).astype(BF16)
            ds = lax.dot_general(dyp, wc_ref[b], (NT, ((), ())), preferred_element_type=F32)
            lam = _scan_block(ds, loc.at[b], ar, ai, st.at[b], tab_r.at[b], tab_i.at[b], sub, True)
            lamb = lam.astype(BF16)
            du = lax.dot_general(lamb, wb_ref[b], (NT, ((), ())), preferred_element_type=F32)
            du_ref[:, cols] = (_permute_rows_f32(pmt_ref[...], du) + du1_ref[:, cols]).astype(du_ref.dtype)
            sv = s_ref[:, wide]
            dwb = lax.dot_general(up, lamb, (TN, ((), ())), preferred_element_type=F32)
            dwc = lax.dot_general(sv.astype(BF16), dyp, (TN, ((), ())), preferred_element_type=F32)

            prev_last = sprev_ref[SUBLANES - 1:SUBLANES, wide]
            prev_last = jnp.where(ib == nb - 1, jnp.zeros_like(prev_last), prev_last)
            tail = sv[t_blk - SUBLANES:, :]
            sl = lax.broadcasted_iota(jnp.int32, tail.shape, 0)
            head = jnp.where(sl >= 1, pltpu.roll(tail, 1, 0), prev_last)
            s_sh = jnp.concatenate([head, sv[:t_blk - SUBLANES, :]], axis=0)
            lam_r, lam_i = lam[:, :hb], lam[:, hb:]
            sr_, si_ = s_sh[:, :hb], s_sh[:, hb:]
            dar = jnp.sum(lam_r * sr_ + lam_i * si_, axis=0, keepdims=True)
            dai = jnp.sum(lam_i * sr_ - lam_r * si_, axis=0, keepdims=True)
            sums.append((wide, jnp.concatenate([dar, dai], axis=1), dwb, dwc))

        @pl.when(ib == 0)
        def _():
            for b, (wide, contrib, dwb, dwc) in enumerate(sums):
                da_ref[:, wide] = contrib
                dwb_ref[b] = dwb
                dwc_ref[b] = dwc

        @pl.when(ib != 0)
        def _():
            for b, (wide, contrib, dwb, dwc) in enumerate(sums):
                da_ref[:, wide] += contrib
                dwb_ref[b] += dwb
                dwc_ref[b] += dwc

    blk = lambda j, i: (nb - 1 - i, j)
    prev_blk = lambda j, i: (jnp.maximum((nb - 1 - i) * sub - 1, 0), j)
    sd = jax.ShapeDtypeStruct
    return pl.pallas_call(
        body, name="ssm_bwd", grid=(nj // npair, nb),
        in_specs=[pl.BlockSpec((t_blk, npair * LANES), blk), pl.BlockSpec((t_blk, npair * w2), blk),
                  pl.BlockSpec((SUBLANES, npair * w2), prev_blk), pl.BlockSpec((t_blk, npair * LANES), blk),
                  pl.BlockSpec((t_blk, npair * LANES), blk),
                  pl.BlockSpec((npair, LANES, w2), lambda j, i: (j, 0, 0)),
                  pl.BlockSpec((npair, w2, LANES), lambda j, i: (j, 0, 0)),
                  pl.BlockSpec((1, npair * w2), lambda j, i: (0, j)),
                  _full_spec((t_blk, t_blk)), _full_spec((t_blk, t_blk))],
        out_specs=[pl.BlockSpec((t_blk, npair * LANES), blk),
                   pl.BlockSpec((npair, LANES, w2), lambda j, i: (j, 0, 0)),
                   pl.BlockSpec((npair, w2, LANES), lambda j, i: (j, 0, 0)),
                   pl.BlockSpec((1, npair * w2), lambda j, i: (0, j))],
        out_shape=[sd((seq, nj * LANES), BF16), sd((nj, LANES, w2), F32), sd((nj, w2, LANES), F32),
                   sd((1, nj * w2), F32)],
        scratch_shapes=_scan_scratch(npair, t_blk, sub, hb), compiler_params=_cparams(),
    )(dy, s, s, proj, du1, wb, wc, a, pm, pm.T)


def _rope128(x, cos, sa, sb):
    return x * cos + pltpu.roll(x, 96, 1) * sa + pltpu.roll(x, 32, 1) * sb


def _rope128_t(dy, cos, sa, sb):
    return dy * cos + pltpu.roll(dy * sa, 32, 1) + pltpu.roll(dy * sb, 96, 1)


ATT_BQ = 256


def _probs(qn, qp, kn, kp, r0, scale):
    s = lax.dot_general(qn, kn, (NT, ((), ())), preferred_element_type=F32)
    s = s + lax.dot_general(qp, kp, (NT, ((), ())), preferred_element_type=F32)
    s = s * scale
    diag = s[:, r0:]
    row = lax.broadcasted_iota(jnp.int32, diag.shape, 0)
    col = lax.broadcasted_iota(jnp.int32, diag.shape, 1)
    diag = jnp.where(col <= row, diag, jnp.finfo(F32).min)
    s = diag if r0 == 0 else jnp.concatenate([s[:, :r0], diag], axis=1)
    m = jnp.max(s, axis=-1, keepdims=True)
    e = jnp.exp(s - m)
    return e / jnp.sum(e, axis=-1, keepdims=True)


def _attn_specs(seq):
    tab = pl.BlockSpec((seq, LANES), lambda h: (0, 0))
    return [pl.BlockSpec((None, seq, 256), lambda h: (h, 0, 0)), pl.BlockSpec((None, seq, 128), lambda h: (h, 0, 0)),
            pl.BlockSpec((None, seq, 128), lambda h: (h, 0, 1)), tab, tab, tab, tab]


def _attn_fwd(q_raw, kv, kpe, cos, sa, sb):
    nh, seq, _ = q_raw.shape
    bq = min(ATT_BQ, seq)
    scale = (QK_NOPE + QK_ROPE) ** -0.5

    def body(q_ref, kn_ref, v_ref, kp_ref, cos_ref, sa_ref, sb_ref, o_ref):
        for r0 in range(0, seq, bq):
            rows, kend = pl.ds(r0, bq), r0 + bq
            qn = q_ref[rows, :QK_NOPE].astype(BF16)
            qp = _rope128(q_ref[rows, QK_NOPE:], cos_ref[rows, :], sa_ref[rows, :], sb_ref[rows, :]).astype(BF16)
            p = _probs(qn, qp, kn_ref[:kend, :], kp_ref[:kend, :], r0, scale)
            o_ref[rows, :] = jnp.dot(p.astype(BF16), v_ref[:kend, :], preferred_element_type=F32)

    return pl.pallas_call(
        body, name="attn_fwd", grid=(nh,), in_specs=_attn_specs(seq),
        out_specs=pl.BlockSpec((seq, V_DIM), lambda h: (0, h)),
        out_shape=jax.ShapeDtypeStruct((seq, nh * V_DIM), F32), compiler_params=_cparams(),
    )(q_raw, kv, kv, kpe, cos, sa, sb)


def _attn_bwd(q_raw, kv, kpe, cos, sa, sb, do):
    nh, seq, _ = q_raw.shape
    bq = min(ATT_BQ, seq)
    scale = (QK_NOPE + QK_ROPE) ** -0.5

    def body(q_ref, kn_ref, v_ref, kp_ref, cos_ref, sa_ref, sb_ref, do_ref, dq_ref, dkv_ref, dkp_ref):
        dkv_ref[...] = jnp.zeros_like(dkv_ref)
        dkp_ref[...] = jnp.zeros_like(dkp_ref)
        for r0 in range(0, seq, bq):
            rows, kend = pl.ds(r0, bq), r0 + bq
            cos_b, sa_b, sb_b = cos_ref[rows, :], sa_ref[rows, :], sb_ref[rows, :]
            qn = q_ref[rows, :QK_NOPE].astype(BF16)
            qp = _rope128(q_ref[rows, QK_NOPE:], cos_b, sa_b, sb_b).astype(BF16)
            kn, v, kp = kn_ref[:kend, :], v_ref[:kend, :], kp_ref[:kend, :]
            p = _probs(qn, qp, kn, kp, r0, scale)
            dob = do_ref[rows, :].astype(BF16)
            dp = lax.dot_general(dob, v, (NT, ((), ())), preferred_element_type=F32)
            ds = p * (dp - jnp.sum(p * dp, axis=-1, keepdims=True)) * scale
            dsb = ds.astype(BF16)
            pb = p.astype(BF16)
            dq_ref[rows, :QK_NOPE] = jnp.dot(dsb, kn, preferred_element_type=F32).astype(dq_ref.dtype)
            dqp = jnp.dot(dsb, kp, preferred_element_type=F32)
            dq_ref[rows, QK_NOPE:] = _rope128_t(dqp, cos_b, sa_b, sb_b).astype(dq_ref.dtype)
            dkv_ref[:kend, :QK_NOPE] += lax.dot_general(dsb, qn, (TN, ((), ())), preferred_element_type=F32)
            dkv_ref[:kend, QK_NOPE:] += lax.dot_general(pb, dob, (TN, ((), ())), preferred_element_type=F32)
            dkp_ref[:kend, :] += lax.dot_general(dsb, qp, (TN, ((), ())), preferred_element_type=F32)

    sd = jax.ShapeDtypeStruct
    return pl.pallas_call(
        body, name="attn_bwd", grid=(nh,),
        in_specs=_attn_specs(seq) + [pl.BlockSpec((seq, V_DIM), lambda h: (0, h))],
        out_specs=[pl.BlockSpec((None, seq, 256), lambda h: (h, 0, 0)),
                   pl.BlockSpec((None, seq, 256), lambda h: (h, 0, 0)),
                   pl.BlockSpec((None, seq, 128), lambda h: (h, 0, 0))],
        out_shape=[sd((nh, seq, 256), BF16), sd((nh, seq, 256), F32), sd((nh, seq, 128), F32)],
        compiler_params=_cparams(),
    )(q_raw, kv, kv, kpe, cos, sa, sb, do)


def _shift_rows(a, k):
    seq = a.shape[0]
    r = pltpu.roll(a, k % seq, 0)
    rows = lax.broadcasted_iota(jnp.int32, (SUBLANES, a.shape[1]), 0)
    if k > 0:
        return jnp.concatenate([jnp.where(rows >= k, r[:SUBLANES], 0.0), r[SUBLANES:]], axis=0)
    return jnp.concatenate([r[:seq - SUBLANES], jnp.where(rows < SUBLANES + k, r[seq - SUBLANES:], 0.0)], axis=0)


def _conv3(a, w, b):
    a1 = _shift_rows(a, 1)
    a2 = _shift_rows(a, 2)
    return w[2:3] * a + w[1:2] * a1 + w[0:1] * a2 + b, a1, a2


def _conv_gate_fwd(a, cw, cb):
    half, _, seq, c = a.shape
    nc = c // LANES

    def fn(pair, wg, wv, bg, bv):
        gc, _, _ = _conv3(pair[0], wg, bg)
        vc, _, _ = _conv3(pair[1], wv, bv)
        return gc * jax.nn.sigmoid(gc) * vc

    def w_spec(off, r):
        return pl.BlockSpec((None, r, LANES), lambda k, j: (k + off, 0, j))

    return _blockwise(
        "conv_gate_fwd", fn, [a, cw, cw, cb, cb],
        [pl.BlockSpec((None, 2, seq, LANES), lambda k, j: (k, 0, 0, j)),
         w_spec(0, 3), w_spec(half, 3), w_spec(0, 1), w_spec(half, 1)],
        [((seq, half * c), BF16)], [pl.BlockSpec((seq, LANES), lambda k, j: (0, k * nc + j))],
        grid=(half, nc))[0]


def _conv_gate_bwd(a, cw, cb, dm):
    half, _, seq, c = a.shape
    nc = c // LANES

    def body(a_ref, wg_ref, wv_ref, bg_ref, bv_ref, dm_ref, da_ref, dw_ref, db_ref):
        dmv = dm_ref[...]
        ga, wg = a_ref[0], wg_ref[...]
        va, wv = a_ref[1], wv_ref[...]
        gc, g1, g2 = _conv3(ga, wg, bg_ref[...])
        vc, v1, v2 = _conv3(va, wv, bv_ref[...])
        sg = jax.nn.sigmoid(gc)
        dms = dmv * sg
        d_val = dms * gc
        d_gate = dms * vc * (1.0 + gc * (1.0 - sg))

        def back(r, dc, own, a1, a2, w):
            up1 = _shift_rows(dc, -1)
            up2 = _shift_rows(dc, -2)
            da_ref[r] = (w[2:3] * dc + w[1:2] * up1 + w[0:1] * up2).astype(da_ref.dtype)
            dw_ref[r, 0:1, :] = jnp.sum(dc * a2, axis=0, keepdims=True)
            dw_ref[r, 1:2, :] = jnp.sum(dc * a1, axis=0, keepdims=True)
            dw_ref[r, 2:3, :] = jnp.sum(dc * own, axis=0, keepdims=True)
            db_ref[r] = jnp.sum(dc, axis=0, keepdims=True)

        back(0, d_gate, ga, g1, g2, wg)
        back(1, d_val, va, v1, v2, wv)

    def w_spec(off, r):
        return pl.BlockSpec((None, r, LANES), lambda k, j: (k + off, 0, j))

    def pair_spec(r):
        return pl.BlockSpec((None, 2, r, LANES), lambda k, j: (k, 0, 0, j))

    sd = jax.ShapeDtypeStruct
    return pl.pallas_call(
        body, name="conv_gate_bwd", grid=(half, nc),
        in_specs=[pair_spec(seq), w_spec(0, 3), w_spec(half, 3), w_spec(0, 1), w_spec(half, 1),
                  pl.BlockSpec((seq, LANES), lambda k, j: (0, k * nc + j))],
        out_specs=[pair_spec(seq), pair_spec(3), pair_spec(1)],
        out_shape=[sd((half, 2, seq, c), BF16), sd((half, 2, 3, c), F32), sd((half, 2, 1, c), F32)],
        compiler_params=_cparams(),
    )(a, cw, cw, cb, cb, dm)


ROW_T = 256


def _local_step(x, positions, target, w, emit=lambda **grads: None):
    seq, d = x.shape
    t_row = min(ROW_T, seq)
    nrow = seq // t_row
    ssm_w = d // 2
    nj = ssm_w // LANES
    n_groups = ssm_w // SSM_GROUP
    nh = w["wuq"].shape[0]
    q_rank = w["wuq"].shape[1]
    kv_rank = w["wukv"].shape[1]
    ns = w["wup"].shape[0]
    c_ff = w["wup"].shape[2]
    in_pad = w["win"].shape[1]
    tm = min(1024, seq)
    nm = seq // tm
    sw = 2 * STATE_BLOCK
    g1 = (nrow,)

    lr3 = w["lam_re"].reshape(n_groups, 1, SSM_STATE)
    li3 = w["lam_im"].reshape(n_groups, 1, SSM_STATE)
    ldt3 = w["log_dt"].reshape(n_groups, 1, 1)
    bt_re = jnp.swapaxes(w["b_re"].reshape(n_groups, SSM_STATE, SSM_GROUP), 1, 2)
    bt_im = jnp.swapaxes(w["b_im"].reshape(n_groups, SSM_STATE, SSM_GROUP), 1, 2)
    abar_re, abar_im, bbt_re, bbt_im = _s5_prep(lr3, li3, ldt3, bt_re, bt_im)
    eye = jnp.eye(GROUPS_PER_BLOCK, dtype=F32)

    def blockdiag_in(bb):
        t = bb.reshape(nj, GROUPS_PER_BLOCK, SSM_GROUP, SSM_STATE)
        return jnp.einsum("jghp,gk->jghkp", t, eye).reshape(nj, LANES, STATE_BLOCK)

    def blockdiag_in_t(dwb):
        t = dwb.reshape(nj, GROUPS_PER_BLOCK, SSM_GROUP, GROUPS_PER_BLOCK, SSM_STATE)
        return jnp.einsum("jghkp,gk->jghp", t, eye).reshape(n_groups, SSM_GROUP, SSM_STATE)

    def blockdiag_out(cc):
        t = cc.reshape(nj, GROUPS_PER_BLOCK, SSM_GROUP, SSM_STATE)
        return jnp.einsum("jghp,gk->jkpgh", t, eye).reshape(nj, STATE_BLOCK, LANES)

    def blockdiag_out_t(dwc):
        t = dwc.reshape(nj, GROUPS_PER_BLOCK, SSM_STATE, GROUPS_PER_BLOCK, SSM_GROUP)
        return jnp.einsum("jkpgh,gk->jghp", t, eye).reshape(n_groups, SSM_GROUP, SSM_STATE)

    c_re = w["c_re"].reshape(n_groups, SSM_GROUP, SSM_STATE)
    c_im = w["c_im"].reshape(n_groups, SSM_GROUP, SSM_STATE)
    wb = jnp.concatenate([blockdiag_in(bbt_re), blockdiag_in(bbt_im)], axis=2).astype(BF16)
    wc = jnp.concatenate([blockdiag_out(c_re), -blockdiag_out(c_im)], axis=1).astype(BF16)
    a_lay = jnp.concatenate([abar_re.reshape(nj, 1, STATE_BLOCK), abar_im.reshape(nj, 1, STATE_BLOCK)],
                            axis=1).reshape(1, nj * sw)

    attn_w = w["attn_norm"]
    hn = _blockwise("norm1", lambda xb, wv: _rms(xb, wv), [x, attn_w], [_row_spec(t_row, d), _full_spec((1, d))],
                    [((seq, d), BF16)], [_row_spec(t_row, d)], g1)[0]
    proj = _mm2d("proj", hn, w["win"], NN, F32, tn=640)

    s_all, ylin = _ssm_fwd(proj, wb, wc, a_lay)
    u_spec = pl.BlockSpec((t_row, ssm_w), lambda i: (i, 0))

    def ypre_fn(yl, ub, dsk):
        yp = yl + dsk * ub
        return yp, jax.nn.gelu(yp)

    y_pre, yg = _blockwise("ssm_gelu", ypre_fn, [ylin, proj, w["ssm_d"]],
                           [_row_spec(t_row, ssm_w), u_spec, _full_spec((1, ssm_w))],
                           [((seq, ssm_w), F32), ((seq, ssm_w), BF16)],
                           [_row_spec(t_row, ssm_w)] * 2, g1)
    z = _mm2d("ssm_glu", yg, w["wglu"], NN, F32, res=w["b_glu"])
    y_ssm = _blockwise("ssm_gate", lambda yp, zb: jax.nn.gelu(yp) * jax.nn.sigmoid(zb), [y_pre, z],
                       [_row_spec(t_row, ssm_w)] * 2, [((seq, ssm_w), F32)], [_row_spec(t_row, ssm_w)], g1)[0]

    cq_off, ckv_off, kpe_off = ssm_w, ssm_w + q_rank, ssm_w + q_rank + kv_rank
    c_q = proj[:, cq_off:ckv_off]
    c_kv = proj[:, ckv_off:kpe_off]
    kpe_raw = proj[:, kpe_off:kpe_off + LANES]
    pos_b = jnp.broadcast_to(positions.astype(F32)[:, None], (seq, LANES))
    inv_freq = ROPE_THETA ** (-jnp.arange(0, QK_ROPE, 2, dtype=F32) / QK_ROPE)
    inv128 = jnp.tile(inv_freq, 4).reshape(1, LANES)

    def mla_prep_fn(cq, ckv, kp, pb, inv, wq, wkv):
        ang = pb * inv
        lane = lax.broadcasted_iota(jnp.int32, ang.shape, 1)
        cs, sn = jnp.cos(ang), jnp.sin(ang)
        cos = jnp.where(lane < QK_ROPE, cs, 0.0)
        sa = jnp.where(lane < QK_ROPE // 2, -sn, 0.0)
        sb = jnp.where(jnp.logical_and(lane >= QK_ROPE // 2, lane < QK_ROPE), sn, 0.0)
        return _rms(cq, wq), _rms(ckv, wkv), _rope128(kp, cos, sa, sb), cos, sa, sb

    qn, kvn, kpe, cos_t, sa_t, sb_t = _blockwise(
        "mla_prep", mla_prep_fn, [c_q, c_kv, kpe_raw, pos_b, inv128, w["q_norm"], w["kv_norm"]],
        [_row_spec(t_row, q_rank), _row_spec(t_row, kv_rank), _row_spec(t_row, LANES), _row_spec(t_row, LANES),
         _full_spec((1, LANES)), _full_spec((1, q_rank)), _full_spec((1, kv_rank))],
        [((seq, q_rank), BF16), ((seq, kv_rank), BF16), ((seq, LANES), BF16)] + [((seq, LANES), F32)] * 3,
        [_row_spec(t_row, q_rank), _row_spec(t_row, kv_rank)] + [_row_spec(t_row, LANES)] * 4, g1)

    def head_mm(name, act, wh, out_dtype):
        kdim, ndim = wh.shape[1], wh.shape[2]
        return _mm(name, act, wh, grid=(nh, 1, 1),
                   a_spec=pl.BlockSpec((seq, kdim), lambda h, i, k: (i, 0)),
                   b_spec=pl.BlockSpec((None, kdim, ndim), lambda h, i, k: (h, 0, 0)),
                   o_spec=pl.BlockSpec((None, seq, ndim), lambda h, i, k: (h, i, 0)),
                   out_shape=(nh, seq, ndim), out_dtype=out_dtype)

    q_raw = head_mm("mla_q", qn, w["wuq"], F32)
    kv = head_mm("mla_kv", kvn, w["wukv"], BF16)
    y_mla = _attn_fwd(q_raw, kv, kpe, cos_t, sa_t, sb_t)
    mla_w = nh * V_DIM

    def outnorm_fn(ys, ym, ws, wm):
        return jnp.concatenate([_rms(ys, ws), _rms(ym, wm)], axis=1)

    ycat = _blockwise("out_norm", outnorm_fn, [y_ssm, y_mla, w["son"], w["mon"]],
                      [_row_spec(t_row, ssm_w), _row_spec(t_row, mla_w), _full_spec((1, ssm_w)), _full_spec((1, mla_w))],
                      [((seq, d), BF16)], [_row_spec(t_row, d)], g1)[0]
    h1 = _mm2d("out_proj", ycat, w["wout"], NN, F32, res=x)

    hn2 = _blockwise("norm2", lambda hb, wv: _rms(hb, wv), [h1, w["ffn_norm"]],
                     [_row_spec(t_row, d), _full_spec((1, d))], [((seq, d), BF16)], [_row_spec(t_row, d)], g1)[0]
    tku = d
    half = ns // 2
    a_ff = _mm("ffn_up", hn2, w["wup"], grid=(ns, nm, d // tku),
               a_spec=pl.BlockSpec((tm, tku), lambda s, i, k: (i, k)),
               b_spec=pl.BlockSpec((None, tku, c_ff), lambda s, i, k: (s, k, 0)),
               o_spec=pl.BlockSpec((None, None, tm, c_ff), lambda s, i, k: (s % half, s // half, i, 0)),
               out_shape=(half, 2, seq, c_ff), out_dtype=F32)
    cb3 = w["conv_b"].reshape(ns, 1, c_ff)
    m_ff = _conv_gate_fwd(a_ff, w["conv_w"], cb3)
    d_ff = half * c_ff
    wdn = w["wdown"]
    tnd = _tile(d, 1024)
    tmx, tnx = min(1024, seq), _tile(d, 1024)
    h2 = _mm2d("ffn_down", m_ff, wdn, NN, F32, tm=512, tn=512, tk=d_ff, res=h1)

    def loss_fn(hb, tb, wv):
        def f(hh, ww):
            err = _rms(hh, ww) - tb
            return 0.5 * jnp.sum(jnp.mean(err * err, axis=-1))

        lossv, (dh, dw) = jax.value_and_grad(f, argnums=(0, 1))(hb, wv)
        return dh, dh, jnp.full((1, LANES), lossv, F32), dw

    fin_w = w["final_norm"].reshape(1, d)
    dh2, dh2b, loss_acc, g_final = _blockwise(
        "loss_head", loss_fn, [h2, target, fin_w], [_row_spec(t_row, d), _row_spec(t_row, d), _full_spec((1, d))],
        [((seq, d), F32), ((seq, d), BF16), ((1, LANES), F32), ((1, d), F32)],
        [_row_spec(t_row, d), _row_spec(t_row, d), _full_spec((1, LANES)), _full_spec((1, d))], g1, n_acc=2)
    loss = loss_acc

    dm = _mm2d("ffn_down_dx", dh2b, wdn, NT, F32, tn=c_ff)
    tks = seq
    g_wdown = _mm2d("ffn_down_dw", m_ff, dh2b, TN, BF16, tm=c_ff)
    emit(wdown=g_wdown)
    da_ff, g_convw2, g_convb2 = _conv_gate_bwd(a_ff, w["conv_w"], cb3, dm)
    g_convw = jnp.swapaxes(g_convw2, 0, 1).reshape(ns, 3, c_ff)
    g_convb = jnp.swapaxes(g_convb2, 0, 1).reshape(ns, 1, c_ff)
    g_wup = _mm("ffn_up_dw", hn2, da_ff, grid=(ns, d // tnd, seq // tks), contract=TN,
                a_spec=pl.BlockSpec((tks, tnd), lambda s, j, k: (k, j)),
                b_spec=pl.BlockSpec((None, None, tks, c_ff), lambda s, j, k: (s % half, s // half, k, 0)),
                o_spec=pl.BlockSpec((None, tnd, c_ff), lambda s, j, k: (s, j, 0)),
                out_shape=(ns, d, c_ff), out_dtype=BF16)
    emit(wup=g_wup)
    dhn2 = _mm("ffn_up_dx", da_ff, w["wup"], grid=(seq // tmx, d // tnx, ns), contract=NT,
               a_spec=pl.BlockSpec((None, None, tmx, c_ff), lambda i, j, s: (s % half, s // half, i, 0)),
               b_spec=pl.BlockSpec((None, tnx, c_ff), lambda i, j, s: (s, j, 0)),
               o_spec=pl.BlockSpec((tmx, tnx), lambda i, j, s: (i, j)),
               out_shape=(seq, d), out_dtype=F32)
    emit(wup_pair_sums_after=dhn2)

    def norm_bwd_fn(hb, dres, dn, wv):
        dx_, dw_ = _rms_bwd(hb, wv, dn)
        dtot = dres + dx_
        return dtot, dtot, dw_

    dh1, dh1b, g_ffn_norm = _blockwise(
        "norm2_bwd", norm_bwd_fn, [h1, dh2, dhn2, w["ffn_norm"]],
        [_row_spec(t_row, d)] * 3 + [_full_spec((1, d))],
        [((seq, d), F32), ((seq, d), BF16), ((1, d), F32)],
        [_row_spec(t_row, d), _row_spec(t_row, d), _full_spec((1, d))], g1, n_acc=1)

    g_wout = _mm2d("out_proj_dw", ycat, dh1b, TN, BF16)

    def outnorm_bwd_fn(dhb, wo, ys, ym, ws, wm):
        dyc = lax.dot_general(dhb, wo, (NT, ((), ())), preferred_element_type=F32)
        dys, dws = _rms_bwd(ys, ws, dyc[:, :ssm_w])
        dym, dwm = _rms_bwd(ym, wm, dyc[:, ssm_w:])
        return dys, dym, dws, dwm

    dy_ssm, dy_mla, g_son, g_mon = _blockwise(
        "out_proj_dx_norm_bwd", outnorm_bwd_fn, [dh1b, w["wout"], y_ssm, y_mla, w["son"], w["mon"]],
        [_row_spec(t_row, d), _full_spec((d, d)), _row_spec(t_row, ssm_w), _row_spec(t_row, mla_w),
         _full_spec((1, ssm_w)), _full_spec((1, mla_w))],
        [((seq, ssm_w), F32), ((seq, mla_w), F32), ((1, ssm_w), F32), ((1, mla_w), F32)],
        [_row_spec(t_row, ssm_w), _row_spec(t_row, mla_w), _full_spec((1, ssm_w)), _full_spec((1, mla_w))],
        g1, n_acc=2)

    def glu_bwd_fn(dy, yp, zb, ub, dsk, wg):
        ygv = jax.nn.gelu(yp)
        sg = jax.nn.sigmoid(zb)
        dz = dy * ygv * sg * (1.0 - sg)
        dzb = dz.astype(BF16)
        dyg = dy * sg + lax.dot_general(dzb, wg, (NT, ((), ())), preferred_element_type=F32)
        _, vjp = jax.vjp(jax.nn.gelu, yp)
        dyp = vjp(dyg)[0]
        return (dzb, dyp, dyp * dsk, jnp.sum(dz, axis=0, keepdims=True), jnp.sum(dyp * ub, axis=0, keepdims=True))

    dz, dy_pre, du1, g_bglu, g_ssmd = _blockwise(
        "ssm_glu_bwd", glu_bwd_fn, [dy_ssm, y_pre, z, proj, w["ssm_d"], w["wglu"]],
        [_row_spec(t_row, ssm_w)] * 3 + [u_spec, _full_spec((1, ssm_w)), _full_spec((ssm_w, ssm_w))],
        [((seq, ssm_w), BF16), ((seq, ssm_w), BF16), ((seq, ssm_w), F32), ((1, ssm_w), F32), ((1, ssm_w), F32)],
        [_row_spec(t_row, ssm_w)] * 3 + [_full_spec((1, ssm_w))] * 2, g1, n_acc=2)
    g_wglu = _mm2d("ssm_glu_dw", yg, dz, TN, BF16)
    dq_raw, dkv, dkp_h = _attn_bwd(q_raw, kv, kpe, cos_t, sa_t, sb_t, dy_mla)

    def head_mm_dx(name, dact, wh):
        kdim, ndim = wh.shape[1], wh.shape[2]
        return _mm(name, dact, wh, grid=(1, 1, nh), contract=NT,
                   a_spec=pl.BlockSpec((None, seq, ndim), lambda i, j, h: (h, i, 0)),
                   b_spec=pl.BlockSpec((None, kdim, ndim), lambda i, j, h: (h, 0, 0)),
                   o_spec=pl.BlockSpec((seq, kdim), lambda i, j, h: (i, 0)),
                   out_shape=(seq, kdim), out_dtype=F32)

    def head_mm_dw(name, act, dact):
        kdim, ndim = act.shape[1], dact.shape[2]
        return _mm(name, act, dact, grid=(nh, 1, seq // tks), contract=TN,
                   a_spec=pl.BlockSpec((tks, kdim), lambda h, j, k: (k, 0)),
                   b_spec=pl.BlockSpec((None, tks, ndim), lambda h, j, k: (h, k, 0)),
                   o_spec=pl.BlockSpec((None, kdim, ndim), lambda h, j, k: (h, 0, 0)),
                   out_shape=(nh, kdim, ndim), out_dtype=BF16)

    g_wuq = head_mm_dw("mla_q_dw", qn, dq_raw)
    g_wukv = head_mm_dw("mla_kv_dw", kvn, dkv)
    dqn = head_mm_dx("mla_q_dx", dq_raw, w["wuq"])
    dkvn = head_mm_dx("mla_kv_dx", dkv, w["wukv"])
    emit(not_before=(dqn, dkvn, dy_pre), wout=g_wout, wuq=g_wuq, wukv=g_wukv, wglu=g_wglu, conv_w=g_convw)

    du, dwb, dwc, da_lay = _ssm_bwd(dy_pre, s_all, proj, du1, wb, wc, a_lay)
    g_c_re = blockdiag_out_t(dwc[:, :STATE_BLOCK, :])
    g_c_im = -blockdiag_out_t(dwc[:, STATE_BLOCK:, :])
    dbbt_re = blockdiag_in_t(dwb[:, :, :STATE_BLOCK])
    dbbt_im = blockdiag_in_t(dwb[:, :, STATE_BLOCK:])
    da3 = da_lay.reshape(nj, 2, STATE_BLOCK)
    dabar_re = da3[:, 0, :].reshape(n_groups, 1, SSM_STATE)
    dabar_im = da3[:, 1, :].reshape(n_groups, 1, SSM_STATE)
    g_lr3, g_li3, g_ldt3, g_bt_re, g_bt_im = _s5_prep_bwd(lr3, li3, ldt3, bt_re, bt_im,
                                                           dabar_re, dabar_im, dbbt_re, dbbt_im)

    def mla_prep_bwd_fn(cq, ckv, dqn_b, dkvn_b, dkp_b, cos, sa, sb, wq, wkv):
        dcq, dwq = _rms_bwd(cq, wq, dqn_b)
        dckv, dwkv = _rms_bwd(ckv, wkv, dkvn_b)
        dkp_sum = dkp_b[0]
        for h in range(1, nh):
            dkp_sum = dkp_sum + dkp_b[h]
        return dcq, dckv, _rope128_t(dkp_sum, cos, sa, sb), dwq, dwkv

    dc_q, dc_kv, dkpe_raw, g_qnorm, g_kvnorm = _blockwise(
        "mla_prep_bwd", mla_prep_bwd_fn, [c_q, c_kv, dqn, dkvn, dkp_h, cos_t, sa_t, sb_t, w["q_norm"], w["kv_norm"]],
        [_row_spec(t_row, q_rank), _row_spec(t_row, kv_rank), _row_spec(t_row, q_rank), _row_spec(t_row, kv_rank),
         pl.BlockSpec((nh, t_row, LANES), lambda i: (0, i, 0))] + [_row_spec(t_row, LANES)] * 3
        + [_full_spec((1, q_rank)), _full_spec((1, kv_rank))],
        [((seq, q_rank), BF16), ((seq, kv_rank), BF16), ((seq, LANES), BF16), ((1, q_rank), F32), ((1, kv_rank), F32)],
        [_row_spec(t_row, q_rank), _row_spec(t_row, kv_rank), _row_spec(t_row, LANES), _full_spec((1, q_rank)),
         _full_spec((1, kv_rank))], g1, n_acc=2)

    dproj = jnp.concatenate([du, dc_q, dc_kv, dkpe_raw], axis=1)
    g_win = _mm2d("proj_dw", hn, dproj, TN, BF16, tn=640)
    emit(win=g_win)
    def norm1_bwd_fn(dpb, wi, xb, dres, wv):
        dn = lax.dot_general(dpb, wi, (NT, ((), ())), preferred_element_type=F32)
        dx_, dw_ = _rms_bwd(xb, wv, dn)
        return dres + dx_, dw_

    grad_x, g_attn_norm = _blockwise(
        "proj_dx_norm1_bwd", norm1_bwd_fn, [dproj, w["win"], x, dh1, attn_w],
        [_row_spec(t_row, in_pad), _full_spec((d, in_pad)), _row_spec(t_row, d), _row_spec(t_row, d), _full_spec((1, d))],
        [((seq, d), F32), ((1, d), F32)], [_row_spec(t_row, d), _full_spec((1, d))], g1, n_acc=1)
    emit(win_pair_sums_after=grad_x)

    grads = dict(
        attn_norm=g_attn_norm, win=g_win, lam_re=g_lr3, lam_im=g_li3, log_dt=g_ldt3,
        bt_re=g_bt_re, bt_im=g_bt_im, c_re=g_c_re, c_im=g_c_im,
        ssm_d=g_ssmd, wglu=g_wglu, b_glu=g_bglu, q_norm=g_qnorm, wuq=g_wuq, kv_norm=g_kvnorm, wukv=g_wukv,
        son=g_son, mon=g_mon, wout=g_wout, ffn_norm=g_ffn_norm, wup=g_wup, conv_w=g_convw, conv_b=g_convb,
        wdown=g_wdown, final_norm=g_final)
    return loss, grad_x, grads


def _mesh_pos():
    return lax.axis_index("x"), lax.axis_index("y"), lax.axis_index("c")


def _handshake_all():
    x, y, c = _mesh_pos()
    barrier = pltpu.get_barrier_semaphore()
    for k in range(1, N_DEV):
        peer = (1 - x if k & 4 else x, 1 - y if k & 2 else y, 1 - c if k & 1 else c)
        pl.semaphore_signal(barrier, inc=1, device_id=peer, device_id_type=MESH)
    pl.semaphore_wait(barrier, N_DEV - 1)


def _handshake(peers):
    barrier = pltpu.get_barrier_semaphore()
    for peer in peers:
        pl.semaphore_signal(barrier, inc=1, device_id=peer, device_id_type=MESH)
    pl.semaphore_wait(barrier, len(peers))


def _comm_call(name, body, n, out_shape, ins, collective_id, after=None, copies=7, n_remote=None, n_local=None):
    n_remote = copies * n if n_remote is None else n_remote
    sems = [pltpu.SemaphoreType.DMA((n_remote,)), pltpu.SemaphoreType.DMA((n_remote,)),
            pltpu.SemaphoreType.DMA((n if n_local is None else n_local,))]
    if collective_id is None:
        any_spec = pl.BlockSpec(memory_space=pl.ANY)
        return pl.pallas_call(body, name=name, out_shape=out_shape, in_specs=[any_spec] * n,
                              out_specs=[any_spec] * n, scratch_shapes=sems)(*ins)
    seq_body = body
    if after:
        n_after = len(after)
        ins = list(ins) + list(after)

        def seq_body(*refs):
            body(*refs[:n], *refs[n + n_after:])

    return pl.kernel(seq_body, name=name, out_type=out_shape,
                     mesh=plsc.ScalarSubcoreMesh(axis_name="seq", num_cores=1), scratch_types=sems,
                     compiler_params=pltpu.CompilerParams(collective_id=collective_id))(*ins)


def _all_gather(name, xs, collective_id=None, after=None, pair_sums=()):
    n = len(xs)
    nh = len(pair_sums)
    m = n + nh

    def body(*refs):
        x_refs, h_refs, o_refs, e_refs = refs[:n], refs[n:m], refs[m:m + n], refs[m + n:2 * m]
        send_sems, recv_sems, local_sems = refs[2 * m:]
        if collective_id is not None:
            _handshake_all()
        finish_pairs = _chip_copies(h_refs, e_refs, send_sems, recv_sems, local_sems, 7 * n, n) if nh else None
        x, y, c = _mesh_pos()
        me, sibling = (x, y, c), (x, y, 1 - c)
        chips = [(1 - x, y), (x, 1 - y), (1 - x, 1 - y)]

        def slot(o_ref, px, py, pc):
            return o_ref.at[4 * px + 2 * py + pc]

        def copy(t, k, block, to, src=None):
            dst = slot(o_refs[t], *block)
            return pltpu.make_async_remote_copy(
                src_ref=dst if src is None else src, dst_ref=dst,
                send_sem=send_sems.at[7 * t + k], recv_sem=recv_sems.at[7 * t + k],
                device_id=to, device_id_type=MESH)

        started = []
        for t in range(n):
            mine = pltpu.make_async_copy(x_refs[t], slot(o_refs[t], *me), local_sems.at[t])
            mine.start()
            started.append(mine)
        first = []
        for t in range(n):
            first.append(copy(t, 0, me, sibling, src=x_refs[t]))
            first += [copy(t, 1 + j, me, (*chip, c), src=x_refs[t]) for j, chip in enumerate(chips)]
        for cp in first:
            cp.start()
        passed = []
        for j, chip in enumerate(chips):
            for t in range(n):
                copy(t, 1 + j, (*chip, c), me).wait_recv()
                fwd = copy(t, 4 + j, (*chip, c), sibling)
                fwd.start()
                passed.append(fwd)
        for t in range(n):
            copy(t, 0, sibling, me).wait_recv()
            for j, chip in enumerate(chips):
                copy(t, 4 + j, (*chip, 1 - c), me).wait_recv()
        for cp in first + passed:
            cp.wait_send()
        for mine in started:
            mine.wait()
        if nh:
            finish_pairs()

    out_shape = ([jax.ShapeDtypeStruct((N_DEV,) + v.shape, v.dtype) for v in xs]
                 + [jax.ShapeDtypeStruct(v.shape, v.dtype) for v in pair_sums])
    return _comm_call(name, body, m, out_shape, list(xs) + list(pair_sums), collective_id, after,
                      n_remote=7 * n + (N_CHIP - 1) * nh, n_local=m)


def _exchange_partials(name, gs, collective_id=None, after=None):
    n = len(gs)

    def body(*refs):
        g_refs, o_refs = refs[:n], refs[n:2 * n]
        send_sems, recv_sems, local_sems = refs[2 * n:]
        if collective_id is not None:
            _handshake_all()
        x, y, c = _mesh_pos()
        me_idx = 4 * x + 2 * y + c
        copies = []
        for t in range(n):
            mine = pltpu.make_async_copy(g_refs[t].at[me_idx], o_refs[t].at[me_idx], local_sems.at[t])
            mine.start()
            copies.append(mine)
        remote = []
        for k in range(1, N_DEV):
            px = 1 - x if k & 4 else x
            py = 1 - y if k & 2 else y
            pc = 1 - c if k & 1 else c
            p_idx = 4 * px + 2 * py + pc
            for t in range(n):
                cp = pltpu.make_async_remote_copy(
                    src_ref=g_refs[t].at[p_idx], dst_ref=o_refs[t].at[me_idx],
                    send_sem=send_sems.at[7 * t + k - 1], recv_sem=recv_sems.at[7 * t + k - 1],
                    device_id=(px, py, pc), device_id_type=MESH)
                cp.start()
                landing = pltpu.make_async_remote_copy(
                    src_ref=g_refs[t].at[p_idx], dst_ref=o_refs[t].at[p_idx],
                    send_sem=send_sems.at[7 * t + k - 1], recv_sem=recv_sems.at[7 * t + k - 1],
                    device_id=(px, py, pc), device_id_type=MESH)
                remote.append((cp, landing))
        for cp, landing in remote:
            landing.wait_recv()
        for cp, landing in remote:
            cp.wait_send()
        for mine in copies:
            mine.wait()

    out_shape = [jax.ShapeDtypeStruct(v.shape, v.dtype) for v in gs]
    return _comm_call(name, body, n, out_shape, gs, collective_id, after)


N_CHIP = N_DEV // 2
PAIR_ADD_BLOCK_ELEMS = 1024 * 1024


def _pair_swap(name, gs, collective_id, after=None):
    n = len(gs)

    def body(*refs):
        g_refs, o_refs = refs[:n], refs[n:2 * n]
        send_sems, recv_sems, _ = refs[2 * n:]
        x, y, c = _mesh_pos()
        sibling = (x, y, 1 - c)
        _handshake([sibling])
        copies = []
        for t in range(n):
            for k in range(N_CHIP):
                copies.append(pltpu.make_async_remote_copy(
                    src_ref=g_refs[t].at[2 * k + 1 - c], dst_ref=o_refs[t].at[k],
                    send_sem=send_sems.at[N_CHIP * t + k], recv_sem=recv_sems.at[N_CHIP * t + k],
                    device_id=sibling, device_id_type=MESH))
        for cp in copies:
            cp.start()
        for cp in copies:
            cp.wait_recv()
        for cp in copies:
            cp.wait_send()

    out_shape = [jax.ShapeDtypeStruct((N_CHIP,) + v.shape[1:], v.dtype) for v in gs]
    return _comm_call(name, body, n, out_shape, gs, collective_id, after, copies=N_CHIP)


def _pair_add(name, g, got):
    _, r, c = g.shape
    tr = r
    if r * c > PAIR_ADD_BLOCK_ELEMS and r % SUBLANES == 0:
        tr = SUBLANES
        while r % (tr * 2) == 0 and tr * 2 * c <= PAIR_ADD_BLOCK_ELEMS:
            tr *= 2

    def body(core_ref, g_ref, got_ref, o_ref):
        o_ref[...] = (g_ref[...].astype(F32) + got_ref[...].astype(F32)).astype(o_ref.dtype)

    grid_spec = pltpu.PrefetchScalarGridSpec(
        num_scalar_prefetch=1, grid=(N_CHIP, r // tr),
        in_specs=[pl.BlockSpec((None, None, tr, c), lambda k, i, core: (k, core[0], i, 0)),
                  pl.BlockSpec((None, tr, c), lambda k, i, core: (k, i, 0))],
        out_specs=pl.BlockSpec((None, tr, c), lambda k, i, core: (k, i, 0)))
    core = lax.axis_index("c").astype(jnp.int32).reshape(1)
    return pl.pallas_call(body, name=name, grid_spec=grid_spec, out_shape=jax.ShapeDtypeStruct((N_CHIP, r, c), g.dtype),
                          compiler_params=_cparams())(core, g.reshape(N_CHIP, 2, r, c), got)


def _chip_copies(h_refs, o_refs, send_sems, recv_sems, local_sems, sem0, local0):
    n = len(h_refs)
    per = N_CHIP - 1
    x, y, c = _mesh_pos()
    others = [(1 - x if k & 2 else x, 1 - y if k & 1 else y) for k in range(1, N_CHIP)]
    my_chip = 2 * x + y
    local = []
    for t in range(n):
        mine = pltpu.make_async_copy(h_refs[t].at[my_chip], o_refs[t].at[my_chip], local_sems.at[local0 + t])
        mine.start()
        local.append(mine)
    remote = []
    for j, (px, py) in enumerate(others):
        chip = 2 * px + py
        for t in range(n):
            sems = dict(send_sem=send_sems.at[sem0 + per * t + j], recv_sem=recv_sems.at[sem0 + per * t + j],
                        device_id=(px, py, c), device_id_type=MESH)
            cp = pltpu.make_async_remote_copy(src_ref=h_refs[t].at[chip], dst_ref=o_refs[t].at[my_chip], **sems)
            cp.start()
            landing = pltpu.make_async_remote_copy(src_ref=h_refs[t].at[chip], dst_ref=o_refs[t].at[chip], **sems)
            remote.append((cp, landing))

    def finish():
        for cp, landing in remote:
            landing.wait_recv()
        for cp, landing in remote:
            cp.wait_send()
        for mine in local:
            mine.wait()

    return finish


def _chip_exchange(name, hs, collective_id, after=None):
    n = len(hs)
    per = N_CHIP - 1

    def body(*refs):
        h_refs, o_refs = refs[:n], refs[n:2 * n]
        send_sems, recv_sems, local_sems = refs[2 * n:]
        x, y, c = _mesh_pos()
        _handshake([(1 - x if k & 2 else x, 1 - y if k & 1 else y, c) for k in range(1, N_CHIP)])
        _chip_copies(h_refs, o_refs, send_sems, recv_sems, local_sems, 0, 0)()

    out_shape = [jax.ShapeDtypeStruct(v.shape, v.dtype) for v in hs]
    return _comm_call(name, body, n, out_shape, hs, collective_id, after, copies=per)


ADAM_BLOCK_ELEMS = 128 * 1024


def _sum_parts(pb):
    g = pb[0].astype(F32)
    for j in range(1, pb.shape[0]):
        g = g + pb[j].astype(F32)
    return g


def _adam_math(g, wb_, mb, vb):
    m_new = ADAM_B1 * mb + (1.0 - ADAM_B1) * g
    v_new = ADAM_B2 * vb + (1.0 - ADAM_B2) * (g * g)
    m_hat = m_new / (1.0 - ADAM_B1 ** ADAM_STEP)
    v_hat = v_new / (1.0 - ADAM_B2 ** ADAM_STEP)
    delta = -ADAM_LR * (m_hat / (jnp.sqrt(v_hat) + ADAM_EPS) + ADAM_WD * wb_)
    return g, delta, m_new, v_new


def _adamw_multi(name, items, nblk=1, packed=None):
    n = len(items)

    def spec(shape, lead):
        blk = list(shape)
        blk[lead + 1] = shape[lead + 1] // nblk
        if nblk == 1:
            return pl.BlockSpec(tuple(blk), lambda i, nd=len(shape): (0,) * nd)
        return pl.BlockSpec(tuple(blk), lambda i, nd=len(shape), ax=lead + 1: (0,) * ax + (i,) + (0,) * (nd - ax - 1))

    ins, in_specs, out_specs, out_shape, where = [], [], [], [], []
    if packed is not None:
        ins.append(packed)
        in_specs.append(spec(packed.shape, 1))
    for parts, wv, mv, vv in items:
        if isinstance(parts, int):
            where.append((0, parts, len(ins)))
        else:
            assert parts.shape[1:] == wv.shape, (name, parts.shape, wv.shape)
            where.append((len(ins), None, len(ins) + 1))
            ins.append(parts)
            in_specs.append(spec(parts.shape, 1))
        ins += [wv, mv, vv]
        in_specs += [spec(wv.shape, 0)] * 3
        out_specs += [spec(wv.shape, 0)] * 4
        out_shape += [jax.ShapeDtypeStruct(wv.shape, F32)] * 4
    n_in = len(ins)

    def body(*refs):
        for t, (ip, off, iw) in enumerate(where):
            wr, mr, vr = refs[iw:iw + 3]
            parts = refs[ip][...] if off is None else refs[ip][:, :, off:off + wr.shape[-1]]
            res = _adam_math(_sum_parts(parts), wr[...], mr[...], vr[...])
            for o, val in zip(refs[n_in + 4 * t:n_in + 4 * t + 4], res):
                o[...] = val

    res = pl.pallas_call(body, name=name, grid=(nblk,), in_specs=in_specs, out_specs=out_specs, out_shape=out_shape,
                         compiler_params=_cparams())(*ins)
    return [tuple(res[4 * t:4 * t + 4]) for t in range(n)]


def _sum_multi(name, parts_list):
    def body(*refs):
        for pr, o in zip(refs[:len(parts_list)], refs[len(parts_list):]):
            o[...] = _sum_parts(pr[...])

    return pl.pallas_call(body, name=name, out_shape=[jax.ShapeDtypeStruct(p.shape[1:], F32) for p in parts_list],
                          compiler_params=_cparams())(*parts_list)


def _adamw_sum(name, parts, wv, mv, vv):
    npart, r, c = parts.shape
    tr = r
    if r * c > ADAM_BLOCK_ELEMS and r % SUBLANES == 0:
        tr = SUBLANES
        while r % (tr * 2) == 0 and tr * 2 * c <= ADAM_BLOCK_ELEMS:
            tr *= 2

    def fn(pb, wb_, mb, vb):
        return _adam_math(_sum_parts(pb), wb_, mb, vb)

    row = pl.BlockSpec((tr, c), lambda i: (i, 0))
    return _blockwise(name, fn, [parts, wv, mv, vv],
                      [pl.BlockSpec((npart, tr, c), lambda i: (0, i, 0)), row, row, row],
                      [((r, c), F32)] * 4, [row] * 4, (r // tr,))


_VECTORS = ["attn_norm", "lam_re", "lam_im", "log_dt", "ssm_d", "b_glu", "q_norm", "kv_norm", "son", "mon",
            "ffn_norm", "conv_b", "final_norm"]
_GHP = ["c_re", "c_im", "bt_re", "bt_im"]
_PACKED = ["attn_norm", "ssm_d", "b_glu", "q_norm", "kv_norm", "son", "mon", "ffn_norm", "conv_b", "final_norm"]
_BIG = ["win", "wglu", "wuq", "wukv", "wout", "wup", "wdown", "conv_w"]
_TWO_LEVEL = ("wup", "win")
_AFTER = "_pair_sums_after"
_ORDER = ["attn_norm", "win", "lam_re", "lam_im", "log_dt", "b_re", "b_im", "c_re", "c_im", "ssm_d", "wglu",
          "b_glu", "q_norm", "wuq", "kv_norm", "wukv", "son", "mon", "wout", "ffn_norm", "wup", "conv_w",
          "conv_b", "wdown", "final_norm"]


def kernel(x, positions, attn_norm_w, w_in, ssm_lambda_re, ssm_lambda_im, ssm_log_dt, ssm_b_re, ssm_b_im, ssm_c_re, ssm_c_im, ssm_d, ssm_w_glu, ssm_b_glu, mla_q_norm_w, mla_w_uq, mla_kv_norm_w, mla_w_ukv, ssm_out_norm_w, mla_out_norm_w, w_out, ffn_norm_w, ffn_w_up, ffn_conv_w, ffn_conv_b, ffn_w_down, final_norm_w, loss_target, m_attn_norm_w, m_w_in, m_ssm_lambda_re, m_ssm_lambda_im, m_ssm_log_dt, m_ssm_b_re, m_ssm_b_im, m_ssm_c_re, m_ssm_c_im, m_ssm_d, m_ssm_w_glu, m_ssm_b_glu, m_mla_q_norm_w, m_mla_w_uq, m_mla_kv_norm_w, m_mla_w_ukv, m_ssm_out_norm_w, m_mla_out_norm_w, m_w_out, m_ffn_norm_w, m_ffn_w_up, m_ffn_conv_w, m_ffn_conv_b, m_ffn_w_down, m_final_norm_w, v_attn_norm_w, v_w_in, v_ssm_lambda_re, v_ssm_lambda_im, v_ssm_log_dt, v_ssm_b_re, v_ssm_b_im, v_ssm_c_re, v_ssm_c_im, v_ssm_d, v_ssm_w_glu, v_ssm_b_glu, v_mla_q_norm_w, v_mla_w_uq, v_mla_kv_norm_w, v_mla_w_ukv, v_ssm_out_norm_w, v_mla_out_norm_w, v_w_out, v_ffn_norm_w, v_ffn_w_up, v_ffn_conv_w, v_ffn_conv_b, v_ffn_w_down, v_final_norm_w):
    wts = dict(attn_norm=attn_norm_w, win=w_in, lam_re=ssm_lambda_re, lam_im=ssm_lambda_im, log_dt=ssm_log_dt,
               b_re=ssm_b_re, b_im=ssm_b_im, c_re=ssm_c_re, c_im=ssm_c_im, ssm_d=ssm_d, wglu=ssm_w_glu,
               b_glu=ssm_b_glu, q_norm=mla_q_norm_w, wuq=mla_w_uq, kv_norm=mla_kv_norm_w, wukv=mla_w_ukv,
               son=ssm_out_norm_w, mon=mla_out_norm_w, wout=w_out, ffn_norm=ffn_norm_w, wup=ffn_w_up,
               conv_w=ffn_conv_w, conv_b=ffn_conv_b, wdown=ffn_w_down, final_norm=final_norm_w)
    moms = dict(zip(_ORDER, [m_attn_norm_w, m_w_in, m_ssm_lambda_re, m_ssm_lambda_im, m_ssm_log_dt, m_ssm_b_re,
                             m_ssm_b_im, m_ssm_c_re, m_ssm_c_im, m_ssm_d, m_ssm_w_glu, m_ssm_b_glu, m_mla_q_norm_w,
                             m_mla_w_uq, m_mla_kv_norm_w, m_mla_w_ukv, m_ssm_out_norm_w, m_mla_out_norm_w, m_w_out,
                             m_ffn_norm_w, m_ffn_w_up, m_ffn_conv_w, m_ffn_conv_b, m_ffn_w_down, m_final_norm_w]))
    vels = dict(zip(_ORDER, [v_attn_norm_w, v_w_in, v_ssm_lambda_re, v_ssm_lambda_im, v_ssm_log_dt, v_ssm_b_re,
                             v_ssm_b_im, v_ssm_c_re, v_ssm_c_im, v_ssm_d, v_ssm_w_glu, v_ssm_b_glu, v_mla_q_norm_w,
                             v_mla_w_uq, v_mla_kv_norm_w, v_mla_w_ukv, v_ssm_out_norm_w, v_mla_out_norm_w, v_w_out,
                             v_ffn_norm_w, v_ffn_w_up, v_ffn_conv_w, v_ffn_conv_b, v_ffn_w_down, v_final_norm_w]))
    seq, d = x.shape[1], x.shape[2]
    in_width = w_in.shape[2]
    in_pad = -(-in_width // LANES) * LANES
    q_cols = mla_w_uq.shape[2]
    q_pad = 2 * LANES

    (win_g,) = _all_gather("gather_w_in", [jnp.pad(w_in[0], ((0, 0), (0, in_pad - in_width))).astype(BF16)])
    wglu_g, wuq_g, wukv_g, wout_g, convw_g = _all_gather(
        "gather_mix", [ssm_w_glu[0].astype(BF16), jnp.pad(mla_w_uq[0], ((0, 0), (0, q_pad - q_cols))).astype(BF16),
                       mla_w_ukv[0].astype(BF16), w_out[0].astype(BF16), ffn_conv_w[0]], collective_id=0)
    (wup_g,) = _all_gather("gather_ffn_up", [ffn_w_up[0].astype(BF16)], collective_id=1)
    (wdown_g,) = _all_gather("gather_ffn_down", [ffn_w_down[0].astype(BF16)], collective_id=2)
    ns = N_DEV
    c_ff = wup_g.shape[2]
    w = dict(
        attn_norm=attn_norm_w, win=win_g.reshape(d, in_pad), lam_re=ssm_lambda_re, lam_im=ssm_lambda_im,
        log_dt=ssm_log_dt, b_re=ssm_b_re, b_im=ssm_b_im, c_re=ssm_c_re, c_im=ssm_c_im, ssm_d=ssm_d,
        wglu=wglu_g.reshape(d // 2, d // 2), b_glu=ssm_b_glu, q_norm=mla_q_norm_w, wuq=wuq_g,
        kv_norm=mla_kv_norm_w, wukv=wukv_g, son=ssm_out_norm_w, mon=mla_out_norm_w, wout=wout_g.reshape(d, d),
        ffn_norm=ffn_norm_w, wup=wup_g, conv_w=convw_g, conv_b=ffn_conv_b,
        wdown=wdown_g.reshape(ns // 2 * c_ff, d), final_norm=final_norm_w)

    shard_layout = dict(
        win=lambda a: a[:, :in_width].reshape(N_DEV, d // N_DEV, in_width),
        wglu=lambda a: a.reshape(N_DEV, d // 2 // N_DEV, d // 2),
        wuq=lambda a: a[:, :, :q_cols], wukv=lambda a: a, wout=lambda a: a.reshape(N_DEV, d // N_DEV, d),
        wup=lambda a: a, wdown=lambda a: a.reshape(N_DEV, c_ff // 2, d), conv_w=lambda a: a)
    recv = {}
    next_id = [3]

    last = [None]

    out = {}

    def update(k):
        shp = wts[k].shape
        r, c = shp[-2], shp[-1]
        res = _adamw_sum("adamw_" + k, recv[k].reshape(-1, r, c), wts[k].reshape(r, c),
                         moms[k].reshape(r, c), vels[k].reshape(r, c))
        out[k] = [a.reshape(shp) for a in res]
        return res[0]

    pending = {}

    def exchange(not_before=(), **grads):
        names = list(grads)
        if len(names) == 1 and names[0] in _TWO_LEVEL:
            k = names[0]
            parts = shard_layout[k](grads[k])
            got = _pair_swap("swap_" + k, [parts], collective_id=next_id[0], after=[last[0]])[0]
            next_id[0] += 1
            pending[k] = (parts, got)
            last[0] = got
            return
        if len(names) == 1 and names[0].endswith(_AFTER):
            k = names[0][:-len(_AFTER)]
            sums = _pair_add("pair_add_" + k, *pending[k])
            if k == "win":
                pending["tail"] = sums
                return
            recv[k] = _chip_exchange("exchange_" + k, [sums], collective_id=next_id[0],
                                     after=[last[0], grads[names[0]]])[0]
            next_id[0] += 1
            last[0] = recv[k]
            return
        got = _exchange_partials("exchange_" + "_".join(names), [shard_layout[k](grads[k]) for k in names],
                                 collective_id=next_id[0], after=[a for a in (last[0], *not_before) if a is not None])
        next_id[0] += 1
        last[0] = got[-1]
        recv.update(zip(names, got))

    loss_part, grad_x, g = _local_step(x[0], positions[0], loss_target[0], w, emit=exchange)
    n_groups = ssm_lambda_re.shape[1]
    two_d = {"lam_re": (n_groups, -1), "lam_im": (n_groups, -1)}
    dense = {k: g[k].reshape(two_d.get(k, (1, -1))) for k in _VECTORS}
    offsets, width = {}, 0
    for k in _PACKED:
        offsets[k] = width
        width += dense[k].shape[1]
    sent = dict(packed=jnp.concatenate([dense[k] for k in _PACKED], axis=1),
                **{k: dense[k] for k in _VECTORS if k not in _PACKED},
                **{k: g[k].reshape(n_groups, -1) for k in _GHP},
                loss=loss_part)
    names = list(sent)
    got = _all_gather("gather_small_grads", [sent[k] for k in names], collective_id=next_id[0], after=[last[0]],
                      pair_sums=[pending["tail"]])
    gathered = dict(zip(names, got))
    recv["win"] = got[len(names)]
    for k in _BIG:
        if k not in out and k != "win":
            update(k)
    update("win")

    def finish(keys, results):
        for k, res in zip(keys, results):
            out[k] = [a.reshape(wts[k].shape) for a in res]

    view = lambda k, a: a.reshape(dense[k].shape)
    finish(_VECTORS, _adamw_multi("adamw_vectors", [(offsets.get(k, gathered.get(k)), view(k, wts[k]), view(k, moms[k]),
                                                     view(k, vels[k])) for k in _VECTORS], packed=gathered["packed"]))
    summed = _GHP + ["loss"]
    sums = dict(zip(summed, _sum_multi("sum_ssm_bc_loss", [gathered[k] for k in summed])))
    loss = sums["loss"][0, 0]
    ghp = lambda k: sums[k].reshape(g[k].shape)
    c_keys = ["c_re", "c_im"]
    finish(c_keys, _adamw_multi("adamw_ssm_c", [(ghp(k)[None, None], wts[k], moms[k], vels[k]) for k in c_keys]))
    b_keys = ["b_re", "b_im"]
    finish(b_keys, _adamw_multi("adamw_ssm_b", [(jnp.swapaxes(ghp(t), 1, 2)[None, None], wts[k], moms[k], vels[k])
                                                for k, t in zip(b_keys, ("bt_re", "bt_im"))], nblk=SUBLANES))

    grad_x = grad_x.reshape(x.shape)
    return (loss, grad_x, *[out[k][0] for k in _ORDER], *[out[k][1] for k in _ORDER],
            *[out[k][2] for k in _ORDER], *[out[k][3] for k in _ORDER])
```

```python
import functools
import math

import jax
import jax.numpy as jnp
from jax import lax
from jax.experimental import pallas as pl
from jax.experimental.pallas import tpu as pltpu
from jax.experimental.pallas import tpu_sc as plsc

F32 = jnp.float32
BF16 = jnp.bfloat16
MESH = pl.DeviceIdType.MESH

N_DEV = 8
LANES = 128
SUBLANES = 8
VMEM_LIMIT = 48 * 1024 * 1024

SSM_GROUP = 16
SSM_STATE = 64
GROUPS_PER_BLOCK = LANES // SSM_GROUP
STATE_BLOCK = GROUPS_PER_BLOCK * SSM_STATE
QK_NOPE = 128
QK_ROPE = 64
V_DIM = 128
ROPE_THETA = 10000.0
RMS_EPS = 1e-6

ADAM_LR = 0.001
ADAM_B1 = 0.9
ADAM_B2 = 0.999
ADAM_EPS = 1e-08
ADAM_WD = 0.01
ADAM_STEP = 10

NN = ((1,), (0,))
NT = ((1,), (1,))
TN = ((0,), (0,))


def _cparams():
    return pltpu.CompilerParams(vmem_limit_bytes=VMEM_LIMIT)


def _tile(n, want):
    if n <= want:
        return n
    t = (want // LANES) * LANES
    while t >= LANES:
        if n % t == 0:
            return t
        t -= LANES
    return n


def _mm(name, a, b, *, grid, a_spec, b_spec, o_spec, out_shape, out_dtype, contract=NN,
        res=None, res_spec=None):
    nk = grid[-1]
    kaxis = len(grid) - 1
    acc_shape = tuple(d for d in o_spec.block_shape if d is not None)

    def body(*refs):
        a_ref, b_ref = refs[:2]
        r_ref = None if res is None else refs[2]
        o_ref = refs[2 if res is None else 3]
        part = lax.dot_general(a_ref[...].astype(BF16), b_ref[...].astype(BF16),
                               (contract, ((), ())), preferred_element_type=F32)
        if nk == 1:
            if r_ref is not None:
                part = part + r_ref[...].astype(F32)
            o_ref[...] = part.astype(o_ref.dtype)
            return
        acc = refs[-1]
        k = pl.program_id(kaxis)

        @pl.when(k == 0)
        def _():
            acc[...] = part

        @pl.when(k != 0)
        def _():
            acc[...] += part

        @pl.when(k == nk - 1)
        def _():
            r = acc[...]
            if r_ref is not None:
                r = r + r_ref[...].astype(F32)
            o_ref[...] = r.astype(o_ref.dtype)

    ins = [a, b] + ([] if res is None else [res])
    in_specs = [a_spec, b_spec] + ([] if res is None else [res_spec])
    return pl.pallas_call(
        body, name=name, grid=grid, in_specs=in_specs, out_specs=o_spec,
        out_shape=jax.ShapeDtypeStruct(out_shape, out_dtype),
        scratch_shapes=[pltpu.VMEM(acc_shape, F32)] if nk > 1 else [], compiler_params=_cparams(),
    )(*ins)


def _mm2d(name, a, b, contract, out_dtype, tm=1024, tn=1024, tk=2048, res=None):
    if contract == NN:
        (m, kk), n = a.shape, b.shape[1]
    elif contract == NT:
        (m, kk), n = a.shape, b.shape[0]
    else:
        (kk, m), n = a.shape, b.shape[1]
    tm, tn, tk = _tile(m, tm), _tile(n, tn), _tile(kk, tk)
    grid = (m // tm, n // tn, kk // tk)
    if contract == TN:
        a_spec = pl.BlockSpec((tk, tm), lambda i, j, k: (k, i))
    else:
        a_spec = pl.BlockSpec((tm, tk), lambda i, j, k: (i, k))
    if contract == NT:
        b_spec = pl.BlockSpec((tn, tk), lambda i, j, k: (j, k))
    else:
        b_spec = pl.BlockSpec((tk, tn), lambda i, j, k: (k, j))
    o_spec = pl.BlockSpec((tm, tn), lambda i, j, k: (i, j))
    res_spec = None
    if res is not None:
        if res.shape[0] == 1:
            res_spec = pl.BlockSpec((1, tn), lambda i, j, k: (0, j))
        else:
            res_spec = pl.BlockSpec((tm, tn), lambda i, j, k: (i, j))
    return _mm(name, a, b, grid=grid, a_spec=a_spec, b_spec=b_spec, o_spec=o_spec,
               out_shape=(m, n), out_dtype=out_dtype, contract=contract, res=res, res_spec=res_spec)


def _blockwise(name, fn, ins, in_specs, outs, out_specs, grid, n_acc=0, acc_all=True):
    n_in, n_out = len(ins), len(outs)
    n_plain = n_out - n_acc

    def body(*refs):
        vals = fn(*[r[...] for r in refs[:n_in]])
        if not isinstance(vals, (tuple, list)):
            vals = (vals,)
        o_refs = refs[n_in:n_in + n_out]
        for r, v in zip(o_refs[:n_plain], vals[:n_plain]):
            r[...] = v.astype(r.dtype)
        if n_acc:
            if acc_all:
                first = functools.reduce(jnp.logical_and, [pl.program_id(d) == 0 for d in range(len(grid))])
            else:
                first = pl.program_id(len(grid) - 1) == 0

            @pl.when(first)
            def _():
                for r, v in zip(o_refs[n_plain:], vals[n_plain:]):
                    r[...] = v.astype(r.dtype)

            @pl.when(jnp.logical_not(first))
            def _():
                for r, v in zip(o_refs[n_plain:], vals[n_plain:]):
                    r[...] += v.astype(r.dtype)

    return pl.pallas_call(
        body, name=name, grid=grid, in_specs=in_specs, out_specs=out_specs,
        out_shape=[jax.ShapeDtypeStruct(s, d) for s, d in outs], compiler_params=_cparams(),
    )(*ins)


def _row_spec(t, c):
    return pl.BlockSpec((t, c), lambda i: (i, 0))


def _full_spec(shape):
    nd = len(shape)
    return pl.BlockSpec(tuple(shape), lambda *g: (0,) * nd)


def _rms(xf, w):
    return xf * lax.rsqrt(jnp.mean(xf * xf, axis=-1, keepdims=True) + RMS_EPS) * w


def _rms_bwd(xf, w, dy):
    _, vjp = jax.vjp(_rms, xf, w)
    return vjp(dy)


def _s5_disc(lr, li, ldt, bre, bim):
    dt = jnp.exp(ldt)
    mag = jnp.exp(lr * dt)
    ar = mag * jnp.cos(li * dt)
    ai = mag * jnp.sin(li * dt)
    nr, ni = ar - 1.0, ai
    den = lr * lr + li * li
    zr = (nr * lr + ni * li) / den
    zi = (ni * lr - nr * li) / den
    return ar, ai, zr * bre - zi * bim, zr * bim + zi * bre


def _s5_prep(lr, li, ldt, bre, bim):
    def body(lr_r, li_r, ldt_r, bre_r, bim_r, ar_r, ai_r, br_r, bi_r):
        ar, ai, br, bi = _s5_disc(lr_r[...], li_r[...], ldt_r[...], bre_r[...], bim_r[...])
        ar_r[...] = ar
        ai_r[...] = ai
        br_r[...] = br
        bi_r[...] = bi

    sd = jax.ShapeDtypeStruct
    return pl.pallas_call(
        body, name="s5_prep",
        out_shape=[sd(lr.shape, F32), sd(lr.shape, F32), sd(bre.shape, F32), sd(bre.shape, F32)],
        compiler_params=_cparams(),
    )(lr, li, ldt, bre, bim)


def _s5_prep_bwd(lr, li, ldt, bre, bim, dar, dai, dbr, dbi):
    def body(lr_r, li_r, ldt_r, bre_r, bim_r, dar_r, dai_r, dbr_r, dbi_r, o0, o1, o2, o3, o4):
        _, vjp = jax.vjp(_s5_disc, lr_r[...], li_r[...], ldt_r[...], bre_r[...], bim_r[...])
        g = vjp((dar_r[...], dai_r[...], dbr_r[...], dbi_r[...]))
        for o, v in zip((o0, o1, o2, o3, o4), g):
            o[...] = v

    sd = jax.ShapeDtypeStruct
    return pl.pallas_call(
        body, name="s5_prep_bwd",
        out_shape=[sd(lr.shape, F32), sd(li.shape, F32), sd(ldt.shape, F32), sd(bre.shape, F32), sd(bim.shape, F32)],
        compiler_params=_cparams(),
    )(lr, li, ldt, bre, bim, dar, dai, dbr, dbi)


SCAN_T = 256


def _scan_tables(ar, ai, tab_r, tab_i, sub, reverse):
    pr, pi = ar, ai
    for k in range(sub):
        row = sub - 1 - k if reverse else k
        tab_r[row:row + 1, :] = pr
        tab_i[row:row + 1, :] = pi
        pr, pi = ar * pr - ai * pi, ar * pi + ai * pr


def _pack_matrix(t_blk, dtype):
    sub = t_blk // SUBLANES
    dst = jnp.arange(t_blk)
    src = (dst % SUBLANES) * sub + dst // SUBLANES
    return (src[:, None] == jnp.arange(t_blk)[None, :]).astype(dtype)


def _permute_rows_f32(pm, x):
    hi = x.astype(BF16)
    r1 = x - hi.astype(F32)
    mid = r1.astype(BF16)
    lo = (r1 - mid.astype(F32)).astype(BF16)
    dot = lambda v: jnp.dot(pm, v, preferred_element_type=F32)
    return dot(hi) + dot(mid) + dot(lo)


def _scan_block(x, loc, ar, ai, st, tab_r, tab_i, sub, reverse):
    hb = STATE_BLOCK
    a8r = jnp.broadcast_to(ar, (SUBLANES, hb))
    a8i = jnp.broadcast_to(ai, (SUBLANES, hb))
    sr = jnp.zeros((SUBLANES, hb), F32)
    si = jnp.zeros((SUBLANES, hb), F32)
    steps = range(sub - 1, -1, -1) if reverse else range(sub)
    for t in steps:
        rows = slice(t * SUBLANES, (t + 1) * SUBLANES)
        sr, si = a8r * sr - a8i * si + x[rows, :hb], a8r * si + a8i * sr + x[rows, hb:]
        loc[rows, :hb] = sr
        loc[rows, hb:] = si
    cr, ci = st[0:1, :], st[1:2, :]
    far = 0 if reverse else sub - 1
    fr, fi = tab_r[far:far + 1, :], tab_i[far:far + 1, :]
    ent_r, ent_i = [None] * SUBLANES, [None] * SUBLANES
    for c in (range(SUBLANES - 1, -1, -1) if reverse else range(SUBLANES)):
        ent_r[c], ent_i[c] = cr, ci
        cr, ci = sr[c:c + 1, :] + (fr * cr - fi * ci), si[c:c + 1, :] + (fr * ci + fi * cr)
    st[0:1, :] = cr
    st[1:2, :] = ci
    c8r = jnp.concatenate(ent_r, axis=0)
    c8i = jnp.concatenate(ent_i, axis=0)
    out = []
    for t in range(sub):
        rows = slice(t * SUBLANES, (t + 1) * SUBLANES)
        tr, ti = tab_r[t:t + 1, :], tab_i[t:t + 1, :]
        out.append(jnp.concatenate([loc[rows, :hb] + (tr * c8r - ti * c8i), loc[rows, hb:] + (tr * c8i + ti * c8r)],
                                   axis=1))
    return jnp.concatenate(out, axis=0)


SSM_BLOCKS_PER_STEP = 2


def _scan_scratch(nblk, t_blk, sub, hb):
    return [pltpu.VMEM((nblk, SUBLANES, hb), F32), pltpu.VMEM((nblk, sub, hb), F32), pltpu.VMEM((nblk, sub, hb), F32),
            pltpu.VMEM((nblk, t_blk, 2 * hb), F32)]


def _ssm_fwd(proj, wb, wc, a):
    seq = proj.shape[0]
    nj = wb.shape[0]
    w2 = 2 * STATE_BLOCK
    hb = STATE_BLOCK
    t_blk = min(SCAN_T, seq)
    sub = t_blk // SUBLANES
    pm = _pack_matrix(t_blk, BF16)

    npair = SSM_BLOCKS_PER_STEP

    def body(u_ref, wb_ref, wc_ref, a_ref, pm_ref, pmt_ref, s_ref, y_ref, st, tab_r, tab_i, loc):
        coef = [(a_ref[:, b * w2:b * w2 + hb], a_ref[:, b * w2 + hb:(b + 1) * w2]) for b in range(npair)]

        @pl.when(pl.program_id(1) == 0)
        def _():
            for b, (ar, ai) in enumerate(coef):
                st[b] = jnp.zeros((SUBLANES, hb), F32)
                _scan_tables(ar, ai, tab_r.at[b], tab_i.at[b], sub, False)

        for b, (ar, ai) in enumerate(coef):
            ub = u_ref[:, b * LANES:(b + 1) * LANES].astype(BF16)
            up = jnp.dot(pm_ref[...], ub, preferred_element_type=F32).astype(BF16)
            bu = jnp.dot(up, wb_ref[b], preferred_element_type=F32)
            s = _scan_block(bu, loc.at[b], ar, ai, st.at[b], tab_r.at[b], tab_i.at[b], sub, False)
            s_ref[:, b * w2:(b + 1) * w2] = s
            yp = jnp.dot(s.astype(BF16), wc_ref[b], preferred_element_type=F32)
            y_ref[:, b * LANES:(b + 1) * LANES] = _permute_rows_f32(pmt_ref[...], yp)

    sd = jax.ShapeDtypeStruct
    return pl.pallas_call(
        body, name="ssm_fwd", grid=(nj // npair, seq // t_blk),
        in_specs=[pl.BlockSpec((t_blk, npair * LANES), lambda j, i: (i, j)),
                  pl.BlockSpec((npair, LANES, w2), lambda j, i: (j, 0, 0)),
                  pl.BlockSpec((npair, w2, LANES), lambda j, i: (j, 0, 0)),
                  pl.BlockSpec((1, npair * w2), lambda j, i: (0, j)),
                  _full_spec((t_blk, t_blk)), _full_spec((t_blk, t_blk))],
        out_specs=[pl.BlockSpec((t_blk, npair * w2), lambda j, i: (i, j)),
                   pl.BlockSpec((t_blk, npair * LANES), lambda j, i: (i, j))],
        out_shape=[sd((seq, nj * w2), F32), sd((seq, nj * LANES), F32)],
        scratch_shapes=_scan_scratch(npair, t_blk, sub, hb), compiler_params=_cparams(),
    )(proj, wb, wc, a, pm, pm.T)


def _ssm_bwd(dy, s, proj, du1, wb, wc, a):
    seq = dy.shape[0]
    nj = wb.shape[0]
    w2 = 2 * STATE_BLOCK
    hb = STATE_BLOCK
    t_blk = min(SCAN_T, seq)
    sub = t_blk // SUBLANES
    nb = seq // t_blk
    pm = _pack_matrix(t_blk, BF16)

    npair = SSM_BLOCKS_PER_STEP

    def body(dy_ref, s_ref, sprev_ref, u_ref, du1_ref, wb_ref, wc_ref, a_ref, pm_ref, pmt_ref,
             du_ref, dwb_ref, dwc_ref, da_ref, st, tab_r, tab_i, loc):
        ib = pl.program_id(1)
        pmv = pm_ref[...]
        coef = [(a_ref[:, b * w2:b * w2 + hb], -a_ref[:, b * w2 + hb:(b + 1) * w2]) for b in range(npair)]

        @pl.when(ib == 0)
        def _():
            for b, (ar, ai) in enumerate(coef):
                st[b] = jnp.zeros((SUBLANES, hb), F32)
                _scan_tables(ar, ai, tab_r.at[b], tab_i.at[b], sub, True)

        sums = []
        for b, (ar, ai) in enumerate(coef):
            cols, wide = slice(b * LANES, (b + 1) * LANES), slice(b * w2, (b + 1) * w2)
            dyp = jnp.dot(pmv, dy_ref[:, cols], preferred_element_type=F32).astype(BF16)
            up = jnp.dot(pmv, u_ref[:, cols].astype(BF16), preferred_element_type=F32).astype(BF16)
            ds = lax.dot_general(dyp, wc_ref[b], (NT, ((), ())), preferred_element_type=F32)
            lam = _scan_block(ds, loc.at[b], ar, ai, st.at[b], tab_r.at[b], tab_i.at[b], sub, True)
            lamb = lam.astype(BF16)
            du = lax.dot_general(lamb, wb_ref[b], (NT, ((), ())), preferred_element_type=F32)
            du_ref[:, cols] = (_permute_rows_f32(pmt_ref[...], du) + du1_ref[:, cols]).astype(du_ref.dtype)
            sv = s_ref[:, wide]
            dwb = lax.dot_general(up, lamb, (TN, ((), ())), preferred_element_type=F32)
            dwc = lax.dot_general(sv.astype(BF16), dyp, (TN, ((), ())), preferred_element_type=F32)

            prev_last = sprev_ref[SUBLANES - 1:SUBLANES, wide]
            prev_last = jnp.where(ib == nb - 1, jnp.zeros_like(prev_last), prev_last)
            tail = sv[t_blk - SUBLANES:, :]
            sl = lax.broadcasted_iota(jnp.int32, tail.shape, 0)
            head = jnp.where(sl >= 1, pltpu.roll(tail, 1, 0), prev_last)
            s_sh = jnp.concatenate([head, sv[:t_blk - SUBLANES, :]], axis=0)
            lam_r, lam_i = lam[:, :hb], lam[:, hb:]
            sr_, si_ = s_sh[:, :hb], s_sh[:, hb:]
            dar = jnp.sum(lam_r * sr_ + lam_i * si_, axis=0, keepdims=True)
            dai = jnp.sum(lam_i * sr_ - lam_r * si_, axis=0, keepdims=True)
            sums.append((wide, jnp.concatenate([dar, dai], axis=1), dwb, dwc))

        @pl.when(ib == 0)
        def _():
            for b, (wide, contrib, dwb, dwc) in enumerate(sums):
                da_ref[:, wide] = contrib
                dwb_ref[b] = dwb
                dwc_ref[b] = dwc

        @pl.when(ib != 0)
        def _():
            for b, (wide, contrib, dwb, dwc) in enumerate(sums):
                da_ref[:, wide] += contrib
                dwb_ref[b] += dwb
                dwc_ref[b] += dwc

    blk = lambda j, i: (nb - 1 - i, j)
    prev_blk = lambda j, i: (jnp.maximum((nb - 1 - i) * sub - 1, 0), j)
    sd = jax.ShapeDtypeStruct
    return pl.pallas_call(
        body, name="ssm_bwd", grid=(nj // npair, nb),
        in_specs=[pl.BlockSpec((t_blk, npair * LANES), blk), pl.BlockSpec((t_blk, npair * w2), blk),
                  pl.BlockSpec((SUBLANES, npair * w2), prev_blk), pl.BlockSpec((t_blk, npair * LANES), blk),
                  pl.BlockSpec((t_blk, npair * LANES), blk),
                  pl.BlockSpec((npair, LANES, w2), lambda j, i: (j, 0, 0)),
                  pl.BlockSpec((npair, w2, LANES), lambda j, i: (j, 0, 0)),
                  pl.BlockSpec((1, npair * w2), lambda j, i: (0, j)),
                  _full_spec((t_blk, t_blk)), _full_spec((t_blk, t_blk))],
        out_specs=[pl.BlockSpec((t_blk, npair * LANES), blk),
                   pl.BlockSpec((npair, LANES, w2), lambda j, i: (j, 0, 0)),
                   pl.BlockSpec((npair, w2, LANES), lambda j, i: (j, 0, 0)),
                   pl.BlockSpec((1, npair * w2), lambda j, i: (0, j))],
        out_shape=[sd((seq, nj * LANES), BF16), sd((nj, LANES, w2), F32), sd((nj, w2, LANES), F32),
                   sd((1, nj * w2), F32)],
        scratch_shapes=_scan_scratch(npair, t_blk, sub, hb), compiler_params=_cparams(),
    )(dy, s, s, proj, du1, wb, wc, a, pm, pm.T)


def _rope128(x, cos, sa, sb):
    return x * cos + pltpu.roll(x, 96, 1) * sa + pltpu.roll(x, 32, 1) * sb


def _rope128_t(dy, cos, sa, sb):
    return dy * cos + pltpu.roll(dy * sa, 32, 1) + pltpu.roll(dy * sb, 96, 1)


ATT_BQ = 256


def _probs(qn, qp, kn, kp, r0, scale):
    s = lax.dot_general(qn, kn, (NT, ((), ())), preferred_element_type=F32)
    s = s + lax.dot_general(qp, kp, (NT, ((), ())), preferred_element_type=F32)
    s = s * scale
    diag = s[:, r0:]
    row = lax.broadcasted_iota(jnp.int32, diag.shape, 0)
    col = lax.broadcasted_iota(jnp.int32, diag.shape, 1)
    diag = jnp.where(col <= row, diag, jnp.finfo(F32).min)
    s = diag if r0 == 0 else jnp.concatenate([s[:, :r0], diag], axis=1)
    m = jnp.max(s, axis=-1, keepdims=True)
    e = jnp.exp(s - m)
    return e / jnp.sum(e, axis=-1, keepdims=True)


def _attn_specs(seq):
    tab = pl.BlockSpec((seq, LANES), lambda h: (0, 0))
    return [pl.BlockSpec((None, seq, 256), lambda h: (h, 0, 0)), pl.BlockSpec((None, seq, 128), lambda h: (h, 0, 0)),
            pl.BlockSpec((None, seq, 128), lambda h: (h, 0, 1)), tab, tab, tab, tab]


def _attn_fwd(q_raw, kv, kpe, cos, sa, sb):
    nh, seq, _ = q_raw.shape
    bq = min(ATT_BQ, seq)
    scale = (QK_NOPE + QK_ROPE) ** -0.5

    def body(q_ref, kn_ref, v_ref, kp_ref, cos_ref, sa_ref, sb_ref, o_ref):
        for r0 in range(0, seq, bq):
            rows, kend = pl.ds(r0, bq), r0 + bq
            qn = q_ref[rows, :QK_NOPE].astype(BF16)
            qp = _rope128(q_ref[rows, QK_NOPE:], cos_ref[rows, :], sa_ref[rows, :], sb_ref[rows, :]).astype(BF16)
            p = _probs(qn, qp, kn_ref[:kend, :], kp_ref[:kend, :], r0, scale)
            o_ref[rows, :] = jnp.dot(p.astype(BF16), v_ref[:kend, :], preferred_element_type=F32)

    return pl.pallas_call(
        body, name="attn_fwd", grid=(nh,), in_specs=_attn_specs(seq),
        out_specs=pl.BlockSpec((seq, V_DIM), lambda h: (0, h)),
        out_shape=jax.ShapeDtypeStruct((seq, nh * V_DIM), F32), compiler_params=_cparams(),
    )(q_raw, kv, kv, kpe, cos, sa, sb)


def _attn_bwd(q_raw, kv, kpe, cos, sa, sb, do):
    nh, seq, _ = q_raw.shape
    bq = min(ATT_BQ, seq)
    scale = (QK_NOPE + QK_ROPE) ** -0.5

    def body(q_ref, kn_ref, v_ref, kp_ref, cos_ref, sa_ref, sb_ref, do_ref, dq_ref, dkv_ref, dkp_ref):
        dkv_ref[...] = jnp.zeros_like(dkv_ref)
        dkp_ref[...] = jnp.zeros_like(dkp_ref)
        for r0 in range(0, seq, bq):
            rows, kend = pl.ds(r0, bq), r0 + bq
            cos_b, sa_b, sb_b = cos_ref[rows, :], sa_ref[rows, :], sb_ref[rows, :]
            qn = q_ref[rows, :QK_NOPE].astype(BF16)
            qp = _rope128(q_ref[rows, QK_NOPE:], cos_b, sa_b, sb_b).astype(BF16)
            kn, v, kp = kn_ref[:kend, :], v_ref[:kend, :], kp_ref[:kend, :]
            p = _probs(qn, qp, kn, kp, r0, scale)
            dob = do_ref[rows, :].astype(BF16)
            dp = lax.dot_general(dob, v, (NT, ((), ())), preferred_element_type=F32)
            ds = p * (dp - jnp.sum(p * dp, axis=-1, keepdims=True)) * scale
            dsb = ds.astype(BF16)
            pb = p.astype(BF16)
            dq_ref[rows, :QK_NOPE] = jnp.dot(dsb, kn, preferred_element_type=F32).astype(dq_ref.dtype)
            dqp = jnp.dot(dsb, kp, preferred_element_type=F32)
            dq_ref[rows, QK_NOPE:] = _rope128_t(dqp, cos_b, sa_b, sb_b).astype(dq_ref.dtype)
            dkv_ref[:kend, :QK_NOPE] += lax.dot_general(dsb, qn, (TN, ((), ())), preferred_element_type=F32)
            dkv_ref[:kend, QK_NOPE:] += lax.dot_general(pb, dob, (TN, ((), ())), preferred_element_type=F32)
            dkp_ref[:kend, :] += lax.dot_general(dsb, qp, (TN, ((), ())), preferred_element_type=F32)

    sd = jax.ShapeDtypeStruct
    return pl.pallas_call(
        body, name="attn_bwd", grid=(nh,),
        in_specs=_attn_specs(seq) + [pl.BlockSpec((seq, V_DIM), lambda h: (0, h))],
        out_specs=[pl.BlockSpec((None, seq, 256), lambda h: (h, 0, 0)),
                   pl.BlockSpec((None, seq, 256), lambda h: (h, 0, 0)),
                   pl.BlockSpec((None, seq, 128), lambda h: (h, 0, 0))],
        out_shape=[sd((nh, seq, 256), BF16), sd((nh, seq, 256), F32), sd((nh, seq, 128), F32)],
        compiler_params=_cparams(),
    )(q_raw, kv, kv, kpe, cos, sa, sb, do)


def _shift_rows(a, k):
    seq = a.shape[0]
    r = pltpu.roll(a, k % seq, 0)
    rows = lax.broadcasted_iota(jnp.int32, (SUBLANES, a.shape[1]), 0)
    if k > 0:
        return jnp.concatenate([jnp.where(rows >= k, r[:SUBLANES], 0.0), r[SUBLANES:]], axis=0)
    return jnp.concatenate([r[:seq - SUBLANES], jnp.where(rows < SUBLANES + k, r[seq - SUBLANES:], 0.0)], axis=0)


def _conv3(a, w, b):
    a1 = _shift_rows(a, 1)
    a2 = _shift_rows(a, 2)
    return w[2:3] * a + w[1:2] * a1 + w[0:1] * a2 + b, a1, a2


def _conv_gate_fwd(a, cw, cb):
    half, _, seq, c = a.shape
    nc = c // LANES

    def fn(pair, wg, wv, bg, bv):
        gc, _, _ = _conv3(pair[0], wg, bg)
        vc, _, _ = _conv3(pair[1], wv, bv)
        return gc * jax.nn.sigmoid(gc) * vc

    def w_spec(off, r):
        return pl.BlockSpec((None, r, LANES), lambda k, j: (k + off, 0, j))

    return _blockwise(
        "conv_gate_fwd", fn, [a, cw, cw, cb, cb],
        [pl.BlockSpec((None, 2, seq, LANES), lambda k, j: (k, 0, 0, j)),
         w_spec(0, 3), w_spec(half, 3), w_spec(0, 1), w_spec(half, 1)],
        [((seq, half * c), BF16)], [pl.BlockSpec((seq, LANES), lambda k, j: (0, k * nc + j))],
        grid=(half, nc))[0]


def _conv_gate_bwd(a, cw, cb, dm):
    half, _, seq, c = a.shape
    nc = c // LANES

    def body(a_ref, wg_ref, wv_ref, bg_ref, bv_ref, dm_ref, da_ref, dw_ref, db_ref):
        dmv = dm_ref[...]
        ga, wg = a_ref[0], wg_ref[...]
        va, wv = a_ref[1], wv_ref[...]
        gc, g1, g2 = _conv3(ga, wg, bg_ref[...])
        vc, v1, v2 = _conv3(va, wv, bv_ref[...])
        sg = jax.nn.sigmoid(gc)
        dms = dmv * sg
        d_val = dms * gc
        d_gate = dms * vc * (1.0 + gc * (1.0 - sg))

        def back(r, dc, own, a1, a2, w):
            up1 = _shift_rows(dc, -1)
            up2 = _shift_rows(dc, -2)
            da_ref[r] = (w[2:3] * dc + w[1:2] * up1 + w[0:1] * up2).astype(da_ref.dtype)
            dw_ref[r, 0:1, :] = jnp.sum(dc * a2, axis=0, keepdims=True)
            dw_ref[r, 1:2, :] = jnp.sum(dc * a1, axis=0, keepdims=True)
            dw_ref[r, 2:3, :] = jnp.sum(dc * own, axis=0, keepdims=True)
            db_ref[r] = jnp.sum(dc, axis=0, keepdims=True)

        back(0, d_gate, ga, g1, g2, wg)
        back(1, d_val, va, v1, v2, wv)

    def w_spec(off, r):
        return pl.BlockSpec((None, r, LANES), lambda k, j: (k + off, 0, j))

    def pair_spec(r):
        return pl.BlockSpec((None, 2, r, LANES), lambda k, j: (k, 0, 0, j))

    sd = jax.ShapeDtypeStruct
    return pl.pallas_call(
        body, name="conv_gate_bwd", grid=(half, nc),
        in_specs=[pair_spec(seq), w_spec(0, 3), w_spec(half, 3), w_spec(0, 1), w_spec(half, 1),
                  pl.BlockSpec((seq, LANES), lambda k, j: (0, k * nc + j))],
        out_specs=[pair_spec(seq), pair_spec(3), pair_spec(1)],
        out_shape=[sd((half, 2, seq, c), BF16), sd((half, 2, 3, c), F32), sd((half, 2, 1, c), F32)],
        compiler_params=_cparams(),
    )(a, cw, cw, cb, cb, dm)


ROW_T = 256


def _local_step(x, positions, target, w, emit=lambda **grads: None):
    seq, d = x.shape
    t_row = min(ROW_T, seq)
    nrow = seq // t_row
    ssm_w = d // 2
    nj = ssm_w // LANES
    n_groups = ssm_w // SSM_GROUP
    nh = w["wuq"].shape[0]
    q_rank = w["wuq"].shape[1]
    kv_rank = w["wukv"].shape[1]
    ns = w["wup"].shape[0]
    c_ff = w["wup"].shape[2]
    in_pad = w["win"].shape[1]
    tm = min(1024, seq)
    nm = seq // tm
    sw = 2 * STATE_BLOCK
    g1 = (nrow,)

    lr3 = w["lam_re"].reshape(n_groups, 1, SSM_STATE)
    li3 = w["lam_im"].reshape(n_groups, 1, SSM_STATE)
    ldt3 = w["log_dt"].reshape(n_groups, 1, 1)
    bt_re = jnp.swapaxes(w["b_re"].reshape(n_groups, SSM_STATE, SSM_GROUP), 1, 2)
    bt_im = jnp.swapaxes(w["b_im"].reshape(n_groups, SSM_STATE, SSM_GROUP), 1, 2)
    abar_re, abar_im, bbt_re, bbt_im = _s5_prep(lr3, li3, ldt3, bt_re, bt_im)
    eye = jnp.eye(GROUPS_PER_BLOCK, dtype=F32)

    def blockdiag_in(bb):
        t = bb.reshape(nj, GROUPS_PER_BLOCK, SSM_GROUP, SSM_STATE)
        return jnp.einsum("jghp,gk->jghkp", t, eye).reshape(nj, LANES, STATE_BLOCK)

    def blockdiag_in_t(dwb):
        t = dwb.reshape(nj, GROUPS_PER_BLOCK, SSM_GROUP, GROUPS_PER_BLOCK, SSM_STATE)
        return jnp.einsum("jghkp,gk->jghp", t, eye).reshape(n_groups, SSM_GROUP, SSM_STATE)

    def blockdiag_out(cc):
        t = cc.reshape(nj, GROUPS_PER_BLOCK, SSM_GROUP, SSM_STATE)
        return jnp.einsum("jghp,gk->jkpgh", t, eye).reshape(nj, STATE_BLOCK, LANES)

    def blockdiag_out_t(dwc):
        t = dwc.reshape(nj, GROUPS_PER_BLOCK, SSM_STATE, GROUPS_PER_BLOCK, SSM_GROUP)
        return jnp.einsum("jkpgh,gk->jghp", t, eye).reshape(n_groups, SSM_GROUP, SSM_STATE)

    c_re = w["c_re"].reshape(n_groups, SSM_GROUP, SSM_STATE)
    c_im = w["c_im"].reshape(n_groups, SSM_GROUP, SSM_STATE)
    wb = jnp.concatenate([blockdiag_in(bbt_re), blockdiag_in(bbt_im)], axis=2).astype(BF16)
    wc = jnp.concatenate([blockdiag_out(c_re), -blockdiag_out(c_im)], axis=1).astype(BF16)
    a_lay = jnp.concatenate([abar_re.reshape(nj, 1, STATE_BLOCK), abar_im.reshape(nj, 1, STATE_BLOCK)],
                            axis=1).reshape(1, nj * sw)

    attn_w = w["attn_norm"]
    hn = _blockwise("norm1", lambda xb, wv: _rms(xb, wv), [x, attn_w], [_row_spec(t_row, d), _full_spec((1, d))],
                    [((seq, d), BF16)], [_row_spec(t_row, d)], g1)[0]
    proj = _mm2d("proj", hn, w["win"], NN, F32, tn=640)

    s_all, ylin = _ssm_fwd(proj, wb, wc, a_lay)
    u_spec = pl.BlockSpec((t_row, ssm_w), lambda i: (i, 0))

    def ypre_fn(yl, ub, dsk):
        yp = yl + dsk * ub
        return yp, jax.nn.gelu(yp)

    y_pre, yg = _blockwise("ssm_gelu", ypre_fn, [ylin, proj, w["ssm_d"]],
                           [_row_spec(t_row, ssm_w), u_spec, _full_spec((1, ssm_w))],
                           [((seq, ssm_w), F32), ((seq, ssm_w), BF16)],
                           [_row_spec(t_row, ssm_w)] * 2, g1)
    z = _mm2d("ssm_glu", yg, w["wglu"], NN, F32, res=w["b_glu"])
    y_ssm = _blockwise("ssm_gate", lambda yp, zb: jax.nn.gelu(yp) * jax.nn.sigmoid(zb), [y_pre, z],
                       [_row_spec(t_row, ssm_w)] * 2, [((seq, ssm_w), F32)], [_row_spec(t_row, ssm_w)], g1)[0]

    cq_off, ckv_off, kpe_off = ssm_w, ssm_w + q_rank, ssm_w + q_rank + kv_rank
    c_q = proj[:, cq_off:ckv_off]
    c_kv = proj[:, ckv_off:kpe_off]
    kpe_raw = proj[:, kpe_off:kpe_off + LANES]
    pos_b = jnp.broadcast_to(positions.astype(F32)[:, None], (seq, LANES))
    inv_freq = ROPE_THETA ** (-jnp.arange(0, QK_ROPE, 2, dtype=F32) / QK_ROPE)
    inv128 = jnp.tile(inv_freq, 4).reshape(1, LANES)

    def mla_prep_fn(cq, ckv, kp, pb, inv, wq, wkv):
        ang = pb * inv
        lane = lax.broadcasted_iota(jnp.int32, ang.shape, 1)
        cs, sn = jnp.cos(ang), jnp.sin(ang)
        cos = jnp.where(lane < QK_ROPE, cs, 0.0)
        sa = jnp.where(lane < QK_ROPE // 2, -sn, 0.0)
        sb = jnp.where(jnp.logical_and(lane >= QK_ROPE // 2, lane < QK_ROPE), sn, 0.0)
        return _rms(cq, wq), _rms(ckv, wkv), _rope128(kp, cos, sa, sb), cos, sa, sb

    qn, kvn, kpe, cos_t, sa_t, sb_t = _blockwise(
        "mla_prep", mla_prep_fn, [c_q, c_kv, kpe_raw, pos_b, inv128, w["q_norm"], w["kv_norm"]],
        [_row_spec(t_row, q_rank), _row_spec(t_row, kv_rank), _row_spec(t_row, LANES), _row_spec(t_row, LANES),
         _full_spec((1, LANES)), _full_spec((1, q_rank)), _full_spec((1, kv_rank))],
        [((seq, q_rank), BF16), ((seq, kv_rank), BF16), ((seq, LANES), BF16)] + [((seq, LANES), F32)] * 3,
        [_row_spec(t_row, q_rank), _row_spec(t_row, kv_rank)] + [_row_spec(t_row, LANES)] * 4, g1)

    def head_mm(name, act, wh, out_dtype):
        kdim, ndim = wh.shape[1], wh.shape[2]
        return _mm(name, act, wh, grid=(nh, 1, 1),
                   a_spec=pl.BlockSpec((seq, kdim), lambda h, i, k: (i, 0)),
                   b_spec=pl.BlockSpec((None, kdim, ndim), lambda h, i, k: (h, 0, 0)),
                   o_spec=pl.BlockSpec((None, seq, ndim), lambda h, i, k: (h, i, 0)),
                   out_shape=(nh, seq, ndim), out_dtype=out_dtype)

    q_raw = head_mm("mla_q", qn, w["wuq"], F32)
    kv = head_mm("mla_kv", kvn, w["wukv"], BF16)
    y_mla = _attn_fwd(q_raw, kv, kpe, cos_t, sa_t, sb_t)
    mla_w = nh * V_DIM

    def outnorm_fn(ys, ym, ws, wm):
        return jnp.concatenate([_rms(ys, ws), _rms(ym, wm)], axis=1)

    ycat = _blockwise("out_norm", outnorm_fn, [y_ssm, y_mla, w["son"], w["mon"]],
                      [_row_spec(t_row, ssm_w), _row_spec(t_row, mla_w), _full_spec((1, ssm_w)), _full_spec((1, mla_w))],
                      [((seq, d), BF16)], [_row_spec(t_row, d)], g1)[0]
    h1 = _mm2d("out_proj", ycat, w["wout"], NN, F32, res=x)

    hn2 = _blockwise("norm2", lambda hb, wv: _rms(hb, wv), [h1, w["ffn_norm"]],
                     [_row_spec(t_row, d), _full_spec((1, d))], [((seq, d), BF16)], [_row_spec(t_row, d)], g1)[0]
    tku = d
    half = ns // 2
    a_ff = _mm("ffn_up", hn2, w["wup"], grid=(ns, nm, d // tku),
               a_spec=pl.BlockSpec((tm, tku), lambda s, i, k: (i, k)),
               b_spec=pl.BlockSpec((None, tku, c_ff), lambda s, i, k: (s, k, 0)),
               o_spec=pl.BlockSpec((None, None, tm, c_ff), lambda s, i, k: (s % half, s // half, i, 0)),
               out_shape=(half, 2, seq, c_ff), out_dtype=F32)
    cb3 = w["conv_b"].reshape(ns, 1, c_ff)
    m_ff = _conv_gate_fwd(a_ff, w["conv_w"], cb3)
    d_ff = half * c_ff
    wdn = w["wdown"]
    tnd = _tile(d, 1024)
    tmx, tnx = min(1024, seq), _tile(d, 1024)
    h2 = _mm2d("ffn_down", m_ff, wdn, NN, F32, tm=512, tn=512, tk=d_ff, res=h1)

    def loss_fn(hb, tb, wv):
        def f(hh, ww):
            err = _rms(hh, ww) - tb
            return 0.5 * jnp.sum(jnp.mean(err * err, axis=-1))

        lossv, (dh, dw) = jax.value_and_grad(f, argnums=(0, 1))(hb, wv)
        return dh, dh, jnp.full((1, LANES), lossv, F32), dw

    fin_w = w["final_norm"].reshape(1, d)
    dh2, dh2b, loss_acc, g_final = _blockwise(
        "loss_head", loss_fn, [h2, target, fin_w], [_row_spec(t_row, d), _row_spec(t_row, d), _full_spec((1, d))],
        [((seq, d), F32), ((seq, d), BF16), ((1, LANES), F32), ((1, d), F32)],
        [_row_spec(t_row, d), _row_spec(t_row, d), _full_spec((1, LANES)), _full_spec((1, d))], g1, n_acc=2)
    loss = loss_acc

    dm = _mm2d("ffn_down_dx", dh2b, wdn, NT, F32, tn=c_ff)
    tks = seq
    g_wdown = _mm2d("ffn_down_dw", m_ff, dh2b, TN, BF16, tm=c_ff)
    emit(wdown=g_wdown)
    da_ff, g_convw2, g_convb2 = _conv_gate_bwd(a_ff, w["conv_w"], cb3, dm)
    g_convw = jnp.swapaxes(g_convw2, 0, 1).reshape(ns, 3, c_ff)
    g_convb = jnp.swapaxes(g_convb2, 0, 1).reshape(ns, 1, c_ff)
    g_wup = _mm("ffn_up_dw", hn2, da_ff, grid=(ns, d // tnd, seq // tks), contract=TN,
                a_spec=pl.BlockSpec((tks, tnd), lambda s, j, k: (k, j)),
                b_spec=pl.BlockSpec((None, None, tks, c_ff), lambda s, j, k: (s % half, s // half, k, 0)),
                o_spec=pl.BlockSpec((None, tnd, c_ff), lambda s, j, k: (s, j, 0)),
                out_shape=(ns, d, c_ff), out_dtype=BF16)
    emit(wup=g_wup)
    dhn2 = _mm("ffn_up_dx", da_ff, w["wup"], grid=(seq // tmx, d // tnx, ns), contract=NT,
               a_spec=pl.BlockSpec((None, None, tmx, c_ff), lambda i, j, s: (s % half, s // half, i, 0)),
               b_spec=pl.BlockSpec((None, tnx, c_ff), lambda i, j, s: (s, j, 0)),
               o_spec=pl.BlockSpec((tmx, tnx), lambda i, j, s: (i, j)),
               out_shape=(seq, d), out_dtype=F32)
    emit(wup_pair_sums_after=dhn2)

    def norm_bwd_fn(hb, dres, dn, wv):
        dx_, dw_ = _rms_bwd(hb, wv, dn)
        dtot = dres + dx_
        return dtot, dtot, dw_

    dh1, dh1b, g_ffn_norm = _blockwise(
        "norm2_bwd", norm_bwd_fn, [h1, dh2, dhn2, w["ffn_norm"]],
        [_row_spec(t_row, d)] * 3 + [_full_spec((1, d))],
        [((seq, d), F32), ((seq, d), BF16), ((1, d), F32)],
        [_row_spec(t_row, d), _row_spec(t_row, d), _full_spec((1, d))], g1, n_acc=1)

    g_wout = _mm2d("out_proj_dw", ycat, dh1b, TN, BF16)

    def outnorm_bwd_fn(dhb, wo, ys, ym, ws, wm):
        dyc = lax.dot_general(dhb, wo, (NT, ((), ())), preferred_element_type=F32)
        dys, dws = _rms_bwd(ys, ws, dyc[:, :ssm_w])
        dym, dwm = _rms_bwd(ym, wm, dyc[:, ssm_w:])
        return dys, dym, dws, dwm

    dy_ssm, dy_mla, g_son, g_mon = _blockwise(
        "out_proj_dx_norm_bwd", outnorm_bwd_fn, [dh1b, w["wout"], y_ssm, y_mla, w["son"], w["mon"]],
        [_row_spec(t_row, d), _full_spec((d, d)), _row_spec(t_row, ssm_w), _row_spec(t_row, mla_w),
         _full_spec((1, ssm_w)), _full_spec((1, mla_w))],
        [((seq, ssm_w), F32), ((seq, mla_w), F32), ((1, ssm_w), F32), ((1, mla_w), F32)],
        [_row_spec(t_row, ssm_w), _row_spec(t_row, mla_w), _full_spec((1, ssm_w)), _full_spec((1, mla_w))],
        g1, n_acc=2)

    def glu_bwd_fn(dy, yp, zb, ub, dsk, wg):
        ygv = jax.nn.gelu(yp)
        sg = jax.nn.sigmoid(zb)
        dz = dy * ygv * sg * (1.0 - sg)
        dzb = dz.astype(BF16)
        dyg = dy * sg + lax.dot_general(dzb, wg, (NT, ((), ())), preferred_element_type=F32)
        _, vjp = jax.vjp(jax.nn.gelu, yp)
        dyp = vjp(dyg)[0]
        return (dzb, dyp, dyp * dsk, jnp.sum(dz, axis=0, keepdims=True), jnp.sum(dyp * ub, axis=0, keepdims=True))

    dz, dy_pre, du1, g_bglu, g_ssmd = _blockwise(
        "ssm_glu_bwd", glu_bwd_fn, [dy_ssm, y_pre, z, proj, w["ssm_d"], w["wglu"]],
        [_row_spec(t_row, ssm_w)] * 3 + [u_spec, _full_spec((1, ssm_w)), _full_spec((ssm_w, ssm_w))],
        [((seq, ssm_w), BF16), ((seq, ssm_w), BF16), ((seq, ssm_w), F32), ((1, ssm_w), F32), ((1, ssm_w), F32)],
        [_row_spec(t_row, ssm_w)] * 3 + [_full_spec((1, ssm_w))] * 2, g1, n_acc=2)
    g_wglu = _mm2d("ssm_glu_dw", yg, dz, TN, BF16)
    dq_raw, dkv, dkp_h = _attn_bwd(q_raw, kv, kpe, cos_t, sa_t, sb_t, dy_mla)

    def head_mm_dx(name, dact, wh):
        kdim, ndim = wh.shape[1], wh.shape[2]
        return _mm(name, dact, wh, grid=(1, 1, nh), contract=NT,
                   a_spec=pl.BlockSpec((None, seq, ndim), lambda i, j, h: (h, i, 0)),
                   b_spec=pl.BlockSpec((None, kdim, ndim), lambda i, j, h: (h, 0, 0)),
                   o_spec=pl.BlockSpec((seq, kdim), lambda i, j, h: (i, 0)),
                   out_shape=(seq, kdim), out_dtype=F32)

    def head_mm_dw(name, act, dact):
        kdim, ndim = act.shape[1], dact.shape[2]
        return _mm(name, act, dact, grid=(nh, 1, seq // tks), contract=TN,
                   a_spec=pl.BlockSpec((tks, kdim), lambda h, j, k: (k, 0)),
                   b_spec=pl.BlockSpec((None, tks, ndim), lambda h, j, k: (h, k, 0)),
                   o_spec=pl.BlockSpec((None, kdim, ndim), lambda h, j, k: (h, 0, 0)),
                   out_shape=(nh, kdim, ndim), out_dtype=BF16)

    g_wuq = head_mm_dw("mla_q_dw", qn, dq_raw)
    g_wukv = head_mm_dw("mla_kv_dw", kvn, dkv)
    dqn = head_mm_dx("mla_q_dx", dq_raw, w["wuq"])
    dkvn = head_mm_dx("mla_kv_dx", dkv, w["wukv"])
    emit(not_before=(dqn, dkvn, dy_pre), wout=g_wout, wuq=g_wuq, wukv=g_wukv, wglu=g_wglu, conv_w=g_convw)

    du, dwb, dwc, da_lay = _ssm_bwd(dy_pre, s_all, proj, du1, wb, wc, a_lay)
    g_c_re = blockdiag_out_t(dwc[:, :STATE_BLOCK, :])
    g_c_im = -blockdiag_out_t(dwc[:, STATE_BLOCK:, :])
    dbbt_re = blockdiag_in_t(dwb[:, :, :STATE_BLOCK])
    dbbt_im = blockdiag_in_t(dwb[:, :, STATE_BLOCK:])
    da3 = da_lay.reshape(nj, 2, STATE_BLOCK)
    dabar_re = da3[:, 0, :].reshape(n_groups, 1, SSM_STATE)
    dabar_im = da3[:, 1, :].reshape(n_groups, 1, SSM_STATE)
    g_lr3, g_li3, g_ldt3, g_bt_re, g_bt_im = _s5_prep_bwd(lr3, li3, ldt3, bt_re, bt_im,
                                                           dabar_re, dabar_im, dbbt_re, dbbt_im)

    def mla_prep_bwd_fn(cq, ckv, dqn_b, dkvn_b, dkp_b, cos, sa, sb, wq, wkv):
        dcq, dwq = _rms_bwd(cq, wq, dqn_b)
        dckv, dwkv = _rms_bwd(ckv, wkv, dkvn_b)
        dkp_sum = dkp_b[0]
        for h in range(1, nh):
            dkp_sum = dkp_sum + dkp_b[h]
        return dcq, dckv, _rope128_t(dkp_sum, cos, sa, sb), dwq, dwkv

    dc_q, dc_kv, dkpe_raw, g_qnorm, g_kvnorm = _blockwise(
        "mla_prep_bwd", mla_prep_bwd_fn, [c_q, c_kv, dqn, dkvn, dkp_h, cos_t, sa_t, sb_t, w["q_norm"], w["kv_norm"]],
        [_row_spec(t_row, q_rank), _row_spec(t_row, kv_rank), _row_spec(t_row, q_rank), _row_spec(t_row, kv_rank),
         pl.BlockSpec((nh, t_row, LANES), lambda i: (0, i, 0))] + [_row_spec(t_row, LANES)] * 3
        + [_full_spec((1, q_rank)), _full_spec((1, kv_rank))],
        [((seq, q_rank), BF16), ((seq, kv_rank), BF16), ((seq, LANES), BF16), ((1, q_rank), F32), ((1, kv_rank), F32)],
        [_row_spec(t_row, q_rank), _row_spec(t_row, kv_rank), _row_spec(t_row, LANES), _full_spec((1, q_rank)),
         _full_spec((1, kv_rank))], g1, n_acc=2)

    dproj = jnp.concatenate([du, dc_q, dc_kv, dkpe_raw], axis=1)
    g_win = _mm2d("proj_dw", hn, dproj, TN, BF16, tn=640)
    emit(win=g_win)
    def norm1_bwd_fn(dpb, wi, xb, dres, wv):
        dn = lax.dot_general(dpb, wi, (NT, ((), ())), preferred_element_type=F32)
        dx_, dw_ = _rms_bwd(xb, wv, dn)
        return dres + dx_, dw_

    grad_x, g_attn_norm = _blockwise(
        "proj_dx_norm1_bwd", norm1_bwd_fn, [dproj, w["win"], x, dh1, attn_w],
        [_row_spec(t_row, in_pad), _full_spec((d, in_pad)), _row_spec(t_row, d), _row_spec(t_row, d), _full_spec((1, d))],
        [((seq, d), F32), ((1, d), F32)], [_row_spec(t_row, d), _full_spec((1, d))], g1, n_acc=1)
    emit(win_pair_sums_after=grad_x)

    grads = dict(
        attn_norm=g_attn_norm, win=g_win, lam_re=g_lr3, lam_im=g_li3, log_dt=g_ldt3,
        bt_re=g_bt_re, bt_im=g_bt_im, c_re=g_c_re, c_im=g_c_im,
        ssm_d=g_ssmd, wglu=g_wglu, b_glu=g_bglu, q_norm=g_qnorm, wuq=g_wuq, kv_norm=g_kvnorm, wukv=g_wukv,
        son=g_son, mon=g_mon, wout=g_wout, ffn_norm=g_ffn_norm, wup=g_wup, conv_w=g_convw, conv_b=g_convb,
        wdown=g_wdown, final_norm=g_final)
    return loss, grad_x, grads


def _mesh_pos():
    return lax.axis_index("x"), lax.axis_index("y"), lax.axis_index("c")


def _handshake_all():
    x, y, c = _mesh_pos()
    barrier = pltpu.get_barrier_semaphore()
    for k in range(1, N_DEV):
        peer = (1 - x if k & 4 else x, 1 - y if k & 2 else y, 1 - c if k & 1 else c)
        pl.semaphore_signal(barrier, inc=1, device_id=peer, device_id_type=MESH)
    pl.semaphore_wait(barrier, N_DEV - 1)


def _handshake(peers):
    barrier = pltpu.get_barrier_semaphore()
    for peer in peers:
        pl.semaphore_signal(barrier, inc=1, device_id=peer, device_id_type=MESH)
    pl.semaphore_wait(barrier, len(peers))


def _comm_call(name, body, n, out_shape, ins, collective_id, after=None, copies=7, n_remote=None, n_local=None):
    n_remote = copies * n if n_remote is None else n_remote
    sems = [pltpu.SemaphoreType.DMA((n_remote,)), pltpu.SemaphoreType.DMA((n_remote,)),
            pltpu.SemaphoreType.DMA((n if n_local is None else n_local,))]
    if collective_id is None:
        any_spec = pl.BlockSpec(memory_space=pl.ANY)
        return pl.pallas_call(body, name=name, out_shape=out_shape, in_specs=[any_spec] * n,
                              out_specs=[any_spec] * n, scratch_shapes=sems)(*ins)
    seq_body = body
    if after:
        n_after = len(after)
        ins = list(ins) + list(after)

        def seq_body(*refs):
            body(*refs[:n], *refs[n + n_after:])

    return pl.kernel(seq_body, name=name, out_type=out_shape,
                     mesh=plsc.ScalarSubcoreMesh(axis_name="seq", num_cores=1), scratch_types=sems,
                     compiler_params=pltpu.CompilerParams(collective_id=collective_id))(*ins)


def _all_gather(name, xs, collective_id=None, after=None, pair_sums=()):
    n = len(xs)
    nh = len(pair_sums)
    m = n + nh

    def body(*refs):
        x_refs, h_refs, o_refs, e_refs = refs[:n], refs[n:m], refs[m:m + n], refs[m + n:2 * m]
        send_sems, recv_sems, local_sems = refs[2 * m:]
        if collective_id is not None:
            _handshake_all()
        finish_pairs = _chip_copies(h_refs, e_refs, send_sems, recv_sems, local_sems, 7 * n, n) if nh else None
        x, y, c = _mesh_pos()
        me, sibling = (x, y, c), (x, y, 1 - c)
        chips = [(1 - x, y), (x, 1 - y), (1 - x, 1 - y)]

        def slot(o_ref, px, py, pc):
            return o_ref.at[4 * px + 2 * py + pc]

        def copy(t, k, block, to, src=None):
            dst = slot(o_refs[t], *block)
            return pltpu.make_async_remote_copy(
                src_ref=dst if src is None else src, dst_ref=dst,
                send_sem=send_sems.at[7 * t + k], recv_sem=recv_sems.at[7 * t + k],
                device_id=to, device_id_type=MESH)

        started = []
        for t in range(n):
            mine = pltpu.make_async_copy(x_refs[t], slot(o_refs[t], *me), local_sems.at[t])
            mine.start()
            started.append(mine)
        first = []
        for t in range(n):
            first.append(copy(t, 0, me, sibling, src=x_refs[t]))
            first += [copy(t, 1 + j, me, (*chip, c), src=x_refs[t]) for j, chip in enumerate(chips)]
        for cp in first:
            cp.start()
        passed = []
        for j, chip in enumerate(chips):
            for t in range(n):
                copy(t, 1 + j, (*chip, c), me).wait_recv()
                fwd = copy(t, 4 + j, (*chip, c), sibling)
                fwd.start()
                passed.append(fwd)
        for t in range(n):
            copy(t, 0, sibling, me).wait_recv()
            for j, chip in enumerate(chips):
                copy(t, 4 + j, (*chip, 1 - c), me).wait_recv()
        for cp in first + passed:
            cp.wait_send()
        for mine in started:
            mine.wait()
        if nh:
            finish_pairs()

    out_shape = ([jax.ShapeDtypeStruct((N_DEV,) + v.shape, v.dtype) for v in xs]
                 + [jax.ShapeDtypeStruct(v.shape, v.dtype) for v in pair_sums])
    return _comm_call(name, body, m, out_shape, list(xs) + list(pair_sums), collective_id, after,
                      n_remote=7 * n + (N_CHIP - 1) * nh, n_local=m)


def _exchange_partials(name, gs, collective_id=None, after=None):
    n = len(gs)

    def body(*refs):
        g_refs, o_refs = refs[:n], refs[n:2 * n]
        send_sems, recv_sems, local_sems = refs[2 * n:]
        if collective_id is not None:
            _handshake_all()
        x, y, c = _mesh_pos()
        me_idx = 4 * x + 2 * y + c
        copies = []
        for t in range(n):
            mine = pltpu.make_async_copy(g_refs[t].at[me_idx], o_refs[t].at[me_idx], local_sems.at[t])
            mine.start()
            copies.append(mine)
        remote = []
        for k in range(1, N_DEV):
            px = 1 - x if k & 4 else x
            py = 1 - y if k & 2 else y
            pc = 1 - c if k & 1 else c
            p_idx = 4 * px + 2 * py + pc
            for t in range(n):
                cp = pltpu.make_async_remote_copy(
                    src_ref=g_refs[t].at[p_idx], dst_ref=o_refs[t].at[me_idx],
                    send_sem=send_sems.at[7 * t + k - 1], recv_sem=recv_sems.at[7 * t + k - 1],
                    device_id=(px, py, pc), device_id_type=MESH)
                cp.start()
                landing = pltpu.make_async_remote_copy(
                    src_ref=g_refs[t].at[p_idx], dst_ref=o_refs[t].at[p_idx],
                    send_sem=send_sems.at[7 * t + k - 1], recv_sem=recv_sems.at[7 * t + k - 1],
                    device_id=(px, py, pc), device_id_type=MESH)
                remote.append((cp, landing))
        for cp, landing in remote:
            landing.wait_recv()
        for cp, landing in remote:
            cp.wait_send()
        for mine in copies:
            mine.wait()

    out_shape = [jax.ShapeDtypeStruct(v.shape, v.dtype) for v in gs]
    return _comm_call(name, body, n, out_shape, gs, collective_id, after)


N_CHIP = N_DEV // 2
PAIR_ADD_BLOCK_ELEMS = 1024 * 1024


def _pair_swap(name, gs, collective_id, after=None):
    n = len(gs)

    def body(*refs):
        g_refs, o_refs = refs[:n], refs[n:2 * n]
        send_sems, recv_sems, _ = refs[2 * n:]
        x, y, c = _mesh_pos()
        sibling = (x, y, 1 - c)
        _handshake([sibling])
        copies = []
        for t in range(n):
            for k in range(N_CHIP):
                copies.append(pltpu.make_async_remote_copy(
                    src_ref=g_refs[t].at[2 * k + 1 - c], dst_ref=o_refs[t].at[k],
                    send_sem=send_sems.at[N_CHIP * t + k], recv_sem=recv_sems.at[N_CHIP * t + k],
                    device_id=sibling, device_id_type=MESH))
        for cp in copies:
            cp.start()
        for cp in copies:
            cp.wait_recv()
        for cp in copies:
            cp.wait_send()

    out_shape = [jax.ShapeDtypeStruct((N_CHIP,) + v.shape[1:], v.dtype) for v in gs]
    return _comm_call(name, body, n, out_shape, gs, collective_id, after, copies=N_CHIP)


def _pair_add(name, g, got):
    _, r, c = g.shape
    tr = r
    if r * c > PAIR_ADD_BLOCK_ELEMS and r % SUBLANES == 0:
        tr = SUBLANES
        while r % (tr * 2) == 0 and tr * 2 * c <= PAIR_ADD_BLOCK_ELEMS:
            tr *= 2

    def body(core_ref, g_ref, got_ref, o_ref):
        o_ref[...] = (g_ref[...].astype(F32) + got_ref[...].astype(F32)).astype(o_ref.dtype)

    grid_spec = pltpu.PrefetchScalarGridSpec(
        num_scalar_prefetch=1, grid=(N_CHIP, r // tr),
        in_specs=[pl.BlockSpec((None, None, tr, c), lambda k, i, core: (k, core[0], i, 0)),
                  pl.BlockSpec((None, tr, c), lambda k, i, core: (k, i, 0))],
        out_specs=pl.BlockSpec((None, tr, c), lambda k, i, core: (k, i, 0)))
    core = lax.axis_index("c").astype(jnp.int32).reshape(1)
    return pl.pallas_call(body, name=name, grid_spec=grid_spec, out_shape=jax.ShapeDtypeStruct((N_CHIP, r, c), g.dtype),
                          compiler_params=_cparams())(core, g.reshape(N_CHIP, 2, r, c), got)


def _chip_copies(h_refs, o_refs, send_sems, recv_sems, local_sems, sem0, local0):
    n = len(h_refs)
    per = N_CHIP - 1
    x, y, c = _mesh_pos()
    others = [(1 - x if k & 2 else x, 1 - y if k & 1 else y) for k in range(1, N_CHIP)]
    my_chip = 2 * x + y
    local = []
    for t in range(n):
        mine = pltpu.make_async_copy(h_refs[t].at[my_chip], o_refs[t].at[my_chip], local_sems.at[local0 + t])
        mine.start()
        local.append(mine)
    remote = []
    for j, (px, py) in enumerate(others):
        chip = 2 * px + py
        for t in range(n):
            sems = dict(send_sem=send_sems.at[sem0 + per * t + j], recv_sem=recv_sems.at[sem0 + per * t + j],
                        device_id=(px, py, c), device_id_type=MESH)
            cp = pltpu.make_async_remote_copy(src_ref=h_refs[t].at[chip], dst_ref=o_refs[t].at[my_chip], **sems)
            cp.start()
            landing = pltpu.make_async_remote_copy(src_ref=h_refs[t].at[chip], dst_ref=o_refs[t].at[chip], **sems)
            remote.append((cp, landing))

    def finish():
        for cp, landing in remote:
            landing.wait_recv()
        for cp, landing in remote:
            cp.wait_send()
        for mine in local:
            mine.wait()

    return finish


def _chip_exchange(name, hs, collective_id, after=None):
    n = len(hs)
    per = N_CHIP - 1

    def body(*refs):
        h_refs, o_refs = refs[:n], refs[n:2 * n]
        send_sems, recv_sems, local_sems = refs[2 * n:]
        x, y, c = _mesh_pos()
        _handshake([(1 - x if k & 2 else x, 1 - y if k & 1 else y, c) for k in range(1, N_CHIP)])
        _chip_copies(h_refs, o_refs, send_sems, recv_sems, local_sems, 0, 0)()

    out_shape = [jax.ShapeDtypeStruct(v.shape, v.dtype) for v in hs]
    return _comm_call(name, body, n, out_shape, hs, collective_id, after, copies=per)


ADAM_BLOCK_ELEMS = 128 * 1024


def _sum_parts(pb):
    g = pb[0].astype(F32)
    for j in range(1, pb.shape[0]):
        g = g + pb[j].astype(F32)
    return g


def _adam_math(g, wb_, mb, vb):
    m_new = ADAM_B1 * mb + (1.0 - ADAM_B1) * g
    v_new = ADAM_B2 * vb + (1.0 - ADAM_B2) * (g * g)
    m_hat = m_new / (1.0 - ADAM_B1 ** ADAM_STEP)
    v_hat = v_new / (1.0 - ADAM_B2 ** ADAM_STEP)
    delta = -ADAM_LR * (m_hat / (jnp.sqrt(v_hat) + ADAM_EPS) + ADAM_WD * wb_)
    return g, delta, m_new, v_new


def _adamw_multi(name, items, nblk=1, packed=None):
    n = len(items)

    def spec(shape, lead):
        blk = list(shape)
        blk[lead + 1] = shape[lead + 1] // nblk
        if nblk == 1:
            return pl.BlockSpec(tuple(blk), lambda i, nd=len(shape): (0,) * nd)
        return pl.BlockSpec(tuple(blk), lambda i, nd=len(shape), ax=lead + 1: (0,) * ax + (i,) + (0,) * (nd - ax - 1))

    ins, in_specs, out_specs, out_shape, where = [], [], [], [], []
    if packed is not None:
        ins.append(packed)
        in_specs.append(spec(packed.shape, 1))
    for parts, wv, mv, vv in items:
        if isinstance(parts, int):
            where.append((0, parts, len(ins)))
        else:
            assert parts.shape[1:] == wv.shape, (name, parts.shape, wv.shape)
            where.append((len(ins), None, len(ins) + 1))
            ins.append(parts)
            in_specs.append(spec(parts.shape, 1))
        ins += [wv, mv, vv]
        in_specs += [spec(wv.shape, 0)] * 3
        out_specs += [spec(wv.shape, 0)] * 4
        out_shape += [jax.ShapeDtypeStruct(wv.shape, F32)] * 4
    n_in = len(ins)

    def body(*refs):
        for t, (ip, off, iw) in enumerate(where):
            wr, mr, vr = refs[iw:iw + 3]
            parts = refs[ip][...] if off is None else refs[ip][:, :, off:off + wr.shape[-1]]
            res = _adam_math(_sum_parts(parts), wr[...], mr[...], vr[...])
            for o, val in zip(refs[n_in + 4 * t:n_in + 4 * t + 4], res):
                o[...] = val

    res = pl.pallas_call(body, name=name, grid=(nblk,), in_specs=in_specs, out_specs=out_specs, out_shape=out_shape,
                         compiler_params=_cparams())(*ins)
    return [tuple(res[4 * t:4 * t + 4]) for t in range(n)]


def _sum_multi(name, parts_list):
    def body(*refs):
        for pr, o in zip(refs[:len(parts_list)], refs[len(parts_list):]):
            o[...] = _sum_parts(pr[...])

    return pl.pallas_call(body, name=name, out_shape=[jax.ShapeDtypeStruct(p.shape[1:], F32) for p in parts_list],
                          compiler_params=_cparams())(*parts_list)


def _adamw_sum(name, parts, wv, mv, vv):
    npart, r, c = parts.shape
    tr = r
    if r * c > ADAM_BLOCK_ELEMS and r % SUBLANES == 0:
        tr = SUBLANES
        while r % (tr * 2) == 0 and tr * 2 * c <= ADAM_BLOCK_ELEMS:
            tr *= 2

    def fn(pb, wb_, mb, vb):
        return _adam_math(_sum_parts(pb), wb_, mb, vb)

    row = pl.BlockSpec((tr, c), lambda i: (i, 0))
    return _blockwise(name, fn, [parts, wv, mv, vv],
                      [pl.BlockSpec((npart, tr, c), lambda i: (0, i, 0)), row, row, row],
                      [((r, c), F32)] * 4, [row] * 4, (r // tr,))


_VECTORS = ["attn_norm", "lam_re", "lam_im", "log_dt", "ssm_d", "b_glu", "q_norm", "kv_norm", "son", "mon",
            "ffn_norm", "conv_b", "final_norm"]
_GHP = ["c_re", "c_im", "bt_re", "bt_im"]
_PACKED = ["attn_norm", "ssm_d", "b_glu", "q_norm", "kv_norm", "son", "mon", "ffn_norm", "conv_b", "final_norm"]
_BIG = ["win", "wglu", "wuq", "wukv", "wout", "wup", "wdown", "conv_w"]
_ROWS_IN_LANES = ("win", "wuq")
_TWO_LEVEL = ("wup", "win")
_AFTER = "_pair_sums_after"
_ORDER = ["attn_norm", "win", "lam_re", "lam_im", "log_dt", "b_re", "b_im", "c_re", "c_im", "ssm_d", "wglu",
          "b_glu", "q_norm", "wuq", "kv_norm", "wukv", "son", "mon", "wout", "ffn_norm", "wup", "conv_w",
          "conv_b", "wdown", "final_norm"]


def kernel(x, positions, attn_norm_w, w_in, ssm_lambda_re, ssm_lambda_im, ssm_log_dt, ssm_b_re, ssm_b_im, ssm_c_re, ssm_c_im, ssm_d, ssm_w_glu, ssm_b_glu, mla_q_norm_w, mla_w_uq, mla_kv_norm_w, mla_w_ukv, ssm_out_norm_w, mla_out_norm_w, w_out, ffn_norm_w, ffn_w_up, ffn_conv_w, ffn_conv_b, ffn_w_down, final_norm_w, loss_target, m_attn_norm_w, m_w_in, m_ssm_lambda_re, m_ssm_lambda_im, m_ssm_log_dt, m_ssm_b_re, m_ssm_b_im, m_ssm_c_re, m_ssm_c_im, m_ssm_d, m_ssm_w_glu, m_ssm_b_glu, m_mla_q_norm_w, m_mla_w_uq, m_mla_kv_norm_w, m_mla_w_ukv, m_ssm_out_norm_w, m_mla_out_norm_w, m_w_out, m_ffn_norm_w, m_ffn_w_up, m_ffn_conv_w, m_ffn_conv_b, m_ffn_w_down, m_final_norm_w, v_attn_norm_w, v_w_in, v_ssm_lambda_re, v_ssm_lambda_im, v_ssm_log_dt, v_ssm_b_re, v_ssm_b_im, v_ssm_c_re, v_ssm_c_im, v_ssm_d, v_ssm_w_glu, v_ssm_b_glu, v_mla_q_norm_w, v_mla_w_uq, v_mla_kv_norm_w, v_mla_w_ukv, v_ssm_out_norm_w, v_mla_out_norm_w, v_w_out, v_ffn_norm_w, v_ffn_w_up, v_ffn_conv_w, v_ffn_conv_b, v_ffn_w_down, v_final_norm_w):
    wts = dict(attn_norm=attn_norm_w, win=w_in, lam_re=ssm_lambda_re, lam_im=ssm_lambda_im, log_dt=ssm_log_dt,
               b_re=ssm_b_re, b_im=ssm_b_im, c_re=ssm_c_re, c_im=ssm_c_im, ssm_d=ssm_d, wglu=ssm_w_glu,
               b_glu=ssm_b_glu, q_norm=mla_q_norm_w, wuq=mla_w_uq, kv_norm=mla_kv_norm_w, wukv=mla_w_ukv,
               son=ssm_out_norm_w, mon=mla_out_norm_w, wout=w_out, ffn_norm=ffn_norm_w, wup=ffn_w_up,
               conv_w=ffn_conv_w, conv_b=ffn_conv_b, wdown=ffn_w_down, final_norm=final_norm_w)
    moms = dict(zip(_ORDER, [m_attn_norm_w, m_w_in, m_ssm_lambda_re, m_ssm_lambda_im, m_ssm_log_dt, m_ssm_b_re,
                             m_ssm_b_im, m_ssm_c_re, m_ssm_c_im, m_ssm_d, m_ssm_w_glu, m_ssm_b_glu, m_mla_q_norm_w,
                             m_mla_w_uq, m_mla_kv_norm_w, m_mla_w_ukv, m_ssm_out_norm_w, m_mla_out_norm_w, m_w_out,
                             m_ffn_norm_w, m_ffn_w_up, m_ffn_conv_w, m_ffn_conv_b, m_ffn_w_down, m_final_norm_w]))
    vels = dict(zip(_ORDER, [v_attn_norm_w, v_w_in, v_ssm_lambda_re, v_ssm_lambda_im, v_ssm_log_dt, v_ssm_b_re,
                             v_ssm_b_im, v_ssm_c_re, v_ssm_c_im, v_ssm_d, v_ssm_w_glu, v_ssm_b_glu, v_mla_q_norm_w,
                             v_mla_w_uq, v_mla_kv_norm_w, v_mla_w_ukv, v_ssm_out_norm_w, v_mla_out_norm_w, v_w_out,
                             v_ffn_norm_w, v_ffn_w_up, v_ffn_conv_w, v_ffn_conv_b, v_ffn_w_down, v_final_norm_w]))
    seq, d = x.shape[1], x.shape[2]
    in_width = w_in.shape[2]
    in_pad = -(-in_width // LANES) * LANES
    q_cols = mla_w_uq.shape[2]
    q_pad = 2 * LANES

    (win_g,) = _all_gather("gather_w_in", [jnp.pad(w_in[0], ((0, 0), (0, in_pad - in_width))).astype(BF16)])
    wglu_g, wuq_g, wukv_g, wout_g, convw_g = _all_gather(
        "gather_mix", [ssm_w_glu[0].astype(BF16), jnp.pad(mla_w_uq[0], ((0, 0), (0, q_pad - q_cols))).astype(BF16),
                       mla_w_ukv[0].astype(BF16), w_out[0].astype(BF16), ffn_conv_w[0]], collective_id=0)
    (wup_g,) = _all_gather("gather_ffn_up", [ffn_w_up[0].astype(BF16)], collective_id=1)
    (wdown_g,) = _all_gather("gather_ffn_down", [ffn_w_down[0].astype(BF16)], collective_id=2)
    ns = N_DEV
    c_ff = wup_g.shape[2]
    w = dict(
        attn_norm=attn_norm_w, win=win_g.reshape(d, in_pad), lam_re=ssm_lambda_re, lam_im=ssm_lambda_im,
        log_dt=ssm_log_dt, b_re=ssm_b_re, b_im=ssm_b_im, c_re=ssm_c_re, c_im=ssm_c_im, ssm_d=ssm_d,
        wglu=wglu_g.reshape(d // 2, d // 2), b_glu=ssm_b_glu, q_norm=mla_q_norm_w, wuq=wuq_g,
        kv_norm=mla_kv_norm_w, wukv=wukv_g, son=ssm_out_norm_w, mon=mla_out_norm_w, wout=wout_g.reshape(d, d),
        ffn_norm=ffn_norm_w, wup=wup_g, conv_w=convw_g, conv_b=ffn_conv_b,
        wdown=wdown_g.reshape(ns // 2 * c_ff, d), final_norm=final_norm_w)

    shard_layout = dict(
        win=lambda a: a[:, :in_width].reshape(N_DEV, d // N_DEV, in_width),
        wglu=lambda a: a.reshape(N_DEV, d // 2 // N_DEV, d // 2),
        wuq=lambda a: a[:, :, :q_cols], wukv=lambda a: a, wout=lambda a: a.reshape(N_DEV, d // N_DEV, d),
        wup=lambda a: a, wdown=lambda a: a.reshape(N_DEV, c_ff // 2, d), conv_w=lambda a: a)
    recv = {}
    next_id = [3]

    last = [None]

    out = {}

    def update(k):
        shp = wts[k].shape
        r, c = shp[-2], shp[-1]
        if k in _ROWS_IN_LANES:
            t = lambda a: jnp.swapaxes(a.reshape(-1, r, c), 1, 2)
            res = _adamw_sum("adamw_" + k, t(recv[k]), t(wts[k])[0], t(moms[k])[0], t(vels[k])[0])
            out[k] = [jnp.swapaxes(a, 0, 1).reshape(shp) for a in res]
            return res[0]
        res = _adamw_sum("adamw_" + k, recv[k].reshape(-1, r, c), wts[k].reshape(r, c),
                         moms[k].reshape(r, c), vels[k].reshape(r, c))
        out[k] = [a.reshape(shp) for a in res]
        return res[0]

    pending = {}

    def exchange(not_before=(), **grads):
        names = list(grads)
        if len(names) == 1 and names[0] in _TWO_LEVEL:
            k = names[0]
            parts = shard_layout[k](grads[k])
            got = _pair_swap("swap_" + k, [parts], collective_id=next_id[0], after=[last[0]])[0]
            next_id[0] += 1
            pending[k] = (parts, got)
            last[0] = got
            return
        if len(names) == 1 and names[0].endswith(_AFTER):
            k = names[0][:-len(_AFTER)]
            sums = _pair_add("pair_add_" + k, *pending[k])
            if k == "win":
                pending["tail"] = sums
                return
            recv[k] = _chip_exchange("exchange_" + k, [sums], collective_id=next_id[0],
                                     after=[last[0], grads[names[0]]])[0]
            next_id[0] += 1
            last[0] = recv[k]
            return
        got = _exchange_partials("exchange_" + "_".join(names), [shard_layout[k](grads[k]) for k in names],
                                 collective_id=next_id[0], after=[a for a in (last[0], *not_before) if a is not None])
        next_id[0] += 1
        last[0] = got[-1]
        recv.update(zip(names, got))

    loss_part, grad_x, g = _local_step(x[0], positions[0], loss_target[0], w, emit=exchange)
    n_groups = ssm_lambda_re.shape[1]
    two_d = {"lam_re": (n_groups, -1), "lam_im": (n_groups, -1)}
    dense = {k: g[k].reshape(two_d.get(k, (1, -1))) for k in _VECTORS}
    offsets, width = {}, 0
    for k in _PACKED:
        offsets[k] = width
        width += dense[k].shape[1]
    sent = dict(packed=jnp.concatenate([dense[k] for k in _PACKED], axis=1),
                **{k: dense[k] for k in _VECTORS if k not in _PACKED},
                **{k: g[k].reshape(n_groups, -1) for k in _GHP},
                loss=loss_part)
    names = list(sent)
    got = _all_gather("gather_small_grads", [sent[k] for k in names], collective_id=next_id[0], after=[last[0]],
                      pair_sums=[pending["tail"]])
    gathered = dict(zip(names, got))
    recv["win"] = got[len(names)]
    for k in _BIG:
        if k not in out and k != "win":
            update(k)
    update("win")

    def finish(keys, results):
        for k, res in zip(keys, results):
            out[k] = [a.reshape(wts[k].shape) for a in res]

    view = lambda k, a: a.reshape(dense[k].shape)
    finish(_VECTORS, _adamw_multi("adamw_vectors", [(offsets.get(k, gathered.get(k)), view(k, wts[k]), view(k, moms[k]),
                                                     view(k, vels[k])) for k in _VECTORS], packed=gathered["packed"]))
    summed = _GHP + ["loss"]
    sums = dict(zip(summed, _sum_multi("sum_ssm_bc_loss", [gathered[k] for k in summed])))
    loss = sums["loss"][0, 0]
    ghp = lambda k: sums[k].reshape(g[k].shape)
    t_hp = lambda a: jnp.swapaxes(a, 2, 3)
    bc_keys = ["c_re", "c_im", "b_re", "b_im"]
    items = [(ghp(k)[None, None], wts[k], moms[k], vels[k]) for k in bc_keys[:2]]
    items += [(ghp(t)[None, None], t_hp(wts[k]), t_hp(moms[k]), t_hp(vels[k]))
              for k, t in zip(bc_keys[2:], ("bt_re", "bt_im"))]
    res = _adamw_multi("adamw_ssm_bc", items)
    finish(bc_keys, res[:2] + [tuple(t_hp(a) for a in r) for r in res[2:]])

    grad_x = grad_x.reshape(x.shape)
    return (loss, grad_x, *[out[k][0] for k in _ORDER], *[out[k][1] for k in _ORDER],
            *[out[k][2] for k in _ORDER], *[out[k][3] for k in _ORDER])
```

```python
import functools
import math

import jax
import jax.numpy as jnp
from jax import lax
from jax.experimental import pallas as pl
from jax.experimental.pallas import tpu as pltpu
from jax.experimental.pallas import tpu_sc as plsc

F32 = jnp.float32
BF16 = jnp.bfloat16
MESH = pl.DeviceIdType.MESH

N_DEV = 8
LANES = 128
SUBLANES = 8
VMEM_LIMIT = 48 * 1024 * 1024

SSM_GROUP = 16
SSM_STATE = 64
GROUPS_PER_BLOCK = LANES // SSM_GROUP
STATE_BLOCK = GROUPS_PER_BLOCK * SSM_STATE
QK_NOPE = 128
QK_ROPE = 64
V_DIM = 128
ROPE_THETA = 10000.0
RMS_EPS = 1e-6

ADAM_LR = 0.001
ADAM_B1 = 0.9
ADAM_B2 = 0.999
ADAM_EPS = 1e-08
ADAM_WD = 0.01
ADAM_STEP = 10

NN = ((1,), (0,))
NT = ((1,), (1,))
TN = ((0,), (0,))


def _cparams():
    return pltpu.CompilerParams(vmem_limit_bytes=VMEM_LIMIT)


def _tile(n, want):
    if n <= want:
        return n
    t = (want // LANES) * LANES
    while t >= LANES:
        if n % t == 0:
            return t
        t -= LANES
    return n


def _mm(name, a, b, *, grid, a_spec, b_spec, o_spec, out_shape, out_dtype, contract=NN,
        res=None, res_spec=None):
    nk = grid[-1]
    kaxis = len(grid) - 1
    acc_shape = tuple(d for d in o_spec.block_shape if d is not None)

    def body(*refs):
        a_ref, b_ref = refs[:2]
        r_ref = None if res is None else refs[2]
        o_ref = refs[2 if res is None else 3]
        part = lax.dot_general(a_ref[...].astype(BF16), b_ref[...].astype(BF16),
                               (contract, ((), ())), preferred_element_type=F32)
        if nk == 1:
            if r_ref is not None:
                part = part + r_ref[...].astype(F32)
            o_ref[...] = part.astype(o_ref.dtype)
            return
        acc = refs[-1]
        k = pl.program_id(kaxis)

        @pl.when(k == 0)
        def _():
            acc[...] = part

        @pl.when(k != 0)
        def _():
            acc[...] += part

        @pl.when(k == nk - 1)
        def _():
            r = acc[...]
            if r_ref is not None:
                r = r + r_ref[...].astype(F32)
            o_ref[...] = r.astype(o_ref.dtype)

    ins = [a, b] + ([] if res is None else [res])
    in_specs = [a_spec, b_spec] + ([] if res is None else [res_spec])
    return pl.pallas_call(
        body, name=name, grid=grid, in_specs=in_specs, out_specs=o_spec,
        out_shape=jax.ShapeDtypeStruct(out_shape, out_dtype),
        scratch_shapes=[pltpu.VMEM(acc_shape, F32)] if nk > 1 else [], compiler_params=_cparams(),
    )(*ins)


def _mm2d(name, a, b, contract, out_dtype, tm=1024, tn=1024, tk=2048, res=None):
    if contract == NN:
        (m, kk), n = a.shape, b.shape[1]
    elif contract == NT:
        (m, kk), n = a.shape, b.shape[0]
    else:
        (kk, m), n = a.shape, b.shape[1]
    tm, tn, tk = _tile(m, tm), _tile(n, tn), _tile(kk, tk)
    grid = (m // tm, n // tn, kk // tk)
    if contract == TN:
        a_spec = pl.BlockSpec((tk, tm), lambda i, j, k: (k, i))
    else:
        a_spec = pl.BlockSpec((tm, tk), lambda i, j, k: (i, k))
    if contract == NT:
        b_spec = pl.BlockSpec((tn, tk), lambda i, j, k: (j, k))
    else:
        b_spec = pl.BlockSpec((tk, tn), lambda i, j, k: (k, j))
    o_spec = pl.BlockSpec((tm, tn), lambda i, j, k: (i, j))
    res_spec = None
    if res is not None:
        if res.shape[0] == 1:
            res_spec = pl.BlockSpec((1, tn), lambda i, j, k: (0, j))
        else:
            res_spec = pl.BlockSpec((tm, tn), lambda i, j, k: (i, j))
    return _mm(name, a, b, grid=grid, a_spec=a_spec, b_spec=b_spec, o_spec=o_spec,
               out_shape=(m, n), out_dtype=out_dtype, contract=contract, res=res, res_spec=res_spec)


def _blockwise(name, fn, ins, in_specs, outs, out_specs, grid, n_acc=0, acc_all=True):
    n_in, n_out = len(ins), len(outs)
    n_plain = n_out - n_acc

    def body(*refs):
        vals = fn(*[r[...] for r in refs[:n_in]])
        if not isinstance(vals, (tuple, list)):
            vals = (vals,)
        o_refs = refs[n_in:n_in + n_out]
        for r, v in zip(o_refs[:n_plain], vals[:n_plain]):
            r[...] = v.astype(r.dtype)
        if n_acc:
            if acc_all:
                first = functools.reduce(jnp.logical_and, [pl.program_id(d) == 0 for d in range(len(grid))])
            else:
                first = pl.program_id(len(grid) - 1) == 0

            @pl.when(first)
            def _():
                for r, v in zip(o_refs[n_plain:], vals[n_plain:]):
                    r[...] = v.astype(r.dtype)

            @pl.when(jnp.logical_not(first))
            def _():
                for r, v in zip(o_refs[n_plain:], vals[n_plain:]):
                    r[...] += v.astype(r.dtype)

    return pl.pallas_call(
        body, name=name, grid=grid, in_specs=in_specs, out_specs=out_specs,
        out_shape=[jax.ShapeDtypeStruct(s, d) for s, d in outs], compiler_params=_cparams(),
    )(*ins)


def _row_spec(t, c):
    return pl.BlockSpec((t, c), lambda i: (i, 0))


def _full_spec(shape, single=False):
    nd = len(shape)
    if single:
        return pl.BlockSpec(tuple(shape), lambda *g: (0,) * nd, pipeline_mode=pl.Buffered(1))
    return pl.BlockSpec(tuple(shape), lambda *g: (0,) * nd)


def _rms(xf, w):
    return xf * lax.rsqrt(jnp.mean(xf * xf, axis=-1, keepdims=True) + RMS_EPS) * w


def _rms_bwd(xf, w, dy):
    _, vjp = jax.vjp(_rms, xf, w)
    return vjp(dy)


def _s5_disc(lr, li, ldt, bre, bim):
    dt = jnp.exp(ldt)
    mag = jnp.exp(lr * dt)
    ar = mag * jnp.cos(li * dt)
    ai = mag * jnp.sin(li * dt)
    nr, ni = ar - 1.0, ai
    den = lr * lr + li * li
    zr = (nr * lr + ni * li) / den
    zi = (ni * lr - nr * li) / den
    return ar, ai, zr * bre - zi * bim, zr * bim + zi * bre


def _s5_prep(lr, li, ldt, bre, bim):
    def body(lr_r, li_r, ldt_r, bre_r, bim_r, ar_r, ai_r, br_r, bi_r):
        ar, ai, br, bi = _s5_disc(lr_r[...], li_r[...], ldt_r[...], bre_r[...], bim_r[...])
        ar_r[...] = ar
        ai_r[...] = ai
        br_r[...] = br
        bi_r[...] = bi

    sd = jax.ShapeDtypeStruct
    return pl.pallas_call(
        body, name="s5_prep",
        out_shape=[sd(lr.shape, F32), sd(lr.shape, F32), sd(bre.shape, F32), sd(bre.shape, F32)],
        compiler_params=_cparams(),
    )(lr, li, ldt, bre, bim)


def _s5_prep_bwd(lr, li, ldt, bre, bim, dar, dai, dbr, dbi):
    def body(lr_r, li_r, ldt_r, bre_r, bim_r, dar_r, dai_r, dbr_r, dbi_r, o0, o1, o2, o3, o4):
        _, vjp = jax.vjp(_s5_disc, lr_r[...], li_r[...], ldt_r[...], bre_r[...], bim_r[...])
        g = vjp((dar_r[...], dai_r[...], dbr_r[...], dbi_r[...]))
        for o, v in zip((o0, o1, o2, o3, o4), g):
            o[...] = v

    sd = jax.ShapeDtypeStruct
    return pl.pallas_call(
        body, name="s5_prep_bwd",
        out_shape=[sd(lr.shape, F32), sd(li.shape, F32), sd(ldt.shape, F32), sd(bre.shape, F32), sd(bim.shape, F32)],
        compiler_params=_cparams(),
    )(lr, li, ldt, bre, bim, dar, dai, dbr, dbi)


SCAN_T = 256


def _scan_tables(ar, ai, tab_r, tab_i, sub, reverse):
    pr, pi = ar, ai
    for k in range(sub):
        row = sub - 1 - k if reverse else k
        tab_r[row:row + 1, :] = pr
        tab_i[row:row + 1, :] = pi
        pr, pi = ar * pr - ai * pi, ar * pi + ai * pr


def _pack_matrix(t_blk, dtype):
    sub = t_blk // SUBLANES
    dst = jnp.arange(t_blk)
    src = (dst % SUBLANES) * sub + dst // SUBLANES
    return (src[:, None] == jnp.arange(t_blk)[None, :]).astype(dtype)


def _permute_rows_f32(pm, x):
    hi = x.astype(BF16)
    r1 = x - hi.astype(F32)
    mid = r1.astype(BF16)
    lo = (r1 - mid.astype(F32)).astype(BF16)
    dot = lambda v: jnp.dot(pm, v, preferred_element_type=F32)
    return dot(hi) + dot(mid) + dot(lo)


def _scan_block(x, loc, ar, ai, st, tab_r, tab_i, sub, reverse):
    hb = STATE_BLOCK
    a8r = jnp.broadcast_to(ar, (SUBLANES, hb))
    a8i = jnp.broadcast_to(ai, (SUBLANES, hb))
    sr = jnp.zeros((SUBLANES, hb), F32)
    si = jnp.zeros((SUBLANES, hb), F32)
    steps = range(sub - 1, -1, -1) if reverse else range(sub)
    for t in steps:
        rows = slice(t * SUBLANES, (t + 1) * SUBLANES)
        sr, si = a8r * sr - a8i * si + x[rows, :hb], a8r * si + a8i * sr + x[rows, hb:]
        loc[rows, :hb] = sr
        loc[rows, hb:] = si
    cr, ci = st[0:1, :], st[1:2, :]
    far = 0 if reverse else sub - 1
    fr, fi = tab_r[far:far + 1, :], tab_i[far:far + 1, :]
    ent_r, ent_i = [None] * SUBLANES, [None] * SUBLANES
    for c in (range(SUBLANES - 1, -1, -1) if reverse else range(SUBLANES)):
        ent_r[c], ent_i[c] = cr, ci
        cr, ci = sr[c:c + 1, :] + (fr * cr - fi * ci), si[c:c + 1, :] + (fr * ci + fi * cr)
    st[0:1, :] = cr
    st[1:2, :] = ci
    c8r = jnp.concatenate(ent_r, axis=0)
    c8i = jnp.concatenate(ent_i, axis=0)
    out = []
    for t in range(sub):
        rows = slice(t * SUBLANES, (t + 1) * SUBLANES)
        tr, ti = tab_r[t:t + 1, :], tab_i[t:t + 1, :]
        out.append(jnp.concatenate([loc[rows, :hb] + (tr * c8r - ti * c8i), loc[rows, hb:] + (tr * c8i + ti * c8r)],
                                   axis=1))
    return jnp.concatenate(out, axis=0)


SSM_BLOCKS_PER_STEP = 2


def _scan_scratch(nblk, t_blk, sub, hb):
    return [pltpu.VMEM((nblk, SUBLANES, hb), F32), pltpu.VMEM((nblk, sub, hb), F32), pltpu.VMEM((nblk, sub, hb), F32),
            pltpu.VMEM((nblk, t_blk, 2 * hb), F32)]


def _ssm_fwd(proj, wb, wc, a):
    seq = proj.shape[0]
    nj = wb.shape[0]
    w2 = 2 * STATE_BLOCK
    hb = STATE_BLOCK
    t_blk = min(SCAN_T, seq)
    sub = t_blk // SUBLANES
    pm = _pack_matrix(t_blk, BF16)

    npair = SSM_BLOCKS_PER_STEP

    def body(u_ref, wb_ref, wc_ref, a_ref, pm_ref, pmt_ref, s_ref, y_ref, st, tab_r, tab_i, loc):
        coef = [(a_ref[:, b * w2:b * w2 + hb], a_ref[:, b * w2 + hb:(b + 1) * w2]) for b in range(npair)]

        @pl.when(pl.program_id(1) == 0)
        def _():
            for b, (ar, ai) in enumerate(coef):
                st[b] = jnp.zeros((SUBLANES, hb), F32)
                _scan_tables(ar, ai, tab_r.at[b], tab_i.at[b], sub, False)

        for b, (ar, ai) in enumerate(coef):
            ub = u_ref[:, b * LANES:(b + 1) * LANES].astype(BF16)
            up = jnp.dot(pm_ref[...], ub, preferred_element_type=F32).astype(BF16)
            bu = jnp.dot(up, wb_ref[b], preferred_element_type=F32)
            s = _scan_block(bu, loc.at[b], ar, ai, st.at[b], tab_r.at[b], tab_i.at[b], sub, False)
            s_ref[:, b * w2:(b + 1) * w2] = s
            yp = jnp.dot(s.astype(BF16), wc_ref[b], preferred_element_type=F32)
            y_ref[:, b * LANES:(b + 1) * LANES] = _permute_rows_f32(pmt_ref[...], yp)

    sd = jax.ShapeDtypeStruct
    return pl.pallas_call(
        body, name="ssm_fwd", grid=(nj // npair, seq // t_blk),
        in_specs=[pl.BlockSpec((t_blk, npair * LANES), lambda j, i: (i, j)),
                  pl.BlockSpec((npair, LANES, w2), lambda j, i: (j, 0, 0)),
                  pl.BlockSpec((npair, w2, LANES), lambda j, i: (j, 0, 0)),
                  pl.BlockSpec((1, npair * w2), lambda j, i: (0, j)),
                  _full_spec((t_blk, t_blk)), _full_spec((t_blk, t_blk))],
        out_specs=[pl.BlockSpec((t_blk, npair * w2), lambda j, i: (i, j)),
                   pl.BlockSpec((t_blk, npair * LANES), lambda j, i: (i, j))],
        out_shape=[sd((seq, nj * w2), F32), sd((seq, nj * LANES), F32)],
        scratch_shapes=_scan_scratch(npair, t_blk, sub, hb), compiler_params=_cparams(),
    )(proj, wb, wc, a, pm, pm.T)


def _ssm_bwd(dy, s, proj, du1, wb, wc, a):
    seq = dy.shape[0]
    nj = wb.shape[0]
    w2 = 2 * STATE_BLOCK
    hb = STATE_BLOCK
    t_blk = min(SCAN_T, seq)
    sub = t_blk // SUBLANES
    nb = seq // t_blk
    pm = _pack_matrix(t_blk, BF16)

    npair = SSM_BLOCKS_PER_STEP

    def body(dy_ref, s_ref, sprev_ref, u_ref, du1_ref, wb_ref, wc_ref, a_ref, pm_ref, pmt_ref,
             du_ref, dwb_ref, dwc_ref, da_ref, st, tab_r, tab_i, loc):
        ib = pl.program_id(1)
        pmv = pm_ref[...]
        coef = [(a_ref[:, b * w2:b * w2 + hb], -a_ref[:, b * w2 + hb:(b + 1) * w2]) for b in range(npair)]

        @pl.when(ib == 0)
        def _():
            for b, (ar, ai) in enumerate(coef):
                st[b] = jnp.zeros((SUBLANES, hb), F32)
                _scan_tables(ar, ai, tab_r.at[b], tab_i.at[b], sub, True)

        sums = []
        for b, (ar, ai) in enumerate(coef):
            cols, wide = slice(b * LANES, (b + 1) * LANES), slice(b * w2, (b + 1) * w2)
            dyp = jnp.dot(pmv, dy_ref[:, cols], preferred_element_type=F32).astype(BF16)
            up = jnp.dot(pmv, u_ref[:, cols].astype(BF16), preferred_element_type=F32).astype(BF16)
            ds = lax.dot_general(dyp, wc_ref[b], (NT, ((), ())), preferred_element_type=F32)
            lam = _scan_block(ds, loc.at[b], ar, ai, st.at[b], tab_r.at[b], tab_i.at[b], sub, True)
            lamb = lam.astype(BF16)
            du = lax.dot_general(lamb, wb_ref[b], (NT, ((), ())), preferred_element_type=F32)
            du_ref[:, cols] = (_permute_rows_f32(pmt_ref[...], du) + du1_ref[:, cols]).astype(du_ref.dtype)
            sv = s_ref[:, wide]
            dwb = lax.dot_general(up, lamb, (TN, ((), ())), preferred_element_type=F32)
            dwc = lax.dot_general(sv.astype(BF16), dyp, (TN, ((), ())), preferred_element_type=F32)

            prev_last = sprev_ref[SUBLANES - 1:SUBLANES, wide]
            prev_last = jnp.where(ib == nb - 1, jnp.zeros_like(prev_last), prev_last)
            tail = sv[t_blk - SUBLANES:, :]
            sl = lax.broadcasted_iota(jnp.int32, tail.shape, 0)
            head = jnp.where(sl >= 1, pltpu.roll(tail, 1, 0), prev_last)
            s_sh = jnp.concatenate([head, sv[:t_blk - SUBLANES, :]], axis=0)
            lam_r, lam_i = lam[:, :hb], lam[:, hb:]
            sr_, si_ = s_sh[:, :hb], s_sh[:, hb:]
            dar = jnp.sum(lam_r * sr_ + lam_i * si_, axis=0, keepdims=True)
            dai = jnp.sum(lam_i * sr_ - lam_r * si_, axis=0, keepdims=True)
            sums.append((wide, jnp.concatenate([dar, dai], axis=1), dwb, dwc))

        @pl.when(ib == 0)
        def _():
            for b, (wide, contrib, dwb, dwc) in enumerate(sums):
                da_ref[:, wide] = contrib
                dwb_ref[b] = dwb
                dwc_ref[b] = dwc

        @pl.when(ib != 0)
        def _():
            for b, (wide, contrib, dwb, dwc) in enumerate(sums):
                da_ref[:, wide] += contrib
                dwb_ref[b] += dwb
                dwc_ref[b] += dwc

    blk = lambda j, i: (nb - 1 - i, j)
    prev_blk = lambda j, i: (jnp.maximum((nb - 1 - i) * sub - 1, 0), j)
    sd = jax.ShapeDtypeStruct
    return pl.pallas_call(
        body, name="ssm_bwd", grid=(nj // npair, nb),
        in_specs=[pl.BlockSpec((t_blk, npair * LANES), blk), pl.BlockSpec((t_blk, npair * w2), blk),
                  pl.BlockSpec((SUBLANES, npair * w2), prev_blk), pl.BlockSpec((t_blk, npair * LANES), blk),
                  pl.BlockSpec((t_blk, npair * LANES), blk),
                  pl.BlockSpec((npair, LANES, w2), lambda j, i: (j, 0, 0)),
                  pl.BlockSpec((npair, w2, LANES), lambda j, i: (j, 0, 0)),
                  pl.BlockSpec((1, npair * w2), lambda j, i: (0, j)),
                  _full_spec((t_blk, t_blk)), _full_spec((t_blk, t_blk))],
        out_specs=[pl.BlockSpec((t_blk, npair * LANES), blk),
                   pl.BlockSpec((npair, LANES, w2), lambda j, i: (j, 0, 0)),
                   pl.BlockSpec((npair, w2, LANES), lambda j, i: (j, 0, 0)),
                   pl.BlockSpec((1, npair * w2), lambda j, i: (0, j))],
        out_shape=[sd((seq, nj * LANES), BF16), sd((nj, LANES, w2), F32), sd((nj, w2, LANES), F32),
                   sd((1, nj * w2), F32)],
        scratch_shapes=_scan_scratch(npair, t_blk, sub, hb), compiler_params=_cparams(),
    )(dy, s, s, proj, du1, wb, wc, a, pm, pm.T)


def _rope128(x, cos, sa, sb):
    return x * cos + pltpu.roll(x, 96, 1) * sa + pltpu.roll(x, 32, 1) * sb


def _rope128_t(dy, cos, sa, sb):
    return dy * cos + pltpu.roll(dy * sa, 32, 1) + pltpu.roll(dy * sb, 96, 1)


ATT_BQ = 256


def _probs(qn, qp, kn, kp, r0, scale):
    s = lax.dot_general(qn, kn, (NT, ((), ())), preferred_element_type=F32)
    s = s + lax.dot_general(qp, kp, (NT, ((), ())), preferred_element_type=F32)
    s = s * scale
    diag = s[:, r0:]
    row = lax.broadcasted_iota(jnp.int32, diag.shape, 0)
    col = lax.broadcasted_iota(jnp.int32, diag.shape, 1)
    diag = jnp.where(col <= row, diag, jnp.finfo(F32).min)
    s = diag if r0 == 0 else jnp.concatenate([s[:, :r0], diag], axis=1)
    m = jnp.max(s, axis=-1, keepdims=True)
    e = jnp.exp(s - m)
    return e / jnp.sum(e, axis=-1, keepdims=True)


def _attn_specs(seq):
    tab = pl.BlockSpec((seq, LANES), lambda h: (0, 0))
    return [pl.BlockSpec((None, seq, 256), lambda h: (h, 0, 0)), pl.BlockSpec((None, seq, 128), lambda h: (h, 0, 0)),
            pl.BlockSpec((None, seq, 128), lambda h: (h, 0, 1)), tab, tab, tab, tab]


def _attn_fwd(q_raw, kv, kpe, cos, sa, sb):
    nh, seq, _ = q_raw.shape
    bq = min(ATT_BQ, seq)
    scale = (QK_NOPE + QK_ROPE) ** -0.5

    def body(q_ref, kn_ref, v_ref, kp_ref, cos_ref, sa_ref, sb_ref, o_ref):
        for r0 in range(0, seq, bq):
            rows, kend = pl.ds(r0, bq), r0 + bq
            qn = q_ref[rows, :QK_NOPE].astype(BF16)
            qp = _rope128(q_ref[rows, QK_NOPE:], cos_ref[rows, :], sa_ref[rows, :], sb_ref[rows, :]).astype(BF16)
            p = _probs(qn, qp, kn_ref[:kend, :], kp_ref[:kend, :], r0, scale)
            o_ref[rows, :] = jnp.dot(p.astype(BF16), v_ref[:kend, :], preferred_element_type=F32)

    return pl.pallas_call(
        body, name="attn_fwd", grid=(nh,), in_specs=_attn_specs(seq),
        out_specs=pl.BlockSpec((seq, V_DIM), lambda h: (0, h)),
        out_shape=jax.ShapeDtypeStruct((seq, nh * V_DIM), F32), compiler_params=_cparams(),
    )(q_raw, kv, kv, kpe, cos, sa, sb)


def _attn_bwd(q_raw, kv, kpe, cos, sa, sb, do):
    nh, seq, _ = q_raw.shape
    bq = min(ATT_BQ, seq)
    scale = (QK_NOPE + QK_ROPE) ** -0.5

    def body(q_ref, kn_ref, v_ref, kp_ref, cos_ref, sa_ref, sb_ref, do_ref, dq_ref, dkv_ref, dkp_ref):
        dkv_ref[...] = jnp.zeros_like(dkv_ref)
        dkp_ref[...] = jnp.zeros_like(dkp_ref)
        for r0 in range(0, seq, bq):
            rows, kend = pl.ds(r0, bq), r0 + bq
            cos_b, sa_b, sb_b = cos_ref[rows, :], sa_ref[rows, :], sb_ref[rows, :]
            qn = q_ref[rows, :QK_NOPE].astype(BF16)
            qp = _rope128(q_ref[rows, QK_NOPE:], cos_b, sa_b, sb_b).astype(BF16)
            kn, v, kp = kn_ref[:kend, :], v_ref[:kend, :], kp_ref[:kend, :]
            p = _probs(qn, qp, kn, kp, r0, scale)
            dob = do_ref[rows, :].astype(BF16)
            dp = lax.dot_general(dob, v, (NT, ((), ())), preferred_element_type=F32)
            ds = p * (dp - jnp.sum(p * dp, axis=-1, keepdims=True)) * scale
            dsb = ds.astype(BF16)
            pb = p.astype(BF16)
            dq_ref[rows, :QK_NOPE] = jnp.dot(dsb, kn, preferred_element_type=F32).astype(dq_ref.dtype)
            dqp = jnp.dot(dsb, kp, preferred_element_type=F32)
            dq_ref[rows, QK_NOPE:] = _rope128_t(dqp, cos_b, sa_b, sb_b).astype(dq_ref.dtype)
            dkv_ref[:kend, :QK_NOPE] += lax.dot_general(dsb, qn, (TN, ((), ())), preferred_element_type=F32)
            dkv_ref[:kend, QK_NOPE:] += lax.dot_general(pb, dob, (TN, ((), ())), preferred_element_type=F32)
            dkp_ref[:kend, :] += lax.dot_general(dsb, qp, (TN, ((), ())), preferred_element_type=F32)

    sd = jax.ShapeDtypeStruct
    return pl.pallas_call(
        body, name="attn_bwd", grid=(nh,),
        in_specs=_attn_specs(seq) + [pl.BlockSpec((seq, V_DIM), lambda h: (0, h))],
        out_specs=[pl.BlockSpec((None, seq, 256), lambda h: (h, 0, 0)),
                   pl.BlockSpec((None, seq, 256), lambda h: (h, 0, 0)),
                   pl.BlockSpec((None, seq, 128), lambda h: (h, 0, 0))],
        out_shape=[sd((nh, seq, 256), BF16), sd((nh, seq, 256), F32), sd((nh, seq, 128), F32)],
        compiler_params=_cparams(),
    )(q_raw, kv, kv, kpe, cos, sa, sb, do)


def _shift_rows(a, k):
    seq = a.shape[0]
    r = pltpu.roll(a, k % seq, 0)
    rows = lax.broadcasted_iota(jnp.int32, (SUBLANES, a.shape[1]), 0)
    if k > 0:
        return jnp.concatenate([jnp.where(rows >= k, r[:SUBLANES], 0.0), r[SUBLANES:]], axis=0)
    return jnp.concatenate([r[:seq - SUBLANES], jnp.where(rows < SUBLANES + k, r[seq - SUBLANES:], 0.0)], axis=0)


def _conv3(a, w, b):
    a1 = _shift_rows(a, 1)
    a2 = _shift_rows(a, 2)
    return w[2:3] * a + w[1:2] * a1 + w[0:1] * a2 + b, a1, a2


def _conv_gate_fwd(a, cw, cb):
    half, _, seq, c = a.shape
    nc = c // LANES

    def fn(pair, wg, wv, bg, bv):
        gc, _, _ = _conv3(pair[0], wg, bg)
        vc, _, _ = _conv3(pair[1], wv, bv)
        return gc * jax.nn.sigmoid(gc) * vc

    def w_spec(off, r):
        return pl.BlockSpec((None, r, LANES), lambda k, j: (k + off, 0, j))

    return _blockwise(
        "conv_gate_fwd", fn, [a, cw, cw, cb, cb],
        [pl.BlockSpec((None, 2, seq, LANES), lambda k, j: (k, 0, 0, j)),
         w_spec(0, 3), w_spec(half, 3), w_spec(0, 1), w_spec(half, 1)],
        [((seq, half * c), BF16)], [pl.BlockSpec((seq, LANES), lambda k, j: (0, k * nc + j))],
        grid=(half, nc))[0]


def _conv_gate_bwd(a, cw, cb, dm):
    half, _, seq, c = a.shape
    nc = c // LANES

    def body(a_ref, wg_ref, wv_ref, bg_ref, bv_ref, dm_ref, da_ref, dw_ref, db_ref):
        dmv = dm_ref[...]
        ga, wg = a_ref[0], wg_ref[...]
        va, wv = a_ref[1], wv_ref[...]
        gc, g1, g2 = _conv3(ga, wg, bg_ref[...])
        vc, v1, v2 = _conv3(va, wv, bv_ref[...])
        sg = jax.nn.sigmoid(gc)
        dms = dmv * sg
        d_val = dms * gc
        d_gate = dms * vc * (1.0 + gc * (1.0 - sg))

        def back(r, dc, own, a1, a2, w):
            up1 = _shift_rows(dc, -1)
            up2 = _shift_rows(dc, -2)
            da_ref[r] = (w[2:3] * dc + w[1:2] * up1 + w[0:1] * up2).astype(da_ref.dtype)
            dw_ref[r, 0:1, :] = jnp.sum(dc * a2, axis=0, keepdims=True)
            dw_ref[r, 1:2, :] = jnp.sum(dc * a1, axis=0, keepdims=True)
            dw_ref[r, 2:3, :] = jnp.sum(dc * own, axis=0, keepdims=True)
            db_ref[r] = jnp.sum(dc, axis=0, keepdims=True)

        back(0, d_gate, ga, g1, g2, wg)
        back(1, d_val, va, v1, v2, wv)

    def w_spec(off, r):
        return pl.BlockSpec((None, r, LANES), lambda k, j: (k + off, 0, j))

    def pair_spec(r):
        return pl.BlockSpec((None, 2, r, LANES), lambda k, j: (k, 0, 0, j))

    sd = jax.ShapeDtypeStruct
    return pl.pallas_call(
        body, name="conv_gate_bwd", grid=(half, nc),
        in_specs=[pair_spec(seq), w_spec(0, 3), w_spec(half, 3), w_spec(0, 1), w_spec(half, 1),
                  pl.BlockSpec((seq, LANES), lambda k, j: (0, k * nc + j))],
        out_specs=[pair_spec(seq), pair_spec(3), pair_spec(1)],
        out_shape=[sd((half, 2, seq, c), BF16), sd((half, 2, 3, c), F32), sd((half, 2, 1, c), F32)],
        compiler_params=_cparams(),
    )(a, cw, cw, cb, cb, dm)


ROW_T = 256


def _local_step(x, positions, target, w, emit=lambda **grads: None):
    seq, d = x.shape
    t_row = min(ROW_T, seq)
    nrow = seq // t_row
    ssm_w = d // 2
    nj = ssm_w // LANES
    n_groups = ssm_w // SSM_GROUP
    nh = w["wuq"].shape[0]
    q_rank = w["wuq"].shape[1]
    kv_rank = w["wukv"].shape[1]
    ns = w["wup"].shape[0]
    c_ff = w["wup"].shape[2]
    in_pad = w["win"].shape[1]
    tm = min(1024, seq)
    nm = seq // tm
    sw = 2 * STATE_BLOCK
    g1 = (nrow,)

    lr3 = w["lam_re"].reshape(n_groups, 1, SSM_STATE)
    li3 = w["lam_im"].reshape(n_groups, 1, SSM_STATE)
    ldt3 = w["log_dt"].reshape(n_groups, 1, 1)
    bt_re = jnp.swapaxes(w["b_re"].reshape(n_groups, SSM_STATE, SSM_GROUP), 1, 2)
    bt_im = jnp.swapaxes(w["b_im"].reshape(n_groups, SSM_STATE, SSM_GROUP), 1, 2)
    abar_re, abar_im, bbt_re, bbt_im = _s5_prep(lr3, li3, ldt3, bt_re, bt_im)
    eye = jnp.eye(GROUPS_PER_BLOCK, dtype=F32)

    def blockdiag_in(bb):
        t = bb.reshape(nj, GROUPS_PER_BLOCK, SSM_GROUP, SSM_STATE)
        return jnp.einsum("jghp,gk->jghkp", t, eye).reshape(nj, LANES, STATE_BLOCK)

    def blockdiag_in_t(dwb):
        t = dwb.reshape(nj, GROUPS_PER_BLOCK, SSM_GROUP, GROUPS_PER_BLOCK, SSM_STATE)
        return jnp.einsum("jghkp,gk->jghp", t, eye).reshape(n_groups, SSM_GROUP, SSM_STATE)

    def blockdiag_out(cc):
        t = cc.reshape(nj, GROUPS_PER_BLOCK, SSM_GROUP, SSM_STATE)
        return jnp.einsum("jghp,gk->jkpgh", t, eye).reshape(nj, STATE_BLOCK, LANES)

    def blockdiag_out_t(dwc):
        t = dwc.reshape(nj, GROUPS_PER_BLOCK, SSM_STATE, GROUPS_PER_BLOCK, SSM_GROUP)
        return jnp.einsum("jkpgh,gk->jghp", t, eye).reshape(n_groups, SSM_GROUP, SSM_STATE)

    c_re = w["c_re"].reshape(n_groups, SSM_GROUP, SSM_STATE)
    c_im = w["c_im"].reshape(n_groups, SSM_GROUP, SSM_STATE)
    wb = jnp.concatenate([blockdiag_in(bbt_re), blockdiag_in(bbt_im)], axis=2).astype(BF16)
    wc = jnp.concatenate([blockdiag_out(c_re), -blockdiag_out(c_im)], axis=1).astype(BF16)
    a_lay = jnp.concatenate([abar_re.reshape(nj, 1, STATE_BLOCK), abar_im.reshape(nj, 1, STATE_BLOCK)],
                            axis=1).reshape(1, nj * sw)

    attn_w = w["attn_norm"]
    hn = _blockwise("norm1", lambda xb, wv: _rms(xb, wv), [x, attn_w], [_row_spec(t_row, d), _full_spec((1, d))],
                    [((seq, d), BF16)], [_row_spec(t_row, d)], g1)[0]
    proj = _mm2d("proj", hn, w["win"], NN, F32, tn=640)

    s_all, ylin = _ssm_fwd(proj, wb, wc, a_lay)

    def glu_fwd_fn(yl, ub, dsk, wg, bg):
        yp = yl + dsk * ub
        ygv = jax.nn.gelu(yp)
        ygb = ygv.astype(BF16)
        zb = jnp.dot(ygb, wg, preferred_element_type=F32) + bg
        return yp, ygb, zb, ygv * jax.nn.sigmoid(zb)

    t_wide = min(2 * t_row, seq)
    g_wide = (seq // t_wide,)
    wide = pl.BlockSpec((t_wide, ssm_w), lambda i: (i, 0))
    y_pre, yg, z, y_ssm = _blockwise(
        "ssm_glu_fwd", glu_fwd_fn, [ylin, proj, w["ssm_d"], w["wglu"], w["b_glu"]],
        [wide, wide, _full_spec((1, ssm_w)), _full_spec((ssm_w, ssm_w), single=True), _full_spec((1, ssm_w))],
        [((seq, ssm_w), F32), ((seq, ssm_w), BF16), ((seq, ssm_w), F32), ((seq, ssm_w), F32)], [wide] * 4, g_wide)

    cq_off, ckv_off, kpe_off = ssm_w, ssm_w + q_rank, ssm_w + q_rank + kv_rank
    assert cq_off % q_rank == 0 and ckv_off % kv_rank == 0 and kpe_off % LANES == 0
    cq_spec = pl.BlockSpec((t_row, q_rank), lambda i: (i, cq_off // q_rank))
    ckv_spec = pl.BlockSpec((t_row, kv_rank), lambda i: (i, ckv_off // kv_rank))
    kpe_spec = pl.BlockSpec((t_row, LANES), lambda i: (i, kpe_off // LANES))
    pos_b = jnp.broadcast_to(positions.astype(F32)[:, None], (seq, LANES))
    inv_freq = ROPE_THETA ** (-jnp.arange(0, QK_ROPE, 2, dtype=F32) / QK_ROPE)
    inv128 = jnp.tile(inv_freq, 4).reshape(1, LANES)

    def mla_prep_fn(cq, ckv, kp, pb, inv, wq, wkv):
        ang = pb * inv
        lane = lax.broadcasted_iota(jnp.int32, ang.shape, 1)
        cs, sn = jnp.cos(ang), jnp.sin(ang)
        cos = jnp.where(lane < QK_ROPE, cs, 0.0)
        sa = jnp.where(lane < QK_ROPE // 2, -sn, 0.0)
        sb = jnp.where(jnp.logical_and(lane >= QK_ROPE // 2, lane < QK_ROPE), sn, 0.0)
        return _rms(cq, wq), _rms(ckv, wkv), _rope128(kp, cos, sa, sb), cos, sa, sb

    qn, kvn, kpe, cos_t, sa_t, sb_t = _blockwise(
        "mla_prep", mla_prep_fn, [proj, proj, proj, pos_b, inv128, w["q_norm"], w["kv_norm"]],
        [cq_spec, ckv_spec, kpe_spec, _row_spec(t_row, LANES),
         _full_spec((1, LANES)), _full_spec((1, q_rank)), _full_spec((1, kv_rank))],
        [((seq, q_rank), BF16), ((seq, kv_rank), BF16), ((seq, LANES), BF16)] + [((seq, LANES), F32)] * 3,
        [_row_spec(t_row, q_rank), _row_spec(t_row, kv_rank)] + [_row_spec(t_row, LANES)] * 4, g1)

    def head_mm(name, act, wh, out_dtype):
        kdim, ndim = wh.shape[1], wh.shape[2]
        return _mm(name, act, wh, grid=(nh, 1, 1),
                   a_spec=pl.BlockSpec((seq, kdim), lambda h, i, k: (i, 0)),
                   b_spec=pl.BlockSpec((None, kdim, ndim), lambda h, i, k: (h, 0, 0)),
                   o_spec=pl.BlockSpec((None, seq, ndim), lambda h, i, k: (h, i, 0)),
                   out_shape=(nh, seq, ndim), out_dtype=out_dtype)

    q_raw = head_mm("mla_q", qn, w["wuq"], F32)
    kv = head_mm("mla_kv", kvn, w["wukv"], BF16)
    y_mla = _attn_fwd(q_raw, kv, kpe, cos_t, sa_t, sb_t)
    mla_w = nh * V_DIM

    def outnorm_fn(ys, ym, ws, wm):
        return jnp.concatenate([_rms(ys, ws), _rms(ym, wm)], axis=1)

    ycat = _blockwise("out_norm", outnorm_fn, [y_ssm, y_mla, w["son"], w["mon"]],
                      [_row_spec(t_row, ssm_w), _row_spec(t_row, mla_w), _full_spec((1, ssm_w)), _full_spec((1, mla_w))],
                      [((seq, d), BF16)], [_row_spec(t_row, d)], g1)[0]
    h1 = _mm2d("out_proj", ycat, w["wout"], NN, F32, res=x)

    hn2 = _blockwise("norm2", lambda hb, wv: _rms(hb, wv), [h1, w["ffn_norm"]],
                     [_row_spec(t_row, d), _full_spec((1, d))], [((seq, d), BF16)], [_row_spec(t_row, d)], g1)[0]
    tku = d
    half = ns // 2
    a_ff = _mm("ffn_up", hn2, w["wup"], grid=(ns, nm, d // tku),
               a_spec=pl.BlockSpec((tm, tku), lambda s, i, k: (i, k)),
               b_spec=pl.BlockSpec((None, tku, c_ff), lambda s, i, k: (s, k, 0)),
               o_spec=pl.BlockSpec((None, None, tm, c_ff), lambda s, i, k: (s % half, s // half, i, 0)),
               out_shape=(half, 2, seq, c_ff), out_dtype=F32)
    cb3 = w["conv_b"].reshape(ns, 1, c_ff)
    m_ff = _conv_gate_fwd(a_ff, w["conv_w"], cb3)
    d_ff = half * c_ff
    wdn = w["wdown"]
    tnd = _tile(d, 1024)
    tmx, tnx = min(1024, seq), _tile(d, 1024)
    h2 = _mm2d("ffn_down", m_ff, wdn, NN, F32, tm=512, tn=512, tk=d_ff, res=h1)

    def loss_fn(hb, tb, wv):
        def f(hh, ww):
            err = _rms(hh, ww) - tb
            return 0.5 * jnp.sum(jnp.mean(err * err, axis=-1))

        lossv, (dh, dw) = jax.value_and_grad(f, argnums=(0, 1))(hb, wv)
        return dh, dh, jnp.full((1, LANES), lossv, F32), dw

    fin_w = w["final_norm"].reshape(1, d)
    dh2, dh2b, loss_acc, g_final = _blockwise(
        "loss_head", loss_fn, [h2, target, fin_w], [_row_spec(t_row, d), _row_spec(t_row, d), _full_spec((1, d))],
        [((seq, d), F32), ((seq, d), BF16), ((1, LANES), F32), ((1, d), F32)],
        [_row_spec(t_row, d), _row_spec(t_row, d), _full_spec((1, LANES)), _full_spec((1, d))], g1, n_acc=2)
    loss = loss_acc

    dm = _mm2d("ffn_down_dx", dh2b, wdn, NT, F32, tn=c_ff)
    tks = seq
    g_wdown = _mm2d("ffn_down_dw", m_ff, dh2b, TN, BF16, tm=c_ff)
    emit(wdown=g_wdown)
    da_ff, g_convw2, g_convb2 = _conv_gate_bwd(a_ff, w["conv_w"], cb3, dm)
    g_convw = jnp.swapaxes(g_convw2, 0, 1).reshape(ns, 3, c_ff)
    g_convb = jnp.swapaxes(g_convb2, 0, 1).reshape(ns, 1, c_ff)
    g_wup = _mm("ffn_up_dw", hn2, da_ff, grid=(ns, d // tnd, seq // tks), contract=TN,
                a_spec=pl.BlockSpec((tks, tnd), lambda s, j, k: (k, j)),
                b_spec=pl.BlockSpec((None, None, tks, c_ff), lambda s, j, k: (s % half, s // half, k, 0)),
                o_spec=pl.BlockSpec((None, tnd, c_ff), lambda s, j, k: (s, j, 0)),
                out_shape=(ns, d, c_ff), out_dtype=BF16)
    emit(wup=g_wup)
    dhn2 = _mm("ffn_up_dx", da_ff, w["wup"], grid=(seq // tmx, d // tnx, ns), contract=NT,
               a_spec=pl.BlockSpec((None, None, tmx, c_ff), lambda i, j, s: (s % half, s // half, i, 0)),
               b_spec=pl.BlockSpec((None, tnx, c_ff), lambda i, j, s: (s, j, 0)),
               o_spec=pl.BlockSpec((tmx, tnx), lambda i, j, s: (i, j)),
               out_shape=(seq, d), out_dtype=F32)
    emit(wup_pair_sums_after=dhn2)

    def norm_bwd_fn(hb, dres, dn, wv):
        dx_, dw_ = _rms_bwd(hb, wv, dn)
        dtot = dres + dx_
        return dtot, dtot, dw_

    dh1, dh1b, g_ffn_norm = _blockwise(
        "norm2_bwd", norm_bwd_fn, [h1, dh2, dhn2, w["ffn_norm"]],
        [_row_spec(t_row, d)] * 3 + [_full_spec((1, d))],
        [((seq, d), F32), ((seq, d), BF16), ((1, d), F32)],
        [_row_spec(t_row, d), _row_spec(t_row, d), _full_spec((1, d))], g1, n_acc=1)

    g_wout = _mm2d("out_proj_dw", ycat, dh1b, TN, BF16)

    def outnorm_bwd_fn(dhb, wo, ys, ym, ws, wm):
        dyc = lax.dot_general(dhb, wo, (NT, ((), ())), preferred_element_type=F32)
        dys, dws = _rms_bwd(ys, ws, dyc[:, :ssm_w])
        dym, dwm = _rms_bwd(ym, wm, dyc[:, ssm_w:])
        return dys, dym, dws, dwm

    dy_ssm, dy_mla, g_son, g_mon = _blockwise(
        "out_proj_dx_norm_bwd", outnorm_bwd_fn, [dh1b, w["wout"], y_ssm, y_mla, w["son"], w["mon"]],
        [_row_spec(t_wide, d), _full_spec((d, d), single=True), wide, _row_spec(t_wide, mla_w),
         _full_spec((1, ssm_w)), _full_spec((1, mla_w))],
        [((seq, ssm_w), F32), ((seq, mla_w), F32), ((1, ssm_w), F32), ((1, mla_w), F32)],
        [wide, _row_spec(t_wide, mla_w), _full_spec((1, ssm_w)), _full_spec((1, mla_w))],
        g_wide, n_acc=2)

    def glu_bwd_fn(dy, yp, zb, ub, dsk, wg):
        ygv = jax.nn.gelu(yp)
        sg = jax.nn.sigmoid(zb)
        dz = dy * ygv * sg * (1.0 - sg)
        dzb = dz.astype(BF16)
        dyg = dy * sg + lax.dot_general(dzb, wg, (NT, ((), ())), preferred_element_type=F32)
        _, vjp = jax.vjp(jax.nn.gelu, yp)
        dyp = vjp(dyg)[0]
        return (dzb, dyp, dyp * dsk, jnp.sum(dz, axis=0, keepdims=True), jnp.sum(dyp * ub, axis=0, keepdims=True))

    dz, dy_pre, du1, g_bglu, g_ssmd = _blockwise(
        "ssm_glu_bwd", glu_bwd_fn, [dy_ssm, y_pre, z, proj, w["ssm_d"], w["wglu"]],
        [wide] * 4 + [_full_spec((1, ssm_w)), _full_spec((ssm_w, ssm_w), single=True)],
        [((seq, ssm_w), BF16), ((seq, ssm_w), BF16), ((seq, ssm_w), F32), ((1, ssm_w), F32), ((1, ssm_w), F32)],
        [wide] * 3 + [_full_spec((1, ssm_w))] * 2, g_wide, n_acc=2)
    g_wglu = _mm2d("ssm_glu_dw", yg, dz, TN, BF16)
    dq_raw, dkv, dkp_h = _attn_bwd(q_raw, kv, kpe, cos_t, sa_t, sb_t, dy_mla)

    def head_mm_dx(name, dact, wh):
        kdim, ndim = wh.shape[1], wh.shape[2]
        return _mm(name, dact, wh, grid=(1, 1, nh), contract=NT,
                   a_spec=pl.BlockSpec((None, seq, ndim), lambda i, j, h: (h, i, 0)),
                   b_spec=pl.BlockSpec((None, kdim, ndim), lambda i, j, h: (h, 0, 0)),
                   o_spec=pl.BlockSpec((seq, kdim), lambda i, j, h: (i, 0)),
                   out_shape=(seq, kdim), out_dtype=F32)

    def head_mm_dw(name, act, dact):
        kdim, ndim = act.shape[1], dact.shape[2]
        return _mm(name, act, dact, grid=(nh, 1, seq // tks), contract=TN,
                   a_spec=pl.BlockSpec((tks, kdim), lambda h, j, k: (k, 0)),
                   b_spec=pl.BlockSpec((None, tks, ndim), lambda h, j, k: (h, k, 0)),
                   o_spec=pl.BlockSpec((None, kdim, ndim), lambda h, j, k: (h, 0, 0)),
                   out_shape=(nh, kdim, ndim), out_dtype=BF16)

    g_wuq = head_mm_dw("mla_q_dw", qn, dq_raw)
    g_wukv = head_mm_dw("mla_kv_dw", kvn, dkv)
    dqn = head_mm_dx("mla_q_dx", dq_raw, w["wuq"])
    dkvn = head_mm_dx("mla_kv_dx", dkv, w["wukv"])
    emit(not_before=(dqn, dkvn, dy_pre), wout=g_wout, wuq=g_wuq, wukv=g_wukv, wglu=g_wglu, conv_w=g_convw)

    du, dwb, dwc, da_lay = _ssm_bwd(dy_pre, s_all, proj, du1, wb, wc, a_lay)
    g_c_re = blockdiag_out_t(dwc[:, :STATE_BLOCK, :])
    g_c_im = -blockdiag_out_t(dwc[:, STATE_BLOCK:, :])
    dbbt_re = blockdiag_in_t(dwb[:, :, :STATE_BLOCK])
    dbbt_im = blockdiag_in_t(dwb[:, :, STATE_BLOCK:])
    da3 = da_lay.reshape(nj, 2, STATE_BLOCK)
    dabar_re = da3[:, 0, :].reshape(n_groups, 1, SSM_STATE)
    dabar_im = da3[:, 1, :].reshape(n_groups, 1, SSM_STATE)
    g_lr3, g_li3, g_ldt3, g_bt_re, g_bt_im = _s5_prep_bwd(lr3, li3, ldt3, bt_re, bt_im,
                                                           dabar_re, dabar_im, dbbt_re, dbbt_im)

    def mla_prep_bwd_fn(cq, ckv, dqn_b, dkvn_b, dkp_b, cos, sa, sb, wq, wkv):
        dcq, dwq = _rms_bwd(cq, wq, dqn_b)
        dckv, dwkv = _rms_bwd(ckv, wkv, dkvn_b)
        dkp_sum = dkp_b[0]
        for h in range(1, nh):
            dkp_sum = dkp_sum + dkp_b[h]
        return dcq, dckv, _rope128_t(dkp_sum, cos, sa, sb), dwq, dwkv

    dc_q, dc_kv, dkpe_raw, g_qnorm, g_kvnorm = _blockwise(
        "mla_prep_bwd", mla_prep_bwd_fn, [proj, proj, dqn, dkvn, dkp_h, cos_t, sa_t, sb_t, w["q_norm"], w["kv_norm"]],
        [cq_spec, ckv_spec, _row_spec(t_row, q_rank), _row_spec(t_row, kv_rank),
         pl.BlockSpec((nh, t_row, LANES), lambda i: (0, i, 0))] + [_row_spec(t_row, LANES)] * 3
        + [_full_spec((1, q_rank)), _full_spec((1, kv_rank))],
        [((seq, q_rank), BF16), ((seq, kv_rank), BF16), ((seq, LANES), BF16), ((1, q_rank), F32), ((1, kv_rank), F32)],
        [_row_spec(t_row, q_rank), _row_spec(t_row, kv_rank), _row_spec(t_row, LANES), _full_spec((1, q_rank)),
         _full_spec((1, kv_rank))], g1, n_acc=2)

    dproj = jnp.concatenate([du, dc_q, dc_kv, dkpe_raw], axis=1)
    g_win = _mm2d("proj_dw", hn, dproj, TN, BF16, tn=640)
    emit(win=g_win)
    def norm1_bwd_fn(dpb, wi, xb, dres, wv):
        dn = lax.dot_general(dpb, wi, (NT, ((), ())), preferred_element_type=F32)
        dx_, dw_ = _rms_bwd(xb, wv, dn)
        return dres + dx_, dw_

    grad_x, g_attn_norm = _blockwise(
        "proj_dx_norm1_bwd", norm1_bwd_fn, [dproj, w["win"], x, dh1, attn_w],
        [_row_spec(t_row, in_pad), _full_spec((d, in_pad), single=True), _row_spec(t_row, d), _row_spec(t_row, d), _full_spec((1, d))],
        [((seq, d), F32), ((1, d), F32)], [_row_spec(t_row, d), _full_spec((1, d))], g1, n_acc=1)
    emit(win_pair_sums_after=grad_x)

    grads = dict(
        attn_norm=g_attn_norm, win=g_win, lam_re=g_lr3, lam_im=g_li3, log_dt=g_ldt3,
        bt_re=g_bt_re, bt_im=g_bt_im, c_re=g_c_re, c_im=g_c_im,
        ssm_d=g_ssmd, wglu=g_wglu, b_glu=g_bglu, q_norm=g_qnorm, wuq=g_wuq, kv_norm=g_kvnorm, wukv=g_wukv,
        son=g_son, mon=g_mon, wout=g_wout, ffn_norm=g_ffn_norm, wup=g_wup, conv_w=g_convw, conv_b=g_convb,
        wdown=g_wdown, final_norm=g_final)
    return loss, grad_x, grads


def _mesh_pos():
    return lax.axis_index("x"), lax.axis_index("y"), lax.axis_index("c")


def _handshake_all():
    x, y, c = _mesh_pos()
    barrier = pltpu.get_barrier_semaphore()
    for k in range(1, N_DEV):
        peer = (1 - x if k & 4 else x, 1 - y if k & 2 else y, 1 - c if k & 1 else c)
        pl.semaphore_signal(barrier, inc=1, device_id=peer, device_id_type=MESH)
    pl.semaphore_wait(barrier, N_DEV - 1)


def _handshake(peers):
    barrier = pltpu.get_barrier_semaphore()
    for peer in peers:
        pl.semaphore_signal(barrier, inc=1, device_id=peer, device_id_type=MESH)
    pl.semaphore_wait(barrier, len(peers))


def _comm_call(name, body, n, out_shape, ins, collective_id, after=None, copies=7, n_remote=None, n_local=None):
    n_remote = copies * n if n_remote is None else n_remote
    sems = [pltpu.SemaphoreType.DMA((n_remote,)), pltpu.SemaphoreType.DMA((n_remote,)),
            pltpu.SemaphoreType.DMA((n if n_local is None else n_local,))]
    if collective_id is None:
        any_spec = pl.BlockSpec(memory_space=pl.ANY)
        return pl.pallas_call(body, name=name, out_shape=out_shape, in_specs=[any_spec] * n,
                              out_specs=[any_spec] * n, scratch_shapes=sems)(*ins)
    seq_body = body
    if after:
        n_after = len(after)
        ins = list(ins) + list(after)

        def seq_body(*refs):
            body(*refs[:n], *refs[n + n_after:])

    return pl.kernel(seq_body, name=name, out_type=out_shape,
                     mesh=plsc.ScalarSubcoreMesh(axis_name="seq", num_cores=1), scratch_types=sems,
                     compiler_params=pltpu.CompilerParams(collective_id=collective_id))(*ins)


def _all_gather(name, xs, collective_id=None, after=None, pair_sums=()):
    n = len(xs)
    nh = len(pair_sums)
    m = n + nh

    def body(*refs):
        x_refs, h_refs, o_refs, e_refs = refs[:n], refs[n:m], refs[m:m + n], refs[m + n:2 * m]
        send_sems, recv_sems, local_sems = refs[2 * m:]
        if collective_id is not None:
            _handshake_all()
        finish_pairs = _chip_copies(h_refs, e_refs, send_sems, recv_sems, local_sems, 7 * n, n) if nh else None
        x, y, c = _mesh_pos()
        me, sibling = (x, y, c), (x, y, 1 - c)
        chips = [(1 - x, y), (x, 1 - y), (1 - x, 1 - y)]

        def slot(o_ref, px, py, pc):
            return o_ref.at[4 * px + 2 * py + pc]

        def copy(t, k, block, to, src=None):
            dst = slot(o_refs[t], *block)
            return pltpu.make_async_remote_copy(
                src_ref=dst if src is None else src, dst_ref=dst,
                send_sem=send_sems.at[7 * t + k], recv_sem=recv_sems.at[7 * t + k],
                device_id=to, device_id_type=MESH)

        started = []
        for t in range(n):
            mine = pltpu.make_async_copy(x_refs[t], slot(o_refs[t], *me), local_sems.at[t])
            mine.start()
            started.append(mine)
        first = []
        for t in range(n):
            first.append(copy(t, 0, me, sibling, src=x_refs[t]))
            first += [copy(t, 1 + j, me, (*chip, c), src=x_refs[t]) for j, chip in enumerate(chips)]
        for cp in first:
            cp.start()
        passed = []
        for j, chip in enumerate(chips):
            for t in range(n):
                copy(t, 1 + j, (*chip, c), me).wait_recv()
                fwd = copy(t, 4 + j, (*chip, c), sibling)
                fwd.start()
                passed.append(fwd)
        for t in range(n):
            copy(t, 0, sibling, me).wait_recv()
            for j, chip in enumerate(chips):
                copy(t, 4 + j, (*chip, 1 - c), me).wait_recv()
        for cp in first + passed:
            cp.wait_send()
        for mine in started:
            mine.wait()
        if nh:
            finish_pairs()

    out_shape = ([jax.ShapeDtypeStruct((N_DEV,) + v.shape, v.dtype) for v in xs]
                 + [jax.ShapeDtypeStruct(v.shape, v.dtype) for v in pair_sums])
    return _comm_call(name, body, m, out_shape, list(xs) + list(pair_sums), collective_id, after,
                      n_remote=7 * n + (N_CHIP - 1) * nh, n_local=m)


def _exchange_partials(name, gs, collective_id=None, after=None):
    n = len(gs)

    def body(*refs):
        g_refs, o_refs = refs[:n], refs[n:2 * n]
        send_sems, recv_sems, local_sems = refs[2 * n:]
        if collective_id is not None:
            _handshake_all()
        x, y, c = _mesh_pos()
        me_idx = 4 * x + 2 * y + c
        copies = []
        for t in range(n):
            mine = pltpu.make_async_copy(g_refs[t].at[me_idx], o_refs[t].at[me_idx], local_sems.at[t])
            mine.start()
            copies.append(mine)
        remote = []
        for k in range(1, N_DEV):
            px = 1 - x if k & 4 else x
            py = 1 - y if k & 2 else y
            pc = 1 - c if k & 1 else c
            p_idx = 4 * px + 2 * py + pc
            for t in range(n):
                cp = pltpu.make_async_remote_copy(
                    src_ref=g_refs[t].at[p_idx], dst_ref=o_refs[t].at[me_idx],
                    send_sem=send_sems.at[7 * t + k - 1], recv_sem=recv_sems.at[7 * t + k - 1],
                    device_id=(px, py, pc), device_id_type=MESH)
                cp.start()
                landing = pltpu.make_async_remote_copy(
                    src_ref=g_refs[t].at[p_idx], dst_ref=o_refs[t].at[p_idx],
                    send_sem=send_sems.at[7 * t + k - 1], recv_sem=recv_sems.at[7 * t + k - 1],
                    device_id=(px, py, pc), device_id_type=MESH)
                remote.append((cp, landing))
        for cp, landing in remote:
            landing.wait_recv()
        for cp, landing in remote:
            cp.wait_send()
        for mine in copies:
            mine.wait()

    out_shape = [jax.ShapeDtypeStruct(v.shape, v.dtype) for v in gs]
    return _comm_call(name, body, n, out_shape, gs, collective_id, after)


N_CHIP = N_DEV // 2
PAIR_ADD_BLOCK_ELEMS = 1024 * 1024


def _pair_swap(name, gs, collective_id, after=None):
    n = len(gs)

    def body(*refs):
        g_refs, o_refs = refs[:n], refs[n:2 * n]
        send_sems, recv_sems, _ = refs[2 * n:]
        x, y, c = _mesh_pos()
        sibling = (x, y, 1 - c)
        _handshake([sibling])
        copies = []
        for t in range(n):
            for k in range(N_CHIP):
                copies.append(pltpu.make_async_remote_copy(
                    src_ref=g_refs[t].at[2 * k + 1 - c], dst_ref=o_refs[t].at[k],
                    send_sem=send_sems.at[N_CHIP * t + k], recv_sem=recv_sems.at[N_CHIP * t + k],
                    device_id=sibling, device_id_type=MESH))
        for cp in copies:
            cp.start()
        for cp in copies:
            cp.wait_recv()
        for cp in copies:
            cp.wait_send()

    out_shape = [jax.ShapeDtypeStruct((N_CHIP,) + v.shape[1:], v.dtype) for v in gs]
    return _comm_call(name, body, n, out_shape, gs, collective_id, after, copies=N_CHIP)


def _pair_add(name, g, got):
    _, r, c = g.shape
    tr = r
    if r * c > PAIR_ADD_BLOCK_ELEMS and r % SUBLANES == 0:
        tr = SUBLANES
        while r % (tr * 2) == 0 and tr * 2 * c <= PAIR_ADD_BLOCK_ELEMS:
            tr *= 2

    def body(core_ref, g_ref, got_ref, o_ref):
        o_ref[...] = (g_ref[...].astype(F32) + got_ref[...].astype(F32)).astype(o_ref.dtype)

    grid_spec = pltpu.PrefetchScalarGridSpec(
        num_scalar_prefetch=1, grid=(N_CHIP, r // tr),
        in_specs=[pl.BlockSpec((None, None, tr, c), lambda k, i, core: (k, core[0], i, 0)),
                  pl.BlockSpec((None, tr, c), lambda k, i, core: (k, i, 0))],
        out_specs=pl.BlockSpec((None, tr, c), lambda k, i, core: (k, i, 0)))
    core = lax.axis_index("c").astype(jnp.int32).reshape(1)
    return pl.pallas_call(body, name=name, grid_spec=grid_spec, out_shape=jax.ShapeDtypeStruct((N_CHIP, r, c), g.dtype),
                          compiler_params=_cparams())(core, g.reshape(N_CHIP, 2, r, c), got)


def _chip_copies(h_refs, o_refs, send_sems, recv_sems, local_sems, sem0, local0):
    n = len(h_refs)
    per = N_CHIP - 1
    x, y, c = _mesh_pos()
    others = [(1 - x if k & 2 else x, 1 - y if k & 1 else y) for k in range(1, N_CHIP)]
    my_chip = 2 * x + y
    local = []
    for t in range(n):
        mine = pltpu.make_async_copy(h_refs[t].at[my_chip], o_refs[t].at[my_chip], local_sems.at[local0 + t])
        mine.start()
        local.append(mine)
    remote = []
    for j, (px, py) in enumerate(others):
        chip = 2 * px + py
        for t in range(n):
            sems = dict(send_sem=send_sems.at[sem0 + per * t + j], recv_sem=recv_sems.at[sem0 + per * t + j],
                        device_id=(px, py, c), device_id_type=MESH)
            cp = pltpu.make_async_remote_copy(src_ref=h_refs[t].at[chip], dst_ref=o_refs[t].at[my_chip], **sems)
            cp.start()
            landing = pltpu.make_async_remote_copy(src_ref=h_refs[t].at[chip], dst_ref=o_refs[t].at[chip], **sems)
            remote.append((cp, landing))

    def finish():
        for cp, landing in remote:
            landing.wait_recv()
        for cp, landing in remote:
            cp.wait_send()
        for mine in local:
            mine.wait()

    return finish


def _chip_exchange(name, hs, collective_id, after=None):
    n = len(hs)
    per = N_CHIP - 1

    def body(*refs):
        h_refs, o_refs = refs[:n], refs[n:2 * n]
        send_sems, recv_sems, local_sems = refs[2 * n:]
        x, y, c = _mesh_pos()
        _handshake([(1 - x if k & 2 else x, 1 - y if k & 1 else y, c) for k in range(1, N_CHIP)])
        _chip_copies(h_refs, o_refs, send_sems, recv_sems, local_sems, 0, 0)()

    out_shape = [jax.ShapeDtypeStruct(v.shape, v.dtype) for v in hs]
    return _comm_call(name, body, n, out_shape, hs, collective_id, after, copies=per)


ADAM_BLOCK_ELEMS = 128 * 1024


def _sum_parts(pb):
    g = pb[0].astype(F32)
    for j in range(1, pb.shape[0]):
        g = g + pb[j].astype(F32)
    return g


def _adam_math(g, wb_, mb, vb):
    m_new = ADAM_B1 * mb + (1.0 - ADAM_B1) * g
    v_new = ADAM_B2 * vb + (1.0 - ADAM_B2) * (g * g)
    m_hat = m_new / (1.0 - ADAM_B1 ** ADAM_STEP)
    v_hat = v_new / (1.0 - ADAM_B2 ** ADAM_STEP)
    delta = -ADAM_LR * (m_hat / (jnp.sqrt(v_hat) + ADAM_EPS) + ADAM_WD * wb_)
    return g, delta, m_new, v_new


def _adamw_multi(name, items, nblk=1, packed=None):
    n = len(items)

    def spec(shape, lead):
        blk = list(shape)
        blk[lead + 1] = shape[lead + 1] // nblk
        if nblk == 1:
            return pl.BlockSpec(tuple(blk), lambda i, nd=len(shape): (0,) * nd)
        return pl.BlockSpec(tuple(blk), lambda i, nd=len(shape), ax=lead + 1: (0,) * ax + (i,) + (0,) * (nd - ax - 1))

    ins, in_specs, out_specs, out_shape, where = [], [], [], [], []
    if packed is not None:
        ins.append(packed)
        in_specs.append(spec(packed.shape, 1))
    for parts, wv, mv, vv in items:
        if isinstance(parts, int):
            where.append((0, parts, len(ins)))
        else:
            assert parts.shape[1:] == wv.shape, (name, parts.shape, wv.shape)
            where.append((len(ins), None, len(ins) + 1))
            ins.append(parts)
            in_specs.append(spec(parts.shape, 1))
        ins += [wv, mv, vv]
        in_specs += [spec(wv.shape, 0)] * 3
        out_specs += [spec(wv.shape, 0)] * 4
        out_shape += [jax.ShapeDtypeStruct(wv.shape, F32)] * 4
    n_in = len(ins)

    def body(*refs):
        for t, (ip, off, iw) in enumerate(where):
            wr, mr, vr = refs[iw:iw + 3]
            parts = refs[ip][...] if off is None else refs[ip][:, :, off:off + wr.shape[-1]]
            res = _adam_math(_sum_parts(parts), wr[...], mr[...], vr[...])
            for o, val in zip(refs[n_in + 4 * t:n_in + 4 * t + 4], res):
                o[...] = val

    res = pl.pallas_call(body, name=name, grid=(nblk,), in_specs=in_specs, out_specs=out_specs, out_shape=out_shape,
                         compiler_params=_cparams())(*ins)
    return [tuple(res[4 * t:4 * t + 4]) for t in range(n)]


def _sum_multi(name, parts_list):
    def body(*refs):
        for pr, o in zip(refs[:len(parts_list)], refs[len(parts_list):]):
            o[...] = _sum_parts(pr[...])

    return pl.pallas_call(body, name=name, out_shape=[jax.ShapeDtypeStruct(p.shape[1:], F32) for p in parts_list],
                          compiler_params=_cparams())(*parts_list)


def _adamw_sum(name, parts, wv, mv, vv):
    npart, r, c = parts.shape
    tr = r
    if r * c > ADAM_BLOCK_ELEMS and r % SUBLANES == 0:
        tr = SUBLANES
        while r % (tr * 2) == 0 and tr * 2 * c <= ADAM_BLOCK_ELEMS:
            tr *= 2

    def fn(pb, wb_, mb, vb):
        return _adam_math(_sum_parts(pb), wb_, mb, vb)

    row = pl.BlockSpec((tr, c), lambda i: (i, 0))
    return _blockwise(name, fn, [parts, wv, mv, vv],
                      [pl.BlockSpec((npart, tr, c), lambda i: (0, i, 0)), row, row, row],
                      [((r, c), F32)] * 4, [row] * 4, (r // tr,))


_VECTORS = ["attn_norm", "lam_re", "lam_im", "log_dt", "ssm_d", "b_glu", "q_norm", "kv_norm", "son", "mon",
            "ffn_norm", "conv_b", "final_norm"]
_GHP = ["c_re", "c_im", "bt_re", "bt_im"]
_PACKED = ["attn_norm", "ssm_d", "b_glu", "q_norm", "kv_norm", "son", "mon", "ffn_norm", "conv_b", "final_norm"]
_BIG = ["win", "wglu", "wuq", "wukv", "wout", "wup", "wdown", "conv_w"]
_ROWS_IN_LANES = ("win", "wuq")
_TWO_LEVEL = ("wup", "win")
_AFTER = "_pair_sums_after"
_ORDER = ["attn_norm", "win", "lam_re", "lam_im", "log_dt", "b_re", "b_im", "c_re", "c_im", "ssm_d", "wglu",
          "b_glu", "q_norm", "wuq", "kv_norm", "wukv", "son", "mon", "wout", "ffn_norm", "wup", "conv_w",
          "conv_b", "wdown", "final_norm"]


def kernel(x, positions, attn_norm_w, w_in, ssm_lambda_re, ssm_lambda_im, ssm_log_dt, ssm_b_re, ssm_b_im, ssm_c_re, ssm_c_im, ssm_d, ssm_w_glu, ssm_b_glu, mla_q_norm_w, mla_w_uq, mla_kv_norm_w, mla_w_ukv, ssm_out_norm_w, mla_out_norm_w, w_out, ffn_norm_w, ffn_w_up, ffn_conv_w, ffn_conv_b, ffn_w_down, final_norm_w, loss_target, m_attn_norm_w, m_w_in, m_ssm_lambda_re, m_ssm_lambda_im, m_ssm_log_dt, m_ssm_b_re, m_ssm_b_im, m_ssm_c_re, m_ssm_c_im, m_ssm_d, m_ssm_w_glu, m_ssm_b_glu, m_mla_q_norm_w, m_mla_w_uq, m_mla_kv_norm_w, m_mla_w_ukv, m_ssm_out_norm_w, m_mla_out_norm_w, m_w_out, m_ffn_norm_w, m_ffn_w_up, m_ffn_conv_w, m_ffn_conv_b, m_ffn_w_down, m_final_norm_w, v_attn_norm_w, v_w_in, v_ssm_lambda_re, v_ssm_lambda_im, v_ssm_log_dt, v_ssm_b_re, v_ssm_b_im, v_ssm_c_re, v_ssm_c_im, v_ssm_d, v_ssm_w_glu, v_ssm_b_glu, v_mla_q_norm_w, v_mla_w_uq, v_mla_kv_norm_w, v_mla_w_ukv, v_ssm_out_norm_w, v_mla_out_norm_w, v_w_out, v_ffn_norm_w, v_ffn_w_up, v_ffn_conv_w, v_ffn_conv_b, v_ffn_w_down, v_final_norm_w):
    wts = dict(attn_norm=attn_norm_w, win=w_in, lam_re=ssm_lambda_re, lam_im=ssm_lambda_im, log_dt=ssm_log_dt,
               b_re=ssm_b_re, b_im=ssm_b_im, c_re=ssm_c_re, c_im=ssm_c_im, ssm_d=ssm_d, wglu=ssm_w_glu,
               b_glu=ssm_b_glu, q_norm=mla_q_norm_w, wuq=mla_w_uq, kv_norm=mla_kv_norm_w, wukv=mla_w_ukv,
               son=ssm_out_norm_w, mon=mla_out_norm_w, wout=w_out, ffn_norm=ffn_norm_w, wup=ffn_w_up,
               conv_w=ffn_conv_w, conv_b=ffn_conv_b, wdown=ffn_w_down, final_norm=final_norm_w)
    moms = dict(zip(_ORDER, [m_attn_norm_w, m_w_in, m_ssm_lambda_re, m_ssm_lambda_im, m_ssm_log_dt, m_ssm_b_re,
                             m_ssm_b_im, m_ssm_c_re, m_ssm_c_im, m_ssm_d, m_ssm_w_glu, m_ssm_b_glu, m_mla_q_norm_w,
                             m_mla_w_uq, m_mla_kv_norm_w, m_mla_w_ukv, m_ssm_out_norm_w, m_mla_out_norm_w, m_w_out,
                             m_ffn_norm_w, m_ffn_w_up, m_ffn_conv_w, m_ffn_conv_b, m_ffn_w_down, m_final_norm_w]))
    vels = dict(zip(_ORDER, [v_attn_norm_w, v_w_in, v_ssm_lambda_re, v_ssm_lambda_im, v_ssm_log_dt, v_ssm_b_re,
                             v_ssm_b_im, v_ssm_c_re, v_ssm_c_im, v_ssm_d, v_ssm_w_glu, v_ssm_b_glu, v_mla_q_norm_w,
                             v_mla_w_uq, v_mla_kv_norm_w, v_mla_w_ukv, v_ssm_out_norm_w, v_mla_out_norm_w, v_w_out,
                             v_ffn_norm_w, v_ffn_w_up, v_ffn_conv_w, v_ffn_conv_b, v_ffn_w_down, v_final_norm_w]))
    seq, d = x.shape[1], x.shape[2]
    in_width = w_in.shape[2]
    in_pad = -(-in_width // LANES) * LANES
    q_cols = mla_w_uq.shape[2]
    q_pad = 2 * LANES

    (win_g,) = _all_gather("gather_w_in", [jnp.pad(w_in[0], ((0, 0), (0, in_pad - in_width))).astype(BF16)])
    wglu_g, wuq_g, wukv_g, wout_g, convw_g = _all_gather(
        "gather_mix", [ssm_w_glu[0].astype(BF16), jnp.pad(mla_w_uq[0], ((0, 0), (0, q_pad - q_cols))).astype(BF16),
                       mla_w_ukv[0].astype(BF16), w_out[0].astype(BF16), ffn_conv_w[0]], collective_id=0)
    (wup_g,) = _all_gather("gather_ffn_up", [ffn_w_up[0].astype(BF16)], collective_id=1)
    (wdown_g,) = _all_gather("gather_ffn_down", [ffn_w_down[0].astype(BF16)], collective_id=2)
    ns = N_DEV
    c_ff = wup_g.shape[2]
    w = dict(
        attn_norm=attn_norm_w, win=win_g.reshape(d, in_pad), lam_re=ssm_lambda_re, lam_im=ssm_lambda_im,
        log_dt=ssm_log_dt, b_re=ssm_b_re, b_im=ssm_b_im, c_re=ssm_c_re, c_im=ssm_c_im, ssm_d=ssm_d,
        wglu=wglu_g.reshape(d // 2, d // 2), b_glu=ssm_b_glu, q_norm=mla_q_norm_w, wuq=wuq_g,
        kv_norm=mla_kv_norm_w, wukv=wukv_g, son=ssm_out_norm_w, mon=mla_out_norm_w, wout=wout_g.reshape(d, d),
        ffn_norm=ffn_norm_w, wup=wup_g, conv_w=convw_g, conv_b=ffn_conv_b,
        wdown=wdown_g.reshape(ns // 2 * c_ff, d), final_norm=final_norm_w)

    shard_layout = dict(
        win=lambda a: a[:, :in_width].reshape(N_DEV, d // N_DEV, in_width),
        wglu=lambda a: a.reshape(N_DEV, d // 2 // N_DEV, d // 2),
        wuq=lambda a: a[:, :, :q_cols], wukv=lambda a: a, wout=lambda a: a.reshape(N_DEV, d // N_DEV, d),
        wup=lambda a: a, wdown=lambda a: a.reshape(N_DEV, c_ff // 2, d), conv_w=lambda a: a)
    recv = {}
    next_id = [3]

    last = [None]

    out = {}

    def update(k):
        shp = wts[k].shape
        r, c = shp[-2], shp[-1]
        if k in _ROWS_IN_LANES:
            t = lambda a: jnp.swapaxes(a.reshape(-1, r, c), 1, 2)
            res = _adamw_sum("adamw_" + k, t(recv[k]), t(wts[k])[0], t(moms[k])[0], t(vels[k])[0])
            out[k] = [jnp.swapaxes(a, 0, 1).reshape(shp) for a in res]
            return res[0]
        res = _adamw_sum("adamw_" + k, recv[k].reshape(-1, r, c), wts[k].reshape(r, c),
                         moms[k].reshape(r, c), vels[k].reshape(r, c))
        out[k] = [a.reshape(shp) for a in res]
        return res[0]

    pending = {}

    def exchange(not_before=(), **grads):
        names = list(grads)
        if len(names) == 1 and names[0] in _TWO_LEVEL:
            k = names[0]
            parts = shard_layout[k](grads[k])
            got = _pair_swap("swap_" + k, [parts], collective_id=next_id[0], after=[last[0]])[0]
            next_id[0] += 1
            pending[k] = (parts, got)
            last[0] = got
            return
        if len(names) == 1 and names[0].endswith(_AFTER):
            k = names[0][:-len(_AFTER)]
            sums = _pair_add("pair_add_" + k, *pending[k])
            if k == "win":
                pending["tail"] = sums
                return
            recv[k] = _chip_exchange("exchange_" + k, [sums], collective_id=next_id[0],
                                     after=[last[0], grads[names[0]]])[0]
            next_id[0] += 1
            last[0] = recv[k]
            return
        got = _exchange_partials("exchange_" + "_".join(names), [shard_layout[k](grads[k]) for k in names],
                                 collective_id=next_id[0], after=[a for a in (last[0], *not_before) if a is not None])
        next_id[0] += 1
        last[0] = got[-1]
        recv.update(zip(names, got))

    loss_part, grad_x, g = _local_step(x[0], positions[0], loss_target[0], w, emit=exchange)
    n_groups = ssm_lambda_re.shape[1]
    two_d = {"lam_re": (n_groups, -1), "lam_im": (n_groups, -1)}
    dense = {k: g[k].reshape(two_d.get(k, (1, -1))) for k in _VECTORS}
    offsets, width = {}, 0
    for k in _PACKED:
        offsets[k] = width
        width += dense[k].shape[1]
    sent = dict(packed=jnp.concatenate([dense[k] for k in _PACKED], axis=1),
                **{k: dense[k] for k in _VECTORS if k not in _PACKED},
                **{k: g[k].reshape(n_groups, -1) for k in _GHP},
                loss=loss_part)
    names = list(sent)
    got = _all_gather("gather_small_grads", [sent[k] for k in names], collective_id=next_id[0], after=[last[0]],
                      pair_sums=[pending["tail"]])
    gathered = dict(zip(names, got))
    recv["win"] = got[len(names)]
    for k in _BIG:
        if k not in out and k != "win":
            update(k)
    update("win")

    def finish(keys, results):
        for k, res in zip(keys, results):
            out[k] = [a.reshape(wts[k].shape) for a in res]

    view = lambda k, a: a.reshape(dense[k].shape)
    finish(_VECTORS, _adamw_multi("adamw_vectors", [(offsets.get(k, gathered.get(k)), view(k, wts[k]), view(k, moms[k]),
                                                     view(k, vels[k])) for k in _VECTORS], packed=gathered["packed"]))
    summed = _GHP + ["loss"]
    sums = dict(zip(summed, _sum_multi("sum_ssm_bc_loss", [gathered[k] for k in summed])))
    loss = sums["loss"][0, 0]
    ghp = lambda k: sums[k].reshape(g[k].shape)
    t_hp = lambda a: jnp.swapaxes(a, 2, 3)
    bc_keys = ["c_re", "c_im", "b_re", "b_im"]
    items = [(ghp(k)[None, None], wts[k], moms[k], vels[k]) for k in bc_keys[:2]]
    items += [(ghp(t)[None, None], t_hp(wts[k]), t_hp(moms[k]), t_hp(vels[k]))
              for k, t in zip(bc_keys[2:], ("bt_re", "bt_im"))]
    res = _adamw_multi("adamw_ssm_bc", items)
    finish(bc_keys, res[:2] + [tuple(t_hp(a) for a in r) for r in res[2:]])

    grad_x = grad_x.reshape(x.shape)
    return (loss, grad_x, *[out[k][0] for k in _ORDER], *[out[k][1] for k in _ORDER],
            *[out[k][2] for k in _ORDER], *[out[k][3] for k in _ORDER])
```

```python
import functools
import math

import jax
import jax.numpy as jnp
from jax import lax
from jax.experimental import pallas as pl
from jax.experimental.pallas import tpu as pltpu
from jax.experimental.pallas import tpu_sc as plsc

F32 = jnp.float32
BF16 = jnp.bfloat16
MESH = pl.DeviceIdType.MESH

N_DEV = 8
LANES = 128
SUBLANES = 8
VMEM_LIMIT = 48 * 1024 * 1024

SSM_GROUP = 16
SSM_STATE = 64
GROUPS_PER_BLOCK = LANES // SSM_GROUP
STATE_BLOCK = GROUPS_PER_BLOCK * SSM_STATE
QK_NOPE = 128
QK_ROPE = 64
V_DIM = 128
ROPE_THETA = 10000.0
RMS_EPS = 1e-6

ADAM_LR = 0.001
ADAM_B1 = 0.9
ADAM_B2 = 0.999
ADAM_EPS = 1e-08
ADAM_WD = 0.01
ADAM_STEP = 10

NN = ((1,), (0,))
NT = ((1,), (1,))
TN = ((0,), (0,))


def _cparams():
    return pltpu.CompilerParams(vmem_limit_bytes=VMEM_LIMIT)


def _tile(n, want):
    if n <= want:
        return n
    t = (want // LANES) * LANES
    while t >= LANES:
        if n % t == 0:
            return t
        t -= LANES
    return n


def _mm(name, a, b, *, grid, a_spec, b_spec, o_spec, out_shape, out_dtype, contract=NN,
        res=None, res_spec=None):
    nk = grid[-1]
    kaxis = len(grid) - 1
    acc_shape = tuple(d for d in o_spec.block_shape if d is not None)

    def body(*refs):
        a_ref, b_ref = refs[:2]
        r_ref = None if res is None else refs[2]
        o_ref = refs[2 if res is None else 3]
        part = lax.dot_general(a_ref[...].astype(BF16), b_ref[...].astype(BF16),
                               (contract, ((), ())), preferred_element_type=F32)
        if nk == 1:
            if r_ref is not None:
                part = part + r_ref[...].astype(F32)
            o_ref[...] = part.astype(o_ref.dtype)
            return
        acc = refs[-1]
        k = pl.program_id(kaxis)

        @pl.when(k == 0)
        def _():
            acc[...] = part

        @pl.when(k != 0)
        def _():
            acc[...] += part

        @pl.when(k == nk - 1)
        def _():
            r = acc[...]
            if r_ref is not None:
                r = r + r_ref[...].astype(F32)
            o_ref[...] = r.astype(o_ref.dtype)

    ins = [a, b] + ([] if res is None else [res])
    in_specs = [a_spec, b_spec] + ([] if res is None else [res_spec])
    return pl.pallas_call(
        body, name=name, grid=grid, in_specs=in_specs, out_specs=o_spec,
        out_shape=jax.ShapeDtypeStruct(out_shape, out_dtype),
        scratch_shapes=[pltpu.VMEM(acc_shape, F32)] if nk > 1 else [], compiler_params=_cparams(),
    )(*ins)


def _mm2d(name, a, b, contract, out_dtype, tm=1024, tn=1024, tk=2048, res=None):
    if contract == NN:
        (m, kk), n = a.shape, b.shape[1]
    elif contract == NT:
        (m, kk), n = a.shape, b.shape[0]
    else:
        (kk, m), n = a.shape, b.shape[1]
    tm, tn, tk = _tile(m, tm), _tile(n, tn), _tile(kk, tk)
    grid = (m // tm, n // tn, kk // tk)
    if contract == TN:
        a_spec = pl.BlockSpec((tk, tm), lambda i, j, k: (k, i))
    else:
        a_spec = pl.BlockSpec((tm, tk), lambda i, j, k: (i, k))
    if contract == NT:
        b_spec = pl.BlockSpec((tn, tk), lambda i, j, k: (j, k))
    else:
        b_spec = pl.BlockSpec((tk, tn), lambda i, j, k: (k, j))
    o_spec = pl.BlockSpec((tm, tn), lambda i, j, k: (i, j))
    res_spec = None
    if res is not None:
        if res.shape[0] == 1:
            res_spec = pl.BlockSpec((1, tn), lambda i, j, k: (0, j))
        else:
            res_spec = pl.BlockSpec((tm, tn), lambda i, j, k: (i, j))
    return _mm(name, a, b, grid=grid, a_spec=a_spec, b_spec=b_spec, o_spec=o_spec,
               out_shape=(m, n), out_dtype=out_dtype, contract=contract, res=res, res_spec=res_spec)


def _blockwise(name, fn, ins, in_specs, outs, out_specs, grid, n_acc=0, acc_all=True):
    n_in, n_out = len(ins), len(outs)
    n_plain = n_out - n_acc

    def body(*refs):
        vals = fn(*[r[...] for r in refs[:n_in]])
        if not isinstance(vals, (tuple, list)):
            vals = (vals,)
        o_refs = refs[n_in:n_in + n_out]
        for r, v in zip(o_refs[:n_plain], vals[:n_plain]):
            r[...] = v.astype(r.dtype)
        if n_acc:
            if acc_all:
                first = functools.reduce(jnp.logical_and, [pl.program_id(d) == 0 for d in range(len(grid))])
            else:
                first = pl.program_id(len(grid) - 1) == 0

            @pl.when(first)
            def _():
                for r, v in zip(o_refs[n_plain:], vals[n_plain:]):
                    r[...] = v.astype(r.dtype)

            @pl.when(jnp.logical_not(first))
            def _():
                for r, v in zip(o_refs[n_plain:], vals[n_plain:]):
                    r[...] += v.astype(r.dtype)

    return pl.pallas_call(
        body, name=name, grid=grid, in_specs=in_specs, out_specs=out_specs,
        out_shape=[jax.ShapeDtypeStruct(s, d) for s, d in outs], compiler_params=_cparams(),
    )(*ins)


def _row_spec(t, c):
    return pl.BlockSpec((t, c), lambda i: (i, 0))


def _full_spec(shape, single=False):
    nd = len(shape)
    if single:
        return pl.BlockSpec(tuple(shape), lambda *g: (0,) * nd, pipeline_mode=pl.Buffered(1))
    return pl.BlockSpec(tuple(shape), lambda *g: (0,) * nd)


def _rms(xf, w):
    return xf * lax.rsqrt(jnp.mean(xf * xf, axis=-1, keepdims=True) + RMS_EPS) * w


def _rms_bwd(xf, w, dy):
    _, vjp = jax.vjp(_rms, xf, w)
    return vjp(dy)


def _s5_disc(lr, li, ldt, bre, bim):
    dt = jnp.exp(ldt)
    mag = jnp.exp(lr * dt)
    ar = mag * jnp.cos(li * dt)
    ai = mag * jnp.sin(li * dt)
    nr, ni = ar - 1.0, ai
    den = lr * lr + li * li
    zr = (nr * lr + ni * li) / den
    zi = (ni * lr - nr * li) / den
    return ar, ai, zr * bre - zi * bim, zr * bim + zi * bre


def _s5_prep(lr, li, ldt, bre, bim):
    def body(lr_r, li_r, ldt_r, bre_r, bim_r, ar_r, ai_r, br_r, bi_r):
        ar, ai, br, bi = _s5_disc(lr_r[...], li_r[...], ldt_r[...], bre_r[...], bim_r[...])
        ar_r[...] = ar
        ai_r[...] = ai
        br_r[...] = br
        bi_r[...] = bi

    sd = jax.ShapeDtypeStruct
    return pl.pallas_call(
        body, name="s5_prep",
        out_shape=[sd(lr.shape, F32), sd(lr.shape, F32), sd(bre.shape, F32), sd(bre.shape, F32)],
        compiler_params=_cparams(),
    )(lr, li, ldt, bre, bim)


def _s5_prep_bwd(lr, li, ldt, bre, bim, dar, dai, dbr, dbi):
    def body(lr_r, li_r, ldt_r, bre_r, bim_r, dar_r, dai_r, dbr_r, dbi_r, o0, o1, o2, o3, o4):
        _, vjp = jax.vjp(_s5_disc, lr_r[...], li_r[...], ldt_r[...], bre_r[...], bim_r[...])
        g = vjp((dar_r[...], dai_r[...], dbr_r[...], dbi_r[...]))
        for o, v in zip((o0, o1, o2, o3, o4), g):
            o[...] = v

    sd = jax.ShapeDtypeStruct
    return pl.pallas_call(
        body, name="s5_prep_bwd",
        out_shape=[sd(lr.shape, F32), sd(li.shape, F32), sd(ldt.shape, F32), sd(bre.shape, F32), sd(bim.shape, F32)],
        compiler_params=_cparams(),
    )(lr, li, ldt, bre, bim, dar, dai, dbr, dbi)


SCAN_T = 256


def _scan_tables(ar, ai, tab_r, tab_i, sub, reverse):
    pr, pi = ar, ai
    for k in range(sub):
        row = sub - 1 - k if reverse else k
        tab_r[row:row + 1, :] = pr
        tab_i[row:row + 1, :] = pi
        pr, pi = ar * pr - ai * pi, ar * pi + ai * pr


def _pack_matrix(t_blk, dtype):
    sub = t_blk // SUBLANES
    dst = jnp.arange(t_blk)
    src = (dst % SUBLANES) * sub + dst // SUBLANES
    return (src[:, None] == jnp.arange(t_blk)[None, :]).astype(dtype)


def _permute_rows_f32(pm, x):
    hi = x.astype(BF16)
    r1 = x - hi.astype(F32)
    mid = r1.astype(BF16)
    lo = (r1 - mid.astype(F32)).astype(BF16)
    dot = lambda v: jnp.dot(pm, v, preferred_element_type=F32)
    return dot(hi) + dot(mid) + dot(lo)


def _scan_block(x, loc, ar, ai, st, tab_r, tab_i, sub, reverse):
    hb = STATE_BLOCK
    a8r = jnp.broadcast_to(ar, (SUBLANES, hb))
    a8i = jnp.broadcast_to(ai, (SUBLANES, hb))
    sr = jnp.zeros((SUBLANES, hb), F32)
    si = jnp.zeros((SUBLANES, hb), F32)
    steps = range(sub - 1, -1, -1) if reverse else range(sub)
    for t in steps:
        rows = slice(t * SUBLANES, (t + 1) * SUBLANES)
        sr, si = a8r * sr - a8i * si + x[rows, :hb], a8r * si + a8i * sr + x[rows, hb:]
        loc[rows, :hb] = sr
        loc[rows, hb:] = si
    cr, ci = st[0:1, :], st[1:2, :]
    far = 0 if reverse else sub - 1
    fr, fi = tab_r[far:far + 1, :], tab_i[far:far + 1, :]
    ent_r, ent_i = [None] * SUBLANES, [None] * SUBLANES
    for c in (range(SUBLANES - 1, -1, -1) if reverse else range(SUBLANES)):
        ent_r[c], ent_i[c] = cr, ci
        cr, ci = sr[c:c + 1, :] + (fr * cr - fi * ci), si[c:c + 1, :] + (fr * ci + fi * cr)
    st[0:1, :] = cr
    st[1:2, :] = ci
    c8r = jnp.concatenate(ent_r, axis=0)
    c8i = jnp.concatenate(ent_i, axis=0)
    out = []
    for t in range(sub):
        rows = slice(t * SUBLANES, (t + 1) * SUBLANES)
        tr, ti = tab_r[t:t + 1, :], tab_i[t:t + 1, :]
        out.append(jnp.concatenate([loc[rows, :hb] + (tr * c8r - ti * c8i), loc[rows, hb:] + (tr * c8i + ti * c8r)],
                                   axis=1))
    return jnp.concatenate(out, axis=0)


SSM_BLOCKS_PER_STEP = 2


def _scan_scratch(nblk, t_blk, sub, hb):
    return [pltpu.VMEM((nblk, SUBLANES, hb), F32), pltpu.VMEM((nblk, sub, hb), F32), pltpu.VMEM((nblk, sub, hb), F32),
            pltpu.VMEM((nblk, t_blk, 2 * hb), F32)]


def _ssm_fwd(proj, wb, wc, a):
    seq = proj.shape[0]
    nj = wb.shape[0]
    w2 = 2 * STATE_BLOCK
    hb = STATE_BLOCK
    t_blk = min(SCAN_T, seq)
    sub = t_blk // SUBLANES
    pm = _pack_matrix(t_blk, BF16)

    npair = SSM_BLOCKS_PER_STEP

    def body(u_ref, wb_ref, wc_ref, a_ref, pm_ref, pmt_ref, s_ref, y_ref, st, tab_r, tab_i, loc):
        coef = [(a_ref[:, b * w2:b * w2 + hb], a_ref[:, b * w2 + hb:(b + 1) * w2]) for b in range(npair)]

        @pl.when(pl.program_id(1) == 0)
        def _():
            for b, (ar, ai) in enumerate(coef):
                st[b] = jnp.zeros((SUBLANES, hb), F32)
                _scan_tables(ar, ai, tab_r.at[b], tab_i.at[b], sub, False)

        for b, (ar, ai) in enumerate(coef):
            ub = u_ref[:, b * LANES:(b + 1) * LANES].astype(BF16)
            up = jnp.dot(pm_ref[...], ub, preferred_element_type=F32).astype(BF16)
            bu = jnp.dot(up, wb_ref[b], preferred_element_type=F32)
            s = _scan_block(bu, loc.at[b], ar, ai, st.at[b], tab_r.at[b], tab_i.at[b], sub, False)
            s_ref[:, b * w2:(b + 1) * w2] = s
            yp = jnp.dot(s.astype(BF16), wc_ref[b], preferred_element_type=F32)
            y_ref[:, b * LANES:(b + 1) * LANES] = _permute_rows_f32(pmt_ref[...], yp)

    sd = jax.ShapeDtypeStruct
    return pl.pallas_call(
        body, name="ssm_fwd", grid=(nj // npair, seq // t_blk),
        in_specs=[pl.BlockSpec((t_blk, npair * LANES), lambda j, i: (i, j)),
                  pl.BlockSpec((npair, LANES, w2), lambda j, i: (j, 0, 0)),
                  pl.BlockSpec((npair, w2, LANES), lambda j, i: (j, 0, 0)),
                  pl.BlockSpec((1, npair * w2), lambda j, i: (0, j)),
                  _full_spec((t_blk, t_blk)), _full_spec((t_blk, t_blk))],
        out_specs=[pl.BlockSpec((t_blk, npair * w2), lambda j, i: (i, j)),
                   pl.BlockSpec((t_blk, npair * LANES), lambda j, i: (i, j))],
        out_shape=[sd((seq, nj * w2), F32), sd((seq, nj * LANES), F32)],
        scratch_shapes=_scan_scratch(npair, t_blk, sub, hb), compiler_params=_cparams(),
    )(proj, wb, wc, a, pm, pm.T)


def _ssm_bwd(dy, s, proj, du1, wb, wc, a):
    seq = dy.shape[0]
    nj = wb.shape[0]
    w2 = 2 * STATE_BLOCK
    hb = STATE_BLOCK
    t_blk = min(SCAN_T, seq)
    sub = t_blk // SUBLANES
    nb = seq // t_blk
    pm = _pack_matrix(t_blk, BF16)

    npair = SSM_BLOCKS_PER_STEP

    def body(dy_ref, s_ref, sprev_ref, u_ref, du1_ref, wb_ref, wc_ref, a_ref, pm_ref, pmt_ref,
             du_ref, dwb_ref, dwc_ref, da_ref, st, tab_r, tab_i, loc):
        ib = pl.program_id(1)
        pmv = pm_ref[...]
        coef = [(a_ref[:, b * w2:b * w2 + hb], -a_ref[:, b * w2 + hb:(b + 1) * w2]) for b in range(npair)]

        @pl.when(ib == 0)
        def _():
            for b, (ar, ai) in enumerate(coef):
                st[b] = jnp.zeros((SUBLANES, hb), F32)
                _scan_tables(ar, ai, tab_r.at[b], tab_i.at[b], sub, True)

        sums = []
        for b, (ar, ai) in enumerate(coef):
            cols, wide = slice(b * LANES, (b + 1) * LANES), slice(b * w2, (b + 1) * w2)
            dyp = jnp.dot(pmv, dy_ref[:, cols], preferred_element_type=F32).astype(BF16)
            up = jnp.dot(pmv, u_ref[:, cols].astype(BF16), preferred_element_type=F32).astype(BF16)
            ds = lax.dot_general(dyp, wc_ref[b], (NT, ((), ())), preferred_element_type=F32)
            lam = _scan_block(ds, loc.at[b], ar, ai, st.at[b], tab_r.at[b], tab_i.at[b], sub, True)
            lamb = lam.astype(BF16)
            du = lax.dot_general(lamb, wb_ref[b], (NT, ((), ())), preferred_element_type=F32)
            du_ref[:, cols] = (_permute_rows_f32(pmt_ref[...], du) + du1_ref[:, cols]).astype(du_ref.dtype)
            sv = s_ref[:, wide]
            dwb = lax.dot_general(up, lamb, (TN, ((), ())), preferred_element_type=F32)
            dwc = lax.dot_general(sv.astype(BF16), dyp, (TN, ((), ())), preferred_element_type=F32)

            prev_last = sprev_ref[SUBLANES - 1:SUBLANES, wide]
            prev_last = jnp.where(ib == nb - 1, jnp.zeros_like(prev_last), prev_last)
            tail = sv[t_blk - SUBLANES:, :]
            sl = lax.broadcasted_iota(jnp.int32, tail.shape, 0)
            head = jnp.where(sl >= 1, pltpu.roll(tail, 1, 0), prev_last)
            s_sh = jnp.concatenate([head, sv[:t_blk - SUBLANES, :]], axis=0)
            lam_r, lam_i = lam[:, :hb], lam[:, hb:]
            sr_, si_ = s_sh[:, :hb], s_sh[:, hb:]
            dar = jnp.sum(lam_r * sr_ + lam_i * si_, axis=0, keepdims=True)
            dai = jnp.sum(lam_i * sr_ - lam_r * si_, axis=0, keepdims=True)
            sums.append((wide, jnp.concatenate([dar, dai], axis=1), dwb, dwc))

        @pl.when(ib == 0)
        def _():
            for b, (wide, contrib, dwb, dwc) in enumerate(sums):
                da_ref[:, wide] = contrib
                dwb_ref[b] = dwb
                dwc_ref[b] = dwc

        @pl.when(ib != 0)
        def _():
            for b, (wide, contrib, dwb, dwc) in enumerate(sums):
                da_ref[:, wide] += contrib
                dwb_ref[b] += dwb
                dwc_ref[b] += dwc

    blk = lambda j, i: (nb - 1 - i, j)
    prev_blk = lambda j, i: (jnp.maximum((nb - 1 - i) * sub - 1, 0), j)
    sd = jax.ShapeDtypeStruct
    return pl.pallas_call(
        body, name="ssm_bwd", grid=(nj // npair, nb),
        in_specs=[pl.BlockSpec((t_blk, npair * LANES), blk), pl.BlockSpec((t_blk, npair * w2), blk),
                  pl.BlockSpec((SUBLANES, npair * w2), prev_blk), pl.BlockSpec((t_blk, npair * LANES), blk),
                  pl.BlockSpec((t_blk, npair * LANES), blk),
                  pl.BlockSpec((npair, LANES, w2), lambda j, i: (j, 0, 0)),
                  pl.BlockSpec((npair, w2, LANES), lambda j, i: (j, 0, 0)),
                  pl.BlockSpec((1, npair * w2), lambda j, i: (0, j)),
                  _full_spec((t_blk, t_blk)), _full_spec((t_blk, t_blk))],
        out_specs=[pl.BlockSpec((t_blk, npair * LANES), blk),
                   pl.BlockSpec((npair, LANES, w2), lambda j, i: (j, 0, 0)),
                   pl.BlockSpec((npair, w2, LANES), lambda j, i: (j, 0, 0)),
                   pl.BlockSpec((1, npair * w2), lambda j, i: (0, j))],
        out_shape=[sd((seq, nj * LANES), BF16), sd((nj, LANES, w2), F32), sd((nj, w2, LANES), F32),
                   sd((1, nj * w2), F32)],
        scratch_shapes=_scan_scratch(npair, t_blk, sub, hb), compiler_params=_cparams(),
    )(dy, s, s, proj, du1, wb, wc, a, pm, pm.T)


def _rope128(x, cos, sa, sb):
    return x * cos + pltpu.roll(x, 96, 1) * sa + pltpu.roll(x, 32, 1) * sb


def _rope128_t(dy, cos, sa, sb):
    return dy * cos + pltpu.roll(dy * sa, 32, 1) + pltpu.roll(dy * sb, 96, 1)


ATT_BQ = 512


def _probs(qn, qp, kn, kp, r0, scale):
    s = lax.dot_general(qn, kn, (NT, ((), ())), preferred_element_type=F32)
    s = s + lax.dot_general(qp, kp, (NT, ((), ())), preferred_element_type=F32)
    s = s * scale
    diag = s[:, r0:]
    row = lax.broadcasted_iota(jnp.int32, diag.shape, 0)
    col = lax.broadcasted_iota(jnp.int32, diag.shape, 1)
    diag = jnp.where(col <= row, diag, jnp.finfo(F32).min)
    s = diag if r0 == 0 else jnp.concatenate([s[:, :r0], diag], axis=1)
    m = jnp.max(s, axis=-1, keepdims=True)
    e = jnp.exp(s - m)
    return e / jnp.sum(e, axis=-1, keepdims=True)


def _attn_specs(seq):
    tab = pl.BlockSpec((seq, LANES), lambda h: (0, 0))
    return [pl.BlockSpec((None, seq, 256), lambda h: (h, 0, 0)), pl.BlockSpec((None, seq, 128), lambda h: (h, 0, 0)),
            pl.BlockSpec((None, seq, 128), lambda h: (h, 0, 1)), tab, tab, tab, tab]


def _attn_fwd(q_raw, kv, kpe, cos, sa, sb):
    nh, seq, _ = q_raw.shape
    bq = min(ATT_BQ, seq)
    scale = (QK_NOPE + QK_ROPE) ** -0.5

    def body(q_ref, kn_ref, v_ref, kp_ref, cos_ref, sa_ref, sb_ref, o_ref):
        for r0 in range(0, seq, bq):
            rows, kend = pl.ds(r0, bq), r0 + bq
            qn = q_ref[rows, :QK_NOPE].astype(BF16)
            qp = _rope128(q_ref[rows, QK_NOPE:], cos_ref[rows, :], sa_ref[rows, :], sb_ref[rows, :]).astype(BF16)
            p = _probs(qn, qp, kn_ref[:kend, :], kp_ref[:kend, :], r0, scale)
            o_ref[rows, :] = jnp.dot(p.astype(BF16), v_ref[:kend, :], preferred_element_type=F32)

    return pl.pallas_call(
        body, name="attn_fwd", grid=(nh,), in_specs=_attn_specs(seq),
        out_specs=pl.BlockSpec((seq, V_DIM), lambda h: (0, h)),
        out_shape=jax.ShapeDtypeStruct((seq, nh * V_DIM), F32), compiler_params=_cparams(),
    )(q_raw, kv, kv, kpe, cos, sa, sb)


def _attn_bwd(q_raw, kv, kpe, cos, sa, sb, do):
    nh, seq, _ = q_raw.shape
    bq = min(ATT_BQ, seq)
    scale = (QK_NOPE + QK_ROPE) ** -0.5

    def body(q_ref, kn_ref, v_ref, kp_ref, cos_ref, sa_ref, sb_ref, do_ref, dq_ref, dkv_ref, dkp_ref):
        dkv_ref[...] = jnp.zeros_like(dkv_ref)
        dkp_ref[...] = jnp.zeros_like(dkp_ref)
        for r0 in range(0, seq, bq):
            rows, kend = pl.ds(r0, bq), r0 + bq
            cos_b, sa_b, sb_b = cos_ref[rows, :], sa_ref[rows, :], sb_ref[rows, :]
            qn = q_ref[rows, :QK_NOPE].astype(BF16)
            qp = _rope128(q_ref[rows, QK_NOPE:], cos_b, sa_b, sb_b).astype(BF16)
            kn, v, kp = kn_ref[:kend, :], v_ref[:kend, :], kp_ref[:kend, :]
            p = _probs(qn, qp, kn, kp, r0, scale)
            dob = do_ref[rows, :].astype(BF16)
            dp = lax.dot_general(dob, v, (NT, ((), ())), preferred_element_type=F32)
            ds = p * (dp - jnp.sum(p * dp, axis=-1, keepdims=True)) * scale
            dsb = ds.astype(BF16)
            pb = p.astype(BF16)
            dq_ref[rows, :QK_NOPE] = jnp.dot(dsb, kn, preferred_element_type=F32).astype(dq_ref.dtype)
            dqp = jnp.dot(dsb, kp, preferred_element_type=F32)
            dq_ref[rows, QK_NOPE:] = _rope128_t(dqp, cos_b, sa_b, sb_b).astype(dq_ref.dtype)
            dkv_ref[:kend, :QK_NOPE] += lax.dot_general(dsb, qn, (TN, ((), ())), preferred_element_type=F32)
            dkv_ref[:kend, QK_NOPE:] += lax.dot_general(pb, dob, (TN, ((), ())), preferred_element_type=F32)
            dkp_ref[:kend, :] += lax.dot_general(dsb, qp, (TN, ((), ())), preferred_element_type=F32)

    sd = jax.ShapeDtypeStruct
    return pl.pallas_call(
        body, name="attn_bwd", grid=(nh,),
        in_specs=_attn_specs(seq) + [pl.BlockSpec((seq, V_DIM), lambda h: (0, h))],
        out_specs=[pl.BlockSpec((None, seq, 256), lambda h: (h, 0, 0)),
                   pl.BlockSpec((None, seq, 256), lambda h: (h, 0, 0)),
                   pl.BlockSpec((None, seq, 128), lambda h: (h, 0, 0))],
        out_shape=[sd((nh, seq, 256), BF16), sd((nh, seq, 256), F32), sd((nh, seq, 128), F32)],
        compiler_params=_cparams(),
    )(q_raw, kv, kv, kpe, cos, sa, sb, do)


def _shift_rows(a, k):
    seq = a.shape[0]
    r = pltpu.roll(a, k % seq, 0)
    rows = lax.broadcasted_iota(jnp.int32, (SUBLANES, a.shape[1]), 0)
    if k > 0:
        return jnp.concatenate([jnp.where(rows >= k, r[:SUBLANES], 0.0), r[SUBLANES:]], axis=0)
    return jnp.concatenate([r[:seq - SUBLANES], jnp.where(rows < SUBLANES + k, r[seq - SUBLANES:], 0.0)], axis=0)


def _conv3(a, w, b):
    a1 = _shift_rows(a, 1)
    a2 = _shift_rows(a, 2)
    return w[2:3] * a + w[1:2] * a1 + w[0:1] * a2 + b, a1, a2


def _conv_gate_fwd(a, cw, cb):
    half, _, seq, c = a.shape
    nc = c // LANES

    def fn(pair, wg, wv, bg, bv):
        gc, _, _ = _conv3(pair[0], wg, bg)
        vc, _, _ = _conv3(pair[1], wv, bv)
        return gc * jax.nn.sigmoid(gc) * vc

    def w_spec(off, r):
        return pl.BlockSpec((None, r, LANES), lambda k, j: (k + off, 0, j))

    return _blockwise(
        "conv_gate_fwd", fn, [a, cw, cw, cb, cb],
        [pl.BlockSpec((None, 2, seq, LANES), lambda k, j: (k, 0, 0, j)),
         w_spec(0, 3), w_spec(half, 3), w_spec(0, 1), w_spec(half, 1)],
        [((seq, half * c), BF16)], [pl.BlockSpec((seq, LANES), lambda k, j: (0, k * nc + j))],
        grid=(half, nc))[0]


def _conv_gate_bwd(a, cw, cb, dm):
    half, _, seq, c = a.shape
    nc = c // LANES

    def body(a_ref, wg_ref, wv_ref, bg_ref, bv_ref, dm_ref, da_ref, dw_ref, db_ref):
        dmv = dm_ref[...]
        ga, wg = a_ref[0], wg_ref[...]
        va, wv = a_ref[1], wv_ref[...]
        gc, g1, g2 = _conv3(ga, wg, bg_ref[...])
        vc, v1, v2 = _conv3(va, wv, bv_ref[...])
        sg = jax.nn.sigmoid(gc)
        dms = dmv * sg
        d_val = dms * gc
        d_gate = dms * vc * (1.0 + gc * (1.0 - sg))

        def back(r, dc, own, a1, a2, w):
            up1 = _shift_rows(dc, -1)
            up2 = _shift_rows(dc, -2)
            da_ref[r] = (w[2:3] * dc + w[1:2] * up1 + w[0:1] * up2).astype(da_ref.dtype)
            dw_ref[r, 0:1, :] = jnp.sum(dc * a2, axis=0, keepdims=True)
            dw_ref[r, 1:2, :] = jnp.sum(dc * a1, axis=0, keepdims=True)
            dw_ref[r, 2:3, :] = jnp.sum(dc * own, axis=0, keepdims=True)
            db_ref[r] = jnp.sum(dc, axis=0, keepdims=True)

        back(0, d_gate, ga, g1, g2, wg)
        back(1, d_val, va, v1, v2, wv)

    def w_spec(off, r):
        return pl.BlockSpec((None, r, LANES), lambda k, j: (k + off, 0, j))

    def pair_spec(r):
        return pl.BlockSpec((None, 2, r, LANES), lambda k, j: (k, 0, 0, j))

    sd = jax.ShapeDtypeStruct
    return pl.pallas_call(
        body, name="conv_gate_bwd", grid=(half, nc),
        in_specs=[pair_spec(seq), w_spec(0, 3), w_spec(half, 3), w_spec(0, 1), w_spec(half, 1),
                  pl.BlockSpec((seq, LANES), lambda k, j: (0, k * nc + j))],
        out_specs=[pair_spec(seq), pair_spec(3), pair_spec(1)],
        out_shape=[sd((half, 2, seq, c), BF16), sd((half, 2, 3, c), F32), sd((half, 2, 1, c), F32)],
        compiler_params=_cparams(),
    )(a, cw, cw, cb, cb, dm)


ROW_T = 256


def _local_step(x, positions, target, w, emit=lambda **grads: None):
    seq, d = x.shape
    t_row = min(ROW_T, seq)
    nrow = seq // t_row
    ssm_w = d // 2
    nj = ssm_w // LANES
    n_groups = ssm_w // SSM_GROUP
    nh = w["wuq"].shape[0]
    q_rank = w["wuq"].shape[1]
    kv_rank = w["wukv"].shape[1]
    ns = w["wup"].shape[0]
    c_ff = w["wup"].shape[2]
    in_pad = w["win"].shape[1]
    tm = min(1024, seq)
    nm = seq // tm
    sw = 2 * STATE_BLOCK
    g1 = (nrow,)

    lr3 = w["lam_re"].reshape(n_groups, 1, SSM_STATE)
    li3 = w["lam_im"].reshape(n_groups, 1, SSM_STATE)
    ldt3 = w["log_dt"].reshape(n_groups, 1, 1)
    bt_re = jnp.swapaxes(w["b_re"].reshape(n_groups, SSM_STATE, SSM_GROUP), 1, 2)
    bt_im = jnp.swapaxes(w["b_im"].reshape(n_groups, SSM_STATE, SSM_GROUP), 1, 2)
    abar_re, abar_im, bbt_re, bbt_im = _s5_prep(lr3, li3, ldt3, bt_re, bt_im)
    eye = jnp.eye(GROUPS_PER_BLOCK, dtype=F32)

    def blockdiag_in(bb):
        t = bb.reshape(nj, GROUPS_PER_BLOCK, SSM_GROUP, SSM_STATE)
        return jnp.einsum("jghp,gk->jghkp", t, eye).reshape(nj, LANES, STATE_BLOCK)

    def blockdiag_in_t(dwb):
        t = dwb.reshape(nj, GROUPS_PER_BLOCK, SSM_GROUP, GROUPS_PER_BLOCK, SSM_STATE)
        return jnp.einsum("jghkp,gk->jghp", t, eye).reshape(n_groups, SSM_GROUP, SSM_STATE)

    def blockdiag_out(cc):
        t = cc.reshape(nj, GROUPS_PER_BLOCK, SSM_GROUP, SSM_STATE)
        return jnp.einsum("jghp,gk->jkpgh", t, eye).reshape(nj, STATE_BLOCK, LANES)

    def blockdiag_out_t(dwc):
        t = dwc.reshape(nj, GROUPS_PER_BLOCK, SSM_STATE, GROUPS_PER_BLOCK, SSM_GROUP)
        return jnp.einsum("jkpgh,gk->jghp", t, eye).reshape(n_groups, SSM_GROUP, SSM_STATE)

    c_re = w["c_re"].reshape(n_groups, SSM_GROUP, SSM_STATE)
    c_im = w["c_im"].reshape(n_groups, SSM_GROUP, SSM_STATE)
    wb = jnp.concatenate([blockdiag_in(bbt_re), blockdiag_in(bbt_im)], axis=2).astype(BF16)
    wc = jnp.concatenate([blockdiag_out(c_re), -blockdiag_out(c_im)], axis=1).astype(BF16)
    a_lay = jnp.concatenate([abar_re.reshape(nj, 1, STATE_BLOCK), abar_im.reshape(nj, 1, STATE_BLOCK)],
                            axis=1).reshape(1, nj * sw)

    attn_w = w["attn_norm"]
    hn = _blockwise("norm1", lambda xb, wv: _rms(xb, wv), [x, attn_w], [_row_spec(t_row, d), _full_spec((1, d))],
                    [((seq, d), BF16)], [_row_spec(t_row, d)], g1)[0]
    proj = _mm2d("proj", hn, w["win"], NN, F32, tn=640)

    s_all, ylin = _ssm_fwd(proj, wb, wc, a_lay)

    def glu_fwd_fn(yl, ub, dsk, wg, bg):
        yp = yl + dsk * ub
        ygv = jax.nn.gelu(yp)
        ygb = ygv.astype(BF16)
        zb = jnp.dot(ygb, wg, preferred_element_type=F32) + bg
        return yp, ygb, zb, ygv * jax.nn.sigmoid(zb)

    t_wide = min(2 * t_row, seq)
    g_wide = (seq // t_wide,)
    wide = pl.BlockSpec((t_wide, ssm_w), lambda i: (i, 0))
    y_pre, yg, z, y_ssm = _blockwise(
        "ssm_glu_fwd", glu_fwd_fn, [ylin, proj, w["ssm_d"], w["wglu"], w["b_glu"]],
        [wide, wide, _full_spec((1, ssm_w)), _full_spec((ssm_w, ssm_w), single=True), _full_spec((1, ssm_w))],
        [((seq, ssm_w), F32), ((seq, ssm_w), BF16), ((seq, ssm_w), F32), ((seq, ssm_w), F32)], [wide] * 4, g_wide)

    cq_off, ckv_off, kpe_off = ssm_w, ssm_w + q_rank, ssm_w + q_rank + kv_rank
    assert cq_off % q_rank == 0 and ckv_off % kv_rank == 0 and kpe_off % LANES == 0
    cq_spec = pl.BlockSpec((t_row, q_rank), lambda i: (i, cq_off // q_rank))
    ckv_spec = pl.BlockSpec((t_row, kv_rank), lambda i: (i, ckv_off // kv_rank))
    kpe_spec = pl.BlockSpec((t_row, LANES), lambda i: (i, kpe_off // LANES))
    pos_b = jnp.broadcast_to(positions.astype(F32)[:, None], (seq, LANES))
    inv_freq = ROPE_THETA ** (-jnp.arange(0, QK_ROPE, 2, dtype=F32) / QK_ROPE)
    inv128 = jnp.tile(inv_freq, 4).reshape(1, LANES)

    def mla_prep_fn(cq, ckv, kp, pb, inv, wq, wkv):
        ang = pb * inv
        lane = lax.broadcasted_iota(jnp.int32, ang.shape, 1)
        cs, sn = jnp.cos(ang), jnp.sin(ang)
        cos = jnp.where(lane < QK_ROPE, cs, 0.0)
        sa = jnp.where(lane < QK_ROPE // 2, -sn, 0.0)
        sb = jnp.where(jnp.logical_and(lane >= QK_ROPE // 2, lane < QK_ROPE), sn, 0.0)
        return _rms(cq, wq), _rms(ckv, wkv), _rope128(kp, cos, sa, sb), cos, sa, sb

    qn, kvn, kpe, cos_t, sa_t, sb_t = _blockwise(
        "mla_prep", mla_prep_fn, [proj, proj, proj, pos_b, inv128, w["q_norm"], w["kv_norm"]],
        [cq_spec, ckv_spec, kpe_spec, _row_spec(t_row, LANES),
         _full_spec((1, LANES)), _full_spec((1, q_rank)), _full_spec((1, kv_rank))],
        [((seq, q_rank), BF16), ((seq, kv_rank), BF16), ((seq, LANES), BF16)] + [((seq, LANES), F32)] * 3,
        [_row_spec(t_row, q_rank), _row_spec(t_row, kv_rank)] + [_row_spec(t_row, LANES)] * 4, g1)

    def head_mm(name, act, wh, out_dtype):
        kdim, ndim = wh.shape[1], wh.shape[2]
        return _mm(name, act, wh, grid=(nh, 1, 1),
                   a_spec=pl.BlockSpec((seq, kdim), lambda h, i, k: (i, 0)),
                   b_spec=pl.BlockSpec((None, kdim, ndim), lambda h, i, k: (h, 0, 0)),
                   o_spec=pl.BlockSpec((None, seq, ndim), lambda h, i, k: (h, i, 0)),
                   out_shape=(nh, seq, ndim), out_dtype=out_dtype)

    q_raw = head_mm("mla_q", qn, w["wuq"], F32)
    kv = head_mm("mla_kv", kvn, w["wukv"], BF16)
    y_mla = _attn_fwd(q_raw, kv, kpe, cos_t, sa_t, sb_t)
    mla_w = nh * V_DIM

    def outnorm_fn(ys, ym, ws, wm):
        return jnp.concatenate([_rms(ys, ws), _rms(ym, wm)], axis=1)

    ycat = _blockwise("out_norm", outnorm_fn, [y_ssm, y_mla, w["son"], w["mon"]],
                      [_row_spec(t_row, ssm_w), _row_spec(t_row, mla_w), _full_spec((1, ssm_w)), _full_spec((1, mla_w))],
                      [((seq, d), BF16)], [_row_spec(t_row, d)], g1)[0]
    h1 = _mm2d("out_proj", ycat, w["wout"], NN, F32, res=x)

    hn2 = _blockwise("norm2", lambda hb, wv: _rms(hb, wv), [h1, w["ffn_norm"]],
                     [_row_spec(t_row, d), _full_spec((1, d))], [((seq, d), BF16)], [_row_spec(t_row, d)], g1)[0]
    tku = d
    half = ns // 2
    a_ff = _mm("ffn_up", hn2, w["wup"], grid=(ns, nm, d // tku),
               a_spec=pl.BlockSpec((tm, tku), lambda s, i, k: (i, k)),
               b_spec=pl.BlockSpec((None, tku, c_ff), lambda s, i, k: (s, k, 0)),
               o_spec=pl.BlockSpec((None, None, tm, c_ff), lambda s, i, k: (s % half, s // half, i, 0)),
               out_shape=(half, 2, seq, c_ff), out_dtype=F32)
    cb3 = w["conv_b"].reshape(ns, 1, c_ff)
    m_ff = _conv_gate_fwd(a_ff, w["conv_w"], cb3)
    d_ff = half * c_ff
    wdn = w["wdown"]
    tnd = _tile(d, 1024)
    tmx, tnx = min(1024, seq), _tile(d, 1024)
    h2 = _mm2d("ffn_down", m_ff, wdn, NN, F32, tm=512, tn=512, tk=d_ff, res=h1)

    def loss_fn(hb, tb, wv):
        def f(hh, ww):
            err = _rms(hh, ww) - tb
            return 0.5 * jnp.sum(jnp.mean(err * err, axis=-1))

        lossv, (dh, dw) = jax.value_and_grad(f, argnums=(0, 1))(hb, wv)
        return dh, dh, jnp.full((1, LANES), lossv, F32), dw

    fin_w = w["final_norm"].reshape(1, d)
    dh2, dh2b, loss_acc, g_final = _blockwise(
        "loss_head", loss_fn, [h2, target, fin_w], [_row_spec(t_row, d), _row_spec(t_row, d), _full_spec((1, d))],
        [((seq, d), F32), ((seq, d), BF16), ((1, LANES), F32), ((1, d), F32)],
        [_row_spec(t_row, d), _row_spec(t_row, d), _full_spec((1, LANES)), _full_spec((1, d))], g1, n_acc=2)
    loss = loss_acc

    dm = _mm2d("ffn_down_dx", dh2b, wdn, NT, F32, tn=c_ff)
    tks = seq
    g_wdown = _mm2d("ffn_down_dw", m_ff, dh2b, TN, BF16, tm=c_ff)
    emit(wdown=g_wdown)
    da_ff, g_convw2, g_convb2 = _conv_gate_bwd(a_ff, w["conv_w"], cb3, dm)
    g_convw = jnp.swapaxes(g_convw2, 0, 1).reshape(ns, 3, c_ff)
    g_convb = jnp.swapaxes(g_convb2, 0, 1).reshape(ns, 1, c_ff)
    g_wup = _mm("ffn_up_dw", hn2, da_ff, grid=(ns, d // tnd, seq // tks), contract=TN,
                a_spec=pl.BlockSpec((tks, tnd), lambda s, j, k: (k, j)),
                b_spec=pl.BlockSpec((None, None, tks, c_ff), lambda s, j, k: (s % half, s // half, k, 0)),
                o_spec=pl.BlockSpec((None, tnd, c_ff), lambda s, j, k: (s, j, 0)),
                out_shape=(ns, d, c_ff), out_dtype=BF16)
    emit(wup=g_wup)
    dhn2 = _mm("ffn_up_dx", da_ff, w["wup"], grid=(seq // tmx, d // tnx, ns), contract=NT,
               a_spec=pl.BlockSpec((None, None, tmx, c_ff), lambda i, j, s: (s % half, s // half, i, 0)),
               b_spec=pl.BlockSpec((None, tnx, c_ff), lambda i, j, s: (s, j, 0)),
               o_spec=pl.BlockSpec((tmx, tnx), lambda i, j, s: (i, j)),
               out_shape=(seq, d), out_dtype=F32)
    emit(wup_pair_sums_after=dhn2)

    def norm_bwd_fn(hb, dres, dn, wv):
        dx_, dw_ = _rms_bwd(hb, wv, dn)
        dtot = dres + dx_
        return dtot, dtot, dw_

    dh1, dh1b, g_ffn_norm = _blockwise(
        "norm2_bwd", norm_bwd_fn, [h1, dh2, dhn2, w["ffn_norm"]],
        [_row_spec(t_row, d)] * 3 + [_full_spec((1, d))],
        [((seq, d), F32), ((seq, d), BF16), ((1, d), F32)],
        [_row_spec(t_row, d), _row_spec(t_row, d), _full_spec((1, d))], g1, n_acc=1)

    g_wout = _mm2d("out_proj_dw", ycat, dh1b, TN, BF16)

    def outnorm_bwd_fn(dhb, wo, ys, ym, ws, wm):
        dyc = lax.dot_general(dhb, wo, (NT, ((), ())), preferred_element_type=F32)
        dys, dws = _rms_bwd(ys, ws, dyc[:, :ssm_w])
        dym, dwm = _rms_bwd(ym, wm, dyc[:, ssm_w:])
        return dys, dym, dws, dwm

    dy_ssm, dy_mla, g_son, g_mon = _blockwise(
        "out_proj_dx_norm_bwd", outnorm_bwd_fn, [dh1b, w["wout"], y_ssm, y_mla, w["son"], w["mon"]],
        [_row_spec(t_wide, d), _full_spec((d, d), single=True), wide, _row_spec(t_wide, mla_w),
         _full_spec((1, ssm_w)), _full_spec((1, mla_w))],
        [((seq, ssm_w), F32), ((seq, mla_w), F32), ((1, ssm_w), F32), ((1, mla_w), F32)],
        [wide, _row_spec(t_wide, mla_w), _full_spec((1, ssm_w)), _full_spec((1, mla_w))],
        g_wide, n_acc=2)

    def glu_bwd_fn(dy, yp, zb, ub, dsk, wg):
        ygv = jax.nn.gelu(yp)
        sg = jax.nn.sigmoid(zb)
        dz = dy * ygv * sg * (1.0 - sg)
        dzb = dz.astype(BF16)
        dyg = dy * sg + lax.dot_general(dzb, wg, (NT, ((), ())), preferred_element_type=F32)
        _, vjp = jax.vjp(jax.nn.gelu, yp)
        dyp = vjp(dyg)[0]
        return (dzb, dyp, dyp * dsk, jnp.sum(dz, axis=0, keepdims=True), jnp.sum(dyp * ub, axis=0, keepdims=True))

    dz, dy_pre, du1, g_bglu, g_ssmd = _blockwise(
        "ssm_glu_bwd", glu_bwd_fn, [dy_ssm, y_pre, z, proj, w["ssm_d"], w["wglu"]],
        [wide] * 4 + [_full_spec((1, ssm_w)), _full_spec((ssm_w, ssm_w), single=True)],
        [((seq, ssm_w), BF16), ((seq, ssm_w), BF16), ((seq, ssm_w), F32), ((1, ssm_w), F32), ((1, ssm_w), F32)],
        [wide] * 3 + [_full_spec((1, ssm_w))] * 2, g_wide, n_acc=2)
    g_wglu = _mm2d("ssm_glu_dw", yg, dz, TN, BF16)
    dq_raw, dkv, dkp_h = _attn_bwd(q_raw, kv, kpe, cos_t, sa_t, sb_t, dy_mla)

    def head_mm_dx(name, dact, wh):
        kdim, ndim = wh.shape[1], wh.shape[2]
        return _mm(name, dact, wh, grid=(1, 1, nh), contract=NT,
                   a_spec=pl.BlockSpec((None, seq, ndim), lambda i, j, h: (h, i, 0)),
                   b_spec=pl.BlockSpec((None, kdim, ndim), lambda i, j, h: (h, 0, 0)),
                   o_spec=pl.BlockSpec((seq, kdim), lambda i, j, h: (i, 0)),
                   out_shape=(seq, kdim), out_dtype=F32)

    def head_mm_dw(name, act, dact):
        kdim, ndim = act.shape[1], dact.shape[2]
        return _mm(name, act, dact, grid=(nh, 1, seq // tks), contract=TN,
                   a_spec=pl.BlockSpec((tks, kdim), lambda h, j, k: (k, 0)),
                   b_spec=pl.BlockSpec((None, tks, ndim), lambda h, j, k: (h, k, 0)),
                   o_spec=pl.BlockSpec((None, kdim, ndim), lambda h, j, k: (h, 0, 0)),
                   out_shape=(nh, kdim, ndim), out_dtype=BF16)

    g_wuq = head_mm_dw("mla_q_dw", qn, dq_raw)
    g_wukv = head_mm_dw("mla_kv_dw", kvn, dkv)
    dqn = head_mm_dx("mla_q_dx", dq_raw, w["wuq"])
    dkvn = head_mm_dx("mla_kv_dx", dkv, w["wukv"])
    emit(not_before=(dqn, dkvn, dy_pre), wout=g_wout, wuq=g_wuq, wukv=g_wukv, wglu=g_wglu, conv_w=g_convw)

    du, dwb, dwc, da_lay = _ssm_bwd(dy_pre, s_all, proj, du1, wb, wc, a_lay)
    g_c_re = blockdiag_out_t(dwc[:, :STATE_BLOCK, :])
    g_c_im = -blockdiag_out_t(dwc[:, STATE_BLOCK:, :])
    dbbt_re = blockdiag_in_t(dwb[:, :, :STATE_BLOCK])
    dbbt_im = blockdiag_in_t(dwb[:, :, STATE_BLOCK:])
    da3 = da_lay.reshape(nj, 2, STATE_BLOCK)
    dabar_re = da3[:, 0, :].reshape(n_groups, 1, SSM_STATE)
    dabar_im = da3[:, 1, :].reshape(n_groups, 1, SSM_STATE)
    g_lr3, g_li3, g_ldt3, g_bt_re, g_bt_im = _s5_prep_bwd(lr3, li3, ldt3, bt_re, bt_im,
                                                           dabar_re, dabar_im, dbbt_re, dbbt_im)

    def mla_prep_bwd_fn(cq, ckv, dqn_b, dkvn_b, dkp_b, cos, sa, sb, wq, wkv):
        dcq, dwq = _rms_bwd(cq, wq, dqn_b)
        dckv, dwkv = _rms_bwd(ckv, wkv, dkvn_b)
        dkp_sum = dkp_b[0]
        for h in range(1, nh):
            dkp_sum = dkp_sum + dkp_b[h]
        return dcq, dckv, _rope128_t(dkp_sum, cos, sa, sb), dwq, dwkv

    dc_q, dc_kv, dkpe_raw, g_qnorm, g_kvnorm = _blockwise(
        "mla_prep_bwd", mla_prep_bwd_fn, [proj, proj, dqn, dkvn, dkp_h, cos_t, sa_t, sb_t, w["q_norm"], w["kv_norm"]],
        [cq_spec, ckv_spec, _row_spec(t_row, q_rank), _row_spec(t_row, kv_rank),
         pl.BlockSpec((nh, t_row, LANES), lambda i: (0, i, 0))] + [_row_spec(t_row, LANES)] * 3
        + [_full_spec((1, q_rank)), _full_spec((1, kv_rank))],
        [((seq, q_rank), BF16), ((seq, kv_rank), BF16), ((seq, LANES), BF16), ((1, q_rank), F32), ((1, kv_rank), F32)],
        [_row_spec(t_row, q_rank), _row_spec(t_row, kv_rank), _row_spec(t_row, LANES), _full_spec((1, q_rank)),
         _full_spec((1, kv_rank))], g1, n_acc=2)

    dproj = jnp.concatenate([du, dc_q, dc_kv, dkpe_raw], axis=1)
    g_win = _mm2d("proj_dw", hn, dproj, TN, BF16, tn=640)
    emit(win=g_win)
    def norm1_bwd_fn(dpb, wi, xb, dres, wv):
        dn = lax.dot_general(dpb, wi, (NT, ((), ())), preferred_element_type=F32)
        dx_, dw_ = _rms_bwd(xb, wv, dn)
        return dres + dx_, dw_

    grad_x, g_attn_norm = _blockwise(
        "proj_dx_norm1_bwd", norm1_bwd_fn, [dproj, w["win"], x, dh1, attn_w],
        [_row_spec(t_row, in_pad), _full_spec((d, in_pad), single=True), _row_spec(t_row, d), _row_spec(t_row, d), _full_spec((1, d))],
        [((seq, d), F32), ((1, d), F32)], [_row_spec(t_row, d), _full_spec((1, d))], g1, n_acc=1)
    emit(win_pair_sums_after=grad_x)

    grads = dict(
        attn_norm=g_attn_norm, win=g_win, lam_re=g_lr3, lam_im=g_li3, log_dt=g_ldt3,
        bt_re=g_bt_re, bt_im=g_bt_im, c_re=g_c_re, c_im=g_c_im,
        ssm_d=g_ssmd, wglu=g_wglu, b_glu=g_bglu, q_norm=g_qnorm, wuq=g_wuq, kv_norm=g_kvnorm, wukv=g_wukv,
        son=g_son, mon=g_mon, wout=g_wout, ffn_norm=g_ffn_norm, wup=g_wup, conv_w=g_convw, conv_b=g_convb,
        wdown=g_wdown, final_norm=g_final)
    return loss, grad_x, grads


def _mesh_pos():
    return lax.axis_index("x"), lax.axis_index("y"), lax.axis_index("c")


def _handshake_all():
    x, y, c = _mesh_pos()
    barrier = pltpu.get_barrier_semaphore()
    for k in range(1, N_DEV):
        peer = (1 - x if k & 4 else x, 1 - y if k & 2 else y, 1 - c if k & 1 else c)
        pl.semaphore_signal(barrier, inc=1, device_id=peer, device_id_type=MESH)
    pl.semaphore_wait(barrier, N_DEV - 1)


def _handshake(peers):
    barrier = pltpu.get_barrier_semaphore()
    for peer in peers:
        pl.semaphore_signal(barrier, inc=1, device_id=peer, device_id_type=MESH)
    pl.semaphore_wait(barrier, len(peers))


def _comm_call(name, body, n, out_shape, ins, collective_id, after=None, copies=7, n_remote=None, n_local=None):
    n_remote = copies * n if n_remote is None else n_remote
    sems = [pltpu.SemaphoreType.DMA((n_remote,)), pltpu.SemaphoreType.DMA((n_remote,)),
            pltpu.SemaphoreType.DMA((n if n_local is None else n_local,))]
    if collective_id is None:
        any_spec = pl.BlockSpec(memory_space=pl.ANY)
        return pl.pallas_call(body, name=name, out_shape=out_shape, in_specs=[any_spec] * n,
                              out_specs=[any_spec] * n, scratch_shapes=sems)(*ins)
    seq_body = body
    if after:
        n_after = len(after)
        ins = list(ins) + list(after)

        def seq_body(*refs):
            body(*refs[:n], *refs[n + n_after:])

    return pl.kernel(seq_body, name=name, out_type=out_shape,
                     mesh=plsc.ScalarSubcoreMesh(axis_name="seq", num_cores=1), scratch_types=sems,
                     compiler_params=pltpu.CompilerParams(collective_id=collective_id))(*ins)


def _all_gather(name, xs, collective_id=None, after=None, pair_sums=()):
    n = len(xs)
    nh = len(pair_sums)
    m = n + nh

    def body(*refs):
        x_refs, h_refs, o_refs, e_refs = refs[:n], refs[n:m], refs[m:m + n], refs[m + n:2 * m]
        send_sems, recv_sems, local_sems = refs[2 * m:]
        if collective_id is not None:
            _handshake_all()
        finish_pairs = _chip_copies(h_refs, e_refs, send_sems, recv_sems, local_sems, 7 * n, n) if nh else None
        x, y, c = _mesh_pos()
        me, sibling = (x, y, c), (x, y, 1 - c)
        chips = [(1 - x, y), (x, 1 - y), (1 - x, 1 - y)]

        def slot(o_ref, px, py, pc):
            return o_ref.at[4 * px + 2 * py + pc]

        def copy(t, k, block, to, src=None):
            dst = slot(o_refs[t], *block)
            return pltpu.make_async_remote_copy(
                src_ref=dst if src is None else src, dst_ref=dst,
                send_sem=send_sems.at[7 * t + k], recv_sem=recv_sems.at[7 * t + k],
                device_id=to, device_id_type=MESH)

        started = []
        for t in range(n):
            mine = pltpu.make_async_copy(x_refs[t], slot(o_refs[t], *me), local_sems.at[t])
            mine.start()
            started.append(mine)
        first = []
        for t in range(n):
            first.append(copy(t, 0, me, sibling, src=x_refs[t]))
            first += [copy(t, 1 + j, me, (*chip, c), src=x_refs[t]) for j, chip in enumerate(chips)]
        for cp in first:
            cp.start()
        passed = []
        for j, chip in enumerate(chips):
            for t in range(n):
                copy(t, 1 + j, (*chip, c), me).wait_recv()
                fwd = copy(t, 4 + j, (*chip, c), sibling)
                fwd.start()
                passed.append(fwd)
        for t in range(n):
            copy(t, 0, sibling, me).wait_recv()
            for j, chip in enumerate(chips):
                copy(t, 4 + j, (*chip, 1 - c), me).wait_recv()
        for cp in first + passed:
            cp.wait_send()
        for mine in started:
            mine.wait()
        if nh:
            finish_pairs()

    out_shape = ([jax.ShapeDtypeStruct((N_DEV,) + v.shape, v.dtype) for v in xs]
                 + [jax.ShapeDtypeStruct(v.shape, v.dtype) for v in pair_sums])
    return _comm_call(name, body, m, out_shape, list(xs) + list(pair_sums), collective_id, after,
                      n_remote=7 * n + (N_CHIP - 1) * nh, n_local=m)


def _exchange_partials(name, gs, collective_id=None, after=None):
    n = len(gs)

    def body(*refs):
        g_refs, o_refs = refs[:n], refs[n:2 * n]
        send_sems, recv_sems, local_sems = refs[2 * n:]
        if collective_id is not None:
            _handshake_all()
        x, y, c = _mesh_pos()
        me_idx = 4 * x + 2 * y + c
        copies = []
        for t in range(n):
            mine = pltpu.make_async_copy(g_refs[t].at[me_idx], o_refs[t].at[me_idx], local_sems.at[t])
            mine.start()
            copies.append(mine)
        remote = []
        for k in range(1, N_DEV):
            px = 1 - x if k & 4 else x
            py = 1 - y if k & 2 else y
            pc = 1 - c if k & 1 else c
            p_idx = 4 * px + 2 * py + pc
            for t in range(n):
                cp = pltpu.make_async_remote_copy(
                    src_ref=g_refs[t].at[p_idx], dst_ref=o_refs[t].at[me_idx],
                    send_sem=send_sems.at[7 * t + k - 1], recv_sem=recv_sems.at[7 * t + k - 1],
                    device_id=(px, py, pc), device_id_type=MESH)
                cp.start()
                landing = pltpu.make_async_remote_copy(
                    src_ref=g_refs[t].at[p_idx], dst_ref=o_refs[t].at[p_idx],
                    send_sem=send_sems.at[7 * t + k - 1], recv_sem=recv_sems.at[7 * t + k - 1],
                    device_id=(px, py, pc), device_id_type=MESH)
                remote.append((cp, landing))
        for cp, landing in remote:
            landing.wait_recv()
        for cp, landing in remote:
            cp.wait_send()
        for mine in copies:
            mine.wait()

    out_shape = [jax.ShapeDtypeStruct(v.shape, v.dtype) for v in gs]
    return _comm_call(name, body, n, out_shape, gs, collective_id, after)


N_CHIP = N_DEV // 2
PAIR_ADD_BLOCK_ELEMS = 1024 * 1024


def _pair_swap(name, gs, collective_id, after=None):
    n = len(gs)

    def body(*refs):
        g_refs, o_refs = refs[:n], refs[n:2 * n]
        send_sems, recv_sems, _ = refs[2 * n:]
        x, y, c = _mesh_pos()
        sibling = (x, y, 1 - c)
        _handshake([sibling])
        copies = []
        for t in range(n):
            for k in range(N_CHIP):
                copies.append(pltpu.make_async_remote_copy(
                    src_ref=g_refs[t].at[2 * k + 1 - c], dst_ref=o_refs[t].at[k],
                    send_sem=send_sems.at[N_CHIP * t + k], recv_sem=recv_sems.at[N_CHIP * t + k],
                    device_id=sibling, device_id_type=MESH))
        for cp in copies:
            cp.start()
        for cp in copies:
            cp.wait_recv()
        for cp in copies:
            cp.wait_send()

    out_shape = [jax.ShapeDtypeStruct((N_CHIP,) + v.shape[1:], v.dtype) for v in gs]
    return _comm_call(name, body, n, out_shape, gs, collective_id, after, copies=N_CHIP)


def _pair_add(name, g, got):
    _, r, c = g.shape
    tr = r
    if r * c > PAIR_ADD_BLOCK_ELEMS and r % SUBLANES == 0:
        tr = SUBLANES
        while r % (tr * 2) == 0 and tr * 2 * c <= PAIR_ADD_BLOCK_ELEMS:
            tr *= 2

    def body(core_ref, g_ref, got_ref, o_ref):
        o_ref[...] = (g_ref[...].astype(F32) + got_ref[...].astype(F32)).astype(o_ref.dtype)

    grid_spec = pltpu.PrefetchScalarGridSpec(
        num_scalar_prefetch=1, grid=(N_CHIP, r // tr),
        in_specs=[pl.BlockSpec((None, None, tr, c), lambda k, i, core: (k, core[0], i, 0)),
                  pl.BlockSpec((None, tr, c), lambda k, i, core: (k, i, 0))],
        out_specs=pl.BlockSpec((None, tr, c), lambda k, i, core: (k, i, 0)))
    core = lax.axis_index("c").astype(jnp.int32).reshape(1)
    return pl.pallas_call(body, name=name, grid_spec=grid_spec, out_shape=jax.ShapeDtypeStruct((N_CHIP, r, c), g.dtype),
                          compiler_params=_cparams())(core, g.reshape(N_CHIP, 2, r, c), got)


def _chip_copies(h_refs, o_refs, send_sems, recv_sems, local_sems, sem0, local0):
    n = len(h_refs)
    per = N_CHIP - 1
    x, y, c = _mesh_pos()
    others = [(1 - x if k & 2 else x, 1 - y if k & 1 else y) for k in range(1, N_CHIP)]
    my_chip = 2 * x + y
    local = []
    for t in range(n):
        mine = pltpu.make_async_copy(h_refs[t].at[my_chip], o_refs[t].at[my_chip], local_sems.at[local0 + t])
        mine.start()
        local.append(mine)
    remote = []
    for j, (px, py) in enumerate(others):
        chip = 2 * px + py
        for t in range(n):
            sems = dict(send_sem=send_sems.at[sem0 + per * t + j], recv_sem=recv_sems.at[sem0 + per * t + j],
                        device_id=(px, py, c), device_id_type=MESH)
            cp = pltpu.make_async_remote_copy(src_ref=h_refs[t].at[chip], dst_ref=o_refs[t].at[my_chip], **sems)
            cp.start()
            landing = pltpu.make_async_remote_copy(src_ref=h_refs[t].at[chip], dst_ref=o_refs[t].at[chip], **sems)
            remote.append((cp, landing))

    def finish():
        for cp, landing in remote:
            landing.wait_recv()
        for cp, landing in remote:
            cp.wait_send()
        for mine in local:
            mine.wait()

    return finish


def _chip_exchange(name, hs, collective_id, after=None):
    n = len(hs)
    per = N_CHIP - 1

    def body(*refs):
        h_refs, o_refs = refs[:n], refs[n:2 * n]
        send_sems, recv_sems, local_sems = refs[2 * n:]
        x, y, c = _mesh_pos()
        _handshake([(1 - x if k & 2 else x, 1 - y if k & 1 else y, c) for k in range(1, N_CHIP)])
        _chip_copies(h_refs, o_refs, send_sems, recv_sems, local_sems, 0, 0)()

    out_shape = [jax.ShapeDtypeStruct(v.shape, v.dtype) for v in hs]
    return _comm_call(name, body, n, out_shape, hs, collective_id, after, copies=per)


ADAM_BLOCK_ELEMS = 128 * 1024


def _sum_parts(pb):
    g = pb[0].astype(F32)
    for j in range(1, pb.shape[0]):
        g = g + pb[j].astype(F32)
    return g


def _adam_math(g, wb_, mb, vb):
    m_new = ADAM_B1 * mb + (1.0 - ADAM_B1) * g
    v_new = ADAM_B2 * vb + (1.0 - ADAM_B2) * (g * g)
    m_hat = m_new / (1.0 - ADAM_B1 ** ADAM_STEP)
    v_hat = v_new / (1.0 - ADAM_B2 ** ADAM_STEP)
    delta = -ADAM_LR * (m_hat / (jnp.sqrt(v_hat) + ADAM_EPS) + ADAM_WD * wb_)
    return g, delta, m_new, v_new


def _adamw_multi(name, items, nblk=1, packed=None):
    n = len(items)

    def spec(shape, lead):
        blk = list(shape)
        blk[lead + 1] = shape[lead + 1] // nblk
        if nblk == 1:
            return pl.BlockSpec(tuple(blk), lambda i, nd=len(shape): (0,) * nd)
        return pl.BlockSpec(tuple(blk), lambda i, nd=len(shape), ax=lead + 1: (0,) * ax + (i,) + (0,) * (nd - ax - 1))

    ins, in_specs, out_specs, out_shape, where = [], [], [], [], []
    if packed is not None:
        ins.append(packed)
        in_specs.append(spec(packed.shape, 1))
    for parts, wv, mv, vv in items:
        if isinstance(parts, int):
            where.append((0, parts, len(ins)))
        else:
            assert parts.shape[1:] == wv.shape, (name, parts.shape, wv.shape)
            where.append((len(ins), None, len(ins) + 1))
            ins.append(parts)
            in_specs.append(spec(parts.shape, 1))
        ins += [wv, mv, vv]
        in_specs += [spec(wv.shape, 0)] * 3
        out_specs += [spec(wv.shape, 0)] * 4
        out_shape += [jax.ShapeDtypeStruct(wv.shape, F32)] * 4
    n_in = len(ins)

    def body(*refs):
        for t, (ip, off, iw) in enumerate(where):
            wr, mr, vr = refs[iw:iw + 3]
            parts = refs[ip][...] if off is None else refs[ip][:, :, off:off + wr.shape[-1]]
            res = _adam_math(_sum_parts(parts), wr[...], mr[...], vr[...])
            for o, val in zip(refs[n_in + 4 * t:n_in + 4 * t + 4], res):
                o[...] = val

    res = pl.pallas_call(body, name=name, grid=(nblk,), in_specs=in_specs, out_specs=out_specs, out_shape=out_shape,
                         compiler_params=_cparams())(*ins)
    return [tuple(res[4 * t:4 * t + 4]) for t in range(n)]


def _sum_multi(name, parts_list):
    def body(*refs):
        for pr, o in zip(refs[:len(parts_list)], refs[len(parts_list):]):
            o[...] = _sum_parts(pr[...])

    return pl.pallas_call(body, name=name, out_shape=[jax.ShapeDtypeStruct(p.shape[1:], F32) for p in parts_list],
                          compiler_params=_cparams())(*parts_list)


def _adamw_sum(name, parts, wv, mv, vv):
    npart, r, c = parts.shape
    tr = r
    if r * c > ADAM_BLOCK_ELEMS and r % SUBLANES == 0:
        tr = SUBLANES
        while r % (tr * 2) == 0 and tr * 2 * c <= ADAM_BLOCK_ELEMS:
            tr *= 2

    def fn(pb, wb_, mb, vb):
        return _adam_math(_sum_parts(pb), wb_, mb, vb)

    row = pl.BlockSpec((tr, c), lambda i: (i, 0))
    return _blockwise(name, fn, [parts, wv, mv, vv],
                      [pl.BlockSpec((npart, tr, c), lambda i: (0, i, 0)), row, row, row],
                      [((r, c), F32)] * 4, [row] * 4, (r // tr,))


_VECTORS = ["attn_norm", "lam_re", "lam_im", "log_dt", "ssm_d", "b_glu", "q_norm", "kv_norm", "son", "mon",
            "ffn_norm", "conv_b", "final_norm"]
_GHP = ["c_re", "c_im", "bt_re", "bt_im"]
_PACKED = ["attn_norm", "ssm_d", "b_glu", "q_norm", "kv_norm", "son", "mon", "ffn_norm", "conv_b", "final_norm"]
_BIG = ["win", "wglu", "wuq", "wukv", "wout", "wup", "wdown", "conv_w"]
_ROWS_IN_LANES = ("win", "wuq")
_TWO_LEVEL = ("wup", "win")
_AFTER = "_pair_sums_after"
_ORDER = ["attn_norm", "win", "lam_re", "lam_im", "log_dt", "b_re", "b_im", "c_re", "c_im", "ssm_d", "wglu",
          "b_glu", "q_norm", "wuq", "kv_norm", "wukv", "son", "mon", "wout", "ffn_norm", "wup", "conv_w",
          "conv_b", "wdown", "final_norm"]


def kernel(x, positions, attn_norm_w, w_in, ssm_lambda_re, ssm_lambda_im, ssm_log_dt, ssm_b_re, ssm_b_im, ssm_c_re, ssm_c_im, ssm_d, ssm_w_glu, ssm_b_glu, mla_q_norm_w, mla_w_uq, mla_kv_norm_w, mla_w_ukv, ssm_out_norm_w, mla_out_norm_w, w_out, ffn_norm_w, ffn_w_up, ffn_conv_w, ffn_conv_b, ffn_w_down, final_norm_w, loss_target, m_attn_norm_w, m_w_in, m_ssm_lambda_re, m_ssm_lambda_im, m_ssm_log_dt, m_ssm_b_re, m_ssm_b_im, m_ssm_c_re, m_ssm_c_im, m_ssm_d, m_ssm_w_glu, m_ssm_b_glu, m_mla_q_norm_w, m_mla_w_uq, m_mla_kv_norm_w, m_mla_w_ukv, m_ssm_out_norm_w, m_mla_out_norm_w, m_w_out, m_ffn_norm_w, m_ffn_w_up, m_ffn_conv_w, m_ffn_conv_b, m_ffn_w_down, m_final_norm_w, v_attn_norm_w, v_w_in, v_ssm_lambda_re, v_ssm_lambda_im, v_ssm_log_dt, v_ssm_b_re, v_ssm_b_im, v_ssm_c_re, v_ssm_c_im, v_ssm_d, v_ssm_w_glu, v_ssm_b_glu, v_mla_q_norm_w, v_mla_w_uq, v_mla_kv_norm_w, v_mla_w_ukv, v_ssm_out_norm_w, v_mla_out_norm_w, v_w_out, v_ffn_norm_w, v_ffn_w_up, v_ffn_conv_w, v_ffn_conv_b, v_ffn_w_down, v_final_norm_w):
    wts = dict(attn_norm=attn_norm_w, win=w_in, lam_re=ssm_lambda_re, lam_im=ssm_lambda_im, log_dt=ssm_log_dt,
               b_re=ssm_b_re, b_im=ssm_b_im, c_re=ssm_c_re, c_im=ssm_c_im, ssm_d=ssm_d, wglu=ssm_w_glu,
               b_glu=ssm_b_glu, q_norm=mla_q_norm_w, wuq=mla_w_uq, kv_norm=mla_kv_norm_w, wukv=mla_w_ukv,
               son=ssm_out_norm_w, mon=mla_out_norm_w, wout=w_out, ffn_norm=ffn_norm_w, wup=ffn_w_up,
               conv_w=ffn_conv_w, conv_b=ffn_conv_b, wdown=ffn_w_down, final_norm=final_norm_w)
    moms = dict(zip(_ORDER, [m_attn_norm_w, m_w_in, m_ssm_lambda_re, m_ssm_lambda_im, m_ssm_log_dt, m_ssm_b_re,
                             m_ssm_b_im, m_ssm_c_re, m_ssm_c_im, m_ssm_d, m_ssm_w_glu, m_ssm_b_glu, m_mla_q_norm_w,
                             m_mla_w_uq, m_mla_kv_norm_w, m_mla_w_ukv, m_ssm_out_norm_w, m_mla_out_norm_w, m_w_out,
                             m_ffn_norm_w, m_ffn_w_up, m_ffn_conv_w, m_ffn_conv_b, m_ffn_w_down, m_final_norm_w]))
    vels = dict(zip(_ORDER, [v_attn_norm_w, v_w_in, v_ssm_lambda_re, v_ssm_lambda_im, v_ssm_log_dt, v_ssm_b_re,
                             v_ssm_b_im, v_ssm_c_re, v_ssm_c_im, v_ssm_d, v_ssm_w_glu, v_ssm_b_glu, v_mla_q_norm_w,
                             v_mla_w_uq, v_mla_kv_norm_w, v_mla_w_ukv, v_ssm_out_norm_w, v_mla_out_norm_w, v_w_out,
                             v_ffn_norm_w, v_ffn_w_up, v_ffn_conv_w, v_ffn_conv_b, v_ffn_w_down, v_final_norm_w]))
    seq, d = x.shape[1], x.shape[2]
    in_width = w_in.shape[2]
    in_pad = -(-in_width // LANES) * LANES
    q_cols = mla_w_uq.shape[2]
    q_pad = 2 * LANES

    (win_g,) = _all_gather("gather_w_in", [jnp.pad(w_in[0], ((0, 0), (0, in_pad - in_width))).astype(BF16)])
    wglu_g, wuq_g, wukv_g, wout_g, convw_g = _all_gather(
        "gather_mix", [ssm_w_glu[0].astype(BF16), jnp.pad(mla_w_uq[0], ((0, 0), (0, q_pad - q_cols))).astype(BF16),
                       mla_w_ukv[0].astype(BF16), w_out[0].astype(BF16), ffn_conv_w[0]], collective_id=0)
    (wup_g,) = _all_gather("gather_ffn_up", [ffn_w_up[0].astype(BF16)], collective_id=1)
    (wdown_g,) = _all_gather("gather_ffn_down", [ffn_w_down[0].astype(BF16)], collective_id=2)
    ns = N_DEV
    c_ff = wup_g.shape[2]
    w = dict(
        attn_norm=attn_norm_w, win=win_g.reshape(d, in_pad), lam_re=ssm_lambda_re, lam_im=ssm_lambda_im,
        log_dt=ssm_log_dt, b_re=ssm_b_re, b_im=ssm_b_im, c_re=ssm_c_re, c_im=ssm_c_im, ssm_d=ssm_d,
        wglu=wglu_g.reshape(d // 2, d // 2), b_glu=ssm_b_glu, q_norm=mla_q_norm_w, wuq=wuq_g,
        kv_norm=mla_kv_norm_w, wukv=wukv_g, son=ssm_out_norm_w, mon=mla_out_norm_w, wout=wout_g.reshape(d, d),
        ffn_norm=ffn_norm_w, wup=wup_g, conv_w=convw_g, conv_b=ffn_conv_b,
        wdown=wdown_g.reshape(ns // 2 * c_ff, d), final_norm=final_norm_w)

    shard_layout = dict(
        win=lambda a: a[:, :in_width].reshape(N_DEV, d // N_DEV, in_width),
        wglu=lambda a: a.reshape(N_DEV, d // 2 // N_DEV, d // 2),
        wuq=lambda a: a[:, :, :q_cols], wukv=lambda a: a, wout=lambda a: a.reshape(N_DEV, d // N_DEV, d),
        wup=lambda a: a, wdown=lambda a: a.reshape(N_DEV, c_ff // 2, d), conv_w=lambda a: a)
    recv = {}
    next_id = [3]

    last = [None]

    out = {}

    def update(k):
        shp = wts[k].shape
        r, c = shp[-2], shp[-1]
        if k in _ROWS_IN_LANES:
            t = lambda a: jnp.swapaxes(a.reshape(-1, r, c), 1, 2)
            res = _adamw_sum("adamw_" + k, t(recv[k]), t(wts[k])[0], t(moms[k])[0], t(vels[k])[0])
            out[k] = [jnp.swapaxes(a, 0, 1).reshape(shp) for a in res]
            return res[0]
        res = _adamw_sum("adamw_" + k, recv[k].reshape(-1, r, c), wts[k].reshape(r, c),
                         moms[k].reshape(r, c), vels[k].reshape(r, c))
        out[k] = [a.reshape(shp) for a in res]
        return res[0]

    pending = {}

    def exchange(not_before=(), **grads):
        names = list(grads)
        if len(names) == 1 and names[0] in _TWO_LEVEL:
            k = names[0]
            parts = shard_layout[k](grads[k])
            got = _pair_swap("swap_" + k, [parts], collective_id=next_id[0], after=[last[0]])[0]
            next_id[0] += 1
            pending[k] = (parts, got)
            last[0] = got
            return
        if len(names) == 1 and names[0].endswith(_AFTER):
            k = names[0][:-len(_AFTER)]
            sums = _pair_add("pair_add_" + k, *pending[k])
            if k == "win":
                pending["tail"] = sums
                return
            recv[k] = _chip_exchange("exchange_" + k, [sums], collective_id=next_id[0],
                                     after=[last[0], grads[names[0]]])[0]
            next_id[0] += 1
            last[0] = recv[k]
            return
        got = _exchange_partials("exchange_" + "_".join(names), [shard_layout[k](grads[k]) for k in names],
                                 collective_id=next_id[0], after=[a for a in (last[0], *not_before) if a is not None])
        next_id[0] += 1
        last[0] = got[-1]
        recv.update(zip(names, got))

    loss_part, grad_x, g = _local_step(x[0], positions[0], loss_target[0], w, emit=exchange)
    n_groups = ssm_lambda_re.shape[1]
    two_d = {"lam_re": (n_groups, -1), "lam_im": (n_groups, -1)}
    dense = {k: g[k].reshape(two_d.get(k, (1, -1))) for k in _VECTORS}
    offsets, width = {}, 0
    for k in _PACKED:
        offsets[k] = width
        width += dense[k].shape[1]
    sent = dict(packed=jnp.concatenate([dense[k] for k in _PACKED], axis=1),
                **{k: dense[k] for k in _VECTORS if k not in _PACKED},
                **{k: g[k].reshape(n_groups, -1) for k in _GHP},
                loss=loss_part)
    names = list(sent)
    got = _all_gather("gather_small_grads", [sent[k] for k in names], collective_id=next_id[0], after=[last[0]],
                      pair_sums=[pending["tail"]])
    gathered = dict(zip(names, got))
    recv["win"] = got[len(names)]
    for k in _BIG:
        if k not in out and k != "win":
            update(k)
    update("win")

    def finish(keys, results):
        for k, res in zip(keys, results):
            out[k] = [a.reshape(wts[k].shape) for a in res]

    view = lambda k, a: a.reshape(dense[k].shape)
    finish(_VECTORS, _adamw_multi("adamw_vectors", [(offsets.get(k, gathered.get(k)), view(k, wts[k]), view(k, moms[k]),
                                                     view(k, vels[k])) for k in _VECTORS], packed=gathered["packed"]))
    summed = _GHP + ["loss"]
    sums = dict(zip(summed, _sum_multi("sum_ssm_bc_loss", [gathered[k] for k in summed])))
    loss = sums["loss"][0, 0]
    ghp = lambda k: sums[k].reshape(g[k].shape)
    t_hp = lambda a: jnp.swapaxes(a, 2, 3)
    bc_keys = ["c_re", "c_im", "b_re", "b_im"]
    items = [(ghp(k)[None, None], wts[k], moms[k], vels[k]) for k in bc_keys[:2]]
    items += [(ghp(t)[None, None], t_hp(wts[k]), t_hp(moms[k]), t_hp(vels[k]))
              for k, t in zip(bc_keys[2:], ("bt_re", "bt_im"))]
    res = _adamw_multi("adamw_ssm_bc", items)
    finish(bc_keys, res[:2] + [tuple(t_hp(a) for a in r) for r in res[2:]])

    grad_x = grad_x.reshape(x.shape)
    return (loss, grad_x, *[out[k][0] for k in _ORDER], *[out[k][1] for k in _ORDER],
            *[out[k][2] for k in _ORDER], *[out[k][3] for k in _ORDER])
```

```python
import functools

import jax
import jax.numpy as jnp
from jax import lax
from jax.experimental import pallas as pl
from jax.experimental.pallas import tpu as pltpu
from jax.experimental.pallas import tpu_sc as plsc

F32 = jnp.float32
BF16 = jnp.bfloat16
MESH = pl.DeviceIdType.MESH

N_DEV = 8
LANES = 128
SUBLANES = 8
VMEM_LIMIT = 48 * 1024 * 1024

SSM_GROUP = 16
SSM_STATE = 64
GROUPS_PER_BLOCK = LANES // SSM_GROUP
STATE_BLOCK = GROUPS_PER_BLOCK * SSM_STATE
QK_NOPE = 128
QK_ROPE = 64
V_DIM = 128
ROPE_THETA = 10000.0
RMS_EPS = 1e-6

ADAM_LR = 0.001
ADAM_B1 = 0.9
ADAM_B2 = 0.999
ADAM_EPS = 1e-08
ADAM_WD = 0.01
ADAM_STEP = 10

NN = ((1,), (0,))
NT = ((1,), (1,))
TN = ((0,), (0,))


def _cparams():
    return pltpu.CompilerParams(vmem_limit_bytes=VMEM_LIMIT)


def _tile(n, want):
    if n <= want:
        return n
    t = (want // LANES) * LANES
    while t >= LANES:
        if n % t == 0:
            return t
        t -= LANES
    return n


def _mm(name, a, b, *, grid, a_spec, b_spec, o_spec, out_shape, out_dtype, contract=NN,
        res=None, res_spec=None):
    nk = grid[-1]
    kaxis = len(grid) - 1
    acc_shape = tuple(d for d in o_spec.block_shape if d is not None)

    def body(*refs):
        a_ref, b_ref = refs[:2]
        r_ref = None if res is None else refs[2]
        o_ref = refs[2 if res is None else 3]
        part = lax.dot_general(a_ref[...].astype(BF16), b_ref[...].astype(BF16),
                               (contract, ((), ())), preferred_element_type=F32)
        if nk == 1:
            if r_ref is not None:
                part = part + r_ref[...].astype(F32)
            o_ref[...] = part.astype(o_ref.dtype)
            return
        acc = refs[-1]
        k = pl.program_id(kaxis)

        @pl.when(k == 0)
        def _():
            acc[...] = part

        @pl.when(k != 0)
        def _():
            acc[...] += part

        @pl.when(k == nk - 1)
        def _():
            r = acc[...]
            if r_ref is not None:
                r = r + r_ref[...].astype(F32)
            o_ref[...] = r.astype(o_ref.dtype)

    ins = [a, b] + ([] if res is None else [res])
    in_specs = [a_spec, b_spec] + ([] if res is None else [res_spec])
    return pl.pallas_call(
        body, name=name, grid=grid, in_specs=in_specs, out_specs=o_spec,
        out_shape=jax.ShapeDtypeStruct(out_shape, out_dtype),
        scratch_shapes=[pltpu.VMEM(acc_shape, F32)] if nk > 1 else [], compiler_params=_cparams(),
    )(*ins)


def _mm2d(name, a, b, contract, out_dtype, tm=1024, tn=1024, tk=2048, res=None):
    if contract == NN:
        (m, kk), n = a.shape, b.shape[1]
    elif contract == NT:
        (m, kk), n = a.shape, b.shape[0]
    else:
        (kk, m), n = a.shape, b.shape[1]
    tm, tn, tk = _tile(m, tm), _tile(n, tn), _tile(kk, tk)
    grid = (m // tm, n // tn, kk // tk)
    if contract == TN:
        a_spec = pl.BlockSpec((tk, tm), lambda i, j, k: (k, i))
    else:
        a_spec = pl.BlockSpec((tm, tk), lambda i, j, k: (i, k))
    if contract == NT:
        b_spec = pl.BlockSpec((tn, tk), lambda i, j, k: (j, k))
    else:
        b_spec = pl.BlockSpec((tk, tn), lambda i, j, k: (k, j))
    o_spec = pl.BlockSpec((tm, tn), lambda i, j, k: (i, j))
    res_spec = None
    if res is not None:
        if res.shape[0] == 1:
            res_spec = pl.BlockSpec((1, tn), lambda i, j, k: (0, j))
        else:
            res_spec = pl.BlockSpec((tm, tn), lambda i, j, k: (i, j))
    return _mm(name, a, b, grid=grid, a_spec=a_spec, b_spec=b_spec, o_spec=o_spec,
               out_shape=(m, n), out_dtype=out_dtype, contract=contract, res=res, res_spec=res_spec)


def _blockwise(name, fn, ins, in_specs, outs, out_specs, grid, n_acc=0, acc_all=True):
    n_in, n_out = len(ins), len(outs)
    n_plain = n_out - n_acc

    def body(*refs):
        vals = fn(*[r[...] for r in refs[:n_in]])
        if not isinstance(vals, (tuple, list)):
            vals = (vals,)
        o_refs = refs[n_in:n_in + n_out]
        for r, v in zip(o_refs[:n_plain], vals[:n_plain]):
            r[...] = v.astype(r.dtype)
        if n_acc:
            if acc_all:
                first = functools.reduce(jnp.logical_and, [pl.program_id(d) == 0 for d in range(len(grid))])
            else:
                first = pl.program_id(len(grid) - 1) == 0

            @pl.when(first)
            def _():
                for r, v in zip(o_refs[n_plain:], vals[n_plain:]):
                    r[...] = v.astype(r.dtype)

            @pl.when(jnp.logical_not(first))
            def _():
                for r, v in zip(o_refs[n_plain:], vals[n_plain:]):
                    r[...] += v.astype(r.dtype)

    return pl.pallas_call(
        body, name=name, grid=grid, in_specs=in_specs, out_specs=out_specs,
        out_shape=[jax.ShapeDtypeStruct(s, d) for s, d in outs], compiler_params=_cparams(),
    )(*ins)


def _row_spec(t, c):
    return pl.BlockSpec((t, c), lambda i: (i, 0))


def _full_spec(shape, single=False):
    nd = len(shape)
    if single:
        return pl.BlockSpec(tuple(shape), lambda *g: (0,) * nd, pipeline_mode=pl.Buffered(1))
    return pl.BlockSpec(tuple(shape), lambda *g: (0,) * nd)


def _rms(xf, w):
    return xf * lax.rsqrt(jnp.mean(xf * xf, axis=-1, keepdims=True) + RMS_EPS) * w


def _rms_bwd(xf, w, dy):
    _, vjp = jax.vjp(_rms, xf, w)
    return vjp(dy)


def _s5_disc(lr, li, ldt, bre, bim):
    dt = jnp.exp(ldt)
    mag = jnp.exp(lr * dt)
    ar = mag * jnp.cos(li * dt)
    ai = mag * jnp.sin(li * dt)
    nr, ni = ar - 1.0, ai
    den = lr * lr + li * li
    zr = (nr * lr + ni * li) / den
    zi = (ni * lr - nr * li) / den
    return ar, ai, zr * bre - zi * bim, zr * bim + zi * bre


def _s5_prep(lr, li, ldt, bre, bim):
    def body(lr_r, li_r, ldt_r, bre_r, bim_r, ar_r, ai_r, br_r, bi_r):
        ar, ai, br, bi = _s5_disc(lr_r[...], li_r[...], ldt_r[...], bre_r[...], bim_r[...])
        ar_r[...] = ar
        ai_r[...] = ai
        br_r[...] = br
        bi_r[...] = bi

    sd = jax.ShapeDtypeStruct
    return pl.pallas_call(
        body, name="s5_prep",
        out_shape=[sd(lr.shape, F32), sd(lr.shape, F32), sd(bre.shape, F32), sd(bre.shape, F32)],
        compiler_params=_cparams(),
    )(lr, li, ldt, bre, bim)


def _s5_prep_bwd(lr, li, ldt, bre, bim, dar, dai, dbr, dbi):
    def body(lr_r, li_r, ldt_r, bre_r, bim_r, dar_r, dai_r, dbr_r, dbi_r, o0, o1, o2, o3, o4):
        _, vjp = jax.vjp(_s5_disc, lr_r[...], li_r[...], ldt_r[...], bre_r[...], bim_r[...])
        g = vjp((dar_r[...], dai_r[...], dbr_r[...], dbi_r[...]))
        for o, v in zip((o0, o1, o2, o3, o4), g):
            o[...] = v

    sd = jax.ShapeDtypeStruct
    return pl.pallas_call(
        body, name="s5_prep_bwd",
        out_shape=[sd(lr.shape, F32), sd(li.shape, F32), sd(ldt.shape, F32), sd(bre.shape, F32), sd(bim.shape, F32)],
        compiler_params=_cparams(),
    )(lr, li, ldt, bre, bim, dar, dai, dbr, dbi)


SCAN_T = 256


def _scan_tables(ar, ai, tab_r, tab_i, sub, reverse):
    pr, pi = ar, ai
    for k in range(sub):
        row = sub - 1 - k if reverse else k
        tab_r[row:row + 1, :] = pr
        tab_i[row:row + 1, :] = pi
        pr, pi = ar * pr - ai * pi, ar * pi + ai * pr


def _pack_matrix(t_blk, dtype):
    sub = t_blk // SUBLANES
    dst = jnp.arange(t_blk)
    src = (dst % SUBLANES) * sub + dst // SUBLANES
    return (src[:, None] == jnp.arange(t_blk)[None, :]).astype(dtype)


def _permute_rows_f32(pm, x):
    hi = x.astype(BF16)
    r1 = x - hi.astype(F32)
    mid = r1.astype(BF16)
    lo = (r1 - mid.astype(F32)).astype(BF16)
    dot = lambda v: jnp.dot(pm, v, preferred_element_type=F32)
    return dot(hi) + dot(mid) + dot(lo)


def _scan_block(x, loc, ar, ai, st, tab_r, tab_i, sub, reverse):
    hb = STATE_BLOCK
    a8r = jnp.broadcast_to(ar, (SUBLANES, hb))
    a8i = jnp.broadcast_to(ai, (SUBLANES, hb))
    sr = jnp.zeros((SUBLANES, hb), F32)
    si = jnp.zeros((SUBLANES, hb), F32)
    steps = range(sub - 1, -1, -1) if reverse else range(sub)
    for t in steps:
        rows = slice(t * SUBLANES, (t + 1) * SUBLANES)
        sr, si = a8r * sr - a8i * si + x[rows, :hb], a8r * si + a8i * sr + x[rows, hb:]
        loc[rows, :hb] = sr
        loc[rows, hb:] = si
    cr, ci = st[0:1, :], st[1:2, :]
    far = 0 if reverse else sub - 1
    fr, fi = tab_r[far:far + 1, :], tab_i[far:far + 1, :]
    ent_r, ent_i = [None] * SUBLANES, [None] * SUBLANES
    for c in (range(SUBLANES - 1, -1, -1) if reverse else range(SUBLANES)):
        ent_r[c], ent_i[c] = cr, ci
        cr, ci = sr[c:c + 1, :] + (fr * cr - fi * ci), si[c:c + 1, :] + (fr * ci + fi * cr)
    st[0:1, :] = cr
    st[1:2, :] = ci
    c8r = jnp.concatenate(ent_r, axis=0)
    c8i = jnp.concatenate(ent_i, axis=0)
    out = []
    for t in range(sub):
        rows = slice(t * SUBLANES, (t + 1) * SUBLANES)
        tr, ti = tab_r[t:t + 1, :], tab_i[t:t + 1, :]
        out.append(jnp.concatenate([loc[rows, :hb] + (tr * c8r - ti * c8i), loc[rows, hb:] + (tr * c8i + ti * c8r)],
                                   axis=1))
    return jnp.concatenate(out, axis=0)


SSM_BLOCKS_PER_STEP = 2


def _scan_scratch(nblk, t_blk, sub, hb):
    return [pltpu.VMEM((nblk, SUBLANES, hb), F32), pltpu.VMEM((nblk, sub, hb), F32), pltpu.VMEM((nblk, sub, hb), F32),
            pltpu.VMEM((nblk, t_blk, 2 * hb), F32)]


def _ssm_fwd(proj, wb, wc, a):
    seq = proj.shape[0]
    nj = wb.shape[0]
    w2 = 2 * STATE_BLOCK
    hb = STATE_BLOCK
    t_blk = min(SCAN_T, seq)
    sub = t_blk // SUBLANES
    pm = _pack_matrix(t_blk, BF16)

    npair = SSM_BLOCKS_PER_STEP

    def body(u_ref, wb_ref, wc_ref, a_ref, pm_ref, pmt_ref, s_ref, y_ref, st, tab_r, tab_i, loc):
        coef = [(a_ref[:, b * w2:b * w2 + hb], a_ref[:, b * w2 + hb:(b + 1) * w2]) for b in range(npair)]

        @pl.when(pl.program_id(1) == 0)
        def _():
            for b, (ar, ai) in enumerate(coef):
                st[b] = jnp.zeros((SUBLANES, hb), F32)
                _scan_tables(ar, ai, tab_r.at[b], tab_i.at[b], sub, False)

        for b, (ar, ai) in enumerate(coef):
            ub = u_ref[:, b * LANES:(b + 1) * LANES].astype(BF16)
            up = jnp.dot(pm_ref[...], ub, preferred_element_type=F32).astype(BF16)
            bu = jnp.dot(up, wb_ref[b], preferred_element_type=F32)
            s = _scan_block(bu, loc.at[b], ar, ai, st.at[b], tab_r.at[b], tab_i.at[b], sub, False)
            s_ref[:, b * w2:(b + 1) * w2] = s
            yp = jnp.dot(s.astype(BF16), wc_ref[b], preferred_element_type=F32)
            y_ref[:, b * LANES:(b + 1) * LANES] = _permute_rows_f32(pmt_ref[...], yp)

    sd = jax.ShapeDtypeStruct
    return pl.pallas_call(
        body, name="ssm_fwd", grid=(nj // npair, seq // t_blk),
        in_specs=[pl.BlockSpec((t_blk, npair * LANES), lambda j, i: (i, j)),
                  pl.BlockSpec((npair, LANES, w2), lambda j, i: (j, 0, 0)),
                  pl.BlockSpec((npair, w2, LANES), lambda j, i: (j, 0, 0)),
                  pl.BlockSpec((1, npair * w2), lambda j, i: (0, j)),
                  _full_spec((t_blk, t_blk)), _full_spec((t_blk, t_blk))],
        out_specs=[pl.BlockSpec((t_blk, npair * w2), lambda j, i: (i, j)),
                   pl.BlockSpec((t_blk, npair * LANES), lambda j, i: (i, j))],
        out_shape=[sd((seq, nj * w2), F32), sd((seq, nj * LANES), F32)],
        scratch_shapes=_scan_scratch(npair, t_blk, sub, hb), compiler_params=_cparams(),
    )(proj, wb, wc, a, pm, pm.T)


def _ssm_bwd(dy, s, proj, du1, wb, wc, a):
    seq = dy.shape[0]
    nj = wb.shape[0]
    w2 = 2 * STATE_BLOCK
    hb = STATE_BLOCK
    t_blk = min(SCAN_T, seq)
    sub = t_blk // SUBLANES
    nb = seq // t_blk
    pm = _pack_matrix(t_blk, BF16)

    npair = SSM_BLOCKS_PER_STEP

    def body(dy_ref, s_ref, sprev_ref, u_ref, du1_ref, wb_ref, wc_ref, a_ref, pm_ref, pmt_ref,
             du_ref, dwb_ref, dwc_ref, da_ref, st, tab_r, tab_i, loc):
        ib = pl.program_id(1)
        pmv = pm_ref[...]
        coef = [(a_ref[:, b * w2:b * w2 + hb], -a_ref[:, b * w2 + hb:(b + 1) * w2]) for b in range(npair)]

        @pl.when(ib == 0)
        def _():
            for b, (ar, ai) in enumerate(coef):
                st[b] = jnp.zeros((SUBLANES, hb), F32)
                _scan_tables(ar, ai, tab_r.at[b], tab_i.at[b], sub, True)

        sums = []
        for b, (ar, ai) in enumerate(coef):
            cols, wide = slice(b * LANES, (b + 1) * LANES), slice(b * w2, (b + 1) * w2)
            dyp = jnp.dot(pmv, dy_ref[:, cols], preferred_element_type=F32).astype(BF16)
            up = jnp.dot(pmv, u_ref[:, cols].astype(BF16), preferred_element_type=F32).astype(BF16)
            ds = lax.dot_general(dyp, wc_ref[b], (NT, ((), ())), preferred_element_type=F32)
            lam = _scan_block(ds, loc.at[b], ar, ai, st.at[b], tab_r.at[b], tab_i.at[b], sub, True)
            lamb = lam.astype(BF16)
            du = lax.dot_general(lamb, wb_ref[b], (NT, ((), ())), preferred_element_type=F32)
            du_ref[:, cols] = (_permute_rows_f32(pmt_ref[...], du) + du1_ref[:, cols]).astype(du_ref.dtype)
            sv = s_ref[:, wide]
            dwb = lax.dot_general(up, lamb, (TN, ((), ())), preferred_element_type=F32)
            dwc = lax.dot_general(sv.astype(BF16), dyp, (TN, ((), ())), preferred_element_type=F32)

            prev_last = sprev_ref[SUBLANES - 1:SUBLANES, wide]
            prev_last = jnp.where(ib == nb - 1, jnp.zeros_like(prev_last), prev_last)
            tail = sv[t_blk - SUBLANES:, :]
            sl = lax.broadcasted_iota(jnp.int32, tail.shape, 0)
            head = jnp.where(sl >= 1, pltpu.roll(tail, 1, 0), prev_last)
            s_sh = jnp.concatenate([head, sv[:t_blk - SUBLANES, :]], axis=0)
            lam_r, lam_i = lam[:, :hb], lam[:, hb:]
            sr_, si_ = s_sh[:, :hb], s_sh[:, hb:]
            dar = jnp.sum(lam_r * sr_ + lam_i * si_, axis=0, keepdims=True)
            dai = jnp.sum(lam_i * sr_ - lam_r * si_, axis=0, keepdims=True)
            sums.append((wide, jnp.concatenate([dar, dai], axis=1), dwb, dwc))

        @pl.when(ib == 0)
        def _():
            for b, (wide, contrib, dwb, dwc) in enumerate(sums):
                da_ref[:, wide] = contrib
                dwb_ref[b] = dwb
                dwc_ref[b] = dwc

        @pl.when(ib != 0)
        def _():
            for b, (wide, contrib, dwb, dwc) in enumerate(sums):
                da_ref[:, wide] += contrib
                dwb_ref[b] += dwb
                dwc_ref[b] += dwc

    blk = lambda j, i: (nb - 1 - i, j)
    prev_blk = lambda j, i: (jnp.maximum((nb - 1 - i) * sub - 1, 0), j)
    sd = jax.ShapeDtypeStruct
    return pl.pallas_call(
        body, name="ssm_bwd", grid=(nj // npair, nb),
        in_specs=[pl.BlockSpec((t_blk, npair * LANES), blk), pl.BlockSpec((t_blk, npair * w2), blk),
                  pl.BlockSpec((SUBLANES, npair * w2), prev_blk), pl.BlockSpec((t_blk, npair * LANES), blk),
                  pl.BlockSpec((t_blk, npair * LANES), blk),
                  pl.BlockSpec((npair, LANES, w2), lambda j, i: (j, 0, 0)),
                  pl.BlockSpec((npair, w2, LANES), lambda j, i: (j, 0, 0)),
                  pl.BlockSpec((1, npair * w2), lambda j, i: (0, j)),
                  _full_spec((t_blk, t_blk)), _full_spec((t_blk, t_blk))],
        out_specs=[pl.BlockSpec((t_blk, npair * LANES), blk),
                   pl.BlockSpec((npair, LANES, w2), lambda j, i: (j, 0, 0)),
                   pl.BlockSpec((npair, w2, LANES), lambda j, i: (j, 0, 0)),
                   pl.BlockSpec((1, npair * w2), lambda j, i: (0, j))],
        out_shape=[sd((seq, nj * LANES), BF16), sd((nj, LANES, w2), F32), sd((nj, w2, LANES), F32),
                   sd((1, nj * w2), F32)],
        scratch_shapes=_scan_scratch(npair, t_blk, sub, hb), compiler_params=_cparams(),
    )(dy, s, s, proj, du1, wb, wc, a, pm, pm.T)


def _rope128(x, cos, sa, sb):
    return x * cos + pltpu.roll(x, 96, 1) * sa + pltpu.roll(x, 32, 1) * sb


def _rope128_t(dy, cos, sa, sb):
    return dy * cos + pltpu.roll(dy * sa, 32, 1) + pltpu.roll(dy * sb, 96, 1)


ATT_BQ = 512


def _probs(qn, qp, kn, kp, r0, scale):
    s = lax.dot_general(qn, kn, (NT, ((), ())), preferred_element_type=F32)
    s = s + lax.dot_general(qp, kp, (NT, ((), ())), preferred_element_type=F32)
    s = s * scale
    diag = s[:, r0:]
    row = lax.broadcasted_iota(jnp.int32, diag.shape, 0)
    col = lax.broadcasted_iota(jnp.int32, diag.shape, 1)
    diag = jnp.where(col <= row, diag, jnp.finfo(F32).min)
    s = diag if r0 == 0 else jnp.concatenate([s[:, :r0], diag], axis=1)
    m = jnp.max(s, axis=-1, keepdims=True)
    e = jnp.exp(s - m)
    return e / jnp.sum(e, axis=-1, keepdims=True)


def _attn_specs(seq):
    tab = pl.BlockSpec((seq, LANES), lambda h: (0, 0))
    return [pl.BlockSpec((None, seq, 256), lambda h: (h, 0, 0)), pl.BlockSpec((None, seq, 128), lambda h: (h, 0, 0)),
            pl.BlockSpec((None, seq, 128), lambda h: (h, 0, 1)), tab, tab, tab, tab]


def _attn_fwd(q_raw, kv, kpe, cos, sa, sb):
    nh, seq, _ = q_raw.shape
    bq = min(ATT_BQ, seq)
    scale = (QK_NOPE + QK_ROPE) ** -0.5

    def body(q_ref, kn_ref, v_ref, kp_ref, cos_ref, sa_ref, sb_ref, o_ref):
        for r0 in range(0, seq, bq):
            rows, kend = pl.ds(r0, bq), r0 + bq
            qn = q_ref[rows, :QK_NOPE].astype(BF16)
            qp = _rope128(q_ref[rows, QK_NOPE:], cos_ref[rows, :], sa_ref[rows, :], sb_ref[rows, :]).astype(BF16)
            p = _probs(qn, qp, kn_ref[:kend, :], kp_ref[:kend, :], r0, scale)
            o_ref[rows, :] = jnp.dot(p.astype(BF16), v_ref[:kend, :], preferred_element_type=F32)

    return pl.pallas_call(
        body, name="attn_fwd", grid=(nh,), in_specs=_attn_specs(seq),
        out_specs=pl.BlockSpec((seq, V_DIM), lambda h: (0, h)),
        out_shape=jax.ShapeDtypeStruct((seq, nh * V_DIM), F32), compiler_params=_cparams(),
    )(q_raw, kv, kv, kpe, cos, sa, sb)


def _attn_bwd(q_raw, kv, kpe, cos, sa, sb, do):
    nh, seq, _ = q_raw.shape
    bq = min(ATT_BQ, seq)
    scale = (QK_NOPE + QK_ROPE) ** -0.5

    def body(q_ref, kn_ref, v_ref, kp_ref, cos_ref, sa_ref, sb_ref, do_ref, dq_ref, dkv_ref, dkp_ref):
        dkv_ref[...] = jnp.zeros_like(dkv_ref)
        dkp_ref[...] = jnp.zeros_like(dkp_ref)
        for r0 in range(0, seq, bq):
            rows, kend = pl.ds(r0, bq), r0 + bq
            cos_b, sa_b, sb_b = cos_ref[rows, :], sa_ref[rows, :], sb_ref[rows, :]
            qn = q_ref[rows, :QK_NOPE].astype(BF16)
            qp = _rope128(q_ref[rows, QK_NOPE:], cos_b, sa_b, sb_b).astype(BF16)
            kn, v, kp = kn_ref[:kend, :], v_ref[:kend, :], kp_ref[:kend, :]
            p = _probs(qn, qp, kn, kp, r0, scale)
            dob = do_ref[rows, :].astype(BF16)
            dp = lax.dot_general(dob, v, (NT, ((), ())), preferred_element_type=F32)
            ds = p * (dp - jnp.sum(p * dp, axis=-1, keepdims=True)) * scale
            dsb = ds.astype(BF16)
            pb = p.astype(BF16)
            dq_ref[rows, :QK_NOPE] = jnp.dot(dsb, kn, preferred_element_type=F32).astype(dq_ref.dtype)
            dqp = jnp.dot(dsb, kp, preferred_element_type=F32)
            dq_ref[rows, QK_NOPE:] = _rope128_t(dqp, cos_b, sa_b, sb_b).astype(dq_ref.dtype)
            dkv_ref[:kend, :QK_NOPE] += lax.dot_general(dsb, qn, (TN, ((), ())), preferred_element_type=F32)
            dkv_ref[:kend, QK_NOPE:] += lax.dot_general(pb, dob, (TN, ((), ())), preferred_element_type=F32)
            dkp_ref[:kend, :] += lax.dot_general(dsb, qp, (TN, ((), ())), preferred_element_type=F32)

    sd = jax.ShapeDtypeStruct
    return pl.pallas_call(
        body, name="attn_bwd", grid=(nh,),
        in_specs=_attn_specs(seq) + [pl.BlockSpec((seq, V_DIM), lambda h: (0, h))],
        out_specs=[pl.BlockSpec((None, seq, 256), lambda h: (h, 0, 0)),
                   pl.BlockSpec((None, seq, 256), lambda h: (h, 0, 0)),
                   pl.BlockSpec((None, seq, 128), lambda h: (h, 0, 0))],
        out_shape=[sd((nh, seq, 256), BF16), sd((nh, seq, 256), F32), sd((nh, seq, 128), F32)],
        compiler_params=_cparams(),
    )(q_raw, kv, kv, kpe, cos, sa, sb, do)


def _shift_rows(a, k):
    seq = a.shape[0]
    r = pltpu.roll(a, k % seq, 0)
    rows = lax.broadcasted_iota(jnp.int32, (SUBLANES, a.shape[1]), 0)
    if k > 0:
        return jnp.concatenate([jnp.where(rows >= k, r[:SUBLANES], 0.0), r[SUBLANES:]], axis=0)
    return jnp.concatenate([r[:seq - SUBLANES], jnp.where(rows < SUBLANES + k, r[seq - SUBLANES:], 0.0)], axis=0)


def _conv3(a, w, b):
    a1 = _shift_rows(a, 1)
    a2 = _shift_rows(a, 2)
    return w[2:3] * a + w[1:2] * a1 + w[0:1] * a2 + b, a1, a2


def _conv_gate_fwd(a, cw, cb):
    half, _, seq, c = a.shape
    nc = c // LANES

    def fn(pair, wg, wv, bg, bv):
        gc, _, _ = _conv3(pair[0], wg, bg)
        vc, _, _ = _conv3(pair[1], wv, bv)
        return gc * jax.nn.sigmoid(gc) * vc

    def w_spec(off, r):
        return pl.BlockSpec((None, r, LANES), lambda k, j: (k + off, 0, j))

    return _blockwise(
        "conv_gate_fwd", fn, [a, cw, cw, cb, cb],
        [pl.BlockSpec((None, 2, seq, LANES), lambda k, j: (k, 0, 0, j)),
         w_spec(0, 3), w_spec(half, 3), w_spec(0, 1), w_spec(half, 1)],
        [((seq, half * c), BF16)], [pl.BlockSpec((seq, LANES), lambda k, j: (0, k * nc + j))],
        grid=(half, nc))[0]


def _conv_gate_bwd(a, cw, cb, dm):
    half, _, seq, c = a.shape
    nc = c // LANES

    def body(a_ref, wg_ref, wv_ref, bg_ref, bv_ref, dm_ref, da_ref, dw_ref, db_ref):
        dmv = dm_ref[...]
        ga, wg = a_ref[0], wg_ref[...]
        va, wv = a_ref[1], wv_ref[...]
        gc, g1, g2 = _conv3(ga, wg, bg_ref[...])
        vc, v1, v2 = _conv3(va, wv, bv_ref[...])
        sg = jax.nn.sigmoid(gc)
        dms = dmv * sg
        d_val = dms * gc
        d_gate = dms * vc * (1.0 + gc * (1.0 - sg))

        def back(r, dc, own, a1, a2, w):
            up1 = _shift_rows(dc, -1)
            up2 = _shift_rows(dc, -2)
            da_ref[r] = (w[2:3] * dc + w[1:2] * up1 + w[0:1] * up2).astype(da_ref.dtype)
            dw_ref[r, 0:1, :] = jnp.sum(dc * a2, axis=0, keepdims=True)
            dw_ref[r, 1:2, :] = jnp.sum(dc * a1, axis=0, keepdims=True)
            dw_ref[r, 2:3, :] = jnp.sum(dc * own, axis=0, keepdims=True)
            db_ref[r] = jnp.sum(dc, axis=0, keepdims=True)

        back(0, d_gate, ga, g1, g2, wg)
        back(1, d_val, va, v1, v2, wv)

    def w_spec(off, r):
        return pl.BlockSpec((None, r, LANES), lambda k, j: (k + off, 0, j))

    def pair_spec(r):
        return pl.BlockSpec((None, 2, r, LANES), lambda k, j: (k, 0, 0, j))

    sd = jax.ShapeDtypeStruct
    return pl.pallas_call(
        body, name="conv_gate_bwd", grid=(half, nc),
        in_specs=[pair_spec(seq), w_spec(0, 3), w_spec(half, 3), w_spec(0, 1), w_spec(half, 1),
                  pl.BlockSpec((seq, LANES), lambda k, j: (0, k * nc + j))],
        out_specs=[pair_spec(seq), pair_spec(3), pair_spec(1)],
        out_shape=[sd((half, 2, seq, c), BF16), sd((half, 2, 3, c), F32), sd((half, 2, 1, c), F32)],
        compiler_params=_cparams(),
    )(a, cw, cw, cb, cb, dm)


ROW_T = 256


def _local_step(x, positions, target, w, emit=lambda **grads: None):
    seq, d = x.shape
    t_row = min(ROW_T, seq)
    nrow = seq // t_row
    ssm_w = d // 2
    nj = ssm_w // LANES
    n_groups = ssm_w // SSM_GROUP
    nh = w["wuq"].shape[0]
    q_rank = w["wuq"].shape[1]
    kv_rank = w["wukv"].shape[1]
    ns = w["wup"].shape[0]
    c_ff = w["wup"].shape[2]
    in_pad = w["win"].shape[1]
    tm = min(1024, seq)
    nm = seq // tm
    sw = 2 * STATE_BLOCK
    g1 = (nrow,)

    lr3 = w["lam_re"].reshape(n_groups, 1, SSM_STATE)
    li3 = w["lam_im"].reshape(n_groups, 1, SSM_STATE)
    ldt3 = w["log_dt"].reshape(n_groups, 1, 1)
    bt_re = jnp.swapaxes(w["b_re"].reshape(n_groups, SSM_STATE, SSM_GROUP), 1, 2)
    bt_im = jnp.swapaxes(w["b_im"].reshape(n_groups, SSM_STATE, SSM_GROUP), 1, 2)
    abar_re, abar_im, bbt_re, bbt_im = _s5_prep(lr3, li3, ldt3, bt_re, bt_im)
    eye = jnp.eye(GROUPS_PER_BLOCK, dtype=F32)

    def blockdiag_in(bb):
        t = bb.reshape(nj, GROUPS_PER_BLOCK, SSM_GROUP, SSM_STATE)
        return jnp.einsum("jghp,gk->jghkp", t, eye).reshape(nj, LANES, STATE_BLOCK)

    def blockdiag_in_t(dwb):
        t = dwb.reshape(nj, GROUPS_PER_BLOCK, SSM_GROUP, GROUPS_PER_BLOCK, SSM_STATE)
        return jnp.einsum("jghkp,gk->jghp", t, eye).reshape(n_groups, SSM_GROUP, SSM_STATE)

    def blockdiag_out(cc):
        t = cc.reshape(nj, GROUPS_PER_BLOCK, SSM_GROUP, SSM_STATE)
        return jnp.einsum("jghp,gk->jkpgh", t, eye).reshape(nj, STATE_BLOCK, LANES)

    def blockdiag_out_t(dwc):
        t = dwc.reshape(nj, GROUPS_PER_BLOCK, SSM_STATE, GROUPS_PER_BLOCK, SSM_GROUP)
        return jnp.einsum("jkpgh,gk->jghp", t, eye).reshape(n_groups, SSM_GROUP, SSM_STATE)

    c_re = w["c_re"].reshape(n_groups, SSM_GROUP, SSM_STATE)
    c_im = w["c_im"].reshape(n_groups, SSM_GROUP, SSM_STATE)
    wb = jnp.concatenate([blockdiag_in(bbt_re), blockdiag_in(bbt_im)], axis=2).astype(BF16)
    wc = jnp.concatenate([blockdiag_out(c_re), -blockdiag_out(c_im)], axis=1).astype(BF16)
    a_lay = jnp.concatenate([abar_re.reshape(nj, 1, STATE_BLOCK), abar_im.reshape(nj, 1, STATE_BLOCK)],
                            axis=1).reshape(1, nj * sw)

    attn_w = w["attn_norm"]
    t_wide = min(2 * t_row, seq)
    g_wide = (seq // t_wide,)

    def proj_fn(xb, wv, wi):
        hb = _rms(xb, wv).astype(BF16)
        return hb, jnp.dot(hb, wi, preferred_element_type=F32)

    hn, proj = _blockwise(
        "norm1_proj", proj_fn, [x, attn_w, w["win"]],
        [_row_spec(t_wide, d), _full_spec((1, d)), _full_spec((d, in_pad), single=True)],
        [((seq, d), BF16), ((seq, in_pad), F32)], [_row_spec(t_wide, d), _row_spec(t_wide, in_pad)], g_wide)

    s_all, ylin = _ssm_fwd(proj, wb, wc, a_lay)

    def glu_fwd_fn(yl, ub, dsk, wg, bg):
        yp = yl + dsk * ub
        ygv = jax.nn.gelu(yp)
        ygb = ygv.astype(BF16)
        zb = jnp.dot(ygb, wg, preferred_element_type=F32) + bg
        return yp, ygb, zb, ygv * jax.nn.sigmoid(zb)

    wide = pl.BlockSpec((t_wide, ssm_w), lambda i: (i, 0))
    y_pre, yg, z, y_ssm = _blockwise(
        "ssm_glu_fwd", glu_fwd_fn, [ylin, proj, w["ssm_d"], w["wglu"], w["b_glu"]],
        [wide, wide, _full_spec((1, ssm_w)), _full_spec((ssm_w, ssm_w), single=True), _full_spec((1, ssm_w))],
        [((seq, ssm_w), F32), ((seq, ssm_w), BF16), ((seq, ssm_w), F32), ((seq, ssm_w), F32)], [wide] * 4, g_wide)

    cq_off, ckv_off, kpe_off = ssm_w, ssm_w + q_rank, ssm_w + q_rank + kv_rank
    assert cq_off % q_rank == 0 and ckv_off % kv_rank == 0 and kpe_off % LANES == 0
    cq_spec = pl.BlockSpec((t_row, q_rank), lambda i: (i, cq_off // q_rank))
    ckv_spec = pl.BlockSpec((t_row, kv_rank), lambda i: (i, ckv_off // kv_rank))
    kpe_spec = pl.BlockSpec((t_row, LANES), lambda i: (i, kpe_off // LANES))
    pos_b = jnp.broadcast_to(positions.astype(F32)[:, None], (seq, LANES))
    inv_freq = ROPE_THETA ** (-jnp.arange(0, QK_ROPE, 2, dtype=F32) / QK_ROPE)
    inv128 = jnp.tile(inv_freq, 4).reshape(1, LANES)

    def mla_prep_fn(cq, ckv, kp, pb, inv, wq, wkv):
        ang = pb * inv
        lane = lax.broadcasted_iota(jnp.int32, ang.shape, 1)
        cs, sn = jnp.cos(ang), jnp.sin(ang)
        cos = jnp.where(lane < QK_ROPE, cs, 0.0)
        sa = jnp.where(lane < QK_ROPE // 2, -sn, 0.0)
        sb = jnp.where(jnp.logical_and(lane >= QK_ROPE // 2, lane < QK_ROPE), sn, 0.0)
        return _rms(cq, wq), _rms(ckv, wkv), _rope128(kp, cos, sa, sb), cos, sa, sb

    qn, kvn, kpe, cos_t, sa_t, sb_t = _blockwise(
        "mla_prep", mla_prep_fn, [proj, proj, proj, pos_b, inv128, w["q_norm"], w["kv_norm"]],
        [cq_spec, ckv_spec, kpe_spec, _row_spec(t_row, LANES),
         _full_spec((1, LANES)), _full_spec((1, q_rank)), _full_spec((1, kv_rank))],
        [((seq, q_rank), BF16), ((seq, kv_rank), BF16), ((seq, LANES), BF16)] + [((seq, LANES), F32)] * 3,
        [_row_spec(t_row, q_rank), _row_spec(t_row, kv_rank)] + [_row_spec(t_row, LANES)] * 4, g1)

    def head_mm(name, act, wh, out_dtype):
        kdim, ndim = wh.shape[1], wh.shape[2]
        return _mm(name, act, wh, grid=(nh, 1, 1),
                   a_spec=pl.BlockSpec((seq, kdim), lambda h, i, k: (i, 0)),
                   b_spec=pl.BlockSpec((None, kdim, ndim), lambda h, i, k: (h, 0, 0)),
                   o_spec=pl.BlockSpec((None, seq, ndim), lambda h, i, k: (h, i, 0)),
                   out_shape=(nh, seq, ndim), out_dtype=out_dtype)

    q_raw = head_mm("mla_q", qn, w["wuq"], F32)
    kv = head_mm("mla_kv", kvn, w["wukv"], BF16)
    y_mla = _attn_fwd(q_raw, kv, kpe, cos_t, sa_t, sb_t)
    mla_w = nh * V_DIM

    def outnorm_fn(ys, ym, ws, wm):
        return jnp.concatenate([_rms(ys, ws), _rms(ym, wm)], axis=1)

    ycat = _blockwise("out_norm", outnorm_fn, [y_ssm, y_mla, w["son"], w["mon"]],
                      [_row_spec(t_row, ssm_w), _row_spec(t_row, mla_w), _full_spec((1, ssm_w)), _full_spec((1, mla_w))],
                      [((seq, d), BF16)], [_row_spec(t_row, d)], g1)[0]
    h1 = _mm2d("out_proj", ycat, w["wout"], NN, F32, res=x)

    hn2 = _blockwise("norm2", lambda hb, wv: _rms(hb, wv), [h1, w["ffn_norm"]],
                     [_row_spec(t_row, d), _full_spec((1, d))], [((seq, d), BF16)], [_row_spec(t_row, d)], g1)[0]
    tku = d
    half = ns // 2
    a_ff = _mm("ffn_up", hn2, w["wup"], grid=(ns, nm, d // tku),
               a_spec=pl.BlockSpec((tm, tku), lambda s, i, k: (i, k)),
               b_spec=pl.BlockSpec((None, tku, c_ff), lambda s, i, k: (s, k, 0)),
               o_spec=pl.BlockSpec((None, None, tm, c_ff), lambda s, i, k: (s % half, s // half, i, 0)),
               out_shape=(half, 2, seq, c_ff), out_dtype=F32)
    cb3 = w["conv_b"].reshape(ns, 1, c_ff)
    m_ff = _conv_gate_fwd(a_ff, w["conv_w"], cb3)
    d_ff = half * c_ff
    wdn = w["wdown"]
    tnd = _tile(d, 1024)
    tmx, tnx = min(1024, seq), _tile(d, 1024)
    h2 = _mm2d("ffn_down", m_ff, wdn, NN, F32, tm=512, tn=512, tk=d_ff, res=h1)

    def loss_fn(hb, tb, wv):
        def f(hh, ww):
            err = _rms(hh, ww) - tb
            return 0.5 * jnp.sum(jnp.mean(err * err, axis=-1))

        lossv, (dh, dw) = jax.value_and_grad(f, argnums=(0, 1))(hb, wv)
        return dh, dh, jnp.full((1, LANES), lossv, F32), dw

    fin_w = w["final_norm"].reshape(1, d)
    dh2, dh2b, loss_acc, g_final = _blockwise(
        "loss_head", loss_fn, [h2, target, fin_w], [_row_spec(t_row, d), _row_spec(t_row, d), _full_spec((1, d))],
        [((seq, d), F32), ((seq, d), BF16), ((1, LANES), F32), ((1, d), F32)],
        [_row_spec(t_row, d), _row_spec(t_row, d), _full_spec((1, LANES)), _full_spec((1, d))], g1, n_acc=2)
    loss = loss_acc

    dm = _mm2d("ffn_down_dx", dh2b, wdn, NT, F32, tn=c_ff)
    tks = seq
    g_wdown = _mm2d("ffn_down_dw", m_ff, dh2b, TN, BF16, tm=c_ff)
    emit(wdown=g_wdown)
    da_ff, g_convw2, g_convb2 = _conv_gate_bwd(a_ff, w["conv_w"], cb3, dm)
    g_convw = jnp.swapaxes(g_convw2, 0, 1).reshape(ns, 3, c_ff)
    g_convb = jnp.swapaxes(g_convb2, 0, 1).reshape(ns, 1, c_ff)
    g_wup = _mm("ffn_up_dw", hn2, da_ff, grid=(ns, d // tnd, seq // tks), contract=TN,
                a_spec=pl.BlockSpec((tks, tnd), lambda s, j, k: (k, j)),
                b_spec=pl.BlockSpec((None, None, tks, c_ff), lambda s, j, k: (s % half, s // half, k, 0)),
                o_spec=pl.BlockSpec((None, tnd, c_ff), lambda s, j, k: (s, j, 0)),
                out_shape=(ns, d, c_ff), out_dtype=BF16)
    emit(wup=g_wup)
    dhn2 = _mm("ffn_up_dx", da_ff, w["wup"], grid=(seq // tmx, d // tnx, ns), contract=NT,
               a_spec=pl.BlockSpec((None, None, tmx, c_ff), lambda i, j, s: (s % half, s // half, i, 0)),
               b_spec=pl.BlockSpec((None, tnx, c_ff), lambda i, j, s: (s, j, 0)),
               o_spec=pl.BlockSpec((tmx, tnx), lambda i, j, s: (i, j)),
               out_shape=(seq, d), out_dtype=F32)
    emit(wup_pair_sums_after=dhn2)

    def norm_bwd_fn(hb, dres, dn, wv):
        dx_, dw_ = _rms_bwd(hb, wv, dn)
        dtot = dres + dx_
        return dtot, dtot, dw_

    dh1, dh1b, g_ffn_norm = _blockwise(
        "norm2_bwd", norm_bwd_fn, [h1, dh2, dhn2, w["ffn_norm"]],
        [_row_spec(t_row, d)] * 3 + [_full_spec((1, d))],
        [((seq, d), F32), ((seq, d), BF16), ((1, d), F32)],
        [_row_spec(t_row, d), _row_spec(t_row, d), _full_spec((1, d))], g1, n_acc=1)

    g_wout = _mm2d("out_proj_dw", ycat, dh1b, TN, BF16)

    def outnorm_bwd_fn(dhb, wo, ys, ym, ws, wm):
        dyc = lax.dot_general(dhb, wo, (NT, ((), ())), preferred_element_type=F32)
        dys, dws = _rms_bwd(ys, ws, dyc[:, :ssm_w])
        dym, dwm = _rms_bwd(ym, wm, dyc[:, ssm_w:])
        return dys, dym, dws, dwm

    dy_ssm, dy_mla, g_son, g_mon = _blockwise(
        "out_proj_dx_norm_bwd", outnorm_bwd_fn, [dh1b, w["wout"], y_ssm, y_mla, w["son"], w["mon"]],
        [_row_spec(t_wide, d), _full_spec((d, d), single=True), wide, _row_spec(t_wide, mla_w),
         _full_spec((1, ssm_w)), _full_spec((1, mla_w))],
        [((seq, ssm_w), F32), ((seq, mla_w), F32), ((1, ssm_w), F32), ((1, mla_w), F32)],
        [wide, _row_spec(t_wide, mla_w), _full_spec((1, ssm_w)), _full_spec((1, mla_w))],
        g_wide, n_acc=2)

    def glu_bwd_fn(dy, yp, zb, ub, dsk, wg):
        ygv = jax.nn.gelu(yp)
        sg = jax.nn.sigmoid(zb)
        dz = dy * ygv * sg * (1.0 - sg)
        dzb = dz.astype(BF16)
        dyg = dy * sg + lax.dot_general(dzb, wg, (NT, ((), ())), preferred_element_type=F32)
        _, vjp = jax.vjp(jax.nn.gelu, yp)
        dyp = vjp(dyg)[0]
        return (dzb, dyp, dyp * dsk, jnp.sum(dz, axis=0, keepdims=True), jnp.sum(dyp * ub, axis=0, keepdims=True))

    dz, dy_pre, du1, g_bglu, g_ssmd = _blockwise(
        "ssm_glu_bwd", glu_bwd_fn, [dy_ssm, y_pre, z, proj, w["ssm_d"], w["wglu"]],
        [wide] * 4 + [_full_spec((1, ssm_w)), _full_spec((ssm_w, ssm_w), single=True)],
        [((seq, ssm_w), BF16), ((seq, ssm_w), BF16), ((seq, ssm_w), F32), ((1, ssm_w), F32), ((1, ssm_w), F32)],
        [wide] * 3 + [_full_spec((1, ssm_w))] * 2, g_wide, n_acc=2)
    g_wglu = _mm2d("ssm_glu_dw", yg, dz, TN, BF16)
    dq_raw, dkv, dkp_h = _attn_bwd(q_raw, kv, kpe, cos_t, sa_t, sb_t, dy_mla)

    def head_mm_dx(name, dact, wh):
        kdim, ndim = wh.shape[1], wh.shape[2]
        return _mm(name, dact, wh, grid=(1, 1, nh), contract=NT,
                   a_spec=pl.BlockSpec((None, seq, ndim), lambda i, j, h: (h, i, 0)),
                   b_spec=pl.BlockSpec((None, kdim, ndim), lambda i, j, h: (h, 0, 0)),
                   o_spec=pl.BlockSpec((seq, kdim), lambda i, j, h: (i, 0)),
                   out_shape=(seq, kdim), out_dtype=F32)

    def head_mm_dw(name, act, dact):
        kdim, ndim = act.shape[1], dact.shape[2]
        return _mm(name, act, dact, grid=(nh, 1, seq // tks), contract=TN,
                   a_spec=pl.BlockSpec((tks, kdim), lambda h, j, k: (k, 0)),
                   b_spec=pl.BlockSpec((None, tks, ndim), lambda h, j, k: (h, k, 0)),
                   o_spec=pl.BlockSpec((None, kdim, ndim), lambda h, j, k: (h, 0, 0)),
                   out_shape=(nh, kdim, ndim), out_dtype=BF16)

    g_wuq = head_mm_dw("mla_q_dw", qn, dq_raw)
    g_wukv = head_mm_dw("mla_kv_dw", kvn, dkv)
    dqn = head_mm_dx("mla_q_dx", dq_raw, w["wuq"])
    dkvn = head_mm_dx("mla_kv_dx", dkv, w["wukv"])
    emit(not_before=(dqn, dkvn, dy_pre), wout=g_wout, wuq=g_wuq, wukv=g_wukv, wglu=g_wglu, conv_w=g_convw)

    du, dwb, dwc, da_lay = _ssm_bwd(dy_pre, s_all, proj, du1, wb, wc, a_lay)
    g_c_re = blockdiag_out_t(dwc[:, :STATE_BLOCK, :])
    g_c_im = -blockdiag_out_t(dwc[:, STATE_BLOCK:, :])
    dbbt_re = blockdiag_in_t(dwb[:, :, :STATE_BLOCK])
    dbbt_im = blockdiag_in_t(dwb[:, :, STATE_BLOCK:])
    da3 = da_lay.reshape(nj, 2, STATE_BLOCK)
    dabar_re = da3[:, 0, :].reshape(n_groups, 1, SSM_STATE)
    dabar_im = da3[:, 1, :].reshape(n_groups, 1, SSM_STATE)
    g_lr3, g_li3, g_ldt3, g_bt_re, g_bt_im = _s5_prep_bwd(lr3, li3, ldt3, bt_re, bt_im,
                                                           dabar_re, dabar_im, dbbt_re, dbbt_im)

    def mla_prep_bwd_fn(cq, ckv, dqn_b, dkvn_b, dkp_b, cos, sa, sb, wq, wkv):
        dcq, dwq = _rms_bwd(cq, wq, dqn_b)
        dckv, dwkv = _rms_bwd(ckv, wkv, dkvn_b)
        dkp_sum = dkp_b[0]
        for h in range(1, nh):
            dkp_sum = dkp_sum + dkp_b[h]
        return dcq, dckv, _rope128_t(dkp_sum, cos, sa, sb), dwq, dwkv

    dc_q, dc_kv, dkpe_raw, g_qnorm, g_kvnorm = _blockwise(
        "mla_prep_bwd", mla_prep_bwd_fn, [proj, proj, dqn, dkvn, dkp_h, cos_t, sa_t, sb_t, w["q_norm"], w["kv_norm"]],
        [cq_spec, ckv_spec, _row_spec(t_row, q_rank), _row_spec(t_row, kv_rank),
         pl.BlockSpec((nh, t_row, LANES), lambda i: (0, i, 0))] + [_row_spec(t_row, LANES)] * 3
        + [_full_spec((1, q_rank)), _full_spec((1, kv_rank))],
        [((seq, q_rank), BF16), ((seq, kv_rank), BF16), ((seq, LANES), BF16), ((1, q_rank), F32), ((1, kv_rank), F32)],
        [_row_spec(t_row, q_rank), _row_spec(t_row, kv_rank), _row_spec(t_row, LANES), _full_spec((1, q_rank)),
         _full_spec((1, kv_rank))], g1, n_acc=2)

    dproj = jnp.concatenate([du, dc_q, dc_kv, dkpe_raw], axis=1)
    g_win = _mm2d("proj_dw", hn, dproj, TN, BF16, tn=640)
    emit(win=g_win)
    def norm1_bwd_fn(dpb, wi, xb, dres, wv):
        dn = lax.dot_general(dpb, wi, (NT, ((), ())), preferred_element_type=F32)
        dx_, dw_ = _rms_bwd(xb, wv, dn)
        return dres + dx_, dw_

    grad_x, g_attn_norm = _blockwise(
        "proj_dx_norm1_bwd", norm1_bwd_fn, [dproj, w["win"], x, dh1, attn_w],
        [_row_spec(t_row, in_pad), _full_spec((d, in_pad), single=True), _row_spec(t_row, d), _row_spec(t_row, d), _full_spec((1, d))],
        [((seq, d), F32), ((1, d), F32)], [_row_spec(t_row, d), _full_spec((1, d))], g1, n_acc=1)
    emit(win_pair_sums_after=grad_x)

    grads = dict(
        attn_norm=g_attn_norm, win=g_win, lam_re=g_lr3, lam_im=g_li3, log_dt=g_ldt3,
        bt_re=g_bt_re, bt_im=g_bt_im, c_re=g_c_re, c_im=g_c_im,
        ssm_d=g_ssmd, wglu=g_wglu, b_glu=g_bglu, q_norm=g_qnorm, wuq=g_wuq, kv_norm=g_kvnorm, wukv=g_wukv,
        son=g_son, mon=g_mon, wout=g_wout, ffn_norm=g_ffn_norm, wup=g_wup, conv_w=g_convw, conv_b=g_convb,
        wdown=g_wdown, final_norm=g_final)
    return loss, grad_x, grads


def _mesh_pos():
    return lax.axis_index("x"), lax.axis_index("y"), lax.axis_index("c")


def _handshake_all():
    x, y, c = _mesh_pos()
    barrier = pltpu.get_barrier_semaphore()
    for k in range(1, N_DEV):
        peer = (1 - x if k & 4 else x, 1 - y if k & 2 else y, 1 - c if k & 1 else c)
        pl.semaphore_signal(barrier, inc=1, device_id=peer, device_id_type=MESH)
    pl.semaphore_wait(barrier, N_DEV - 1)


def _handshake(peers):
    barrier = pltpu.get_barrier_semaphore()
    for peer in peers:
        pl.semaphore_signal(barrier, inc=1, device_id=peer, device_id_type=MESH)
    pl.semaphore_wait(barrier, len(peers))


def _comm_call(name, body, n, out_shape, ins, collective_id, after=None, copies=7, n_remote=None, n_local=None):
    n_remote = copies * n if n_remote is None else n_remote
    sems = [pltpu.SemaphoreType.DMA((n_remote,)), pltpu.SemaphoreType.DMA((n_remote,)),
            pltpu.SemaphoreType.DMA((n if n_local is None else n_local,))]
    if collective_id is None:
        any_spec = pl.BlockSpec(memory_space=pl.ANY)
        return pl.pallas_call(body, name=name, out_shape=out_shape, in_specs=[any_spec] * n,
                              out_specs=[any_spec] * n, scratch_shapes=sems)(*ins)
    seq_body = body
    if after:
        n_after = len(after)
        ins = list(ins) + list(after)

        def seq_body(*refs):
            body(*refs[:n], *refs[n + n_after:])

    return pl.kernel(seq_body, name=name, out_type=out_shape,
                     mesh=plsc.ScalarSubcoreMesh(axis_name="seq", num_cores=1), scratch_types=sems,
                     compiler_params=pltpu.CompilerParams(collective_id=collective_id))(*ins)


def _all_gather(name, xs, collective_id=None, after=None, pair_sums=()):
    n = len(xs)
    nh = len(pair_sums)
    m = n + nh

    def body(*refs):
        x_refs, h_refs, o_refs, e_refs = refs[:n], refs[n:m], refs[m:m + n], refs[m + n:2 * m]
        send_sems, recv_sems, local_sems = refs[2 * m:]
        if collective_id is not None:
            _handshake_all()
        finish_pairs = _chip_copies(h_refs, e_refs, send_sems, recv_sems, local_sems, 7 * n, n) if nh else None
        x, y, c = _mesh_pos()
        me, sibling = (x, y, c), (x, y, 1 - c)
        chips = [(1 - x, y), (x, 1 - y), (1 - x, 1 - y)]

        def slot(o_ref, px, py, pc):
            return o_ref.at[4 * px + 2 * py + pc]

        def copy(t, k, block, to, src=None):
            dst = slot(o_refs[t], *block)
            return pltpu.make_async_remote_copy(
                src_ref=dst if src is None else src, dst_ref=dst,
                send_sem=send_sems.at[7 * t + k], recv_sem=recv_sems.at[7 * t + k],
                device_id=to, device_id_type=MESH)

        started = []
        for t in range(n):
            mine = pltpu.make_async_copy(x_refs[t], slot(o_refs[t], *me), local_sems.at[t])
            mine.start()
            started.append(mine)
        first = []
        for t in range(n):
            first.append(copy(t, 0, me, sibling, src=x_refs[t]))
            first += [copy(t, 1 + j, me, (*chip, c), src=x_refs[t]) for j, chip in enumerate(chips)]
        for cp in first:
            cp.start()
        passed = []
        for j, chip in enumerate(chips):
            for t in range(n):
                copy(t, 1 + j, (*chip, c), me).wait_recv()
                fwd = copy(t, 4 + j, (*chip, c), sibling)
                fwd.start()
                passed.append(fwd)
        for t in range(n):
            copy(t, 0, sibling, me).wait_recv()
            for j, chip in enumerate(chips):
                copy(t, 4 + j, (*chip, 1 - c), me).wait_recv()
        for cp in first + passed:
            cp.wait_send()
        for mine in started:
            mine.wait()
        if nh:
            finish_pairs()

    out_shape = ([jax.ShapeDtypeStruct((N_DEV,) + v.shape, v.dtype) for v in xs]
                 + [jax.ShapeDtypeStruct(v.shape, v.dtype) for v in pair_sums])
    return _comm_call(name, body, m, out_shape, list(xs) + list(pair_sums), collective_id, after,
                      n_remote=7 * n + (N_CHIP - 1) * nh, n_local=m)


def _exchange_partials(name, gs, collective_id=None, after=None):
    n = len(gs)

    def body(*refs):
        g_refs, o_refs = refs[:n], refs[n:2 * n]
        send_sems, recv_sems, local_sems = refs[2 * n:]
        if collective_id is not None:
            _handshake_all()
        x, y, c = _mesh_pos()
        me_idx = 4 * x + 2 * y + c
        copies = []
        for t in range(n):
            mine = pltpu.make_async_copy(g_refs[t].at[me_idx], o_refs[t].at[me_idx], local_sems.at[t])
            mine.start()
            copies.append(mine)
        remote = []
        for k in range(1, N_DEV):
            px = 1 - x if k & 4 else x
            py = 1 - y if k & 2 else y
            pc = 1 - c if k & 1 else c
            p_idx = 4 * px + 2 * py + pc
            for t in range(n):
                cp = pltpu.make_async_remote_copy(
                    src_ref=g_refs[t].at[p_idx], dst_ref=o_refs[t].at[me_idx],
                    send_sem=send_sems.at[7 * t + k - 1], recv_sem=recv_sems.at[7 * t + k - 1],
                    device_id=(px, py, pc), device_id_type=MESH)
                cp.start()
                landing = pltpu.make_async_remote_copy(
                    src_ref=g_refs[t].at[p_idx], dst_ref=o_refs[t].at[p_idx],
                    send_sem=send_sems.at[7 * t + k - 1], recv_sem=recv_sems.at[7 * t + k - 1],
                    device_id=(px, py, pc), device_id_type=MESH)
                remote.append((cp, landing))
        for cp, landing in remote:
            landing.wait_recv()
        for cp, landing in remote:
            cp.wait_send()
        for mine in copies:
            mine.wait()

    out_shape = [jax.ShapeDtypeStruct(v.shape, v.dtype) for v in gs]
    return _comm_call(name, body, n, out_shape, gs, collective_id, after)


N_CHIP = N_DEV // 2
PAIR_ADD_BLOCK_ELEMS = 1024 * 1024


def _pair_swap(name, gs, collective_id, after=None):
    n = len(gs)

    def body(*refs):
        g_refs, o_refs = refs[:n], refs[n:2 * n]
        send_sems, recv_sems, _ = refs[2 * n:]
        x, y, c = _mesh_pos()
        sibling = (x, y, 1 - c)
        _handshake([sibling])
        copies = []
        for t in range(n):
            for k in range(N_CHIP):
                copies.append(pltpu.make_async_remote_copy(
                    src_ref=g_refs[t].at[2 * k + 1 - c], dst_ref=o_refs[t].at[k],
                    send_sem=send_sems.at[N_CHIP * t + k], recv_sem=recv_sems.at[N_CHIP * t + k],
                    device_id=sibling, device_id_type=MESH))
        for cp in copies:
            cp.start()
        for cp in copies:
            cp.wait_recv()
        for cp in copies:
            cp.wait_send()

    out_shape = [jax.ShapeDtypeStruct((N_CHIP,) + v.shape[1:], v.dtype) for v in gs]
    return _comm_call(name, body, n, out_shape, gs, collective_id, after, copies=N_CHIP)


def _pair_add(name, g, got):
    _, r, c = g.shape
    tr = r
    if r * c > PAIR_ADD_BLOCK_ELEMS and r % SUBLANES == 0:
        tr = SUBLANES
        while r % (tr * 2) == 0 and tr * 2 * c <= PAIR_ADD_BLOCK_ELEMS:
            tr *= 2

    def body(core_ref, g_ref, got_ref, o_ref):
        o_ref[...] = (g_ref[...].astype(F32) + got_ref[...].astype(F32)).astype(o_ref.dtype)

    grid_spec = pltpu.PrefetchScalarGridSpec(
        num_scalar_prefetch=1, grid=(N_CHIP, r // tr),
        in_specs=[pl.BlockSpec((None, None, tr, c), lambda k, i, core: (k, core[0], i, 0)),
                  pl.BlockSpec((None, tr, c), lambda k, i, core: (k, i, 0))],
        out_specs=pl.BlockSpec((None, tr, c), lambda k, i, core: (k, i, 0)))
    core = lax.axis_index("c").astype(jnp.int32).reshape(1)
    return pl.pallas_call(body, name=name, grid_spec=grid_spec, out_shape=jax.ShapeDtypeStruct((N_CHIP, r, c), g.dtype),
                          compiler_params=_cparams())(core, g.reshape(N_CHIP, 2, r, c), got)


def _chip_copies(h_refs, o_refs, send_sems, recv_sems, local_sems, sem0, local0):
    n = len(h_refs)
    per = N_CHIP - 1
    x, y, c = _mesh_pos()
    others = [(1 - x if k & 2 else x, 1 - y if k & 1 else y) for k in range(1, N_CHIP)]
    my_chip = 2 * x + y
    local = []
    for t in range(n):
        mine = pltpu.make_async_copy(h_refs[t].at[my_chip], o_refs[t].at[my_chip], local_sems.at[local0 + t])
        mine.start()
        local.append(mine)
    remote = []
    for j, (px, py) in enumerate(others):
        chip = 2 * px + py
        for t in range(n):
            sems = dict(send_sem=send_sems.at[sem0 + per * t + j], recv_sem=recv_sems.at[sem0 + per * t + j],
                        device_id=(px, py, c), device_id_type=MESH)
            cp = pltpu.make_async_remote_copy(src_ref=h_refs[t].at[chip], dst_ref=o_refs[t].at[my_chip], **sems)
            cp.start()
            landing = pltpu.make_async_remote_copy(src_ref=h_refs[t].at[chip], dst_ref=o_refs[t].at[chip], **sems)
            remote.append((cp, landing))

    def finish():
        for cp, landing in remote:
            landing.wait_recv()
        for cp, landing in remote:
            cp.wait_send()
        for mine in local:
            mine.wait()

    return finish


def _chip_exchange(name, hs, collective_id, after=None):
    n = len(hs)
    per = N_CHIP - 1

    def body(*refs):
        h_refs, o_refs = refs[:n], refs[n:2 * n]
        send_sems, recv_sems, local_sems = refs[2 * n:]
        x, y, c = _mesh_pos()
        _handshake([(1 - x if k & 2 else x, 1 - y if k & 1 else y, c) for k in range(1, N_CHIP)])
        _chip_copies(h_refs, o_refs, send_sems, recv_sems, local_sems, 0, 0)()

    out_shape = [jax.ShapeDtypeStruct(v.shape, v.dtype) for v in hs]
    return _comm_call(name, body, n, out_shape, hs, collective_id, after, copies=per)


ADAM_BLOCK_ELEMS = 256 * 1024


def _sum_parts(pb):
    g = pb[0].astype(F32)
    for j in range(1, pb.shape[0]):
        g = g + pb[j].astype(F32)
    return g


def _adam_math(g, wb_, mb, vb):
    m_new = ADAM_B1 * mb + (1.0 - ADAM_B1) * g
    v_new = ADAM_B2 * vb + (1.0 - ADAM_B2) * (g * g)
    m_hat = m_new / (1.0 - ADAM_B1 ** ADAM_STEP)
    v_hat = v_new / (1.0 - ADAM_B2 ** ADAM_STEP)
    delta = -ADAM_LR * (m_hat / (jnp.sqrt(v_hat) + ADAM_EPS) + ADAM_WD * wb_)
    return g, delta, m_new, v_new


def _adamw_multi(name, items, nblk=1, packed=None):
    n = len(items)

    def spec(shape, lead):
        blk = list(shape)
        blk[lead + 1] = shape[lead + 1] // nblk
        if nblk == 1:
            return pl.BlockSpec(tuple(blk), lambda i, nd=len(shape): (0,) * nd)
        return pl.BlockSpec(tuple(blk), lambda i, nd=len(shape), ax=lead + 1: (0,) * ax + (i,) + (0,) * (nd - ax - 1))

    ins, in_specs, out_specs, out_shape, where = [], [], [], [], []
    if packed is not None:
        ins.append(packed)
        in_specs.append(spec(packed.shape, 1))
    for parts, wv, mv, vv in items:
        if isinstance(parts, int):
            where.append((0, parts, len(ins)))
        else:
            assert parts.shape[1:] == wv.shape, (name, parts.shape, wv.shape)
            where.append((len(ins), None, len(ins) + 1))
            ins.append(parts)
            in_specs.append(spec(parts.shape, 1))
        ins += [wv, mv, vv]
        in_specs += [spec(wv.shape, 0)] * 3
        out_specs += [spec(wv.shape, 0)] * 4
        out_shape += [jax.ShapeDtypeStruct(wv.shape, F32)] * 4
    n_in = len(ins)

    def body(*refs):
        for t, (ip, off, iw) in enumerate(where):
            wr, mr, vr = refs[iw:iw + 3]
            parts = refs[ip][...] if off is None else refs[ip][:, :, off:off + wr.shape[-1]]
            res = _adam_math(_sum_parts(parts), wr[...], mr[...], vr[...])
            for o, val in zip(refs[n_in + 4 * t:n_in + 4 * t + 4], res):
                o[...] = val

    res = pl.pallas_call(body, name=name, grid=(nblk,), in_specs=in_specs, out_specs=out_specs, out_shape=out_shape,
                         compiler_params=_cparams())(*ins)
    return [tuple(res[4 * t:4 * t + 4]) for t in range(n)]


def _sum_multi(name, parts_list):
    def body(*refs):
        for pr, o in zip(refs[:len(parts_list)], refs[len(parts_list):]):
            o[...] = _sum_parts(pr[...])

    return pl.pallas_call(body, name=name, out_shape=[jax.ShapeDtypeStruct(p.shape[1:], F32) for p in parts_list],
                          compiler_params=_cparams())(*parts_list)


def _adamw_sum(name, parts, wv, mv, vv):
    npart, r, c = parts.shape
    tr = r
    if r * c > ADAM_BLOCK_ELEMS and r % SUBLANES == 0:
        tr = SUBLANES
        while r % (tr * 2) == 0 and tr * 2 * c <= ADAM_BLOCK_ELEMS:
            tr *= 2

    def fn(pb, wb_, mb, vb):
        return _adam_math(_sum_parts(pb), wb_, mb, vb)

    row = pl.BlockSpec((tr, c), lambda i: (i, 0))
    return _blockwise(name, fn, [parts, wv, mv, vv],
                      [pl.BlockSpec((npart, tr, c), lambda i: (0, i, 0)), row, row, row],
                      [((r, c), F32)] * 4, [row] * 4, (r // tr,))


_VECTORS = ["attn_norm", "lam_re", "lam_im", "log_dt", "ssm_d", "b_glu", "q_norm", "kv_norm", "son", "mon",
            "ffn_norm", "conv_b", "final_norm"]
_GHP = ["c_re", "c_im", "bt_re", "bt_im"]
_PACKED = ["attn_norm", "ssm_d", "b_glu", "q_norm", "kv_norm", "son", "mon", "ffn_norm", "conv_b", "final_norm"]
_BIG = ["win", "wglu", "wuq", "wukv", "wout", "wup", "wdown", "conv_w"]
_ROWS_IN_LANES = ("win", "wuq")
_TWO_LEVEL = ("wup", "win")
_AFTER = "_pair_sums_after"
_ORDER = ["attn_norm", "win", "lam_re", "lam_im", "log_dt", "b_re", "b_im", "c_re", "c_im", "ssm_d", "wglu",
          "b_glu", "q_norm", "wuq", "kv_norm", "wukv", "son", "mon", "wout", "ffn_norm", "wup", "conv_w",
          "conv_b", "wdown", "final_norm"]


def kernel(x, positions, attn_norm_w, w_in, ssm_lambda_re, ssm_lambda_im, ssm_log_dt, ssm_b_re, ssm_b_im, ssm_c_re, ssm_c_im, ssm_d, ssm_w_glu, ssm_b_glu, mla_q_norm_w, mla_w_uq, mla_kv_norm_w, mla_w_ukv, ssm_out_norm_w, mla_out_norm_w, w_out, ffn_norm_w, ffn_w_up, ffn_conv_w, ffn_conv_b, ffn_w_down, final_norm_w, loss_target, m_attn_norm_w, m_w_in, m_ssm_lambda_re, m_ssm_lambda_im, m_ssm_log_dt, m_ssm_b_re, m_ssm_b_im, m_ssm_c_re, m_ssm_c_im, m_ssm_d, m_ssm_w_glu, m_ssm_b_glu, m_mla_q_norm_w, m_mla_w_uq, m_mla_kv_norm_w, m_mla_w_ukv, m_ssm_out_norm_w, m_mla_out_norm_w, m_w_out, m_ffn_norm_w, m_ffn_w_up, m_ffn_conv_w, m_ffn_conv_b, m_ffn_w_down, m_final_norm_w, v_attn_norm_w, v_w_in, v_ssm_lambda_re, v_ssm_lambda_im, v_ssm_log_dt, v_ssm_b_re, v_ssm_b_im, v_ssm_c_re, v_ssm_c_im, v_ssm_d, v_ssm_w_glu, v_ssm_b_glu, v_mla_q_norm_w, v_mla_w_uq, v_mla_kv_norm_w, v_mla_w_ukv, v_ssm_out_norm_w, v_mla_out_norm_w, v_w_out, v_ffn_norm_w, v_ffn_w_up, v_ffn_conv_w, v_ffn_conv_b, v_ffn_w_down, v_final_norm_w):
    wts = dict(attn_norm=attn_norm_w, win=w_in, lam_re=ssm_lambda_re, lam_im=ssm_lambda_im, log_dt=ssm_log_dt,
               b_re=ssm_b_re, b_im=ssm_b_im, c_re=ssm_c_re, c_im=ssm_c_im, ssm_d=ssm_d, wglu=ssm_w_glu,
               b_glu=ssm_b_glu, q_norm=mla_q_norm_w, wuq=mla_w_uq, kv_norm=mla_kv_norm_w, wukv=mla_w_ukv,
               son=ssm_out_norm_w, mon=mla_out_norm_w, wout=w_out, ffn_norm=ffn_norm_w, wup=ffn_w_up,
               conv_w=ffn_conv_w, conv_b=ffn_conv_b, wdown=ffn_w_down, final_norm=final_norm_w)
    moms = dict(zip(_ORDER, [m_attn_norm_w, m_w_in, m_ssm_lambda_re, m_ssm_lambda_im, m_ssm_log_dt, m_ssm_b_re,
                             m_ssm_b_im, m_ssm_c_re, m_ssm_c_im, m_ssm_d, m_ssm_w_glu, m_ssm_b_glu, m_mla_q_norm_w,
                             m_mla_w_uq, m_mla_kv_norm_w, m_mla_w_ukv, m_ssm_out_norm_w, m_mla_out_norm_w, m_w_out,
                             m_ffn_norm_w, m_ffn_w_up, m_ffn_conv_w, m_ffn_conv_b, m_ffn_w_down, m_final_norm_w]))
    vels = dict(zip(_ORDER, [v_attn_norm_w, v_w_in, v_ssm_lambda_re, v_ssm_lambda_im, v_ssm_log_dt, v_ssm_b_re,
                             v_ssm_b_im, v_ssm_c_re, v_ssm_c_im, v_ssm_d, v_ssm_w_glu, v_ssm_b_glu, v_mla_q_norm_w,
                             v_mla_w_uq, v_mla_kv_norm_w, v_mla_w_ukv, v_ssm_out_norm_w, v_mla_out_norm_w, v_w_out,
                             v_ffn_norm_w, v_ffn_w_up, v_ffn_conv_w, v_ffn_conv_b, v_ffn_w_down, v_final_norm_w]))
    seq, d = x.shape[1], x.shape[2]
    in_width = w_in.shape[2]
    in_pad = -(-in_width // LANES) * LANES
    q_cols = mla_w_uq.shape[2]
    q_pad = 2 * LANES

    (win_g,) = _all_gather("gather_w_in", [jnp.pad(w_in[0], ((0, 0), (0, in_pad - in_width))).astype(BF16)])
    wglu_g, wuq_g, wukv_g, wout_g, convw_g = _all_gather(
        "gather_mix", [ssm_w_glu[0].astype(BF16), jnp.pad(mla_w_uq[0], ((0, 0), (0, q_pad - q_cols))).astype(BF16),
                       mla_w_ukv[0].astype(BF16), w_out[0].astype(BF16), ffn_conv_w[0]], collective_id=0)
    (wup_g,) = _all_gather("gather_ffn_up", [ffn_w_up[0].astype(BF16)], collective_id=1)
    (wdown_g,) = _all_gather("gather_ffn_down", [ffn_w_down[0].astype(BF16)], collective_id=2)
    ns = N_DEV
    c_ff = wup_g.shape[2]
    w = dict(
        attn_norm=attn_norm_w, win=win_g.reshape(d, in_pad), lam_re=ssm_lambda_re, lam_im=ssm_lambda_im,
        log_dt=ssm_log_dt, b_re=ssm_b_re, b_im=ssm_b_im, c_re=ssm_c_re, c_im=ssm_c_im, ssm_d=ssm_d,
        wglu=wglu_g.reshape(d // 2, d // 2), b_glu=ssm_b_glu, q_norm=mla_q_norm_w, wuq=wuq_g,
        kv_norm=mla_kv_norm_w, wukv=wukv_g, son=ssm_out_norm_w, mon=mla_out_norm_w, wout=wout_g.reshape(d, d),
        ffn_norm=ffn_norm_w, wup=wup_g, conv_w=convw_g, conv_b=ffn_conv_b,
        wdown=wdown_g.reshape(ns // 2 * c_ff, d), final_norm=final_norm_w)

    shard_layout = dict(
        win=lambda a: a[:, :in_width].reshape(N_DEV, d // N_DEV, in_width),
        wglu=lambda a: a.reshape(N_DEV, d // 2 // N_DEV, d // 2),
        wuq=lambda a: a[:, :, :q_cols], wukv=lambda a: a, wout=lambda a: a.reshape(N_DEV, d // N_DEV, d),
        wup=lambda a: a, wdown=lambda a: a.reshape(N_DEV, c_ff // 2, d), conv_w=lambda a: a)
    recv = {}
    next_id = [3]

    last = [None]

    out = {}

    def update(k):
        shp = wts[k].shape
        r, c = shp[-2], shp[-1]
        if k in _ROWS_IN_LANES:
            t = lambda a: jnp.swapaxes(a.reshape(-1, r, c), 1, 2)
            res = _adamw_sum("adamw_" + k, t(recv[k]), t(wts[k])[0], t(moms[k])[0], t(vels[k])[0])
            out[k] = [jnp.swapaxes(a, 0, 1).reshape(shp) for a in res]
            return res[0]
        res = _adamw_sum("adamw_" + k, recv[k].reshape(-1, r, c), wts[k].reshape(r, c),
                         moms[k].reshape(r, c), vels[k].reshape(r, c))
        out[k] = [a.reshape(shp) for a in res]
        return res[0]

    pending = {}

    def exchange(not_before=(), **grads):
        names = list(grads)
        if len(names) == 1 and names[0] in _TWO_LEVEL:
            k = names[0]
            parts = shard_layout[k](grads[k])
            got = _pair_swap("swap_" + k, [parts], collective_id=next_id[0], after=[last[0]])[0]
            next_id[0] += 1
            pending[k] = (parts, got)
            last[0] = got
            return
        if len(names) == 1 and names[0].endswith(_AFTER):
            k = names[0][:-len(_AFTER)]
            sums = _pair_add("pair_add_" + k, *pending[k])
            if k == "win":
                pending["tail"] = sums
                return
            recv[k] = _chip_exchange("exchange_" + k, [sums], collective_id=next_id[0],
                                     after=[last[0], grads[names[0]]])[0]
            next_id[0] += 1
            last[0] = recv[k]
            return
        got = _exchange_partials("exchange_" + "_".join(names), [shard_layout[k](grads[k]) for k in names],
                                 collective_id=next_id[0], after=[a for a in (last[0], *not_before) if a is not None])
        next_id[0] += 1
        last[0] = got[-1]
        recv.update(zip(names, got))

    loss_part, grad_x, g = _local_step(x[0], positions[0], loss_target[0], w, emit=exchange)
    n_groups = ssm_lambda_re.shape[1]
    two_d = {"lam_re": (n_groups, -1), "lam_im": (n_groups, -1)}
    dense = {k: g[k].reshape(two_d.get(k, (1, -1))) for k in _VECTORS}
    offsets, width = {}, 0
    for k in _PACKED:
        offsets[k] = width
        width += dense[k].shape[1]
    sent = dict(packed=jnp.concatenate([dense[k] for k in _PACKED], axis=1),
                **{k: dense[k] for k in _VECTORS if k not in _PACKED},
                **{k: g[k].reshape(n_groups, -1) for k in _GHP},
                loss=loss_part)
    names = list(sent)
    got = _all_gather("gather_small_grads", [sent[k] for k in names], collective_id=next_id[0], after=[last[0]],
                      pair_sums=[pending["tail"]])
    gathered = dict(zip(names, got))
    recv["win"] = got[len(names)]
    for k in _BIG:
        if k not in out and k != "win":
            update(k)
    update("win")

    def finish(keys, results):
        for k, res in zip(keys, results):
            out[k] = [a.reshape(wts[k].shape) for a in res]

    view = lambda k, a: a.reshape(dense[k].shape)
    finish(_VECTORS, _adamw_multi("adamw_vectors", [(offsets.get(k, gathered.get(k)), view(k, wts[k]), view(k, moms[k]),
                                                     view(k, vels[k])) for k in _VECTORS], packed=gathered["packed"]))
    summed = _GHP + ["loss"]
    sums = dict(zip(summed, _sum_multi("sum_ssm_bc_loss", [gathered[k] for k in summed])))
    loss = sums["loss"][0, 0]
    ghp = lambda k: sums[k].reshape(g[k].shape)
    t_hp = lambda a: jnp.swapaxes(a, 2, 3)
    bc_keys = ["c_re", "c_im", "b_re", "b_im"]
    items = [(ghp(k)[None, None], wts[k], moms[k], vels[k]) for k in bc_keys[:2]]
    items += [(ghp(t)[None, None], t_hp(wts[k]), t_hp(moms[k]), t_hp(vels[k]))
              for k, t in zip(bc_keys[2:], ("bt_re", "bt_im"))]
    res = _adamw_multi("adamw_ssm_bc", items)
    finish(bc_keys, res[:2] + [tuple(t_hp(a) for a in r) for r in res[2:]])

    grad_x = grad_x.reshape(x.shape)
    return (loss, grad_x, *[out[k][0] for k in _ORDER], *[out[k][1] for k in _ORDER],
            *[out[k][2] for k in _ORDER], *[out[k][3] for k in _ORDER])
```

```python
import functools

import jax
import jax.numpy as jnp
from jax import lax
from jax.experimental import pallas as pl
from jax.experimental.pallas import tpu as pltpu
from jax.experimental.pallas import tpu_sc as plsc

F32 = jnp.float32
BF16 = jnp.bfloat16
MESH = pl.DeviceIdType.MESH

N_DEV = 8
LANES = 128
SUBLANES = 8
VMEM_LIMIT = 48 * 1024 * 1024

SSM_GROUP = 16
SSM_STATE = 64
GROUPS_PER_BLOCK = LANES // SSM_GROUP
STATE_BLOCK = GROUPS_PER_BLOCK * SSM_STATE
QK_NOPE = 128
QK_ROPE = 64
V_DIM = 128
ROPE_THETA = 10000.0
RMS_EPS = 1e-6

ADAM_LR = 0.001
ADAM_B1 = 0.9
ADAM_B2 = 0.999
ADAM_EPS = 1e-08
ADAM_WD = 0.01
ADAM_STEP = 10

NN = ((1,), (0,))
NT = ((1,), (1,))
TN = ((0,), (0,))


def _cparams():
    return pltpu.CompilerParams(vmem_limit_bytes=VMEM_LIMIT)


def _tile(n, want):
    if n <= want:
        return n
    t = (want // LANES) * LANES
    while t >= LANES:
        if n % t == 0:
            return t
        t -= LANES
    return n


def _mm(name, a, b, *, grid, a_spec, b_spec, o_spec, out_shape, out_dtype, contract=NN,
        res=None, res_spec=None):
    nk = grid[-1]
    kaxis = len(grid) - 1
    acc_shape = tuple(d for d in o_spec.block_shape if d is not None)

    def body(*refs):
        a_ref, b_ref = refs[:2]
        r_ref = None if res is None else refs[2]
        o_ref = refs[2 if res is None else 3]
        part = lax.dot_general(a_ref[...].astype(BF16), b_ref[...].astype(BF16),
                               (contract, ((), ())), preferred_element_type=F32)
        if nk == 1:
            if r_ref is not None:
                part = part + r_ref[...].astype(F32)
            o_ref[...] = part.astype(o_ref.dtype)
            return
        acc = refs[-1]
        k = pl.program_id(kaxis)

        @pl.when(k == 0)
        def _():
            acc[...] = part

        @pl.when(k != 0)
        def _():
            acc[...] += part

        @pl.when(k == nk - 1)
        def _():
            r = acc[...]
            if r_ref is not None:
                r = r + r_ref[...].astype(F32)
            o_ref[...] = r.astype(o_ref.dtype)

    ins = [a, b] + ([] if res is None else [res])
    in_specs = [a_spec, b_spec] + ([] if res is None else [res_spec])
    return pl.pallas_call(
        body, name=name, grid=grid, in_specs=in_specs, out_specs=o_spec,
        out_shape=jax.ShapeDtypeStruct(out_shape, out_dtype),
        scratch_shapes=[pltpu.VMEM(acc_shape, F32)] if nk > 1 else [], compiler_params=_cparams(),
    )(*ins)


def _mm2d(name, a, b, contract, out_dtype, tm=1024, tn=1024, tk=2048, res=None):
    if contract == NN:
        (m, kk), n = a.shape, b.shape[1]
    elif contract == NT:
        (m, kk), n = a.shape, b.shape[0]
    else:
        (kk, m), n = a.shape, b.shape[1]
    tm, tn, tk = _tile(m, tm), _tile(n, tn), _tile(kk, tk)
    grid = (m // tm, n // tn, kk // tk)
    if contract == TN:
        a_spec = pl.BlockSpec((tk, tm), lambda i, j, k: (k, i))
    else:
        a_spec = pl.BlockSpec((tm, tk), lambda i, j, k: (i, k))
    if contract == NT:
        b_spec = pl.BlockSpec((tn, tk), lambda i, j, k: (j, k))
    else:
        b_spec = pl.BlockSpec((tk, tn), lambda i, j, k: (k, j))
    o_spec = pl.BlockSpec((tm, tn), lambda i, j, k: (i, j))
    res_spec = None
    if res is not None:
        if res.shape[0] == 1:
            res_spec = pl.BlockSpec((1, tn), lambda i, j, k: (0, j))
        else:
            res_spec = pl.BlockSpec((tm, tn), lambda i, j, k: (i, j))
    return _mm(name, a, b, grid=grid, a_spec=a_spec, b_spec=b_spec, o_spec=o_spec,
               out_shape=(m, n), out_dtype=out_dtype, contract=contract, res=res, res_spec=res_spec)


def _blockwise(name, fn, ins, in_specs, outs, out_specs, grid, n_acc=0, acc_all=True):
    n_in, n_out = len(ins), len(outs)
    n_plain = n_out - n_acc

    def body(*refs):
        vals = fn(*[r[...] for r in refs[:n_in]])
        if not isinstance(vals, (tuple, list)):
            vals = (vals,)
        o_refs = refs[n_in:n_in + n_out]
        for r, v in zip(o_refs[:n_plain], vals[:n_plain]):
            r[...] = v.astype(r.dtype)
        if n_acc:
            if acc_all:
                first = functools.reduce(jnp.logical_and, [pl.program_id(d) == 0 for d in range(len(grid))])
            else:
                first = pl.program_id(len(grid) - 1) == 0

            @pl.when(first)
            def _():
                for r, v in zip(o_refs[n_plain:], vals[n_plain:]):
                    r[...] = v.astype(r.dtype)

            @pl.when(jnp.logical_not(first))
            def _():
                for r, v in zip(o_refs[n_plain:], vals[n_plain:]):
                    r[...] += v.astype(r.dtype)

    return pl.pallas_call(
        body, name=name, grid=grid, in_specs=in_specs, out_specs=out_specs,
        out_shape=[jax.ShapeDtypeStruct(s, d) for s, d in outs], compiler_params=_cparams(),
    )(*ins)


def _row_spec(t, c):
    return pl.BlockSpec((t, c), lambda i: (i, 0))


def _full_spec(shape, single=False):
    nd = len(shape)
    if single:
        return pl.BlockSpec(tuple(shape), lambda *g: (0,) * nd, pipeline_mode=pl.Buffered(1))
    return pl.BlockSpec(tuple(shape), lambda *g: (0,) * nd)


def _rms(xf, w):
    return xf * lax.rsqrt(jnp.mean(xf * xf, axis=-1, keepdims=True) + RMS_EPS) * w


def _rms_bwd(xf, w, dy):
    _, vjp = jax.vjp(_rms, xf, w)
    return vjp(dy)


def _s5_disc(lr, li, ldt, bre, bim):
    dt = jnp.exp(ldt)
    mag = jnp.exp(lr * dt)
    ar = mag * jnp.cos(li * dt)
    ai = mag * jnp.sin(li * dt)
    nr, ni = ar - 1.0, ai
    den = lr * lr + li * li
    zr = (nr * lr + ni * li) / den
    zi = (ni * lr - nr * li) / den
    return ar, ai, zr * bre - zi * bim, zr * bim + zi * bre


def _s5_prep(lr, li, ldt, bre, bim):
    def body(lr_r, li_r, ldt_r, bre_r, bim_r, ar_r, ai_r, br_r, bi_r):
        ar, ai, br, bi = _s5_disc(lr_r[...], li_r[...], ldt_r[...], bre_r[...], bim_r[...])
        ar_r[...] = ar
        ai_r[...] = ai
        br_r[...] = br
        bi_r[...] = bi

    sd = jax.ShapeDtypeStruct
    return pl.pallas_call(
        body, name="s5_prep",
        out_shape=[sd(lr.shape, F32), sd(lr.shape, F32), sd(bre.shape, F32), sd(bre.shape, F32)],
        compiler_params=_cparams(),
    )(lr, li, ldt, bre, bim)


def _s5_prep_bwd(lr, li, ldt, bre, bim, dar, dai, dbr, dbi):
    def body(lr_r, li_r, ldt_r, bre_r, bim_r, dar_r, dai_r, dbr_r, dbi_r, o0, o1, o2, o3, o4):
        _, vjp = jax.vjp(_s5_disc, lr_r[...], li_r[...], ldt_r[...], bre_r[...], bim_r[...])
        g = vjp((dar_r[...], dai_r[...], dbr_r[...], dbi_r[...]))
        for o, v in zip((o0, o1, o2, o3, o4), g):
            o[...] = v

    sd = jax.ShapeDtypeStruct
    return pl.pallas_call(
        body, name="s5_prep_bwd",
        out_shape=[sd(lr.shape, F32), sd(li.shape, F32), sd(ldt.shape, F32), sd(bre.shape, F32), sd(bim.shape, F32)],
        compiler_params=_cparams(),
    )(lr, li, ldt, bre, bim, dar, dai, dbr, dbi)


SCAN_T = 256


def _scan_tables(ar, ai, tab_r, tab_i, sub, reverse):
    pr, pi = ar, ai
    for k in range(sub):
        row = sub - 1 - k if reverse else k
        tab_r[row:row + 1, :] = pr
        tab_i[row:row + 1, :] = pi
        pr, pi = ar * pr - ai * pi, ar * pi + ai * pr


def _pack_matrix(t_blk, dtype):
    sub = t_blk // SUBLANES
    dst = jnp.arange(t_blk)
    src = (dst % SUBLANES) * sub + dst // SUBLANES
    return (src[:, None] == jnp.arange(t_blk)[None, :]).astype(dtype)


def _permute_rows_f32(pm, x):
    hi = x.astype(BF16)
    r1 = x - hi.astype(F32)
    mid = r1.astype(BF16)
    lo = (r1 - mid.astype(F32)).astype(BF16)
    dot = lambda v: jnp.dot(pm, v, preferred_element_type=F32)
    return dot(hi) + dot(mid) + dot(lo)


def _scan_block(x, loc, ar, ai, st, tab_r, tab_i, sub, reverse):
    hb = STATE_BLOCK
    a8r = jnp.broadcast_to(ar, (SUBLANES, hb))
    a8i = jnp.broadcast_to(ai, (SUBLANES, hb))
    sr = jnp.zeros((SUBLANES, hb), F32)
    si = jnp.zeros((SUBLANES, hb), F32)
    steps = range(sub - 1, -1, -1) if reverse else range(sub)
    for t in steps:
        rows = slice(t * SUBLANES, (t + 1) * SUBLANES)
        sr, si = a8r * sr - a8i * si + x[rows, :hb], a8r * si + a8i * sr + x[rows, hb:]
        loc[rows, :hb] = sr
        loc[rows, hb:] = si
    cr, ci = st[0:1, :], st[1:2, :]
    far = 0 if reverse else sub - 1
    fr, fi = tab_r[far:far + 1, :], tab_i[far:far + 1, :]
    ent_r, ent_i = [None] * SUBLANES, [None] * SUBLANES
    for c in (range(SUBLANES - 1, -1, -1) if reverse else range(SUBLANES)):
        ent_r[c], ent_i[c] = cr, ci
        cr, ci = sr[c:c + 1, :] + (fr * cr - fi * ci), si[c:c + 1, :] + (fr * ci + fi * cr)
    st[0:1, :] = cr
    st[1:2, :] = ci
    c8r = jnp.concatenate(ent_r, axis=0)
    c8i = jnp.concatenate(ent_i, axis=0)
    out = []
    for t in range(sub):
        rows = slice(t * SUBLANES, (t + 1) * SUBLANES)
        tr, ti = tab_r[t:t + 1, :], tab_i[t:t + 1, :]
        out.append(jnp.concatenate([loc[rows, :hb] + (tr * c8r - ti * c8i), loc[rows, hb:] + (tr * c8i + ti * c8r)],
                                   axis=1))
    return jnp.concatenate(out, axis=0)


SSM_BLOCKS_PER_STEP = 2


def _scan_scratch(nblk, t_blk, sub, hb):
    return [pltpu.VMEM((nblk, SUBLANES, hb), F32), pltpu.VMEM((nblk, sub, hb), F32), pltpu.VMEM((nblk, sub, hb), F32),
            pltpu.VMEM((nblk, t_blk, 2 * hb), F32)]


def _ssm_fwd(proj, wb, wc, a):
    seq = proj.shape[0]
    nj = wb.shape[0]
    w2 = 2 * STATE_BLOCK
    hb = STATE_BLOCK
    t_blk = min(SCAN_T, seq)
    sub = t_blk // SUBLANES
    pm = _pack_matrix(t_blk, BF16)

    npair = SSM_BLOCKS_PER_STEP

    def body(u_ref, wb_ref, wc_ref, a_ref, pm_ref, pmt_ref, s_ref, y_ref, st, tab_r, tab_i, loc):
        coef = [(a_ref[:, b * w2:b * w2 + hb], a_ref[:, b * w2 + hb:(b + 1) * w2]) for b in range(npair)]

        @pl.when(pl.program_id(1) == 0)
        def _():
            for b, (ar, ai) in enumerate(coef):
                st[b] = jnp.zeros((SUBLANES, hb), F32)
                _scan_tables(ar, ai, tab_r.at[b], tab_i.at[b], sub, False)

        for b, (ar, ai) in enumerate(coef):
            ub = u_ref[:, b * LANES:(b + 1) * LANES].astype(BF16)
            up = jnp.dot(pm_ref[...], ub, preferred_element_type=F32).astype(BF16)
            bu = jnp.dot(up, wb_ref[b], preferred_element_type=F32)
            s = _scan_block(bu, loc.at[b], ar, ai, st.at[b], tab_r.at[b], tab_i.at[b], sub, False)
            s_ref[:, b * w2:(b + 1) * w2] = s
            yp = jnp.dot(s.astype(BF16), wc_ref[b], preferred_element_type=F32)
            y_ref[:, b * LANES:(b + 1) * LANES] = _permute_rows_f32(pmt_ref[...], yp)

    sd = jax.ShapeDtypeStruct
    return pl.pallas_call(
        body, name="ssm_fwd", grid=(nj // npair, seq // t_blk),
        in_specs=[pl.BlockSpec((t_blk, npair * LANES), lambda j, i: (i, j)),
                  pl.BlockSpec((npair, LANES, w2), lambda j, i: (j, 0, 0)),
                  pl.BlockSpec((npair, w2, LANES), lambda j, i: (j, 0, 0)),
                  pl.BlockSpec((1, npair * w2), lambda j, i: (0, j)),
                  _full_spec((t_blk, t_blk)), _full_spec((t_blk, t_blk))],
        out_specs=[pl.BlockSpec((t_blk, npair * w2), lambda j, i: (i, j)),
                   pl.BlockSpec((t_blk, npair * LANES), lambda j, i: (i, j))],
        out_shape=[sd((seq, nj * w2), F32), sd((seq, nj * LANES), F32)],
        scratch_shapes=_scan_scratch(npair, t_blk, sub, hb), compiler_params=_cparams(),
    )(proj, wb, wc, a, pm, pm.T)


def _ssm_bwd(dy, s, proj, du1, wb, wc, a):
    seq = dy.shape[0]
    nj = wb.shape[0]
    w2 = 2 * STATE_BLOCK
    hb = STATE_BLOCK
    t_blk = min(SCAN_T, seq)
    sub = t_blk // SUBLANES
    nb = seq // t_blk
    pm = _pack_matrix(t_blk, BF16)

    npair = SSM_BLOCKS_PER_STEP

    def body(dy_ref, s_ref, sprev_ref, u_ref, du1_ref, wb_ref, wc_ref, a_ref, pm_ref, pmt_ref,
             du_ref, dwb_ref, dwc_ref, da_ref, st, tab_r, tab_i, loc):
        ib = pl.program_id(1)
        pmv = pm_ref[...]
        coef = [(a_ref[:, b * w2:b * w2 + hb], -a_ref[:, b * w2 + hb:(b + 1) * w2]) for b in range(npair)]

        @pl.when(ib == 0)
        def _():
            for b, (ar, ai) in enumerate(coef):
                st[b] = jnp.zeros((SUBLANES, hb), F32)
                _scan_tables(ar, ai, tab_r.at[b], tab_i.at[b], sub, True)

        sums = []
        for b, (ar, ai) in enumerate(coef):
            cols, wide = slice(b * LANES, (b + 1) * LANES), slice(b * w2, (b + 1) * w2)
            dyp = jnp.dot(pmv, dy_ref[:, cols], preferred_element_type=F32).astype(BF16)
            up = jnp.dot(pmv, u_ref[:, cols].astype(BF16), preferred_element_type=F32).astype(BF16)
            ds = lax.dot_general(dyp, wc_ref[b], (NT, ((), ())), preferred_element_type=F32)
            lam = _scan_block(ds, loc.at[b], ar, ai, st.at[b], tab_r.at[b], tab_i.at[b], sub, True)
            lamb = lam.astype(BF16)
            du = lax.dot_general(lamb, wb_ref[b], (NT, ((), ())), preferred_element_type=F32)
            du_ref[:, cols] = (_permute_rows_f32(pmt_ref[...], du) + du1_ref[:, cols]).astype(du_ref.dtype)
            sv = s_ref[:, wide]
            dwb = lax.dot_general(up, lamb, (TN, ((), ())), preferred_element_type=F32)
            dwc = lax.dot_general(sv.astype(BF16), dyp, (TN, ((), ())), preferred_element_type=F32)

            prev_last = sprev_ref[SUBLANES - 1:SUBLANES, wide]
            prev_last = jnp.where(ib == nb - 1, jnp.zeros_like(prev_last), prev_last)
            tail = sv[t_blk - SUBLANES:, :]
            sl = lax.broadcasted_iota(jnp.int32, tail.shape, 0)
            head = jnp.where(sl >= 1, pltpu.roll(tail, 1, 0), prev_last)
            s_sh = jnp.concatenate([head, sv[:t_blk - SUBLANES, :]], axis=0)
            lam_r, lam_i = lam[:, :hb], lam[:, hb:]
            sr_, si_ = s_sh[:, :hb], s_sh[:, hb:]
            dar = jnp.sum(lam_r * sr_ + lam_i * si_, axis=0, keepdims=True)
            dai = jnp.sum(lam_i * sr_ - lam_r * si_, axis=0, keepdims=True)
            sums.append((wide, jnp.concatenate([dar, dai], axis=1), dwb, dwc))

        @pl.when(ib == 0)
        def _():
            for b, (wide, contrib, dwb, dwc) in enumerate(sums):
                da_ref[:, wide] = contrib
                dwb_ref[b] = dwb
                dwc_ref[b] = dwc

        @pl.when(ib != 0)
        def _():
            for b, (wide, contrib, dwb, dwc) in enumerate(sums):
                da_ref[:, wide] += contrib
                dwb_ref[b] += dwb
                dwc_ref[b] += dwc

    blk = lambda j, i: (nb - 1 - i, j)
    prev_blk = lambda j, i: (jnp.maximum((nb - 1 - i) * sub - 1, 0), j)
    sd = jax.ShapeDtypeStruct
    return pl.pallas_call(
        body, name="ssm_bwd", grid=(nj // npair, nb),
        in_specs=[pl.BlockSpec((t_blk, npair * LANES), blk), pl.BlockSpec((t_blk, npair * w2), blk),
                  pl.BlockSpec((SUBLANES, npair * w2), prev_blk), pl.BlockSpec((t_blk, npair * LANES), blk),
                  pl.BlockSpec((t_blk, npair * LANES), blk),
                  pl.BlockSpec((npair, LANES, w2), lambda j, i: (j, 0, 0)),
                  pl.BlockSpec((npair, w2, LANES), lambda j, i: (j, 0, 0)),
                  pl.BlockSpec((1, npair * w2), lambda j, i: (0, j)),
                  _full_spec((t_blk, t_blk)), _full_spec((t_blk, t_blk))],
        out_specs=[pl.BlockSpec((t_blk, npair * LANES), blk),
                   pl.BlockSpec((npair, LANES, w2), lambda j, i: (j, 0, 0)),
                   pl.BlockSpec((npair, w2, LANES), lambda j, i: (j, 0, 0)),
                   pl.BlockSpec((1, npair * w2), lambda j, i: (0, j))],
        out_shape=[sd((seq, nj * LANES), BF16), sd((nj, LANES, w2), F32), sd((nj, w2, LANES), F32),
                   sd((1, nj * w2), F32)],
        scratch_shapes=_scan_scratch(npair, t_blk, sub, hb), compiler_params=_cparams(),
    )(dy, s, s, proj, du1, wb, wc, a, pm, pm.T)


def _rope128(x, cos, sa, sb):
    return x * cos + pltpu.roll(x, 96, 1) * sa + pltpu.roll(x, 32, 1) * sb


def _rope128_t(dy, cos, sa, sb):
    return dy * cos + pltpu.roll(dy * sa, 32, 1) + pltpu.roll(dy * sb, 96, 1)


ATT_BQ = 512


def _probs(qn, qp, kn, kp, r0, scale):
    s = lax.dot_general(qn, kn, (NT, ((), ())), preferred_element_type=F32)
    s = s + lax.dot_general(qp, kp, (NT, ((), ())), preferred_element_type=F32)
    s = s * scale
    diag = s[:, r0:]
    row = lax.broadcasted_iota(jnp.int32, diag.shape, 0)
    col = lax.broadcasted_iota(jnp.int32, diag.shape, 1)
    diag = jnp.where(col <= row, diag, jnp.finfo(F32).min)
    s = diag if r0 == 0 else jnp.concatenate([s[:, :r0], diag], axis=1)
    m = jnp.max(s, axis=-1, keepdims=True)
    e = jnp.exp(s - m)
    return e / jnp.sum(e, axis=-1, keepdims=True)


def _attn_specs(seq):
    tab = pl.BlockSpec((seq, LANES), lambda h: (0, 0))
    return [pl.BlockSpec((None, seq, 256), lambda h: (h, 0, 0)), pl.BlockSpec((None, seq, 128), lambda h: (h, 0, 0)),
            pl.BlockSpec((None, seq, 128), lambda h: (h, 0, 1)), tab, tab, tab, tab]


def _attn_fwd(q_raw, kv, kpe, cos, sa, sb):
    nh, seq, _ = q_raw.shape
    bq = min(ATT_BQ, seq)
    scale = (QK_NOPE + QK_ROPE) ** -0.5

    def body(q_ref, kn_ref, v_ref, kp_ref, cos_ref, sa_ref, sb_ref, o_ref):
        for r0 in range(0, seq, bq):
            rows, kend = pl.ds(r0, bq), r0 + bq
            qn = q_ref[rows, :QK_NOPE].astype(BF16)
            qp = _rope128(q_ref[rows, QK_NOPE:], cos_ref[rows, :], sa_ref[rows, :], sb_ref[rows, :]).astype(BF16)
            p = _probs(qn, qp, kn_ref[:kend, :], kp_ref[:kend, :], r0, scale)
            o_ref[rows, :] = jnp.dot(p.astype(BF16), v_ref[:kend, :], preferred_element_type=F32)

    return pl.pallas_call(
        body, name="attn_fwd", grid=(nh,), in_specs=_attn_specs(seq),
        out_specs=pl.BlockSpec((seq, V_DIM), lambda h: (0, h)),
        out_shape=jax.ShapeDtypeStruct((seq, nh * V_DIM), F32), compiler_params=_cparams(),
    )(q_raw, kv, kv, kpe, cos, sa, sb)


def _attn_bwd(q_raw, kv, kpe, cos, sa, sb, do):
    nh, seq, _ = q_raw.shape
    bq = min(ATT_BQ, seq)
    scale = (QK_NOPE + QK_ROPE) ** -0.5

    def body(q_ref, kn_ref, v_ref, kp_ref, cos_ref, sa_ref, sb_ref, do_ref, dq_ref, dkv_ref, dkp_ref):
        dkv_ref[...] = jnp.zeros_like(dkv_ref)
        dkp_ref[...] = jnp.zeros_like(dkp_ref)
        for r0 in range(0, seq, bq):
            rows, kend = pl.ds(r0, bq), r0 + bq
            cos_b, sa_b, sb_b = cos_ref[rows, :], sa_ref[rows, :], sb_ref[rows, :]
            qn = q_ref[rows, :QK_NOPE].astype(BF16)
            qp = _rope128(q_ref[rows, QK_NOPE:], cos_b, sa_b, sb_b).astype(BF16)
            kn, v, kp = kn_ref[:kend, :], v_ref[:kend, :], kp_ref[:kend, :]
            p = _probs(qn, qp, kn, kp, r0, scale)
            dob = do_ref[rows, :].astype(BF16)
            dp = lax.dot_general(dob, v, (NT, ((), ())), preferred_element_type=F32)
            ds = p * (dp - jnp.sum(p * dp, axis=-1, keepdims=True)) * scale
            dsb = ds.astype(BF16)
            pb = p.astype(BF16)
            dq_ref[rows, :QK_NOPE] = jnp.dot(dsb, kn, preferred_element_type=F32).astype(dq_ref.dtype)
            dqp = jnp.dot(dsb, kp, preferred_element_type=F32)
            dq_ref[rows, QK_NOPE:] = _rope128_t(dqp, cos_b, sa_b, sb_b).astype(dq_ref.dtype)
            dkv_ref[:kend, :QK_NOPE] += lax.dot_general(dsb, qn, (TN, ((), ())), preferred_element_type=F32)
            dkv_ref[:kend, QK_NOPE:] += lax.dot_general(pb, dob, (TN, ((), ())), preferred_element_type=F32)
            dkp_ref[:kend, :] += lax.dot_general(dsb, qp, (TN, ((), ())), preferred_element_type=F32)

    sd = jax.ShapeDtypeStruct
    return pl.pallas_call(
        body, name="attn_bwd", grid=(nh,),
        in_specs=_attn_specs(seq) + [pl.BlockSpec((seq, V_DIM), lambda h: (0, h))],
        out_specs=[pl.BlockSpec((None, seq, 256), lambda h: (h, 0, 0)),
                   pl.BlockSpec((None, seq, 256), lambda h: (h, 0, 0)),
                   pl.BlockSpec((None, seq, 128), lambda h: (h, 0, 0))],
        out_shape=[sd((nh, seq, 256), BF16), sd((nh, seq, 256), F32), sd((nh, seq, 128), F32)],
        compiler_params=_cparams(),
    )(q_raw, kv, kv, kpe, cos, sa, sb, do)


def _shift_rows(a, k):
    seq = a.shape[0]
    r = pltpu.roll(a, k % seq, 0)
    rows = lax.broadcasted_iota(jnp.int32, (SUBLANES, a.shape[1]), 0)
    if k > 0:
        return jnp.concatenate([jnp.where(rows >= k, r[:SUBLANES], 0.0), r[SUBLANES:]], axis=0)
    return jnp.concatenate([r[:seq - SUBLANES], jnp.where(rows < SUBLANES + k, r[seq - SUBLANES:], 0.0)], axis=0)


def _conv3(a, w, b):
    a1 = _shift_rows(a, 1)
    a2 = _shift_rows(a, 2)
    return w[2:3] * a + w[1:2] * a1 + w[0:1] * a2 + b, a1, a2


def _conv_gate_fwd(a, cw, cb):
    half, _, seq, c = a.shape
    nc = c // LANES

    def fn(pair, wg, wv, bg, bv):
        gc, _, _ = _conv3(pair[0], wg, bg)
        vc, _, _ = _conv3(pair[1], wv, bv)
        return gc * jax.nn.sigmoid(gc) * vc

    def w_spec(off, r):
        return pl.BlockSpec((None, r, LANES), lambda k, j: (k + off, 0, j))

    return _blockwise(
        "conv_gate_fwd", fn, [a, cw, cw, cb, cb],
        [pl.BlockSpec((None, 2, seq, LANES), lambda k, j: (k, 0, 0, j)),
         w_spec(0, 3), w_spec(half, 3), w_spec(0, 1), w_spec(half, 1)],
        [((seq, half * c), BF16)], [pl.BlockSpec((seq, LANES), lambda k, j: (0, k * nc + j))],
        grid=(half, nc))[0]


def _conv_gate_bwd(a, cw, cb, dm):
    half, _, seq, c = a.shape
    nc = c // LANES

    def body(a_ref, wg_ref, wv_ref, bg_ref, bv_ref, dm_ref, da_ref, dw_ref, db_ref):
        dmv = dm_ref[...]
        ga, wg = a_ref[0], wg_ref[...]
        va, wv = a_ref[1], wv_ref[...]
        gc, g1, g2 = _conv3(ga, wg, bg_ref[...])
        vc, v1, v2 = _conv3(va, wv, bv_ref[...])
        sg = jax.nn.sigmoid(gc)
        dms = dmv * sg
        d_val = dms * gc
        d_gate = dms * vc * (1.0 + gc * (1.0 - sg))

        def back(r, dc, own, a1, a2, w):
            up1 = _shift_rows(dc, -1)
            up2 = _shift_rows(dc, -2)
            da_ref[r] = (w[2:3] * dc + w[1:2] * up1 + w[0:1] * up2).astype(da_ref.dtype)
            dw_ref[r, 0:1, :] = jnp.sum(dc * a2, axis=0, keepdims=True)
            dw_ref[r, 1:2, :] = jnp.sum(dc * a1, axis=0, keepdims=True)
            dw_ref[r, 2:3, :] = jnp.sum(dc * own, axis=0, keepdims=True)
            db_ref[r] = jnp.sum(dc, axis=0, keepdims=True)

        back(0, d_gate, ga, g1, g2, wg)
        back(1, d_val, va, v1, v2, wv)

    def w_spec(off, r):
        return pl.BlockSpec((None, r, LANES), lambda k, j: (k + off, 0, j))

    def pair_spec(r):
        return pl.BlockSpec((None, 2, r, LANES), lambda k, j: (k, 0, 0, j))

    sd = jax.ShapeDtypeStruct
    return pl.pallas_call(
        body, name="conv_gate_bwd", grid=(half, nc),
        in_specs=[pair_spec(seq), w_spec(0, 3), w_spec(half, 3), w_spec(0, 1), w_spec(half, 1),
                  pl.BlockSpec((seq, LANES), lambda k, j: (0, k * nc + j))],
        out_specs=[pair_spec(seq), pair_spec(3), pair_spec(1)],
        out_shape=[sd((half, 2, seq, c), BF16), sd((half, 2, 3, c), F32), sd((half, 2, 1, c), F32)],
        compiler_params=_cparams(),
    )(a, cw, cw, cb, cb, dm)


ROW_T = 256


def _local_step(x, positions, target, w, emit=lambda **grads: None):
    seq, d = x.shape
    t_row = min(ROW_T, seq)
    nrow = seq // t_row
    ssm_w = d // 2
    nj = ssm_w // LANES
    n_groups = ssm_w // SSM_GROUP
    nh = w["wuq"].shape[0]
    q_rank = w["wuq"].shape[1]
    kv_rank = w["wukv"].shape[1]
    ns = w["wup"].shape[0]
    c_ff = w["wup"].shape[2]
    in_pad = w["win"].shape[1]
    tm = min(1024, seq)
    nm = seq // tm
    sw = 2 * STATE_BLOCK
    g1 = (nrow,)

    lr3 = w["lam_re"].reshape(n_groups, 1, SSM_STATE)
    li3 = w["lam_im"].reshape(n_groups, 1, SSM_STATE)
    ldt3 = w["log_dt"].reshape(n_groups, 1, 1)
    bt_re = jnp.swapaxes(w["b_re"].reshape(n_groups, SSM_STATE, SSM_GROUP), 1, 2)
    bt_im = jnp.swapaxes(w["b_im"].reshape(n_groups, SSM_STATE, SSM_GROUP), 1, 2)
    abar_re, abar_im, bbt_re, bbt_im = _s5_prep(lr3, li3, ldt3, bt_re, bt_im)
    eye = jnp.eye(GROUPS_PER_BLOCK, dtype=F32)

    def blockdiag_in(bb):
        t = bb.reshape(nj, GROUPS_PER_BLOCK, SSM_GROUP, SSM_STATE)
        return jnp.einsum("jghp,gk->jghkp", t, eye).reshape(nj, LANES, STATE_BLOCK)

    def blockdiag_in_t(dwb):
        t = dwb.reshape(nj, GROUPS_PER_BLOCK, SSM_GROUP, GROUPS_PER_BLOCK, SSM_STATE)
        return jnp.einsum("jghkp,gk->jghp", t, eye).reshape(n_groups, SSM_GROUP, SSM_STATE)

    def blockdiag_out(cc):
        t = cc.reshape(nj, GROUPS_PER_BLOCK, SSM_GROUP, SSM_STATE)
        return jnp.einsum("jghp,gk->jkpgh", t, eye).reshape(nj, STATE_BLOCK, LANES)

    def blockdiag_out_t(dwc):
        t = dwc.reshape(nj, GROUPS_PER_BLOCK, SSM_STATE, GROUPS_PER_BLOCK, SSM_GROUP)
        return jnp.einsum("jkpgh,gk->jghp", t, eye).reshape(n_groups, SSM_GROUP, SSM_STATE)

    c_re = w["c_re"].reshape(n_groups, SSM_GROUP, SSM_STATE)
    c_im = w["c_im"].reshape(n_groups, SSM_GROUP, SSM_STATE)
    wb = jnp.concatenate([blockdiag_in(bbt_re), blockdiag_in(bbt_im)], axis=2).astype(BF16)
    wc = jnp.concatenate([blockdiag_out(c_re), -blockdiag_out(c_im)], axis=1).astype(BF16)
    a_lay = jnp.concatenate([abar_re.reshape(nj, 1, STATE_BLOCK), abar_im.reshape(nj, 1, STATE_BLOCK)],
                            axis=1).reshape(1, nj * sw)

    attn_w = w["attn_norm"]
    t_wide = min(2 * t_row, seq)
    g_wide = (seq // t_wide,)

    def proj_fn(xb, wv, wi):
        hb = _rms(xb, wv).astype(BF16)
        return hb, jnp.dot(hb, wi, preferred_element_type=F32)

    hn, proj = _blockwise(
        "norm1_proj", proj_fn, [x, attn_w, w["win"]],
        [_row_spec(t_wide, d), _full_spec((1, d)), _full_spec((d, in_pad), single=True)],
        [((seq, d), BF16), ((seq, in_pad), F32)], [_row_spec(t_wide, d), _row_spec(t_wide, in_pad)], g_wide)

    s_all, ylin = _ssm_fwd(proj, wb, wc, a_lay)

    def glu_fwd_fn(yl, ub, dsk, wg, bg):
        yp = yl + dsk * ub
        ygv = jax.nn.gelu(yp)
        ygb = ygv.astype(BF16)
        zb = jnp.dot(ygb, wg, preferred_element_type=F32) + bg
        return yp, ygb, zb, ygv * jax.nn.sigmoid(zb)

    wide = pl.BlockSpec((t_wide, ssm_w), lambda i: (i, 0))
    y_pre, yg, z, y_ssm = _blockwise(
        "ssm_glu_fwd", glu_fwd_fn, [ylin, proj, w["ssm_d"], w["wglu"], w["b_glu"]],
        [wide, wide, _full_spec((1, ssm_w)), _full_spec((ssm_w, ssm_w), single=True), _full_spec((1, ssm_w))],
        [((seq, ssm_w), F32), ((seq, ssm_w), BF16), ((seq, ssm_w), F32), ((seq, ssm_w), F32)], [wide] * 4, g_wide)

    cq_off, ckv_off, kpe_off = ssm_w, ssm_w + q_rank, ssm_w + q_rank + kv_rank
    assert cq_off % q_rank == 0 and ckv_off % kv_rank == 0 and kpe_off % LANES == 0
    cq_spec = pl.BlockSpec((t_row, q_rank), lambda i: (i, cq_off // q_rank))
    ckv_spec = pl.BlockSpec((t_row, kv_rank), lambda i: (i, ckv_off // kv_rank))
    kpe_spec = pl.BlockSpec((t_row, LANES), lambda i: (i, kpe_off // LANES))
    pos_b = jnp.broadcast_to(positions.astype(F32)[:, None], (seq, LANES))
    inv_freq = ROPE_THETA ** (-jnp.arange(0, QK_ROPE, 2, dtype=F32) / QK_ROPE)
    inv128 = jnp.tile(inv_freq, 4).reshape(1, LANES)

    def mla_prep_fn(cq, ckv, kp, pb, inv, wq, wkv):
        ang = pb * inv
        lane = lax.broadcasted_iota(jnp.int32, ang.shape, 1)
        cs, sn = jnp.cos(ang), jnp.sin(ang)
        cos = jnp.where(lane < QK_ROPE, cs, 0.0)
        sa = jnp.where(lane < QK_ROPE // 2, -sn, 0.0)
        sb = jnp.where(jnp.logical_and(lane >= QK_ROPE // 2, lane < QK_ROPE), sn, 0.0)
        return _rms(cq, wq), _rms(ckv, wkv), _rope128(kp, cos, sa, sb), cos, sa, sb

    qn, kvn, kpe, cos_t, sa_t, sb_t = _blockwise(
        "mla_prep", mla_prep_fn, [proj, proj, proj, pos_b, inv128, w["q_norm"], w["kv_norm"]],
        [cq_spec, ckv_spec, kpe_spec, _row_spec(t_row, LANES),
         _full_spec((1, LANES)), _full_spec((1, q_rank)), _full_spec((1, kv_rank))],
        [((seq, q_rank), BF16), ((seq, kv_rank), BF16), ((seq, LANES), BF16)] + [((seq, LANES), F32)] * 3,
        [_row_spec(t_row, q_rank), _row_spec(t_row, kv_rank)] + [_row_spec(t_row, LANES)] * 4, g1)

    def head_mm(name, act, wh, out_dtype):
        kdim, ndim = wh.shape[1], wh.shape[2]
        return _mm(name, act, wh, grid=(nh, 1, 1),
                   a_spec=pl.BlockSpec((seq, kdim), lambda h, i, k: (i, 0)),
                   b_spec=pl.BlockSpec((None, kdim, ndim), lambda h, i, k: (h, 0, 0)),
                   o_spec=pl.BlockSpec((None, seq, ndim), lambda h, i, k: (h, i, 0)),
                   out_shape=(nh, seq, ndim), out_dtype=out_dtype)

    q_raw = head_mm("mla_q", qn, w["wuq"], F32)
    kv = head_mm("mla_kv", kvn, w["wukv"], BF16)
    y_mla = _attn_fwd(q_raw, kv, kpe, cos_t, sa_t, sb_t)
    mla_w = nh * V_DIM

    def out_proj_fn(ys, ym, ws, wm, wo, xb):
        yc = jnp.concatenate([_rms(ys, ws), _rms(ym, wm)], axis=1).astype(BF16)
        return yc, xb + jnp.dot(yc, wo, preferred_element_type=F32)

    ycat, h1 = _blockwise(
        "out_norm_proj", out_proj_fn, [y_ssm, y_mla, w["son"], w["mon"], w["wout"], x],
        [wide, _row_spec(t_wide, mla_w), _full_spec((1, ssm_w)), _full_spec((1, mla_w)),
         _full_spec((d, d), single=True), _row_spec(t_wide, d)],
        [((seq, d), BF16), ((seq, d), F32)], [_row_spec(t_wide, d)] * 2, g_wide)

    hn2 = _blockwise("norm2", lambda hb, wv: _rms(hb, wv), [h1, w["ffn_norm"]],
                     [_row_spec(t_row, d), _full_spec((1, d))], [((seq, d), BF16)], [_row_spec(t_row, d)], g1)[0]
    tku = d
    half = ns // 2
    a_ff = _mm("ffn_up", hn2, w["wup"], grid=(ns, nm, d // tku),
               a_spec=pl.BlockSpec((tm, tku), lambda s, i, k: (i, k)),
               b_spec=pl.BlockSpec((None, tku, c_ff), lambda s, i, k: (s, k, 0)),
               o_spec=pl.BlockSpec((None, None, tm, c_ff), lambda s, i, k: (s % half, s // half, i, 0)),
               out_shape=(half, 2, seq, c_ff), out_dtype=F32)
    cb3 = w["conv_b"].reshape(ns, 1, c_ff)
    m_ff = _conv_gate_fwd(a_ff, w["conv_w"], cb3)
    d_ff = half * c_ff
    wdn = w["wdown"]
    tnd = _tile(d, 1024)
    tmx, tnx = min(1024, seq), _tile(d, 1024)
    h2 = _mm2d("ffn_down", m_ff, wdn, NN, F32, tm=512, tn=512, tk=d_ff, res=h1)

    def loss_fn(hb, tb, wv):
        def f(hh, ww):
            err = _rms(hh, ww) - tb
            return 0.5 * jnp.sum(jnp.mean(err * err, axis=-1))

        lossv, (dh, dw) = jax.value_and_grad(f, argnums=(0, 1))(hb, wv)
        return dh, dh, jnp.full((1, LANES), lossv, F32), dw

    fin_w = w["final_norm"].reshape(1, d)
    dh2, dh2b, loss_acc, g_final = _blockwise(
        "loss_head", loss_fn, [h2, target, fin_w], [_row_spec(t_row, d), _row_spec(t_row, d), _full_spec((1, d))],
        [((seq, d), F32), ((seq, d), BF16), ((1, LANES), F32), ((1, d), F32)],
        [_row_spec(t_row, d), _row_spec(t_row, d), _full_spec((1, LANES)), _full_spec((1, d))], g1, n_acc=2)
    loss = loss_acc

    dm = _mm2d("ffn_down_dx", dh2b, wdn, NT, F32, tn=c_ff)
    tks = seq
    g_wdown = _mm2d("ffn_down_dw", m_ff, dh2b, TN, BF16, tm=c_ff)
    emit(wdown=g_wdown)
    da_ff, g_convw2, g_convb2 = _conv_gate_bwd(a_ff, w["conv_w"], cb3, dm)
    g_convw = jnp.swapaxes(g_convw2, 0, 1).reshape(ns, 3, c_ff)
    g_convb = jnp.swapaxes(g_convb2, 0, 1).reshape(ns, 1, c_ff)
    g_wup = _mm("ffn_up_dw", hn2, da_ff, grid=(ns, d // tnd, seq // tks), contract=TN,
                a_spec=pl.BlockSpec((tks, tnd), lambda s, j, k: (k, j)),
                b_spec=pl.BlockSpec((None, None, tks, c_ff), lambda s, j, k: (s % half, s // half, k, 0)),
                o_spec=pl.BlockSpec((None, tnd, c_ff), lambda s, j, k: (s, j, 0)),
                out_shape=(ns, d, c_ff), out_dtype=BF16)
    emit(wup=g_wup)
    dhn2 = _mm("ffn_up_dx", da_ff, w["wup"], grid=(seq // tmx, d // tnx, ns), contract=NT,
               a_spec=pl.BlockSpec((None, None, tmx, c_ff), lambda i, j, s: (s % half, s // half, i, 0)),
               b_spec=pl.BlockSpec((None, tnx, c_ff), lambda i, j, s: (s, j, 0)),
               o_spec=pl.BlockSpec((tmx, tnx), lambda i, j, s: (i, j)),
               out_shape=(seq, d), out_dtype=F32)
    emit(wup_pair_sums_after=dhn2)

    def norm_bwd_fn(hb, dres, dn, wv):
        dx_, dw_ = _rms_bwd(hb, wv, dn)
        dtot = dres + dx_
        return dtot, dtot, dw_

    dh1, dh1b, g_ffn_norm = _blockwise(
        "norm2_bwd", norm_bwd_fn, [h1, dh2, dhn2, w["ffn_norm"]],
        [_row_spec(t_row, d)] * 3 + [_full_spec((1, d))],
        [((seq, d), F32), ((seq, d), BF16), ((1, d), F32)],
        [_row_spec(t_row, d), _row_spec(t_row, d), _full_spec((1, d))], g1, n_acc=1)

    g_wout = _mm2d("out_proj_dw", ycat, dh1b, TN, BF16)

    def outnorm_bwd_fn(dhb, wo, ys, ym, ws, wm):
        dyc = lax.dot_general(dhb, wo, (NT, ((), ())), preferred_element_type=F32)
        dys, dws = _rms_bwd(ys, ws, dyc[:, :ssm_w])
        dym, dwm = _rms_bwd(ym, wm, dyc[:, ssm_w:])
        return dys, dym, dws, dwm

    dy_ssm, dy_mla, g_son, g_mon = _blockwise(
        "out_proj_dx_norm_bwd", outnorm_bwd_fn, [dh1b, w["wout"], y_ssm, y_mla, w["son"], w["mon"]],
        [_row_spec(t_wide, d), _full_spec((d, d), single=True), wide, _row_spec(t_wide, mla_w),
         _full_spec((1, ssm_w)), _full_spec((1, mla_w))],
        [((seq, ssm_w), F32), ((seq, mla_w), F32), ((1, ssm_w), F32), ((1, mla_w), F32)],
        [wide, _row_spec(t_wide, mla_w), _full_spec((1, ssm_w)), _full_spec((1, mla_w))],
        g_wide, n_acc=2)

    def glu_bwd_fn(dy, yp, zb, ub, dsk, wg):
        ygv = jax.nn.gelu(yp)
        sg = jax.nn.sigmoid(zb)
        dz = dy * ygv * sg * (1.0 - sg)
        dzb = dz.astype(BF16)
        dyg = dy * sg + lax.dot_general(dzb, wg, (NT, ((), ())), preferred_element_type=F32)
        _, vjp = jax.vjp(jax.nn.gelu, yp)
        dyp = vjp(dyg)[0]
        return (dzb, dyp, dyp * dsk, jnp.sum(dz, axis=0, keepdims=True), jnp.sum(dyp * ub, axis=0, keepdims=True))

    dz, dy_pre, du1, g_bglu, g_ssmd = _blockwise(
        "ssm_glu_bwd", glu_bwd_fn, [dy_ssm, y_pre, z, proj, w["ssm_d"], w["wglu"]],
        [wide] * 4 + [_full_spec((1, ssm_w)), _full_spec((ssm_w, ssm_w), single=True)],
        [((seq, ssm_w), BF16), ((seq, ssm_w), BF16), ((seq, ssm_w), F32), ((1, ssm_w), F32), ((1, ssm_w), F32)],
        [wide] * 3 + [_full_spec((1, ssm_w))] * 2, g_wide, n_acc=2)
    g_wglu = _mm2d("ssm_glu_dw", yg, dz, TN, BF16)
    dq_raw, dkv, dkp_h = _attn_bwd(q_raw, kv, kpe, cos_t, sa_t, sb_t, dy_mla)

    def head_mm_dx(name, dact, wh):
        kdim, ndim = wh.shape[1], wh.shape[2]
        return _mm(name, dact, wh, grid=(1, 1, nh), contract=NT,
                   a_spec=pl.BlockSpec((None, seq, ndim), lambda i, j, h: (h, i, 0)),
                   b_spec=pl.BlockSpec((None, kdim, ndim), lambda i, j, h: (h, 0, 0)),
                   o_spec=pl.BlockSpec((seq, kdim), lambda i, j, h: (i, 0)),
                   out_shape=(seq, kdim), out_dtype=F32)

    def head_mm_dw(name, act, dact):
        kdim, ndim = act.shape[1], dact.shape[2]
        return _mm(name, act, dact, grid=(nh, 1, seq // tks), contract=TN,
                   a_spec=pl.BlockSpec((tks, kdim), lambda h, j, k: (k, 0)),
                   b_spec=pl.BlockSpec((None, tks, ndim), lambda h, j, k: (h, k, 0)),
                   o_spec=pl.BlockSpec((None, kdim, ndim), lambda h, j, k: (h, 0, 0)),
                   out_shape=(nh, kdim, ndim), out_dtype=BF16)

    g_wuq = head_mm_dw("mla_q_dw", qn, dq_raw)
    g_wukv = head_mm_dw("mla_kv_dw", kvn, dkv)
    dqn = head_mm_dx("mla_q_dx", dq_raw, w["wuq"])
    dkvn = head_mm_dx("mla_kv_dx", dkv, w["wukv"])
    emit(not_before=(dqn, dkvn, dy_pre), wout=g_wout, wuq=g_wuq, wukv=g_wukv, wglu=g_wglu, conv_w=g_convw)

    du, dwb, dwc, da_lay = _ssm_bwd(dy_pre, s_all, proj, du1, wb, wc, a_lay)
    g_c_re = blockdiag_out_t(dwc[:, :STATE_BLOCK, :])
    g_c_im = -blockdiag_out_t(dwc[:, STATE_BLOCK:, :])
    dbbt_re = blockdiag_in_t(dwb[:, :, :STATE_BLOCK])
    dbbt_im = blockdiag_in_t(dwb[:, :, STATE_BLOCK:])
    da3 = da_lay.reshape(nj, 2, STATE_BLOCK)
    dabar_re = da3[:, 0, :].reshape(n_groups, 1, SSM_STATE)
    dabar_im = da3[:, 1, :].reshape(n_groups, 1, SSM_STATE)
    g_lr3, g_li3, g_ldt3, g_bt_re, g_bt_im = _s5_prep_bwd(lr3, li3, ldt3, bt_re, bt_im,
                                                           dabar_re, dabar_im, dbbt_re, dbbt_im)

    def mla_prep_bwd_fn(cq, ckv, dqn_b, dkvn_b, dkp_b, cos, sa, sb, wq, wkv):
        dcq, dwq = _rms_bwd(cq, wq, dqn_b)
        dckv, dwkv = _rms_bwd(ckv, wkv, dkvn_b)
        dkp_sum = dkp_b[0]
        for h in range(1, nh):
            dkp_sum = dkp_sum + dkp_b[h]
        return dcq, dckv, _rope128_t(dkp_sum, cos, sa, sb), dwq, dwkv

    dc_q, dc_kv, dkpe_raw, g_qnorm, g_kvnorm = _blockwise(
        "mla_prep_bwd", mla_prep_bwd_fn, [proj, proj, dqn, dkvn, dkp_h, cos_t, sa_t, sb_t, w["q_norm"], w["kv_norm"]],
        [cq_spec, ckv_spec, _row_spec(t_row, q_rank), _row_spec(t_row, kv_rank),
         pl.BlockSpec((nh, t_row, LANES), lambda i: (0, i, 0))] + [_row_spec(t_row, LANES)] * 3
        + [_full_spec((1, q_rank)), _full_spec((1, kv_rank))],
        [((seq, q_rank), BF16), ((seq, kv_rank), BF16), ((seq, LANES), BF16), ((1, q_rank), F32), ((1, kv_rank), F32)],
        [_row_spec(t_row, q_rank), _row_spec(t_row, kv_rank), _row_spec(t_row, LANES), _full_spec((1, q_rank)),
         _full_spec((1, kv_rank))], g1, n_acc=2)

    dproj = jnp.concatenate([du, dc_q, dc_kv, dkpe_raw], axis=1)
    g_win = _mm2d("proj_dw", hn, dproj, TN, BF16, tn=640)
    emit(win=g_win)
    def norm1_bwd_fn(dpb, wi, xb, dres, wv):
        dn = lax.dot_general(dpb, wi, (NT, ((), ())), preferred_element_type=F32)
        dx_, dw_ = _rms_bwd(xb, wv, dn)
        return dres + dx_, dw_

    grad_x, g_attn_norm = _blockwise(
        "proj_dx_norm1_bwd", norm1_bwd_fn, [dproj, w["win"], x, dh1, attn_w],
        [_row_spec(t_row, in_pad), _full_spec((d, in_pad), single=True), _row_spec(t_row, d), _row_spec(t_row, d), _full_spec((1, d))],
        [((seq, d), F32), ((1, d), F32)], [_row_spec(t_row, d), _full_spec((1, d))], g1, n_acc=1)
    emit(win_pair_sums_after=grad_x)

    grads = dict(
        attn_norm=g_attn_norm, win=g_win, lam_re=g_lr3, lam_im=g_li3, log_dt=g_ldt3,
        bt_re=g_bt_re, bt_im=g_bt_im, c_re=g_c_re, c_im=g_c_im,
        ssm_d=g_ssmd, wglu=g_wglu, b_glu=g_bglu, q_norm=g_qnorm, wuq=g_wuq, kv_norm=g_kvnorm, wukv=g_wukv,
        son=g_son, mon=g_mon, wout=g_wout, ffn_norm=g_ffn_norm, wup=g_wup, conv_w=g_convw, conv_b=g_convb,
        wdown=g_wdown, final_norm=g_final)
    return loss, grad_x, grads


def _mesh_pos():
    return lax.axis_index("x"), lax.axis_index("y"), lax.axis_index("c")


def _handshake_all():
    x, y, c = _mesh_pos()
    barrier = pltpu.get_barrier_semaphore()
    for k in range(1, N_DEV):
        peer = (1 - x if k & 4 else x, 1 - y if k & 2 else y, 1 - c if k & 1 else c)
        pl.semaphore_signal(barrier, inc=1, device_id=peer, device_id_type=MESH)
    pl.semaphore_wait(barrier, N_DEV - 1)


def _handshake(peers):
    barrier = pltpu.get_barrier_semaphore()
    for peer in peers:
        pl.semaphore_signal(barrier, inc=1, device_id=peer, device_id_type=MESH)
    pl.semaphore_wait(barrier, len(peers))


def _comm_call(name, body, n, out_shape, ins, collective_id, after=None, copies=7, n_remote=None, n_local=None):
    n_remote = copies * n if n_remote is None else n_remote
    sems = [pltpu.SemaphoreType.DMA((n_remote,)), pltpu.SemaphoreType.DMA((n_remote,)),
            pltpu.SemaphoreType.DMA((n if n_local is None else n_local,))]
    if collective_id is None:
        any_spec = pl.BlockSpec(memory_space=pl.ANY)
        return pl.pallas_call(body, name=name, out_shape=out_shape, in_specs=[any_spec] * n,
                              out_specs=[any_spec] * n, scratch_shapes=sems)(*ins)
    seq_body = body
    if after:
        n_after = len(after)
        ins = list(ins) + list(after)

        def seq_body(*refs):
            body(*refs[:n], *refs[n + n_after:])

    return pl.kernel(seq_body, name=name, out_type=out_shape,
                     mesh=plsc.ScalarSubcoreMesh(axis_name="seq", num_cores=1), scratch_types=sems,
                     compiler_params=pltpu.CompilerParams(collective_id=collective_id))(*ins)


def _all_gather(name, xs, collective_id=None, after=None, pair_sums=()):
    n = len(xs)
    nh = len(pair_sums)
    m = n + nh

    def body(*refs):
        x_refs, h_refs, o_refs, e_refs = refs[:n], refs[n:m], refs[m:m + n], refs[m + n:2 * m]
        send_sems, recv_sems, local_sems = refs[2 * m:]
        if collective_id is not None:
            _handshake_all()
        finish_pairs = _chip_copies(h_refs, e_refs, send_sems, recv_sems, local_sems, 7 * n, n) if nh else None
        x, y, c = _mesh_pos()
        me, sibling = (x, y, c), (x, y, 1 - c)
        chips = [(1 - x, y), (x, 1 - y), (1 - x, 1 - y)]

        def slot(o_ref, px, py, pc):
            return o_ref.at[4 * px + 2 * py + pc]

        def copy(t, k, block, to, src=None):
            dst = slot(o_refs[t], *block)
            return pltpu.make_async_remote_copy(
                src_ref=dst if src is None else src, dst_ref=dst,
                send_sem=send_sems.at[7 * t + k], recv_sem=recv_sems.at[7 * t + k],
                device_id=to, device_id_type=MESH)

        started = []
        for t in range(n):
            mine = pltpu.make_async_copy(x_refs[t], slot(o_refs[t], *me), local_sems.at[t])
            mine.start()
            started.append(mine)
        first = []
        for t in range(n):
            first.append(copy(t, 0, me, sibling, src=x_refs[t]))
            first += [copy(t, 1 + j, me, (*chip, c), src=x_refs[t]) for j, chip in enumerate(chips)]
        for cp in first:
            cp.start()
        passed = []
        for j, chip in enumerate(chips):
            for t in range(n):
                copy(t, 1 + j, (*chip, c), me).wait_recv()
                fwd = copy(t, 4 + j, (*chip, c), sibling)
                fwd.start()
                passed.append(fwd)
        for t in range(n):
            copy(t, 0, sibling, me).wait_recv()
            for j, chip in enumerate(chips):
                copy(t, 4 + j, (*chip, 1 - c), me).wait_recv()
        for cp in first + passed:
            cp.wait_send()
        for mine in started:
            mine.wait()
        if nh:
            finish_pairs()

    out_shape = ([jax.ShapeDtypeStruct((N_DEV,) + v.shape, v.dtype) for v in xs]
                 + [jax.ShapeDtypeStruct(v.shape, v.dtype) for v in pair_sums])
    return _comm_call(name, body, m, out_shape, list(xs) + list(pair_sums), collective_id, after,
                      n_remote=7 * n + (N_CHIP - 1) * nh, n_local=m)


def _exchange_partials(name, gs, collective_id=None, after=None):
    n = len(gs)

    def body(*refs):
        g_refs, o_refs = refs[:n], refs[n:2 * n]
        send_sems, recv_sems, local_sems = refs[2 * n:]
        if collective_id is not None:
            _handshake_all()
        x, y, c = _mesh_pos()
        me_idx = 4 * x + 2 * y + c
        copies = []
        for t in range(n):
            mine = pltpu.make_async_copy(g_refs[t].at[me_idx], o_refs[t].at[me_idx], local_sems.at[t])
            mine.start()
            copies.append(mine)
        remote = []
        for k in range(1, N_DEV):
            px = 1 - x if k & 4 else x
            py = 1 - y if k & 2 else y
            pc = 1 - c if k & 1 else c
            p_idx = 4 * px + 2 * py + pc
            for t in range(n):
                cp = pltpu.make_async_remote_copy(
                    src_ref=g_refs[t].at[p_idx], dst_ref=o_refs[t].at[me_idx],
                    send_sem=send_sems.at[7 * t + k - 1], recv_sem=recv_sems.at[7 * t + k - 1],
                    device_id=(px, py, pc), device_id_type=MESH)
                cp.start()
                landing = pltpu.make_async_remote_copy(
                    src_ref=g_refs[t].at[p_idx], dst_ref=o_refs[t].at[p_idx],
                    send_sem=send_sems.at[7 * t + k - 1], recv_sem=recv_sems.at[7 * t + k - 1],
                    device_id=(px, py, pc), device_id_type=MESH)
                remote.append((cp, landing))
        for cp, landing in remote:
            landing.wait_recv()
        for cp, landing in remote:
            cp.wait_send()
        for mine in copies:
            mine.wait()

    out_shape = [jax.ShapeDtypeStruct(v.shape, v.dtype) for v in gs]
    return _comm_call(name, body, n, out_shape, gs, collective_id, after)


N_CHIP = N_DEV // 2
PAIR_ADD_BLOCK_ELEMS = 1024 * 1024


def _pair_swap(name, gs, collective_id, after=None):
    n = len(gs)

    def body(*refs):
        g_refs, o_refs = refs[:n], refs[n:2 * n]
        send_sems, recv_sems, _ = refs[2 * n:]
        x, y, c = _mesh_pos()
        sibling = (x, y, 1 - c)
        _handshake([sibling])
        copies = []
        for t in range(n):
            for k in range(N_CHIP):
                copies.append(pltpu.make_async_remote_copy(
                    src_ref=g_refs[t].at[2 * k + 1 - c], dst_ref=o_refs[t].at[k],
                    send_sem=send_sems.at[N_CHIP * t + k], recv_sem=recv_sems.at[N_CHIP * t + k],
                    device_id=sibling, device_id_type=MESH))
        for cp in copies:
            cp.start()
        for cp in copies:
            cp.wait_recv()
        for cp in copies:
            cp.wait_send()

    out_shape = [jax.ShapeDtypeStruct((N_CHIP,) + v.shape[1:], v.dtype) for v in gs]
    return _comm_call(name, body, n, out_shape, gs, collective_id, after, copies=N_CHIP)


def _pair_add(name, g, got):
    _, r, c = g.shape
    tr = r
    if r * c > PAIR_ADD_BLOCK_ELEMS and r % SUBLANES == 0:
        tr = SUBLANES
        while r % (tr * 2) == 0 and tr * 2 * c <= PAIR_ADD_BLOCK_ELEMS:
            tr *= 2

    def body(core_ref, g_ref, got_ref, o_ref):
        o_ref[...] = (g_ref[...].astype(F32) + got_ref[...].astype(F32)).astype(o_ref.dtype)

    grid_spec = pltpu.PrefetchScalarGridSpec(
        num_scalar_prefetch=1, grid=(N_CHIP, r // tr),
        in_specs=[pl.BlockSpec((None, None, tr, c), lambda k, i, core: (k, core[0], i, 0)),
                  pl.BlockSpec((None, tr, c), lambda k, i, core: (k, i, 0))],
        out_specs=pl.BlockSpec((None, tr, c), lambda k, i, core: (k, i, 0)))
    core = lax.axis_index("c").astype(jnp.int32).reshape(1)
    return pl.pallas_call(body, name=name, grid_spec=grid_spec, out_shape=jax.ShapeDtypeStruct((N_CHIP, r, c), g.dtype),
                          compiler_params=_cparams())(core, g.reshape(N_CHIP, 2, r, c), got)


def _chip_copies(h_refs, o_refs, send_sems, recv_sems, local_sems, sem0, local0):
    n = len(h_refs)
    per = N_CHIP - 1
    x, y, c = _mesh_pos()
    others = [(1 - x if k & 2 else x, 1 - y if k & 1 else y) for k in range(1, N_CHIP)]
    my_chip = 2 * x + y
    local = []
    for t in range(n):
        mine = pltpu.make_async_copy(h_refs[t].at[my_chip], o_refs[t].at[my_chip], local_sems.at[local0 + t])
        mine.start()
        local.append(mine)
    remote = []
    for j, (px, py) in enumerate(others):
        chip = 2 * px + py
        for t in range(n):
            sems = dict(send_sem=send_sems.at[sem0 + per * t + j], recv_sem=recv_sems.at[sem0 + per * t + j],
                        device_id=(px, py, c), device_id_type=MESH)
            cp = pltpu.make_async_remote_copy(src_ref=h_refs[t].at[chip], dst_ref=o_refs[t].at[my_chip], **sems)
            cp.start()
            landing = pltpu.make_async_remote_copy(src_ref=h_refs[t].at[chip], dst_ref=o_refs[t].at[chip], **sems)
            remote.append((cp, landing))

    def finish():
        for cp, landing in remote:
            landing.wait_recv()
        for cp, landing in remote:
            cp.wait_send()
        for mine in local:
            mine.wait()

    return finish


def _chip_exchange(name, hs, collective_id, after=None):
    n = len(hs)
    per = N_CHIP - 1

    def body(*refs):
        h_refs, o_refs = refs[:n], refs[n:2 * n]
        send_sems, recv_sems, local_sems = refs[2 * n:]
        x, y, c = _mesh_pos()
        _handshake([(1 - x if k & 2 else x, 1 - y if k & 1 else y, c) for k in range(1, N_CHIP)])
        _chip_copies(h_refs, o_refs, send_sems, recv_sems, local_sems, 0, 0)()

    out_shape = [jax.ShapeDtypeStruct(v.shape, v.dtype) for v in hs]
    return _comm_call(name, body, n, out_shape, hs, collective_id, after, copies=per)


ADAM_BLOCK_ELEMS = 256 * 1024


def _sum_parts(pb):
    g = pb[0].astype(F32)
    for j in range(1, pb.shape[0]):
        g = g + pb[j].astype(F32)
    return g


def _adam_math(g, wb_, mb, vb):
    m_new = ADAM_B1 * mb + (1.0 - ADAM_B1) * g
    v_new = ADAM_B2 * vb + (1.0 - ADAM_B2) * (g * g)
    m_hat = m_new / (1.0 - ADAM_B1 ** ADAM_STEP)
    v_hat = v_new / (1.0 - ADAM_B2 ** ADAM_STEP)
    delta = -ADAM_LR * (m_hat / (jnp.sqrt(v_hat) + ADAM_EPS) + ADAM_WD * wb_)
    return g, delta, m_new, v_new


def _adamw_multi(name, items, nblk=1, packed=None):
    n = len(items)

    def spec(shape, lead):
        blk = list(shape)
        blk[lead + 1] = shape[lead + 1] // nblk
        if nblk == 1:
            return pl.BlockSpec(tuple(blk), lambda i, nd=len(shape): (0,) * nd)
        return pl.BlockSpec(tuple(blk), lambda i, nd=len(shape), ax=lead + 1: (0,) * ax + (i,) + (0,) * (nd - ax - 1))

    ins, in_specs, out_specs, out_shape, where = [], [], [], [], []
    if packed is not None:
        ins.append(packed)
        in_specs.append(spec(packed.shape, 1))
    for parts, wv, mv, vv in items:
        if isinstance(parts, int):
            where.append((0, parts, len(ins)))
        else:
            assert parts.shape[1:] == wv.shape, (name, parts.shape, wv.shape)
            where.append((len(ins), None, len(ins) + 1))
            ins.append(parts)
            in_specs.append(spec(parts.shape, 1))
        ins += [wv, mv, vv]
        in_specs += [spec(wv.shape, 0)] * 3
        out_specs += [spec(wv.shape, 0)] * 4
        out_shape += [jax.ShapeDtypeStruct(wv.shape, F32)] * 4
    n_in = len(ins)

    def body(*refs):
        for t, (ip, off, iw) in enumerate(where):
            wr, mr, vr = refs[iw:iw + 3]
            parts = refs[ip][...] if off is None else refs[ip][:, :, off:off + wr.shape[-1]]
            res = _adam_math(_sum_parts(parts), wr[...], mr[...], vr[...])
            for o, val in zip(refs[n_in + 4 * t:n_in + 4 * t + 4], res):
                o[...] = val

    res = pl.pallas_call(body, name=name, grid=(nblk,), in_specs=in_specs, out_specs=out_specs, out_shape=out_shape,
                         compiler_params=_cparams())(*ins)
    return [tuple(res[4 * t:4 * t + 4]) for t in range(n)]


def _sum_multi(name, parts_list):
    def body(*refs):
        for pr, o in zip(refs[:len(parts_list)], refs[len(parts_list):]):
            o[...] = _sum_parts(pr[...])

    return pl.pallas_call(body, name=name, out_shape=[jax.ShapeDtypeStruct(p.shape[1:], F32) for p in parts_list],
                          compiler_params=_cparams())(*parts_list)


def _adamw_sum(name, parts, wv, mv, vv):
    npart, r, c = parts.shape
    tr = r
    if r * c > ADAM_BLOCK_ELEMS and r % SUBLANES == 0:
        tr = SUBLANES
        while r % (tr * 2) == 0 and tr * 2 * c <= ADAM_BLOCK_ELEMS:
            tr *= 2

    def fn(pb, wb_, mb, vb):
        return _adam_math(_sum_parts(pb), wb_, mb, vb)

    row = pl.BlockSpec((tr, c), lambda i: (i, 0))
    return _blockwise(name, fn, [parts, wv, mv, vv],
                      [pl.BlockSpec((npart, tr, c), lambda i: (0, i, 0)), row, row, row],
                      [((r, c), F32)] * 4, [row] * 4, (r // tr,))


_VECTORS = ["attn_norm", "lam_re", "lam_im", "log_dt", "ssm_d", "b_glu", "q_norm", "kv_norm", "son", "mon",
            "ffn_norm", "conv_b", "final_norm"]
_GHP = ["c_re", "c_im", "bt_re", "bt_im"]
_PACKED = ["attn_norm", "ssm_d", "b_glu", "q_norm", "kv_norm", "son", "mon", "ffn_norm", "conv_b", "final_norm"]
_BIG = ["win", "wglu", "wuq", "wukv", "wout", "wup", "wdown", "conv_w"]
_ROWS_IN_LANES = ("win", "wuq")
_TWO_LEVEL = ("wup", "win")
_AFTER = "_pair_sums_after"
_ORDER = ["attn_norm", "win", "lam_re", "lam_im", "log_dt", "b_re", "b_im", "c_re", "c_im", "ssm_d", "wglu",
          "b_glu", "q_norm", "wuq", "kv_norm", "wukv", "son", "mon", "wout", "ffn_norm", "wup", "conv_w",
          "conv_b", "wdown", "final_norm"]


def kernel(x, positions, attn_norm_w, w_in, ssm_lambda_re, ssm_lambda_im, ssm_log_dt, ssm_b_re, ssm_b_im, ssm_c_re, ssm_c_im, ssm_d, ssm_w_glu, ssm_b_glu, mla_q_norm_w, mla_w_uq, mla_kv_norm_w, mla_w_ukv, ssm_out_norm_w, mla_out_norm_w, w_out, ffn_norm_w, ffn_w_up, ffn_conv_w, ffn_conv_b, ffn_w_down, final_norm_w, loss_target, m_attn_norm_w, m_w_in, m_ssm_lambda_re, m_ssm_lambda_im, m_ssm_log_dt, m_ssm_b_re, m_ssm_b_im, m_ssm_c_re, m_ssm_c_im, m_ssm_d, m_ssm_w_glu, m_ssm_b_glu, m_mla_q_norm_w, m_mla_w_uq, m_mla_kv_norm_w, m_mla_w_ukv, m_ssm_out_norm_w, m_mla_out_norm_w, m_w_out, m_ffn_norm_w, m_ffn_w_up, m_ffn_conv_w, m_ffn_conv_b, m_ffn_w_down, m_final_norm_w, v_attn_norm_w, v_w_in, v_ssm_lambda_re, v_ssm_lambda_im, v_ssm_log_dt, v_ssm_b_re, v_ssm_b_im, v_ssm_c_re, v_ssm_c_im, v_ssm_d, v_ssm_w_glu, v_ssm_b_glu, v_mla_q_norm_w, v_mla_w_uq, v_mla_kv_norm_w, v_mla_w_ukv, v_ssm_out_norm_w, v_mla_out_norm_w, v_w_out, v_ffn_norm_w, v_ffn_w_up, v_ffn_conv_w, v_ffn_conv_b, v_ffn_w_down, v_final_norm_w):
    wts = dict(attn_norm=attn_norm_w, win=w_in, lam_re=ssm_lambda_re, lam_im=ssm_lambda_im, log_dt=ssm_log_dt,
               b_re=ssm_b_re, b_im=ssm_b_im, c_re=ssm_c_re, c_im=ssm_c_im, ssm_d=ssm_d, wglu=ssm_w_glu,
               b_glu=ssm_b_glu, q_norm=mla_q_norm_w, wuq=mla_w_uq, kv_norm=mla_kv_norm_w, wukv=mla_w_ukv,
               son=ssm_out_norm_w, mon=mla_out_norm_w, wout=w_out, ffn_norm=ffn_norm_w, wup=ffn_w_up,
               conv_w=ffn_conv_w, conv_b=ffn_conv_b, wdown=ffn_w_down, final_norm=final_norm_w)
    moms = dict(zip(_ORDER, [m_attn_norm_w, m_w_in, m_ssm_lambda_re, m_ssm_lambda_im, m_ssm_log_dt, m_ssm_b_re,
                             m_ssm_b_im, m_ssm_c_re, m_ssm_c_im, m_ssm_d, m_ssm_w_glu, m_ssm_b_glu, m_mla_q_norm_w,
                             m_mla_w_uq, m_mla_kv_norm_w, m_mla_w_ukv, m_ssm_out_norm_w, m_mla_out_norm_w, m_w_out,
                             m_ffn_norm_w, m_ffn_w_up, m_ffn_conv_w, m_ffn_conv_b, m_ffn_w_down, m_final_norm_w]))
    vels = dict(zip(_ORDER, [v_attn_norm_w, v_w_in, v_ssm_lambda_re, v_ssm_lambda_im, v_ssm_log_dt, v_ssm_b_re,
                             v_ssm_b_im, v_ssm_c_re, v_ssm_c_im, v_ssm_d, v_ssm_w_glu, v_ssm_b_glu, v_mla_q_norm_w,
                             v_mla_w_uq, v_mla_kv_norm_w, v_mla_w_ukv, v_ssm_out_norm_w, v_mla_out_norm_w, v_w_out,
                             v_ffn_norm_w, v_ffn_w_up, v_ffn_conv_w, v_ffn_conv_b, v_ffn_w_down, v_final_norm_w]))
    seq, d = x.shape[1], x.shape[2]
    in_width = w_in.shape[2]
    in_pad = -(-in_width // LANES) * LANES
    q_cols = mla_w_uq.shape[2]
    q_pad = 2 * LANES

    (win_g,) = _all_gather("gather_w_in", [jnp.pad(w_in[0], ((0, 0), (0, in_pad - in_width))).astype(BF16)])
    wglu_g, wuq_g, wukv_g, wout_g, convw_g = _all_gather(
        "gather_mix", [ssm_w_glu[0].astype(BF16), jnp.pad(mla_w_uq[0], ((0, 0), (0, q_pad - q_cols))).astype(BF16),
                       mla_w_ukv[0].astype(BF16), w_out[0].astype(BF16), ffn_conv_w[0]], collective_id=0)
    (wup_g,) = _all_gather("gather_ffn_up", [ffn_w_up[0].astype(BF16)], collective_id=1)
    (wdown_g,) = _all_gather("gather_ffn_down", [ffn_w_down[0].astype(BF16)], collective_id=2)
    ns = N_DEV
    c_ff = wup_g.shape[2]
    w = dict(
        attn_norm=attn_norm_w, win=win_g.reshape(d, in_pad), lam_re=ssm_lambda_re, lam_im=ssm_lambda_im,
        log_dt=ssm_log_dt, b_re=ssm_b_re, b_im=ssm_b_im, c_re=ssm_c_re, c_im=ssm_c_im, ssm_d=ssm_d,
        wglu=wglu_g.reshape(d // 2, d // 2), b_glu=ssm_b_glu, q_norm=mla_q_norm_w, wuq=wuq_g,
        kv_norm=mla_kv_norm_w, wukv=wukv_g, son=ssm_out_norm_w, mon=mla_out_norm_w, wout=wout_g.reshape(d, d),
        ffn_norm=ffn_norm_w, wup=wup_g, conv_w=convw_g, conv_b=ffn_conv_b,
        wdown=wdown_g.reshape(ns // 2 * c_ff, d), final_norm=final_norm_w)

    shard_layout = dict(
        win=lambda a: a[:, :in_width].reshape(N_DEV, d // N_DEV, in_width),
        wglu=lambda a: a.reshape(N_DEV, d // 2 // N_DEV, d // 2),
        wuq=lambda a: a[:, :, :q_cols], wukv=lambda a: a, wout=lambda a: a.reshape(N_DEV, d // N_DEV, d),
        wup=lambda a: a, wdown=lambda a: a.reshape(N_DEV, c_ff // 2, d), conv_w=lambda a: a)
    recv = {}
    next_id = [3]

    last = [None]

    out = {}

    def update(k):
        shp = wts[k].shape
        r, c = shp[-2], shp[-1]
        if k in _ROWS_IN_LANES:
            t = lambda a: jnp.swapaxes(a.reshape(-1, r, c), 1, 2)
            res = _adamw_sum("adamw_" + k, t(recv[k]), t(wts[k])[0], t(moms[k])[0], t(vels[k])[0])
            out[k] = [jnp.swapaxes(a, 0, 1).reshape(shp) for a in res]
            return res[0]
        res = _adamw_sum("adamw_" + k, recv[k].reshape(-1, r, c), wts[k].reshape(r, c),
                         moms[k].reshape(r, c), vels[k].reshape(r, c))
        out[k] = [a.reshape(shp) for a in res]
        return res[0]

    pending = {}

    def exchange(not_before=(), **grads):
        names = list(grads)
        if len(names) == 1 and names[0] in _TWO_LEVEL:
            k = names[0]
            parts = shard_layout[k](grads[k])
            got = _pair_swap("swap_" + k, [parts], collective_id=next_id[0], after=[last[0]])[0]
            next_id[0] += 1
            pending[k] = (parts, got)
            last[0] = got
            return
        if len(names) == 1 and names[0].endswith(_AFTER):
            k = names[0][:-len(_AFTER)]
            sums = _pair_add("pair_add_" + k, *pending[k])
            if k == "win":
                pending["tail"] = sums
                return
            recv[k] = _chip_exchange("exchange_" + k, [sums], collective_id=next_id[0],
                                     after=[last[0], grads[names[0]]])[0]
            next_id[0] += 1
            last[0] = recv[k]
            return
        got = _exchange_partials("exchange_" + "_".join(names), [shard_layout[k](grads[k]) for k in names],
                                 collective_id=next_id[0], after=[a for a in (last[0], *not_before) if a is not None])
        next_id[0] += 1
        last[0] = got[-1]
        recv.update(zip(names, got))

    loss_part, grad_x, g = _local_step(x[0], positions[0], loss_target[0], w, emit=exchange)
    n_groups = ssm_lambda_re.shape[1]
    two_d = {"lam_re": (n_groups, -1), "lam_im": (n_groups, -1)}
    dense = {k: g[k].reshape(two_d.get(k, (1, -1))) for k in _VECTORS}
    offsets, width = {}, 0
    for k in _PACKED:
        offsets[k] = width
        width += dense[k].shape[1]
    sent = dict(packed=jnp.concatenate([dense[k] for k in _PACKED], axis=1),
                **{k: dense[k] for k in _VECTORS if k not in _PACKED},
                **{k: g[k].reshape(n_groups, -1).astype(BF16) for k in _GHP},
                loss=loss_part)
    names = list(sent)
    got = _all_gather("gather_small_grads", [sent[k] for k in names], collective_id=next_id[0], after=[last[0]],
                      pair_sums=[pending["tail"]])
    gathered = dict(zip(names, got))
    recv["win"] = got[len(names)]
    for k in _BIG:
        if k not in out and k != "win":
            update(k)
    update("win")

    def finish(keys, results):
        for k, res in zip(keys, results):
            out[k] = [a.reshape(wts[k].shape) for a in res]

    view = lambda k, a: a.reshape(dense[k].shape)
    finish(_VECTORS, _adamw_multi("adamw_vectors", [(offsets.get(k, gathered.get(k)), view(k, wts[k]), view(k, moms[k]),
                                                     view(k, vels[k])) for k in _VECTORS], packed=gathered["packed"]))
    summed = _GHP + ["loss"]
    sums = dict(zip(summed, _sum_multi("sum_ssm_bc_loss", [gathered[k] for k in summed])))
    loss = sums["loss"][0, 0]
    ghp = lambda k: sums[k].reshape(g[k].shape)
    t_hp = lambda a: jnp.swapaxes(a, 2, 3)
    bc_keys = ["c_re", "c_im", "b_re", "b_im"]
    items = [(ghp(k)[None, None], wts[k], moms[k], vels[k]) for k in bc_keys[:2]]
    items += [(ghp(t)[None, None], t_hp(wts[k]), t_hp(moms[k]), t_hp(vels[k]))
              for k, t in zip(bc_keys[2:], ("bt_re", "bt_im"))]
    res = _adamw_multi("adamw_ssm_bc", items)
    finish(bc_keys, res[:2] + [tuple(t_hp(a) for a in r) for r in res[2:]])

    grad_x = grad_x.reshape(x.shape)
    return (loss, grad_x, *[out[k][0] for k in _ORDER], *[out[k][1] for k in _ORDER],
            *[out[k][2] for k in _ORDER], *[out[k][3] for k in _ORDER])
```

```python
import functools

import jax
import jax.numpy as jnp
from jax import lax
from jax.experimental import pallas as pl
from jax.experimental.pallas import tpu as pltpu
from jax.experimental.pallas import tpu_sc as plsc

F32 = jnp.float32
BF16 = jnp.bfloat16
MESH = pl.DeviceIdType.MESH

N_DEV = 8
LANES = 128
SUBLANES = 8
VMEM_LIMIT = 48 * 1024 * 1024

SSM_GROUP = 16
SSM_STATE = 64
GROUPS_PER_BLOCK = LANES // SSM_GROUP
STATE_BLOCK = GROUPS_PER_BLOCK * SSM_STATE
QK_NOPE = 128
QK_ROPE = 64
V_DIM = 128
ROPE_THETA = 10000.0
RMS_EPS = 1e-6

ADAM_LR = 0.001
ADAM_B1 = 0.9
ADAM_B2 = 0.999
ADAM_EPS = 1e-08
ADAM_WD = 0.01
ADAM_STEP = 10

NN = ((1,), (0,))
NT = ((1,), (1,))
TN = ((0,), (0,))


def _cparams():
    return pltpu.CompilerParams(vmem_limit_bytes=VMEM_LIMIT)


def _tile(n, want):
    if n <= want:
        return n
    t = (want // LANES) * LANES
    while t >= LANES:
        if n % t == 0:
            return t
        t -= LANES
    return n


def _mm(name, a, b, *, grid, a_spec, b_spec, o_spec, out_shape, out_dtype, contract=NN,
        res=None, res_spec=None):
    nk = grid[-1]
    kaxis = len(grid) - 1
    acc_shape = tuple(d for d in o_spec.block_shape if d is not None)

    def body(*refs):
        a_ref, b_ref = refs[:2]
        r_ref = None if res is None else refs[2]
        o_ref = refs[2 if res is None else 3]
        part = lax.dot_general(a_ref[...].astype(BF16), b_ref[...].astype(BF16),
                               (contract, ((), ())), preferred_element_type=F32)
        if nk == 1:
            if r_ref is not None:
                part = part + r_ref[...].astype(F32)
            o_ref[...] = part.astype(o_ref.dtype)
            return
        acc = refs[-1]
        k = pl.program_id(kaxis)

        @pl.when(k == 0)
        def _():
            acc[...] = part

        @pl.when(k != 0)
        def _():
            acc[...] += part

        @pl.when(k == nk - 1)
        def _():
            r = acc[...]
            if r_ref is not None:
                r = r + r_ref[...].astype(F32)
            o_ref[...] = r.astype(o_ref.dtype)

    ins = [a, b] + ([] if res is None else [res])
    in_specs = [a_spec, b_spec] + ([] if res is None else [res_spec])
    return pl.pallas_call(
        body, name=name, grid=grid, in_specs=in_specs, out_specs=o_spec,
        out_shape=jax.ShapeDtypeStruct(out_shape, out_dtype),
        scratch_shapes=[pltpu.VMEM(acc_shape, F32)] if nk > 1 else [], compiler_params=_cparams(),
    )(*ins)


def _mm2d(name, a, b, contract, out_dtype, tm=1024, tn=1024, tk=2048, res=None):
    if contract == NN:
        (m, kk), n = a.shape, b.shape[1]
    elif contract == NT:
        (m, kk), n = a.shape, b.shape[0]
    else:
        (kk, m), n = a.shape, b.shape[1]
    tm, tn, tk = _tile(m, tm), _tile(n, tn), _tile(kk, tk)
    grid = (m // tm, n // tn, kk // tk)
    if contract == TN:
        a_spec = pl.BlockSpec((tk, tm), lambda i, j, k: (k, i))
    else:
        a_spec = pl.BlockSpec((tm, tk), lambda i, j, k: (i, k))
    if contract == NT:
        b_spec = pl.BlockSpec((tn, tk), lambda i, j, k: (j, k))
    else:
        b_spec = pl.BlockSpec((tk, tn), lambda i, j, k: (k, j))
    o_spec = pl.BlockSpec((tm, tn), lambda i, j, k: (i, j))
    res_spec = None
    if res is not None:
        if res.shape[0] == 1:
            res_spec = pl.BlockSpec((1, tn), lambda i, j, k: (0, j))
        else:
            res_spec = pl.BlockSpec((tm, tn), lambda i, j, k: (i, j))
    return _mm(name, a, b, grid=grid, a_spec=a_spec, b_spec=b_spec, o_spec=o_spec,
               out_shape=(m, n), out_dtype=out_dtype, contract=contract, res=res, res_spec=res_spec)


def _blockwise(name, fn, ins, in_specs, outs, out_specs, grid, n_acc=0, acc_all=True):
    n_in, n_out = len(ins), len(outs)
    n_plain = n_out - n_acc

    def body(*refs):
        vals = fn(*[r[...] for r in refs[:n_in]])
        if not isinstance(vals, (tuple, list)):
            vals = (vals,)
        o_refs = refs[n_in:n_in + n_out]
        for r, v in zip(o_refs[:n_plain], vals[:n_plain]):
            r[...] = v.astype(r.dtype)
        if n_acc:
            if acc_all:
                first = functools.reduce(jnp.logical_and, [pl.program_id(d) == 0 for d in range(len(grid))])
            else:
                first = pl.program_id(len(grid) - 1) == 0

            @pl.when(first)
            def _():
                for r, v in zip(o_refs[n_plain:], vals[n_plain:]):
                    r[...] = v.astype(r.dtype)

            @pl.when(jnp.logical_not(first))
            def _():
                for r, v in zip(o_refs[n_plain:], vals[n_plain:]):
                    r[...] += v.astype(r.dtype)

    return pl.pallas_call(
        body, name=name, grid=grid, in_specs=in_specs, out_specs=out_specs,
        out_shape=[jax.ShapeDtypeStruct(s, d) for s, d in outs], compiler_params=_cparams(),
    )(*ins)


def _row_spec(t, c):
    return pl.BlockSpec((t, c), lambda i: (i, 0))


def _full_spec(shape, single=False):
    nd = len(shape)
    if single:
        return pl.BlockSpec(tuple(shape), lambda *g: (0,) * nd, pipeline_mode=pl.Buffered(1))
    return pl.BlockSpec(tuple(shape), lambda *g: (0,) * nd)


def _rms(xf, w):
    return xf * lax.rsqrt(jnp.mean(xf * xf, axis=-1, keepdims=True) + RMS_EPS) * w


def _rms_bwd(xf, w, dy):
    _, vjp = jax.vjp(_rms, xf, w)
    return vjp(dy)


def _s5_disc(lr, li, ldt, bre, bim):
    dt = jnp.exp(ldt)
    mag = jnp.exp(lr * dt)
    ar = mag * jnp.cos(li * dt)
    ai = mag * jnp.sin(li * dt)
    nr, ni = ar - 1.0, ai
    den = lr * lr + li * li
    zr = (nr * lr + ni * li) / den
    zi = (ni * lr - nr * li) / den
    return ar, ai, zr * bre - zi * bim, zr * bim + zi * bre


def _s5_prep(lr, li, ldt, bre, bim):
    def body(lr_r, li_r, ldt_r, bre_r, bim_r, ar_r, ai_r, br_r, bi_r):
        ar, ai, br, bi = _s5_disc(lr_r[...], li_r[...], ldt_r[...], bre_r[...], bim_r[...])
        ar_r[...] = ar
        ai_r[...] = ai
        br_r[...] = br
        bi_r[...] = bi

    sd = jax.ShapeDtypeStruct
    return pl.pallas_call(
        body, name="s5_prep",
        out_shape=[sd(lr.shape, F32), sd(lr.shape, F32), sd(bre.shape, F32), sd(bre.shape, F32)],
        compiler_params=_cparams(),
    )(lr, li, ldt, bre, bim)


def _s5_prep_bwd(lr, li, ldt, bre, bim, dar, dai, dbr, dbi):
    def body(lr_r, li_r, ldt_r, bre_r, bim_r, dar_r, dai_r, dbr_r, dbi_r, o0, o1, o2, o3, o4):
        _, vjp = jax.vjp(_s5_disc, lr_r[...], li_r[...], ldt_r[...], bre_r[...], bim_r[...])
        g = vjp((dar_r[...], dai_r[...], dbr_r[...], dbi_r[...]))
        for o, v in zip((o0, o1, o2, o3, o4), g):
            o[...] = v

    sd = jax.ShapeDtypeStruct
    return pl.pallas_call(
        body, name="s5_prep_bwd",
        out_shape=[sd(lr.shape, F32), sd(li.shape, F32), sd(ldt.shape, F32), sd(bre.shape, F32), sd(bim.shape, F32)],
        compiler_params=_cparams(),
    )(lr, li, ldt, bre, bim, dar, dai, dbr, dbi)


SCAN_T = 256


def _scan_tables(ar, ai, tab_r, tab_i, sub, reverse):
    pr, pi = ar, ai
    for k in range(sub):
        row = sub - 1 - k if reverse else k
        tab_r[row:row + 1, :] = pr
        tab_i[row:row + 1, :] = pi
        pr, pi = ar * pr - ai * pi, ar * pi + ai * pr


def _pack_matrix(t_blk, dtype):
    sub = t_blk // SUBLANES
    dst = jnp.arange(t_blk)
    src = (dst % SUBLANES) * sub + dst // SUBLANES
    return (src[:, None] == jnp.arange(t_blk)[None, :]).astype(dtype)


def _permute_rows_f32(pm, x):
    hi = x.astype(BF16)
    r1 = x - hi.astype(F32)
    mid = r1.astype(BF16)
    lo = (r1 - mid.astype(F32)).astype(BF16)
    dot = lambda v: jnp.dot(pm, v, preferred_element_type=F32)
    return dot(hi) + dot(mid) + dot(lo)


def _scan_block(x, loc, ar, ai, st, tab_r, tab_i, sub, reverse):
    hb = STATE_BLOCK
    a8r = jnp.broadcast_to(ar, (SUBLANES, hb))
    a8i = jnp.broadcast_to(ai, (SUBLANES, hb))
    sr = jnp.zeros((SUBLANES, hb), F32)
    si = jnp.zeros((SUBLANES, hb), F32)
    steps = range(sub - 1, -1, -1) if reverse else range(sub)
    for t in steps:
        rows = slice(t * SUBLANES, (t + 1) * SUBLANES)
        sr, si = a8r * sr - a8i * si + x[rows, :hb], a8r * si + a8i * sr + x[rows, hb:]
        loc[rows, :hb] = sr
        loc[rows, hb:] = si
    cr, ci = st[0:1, :], st[1:2, :]
    far = 0 if reverse else sub - 1
    fr, fi = tab_r[far:far + 1, :], tab_i[far:far + 1, :]
    ent_r, ent_i = [None] * SUBLANES, [None] * SUBLANES
    for c in (range(SUBLANES - 1, -1, -1) if reverse else range(SUBLANES)):
        ent_r[c], ent_i[c] = cr, ci
        cr, ci = sr[c:c + 1, :] + (fr * cr - fi * ci), si[c:c + 1, :] + (fr * ci + fi * cr)
    st[0:1, :] = cr
    st[1:2, :] = ci
    c8r = jnp.concatenate(ent_r, axis=0)
    c8i = jnp.concatenate(ent_i, axis=0)
    out = []
    for t in range(sub):
        rows = slice(t * SUBLANES, (t + 1) * SUBLANES)
        tr, ti = tab_r[t:t + 1, :], tab_i[t:t + 1, :]
        out.append(jnp.concatenate([loc[rows, :hb] + (tr * c8r - ti * c8i), loc[rows, hb:] + (tr * c8i + ti * c8r)],
                                   axis=1))
    return jnp.concatenate(out, axis=0)


SSM_BLOCKS_PER_STEP = 2


def _scan_scratch(nblk, t_blk, sub, hb):
    return [pltpu.VMEM((nblk, SUBLANES, hb), F32), pltpu.VMEM((nblk, sub, hb), F32), pltpu.VMEM((nblk, sub, hb), F32),
            pltpu.VMEM((nblk, t_blk, 2 * hb), F32)]


def _ssm_fwd(proj, wb, wc, a):
    seq = proj.shape[0]
    nj = wb.shape[0]
    w2 = 2 * STATE_BLOCK
    hb = STATE_BLOCK
    t_blk = min(SCAN_T, seq)
    sub = t_blk // SUBLANES
    pm = _pack_matrix(t_blk, BF16)

    npair = SSM_BLOCKS_PER_STEP

    def body(u_ref, wb_ref, wc_ref, a_ref, pm_ref, pmt_ref, s_ref, y_ref, st, tab_r, tab_i, loc):
        coef = [(a_ref[:, b * w2:b * w2 + hb], a_ref[:, b * w2 + hb:(b + 1) * w2]) for b in range(npair)]

        @pl.when(pl.program_id(1) == 0)
        def _():
            for b, (ar, ai) in enumerate(coef):
                st[b] = jnp.zeros((SUBLANES, hb), F32)
                _scan_tables(ar, ai, tab_r.at[b], tab_i.at[b], sub, False)

        for b, (ar, ai) in enumerate(coef):
            ub = u_ref[:, b * LANES:(b + 1) * LANES].astype(BF16)
            up = jnp.dot(pm_ref[...], ub, preferred_element_type=F32).astype(BF16)
            bu = jnp.dot(up, wb_ref[b], preferred_element_type=F32)
            s = _scan_block(bu, loc.at[b], ar, ai, st.at[b], tab_r.at[b], tab_i.at[b], sub, False)
            s_ref[:, b * w2:(b + 1) * w2] = s
            yp = jnp.dot(s.astype(BF16), wc_ref[b], preferred_element_type=F32)
            y_ref[:, b * LANES:(b + 1) * LANES] = _permute_rows_f32(pmt_ref[...], yp)

    sd = jax.ShapeDtypeStruct
    return pl.pallas_call(
        body, name="ssm_fwd", grid=(nj // npair, seq // t_blk),
        in_specs=[pl.BlockSpec((t_blk, npair * LANES), lambda j, i: (i, j)),
                  pl.BlockSpec((npair, LANES, w2), lambda j, i: (j, 0, 0)),
                  pl.BlockSpec((npair, w2, LANES), lambda j, i: (j, 0, 0)),
                  pl.BlockSpec((1, npair * w2), lambda j, i: (0, j)),
                  _full_spec((t_blk, t_blk)), _full_spec((t_blk, t_blk))],
        out_specs=[pl.BlockSpec((t_blk, npair * w2), lambda j, i: (i, j)),
                   pl.BlockSpec((t_blk, npair * LANES), lambda j, i: (i, j))],
        out_shape=[sd((seq, nj * w2), F32), sd((seq, nj * LANES), F32)],
        scratch_shapes=_scan_scratch(npair, t_blk, sub, hb), compiler_params=_cparams(),
    )(proj, wb, wc, a, pm, pm.T)


def _ssm_bwd(dy, s, proj, du1, wb, wc, a):
    seq = dy.shape[0]
    nj = wb.shape[0]
    w2 = 2 * STATE_BLOCK
    hb = STATE_BLOCK
    t_blk = min(SCAN_T, seq)
    sub = t_blk // SUBLANES
    nb = seq // t_blk
    pm = _pack_matrix(t_blk, BF16)

    npair = SSM_BLOCKS_PER_STEP

    def body(dy_ref, s_ref, sprev_ref, u_ref, du1_ref, wb_ref, wc_ref, a_ref, pm_ref, pmt_ref,
             du_ref, dwb_ref, dwc_ref, da_ref, st, tab_r, tab_i, loc):
        ib = pl.program_id(1)
        pmv = pm_ref[...]
        coef = [(a_ref[:, b * w2:b * w2 + hb], -a_ref[:, b * w2 + hb:(b + 1) * w2]) for b in range(npair)]

        @pl.when(ib == 0)
        def _():
            for b, (ar, ai) in enumerate(coef):
                st[b] = jnp.zeros((SUBLANES, hb), F32)
                _scan_tables(ar, ai, tab_r.at[b], tab_i.at[b], sub, True)

        sums = []
        for b, (ar, ai) in enumerate(coef):
            cols, wide = slice(b * LANES, (b + 1) * LANES), slice(b * w2, (b + 1) * w2)
            dyp = jnp.dot(pmv, dy_ref[:, cols], preferred_element_type=F32).astype(BF16)
            up = jnp.dot(pmv, u_ref[:, cols].astype(BF16), preferred_element_type=F32).astype(BF16)
            ds = lax.dot_general(dyp, wc_ref[b], (NT, ((), ())), preferred_element_type=F32)
            lam = _scan_block(ds, loc.at[b], ar, ai, st.at[b], tab_r.at[b], tab_i.at[b], sub, True)
            lamb = lam.astype(BF16)
            du = lax.dot_general(lamb, wb_ref[b], (NT, ((), ())), preferred_element_type=F32)
            du_ref[:, cols] = (_permute_rows_f32(pmt_ref[...], du) + du1_ref[:, cols]).astype(du_ref.dtype)
            sv = s_ref[:, wide]
            dwb = lax.dot_general(up, lamb, (TN, ((), ())), preferred_element_type=F32)
            dwc = lax.dot_general(sv.astype(BF16), dyp, (TN, ((), ())), preferred_element_type=F32)

            prev_last = sprev_ref[SUBLANES - 1:SUBLANES, wide]
            prev_last = jnp.where(ib == nb - 1, jnp.zeros_like(prev_last), prev_last)
            tail = sv[t_blk - SUBLANES:, :]
            sl = lax.broadcasted_iota(jnp.int32, tail.shape, 0)
            head = jnp.where(sl >= 1, pltpu.roll(tail, 1, 0), prev_last)
            s_sh = jnp.concatenate([head, sv[:t_blk - SUBLANES, :]], axis=0)
            lam_r, lam_i = lam[:, :hb], lam[:, hb:]
            sr_, si_ = s_sh[:, :hb], s_sh[:, hb:]
            dar = jnp.sum(lam_r * sr_ + lam_i * si_, axis=0, keepdims=True)
            dai = jnp.sum(lam_i * sr_ - lam_r * si_, axis=0, keepdims=True)
            sums.append((wide, jnp.concatenate([dar, dai], axis=1), dwb, dwc))

        @pl.when(ib == 0)
        def _():
            for b, (wide, contrib, dwb, dwc) in enumerate(sums):
                da_ref[:, wide] = contrib
                dwb_ref[b] = dwb
                dwc_ref[b] = dwc

        @pl.when(ib != 0)
        def _():
            for b, (wide, contrib, dwb, dwc) in enumerate(sums):
                da_ref[:, wide] += contrib
                dwb_ref[b] += dwb
                dwc_ref[b] += dwc

    blk = lambda j, i: (nb - 1 - i, j)
    prev_blk = lambda j, i: (jnp.maximum((nb - 1 - i) * sub - 1, 0), j)
    sd = jax.ShapeDtypeStruct
    return pl.pallas_call(
        body, name="ssm_bwd", grid=(nj // npair, nb),
        in_specs=[pl.BlockSpec((t_blk, npair * LANES), blk), pl.BlockSpec((t_blk, npair * w2), blk),
                  pl.BlockSpec((SUBLANES, npair * w2), prev_blk), pl.BlockSpec((t_blk, npair * LANES), blk),
                  pl.BlockSpec((t_blk, npair * LANES), blk),
                  pl.BlockSpec((npair, LANES, w2), lambda j, i: (j, 0, 0)),
                  pl.BlockSpec((npair, w2, LANES), lambda j, i: (j, 0, 0)),
                  pl.BlockSpec((1, npair * w2), lambda j, i: (0, j)),
                  _full_spec((t_blk, t_blk)), _full_spec((t_blk, t_blk))],
        out_specs=[pl.BlockSpec((t_blk, npair * LANES), blk),
                   pl.BlockSpec((npair, LANES, w2), lambda j, i: (j, 0, 0)),
                   pl.BlockSpec((npair, w2, LANES), lambda j, i: (j, 0, 0)),
                   pl.BlockSpec((1, npair * w2), lambda j, i: (0, j))],
        out_shape=[sd((seq, nj * LANES), BF16), sd((nj, LANES, w2), F32), sd((nj, w2, LANES), F32),
                   sd((1, nj * w2), F32)],
        scratch_shapes=_scan_scratch(npair, t_blk, sub, hb), compiler_params=_cparams(),
    )(dy, s, s, proj, du1, wb, wc, a, pm, pm.T)


def _rope128(x, cos, sa, sb):
    return x * cos + pltpu.roll(x, 96, 1) * sa + pltpu.roll(x, 32, 1) * sb


def _rope128_t(dy, cos, sa, sb):
    return dy * cos + pltpu.roll(dy * sa, 32, 1) + pltpu.roll(dy * sb, 96, 1)


ATT_BQ = 512


def _probs(qn, qp, kn, kp, r0, scale):
    s = lax.dot_general(qn, kn, (NT, ((), ())), preferred_element_type=F32)
    s = s + lax.dot_general(qp, kp, (NT, ((), ())), preferred_element_type=F32)
    s = s * scale
    diag = s[:, r0:]
    row = lax.broadcasted_iota(jnp.int32, diag.shape, 0)
    col = lax.broadcasted_iota(jnp.int32, diag.shape, 1)
    diag = jnp.where(col <= row, diag, jnp.finfo(F32).min)
    s = diag if r0 == 0 else jnp.concatenate([s[:, :r0], diag], axis=1)
    m = jnp.max(s, axis=-1, keepdims=True)
    e = jnp.exp(s - m)
    return e / jnp.sum(e, axis=-1, keepdims=True)


def _attn_specs(seq):
    tab = pl.BlockSpec((seq, LANES), lambda h: (0, 0))
    return [pl.BlockSpec((None, seq, 256), lambda h: (h, 0, 0)), pl.BlockSpec((None, seq, 128), lambda h: (h, 0, 0)),
            pl.BlockSpec((None, seq, 128), lambda h: (h, 0, 1)), tab, tab, tab, tab]


def _attn_fwd(q_raw, kv, kpe, cos, sa, sb):
    nh, seq, _ = q_raw.shape
    bq = min(ATT_BQ, seq)
    scale = (QK_NOPE + QK_ROPE) ** -0.5

    def body(q_ref, kn_ref, v_ref, kp_ref, cos_ref, sa_ref, sb_ref, o_ref):
        for r0 in range(0, seq, bq):
            rows, kend = pl.ds(r0, bq), r0 + bq
            qn = q_ref[rows, :QK_NOPE].astype(BF16)
            qp = _rope128(q_ref[rows, QK_NOPE:], cos_ref[rows, :], sa_ref[rows, :], sb_ref[rows, :]).astype(BF16)
            p = _probs(qn, qp, kn_ref[:kend, :], kp_ref[:kend, :], r0, scale)
            o_ref[rows, :] = jnp.dot(p.astype(BF16), v_ref[:kend, :], preferred_element_type=F32)

    return pl.pallas_call(
        body, name="attn_fwd", grid=(nh,), in_specs=_attn_specs(seq),
        out_specs=pl.BlockSpec((seq, V_DIM), lambda h: (0, h)),
        out_shape=jax.ShapeDtypeStruct((seq, nh * V_DIM), F32), compiler_params=_cparams(),
    )(q_raw, kv, kv, kpe, cos, sa, sb)


def _attn_bwd(q_raw, kv, kpe, cos, sa, sb, do):
    nh, seq, _ = q_raw.shape
    bq = min(ATT_BQ, seq)
    scale = (QK_NOPE + QK_ROPE) ** -0.5

    def body(q_ref, kn_ref, v_ref, kp_ref, cos_ref, sa_ref, sb_ref, do_ref, dq_ref, dkv_ref, dkp_ref):
        dkv_ref[...] = jnp.zeros_like(dkv_ref)
        dkp_ref[...] = jnp.zeros_like(dkp_ref)
        for r0 in range(0, seq, bq):
            rows, kend = pl.ds(r0, bq), r0 + bq
            cos_b, sa_b, sb_b = cos_ref[rows, :], sa_ref[rows, :], sb_ref[rows, :]
            qn = q_ref[rows, :QK_NOPE].astype(BF16)
            qp = _rope128(q_ref[rows, QK_NOPE:], cos_b, sa_b, sb_b).astype(BF16)
            kn, v, kp = kn_ref[:kend, :], v_ref[:kend, :], kp_ref[:kend, :]
            p = _probs(qn, qp, kn, kp, r0, scale)
            dob = do_ref[rows, :].astype(BF16)
            dp = lax.dot_general(dob, v, (NT, ((), ())), preferred_element_type=F32)
            ds = p * (dp - jnp.sum(p * dp, axis=-1, keepdims=True)) * scale
            dsb = ds.astype(BF16)
            pb = p.astype(BF16)
            dq_ref[rows, :QK_NOPE] = jnp.dot(dsb, kn, preferred_element_type=F32).astype(dq_ref.dtype)
            dqp = jnp.dot(dsb, kp, preferred_element_type=F32)
            dq_ref[rows, QK_NOPE:] = _rope128_t(dqp, cos_b, sa_b, sb_b).astype(dq_ref.dtype)
            dkv_ref[:kend, :QK_NOPE] += lax.dot_general(dsb, qn, (TN, ((), ())), preferred_element_type=F32)
            dkv_ref[:kend, QK_NOPE:] += lax.dot_general(pb, dob, (TN, ((), ())), preferred_element_type=F32)
            dkp_ref[:kend, :] += lax.dot_general(dsb, qp, (TN, ((), ())), preferred_element_type=F32)

    sd = jax.ShapeDtypeStruct
    return pl.pallas_call(
        body, name="attn_bwd", grid=(nh,),
        in_specs=_attn_specs(seq) + [pl.BlockSpec((seq, V_DIM), lambda h: (0, h))],
        out_specs=[pl.BlockSpec((None, seq, 256), lambda h: (h, 0, 0)),
                   pl.BlockSpec((None, seq, 256), lambda h: (h, 0, 0)),
                   pl.BlockSpec((None, seq, 128), lambda h: (h, 0, 0))],
        out_shape=[sd((nh, seq, 256), BF16), sd((nh, seq, 256), F32), sd((nh, seq, 128), F32)],
        compiler_params=_cparams(),
    )(q_raw, kv, kv, kpe, cos, sa, sb, do)


def _shift_rows(a, k):
    seq = a.shape[0]
    r = pltpu.roll(a, k % seq, 0)
    rows = lax.broadcasted_iota(jnp.int32, (SUBLANES, a.shape[1]), 0)
    if k > 0:
        return jnp.concatenate([jnp.where(rows >= k, r[:SUBLANES], 0.0), r[SUBLANES:]], axis=0)
    return jnp.concatenate([r[:seq - SUBLANES], jnp.where(rows < SUBLANES + k, r[seq - SUBLANES:], 0.0)], axis=0)


def _conv3(a, w, b):
    a1 = _shift_rows(a, 1)
    a2 = _shift_rows(a, 2)
    return w[2:3] * a + w[1:2] * a1 + w[0:1] * a2 + b, a1, a2


def _conv_gate_fwd(a, cw, cb):
    half, _, seq, c = a.shape
    nc = c // LANES

    def fn(pair, wg, wv, bg, bv):
        gc, _, _ = _conv3(pair[0], wg, bg)
        vc, _, _ = _conv3(pair[1], wv, bv)
        return gc * jax.nn.sigmoid(gc) * vc

    def w_spec(off, r):
        return pl.BlockSpec((None, r, LANES), lambda k, j: (k + off, 0, j))

    return _blockwise(
        "conv_gate_fwd", fn, [a, cw, cw, cb, cb],
        [pl.BlockSpec((None, 2, seq, LANES), lambda k, j: (k, 0, 0, j)),
         w_spec(0, 3), w_spec(half, 3), w_spec(0, 1), w_spec(half, 1)],
        [((seq, half * c), BF16)], [pl.BlockSpec((seq, LANES), lambda k, j: (0, k * nc + j))],
        grid=(half, nc))[0]


def _conv_gate_bwd(a, cw, cb, dm):
    half, _, seq, c = a.shape
    nc = c // LANES

    def body(a_ref, wg_ref, wv_ref, bg_ref, bv_ref, dm_ref, da_ref, dw_ref, db_ref):
        dmv = dm_ref[...]
        ga, wg = a_ref[0], wg_ref[...]
        va, wv = a_ref[1], wv_ref[...]
        gc, g1, g2 = _conv3(ga, wg, bg_ref[...])
        vc, v1, v2 = _conv3(va, wv, bv_ref[...])
        sg = jax.nn.sigmoid(gc)
        dms = dmv * sg
        d_val = dms * gc
        d_gate = dms * vc * (1.0 + gc * (1.0 - sg))

        def back(r, dc, own, a1, a2, w):
            up1 = _shift_rows(dc, -1)
            up2 = _shift_rows(dc, -2)
            da_ref[r] = (w[2:3] * dc + w[1:2] * up1 + w[0:1] * up2).astype(da_ref.dtype)
            dw_ref[r, 0:1, :] = jnp.sum(dc * a2, axis=0, keepdims=True)
            dw_ref[r, 1:2, :] = jnp.sum(dc * a1, axis=0, keepdims=True)
            dw_ref[r, 2:3, :] = jnp.sum(dc * own, axis=0, keepdims=True)
            db_ref[r] = jnp.sum(dc, axis=0, keepdims=True)

        back(0, d_gate, ga, g1, g2, wg)
        back(1, d_val, va, v1, v2, wv)

    def w_spec(off, r):
        return pl.BlockSpec((None, r, LANES), lambda k, j: (k + off, 0, j))

    def pair_spec(r):
        return pl.BlockSpec((None, 2, r, LANES), lambda k, j: (k, 0, 0, j))

    sd = jax.ShapeDtypeStruct
    return pl.pallas_call(
        body, name="conv_gate_bwd", grid=(half, nc),
        in_specs=[pair_spec(seq), w_spec(0, 3), w_spec(half, 3), w_spec(0, 1), w_spec(half, 1),
                  pl.BlockSpec((seq, LANES), lambda k, j: (0, k * nc + j))],
        out_specs=[pair_spec(seq), pair_spec(3), pair_spec(1)],
        out_shape=[sd((half, 2, seq, c), BF16), sd((half, 2, 3, c), F32), sd((half, 2, 1, c), F32)],
        compiler_params=_cparams(),
    )(a, cw, cw, cb, cb, dm)


ROW_T = 256


def _local_step(x, positions, target, w, emit=lambda **grads: None):
    seq, d = x.shape
    t_row = min(ROW_T, seq)
    nrow = seq // t_row
    ssm_w = d // 2
    nj = ssm_w // LANES
    n_groups = ssm_w // SSM_GROUP
    nh = w["wuq"].shape[0]
    q_rank = w["wuq"].shape[1]
    kv_rank = w["wukv"].shape[1]
    ns = w["wup"].shape[0]
    c_ff = w["wup"].shape[2]
    in_pad = w["win"].shape[1]
    tm = min(1024, seq)
    nm = seq // tm
    sw = 2 * STATE_BLOCK
    g1 = (nrow,)

    lr3 = w["lam_re"].reshape(n_groups, 1, SSM_STATE)
    li3 = w["lam_im"].reshape(n_groups, 1, SSM_STATE)
    ldt3 = w["log_dt"].reshape(n_groups, 1, 1)
    bt_re = jnp.swapaxes(w["b_re"].reshape(n_groups, SSM_STATE, SSM_GROUP), 1, 2)
    bt_im = jnp.swapaxes(w["b_im"].reshape(n_groups, SSM_STATE, SSM_GROUP), 1, 2)
    abar_re, abar_im, bbt_re, bbt_im = _s5_prep(lr3, li3, ldt3, bt_re, bt_im)
    eye = jnp.eye(GROUPS_PER_BLOCK, dtype=F32)

    def blockdiag_in(bb):
        t = bb.reshape(nj, GROUPS_PER_BLOCK, SSM_GROUP, SSM_STATE)
        return jnp.einsum("jghp,gk->jghkp", t, eye).reshape(nj, LANES, STATE_BLOCK)

    def blockdiag_in_t(dwb):
        t = dwb.reshape(nj, GROUPS_PER_BLOCK, SSM_GROUP, GROUPS_PER_BLOCK, SSM_STATE)
        return jnp.einsum("jghkp,gk->jghp", t, eye).reshape(n_groups, SSM_GROUP, SSM_STATE)

    def blockdiag_out(cc):
        t = cc.reshape(nj, GROUPS_PER_BLOCK, SSM_GROUP, SSM_STATE)
        return jnp.einsum("jghp,gk->jkpgh", t, eye).reshape(nj, STATE_BLOCK, LANES)

    def blockdiag_out_t(dwc):
        t = dwc.reshape(nj, GROUPS_PER_BLOCK, SSM_STATE, GROUPS_PER_BLOCK, SSM_GROUP)
        return jnp.einsum("jkpgh,gk->jghp", t, eye).reshape(n_groups, SSM_GROUP, SSM_STATE)

    c_re = w["c_re"].reshape(n_groups, SSM_GROUP, SSM_STATE)
    c_im = w["c_im"].reshape(n_groups, SSM_GROUP, SSM_STATE)
    wb = jnp.concatenate([blockdiag_in(bbt_re), blockdiag_in(bbt_im)], axis=2).astype(BF16)
    wc = jnp.concatenate([blockdiag_out(c_re), -blockdiag_out(c_im)], axis=1).astype(BF16)
    a_lay = jnp.concatenate([abar_re.reshape(nj, 1, STATE_BLOCK), abar_im.reshape(nj, 1, STATE_BLOCK)],
                            axis=1).reshape(1, nj * sw)

    attn_w = w["attn_norm"]
    t_wide = min(2 * t_row, seq)
    g_wide = (seq // t_wide,)

    def proj_fn(xb, wv, wi):
        hb = _rms(xb, wv).astype(BF16)
        return hb, jnp.dot(hb, wi, preferred_element_type=F32)

    hn, proj = _blockwise(
        "norm1_proj", proj_fn, [x, attn_w, w["win"]],
        [_row_spec(t_wide, d), _full_spec((1, d)), _full_spec((d, in_pad), single=True)],
        [((seq, d), BF16), ((seq, in_pad), F32)], [_row_spec(t_wide, d), _row_spec(t_wide, in_pad)], g_wide)

    s_all, ylin = _ssm_fwd(proj, wb, wc, a_lay)

    def glu_fwd_fn(yl, ub, dsk, wg, bg):
        yp = yl + dsk * ub
        ygv = jax.nn.gelu(yp)
        ygb = ygv.astype(BF16)
        zb = jnp.dot(ygb, wg, preferred_element_type=F32) + bg
        return yp, ygb, zb, ygv * jax.nn.sigmoid(zb)

    wide = pl.BlockSpec((t_wide, ssm_w), lambda i: (i, 0))
    y_pre, yg, z, y_ssm = _blockwise(
        "ssm_glu_fwd", glu_fwd_fn, [ylin, proj, w["ssm_d"], w["wglu"], w["b_glu"]],
        [wide, wide, _full_spec((1, ssm_w)), _full_spec((ssm_w, ssm_w), single=True), _full_spec((1, ssm_w))],
        [((seq, ssm_w), F32), ((seq, ssm_w), BF16), ((seq, ssm_w), F32), ((seq, ssm_w), F32)], [wide] * 4, g_wide)

    cq_off, ckv_off, kpe_off = ssm_w, ssm_w + q_rank, ssm_w + q_rank + kv_rank
    assert cq_off % q_rank == 0 and ckv_off % kv_rank == 0 and kpe_off % LANES == 0
    cq_spec = pl.BlockSpec((t_row, q_rank), lambda i: (i, cq_off // q_rank))
    ckv_spec = pl.BlockSpec((t_row, kv_rank), lambda i: (i, ckv_off // kv_rank))
    kpe_spec = pl.BlockSpec((t_row, LANES), lambda i: (i, kpe_off // LANES))
    pos_b = jnp.broadcast_to(positions.astype(F32)[:, None], (seq, LANES))
    inv_freq = ROPE_THETA ** (-jnp.arange(0, QK_ROPE, 2, dtype=F32) / QK_ROPE)
    inv128 = jnp.tile(inv_freq, 4).reshape(1, LANES)

    def mla_prep_fn(cq, ckv, kp, pb, inv, wq, wkv):
        ang = pb * inv
        lane = lax.broadcasted_iota(jnp.int32, ang.shape, 1)
        cs, sn = jnp.cos(ang), jnp.sin(ang)
        cos = jnp.where(lane < QK_ROPE, cs, 0.0)
        sa = jnp.where(lane < QK_ROPE // 2, -sn, 0.0)
        sb = jnp.where(jnp.logical_and(lane >= QK_ROPE // 2, lane < QK_ROPE), sn, 0.0)
        return _rms(cq, wq), _rms(ckv, wkv), _rope128(kp, cos, sa, sb), cos, sa, sb

    qn, kvn, kpe, cos_t, sa_t, sb_t = _blockwise(
        "mla_prep", mla_prep_fn, [proj, proj, proj, pos_b, inv128, w["q_norm"], w["kv_norm"]],
        [cq_spec, ckv_spec, kpe_spec, _row_spec(t_row, LANES),
         _full_spec((1, LANES)), _full_spec((1, q_rank)), _full_spec((1, kv_rank))],
        [((seq, q_rank), BF16), ((seq, kv_rank), BF16), ((seq, LANES), BF16)] + [((seq, LANES), F32)] * 3,
        [_row_spec(t_row, q_rank), _row_spec(t_row, kv_rank)] + [_row_spec(t_row, LANES)] * 4, g1)

    def head_mm(name, act, wh, out_dtype):
        kdim, ndim = wh.shape[1], wh.shape[2]
        return _mm(name, act, wh, grid=(nh, 1, 1),
                   a_spec=pl.BlockSpec((seq, kdim), lambda h, i, k: (i, 0)),
                   b_spec=pl.BlockSpec((None, kdim, ndim), lambda h, i, k: (h, 0, 0)),
                   o_spec=pl.BlockSpec((None, seq, ndim), lambda h, i, k: (h, i, 0)),
                   out_shape=(nh, seq, ndim), out_dtype=out_dtype)

    q_raw = head_mm("mla_q", qn, w["wuq"], F32)
    kv = head_mm("mla_kv", kvn, w["wukv"], BF16)
    y_mla = _attn_fwd(q_raw, kv, kpe, cos_t, sa_t, sb_t)
    mla_w = nh * V_DIM

    def out_proj_fn(ys, ym, ws, wm, wo, xb, wf):
        yc = jnp.concatenate([_rms(ys, ws), _rms(ym, wm)], axis=1).astype(BF16)
        hb = xb + jnp.dot(yc, wo, preferred_element_type=F32)
        return yc, hb, _rms(hb, wf)

    ycat, h1, hn2 = _blockwise(
        "out_norm_proj", out_proj_fn, [y_ssm, y_mla, w["son"], w["mon"], w["wout"], x, w["ffn_norm"]],
        [wide, _row_spec(t_wide, mla_w), _full_spec((1, ssm_w)), _full_spec((1, mla_w)),
         _full_spec((d, d), single=True), _row_spec(t_wide, d), _full_spec((1, d))],
        [((seq, d), BF16), ((seq, d), F32), ((seq, d), BF16)], [_row_spec(t_wide, d)] * 3, g_wide)

    tku = d
    half = ns // 2
    a_ff = _mm("ffn_up", hn2, w["wup"], grid=(ns, nm, d // tku),
               a_spec=pl.BlockSpec((tm, tku), lambda s, i, k: (i, k)),
               b_spec=pl.BlockSpec((None, tku, c_ff), lambda s, i, k: (s, k, 0)),
               o_spec=pl.BlockSpec((None, None, tm, c_ff), lambda s, i, k: (s % half, s // half, i, 0)),
               out_shape=(half, 2, seq, c_ff), out_dtype=F32)
    cb3 = w["conv_b"].reshape(ns, 1, c_ff)
    m_ff = _conv_gate_fwd(a_ff, w["conv_w"], cb3)
    d_ff = half * c_ff
    wdn = w["wdown"]
    tnd = _tile(d, 1024)
    tmx, tnx = min(1024, seq), _tile(d, 1024)
    h2 = _mm2d("ffn_down", m_ff, wdn, NN, F32, tm=512, tn=512, tk=d_ff, res=h1)

    def loss_fn(hb, tb, wv):
        def f(hh, ww):
            err = _rms(hh, ww) - tb
            return 0.5 * jnp.sum(jnp.mean(err * err, axis=-1))

        lossv, (dh, dw) = jax.value_and_grad(f, argnums=(0, 1))(hb, wv)
        return dh, dh, jnp.full((1, LANES), lossv, F32), dw

    fin_w = w["final_norm"].reshape(1, d)
    dh2, dh2b, loss_acc, g_final = _blockwise(
        "loss_head", loss_fn, [h2, target, fin_w], [_row_spec(t_row, d), _row_spec(t_row, d), _full_spec((1, d))],
        [((seq, d), F32), ((seq, d), BF16), ((1, LANES), F32), ((1, d), F32)],
        [_row_spec(t_row, d), _row_spec(t_row, d), _full_spec((1, LANES)), _full_spec((1, d))], g1, n_acc=2)
    loss = loss_acc

    dm = _mm2d("ffn_down_dx", dh2b, wdn, NT, F32, tn=c_ff)
    tks = seq
    g_wdown = _mm2d("ffn_down_dw", m_ff, dh2b, TN, BF16, tm=c_ff)
    emit(wdown=g_wdown)
    da_ff, g_convw2, g_convb2 = _conv_gate_bwd(a_ff, w["conv_w"], cb3, dm)
    g_convw = jnp.swapaxes(g_convw2, 0, 1).reshape(ns, 3, c_ff)
    g_convb = jnp.swapaxes(g_convb2, 0, 1).reshape(ns, 1, c_ff)
    g_wup = _mm("ffn_up_dw", hn2, da_ff, grid=(ns, d // tnd, seq // tks), contract=TN,
                a_spec=pl.BlockSpec((tks, tnd), lambda s, j, k: (k, j)),
                b_spec=pl.BlockSpec((None, None, tks, c_ff), lambda s, j, k: (s % half, s // half, k, 0)),
                o_spec=pl.BlockSpec((None, tnd, c_ff), lambda s, j, k: (s, j, 0)),
                out_shape=(ns, d, c_ff), out_dtype=BF16)
    emit(wup=g_wup)
    dhn2 = _mm("ffn_up_dx", da_ff, w["wup"], grid=(seq // tmx, d // tnx, ns), contract=NT,
               a_spec=pl.BlockSpec((None, None, tmx, c_ff), lambda i, j, s: (s % half, s // half, i, 0)),
               b_spec=pl.BlockSpec((None, tnx, c_ff), lambda i, j, s: (s, j, 0)),
               o_spec=pl.BlockSpec((tmx, tnx), lambda i, j, s: (i, j)),
               out_shape=(seq, d), out_dtype=F32)
    emit(wup_pair_sums_after=dhn2)

    def norm_bwd_fn(hb, dres, dn, wv):
        dx_, dw_ = _rms_bwd(hb, wv, dn)
        dtot = dres + dx_
        return dtot, dtot, dw_

    dh1, dh1b, g_ffn_norm = _blockwise(
        "norm2_bwd", norm_bwd_fn, [h1, dh2, dhn2, w["ffn_norm"]],
        [_row_spec(t_row, d)] * 3 + [_full_spec((1, d))],
        [((seq, d), F32), ((seq, d), BF16), ((1, d), F32)],
        [_row_spec(t_row, d), _row_spec(t_row, d), _full_spec((1, d))], g1, n_acc=1)

    g_wout = _mm2d("out_proj_dw", ycat, dh1b, TN, BF16)

    def outnorm_bwd_fn(dhb, wo, ys, ym, ws, wm):
        dyc = lax.dot_general(dhb, wo, (NT, ((), ())), preferred_element_type=F32)
        dys, dws = _rms_bwd(ys, ws, dyc[:, :ssm_w])
        dym, dwm = _rms_bwd(ym, wm, dyc[:, ssm_w:])
        return dys, dym, dws, dwm

    dy_ssm, dy_mla, g_son, g_mon = _blockwise(
        "out_proj_dx_norm_bwd", outnorm_bwd_fn, [dh1b, w["wout"], y_ssm, y_mla, w["son"], w["mon"]],
        [_row_spec(t_wide, d), _full_spec((d, d), single=True), wide, _row_spec(t_wide, mla_w),
         _full_spec((1, ssm_w)), _full_spec((1, mla_w))],
        [((seq, ssm_w), F32), ((seq, mla_w), F32), ((1, ssm_w), F32), ((1, mla_w), F32)],
        [wide, _row_spec(t_wide, mla_w), _full_spec((1, ssm_w)), _full_spec((1, mla_w))],
        g_wide, n_acc=2)

    def glu_bwd_fn(dy, yp, zb, ub, dsk, wg):
        ygv = jax.nn.gelu(yp)
        sg = jax.nn.sigmoid(zb)
        dz = dy * ygv * sg * (1.0 - sg)
        dzb = dz.astype(BF16)
        dyg = dy * sg + lax.dot_general(dzb, wg, (NT, ((), ())), preferred_element_type=F32)
        _, vjp = jax.vjp(jax.nn.gelu, yp)
        dyp = vjp(dyg)[0]
        return (dzb, dyp, dyp * dsk, jnp.sum(dz, axis=0, keepdims=True), jnp.sum(dyp * ub, axis=0, keepdims=True))

    dz, dy_pre, du1, g_bglu, g_ssmd = _blockwise(
        "ssm_glu_bwd", glu_bwd_fn, [dy_ssm, y_pre, z, proj, w["ssm_d"], w["wglu"]],
        [wide] * 4 + [_full_spec((1, ssm_w)), _full_spec((ssm_w, ssm_w), single=True)],
        [((seq, ssm_w), BF16), ((seq, ssm_w), BF16), ((seq, ssm_w), F32), ((1, ssm_w), F32), ((1, ssm_w), F32)],
        [wide] * 3 + [_full_spec((1, ssm_w))] * 2, g_wide, n_acc=2)
    g_wglu = _mm2d("ssm_glu_dw", yg, dz, TN, BF16)
    dq_raw, dkv, dkp_h = _attn_bwd(q_raw, kv, kpe, cos_t, sa_t, sb_t, dy_mla)

    def head_mm_dx(name, dact, wh):
        kdim, ndim = wh.shape[1], wh.shape[2]
        return _mm(name, dact, wh, grid=(1, 1, nh), contract=NT,
                   a_spec=pl.BlockSpec((None, seq, ndim), lambda i, j, h: (h, i, 0)),
                   b_spec=pl.BlockSpec((None, kdim, ndim), lambda i, j, h: (h, 0, 0)),
                   o_spec=pl.BlockSpec((seq, kdim), lambda i, j, h: (i, 0)),
                   out_shape=(seq, kdim), out_dtype=F32)

    def head_mm_dw(name, act, dact):
        kdim, ndim = act.shape[1], dact.shape[2]
        return _mm(name, act, dact, grid=(nh, 1, seq // tks), contract=TN,
                   a_spec=pl.BlockSpec((tks, kdim), lambda h, j, k: (k, 0)),
                   b_spec=pl.BlockSpec((None, tks, ndim), lambda h, j, k: (h, k, 0)),
                   o_spec=pl.BlockSpec((None, kdim, ndim), lambda h, j, k: (h, 0, 0)),
                   out_shape=(nh, kdim, ndim), out_dtype=BF16)

    g_wuq = head_mm_dw("mla_q_dw", qn, dq_raw)
    g_wukv = head_mm_dw("mla_kv_dw", kvn, dkv)
    dqn = head_mm_dx("mla_q_dx", dq_raw, w["wuq"])
    dkvn = head_mm_dx("mla_kv_dx", dkv, w["wukv"])
    emit(not_before=(dqn, dkvn, dy_pre), wout=g_wout, wuq=g_wuq, wukv=g_wukv, wglu=g_wglu, conv_w=g_convw)

    du, dwb, dwc, da_lay = _ssm_bwd(dy_pre, s_all, proj, du1, wb, wc, a_lay)
    g_c_re = blockdiag_out_t(dwc[:, :STATE_BLOCK, :])
    g_c_im = -blockdiag_out_t(dwc[:, STATE_BLOCK:, :])
    dbbt_re = blockdiag_in_t(dwb[:, :, :STATE_BLOCK])
    dbbt_im = blockdiag_in_t(dwb[:, :, STATE_BLOCK:])
    da3 = da_lay.reshape(nj, 2, STATE_BLOCK)
    dabar_re = da3[:, 0, :].reshape(n_groups, 1, SSM_STATE)
    dabar_im = da3[:, 1, :].reshape(n_groups, 1, SSM_STATE)
    g_lr3, g_li3, g_ldt3, g_bt_re, g_bt_im = _s5_prep_bwd(lr3, li3, ldt3, bt_re, bt_im,
                                                           dabar_re, dabar_im, dbbt_re, dbbt_im)

    def mla_prep_bwd_fn(cq, ckv, dqn_b, dkvn_b, dkp_b, cos, sa, sb, wq, wkv):
        dcq, dwq = _rms_bwd(cq, wq, dqn_b)
        dckv, dwkv = _rms_bwd(ckv, wkv, dkvn_b)
        dkp_sum = dkp_b[0]
        for h in range(1, nh):
            dkp_sum = dkp_sum + dkp_b[h]
        return dcq, dckv, _rope128_t(dkp_sum, cos, sa, sb), dwq, dwkv

    dc_q, dc_kv, dkpe_raw, g_qnorm, g_kvnorm = _blockwise(
        "mla_prep_bwd", mla_prep_bwd_fn, [proj, proj, dqn, dkvn, dkp_h, cos_t, sa_t, sb_t, w["q_norm"], w["kv_norm"]],
        [cq_spec, ckv_spec, _row_spec(t_row, q_rank), _row_spec(t_row, kv_rank),
         pl.BlockSpec((nh, t_row, LANES), lambda i: (0, i, 0))] + [_row_spec(t_row, LANES)] * 3
        + [_full_spec((1, q_rank)), _full_spec((1, kv_rank))],
        [((seq, q_rank), BF16), ((seq, kv_rank), BF16), ((seq, LANES), BF16), ((1, q_rank), F32), ((1, kv_rank), F32)],
        [_row_spec(t_row, q_rank), _row_spec(t_row, kv_rank), _row_spec(t_row, LANES), _full_spec((1, q_rank)),
         _full_spec((1, kv_rank))], g1, n_acc=2)

    dproj = jnp.concatenate([du, dc_q, dc_kv, dkpe_raw], axis=1)
    g_win = _mm2d("proj_dw", hn, dproj, TN, BF16, tn=640)
    emit(win=g_win)
    def norm1_bwd_fn(dpb, wi, xb, dres, wv):
        dn = lax.dot_general(dpb, wi, (NT, ((), ())), preferred_element_type=F32)
        dx_, dw_ = _rms_bwd(xb, wv, dn)
        return dres + dx_, dw_

    grad_x, g_attn_norm = _blockwise(
        "proj_dx_norm1_bwd", norm1_bwd_fn, [dproj, w["win"], x, dh1, attn_w],
        [_row_spec(t_row, in_pad), _full_spec((d, in_pad), single=True), _row_spec(t_row, d), _row_spec(t_row, d), _full_spec((1, d))],
        [((seq, d), F32), ((1, d), F32)], [_row_spec(t_row, d), _full_spec((1, d))], g1, n_acc=1)
    emit(win_pair_sums_after=grad_x)

    grads = dict(
        attn_norm=g_attn_norm, win=g_win, lam_re=g_lr3, lam_im=g_li3, log_dt=g_ldt3,
        bt_re=g_bt_re, bt_im=g_bt_im, c_re=g_c_re, c_im=g_c_im,
        ssm_d=g_ssmd, wglu=g_wglu, b_glu=g_bglu, q_norm=g_qnorm, wuq=g_wuq, kv_norm=g_kvnorm, wukv=g_wukv,
        son=g_son, mon=g_mon, wout=g_wout, ffn_norm=g_ffn_norm, wup=g_wup, conv_w=g_convw, conv_b=g_convb,
        wdown=g_wdown, final_norm=g_final)
    return loss, grad_x, grads


def _mesh_pos():
    return lax.axis_index("x"), lax.axis_index("y"), lax.axis_index("c")


def _handshake_all():
    x, y, c = _mesh_pos()
    barrier = pltpu.get_barrier_semaphore()
    for k in range(1, N_DEV):
        peer = (1 - x if k & 4 else x, 1 - y if k & 2 else y, 1 - c if k & 1 else c)
        pl.semaphore_signal(barrier, inc=1, device_id=peer, device_id_type=MESH)
    pl.semaphore_wait(barrier, N_DEV - 1)


def _handshake(peers):
    barrier = pltpu.get_barrier_semaphore()
    for peer in peers:
        pl.semaphore_signal(barrier, inc=1, device_id=peer, device_id_type=MESH)
    pl.semaphore_wait(barrier, len(peers))


def _comm_call(name, body, n, out_shape, ins, collective_id, after=None, copies=7, n_remote=None, n_local=None):
    n_remote = copies * n if n_remote is None else n_remote
    sems = [pltpu.SemaphoreType.DMA((n_remote,)), pltpu.SemaphoreType.DMA((n_remote,)),
            pltpu.SemaphoreType.DMA((n if n_local is None else n_local,))]
    if collective_id is None:
        any_spec = pl.BlockSpec(memory_space=pl.ANY)
        return pl.pallas_call(body, name=name, out_shape=out_shape, in_specs=[any_spec] * n,
                              out_specs=[any_spec] * n, scratch_shapes=sems)(*ins)
    seq_body = body
    if after:
        n_after = len(after)
        ins = list(ins) + list(after)

        def seq_body(*refs):
            body(*refs[:n], *refs[n + n_after:])

    return pl.kernel(seq_body, name=name, out_type=out_shape,
                     mesh=plsc.ScalarSubcoreMesh(axis_name="seq", num_cores=1), scratch_types=sems,
                     compiler_params=pltpu.CompilerParams(collective_id=collective_id))(*ins)


def _all_gather(name, xs, collective_id=None, after=None, pair_sums=()):
    n = len(xs)
    nh = len(pair_sums)
    m = n + nh

    def body(*refs):
        x_refs, h_refs, o_refs, e_refs = refs[:n], refs[n:m], refs[m:m + n], refs[m + n:2 * m]
        send_sems, recv_sems, local_sems = refs[2 * m:]
        if collective_id is not None:
            _handshake_all()
        finish_pairs = _chip_copies(h_refs, e_refs, send_sems, recv_sems, local_sems, 7 * n, n) if nh else None
        x, y, c = _mesh_pos()
        me, sibling = (x, y, c), (x, y, 1 - c)
        chips = [(1 - x, y), (x, 1 - y), (1 - x, 1 - y)]

        def slot(o_ref, px, py, pc):
            return o_ref.at[4 * px + 2 * py + pc]

        def copy(t, k, block, to, src=None):
            dst = slot(o_refs[t], *block)
            return pltpu.make_async_remote_copy(
                src_ref=dst if src is None else src, dst_ref=dst,
                send_sem=send_sems.at[7 * t + k], recv_sem=recv_sems.at[7 * t + k],
                device_id=to, device_id_type=MESH)

        started = []
        for t in range(n):
            mine = pltpu.make_async_copy(x_refs[t], slot(o_refs[t], *me), local_sems.at[t])
            mine.start()
            started.append(mine)
        first = []
        for t in range(n):
            first.append(copy(t, 0, me, sibling, src=x_refs[t]))
            first += [copy(t, 1 + j, me, (*chip, c), src=x_refs[t]) for j, chip in enumerate(chips)]
        for cp in first:
            cp.start()
        passed = []
        for j, chip in enumerate(chips):
            for t in range(n):
                copy(t, 1 + j, (*chip, c), me).wait_recv()
                fwd = copy(t, 4 + j, (*chip, c), sibling)
                fwd.start()
                passed.append(fwd)
        for t in range(n):
            copy(t, 0, sibling, me).wait_recv()
            for j, chip in enumerate(chips):
                copy(t, 4 + j, (*chip, 1 - c), me).wait_recv()
        for cp in first + passed:
            cp.wait_send()
        for mine in started:
            mine.wait()
        if nh:
            finish_pairs()

    out_shape = ([jax.ShapeDtypeStruct((N_DEV,) + v.shape, v.dtype) for v in xs]
                 + [jax.ShapeDtypeStruct(v.shape, v.dtype) for v in pair_sums])
    return _comm_call(name, body, m, out_shape, list(xs) + list(pair_sums), collective_id, after,
                      n_remote=7 * n + (N_CHIP - 1) * nh, n_local=m)


def _exchange_partials(name, gs, collective_id=None, after=None):
    n = len(gs)

    def body(*refs):
        g_refs, o_refs = refs[:n], refs[n:2 * n]
        send_sems, recv_sems, local_sems = refs[2 * n:]
        if collective_id is not None:
            _handshake_all()
        x, y, c = _mesh_pos()
        me_idx = 4 * x + 2 * y + c
        copies = []
        for t in range(n):
            mine = pltpu.make_async_copy(g_refs[t].at[me_idx], o_refs[t].at[me_idx], local_sems.at[t])
            mine.start()
            copies.append(mine)
        remote = []
        for k in range(1, N_DEV):
            px = 1 - x if k & 4 else x
            py = 1 - y if k & 2 else y
            pc = 1 - c if k & 1 else c
            p_idx = 4 * px + 2 * py + pc
            for t in range(n):
                cp = pltpu.make_async_remote_copy(
                    src_ref=g_refs[t].at[p_idx], dst_ref=o_refs[t].at[me_idx],
                    send_sem=send_sems.at[7 * t + k - 1], recv_sem=recv_sems.at[7 * t + k - 1],
                    device_id=(px, py, pc), device_id_type=MESH)
                cp.start()
                landing = pltpu.make_async_remote_copy(
                    src_ref=g_refs[t].at[p_idx], dst_ref=o_refs[t].at[p_idx],
                    send_sem=send_sems.at[7 * t + k - 1], recv_sem=recv_sems.at[7 * t + k - 1],
                    device_id=(px, py, pc), device_id_type=MESH)
                remote.append((cp, landing))
        for cp, landing in remote:
            landing.wait_recv()
        for cp, landing in remote:
            cp.wait_send()
        for mine in copies:
            mine.wait()

    out_shape = [jax.ShapeDtypeStruct(v.shape, v.dtype) for v in gs]
    return _comm_call(name, body, n, out_shape, gs, collective_id, after)


N_CHIP = N_DEV // 2
PAIR_ADD_BLOCK_ELEMS = 1024 * 1024


def _pair_swap(name, gs, collective_id, after=None):
    n = len(gs)

    def body(*refs):
        g_refs, o_refs = refs[:n], refs[n:2 * n]
        send_sems, recv_sems, _ = refs[2 * n:]
        x, y, c = _mesh_pos()
        sibling = (x, y, 1 - c)
        _handshake([sibling])
        copies = []
        for t in range(n):
            for k in range(N_CHIP):
                copies.append(pltpu.make_async_remote_copy(
                    src_ref=g_refs[t].at[2 * k + 1 - c], dst_ref=o_refs[t].at[k],
                    send_sem=send_sems.at[N_CHIP * t + k], recv_sem=recv_sems.at[N_CHIP * t + k],
                    device_id=sibling, device_id_type=MESH))
        for cp in copies:
            cp.start()
        for cp in copies:
            cp.wait_recv()
        for cp in copies:
            cp.wait_send()

    out_shape = [jax.ShapeDtypeStruct((N_CHIP,) + v.shape[1:], v.dtype) for v in gs]
    return _comm_call(name, body, n, out_shape, gs, collective_id, after, copies=N_CHIP)


def _pair_add(name, g, got):
    _, r, c = g.shape
    tr = r
    if r * c > PAIR_ADD_BLOCK_ELEMS and r % SUBLANES == 0:
        tr = SUBLANES
        while r % (tr * 2) == 0 and tr * 2 * c <= PAIR_ADD_BLOCK_ELEMS:
            tr *= 2

    def body(core_ref, g_ref, got_ref, o_ref):
        o_ref[...] = (g_ref[...].astype(F32) + got_ref[...].astype(F32)).astype(o_ref.dtype)

    grid_spec = pltpu.PrefetchScalarGridSpec(
        num_scalar_prefetch=1, grid=(N_CHIP, r // tr),
        in_specs=[pl.BlockSpec((None, None, tr, c), lambda k, i, core: (k, core[0], i, 0)),
                  pl.BlockSpec((None, tr, c), lambda k, i, core: (k, i, 0))],
        out_specs=pl.BlockSpec((None, tr, c), lambda k, i, core: (k, i, 0)))
    core = lax.axis_index("c").astype(jnp.int32).reshape(1)
    return pl.pallas_call(body, name=name, grid_spec=grid_spec, out_shape=jax.ShapeDtypeStruct((N_CHIP, r, c), g.dtype),
                          compiler_params=_cparams())(core, g.reshape(N_CHIP, 2, r, c), got)


def _chip_copies(h_refs, o_refs, send_sems, recv_sems, local_sems, sem0, local0):
    n = len(h_refs)
    per = N_CHIP - 1
    x, y, c = _mesh_pos()
    others = [(1 - x if k & 2 else x, 1 - y if k & 1 else y) for k in range(1, N_CHIP)]
    my_chip = 2 * x + y
    local = []
    for t in range(n):
        mine = pltpu.make_async_copy(h_refs[t].at[my_chip], o_refs[t].at[my_chip], local_sems.at[local0 + t])
        mine.start()
        local.append(mine)
    remote = []
    for j, (px, py) in enumerate(others):
        chip = 2 * px + py
        for t in range(n):
            sems = dict(send_sem=send_sems.at[sem0 + per * t + j], recv_sem=recv_sems.at[sem0 + per * t + j],
                        device_id=(px, py, c), device_id_type=MESH)
            cp = pltpu.make_async_remote_copy(src_ref=h_refs[t].at[chip], dst_ref=o_refs[t].at[my_chip], **sems)
            cp.start()
            landing = pltpu.make_async_remote_copy(src_ref=h_refs[t].at[chip], dst_ref=o_refs[t].at[chip], **sems)
            remote.append((cp, landing))

    def finish():
        for cp, landing in remote:
            landing.wait_recv()
        for cp, landing in remote:
            cp.wait_send()
        for mine in local:
            mine.wait()

    return finish


def _chip_exchange(name, hs, collective_id, after=None):
    n = len(hs)
    per = N_CHIP - 1

    def body(*refs):
        h_refs, o_refs = refs[:n], refs[n:2 * n]
        send_sems, recv_sems, local_sems = refs[2 * n:]
        x, y, c = _mesh_pos()
        _handshake([(1 - x if k & 2 else x, 1 - y if k & 1 else y, c) for k in range(1, N_CHIP)])
        _chip_copies(h_refs, o_refs, send_sems, recv_sems, local_sems, 0, 0)()

    out_shape = [jax.ShapeDtypeStruct(v.shape, v.dtype) for v in hs]
    return _comm_call(name, body, n, out_shape, hs, collective_id, after, copies=per)


ADAM_BLOCK_ELEMS = 256 * 1024


def _sum_parts(pb):
    g = pb[0].astype(F32)
    for j in range(1, pb.shape[0]):
        g = g + pb[j].astype(F32)
    return g


def _adam_math(g, wb_, mb, vb):
    m_new = ADAM_B1 * mb + (1.0 - ADAM_B1) * g
    v_new = ADAM_B2 * vb + (1.0 - ADAM_B2) * (g * g)
    m_hat = m_new / (1.0 - ADAM_B1 ** ADAM_STEP)
    v_hat = v_new / (1.0 - ADAM_B2 ** ADAM_STEP)
    delta = -ADAM_LR * (m_hat / (jnp.sqrt(v_hat) + ADAM_EPS) + ADAM_WD * wb_)
    return g, delta, m_new, v_new


def _adamw_multi(name, items, nblk=1, packed=None):
    n = len(items)

    def spec(shape, lead):
        blk = list(shape)
        blk[lead + 1] = shape[lead + 1] // nblk
        if nblk == 1:
            return pl.BlockSpec(tuple(blk), lambda i, nd=len(shape): (0,) * nd)
        return pl.BlockSpec(tuple(blk), lambda i, nd=len(shape), ax=lead + 1: (0,) * ax + (i,) + (0,) * (nd - ax - 1))

    ins, in_specs, out_specs, out_shape, where = [], [], [], [], []
    if packed is not None:
        ins.append(packed)
        in_specs.append(spec(packed.shape, 1))
    for parts, wv, mv, vv in items:
        if isinstance(parts, int):
            where.append((0, parts, len(ins)))
        else:
            assert parts.shape[1:] == wv.shape, (name, parts.shape, wv.shape)
            where.append((len(ins), None, len(ins) + 1))
            ins.append(parts)
            in_specs.append(spec(parts.shape, 1))
        ins += [wv, mv, vv]
        in_specs += [spec(wv.shape, 0)] * 3
        out_specs += [spec(wv.shape, 0)] * 4
        out_shape += [jax.ShapeDtypeStruct(wv.shape, F32)] * 4
    n_in = len(ins)

    def body(*refs):
        for t, (ip, off, iw) in enumerate(where):
            wr, mr, vr = refs[iw:iw + 3]
            parts = refs[ip][...] if off is None else refs[ip][:, :, off:off + wr.shape[-1]]
            res = _adam_math(_sum_parts(parts), wr[...], mr[...], vr[...])
            for o, val in zip(refs[n_in + 4 * t:n_in + 4 * t + 4], res):
                o[...] = val

    res = pl.pallas_call(body, name=name, grid=(nblk,), in_specs=in_specs, out_specs=out_specs, out_shape=out_shape,
                         compiler_params=_cparams())(*ins)
    return [tuple(res[4 * t:4 * t + 4]) for t in range(n)]


def _sum_multi(name, parts_list):
    def body(*refs):
        for pr, o in zip(refs[:len(parts_list)], refs[len(parts_list):]):
            o[...] = _sum_parts(pr[...])

    return pl.pallas_call(body, name=name, out_shape=[jax.ShapeDtypeStruct(p.shape[1:], F32) for p in parts_list],
                          compiler_params=_cparams())(*parts_list)


def _adamw_sum(name, parts, wv, mv, vv):
    npart, r, c = parts.shape
    tr = r
    if r * c > ADAM_BLOCK_ELEMS and r % SUBLANES == 0:
        tr = SUBLANES
        while r % (tr * 2) == 0 and tr * 2 * c <= ADAM_BLOCK_ELEMS:
            tr *= 2

    def fn(pb, wb_, mb, vb):
        return _adam_math(_sum_parts(pb), wb_, mb, vb)

    row = pl.BlockSpec((tr, c), lambda i: (i, 0))
    return _blockwise(name, fn, [parts, wv, mv, vv],
                      [pl.BlockSpec((npart, tr, c), lambda i: (0, i, 0)), row, row, row],
                      [((r, c), F32)] * 4, [row] * 4, (r // tr,))


_VECTORS = ["attn_norm", "lam_re", "lam_im", "log_dt", "ssm_d", "b_glu", "q_norm", "kv_norm", "son", "mon",
            "ffn_norm", "conv_b", "final_norm"]
_GHP = ["c_re", "c_im", "bt_re", "bt_im"]
_PACKED = ["attn_norm", "ssm_d", "b_glu", "q_norm", "kv_norm", "son", "mon", "ffn_norm", "conv_b", "final_norm"]
_BIG = ["win", "wglu", "wuq", "wukv", "wout", "wup", "wdown", "conv_w"]
_ROWS_IN_LANES = ("win", "wuq")
_TWO_LEVEL = ("wup", "win")
_AFTER = "_pair_sums_after"
_ORDER = ["attn_norm", "win", "lam_re", "lam_im", "log_dt", "b_re", "b_im", "c_re", "c_im", "ssm_d", "wglu",
          "b_glu", "q_norm", "wuq", "kv_norm", "wukv", "son", "mon", "wout", "ffn_norm", "wup", "conv_w",
          "conv_b", "wdown", "final_norm"]


def kernel(x, positions, attn_norm_w, w_in, ssm_lambda_re, ssm_lambda_im, ssm_log_dt, ssm_b_re, ssm_b_im, ssm_c_re, ssm_c_im, ssm_d, ssm_w_glu, ssm_b_glu, mla_q_norm_w, mla_w_uq, mla_kv_norm_w, mla_w_ukv, ssm_out_norm_w, mla_out_norm_w, w_out, ffn_norm_w, ffn_w_up, ffn_conv_w, ffn_conv_b, ffn_w_down, final_norm_w, loss_target, m_attn_norm_w, m_w_in, m_ssm_lambda_re, m_ssm_lambda_im, m_ssm_log_dt, m_ssm_b_re, m_ssm_b_im, m_ssm_c_re, m_ssm_c_im, m_ssm_d, m_ssm_w_glu, m_ssm_b_glu, m_mla_q_norm_w, m_mla_w_uq, m_mla_kv_norm_w, m_mla_w_ukv, m_ssm_out_norm_w, m_mla_out_norm_w, m_w_out, m_ffn_norm_w, m_ffn_w_up, m_ffn_conv_w, m_ffn_conv_b, m_ffn_w_down, m_final_norm_w, v_attn_norm_w, v_w_in, v_ssm_lambda_re, v_ssm_lambda_im, v_ssm_log_dt, v_ssm_b_re, v_ssm_b_im, v_ssm_c_re, v_ssm_c_im, v_ssm_d, v_ssm_w_glu, v_ssm_b_glu, v_mla_q_norm_w, v_mla_w_uq, v_mla_kv_norm_w, v_mla_w_ukv, v_ssm_out_norm_w, v_mla_out_norm_w, v_w_out, v_ffn_norm_w, v_ffn_w_up, v_ffn_conv_w, v_ffn_conv_b, v_ffn_w_down, v_final_norm_w):
    wts = dict(attn_norm=attn_norm_w, win=w_in, lam_re=ssm_lambda_re, lam_im=ssm_lambda_im, log_dt=ssm_log_dt,
               b_re=ssm_b_re, b_im=ssm_b_im, c_re=ssm_c_re, c_im=ssm_c_im, ssm_d=ssm_d, wglu=ssm_w_glu,
               b_glu=ssm_b_glu, q_norm=mla_q_norm_w, wuq=mla_w_uq, kv_norm=mla_kv_norm_w, wukv=mla_w_ukv,
               son=ssm_out_norm_w, mon=mla_out_norm_w, wout=w_out, ffn_norm=ffn_norm_w, wup=ffn_w_up,
               conv_w=ffn_conv_w, conv_b=ffn_conv_b, wdown=ffn_w_down, final_norm=final_norm_w)
    moms = dict(zip(_ORDER, [m_attn_norm_w, m_w_in, m_ssm_lambda_re, m_ssm_lambda_im, m_ssm_log_dt, m_ssm_b_re,
                             m_ssm_b_im, m_ssm_c_re, m_ssm_c_im, m_ssm_d, m_ssm_w_glu, m_ssm_b_glu, m_mla_q_norm_w,
                             m_mla_w_uq, m_mla_kv_norm_w, m_mla_w_ukv, m_ssm_out_norm_w, m_mla_out_norm_w, m_w_out,
                             m_ffn_norm_w, m_ffn_w_up, m_ffn_conv_w, m_ffn_conv_b, m_ffn_w_down, m_final_norm_w]))
    vels = dict(zip(_ORDER, [v_attn_norm_w, v_w_in, v_ssm_lambda_re, v_ssm_lambda_im, v_ssm_log_dt, v_ssm_b_re,
                             v_ssm_b_im, v_ssm_c_re, v_ssm_c_im, v_ssm_d, v_ssm_w_glu, v_ssm_b_glu, v_mla_q_norm_w,
                             v_mla_w_uq, v_mla_kv_norm_w, v_mla_w_ukv, v_ssm_out_norm_w, v_mla_out_norm_w, v_w_out,
                             v_ffn_norm_w, v_ffn_w_up, v_ffn_conv_w, v_ffn_conv_b, v_ffn_w_down, v_final_norm_w]))
    seq, d = x.shape[1], x.shape[2]
    in_width = w_in.shape[2]
    in_pad = -(-in_width // LANES) * LANES
    q_cols = mla_w_uq.shape[2]
    q_pad = 2 * LANES

    (win_g,) = _all_gather("gather_w_in", [jnp.pad(w_in[0], ((0, 0), (0, in_pad - in_width))).astype(BF16)])
    wglu_g, wuq_g, wukv_g, wout_g, convw_g = _all_gather(
        "gather_mix", [ssm_w_glu[0].astype(BF16), jnp.pad(mla_w_uq[0], ((0, 0), (0, q_pad - q_cols))).astype(BF16),
                       mla_w_ukv[0].astype(BF16), w_out[0].astype(BF16), ffn_conv_w[0]], collective_id=0)
    (wup_g,) = _all_gather("gather_ffn_up", [ffn_w_up[0].astype(BF16)], collective_id=1)
    (wdown_g,) = _all_gather("gather_ffn_down", [ffn_w_down[0].astype(BF16)], collective_id=2)
    ns = N_DEV
    c_ff = wup_g.shape[2]
    w = dict(
        attn_norm=attn_norm_w, win=win_g.reshape(d, in_pad), lam_re=ssm_lambda_re, lam_im=ssm_lambda_im,
        log_dt=ssm_log_dt, b_re=ssm_b_re, b_im=ssm_b_im, c_re=ssm_c_re, c_im=ssm_c_im, ssm_d=ssm_d,
        wglu=wglu_g.reshape(d // 2, d // 2), b_glu=ssm_b_glu, q_norm=mla_q_norm_w, wuq=wuq_g,
        kv_norm=mla_kv_norm_w, wukv=wukv_g, son=ssm_out_norm_w, mon=mla_out_norm_w, wout=wout_g.reshape(d, d),
        ffn_norm=ffn_norm_w, wup=wup_g, conv_w=convw_g, conv_b=ffn_conv_b,
        wdown=wdown_g.reshape(ns // 2 * c_ff, d), final_norm=final_norm_w)

    shard_layout = dict(
        win=lambda a: a[:, :in_width].reshape(N_DEV, d // N_DEV, in_width),
        wglu=lambda a: a.reshape(N_DEV, d // 2 // N_DEV, d // 2),
        wuq=lambda a: a[:, :, :q_cols], wukv=lambda a: a, wout=lambda a: a.reshape(N_DEV, d // N_DEV, d),
        wup=lambda a: a, wdown=lambda a: a.reshape(N_DEV, c_ff // 2, d), conv_w=lambda a: a)
    recv = {}
    next_id = [3]

    last = [None]

    out = {}

    def update(k):
        shp = wts[k].shape
        r, c = shp[-2], shp[-1]
        if k in _ROWS_IN_LANES:
            t = lambda a: jnp.swapaxes(a.reshape(-1, r, c), 1, 2)
            res = _adamw_sum("adamw_" + k, t(recv[k]), t(wts[k])[0], t(moms[k])[0], t(vels[k])[0])
            out[k] = [jnp.swapaxes(a, 0, 1).reshape(shp) for a in res]
            return res[0]
        res = _adamw_sum("adamw_" + k, recv[k].reshape(-1, r, c), wts[k].reshape(r, c),
                         moms[k].reshape(r, c), vels[k].reshape(r, c))
        out[k] = [a.reshape(shp) for a in res]
        return res[0]

    pending = {}

    def exchange(not_before=(), **grads):
        names = list(grads)
        if len(names) == 1 and names[0] in _TWO_LEVEL:
            k = names[0]
            parts = shard_layout[k](grads[k])
            got = _pair_swap("swap_" + k, [parts], collective_id=next_id[0], after=[last[0]])[0]
            next_id[0] += 1
            pending[k] = (parts, got)
            last[0] = got
            return
        if len(names) == 1 and names[0].endswith(_AFTER):
            k = names[0][:-len(_AFTER)]
            sums = _pair_add("pair_add_" + k, *pending[k])
            if k == "win":
                pending["tail"] = sums
                return
            recv[k] = _chip_exchange("exchange_" + k, [sums], collective_id=next_id[0],
                                     after=[last[0], grads[names[0]]])[0]
            next_id[0] += 1
            last[0] = recv[k]
            return
        got = _exchange_partials("exchange_" + "_".join(names), [shard_layout[k](grads[k]) for k in names],
                                 collective_id=next_id[0], after=[a for a in (last[0], *not_before) if a is not None])
        next_id[0] += 1
        last[0] = got[-1]
        recv.update(zip(names, got))

    loss_part, grad_x, g = _local_step(x[0], positions[0], loss_target[0], w, emit=exchange)
    n_groups = ssm_lambda_re.shape[1]
    two_d = {"lam_re": (n_groups, -1), "lam_im": (n_groups, -1)}
    dense = {k: g[k].reshape(two_d.get(k, (1, -1))) for k in _VECTORS}
    offsets, width = {}, 0
    for k in _PACKED:
        offsets[k] = width
        width += dense[k].shape[1]
    sent = dict(packed=jnp.concatenate([dense[k] for k in _PACKED], axis=1),
                **{k: dense[k] for k in _VECTORS if k not in _PACKED},
                **{k: g[k].reshape(n_groups, -1).astype(BF16) for k in _GHP},
                loss=loss_part)
    names = list(sent)
    got = _all_gather("gather_small_grads", [sent[k] for k in names], collective_id=next_id[0], after=[last[0]],
                      pair_sums=[pending["tail"]])
    gathered = dict(zip(names, got))
    recv["win"] = got[len(names)]
    for k in _BIG:
        if k not in out and k != "win":
            update(k)
    update("win")

    def finish(keys, results):
        for k, res in zip(keys, results):
            out[k] = [a.reshape(wts[k].shape) for a in res]

    view = lambda k, a: a.reshape(dense[k].shape)
    finish(_VECTORS, _adamw_multi("adamw_vectors", [(offsets.get(k, gathered.get(k)), view(k, wts[k]), view(k, moms[k]),
                                                     view(k, vels[k])) for k in _VECTORS], packed=gathered["packed"]))
    summed = _GHP + ["loss"]
    sums = dict(zip(summed, _sum_multi("sum_ssm_bc_loss", [gathered[k] for k in summed])))
    loss = sums["loss"][0, 0]
    ghp = lambda k: sums[k].reshape(g[k].shape)
    t_hp = lambda a: jnp.swapaxes(a, 2, 3)
    bc_keys = ["c_re", "c_im", "b_re", "b_im"]
    items = [(ghp(k)[None, None], wts[k], moms[k], vels[k]) for k in bc_keys[:2]]
    items += [(ghp(t)[None, None], t_hp(wts[k]), t_hp(moms[k]), t_hp(vels[k]))
              for k, t in zip(bc_keys[2:], ("bt_re", "bt_im"))]
    res = _adamw_multi("adamw_ssm_bc", items)
    finish(bc_keys, res[:2] + [tuple(t_hp(a) for a in r) for r in res[2:]])

    grad_x = grad_x.reshape(x.shape)
    return (loss, grad_x, *[out[k][0] for k in _ORDER], *[out[k][1] for k in _ORDER],
            *[out[k][2] for k in _ORDER], *[out[k][3] for k in _ORDER])
```

```python
import functools

import jax
import jax.numpy as jnp
from jax import lax
from jax.experimental import pallas as pl
from jax.experimental.pallas import tpu as pltpu
from jax.experimental.pallas import tpu_sc as plsc

F32 = jnp.float32
BF16 = jnp.bfloat16
MESH = pl.DeviceIdType.MESH

N_DEV = 8
LANES = 128
SUBLANES = 8
VMEM_LIMIT = 48 * 1024 * 1024

SSM_GROUP = 16
SSM_STATE = 64
GROUPS_PER_BLOCK = LANES // SSM_GROUP
STATE_BLOCK = GROUPS_PER_BLOCK * SSM_STATE
QK_NOPE = 128
QK_ROPE = 64
V_DIM = 128
ROPE_THETA = 10000.0
RMS_EPS = 1e-6

ADAM_LR = 0.001
ADAM_B1 = 0.9
ADAM_B2 = 0.999
ADAM_EPS = 1e-08
ADAM_WD = 0.01
ADAM_STEP = 10

NN = ((1,), (0,))
NT = ((1,), (1,))
TN = ((0,), (0,))


def _cparams():
    return pltpu.CompilerParams(vmem_limit_bytes=VMEM_LIMIT)


def _tile(n, want):
    if n <= want:
        return n
    t = (want // LANES) * LANES
    while t >= LANES:
        if n % t == 0:
            return t
        t -= LANES
    return n


def _mm(name, a, b, *, grid, a_spec, b_spec, o_spec, out_shape, out_dtype, contract=NN,
        res=None, res_spec=None):
    nk = grid[-1]
    kaxis = len(grid) - 1
    acc_shape = tuple(d for d in o_spec.block_shape if d is not None)

    def body(*refs):
        a_ref, b_ref = refs[:2]
        r_ref = None if res is None else refs[2]
        o_ref = refs[2 if res is None else 3]
        part = lax.dot_general(a_ref[...].astype(BF16), b_ref[...].astype(BF16),
                               (contract, ((), ())), preferred_element_type=F32)
        if nk == 1:
            if r_ref is not None:
                part = part + r_ref[...].astype(F32)
            o_ref[...] = part.astype(o_ref.dtype)
            return
        acc = refs[-1]
        k = pl.program_id(kaxis)

        @pl.when(k == 0)
        def _():
            acc[...] = part

        @pl.when(k != 0)
        def _():
            acc[...] += part

        @pl.when(k == nk - 1)
        def _():
            r = acc[...]
            if r_ref is not None:
                r = r + r_ref[...].astype(F32)
            o_ref[...] = r.astype(o_ref.dtype)

    ins = [a, b] + ([] if res is None else [res])
    in_specs = [a_spec, b_spec] + ([] if res is None else [res_spec])
    return pl.pallas_call(
        body, name=name, grid=grid, in_specs=in_specs, out_specs=o_spec,
        out_shape=jax.ShapeDtypeStruct(out_shape, out_dtype),
        scratch_shapes=[pltpu.VMEM(acc_shape, F32)] if nk > 1 else [], compiler_params=_cparams(),
    )(*ins)


def _mm2d(name, a, b, contract, out_dtype, tm=1024, tn=1024, tk=2048, res=None):
    if contract == NN:
        (m, kk), n = a.shape, b.shape[1]
    elif contract == NT:
        (m, kk), n = a.shape, b.shape[0]
    else:
        (kk, m), n = a.shape, b.shape[1]
    tm, tn, tk = _tile(m, tm), _tile(n, tn), _tile(kk, tk)
    grid = (m // tm, n // tn, kk // tk)
    if contract == TN:
        a_spec = pl.BlockSpec((tk, tm), lambda i, j, k: (k, i))
    else:
        a_spec = pl.BlockSpec((tm, tk), lambda i, j, k: (i, k))
    if contract == NT:
        b_spec = pl.BlockSpec((tn, tk), lambda i, j, k: (j, k))
    else:
        b_spec = pl.BlockSpec((tk, tn), lambda i, j, k: (k, j))
    o_spec = pl.BlockSpec((tm, tn), lambda i, j, k: (i, j))
    res_spec = None
    if res is not None:
        if res.shape[0] == 1:
            res_spec = pl.BlockSpec((1, tn), lambda i, j, k: (0, j))
        else:
            res_spec = pl.BlockSpec((tm, tn), lambda i, j, k: (i, j))
    return _mm(name, a, b, grid=grid, a_spec=a_spec, b_spec=b_spec, o_spec=o_spec,
               out_shape=(m, n), out_dtype=out_dtype, contract=contract, res=res, res_spec=res_spec)


def _blockwise(name, fn, ins, in_specs, outs, out_specs, grid, n_acc=0, acc_all=True):
    n_in, n_out = len(ins), len(outs)
    n_plain = n_out - n_acc

    def body(*refs):
        vals = fn(*[r[...] for r in refs[:n_in]])
        if not isinstance(vals, (tuple, list)):
            vals = (vals,)
        o_refs = refs[n_in:n_in + n_out]
        for r, v in zip(o_refs[:n_plain], vals[:n_plain]):
            r[...] = v.astype(r.dtype)
        if n_acc:
            if acc_all:
                first = functools.reduce(jnp.logical_and, [pl.program_id(d) == 0 for d in range(len(grid))])
            else:
                first = pl.program_id(len(grid) - 1) == 0

            @pl.when(first)
            def _():
                for r, v in zip(o_refs[n_plain:], vals[n_plain:]):
                    r[...] = v.astype(r.dtype)

            @pl.when(jnp.logical_not(first))
            def _():
                for r, v in zip(o_refs[n_plain:], vals[n_plain:]):
                    r[...] += v.astype(r.dtype)

    return pl.pallas_call(
        body, name=name, grid=grid, in_specs=in_specs, out_specs=out_specs,
        out_shape=[jax.ShapeDtypeStruct(s, d) for s, d in outs], compiler_params=_cparams(),
    )(*ins)


def _row_spec(t, c):
    return pl.BlockSpec((t, c), lambda i: (i, 0))


def _full_spec(shape, single=False):
    nd = len(shape)
    if single:
        return pl.BlockSpec(tuple(shape), lambda *g: (0,) * nd, pipeline_mode=pl.Buffered(1))
    return pl.BlockSpec(tuple(shape), lambda *g: (0,) * nd)


def _rms(xf, w):
    return xf * lax.rsqrt(jnp.mean(xf * xf, axis=-1, keepdims=True) + RMS_EPS) * w


def _rms_bwd(xf, w, dy):
    _, vjp = jax.vjp(_rms, xf, w)
    return vjp(dy)


def _s5_disc(lr, li, ldt, bre, bim):
    dt = jnp.exp(ldt)
    mag = jnp.exp(lr * dt)
    ar = mag * jnp.cos(li * dt)
    ai = mag * jnp.sin(li * dt)
    nr, ni = ar - 1.0, ai
    den = lr * lr + li * li
    zr = (nr * lr + ni * li) / den
    zi = (ni * lr - nr * li) / den
    return ar, ai, zr * bre - zi * bim, zr * bim + zi * bre


def _s5_prep(lr, li, ldt, bre, bim):
    def body(lr_r, li_r, ldt_r, bre_r, bim_r, ar_r, ai_r, br_r, bi_r):
        ar, ai, br, bi = _s5_disc(lr_r[...], li_r[...], ldt_r[...], bre_r[...], bim_r[...])
        ar_r[...] = ar
        ai_r[...] = ai
        br_r[...] = br
        bi_r[...] = bi

    sd = jax.ShapeDtypeStruct
    return pl.pallas_call(
        body, name="s5_prep",
        out_shape=[sd(lr.shape, F32), sd(lr.shape, F32), sd(bre.shape, F32), sd(bre.shape, F32)],
        compiler_params=_cparams(),
    )(lr, li, ldt, bre, bim)


def _s5_prep_bwd(lr, li, ldt, bre, bim, dar, dai, dbr, dbi):
    def body(lr_r, li_r, ldt_r, bre_r, bim_r, dar_r, dai_r, dbr_r, dbi_r, o0, o1, o2, o3, o4):
        _, vjp = jax.vjp(_s5_disc, lr_r[...], li_r[...], ldt_r[...], bre_r[...], bim_r[...])
        g = vjp((dar_r[...], dai_r[...], dbr_r[...], dbi_r[...]))
        for o, v in zip((o0, o1, o2, o3, o4), g):
            o[...] = v

    sd = jax.ShapeDtypeStruct
    return pl.pallas_call(
        body, name="s5_prep_bwd",
        out_shape=[sd(lr.shape, F32), sd(li.shape, F32), sd(ldt.shape, F32), sd(bre.shape, F32), sd(bim.shape, F32)],
        compiler_params=_cparams(),
    )(lr, li, ldt, bre, bim, dar, dai, dbr, dbi)


SCAN_T = 256


def _scan_tables(ar, ai, tab_r, tab_i, sub, reverse):
    pr, pi = ar, ai
    for k in range(sub):
        row = sub - 1 - k if reverse else k
        tab_r[row:row + 1, :] = pr
        tab_i[row:row + 1, :] = pi
        pr, pi = ar * pr - ai * pi, ar * pi + ai * pr


def _pack_matrix(t_blk, dtype):
    sub = t_blk // SUBLANES
    dst = jnp.arange(t_blk)
    src = (dst % SUBLANES) * sub + dst // SUBLANES
    return (src[:, None] == jnp.arange(t_blk)[None, :]).astype(dtype)


def _permute_rows_f32(pm, x):
    hi = x.astype(BF16)
    r1 = x - hi.astype(F32)
    mid = r1.astype(BF16)
    lo = (r1 - mid.astype(F32)).astype(BF16)
    dot = lambda v: jnp.dot(pm, v, preferred_element_type=F32)
    return dot(hi) + dot(mid) + dot(lo)


def _scan_block(x, loc, ar, ai, st, tab_r, tab_i, sub, reverse):
    hb = STATE_BLOCK
    a8r = jnp.broadcast_to(ar, (SUBLANES, hb))
    a8i = jnp.broadcast_to(ai, (SUBLANES, hb))
    sr = jnp.zeros((SUBLANES, hb), F32)
    si = jnp.zeros((SUBLANES, hb), F32)
    steps = range(sub - 1, -1, -1) if reverse else range(sub)
    for t in steps:
        rows = slice(t * SUBLANES, (t + 1) * SUBLANES)
        sr, si = a8r * sr - a8i * si + x[rows, :hb], a8r * si + a8i * sr + x[rows, hb:]
        loc[rows, :hb] = sr
        loc[rows, hb:] = si
    cr, ci = st[0:1, :], st[1:2, :]
    far = 0 if reverse else sub - 1
    fr, fi = tab_r[far:far + 1, :], tab_i[far:far + 1, :]
    ent_r, ent_i = [None] * SUBLANES, [None] * SUBLANES
    for c in (range(SUBLANES - 1, -1, -1) if reverse else range(SUBLANES)):
        ent_r[c], ent_i[c] = cr, ci
        cr, ci = sr[c:c + 1, :] + (fr * cr - fi * ci), si[c:c + 1, :] + (fr * ci + fi * cr)
    st[0:1, :] = cr
    st[1:2, :] = ci
    c8r = jnp.concatenate(ent_r, axis=0)
    c8i = jnp.concatenate(ent_i, axis=0)
    out = []
    for t in range(sub):
        rows = slice(t * SUBLANES, (t + 1) * SUBLANES)
        tr, ti = tab_r[t:t + 1, :], tab_i[t:t + 1, :]
        out.append(jnp.concatenate([loc[rows, :hb] + (tr * c8r - ti * c8i), loc[rows, hb:] + (tr * c8i + ti * c8r)],
                                   axis=1))
    return jnp.concatenate(out, axis=0)


SSM_BLOCKS_PER_STEP = 2


def _scan_scratch(nblk, t_blk, sub, hb):
    return [pltpu.VMEM((nblk, SUBLANES, hb), F32), pltpu.VMEM((nblk, sub, hb), F32), pltpu.VMEM((nblk, sub, hb), F32),
            pltpu.VMEM((nblk, t_blk, 2 * hb), F32)]


def _ssm_fwd(proj, wb, wc, a):
    seq = proj.shape[0]
    nj = wb.shape[0]
    w2 = 2 * STATE_BLOCK
    hb = STATE_BLOCK
    t_blk = min(SCAN_T, seq)
    sub = t_blk // SUBLANES
    pm = _pack_matrix(t_blk, BF16)

    npair = SSM_BLOCKS_PER_STEP

    def body(u_ref, wb_ref, wc_ref, a_ref, pm_ref, pmt_ref, s_ref, y_ref, st, tab_r, tab_i, loc):
        coef = [(a_ref[:, b * w2:b * w2 + hb], a_ref[:, b * w2 + hb:(b + 1) * w2]) for b in range(npair)]

        @pl.when(pl.program_id(1) == 0)
        def _():
            for b, (ar, ai) in enumerate(coef):
                st[b] = jnp.zeros((SUBLANES, hb), F32)
                _scan_tables(ar, ai, tab_r.at[b], tab_i.at[b], sub, False)

        for b, (ar, ai) in enumerate(coef):
            ub = u_ref[:, b * LANES:(b + 1) * LANES].astype(BF16)
            up = jnp.dot(pm_ref[...], ub, preferred_element_type=F32).astype(BF16)
            bu = jnp.dot(up, wb_ref[b], preferred_element_type=F32)
            s = _scan_block(bu, loc.at[b], ar, ai, st.at[b], tab_r.at[b], tab_i.at[b], sub, False)
            s_ref[:, b * w2:(b + 1) * w2] = s
            yp = jnp.dot(s.astype(BF16), wc_ref[b], preferred_element_type=F32)
            y_ref[:, b * LANES:(b + 1) * LANES] = _permute_rows_f32(pmt_ref[...], yp)

    sd = jax.ShapeDtypeStruct
    return pl.pallas_call(
        body, name="ssm_fwd", grid=(nj // npair, seq // t_blk),
        in_specs=[pl.BlockSpec((t_blk, npair * LANES), lambda j, i: (i, j)),
                  pl.BlockSpec((npair, LANES, w2), lambda j, i: (j, 0, 0)),
                  pl.BlockSpec((npair, w2, LANES), lambda j, i: (j, 0, 0)),
                  pl.BlockSpec((1, npair * w2), lambda j, i: (0, j)),
                  _full_spec((t_blk, t_blk)), _full_spec((t_blk, t_blk))],
        out_specs=[pl.BlockSpec((t_blk, npair * w2), lambda j, i: (i, j)),
                   pl.BlockSpec((t_blk, npair * LANES), lambda j, i: (i, j))],
        out_shape=[sd((seq, nj * w2), F32), sd((seq, nj * LANES), F32)],
        scratch_shapes=_scan_scratch(npair, t_blk, sub, hb), compiler_params=_cparams(),
    )(proj, wb, wc, a, pm, pm.T)


def _ssm_bwd(dy, s, proj, du1, wb, wc, a):
    seq = dy.shape[0]
    nj = wb.shape[0]
    w2 = 2 * STATE_BLOCK
    hb = STATE_BLOCK
    t_blk = min(SCAN_T, seq)
    sub = t_blk // SUBLANES
    nb = seq // t_blk
    pm = _pack_matrix(t_blk, BF16)

    npair = SSM_BLOCKS_PER_STEP

    def body(dy_ref, s_ref, sprev_ref, u_ref, du1_ref, wb_ref, wc_ref, a_ref, pm_ref, pmt_ref,
             du_ref, dwb_ref, dwc_ref, da_ref, st, tab_r, tab_i, loc):
        ib = pl.program_id(1)
        pmv = pm_ref[...]
        coef = [(a_ref[:, b * w2:b * w2 + hb], -a_ref[:, b * w2 + hb:(b + 1) * w2]) for b in range(npair)]

        @pl.when(ib == 0)
        def _():
            for b, (ar, ai) in enumerate(coef):
                st[b] = jnp.zeros((SUBLANES, hb), F32)
                _scan_tables(ar, ai, tab_r.at[b], tab_i.at[b], sub, True)

        sums = []
        for b, (ar, ai) in enumerate(coef):
            cols, wide = slice(b * LANES, (b + 1) * LANES), slice(b * w2, (b + 1) * w2)
            dyp = jnp.dot(pmv, dy_ref[:, cols], preferred_element_type=F32).astype(BF16)
            up = jnp.dot(pmv, u_ref[:, cols].astype(BF16), preferred_element_type=F32).astype(BF16)
            ds = lax.dot_general(dyp, wc_ref[b], (NT, ((), ())), preferred_element_type=F32)
            lam = _scan_block(ds, loc.at[b], ar, ai, st.at[b], tab_r.at[b], tab_i.at[b], sub, True)
            lamb = lam.astype(BF16)
            du = lax.dot_general(lamb, wb_ref[b], (NT, ((), ())), preferred_element_type=F32)
            du_ref[:, cols] = (_permute_rows_f32(pmt_ref[...], du) + du1_ref[:, cols]).astype(du_ref.dtype)
            sv = s_ref[:, wide]
            dwb = lax.dot_general(up, lamb, (TN, ((), ())), preferred_element_type=F32)
            dwc = lax.dot_general(sv.astype(BF16), dyp, (TN, ((), ())), preferred_element_type=F32)

            prev_last = sprev_ref[SUBLANES - 1:SUBLANES, wide]
            prev_last = jnp.where(ib == nb - 1, jnp.zeros_like(prev_last), prev_last)
            tail = sv[t_blk - SUBLANES:, :]
            sl = lax.broadcasted_iota(jnp.int32, tail.shape, 0)
            head = jnp.where(sl >= 1, pltpu.roll(tail, 1, 0), prev_last)
            s_sh = jnp.concatenate([head, sv[:t_blk - SUBLANES, :]], axis=0)
            lam_r, lam_i = lam[:, :hb], lam[:, hb:]
            sr_, si_ = s_sh[:, :hb], s_sh[:, hb:]
            dar = jnp.sum(lam_r * sr_ + lam_i * si_, axis=0, keepdims=True)
            dai = jnp.sum(lam_i * sr_ - lam_r * si_, axis=0, keepdims=True)
            sums.append((wide, jnp.concatenate([dar, dai], axis=1), dwb, dwc))

        @pl.when(ib == 0)
        def _():
            for b, (wide, contrib, dwb, dwc) in enumerate(sums):
                da_ref[:, wide] = contrib
                dwb_ref[b] = dwb
                dwc_ref[b] = dwc

        @pl.when(ib != 0)
        def _():
            for b, (wide, contrib, dwb, dwc) in enumerate(sums):
                da_ref[:, wide] += contrib
                dwb_ref[b] += dwb
                dwc_ref[b] += dwc

    blk = lambda j, i: (nb - 1 - i, j)
    prev_blk = lambda j, i: (jnp.maximum((nb - 1 - i) * sub - 1, 0), j)
    sd = jax.ShapeDtypeStruct
    return pl.pallas_call(
        body, name="ssm_bwd", grid=(nj // npair, nb),
        in_specs=[pl.BlockSpec((t_blk, npair * LANES), blk), pl.BlockSpec((t_blk, npair * w2), blk),
                  pl.BlockSpec((SUBLANES, npair * w2), prev_blk), pl.BlockSpec((t_blk, npair * LANES), blk),
                  pl.BlockSpec((t_blk, npair * LANES), blk),
                  pl.BlockSpec((npair, LANES, w2), lambda j, i: (j, 0, 0)),
                  pl.BlockSpec((npair, w2, LANES), lambda j, i: (j, 0, 0)),
                  pl.BlockSpec((1, npair * w2), lambda j, i: (0, j)),
                  _full_spec((t_blk, t_blk)), _full_spec((t_blk, t_blk))],
        out_specs=[pl.BlockSpec((t_blk, npair * LANES), blk),
                   pl.BlockSpec((npair, LANES, w2), lambda j, i: (j, 0, 0)),
                   pl.BlockSpec((npair, w2, LANES), lambda j, i: (j, 0, 0)),
                   pl.BlockSpec((1, npair * w2), lambda j, i: (0, j))],
        out_shape=[sd((seq, nj * LANES), BF16), sd((nj, LANES, w2), F32), sd((nj, w2, LANES), F32),
                   sd((1, nj * w2), F32)],
        scratch_shapes=_scan_scratch(npair, t_blk, sub, hb), compiler_params=_cparams(),
    )(dy, s, s, proj, du1, wb, wc, a, pm, pm.T)


def _rope128(x, cos, sa, sb):
    return x * cos + pltpu.roll(x, 96, 1) * sa + pltpu.roll(x, 32, 1) * sb


def _rope128_t(dy, cos, sa, sb):
    return dy * cos + pltpu.roll(dy * sa, 32, 1) + pltpu.roll(dy * sb, 96, 1)


ATT_BQ = 512


def _probs(qn, qp, kn, kp, r0, scale):
    s = lax.dot_general(qn, kn, (NT, ((), ())), preferred_element_type=F32)
    s = s + lax.dot_general(qp, kp, (NT, ((), ())), preferred_element_type=F32)
    s = s * scale
    diag = s[:, r0:]
    row = lax.broadcasted_iota(jnp.int32, diag.shape, 0)
    col = lax.broadcasted_iota(jnp.int32, diag.shape, 1)
    diag = jnp.where(col <= row, diag, jnp.finfo(F32).min)
    s = diag if r0 == 0 else jnp.concatenate([s[:, :r0], diag], axis=1)
    m = jnp.max(s, axis=-1, keepdims=True)
    e = jnp.exp(s - m)
    return e / jnp.sum(e, axis=-1, keepdims=True)


def _attn_specs(seq):
    tab = pl.BlockSpec((seq, LANES), lambda h: (0, 0))
    return [pl.BlockSpec((None, seq, 256), lambda h: (h, 0, 0)), pl.BlockSpec((None, seq, 128), lambda h: (h, 0, 0)),
            pl.BlockSpec((None, seq, 128), lambda h: (h, 0, 1)), tab, tab, tab, tab]


def _attn_fwd(q_raw, kv, kpe, cos, sa, sb):
    nh, seq, _ = q_raw.shape
    bq = min(ATT_BQ, seq)
    scale = (QK_NOPE + QK_ROPE) ** -0.5

    def body(q_ref, kn_ref, v_ref, kp_ref, cos_ref, sa_ref, sb_ref, o_ref):
        for r0 in range(0, seq, bq):
            rows, kend = pl.ds(r0, bq), r0 + bq
            qn = q_ref[rows, :QK_NOPE].astype(BF16)
            qp = _rope128(q_ref[rows, QK_NOPE:], cos_ref[rows, :], sa_ref[rows, :], sb_ref[rows, :]).astype(BF16)
            p = _probs(qn, qp, kn_ref[:kend, :], kp_ref[:kend, :], r0, scale)
            o_ref[rows, :] = jnp.dot(p.astype(BF16), v_ref[:kend, :], preferred_element_type=F32)

    return pl.pallas_call(
        body, name="attn_fwd", grid=(nh,), in_specs=_attn_specs(seq),
        out_specs=pl.BlockSpec((seq, V_DIM), lambda h: (0, h)),
        out_shape=jax.ShapeDtypeStruct((seq, nh * V_DIM), F32), compiler_params=_cparams(),
    )(q_raw, kv, kv, kpe, cos, sa, sb)


def _attn_bwd(q_raw, kv, kpe, cos, sa, sb, do):
    nh, seq, _ = q_raw.shape
    bq = min(ATT_BQ, seq)
    scale = (QK_NOPE + QK_ROPE) ** -0.5

    def body(q_ref, kn_ref, v_ref, kp_ref, cos_ref, sa_ref, sb_ref, do_ref, dq_ref, dkv_ref, dkp_ref):
        dkv_ref[...] = jnp.zeros_like(dkv_ref)
        dkp_ref[...] = jnp.zeros_like(dkp_ref)
        for r0 in range(0, seq, bq):
            rows, kend = pl.ds(r0, bq), r0 + bq
            cos_b, sa_b, sb_b = cos_ref[rows, :], sa_ref[rows, :], sb_ref[rows, :]
            qn = q_ref[rows, :QK_NOPE].astype(BF16)
            qp = _rope128(q_ref[rows, QK_NOPE:], cos_b, sa_b, sb_b).astype(BF16)
            kn, v, kp = kn_ref[:kend, :], v_ref[:kend, :], kp_ref[:kend, :]
            p = _probs(qn, qp, kn, kp, r0, scale)
            dob = do_ref[rows, :].astype(BF16)
            dp = lax.dot_general(dob, v, (NT, ((), ())), preferred_element_type=F32)
            ds = p * (dp - jnp.sum(p * dp, axis=-1, keepdims=True)) * scale
            dsb = ds.astype(BF16)
            pb = p.astype(BF16)
            dq_ref[rows, :QK_NOPE] = jnp.dot(dsb, kn, preferred_element_type=F32).astype(dq_ref.dtype)
            dqp = jnp.dot(dsb, kp, preferred_element_type=F32)
            dq_ref[rows, QK_NOPE:] = _rope128_t(dqp, cos_b, sa_b, sb_b).astype(dq_ref.dtype)
            dkv_ref[:kend, :QK_NOPE] += lax.dot_general(dsb, qn, (TN, ((), ())), preferred_element_type=F32)
            dkv_ref[:kend, QK_NOPE:] += lax.dot_general(pb, dob, (TN, ((), ())), preferred_element_type=F32)
            dkp_ref[:kend, :] += lax.dot_general(dsb, qp, (TN, ((), ())), preferred_element_type=F32)

    sd = jax.ShapeDtypeStruct
    return pl.pallas_call(
        body, name="attn_bwd", grid=(nh,),
        in_specs=_attn_specs(seq) + [pl.BlockSpec((seq, V_DIM), lambda h: (0, h))],
        out_specs=[pl.BlockSpec((None, seq, 256), lambda h: (h, 0, 0)),
                   pl.BlockSpec((None, seq, 256), lambda h: (h, 0, 0)),
                   pl.BlockSpec((None, seq, 128), lambda h: (h, 0, 0))],
        out_shape=[sd((nh, seq, 256), BF16), sd((nh, seq, 256), F32), sd((nh, seq, 128), F32)],
        compiler_params=_cparams(),
    )(q_raw, kv, kv, kpe, cos, sa, sb, do)


def _shift_rows(a, k):
    seq = a.shape[0]
    r = pltpu.roll(a, k % seq, 0)
    rows = lax.broadcasted_iota(jnp.int32, (SUBLANES, a.shape[1]), 0)
    if k > 0:
        return jnp.concatenate([jnp.where(rows >= k, r[:SUBLANES], 0.0), r[SUBLANES:]], axis=0)
    return jnp.concatenate([r[:seq - SUBLANES], jnp.where(rows < SUBLANES + k, r[seq - SUBLANES:], 0.0)], axis=0)


def _conv3(a, w, b):
    a1 = _shift_rows(a, 1)
    a2 = _shift_rows(a, 2)
    return w[2:3] * a + w[1:2] * a1 + w[0:1] * a2 + b, a1, a2


def _conv_gate_fwd(a, cw, cb):
    half, _, seq, c = a.shape
    nc = c // LANES

    def fn(pair, wg, wv, bg, bv):
        gc, _, _ = _conv3(pair[0], wg, bg)
        vc, _, _ = _conv3(pair[1], wv, bv)
        return gc * jax.nn.sigmoid(gc) * vc

    def w_spec(off, r):
        return pl.BlockSpec((None, r, LANES), lambda k, j: (k + off, 0, j))

    return _blockwise(
        "conv_gate_fwd", fn, [a, cw, cw, cb, cb],
        [pl.BlockSpec((None, 2, seq, LANES), lambda k, j: (k, 0, 0, j)),
         w_spec(0, 3), w_spec(half, 3), w_spec(0, 1), w_spec(half, 1)],
        [((seq, half * c), BF16)], [pl.BlockSpec((seq, LANES), lambda k, j: (0, k * nc + j))],
        grid=(half, nc))[0]


def _conv_gate_bwd(a, cw, cb, dm):
    half, _, seq, c = a.shape
    nc = c // LANES

    def body(a_ref, wg_ref, wv_ref, bg_ref, bv_ref, dm_ref, da_ref, dw_ref, db_ref):
        dmv = dm_ref[...]
        ga, wg = a_ref[0], wg_ref[...]
        va, wv = a_ref[1], wv_ref[...]
        gc, g1, g2 = _conv3(ga, wg, bg_ref[...])
        vc, v1, v2 = _conv3(va, wv, bv_ref[...])
        sg = jax.nn.sigmoid(gc)
        dms = dmv * sg
        d_val = dms * gc
        d_gate = dms * vc * (1.0 + gc * (1.0 - sg))

        def back(r, dc, own, a1, a2, w):
            up1 = _shift_rows(dc, -1)
            up2 = _shift_rows(dc, -2)
            da_ref[r] = (w[2:3] * dc + w[1:2] * up1 + w[0:1] * up2).astype(da_ref.dtype)
            dw_ref[r, 0:1, :] = jnp.sum(dc * a2, axis=0, keepdims=True)
            dw_ref[r, 1:2, :] = jnp.sum(dc * a1, axis=0, keepdims=True)
            dw_ref[r, 2:3, :] = jnp.sum(dc * own, axis=0, keepdims=True)
            db_ref[r] = jnp.sum(dc, axis=0, keepdims=True)

        back(0, d_gate, ga, g1, g2, wg)
        back(1, d_val, va, v1, v2, wv)

    def w_spec(off, r):
        return pl.BlockSpec((None, r, LANES), lambda k, j: (k + off, 0, j))

    def pair_spec(r):
        return pl.BlockSpec((None, 2, r, LANES), lambda k, j: (k, 0, 0, j))

    sd = jax.ShapeDtypeStruct
    return pl.pallas_call(
        body, name="conv_gate_bwd", grid=(half, nc),
        in_specs=[pair_spec(seq), w_spec(0, 3), w_spec(half, 3), w_spec(0, 1), w_spec(half, 1),
                  pl.BlockSpec((seq, LANES), lambda k, j: (0, k * nc + j))],
        out_specs=[pair_spec(seq), pair_spec(3), pair_spec(1)],
        out_shape=[sd((half, 2, seq, c), BF16), sd((half, 2, 3, c), F32), sd((half, 2, 1, c), F32)],
        compiler_params=_cparams(),
    )(a, cw, cw, cb, cb, dm)


ROW_T = 256


def _local_step(x, positions, target, w, emit=lambda **grads: None):
    seq, d = x.shape
    t_row = min(ROW_T, seq)
    nrow = seq // t_row
    ssm_w = d // 2
    nj = ssm_w // LANES
    n_groups = ssm_w // SSM_GROUP
    nh = w["wuq"].shape[0]
    q_rank = w["wuq"].shape[1]
    kv_rank = w["wukv"].shape[1]
    ns = w["wup"].shape[0]
    c_ff = w["wup"].shape[2]
    in_pad = w["win"].shape[1]
    sw = 2 * STATE_BLOCK
    g1 = (nrow,)

    lr3 = w["lam_re"].reshape(n_groups, 1, SSM_STATE)
    li3 = w["lam_im"].reshape(n_groups, 1, SSM_STATE)
    ldt3 = w["log_dt"].reshape(n_groups, 1, 1)
    bt_re = jnp.swapaxes(w["b_re"].reshape(n_groups, SSM_STATE, SSM_GROUP), 1, 2)
    bt_im = jnp.swapaxes(w["b_im"].reshape(n_groups, SSM_STATE, SSM_GROUP), 1, 2)
    abar_re, abar_im, bbt_re, bbt_im = _s5_prep(lr3, li3, ldt3, bt_re, bt_im)
    eye = jnp.eye(GROUPS_PER_BLOCK, dtype=F32)

    def blockdiag_in(bb):
        t = bb.reshape(nj, GROUPS_PER_BLOCK, SSM_GROUP, SSM_STATE)
        return jnp.einsum("jghp,gk->jghkp", t, eye).reshape(nj, LANES, STATE_BLOCK)

    def blockdiag_in_t(dwb):
        t = dwb.reshape(nj, GROUPS_PER_BLOCK, SSM_GROUP, GROUPS_PER_BLOCK, SSM_STATE)
        return jnp.einsum("jghkp,gk->jghp", t, eye).reshape(n_groups, SSM_GROUP, SSM_STATE)

    def blockdiag_out(cc):
        t = cc.reshape(nj, GROUPS_PER_BLOCK, SSM_GROUP, SSM_STATE)
        return jnp.einsum("jghp,gk->jkpgh", t, eye).reshape(nj, STATE_BLOCK, LANES)

    def blockdiag_out_t(dwc):
        t = dwc.reshape(nj, GROUPS_PER_BLOCK, SSM_STATE, GROUPS_PER_BLOCK, SSM_GROUP)
        return jnp.einsum("jkpgh,gk->jghp", t, eye).reshape(n_groups, SSM_GROUP, SSM_STATE)

    c_re = w["c_re"].reshape(n_groups, SSM_GROUP, SSM_STATE)
    c_im = w["c_im"].reshape(n_groups, SSM_GROUP, SSM_STATE)
    wb = jnp.concatenate([blockdiag_in(bbt_re), blockdiag_in(bbt_im)], axis=2).astype(BF16)
    wc = jnp.concatenate([blockdiag_out(c_re), -blockdiag_out(c_im)], axis=1).astype(BF16)
    a_lay = jnp.concatenate([abar_re.reshape(nj, 1, STATE_BLOCK), abar_im.reshape(nj, 1, STATE_BLOCK)],
                            axis=1).reshape(1, nj * sw)

    attn_w = w["attn_norm"]
    t_wide = min(2 * t_row, seq)
    g_wide = (seq // t_wide,)

    def proj_fn(xb, wv, wi):
        hb = _rms(xb, wv).astype(BF16)
        return hb, jnp.dot(hb, wi, preferred_element_type=F32)

    hn, proj = _blockwise(
        "norm1_proj", proj_fn, [x, attn_w, w["win"]],
        [_row_spec(t_wide, d), _full_spec((1, d)), _full_spec((d, in_pad), single=True)],
        [((seq, d), BF16), ((seq, in_pad), F32)], [_row_spec(t_wide, d), _row_spec(t_wide, in_pad)], g_wide)

    s_all, ylin = _ssm_fwd(proj, wb, wc, a_lay)

    def glu_fwd_fn(yl, ub, dsk, wg, bg):
        yp = yl + dsk * ub
        ygv = jax.nn.gelu(yp)
        ygb = ygv.astype(BF16)
        zb = jnp.dot(ygb, wg, preferred_element_type=F32) + bg
        return yp, ygb, zb, ygv * jax.nn.sigmoid(zb)

    wide = pl.BlockSpec((t_wide, ssm_w), lambda i: (i, 0))
    y_pre, yg, z, y_ssm = _blockwise(
        "ssm_glu_fwd", glu_fwd_fn, [ylin, proj, w["ssm_d"], w["wglu"], w["b_glu"]],
        [wide, wide, _full_spec((1, ssm_w)), _full_spec((ssm_w, ssm_w), single=True), _full_spec((1, ssm_w))],
        [((seq, ssm_w), F32), ((seq, ssm_w), BF16), ((seq, ssm_w), F32), ((seq, ssm_w), F32)], [wide] * 4, g_wide)

    cq_off, ckv_off, kpe_off = ssm_w, ssm_w + q_rank, ssm_w + q_rank + kv_rank
    assert cq_off % q_rank == 0 and ckv_off % kv_rank == 0 and kpe_off % LANES == 0
    cq_spec = pl.BlockSpec((t_row, q_rank), lambda i: (i, cq_off // q_rank))
    ckv_spec = pl.BlockSpec((t_row, kv_rank), lambda i: (i, ckv_off // kv_rank))
    kpe_spec = pl.BlockSpec((t_row, LANES), lambda i: (i, kpe_off // LANES))
    pos_b = jnp.broadcast_to(positions.astype(F32)[:, None], (seq, LANES))
    inv_freq = ROPE_THETA ** (-jnp.arange(0, QK_ROPE, 2, dtype=F32) / QK_ROPE)
    inv128 = jnp.tile(inv_freq, 4).reshape(1, LANES)

    def mla_prep_fn(cq, ckv, kp, pb, inv, wq, wkv):
        ang = pb * inv
        lane = lax.broadcasted_iota(jnp.int32, ang.shape, 1)
        cs, sn = jnp.cos(ang), jnp.sin(ang)
        cos = jnp.where(lane < QK_ROPE, cs, 0.0)
        sa = jnp.where(lane < QK_ROPE // 2, -sn, 0.0)
        sb = jnp.where(jnp.logical_and(lane >= QK_ROPE // 2, lane < QK_ROPE), sn, 0.0)
        return _rms(cq, wq), _rms(ckv, wkv), _rope128(kp, cos, sa, sb), cos, sa, sb

    qn, kvn, kpe, cos_t, sa_t, sb_t = _blockwise(
        "mla_prep", mla_prep_fn, [proj, proj, proj, pos_b, inv128, w["q_norm"], w["kv_norm"]],
        [cq_spec, ckv_spec, kpe_spec, _row_spec(t_row, LANES),
         _full_spec((1, LANES)), _full_spec((1, q_rank)), _full_spec((1, kv_rank))],
        [((seq, q_rank), BF16), ((seq, kv_rank), BF16), ((seq, LANES), BF16)] + [((seq, LANES), F32)] * 3,
        [_row_spec(t_row, q_rank), _row_spec(t_row, kv_rank)] + [_row_spec(t_row, LANES)] * 4, g1)

    def head_mm(name, act, wh, out_dtype):
        kdim, ndim = wh.shape[1], wh.shape[2]
        return _mm(name, act, wh, grid=(nh, 1, 1),
                   a_spec=pl.BlockSpec((seq, kdim), lambda h, i, k: (i, 0)),
                   b_spec=pl.BlockSpec((None, kdim, ndim), lambda h, i, k: (h, 0, 0)),
                   o_spec=pl.BlockSpec((None, seq, ndim), lambda h, i, k: (h, i, 0)),
                   out_shape=(nh, seq, ndim), out_dtype=out_dtype)

    q_raw = head_mm("mla_q", qn, w["wuq"], F32)
    kv = head_mm("mla_kv", kvn, w["wukv"], BF16)
    y_mla = _attn_fwd(q_raw, kv, kpe, cos_t, sa_t, sb_t)
    mla_w = nh * V_DIM

    def out_proj_fn(ys, ym, ws, wm, wo, xb, wf):
        yc = jnp.concatenate([_rms(ys, ws), _rms(ym, wm)], axis=1).astype(BF16)
        hb = xb + jnp.dot(yc, wo, preferred_element_type=F32)
        return yc, hb, _rms(hb, wf)

    ycat, h1, hn2 = _blockwise(
        "out_norm_proj", out_proj_fn, [y_ssm, y_mla, w["son"], w["mon"], w["wout"], x, w["ffn_norm"]],
        [wide, _row_spec(t_wide, mla_w), _full_spec((1, ssm_w)), _full_spec((1, mla_w)),
         _full_spec((d, d), single=True), _row_spec(t_wide, d), _full_spec((1, d))],
        [((seq, d), BF16), ((seq, d), F32), ((seq, d), BF16)], [_row_spec(t_wide, d)] * 3, g_wide)

    tku = d
    half = ns // 2
    tm = min(1024, seq)
    a_ff = _mm("ffn_up", hn2, w["wup"], grid=(ns, seq // tm, d // tku),
               a_spec=pl.BlockSpec((tm, tku), lambda s, i, k: (i, k)),
               b_spec=pl.BlockSpec((None, tku, c_ff), lambda s, i, k: (s, k, 0)),
               o_spec=pl.BlockSpec((None, None, tm, c_ff), lambda s, i, k: (s % half, s // half, i, 0)),
               out_shape=(half, 2, seq, c_ff), out_dtype=F32)
    cb3 = w["conv_b"].reshape(ns, 1, c_ff)
    m_ff = _conv_gate_fwd(a_ff, w["conv_w"], cb3)
    d_ff = half * c_ff
    wdn = w["wdown"]
    tnd = d
    tmx, tnx = seq, _tile(d, 512)
    h2 = _mm2d("ffn_down", m_ff, wdn, NN, F32, tm=512, tn=512, tk=d_ff, res=h1)

    def loss_fn(hb, tb, wv):
        def f(hh, ww):
            err = _rms(hh, ww) - tb
            return 0.5 * jnp.sum(jnp.mean(err * err, axis=-1))

        lossv, (dh, dw) = jax.value_and_grad(f, argnums=(0, 1))(hb, wv)
        return dh, dh, jnp.full((1, LANES), lossv, F32), dw

    fin_w = w["final_norm"].reshape(1, d)
    dh2, dh2b, loss_acc, g_final = _blockwise(
        "loss_head", loss_fn, [h2, target, fin_w], [_row_spec(t_row, d), _row_spec(t_row, d), _full_spec((1, d))],
        [((seq, d), F32), ((seq, d), BF16), ((1, LANES), F32), ((1, d), F32)],
        [_row_spec(t_row, d), _row_spec(t_row, d), _full_spec((1, LANES)), _full_spec((1, d))], g1, n_acc=2)
    loss = loss_acc

    dm = _mm2d("ffn_down_dx", dh2b, wdn, NT, F32, tn=c_ff)
    tks = seq
    g_wdown = _mm2d("ffn_down_dw", m_ff, dh2b, TN, BF16, tm=c_ff)
    emit(wdown=g_wdown)
    da_ff, g_convw2, g_convb2 = _conv_gate_bwd(a_ff, w["conv_w"], cb3, dm)
    g_convw = jnp.swapaxes(g_convw2, 0, 1).reshape(ns, 3, c_ff)
    g_convb = jnp.swapaxes(g_convb2, 0, 1).reshape(ns, 1, c_ff)
    g_wup = _mm("ffn_up_dw", hn2, da_ff, grid=(ns, d // tnd, seq // tks), contract=TN,
                a_spec=pl.BlockSpec((tks, tnd), lambda s, j, k: (k, j), pipeline_mode=pl.Buffered(1)),
                b_spec=pl.BlockSpec((None, None, tks, c_ff), lambda s, j, k: (s % half, s // half, k, 0)),
                o_spec=pl.BlockSpec((None, tnd, c_ff), lambda s, j, k: (s, j, 0)),
                out_shape=(ns, d, c_ff), out_dtype=BF16)
    emit(wup=g_wup)
    dhn2 = _mm("ffn_up_dx", da_ff, w["wup"], grid=(seq // tmx, d // tnx, ns), contract=NT,
               a_spec=pl.BlockSpec((None, None, tmx, c_ff), lambda i, j, s: (s % half, s // half, i, 0)),
               b_spec=pl.BlockSpec((None, tnx, c_ff), lambda i, j, s: (s, j, 0)),
               o_spec=pl.BlockSpec((tmx, tnx), lambda i, j, s: (i, j)),
               out_shape=(seq, d), out_dtype=F32)
    emit(wup_pair_sums_after=dhn2)

    def norm_bwd_fn(hb, dres, dn, wv):
        dx_, dw_ = _rms_bwd(hb, wv, dn)
        dtot = dres + dx_
        return dtot, dtot, dw_

    dh1, dh1b, g_ffn_norm = _blockwise(
        "norm2_bwd", norm_bwd_fn, [h1, dh2, dhn2, w["ffn_norm"]],
        [_row_spec(t_row, d)] * 3 + [_full_spec((1, d))],
        [((seq, d), F32), ((seq, d), BF16), ((1, d), F32)],
        [_row_spec(t_row, d), _row_spec(t_row, d), _full_spec((1, d))], g1, n_acc=1)

    g_wout = _mm2d("out_proj_dw", ycat, dh1b, TN, BF16)

    def outnorm_bwd_fn(dhb, wo, ys, ym, ws, wm):
        dyc = lax.dot_general(dhb, wo, (NT, ((), ())), preferred_element_type=F32)
        dys, dws = _rms_bwd(ys, ws, dyc[:, :ssm_w])
        dym, dwm = _rms_bwd(ym, wm, dyc[:, ssm_w:])
        return dys, dym, dws, dwm

    dy_ssm, dy_mla, g_son, g_mon = _blockwise(
        "out_proj_dx_norm_bwd", outnorm_bwd_fn, [dh1b, w["wout"], y_ssm, y_mla, w["son"], w["mon"]],
        [_row_spec(t_wide, d), _full_spec((d, d), single=True), wide, _row_spec(t_wide, mla_w),
         _full_spec((1, ssm_w)), _full_spec((1, mla_w))],
        [((seq, ssm_w), F32), ((seq, mla_w), F32), ((1, ssm_w), F32), ((1, mla_w), F32)],
        [wide, _row_spec(t_wide, mla_w), _full_spec((1, ssm_w)), _full_spec((1, mla_w))],
        g_wide, n_acc=2)

    def glu_bwd_fn(dy, yp, zb, ub, dsk, wg):
        ygv = jax.nn.gelu(yp)
        sg = jax.nn.sigmoid(zb)
        dz = dy * ygv * sg * (1.0 - sg)
        dzb = dz.astype(BF16)
        dyg = dy * sg + lax.dot_general(dzb, wg, (NT, ((), ())), preferred_element_type=F32)
        _, vjp = jax.vjp(jax.nn.gelu, yp)
        dyp = vjp(dyg)[0]
        return (dzb, dyp, dyp * dsk, jnp.sum(dz, axis=0, keepdims=True), jnp.sum(dyp * ub, axis=0, keepdims=True))

    dz, dy_pre, du1, g_bglu, g_ssmd = _blockwise(
        "ssm_glu_bwd", glu_bwd_fn, [dy_ssm, y_pre, z, proj, w["ssm_d"], w["wglu"]],
        [wide] * 4 + [_full_spec((1, ssm_w)), _full_spec((ssm_w, ssm_w), single=True)],
        [((seq, ssm_w), BF16), ((seq, ssm_w), BF16), ((seq, ssm_w), F32), ((1, ssm_w), F32), ((1, ssm_w), F32)],
        [wide] * 3 + [_full_spec((1, ssm_w))] * 2, g_wide, n_acc=2)
    g_wglu = _mm2d("ssm_glu_dw", yg, dz, TN, BF16)
    dq_raw, dkv, dkp_h = _attn_bwd(q_raw, kv, kpe, cos_t, sa_t, sb_t, dy_mla)

    def head_mm_dx(name, dact, wh):
        kdim, ndim = wh.shape[1], wh.shape[2]
        return _mm(name, dact, wh, grid=(1, 1, nh), contract=NT,
                   a_spec=pl.BlockSpec((None, seq, ndim), lambda i, j, h: (h, i, 0)),
                   b_spec=pl.BlockSpec((None, kdim, ndim), lambda i, j, h: (h, 0, 0)),
                   o_spec=pl.BlockSpec((seq, kdim), lambda i, j, h: (i, 0)),
                   out_shape=(seq, kdim), out_dtype=F32)

    def head_mm_dw(name, act, dact):
        kdim, ndim = act.shape[1], dact.shape[2]
        return _mm(name, act, dact, grid=(nh, 1, seq // tks), contract=TN,
                   a_spec=pl.BlockSpec((tks, kdim), lambda h, j, k: (k, 0)),
                   b_spec=pl.BlockSpec((None, tks, ndim), lambda h, j, k: (h, k, 0)),
                   o_spec=pl.BlockSpec((None, kdim, ndim), lambda h, j, k: (h, 0, 0)),
                   out_shape=(nh, kdim, ndim), out_dtype=BF16)

    g_wuq = head_mm_dw("mla_q_dw", qn, dq_raw)
    g_wukv = head_mm_dw("mla_kv_dw", kvn, dkv)
    dqn = head_mm_dx("mla_q_dx", dq_raw, w["wuq"])
    dkvn = head_mm_dx("mla_kv_dx", dkv, w["wukv"])
    emit(not_before=(dqn, dkvn, dy_pre), wout=g_wout, wuq=g_wuq, wukv=g_wukv, wglu=g_wglu, conv_w=g_convw)

    du, dwb, dwc, da_lay = _ssm_bwd(dy_pre, s_all, proj, du1, wb, wc, a_lay)
    g_c_re = blockdiag_out_t(dwc[:, :STATE_BLOCK, :])
    g_c_im = -blockdiag_out_t(dwc[:, STATE_BLOCK:, :])
    dbbt_re = blockdiag_in_t(dwb[:, :, :STATE_BLOCK])
    dbbt_im = blockdiag_in_t(dwb[:, :, STATE_BLOCK:])
    da3 = da_lay.reshape(nj, 2, STATE_BLOCK)
    dabar_re = da3[:, 0, :].reshape(n_groups, 1, SSM_STATE)
    dabar_im = da3[:, 1, :].reshape(n_groups, 1, SSM_STATE)
    g_lr3, g_li3, g_ldt3, g_bt_re, g_bt_im = _s5_prep_bwd(lr3, li3, ldt3, bt_re, bt_im,
                                                           dabar_re, dabar_im, dbbt_re, dbbt_im)

    def mla_prep_bwd_fn(cq, ckv, dqn_b, dkvn_b, dkp_b, cos, sa, sb, wq, wkv):
        dcq, dwq = _rms_bwd(cq, wq, dqn_b)
        dckv, dwkv = _rms_bwd(ckv, wkv, dkvn_b)
        dkp_sum = dkp_b[0]
        for h in range(1, nh):
            dkp_sum = dkp_sum + dkp_b[h]
        return dcq, dckv, _rope128_t(dkp_sum, cos, sa, sb), dwq, dwkv

    dc_q, dc_kv, dkpe_raw, g_qnorm, g_kvnorm = _blockwise(
        "mla_prep_bwd", mla_prep_bwd_fn, [proj, proj, dqn, dkvn, dkp_h, cos_t, sa_t, sb_t, w["q_norm"], w["kv_norm"]],
        [cq_spec, ckv_spec, _row_spec(t_row, q_rank), _row_spec(t_row, kv_rank),
         pl.BlockSpec((nh, t_row, LANES), lambda i: (0, i, 0))] + [_row_spec(t_row, LANES)] * 3
        + [_full_spec((1, q_rank)), _full_spec((1, kv_rank))],
        [((seq, q_rank), BF16), ((seq, kv_rank), BF16), ((seq, LANES), BF16), ((1, q_rank), F32), ((1, kv_rank), F32)],
        [_row_spec(t_row, q_rank), _row_spec(t_row, kv_rank), _row_spec(t_row, LANES), _full_spec((1, q_rank)),
         _full_spec((1, kv_rank))], g1, n_acc=2)

    dproj = jnp.concatenate([du, dc_q, dc_kv, dkpe_raw], axis=1)
    g_win = _mm2d("proj_dw", hn, dproj, TN, BF16, tn=640)
    emit(win=g_win)
    def norm1_bwd_fn(dpb, wi, xb, dres, wv):
        dn = lax.dot_general(dpb, wi, (NT, ((), ())), preferred_element_type=F32)
        dx_, dw_ = _rms_bwd(xb, wv, dn)
        return dres + dx_, dw_

    grad_x, g_attn_norm = _blockwise(
        "proj_dx_norm1_bwd", norm1_bwd_fn, [dproj, w["win"], x, dh1, attn_w],
        [_row_spec(t_row, in_pad), _full_spec((d, in_pad), single=True), _row_spec(t_row, d), _row_spec(t_row, d), _full_spec((1, d))],
        [((seq, d), F32), ((1, d), F32)], [_row_spec(t_row, d), _full_spec((1, d))], g1, n_acc=1)
    emit(win_pair_sums_after=grad_x)

    grads = dict(
        attn_norm=g_attn_norm, win=g_win, lam_re=g_lr3, lam_im=g_li3, log_dt=g_ldt3,
        bt_re=g_bt_re, bt_im=g_bt_im, c_re=g_c_re, c_im=g_c_im,
        ssm_d=g_ssmd, wglu=g_wglu, b_glu=g_bglu, q_norm=g_qnorm, wuq=g_wuq, kv_norm=g_kvnorm, wukv=g_wukv,
        son=g_son, mon=g_mon, wout=g_wout, ffn_norm=g_ffn_norm, wup=g_wup, conv_w=g_convw, conv_b=g_convb,
        wdown=g_wdown, final_norm=g_final)
    return loss, grad_x, grads


def _mesh_pos():
    return lax.axis_index("x"), lax.axis_index("y"), lax.axis_index("c")


def _handshake_all():
    x, y, c = _mesh_pos()
    barrier = pltpu.get_barrier_semaphore()
    for k in range(1, N_DEV):
        peer = (1 - x if k & 4 else x, 1 - y if k & 2 else y, 1 - c if k & 1 else c)
        pl.semaphore_signal(barrier, inc=1, device_id=peer, device_id_type=MESH)
    pl.semaphore_wait(barrier, N_DEV - 1)


def _handshake(peers):
    barrier = pltpu.get_barrier_semaphore()
    for peer in peers:
        pl.semaphore_signal(barrier, inc=1, device_id=peer, device_id_type=MESH)
    pl.semaphore_wait(barrier, len(peers))


def _comm_call(name, body, n, out_shape, ins, collective_id, after=None, copies=7, n_remote=None, n_local=None):
    n_remote = copies * n if n_remote is None else n_remote
    sems = [pltpu.SemaphoreType.DMA((n_remote,)), pltpu.SemaphoreType.DMA((n_remote,)),
            pltpu.SemaphoreType.DMA((n if n_local is None else n_local,))]
    if collective_id is None:
        any_spec = pl.BlockSpec(memory_space=pl.ANY)
        return pl.pallas_call(body, name=name, out_shape=out_shape, in_specs=[any_spec] * n,
                              out_specs=[any_spec] * n, scratch_shapes=sems)(*ins)
    seq_body = body
    if after:
        n_after = len(after)
        ins = list(ins) + list(after)

        def seq_body(*refs):
            body(*refs[:n], *refs[n + n_after:])

    return pl.kernel(seq_body, name=name, out_type=out_shape,
                     mesh=plsc.ScalarSubcoreMesh(axis_name="seq", num_cores=1), scratch_types=sems,
                     compiler_params=pltpu.CompilerParams(collective_id=collective_id))(*ins)


def _all_gather(name, xs, collective_id=None, after=None, pair_sums=()):
    n = len(xs)
    nh = len(pair_sums)
    m = n + nh

    def body(*refs):
        x_refs, h_refs, o_refs, e_refs = refs[:n], refs[n:m], refs[m:m + n], refs[m + n:2 * m]
        send_sems, recv_sems, local_sems = refs[2 * m:]
        if collective_id is not None:
            _handshake_all()
        finish_pairs = _chip_copies(h_refs, e_refs, send_sems, recv_sems, local_sems, 7 * n, n) if nh else None
        x, y, c = _mesh_pos()
        me, sibling = (x, y, c), (x, y, 1 - c)
        chips = [(1 - x, y), (x, 1 - y), (1 - x, 1 - y)]

        def slot(o_ref, px, py, pc):
            return o_ref.at[4 * px + 2 * py + pc]

        def copy(t, k, block, to, src=None):
            dst = slot(o_refs[t], *block)
            return pltpu.make_async_remote_copy(
                src_ref=dst if src is None else src, dst_ref=dst,
                send_sem=send_sems.at[7 * t + k], recv_sem=recv_sems.at[7 * t + k],
                device_id=to, device_id_type=MESH)

        started = []
        for t in range(n):
            mine = pltpu.make_async_copy(x_refs[t], slot(o_refs[t], *me), local_sems.at[t])
            mine.start()
            started.append(mine)
        first = []
        for t in range(n):
            first.append(copy(t, 0, me, sibling, src=x_refs[t]))
            first += [copy(t, 1 + j, me, (*chip, c), src=x_refs[t]) for j, chip in enumerate(chips)]
        for cp in first:
            cp.start()
        passed = []
        for j, chip in enumerate(chips):
            for t in range(n):
                copy(t, 1 + j, (*chip, c), me).wait_recv()
                fwd = copy(t, 4 + j, (*chip, c), sibling)
                fwd.start()
                passed.append(fwd)
        for t in range(n):
            copy(t, 0, sibling, me).wait_recv()
            for j, chip in enumerate(chips):
                copy(t, 4 + j, (*chip, 1 - c), me).wait_recv()
        for cp in first + passed:
            cp.wait_send()
        for mine in started:
            mine.wait()
        if nh:
            finish_pairs()

    out_shape = ([jax.ShapeDtypeStruct((N_DEV,) + v.shape, v.dtype) for v in xs]
                 + [jax.ShapeDtypeStruct(v.shape, v.dtype) for v in pair_sums])
    return _comm_call(name, body, m, out_shape, list(xs) + list(pair_sums), collective_id, after,
                      n_remote=7 * n + (N_CHIP - 1) * nh, n_local=m)


def _exchange_partials(name, gs, collective_id=None, after=None):
    n = len(gs)

    def body(*refs):
        g_refs, o_refs = refs[:n], refs[n:2 * n]
        send_sems, recv_sems, local_sems = refs[2 * n:]
        if collective_id is not None:
            _handshake_all()
        x, y, c = _mesh_pos()
        me_idx = 4 * x + 2 * y + c
        copies = []
        for t in range(n):
            mine = pltpu.make_async_copy(g_refs[t].at[me_idx], o_refs[t].at[me_idx], local_sems.at[t])
            mine.start()
            copies.append(mine)
        remote = []
        for k in range(1, N_DEV):
            px = 1 - x if k & 4 else x
            py = 1 - y if k & 2 else y
            pc = 1 - c if k & 1 else c
            p_idx = 4 * px + 2 * py + pc
            for t in range(n):
                cp = pltpu.make_async_remote_copy(
                    src_ref=g_refs[t].at[p_idx], dst_ref=o_refs[t].at[me_idx],
                    send_sem=send_sems.at[7 * t + k - 1], recv_sem=recv_sems.at[7 * t + k - 1],
                    device_id=(px, py, pc), device_id_type=MESH)
                cp.start()
                landing = pltpu.make_async_remote_copy(
                    src_ref=g_refs[t].at[p_idx], dst_ref=o_refs[t].at[p_idx],
                    send_sem=send_sems.at[7 * t + k - 1], recv_sem=recv_sems.at[7 * t + k - 1],
                    device_id=(px, py, pc), device_id_type=MESH)
                remote.append((cp, landing))
        for cp, landing in remote:
            landing.wait_recv()
        for cp, landing in remote:
            cp.wait_send()
        for mine in copies:
            mine.wait()

    out_shape = [jax.ShapeDtypeStruct(v.shape, v.dtype) for v in gs]
    return _comm_call(name, body, n, out_shape, gs, collective_id, after)


N_CHIP = N_DEV // 2
PAIR_ADD_BLOCK_ELEMS = 1024 * 1024


def _pair_swap(name, gs, collective_id, after=None):
    n = len(gs)

    def body(*refs):
        g_refs, o_refs = refs[:n], refs[n:2 * n]
        send_sems, recv_sems, _ = refs[2 * n:]
        x, y, c = _mesh_pos()
        sibling = (x, y, 1 - c)
        _handshake([sibling])
        copies = []
        for t in range(n):
            for k in range(N_CHIP):
                copies.append(pltpu.make_async_remote_copy(
                    src_ref=g_refs[t].at[2 * k + 1 - c], dst_ref=o_refs[t].at[k],
                    send_sem=send_sems.at[N_CHIP * t + k], recv_sem=recv_sems.at[N_CHIP * t + k],
                    device_id=sibling, device_id_type=MESH))
        for cp in copies:
            cp.start()
        for cp in copies:
            cp.wait_recv()
        for cp in copies:
            cp.wait_send()

    out_shape = [jax.ShapeDtypeStruct((N_CHIP,) + v.shape[1:], v.dtype) for v in gs]
    return _comm_call(name, body, n, out_shape, gs, collective_id, after, copies=N_CHIP)


def _pair_add(name, g, got):
    _, r, c = g.shape
    tr = r
    if r * c > PAIR_ADD_BLOCK_ELEMS and r % SUBLANES == 0:
        tr = SUBLANES
        while r % (tr * 2) == 0 and tr * 2 * c <= PAIR_ADD_BLOCK_ELEMS:
            tr *= 2

    def body(core_ref, g_ref, got_ref, o_ref):
        o_ref[...] = (g_ref[...].astype(F32) + got_ref[...].astype(F32)).astype(o_ref.dtype)

    grid_spec = pltpu.PrefetchScalarGridSpec(
        num_scalar_prefetch=1, grid=(N_CHIP, r // tr),
        in_specs=[pl.BlockSpec((None, None, tr, c), lambda k, i, core: (k, core[0], i, 0)),
                  pl.BlockSpec((None, tr, c), lambda k, i, core: (k, i, 0))],
        out_specs=pl.BlockSpec((None, tr, c), lambda k, i, core: (k, i, 0)))
    core = lax.axis_index("c").astype(jnp.int32).reshape(1)
    return pl.pallas_call(body, name=name, grid_spec=grid_spec, out_shape=jax.ShapeDtypeStruct((N_CHIP, r, c), g.dtype),
                          compiler_params=_cparams())(core, g.reshape(N_CHIP, 2, r, c), got)


def _chip_copies(h_refs, o_refs, send_sems, recv_sems, local_sems, sem0, local0):
    n = len(h_refs)
    per = N_CHIP - 1
    x, y, c = _mesh_pos()
    others = [(1 - x if k & 2 else x, 1 - y if k & 1 else y) for k in range(1, N_CHIP)]
    my_chip = 2 * x + y
    local = []
    for t in range(n):
        mine = pltpu.make_async_copy(h_refs[t].at[my_chip], o_refs[t].at[my_chip], local_sems.at[local0 + t])
        mine.start()
        local.append(mine)
    remote = []
    for j, (px, py) in enumerate(others):
        chip = 2 * px + py
        for t in range(n):
            sems = dict(send_sem=send_sems.at[sem0 + per * t + j], recv_sem=recv_sems.at[sem0 + per * t + j],
                        device_id=(px, py, c), device_id_type=MESH)
            cp = pltpu.make_async_remote_copy(src_ref=h_refs[t].at[chip], dst_ref=o_refs[t].at[my_chip], **sems)
            cp.start()
            landing = pltpu.make_async_remote_copy(src_ref=h_refs[t].at[chip], dst_ref=o_refs[t].at[chip], **sems)
            remote.append((cp, landing))

    def finish():
        for cp, landing in remote:
            landing.wait_recv()
        for cp, landing in remote:
            cp.wait_send()
        for mine in local:
            mine.wait()

    return finish


def _chip_exchange(name, hs, collective_id, after=None):
    n = len(hs)
    per = N_CHIP - 1

    def body(*refs):
        h_refs, o_refs = refs[:n], refs[n:2 * n]
        send_sems, recv_sems, local_sems = refs[2 * n:]
        x, y, c = _mesh_pos()
        _handshake([(1 - x if k & 2 else x, 1 - y if k & 1 else y, c) for k in range(1, N_CHIP)])
        _chip_copies(h_refs, o_refs, send_sems, recv_sems, local_sems, 0, 0)()

    out_shape = [jax.ShapeDtypeStruct(v.shape, v.dtype) for v in hs]
    return _comm_call(name, body, n, out_shape, hs, collective_id, after, copies=per)


ADAM_BLOCK_ELEMS = 256 * 1024


def _sum_parts(pb):
    g = pb[0].astype(F32)
    for j in range(1, pb.shape[0]):
        g = g + pb[j].astype(F32)
    return g


def _adam_math(g, wb_, mb, vb):
    m_new = ADAM_B1 * mb + (1.0 - ADAM_B1) * g
    v_new = ADAM_B2 * vb + (1.0 - ADAM_B2) * (g * g)
    m_hat = m_new / (1.0 - ADAM_B1 ** ADAM_STEP)
    v_hat = v_new / (1.0 - ADAM_B2 ** ADAM_STEP)
    delta = -ADAM_LR * (m_hat / (jnp.sqrt(v_hat) + ADAM_EPS) + ADAM_WD * wb_)
    return g, delta, m_new, v_new


def _adamw_multi(name, items, nblk=1, packed=None):
    n = len(items)

    def spec(shape, lead):
        blk = list(shape)
        blk[lead + 1] = shape[lead + 1] // nblk
        if nblk == 1:
            return pl.BlockSpec(tuple(blk), lambda i, nd=len(shape): (0,) * nd)
        return pl.BlockSpec(tuple(blk), lambda i, nd=len(shape), ax=lead + 1: (0,) * ax + (i,) + (0,) * (nd - ax - 1))

    ins, in_specs, out_specs, out_shape, where = [], [], [], [], []
    if packed is not None:
        ins.append(packed)
        in_specs.append(spec(packed.shape, 1))
    for parts, wv, mv, vv in items:
        if isinstance(parts, int):
            where.append((0, parts, len(ins)))
        else:
            assert parts.shape[1:] == wv.shape, (name, parts.shape, wv.shape)
            where.append((len(ins), None, len(ins) + 1))
            ins.append(parts)
            in_specs.append(spec(parts.shape, 1))
        ins += [wv, mv, vv]
        in_specs += [spec(wv.shape, 0)] * 3
        out_specs += [spec(wv.shape, 0)] * 4
        out_shape += [jax.ShapeDtypeStruct(wv.shape, F32)] * 4
    n_in = len(ins)

    def body(*refs):
        for t, (ip, off, iw) in enumerate(where):
            wr, mr, vr = refs[iw:iw + 3]
            parts = refs[ip][...] if off is None else refs[ip][:, :, off:off + wr.shape[-1]]
            res = _adam_math(_sum_parts(parts), wr[...], mr[...], vr[...])
            for o, val in zip(refs[n_in + 4 * t:n_in + 4 * t + 4], res):
                o[...] = val

    res = pl.pallas_call(body, name=name, grid=(nblk,), in_specs=in_specs, out_specs=out_specs, out_shape=out_shape,
                         compiler_params=_cparams())(*ins)
    return [tuple(res[4 * t:4 * t + 4]) for t in range(n)]


def _sum_multi(name, parts_list):
    def body(*refs):
        for pr, o in zip(refs[:len(parts_list)], refs[len(parts_list):]):
            o[...] = _sum_parts(pr[...])

    return pl.pallas_call(body, name=name, out_shape=[jax.ShapeDtypeStruct(p.shape[1:], F32) for p in parts_list],
                          compiler_params=_cparams())(*parts_list)


def _adamw_sum(name, parts, wv, mv, vv):
    npart, r, c = parts.shape
    tr = r
    if r * c > ADAM_BLOCK_ELEMS and r % SUBLANES == 0:
        tr = SUBLANES
        while r % (tr * 2) == 0 and tr * 2 * c <= ADAM_BLOCK_ELEMS:
            tr *= 2

    def fn(pb, wb_, mb, vb):
        return _adam_math(_sum_parts(pb), wb_, mb, vb)

    row = pl.BlockSpec((tr, c), lambda i: (i, 0))
    return _blockwise(name, fn, [parts, wv, mv, vv],
                      [pl.BlockSpec((npart, tr, c), lambda i: (0, i, 0)), row, row, row],
                      [((r, c), F32)] * 4, [row] * 4, (r // tr,))


_VECTORS = ["attn_norm", "lam_re", "lam_im", "log_dt", "ssm_d", "b_glu", "q_norm", "kv_norm", "son", "mon",
            "ffn_norm", "conv_b", "final_norm"]
_GHP = ["c_re", "c_im", "bt_re", "bt_im"]
_PACKED = ["attn_norm", "ssm_d", "b_glu", "q_norm", "kv_norm", "son", "mon", "ffn_norm", "conv_b", "final_norm"]
_BIG = ["win", "wglu", "wuq", "wukv", "wout", "wup", "wdown", "conv_w"]
_ROWS_IN_LANES = ("win", "wuq")
_TWO_LEVEL = ("wup", "win")
_AFTER = "_pair_sums_after"
_ORDER = ["attn_norm", "win", "lam_re", "lam_im", "log_dt", "b_re", "b_im", "c_re", "c_im", "ssm_d", "wglu",
          "b_glu", "q_norm", "wuq", "kv_norm", "wukv", "son", "mon", "wout", "ffn_norm", "wup", "conv_w",
          "conv_b", "wdown", "final_norm"]


def kernel(x, positions, attn_norm_w, w_in, ssm_lambda_re, ssm_lambda_im, ssm_log_dt, ssm_b_re, ssm_b_im, ssm_c_re, ssm_c_im, ssm_d, ssm_w_glu, ssm_b_glu, mla_q_norm_w, mla_w_uq, mla_kv_norm_w, mla_w_ukv, ssm_out_norm_w, mla_out_norm_w, w_out, ffn_norm_w, ffn_w_up, ffn_conv_w, ffn_conv_b, ffn_w_down, final_norm_w, loss_target, m_attn_norm_w, m_w_in, m_ssm_lambda_re, m_ssm_lambda_im, m_ssm_log_dt, m_ssm_b_re, m_ssm_b_im, m_ssm_c_re, m_ssm_c_im, m_ssm_d, m_ssm_w_glu, m_ssm_b_glu, m_mla_q_norm_w, m_mla_w_uq, m_mla_kv_norm_w, m_mla_w_ukv, m_ssm_out_norm_w, m_mla_out_norm_w, m_w_out, m_ffn_norm_w, m_ffn_w_up, m_ffn_conv_w, m_ffn_conv_b, m_ffn_w_down, m_final_norm_w, v_attn_norm_w, v_w_in, v_ssm_lambda_re, v_ssm_lambda_im, v_ssm_log_dt, v_ssm_b_re, v_ssm_b_im, v_ssm_c_re, v_ssm_c_im, v_ssm_d, v_ssm_w_glu, v_ssm_b_glu, v_mla_q_norm_w, v_mla_w_uq, v_mla_kv_norm_w, v_mla_w_ukv, v_ssm_out_norm_w, v_mla_out_norm_w, v_w_out, v_ffn_norm_w, v_ffn_w_up, v_ffn_conv_w, v_ffn_conv_b, v_ffn_w_down, v_final_norm_w):
    wts = dict(attn_norm=attn_norm_w, win=w_in, lam_re=ssm_lambda_re, lam_im=ssm_lambda_im, log_dt=ssm_log_dt,
               b_re=ssm_b_re, b_im=ssm_b_im, c_re=ssm_c_re, c_im=ssm_c_im, ssm_d=ssm_d, wglu=ssm_w_glu,
               b_glu=ssm_b_glu, q_norm=mla_q_norm_w, wuq=mla_w_uq, kv_norm=mla_kv_norm_w, wukv=mla_w_ukv,
               son=ssm_out_norm_w, mon=mla_out_norm_w, wout=w_out, ffn_norm=ffn_norm_w, wup=ffn_w_up,
               conv_w=ffn_conv_w, conv_b=ffn_conv_b, wdown=ffn_w_down, final_norm=final_norm_w)
    moms = dict(zip(_ORDER, [m_attn_norm_w, m_w_in, m_ssm_lambda_re, m_ssm_lambda_im, m_ssm_log_dt, m_ssm_b_re,
                             m_ssm_b_im, m_ssm_c_re, m_ssm_c_im, m_ssm_d, m_ssm_w_glu, m_ssm_b_glu, m_mla_q_norm_w,
                             m_mla_w_uq, m_mla_kv_norm_w, m_mla_w_ukv, m_ssm_out_norm_w, m_mla_out_norm_w, m_w_out,
                             m_ffn_norm_w, m_ffn_w_up, m_ffn_conv_w, m_ffn_conv_b, m_ffn_w_down, m_final_norm_w]))
    vels = dict(zip(_ORDER, [v_attn_norm_w, v_w_in, v_ssm_lambda_re, v_ssm_lambda_im, v_ssm_log_dt, v_ssm_b_re,
                             v_ssm_b_im, v_ssm_c_re, v_ssm_c_im, v_ssm_d, v_ssm_w_glu, v_ssm_b_glu, v_mla_q_norm_w,
                             v_mla_w_uq, v_mla_kv_norm_w, v_mla_w_ukv, v_ssm_out_norm_w, v_mla_out_norm_w, v_w_out,
                             v_ffn_norm_w, v_ffn_w_up, v_ffn_conv_w, v_ffn_conv_b, v_ffn_w_down, v_final_norm_w]))
    seq, d = x.shape[1], x.shape[2]
    in_width = w_in.shape[2]
    in_pad = -(-in_width // LANES) * LANES
    q_cols = mla_w_uq.shape[2]
    q_pad = 2 * LANES

    (win_g,) = _all_gather("gather_w_in", [jnp.pad(w_in[0], ((0, 0), (0, in_pad - in_width))).astype(BF16)])
    wglu_g, wuq_g, wukv_g, wout_g, convw_g = _all_gather(
        "gather_mix", [ssm_w_glu[0].astype(BF16), jnp.pad(mla_w_uq[0], ((0, 0), (0, q_pad - q_cols))).astype(BF16),
                       mla_w_ukv[0].astype(BF16), w_out[0].astype(BF16), ffn_conv_w[0]], collective_id=0)
    (wup_g,) = _all_gather("gather_ffn_up", [ffn_w_up[0].astype(BF16)], collective_id=1)
    (wdown_g,) = _all_gather("gather_ffn_down", [ffn_w_down[0].astype(BF16)], collective_id=2)
    ns = N_DEV
    c_ff = wup_g.shape[2]
    w = dict(
        attn_norm=attn_norm_w, win=win_g.reshape(d, in_pad), lam_re=ssm_lambda_re, lam_im=ssm_lambda_im,
        log_dt=ssm_log_dt, b_re=ssm_b_re, b_im=ssm_b_im, c_re=ssm_c_re, c_im=ssm_c_im, ssm_d=ssm_d,
        wglu=wglu_g.reshape(d // 2, d // 2), b_glu=ssm_b_glu, q_norm=mla_q_norm_w, wuq=wuq_g,
        kv_norm=mla_kv_norm_w, wukv=wukv_g, son=ssm_out_norm_w, mon=mla_out_norm_w, wout=wout_g.reshape(d, d),
        ffn_norm=ffn_norm_w, wup=wup_g, conv_w=convw_g, conv_b=ffn_conv_b,
        wdown=wdown_g.reshape(ns // 2 * c_ff, d), final_norm=final_norm_w)

    shard_layout = dict(
        win=lambda a: a[:, :in_width].reshape(N_DEV, d // N_DEV, in_width),
        wglu=lambda a: a.reshape(N_DEV, d // 2 // N_DEV, d // 2),
        wuq=lambda a: a[:, :, :q_cols], wukv=lambda a: a, wout=lambda a: a.reshape(N_DEV, d // N_DEV, d),
        wup=lambda a: a, wdown=lambda a: a.reshape(N_DEV, c_ff // 2, d), conv_w=lambda a: a)
    recv = {}
    next_id = [3]

    last = [None]

    out = {}

    def update(k):
        shp = wts[k].shape
        r, c = shp[-2], shp[-1]
        if k in _ROWS_IN_LANES:
            t = lambda a: jnp.swapaxes(a.reshape(-1, r, c), 1, 2)
            res = _adamw_sum("adamw_" + k, t(recv[k]), t(wts[k])[0], t(moms[k])[0], t(vels[k])[0])
            out[k] = [jnp.swapaxes(a, 0, 1).reshape(shp) for a in res]
            return res[0]
        res = _adamw_sum("adamw_" + k, recv[k].reshape(-1, r, c), wts[k].reshape(r, c),
                         moms[k].reshape(r, c), vels[k].reshape(r, c))
        out[k] = [a.reshape(shp) for a in res]
        return res[0]

    pending = {}

    def exchange(not_before=(), **grads):
        names = list(grads)
        if len(names) == 1 and names[0] in _TWO_LEVEL:
            k = names[0]
            parts = shard_layout[k](grads[k])
            got = _pair_swap("swap_" + k, [parts], collective_id=next_id[0], after=[last[0]])[0]
            next_id[0] += 1
            pending[k] = (parts, got)
            last[0] = got
            return
        if len(names) == 1 and names[0].endswith(_AFTER):
            k = names[0][:-len(_AFTER)]
            sums = _pair_add("pair_add_" + k, *pending[k])
            if k == "win":
                pending["tail"] = sums
                return
            recv[k] = _chip_exchange("exchange_" + k, [sums], collective_id=next_id[0],
                                     after=[last[0], grads[names[0]]])[0]
            next_id[0] += 1
            last[0] = recv[k]
            return
        got = _exchange_partials("exchange_" + "_".join(names), [shard_layout[k](grads[k]) for k in names],
                                 collective_id=next_id[0], after=[a for a in (last[0], *not_before) if a is not None])
        next_id[0] += 1
        last[0] = got[-1]
        recv.update(zip(names, got))

    loss_part, grad_x, g = _local_step(x[0], positions[0], loss_target[0], w, emit=exchange)
    n_groups = ssm_lambda_re.shape[1]
    two_d = {"lam_re": (n_groups, -1), "lam_im": (n_groups, -1)}
    dense = {k: g[k].reshape(two_d.get(k, (1, -1))) for k in _VECTORS}
    offsets, width = {}, 0
    for k in _PACKED:
        offsets[k] = width
        width += dense[k].shape[1]
    sent = dict(packed=jnp.concatenate([dense[k] for k in _PACKED], axis=1),
                **{k: dense[k] for k in _VECTORS if k not in _PACKED},
                **{k: g[k].reshape(n_groups, -1).astype(BF16) for k in _GHP},
                loss=loss_part)
    names = list(sent)
    got = _all_gather("gather_small_grads", [sent[k] for k in names], collective_id=next_id[0], after=[last[0]],
                      pair_sums=[pending["tail"]])
    gathered = dict(zip(names, got))
    recv["win"] = got[len(names)]
    for k in _BIG:
        if k not in out and k != "win":
            update(k)
    update("win")

    def finish(keys, results):
        for k, res in zip(keys, results):
            out[k] = [a.reshape(wts[k].shape) for a in res]

    view = lambda k, a: a.reshape(dense[k].shape)
    finish(_VECTORS, _adamw_multi("adamw_vectors", [(offsets.get(k, gathered.get(k)), view(k, wts[k]), view(k, moms[k]),
                                                     view(k, vels[k])) for k in _VECTORS], packed=gathered["packed"]))
    summed = _GHP + ["loss"]
    sums = dict(zip(summed, _sum_multi("sum_ssm_bc_loss", [gathered[k] for k in summed])))
    loss = sums["loss"][0, 0]
    ghp = lambda k: sums[k].reshape(g[k].shape)
    t_hp = lambda a: jnp.swapaxes(a, 2, 3)
    bc_keys = ["c_re", "c_im", "b_re", "b_im"]
    items = [(ghp(k)[None, None], wts[k], moms[k], vels[k]) for k in bc_keys[:2]]
    items += [(ghp(t)[None, None], t_hp(wts[k]), t_hp(moms[k]), t_hp(vels[k]))
              for k, t in zip(bc_keys[2:], ("bt_re", "bt_im"))]
    res = _adamw_multi("adamw_ssm_bc", items)
    finish(bc_keys, res[:2] + [tuple(t_hp(a) for a in r) for r in res[2:]])

    grad_x = grad_x.reshape(x.shape)
    return (loss, grad_x, *[out[k][0] for k in _ORDER], *[out[k][1] for k in _ORDER],
            *[out[k][2] for k in _ORDER], *[out[k][3] for k in _ORDER])
```

```python
import functools

import jax
import jax.numpy as jnp
from jax import lax
from jax.experimental import pallas as pl
from jax.experimental.pallas import tpu as pltpu
from jax.experimental.pallas import tpu_sc as plsc

F32 = jnp.float32
BF16 = jnp.bfloat16
MESH = pl.DeviceIdType.MESH

N_DEV = 8
LANES = 128
SUBLANES = 8
VMEM_LIMIT = 48 * 1024 * 1024

SSM_GROUP = 16
SSM_STATE = 64
GROUPS_PER_BLOCK = LANES // SSM_GROUP
STATE_BLOCK = GROUPS_PER_BLOCK * SSM_STATE
QK_NOPE = 128
QK_ROPE = 64
V_DIM = 128
ROPE_THETA = 10000.0
RMS_EPS = 1e-6

ADAM_LR = 0.001
ADAM_B1 = 0.9
ADAM_B2 = 0.999
ADAM_EPS = 1e-08
ADAM_WD = 0.01
ADAM_STEP = 10

NN = ((1,), (0,))
NT = ((1,), (1,))
TN = ((0,), (0,))


def _cparams():
    return pltpu.CompilerParams(vmem_limit_bytes=VMEM_LIMIT)


def _tile(n, want):
    if n <= want:
        return n
    t = (want // LANES) * LANES
    while t >= LANES:
        if n % t == 0:
            return t
        t -= LANES
    return n


def _mm(name, a, b, *, grid, a_spec, b_spec, o_spec, out_shape, out_dtype, contract=NN,
        res=None, res_spec=None):
    nk = grid[-1]
    kaxis = len(grid) - 1
    acc_shape = tuple(d for d in o_spec.block_shape if d is not None)

    def body(*refs):
        a_ref, b_ref = refs[:2]
        r_ref = None if res is None else refs[2]
        o_ref = refs[2 if res is None else 3]
        part = lax.dot_general(a_ref[...].astype(BF16), b_ref[...].astype(BF16),
                               (contract, ((), ())), preferred_element_type=F32)
        if nk == 1:
            if r_ref is not None:
                part = part + r_ref[...].astype(F32)
            o_ref[...] = part.astype(o_ref.dtype)
            return
        acc = refs[-1]
        k = pl.program_id(kaxis)

        @pl.when(k == 0)
        def _():
            acc[...] = part

        @pl.when(k != 0)
        def _():
            acc[...] += part

        @pl.when(k == nk - 1)
        def _():
            r = acc[...]
            if r_ref is not None:
                r = r + r_ref[...].astype(F32)
            o_ref[...] = r.astype(o_ref.dtype)

    ins = [a, b] + ([] if res is None else [res])
    in_specs = [a_spec, b_spec] + ([] if res is None else [res_spec])
    return pl.pallas_call(
        body, name=name, grid=grid, in_specs=in_specs, out_specs=o_spec,
        out_shape=jax.ShapeDtypeStruct(out_shape, out_dtype),
        scratch_shapes=[pltpu.VMEM(acc_shape, F32)] if nk > 1 else [], compiler_params=_cparams(),
    )(*ins)


def _mm2d(name, a, b, contract, out_dtype, tm=1024, tn=1024, tk=2048, res=None):
    if contract == NN:
        (m, kk), n = a.shape, b.shape[1]
    elif contract == NT:
        (m, kk), n = a.shape, b.shape[0]
    else:
        (kk, m), n = a.shape, b.shape[1]
    tm, tn, tk = _tile(m, tm), _tile(n, tn), _tile(kk, tk)
    grid = (m // tm, n // tn, kk // tk)
    if contract == TN:
        a_spec = pl.BlockSpec((tk, tm), lambda i, j, k: (k, i))
    else:
        a_spec = pl.BlockSpec((tm, tk), lambda i, j, k: (i, k))
    if contract == NT:
        b_spec = pl.BlockSpec((tn, tk), lambda i, j, k: (j, k))
    else:
        b_spec = pl.BlockSpec((tk, tn), lambda i, j, k: (k, j))
    o_spec = pl.BlockSpec((tm, tn), lambda i, j, k: (i, j))
    res_spec = None
    if res is not None:
        if res.shape[0] == 1:
            res_spec = pl.BlockSpec((1, tn), lambda i, j, k: (0, j))
        else:
            res_spec = pl.BlockSpec((tm, tn), lambda i, j, k: (i, j))
    return _mm(name, a, b, grid=grid, a_spec=a_spec, b_spec=b_spec, o_spec=o_spec,
               out_shape=(m, n), out_dtype=out_dtype, contract=contract, res=res, res_spec=res_spec)


def _blockwise(name, fn, ins, in_specs, outs, out_specs, grid, n_acc=0, acc_all=True):
    n_in, n_out = len(ins), len(outs)
    n_plain = n_out - n_acc

    def body(*refs):
        vals = fn(*[r[...] for r in refs[:n_in]])
        if not isinstance(vals, (tuple, list)):
            vals = (vals,)
        o_refs = refs[n_in:n_in + n_out]
        for r, v in zip(o_refs[:n_plain], vals[:n_plain]):
            r[...] = v.astype(r.dtype)
        if n_acc:
            if acc_all:
                first = functools.reduce(jnp.logical_and, [pl.program_id(d) == 0 for d in range(len(grid))])
            else:
                first = pl.program_id(len(grid) - 1) == 0

            @pl.when(first)
            def _():
                for r, v in zip(o_refs[n_plain:], vals[n_plain:]):
                    r[...] = v.astype(r.dtype)

            @pl.when(jnp.logical_not(first))
            def _():
                for r, v in zip(o_refs[n_plain:], vals[n_plain:]):
                    r[...] += v.astype(r.dtype)

    return pl.pallas_call(
        body, name=name, grid=grid, in_specs=in_specs, out_specs=out_specs,
        out_shape=[jax.ShapeDtypeStruct(s, d) for s, d in outs], compiler_params=_cparams(),
    )(*ins)


def _row_spec(t, c):
    return pl.BlockSpec((t, c), lambda i: (i, 0))


def _full_spec(shape, single=False):
    nd = len(shape)
    if single:
        return pl.BlockSpec(tuple(shape), lambda *g: (0,) * nd, pipeline_mode=pl.Buffered(1))
    return pl.BlockSpec(tuple(shape), lambda *g: (0,) * nd)


def _rms(xf, w):
    return xf * lax.rsqrt(jnp.mean(xf * xf, axis=-1, keepdims=True) + RMS_EPS) * w


def _rms_bwd(xf, w, dy):
    _, vjp = jax.vjp(_rms, xf, w)
    return vjp(dy)


def _s5_disc(lr, li, ldt, bre, bim):
    dt = jnp.exp(ldt)
    mag = jnp.exp(lr * dt)
    ar = mag * jnp.cos(li * dt)
    ai = mag * jnp.sin(li * dt)
    nr, ni = ar - 1.0, ai
    den = lr * lr + li * li
    zr = (nr * lr + ni * li) / den
    zi = (ni * lr - nr * li) / den
    return ar, ai, zr * bre - zi * bim, zr * bim + zi * bre


def _s5_prep(lr, li, ldt, bre, bim):
    def body(lr_r, li_r, ldt_r, bre_r, bim_r, ar_r, ai_r, br_r, bi_r):
        ar, ai, br, bi = _s5_disc(lr_r[...], li_r[...], ldt_r[...], bre_r[...], bim_r[...])
        ar_r[...] = ar
        ai_r[...] = ai
        br_r[...] = br
        bi_r[...] = bi

    sd = jax.ShapeDtypeStruct
    return pl.pallas_call(
        body, name="s5_prep",
        out_shape=[sd(lr.shape, F32), sd(lr.shape, F32), sd(bre.shape, F32), sd(bre.shape, F32)],
        compiler_params=_cparams(),
    )(lr, li, ldt, bre, bim)


def _s5_prep_bwd(lr, li, ldt, bre, bim, dar, dai, dbr, dbi):
    def body(lr_r, li_r, ldt_r, bre_r, bim_r, dar_r, dai_r, dbr_r, dbi_r, o0, o1, o2, o3, o4):
        _, vjp = jax.vjp(_s5_disc, lr_r[...], li_r[...], ldt_r[...], bre_r[...], bim_r[...])
        g = vjp((dar_r[...], dai_r[...], dbr_r[...], dbi_r[...]))
        for o, v in zip((o0, o1, o2, o3, o4), g):
            o[...] = v

    sd = jax.ShapeDtypeStruct
    return pl.pallas_call(
        body, name="s5_prep_bwd",
        out_shape=[sd(lr.shape, F32), sd(li.shape, F32), sd(ldt.shape, F32), sd(bre.shape, F32), sd(bim.shape, F32)],
        compiler_params=_cparams(),
    )(lr, li, ldt, bre, bim, dar, dai, dbr, dbi)


SCAN_T = 256


def _scan_tables(ar, ai, tab_r, tab_i, sub, reverse):
    pr, pi = ar, ai
    for k in range(sub):
        row = sub - 1 - k if reverse else k
        tab_r[row:row + 1, :] = pr
        tab_i[row:row + 1, :] = pi
        pr, pi = ar * pr - ai * pi, ar * pi + ai * pr


def _pack_matrix(t_blk, dtype):
    sub = t_blk // SUBLANES
    dst = jnp.arange(t_blk)
    src = (dst % SUBLANES) * sub + dst // SUBLANES
    return (src[:, None] == jnp.arange(t_blk)[None, :]).astype(dtype)


def _permute_rows_f32(pm, x):
    hi = x.astype(BF16)
    r1 = x - hi.astype(F32)
    mid = r1.astype(BF16)
    lo = (r1 - mid.astype(F32)).astype(BF16)
    dot = lambda v: jnp.dot(pm, v, preferred_element_type=F32)
    return dot(hi) + dot(mid) + dot(lo)


def _scan_block(x, loc, ar, ai, st, tab_r, tab_i, sub, reverse):
    hb = STATE_BLOCK
    a8r = jnp.broadcast_to(ar, (SUBLANES, hb))
    a8i = jnp.broadcast_to(ai, (SUBLANES, hb))
    sr = jnp.zeros((SUBLANES, hb), F32)
    si = jnp.zeros((SUBLANES, hb), F32)
    steps = range(sub - 1, -1, -1) if reverse else range(sub)
    for t in steps:
        rows = slice(t * SUBLANES, (t + 1) * SUBLANES)
        sr, si = a8r * sr - a8i * si + x[rows, :hb], a8r * si + a8i * sr + x[rows, hb:]
        loc[rows, :hb] = sr
        loc[rows, hb:] = si
    cr, ci = st[0:1, :], st[1:2, :]
    far = 0 if reverse else sub - 1
    fr, fi = tab_r[far:far + 1, :], tab_i[far:far + 1, :]
    ent_r, ent_i = [None] * SUBLANES, [None] * SUBLANES
    for c in (range(SUBLANES - 1, -1, -1) if reverse else range(SUBLANES)):
        ent_r[c], ent_i[c] = cr, ci
        cr, ci = sr[c:c + 1, :] + (fr * cr - fi * ci), si[c:c + 1, :] + (fr * ci + fi * cr)
    st[0:1, :] = cr
    st[1:2, :] = ci
    c8r = jnp.concatenate(ent_r, axis=0)
    c8i = jnp.concatenate(ent_i, axis=0)
    out = []
    for t in range(sub):
        rows = slice(t * SUBLANES, (t + 1) * SUBLANES)
        tr, ti = tab_r[t:t + 1, :], tab_i[t:t + 1, :]
        out.append(jnp.concatenate([loc[rows, :hb] + (tr * c8r - ti * c8i), loc[rows, hb:] + (tr * c8i + ti * c8r)],
                                   axis=1))
    return jnp.concatenate(out, axis=0)


SSM_BLOCKS_PER_STEP = 4


def _scan_scratch(nblk, t_blk, sub, hb):
    return [pltpu.VMEM((nblk, SUBLANES, hb), F32), pltpu.VMEM((nblk, sub, hb), F32), pltpu.VMEM((nblk, sub, hb), F32),
            pltpu.VMEM((nblk, t_blk, 2 * hb), F32)]


def _ssm_fwd(proj, wb, wc, a):
    seq = proj.shape[0]
    nj = wb.shape[0]
    w2 = 2 * STATE_BLOCK
    hb = STATE_BLOCK
    t_blk = min(SCAN_T, seq)
    sub = t_blk // SUBLANES
    pm = _pack_matrix(t_blk, BF16)

    npair = SSM_BLOCKS_PER_STEP

    def body(u_ref, wb_ref, wc_ref, a_ref, pm_ref, pmt_ref, s_ref, y_ref, st, tab_r, tab_i, loc):
        coef = [(a_ref[:, b * w2:b * w2 + hb], a_ref[:, b * w2 + hb:(b + 1) * w2]) for b in range(npair)]

        @pl.when(pl.program_id(1) == 0)
        def _():
            for b, (ar, ai) in enumerate(coef):
                st[b] = jnp.zeros((SUBLANES, hb), F32)
                _scan_tables(ar, ai, tab_r.at[b], tab_i.at[b], sub, False)

        for b, (ar, ai) in enumerate(coef):
            ub = u_ref[:, b * LANES:(b + 1) * LANES].astype(BF16)
            up = jnp.dot(pm_ref[...], ub, preferred_element_type=F32).astype(BF16)
            bu = jnp.dot(up, wb_ref[b], preferred_element_type=F32)
            s = _scan_block(bu, loc.at[b], ar, ai, st.at[b], tab_r.at[b], tab_i.at[b], sub, False)
            s_ref[:, b * w2:(b + 1) * w2] = s
            yp = jnp.dot(s.astype(BF16), wc_ref[b], preferred_element_type=F32)
            y_ref[:, b * LANES:(b + 1) * LANES] = _permute_rows_f32(pmt_ref[...], yp)

    sd = jax.ShapeDtypeStruct
    return pl.pallas_call(
        body, name="ssm_fwd", grid=(nj // npair, seq // t_blk),
        in_specs=[pl.BlockSpec((t_blk, npair * LANES), lambda j, i: (i, j)),
                  pl.BlockSpec((npair, LANES, w2), lambda j, i: (j, 0, 0)),
                  pl.BlockSpec((npair, w2, LANES), lambda j, i: (j, 0, 0)),
                  pl.BlockSpec((1, npair * w2), lambda j, i: (0, j)),
                  _full_spec((t_blk, t_blk)), _full_spec((t_blk, t_blk))],
        out_specs=[pl.BlockSpec((t_blk, npair * w2), lambda j, i: (i, j)),
                   pl.BlockSpec((t_blk, npair * LANES), lambda j, i: (i, j))],
        out_shape=[sd((seq, nj * w2), F32), sd((seq, nj * LANES), F32)],
        scratch_shapes=_scan_scratch(npair, t_blk, sub, hb), compiler_params=_cparams(),
    )(proj, wb, wc, a, pm, pm.T)


def _ssm_bwd(dy, s, proj, du1, wb, wc, a):
    seq = dy.shape[0]
    nj = wb.shape[0]
    w2 = 2 * STATE_BLOCK
    hb = STATE_BLOCK
    t_blk = min(SCAN_T, seq)
    sub = t_blk // SUBLANES
    nb = seq // t_blk
    pm = _pack_matrix(t_blk, BF16)

    npair = SSM_BLOCKS_PER_STEP

    def body(dy_ref, s_ref, sprev_ref, u_ref, du1_ref, wb_ref, wc_ref, a_ref, pm_ref, pmt_ref,
             du_ref, dwb_ref, dwc_ref, da_ref, st, tab_r, tab_i, loc):
        ib = pl.program_id(1)
        pmv = pm_ref[...]
        coef = [(a_ref[:, b * w2:b * w2 + hb], -a_ref[:, b * w2 + hb:(b + 1) * w2]) for b in range(npair)]

        @pl.when(ib == 0)
        def _():
            for b, (ar, ai) in enumerate(coef):
                st[b] = jnp.zeros((SUBLANES, hb), F32)
                _scan_tables(ar, ai, tab_r.at[b], tab_i.at[b], sub, True)

        sums = []
        for b, (ar, ai) in enumerate(coef):
            cols, wide = slice(b * LANES, (b + 1) * LANES), slice(b * w2, (b + 1) * w2)
            dyp = jnp.dot(pmv, dy_ref[:, cols], preferred_element_type=F32).astype(BF16)
            up = jnp.dot(pmv, u_ref[:, cols].astype(BF16), preferred_element_type=F32).astype(BF16)
            ds = lax.dot_general(dyp, wc_ref[b], (NT, ((), ())), preferred_element_type=F32)
            lam = _scan_block(ds, loc.at[b], ar, ai, st.at[b], tab_r.at[b], tab_i.at[b], sub, True)
            lamb = lam.astype(BF16)
            du = lax.dot_general(lamb, wb_ref[b], (NT, ((), ())), preferred_element_type=F32)
            du_ref[:, cols] = (_permute_rows_f32(pmt_ref[...], du) + du1_ref[:, cols]).astype(du_ref.dtype)
            sv = s_ref[:, wide]
            dwb = lax.dot_general(up, lamb, (TN, ((), ())), preferred_element_type=F32)
            dwc = lax.dot_general(sv.astype(BF16), dyp, (TN, ((), ())), preferred_element_type=F32)

            prev_last = sprev_ref[SUBLANES - 1:SUBLANES, wide]
            prev_last = jnp.where(ib == nb - 1, jnp.zeros_like(prev_last), prev_last)
            tail = sv[t_blk - SUBLANES:, :]
            sl = lax.broadcasted_iota(jnp.int32, tail.shape, 0)
            head = jnp.where(sl >= 1, pltpu.roll(tail, 1, 0), prev_last)
            s_sh = jnp.concatenate([head, sv[:t_blk - SUBLANES, :]], axis=0)
            lam_r, lam_i = lam[:, :hb], lam[:, hb:]
            sr_, si_ = s_sh[:, :hb], s_sh[:, hb:]
            dar = jnp.sum(lam_r * sr_ + lam_i * si_, axis=0, keepdims=True)
            dai = jnp.sum(lam_i * sr_ - lam_r * si_, axis=0, keepdims=True)
            sums.append((wide, jnp.concatenate([dar, dai], axis=1), dwb, dwc))

        @pl.when(ib == 0)
        def _():
            for b, (wide, contrib, dwb, dwc) in enumerate(sums):
                da_ref[:, wide] = contrib
                dwb_ref[b] = dwb
                dwc_ref[b] = dwc

        @pl.when(ib != 0)
        def _():
            for b, (wide, contrib, dwb, dwc) in enumerate(sums):
                da_ref[:, wide] += contrib
                dwb_ref[b] += dwb
                dwc_ref[b] += dwc

    blk = lambda j, i: (nb - 1 - i, j)
    prev_blk = lambda j, i: (jnp.maximum((nb - 1 - i) * sub - 1, 0), j)
    sd = jax.ShapeDtypeStruct
    return pl.pallas_call(
        body, name="ssm_bwd", grid=(nj // npair, nb),
        in_specs=[pl.BlockSpec((t_blk, npair * LANES), blk), pl.BlockSpec((t_blk, npair * w2), blk),
                  pl.BlockSpec((SUBLANES, npair * w2), prev_blk), pl.BlockSpec((t_blk, npair * LANES), blk),
                  pl.BlockSpec((t_blk, npair * LANES), blk),
                  pl.BlockSpec((npair, LANES, w2), lambda j, i: (j, 0, 0)),
                  pl.BlockSpec((npair, w2, LANES), lambda j, i: (j, 0, 0)),
                  pl.BlockSpec((1, npair * w2), lambda j, i: (0, j)),
                  _full_spec((t_blk, t_blk)), _full_spec((t_blk, t_blk))],
        out_specs=[pl.BlockSpec((t_blk, npair * LANES), blk),
                   pl.BlockSpec((npair, LANES, w2), lambda j, i: (j, 0, 0)),
                   pl.BlockSpec((npair, w2, LANES), lambda j, i: (j, 0, 0)),
                   pl.BlockSpec((1, npair * w2), lambda j, i: (0, j))],
        out_shape=[sd((seq, nj * LANES), BF16), sd((nj, LANES, w2), F32), sd((nj, w2, LANES), F32),
                   sd((1, nj * w2), F32)],
        scratch_shapes=_scan_scratch(npair, t_blk, sub, hb), compiler_params=_cparams(),
    )(dy, s, s, proj, du1, wb, wc, a, pm, pm.T)


def _rope128(x, cos, sa, sb):
    return x * cos + pltpu.roll(x, 96, 1) * sa + pltpu.roll(x, 32, 1) * sb


def _rope128_t(dy, cos, sa, sb):
    return dy * cos + pltpu.roll(dy * sa, 32, 1) + pltpu.roll(dy * sb, 96, 1)


ATT_BQ = 512


def _probs(qn, qp, kn, kp, r0, scale):
    s = lax.dot_general(qn, kn, (NT, ((), ())), preferred_element_type=F32)
    s = s + lax.dot_general(qp, kp, (NT, ((), ())), preferred_element_type=F32)
    s = s * scale
    diag = s[:, r0:]
    row = lax.broadcasted_iota(jnp.int32, diag.shape, 0)
    col = lax.broadcasted_iota(jnp.int32, diag.shape, 1)
    diag = jnp.where(col <= row, diag, jnp.finfo(F32).min)
    s = diag if r0 == 0 else jnp.concatenate([s[:, :r0], diag], axis=1)
    m = jnp.max(s, axis=-1, keepdims=True)
    e = jnp.exp(s - m)
    return e / jnp.sum(e, axis=-1, keepdims=True)


def _attn_specs(seq):
    tab = pl.BlockSpec((seq, LANES), lambda h: (0, 0))
    return [pl.BlockSpec((None, seq, 256), lambda h: (h, 0, 0)), pl.BlockSpec((None, seq, 128), lambda h: (h, 0, 0)),
            pl.BlockSpec((None, seq, 128), lambda h: (h, 0, 1)), tab, tab, tab, tab]


def _attn_fwd(q_raw, kv, kpe, cos, sa, sb):
    nh, seq, _ = q_raw.shape
    bq = min(ATT_BQ, seq)
    scale = (QK_NOPE + QK_ROPE) ** -0.5

    def body(q_ref, kn_ref, v_ref, kp_ref, cos_ref, sa_ref, sb_ref, o_ref):
        for r0 in range(0, seq, bq):
            rows, kend = pl.ds(r0, bq), r0 + bq
            qn = q_ref[rows, :QK_NOPE].astype(BF16)
            qp = _rope128(q_ref[rows, QK_NOPE:], cos_ref[rows, :], sa_ref[rows, :], sb_ref[rows, :]).astype(BF16)
            p = _probs(qn, qp, kn_ref[:kend, :], kp_ref[:kend, :], r0, scale)
            o_ref[rows, :] = jnp.dot(p.astype(BF16), v_ref[:kend, :], preferred_element_type=F32)

    return pl.pallas_call(
        body, name="attn_fwd", grid=(nh,), in_specs=_attn_specs(seq),
        out_specs=pl.BlockSpec((seq, V_DIM), lambda h: (0, h)),
        out_shape=jax.ShapeDtypeStruct((seq, nh * V_DIM), F32), compiler_params=_cparams(),
    )(q_raw, kv, kv, kpe, cos, sa, sb)


def _attn_bwd(q_raw, kv, kpe, cos, sa, sb, do):
    nh, seq, _ = q_raw.shape
    bq = min(ATT_BQ, seq)
    scale = (QK_NOPE + QK_ROPE) ** -0.5

    def body(q_ref, kn_ref, v_ref, kp_ref, cos_ref, sa_ref, sb_ref, do_ref, dq_ref, dkv_ref, dkp_ref):
        dkv_ref[...] = jnp.zeros_like(dkv_ref)
        dkp_ref[...] = jnp.zeros_like(dkp_ref)
        for r0 in range(0, seq, bq):
            rows, kend = pl.ds(r0, bq), r0 + bq
            cos_b, sa_b, sb_b = cos_ref[rows, :], sa_ref[rows, :], sb_ref[rows, :]
            qn = q_ref[rows, :QK_NOPE].astype(BF16)
            qp = _rope128(q_ref[rows, QK_NOPE:], cos_b, sa_b, sb_b).astype(BF16)
            kn, v, kp = kn_ref[:kend, :], v_ref[:kend, :], kp_ref[:kend, :]
            p = _probs(qn, qp, kn, kp, r0, scale)
            dob = do_ref[rows, :].astype(BF16)
            dp = lax.dot_general(dob, v, (NT, ((), ())), preferred_element_type=F32)
            ds = p * (dp - jnp.sum(p * dp, axis=-1, keepdims=True)) * scale
            dsb = ds.astype(BF16)
            pb = p.astype(BF16)
            dq_ref[rows, :QK_NOPE] = jnp.dot(dsb, kn, preferred_element_type=F32).astype(dq_ref.dtype)
            dqp = jnp.dot(dsb, kp, preferred_element_type=F32)
            dq_ref[rows, QK_NOPE:] = _rope128_t(dqp, cos_b, sa_b, sb_b).astype(dq_ref.dtype)
            dkv_ref[:kend, :QK_NOPE] += lax.dot_general(dsb, qn, (TN, ((), ())), preferred_element_type=F32)
            dkv_ref[:kend, QK_NOPE:] += lax.dot_general(pb, dob, (TN, ((), ())), preferred_element_type=F32)
            dkp_ref[:kend, :] += lax.dot_general(dsb, qp, (TN, ((), ())), preferred_element_type=F32)

    sd = jax.ShapeDtypeStruct
    return pl.pallas_call(
        body, name="attn_bwd", grid=(nh,),
        in_specs=_attn_specs(seq) + [pl.BlockSpec((seq, V_DIM), lambda h: (0, h))],
        out_specs=[pl.BlockSpec((None, seq, 256), lambda h: (h, 0, 0)),
                   pl.BlockSpec((None, seq, 256), lambda h: (h, 0, 0)),
                   pl.BlockSpec((None, seq, 128), lambda h: (h, 0, 0))],
        out_shape=[sd((nh, seq, 256), BF16), sd((nh, seq, 256), F32), sd((nh, seq, 128), F32)],
        compiler_params=_cparams(),
    )(q_raw, kv, kv, kpe, cos, sa, sb, do)


def _shift_rows(a, k):
    seq = a.shape[0]
    r = pltpu.roll(a, k % seq, 0)
    rows = lax.broadcasted_iota(jnp.int32, (SUBLANES, a.shape[1]), 0)
    if k > 0:
        return jnp.concatenate([jnp.where(rows >= k, r[:SUBLANES], 0.0), r[SUBLANES:]], axis=0)
    return jnp.concatenate([r[:seq - SUBLANES], jnp.where(rows < SUBLANES + k, r[seq - SUBLANES:], 0.0)], axis=0)


def _conv3(a, w, b):
    a1 = _shift_rows(a, 1)
    a2 = _shift_rows(a, 2)
    return w[2:3] * a + w[1:2] * a1 + w[0:1] * a2 + b, a1, a2


def _conv_gate_fwd(a, cw, cb):
    half, _, seq, c = a.shape
    nc = c // LANES

    def fn(pair, wg, wv, bg, bv):
        gc, _, _ = _conv3(pair[0], wg, bg)
        vc, _, _ = _conv3(pair[1], wv, bv)
        return gc * jax.nn.sigmoid(gc) * vc

    def w_spec(off, r):
        return pl.BlockSpec((None, r, LANES), lambda k, j: (k + off, 0, j))

    return _blockwise(
        "conv_gate_fwd", fn, [a, cw, cw, cb, cb],
        [pl.BlockSpec((None, 2, seq, LANES), lambda k, j: (k, 0, 0, j)),
         w_spec(0, 3), w_spec(half, 3), w_spec(0, 1), w_spec(half, 1)],
        [((seq, half * c), BF16)], [pl.BlockSpec((seq, LANES), lambda k, j: (0, k * nc + j))],
        grid=(half, nc))[0]


def _conv_gate_bwd(a, cw, cb, dm):
    half, _, seq, c = a.shape
    nc = c // LANES

    def body(a_ref, wg_ref, wv_ref, bg_ref, bv_ref, dm_ref, da_ref, dw_ref, db_ref):
        dmv = dm_ref[...]
        ga, wg = a_ref[0], wg_ref[...]
        va, wv = a_ref[1], wv_ref[...]
        gc, g1, g2 = _conv3(ga, wg, bg_ref[...])
        vc, v1, v2 = _conv3(va, wv, bv_ref[...])
        sg = jax.nn.sigmoid(gc)
        dms = dmv * sg
        d_val = dms * gc
        d_gate = dms * vc * (1.0 + gc * (1.0 - sg))

        def back(r, dc, own, a1, a2, w):
            up1 = _shift_rows(dc, -1)
            up2 = _shift_rows(dc, -2)
            da_ref[r] = (w[2:3] * dc + w[1:2] * up1 + w[0:1] * up2).astype(da_ref.dtype)
            dw_ref[r, 0:1, :] = jnp.sum(dc * a2, axis=0, keepdims=True)
            dw_ref[r, 1:2, :] = jnp.sum(dc * a1, axis=0, keepdims=True)
            dw_ref[r, 2:3, :] = jnp.sum(dc * own, axis=0, keepdims=True)
            db_ref[r] = jnp.sum(dc, axis=0, keepdims=True)

        back(0, d_gate, ga, g1, g2, wg)
        back(1, d_val, va, v1, v2, wv)

    def w_spec(off, r):
        return pl.BlockSpec((None, r, LANES), lambda k, j: (k + off, 0, j))

    def pair_spec(r):
        return pl.BlockSpec((None, 2, r, LANES), lambda k, j: (k, 0, 0, j))

    sd = jax.ShapeDtypeStruct
    return pl.pallas_call(
        body, name="conv_gate_bwd", grid=(half, nc),
        in_specs=[pair_spec(seq), w_spec(0, 3), w_spec(half, 3), w_spec(0, 1), w_spec(half, 1),
                  pl.BlockSpec((seq, LANES), lambda k, j: (0, k * nc + j))],
        out_specs=[pair_spec(seq), pair_spec(3), pair_spec(1)],
        out_shape=[sd((half, 2, seq, c), BF16), sd((half, 2, 3, c), F32), sd((half, 2, 1, c), F32)],
        compiler_params=_cparams(),
    )(a, cw, cw, cb, cb, dm)


ROW_T = 256


def _local_step(x, positions, target, w, emit=lambda **grads: None):
    seq, d = x.shape
    t_row = min(ROW_T, seq)
    nrow = seq // t_row
    ssm_w = d // 2
    nj = ssm_w // LANES
    n_groups = ssm_w // SSM_GROUP
    nh = w["wuq"].shape[0]
    q_rank = w["wuq"].shape[1]
    kv_rank = w["wukv"].shape[1]
    ns = w["wup"].shape[0]
    c_ff = w["wup"].shape[2]
    in_pad = w["win"].shape[1]
    tm = min(1024, seq)
    nm = seq // tm
    sw = 2 * STATE_BLOCK
    g1 = (nrow,)

    lr3 = w["lam_re"].reshape(n_groups, 1, SSM_STATE)
    li3 = w["lam_im"].reshape(n_groups, 1, SSM_STATE)
    ldt3 = w["log_dt"].reshape(n_groups, 1, 1)
    bt_re = jnp.swapaxes(w["b_re"].reshape(n_groups, SSM_STATE, SSM_GROUP), 1, 2)
    bt_im = jnp.swapaxes(w["b_im"].reshape(n_groups, SSM_STATE, SSM_GROUP), 1, 2)
    abar_re, abar_im, bbt_re, bbt_im = _s5_prep(lr3, li3, ldt3, bt_re, bt_im)
    eye = jnp.eye(GROUPS_PER_BLOCK, dtype=F32)

    def blockdiag_in(bb):
        t = bb.reshape(nj, GROUPS_PER_BLOCK, SSM_GROUP, SSM_STATE)
        return jnp.einsum("jghp,gk->jghkp", t, eye).reshape(nj, LANES, STATE_BLOCK)

    def blockdiag_in_t(dwb):
        t = dwb.reshape(nj, GROUPS_PER_BLOCK, SSM_GROUP, GROUPS_PER_BLOCK, SSM_STATE)
        return jnp.einsum("jghkp,gk->jghp", t, eye).reshape(n_groups, SSM_GROUP, SSM_STATE)

    def blockdiag_out(cc):
        t = cc.reshape(nj, GROUPS_PER_BLOCK, SSM_GROUP, SSM_STATE)
        return jnp.einsum("jghp,gk->jkpgh", t, eye).reshape(nj, STATE_BLOCK, LANES)

    def blockdiag_out_t(dwc):
        t = dwc.reshape(nj, GROUPS_PER_BLOCK, SSM_STATE, GROUPS_PER_BLOCK, SSM_GROUP)
        return jnp.einsum("jkpgh,gk->jghp", t, eye).reshape(n_groups, SSM_GROUP, SSM_STATE)

    c_re = w["c_re"].reshape(n_groups, SSM_GROUP, SSM_STATE)
    c_im = w["c_im"].reshape(n_groups, SSM_GROUP, SSM_STATE)
    wb = jnp.concatenate([blockdiag_in(bbt_re), blockdiag_in(bbt_im)], axis=2).astype(BF16)
    wc = jnp.concatenate([blockdiag_out(c_re), -blockdiag_out(c_im)], axis=1).astype(BF16)
    a_lay = jnp.concatenate([abar_re.reshape(nj, 1, STATE_BLOCK), abar_im.reshape(nj, 1, STATE_BLOCK)],
                            axis=1).reshape(1, nj * sw)

    attn_w = w["attn_norm"]
    t_wide = min(2 * t_row, seq)
    g_wide = (seq // t_wide,)

    def proj_fn(xb, wv, wi):
        hb = _rms(xb, wv).astype(BF16)
        return hb, jnp.dot(hb, wi, preferred_element_type=F32)

    hn, proj = _blockwise(
        "norm1_proj", proj_fn, [x, attn_w, w["win"]],
        [_row_spec(t_wide, d), _full_spec((1, d)), _full_spec((d, in_pad), single=True)],
        [((seq, d), BF16), ((seq, in_pad), F32)], [_row_spec(t_wide, d), _row_spec(t_wide, in_pad)], g_wide)

    s_all, ylin = _ssm_fwd(proj, wb, wc, a_lay)

    def glu_fwd_fn(yl, ub, dsk, wg, bg):
        yp = yl + dsk * ub
        ygv = jax.nn.gelu(yp)
        ygb = ygv.astype(BF16)
        zb = jnp.dot(ygb, wg, preferred_element_type=F32) + bg
        return yp, ygb, zb, ygv * jax.nn.sigmoid(zb)

    wide = pl.BlockSpec((t_wide, ssm_w), lambda i: (i, 0))
    y_pre, yg, z, y_ssm = _blockwise(
        "ssm_glu_fwd", glu_fwd_fn, [ylin, proj, w["ssm_d"], w["wglu"], w["b_glu"]],
        [wide, wide, _full_spec((1, ssm_w)), _full_spec((ssm_w, ssm_w), single=True), _full_spec((1, ssm_w))],
        [((seq, ssm_w), F32), ((seq, ssm_w), BF16), ((seq, ssm_w), F32), ((seq, ssm_w), F32)], [wide] * 4, g_wide)

    cq_off, ckv_off, kpe_off = ssm_w, ssm_w + q_rank, ssm_w + q_rank + kv_rank
    assert cq_off % q_rank == 0 and ckv_off % kv_rank == 0 and kpe_off % LANES == 0
    cq_spec = pl.BlockSpec((t_row, q_rank), lambda i: (i, cq_off // q_rank))
    ckv_spec = pl.BlockSpec((t_row, kv_rank), lambda i: (i, ckv_off // kv_rank))
    kpe_spec = pl.BlockSpec((t_row, LANES), lambda i: (i, kpe_off // LANES))
    pos_b = jnp.broadcast_to(positions.astype(F32)[:, None], (seq, LANES))
    inv_freq = ROPE_THETA ** (-jnp.arange(0, QK_ROPE, 2, dtype=F32) / QK_ROPE)
    inv128 = jnp.tile(inv_freq, 4).reshape(1, LANES)

    def mla_prep_fn(cq, ckv, kp, pb, inv, wq, wkv):
        ang = pb * inv
        lane = lax.broadcasted_iota(jnp.int32, ang.shape, 1)
        cs, sn = jnp.cos(ang), jnp.sin(ang)
        cos = jnp.where(lane < QK_ROPE, cs, 0.0)
        sa = jnp.where(lane < QK_ROPE // 2, -sn, 0.0)
        sb = jnp.where(jnp.logical_and(lane >= QK_ROPE // 2, lane < QK_ROPE), sn, 0.0)
        return _rms(cq, wq), _rms(ckv, wkv), _rope128(kp, cos, sa, sb), cos, sa, sb

    qn, kvn, kpe, cos_t, sa_t, sb_t = _blockwise(
        "mla_prep", mla_prep_fn, [proj, proj, proj, pos_b, inv128, w["q_norm"], w["kv_norm"]],
        [cq_spec, ckv_spec, kpe_spec, _row_spec(t_row, LANES),
         _full_spec((1, LANES)), _full_spec((1, q_rank)), _full_spec((1, kv_rank))],
        [((seq, q_rank), BF16), ((seq, kv_rank), BF16), ((seq, LANES), BF16)] + [((seq, LANES), F32)] * 3,
        [_row_spec(t_row, q_rank), _row_spec(t_row, kv_rank)] + [_row_spec(t_row, LANES)] * 4, g1)

    def head_mm(name, act, wh, out_dtype):
        kdim, ndim = wh.shape[1], wh.shape[2]
        return _mm(name, act, wh, grid=(nh, 1, 1),
                   a_spec=pl.BlockSpec((seq, kdim), lambda h, i, k: (i, 0)),
                   b_spec=pl.BlockSpec((None, kdim, ndim), lambda h, i, k: (h, 0, 0)),
                   o_spec=pl.BlockSpec((None, seq, ndim), lambda h, i, k: (h, i, 0)),
                   out_shape=(nh, seq, ndim), out_dtype=out_dtype)

    q_raw = head_mm("mla_q", qn, w["wuq"], F32)
    kv = head_mm("mla_kv", kvn, w["wukv"], BF16)
    y_mla = _attn_fwd(q_raw, kv, kpe, cos_t, sa_t, sb_t)
    mla_w = nh * V_DIM

    def out_proj_fn(ys, ym, ws, wm, wo, xb, wf):
        yc = jnp.concatenate([_rms(ys, ws), _rms(ym, wm)], axis=1).astype(BF16)
        hb = xb + jnp.dot(yc, wo, preferred_element_type=F32)
        return yc, hb, _rms(hb, wf)

    ycat, h1, hn2 = _blockwise(
        "out_norm_proj", out_proj_fn, [y_ssm, y_mla, w["son"], w["mon"], w["wout"], x, w["ffn_norm"]],
        [wide, _row_spec(t_wide, mla_w), _full_spec((1, ssm_w)), _full_spec((1, mla_w)),
         _full_spec((d, d), single=True), _row_spec(t_wide, d), _full_spec((1, d))],
        [((seq, d), BF16), ((seq, d), F32), ((seq, d), BF16)], [_row_spec(t_wide, d)] * 3, g_wide)

    tku = d
    half = ns // 2
    a_ff = _mm("ffn_up", hn2, w["wup"], grid=(ns, nm, d // tku),
               a_spec=pl.BlockSpec((tm, tku), lambda s, i, k: (i, k)),
               b_spec=pl.BlockSpec((None, tku, c_ff), lambda s, i, k: (s, k, 0)),
               o_spec=pl.BlockSpec((None, None, tm, c_ff), lambda s, i, k: (s % half, s // half, i, 0)),
               out_shape=(half, 2, seq, c_ff), out_dtype=F32)
    cb3 = w["conv_b"].reshape(ns, 1, c_ff)
    m_ff = _conv_gate_fwd(a_ff, w["conv_w"], cb3)
    d_ff = half * c_ff
    wdn = w["wdown"]
    tnd = _tile(d, 1024)
    tmx, tnx = min(1024, seq), _tile(d, 1024)
    h2 = _mm2d("ffn_down", m_ff, wdn, NN, F32, tm=512, tn=512, tk=d_ff, res=h1)

    def loss_fn(hb, tb, wv):
        def f(hh, ww):
            err = _rms(hh, ww) - tb
            return 0.5 * jnp.sum(jnp.mean(err * err, axis=-1))

        lossv, (dh, dw) = jax.value_and_grad(f, argnums=(0, 1))(hb, wv)
        return dh, dh, jnp.full((1, LANES), lossv, F32), dw

    fin_w = w["final_norm"].reshape(1, d)
    dh2, dh2b, loss_acc, g_final = _blockwise(
        "loss_head", loss_fn, [h2, target, fin_w], [_row_spec(t_row, d), _row_spec(t_row, d), _full_spec((1, d))],
        [((seq, d), F32), ((seq, d), BF16), ((1, LANES), F32), ((1, d), F32)],
        [_row_spec(t_row, d), _row_spec(t_row, d), _full_spec((1, LANES)), _full_spec((1, d))], g1, n_acc=2)
    loss = loss_acc

    dm = _mm2d("ffn_down_dx", dh2b, wdn, NT, F32, tn=c_ff)
    tks = seq
    g_wdown = _mm2d("ffn_down_dw", m_ff, dh2b, TN, BF16, tm=c_ff)
    emit(wdown=g_wdown)
    da_ff, g_convw2, g_convb2 = _conv_gate_bwd(a_ff, w["conv_w"], cb3, dm)
    g_convw = jnp.swapaxes(g_convw2, 0, 1).reshape(ns, 3, c_ff)
    g_convb = jnp.swapaxes(g_convb2, 0, 1).reshape(ns, 1, c_ff)
    g_wup = _mm("ffn_up_dw", hn2, da_ff, grid=(ns, d // tnd, seq // tks), contract=TN,
                a_spec=pl.BlockSpec((tks, tnd), lambda s, j, k: (k, j)),
                b_spec=pl.BlockSpec((None, None, tks, c_ff), lambda s, j, k: (s % half, s // half, k, 0)),
                o_spec=pl.BlockSpec((None, tnd, c_ff), lambda s, j, k: (s, j, 0)),
                out_shape=(ns, d, c_ff), out_dtype=BF16)
    emit(wup=g_wup)
    dhn2 = _mm("ffn_up_dx", da_ff, w["wup"], grid=(seq // tmx, d // tnx, ns), contract=NT,
               a_spec=pl.BlockSpec((None, None, tmx, c_ff), lambda i, j, s: (s % half, s // half, i, 0)),
               b_spec=pl.BlockSpec((None, tnx, c_ff), lambda i, j, s: (s, j, 0)),
               o_spec=pl.BlockSpec((tmx, tnx), lambda i, j, s: (i, j)),
               out_shape=(seq, d), out_dtype=F32)
    emit(wup_pair_sums_after=dhn2)

    def norm_bwd_fn(hb, dres, dn, wv):
        dx_, dw_ = _rms_bwd(hb, wv, dn)
        dtot = dres + dx_
        return dtot, dtot, dw_

    dh1, dh1b, g_ffn_norm = _blockwise(
        "norm2_bwd", norm_bwd_fn, [h1, dh2, dhn2, w["ffn_norm"]],
        [_row_spec(t_row, d)] * 3 + [_full_spec((1, d))],
        [((seq, d), F32), ((seq, d), BF16), ((1, d), F32)],
        [_row_spec(t_row, d), _row_spec(t_row, d), _full_spec((1, d))], g1, n_acc=1)

    g_wout = _mm2d("out_proj_dw", ycat, dh1b, TN, BF16)

    def outnorm_bwd_fn(dhb, wo, ys, ym, ws, wm):
        dyc = lax.dot_general(dhb, wo, (NT, ((), ())), preferred_element_type=F32)
        dys, dws = _rms_bwd(ys, ws, dyc[:, :ssm_w])
        dym, dwm = _rms_bwd(ym, wm, dyc[:, ssm_w:])
        return dys, dym, dws, dwm

    dy_ssm, dy_mla, g_son, g_mon = _blockwise(
        "out_proj_dx_norm_bwd", outnorm_bwd_fn, [dh1b, w["wout"], y_ssm, y_mla, w["son"], w["mon"]],
        [_row_spec(t_wide, d), _full_spec((d, d), single=True), wide, _row_spec(t_wide, mla_w),
         _full_spec((1, ssm_w)), _full_spec((1, mla_w))],
        [((seq, ssm_w), F32), ((seq, mla_w), F32), ((1, ssm_w), F32), ((1, mla_w), F32)],
        [wide, _row_spec(t_wide, mla_w), _full_spec((1, ssm_w)), _full_spec((1, mla_w))],
        g_wide, n_acc=2)

    def glu_bwd_fn(dy, yp, zb, ub, dsk, wg):
        ygv = jax.nn.gelu(yp)
        sg = jax.nn.sigmoid(zb)
        dz = dy * ygv * sg * (1.0 - sg)
        dzb = dz.astype(BF16)
        dyg = dy * sg + lax.dot_general(dzb, wg, (NT, ((), ())), preferred_element_type=F32)
        _, vjp = jax.vjp(jax.nn.gelu, yp)
        dyp = vjp(dyg)[0]
        return (dzb, dyp, dyp * dsk, jnp.sum(dz, axis=0, keepdims=True), jnp.sum(dyp * ub, axis=0, keepdims=True))

    dz, dy_pre, du1, g_bglu, g_ssmd = _blockwise(
        "ssm_glu_bwd", glu_bwd_fn, [dy_ssm, y_pre, z, proj, w["ssm_d"], w["wglu"]],
        [wide] * 4 + [_full_spec((1, ssm_w)), _full_spec((ssm_w, ssm_w), single=True)],
        [((seq, ssm_w), BF16), ((seq, ssm_w), BF16), ((seq, ssm_w), F32), ((1, ssm_w), F32), ((1, ssm_w), F32)],
        [wide] * 3 + [_full_spec((1, ssm_w))] * 2, g_wide, n_acc=2)
    g_wglu = _mm2d("ssm_glu_dw", yg, dz, TN, BF16)
    dq_raw, dkv, dkp_h = _attn_bwd(q_raw, kv, kpe, cos_t, sa_t, sb_t, dy_mla)

    def head_mm_dx(name, dact, wh):
        kdim, ndim = wh.shape[1], wh.shape[2]
        return _mm(name, dact, wh, grid=(1, 1, nh), contract=NT,
                   a_spec=pl.BlockSpec((None, seq, ndim), lambda i, j, h: (h, i, 0)),
                   b_spec=pl.BlockSpec((None, kdim, ndim), lambda i, j, h: (h, 0, 0)),
                   o_spec=pl.BlockSpec((seq, kdim), lambda i, j, h: (i, 0)),
                   out_shape=(seq, kdim), out_dtype=F32)

    def head_mm_dw(name, act, dact):
        kdim, ndim = act.shape[1], dact.shape[2]
        return _mm(name, act, dact, grid=(nh, 1, seq // tks), contract=TN,
                   a_spec=pl.BlockSpec((tks, kdim), lambda h, j, k: (k, 0)),
                   b_spec=pl.BlockSpec((None, tks, ndim), lambda h, j, k: (h, k, 0)),
                   o_spec=pl.BlockSpec((None, kdim, ndim), lambda h, j, k: (h, 0, 0)),
                   out_shape=(nh, kdim, ndim), out_dtype=BF16)

    g_wuq = head_mm_dw("mla_q_dw", qn, dq_raw)
    g_wukv = head_mm_dw("mla_kv_dw", kvn, dkv)
    dqn = head_mm_dx("mla_q_dx", dq_raw, w["wuq"])
    dkvn = head_mm_dx("mla_kv_dx", dkv, w["wukv"])
    emit(not_before=(dqn, dkvn, dy_pre), wout=g_wout, wuq=g_wuq, wukv=g_wukv, wglu=g_wglu, conv_w=g_convw)

    du, dwb, dwc, da_lay = _ssm_bwd(dy_pre, s_all, proj, du1, wb, wc, a_lay)
    g_c_re = blockdiag_out_t(dwc[:, :STATE_BLOCK, :])
    g_c_im = -blockdiag_out_t(dwc[:, STATE_BLOCK:, :])
    dbbt_re = blockdiag_in_t(dwb[:, :, :STATE_BLOCK])
    dbbt_im = blockdiag_in_t(dwb[:, :, STATE_BLOCK:])
    da3 = da_lay.reshape(nj, 2, STATE_BLOCK)
    dabar_re = da3[:, 0, :].reshape(n_groups, 1, SSM_STATE)
    dabar_im = da3[:, 1, :].reshape(n_groups, 1, SSM_STATE)
    g_lr3, g_li3, g_ldt3, g_bt_re, g_bt_im = _s5_prep_bwd(lr3, li3, ldt3, bt_re, bt_im,
                                                           dabar_re, dabar_im, dbbt_re, dbbt_im)

    def mla_prep_bwd_fn(cq, ckv, dqn_b, dkvn_b, dkp_b, cos, sa, sb, wq, wkv):
        dcq, dwq = _rms_bwd(cq, wq, dqn_b)
        dckv, dwkv = _rms_bwd(ckv, wkv, dkvn_b)
        dkp_sum = dkp_b[0]
        for h in range(1, nh):
            dkp_sum = dkp_sum + dkp_b[h]
        return dcq, dckv, _rope128_t(dkp_sum, cos, sa, sb), dwq, dwkv

    dc_q, dc_kv, dkpe_raw, g_qnorm, g_kvnorm = _blockwise(
        "mla_prep_bwd", mla_prep_bwd_fn, [proj, proj, dqn, dkvn, dkp_h, cos_t, sa_t, sb_t, w["q_norm"], w["kv_norm"]],
        [cq_spec, ckv_spec, _row_spec(t_row, q_rank), _row_spec(t_row, kv_rank),
         pl.BlockSpec((nh, t_row, LANES), lambda i: (0, i, 0))] + [_row_spec(t_row, LANES)] * 3
        + [_full_spec((1, q_rank)), _full_spec((1, kv_rank))],
        [((seq, q_rank), BF16), ((seq, kv_rank), BF16), ((seq, LANES), BF16), ((1, q_rank), F32), ((1, kv_rank), F32)],
        [_row_spec(t_row, q_rank), _row_spec(t_row, kv_rank), _row_spec(t_row, LANES), _full_spec((1, q_rank)),
         _full_spec((1, kv_rank))], g1, n_acc=2)

    dproj = jnp.concatenate([du, dc_q, dc_kv, dkpe_raw], axis=1)
    g_win = _mm2d("proj_dw", hn, dproj, TN, BF16, tn=640)
    emit(win=g_win)
    def norm1_bwd_fn(dpb, wi, xb, dres, wv):
        dn = lax.dot_general(dpb, wi, (NT, ((), ())), preferred_element_type=F32)
        dx_, dw_ = _rms_bwd(xb, wv, dn)
        return dres + dx_, dw_

    grad_x, g_attn_norm = _blockwise(
        "proj_dx_norm1_bwd", norm1_bwd_fn, [dproj, w["win"], x, dh1, attn_w],
        [_row_spec(t_row, in_pad), _full_spec((d, in_pad), single=True), _row_spec(t_row, d), _row_spec(t_row, d), _full_spec((1, d))],
        [((seq, d), F32), ((1, d), F32)], [_row_spec(t_row, d), _full_spec((1, d))], g1, n_acc=1)
    emit(win_pair_sums_after=grad_x)

    grads = dict(
        attn_norm=g_attn_norm, win=g_win, lam_re=g_lr3, lam_im=g_li3, log_dt=g_ldt3,
        bt_re=g_bt_re, bt_im=g_bt_im, c_re=g_c_re, c_im=g_c_im,
        ssm_d=g_ssmd, wglu=g_wglu, b_glu=g_bglu, q_norm=g_qnorm, wuq=g_wuq, kv_norm=g_kvnorm, wukv=g_wukv,
        son=g_son, mon=g_mon, wout=g_wout, ffn_norm=g_ffn_norm, wup=g_wup, conv_w=g_convw, conv_b=g_convb,
        wdown=g_wdown, final_norm=g_final)
    return loss, grad_x, grads


def _mesh_pos():
    return lax.axis_index("x"), lax.axis_index("y"), lax.axis_index("c")


def _handshake_all():
    x, y, c = _mesh_pos()
    barrier = pltpu.get_barrier_semaphore()
    for k in range(1, N_DEV):
        peer = (1 - x if k & 4 else x, 1 - y if k & 2 else y, 1 - c if k & 1 else c)
        pl.semaphore_signal(barrier, inc=1, device_id=peer, device_id_type=MESH)
    pl.semaphore_wait(barrier, N_DEV - 1)


def _handshake(peers):
    barrier = pltpu.get_barrier_semaphore()
    for peer in peers:
        pl.semaphore_signal(barrier, inc=1, device_id=peer, device_id_type=MESH)
    pl.semaphore_wait(barrier, len(peers))


def _comm_call(name, body, n, out_shape, ins, collective_id, after=None, copies=7, n_remote=None, n_local=None):
    n_remote = copies * n if n_remote is None else n_remote
    sems = [pltpu.SemaphoreType.DMA((n_remote,)), pltpu.SemaphoreType.DMA((n_remote,)),
            pltpu.SemaphoreType.DMA((n if n_local is None else n_local,))]
    if collective_id is None:
        any_spec = pl.BlockSpec(memory_space=pl.ANY)
        return pl.pallas_call(body, name=name, out_shape=out_shape, in_specs=[any_spec] * n,
                              out_specs=[any_spec] * n, scratch_shapes=sems)(*ins)
    seq_body = body
    if after:
        n_after = len(after)
        ins = list(ins) + list(after)

        def seq_body(*refs):
            body(*refs[:n], *refs[n + n_after:])

    return pl.kernel(seq_body, name=name, out_type=out_shape,
                     mesh=plsc.ScalarSubcoreMesh(axis_name="seq", num_cores=1), scratch_types=sems,
                     compiler_params=pltpu.CompilerParams(collective_id=collective_id))(*ins)


def _all_gather(name, xs, collective_id=None, after=None, pair_sums=()):
    n = len(xs)
    nh = len(pair_sums)
    m = n + nh

    def body(*refs):
        x_refs, h_refs, o_refs, e_refs = refs[:n], refs[n:m], refs[m:m + n], refs[m + n:2 * m]
        send_sems, recv_sems, local_sems = refs[2 * m:]
        if collective_id is not None:
            _handshake_all()
        finish_pairs = _chip_copies(h_refs, e_refs, send_sems, recv_sems, local_sems, 7 * n, n) if nh else None
        x, y, c = _mesh_pos()
        me, sibling = (x, y, c), (x, y, 1 - c)
        chips = [(1 - x, y), (x, 1 - y), (1 - x, 1 - y)]

        def slot(o_ref, px, py, pc):
            return o_ref.at[4 * px + 2 * py + pc]

        def copy(t, k, block, to, src=None):
            dst = slot(o_refs[t], *block)
            return pltpu.make_async_remote_copy(
                src_ref=dst if src is None else src, dst_ref=dst,
                send_sem=send_sems.at[7 * t + k], recv_sem=recv_sems.at[7 * t + k],
                device_id=to, device_id_type=MESH)

        started = []
        for t in range(n):
            mine = pltpu.make_async_copy(x_refs[t], slot(o_refs[t], *me), local_sems.at[t])
            mine.start()
            started.append(mine)
        first = []
        for t in range(n):
            first.append(copy(t, 0, me, sibling, src=x_refs[t]))
            first += [copy(t, 1 + j, me, (*chip, c), src=x_refs[t]) for j, chip in enumerate(chips)]
        for cp in first:
            cp.start()
        passed = []
        for j, chip in enumerate(chips):
            for t in range(n):
                copy(t, 1 + j, (*chip, c), me).wait_recv()
                fwd = copy(t, 4 + j, (*chip, c), sibling)
                fwd.start()
                passed.append(fwd)
        for t in range(n):
            copy(t, 0, sibling, me).wait_recv()
            for j, chip in enumerate(chips):
                copy(t, 4 + j, (*chip, 1 - c), me).wait_recv()
        for cp in first + passed:
            cp.wait_send()
        for mine in started:
            mine.wait()
        if nh:
            finish_pairs()

    out_shape = ([jax.ShapeDtypeStruct((N_DEV,) + v.shape, v.dtype) for v in xs]
                 + [jax.ShapeDtypeStruct(v.shape, v.dtype) for v in pair_sums])
    return _comm_call(name, body, m, out_shape, list(xs) + list(pair_sums), collective_id, after,
                      n_remote=7 * n + (N_CHIP - 1) * nh, n_local=m)


def _exchange_partials(name, gs, collective_id=None, after=None):
    n = len(gs)

    def body(*refs):
        g_refs, o_refs = refs[:n], refs[n:2 * n]
        send_sems, recv_sems, local_sems = refs[2 * n:]
        if collective_id is not None:
            _handshake_all()
        x, y, c = _mesh_pos()
        me_idx = 4 * x + 2 * y + c
        copies = []
        for t in range(n):
            mine = pltpu.make_async_copy(g_refs[t].at[me_idx], o_refs[t].at[me_idx], local_sems.at[t])
            mine.start()
            copies.append(mine)
        remote = []
        for k in range(1, N_DEV):
            px = 1 - x if k & 4 else x
            py = 1 - y if k & 2 else y
            pc = 1 - c if k & 1 else c
            p_idx = 4 * px + 2 * py + pc
            for t in range(n):
                cp = pltpu.make_async_remote_copy(
                    src_ref=g_refs[t].at[p_idx], dst_ref=o_refs[t].at[me_idx],
                    send_sem=send_sems.at[7 * t + k - 1], recv_sem=recv_sems.at[7 * t + k - 1],
                    device_id=(px, py, pc), device_id_type=MESH)
                cp.start()
                landing = pltpu.make_async_remote_copy(
                    src_ref=g_refs[t].at[p_idx], dst_ref=o_refs[t].at[p_idx],
                    send_sem=send_sems.at[7 * t + k - 1], recv_sem=recv_sems.at[7 * t + k - 1],
                    device_id=(px, py, pc), device_id_type=MESH)
                remote.append((cp, landing))
        for cp, landing in remote:
            landing.wait_recv()
        for cp, landing in remote:
            cp.wait_send()
        for mine in copies:
            mine.wait()

    out_shape = [jax.ShapeDtypeStruct(v.shape, v.dtype) for v in gs]
    return _comm_call(name, body, n, out_shape, gs, collective_id, after)


N_CHIP = N_DEV // 2
PAIR_ADD_BLOCK_ELEMS = 1024 * 1024


def _pair_swap(name, gs, collective_id, after=None):
    n = len(gs)

    def body(*refs):
        g_refs, o_refs = refs[:n], refs[n:2 * n]
        send_sems, recv_sems, _ = refs[2 * n:]
        x, y, c = _mesh_pos()
        sibling = (x, y, 1 - c)
        _handshake([sibling])
        copies = []
        for t in range(n):
            for k in range(N_CHIP):
                copies.append(pltpu.make_async_remote_copy(
                    src_ref=g_refs[t].at[2 * k + 1 - c], dst_ref=o_refs[t].at[k],
                    send_sem=send_sems.at[N_CHIP * t + k], recv_sem=recv_sems.at[N_CHIP * t + k],
                    device_id=sibling, device_id_type=MESH))
        for cp in copies:
            cp.start()
        for cp in copies:
            cp.wait_recv()
        for cp in copies:
            cp.wait_send()

    out_shape = [jax.ShapeDtypeStruct((N_CHIP,) + v.shape[1:], v.dtype) for v in gs]
    return _comm_call(name, body, n, out_shape, gs, collective_id, after, copies=N_CHIP)


def _pair_add(name, g, got):
    _, r, c = g.shape
    tr = r
    if r * c > PAIR_ADD_BLOCK_ELEMS and r % SUBLANES == 0:
        tr = SUBLANES
        while r % (tr * 2) == 0 and tr * 2 * c <= PAIR_ADD_BLOCK_ELEMS:
            tr *= 2

    def body(core_ref, g_ref, got_ref, o_ref):
        o_ref[...] = (g_ref[...].astype(F32) + got_ref[...].astype(F32)).astype(o_ref.dtype)

    grid_spec = pltpu.PrefetchScalarGridSpec(
        num_scalar_prefetch=1, grid=(N_CHIP, r // tr),
        in_specs=[pl.BlockSpec((None, None, tr, c), lambda k, i, core: (k, core[0], i, 0)),
                  pl.BlockSpec((None, tr, c), lambda k, i, core: (k, i, 0))],
        out_specs=pl.BlockSpec((None, tr, c), lambda k, i, core: (k, i, 0)))
    core = lax.axis_index("c").astype(jnp.int32).reshape(1)
    return pl.pallas_call(body, name=name, grid_spec=grid_spec, out_shape=jax.ShapeDtypeStruct((N_CHIP, r, c), g.dtype),
                          compiler_params=_cparams())(core, g.reshape(N_CHIP, 2, r, c), got)


def _chip_copies(h_refs, o_refs, send_sems, recv_sems, local_sems, sem0, local0):
    n = len(h_refs)
    per = N_CHIP - 1
    x, y, c = _mesh_pos()
    others = [(1 - x if k & 2 else x, 1 - y if k & 1 else y) for k in range(1, N_CHIP)]
    my_chip = 2 * x + y
    local = []
    for t in range(n):
        mine = pltpu.make_async_copy(h_refs[t].at[my_chip], o_refs[t].at[my_chip], local_sems.at[local0 + t])
        mine.start()
        local.append(mine)
    remote = []
    for j, (px, py) in enumerate(others):
        chip = 2 * px + py
        for t in range(n):
            sems = dict(send_sem=send_sems.at[sem0 + per * t + j], recv_sem=recv_sems.at[sem0 + per * t + j],
                        device_id=(px, py, c), device_id_type=MESH)
            cp = pltpu.make_async_remote_copy(src_ref=h_refs[t].at[chip], dst_ref=o_refs[t].at[my_chip], **sems)
            cp.start()
            landing = pltpu.make_async_remote_copy(src_ref=h_refs[t].at[chip], dst_ref=o_refs[t].at[chip], **sems)
            remote.append((cp, landing))

    def finish():
        for cp, landing in remote:
            landing.wait_recv()
        for cp, landing in remote:
            cp.wait_send()
        for mine in local:
            mine.wait()

    return finish


def _chip_exchange(name, hs, collective_id, after=None):
    n = len(hs)
    per = N_CHIP - 1

    def body(*refs):
        h_refs, o_refs = refs[:n], refs[n:2 * n]
        send_sems, recv_sems, local_sems = refs[2 * n:]
        x, y, c = _mesh_pos()
        _handshake([(1 - x if k & 2 else x, 1 - y if k & 1 else y, c) for k in range(1, N_CHIP)])
        _chip_copies(h_refs, o_refs, send_sems, recv_sems, local_sems, 0, 0)()

    out_shape = [jax.ShapeDtypeStruct(v.shape, v.dtype) for v in hs]
    return _comm_call(name, body, n, out_shape, hs, collective_id, after, copies=per)


ADAM_BLOCK_ELEMS = 256 * 1024


def _sum_parts(pb):
    g = pb[0].astype(F32)
    for j in range(1, pb.shape[0]):
        g = g + pb[j].astype(F32)
    return g


def _adam_math(g, wb_, mb, vb):
    m_new = ADAM_B1 * mb + (1.0 - ADAM_B1) * g
    v_new = ADAM_B2 * vb + (1.0 - ADAM_B2) * (g * g)
    m_hat = m_new / (1.0 - ADAM_B1 ** ADAM_STEP)
    v_hat = v_new / (1.0 - ADAM_B2 ** ADAM_STEP)
    delta = -ADAM_LR * (m_hat / (jnp.sqrt(v_hat) + ADAM_EPS) + ADAM_WD * wb_)
    return g, delta, m_new, v_new


def _adamw_multi(name, items, nblk=1, packed=None):
    n = len(items)

    def spec(shape, lead):
        blk = list(shape)
        blk[lead + 1] = shape[lead + 1] // nblk
        if nblk == 1:
            return pl.BlockSpec(tuple(blk), lambda i, nd=len(shape): (0,) * nd)
        return pl.BlockSpec(tuple(blk), lambda i, nd=len(shape), ax=lead + 1: (0,) * ax + (i,) + (0,) * (nd - ax - 1))

    ins, in_specs, out_specs, out_shape, where = [], [], [], [], []
    if packed is not None:
        ins.append(packed)
        in_specs.append(spec(packed.shape, 1))
    for parts, wv, mv, vv in items:
        if isinstance(parts, int):
            where.append((0, parts, len(ins)))
        else:
            assert parts.shape[1:] == wv.shape, (name, parts.shape, wv.shape)
            where.append((len(ins), None, len(ins) + 1))
            ins.append(parts)
            in_specs.append(spec(parts.shape, 1))
        ins += [wv, mv, vv]
        in_specs += [spec(wv.shape, 0)] * 3
        out_specs += [spec(wv.shape, 0)] * 4
        out_shape += [jax.ShapeDtypeStruct(wv.shape, F32)] * 4
    n_in = len(ins)

    def body(*refs):
        for t, (ip, off, iw) in enumerate(where):
            wr, mr, vr = refs[iw:iw + 3]
            parts = refs[ip][...] if off is None else refs[ip][:, :, off:off + wr.shape[-1]]
            res = _adam_math(_sum_parts(parts), wr[...], mr[...], vr[...])
            for o, val in zip(refs[n_in + 4 * t:n_in + 4 * t + 4], res):
                o[...] = val

    res = pl.pallas_call(body, name=name, grid=(nblk,), in_specs=in_specs, out_specs=out_specs, out_shape=out_shape,
                         compiler_params=_cparams())(*ins)
    return [tuple(res[4 * t:4 * t + 4]) for t in range(n)]


def _sum_multi(name, parts_list):
    def body(*refs):
        for pr, o in zip(refs[:len(parts_list)], refs[len(parts_list):]):
            o[...] = _sum_parts(pr[...])

    return pl.pallas_call(body, name=name, out_shape=[jax.ShapeDtypeStruct(p.shape[1:], F32) for p in parts_list],
                          compiler_params=_cparams())(*parts_list)


def _adamw_sum(name, parts, wv, mv, vv):
    npart, r, c = parts.shape
    tr = r
    if r * c > ADAM_BLOCK_ELEMS and r % SUBLANES == 0:
        tr = SUBLANES
        while r % (tr * 2) == 0 and tr * 2 * c <= ADAM_BLOCK_ELEMS:
            tr *= 2

    def fn(pb, wb_, mb, vb):
        return _adam_math(_sum_parts(pb), wb_, mb, vb)

    row = pl.BlockSpec((tr, c), lambda i: (i, 0))
    return _blockwise(name, fn, [parts, wv, mv, vv],
                      [pl.BlockSpec((npart, tr, c), lambda i: (0, i, 0)), row, row, row],
                      [((r, c), F32)] * 4, [row] * 4, (r // tr,))


_VECTORS = ["attn_norm", "lam_re", "lam_im", "log_dt", "ssm_d", "b_glu", "q_norm", "kv_norm", "son", "mon",
            "ffn_norm", "conv_b", "final_norm"]
_GHP = ["c_re", "c_im", "bt_re", "bt_im"]
_PACKED = ["attn_norm", "ssm_d", "b_glu", "q_norm", "kv_norm", "son", "mon", "ffn_norm", "conv_b", "final_norm"]
_BIG = ["win", "wglu", "wuq", "wukv", "wout", "wup", "wdown", "conv_w"]
_ROWS_IN_LANES = ("win", "wuq")
_TWO_LEVEL = ("wup", "win")
_AFTER = "_pair_sums_after"
_ORDER = ["attn_norm", "win", "lam_re", "lam_im", "log_dt", "b_re", "b_im", "c_re", "c_im", "ssm_d", "wglu",
          "b_glu", "q_norm", "wuq", "kv_norm", "wukv", "son", "mon", "wout", "ffn_norm", "wup", "conv_w",
          "conv_b", "wdown", "final_norm"]


def kernel(x, positions, attn_norm_w, w_in, ssm_lambda_re, ssm_lambda_im, ssm_log_dt, ssm_b_re, ssm_b_im, ssm_c_re, ssm_c_im, ssm_d, ssm_w_glu, ssm_b_glu, mla_q_norm_w, mla_w_uq, mla_kv_norm_w, mla_w_ukv, ssm_out_norm_w, mla_out_norm_w, w_out, ffn_norm_w, ffn_w_up, ffn_conv_w, ffn_conv_b, ffn_w_down, final_norm_w, loss_target, m_attn_norm_w, m_w_in, m_ssm_lambda_re, m_ssm_lambda_im, m_ssm_log_dt, m_ssm_b_re, m_ssm_b_im, m_ssm_c_re, m_ssm_c_im, m_ssm_d, m_ssm_w_glu, m_ssm_b_glu, m_mla_q_norm_w, m_mla_w_uq, m_mla_kv_norm_w, m_mla_w_ukv, m_ssm_out_norm_w, m_mla_out_norm_w, m_w_out, m_ffn_norm_w, m_ffn_w_up, m_ffn_conv_w, m_ffn_conv_b, m_ffn_w_down, m_final_norm_w, v_attn_norm_w, v_w_in, v_ssm_lambda_re, v_ssm_lambda_im, v_ssm_log_dt, v_ssm_b_re, v_ssm_b_im, v_ssm_c_re, v_ssm_c_im, v_ssm_d, v_ssm_w_glu, v_ssm_b_glu, v_mla_q_norm_w, v_mla_w_uq, v_mla_kv_norm_w, v_mla_w_ukv, v_ssm_out_norm_w, v_mla_out_norm_w, v_w_out, v_ffn_norm_w, v_ffn_w_up, v_ffn_conv_w, v_ffn_conv_b, v_ffn_w_down, v_final_norm_w):
    wts = dict(attn_norm=attn_norm_w, win=w_in, lam_re=ssm_lambda_re, lam_im=ssm_lambda_im, log_dt=ssm_log_dt,
               b_re=ssm_b_re, b_im=ssm_b_im, c_re=ssm_c_re, c_im=ssm_c_im, ssm_d=ssm_d, wglu=ssm_w_glu,
               b_glu=ssm_b_glu, q_norm=mla_q_norm_w, wuq=mla_w_uq, kv_norm=mla_kv_norm_w, wukv=mla_w_ukv,
               son=ssm_out_norm_w, mon=mla_out_norm_w, wout=w_out, ffn_norm=ffn_norm_w, wup=ffn_w_up,
               conv_w=ffn_conv_w, conv_b=ffn_conv_b, wdown=ffn_w_down, final_norm=final_norm_w)
    moms = dict(zip(_ORDER, [m_attn_norm_w, m_w_in, m_ssm_lambda_re, m_ssm_lambda_im, m_ssm_log_dt, m_ssm_b_re,
                             m_ssm_b_im, m_ssm_c_re, m_ssm_c_im, m_ssm_d, m_ssm_w_glu, m_ssm_b_glu, m_mla_q_norm_w,
                             m_mla_w_uq, m_mla_kv_norm_w, m_mla_w_ukv, m_ssm_out_norm_w, m_mla_out_norm_w, m_w_out,
                             m_ffn_norm_w, m_ffn_w_up, m_ffn_conv_w, m_ffn_conv_b, m_ffn_w_down, m_final_norm_w]))
    vels = dict(zip(_ORDER, [v_attn_norm_w, v_w_in, v_ssm_lambda_re, v_ssm_lambda_im, v_ssm_log_dt, v_ssm_b_re,
                             v_ssm_b_im, v_ssm_c_re, v_ssm_c_im, v_ssm_d, v_ssm_w_glu, v_ssm_b_glu, v_mla_q_norm_w,
                             v_mla_w_uq, v_mla_kv_norm_w, v_mla_w_ukv, v_ssm_out_norm_w, v_mla_out_norm_w, v_w_out,
                             v_ffn_norm_w, v_ffn_w_up, v_ffn_conv_w, v_ffn_conv_b, v_ffn_w_down, v_final_norm_w]))
    seq, d = x.shape[1], x.shape[2]
    in_width = w_in.shape[2]
    in_pad = -(-in_width // LANES) * LANES
    q_cols = mla_w_uq.shape[2]
    q_pad = 2 * LANES

    (win_g,) = _all_gather("gather_w_in", [jnp.pad(w_in[0], ((0, 0), (0, in_pad - in_width))).astype(BF16)])
    wglu_g, wuq_g, wukv_g, wout_g, convw_g = _all_gather(
        "gather_mix", [ssm_w_glu[0].astype(BF16), jnp.pad(mla_w_uq[0], ((0, 0), (0, q_pad - q_cols))).astype(BF16),
                       mla_w_ukv[0].astype(BF16), w_out[0].astype(BF16), ffn_conv_w[0]], collective_id=0)
    (wup_g,) = _all_gather("gather_ffn_up", [ffn_w_up[0].astype(BF16)], collective_id=1)
    (wdown_g,) = _all_gather("gather_ffn_down", [ffn_w_down[0].astype(BF16)], collective_id=2)
    ns = N_DEV
    c_ff = wup_g.shape[2]
    w = dict(
        attn_norm=attn_norm_w, win=win_g.reshape(d, in_pad), lam_re=ssm_lambda_re, lam_im=ssm_lambda_im,
        log_dt=ssm_log_dt, b_re=ssm_b_re, b_im=ssm_b_im, c_re=ssm_c_re, c_im=ssm_c_im, ssm_d=ssm_d,
        wglu=wglu_g.reshape(d // 2, d // 2), b_glu=ssm_b_glu, q_norm=mla_q_norm_w, wuq=wuq_g,
        kv_norm=mla_kv_norm_w, wukv=wukv_g, son=ssm_out_norm_w, mon=mla_out_norm_w, wout=wout_g.reshape(d, d),
        ffn_norm=ffn_norm_w, wup=wup_g, conv_w=convw_g, conv_b=ffn_conv_b,
        wdown=wdown_g.reshape(ns // 2 * c_ff, d), final_norm=final_norm_w)

    shard_layout = dict(
        win=lambda a: a[:, :in_width].reshape(N_DEV, d // N_DEV, in_width),
        wglu=lambda a: a.reshape(N_DEV, d // 2 // N_DEV, d // 2),
        wuq=lambda a: a[:, :, :q_cols], wukv=lambda a: a, wout=lambda a: a.reshape(N_DEV, d // N_DEV, d),
        wup=lambda a: a, wdown=lambda a: a.reshape(N_DEV, c_ff // 2, d), conv_w=lambda a: a)
    recv = {}
    next_id = [3]

    last = [None]

    out = {}

    def update(k):
        shp = wts[k].shape
        r, c = shp[-2], shp[-1]
        if k in _ROWS_IN_LANES:
            t = lambda a: jnp.swapaxes(a.reshape(-1, r, c), 1, 2)
            res = _adamw_sum("adamw_" + k, t(recv[k]), t(wts[k])[0], t(moms[k])[0], t(vels[k])[0])
            out[k] = [jnp.swapaxes(a, 0, 1).reshape(shp) for a in res]
            return res[0]
        res = _adamw_sum("adamw_" + k, recv[k].reshape(-1, r, c), wts[k].reshape(r, c),
                         moms[k].reshape(r, c), vels[k].reshape(r, c))
        out[k] = [a.reshape(shp) for a in res]
        return res[0]

    pending = {}

    def exchange(not_before=(), **grads):
        names = list(grads)
        if len(names) == 1 and names[0] in _TWO_LEVEL:
            k = names[0]
            parts = shard_layout[k](grads[k])
            got = _pair_swap("swap_" + k, [parts], collective_id=next_id[0], after=[last[0]])[0]
            next_id[0] += 1
            pending[k] = (parts, got)
            last[0] = got
            return
        if len(names) == 1 and names[0].endswith(_AFTER):
            k = names[0][:-len(_AFTER)]
            sums = _pair_add("pair_add_" + k, *pending[k])
            if k == "win":
                pending["tail"] = sums
                return
            recv[k] = _chip_exchange("exchange_" + k, [sums], collective_id=next_id[0],
                                     after=[last[0], grads[names[0]]])[0]
            next_id[0] += 1
            last[0] = recv[k]
            return
        got = _exchange_partials("exchange_" + "_".join(names), [shard_layout[k](grads[k]) for k in names],
                                 collective_id=next_id[0], after=[a for a in (last[0], *not_before) if a is not None])
        next_id[0] += 1
        last[0] = got[-1]
        recv.update(zip(names, got))

    loss_part, grad_x, g = _local_step(x[0], positions[0], loss_target[0], w, emit=exchange)
    n_groups = ssm_lambda_re.shape[1]
    two_d = {"lam_re": (n_groups, -1), "lam_im": (n_groups, -1)}
    dense = {k: g[k].reshape(two_d.get(k, (1, -1))) for k in _VECTORS}
    offsets, width = {}, 0
    for k in _PACKED:
        offsets[k] = width
        width += dense[k].shape[1]
    sent = dict(packed=jnp.concatenate([dense[k] for k in _PACKED], axis=1),
                **{k: dense[k] for k in _VECTORS if k not in _PACKED},
                **{k: g[k].reshape(n_groups, -1).astype(BF16) for k in _GHP},
                loss=loss_part)
    names = list(sent)
    got = _all_gather("gather_small_grads", [sent[k] for k in names], collective_id=next_id[0], after=[last[0]],
                      pair_sums=[pending["tail"]])
    gathered = dict(zip(names, got))
    recv["win"] = got[len(names)]
    for k in _BIG:
        if k not in out and k != "win":
            update(k)
    update("win")

    def finish(keys, results):
        for k, res in zip(keys, results):
            out[k] = [a.reshape(wts[k].shape) for a in res]

    view = lambda k, a: a.reshape(dense[k].shape)
    finish(_VECTORS, _adamw_multi("adamw_vectors", [(offsets.get(k, gathered.get(k)), view(k, wts[k]), view(k, moms[k]),
                                                     view(k, vels[k])) for k in _VECTORS], packed=gathered["packed"]))
    summed = _GHP + ["loss"]
    sums = dict(zip(summed, _sum_multi("sum_ssm_bc_loss", [gathered[k] for k in summed])))
    loss = sums["loss"][0, 0]
    ghp = lambda k: sums[k].reshape(g[k].shape)
    t_hp = lambda a: jnp.swapaxes(a, 2, 3)
    bc_keys = ["c_re", "c_im", "b_re", "b_im"]
    items = [(ghp(k)[None, None], wts[k], moms[k], vels[k]) for k in bc_keys[:2]]
    items += [(ghp(t)[None, None], t_hp(wts[k]), t_hp(moms[k]), t_hp(vels[k]))
              for k, t in zip(bc_keys[2:], ("bt_re", "bt_im"))]
    res = _adamw_multi("adamw_ssm_bc", items)
    finish(bc_keys, res[:2] + [tuple(t_hp(a) for a in r) for r in res[2:]])

    grad_x = grad_x.reshape(x.shape)
    return (loss, grad_x, *[out[k][0] for k in _ORDER], *[out[k][1] for k in _ORDER],
            *[out[k][2] for k in _ORDER], *[out[k][3] for k in _ORDER])
```

```python
import functools

import jax
import jax.numpy as jnp
from jax import lax
from jax.experimental import pallas as pl
from jax.experimental.pallas import tpu as pltpu
from jax.experimental.pallas import tpu_sc as plsc

F32 = jnp.float32
BF16 = jnp.bfloat16
MESH = pl.DeviceIdType.MESH

N_DEV = 8
LANES = 128
SUBLANES = 8
VMEM_LIMIT = 48 * 1024 * 1024

SSM_GROUP = 16
SSM_STATE = 64
GROUPS_PER_BLOCK = LANES // SSM_GROUP
STATE_BLOCK = GROUPS_PER_BLOCK * SSM_STATE
QK_NOPE = 128
QK_ROPE = 64
V_DIM = 128
ROPE_THETA = 10000.0
RMS_EPS = 1e-6

ADAM_LR = 0.001
ADAM_B1 = 0.9
ADAM_B2 = 0.999
ADAM_EPS = 1e-08
ADAM_WD = 0.01
ADAM_STEP = 10

NN = ((1,), (0,))
NT = ((1,), (1,))
TN = ((0,), (0,))


def _cparams():
    return pltpu.CompilerParams(vmem_limit_bytes=VMEM_LIMIT)


def _tile(n, want):
    if n <= want:
        return n
    t = (want // LANES) * LANES
    while t >= LANES:
        if n % t == 0:
            return t
        t -= LANES
    return n


def _mm(name, a, b, *, grid, a_spec, b_spec, o_spec, out_shape, out_dtype, contract=NN,
        res=None, res_spec=None):
    nk = grid[-1]
    kaxis = len(grid) - 1
    acc_shape = tuple(d for d in o_spec.block_shape if d is not None)

    def body(*refs):
        a_ref, b_ref = refs[:2]
        r_ref = None if res is None else refs[2]
        o_ref = refs[2 if res is None else 3]
        part = lax.dot_general(a_ref[...].astype(BF16), b_ref[...].astype(BF16),
                               (contract, ((), ())), preferred_element_type=F32)
        if nk == 1:
            if r_ref is not None:
                part = part + r_ref[...].astype(F32)
            o_ref[...] = part.astype(o_ref.dtype)
            return
        acc = refs[-1]
        k = pl.program_id(kaxis)

        @pl.when(k == 0)
        def _():
            acc[...] = part

        @pl.when(k != 0)
        def _():
            acc[...] += part

        @pl.when(k == nk - 1)
        def _():
            r = acc[...]
            if r_ref is not None:
                r = r + r_ref[...].astype(F32)
            o_ref[...] = r.astype(o_ref.dtype)

    ins = [a, b] + ([] if res is None else [res])
    in_specs = [a_spec, b_spec] + ([] if res is None else [res_spec])
    return pl.pallas_call(
        body, name=name, grid=grid, in_specs=in_specs, out_specs=o_spec,
        out_shape=jax.ShapeDtypeStruct(out_shape, out_dtype),
        scratch_shapes=[pltpu.VMEM(acc_shape, F32)] if nk > 1 else [], compiler_params=_cparams(),
    )(*ins)


def _mm2d(name, a, b, contract, out_dtype, tm=1024, tn=1024, tk=2048, res=None):
    if contract == NN:
        (m, kk), n = a.shape, b.shape[1]
    elif contract == NT:
        (m, kk), n = a.shape, b.shape[0]
    else:
        (kk, m), n = a.shape, b.shape[1]
    tm, tn, tk = _tile(m, tm), _tile(n, tn), _tile(kk, tk)
    grid = (m // tm, n // tn, kk // tk)
    if contract == TN:
        a_spec = pl.BlockSpec((tk, tm), lambda i, j, k: (k, i))
    else:
        a_spec = pl.BlockSpec((tm, tk), lambda i, j, k: (i, k))
    if contract == NT:
        b_spec = pl.BlockSpec((tn, tk), lambda i, j, k: (j, k))
    else:
        b_spec = pl.BlockSpec((tk, tn), lambda i, j, k: (k, j))
    o_spec = pl.BlockSpec((tm, tn), lambda i, j, k: (i, j))
    res_spec = None
    if res is not None:
        if res.shape[0] == 1:
            res_spec = pl.BlockSpec((1, tn), lambda i, j, k: (0, j))
        else:
            res_spec = pl.BlockSpec((tm, tn), lambda i, j, k: (i, j))
    return _mm(name, a, b, grid=grid, a_spec=a_spec, b_spec=b_spec, o_spec=o_spec,
               out_shape=(m, n), out_dtype=out_dtype, contract=contract, res=res, res_spec=res_spec)


def _blockwise(name, fn, ins, in_specs, outs, out_specs, grid, n_acc=0, acc_all=True):
    n_in, n_out = len(ins), len(outs)
    n_plain = n_out - n_acc

    def body(*refs):
        vals = fn(*[r[...] for r in refs[:n_in]])
        if not isinstance(vals, (tuple, list)):
            vals = (vals,)
        o_refs = refs[n_in:n_in + n_out]
        for r, v in zip(o_refs[:n_plain], vals[:n_plain]):
            r[...] = v.astype(r.dtype)
        if n_acc:
            if acc_all:
                first = functools.reduce(jnp.logical_and, [pl.program_id(d) == 0 for d in range(len(grid))])
            else:
                first = pl.program_id(len(grid) - 1) == 0

            @pl.when(first)
            def _():
                for r, v in zip(o_refs[n_plain:], vals[n_plain:]):
                    r[...] = v.astype(r.dtype)

            @pl.when(jnp.logical_not(first))
            def _():
                for r, v in zip(o_refs[n_plain:], vals[n_plain:]):
                    r[...] += v.astype(r.dtype)

    return pl.pallas_call(
        body, name=name, grid=grid, in_specs=in_specs, out_specs=out_specs,
        out_shape=[jax.ShapeDtypeStruct(s, d) for s, d in outs], compiler_params=_cparams(),
    )(*ins)


def _row_spec(t, c):
    return pl.BlockSpec((t, c), lambda i: (i, 0))


def _full_spec(shape, single=False):
    nd = len(shape)
    if single:
        return pl.BlockSpec(tuple(shape), lambda *g: (0,) * nd, pipeline_mode=pl.Buffered(1))
    return pl.BlockSpec(tuple(shape), lambda *g: (0,) * nd)


def _rms(xf, w):
    return xf * lax.rsqrt(jnp.mean(xf * xf, axis=-1, keepdims=True) + RMS_EPS) * w


def _rms_bwd(xf, w, dy):
    _, vjp = jax.vjp(_rms, xf, w)
    return vjp(dy)


def _s5_disc(lr, li, ldt, bre, bim):
    dt = jnp.exp(ldt)
    mag = jnp.exp(lr * dt)
    ar = mag * jnp.cos(li * dt)
    ai = mag * jnp.sin(li * dt)
    nr, ni = ar - 1.0, ai
    den = lr * lr + li * li
    zr = (nr * lr + ni * li) / den
    zi = (ni * lr - nr * li) / den
    return ar, ai, zr * bre - zi * bim, zr * bim + zi * bre


def _s5_prep(lr, li, ldt, bre, bim):
    def body(lr_r, li_r, ldt_r, bre_r, bim_r, ar_r, ai_r, br_r, bi_r):
        ar, ai, br, bi = _s5_disc(lr_r[...], li_r[...], ldt_r[...], bre_r[...], bim_r[...])
        ar_r[...] = ar
        ai_r[...] = ai
        br_r[...] = br
        bi_r[...] = bi

    sd = jax.ShapeDtypeStruct
    return pl.pallas_call(
        body, name="s5_prep",
        out_shape=[sd(lr.shape, F32), sd(lr.shape, F32), sd(bre.shape, F32), sd(bre.shape, F32)],
        compiler_params=_cparams(),
    )(lr, li, ldt, bre, bim)


def _s5_prep_bwd(lr, li, ldt, bre, bim, dar, dai, dbr, dbi):
    def body(lr_r, li_r, ldt_r, bre_r, bim_r, dar_r, dai_r, dbr_r, dbi_r, o0, o1, o2, o3, o4):
        _, vjp = jax.vjp(_s5_disc, lr_r[...], li_r[...], ldt_r[...], bre_r[...], bim_r[...])
        g = vjp((dar_r[...], dai_r[...], dbr_r[...], dbi_r[...]))
        for o, v in zip((o0, o1, o2, o3, o4), g):
            o[...] = v

    sd = jax.ShapeDtypeStruct
    return pl.pallas_call(
        body, name="s5_prep_bwd",
        out_shape=[sd(lr.shape, F32), sd(li.shape, F32), sd(ldt.shape, F32), sd(bre.shape, F32), sd(bim.shape, F32)],
        compiler_params=_cparams(),
    )(lr, li, ldt, bre, bim, dar, dai, dbr, dbi)


SCAN_T = 256


def _scan_tables(ar, ai, tab_r, tab_i, sub, reverse):
    pr, pi = ar, ai
    for k in range(sub):
        row = sub - 1 - k if reverse else k
        tab_r[row:row + 1, :] = pr
        tab_i[row:row + 1, :] = pi
        pr, pi = ar * pr - ai * pi, ar * pi + ai * pr


def _pack_matrix(t_blk, dtype):
    sub = t_blk // SUBLANES
    dst = jnp.arange(t_blk)
    src = (dst % SUBLANES) * sub + dst // SUBLANES
    return (src[:, None] == jnp.arange(t_blk)[None, :]).astype(dtype)


def _permute_rows_f32(pm, x):
    hi = x.astype(BF16)
    r1 = x - hi.astype(F32)
    mid = r1.astype(BF16)
    lo = (r1 - mid.astype(F32)).astype(BF16)
    dot = lambda v: jnp.dot(pm, v, preferred_element_type=F32)
    return dot(hi) + dot(mid) + dot(lo)


def _scan_block(x, loc, ar, ai, st, tab_r, tab_i, sub, reverse):
    hb = STATE_BLOCK
    a8r = jnp.broadcast_to(ar, (SUBLANES, hb))
    a8i = jnp.broadcast_to(ai, (SUBLANES, hb))
    sr = jnp.zeros((SUBLANES, hb), F32)
    si = jnp.zeros((SUBLANES, hb), F32)
    steps = range(sub - 1, -1, -1) if reverse else range(sub)
    for t in steps:
        rows = slice(t * SUBLANES, (t + 1) * SUBLANES)
        sr, si = a8r * sr - a8i * si + x[rows, :hb], a8r * si + a8i * sr + x[rows, hb:]
        loc[rows, :hb] = sr
        loc[rows, hb:] = si
    cr, ci = st[0:1, :], st[1:2, :]
    far = 0 if reverse else sub - 1
    fr, fi = tab_r[far:far + 1, :], tab_i[far:far + 1, :]
    ent_r, ent_i = [None] * SUBLANES, [None] * SUBLANES
    for c in (range(SUBLANES - 1, -1, -1) if reverse else range(SUBLANES)):
        ent_r[c], ent_i[c] = cr, ci
        cr, ci = sr[c:c + 1, :] + (fr * cr - fi * ci), si[c:c + 1, :] + (fr * ci + fi * cr)
    st[0:1, :] = cr
    st[1:2, :] = ci
    c8r = jnp.concatenate(ent_r, axis=0)
    c8i = jnp.concatenate(ent_i, axis=0)
    out = []
    for t in range(sub):
        rows = slice(t * SUBLANES, (t + 1) * SUBLANES)
        tr, ti = tab_r[t:t + 1, :], tab_i[t:t + 1, :]
        out.append(jnp.concatenate([loc[rows, :hb] + (tr * c8r - ti * c8i), loc[rows, hb:] + (tr * c8i + ti * c8r)],
                                   axis=1))
    return jnp.concatenate(out, axis=0)


SSM_BLOCKS_PER_STEP = 8


def _scan_scratch(nblk, t_blk, sub, hb):
    return [pltpu.VMEM((nblk, SUBLANES, hb), F32), pltpu.VMEM((nblk, sub, hb), F32), pltpu.VMEM((nblk, sub, hb), F32),
            pltpu.VMEM((nblk, t_blk, 2 * hb), F32)]


def _ssm_fwd(proj, wb, wc, a):
    seq = proj.shape[0]
    nj = wb.shape[0]
    w2 = 2 * STATE_BLOCK
    hb = STATE_BLOCK
    t_blk = min(SCAN_T, seq)
    sub = t_blk // SUBLANES
    pm = _pack_matrix(t_blk, BF16)

    npair = SSM_BLOCKS_PER_STEP

    def body(u_ref, wb_ref, wc_ref, a_ref, pm_ref, pmt_ref, s_ref, y_ref, st, tab_r, tab_i, loc):
        coef = [(a_ref[:, b * w2:b * w2 + hb], a_ref[:, b * w2 + hb:(b + 1) * w2]) for b in range(npair)]

        @pl.when(pl.program_id(1) == 0)
        def _():
            for b, (ar, ai) in enumerate(coef):
                st[b] = jnp.zeros((SUBLANES, hb), F32)
                _scan_tables(ar, ai, tab_r.at[b], tab_i.at[b], sub, False)

        for b, (ar, ai) in enumerate(coef):
            ub = u_ref[:, b * LANES:(b + 1) * LANES].astype(BF16)
            up = jnp.dot(pm_ref[...], ub, preferred_element_type=F32).astype(BF16)
            bu = jnp.dot(up, wb_ref[b], preferred_element_type=F32)
            s = _scan_block(bu, loc.at[b], ar, ai, st.at[b], tab_r.at[b], tab_i.at[b], sub, False)
            s_ref[:, b * w2:(b + 1) * w2] = s
            yp = jnp.dot(s.astype(BF16), wc_ref[b], preferred_element_type=F32)
            y_ref[:, b * LANES:(b + 1) * LANES] = _permute_rows_f32(pmt_ref[...], yp)

    sd = jax.ShapeDtypeStruct
    return pl.pallas_call(
        body, name="ssm_fwd", grid=(nj // npair, seq // t_blk),
        in_specs=[pl.BlockSpec((t_blk, npair * LANES), lambda j, i: (i, j)),
                  pl.BlockSpec((npair, LANES, w2), lambda j, i: (j, 0, 0)),
                  pl.BlockSpec((npair, w2, LANES), lambda j, i: (j, 0, 0)),
                  pl.BlockSpec((1, npair * w2), lambda j, i: (0, j)),
                  _full_spec((t_blk, t_blk)), _full_spec((t_blk, t_blk))],
        out_specs=[pl.BlockSpec((t_blk, npair * w2), lambda j, i: (i, j)),
                   pl.BlockSpec((t_blk, npair * LANES), lambda j, i: (i, j))],
        out_shape=[sd((seq, nj * w2), F32), sd((seq, nj * LANES), F32)],
        scratch_shapes=_scan_scratch(npair, t_blk, sub, hb), compiler_params=_cparams(),
    )(proj, wb, wc, a, pm, pm.T)


def _ssm_bwd(dy, s, proj, du1, wb, wc, a):
    seq = dy.shape[0]
    nj = wb.shape[0]
    w2 = 2 * STATE_BLOCK
    hb = STATE_BLOCK
    t_blk = min(SCAN_T, seq)
    sub = t_blk // SUBLANES
    nb = seq // t_blk
    pm = _pack_matrix(t_blk, BF16)

    npair = SSM_BLOCKS_PER_STEP

    def body(dy_ref, s_ref, sprev_ref, u_ref, du1_ref, wb_ref, wc_ref, a_ref, pm_ref, pmt_ref,
             du_ref, dwb_ref, dwc_ref, da_ref, st, tab_r, tab_i, loc):
        ib = pl.program_id(1)
        pmv = pm_ref[...]
        coef = [(a_ref[:, b * w2:b * w2 + hb], -a_ref[:, b * w2 + hb:(b + 1) * w2]) for b in range(npair)]

        @pl.when(ib == 0)
        def _():
            for b, (ar, ai) in enumerate(coef):
                st[b] = jnp.zeros((SUBLANES, hb), F32)
                _scan_tables(ar, ai, tab_r.at[b], tab_i.at[b], sub, True)

        sums = []
        for b, (ar, ai) in enumerate(coef):
            cols, wide = slice(b * LANES, (b + 1) * LANES), slice(b * w2, (b + 1) * w2)
            dyp = jnp.dot(pmv, dy_ref[:, cols], preferred_element_type=F32).astype(BF16)
            up = jnp.dot(pmv, u_ref[:, cols].astype(BF16), preferred_element_type=F32).astype(BF16)
            ds = lax.dot_general(dyp, wc_ref[b], (NT, ((), ())), preferred_element_type=F32)
            lam = _scan_block(ds, loc.at[b], ar, ai, st.at[b], tab_r.at[b], tab_i.at[b], sub, True)
            lamb = lam.astype(BF16)
            du = lax.dot_general(lamb, wb_ref[b], (NT, ((), ())), preferred_element_type=F32)
            du_ref[:, cols] = (_permute_rows_f32(pmt_ref[...], du) + du1_ref[:, cols]).astype(du_ref.dtype)
            sv = s_ref[:, wide]
            dwb = lax.dot_general(up, lamb, (TN, ((), ())), preferred_element_type=F32)
            dwc = lax.dot_general(sv.astype(BF16), dyp, (TN, ((), ())), preferred_element_type=F32)

            prev_last = sprev_ref[SUBLANES - 1:SUBLANES, wide]
            prev_last = jnp.where(ib == nb - 1, jnp.zeros_like(prev_last), prev_last)
            tail = sv[t_blk - SUBLANES:, :]
            sl = lax.broadcasted_iota(jnp.int32, tail.shape, 0)
            head = jnp.where(sl >= 1, pltpu.roll(tail, 1, 0), prev_last)
            s_sh = jnp.concatenate([head, sv[:t_blk - SUBLANES, :]], axis=0)
            lam_r, lam_i = lam[:, :hb], lam[:, hb:]
            sr_, si_ = s_sh[:, :hb], s_sh[:, hb:]
            dar = jnp.sum(lam_r * sr_ + lam_i * si_, axis=0, keepdims=True)
            dai = jnp.sum(lam_i * sr_ - lam_r * si_, axis=0, keepdims=True)
            sums.append((wide, jnp.concatenate([dar, dai], axis=1), dwb, dwc))

        @pl.when(ib == 0)
        def _():
            for b, (wide, contrib, dwb, dwc) in enumerate(sums):
                da_ref[:, wide] = contrib
                dwb_ref[b] = dwb
                dwc_ref[b] = dwc

        @pl.when(ib != 0)
        def _():
            for b, (wide, contrib, dwb, dwc) in enumerate(sums):
                da_ref[:, wide] += contrib
                dwb_ref[b] += dwb
                dwc_ref[b] += dwc

    blk = lambda j, i: (nb - 1 - i, j)
    prev_blk = lambda j, i: (jnp.maximum((nb - 1 - i) * sub - 1, 0), j)
    sd = jax.ShapeDtypeStruct
    return pl.pallas_call(
        body, name="ssm_bwd", grid=(nj // npair, nb),
        in_specs=[pl.BlockSpec((t_blk, npair * LANES), blk), pl.BlockSpec((t_blk, npair * w2), blk),
                  pl.BlockSpec((SUBLANES, npair * w2), prev_blk), pl.BlockSpec((t_blk, npair * LANES), blk),
                  pl.BlockSpec((t_blk, npair * LANES), blk),
                  pl.BlockSpec((npair, LANES, w2), lambda j, i: (j, 0, 0)),
                  pl.BlockSpec((npair, w2, LANES), lambda j, i: (j, 0, 0)),
                  pl.BlockSpec((1, npair * w2), lambda j, i: (0, j)),
                  _full_spec((t_blk, t_blk)), _full_spec((t_blk, t_blk))],
        out_specs=[pl.BlockSpec((t_blk, npair * LANES), blk),
                   pl.BlockSpec((npair, LANES, w2), lambda j, i: (j, 0, 0)),
                   pl.BlockSpec((npair, w2, LANES), lambda j, i: (j, 0, 0)),
                   pl.BlockSpec((1, npair * w2), lambda j, i: (0, j))],
        out_shape=[sd((seq, nj * LANES), BF16), sd((nj, LANES, w2), F32), sd((nj, w2, LANES), F32),
                   sd((1, nj * w2), F32)],
        scratch_shapes=_scan_scratch(npair, t_blk, sub, hb), compiler_params=_cparams(),
    )(dy, s, s, proj, du1, wb, wc, a, pm, pm.T)


def _rope128(x, cos, sa, sb):
    return x * cos + pltpu.roll(x, 96, 1) * sa + pltpu.roll(x, 32, 1) * sb


def _rope128_t(dy, cos, sa, sb):
    return dy * cos + pltpu.roll(dy * sa, 32, 1) + pltpu.roll(dy * sb, 96, 1)


ATT_BQ = 512


def _probs(qn, qp, kn, kp, r0, scale):
    s = lax.dot_general(qn, kn, (NT, ((), ())), preferred_element_type=F32)
    s = s + lax.dot_general(qp, kp, (NT, ((), ())), preferred_element_type=F32)
    s = s * scale
    diag = s[:, r0:]
    row = lax.broadcasted_iota(jnp.int32, diag.shape, 0)
    col = lax.broadcasted_iota(jnp.int32, diag.shape, 1)
    diag = jnp.where(col <= row, diag, jnp.finfo(F32).min)
    s = diag if r0 == 0 else jnp.concatenate([s[:, :r0], diag], axis=1)
    m = jnp.max(s, axis=-1, keepdims=True)
    e = jnp.exp(s - m)
    return e / jnp.sum(e, axis=-1, keepdims=True)


def _attn_specs(seq):
    tab = pl.BlockSpec((seq, LANES), lambda h: (0, 0))
    return [pl.BlockSpec((None, seq, 256), lambda h: (h, 0, 0)), pl.BlockSpec((None, seq, 128), lambda h: (h, 0, 0)),
            pl.BlockSpec((None, seq, 128), lambda h: (h, 0, 1)), tab, tab, tab, tab]


def _attn_fwd(q_raw, kv, kpe, cos, sa, sb):
    nh, seq, _ = q_raw.shape
    bq = min(ATT_BQ, seq)
    scale = (QK_NOPE + QK_ROPE) ** -0.5

    def body(q_ref, kn_ref, v_ref, kp_ref, cos_ref, sa_ref, sb_ref, o_ref):
        for r0 in range(0, seq, bq):
            rows, kend = pl.ds(r0, bq), r0 + bq
            qn = q_ref[rows, :QK_NOPE].astype(BF16)
            qp = _rope128(q_ref[rows, QK_NOPE:], cos_ref[rows, :], sa_ref[rows, :], sb_ref[rows, :]).astype(BF16)
            p = _probs(qn, qp, kn_ref[:kend, :], kp_ref[:kend, :], r0, scale)
            o_ref[rows, :] = jnp.dot(p.astype(BF16), v_ref[:kend, :], preferred_element_type=F32)

    return pl.pallas_call(
        body, name="attn_fwd", grid=(nh,), in_specs=_attn_specs(seq),
        out_specs=pl.BlockSpec((seq, V_DIM), lambda h: (0, h)),
        out_shape=jax.ShapeDtypeStruct((seq, nh * V_DIM), F32), compiler_params=_cparams(),
    )(q_raw, kv, kv, kpe, cos, sa, sb)


def _attn_bwd(q_raw, kv, kpe, cos, sa, sb, do):
    nh, seq, _ = q_raw.shape
    bq = min(ATT_BQ, seq)
    scale = (QK_NOPE + QK_ROPE) ** -0.5

    def body(q_ref, kn_ref, v_ref, kp_ref, cos_ref, sa_ref, sb_ref, do_ref, dq_ref, dkv_ref, dkp_ref):
        dkv_ref[...] = jnp.zeros_like(dkv_ref)
        dkp_ref[...] = jnp.zeros_like(dkp_ref)
        for r0 in range(0, seq, bq):
            rows, kend = pl.ds(r0, bq), r0 + bq
            cos_b, sa_b, sb_b = cos_ref[rows, :], sa_ref[rows, :], sb_ref[rows, :]
            qn = q_ref[rows, :QK_NOPE].astype(BF16)
            qp = _rope128(q_ref[rows, QK_NOPE:], cos_b, sa_b, sb_b).astype(BF16)
            kn, v, kp = kn_ref[:kend, :], v_ref[:kend, :], kp_ref[:kend, :]
            p = _probs(qn, qp, kn, kp, r0, scale)
            dob = do_ref[rows, :].astype(BF16)
            dp = lax.dot_general(dob, v, (NT, ((), ())), preferred_element_type=F32)
            ds = p * (dp - jnp.sum(p * dp, axis=-1, keepdims=True)) * scale
            dsb = ds.astype(BF16)
            pb = p.astype(BF16)
            dq_ref[rows, :QK_NOPE] = jnp.dot(dsb, kn, preferred_element_type=F32).astype(dq_ref.dtype)
            dqp = jnp.dot(dsb, kp, preferred_element_type=F32)
            dq_ref[rows, QK_NOPE:] = _rope128_t(dqp, cos_b, sa_b, sb_b).astype(dq_ref.dtype)
            dkv_ref[:kend, :QK_NOPE] += lax.dot_general(dsb, qn, (TN, ((), ())), preferred_element_type=F32)
            dkv_ref[:kend, QK_NOPE:] += lax.dot_general(pb, dob, (TN, ((), ())), preferred_element_type=F32)
            dkp_ref[:kend, :] += lax.dot_general(dsb, qp, (TN, ((), ())), preferred_element_type=F32)

    sd = jax.ShapeDtypeStruct
    return pl.pallas_call(
        body, name="attn_bwd", grid=(nh,),
        in_specs=_attn_specs(seq) + [pl.BlockSpec((seq, V_DIM), lambda h: (0, h))],
        out_specs=[pl.BlockSpec((None, seq, 256), lambda h: (h, 0, 0)),
                   pl.BlockSpec((None, seq, 256), lambda h: (h, 0, 0)),
                   pl.BlockSpec((None, seq, 128), lambda h: (h, 0, 0))],
        out_shape=[sd((nh, seq, 256), BF16), sd((nh, seq, 256), F32), sd((nh, seq, 128), F32)],
        compiler_params=_cparams(),
    )(q_raw, kv, kv, kpe, cos, sa, sb, do)


def _shift_rows(a, k):
    seq = a.shape[0]
    r = pltpu.roll(a, k % seq, 0)
    rows = lax.broadcasted_iota(jnp.int32, (SUBLANES, a.shape[1]), 0)
    if k > 0:
        return jnp.concatenate([jnp.where(rows >= k, r[:SUBLANES], 0.0), r[SUBLANES:]], axis=0)
    return jnp.concatenate([r[:seq - SUBLANES], jnp.where(rows < SUBLANES + k, r[seq - SUBLANES:], 0.0)], axis=0)


def _conv3(a, w, b):
    a1 = _shift_rows(a, 1)
    a2 = _shift_rows(a, 2)
    return w[2:3] * a + w[1:2] * a1 + w[0:1] * a2 + b, a1, a2


def _conv_gate_fwd(a, cw, cb):
    half, _, seq, c = a.shape
    nc = c // LANES

    def fn(pair, wg, wv, bg, bv):
        gc, _, _ = _conv3(pair[0], wg, bg)
        vc, _, _ = _conv3(pair[1], wv, bv)
        return gc * jax.nn.sigmoid(gc) * vc

    def w_spec(off, r):
        return pl.BlockSpec((None, r, LANES), lambda k, j: (k + off, 0, j))

    return _blockwise(
        "conv_gate_fwd", fn, [a, cw, cw, cb, cb],
        [pl.BlockSpec((None, 2, seq, LANES), lambda k, j: (k, 0, 0, j)),
         w_spec(0, 3), w_spec(half, 3), w_spec(0, 1), w_spec(half, 1)],
        [((seq, half * c), BF16)], [pl.BlockSpec((seq, LANES), lambda k, j: (0, k * nc + j))],
        grid=(half, nc))[0]


def _conv_gate_bwd(a, cw, cb, dm):
    half, _, seq, c = a.shape
    nc = c // LANES

    def body(a_ref, wg_ref, wv_ref, bg_ref, bv_ref, dm_ref, da_ref, dw_ref, db_ref):
        dmv = dm_ref[...]
        ga, wg = a_ref[0], wg_ref[...]
        va, wv = a_ref[1], wv_ref[...]
        gc, g1, g2 = _conv3(ga, wg, bg_ref[...])
        vc, v1, v2 = _conv3(va, wv, bv_ref[...])
        sg = jax.nn.sigmoid(gc)
        dms = dmv * sg
        d_val = dms * gc
        d_gate = dms * vc * (1.0 + gc * (1.0 - sg))

        def back(r, dc, own, a1, a2, w):
            up1 = _shift_rows(dc, -1)
            up2 = _shift_rows(dc, -2)
            da_ref[r] = (w[2:3] * dc + w[1:2] * up1 + w[0:1] * up2).astype(da_ref.dtype)
            dw_ref[r, 0:1, :] = jnp.sum(dc * a2, axis=0, keepdims=True)
            dw_ref[r, 1:2, :] = jnp.sum(dc * a1, axis=0, keepdims=True)
            dw_ref[r, 2:3, :] = jnp.sum(dc * own, axis=0, keepdims=True)
            db_ref[r] = jnp.sum(dc, axis=0, keepdims=True)

        back(0, d_gate, ga, g1, g2, wg)
        back(1, d_val, va, v1, v2, wv)

    def w_spec(off, r):
        return pl.BlockSpec((None, r, LANES), lambda k, j: (k + off, 0, j))

    def pair_spec(r):
        return pl.BlockSpec((None, 2, r, LANES), lambda k, j: (k, 0, 0, j))

    sd = jax.ShapeDtypeStruct
    return pl.pallas_call(
        body, name="conv_gate_bwd", grid=(half, nc),
        in_specs=[pair_spec(seq), w_spec(0, 3), w_spec(half, 3), w_spec(0, 1), w_spec(half, 1),
                  pl.BlockSpec((seq, LANES), lambda k, j: (0, k * nc + j))],
        out_specs=[pair_spec(seq), pair_spec(3), pair_spec(1)],
        out_shape=[sd((half, 2, seq, c), BF16), sd((half, 2, 3, c), F32), sd((half, 2, 1, c), F32)],
        compiler_params=_cparams(),
    )(a, cw, cw, cb, cb, dm)


ROW_T = 256


def _local_step(x, positions, target, w, emit=lambda **grads: None):
    seq, d = x.shape
    t_row = min(ROW_T, seq)
    nrow = seq // t_row
    ssm_w = d // 2
    nj = ssm_w // LANES
    n_groups = ssm_w // SSM_GROUP
    nh = w["wuq"].shape[0]
    q_rank = w["wuq"].shape[1]
    kv_rank = w["wukv"].shape[1]
    ns = w["wup"].shape[0]
    c_ff = w["wup"].shape[2]
    in_pad = w["win"].shape[1]
    tm = min(1024, seq)
    nm = seq // tm
    sw = 2 * STATE_BLOCK
    g1 = (nrow,)

    lr3 = w["lam_re"].reshape(n_groups, 1, SSM_STATE)
    li3 = w["lam_im"].reshape(n_groups, 1, SSM_STATE)
    ldt3 = w["log_dt"].reshape(n_groups, 1, 1)
    bt_re = jnp.swapaxes(w["b_re"].reshape(n_groups, SSM_STATE, SSM_GROUP), 1, 2)
    bt_im = jnp.swapaxes(w["b_im"].reshape(n_groups, SSM_STATE, SSM_GROUP), 1, 2)
    abar_re, abar_im, bbt_re, bbt_im = _s5_prep(lr3, li3, ldt3, bt_re, bt_im)
    eye = jnp.eye(GROUPS_PER_BLOCK, dtype=F32)

    def blockdiag_in(bb):
        t = bb.reshape(nj, GROUPS_PER_BLOCK, SSM_GROUP, SSM_STATE)
        return jnp.einsum("jghp,gk->jghkp", t, eye).reshape(nj, LANES, STATE_BLOCK)

    def blockdiag_in_t(dwb):
        t = dwb.reshape(nj, GROUPS_PER_BLOCK, SSM_GROUP, GROUPS_PER_BLOCK, SSM_STATE)
        return jnp.einsum("jghkp,gk->jghp", t, eye).reshape(n_groups, SSM_GROUP, SSM_STATE)

    def blockdiag_out(cc):
        t = cc.reshape(nj, GROUPS_PER_BLOCK, SSM_GROUP, SSM_STATE)
        return jnp.einsum("jghp,gk->jkpgh", t, eye).reshape(nj, STATE_BLOCK, LANES)

    def blockdiag_out_t(dwc):
        t = dwc.reshape(nj, GROUPS_PER_BLOCK, SSM_STATE, GROUPS_PER_BLOCK, SSM_GROUP)
        return jnp.einsum("jkpgh,gk->jghp", t, eye).reshape(n_groups, SSM_GROUP, SSM_STATE)

    c_re = w["c_re"].reshape(n_groups, SSM_GROUP, SSM_STATE)
    c_im = w["c_im"].reshape(n_groups, SSM_GROUP, SSM_STATE)
    wb = jnp.concatenate([blockdiag_in(bbt_re), blockdiag_in(bbt_im)], axis=2).astype(BF16)
    wc = jnp.concatenate([blockdiag_out(c_re), -blockdiag_out(c_im)], axis=1).astype(BF16)
    a_lay = jnp.concatenate([abar_re.reshape(nj, 1, STATE_BLOCK), abar_im.reshape(nj, 1, STATE_BLOCK)],
                            axis=1).reshape(1, nj * sw)

    attn_w = w["attn_norm"]
    t_wide = min(2 * t_row, seq)
    g_wide = (seq // t_wide,)

    def proj_fn(xb, wv, wi):
        hb = _rms(xb, wv).astype(BF16)
        return hb, jnp.dot(hb, wi, preferred_element_type=F32)

    hn, proj = _blockwise(
        "norm1_proj", proj_fn, [x, attn_w, w["win"]],
        [_row_spec(t_wide, d), _full_spec((1, d)), _full_spec((d, in_pad), single=True)],
        [((seq, d), BF16), ((seq, in_pad), F32)], [_row_spec(t_wide, d), _row_spec(t_wide, in_pad)], g_wide)

    s_all, ylin = _ssm_fwd(proj, wb, wc, a_lay)

    def glu_fwd_fn(yl, ub, dsk, wg, bg):
        yp = yl + dsk * ub
        ygv = jax.nn.gelu(yp)
        ygb = ygv.astype(BF16)
        zb = jnp.dot(ygb, wg, preferred_element_type=F32) + bg
        return yp, ygb, zb, ygv * jax.nn.sigmoid(zb)

    wide = pl.BlockSpec((t_wide, ssm_w), lambda i: (i, 0))
    y_pre, yg, z, y_ssm = _blockwise(
        "ssm_glu_fwd", glu_fwd_fn, [ylin, proj, w["ssm_d"], w["wglu"], w["b_glu"]],
        [wide, wide, _full_spec((1, ssm_w)), _full_spec((ssm_w, ssm_w), single=True), _full_spec((1, ssm_w))],
        [((seq, ssm_w), F32), ((seq, ssm_w), BF16), ((seq, ssm_w), F32), ((seq, ssm_w), F32)], [wide] * 4, g_wide)

    cq_off, ckv_off, kpe_off = ssm_w, ssm_w + q_rank, ssm_w + q_rank + kv_rank
    assert cq_off % q_rank == 0 and ckv_off % kv_rank == 0 and kpe_off % LANES == 0
    cq_spec = pl.BlockSpec((t_row, q_rank), lambda i: (i, cq_off // q_rank))
    ckv_spec = pl.BlockSpec((t_row, kv_rank), lambda i: (i, ckv_off // kv_rank))
    kpe_spec = pl.BlockSpec((t_row, LANES), lambda i: (i, kpe_off // LANES))
    pos_b = jnp.broadcast_to(positions.astype(F32)[:, None], (seq, LANES))
    inv_freq = ROPE_THETA ** (-jnp.arange(0, QK_ROPE, 2, dtype=F32) / QK_ROPE)
    inv128 = jnp.tile(inv_freq, 4).reshape(1, LANES)

    def mla_prep_fn(cq, ckv, kp, pb, inv, wq, wkv):
        ang = pb * inv
        lane = lax.broadcasted_iota(jnp.int32, ang.shape, 1)
        cs, sn = jnp.cos(ang), jnp.sin(ang)
        cos = jnp.where(lane < QK_ROPE, cs, 0.0)
        sa = jnp.where(lane < QK_ROPE // 2, -sn, 0.0)
        sb = jnp.where(jnp.logical_and(lane >= QK_ROPE // 2, lane < QK_ROPE), sn, 0.0)
        return _rms(cq, wq), _rms(ckv, wkv), _rope128(kp, cos, sa, sb), cos, sa, sb

    qn, kvn, kpe, cos_t, sa_t, sb_t = _blockwise(
        "mla_prep", mla_prep_fn, [proj, proj, proj, pos_b, inv128, w["q_norm"], w["kv_norm"]],
        [cq_spec, ckv_spec, kpe_spec, _row_spec(t_row, LANES),
         _full_spec((1, LANES)), _full_spec((1, q_rank)), _full_spec((1, kv_rank))],
        [((seq, q_rank), BF16), ((seq, kv_rank), BF16), ((seq, LANES), BF16)] + [((seq, LANES), F32)] * 3,
        [_row_spec(t_row, q_rank), _row_spec(t_row, kv_rank)] + [_row_spec(t_row, LANES)] * 4, g1)

    def head_mm(name, act, wh, out_dtype):
        kdim, ndim = wh.shape[1], wh.shape[2]
        return _mm(name, act, wh, grid=(nh, 1, 1),
                   a_spec=pl.BlockSpec((seq, kdim), lambda h, i, k: (i, 0)),
                   b_spec=pl.BlockSpec((None, kdim, ndim), lambda h, i, k: (h, 0, 0)),
                   o_spec=pl.BlockSpec((None, seq, ndim), lambda h, i, k: (h, i, 0)),
                   out_shape=(nh, seq, ndim), out_dtype=out_dtype)

    q_raw = head_mm("mla_q", qn, w["wuq"], F32)
    kv = head_mm("mla_kv", kvn, w["wukv"], BF16)
    y_mla = _attn_fwd(q_raw, kv, kpe, cos_t, sa_t, sb_t)
    mla_w = nh * V_DIM

    def out_proj_fn(ys, ym, ws, wm, wo, xb, wf):
        yc = jnp.concatenate([_rms(ys, ws), _rms(ym, wm)], axis=1).astype(BF16)
        hb = xb + jnp.dot(yc, wo, preferred_element_type=F32)
        return yc, hb, _rms(hb, wf)

    ycat, h1, hn2 = _blockwise(
        "out_norm_proj", out_proj_fn, [y_ssm, y_mla, w["son"], w["mon"], w["wout"], x, w["ffn_norm"]],
        [wide, _row_spec(t_wide, mla_w), _full_spec((1, ssm_w)), _full_spec((1, mla_w)),
         _full_spec((d, d), single=True), _row_spec(t_wide, d), _full_spec((1, d))],
        [((seq, d), BF16), ((seq, d), F32), ((seq, d), BF16)], [_row_spec(t_wide, d)] * 3, g_wide)

    tku = d
    half = ns // 2
    a_ff = _mm("ffn_up", hn2, w["wup"], grid=(ns, nm, d // tku),
               a_spec=pl.BlockSpec((tm, tku), lambda s, i, k: (i, k)),
               b_spec=pl.BlockSpec((None, tku, c_ff), lambda s, i, k: (s, k, 0)),
               o_spec=pl.BlockSpec((None, None, tm, c_ff), lambda s, i, k: (s % half, s // half, i, 0)),
               out_shape=(half, 2, seq, c_ff), out_dtype=F32)
    cb3 = w["conv_b"].reshape(ns, 1, c_ff)
    m_ff = _conv_gate_fwd(a_ff, w["conv_w"], cb3)
    d_ff = half * c_ff
    wdn = w["wdown"]
    tnd = _tile(d, 1024)
    tmx, tnx = min(1024, seq), _tile(d, 1024)
    h2 = _mm2d("ffn_down", m_ff, wdn, NN, F32, tm=512, tn=512, tk=d_ff, res=h1)

    def loss_fn(hb, tb, wv):
        def f(hh, ww):
            err = _rms(hh, ww) - tb
            return 0.5 * jnp.sum(jnp.mean(err * err, axis=-1))

        lossv, (dh, dw) = jax.value_and_grad(f, argnums=(0, 1))(hb, wv)
        return dh, dh, jnp.full((1, LANES), lossv, F32), dw

    fin_w = w["final_norm"].reshape(1, d)
    dh2, dh2b, loss_acc, g_final = _blockwise(
        "loss_head", loss_fn, [h2, target, fin_w], [_row_spec(t_row, d), _row_spec(t_row, d), _full_spec((1, d))],
        [((seq, d), F32), ((seq, d), BF16), ((1, LANES), F32), ((1, d), F32)],
        [_row_spec(t_row, d), _row_spec(t_row, d), _full_spec((1, LANES)), _full_spec((1, d))], g1, n_acc=2)
    loss = loss_acc

    dm = _mm2d("ffn_down_dx", dh2b, wdn, NT, F32, tn=c_ff)
    tks = seq
    g_wdown = _mm2d("ffn_down_dw", m_ff, dh2b, TN, BF16, tm=c_ff)
    emit(wdown=g_wdown)
    da_ff, g_convw2, g_convb2 = _conv_gate_bwd(a_ff, w["conv_w"], cb3, dm)
    g_convw = jnp.swapaxes(g_convw2, 0, 1).reshape(ns, 3, c_ff)
    g_convb = jnp.swapaxes(g_convb2, 0, 1).reshape(ns, 1, c_ff)
    g_wup = _mm("ffn_up_dw", hn2, da_ff, grid=(ns, d // tnd, seq // tks), contract=TN,
                a_spec=pl.BlockSpec((tks, tnd), lambda s, j, k: (k, j)),
                b_spec=pl.BlockSpec((None, None, tks, c_ff), lambda s, j, k: (s % half, s // half, k, 0)),
                o_spec=pl.BlockSpec((None, tnd, c_ff), lambda s, j, k: (s, j, 0)),
                out_shape=(ns, d, c_ff), out_dtype=BF16)
    emit(wup=g_wup)
    dhn2 = _mm("ffn_up_dx", da_ff, w["wup"], grid=(seq // tmx, d // tnx, ns), contract=NT,
               a_spec=pl.BlockSpec((None, None, tmx, c_ff), lambda i, j, s: (s % half, s // half, i, 0)),
               b_spec=pl.BlockSpec((None, tnx, c_ff), lambda i, j, s: (s, j, 0)),
               o_spec=pl.BlockSpec((tmx, tnx), lambda i, j, s: (i, j)),
               out_shape=(seq, d), out_dtype=F32)
    emit(wup_pair_sums_after=dhn2)

    def norm_bwd_fn(hb, dres, dn, wv):
        dx_, dw_ = _rms_bwd(hb, wv, dn)
        dtot = dres + dx_
        return dtot, dtot, dw_

    dh1, dh1b, g_ffn_norm = _blockwise(
        "norm2_bwd", norm_bwd_fn, [h1, dh2, dhn2, w["ffn_norm"]],
        [_row_spec(t_row, d)] * 3 + [_full_spec((1, d))],
        [((seq, d), F32), ((seq, d), BF16), ((1, d), F32)],
        [_row_spec(t_row, d), _row_spec(t_row, d), _full_spec((1, d))], g1, n_acc=1)

    g_wout = _mm2d("out_proj_dw", ycat, dh1b, TN, BF16)

    def outnorm_bwd_fn(dhb, wo, ys, ym, ws, wm):
        dyc = lax.dot_general(dhb, wo, (NT, ((), ())), preferred_element_type=F32)
        dys, dws = _rms_bwd(ys, ws, dyc[:, :ssm_w])
        dym, dwm = _rms_bwd(ym, wm, dyc[:, ssm_w:])
        return dys, dym, dws, dwm

    dy_ssm, dy_mla, g_son, g_mon = _blockwise(
        "out_proj_dx_norm_bwd", outnorm_bwd_fn, [dh1b, w["wout"], y_ssm, y_mla, w["son"], w["mon"]],
        [_row_spec(t_wide, d), _full_spec((d, d), single=True), wide, _row_spec(t_wide, mla_w),
         _full_spec((1, ssm_w)), _full_spec((1, mla_w))],
        [((seq, ssm_w), F32), ((seq, mla_w), F32), ((1, ssm_w), F32), ((1, mla_w), F32)],
        [wide, _row_spec(t_wide, mla_w), _full_spec((1, ssm_w)), _full_spec((1, mla_w))],
        g_wide, n_acc=2)

    def glu_bwd_fn(dy, yp, zb, ub, dsk, wg):
        ygv = jax.nn.gelu(yp)
        sg = jax.nn.sigmoid(zb)
        dz = dy * ygv * sg * (1.0 - sg)
        dzb = dz.astype(BF16)
        dyg = dy * sg + lax.dot_general(dzb, wg, (NT, ((), ())), preferred_element_type=F32)
        _, vjp = jax.vjp(jax.nn.gelu, yp)
        dyp = vjp(dyg)[0]
        return (dzb, dyp, dyp * dsk, jnp.sum(dz, axis=0, keepdims=True), jnp.sum(dyp * ub, axis=0, keepdims=True))

    dz, dy_pre, du1, g_bglu, g_ssmd = _blockwise(
        "ssm_glu_bwd", glu_bwd_fn, [dy_ssm, y_pre, z, proj, w["ssm_d"], w["wglu"]],
        [wide] * 4 + [_full_spec((1, ssm_w)), _full_spec((ssm_w, ssm_w), single=True)],
        [((seq, ssm_w), BF16), ((seq, ssm_w), BF16), ((seq, ssm_w), F32), ((1, ssm_w), F32), ((1, ssm_w), F32)],
        [wide] * 3 + [_full_spec((1, ssm_w))] * 2, g_wide, n_acc=2)
    g_wglu = _mm2d("ssm_glu_dw", yg, dz, TN, BF16)
    dq_raw, dkv, dkp_h = _attn_bwd(q_raw, kv, kpe, cos_t, sa_t, sb_t, dy_mla)

    def head_mm_dx(name, dact, wh):
        kdim, ndim = wh.shape[1], wh.shape[2]
        return _mm(name, dact, wh, grid=(1, 1, nh), contract=NT,
                   a_spec=pl.BlockSpec((None, seq, ndim), lambda i, j, h: (h, i, 0)),
                   b_spec=pl.BlockSpec((None, kdim, ndim), lambda i, j, h: (h, 0, 0)),
                   o_spec=pl.BlockSpec((seq, kdim), lambda i, j, h: (i, 0)),
                   out_shape=(seq, kdim), out_dtype=F32)

    def head_mm_dw(name, act, dact):
        kdim, ndim = act.shape[1], dact.shape[2]
        return _mm(name, act, dact, grid=(nh, 1, seq // tks), contract=TN,
                   a_spec=pl.BlockSpec((tks, kdim), lambda h, j, k: (k, 0)),
                   b_spec=pl.BlockSpec((None, tks, ndim), lambda h, j, k: (h, k, 0)),
                   o_spec=pl.BlockSpec((None, kdim, ndim), lambda h, j, k: (h, 0, 0)),
                   out_shape=(nh, kdim, ndim), out_dtype=BF16)

    g_wuq = head_mm_dw("mla_q_dw", qn, dq_raw)
    g_wukv = head_mm_dw("mla_kv_dw", kvn, dkv)
    dqn = head_mm_dx("mla_q_dx", dq_raw, w["wuq"])
    dkvn = head_mm_dx("mla_kv_dx", dkv, w["wukv"])
    emit(not_before=(dqn, dkvn, dy_pre), wout=g_wout, wuq=g_wuq, wukv=g_wukv, wglu=g_wglu, conv_w=g_convw)

    du, dwb, dwc, da_lay = _ssm_bwd(dy_pre, s_all, proj, du1, wb, wc, a_lay)
    g_c_re = blockdiag_out_t(dwc[:, :STATE_BLOCK, :])
    g_c_im = -blockdiag_out_t(dwc[:, STATE_BLOCK:, :])
    dbbt_re = blockdiag_in_t(dwb[:, :, :STATE_BLOCK])
    dbbt_im = blockdiag_in_t(dwb[:, :, STATE_BLOCK:])
    da3 = da_lay.reshape(nj, 2, STATE_BLOCK)
    dabar_re = da3[:, 0, :].reshape(n_groups, 1, SSM_STATE)
    dabar_im = da3[:, 1, :].reshape(n_groups, 1, SSM_STATE)
    g_lr3, g_li3, g_ldt3, g_bt_re, g_bt_im = _s5_prep_bwd(lr3, li3, ldt3, bt_re, bt_im,
                                                           dabar_re, dabar_im, dbbt_re, dbbt_im)

    def mla_prep_bwd_fn(cq, ckv, dqn_b, dkvn_b, dkp_b, cos, sa, sb, wq, wkv):
        dcq, dwq = _rms_bwd(cq, wq, dqn_b)
        dckv, dwkv = _rms_bwd(ckv, wkv, dkvn_b)
        dkp_sum = dkp_b[0]
        for h in range(1, nh):
            dkp_sum = dkp_sum + dkp_b[h]
        return dcq, dckv, _rope128_t(dkp_sum, cos, sa, sb), dwq, dwkv

    dc_q, dc_kv, dkpe_raw, g_qnorm, g_kvnorm = _blockwise(
        "mla_prep_bwd", mla_prep_bwd_fn, [proj, proj, dqn, dkvn, dkp_h, cos_t, sa_t, sb_t, w["q_norm"], w["kv_norm"]],
        [cq_spec, ckv_spec, _row_spec(t_row, q_rank), _row_spec(t_row, kv_rank),
         pl.BlockSpec((nh, t_row, LANES), lambda i: (0, i, 0))] + [_row_spec(t_row, LANES)] * 3
        + [_full_spec((1, q_rank)), _full_spec((1, kv_rank))],
        [((seq, q_rank), BF16), ((seq, kv_rank), BF16), ((seq, LANES), BF16), ((1, q_rank), F32), ((1, kv_rank), F32)],
        [_row_spec(t_row, q_rank), _row_spec(t_row, kv_rank), _row_spec(t_row, LANES), _full_spec((1, q_rank)),
         _full_spec((1, kv_rank))], g1, n_acc=2)

    dproj = jnp.concatenate([du, dc_q, dc_kv, dkpe_raw], axis=1)
    g_win = _mm2d("proj_dw", hn, dproj, TN, BF16, tn=640)
    emit(win=g_win)
    def norm1_bwd_fn(dpb, wi, xb, dres, wv):
        dn = lax.dot_general(dpb, wi, (NT, ((), ())), preferred_element_type=F32)
        dx_, dw_ = _rms_bwd(xb, wv, dn)
        return dres + dx_, dw_

    grad_x, g_attn_norm = _blockwise(
        "proj_dx_norm1_bwd", norm1_bwd_fn, [dproj, w["win"], x, dh1, attn_w],
        [_row_spec(t_row, in_pad), _full_spec((d, in_pad), single=True), _row_spec(t_row, d), _row_spec(t_row, d), _full_spec((1, d))],
        [((seq, d), F32), ((1, d), F32)], [_row_spec(t_row, d), _full_spec((1, d))], g1, n_acc=1)
    emit(win_pair_sums_after=grad_x)

    grads = dict(
        attn_norm=g_attn_norm, win=g_win, lam_re=g_lr3, lam_im=g_li3, log_dt=g_ldt3,
        bt_re=g_bt_re, bt_im=g_bt_im, c_re=g_c_re, c_im=g_c_im,
        ssm_d=g_ssmd, wglu=g_wglu, b_glu=g_bglu, q_norm=g_qnorm, wuq=g_wuq, kv_norm=g_kvnorm, wukv=g_wukv,
        son=g_son, mon=g_mon, wout=g_wout, ffn_norm=g_ffn_norm, wup=g_wup, conv_w=g_convw, conv_b=g_convb,
        wdown=g_wdown, final_norm=g_final)
    return loss, grad_x, grads


def _mesh_pos():
    return lax.axis_index("x"), lax.axis_index("y"), lax.axis_index("c")


def _handshake_all():
    x, y, c = _mesh_pos()
    barrier = pltpu.get_barrier_semaphore()
    for k in range(1, N_DEV):
        peer = (1 - x if k & 4 else x, 1 - y if k & 2 else y, 1 - c if k & 1 else c)
        pl.semaphore_signal(barrier, inc=1, device_id=peer, device_id_type=MESH)
    pl.semaphore_wait(barrier, N_DEV - 1)


def _handshake(peers):
    barrier = pltpu.get_barrier_semaphore()
    for peer in peers:
        pl.semaphore_signal(barrier, inc=1, device_id=peer, device_id_type=MESH)
    pl.semaphore_wait(barrier, len(peers))


def _comm_call(name, body, n, out_shape, ins, collective_id, after=None, copies=7, n_remote=None, n_local=None):
    n_remote = copies * n if n_remote is None else n_remote
    sems = [pltpu.SemaphoreType.DMA((n_remote,)), pltpu.SemaphoreType.DMA((n_remote,)),
            pltpu.SemaphoreType.DMA((n if n_local is None else n_local,))]
    if collective_id is None:
        any_spec = pl.BlockSpec(memory_space=pl.ANY)
        return pl.pallas_call(body, name=name, out_shape=out_shape, in_specs=[any_spec] * n,
                              out_specs=[any_spec] * n, scratch_shapes=sems)(*ins)
    seq_body = body
    if after:
        n_after = len(after)
        ins = list(ins) + list(after)

        def seq_body(*refs):
            body(*refs[:n], *refs[n + n_after:])

    return pl.kernel(seq_body, name=name, out_type=out_shape,
                     mesh=plsc.ScalarSubcoreMesh(axis_name="seq", num_cores=1), scratch_types=sems,
                     compiler_params=pltpu.CompilerParams(collective_id=collective_id))(*ins)


def _all_gather(name, xs, collective_id=None, after=None, pair_sums=()):
    n = len(xs)
    nh = len(pair_sums)
    m = n + nh

    def body(*refs):
        x_refs, h_refs, o_refs, e_refs = refs[:n], refs[n:m], refs[m:m + n], refs[m + n:2 * m]
        send_sems, recv_sems, local_sems = refs[2 * m:]
        if collective_id is not None:
            _handshake_all()
        finish_pairs = _chip_copies(h_refs, e_refs, send_sems, recv_sems, local_sems, 7 * n, n) if nh else None
        x, y, c = _mesh_pos()
        me, sibling = (x, y, c), (x, y, 1 - c)
        chips = [(1 - x, y), (x, 1 - y), (1 - x, 1 - y)]

        def slot(o_ref, px, py, pc):
            return o_ref.at[4 * px + 2 * py + pc]

        def copy(t, k, block, to, src=None):
            dst = slot(o_refs[t], *block)
            return pltpu.make_async_remote_copy(
                src_ref=dst if src is None else src, dst_ref=dst,
                send_sem=send_sems.at[7 * t + k], recv_sem=recv_sems.at[7 * t + k],
                device_id=to, device_id_type=MESH)

        started = []
        for t in range(n):
            mine = pltpu.make_async_copy(x_refs[t], slot(o_refs[t], *me), local_sems.at[t])
            mine.start()
            started.append(mine)
        first = []
        for t in range(n):
            first.append(copy(t, 0, me, sibling, src=x_refs[t]))
            first += [copy(t, 1 + j, me, (*chip, c), src=x_refs[t]) for j, chip in enumerate(chips)]
        for cp in first:
            cp.start()
        passed = []
        for j, chip in enumerate(chips):
            for t in range(n):
                copy(t, 1 + j, (*chip, c), me).wait_recv()
                fwd = copy(t, 4 + j, (*chip, c), sibling)
                fwd.start()
                passed.append(fwd)
        for t in range(n):
            copy(t, 0, sibling, me).wait_recv()
            for j, chip in enumerate(chips):
                copy(t, 4 + j, (*chip, 1 - c), me).wait_recv()
        for cp in first + passed:
            cp.wait_send()
        for mine in started:
            mine.wait()
        if nh:
            finish_pairs()

    out_shape = ([jax.ShapeDtypeStruct((N_DEV,) + v.shape, v.dtype) for v in xs]
                 + [jax.ShapeDtypeStruct(v.shape, v.dtype) for v in pair_sums])
    return _comm_call(name, body, m, out_shape, list(xs) + list(pair_sums), collective_id, after,
                      n_remote=7 * n + (N_CHIP - 1) * nh, n_local=m)


def _exchange_partials(name, gs, collective_id=None, after=None):
    n = len(gs)

    def body(*refs):
        g_refs, o_refs = refs[:n], refs[n:2 * n]
        send_sems, recv_sems, local_sems = refs[2 * n:]
        if collective_id is not None:
            _handshake_all()
        x, y, c = _mesh_pos()
        me_idx = 4 * x + 2 * y + c
        copies = []
        for t in range(n):
            mine = pltpu.make_async_copy(g_refs[t].at[me_idx], o_refs[t].at[me_idx], local_sems.at[t])
            mine.start()
            copies.append(mine)
        remote = []
        for k in range(1, N_DEV):
            px = 1 - x if k & 4 else x
            py = 1 - y if k & 2 else y
            pc = 1 - c if k & 1 else c
            p_idx = 4 * px + 2 * py + pc
            for t in range(n):
                cp = pltpu.make_async_remote_copy(
                    src_ref=g_refs[t].at[p_idx], dst_ref=o_refs[t].at[me_idx],
                    send_sem=send_sems.at[7 * t + k - 1], recv_sem=recv_sems.at[7 * t + k - 1],
                    device_id=(px, py, pc), device_id_type=MESH)
                cp.start()
                landing = pltpu.make_async_remote_copy(
                    src_ref=g_refs[t].at[p_idx], dst_ref=o_refs[t].at[p_idx],
                    send_sem=send_sems.at[7 * t + k - 1], recv_sem=recv_sems.at[7 * t + k - 1],
                    device_id=(px, py, pc), device_id_type=MESH)
                remote.append((cp, landing))
        for cp, landing in remote:
            landing.wait_recv()
        for cp, landing in remote:
            cp.wait_send()
        for mine in copies:
            mine.wait()

    out_shape = [jax.ShapeDtypeStruct(v.shape, v.dtype) for v in gs]
    return _comm_call(name, body, n, out_shape, gs, collective_id, after)


N_CHIP = N_DEV // 2
PAIR_ADD_BLOCK_ELEMS = 1024 * 1024


def _pair_swap(name, gs, collective_id, after=None):
    n = len(gs)

    def body(*refs):
        g_refs, o_refs = refs[:n], refs[n:2 * n]
        send_sems, recv_sems, _ = refs[2 * n:]
        x, y, c = _mesh_pos()
        sibling = (x, y, 1 - c)
        _handshake([sibling])
        copies = []
        for t in range(n):
            for k in range(N_CHIP):
                copies.append(pltpu.make_async_remote_copy(
                    src_ref=g_refs[t].at[2 * k + 1 - c], dst_ref=o_refs[t].at[k],
                    send_sem=send_sems.at[N_CHIP * t + k], recv_sem=recv_sems.at[N_CHIP * t + k],
                    device_id=sibling, device_id_type=MESH))
        for cp in copies:
            cp.start()
        for cp in copies:
            cp.wait_recv()
        for cp in copies:
            cp.wait_send()

    out_shape = [jax.ShapeDtypeStruct((N_CHIP,) + v.shape[1:], v.dtype) for v in gs]
    return _comm_call(name, body, n, out_shape, gs, collective_id, after, copies=N_CHIP)


def _pair_add(name, g, got):
    _, r, c = g.shape
    tr = r
    if r * c > PAIR_ADD_BLOCK_ELEMS and r % SUBLANES == 0:
        tr = SUBLANES
        while r % (tr * 2) == 0 and tr * 2 * c <= PAIR_ADD_BLOCK_ELEMS:
            tr *= 2

    def body(core_ref, g_ref, got_ref, o_ref):
        o_ref[...] = (g_ref[...].astype(F32) + got_ref[...].astype(F32)).astype(o_ref.dtype)

    grid_spec = pltpu.PrefetchScalarGridSpec(
        num_scalar_prefetch=1, grid=(N_CHIP, r // tr),
        in_specs=[pl.BlockSpec((None, None, tr, c), lambda k, i, core: (k, core[0], i, 0)),
                  pl.BlockSpec((None, tr, c), lambda k, i, core: (k, i, 0))],
        out_specs=pl.BlockSpec((None, tr, c), lambda k, i, core: (k, i, 0)))
    core = lax.axis_index("c").astype(jnp.int32).reshape(1)
    return pl.pallas_call(body, name=name, grid_spec=grid_spec, out_shape=jax.ShapeDtypeStruct((N_CHIP, r, c), g.dtype),
                          compiler_params=_cparams())(core, g.reshape(N_CHIP, 2, r, c), got)


def _chip_copies(h_refs, o_refs, send_sems, recv_sems, local_sems, sem0, local0):
    n = len(h_refs)
    per = N_CHIP - 1
    x, y, c = _mesh_pos()
    others = [(1 - x if k & 2 else x, 1 - y if k & 1 else y) for k in range(1, N_CHIP)]
    my_chip = 2 * x + y
    local = []
    for t in range(n):
        mine = pltpu.make_async_copy(h_refs[t].at[my_chip], o_refs[t].at[my_chip], local_sems.at[local0 + t])
        mine.start()
        local.append(mine)
    remote = []
    for j, (px, py) in enumerate(others):
        chip = 2 * px + py
        for t in range(n):
            sems = dict(send_sem=send_sems.at[sem0 + per * t + j], recv_sem=recv_sems.at[sem0 + per * t + j],
                        device_id=(px, py, c), device_id_type=MESH)
            cp = pltpu.make_async_remote_copy(src_ref=h_refs[t].at[chip], dst_ref=o_refs[t].at[my_chip], **sems)
            cp.start()
            landing = pltpu.make_async_remote_copy(src_ref=h_refs[t].at[chip], dst_ref=o_refs[t].at[chip], **sems)
            remote.append((cp, landing))

    def finish():
        for cp, landing in remote:
            landing.wait_recv()
        for cp, landing in remote:
            cp.wait_send()
        for mine in local:
            mine.wait()

    return finish


def _chip_exchange(name, hs, collective_id, after=None):
    n = len(hs)
    per = N_CHIP - 1

    def body(*refs):
        h_refs, o_refs = refs[:n], refs[n:2 * n]
        send_sems, recv_sems, local_sems = refs[2 * n:]
        x, y, c = _mesh_pos()
        _handshake([(1 - x if k & 2 else x, 1 - y if k & 1 else y, c) for k in range(1, N_CHIP)])
        _chip_copies(h_refs, o_refs, send_sems, recv_sems, local_sems, 0, 0)()

    out_shape = [jax.ShapeDtypeStruct(v.shape, v.dtype) for v in hs]
    return _comm_call(name, body, n, out_shape, hs, collective_id, after, copies=per)


ADAM_BLOCK_ELEMS = 256 * 1024


def _sum_parts(pb):
    g = pb[0].astype(F32)
    for j in range(1, pb.shape[0]):
        g = g + pb[j].astype(F32)
    return g


def _adam_math(g, wb_, mb, vb):
    m_new = ADAM_B1 * mb + (1.0 - ADAM_B1) * g
    v_new = ADAM_B2 * vb + (1.0 - ADAM_B2) * (g * g)
    m_hat = m_new / (1.0 - ADAM_B1 ** ADAM_STEP)
    v_hat = v_new / (1.0 - ADAM_B2 ** ADAM_STEP)
    delta = -ADAM_LR * (m_hat / (jnp.sqrt(v_hat) + ADAM_EPS) + ADAM_WD * wb_)
    return g, delta, m_new, v_new


def _adamw_multi(name, items, nblk=1, packed=None):
    n = len(items)

    def spec(shape, lead):
        blk = list(shape)
        blk[lead + 1] = shape[lead + 1] // nblk
        if nblk == 1:
            return pl.BlockSpec(tuple(blk), lambda i, nd=len(shape): (0,) * nd)
        return pl.BlockSpec(tuple(blk), lambda i, nd=len(shape), ax=lead + 1: (0,) * ax + (i,) + (0,) * (nd - ax - 1))

    ins, in_specs, out_specs, out_shape, where = [], [], [], [], []
    if packed is not None:
        ins.append(packed)
        in_specs.append(spec(packed.shape, 1))
    for parts, wv, mv, vv in items:
        if isinstance(parts, int):
            where.append((0, parts, len(ins)))
        else:
            assert parts.shape[1:] == wv.shape, (name, parts.shape, wv.shape)
            where.append((len(ins), None, len(ins) + 1))
            ins.append(parts)
            in_specs.append(spec(parts.shape, 1))
        ins += [wv, mv, vv]
        in_specs += [spec(wv.shape, 0)] * 3
        out_specs += [spec(wv.shape, 0)] * 4
        out_shape += [jax.ShapeDtypeStruct(wv.shape, F32)] * 4
    n_in = len(ins)

    def body(*refs):
        for t, (ip, off, iw) in enumerate(where):
            wr, mr, vr = refs[iw:iw + 3]
            parts = refs[ip][...] if off is None else refs[ip][:, :, off:off + wr.shape[-1]]
            res = _adam_math(_sum_parts(parts), wr[...], mr[...], vr[...])
            for o, val in zip(refs[n_in + 4 * t:n_in + 4 * t + 4], res):
                o[...] = val

    res = pl.pallas_call(body, name=name, grid=(nblk,), in_specs=in_specs, out_specs=out_specs, out_shape=out_shape,
                         compiler_params=_cparams())(*ins)
    return [tuple(res[4 * t:4 * t + 4]) for t in range(n)]


def _sum_multi(name, parts_list):
    def body(*refs):
        for pr, o in zip(refs[:len(parts_list)], refs[len(parts_list):]):
            o[...] = _sum_parts(pr[...])

    return pl.pallas_call(body, name=name, out_shape=[jax.ShapeDtypeStruct(p.shape[1:], F32) for p in parts_list],
                          compiler_params=_cparams())(*parts_list)


def _adamw_sum(name, parts, wv, mv, vv):
    npart, r, c = parts.shape
    tr = r
    if r * c > ADAM_BLOCK_ELEMS and r % SUBLANES == 0:
        tr = SUBLANES
        while r % (tr * 2) == 0 and tr * 2 * c <= ADAM_BLOCK_ELEMS:
            tr *= 2

    def fn(pb, wb_, mb, vb):
        return _adam_math(_sum_parts(pb), wb_, mb, vb)

    row = pl.BlockSpec((tr, c), lambda i: (i, 0))
    return _blockwise(name, fn, [parts, wv, mv, vv],
                      [pl.BlockSpec((npart, tr, c), lambda i: (0, i, 0)), row, row, row],
                      [((r, c), F32)] * 4, [row] * 4, (r // tr,))


_VECTORS = ["attn_norm", "lam_re", "lam_im", "log_dt", "ssm_d", "b_glu", "q_norm", "kv_norm", "son", "mon",
            "ffn_norm", "conv_b", "final_norm"]
_GHP = ["c_re", "c_im", "bt_re", "bt_im"]
_PACKED = ["attn_norm", "ssm_d", "b_glu", "q_norm", "kv_norm", "son", "mon", "ffn_norm", "conv_b", "final_norm"]
_BIG = ["win", "wglu", "wuq", "wukv", "wout", "wup", "wdown", "conv_w"]
_ROWS_IN_LANES = ("win", "wuq")
_TWO_LEVEL = ("wup", "win")
_AFTER = "_pair_sums_after"
_ORDER = ["attn_norm", "win", "lam_re", "lam_im", "log_dt", "b_re", "b_im", "c_re", "c_im", "ssm_d", "wglu",
          "b_glu", "q_norm", "wuq", "kv_norm", "wukv", "son", "mon", "wout", "ffn_norm", "wup", "conv_w",
          "conv_b", "wdown", "final_norm"]


def kernel(x, positions, attn_norm_w, w_in, ssm_lambda_re, ssm_lambda_im, ssm_log_dt, ssm_b_re, ssm_b_im, ssm_c_re, ssm_c_im, ssm_d, ssm_w_glu, ssm_b_glu, mla_q_norm_w, mla_w_uq, mla_kv_norm_w, mla_w_ukv, ssm_out_norm_w, mla_out_norm_w, w_out, ffn_norm_w, ffn_w_up, ffn_conv_w, ffn_conv_b, ffn_w_down, final_norm_w, loss_target, m_attn_norm_w, m_w_in, m_ssm_lambda_re, m_ssm_lambda_im, m_ssm_log_dt, m_ssm_b_re, m_ssm_b_im, m_ssm_c_re, m_ssm_c_im, m_ssm_d, m_ssm_w_glu, m_ssm_b_glu, m_mla_q_norm_w, m_mla_w_uq, m_mla_kv_norm_w, m_mla_w_ukv, m_ssm_out_norm_w, m_mla_out_norm_w, m_w_out, m_ffn_norm_w, m_ffn_w_up, m_ffn_conv_w, m_ffn_conv_b, m_ffn_w_down, m_final_norm_w, v_attn_norm_w, v_w_in, v_ssm_lambda_re, v_ssm_lambda_im, v_ssm_log_dt, v_ssm_b_re, v_ssm_b_im, v_ssm_c_re, v_ssm_c_im, v_ssm_d, v_ssm_w_glu, v_ssm_b_glu, v_mla_q_norm_w, v_mla_w_uq, v_mla_kv_norm_w, v_mla_w_ukv, v_ssm_out_norm_w, v_mla_out_norm_w, v_w_out, v_ffn_norm_w, v_ffn_w_up, v_ffn_conv_w, v_ffn_conv_b, v_ffn_w_down, v_final_norm_w):
    wts = dict(attn_norm=attn_norm_w, win=w_in, lam_re=ssm_lambda_re, lam_im=ssm_lambda_im, log_dt=ssm_log_dt,
               b_re=ssm_b_re, b_im=ssm_b_im, c_re=ssm_c_re, c_im=ssm_c_im, ssm_d=ssm_d, wglu=ssm_w_glu,
               b_glu=ssm_b_glu, q_norm=mla_q_norm_w, wuq=mla_w_uq, kv_norm=mla_kv_norm_w, wukv=mla_w_ukv,
               son=ssm_out_norm_w, mon=mla_out_norm_w, wout=w_out, ffn_norm=ffn_norm_w, wup=ffn_w_up,
               conv_w=ffn_conv_w, conv_b=ffn_conv_b, wdown=ffn_w_down, final_norm=final_norm_w)
    moms = dict(zip(_ORDER, [m_attn_norm_w, m_w_in, m_ssm_lambda_re, m_ssm_lambda_im, m_ssm_log_dt, m_ssm_b_re,
                             m_ssm_b_im, m_ssm_c_re, m_ssm_c_im, m_ssm_d, m_ssm_w_glu, m_ssm_b_glu, m_mla_q_norm_w,
                             m_mla_w_uq, m_mla_kv_norm_w, m_mla_w_ukv, m_ssm_out_norm_w, m_mla_out_norm_w, m_w_out,
                             m_ffn_norm_w, m_ffn_w_up, m_ffn_conv_w, m_ffn_conv_b, m_ffn_w_down, m_final_norm_w]))
    vels = dict(zip(_ORDER, [v_attn_norm_w, v_w_in, v_ssm_lambda_re, v_ssm_lambda_im, v_ssm_log_dt, v_ssm_b_re,
                             v_ssm_b_im, v_ssm_c_re, v_ssm_c_im, v_ssm_d, v_ssm_w_glu, v_ssm_b_glu, v_mla_q_norm_w,
                             v_mla_w_uq, v_mla_kv_norm_w, v_mla_w_ukv, v_ssm_out_norm_w, v_mla_out_norm_w, v_w_out,
                             v_ffn_norm_w, v_ffn_w_up, v_ffn_conv_w, v_ffn_conv_b, v_ffn_w_down, v_final_norm_w]))
    seq, d = x.shape[1], x.shape[2]
    in_width = w_in.shape[2]
    in_pad = -(-in_width // LANES) * LANES
    q_cols = mla_w_uq.shape[2]
    q_pad = 2 * LANES

    (win_g,) = _all_gather("gather_w_in", [jnp.pad(w_in[0], ((0, 0), (0, in_pad - in_width))).astype(BF16)])
    wglu_g, wuq_g, wukv_g, wout_g, convw_g = _all_gather(
        "gather_mix", [ssm_w_glu[0].astype(BF16), jnp.pad(mla_w_uq[0], ((0, 0), (0, q_pad - q_cols))).astype(BF16),
                       mla_w_ukv[0].astype(BF16), w_out[0].astype(BF16), ffn_conv_w[0]], collective_id=0)
    (wup_g,) = _all_gather("gather_ffn_up", [ffn_w_up[0].astype(BF16)], collective_id=1)
    (wdown_g,) = _all_gather("gather_ffn_down", [ffn_w_down[0].astype(BF16)], collective_id=2)
    ns = N_DEV
    c_ff = wup_g.shape[2]
    w = dict(
        attn_norm=attn_norm_w, win=win_g.reshape(d, in_pad), lam_re=ssm_lambda_re, lam_im=ssm_lambda_im,
        log_dt=ssm_log_dt, b_re=ssm_b_re, b_im=ssm_b_im, c_re=ssm_c_re, c_im=ssm_c_im, ssm_d=ssm_d,
        wglu=wglu_g.reshape(d // 2, d // 2), b_glu=ssm_b_glu, q_norm=mla_q_norm_w, wuq=wuq_g,
        kv_norm=mla_kv_norm_w, wukv=wukv_g, son=ssm_out_norm_w, mon=mla_out_norm_w, wout=wout_g.reshape(d, d),
        ffn_norm=ffn_norm_w, wup=wup_g, conv_w=convw_g, conv_b=ffn_conv_b,
        wdown=wdown_g.reshape(ns // 2 * c_ff, d), final_norm=final_norm_w)

    shard_layout = dict(
        win=lambda a: a[:, :in_width].reshape(N_DEV, d // N_DEV, in_width),
        wglu=lambda a: a.reshape(N_DEV, d // 2 // N_DEV, d // 2),
        wuq=lambda a: a[:, :, :q_cols], wukv=lambda a: a, wout=lambda a: a.reshape(N_DEV, d // N_DEV, d),
        wup=lambda a: a, wdown=lambda a: a.reshape(N_DEV, c_ff // 2, d), conv_w=lambda a: a)
    recv = {}
    next_id = [3]

    last = [None]

    out = {}

    def update(k):
        shp = wts[k].shape
        r, c = shp[-2], shp[-1]
        if k in _ROWS_IN_LANES:
            t = lambda a: jnp.swapaxes(a.reshape(-1, r, c), 1, 2)
            res = _adamw_sum("adamw_" + k, t(recv[k]), t(wts[k])[0], t(moms[k])[0], t(vels[k])[0])
            out[k] = [jnp.swapaxes(a, 0, 1).reshape(shp) for a in res]
            return res[0]
        res = _adamw_sum("adamw_" + k, recv[k].reshape(-1, r, c), wts[k].reshape(r, c),
                         moms[k].reshape(r, c), vels[k].reshape(r, c))
        out[k] = [a.reshape(shp) for a in res]
        return res[0]

    pending = {}

    def exchange(not_before=(), **grads):
        names = list(grads)
        if len(names) == 1 and names[0] in _TWO_LEVEL:
            k = names[0]
            parts = shard_layout[k](grads[k])
            got = _pair_swap("swap_" + k, [parts], collective_id=next_id[0], after=[last[0]])[0]
            next_id[0] += 1
            pending[k] = (parts, got)
            last[0] = got
            return
        if len(names) == 1 and names[0].endswith(_AFTER):
            k = names[0][:-len(_AFTER)]
            sums = _pair_add("pair_add_" + k, *pending[k])
            if k == "win":
                pending["tail"] = sums
                return
            recv[k] = _chip_exchange("exchange_" + k, [sums], collective_id=next_id[0],
                                     after=[last[0], grads[names[0]]])[0]
            next_id[0] += 1
            last[0] = recv[k]
            return
        got = _exchange_partials("exchange_" + "_".join(names), [shard_layout[k](grads[k]) for k in names],
                                 collective_id=next_id[0], after=[a for a in (last[0], *not_before) if a is not None])
        next_id[0] += 1
        last[0] = got[-1]
        recv.update(zip(names, got))

    loss_part, grad_x, g = _local_step(x[0], positions[0], loss_target[0], w, emit=exchange)
    n_groups = ssm_lambda_re.shape[1]
    two_d = {"lam_re": (n_groups, -1), "lam_im": (n_groups, -1)}
    dense = {k: g[k].reshape(two_d.get(k, (1, -1))) for k in _VECTORS}
    offsets, width = {}, 0
    for k in _PACKED:
        offsets[k] = width
        width += dense[k].shape[1]
    sent = dict(packed=jnp.concatenate([dense[k] for k in _PACKED], axis=1),
                **{k: dense[k] for k in _VECTORS if k not in _PACKED},
                **{k: g[k].reshape(n_groups, -1).astype(BF16) for k in _GHP},
                loss=loss_part)
    names = list(sent)
    got = _all_gather("gather_small_grads", [sent[k] for k in names], collective_id=next_id[0], after=[last[0]],
                      pair_sums=[pending["tail"]])
    gathered = dict(zip(names, got))
    recv["win"] = got[len(names)]
    for k in _BIG:
        if k not in out and k != "win":
            update(k)
    update("win")

    def finish(keys, results):
        for k, res in zip(keys, results):
            out[k] = [a.reshape(wts[k].shape) for a in res]

    view = lambda k, a: a.reshape(dense[k].shape)
    finish(_VECTORS, _adamw_multi("adamw_vectors", [(offsets.get(k, gathered.get(k)), view(k, wts[k]), view(k, moms[k]),
                                                     view(k, vels[k])) for k in _VECTORS], packed=gathered["packed"]))
    summed = _GHP + ["loss"]
    sums = dict(zip(summed, _sum_multi("sum_ssm_bc_loss", [gathered[k] for k in summed])))
    loss = sums["loss"][0, 0]
    ghp = lambda k: sums[k].reshape(g[k].shape)
    t_hp = lambda a: jnp.swapaxes(a, 2, 3)
    bc_keys = ["c_re", "c_im", "b_re", "b_im"]
    items = [(ghp(k)[None, None], wts[k], moms[k], vels[k]) for k in bc_keys[:2]]
    items += [(ghp(t)[None, None], t_hp(wts[k]), t_hp(moms[k]), t_hp(vels[k]))
              for k, t in zip(bc_keys[2:], ("bt_re", "bt_im"))]
    res = _adamw_multi("adamw_ssm_bc", items)
    finish(bc_keys, res[:2] + [tuple(t_hp(a) for a in r) for r in res[2:]])

    grad_x = grad_x.reshape(x.shape)
    return (loss, grad_x, *[out[k][0] for k in _ORDER], *[out[k][1] for k in _ORDER],
            *[out[k][2] for k in _ORDER], *[out[k][3] for k in _ORDER])
```

```python
import functools

import jax
import jax.numpy as jnp
from jax import lax
from jax.experimental import pallas as pl
from jax.experimental.pallas import tpu as pltpu
from jax.experimental.pallas import tpu_sc as plsc

F32 = jnp.float32
BF16 = jnp.bfloat16
MESH = pl.DeviceIdType.MESH

N_DEV = 8
LANES = 128
SUBLANES = 8
VMEM_LIMIT = 48 * 1024 * 1024

SSM_GROUP = 16
SSM_STATE = 64
GROUPS_PER_BLOCK = LANES // SSM_GROUP
STATE_BLOCK = GROUPS_PER_BLOCK * SSM_STATE
QK_NOPE = 128
QK_ROPE = 64
V_DIM = 128
ROPE_THETA = 10000.0
RMS_EPS = 1e-6

ADAM_LR = 0.001
ADAM_B1 = 0.9
ADAM_B2 = 0.999
ADAM_EPS = 1e-08
ADAM_WD = 0.01
ADAM_STEP = 10

NN = ((1,), (0,))
NT = ((1,), (1,))
TN = ((0,), (0,))


def _cparams():
    return pltpu.CompilerParams(vmem_limit_bytes=VMEM_LIMIT)


def _tile(n, want):
    if n <= want:
        return n
    t = (want // LANES) * LANES
    while t >= LANES:
        if n % t == 0:
            return t
        t -= LANES
    return n


def _mm(name, a, b, *, grid, a_spec, b_spec, o_spec, out_shape, out_dtype, contract=NN,
        res=None, res_spec=None):
    nk = grid[-1]
    kaxis = len(grid) - 1
    acc_shape = tuple(d for d in o_spec.block_shape if d is not None)

    def body(*refs):
        a_ref, b_ref = refs[:2]
        r_ref = None if res is None else refs[2]
        o_ref = refs[2 if res is None else 3]
        part = lax.dot_general(a_ref[...].astype(BF16), b_ref[...].astype(BF16),
                               (contract, ((), ())), preferred_element_type=F32)
        if nk == 1:
            if r_ref is not None:
                part = part + r_ref[...].astype(F32)
            o_ref[...] = part.astype(o_ref.dtype)
            return
        acc = refs[-1]
        k = pl.program_id(kaxis)

        @pl.when(k == 0)
        def _():
            acc[...] = part

        @pl.when(k != 0)
        def _():
            acc[...] += part

        @pl.when(k == nk - 1)
        def _():
            r = acc[...]
            if r_ref is not None:
                r = r + r_ref[...].astype(F32)
            o_ref[...] = r.astype(o_ref.dtype)

    ins = [a, b] + ([] if res is None else [res])
    in_specs = [a_spec, b_spec] + ([] if res is None else [res_spec])
    return pl.pallas_call(
        body, name=name, grid=grid, in_specs=in_specs, out_specs=o_spec,
        out_shape=jax.ShapeDtypeStruct(out_shape, out_dtype),
        scratch_shapes=[pltpu.VMEM(acc_shape, F32)] if nk > 1 else [], compiler_params=_cparams(),
    )(*ins)


def _mm2d(name, a, b, contract, out_dtype, tm=1024, tn=1024, tk=2048, res=None):
    if contract == NN:
        (m, kk), n = a.shape, b.shape[1]
    elif contract == NT:
        (m, kk), n = a.shape, b.shape[0]
    else:
        (kk, m), n = a.shape, b.shape[1]
    tm, tn, tk = _tile(m, tm), _tile(n, tn), _tile(kk, tk)
    grid = (m // tm, n // tn, kk // tk)
    if contract == TN:
        a_spec = pl.BlockSpec((tk, tm), lambda i, j, k: (k, i))
    else:
        a_spec = pl.BlockSpec((tm, tk), lambda i, j, k: (i, k))
    if contract == NT:
        b_spec = pl.BlockSpec((tn, tk), lambda i, j, k: (j, k))
    else:
        b_spec = pl.BlockSpec((tk, tn), lambda i, j, k: (k, j))
    o_spec = pl.BlockSpec((tm, tn), lambda i, j, k: (i, j))
    res_spec = None
    if res is not None:
        if res.shape[0] == 1:
            res_spec = pl.BlockSpec((1, tn), lambda i, j, k: (0, j))
        else:
            res_spec = pl.BlockSpec((tm, tn), lambda i, j, k: (i, j))
    return _mm(name, a, b, grid=grid, a_spec=a_spec, b_spec=b_spec, o_spec=o_spec,
               out_shape=(m, n), out_dtype=out_dtype, contract=contract, res=res, res_spec=res_spec)


def _blockwise(name, fn, ins, in_specs, outs, out_specs, grid, n_acc=0, acc_all=True):
    n_in, n_out = len(ins), len(outs)
    n_plain = n_out - n_acc

    def body(*refs):
        vals = fn(*[r[...] for r in refs[:n_in]])
        if not isinstance(vals, (tuple, list)):
            vals = (vals,)
        o_refs = refs[n_in:n_in + n_out]
        for r, v in zip(o_refs[:n_plain], vals[:n_plain]):
            r[...] = v.astype(r.dtype)
        if n_acc:
            if acc_all:
                first = functools.reduce(jnp.logical_and, [pl.program_id(d) == 0 for d in range(len(grid))])
            else:
                first = pl.program_id(len(grid) - 1) == 0

            @pl.when(first)
            def _():
                for r, v in zip(o_refs[n_plain:], vals[n_plain:]):
                    r[...] = v.astype(r.dtype)

            @pl.when(jnp.logical_not(first))
            def _():
                for r, v in zip(o_refs[n_plain:], vals[n_plain:]):
                    r[...] += v.astype(r.dtype)

    return pl.pallas_call(
        body, name=name, grid=grid, in_specs=in_specs, out_specs=out_specs,
        out_shape=[jax.ShapeDtypeStruct(s, d) for s, d in outs], compiler_params=_cparams(),
    )(*ins)


def _row_spec(t, c):
    return pl.BlockSpec((t, c), lambda i: (i, 0))


def _full_spec(shape, single=False):
    nd = len(shape)
    if single:
        return pl.BlockSpec(tuple(shape), lambda *g: (0,) * nd, pipeline_mode=pl.Buffered(1))
    return pl.BlockSpec(tuple(shape), lambda *g: (0,) * nd)


def _rms(xf, w):
    return xf * lax.rsqrt(jnp.mean(xf * xf, axis=-1, keepdims=True) + RMS_EPS) * w


def _rms_bwd(xf, w, dy):
    _, vjp = jax.vjp(_rms, xf, w)
    return vjp(dy)


def _s5_disc(lr, li, ldt, bre, bim):
    dt = jnp.exp(ldt)
    mag = jnp.exp(lr * dt)
    ar = mag * jnp.cos(li * dt)
    ai = mag * jnp.sin(li * dt)
    nr, ni = ar - 1.0, ai
    den = lr * lr + li * li
    zr = (nr * lr + ni * li) / den
    zi = (ni * lr - nr * li) / den
    return ar, ai, zr * bre - zi * bim, zr * bim + zi * bre


def _s5_prep(lr, li, ldt, bre, bim):
    def body(lr_r, li_r, ldt_r, bre_r, bim_r, ar_r, ai_r, br_r, bi_r):
        ar, ai, br, bi = _s5_disc(lr_r[...], li_r[...], ldt_r[...], bre_r[...], bim_r[...])
        ar_r[...] = ar
        ai_r[...] = ai
        br_r[...] = br
        bi_r[...] = bi

    sd = jax.ShapeDtypeStruct
    return pl.pallas_call(
        body, name="s5_prep",
        out_shape=[sd(lr.shape, F32), sd(lr.shape, F32), sd(bre.shape, F32), sd(bre.shape, F32)],
        compiler_params=_cparams(),
    )(lr, li, ldt, bre, bim)


def _s5_prep_bwd(lr, li, ldt, bre, bim, dar, dai, dbr, dbi):
    def body(lr_r, li_r, ldt_r, bre_r, bim_r, dar_r, dai_r, dbr_r, dbi_r, o0, o1, o2, o3, o4):
        _, vjp = jax.vjp(_s5_disc, lr_r[...], li_r[...], ldt_r[...], bre_r[...], bim_r[...])
        g = vjp((dar_r[...], dai_r[...], dbr_r[...], dbi_r[...]))
        for o, v in zip((o0, o1, o2, o3, o4), g):
            o[...] = v

    sd = jax.ShapeDtypeStruct
    return pl.pallas_call(
        body, name="s5_prep_bwd",
        out_shape=[sd(lr.shape, F32), sd(li.shape, F32), sd(ldt.shape, F32), sd(bre.shape, F32), sd(bim.shape, F32)],
        compiler_params=_cparams(),
    )(lr, li, ldt, bre, bim, dar, dai, dbr, dbi)


SCAN_T = 256


def _scan_tables(ar, ai, tab_r, tab_i, sub, reverse):
    pr, pi = ar, ai
    for k in range(sub):
        row = sub - 1 - k if reverse else k
        tab_r[row:row + 1, :] = pr
        tab_i[row:row + 1, :] = pi
        pr, pi = ar * pr - ai * pi, ar * pi + ai * pr


def _pack_matrix(t_blk, dtype):
    sub = t_blk // SUBLANES
    dst = jnp.arange(t_blk)
    src = (dst % SUBLANES) * sub + dst // SUBLANES
    return (src[:, None] == jnp.arange(t_blk)[None, :]).astype(dtype)


def _permute_rows_f32(pm, x):
    hi = x.astype(BF16)
    r1 = x - hi.astype(F32)
    mid = r1.astype(BF16)
    lo = (r1 - mid.astype(F32)).astype(BF16)
    dot = lambda v: jnp.dot(pm, v, preferred_element_type=F32)
    return dot(hi) + dot(mid) + dot(lo)


def _scan_block(x, loc, ar, ai, st, tab_r, tab_i, sub, reverse):
    hb = STATE_BLOCK
    a8r = jnp.broadcast_to(ar, (SUBLANES, hb))
    a8i = jnp.broadcast_to(ai, (SUBLANES, hb))
    sr = jnp.zeros((SUBLANES, hb), F32)
    si = jnp.zeros((SUBLANES, hb), F32)
    steps = range(sub - 1, -1, -1) if reverse else range(sub)
    for t in steps:
        rows = slice(t * SUBLANES, (t + 1) * SUBLANES)
        sr, si = a8r * sr - a8i * si + x[rows, :hb], a8r * si + a8i * sr + x[rows, hb:]
        loc[rows, :hb] = sr
        loc[rows, hb:] = si
    cr, ci = st[0:1, :], st[1:2, :]
    far = 0 if reverse else sub - 1
    fr, fi = tab_r[far:far + 1, :], tab_i[far:far + 1, :]
    ent_r, ent_i = [None] * SUBLANES, [None] * SUBLANES
    for c in (range(SUBLANES - 1, -1, -1) if reverse else range(SUBLANES)):
        ent_r[c], ent_i[c] = cr, ci
        cr, ci = sr[c:c + 1, :] + (fr * cr - fi * ci), si[c:c + 1, :] + (fr * ci + fi * cr)
    st[0:1, :] = cr
    st[1:2, :] = ci
    c8r = jnp.concatenate(ent_r, axis=0)
    c8i = jnp.concatenate(ent_i, axis=0)
    out = []
    for t in range(sub):
        rows = slice(t * SUBLANES, (t + 1) * SUBLANES)
        tr, ti = tab_r[t:t + 1, :], tab_i[t:t + 1, :]
        out.append(jnp.concatenate([loc[rows, :hb] + (tr * c8r - ti * c8i), loc[rows, hb:] + (tr * c8i + ti * c8r)],
                                   axis=1))
    return jnp.concatenate(out, axis=0)


SSM_BLOCKS_PER_STEP = 4


def _scan_scratch(nblk, t_blk, sub, hb):
    return [pltpu.VMEM((nblk, SUBLANES, hb), F32), pltpu.VMEM((nblk, sub, hb), F32), pltpu.VMEM((nblk, sub, hb), F32),
            pltpu.VMEM((nblk, t_blk, 2 * hb), F32)]


def _ssm_fwd(proj, wb, wc, a):
    seq = proj.shape[0]
    nj = wb.shape[0]
    w2 = 2 * STATE_BLOCK
    hb = STATE_BLOCK
    t_blk = min(SCAN_T, seq)
    sub = t_blk // SUBLANES
    pm = _pack_matrix(t_blk, BF16)

    npair = SSM_BLOCKS_PER_STEP

    def body(u_ref, wb_ref, wc_ref, a_ref, pm_ref, pmt_ref, s_ref, y_ref, st, tab_r, tab_i, loc):
        coef = [(a_ref[:, b * w2:b * w2 + hb], a_ref[:, b * w2 + hb:(b + 1) * w2]) for b in range(npair)]

        @pl.when(pl.program_id(1) == 0)
        def _():
            for b, (ar, ai) in enumerate(coef):
                st[b] = jnp.zeros((SUBLANES, hb), F32)
                _scan_tables(ar, ai, tab_r.at[b], tab_i.at[b], sub, False)

        for b, (ar, ai) in enumerate(coef):
            ub = u_ref[:, b * LANES:(b + 1) * LANES].astype(BF16)
            up = jnp.dot(pm_ref[...], ub, preferred_element_type=F32).astype(BF16)
            bu = jnp.dot(up, wb_ref[b], preferred_element_type=F32)
            s = _scan_block(bu, loc.at[b], ar, ai, st.at[b], tab_r.at[b], tab_i.at[b], sub, False)
            s_ref[:, b * w2:(b + 1) * w2] = s
            yp = jnp.dot(s.astype(BF16), wc_ref[b], preferred_element_type=F32)
            y_ref[:, b * LANES:(b + 1) * LANES] = _permute_rows_f32(pmt_ref[...], yp)

    sd = jax.ShapeDtypeStruct
    return pl.pallas_call(
        body, name="ssm_fwd", grid=(nj // npair, seq // t_blk),
        in_specs=[pl.BlockSpec((t_blk, npair * LANES), lambda j, i: (i, j)),
                  pl.BlockSpec((npair, LANES, w2), lambda j, i: (j, 0, 0)),
                  pl.BlockSpec((npair, w2, LANES), lambda j, i: (j, 0, 0)),
                  pl.BlockSpec((1, npair * w2), lambda j, i: (0, j)),
                  _full_spec((t_blk, t_blk)), _full_spec((t_blk, t_blk))],
        out_specs=[pl.BlockSpec((t_blk, npair * w2), lambda j, i: (i, j)),
                   pl.BlockSpec((t_blk, npair * LANES), lambda j, i: (i, j))],
        out_shape=[sd((seq, nj * w2), F32), sd((seq, nj * LANES), F32)],
        scratch_shapes=_scan_scratch(npair, t_blk, sub, hb), compiler_params=_cparams(),
    )(proj, wb, wc, a, pm, pm.T)


def _ssm_bwd(dy, s, proj, du1, wb, wc, a):
    seq = dy.shape[0]
    nj = wb.shape[0]
    w2 = 2 * STATE_BLOCK
    hb = STATE_BLOCK
    t_blk = min(SCAN_T, seq)
    sub = t_blk // SUBLANES
    nb = seq // t_blk
    pm = _pack_matrix(t_blk, BF16)

    npair = SSM_BLOCKS_PER_STEP

    def body(dy_ref, s_ref, sprev_ref, u_ref, du1_ref, wb_ref, wc_ref, a_ref, pm_ref, pmt_ref,
             du_ref, dwb_ref, dwc_ref, da_ref, st, tab_r, tab_i, loc):
        ib = pl.program_id(1)
        pmv = pm_ref[...]
        coef = [(a_ref[:, b * w2:b * w2 + hb], -a_ref[:, b * w2 + hb:(b + 1) * w2]) for b in range(npair)]

        @pl.when(ib == 0)
        def _():
            for b, (ar, ai) in enumerate(coef):
                st[b] = jnp.zeros((SUBLANES, hb), F32)
                _scan_tables(ar, ai, tab_r.at[b], tab_i.at[b], sub, True)

        sums = []
        for b, (ar, ai) in enumerate(coef):
            cols, wide = slice(b * LANES, (b + 1) * LANES), slice(b * w2, (b + 1) * w2)
            dyp = jnp.dot(pmv, dy_ref[:, cols], preferred_element_type=F32).astype(BF16)
            up = jnp.dot(pmv, u_ref[:, cols].astype(BF16), preferred_element_type=F32).astype(BF16)
            ds = lax.dot_general(dyp, wc_ref[b], (NT, ((), ())), preferred_element_type=F32)
            lam = _scan_block(ds, loc.at[b], ar, ai, st.at[b], tab_r.at[b], tab_i.at[b], sub, True)
            lamb = lam.astype(BF16)
            du = lax.dot_general(lamb, wb_ref[b], (NT, ((), ())), preferred_element_type=F32)
            du_ref[:, cols] = (_permute_rows_f32(pmt_ref[...], du) + du1_ref[:, cols]).astype(du_ref.dtype)
            sv = s_ref[:, wide]
            dwb = lax.dot_general(up, lamb, (TN, ((), ())), preferred_element_type=F32)
            dwc = lax.dot_general(sv.astype(BF16), dyp, (TN, ((), ())), preferred_element_type=F32)

            prev_last = sprev_ref[SUBLANES - 1:SUBLANES, wide]
            prev_last = jnp.where(ib == nb - 1, jnp.zeros_like(prev_last), prev_last)
            tail = sv[t_blk - SUBLANES:, :]
            sl = lax.broadcasted_iota(jnp.int32, tail.shape, 0)
            head = jnp.where(sl >= 1, pltpu.roll(tail, 1, 0), prev_last)
            s_sh = jnp.concatenate([head, sv[:t_blk - SUBLANES, :]], axis=0)
            lam_r, lam_i = lam[:, :hb], lam[:, hb:]
            sr_, si_ = s_sh[:, :hb], s_sh[:, hb:]
            dar = jnp.sum(lam_r * sr_ + lam_i * si_, axis=0, keepdims=True)
            dai = jnp.sum(lam_i * sr_ - lam_r * si_, axis=0, keepdims=True)
            sums.append((wide, jnp.concatenate([dar, dai], axis=1), dwb, dwc))

        @pl.when(ib == 0)
        def _():
            for b, (wide, contrib, dwb, dwc) in enumerate(sums):
                da_ref[:, wide] = contrib
                dwb_ref[b] = dwb
                dwc_ref[b] = dwc

        @pl.when(ib != 0)
        def _():
            for b, (wide, contrib, dwb, dwc) in enumerate(sums):
                da_ref[:, wide] += contrib
                dwb_ref[b] += dwb
                dwc_ref[b] += dwc

    blk = lambda j, i: (nb - 1 - i, j)
    prev_blk = lambda j, i: (jnp.maximum((nb - 1 - i) * sub - 1, 0), j)
    sd = jax.ShapeDtypeStruct
    return pl.pallas_call(
        body, name="ssm_bwd", grid=(nj // npair, nb),
        in_specs=[pl.BlockSpec((t_blk, npair * LANES), blk), pl.BlockSpec((t_blk, npair * w2), blk),
                  pl.BlockSpec((SUBLANES, npair * w2), prev_blk), pl.BlockSpec((t_blk, npair * LANES), blk),
                  pl.BlockSpec((t_blk, npair * LANES), blk),
                  pl.BlockSpec((npair, LANES, w2), lambda j, i: (j, 0, 0)),
                  pl.BlockSpec((npair, w2, LANES), lambda j, i: (j, 0, 0)),
                  pl.BlockSpec((1, npair * w2), lambda j, i: (0, j)),
                  _full_spec((t_blk, t_blk)), _full_spec((t_blk, t_blk))],
        out_specs=[pl.BlockSpec((t_blk, npair * LANES), blk),
                   pl.BlockSpec((npair, LANES, w2), lambda j, i: (j, 0, 0)),
                   pl.BlockSpec((npair, w2, LANES), lambda j, i: (j, 0, 0)),
                   pl.BlockSpec((1, npair * w2), lambda j, i: (0, j))],
        out_shape=[sd((seq, nj * LANES), BF16), sd((nj, LANES, w2), F32), sd((nj, w2, LANES), F32),
                   sd((1, nj * w2), F32)],
        scratch_shapes=_scan_scratch(npair, t_blk, sub, hb), compiler_params=_cparams(),
    )(dy, s, s, proj, du1, wb, wc, a, pm, pm.T)


def _rope128(x, cos, sa, sb):
    return x * cos + pltpu.roll(x, 96, 1) * sa + pltpu.roll(x, 32, 1) * sb


def _rope128_t(dy, cos, sa, sb):
    return dy * cos + pltpu.roll(dy * sa, 32, 1) + pltpu.roll(dy * sb, 96, 1)


ATT_BQ = 512


def _probs(q2, k2, r0, scale):
    s = lax.dot_general(q2, k2, (NT, ((), ())), preferred_element_type=F32)
    s = s * scale
    diag = s[:, r0:]
    row = lax.broadcasted_iota(jnp.int32, diag.shape, 0)
    col = lax.broadcasted_iota(jnp.int32, diag.shape, 1)
    diag = jnp.where(col <= row, diag, jnp.finfo(F32).min)
    s = diag if r0 == 0 else jnp.concatenate([s[:, :r0], diag], axis=1)
    m = jnp.max(s, axis=-1, keepdims=True)
    e = jnp.exp(s - m)
    return e / jnp.sum(e, axis=-1, keepdims=True)


def _attn_specs(seq):
    tab = pl.BlockSpec((seq, LANES), lambda h: (0, 0))
    return [pl.BlockSpec((None, seq, 256), lambda h: (h, 0, 0)), pl.BlockSpec((None, seq, 128), lambda h: (h, 0, 0)),
            pl.BlockSpec((None, seq, 128), lambda h: (h, 0, 1)), tab, tab, tab, tab]


def _attn_fwd(q_raw, kv, kpe, cos, sa, sb):
    nh, seq, _ = q_raw.shape
    bq = min(ATT_BQ, seq)
    scale = (QK_NOPE + QK_ROPE) ** -0.5

    def body(q_ref, kn_ref, v_ref, kp_ref, cos_ref, sa_ref, sb_ref, o_ref):
        k2 = jnp.concatenate([kn_ref[...], kp_ref[...]], axis=1)
        for r0 in range(0, seq, bq):
            rows, kend = pl.ds(r0, bq), r0 + bq
            qn = q_ref[rows, :QK_NOPE].astype(BF16)
            qp = _rope128(q_ref[rows, QK_NOPE:], cos_ref[rows, :], sa_ref[rows, :], sb_ref[rows, :]).astype(BF16)
            p = _probs(jnp.concatenate([qn, qp], axis=1), k2[:kend], r0, scale)
            o_ref[rows, :] = jnp.dot(p.astype(BF16), v_ref[:kend, :], preferred_element_type=F32)

    return pl.pallas_call(
        body, name="attn_fwd", grid=(nh,), in_specs=_attn_specs(seq),
        out_specs=pl.BlockSpec((seq, V_DIM), lambda h: (0, h)),
        out_shape=jax.ShapeDtypeStruct((seq, nh * V_DIM), F32), compiler_params=_cparams(),
    )(q_raw, kv, kv, kpe, cos, sa, sb)


def _attn_bwd(q_raw, kv, kpe, cos, sa, sb, do):
    nh, seq, _ = q_raw.shape
    bq = min(ATT_BQ, seq)
    scale = (QK_NOPE + QK_ROPE) ** -0.5

    def body(q_ref, kn_ref, v_ref, kp_ref, cos_ref, sa_ref, sb_ref, do_ref, dq_ref, dkv_ref, dkp_ref):
        dkv_ref[...] = jnp.zeros_like(dkv_ref)
        dkp_ref[...] = jnp.zeros_like(dkp_ref)
        k2_all = jnp.concatenate([kn_ref[...], kp_ref[...]], axis=1)
        for r0 in range(0, seq, bq):
            rows, kend = pl.ds(r0, bq), r0 + bq
            cos_b, sa_b, sb_b = cos_ref[rows, :], sa_ref[rows, :], sb_ref[rows, :]
            qn = q_ref[rows, :QK_NOPE].astype(BF16)
            qp = _rope128(q_ref[rows, QK_NOPE:], cos_b, sa_b, sb_b).astype(BF16)
            q2 = jnp.concatenate([qn, qp], axis=1)
            k2, v = k2_all[:kend], v_ref[:kend, :]
            p = _probs(q2, k2, r0, scale)
            dob = do_ref[rows, :].astype(BF16)
            dp = lax.dot_general(dob, v, (NT, ((), ())), preferred_element_type=F32)
            ds = p * (dp - jnp.sum(p * dp, axis=-1, keepdims=True)) * scale
            dsb = ds.astype(BF16)
            pb = p.astype(BF16)
            dq2 = jnp.dot(dsb, k2, preferred_element_type=F32)
            dq_ref[rows, :QK_NOPE] = dq2[:, :QK_NOPE].astype(dq_ref.dtype)
            dq_ref[rows, QK_NOPE:] = _rope128_t(dq2[:, QK_NOPE:], cos_b, sa_b, sb_b).astype(dq_ref.dtype)
            dk2 = lax.dot_general(dsb, q2, (TN, ((), ())), preferred_element_type=F32)
            dkv_ref[:kend, :QK_NOPE] += dk2[:, :QK_NOPE]
            dkv_ref[:kend, QK_NOPE:] += lax.dot_general(pb, dob, (TN, ((), ())), preferred_element_type=F32)
            dkp_ref[:kend, :] += dk2[:, QK_NOPE:]

    sd = jax.ShapeDtypeStruct
    return pl.pallas_call(
        body, name="attn_bwd", grid=(nh,),
        in_specs=_attn_specs(seq) + [pl.BlockSpec((seq, V_DIM), lambda h: (0, h))],
        out_specs=[pl.BlockSpec((None, seq, 256), lambda h: (h, 0, 0)),
                   pl.BlockSpec((None, seq, 256), lambda h: (h, 0, 0)),
                   pl.BlockSpec((None, seq, 128), lambda h: (h, 0, 0))],
        out_shape=[sd((nh, seq, 256), BF16), sd((nh, seq, 256), F32), sd((nh, seq, 128), F32)],
        compiler_params=_cparams(),
    )(q_raw, kv, kv, kpe, cos, sa, sb, do)


def _shift_rows(a, k):
    seq = a.shape[0]
    r = pltpu.roll(a, k % seq, 0)
    rows = lax.broadcasted_iota(jnp.int32, (SUBLANES, a.shape[1]), 0)
    if k > 0:
        return jnp.concatenate([jnp.where(rows >= k, r[:SUBLANES], 0.0), r[SUBLANES:]], axis=0)
    return jnp.concatenate([r[:seq - SUBLANES], jnp.where(rows < SUBLANES + k, r[seq - SUBLANES:], 0.0)], axis=0)


def _conv3(a, w, b):
    a1 = _shift_rows(a, 1)
    a2 = _shift_rows(a, 2)
    return w[2:3] * a + w[1:2] * a1 + w[0:1] * a2 + b, a1, a2


def _conv_gate_fwd(a, cw, cb):
    half, _, seq, c = a.shape
    nc = c // LANES

    def fn(pair, wg, wv, bg, bv):
        gc, _, _ = _conv3(pair[0], wg, bg)
        vc, _, _ = _conv3(pair[1], wv, bv)
        return gc * jax.nn.sigmoid(gc) * vc

    def w_spec(off, r):
        return pl.BlockSpec((None, r, LANES), lambda k, j: (k + off, 0, j))

    return _blockwise(
        "conv_gate_fwd", fn, [a, cw, cw, cb, cb],
        [pl.BlockSpec((None, 2, seq, LANES), lambda k, j: (k, 0, 0, j)),
         w_spec(0, 3), w_spec(half, 3), w_spec(0, 1), w_spec(half, 1)],
        [((seq, half * c), BF16)], [pl.BlockSpec((seq, LANES), lambda k, j: (0, k * nc + j))],
        grid=(half, nc))[0]


def _conv_gate_bwd(a, cw, cb, dm):
    half, _, seq, c = a.shape
    nc = c // LANES

    def body(a_ref, wg_ref, wv_ref, bg_ref, bv_ref, dm_ref, da_ref, dw_ref, db_ref):
        dmv = dm_ref[...]
        ga, wg = a_ref[0], wg_ref[...]
        va, wv = a_ref[1], wv_ref[...]
        gc, g1, g2 = _conv3(ga, wg, bg_ref[...])
        vc, v1, v2 = _conv3(va, wv, bv_ref[...])
        sg = jax.nn.sigmoid(gc)
        dms = dmv * sg
        d_val = dms * gc
        d_gate = dms * vc * (1.0 + gc * (1.0 - sg))

        def back(r, dc, own, a1, a2, w):
            up1 = _shift_rows(dc, -1)
            up2 = _shift_rows(dc, -2)
            da_ref[r] = (w[2:3] * dc + w[1:2] * up1 + w[0:1] * up2).astype(da_ref.dtype)
            dw_ref[r, 0:1, :] = jnp.sum(dc * a2, axis=0, keepdims=True)
            dw_ref[r, 1:2, :] = jnp.sum(dc * a1, axis=0, keepdims=True)
            dw_ref[r, 2:3, :] = jnp.sum(dc * own, axis=0, keepdims=True)
            db_ref[r] = jnp.sum(dc, axis=0, keepdims=True)

        back(0, d_gate, ga, g1, g2, wg)
        back(1, d_val, va, v1, v2, wv)

    def w_spec(off, r):
        return pl.BlockSpec((None, r, LANES), lambda k, j: (k + off, 0, j))

    def pair_spec(r):
        return pl.BlockSpec((None, 2, r, LANES), lambda k, j: (k, 0, 0, j))

    sd = jax.ShapeDtypeStruct
    return pl.pallas_call(
        body, name="conv_gate_bwd", grid=(half, nc),
        in_specs=[pair_spec(seq), w_spec(0, 3), w_spec(half, 3), w_spec(0, 1), w_spec(half, 1),
                  pl.BlockSpec((seq, LANES), lambda k, j: (0, k * nc + j))],
        out_specs=[pair_spec(seq), pair_spec(3), pair_spec(1)],
        out_shape=[sd((half, 2, seq, c), BF16), sd((half, 2, 3, c), F32), sd((half, 2, 1, c), F32)],
        compiler_params=_cparams(),
    )(a, cw, cw, cb, cb, dm)


ROW_T = 256


def _local_step(x, positions, target, w, emit=lambda **grads: None):
    seq, d = x.shape
    t_row = min(ROW_T, seq)
    nrow = seq // t_row
    ssm_w = d // 2
    nj = ssm_w // LANES
    n_groups = ssm_w // SSM_GROUP
    nh = w["wuq"].shape[0]
    q_rank = w["wuq"].shape[1]
    kv_rank = w["wukv"].shape[1]
    ns = w["wup"].shape[0]
    c_ff = w["wup"].shape[2]
    in_pad = w["win"].shape[1]
    tm = min(1024, seq)
    nm = seq // tm
    sw = 2 * STATE_BLOCK
    g1 = (nrow,)

    lr3 = w["lam_re"].reshape(n_groups, 1, SSM_STATE)
    li3 = w["lam_im"].reshape(n_groups, 1, SSM_STATE)
    ldt3 = w["log_dt"].reshape(n_groups, 1, 1)
    bt_re = jnp.swapaxes(w["b_re"].reshape(n_groups, SSM_STATE, SSM_GROUP), 1, 2)
    bt_im = jnp.swapaxes(w["b_im"].reshape(n_groups, SSM_STATE, SSM_GROUP), 1, 2)
    abar_re, abar_im, bbt_re, bbt_im = _s5_prep(lr3, li3, ldt3, bt_re, bt_im)
    eye = jnp.eye(GROUPS_PER_BLOCK, dtype=F32)

    def blockdiag_in(bb):
        t = bb.reshape(nj, GROUPS_PER_BLOCK, SSM_GROUP, SSM_STATE)
        return jnp.einsum("jghp,gk->jghkp", t, eye).reshape(nj, LANES, STATE_BLOCK)

    def blockdiag_in_t(dwb):
        t = dwb.reshape(nj, GROUPS_PER_BLOCK, SSM_GROUP, GROUPS_PER_BLOCK, SSM_STATE)
        return jnp.einsum("jghkp,gk->jghp", t, eye).reshape(n_groups, SSM_GROUP, SSM_STATE)

    def blockdiag_out(cc):
        t = cc.reshape(nj, GROUPS_PER_BLOCK, SSM_GROUP, SSM_STATE)
        return jnp.einsum("jghp,gk->jkpgh", t, eye).reshape(nj, STATE_BLOCK, LANES)

    def blockdiag_out_t(dwc):
        t = dwc.reshape(nj, GROUPS_PER_BLOCK, SSM_STATE, GROUPS_PER_BLOCK, SSM_GROUP)
        return jnp.einsum("jkpgh,gk->jghp", t, eye).reshape(n_groups, SSM_GROUP, SSM_STATE)

    c_re = w["c_re"].reshape(n_groups, SSM_GROUP, SSM_STATE)
    c_im = w["c_im"].reshape(n_groups, SSM_GROUP, SSM_STATE)
    wb = jnp.concatenate([blockdiag_in(bbt_re), blockdiag_in(bbt_im)], axis=2).astype(BF16)
    wc = jnp.concatenate([blockdiag_out(c_re), -blockdiag_out(c_im)], axis=1).astype(BF16)
    a_lay = jnp.concatenate([abar_re.reshape(nj, 1, STATE_BLOCK), abar_im.reshape(nj, 1, STATE_BLOCK)],
                            axis=1).reshape(1, nj * sw)

    attn_w = w["attn_norm"]
    t_wide = min(2 * t_row, seq)
    g_wide = (seq // t_wide,)

    def proj_fn(xb, wv, wi):
        hb = _rms(xb, wv).astype(BF16)
        return hb, jnp.dot(hb, wi, preferred_element_type=F32)

    hn, proj = _blockwise(
        "norm1_proj", proj_fn, [x, attn_w, w["win"]],
        [_row_spec(t_wide, d), _full_spec((1, d)), _full_spec((d, in_pad), single=True)],
        [((seq, d), BF16), ((seq, in_pad), F32)], [_row_spec(t_wide, d), _row_spec(t_wide, in_pad)], g_wide)

    s_all, ylin = _ssm_fwd(proj, wb, wc, a_lay)

    def glu_fwd_fn(yl, ub, dsk, wg, bg):
        yp = yl + dsk * ub
        ygv = jax.nn.gelu(yp)
        ygb = ygv.astype(BF16)
        zb = jnp.dot(ygb, wg, preferred_element_type=F32) + bg
        return yp, ygb, zb, ygv * jax.nn.sigmoid(zb)

    wide = pl.BlockSpec((t_wide, ssm_w), lambda i: (i, 0))
    y_pre, yg, z, y_ssm = _blockwise(
        "ssm_glu_fwd", glu_fwd_fn, [ylin, proj, w["ssm_d"], w["wglu"], w["b_glu"]],
        [wide, wide, _full_spec((1, ssm_w)), _full_spec((ssm_w, ssm_w), single=True), _full_spec((1, ssm_w))],
        [((seq, ssm_w), F32), ((seq, ssm_w), BF16), ((seq, ssm_w), F32), ((seq, ssm_w), F32)], [wide] * 4, g_wide)

    cq_off, ckv_off, kpe_off = ssm_w, ssm_w + q_rank, ssm_w + q_rank + kv_rank
    assert cq_off % q_rank == 0 and ckv_off % kv_rank == 0 and kpe_off % LANES == 0
    cq_spec = pl.BlockSpec((t_row, q_rank), lambda i: (i, cq_off // q_rank))
    ckv_spec = pl.BlockSpec((t_row, kv_rank), lambda i: (i, ckv_off // kv_rank))
    kpe_spec = pl.BlockSpec((t_row, LANES), lambda i: (i, kpe_off // LANES))
    pos_b = jnp.broadcast_to(positions.astype(F32)[:, None], (seq, LANES))
    inv_freq = ROPE_THETA ** (-jnp.arange(0, QK_ROPE, 2, dtype=F32) / QK_ROPE)
    inv128 = jnp.tile(inv_freq, 4).reshape(1, LANES)

    def mla_prep_fn(cq, ckv, kp, pb, inv, wq, wkv):
        ang = pb * inv
        lane = lax.broadcasted_iota(jnp.int32, ang.shape, 1)
        cs, sn = jnp.cos(ang), jnp.sin(ang)
        cos = jnp.where(lane < QK_ROPE, cs, 0.0)
        sa = jnp.where(lane < QK_ROPE // 2, -sn, 0.0)
        sb = jnp.where(jnp.logical_and(lane >= QK_ROPE // 2, lane < QK_ROPE), sn, 0.0)
        return _rms(cq, wq), _rms(ckv, wkv), _rope128(kp, cos, sa, sb), cos, sa, sb

    qn, kvn, kpe, cos_t, sa_t, sb_t = _blockwise(
        "mla_prep", mla_prep_fn, [proj, proj, proj, pos_b, inv128, w["q_norm"], w["kv_norm"]],
        [cq_spec, ckv_spec, kpe_spec, _row_spec(t_row, LANES),
         _full_spec((1, LANES)), _full_spec((1, q_rank)), _full_spec((1, kv_rank))],
        [((seq, q_rank), BF16), ((seq, kv_rank), BF16), ((seq, LANES), BF16)] + [((seq, LANES), F32)] * 3,
        [_row_spec(t_row, q_rank), _row_spec(t_row, kv_rank)] + [_row_spec(t_row, LANES)] * 4, g1)

    def head_mm(name, act, wh, out_dtype):
        kdim, ndim = wh.shape[1], wh.shape[2]
        return _mm(name, act, wh, grid=(nh, 1, 1),
                   a_spec=pl.BlockSpec((seq, kdim), lambda h, i, k: (i, 0)),
                   b_spec=pl.BlockSpec((None, kdim, ndim), lambda h, i, k: (h, 0, 0)),
                   o_spec=pl.BlockSpec((None, seq, ndim), lambda h, i, k: (h, i, 0)),
                   out_shape=(nh, seq, ndim), out_dtype=out_dtype)

    q_raw = head_mm("mla_q", qn, w["wuq"], F32)
    kv = head_mm("mla_kv", kvn, w["wukv"], BF16)
    y_mla = _attn_fwd(q_raw, kv, kpe, cos_t, sa_t, sb_t)
    mla_w = nh * V_DIM

    def out_proj_fn(ys, ym, ws, wm, wo, xb, wf):
        yc = jnp.concatenate([_rms(ys, ws), _rms(ym, wm)], axis=1).astype(BF16)
        hb = xb + jnp.dot(yc, wo, preferred_element_type=F32)
        return yc, hb, _rms(hb, wf)

    ycat, h1, hn2 = _blockwise(
        "out_norm_proj", out_proj_fn, [y_ssm, y_mla, w["son"], w["mon"], w["wout"], x, w["ffn_norm"]],
        [wide, _row_spec(t_wide, mla_w), _full_spec((1, ssm_w)), _full_spec((1, mla_w)),
         _full_spec((d, d), single=True), _row_spec(t_wide, d), _full_spec((1, d))],
        [((seq, d), BF16), ((seq, d), F32), ((seq, d), BF16)], [_row_spec(t_wide, d)] * 3, g_wide)

    tku = d
    half = ns // 2
    a_ff = _mm("ffn_up", hn2, w["wup"], grid=(ns, nm, d // tku),
               a_spec=pl.BlockSpec((tm, tku), lambda s, i, k: (i, k)),
               b_spec=pl.BlockSpec((None, tku, c_ff), lambda s, i, k: (s, k, 0)),
               o_spec=pl.BlockSpec((None, None, tm, c_ff), lambda s, i, k: (s % half, s // half, i, 0)),
               out_shape=(half, 2, seq, c_ff), out_dtype=F32)
    cb3 = w["conv_b"].reshape(ns, 1, c_ff)
    m_ff = _conv_gate_fwd(a_ff, w["conv_w"], cb3)
    d_ff = half * c_ff
    wdn = w["wdown"]
    tnd = _tile(d, 1024)
    tmx, tnx = min(1024, seq), _tile(d, 1024)
    h2 = _mm2d("ffn_down", m_ff, wdn, NN, F32, tm=512, tn=512, tk=d_ff, res=h1)

    def loss_fn(hb, tb, wv):
        def f(hh, ww):
            err = _rms(hh, ww) - tb
            return 0.5 * jnp.sum(jnp.mean(err * err, axis=-1))

        lossv, (dh, dw) = jax.value_and_grad(f, argnums=(0, 1))(hb, wv)
        return dh, dh, jnp.full((1, LANES), lossv, F32), dw

    fin_w = w["final_norm"].reshape(1, d)
    dh2, dh2b, loss_acc, g_final = _blockwise(
        "loss_head", loss_fn, [h2, target, fin_w], [_row_spec(t_row, d), _row_spec(t_row, d), _full_spec((1, d))],
        [((seq, d), F32), ((seq, d), BF16), ((1, LANES), F32), ((1, d), F32)],
        [_row_spec(t_row, d), _row_spec(t_row, d), _full_spec((1, LANES)), _full_spec((1, d))], g1, n_acc=2)
    loss = loss_acc

    dm = _mm2d("ffn_down_dx", dh2b, wdn, NT, F32, tn=c_ff)
    tks = seq
    g_wdown = _mm2d("ffn_down_dw", m_ff, dh2b, TN, BF16, tm=c_ff)
    emit(wdown=g_wdown)
    da_ff, g_convw2, g_convb2 = _conv_gate_bwd(a_ff, w["conv_w"], cb3, dm)
    g_convw = jnp.swapaxes(g_convw2, 0, 1).reshape(ns, 3, c_ff)
    g_convb = jnp.swapaxes(g_convb2, 0, 1).reshape(ns, 1, c_ff)
    g_wup = _mm("ffn_up_dw", hn2, da_ff, grid=(ns, d // tnd, seq // tks), contract=TN,
                a_spec=pl.BlockSpec((tks, tnd), lambda s, j, k: (k, j)),
                b_spec=pl.BlockSpec((None, None, tks, c_ff), lambda s, j, k: (s % half, s // half, k, 0)),
                o_spec=pl.BlockSpec((None, tnd, c_ff), lambda s, j, k: (s, j, 0)),
                out_shape=(ns, d, c_ff), out_dtype=BF16)
    emit(wup=g_wup)
    dhn2 = _mm("ffn_up_dx", da_ff, w["wup"], grid=(seq // tmx, d // tnx, ns), contract=NT,
               a_spec=pl.BlockSpec((None, None, tmx, c_ff), lambda i, j, s: (s % half, s // half, i, 0)),
               b_spec=pl.BlockSpec((None, tnx, c_ff), lambda i, j, s: (s, j, 0)),
               o_spec=pl.BlockSpec((tmx, tnx), lambda i, j, s: (i, j)),
               out_shape=(seq, d), out_dtype=F32)
    emit(wup_pair_sums_after=dhn2)

    def norm_bwd_fn(hb, dres, dn, wv):
        dx_, dw_ = _rms_bwd(hb, wv, dn)
        dtot = dres + dx_
        return dtot, dtot, dw_

    dh1, dh1b, g_ffn_norm = _blockwise(
        "norm2_bwd", norm_bwd_fn, [h1, dh2, dhn2, w["ffn_norm"]],
        [_row_spec(t_row, d)] * 3 + [_full_spec((1, d))],
        [((seq, d), F32), ((seq, d), BF16), ((1, d), F32)],
        [_row_spec(t_row, d), _row_spec(t_row, d), _full_spec((1, d))], g1, n_acc=1)

    g_wout = _mm2d("out_proj_dw", ycat, dh1b, TN, BF16)

    def outnorm_bwd_fn(dhb, wo, ys, ym, ws, wm):
        dyc = lax.dot_general(dhb, wo, (NT, ((), ())), preferred_element_type=F32)
        dys, dws = _rms_bwd(ys, ws, dyc[:, :ssm_w])
        dym, dwm = _rms_bwd(ym, wm, dyc[:, ssm_w:])
        return dys, dym, dws, dwm

    dy_ssm, dy_mla, g_son, g_mon = _blockwise(
        "out_proj_dx_norm_bwd", outnorm_bwd_fn, [dh1b, w["wout"], y_ssm, y_mla, w["son"], w["mon"]],
        [_row_spec(t_wide, d), _full_spec((d, d), single=True), wide, _row_spec(t_wide, mla_w),
         _full_spec((1, ssm_w)), _full_spec((1, mla_w))],
        [((seq, ssm_w), F32), ((seq, mla_w), F32), ((1, ssm_w), F32), ((1, mla_w), F32)],
        [wide, _row_spec(t_wide, mla_w), _full_spec((1, ssm_w)), _full_spec((1, mla_w))],
        g_wide, n_acc=2)

    def glu_bwd_fn(dy, yp, zb, ub, dsk, wg):
        ygv = jax.nn.gelu(yp)
        sg = jax.nn.sigmoid(zb)
        dz = dy * ygv * sg * (1.0 - sg)
        dzb = dz.astype(BF16)
        dyg = dy * sg + lax.dot_general(dzb, wg, (NT, ((), ())), preferred_element_type=F32)
        _, vjp = jax.vjp(jax.nn.gelu, yp)
        dyp = vjp(dyg)[0]
        return (dzb, dyp, dyp * dsk, jnp.sum(dz, axis=0, keepdims=True), jnp.sum(dyp * ub, axis=0, keepdims=True))

    dz, dy_pre, du1, g_bglu, g_ssmd = _blockwise(
        "ssm_glu_bwd", glu_bwd_fn, [dy_ssm, y_pre, z, proj, w["ssm_d"], w["wglu"]],
        [wide] * 4 + [_full_spec((1, ssm_w)), _full_spec((ssm_w, ssm_w), single=True)],
        [((seq, ssm_w), BF16), ((seq, ssm_w), BF16), ((seq, ssm_w), F32), ((1, ssm_w), F32), ((1, ssm_w), F32)],
        [wide] * 3 + [_full_spec((1, ssm_w))] * 2, g_wide, n_acc=2)
    g_wglu = _mm2d("ssm_glu_dw", yg, dz, TN, BF16)
    dq_raw, dkv, dkp_h = _attn_bwd(q_raw, kv, kpe, cos_t, sa_t, sb_t, dy_mla)

    def head_mm_dx(name, dact, wh):
        kdim, ndim = wh.shape[1], wh.shape[2]
        return _mm(name, dact, wh, grid=(1, 1, nh), contract=NT,
                   a_spec=pl.BlockSpec((None, seq, ndim), lambda i, j, h: (h, i, 0)),
                   b_spec=pl.BlockSpec((None, kdim, ndim), lambda i, j, h: (h, 0, 0)),
                   o_spec=pl.BlockSpec((seq, kdim), lambda i, j, h: (i, 0)),
                   out_shape=(seq, kdim), out_dtype=F32)

    def head_mm_dw(name, act, dact):
        kdim, ndim = act.shape[1], dact.shape[2]
        return _mm(name, act, dact, grid=(nh, 1, seq // tks), contract=TN,
                   a_spec=pl.BlockSpec((tks, kdim), lambda h, j, k: (k, 0)),
                   b_spec=pl.BlockSpec((None, tks, ndim), lambda h, j, k: (h, k, 0)),
                   o_spec=pl.BlockSpec((None, kdim, ndim), lambda h, j, k: (h, 0, 0)),
                   out_shape=(nh, kdim, ndim), out_dtype=BF16)

    g_wuq = head_mm_dw("mla_q_dw", qn, dq_raw)
    g_wukv = head_mm_dw("mla_kv_dw", kvn, dkv)
    dqn = head_mm_dx("mla_q_dx", dq_raw, w["wuq"])
    dkvn = head_mm_dx("mla_kv_dx", dkv, w["wukv"])
    emit(not_before=(dqn, dkvn, dy_pre), wout=g_wout, wuq=g_wuq, wukv=g_wukv, wglu=g_wglu, conv_w=g_convw)

    du, dwb, dwc, da_lay = _ssm_bwd(dy_pre, s_all, proj, du1, wb, wc, a_lay)
    g_c_re = blockdiag_out_t(dwc[:, :STATE_BLOCK, :])
    g_c_im = -blockdiag_out_t(dwc[:, STATE_BLOCK:, :])
    dbbt_re = blockdiag_in_t(dwb[:, :, :STATE_BLOCK])
    dbbt_im = blockdiag_in_t(dwb[:, :, STATE_BLOCK:])
    da3 = da_lay.reshape(nj, 2, STATE_BLOCK)
    dabar_re = da3[:, 0, :].reshape(n_groups, 1, SSM_STATE)
    dabar_im = da3[:, 1, :].reshape(n_groups, 1, SSM_STATE)
    g_lr3, g_li3, g_ldt3, g_bt_re, g_bt_im = _s5_prep_bwd(lr3, li3, ldt3, bt_re, bt_im,
                                                           dabar_re, dabar_im, dbbt_re, dbbt_im)

    def mla_prep_bwd_fn(cq, ckv, dqn_b, dkvn_b, dkp_b, cos, sa, sb, wq, wkv):
        dcq, dwq = _rms_bwd(cq, wq, dqn_b)
        dckv, dwkv = _rms_bwd(ckv, wkv, dkvn_b)
        dkp_sum = dkp_b[0]
        for h in range(1, nh):
            dkp_sum = dkp_sum + dkp_b[h]
        return dcq, dckv, _rope128_t(dkp_sum, cos, sa, sb), dwq, dwkv

    dc_q, dc_kv, dkpe_raw, g_qnorm, g_kvnorm = _blockwise(
        "mla_prep_bwd", mla_prep_bwd_fn, [proj, proj, dqn, dkvn, dkp_h, cos_t, sa_t, sb_t, w["q_norm"], w["kv_norm"]],
        [cq_spec, ckv_spec, _row_spec(t_row, q_rank), _row_spec(t_row, kv_rank),
         pl.BlockSpec((nh, t_row, LANES), lambda i: (0, i, 0))] + [_row_spec(t_row, LANES)] * 3
        + [_full_spec((1, q_rank)), _full_spec((1, kv_rank))],
        [((seq, q_rank), BF16), ((seq, kv_rank), BF16), ((seq, LANES), BF16), ((1, q_rank), F32), ((1, kv_rank), F32)],
        [_row_spec(t_row, q_rank), _row_spec(t_row, kv_rank), _row_spec(t_row, LANES), _full_spec((1, q_rank)),
         _full_spec((1, kv_rank))], g1, n_acc=2)

    dproj = jnp.concatenate([du, dc_q, dc_kv, dkpe_raw], axis=1)
    g_win = _mm2d("proj_dw", hn, dproj, TN, BF16, tn=640)
    emit(win=g_win)
    def norm1_bwd_fn(dpb, wi, xb, dres, wv):
        dn = lax.dot_general(dpb, wi, (NT, ((), ())), preferred_element_type=F32)
        dx_, dw_ = _rms_bwd(xb, wv, dn)
        return dres + dx_, dw_

    grad_x, g_attn_norm = _blockwise(
        "proj_dx_norm1_bwd", norm1_bwd_fn, [dproj, w["win"], x, dh1, attn_w],
        [_row_spec(t_row, in_pad), _full_spec((d, in_pad), single=True), _row_spec(t_row, d), _row_spec(t_row, d), _full_spec((1, d))],
        [((seq, d), F32), ((1, d), F32)], [_row_spec(t_row, d), _full_spec((1, d))], g1, n_acc=1)
    emit(win_pair_sums_after=grad_x)

    grads = dict(
        attn_norm=g_attn_norm, win=g_win, lam_re=g_lr3, lam_im=g_li3, log_dt=g_ldt3,
        bt_re=g_bt_re, bt_im=g_bt_im, c_re=g_c_re, c_im=g_c_im,
        ssm_d=g_ssmd, wglu=g_wglu, b_glu=g_bglu, q_norm=g_qnorm, wuq=g_wuq, kv_norm=g_kvnorm, wukv=g_wukv,
        son=g_son, mon=g_mon, wout=g_wout, ffn_norm=g_ffn_norm, wup=g_wup, conv_w=g_convw, conv_b=g_convb,
        wdown=g_wdown, final_norm=g_final)
    return loss, grad_x, grads


def _mesh_pos():
    return lax.axis_index("x"), lax.axis_index("y"), lax.axis_index("c")


def _handshake_all():
    x, y, c = _mesh_pos()
    barrier = pltpu.get_barrier_semaphore()
    for k in range(1, N_DEV):
        peer = (1 - x if k & 4 else x, 1 - y if k & 2 else y, 1 - c if k & 1 else c)
        pl.semaphore_signal(barrier, inc=1, device_id=peer, device_id_type=MESH)
    pl.semaphore_wait(barrier, N_DEV - 1)


def _handshake(peers):
    barrier = pltpu.get_barrier_semaphore()
    for peer in peers:
        pl.semaphore_signal(barrier, inc=1, device_id=peer, device_id_type=MESH)
    pl.semaphore_wait(barrier, len(peers))


def _comm_call(name, body, n, out_shape, ins, collective_id, after=None, copies=7, n_remote=None, n_local=None):
    n_remote = copies * n if n_remote is None else n_remote
    sems = [pltpu.SemaphoreType.DMA((n_remote,)), pltpu.SemaphoreType.DMA((n_remote,)),
            pltpu.SemaphoreType.DMA((n if n_local is None else n_local,))]
    if collective_id is None:
        any_spec = pl.BlockSpec(memory_space=pl.ANY)
        return pl.pallas_call(body, name=name, out_shape=out_shape, in_specs=[any_spec] * n,
                              out_specs=[any_spec] * n, scratch_shapes=sems)(*ins)
    seq_body = body
    if after:
        n_after = len(after)
        ins = list(ins) + list(after)

        def seq_body(*refs):
            body(*refs[:n], *refs[n + n_after:])

    return pl.kernel(seq_body, name=name, out_type=out_shape,
                     mesh=plsc.ScalarSubcoreMesh(axis_name="seq", num_cores=1), scratch_types=sems,
                     compiler_params=pltpu.CompilerParams(collective_id=collective_id))(*ins)


def _all_gather(name, xs, collective_id=None, after=None, pair_sums=()):
    n = len(xs)
    nh = len(pair_sums)
    m = n + nh

    def body(*refs):
        x_refs, h_refs, o_refs, e_refs = refs[:n], refs[n:m], refs[m:m + n], refs[m + n:2 * m]
        send_sems, recv_sems, local_sems = refs[2 * m:]
        if collective_id is not None:
            _handshake_all()
        finish_pairs = _chip_copies(h_refs, e_refs, send_sems, recv_sems, local_sems, 7 * n, n) if nh else None
        x, y, c = _mesh_pos()
        me, sibling = (x, y, c), (x, y, 1 - c)
        chips = [(1 - x, y), (x, 1 - y), (1 - x, 1 - y)]

        def slot(o_ref, px, py, pc):
            return o_ref.at[4 * px + 2 * py + pc]

        def copy(t, k, block, to, src=None):
            dst = slot(o_refs[t], *block)
            return pltpu.make_async_remote_copy(
                src_ref=dst if src is None else src, dst_ref=dst,
                send_sem=send_sems.at[7 * t + k], recv_sem=recv_sems.at[7 * t + k],
                device_id=to, device_id_type=MESH)

        started = []
        for t in range(n):
            mine = pltpu.make_async_copy(x_refs[t], slot(o_refs[t], *me), local_sems.at[t])
            mine.start()
            started.append(mine)
        first = []
        for t in range(n):
            first.append(copy(t, 0, me, sibling, src=x_refs[t]))
            first += [copy(t, 1 + j, me, (*chip, c), src=x_refs[t]) for j, chip in enumerate(chips)]
        for cp in first:
            cp.start()
        passed = []
        for j, chip in enumerate(chips):
            for t in range(n):
                copy(t, 1 + j, (*chip, c), me).wait_recv()
                fwd = copy(t, 4 + j, (*chip, c), sibling)
                fwd.start()
                passed.append(fwd)
        for t in range(n):
            copy(t, 0, sibling, me).wait_recv()
            for j, chip in enumerate(chips):
                copy(t, 4 + j, (*chip, 1 - c), me).wait_recv()
        for cp in first + passed:
            cp.wait_send()
        for mine in started:
            mine.wait()
        if nh:
            finish_pairs()

    out_shape = ([jax.ShapeDtypeStruct((N_DEV,) + v.shape, v.dtype) for v in xs]
                 + [jax.ShapeDtypeStruct(v.shape, v.dtype) for v in pair_sums])
    return _comm_call(name, body, m, out_shape, list(xs) + list(pair_sums), collective_id, after,
                      n_remote=7 * n + (N_CHIP - 1) * nh, n_local=m)


def _exchange_partials(name, gs, collective_id=None, after=None):
    n = len(gs)

    def body(*refs):
        g_refs, o_refs = refs[:n], refs[n:2 * n]
        send_sems, recv_sems, local_sems = refs[2 * n:]
        if collective_id is not None:
            _handshake_all()
        x, y, c = _mesh_pos()
        me_idx = 4 * x + 2 * y + c
        copies = []
        for t in range(n):
            mine = pltpu.make_async_copy(g_refs[t].at[me_idx], o_refs[t].at[me_idx], local_sems.at[t])
            mine.start()
            copies.append(mine)
        remote = []
        for k in range(1, N_DEV):
            px = 1 - x if k & 4 else x
            py = 1 - y if k & 2 else y
            pc = 1 - c if k & 1 else c
            p_idx = 4 * px + 2 * py + pc
            for t in range(n):
                cp = pltpu.make_async_remote_copy(
                    src_ref=g_refs[t].at[p_idx], dst_ref=o_refs[t].at[me_idx],
                    send_sem=send_sems.at[7 * t + k - 1], recv_sem=recv_sems.at[7 * t + k - 1],
                    device_id=(px, py, pc), device_id_type=MESH)
                cp.start()
                landing = pltpu.make_async_remote_copy(
                    src_ref=g_refs[t].at[p_idx], dst_ref=o_refs[t].at[p_idx],
                    send_sem=send_sems.at[7 * t + k - 1], recv_sem=recv_sems.at[7 * t + k - 1],
                    device_id=(px, py, pc), device_id_type=MESH)
                remote.append((cp, landing))
        for cp, landing in remote:
            landing.wait_recv()
        for cp, landing in remote:
            cp.wait_send()
        for mine in copies:
            mine.wait()

    out_shape = [jax.ShapeDtypeStruct(v.shape, v.dtype) for v in gs]
    return _comm_call(name, body, n, out_shape, gs, collective_id, after)


N_CHIP = N_DEV // 2
PAIR_ADD_BLOCK_ELEMS = 1024 * 1024


def _pair_swap(name, gs, collective_id, after=None):
    n = len(gs)

    def body(*refs):
        g_refs, o_refs = refs[:n], refs[n:2 * n]
        send_sems, recv_sems, _ = refs[2 * n:]
        x, y, c = _mesh_pos()
        sibling = (x, y, 1 - c)
        _handshake([sibling])
        copies = []
        for t in range(n):
            for k in range(N_CHIP):
                copies.append(pltpu.make_async_remote_copy(
                    src_ref=g_refs[t].at[2 * k + 1 - c], dst_ref=o_refs[t].at[k],
                    send_sem=send_sems.at[N_CHIP * t + k], recv_sem=recv_sems.at[N_CHIP * t + k],
                    device_id=sibling, device_id_type=MESH))
        for cp in copies:
            cp.start()
        for cp in copies:
            cp.wait_recv()
        for cp in copies:
            cp.wait_send()

    out_shape = [jax.ShapeDtypeStruct((N_CHIP,) + v.shape[1:], v.dtype) for v in gs]
    return _comm_call(name, body, n, out_shape, gs, collective_id, after, copies=N_CHIP)


def _pair_add(name, g, got):
    _, r, c = g.shape
    tr = r
    if r * c > PAIR_ADD_BLOCK_ELEMS and r % SUBLANES == 0:
        tr = SUBLANES
        while r % (tr * 2) == 0 and tr * 2 * c <= PAIR_ADD_BLOCK_ELEMS:
            tr *= 2

    def body(core_ref, g_ref, got_ref, o_ref):
        o_ref[...] = (g_ref[...].astype(F32) + got_ref[...].astype(F32)).astype(o_ref.dtype)

    grid_spec = pltpu.PrefetchScalarGridSpec(
        num_scalar_prefetch=1, grid=(N_CHIP, r // tr),
        in_specs=[pl.BlockSpec((None, None, tr, c), lambda k, i, core: (k, core[0], i, 0)),
                  pl.BlockSpec((None, tr, c), lambda k, i, core: (k, i, 0))],
        out_specs=pl.BlockSpec((None, tr, c), lambda k, i, core: (k, i, 0)))
    core = lax.axis_index("c").astype(jnp.int32).reshape(1)
    return pl.pallas_call(body, name=name, grid_spec=grid_spec, out_shape=jax.ShapeDtypeStruct((N_CHIP, r, c), g.dtype),
                          compiler_params=_cparams())(core, g.reshape(N_CHIP, 2, r, c), got)


def _chip_copies(h_refs, o_refs, send_sems, recv_sems, local_sems, sem0, local0):
    n = len(h_refs)
    per = N_CHIP - 1
    x, y, c = _mesh_pos()
    others = [(1 - x if k & 2 else x, 1 - y if k & 1 else y) for k in range(1, N_CHIP)]
    my_chip = 2 * x + y
    local = []
    for t in range(n):
        mine = pltpu.make_async_copy(h_refs[t].at[my_chip], o_refs[t].at[my_chip], local_sems.at[local0 + t])
        mine.start()
        local.append(mine)
    remote = []
    for j, (px, py) in enumerate(others):
        chip = 2 * px + py
        for t in range(n):
            sems = dict(send_sem=send_sems.at[sem0 + per * t + j], recv_sem=recv_sems.at[sem0 + per * t + j],
                        device_id=(px, py, c), device_id_type=MESH)
            cp = pltpu.make_async_remote_copy(src_ref=h_refs[t].at[chip], dst_ref=o_refs[t].at[my_chip], **sems)
            cp.start()
            landing = pltpu.make_async_remote_copy(src_ref=h_refs[t].at[chip], dst_ref=o_refs[t].at[chip], **sems)
            remote.append((cp, landing))

    def finish():
        for cp, landing in remote:
            landing.wait_recv()
        for cp, landing in remote:
            cp.wait_send()
        for mine in local:
            mine.wait()

    return finish


def _chip_exchange(name, hs, collective_id, after=None):
    n = len(hs)
    per = N_CHIP - 1

    def body(*refs):
        h_refs, o_refs = refs[:n], refs[n:2 * n]
        send_sems, recv_sems, local_sems = refs[2 * n:]
        x, y, c = _mesh_pos()
        _handshake([(1 - x if k & 2 else x, 1 - y if k & 1 else y, c) for k in range(1, N_CHIP)])
        _chip_copies(h_refs, o_refs, send_sems, recv_sems, local_sems, 0, 0)()

    out_shape = [jax.ShapeDtypeStruct(v.shape, v.dtype) for v in hs]
    return _comm_call(name, body, n, out_shape, hs, collective_id, after, copies=per)


ADAM_BLOCK_ELEMS = 256 * 1024


def _sum_parts(pb):
    g = pb[0].astype(F32)
    for j in range(1, pb.shape[0]):
        g = g + pb[j].astype(F32)
    return g


def _adam_math(g, wb_, mb, vb):
    m_new = ADAM_B1 * mb + (1.0 - ADAM_B1) * g
    v_new = ADAM_B2 * vb + (1.0 - ADAM_B2) * (g * g)
    m_hat = m_new / (1.0 - ADAM_B1 ** ADAM_STEP)
    v_hat = v_new / (1.0 - ADAM_B2 ** ADAM_STEP)
    delta = -ADAM_LR * (m_hat / (jnp.sqrt(v_hat) + ADAM_EPS) + ADAM_WD * wb_)
    return g, delta, m_new, v_new


def _adamw_multi(name, items, nblk=1, packed=None):
    n = len(items)

    def spec(shape, lead):
        blk = list(shape)
        blk[lead + 1] = shape[lead + 1] // nblk
        if nblk == 1:
            return pl.BlockSpec(tuple(blk), lambda i, nd=len(shape): (0,) * nd)
        return pl.BlockSpec(tuple(blk), lambda i, nd=len(shape), ax=lead + 1: (0,) * ax + (i,) + (0,) * (nd - ax - 1))

    ins, in_specs, out_specs, out_shape, where = [], [], [], [], []
    if packed is not None:
        ins.append(packed)
        in_specs.append(spec(packed.shape, 1))
    for parts, wv, mv, vv in items:
        if isinstance(parts, int):
            where.append((0, parts, len(ins)))
        else:
            assert parts.shape[1:] == wv.shape, (name, parts.shape, wv.shape)
            where.append((len(ins), None, len(ins) + 1))
            ins.append(parts)
            in_specs.append(spec(parts.shape, 1))
        ins += [wv, mv, vv]
        in_specs += [spec(wv.shape, 0)] * 3
        out_specs += [spec(wv.shape, 0)] * 4
        out_shape += [jax.ShapeDtypeStruct(wv.shape, F32)] * 4
    n_in = len(ins)

    def body(*refs):
        for t, (ip, off, iw) in enumerate(where):
            wr, mr, vr = refs[iw:iw + 3]
            parts = refs[ip][...] if off is None else refs[ip][:, :, off:off + wr.shape[-1]]
            res = _adam_math(_sum_parts(parts), wr[...], mr[...], vr[...])
            for o, val in zip(refs[n_in + 4 * t:n_in + 4 * t + 4], res):
                o[...] = val

    res = pl.pallas_call(body, name=name, grid=(nblk,), in_specs=in_specs, out_specs=out_specs, out_shape=out_shape,
                         compiler_params=_cparams())(*ins)
    return [tuple(res[4 * t:4 * t + 4]) for t in range(n)]


def _sum_multi(name, parts_list):
    def body(*refs):
        for pr, o in zip(refs[:len(parts_list)], refs[len(parts_list):]):
            o[...] = _sum_parts(pr[...])

    return pl.pallas_call(body, name=name, out_shape=[jax.ShapeDtypeStruct(p.shape[1:], F32) for p in parts_list],
                          compiler_params=_cparams())(*parts_list)


def _adamw_sum(name, parts, wv, mv, vv):
    npart, r, c = parts.shape
    tr = r
    if r * c > ADAM_BLOCK_ELEMS and r % SUBLANES == 0:
        tr = SUBLANES
        while r % (tr * 2) == 0 and tr * 2 * c <= ADAM_BLOCK_ELEMS:
            tr *= 2

    def fn(pb, wb_, mb, vb):
        return _adam_math(_sum_parts(pb), wb_, mb, vb)

    row = pl.BlockSpec((tr, c), lambda i: (i, 0))
    return _blockwise(name, fn, [parts, wv, mv, vv],
                      [pl.BlockSpec((npart, tr, c), lambda i: (0, i, 0)), row, row, row],
                      [((r, c), F32)] * 4, [row] * 4, (r // tr,))


_VECTORS = ["attn_norm", "lam_re", "lam_im", "log_dt", "ssm_d", "b_glu", "q_norm", "kv_norm", "son", "mon",
            "ffn_norm", "conv_b", "final_norm"]
_GHP = ["c_re", "c_im", "bt_re", "bt_im"]
_PACKED = ["attn_norm", "ssm_d", "b_glu", "q_norm", "kv_norm", "son", "mon", "ffn_norm", "conv_b", "final_norm"]
_BIG = ["win", "wglu", "wuq", "wukv", "wout", "wup", "wdown", "conv_w"]
_ROWS_IN_LANES = ("win", "wuq")
_TWO_LEVEL = ("wup", "win")
_AFTER = "_pair_sums_after"
_ORDER = ["attn_norm", "win", "lam_re", "lam_im", "log_dt", "b_re", "b_im", "c_re", "c_im", "ssm_d", "wglu",
          "b_glu", "q_norm", "wuq", "kv_norm", "wukv", "son", "mon", "wout", "ffn_norm", "wup", "conv_w",
          "conv_b", "wdown", "final_norm"]


def kernel(x, positions, attn_norm_w, w_in, ssm_lambda_re, ssm_lambda_im, ssm_log_dt, ssm_b_re, ssm_b_im, ssm_c_re, ssm_c_im, ssm_d, ssm_w_glu, ssm_b_glu, mla_q_norm_w, mla_w_uq, mla_kv_norm_w, mla_w_ukv, ssm_out_norm_w, mla_out_norm_w, w_out, ffn_norm_w, ffn_w_up, ffn_conv_w, ffn_conv_b, ffn_w_down, final_norm_w, loss_target, m_attn_norm_w, m_w_in, m_ssm_lambda_re, m_ssm_lambda_im, m_ssm_log_dt, m_ssm_b_re, m_ssm_b_im, m_ssm_c_re, m_ssm_c_im, m_ssm_d, m_ssm_w_glu, m_ssm_b_glu, m_mla_q_norm_w, m_mla_w_uq, m_mla_kv_norm_w, m_mla_w_ukv, m_ssm_out_norm_w, m_mla_out_norm_w, m_w_out, m_ffn_norm_w, m_ffn_w_up, m_ffn_conv_w, m_ffn_conv_b, m_ffn_w_down, m_final_norm_w, v_attn_norm_w, v_w_in, v_ssm_lambda_re, v_ssm_lambda_im, v_ssm_log_dt, v_ssm_b_re, v_ssm_b_im, v_ssm_c_re, v_ssm_c_im, v_ssm_d, v_ssm_w_glu, v_ssm_b_glu, v_mla_q_norm_w, v_mla_w_uq, v_mla_kv_norm_w, v_mla_w_ukv, v_ssm_out_norm_w, v_mla_out_norm_w, v_w_out, v_ffn_norm_w, v_ffn_w_up, v_ffn_conv_w, v_ffn_conv_b, v_ffn_w_down, v_final_norm_w):
    wts = dict(attn_norm=attn_norm_w, win=w_in, lam_re=ssm_lambda_re, lam_im=ssm_lambda_im, log_dt=ssm_log_dt,
               b_re=ssm_b_re, b_im=ssm_b_im, c_re=ssm_c_re, c_im=ssm_c_im, ssm_d=ssm_d, wglu=ssm_w_glu,
               b_glu=ssm_b_glu, q_norm=mla_q_norm_w, wuq=mla_w_uq, kv_norm=mla_kv_norm_w, wukv=mla_w_ukv,
               son=ssm_out_norm_w, mon=mla_out_norm_w, wout=w_out, ffn_norm=ffn_norm_w, wup=ffn_w_up,
               conv_w=ffn_conv_w, conv_b=ffn_conv_b, wdown=ffn_w_down, final_norm=final_norm_w)
    moms = dict(zip(_ORDER, [m_attn_norm_w, m_w_in, m_ssm_lambda_re, m_ssm_lambda_im, m_ssm_log_dt, m_ssm_b_re,
                             m_ssm_b_im, m_ssm_c_re, m_ssm_c_im, m_ssm_d, m_ssm_w_glu, m_ssm_b_glu, m_mla_q_norm_w,
                             m_mla_w_uq, m_mla_kv_norm_w, m_mla_w_ukv, m_ssm_out_norm_w, m_mla_out_norm_w, m_w_out,
                             m_ffn_norm_w, m_ffn_w_up, m_ffn_conv_w, m_ffn_conv_b, m_ffn_w_down, m_final_norm_w]))
    vels = dict(zip(_ORDER, [v_attn_norm_w, v_w_in, v_ssm_lambda_re, v_ssm_lambda_im, v_ssm_log_dt, v_ssm_b_re,
                             v_ssm_b_im, v_ssm_c_re, v_ssm_c_im, v_ssm_d, v_ssm_w_glu, v_ssm_b_glu, v_mla_q_norm_w,
                             v_mla_w_uq, v_mla_kv_norm_w, v_mla_w_ukv, v_ssm_out_norm_w, v_mla_out_norm_w, v_w_out,
                             v_ffn_norm_w, v_ffn_w_up, v_ffn_conv_w, v_ffn_conv_b, v_ffn_w_down, v_final_norm_w]))
    seq, d = x.shape[1], x.shape[2]
    in_width = w_in.shape[2]
    in_pad = -(-in_width // LANES) * LANES
    q_cols = mla_w_uq.shape[2]
    q_pad = 2 * LANES

    (win_g,) = _all_gather("gather_w_in", [jnp.pad(w_in[0], ((0, 0), (0, in_pad - in_width))).astype(BF16)])
    wglu_g, wuq_g, wukv_g, wout_g, convw_g = _all_gather(
        "gather_mix", [ssm_w_glu[0].astype(BF16), jnp.pad(mla_w_uq[0], ((0, 0), (0, q_pad - q_cols))).astype(BF16),
                       mla_w_ukv[0].astype(BF16), w_out[0].astype(BF16), ffn_conv_w[0]], collective_id=0)
    (wup_g,) = _all_gather("gather_ffn_up", [ffn_w_up[0].astype(BF16)], collective_id=1)
    (wdown_g,) = _all_gather("gather_ffn_down", [ffn_w_down[0].astype(BF16)], collective_id=2)
    ns = N_DEV
    c_ff = wup_g.shape[2]
    w = dict(
        attn_norm=attn_norm_w, win=win_g.reshape(d, in_pad), lam_re=ssm_lambda_re, lam_im=ssm_lambda_im,
        log_dt=ssm_log_dt, b_re=ssm_b_re, b_im=ssm_b_im, c_re=ssm_c_re, c_im=ssm_c_im, ssm_d=ssm_d,
        wglu=wglu_g.reshape(d // 2, d // 2), b_glu=ssm_b_glu, q_norm=mla_q_norm_w, wuq=wuq_g,
        kv_norm=mla_kv_norm_w, wukv=wukv_g, son=ssm_out_norm_w, mon=mla_out_norm_w, wout=wout_g.reshape(d, d),
        ffn_norm=ffn_norm_w, wup=wup_g, conv_w=convw_g, conv_b=ffn_conv_b,
        wdown=wdown_g.reshape(ns // 2 * c_ff, d), final_norm=final_norm_w)

    shard_layout = dict(
        win=lambda a: a[:, :in_width].reshape(N_DEV, d // N_DEV, in_width),
        wglu=lambda a: a.reshape(N_DEV, d // 2 // N_DEV, d // 2),
        wuq=lambda a: a[:, :, :q_cols], wukv=lambda a: a, wout=lambda a: a.reshape(N_DEV, d // N_DEV, d),
        wup=lambda a: a, wdown=lambda a: a.reshape(N_DEV, c_ff // 2, d), conv_w=lambda a: a)
    recv = {}
    next_id = [3]

    last = [None]

    out = {}

    def update(k):
        shp = wts[k].shape
        r, c = shp[-2], shp[-1]
        if k in _ROWS_IN_LANES:
            t = lambda a: jnp.swapaxes(a.reshape(-1, r, c), 1, 2)
            res = _adamw_sum("adamw_" + k, t(recv[k]), t(wts[k])[0], t(moms[k])[0], t(vels[k])[0])
            out[k] = [jnp.swapaxes(a, 0, 1).reshape(shp) for a in res]
            return res[0]
        res = _adamw_sum("adamw_" + k, recv[k].reshape(-1, r, c), wts[k].reshape(r, c),
                         moms[k].reshape(r, c), vels[k].reshape(r, c))
        out[k] = [a.reshape(shp) for a in res]
        return res[0]

    pending = {}

    def exchange(not_before=(), **grads):
        names = list(grads)
        if len(names) == 1 and names[0] in _TWO_LEVEL:
            k = names[0]
            parts = shard_layout[k](grads[k])
            got = _pair_swap("swap_" + k, [parts], collective_id=next_id[0], after=[last[0]])[0]
            next_id[0] += 1
            pending[k] = (parts, got)
            last[0] = got
            return
        if len(names) == 1 and names[0].endswith(_AFTER):
            k = names[0][:-len(_AFTER)]
            sums = _pair_add("pair_add_" + k, *pending[k])
            if k == "win":
                pending["tail"] = sums
                return
            recv[k] = _chip_exchange("exchange_" + k, [sums], collective_id=next_id[0],
                                     after=[last[0], grads[names[0]]])[0]
            next_id[0] += 1
            last[0] = recv[k]
            return
        got = _exchange_partials("exchange_" + "_".join(names), [shard_layout[k](grads[k]) for k in names],
                                 collective_id=next_id[0], after=[a for a in (last[0], *not_before) if a is not None])
        next_id[0] += 1
        last[0] = got[-1]
        recv.update(zip(names, got))

    loss_part, grad_x, g = _local_step(x[0], positions[0], loss_target[0], w, emit=exchange)
    n_groups = ssm_lambda_re.shape[1]
    two_d = {"lam_re": (n_groups, -1), "lam_im": (n_groups, -1)}
    dense = {k: g[k].reshape(two_d.get(k, (1, -1))) for k in _VECTORS}
    offsets, width = {}, 0
    for k in _PACKED:
        offsets[k] = width
        width += dense[k].shape[1]
    sent = dict(packed=jnp.concatenate([dense[k] for k in _PACKED], axis=1),
                **{k: dense[k] for k in _VECTORS if k not in _PACKED},
                **{k: g[k].reshape(n_groups, -1).astype(BF16) for k in _GHP},
                loss=loss_part)
    names = list(sent)
    got = _all_gather("gather_small_grads", [sent[k] for k in names], collective_id=next_id[0], after=[last[0]],
                      pair_sums=[pending["tail"]])
    gathered = dict(zip(names, got))
    recv["win"] = got[len(names)]
    for k in _BIG:
        if k not in out and k != "win":
            update(k)
    update("win")

    def finish(keys, results):
        for k, res in zip(keys, results):
            out[k] = [a.reshape(wts[k].shape) for a in res]

    view = lambda k, a: a.reshape(dense[k].shape)
    finish(_VECTORS, _adamw_multi("adamw_vectors", [(offsets.get(k, gathered.get(k)), view(k, wts[k]), view(k, moms[k]),
                                                     view(k, vels[k])) for k in _VECTORS], packed=gathered["packed"]))
    summed = _GHP + ["loss"]
    sums = dict(zip(summed, _sum_multi("sum_ssm_bc_loss", [gathered[k] for k in summed])))
    loss = sums["loss"][0, 0]
    ghp = lambda k: sums[k].reshape(g[k].shape)
    t_hp = lambda a: jnp.swapaxes(a, 2, 3)
    bc_keys = ["c_re", "c_im", "b_re", "b_im"]
    items = [(ghp(k)[None, None], wts[k], moms[k], vels[k]) for k in bc_keys[:2]]
    items += [(ghp(t)[None, None], t_hp(wts[k]), t_hp(moms[k]), t_hp(vels[k]))
              for k, t in zip(bc_keys[2:], ("bt_re", "bt_im"))]
    res = _adamw_multi("adamw_ssm_bc", items)
    finish(bc_keys, res[:2] + [tuple(t_hp(a) for a in r) for r in res[2:]])

    grad_x = grad_x.reshape(x.shape)
    return (loss, grad_x, *[out[k][0] for k in _ORDER], *[out[k][1] for k in _ORDER],
            *[out[k][2] for k in _ORDER], *[out[k][3] for k in _ORDER])
```

```python
import functools

import jax
import jax.numpy as jnp
from jax import lax
from jax.experimental import pallas as pl
from jax.experimental.pallas import tpu as pltpu
from jax.experimental.pallas import tpu_sc as plsc

F32 = jnp.float32
BF16 = jnp.bfloat16
MESH = pl.DeviceIdType.MESH

N_DEV = 8
LANES = 128
SUBLANES = 8
VMEM_LIMIT = 48 * 1024 * 1024

SSM_GROUP = 16
SSM_STATE = 64
GROUPS_PER_BLOCK = LANES // SSM_GROUP
STATE_BLOCK = GROUPS_PER_BLOCK * SSM_STATE
QK_NOPE = 128
QK_ROPE = 64
V_DIM = 128
ROPE_THETA = 10000.0
RMS_EPS = 1e-6

ADAM_LR = 0.001
ADAM_B1 = 0.9
ADAM_B2 = 0.999
ADAM_EPS = 1e-08
ADAM_WD = 0.01
ADAM_STEP = 10

NN = ((1,), (0,))
NT = ((1,), (1,))
TN = ((0,), (0,))


def _cparams():
    return pltpu.CompilerParams(vmem_limit_bytes=VMEM_LIMIT)


def _tile(n, want):
    if n <= want:
        return n
    t = (want // LANES) * LANES
    while t >= LANES:
        if n % t == 0:
            return t
        t -= LANES
    return n


def _mm(name, a, b, *, grid, a_spec, b_spec, o_spec, out_shape, out_dtype, contract=NN,
        res=None, res_spec=None):
    nk = grid[-1]
    kaxis = len(grid) - 1
    acc_shape = tuple(d for d in o_spec.block_shape if d is not None)

    def body(*refs):
        a_ref, b_ref = refs[:2]
        r_ref = None if res is None else refs[2]
        o_ref = refs[2 if res is None else 3]
        part = lax.dot_general(a_ref[...].astype(BF16), b_ref[...].astype(BF16),
                               (contract, ((), ())), preferred_element_type=F32)
        if nk == 1:
            if r_ref is not None:
                part = part + r_ref[...].astype(F32)
            o_ref[...] = part.astype(o_ref.dtype)
            return
        acc = refs[-1]
        k = pl.program_id(kaxis)

        @pl.when(k == 0)
        def _():
            acc[...] = part

        @pl.when(k != 0)
        def _():
            acc[...] += part

        @pl.when(k == nk - 1)
        def _():
            r = acc[...]
            if r_ref is not None:
                r = r + r_ref[...].astype(F32)
            o_ref[...] = r.astype(o_ref.dtype)

    ins = [a, b] + ([] if res is None else [res])
    in_specs = [a_spec, b_spec] + ([] if res is None else [res_spec])
    return pl.pallas_call(
        body, name=name, grid=grid, in_specs=in_specs, out_specs=o_spec,
        out_shape=jax.ShapeDtypeStruct(out_shape, out_dtype),
        scratch_shapes=[pltpu.VMEM(acc_shape, F32)] if nk > 1 else [], compiler_params=_cparams(),
    )(*ins)


def _mm2d(name, a, b, contract, out_dtype, tm=1024, tn=1024, tk=2048, res=None):
    if contract == NN:
        (m, kk), n = a.shape, b.shape[1]
    elif contract == NT:
        (m, kk), n = a.shape, b.shape[0]
    else:
        (kk, m), n = a.shape, b.shape[1]
    tm, tn, tk = _tile(m, tm), _tile(n, tn), _tile(kk, tk)
    grid = (m // tm, n // tn, kk // tk)
    if contract == TN:
        a_spec = pl.BlockSpec((tk, tm), lambda i, j, k: (k, i))
    else:
        a_spec = pl.BlockSpec((tm, tk), lambda i, j, k: (i, k))
    if contract == NT:
        b_spec = pl.BlockSpec((tn, tk), lambda i, j, k: (j, k))
    else:
        b_spec = pl.BlockSpec((tk, tn), lambda i, j, k: (k, j))
    o_spec = pl.BlockSpec((tm, tn), lambda i, j, k: (i, j))
    res_spec = None
    if res is not None:
        if res.shape[0] == 1:
            res_spec = pl.BlockSpec((1, tn), lambda i, j, k: (0, j))
        else:
            res_spec = pl.BlockSpec((tm, tn), lambda i, j, k: (i, j))
    return _mm(name, a, b, grid=grid, a_spec=a_spec, b_spec=b_spec, o_spec=o_spec,
               out_shape=(m, n), out_dtype=out_dtype, contract=contract, res=res, res_spec=res_spec)


def _blockwise(name, fn, ins, in_specs, outs, out_specs, grid, n_acc=0, acc_all=True):
    n_in, n_out = len(ins), len(outs)
    n_plain = n_out - n_acc

    def body(*refs):
        vals = fn(*[r[...] for r in refs[:n_in]])
        if not isinstance(vals, (tuple, list)):
            vals = (vals,)
        o_refs = refs[n_in:n_in + n_out]
        for r, v in zip(o_refs[:n_plain], vals[:n_plain]):
            r[...] = v.astype(r.dtype)
        if n_acc:
            if acc_all:
                first = functools.reduce(jnp.logical_and, [pl.program_id(d) == 0 for d in range(len(grid))])
            else:
                first = pl.program_id(len(grid) - 1) == 0

            @pl.when(first)
            def _():
                for r, v in zip(o_refs[n_plain:], vals[n_plain:]):
                    r[...] = v.astype(r.dtype)

            @pl.when(jnp.logical_not(first))
            def _():
                for r, v in zip(o_refs[n_plain:], vals[n_plain:]):
                    r[...] += v.astype(r.dtype)

    return pl.pallas_call(
        body, name=name, grid=grid, in_specs=in_specs, out_specs=out_specs,
        out_shape=[jax.ShapeDtypeStruct(s, d) for s, d in outs], compiler_params=_cparams(),
    )(*ins)


def _row_spec(t, c):
    return pl.BlockSpec((t, c), lambda i: (i, 0))


def _full_spec(shape, single=False):
    nd = len(shape)
    if single:
        return pl.BlockSpec(tuple(shape), lambda *g: (0,) * nd, pipeline_mode=pl.Buffered(1))
    return pl.BlockSpec(tuple(shape), lambda *g: (0,) * nd)


def _rms(xf, w):
    return xf * lax.rsqrt(jnp.mean(xf * xf, axis=-1, keepdims=True) + RMS_EPS) * w


def _rms_bwd(xf, w, dy):
    _, vjp = jax.vjp(_rms, xf, w)
    return vjp(dy)


def _s5_disc(lr, li, ldt, bre, bim):
    dt = jnp.exp(ldt)
    mag = jnp.exp(lr * dt)
    ar = mag * jnp.cos(li * dt)
    ai = mag * jnp.sin(li * dt)
    nr, ni = ar - 1.0, ai
    den = lr * lr + li * li
    zr = (nr * lr + ni * li) / den
    zi = (ni * lr - nr * li) / den
    return ar, ai, zr * bre - zi * bim, zr * bim + zi * bre


def _s5_prep(lr, li, ldt, bre, bim):
    def body(lr_r, li_r, ldt_r, bre_r, bim_r, ar_r, ai_r, br_r, bi_r):
        ar, ai, br, bi = _s5_disc(lr_r[...], li_r[...], ldt_r[...], bre_r[...], bim_r[...])
        ar_r[...] = ar
        ai_r[...] = ai
        br_r[...] = br
        bi_r[...] = bi

    sd = jax.ShapeDtypeStruct
    return pl.pallas_call(
        body, name="s5_prep",
        out_shape=[sd(lr.shape, F32), sd(lr.shape, F32), sd(bre.shape, F32), sd(bre.shape, F32)],
        compiler_params=_cparams(),
    )(lr, li, ldt, bre, bim)


def _s5_prep_bwd(lr, li, ldt, bre, bim, dar, dai, dbr, dbi):
    def body(lr_r, li_r, ldt_r, bre_r, bim_r, dar_r, dai_r, dbr_r, dbi_r, o0, o1, o2, o3, o4):
        _, vjp = jax.vjp(_s5_disc, lr_r[...], li_r[...], ldt_r[...], bre_r[...], bim_r[...])
        g = vjp((dar_r[...], dai_r[...], dbr_r[...], dbi_r[...]))
        for o, v in zip((o0, o1, o2, o3, o4), g):
            o[...] = v

    sd = jax.ShapeDtypeStruct
    return pl.pallas_call(
        body, name="s5_prep_bwd",
        out_shape=[sd(lr.shape, F32), sd(li.shape, F32), sd(ldt.shape, F32), sd(bre.shape, F32), sd(bim.shape, F32)],
        compiler_params=_cparams(),
    )(lr, li, ldt, bre, bim, dar, dai, dbr, dbi)


SCAN_T = 256


def _scan_tables(ar, ai, tab_r, tab_i, sub, reverse):
    pr, pi = ar, ai
    for k in range(sub):
        row = sub - 1 - k if reverse else k
        tab_r[row:row + 1, :] = pr
        tab_i[row:row + 1, :] = pi
        pr, pi = ar * pr - ai * pi, ar * pi + ai * pr


def _pack_matrix(t_blk, dtype):
    sub = t_blk // SUBLANES
    dst = jnp.arange(t_blk)
    src = (dst % SUBLANES) * sub + dst // SUBLANES
    return (src[:, None] == jnp.arange(t_blk)[None, :]).astype(dtype)


def _permute_rows_f32(pm, x):
    hi = x.astype(BF16)
    r1 = x - hi.astype(F32)
    mid = r1.astype(BF16)
    lo = (r1 - mid.astype(F32)).astype(BF16)
    dot = lambda v: jnp.dot(pm, v, preferred_element_type=F32)
    return dot(hi) + dot(mid) + dot(lo)


def _scan_block(x, loc, ar, ai, st, tab_r, tab_i, sub, reverse):
    hb = STATE_BLOCK
    a8r = jnp.broadcast_to(ar, (SUBLANES, hb))
    a8i = jnp.broadcast_to(ai, (SUBLANES, hb))
    sr = jnp.zeros((SUBLANES, hb), F32)
    si = jnp.zeros((SUBLANES, hb), F32)
    steps = range(sub - 1, -1, -1) if reverse else range(sub)
    for t in steps:
        rows = slice(t * SUBLANES, (t + 1) * SUBLANES)
        sr, si = a8r * sr - a8i * si + x[rows, :hb], a8r * si + a8i * sr + x[rows, hb:]
        loc[rows, :hb] = sr
        loc[rows, hb:] = si
    cr, ci = st[0:1, :], st[1:2, :]
    far = 0 if reverse else sub - 1
    fr, fi = tab_r[far:far + 1, :], tab_i[far:far + 1, :]
    ent_r, ent_i = [None] * SUBLANES, [None] * SUBLANES
    for c in (range(SUBLANES - 1, -1, -1) if reverse else range(SUBLANES)):
        ent_r[c], ent_i[c] = cr, ci
        cr, ci = sr[c:c + 1, :] + (fr * cr - fi * ci), si[c:c + 1, :] + (fr * ci + fi * cr)
    st[0:1, :] = cr
    st[1:2, :] = ci
    c8r = jnp.concatenate(ent_r, axis=0)
    c8i = jnp.concatenate(ent_i, axis=0)
    out = []
    for t in range(sub):
        rows = slice(t * SUBLANES, (t + 1) * SUBLANES)
        tr, ti = tab_r[t:t + 1, :], tab_i[t:t + 1, :]
        out.append(jnp.concatenate([loc[rows, :hb] + (tr * c8r - ti * c8i), loc[rows, hb:] + (tr * c8i + ti * c8r)],
                                   axis=1))
    return jnp.concatenate(out, axis=0)


SSM_BLOCKS_PER_STEP = 4


def _scan_scratch(nblk, t_blk, sub, hb):
    return [pltpu.VMEM((nblk, SUBLANES, hb), F32), pltpu.VMEM((nblk, sub, hb), F32), pltpu.VMEM((nblk, sub, hb), F32),
            pltpu.VMEM((nblk, t_blk, 2 * hb), F32)]


def _ssm_fwd(proj, wb, wc, a):
    seq = proj.shape[0]
    nj = wb.shape[0]
    w2 = 2 * STATE_BLOCK
    hb = STATE_BLOCK
    t_blk = min(SCAN_T, seq)
    sub = t_blk // SUBLANES
    pm = _pack_matrix(t_blk, BF16)

    npair = SSM_BLOCKS_PER_STEP

    def body(u_ref, wb_ref, wc_ref, a_ref, pm_ref, pmt_ref, s_ref, y_ref, st, tab_r, tab_i, loc):
        coef = [(a_ref[:, b * w2:b * w2 + hb], a_ref[:, b * w2 + hb:(b + 1) * w2]) for b in range(npair)]

        @pl.when(pl.program_id(1) == 0)
        def _():
            for b, (ar, ai) in enumerate(coef):
                st[b] = jnp.zeros((SUBLANES, hb), F32)
                _scan_tables(ar, ai, tab_r.at[b], tab_i.at[b], sub, False)

        for b, (ar, ai) in enumerate(coef):
            ub = u_ref[:, b * LANES:(b + 1) * LANES].astype(BF16)
            up = jnp.dot(pm_ref[...], ub, preferred_element_type=F32).astype(BF16)
            bu = jnp.dot(up, wb_ref[b], preferred_element_type=F32)
            s = _scan_block(bu, loc.at[b], ar, ai, st.at[b], tab_r.at[b], tab_i.at[b], sub, False)
            s_ref[:, b * w2:(b + 1) * w2] = s
            yp = jnp.dot(s.astype(BF16), wc_ref[b], preferred_element_type=F32)
            y_ref[:, b * LANES:(b + 1) * LANES] = _permute_rows_f32(pmt_ref[...], yp)

    sd = jax.ShapeDtypeStruct
    return pl.pallas_call(
        body, name="ssm_fwd", grid=(nj // npair, seq // t_blk),
        in_specs=[pl.BlockSpec((t_blk, npair * LANES), lambda j, i: (i, j)),
                  pl.BlockSpec((npair, LANES, w2), lambda j, i: (j, 0, 0)),
                  pl.BlockSpec((npair, w2, LANES), lambda j, i: (j, 0, 0)),
                  pl.BlockSpec((1, npair * w2), lambda j, i: (0, j)),
                  _full_spec((t_blk, t_blk)), _full_spec((t_blk, t_blk))],
        out_specs=[pl.BlockSpec((t_blk, npair * w2), lambda j, i: (i, j)),
                   pl.BlockSpec((t_blk, npair * LANES), lambda j, i: (i, j))],
        out_shape=[sd((seq, nj * w2), F32), sd((seq, nj * LANES), F32)],
        scratch_shapes=_scan_scratch(npair, t_blk, sub, hb), compiler_params=_cparams(),
    )(proj, wb, wc, a, pm, pm.T)


def _ssm_bwd(dy, s, proj, du1, wb, wc, a):
    seq = dy.shape[0]
    nj = wb.shape[0]
    w2 = 2 * STATE_BLOCK
    hb = STATE_BLOCK
    t_blk = min(SCAN_T, seq)
    sub = t_blk // SUBLANES
    nb = seq // t_blk
    pm = _pack_matrix(t_blk, BF16)

    npair = SSM_BLOCKS_PER_STEP

    def body(dy_ref, s_ref, sprev_ref, u_ref, du1_ref, wb_ref, wc_ref, a_ref, pm_ref, pmt_ref,
             du_ref, dwb_ref, dwc_ref, da_ref, st, tab_r, tab_i, loc):
        ib = pl.program_id(1)
        pmv = pm_ref[...]
        coef = [(a_ref[:, b * w2:b * w2 + hb], -a_ref[:, b * w2 + hb:(b + 1) * w2]) for b in range(npair)]

        @pl.when(ib == 0)
        def _():
            for b, (ar, ai) in enumerate(coef):
                st[b] = jnp.zeros((SUBLANES, hb), F32)
                _scan_tables(ar, ai, tab_r.at[b], tab_i.at[b], sub, True)

        sums = []
        for b, (ar, ai) in enumerate(coef):
            cols, wide = slice(b * LANES, (b + 1) * LANES), slice(b * w2, (b + 1) * w2)
            dyp = jnp.dot(pmv, dy_ref[:, cols], preferred_element_type=F32).astype(BF16)
            up = jnp.dot(pmv, u_ref[:, cols].astype(BF16), preferred_element_type=F32).astype(BF16)
            ds = lax.dot_general(dyp, wc_ref[b], (NT, ((), ())), preferred_element_type=F32)
            lam = _scan_block(ds, loc.at[b], ar, ai, st.at[b], tab_r.at[b], tab_i.at[b], sub, True)
            lamb = lam.astype(BF16)
            du = lax.dot_general(lamb, wb_ref[b], (NT, ((), ())), preferred_element_type=F32)
            du_ref[:, cols] = (_permute_rows_f32(pmt_ref[...], du) + du1_ref[:, cols]).astype(du_ref.dtype)
            sv = s_ref[:, wide]
            dwb = lax.dot_general(up, lamb, (TN, ((), ())), preferred_element_type=F32)
            dwc = lax.dot_general(sv.astype(BF16), dyp, (TN, ((), ())), preferred_element_type=F32)

            prev_last = sprev_ref[SUBLANES - 1:SUBLANES, wide]
            prev_last = jnp.where(ib == nb - 1, jnp.zeros_like(prev_last), prev_last)
            tail = sv[t_blk - SUBLANES:, :]
            sl = lax.broadcasted_iota(jnp.int32, tail.shape, 0)
            head = jnp.where(sl >= 1, pltpu.roll(tail, 1, 0), prev_last)
            s_sh = jnp.concatenate([head, sv[:t_blk - SUBLANES, :]], axis=0)
            lam_r, lam_i = lam[:, :hb], lam[:, hb:]
            sr_, si_ = s_sh[:, :hb], s_sh[:, hb:]
            dar = jnp.sum(lam_r * sr_ + lam_i * si_, axis=0, keepdims=True)
            dai = jnp.sum(lam_i * sr_ - lam_r * si_, axis=0, keepdims=True)
            sums.append((wide, jnp.concatenate([dar, dai], axis=1), dwb, dwc))

        @pl.when(ib == 0)
        def _():
            for b, (wide, contrib, dwb, dwc) in enumerate(sums):
                da_ref[:, wide] = contrib
                dwb_ref[b] = dwb
                dwc_ref[b] = dwc

        @pl.when(ib != 0)
        def _():
            for b, (wide, contrib, dwb, dwc) in enumerate(sums):
                da_ref[:, wide] += contrib
                dwb_ref[b] += dwb
                dwc_ref[b] += dwc

    blk = lambda j, i: (nb - 1 - i, j)
    prev_blk = lambda j, i: (jnp.maximum((nb - 1 - i) * sub - 1, 0), j)
    sd = jax.ShapeDtypeStruct
    return pl.pallas_call(
        body, name="ssm_bwd", grid=(nj // npair, nb),
        in_specs=[pl.BlockSpec((t_blk, npair * LANES), blk), pl.BlockSpec((t_blk, npair * w2), blk),
                  pl.BlockSpec((SUBLANES, npair * w2), prev_blk), pl.BlockSpec((t_blk, npair * LANES), blk),
                  pl.BlockSpec((t_blk, npair * LANES), blk),
                  pl.BlockSpec((npair, LANES, w2), lambda j, i: (j, 0, 0)),
                  pl.BlockSpec((npair, w2, LANES), lambda j, i: (j, 0, 0)),
                  pl.BlockSpec((1, npair * w2), lambda j, i: (0, j)),
                  _full_spec((t_blk, t_blk)), _full_spec((t_blk, t_blk))],
        out_specs=[pl.BlockSpec((t_blk, npair * LANES), blk),
                   pl.BlockSpec((npair, LANES, w2), lambda j, i: (j, 0, 0)),
                   pl.BlockSpec((npair, w2, LANES), lambda j, i: (j, 0, 0)),
                   pl.BlockSpec((1, npair * w2), lambda j, i: (0, j))],
        out_shape=[sd((seq, nj * LANES), BF16), sd((nj, LANES, w2), F32), sd((nj, w2, LANES), F32),
                   sd((1, nj * w2), F32)],
        scratch_shapes=_scan_scratch(npair, t_blk, sub, hb), compiler_params=_cparams(),
    )(dy, s, s, proj, du1, wb, wc, a, pm, pm.T)


def _rope128(x, cos, sa, sb):
    return x * cos + pltpu.roll(x, 96, 1) * sa + pltpu.roll(x, 32, 1) * sb


def _rope128_t(dy, cos, sa, sb):
    return dy * cos + pltpu.roll(dy * sa, 32, 1) + pltpu.roll(dy * sb, 96, 1)


ATT_BQ = 512


def _probs(s, r0, scale):
    s = s * scale
    diag = s[:, r0:]
    row = lax.broadcasted_iota(jnp.int32, diag.shape, 0)
    col = lax.broadcasted_iota(jnp.int32, diag.shape, 1)
    diag = jnp.where(col <= row, diag, jnp.finfo(F32).min)
    s = diag if r0 == 0 else jnp.concatenate([s[:, :r0], diag], axis=1)
    m = jnp.max(s, axis=-1, keepdims=True)
    e = jnp.exp(s - m)
    return e / jnp.sum(e, axis=-1, keepdims=True)


def _attn_specs(seq):
    tab = pl.BlockSpec((seq, LANES), lambda h: (0, 0))
    return [pl.BlockSpec((None, seq, 256), lambda h: (h, 0, 0)), pl.BlockSpec((None, seq, 128), lambda h: (h, 0, 0)),
            pl.BlockSpec((None, seq, 128), lambda h: (h, 0, 1)), tab, tab, tab, tab]


def _attn_fwd(q_raw, kv, kpe, cos, sa, sb):
    nh, seq, _ = q_raw.shape
    bq = min(ATT_BQ, seq)
    scale = (QK_NOPE + QK_ROPE) ** -0.5

    def body(q_ref, kn_ref, v_ref, kp_ref, cos_ref, sa_ref, sb_ref, o_ref):
        for r0 in range(0, seq, bq):
            rows, kend = pl.ds(r0, bq), r0 + bq
            qn = q_ref[rows, :QK_NOPE].astype(BF16)
            qp = _rope128(q_ref[rows, QK_NOPE:], cos_ref[rows, :], sa_ref[rows, :], sb_ref[rows, :]).astype(BF16)
            s = lax.dot_general(qn, kn_ref[:kend, :], (NT, ((), ())), preferred_element_type=F32)
            s = s + lax.dot_general(qp, kp_ref[:kend, :], (NT, ((), ())), preferred_element_type=F32)
            p = _probs(s, r0, scale)
            o_ref[rows, :] = jnp.dot(p.astype(BF16), v_ref[:kend, :], preferred_element_type=F32)

    return pl.pallas_call(
        body, name="attn_fwd", grid=(nh,), in_specs=_attn_specs(seq),
        out_specs=pl.BlockSpec((seq, V_DIM), lambda h: (0, h)),
        out_shape=jax.ShapeDtypeStruct((seq, nh * V_DIM), F32), compiler_params=_cparams(),
    )(q_raw, kv, kv, kpe, cos, sa, sb)


def _attn_bwd(q_raw, kv, kpe, cos, sa, sb, do):
    nh, seq, _ = q_raw.shape
    bq = min(ATT_BQ, seq)
    scale = (QK_NOPE + QK_ROPE) ** -0.5

    def body(q_ref, kn_ref, v_ref, kp_ref, cos_ref, sa_ref, sb_ref, do_ref, dq_ref, dkv_ref, dkp_ref):
        dkv_ref[...] = jnp.zeros_like(dkv_ref)
        dkp_ref[...] = jnp.zeros_like(dkp_ref)
        k2_all = jnp.concatenate([kn_ref[...], kp_ref[...]], axis=1)
        for r0 in range(0, seq, bq):
            rows, kend = pl.ds(r0, bq), r0 + bq
            cos_b, sa_b, sb_b = cos_ref[rows, :], sa_ref[rows, :], sb_ref[rows, :]
            qn = q_ref[rows, :QK_NOPE].astype(BF16)
            qp = _rope128(q_ref[rows, QK_NOPE:], cos_b, sa_b, sb_b).astype(BF16)
            q2 = jnp.concatenate([qn, qp], axis=1)
            k2, v = k2_all[:kend], v_ref[:kend, :]
            p = _probs(lax.dot_general(q2, k2, (NT, ((), ())), preferred_element_type=F32), r0, scale)
            dob = do_ref[rows, :].astype(BF16)
            dp = lax.dot_general(dob, v, (NT, ((), ())), preferred_element_type=F32)
            ds = p * (dp - jnp.sum(p * dp, axis=-1, keepdims=True)) * scale
            dsb = ds.astype(BF16)
            pb = p.astype(BF16)
            dq2 = jnp.dot(dsb, k2, preferred_element_type=F32)
            dq_ref[rows, :QK_NOPE] = dq2[:, :QK_NOPE].astype(dq_ref.dtype)
            dq_ref[rows, QK_NOPE:] = _rope128_t(dq2[:, QK_NOPE:], cos_b, sa_b, sb_b).astype(dq_ref.dtype)
            dk2 = lax.dot_general(dsb, q2, (TN, ((), ())), preferred_element_type=F32)
            dkv_ref[:kend, :QK_NOPE] += dk2[:, :QK_NOPE]
            dkv_ref[:kend, QK_NOPE:] += lax.dot_general(pb, dob, (TN, ((), ())), preferred_element_type=F32)
            dkp_ref[:kend, :] += dk2[:, QK_NOPE:]

    sd = jax.ShapeDtypeStruct
    return pl.pallas_call(
        body, name="attn_bwd", grid=(nh,),
        in_specs=_attn_specs(seq) + [pl.BlockSpec((seq, V_DIM), lambda h: (0, h))],
        out_specs=[pl.BlockSpec((None, seq, 256), lambda h: (h, 0, 0)),
                   pl.BlockSpec((None, seq, 256), lambda h: (h, 0, 0)),
                   pl.BlockSpec((None, seq, 128), lambda h: (h, 0, 0))],
        out_shape=[sd((nh, seq, 256), BF16), sd((nh, seq, 256), F32), sd((nh, seq, 128), F32)],
        compiler_params=_cparams(),
    )(q_raw, kv, kv, kpe, cos, sa, sb, do)


def _shift_rows(a, k):
    seq = a.shape[0]
    r = pltpu.roll(a, k % seq, 0)
    rows = lax.broadcasted_iota(jnp.int32, (SUBLANES, a.shape[1]), 0)
    if k > 0:
        return jnp.concatenate([jnp.where(rows >= k, r[:SUBLANES], 0.0), r[SUBLANES:]], axis=0)
    return jnp.concatenate([r[:seq - SUBLANES], jnp.where(rows < SUBLANES + k, r[seq - SUBLANES:], 0.0)], axis=0)


def _conv3(a, w, b):
    a1 = _shift_rows(a, 1)
    a2 = _shift_rows(a, 2)
    return w[2:3] * a + w[1:2] * a1 + w[0:1] * a2 + b, a1, a2


def _conv_gate_fwd(a, cw, cb):
    half, _, seq, c = a.shape
    nc = c // LANES

    def fn(pair, wg, wv, bg, bv):
        gc, _, _ = _conv3(pair[0], wg, bg)
        vc, _, _ = _conv3(pair[1], wv, bv)
        return gc * jax.nn.sigmoid(gc) * vc

    def w_spec(off, r):
        return pl.BlockSpec((None, r, LANES), lambda k, j: (k + off, 0, j))

    return _blockwise(
        "conv_gate_fwd", fn, [a, cw, cw, cb, cb],
        [pl.BlockSpec((None, 2, seq, LANES), lambda k, j: (k, 0, 0, j)),
         w_spec(0, 3), w_spec(half, 3), w_spec(0, 1), w_spec(half, 1)],
        [((seq, half * c), BF16)], [pl.BlockSpec((seq, LANES), lambda k, j: (0, k * nc + j))],
        grid=(half, nc))[0]


def _conv_gate_bwd(a, cw, cb, dm):
    half, _, seq, c = a.shape
    nc = c // LANES

    def body(a_ref, wg_ref, wv_ref, bg_ref, bv_ref, dm_ref, da_ref, dw_ref, db_ref):
        dmv = dm_ref[...]
        ga, wg = a_ref[0], wg_ref[...]
        va, wv = a_ref[1], wv_ref[...]
        gc, g1, g2 = _conv3(ga, wg, bg_ref[...])
        vc, v1, v2 = _conv3(va, wv, bv_ref[...])
        sg = jax.nn.sigmoid(gc)
        dms = dmv * sg
        d_val = dms * gc
        d_gate = dms * vc * (1.0 + gc * (1.0 - sg))

        def back(r, dc, own, a1, a2, w):
            up1 = _shift_rows(dc, -1)
            up2 = _shift_rows(dc, -2)
            da_ref[r] = (w[2:3] * dc + w[1:2] * up1 + w[0:1] * up2).astype(da_ref.dtype)
            dw_ref[r, 0:1, :] = jnp.sum(dc * a2, axis=0, keepdims=True)
            dw_ref[r, 1:2, :] = jnp.sum(dc * a1, axis=0, keepdims=True)
            dw_ref[r, 2:3, :] = jnp.sum(dc * own, axis=0, keepdims=True)
            db_ref[r] = jnp.sum(dc, axis=0, keepdims=True)

        back(0, d_gate, ga, g1, g2, wg)
        back(1, d_val, va, v1, v2, wv)

    def w_spec(off, r):
        return pl.BlockSpec((None, r, LANES), lambda k, j: (k + off, 0, j))

    def pair_spec(r):
        return pl.BlockSpec((None, 2, r, LANES), lambda k, j: (k, 0, 0, j))

    sd = jax.ShapeDtypeStruct
    return pl.pallas_call(
        body, name="conv_gate_bwd", grid=(half, nc),
        in_specs=[pair_spec(seq), w_spec(0, 3), w_spec(half, 3), w_spec(0, 1), w_spec(half, 1),
                  pl.BlockSpec((seq, LANES), lambda k, j: (0, k * nc + j))],
        out_specs=[pair_spec(seq), pair_spec(3), pair_spec(1)],
        out_shape=[sd((half, 2, seq, c), BF16), sd((half, 2, 3, c), F32), sd((half, 2, 1, c), F32)],
        compiler_params=_cparams(),
    )(a, cw, cw, cb, cb, dm)


ROW_T = 256


def _local_step(x, positions, target, w, emit=lambda **grads: None):
    seq, d = x.shape
    t_row = min(ROW_T, seq)
    nrow = seq // t_row
    ssm_w = d // 2
    nj = ssm_w // LANES
    n_groups = ssm_w // SSM_GROUP
    nh = w["wuq"].shape[0]
    q_rank = w["wuq"].shape[1]
    kv_rank = w["wukv"].shape[1]
    ns = w["wup"].shape[0]
    c_ff = w["wup"].shape[2]
    in_pad = w["win"].shape[1]
    tm = min(1024, seq)
    nm = seq // tm
    sw = 2 * STATE_BLOCK
    g1 = (nrow,)

    lr3 = w["lam_re"].reshape(n_groups, 1, SSM_STATE)
    li3 = w["lam_im"].reshape(n_groups, 1, SSM_STATE)
    ldt3 = w["log_dt"].reshape(n_groups, 1, 1)
    bt_re = jnp.swapaxes(w["b_re"].reshape(n_groups, SSM_STATE, SSM_GROUP), 1, 2)
    bt_im = jnp.swapaxes(w["b_im"].reshape(n_groups, SSM_STATE, SSM_GROUP), 1, 2)
    abar_re, abar_im, bbt_re, bbt_im = _s5_prep(lr3, li3, ldt3, bt_re, bt_im)
    eye = jnp.eye(GROUPS_PER_BLOCK, dtype=F32)

    def blockdiag_in(bb):
        t = bb.reshape(nj, GROUPS_PER_BLOCK, SSM_GROUP, SSM_STATE)
        return jnp.einsum("jghp,gk->jghkp", t, eye).reshape(nj, LANES, STATE_BLOCK)

    def blockdiag_in_t(dwb):
        t = dwb.reshape(nj, GROUPS_PER_BLOCK, SSM_GROUP, GROUPS_PER_BLOCK, SSM_STATE)
        return jnp.einsum("jghkp,gk->jghp", t, eye).reshape(n_groups, SSM_GROUP, SSM_STATE)

    def blockdiag_out(cc):
        t = cc.reshape(nj, GROUPS_PER_BLOCK, SSM_GROUP, SSM_STATE)
        return jnp.einsum("jghp,gk->jkpgh", t, eye).reshape(nj, STATE_BLOCK, LANES)

    def blockdiag_out_t(dwc):
        t = dwc.reshape(nj, GROUPS_PER_BLOCK, SSM_STATE, GROUPS_PER_BLOCK, SSM_GROUP)
        return jnp.einsum("jkpgh,gk->jghp", t, eye).reshape(n_groups, SSM_GROUP, SSM_STATE)

    c_re = w["c_re"].reshape(n_groups, SSM_GROUP, SSM_STATE)
    c_im = w["c_im"].reshape(n_groups, SSM_GROUP, SSM_STATE)
    wb = jnp.concatenate([blockdiag_in(bbt_re), blockdiag_in(bbt_im)], axis=2).astype(BF16)
    wc = jnp.concatenate([blockdiag_out(c_re), -blockdiag_out(c_im)], axis=1).astype(BF16)
    a_lay = jnp.concatenate([abar_re.reshape(nj, 1, STATE_BLOCK), abar_im.reshape(nj, 1, STATE_BLOCK)],
                            axis=1).reshape(1, nj * sw)

    attn_w = w["attn_norm"]
    t_wide = min(2 * t_row, seq)
    g_wide = (seq // t_wide,)

    def proj_fn(xb, wv, wi):
        hb = _rms(xb, wv).astype(BF16)
        return hb, jnp.dot(hb, wi, preferred_element_type=F32)

    hn, proj = _blockwise(
        "norm1_proj", proj_fn, [x, attn_w, w["win"]],
        [_row_spec(t_wide, d), _full_spec((1, d)), _full_spec((d, in_pad), single=True)],
        [((seq, d), BF16), ((seq, in_pad), F32)], [_row_spec(t_wide, d), _row_spec(t_wide, in_pad)], g_wide)

    s_all, ylin = _ssm_fwd(proj, wb, wc, a_lay)

    def glu_fwd_fn(yl, ub, dsk, wg, bg):
        yp = yl + dsk * ub
        ygv = jax.nn.gelu(yp)
        ygb = ygv.astype(BF16)
        zb = jnp.dot(ygb, wg, preferred_element_type=F32) + bg
        return yp, ygb, zb, ygv * jax.nn.sigmoid(zb)

    wide = pl.BlockSpec((t_wide, ssm_w), lambda i: (i, 0))
    y_pre, yg, z, y_ssm = _blockwise(
        "ssm_glu_fwd", glu_fwd_fn, [ylin, proj, w["ssm_d"], w["wglu"], w["b_glu"]],
        [wide, wide, _full_spec((1, ssm_w)), _full_spec((ssm_w, ssm_w), single=True), _full_spec((1, ssm_w))],
        [((seq, ssm_w), F32), ((seq, ssm_w), BF16), ((seq, ssm_w), F32), ((seq, ssm_w), F32)], [wide] * 4, g_wide)

    cq_off, ckv_off, kpe_off = ssm_w, ssm_w + q_rank, ssm_w + q_rank + kv_rank
    assert cq_off % q_rank == 0 and ckv_off % kv_rank == 0 and kpe_off % LANES == 0
    cq_spec = pl.BlockSpec((t_row, q_rank), lambda i: (i, cq_off // q_rank))
    ckv_spec = pl.BlockSpec((t_row, kv_rank), lambda i: (i, ckv_off // kv_rank))
    kpe_spec = pl.BlockSpec((t_row, LANES), lambda i: (i, kpe_off // LANES))
    pos_b = jnp.broadcast_to(positions.astype(F32)[:, None], (seq, LANES))
    inv_freq = ROPE_THETA ** (-jnp.arange(0, QK_ROPE, 2, dtype=F32) / QK_ROPE)
    inv128 = jnp.tile(inv_freq, 4).reshape(1, LANES)

    def mla_prep_fn(cq, ckv, kp, pb, inv, wq, wkv):
        ang = pb * inv
        lane = lax.broadcasted_iota(jnp.int32, ang.shape, 1)
        cs, sn = jnp.cos(ang), jnp.sin(ang)
        cos = jnp.where(lane < QK_ROPE, cs, 0.0)
        sa = jnp.where(lane < QK_ROPE // 2, -sn, 0.0)
        sb = jnp.where(jnp.logical_and(lane >= QK_ROPE // 2, lane < QK_ROPE), sn, 0.0)
        return _rms(cq, wq), _rms(ckv, wkv), _rope128(kp, cos, sa, sb), cos, sa, sb

    qn, kvn, kpe, cos_t, sa_t, sb_t = _blockwise(
        "mla_prep", mla_prep_fn, [proj, proj, proj, pos_b, inv128, w["q_norm"], w["kv_norm"]],
        [cq_spec, ckv_spec, kpe_spec, _row_spec(t_row, LANES),
         _full_spec((1, LANES)), _full_spec((1, q_rank)), _full_spec((1, kv_rank))],
        [((seq, q_rank), BF16), ((seq, kv_rank), BF16), ((seq, LANES), BF16)] + [((seq, LANES), F32)] * 3,
        [_row_spec(t_row, q_rank), _row_spec(t_row, kv_rank)] + [_row_spec(t_row, LANES)] * 4, g1)

    def head_mm(name, act, wh, out_dtype):
        kdim, ndim = wh.shape[1], wh.shape[2]
        return _mm(name, act, wh, grid=(nh, 1, 1),
                   a_spec=pl.BlockSpec((seq, kdim), lambda h, i, k: (i, 0)),
                   b_spec=pl.BlockSpec((None, kdim, ndim), lambda h, i, k: (h, 0, 0)),
                   o_spec=pl.BlockSpec((None, seq, ndim), lambda h, i, k: (h, i, 0)),
                   out_shape=(nh, seq, ndim), out_dtype=out_dtype)

    q_raw = head_mm("mla_q", qn, w["wuq"], F32)
    kv = head_mm("mla_kv", kvn, w["wukv"], BF16)
    y_mla = _attn_fwd(q_raw, kv, kpe, cos_t, sa_t, sb_t)
    mla_w = nh * V_DIM

    def out_proj_fn(ys, ym, ws, wm, wo, xb, wf):
        yc = jnp.concatenate([_rms(ys, ws), _rms(ym, wm)], axis=1).astype(BF16)
        hb = xb + jnp.dot(yc, wo, preferred_element_type=F32)
        return yc, hb, _rms(hb, wf)

    ycat, h1, hn2 = _blockwise(
        "out_norm_proj", out_proj_fn, [y_ssm, y_mla, w["son"], w["mon"], w["wout"], x, w["ffn_norm"]],
        [wide, _row_spec(t_wide, mla_w), _full_spec((1, ssm_w)), _full_spec((1, mla_w)),
         _full_spec((d, d), single=True), _row_spec(t_wide, d), _full_spec((1, d))],
        [((seq, d), BF16), ((seq, d), F32), ((seq, d), BF16)], [_row_spec(t_wide, d)] * 3, g_wide)

    tku = d
    half = ns // 2
    a_ff = _mm("ffn_up", hn2, w["wup"], grid=(ns, nm, d // tku),
               a_spec=pl.BlockSpec((tm, tku), lambda s, i, k: (i, k)),
               b_spec=pl.BlockSpec((None, tku, c_ff), lambda s, i, k: (s, k, 0)),
               o_spec=pl.BlockSpec((None, None, tm, c_ff), lambda s, i, k: (s % half, s // half, i, 0)),
               out_shape=(half, 2, seq, c_ff), out_dtype=F32)
    cb3 = w["conv_b"].reshape(ns, 1, c_ff)
    m_ff = _conv_gate_fwd(a_ff, w["conv_w"], cb3)
    d_ff = half * c_ff
    wdn = w["wdown"]
    tnd = _tile(d, 1024)
    tmx, tnx = min(1024, seq), _tile(d, 1024)
    h2 = _mm2d("ffn_down", m_ff, wdn, NN, F32, tm=512, tn=512, tk=d_ff, res=h1)

    def loss_fn(hb, tb, wv):
        def f(hh, ww):
            err = _rms(hh, ww) - tb
            return 0.5 * jnp.sum(jnp.mean(err * err, axis=-1))

        lossv, (dh, dw) = jax.value_and_grad(f, argnums=(0, 1))(hb, wv)
        return dh, dh, jnp.full((1, LANES), lossv, F32), dw

    fin_w = w["final_norm"].reshape(1, d)
    dh2, dh2b, loss_acc, g_final = _blockwise(
        "loss_head", loss_fn, [h2, target, fin_w], [_row_spec(t_row, d), _row_spec(t_row, d), _full_spec((1, d))],
        [((seq, d), F32), ((seq, d), BF16), ((1, LANES), F32), ((1, d), F32)],
        [_row_spec(t_row, d), _row_spec(t_row, d), _full_spec((1, LANES)), _full_spec((1, d))], g1, n_acc=2)
    loss = loss_acc

    dm = _mm2d("ffn_down_dx", dh2b, wdn, NT, F32, tn=c_ff)
    tks = seq
    g_wdown = _mm2d("ffn_down_dw", m_ff, dh2b, TN, BF16, tm=c_ff)
    emit(wdown=g_wdown)
    da_ff, g_convw2, g_convb2 = _conv_gate_bwd(a_ff, w["conv_w"], cb3, dm)
    g_convw = jnp.swapaxes(g_convw2, 0, 1).reshape(ns, 3, c_ff)
    g_convb = jnp.swapaxes(g_convb2, 0, 1).reshape(ns, 1, c_ff)
    g_wup = _mm("ffn_up_dw", hn2, da_ff, grid=(ns, d // tnd, seq // tks), contract=TN,
                a_spec=pl.BlockSpec((tks, tnd), lambda s, j, k: (k, j)),
                b_spec=pl.BlockSpec((None, None, tks, c_ff), lambda s, j, k: (s % half, s // half, k, 0)),
                o_spec=pl.BlockSpec((None, tnd, c_ff), lambda s, j, k: (s, j, 0)),
                out_shape=(ns, d, c_ff), out_dtype=BF16)
    emit(wup=g_wup)
    dhn2 = _mm("ffn_up_dx", da_ff, w["wup"], grid=(seq // tmx, d // tnx, ns), contract=NT,
               a_spec=pl.BlockSpec((None, None, tmx, c_ff), lambda i, j, s: (s % half, s // half, i, 0)),
               b_spec=pl.BlockSpec((None, tnx, c_ff), lambda i, j, s: (s, j, 0)),
               o_spec=pl.BlockSpec((tmx, tnx), lambda i, j, s: (i, j)),
               out_shape=(seq, d), out_dtype=F32)
    emit(wup_pair_sums_after=dhn2)

    def norm_bwd_fn(hb, dres, dn, wv):
        dx_, dw_ = _rms_bwd(hb, wv, dn)
        dtot = dres + dx_
        return dtot, dtot, dw_

    dh1, dh1b, g_ffn_norm = _blockwise(
        "norm2_bwd", norm_bwd_fn, [h1, dh2, dhn2, w["ffn_norm"]],
        [_row_spec(t_row, d)] * 3 + [_full_spec((1, d))],
        [((seq, d), F32), ((seq, d), BF16), ((1, d), F32)],
        [_row_spec(t_row, d), _row_spec(t_row, d), _full_spec((1, d))], g1, n_acc=1)

    g_wout = _mm2d("out_proj_dw", ycat, dh1b, TN, BF16)

    def outnorm_bwd_fn(dhb, wo, ys, ym, ws, wm):
        dyc = lax.dot_general(dhb, wo, (NT, ((), ())), preferred_element_type=F32)
        dys, dws = _rms_bwd(ys, ws, dyc[:, :ssm_w])
        dym, dwm = _rms_bwd(ym, wm, dyc[:, ssm_w:])
        return dys, dym, dws, dwm

    dy_ssm, dy_mla, g_son, g_mon = _blockwise(
        "out_proj_dx_norm_bwd", outnorm_bwd_fn, [dh1b, w["wout"], y_ssm, y_mla, w["son"], w["mon"]],
        [_row_spec(t_wide, d), _full_spec((d, d), single=True), wide, _row_spec(t_wide, mla_w),
         _full_spec((1, ssm_w)), _full_spec((1, mla_w))],
        [((seq, ssm_w), F32), ((seq, mla_w), F32), ((1, ssm_w), F32), ((1, mla_w), F32)],
        [wide, _row_spec(t_wide, mla_w), _full_spec((1, ssm_w)), _full_spec((1, mla_w))],
        g_wide, n_acc=2)

    def glu_bwd_fn(dy, yp, zb, ub, dsk, wg):
        ygv = jax.nn.gelu(yp)
        sg = jax.nn.sigmoid(zb)
        dz = dy * ygv * sg * (1.0 - sg)
        dzb = dz.astype(BF16)
        dyg = dy * sg + lax.dot_general(dzb, wg, (NT, ((), ())), preferred_element_type=F32)
        _, vjp = jax.vjp(jax.nn.gelu, yp)
        dyp = vjp(dyg)[0]
        return (dzb, dyp, dyp * dsk, jnp.sum(dz, axis=0, keepdims=True), jnp.sum(dyp * ub, axis=0, keepdims=True))

    dz, dy_pre, du1, g_bglu, g_ssmd = _blockwise(
        "ssm_glu_bwd", glu_bwd_fn, [dy_ssm, y_pre, z, proj, w["ssm_d"], w["wglu"]],
        [wide] * 4 + [_full_spec((1, ssm_w)), _full_spec((ssm_w, ssm_w), single=True)],
        [((seq, ssm_w), BF16), ((seq, ssm_w), BF16), ((seq, ssm_w), F32), ((1, ssm_w), F32), ((1, ssm_w), F32)],
        [wide] * 3 + [_full_spec((1, ssm_w))] * 2, g_wide, n_acc=2)
    g_wglu = _mm2d("ssm_glu_dw", yg, dz, TN, BF16)
    dq_raw, dkv, dkp_h = _attn_bwd(q_raw, kv, kpe, cos_t, sa_t, sb_t, dy_mla)

    def head_mm_dx(name, dact, wh):
        kdim, ndim = wh.shape[1], wh.shape[2]
        return _mm(name, dact, wh, grid=(1, 1, nh), contract=NT,
                   a_spec=pl.BlockSpec((None, seq, ndim), lambda i, j, h: (h, i, 0)),
                   b_spec=pl.BlockSpec((None, kdim, ndim), lambda i, j, h: (h, 0, 0)),
                   o_spec=pl.BlockSpec((seq, kdim), lambda i, j, h: (i, 0)),
                   out_shape=(seq, kdim), out_dtype=F32)

    def head_mm_dw(name, act, dact):
        kdim, ndim = act.shape[1], dact.shape[2]
        return _mm(name, act, dact, grid=(nh, 1, seq // tks), contract=TN,
                   a_spec=pl.BlockSpec((tks, kdim), lambda h, j, k: (k, 0)),
                   b_spec=pl.BlockSpec((None, tks, ndim), lambda h, j, k: (h, k, 0)),
                   o_spec=pl.BlockSpec((None, kdim, ndim), lambda h, j, k: (h, 0, 0)),
                   out_shape=(nh, kdim, ndim), out_dtype=BF16)

    g_wuq = head_mm_dw("mla_q_dw", qn, dq_raw)
    g_wukv = head_mm_dw("mla_kv_dw", kvn, dkv)
    dqn = head_mm_dx("mla_q_dx", dq_raw, w["wuq"])
    dkvn = head_mm_dx("mla_kv_dx", dkv, w["wukv"])
    emit(not_before=(dqn, dkvn, dy_pre), wout=g_wout, wuq=g_wuq, wukv=g_wukv, wglu=g_wglu, conv_w=g_convw)

    du, dwb, dwc, da_lay = _ssm_bwd(dy_pre, s_all, proj, du1, wb, wc, a_lay)
    g_c_re = blockdiag_out_t(dwc[:, :STATE_BLOCK, :])
    g_c_im = -blockdiag_out_t(dwc[:, STATE_BLOCK:, :])
    dbbt_re = blockdiag_in_t(dwb[:, :, :STATE_BLOCK])
    dbbt_im = blockdiag_in_t(dwb[:, :, STATE_BLOCK:])
    da3 = da_lay.reshape(nj, 2, STATE_BLOCK)
    dabar_re = da3[:, 0, :].reshape(n_groups, 1, SSM_STATE)
    dabar_im = da3[:, 1, :].reshape(n_groups, 1, SSM_STATE)
    g_lr3, g_li3, g_ldt3, g_bt_re, g_bt_im = _s5_prep_bwd(lr3, li3, ldt3, bt_re, bt_im,
                                                           dabar_re, dabar_im, dbbt_re, dbbt_im)

    def mla_prep_bwd_fn(cq, ckv, dqn_b, dkvn_b, dkp_b, cos, sa, sb, wq, wkv):
        dcq, dwq = _rms_bwd(cq, wq, dqn_b)
        dckv, dwkv = _rms_bwd(ckv, wkv, dkvn_b)
        dkp_sum = dkp_b[0]
        for h in range(1, nh):
            dkp_sum = dkp_sum + dkp_b[h]
        return dcq, dckv, _rope128_t(dkp_sum, cos, sa, sb), dwq, dwkv

    dc_q, dc_kv, dkpe_raw, g_qnorm, g_kvnorm = _blockwise(
        "mla_prep_bwd", mla_prep_bwd_fn, [proj, proj, dqn, dkvn, dkp_h, cos_t, sa_t, sb_t, w["q_norm"], w["kv_norm"]],
        [cq_spec, ckv_spec, _row_spec(t_row, q_rank), _row_spec(t_row, kv_rank),
         pl.BlockSpec((nh, t_row, LANES), lambda i: (0, i, 0))] + [_row_spec(t_row, LANES)] * 3
        + [_full_spec((1, q_rank)), _full_spec((1, kv_rank))],
        [((seq, q_rank), BF16), ((seq, kv_rank), BF16), ((seq, LANES), BF16), ((1, q_rank), F32), ((1, kv_rank), F32)],
        [_row_spec(t_row, q_rank), _row_spec(t_row, kv_rank), _row_spec(t_row, LANES), _full_spec((1, q_rank)),
         _full_spec((1, kv_rank))], g1, n_acc=2)

    dproj = jnp.concatenate([du, dc_q, dc_kv, dkpe_raw], axis=1)
    g_win = _mm2d("proj_dw", hn, dproj, TN, BF16, tn=640)
    emit(win=g_win)
    def norm1_bwd_fn(dpb, wi, xb, dres, wv):
        dn = lax.dot_general(dpb, wi, (NT, ((), ())), preferred_element_type=F32)
        dx_, dw_ = _rms_bwd(xb, wv, dn)
        return dres + dx_, dw_

    grad_x, g_attn_norm = _blockwise(
        "proj_dx_norm1_bwd", norm1_bwd_fn, [dproj, w["win"], x, dh1, attn_w],
        [_row_spec(t_row, in_pad), _full_spec((d, in_pad), single=True), _row_spec(t_row, d), _row_spec(t_row, d), _full_spec((1, d))],
        [((seq, d), F32), ((1, d), F32)], [_row_spec(t_row, d), _full_spec((1, d))], g1, n_acc=1)
    emit(win_pair_sums_after=grad_x)

    grads = dict(
        attn_norm=g_attn_norm, win=g_win, lam_re=g_lr3, lam_im=g_li3, log_dt=g_ldt3,
        bt_re=g_bt_re, bt_im=g_bt_im, c_re=g_c_re, c_im=g_c_im,
        ssm_d=g_ssmd, wglu=g_wglu, b_glu=g_bglu, q_norm=g_qnorm, wuq=g_wuq, kv_norm=g_kvnorm, wukv=g_wukv,
        son=g_son, mon=g_mon, wout=g_wout, ffn_norm=g_ffn_norm, wup=g_wup, conv_w=g_convw, conv_b=g_convb,
        wdown=g_wdown, final_norm=g_final)
    return loss, grad_x, grads


def _mesh_pos():
    return lax.axis_index("x"), lax.axis_index("y"), lax.axis_index("c")


def _handshake_all():
    x, y, c = _mesh_pos()
    barrier = pltpu.get_barrier_semaphore()
    for k in range(1, N_DEV):
        peer = (1 - x if k & 4 else x, 1 - y if k & 2 else y, 1 - c if k & 1 else c)
        pl.semaphore_signal(barrier, inc=1, device_id=peer, device_id_type=MESH)
    pl.semaphore_wait(barrier, N_DEV - 1)


def _handshake(peers):
    barrier = pltpu.get_barrier_semaphore()
    for peer in peers:
        pl.semaphore_signal(barrier, inc=1, device_id=peer, device_id_type=MESH)
    pl.semaphore_wait(barrier, len(peers))


def _comm_call(name, body, n, out_shape, ins, collective_id, after=None, copies=7, n_remote=None, n_local=None):
    n_remote = copies * n if n_remote is None else n_remote
    sems = [pltpu.SemaphoreType.DMA((n_remote,)), pltpu.SemaphoreType.DMA((n_remote,)),
            pltpu.SemaphoreType.DMA((n if n_local is None else n_local,))]
    if collective_id is None:
        any_spec = pl.BlockSpec(memory_space=pl.ANY)
        return pl.pallas_call(body, name=name, out_shape=out_shape, in_specs=[any_spec] * n,
                              out_specs=[any_spec] * n, scratch_shapes=sems)(*ins)
    seq_body = body
    if after:
        n_after = len(after)
        ins = list(ins) + list(after)

        def seq_body(*refs):
            body(*refs[:n], *refs[n + n_after:])

    return pl.kernel(seq_body, name=name, out_type=out_shape,
                     mesh=plsc.ScalarSubcoreMesh(axis_name="seq", num_cores=1), scratch_types=sems,
                     compiler_params=pltpu.CompilerParams(collective_id=collective_id))(*ins)


def _all_gather(name, xs, collective_id=None, after=None, pair_sums=()):
    n = len(xs)
    nh = len(pair_sums)
    m = n + nh

    def body(*refs):
        x_refs, h_refs, o_refs, e_refs = refs[:n], refs[n:m], refs[m:m + n], refs[m + n:2 * m]
        send_sems, recv_sems, local_sems = refs[2 * m:]
        if collective_id is not None:
            _handshake_all()
        finish_pairs = _chip_copies(h_refs, e_refs, send_sems, recv_sems, local_sems, 7 * n, n) if nh else None
        x, y, c = _mesh_pos()
        me, sibling = (x, y, c), (x, y, 1 - c)
        chips = [(1 - x, y), (x, 1 - y), (1 - x, 1 - y)]

        def slot(o_ref, px, py, pc):
            return o_ref.at[4 * px + 2 * py + pc]

        def copy(t, k, block, to, src=None):
            dst = slot(o_refs[t], *block)
            return pltpu.make_async_remote_copy(
                src_ref=dst if src is None else src, dst_ref=dst,
                send_sem=send_sems.at[7 * t + k], recv_sem=recv_sems.at[7 * t + k],
                device_id=to, device_id_type=MESH)

        started = []
        for t in range(n):
            mine = pltpu.make_async_copy(x_refs[t], slot(o_refs[t], *me), local_sems.at[t])
            mine.start()
            started.append(mine)
        first = []
        for t in range(n):
            first.append(copy(t, 0, me, sibling, src=x_refs[t]))
            first += [copy(t, 1 + j, me, (*chip, c), src=x_refs[t]) for j, chip in enumerate(chips)]
        for cp in first:
            cp.start()
        passed = []
        for j, chip in enumerate(chips):
            for t in range(n):
                copy(t, 1 + j, (*chip, c), me).wait_recv()
                fwd = copy(t, 4 + j, (*chip, c), sibling)
                fwd.start()
                passed.append(fwd)
        for t in range(n):
            copy(t, 0, sibling, me).wait_recv()
            for j, chip in enumerate(chips):
                copy(t, 4 + j, (*chip, 1 - c), me).wait_recv()
        for cp in first + passed:
            cp.wait_send()
        for mine in started:
            mine.wait()
        if nh:
            finish_pairs()

    out_shape = ([jax.ShapeDtypeStruct((N_DEV,) + v.shape, v.dtype) for v in xs]
                 + [jax.ShapeDtypeStruct(v.shape, v.dtype) for v in pair_sums])
    return _comm_call(name, body, m, out_shape, list(xs) + list(pair_sums), collective_id, after,
                      n_remote=7 * n + (N_CHIP - 1) * nh, n_local=m)


def _exchange_partials(name, gs, collective_id=None, after=None):
    n = len(gs)

    def body(*refs):
        g_refs, o_refs = refs[:n], refs[n:2 * n]
        send_sems, recv_sems, local_sems = refs[2 * n:]
        if collective_id is not None:
            _handshake_all()
        x, y, c = _mesh_pos()
        me_idx = 4 * x + 2 * y + c
        copies = []
        for t in range(n):
            mine = pltpu.make_async_copy(g_refs[t].at[me_idx], o_refs[t].at[me_idx], local_sems.at[t])
            mine.start()
            copies.append(mine)
        remote = []
        for k in range(1, N_DEV):
            px = 1 - x if k & 4 else x
            py = 1 - y if k & 2 else y
            pc = 1 - c if k & 1 else c
            p_idx = 4 * px + 2 * py + pc
            for t in range(n):
                cp = pltpu.make_async_remote_copy(
                    src_ref=g_refs[t].at[p_idx], dst_ref=o_refs[t].at[me_idx],
                    send_sem=send_sems.at[7 * t + k - 1], recv_sem=recv_sems.at[7 * t + k - 1],
                    device_id=(px, py, pc), device_id_type=MESH)
                cp.start()
                landing = pltpu.make_async_remote_copy(
                    src_ref=g_refs[t].at[p_idx], dst_ref=o_refs[t].at[p_idx],
                    send_sem=send_sems.at[7 * t + k - 1], recv_sem=recv_sems.at[7 * t + k - 1],
                    device_id=(px, py, pc), device_id_type=MESH)
                remote.append((cp, landing))
        for cp, landing in remote:
            landing.wait_recv()
        for cp, landing in remote:
            cp.wait_send()
        for mine in copies:
            mine.wait()

    out_shape = [jax.ShapeDtypeStruct(v.shape, v.dtype) for v in gs]
    return _comm_call(name, body, n, out_shape, gs, collective_id, after)


N_CHIP = N_DEV // 2
PAIR_ADD_BLOCK_ELEMS = 1024 * 1024


def _pair_swap(name, gs, collective_id, after=None):
    n = len(gs)

    def body(*refs):
        g_refs, o_refs = refs[:n], refs[n:2 * n]
        send_sems, recv_sems, _ = refs[2 * n:]
        x, y, c = _mesh_pos()
        sibling = (x, y, 1 - c)
        _handshake([sibling])
        copies = []
        for t in range(n):
            for k in range(N_CHIP):
                copies.append(pltpu.make_async_remote_copy(
                    src_ref=g_refs[t].at[2 * k + 1 - c], dst_ref=o_refs[t].at[k],
                    send_sem=send_sems.at[N_CHIP * t + k], recv_sem=recv_sems.at[N_CHIP * t + k],
                    device_id=sibling, device_id_type=MESH))
        for cp in copies:
            cp.start()
        for cp in copies:
            cp.wait_recv()
        for cp in copies:
            cp.wait_send()

    out_shape = [jax.ShapeDtypeStruct((N_CHIP,) + v.shape[1:], v.dtype) for v in gs]
    return _comm_call(name, body, n, out_shape, gs, collective_id, after, copies=N_CHIP)


def _pair_add(name, g, got):
    _, r, c = g.shape
    tr = r
    if r * c > PAIR_ADD_BLOCK_ELEMS and r % SUBLANES == 0:
        tr = SUBLANES
        while r % (tr * 2) == 0 and tr * 2 * c <= PAIR_ADD_BLOCK_ELEMS:
            tr *= 2

    def body(core_ref, g_ref, got_ref, o_ref):
        o_ref[...] = (g_ref[...].astype(F32) + got_ref[...].astype(F32)).astype(o_ref.dtype)

    grid_spec = pltpu.PrefetchScalarGridSpec(
        num_scalar_prefetch=1, grid=(N_CHIP, r // tr),
        in_specs=[pl.BlockSpec((None, None, tr, c), lambda k, i, core: (k, core[0], i, 0)),
                  pl.BlockSpec((None, tr, c), lambda k, i, core: (k, i, 0))],
        out_specs=pl.BlockSpec((None, tr, c), lambda k, i, core: (k, i, 0)))
    core = lax.axis_index("c").astype(jnp.int32).reshape(1)
    return pl.pallas_call(body, name=name, grid_spec=grid_spec, out_shape=jax.ShapeDtypeStruct((N_CHIP, r, c), g.dtype),
                          compiler_params=_cparams())(core, g.reshape(N_CHIP, 2, r, c), got)


def _chip_copies(h_refs, o_refs, send_sems, recv_sems, local_sems, sem0, local0):
    n = len(h_refs)
    per = N_CHIP - 1
    x, y, c = _mesh_pos()
    others = [(1 - x if k & 2 else x, 1 - y if k & 1 else y) for k in range(1, N_CHIP)]
    my_chip = 2 * x + y
    local = []
    for t in range(n):
        mine = pltpu.make_async_copy(h_refs[t].at[my_chip], o_refs[t].at[my_chip], local_sems.at[local0 + t])
        mine.start()
        local.append(mine)
    remote = []
    for j, (px, py) in enumerate(others):
        chip = 2 * px + py
        for t in range(n):
            sems = dict(send_sem=send_sems.at[sem0 + per * t + j], recv_sem=recv_sems.at[sem0 + per * t + j],
                        device_id=(px, py, c), device_id_type=MESH)
            cp = pltpu.make_async_remote_copy(src_ref=h_refs[t].at[chip], dst_ref=o_refs[t].at[my_chip], **sems)
            cp.start()
            landing = pltpu.make_async_remote_copy(src_ref=h_refs[t].at[chip], dst_ref=o_refs[t].at[chip], **sems)
            remote.append((cp, landing))

    def finish():
        for cp, landing in remote:
            landing.wait_recv()
        for cp, landing in remote:
            cp.wait_send()
        for mine in local:
            mine.wait()

    return finish


def _chip_exchange(name, hs, collective_id, after=None):
    n = len(hs)
    per = N_CHIP - 1

    def body(*refs):
        h_refs, o_refs = refs[:n], refs[n:2 * n]
        send_sems, recv_sems, local_sems = refs[2 * n:]
        x, y, c = _mesh_pos()
        _handshake([(1 - x if k & 2 else x, 1 - y if k & 1 else y, c) for k in range(1, N_CHIP)])
        _chip_copies(h_refs, o_refs, send_sems, recv_sems, local_sems, 0, 0)()

    out_shape = [jax.ShapeDtypeStruct(v.shape, v.dtype) for v in hs]
    return _comm_call(name, body, n, out_shape, hs, collective_id, after, copies=per)


ADAM_BLOCK_ELEMS = 256 * 1024


def _sum_parts(pb):
    g = pb[0].astype(F32)
    for j in range(1, pb.shape[0]):
        g = g + pb[j].astype(F32)
    return g


def _adam_math(g, wb_, mb, vb):
    m_new = ADAM_B1 * mb + (1.0 - ADAM_B1) * g
    v_new = ADAM_B2 * vb + (1.0 - ADAM_B2) * (g * g)
    m_hat = m_new / (1.0 - ADAM_B1 ** ADAM_STEP)
    v_hat = v_new / (1.0 - ADAM_B2 ** ADAM_STEP)
    delta = -ADAM_LR * (m_hat / (jnp.sqrt(v_hat) + ADAM_EPS) + ADAM_WD * wb_)
    return g, delta, m_new, v_new


def _adamw_multi(name, items, nblk=1, packed=None):
    n = len(items)

    def spec(shape, lead):
        blk = list(shape)
        blk[lead + 1] = shape[lead + 1] // nblk
        if nblk == 1:
            return pl.BlockSpec(tuple(blk), lambda i, nd=len(shape): (0,) * nd)
        return pl.BlockSpec(tuple(blk), lambda i, nd=len(shape), ax=lead + 1: (0,) * ax + (i,) + (0,) * (nd - ax - 1))

    ins, in_specs, out_specs, out_shape, where = [], [], [], [], []
    if packed is not None:
        ins.append(packed)
        in_specs.append(spec(packed.shape, 1))
    for parts, wv, mv, vv in items:
        if isinstance(parts, int):
            where.append((0, parts, len(ins)))
        else:
            assert parts.shape[1:] == wv.shape, (name, parts.shape, wv.shape)
            where.append((len(ins), None, len(ins) + 1))
            ins.append(parts)
            in_specs.append(spec(parts.shape, 1))
        ins += [wv, mv, vv]
        in_specs += [spec(wv.shape, 0)] * 3
        out_specs += [spec(wv.shape, 0)] * 4
        out_shape += [jax.ShapeDtypeStruct(wv.shape, F32)] * 4
    n_in = len(ins)

    def body(*refs):
        for t, (ip, off, iw) in enumerate(where):
            wr, mr, vr = refs[iw:iw + 3]
            parts = refs[ip][...] if off is None else refs[ip][:, :, off:off + wr.shape[-1]]
            res = _adam_math(_sum_parts(parts), wr[...], mr[...], vr[...])
            for o, val in zip(refs[n_in + 4 * t:n_in + 4 * t + 4], res):
                o[...] = val

    res = pl.pallas_call(body, name=name, grid=(nblk,), in_specs=in_specs, out_specs=out_specs, out_shape=out_shape,
                         compiler_params=_cparams())(*ins)
    return [tuple(res[4 * t:4 * t + 4]) for t in range(n)]


def _sum_multi(name, parts_list):
    def body(*refs):
        for pr, o in zip(refs[:len(parts_list)], refs[len(parts_list):]):
            o[...] = _sum_parts(pr[...])

    return pl.pallas_call(body, name=name, out_shape=[jax.ShapeDtypeStruct(p.shape[1:], F32) for p in parts_list],
                          compiler_params=_cparams())(*parts_list)


def _adamw_sum(name, parts, wv, mv, vv):
    npart, r, c = parts.shape
    tr = r
    if r * c > ADAM_BLOCK_ELEMS and r % SUBLANES == 0:
        tr = SUBLANES
        while r % (tr * 2) == 0 and tr * 2 * c <= ADAM_BLOCK_ELEMS:
            tr *= 2

    def fn(pb, wb_, mb, vb):
        return _adam_math(_sum_parts(pb), wb_, mb, vb)

    row = pl.BlockSpec((tr, c), lambda i: (i, 0))
    return _blockwise(name, fn, [parts, wv, mv, vv],
                      [pl.BlockSpec((npart, tr, c), lambda i: (0, i, 0)), row, row, row],
                      [((r, c), F32)] * 4, [row] * 4, (r // tr,))


_VECTORS = ["attn_norm", "lam_re", "lam_im", "log_dt", "ssm_d", "b_glu", "q_norm", "kv_norm", "son", "mon",
            "ffn_norm", "conv_b", "final_norm"]
_GHP = ["c_re", "c_im", "bt_re", "bt_im"]
_PACKED = ["attn_norm", "ssm_d", "b_glu", "q_norm", "kv_norm", "son", "mon", "ffn_norm", "conv_b", "final_norm"]
_BIG = ["win", "wglu", "wuq", "wukv", "wout", "wup", "wdown", "conv_w"]
_ROWS_IN_LANES = ("win", "wuq")
_TWO_LEVEL = ("wup", "win")
_AFTER = "_pair_sums_after"
_ORDER = ["attn_norm", "win", "lam_re", "lam_im", "log_dt", "b_re", "b_im", "c_re", "c_im", "ssm_d", "wglu",
          "b_glu", "q_norm", "wuq", "kv_norm", "wukv", "son", "mon", "wout", "ffn_norm", "wup", "conv_w",
          "conv_b", "wdown", "final_norm"]


def kernel(x, positions, attn_norm_w, w_in, ssm_lambda_re, ssm_lambda_im, ssm_log_dt, ssm_b_re, ssm_b_im, ssm_c_re, ssm_c_im, ssm_d, ssm_w_glu, ssm_b_glu, mla_q_norm_w, mla_w_uq, mla_kv_norm_w, mla_w_ukv, ssm_out_norm_w, mla_out_norm_w, w_out, ffn_norm_w, ffn_w_up, ffn_conv_w, ffn_conv_b, ffn_w_down, final_norm_w, loss_target, m_attn_norm_w, m_w_in, m_ssm_lambda_re, m_ssm_lambda_im, m_ssm_log_dt, m_ssm_b_re, m_ssm_b_im, m_ssm_c_re, m_ssm_c_im, m_ssm_d, m_ssm_w_glu, m_ssm_b_glu, m_mla_q_norm_w, m_mla_w_uq, m_mla_kv_norm_w, m_mla_w_ukv, m_ssm_out_norm_w, m_mla_out_norm_w, m_w_out, m_ffn_norm_w, m_ffn_w_up, m_ffn_conv_w, m_ffn_conv_b, m_ffn_w_down, m_final_norm_w, v_attn_norm_w, v_w_in, v_ssm_lambda_re, v_ssm_lambda_im, v_ssm_log_dt, v_ssm_b_re, v_ssm_b_im, v_ssm_c_re, v_ssm_c_im, v_ssm_d, v_ssm_w_glu, v_ssm_b_glu, v_mla_q_norm_w, v_mla_w_uq, v_mla_kv_norm_w, v_mla_w_ukv, v_ssm_out_norm_w, v_mla_out_norm_w, v_w_out, v_ffn_norm_w, v_ffn_w_up, v_ffn_conv_w, v_ffn_conv_b, v_ffn_w_down, v_final_norm_w):
    wts = dict(attn_norm=attn_norm_w, win=w_in, lam_re=ssm_lambda_re, lam_im=ssm_lambda_im, log_dt=ssm_log_dt,
               b_re=ssm_b_re, b_im=ssm_b_im, c_re=ssm_c_re, c_im=ssm_c_im, ssm_d=ssm_d, wglu=ssm_w_glu,
               b_glu=ssm_b_glu, q_norm=mla_q_norm_w, wuq=mla_w_uq, kv_norm=mla_kv_norm_w, wukv=mla_w_ukv,
               son=ssm_out_norm_w, mon=mla_out_norm_w, wout=w_out, ffn_norm=ffn_norm_w, wup=ffn_w_up,
               conv_w=ffn_conv_w, conv_b=ffn_conv_b, wdown=ffn_w_down, final_norm=final_norm_w)
    moms = dict(zip(_ORDER, [m_attn_norm_w, m_w_in, m_ssm_lambda_re, m_ssm_lambda_im, m_ssm_log_dt, m_ssm_b_re,
                             m_ssm_b_im, m_ssm_c_re, m_ssm_c_im, m_ssm_d, m_ssm_w_glu, m_ssm_b_glu, m_mla_q_norm_w,
                             m_mla_w_uq, m_mla_kv_norm_w, m_mla_w_ukv, m_ssm_out_norm_w, m_mla_out_norm_w, m_w_out,
                             m_ffn_norm_w, m_ffn_w_up, m_ffn_conv_w, m_ffn_conv_b, m_ffn_w_down, m_final_norm_w]))
    vels = dict(zip(_ORDER, [v_attn_norm_w, v_w_in, v_ssm_lambda_re, v_ssm_lambda_im, v_ssm_log_dt, v_ssm_b_re,
                             v_ssm_b_im, v_ssm_c_re, v_ssm_c_im, v_ssm_d, v_ssm_w_glu, v_ssm_b_glu, v_mla_q_norm_w,
                             v_mla_w_uq, v_mla_kv_norm_w, v_mla_w_ukv, v_ssm_out_norm_w, v_mla_out_norm_w, v_w_out,
                             v_ffn_norm_w, v_ffn_w_up, v_ffn_conv_w, v_ffn_conv_b, v_ffn_w_down, v_final_norm_w]))
    seq, d = x.shape[1], x.shape[2]
    in_width = w_in.shape[2]
    in_pad = -(-in_width // LANES) * LANES
    q_cols = mla_w_uq.shape[2]
    q_pad = 2 * LANES

    (win_g,) = _all_gather("gather_w_in", [jnp.pad(w_in[0], ((0, 0), (0, in_pad - in_width))).astype(BF16)])
    wglu_g, wuq_g, wukv_g, wout_g, convw_g = _all_gather(
        "gather_mix", [ssm_w_glu[0].astype(BF16), jnp.pad(mla_w_uq[0], ((0, 0), (0, q_pad - q_cols))).astype(BF16),
                       mla_w_ukv[0].astype(BF16), w_out[0].astype(BF16), ffn_conv_w[0]], collective_id=0)
    (wup_g,) = _all_gather("gather_ffn_up", [ffn_w_up[0].astype(BF16)], collective_id=1)
    (wdown_g,) = _all_gather("gather_ffn_down", [ffn_w_down[0].astype(BF16)], collective_id=2)
    ns = N_DEV
    c_ff = wup_g.shape[2]
    w = dict(
        attn_norm=attn_norm_w, win=win_g.reshape(d, in_pad), lam_re=ssm_lambda_re, lam_im=ssm_lambda_im,
        log_dt=ssm_log_dt, b_re=ssm_b_re, b_im=ssm_b_im, c_re=ssm_c_re, c_im=ssm_c_im, ssm_d=ssm_d,
        wglu=wglu_g.reshape(d // 2, d // 2), b_glu=ssm_b_glu, q_norm=mla_q_norm_w, wuq=wuq_g,
        kv_norm=mla_kv_norm_w, wukv=wukv_g, son=ssm_out_norm_w, mon=mla_out_norm_w, wout=wout_g.reshape(d, d),
        ffn_norm=ffn_norm_w, wup=wup_g, conv_w=convw_g, conv_b=ffn_conv_b,
        wdown=wdown_g.reshape(ns // 2 * c_ff, d), final_norm=final_norm_w)

    shard_layout = dict(
        win=lambda a: a[:, :in_width].reshape(N_DEV, d // N_DEV, in_width),
        wglu=lambda a: a.reshape(N_DEV, d // 2 // N_DEV, d // 2),
        wuq=lambda a: a[:, :, :q_cols], wukv=lambda a: a, wout=lambda a: a.reshape(N_DEV, d // N_DEV, d),
        wup=lambda a: a, wdown=lambda a: a.reshape(N_DEV, c_ff // 2, d), conv_w=lambda a: a)
    recv = {}
    next_id = [3]

    last = [None]

    out = {}

    def update(k):
        shp = wts[k].shape
        r, c = shp[-2], shp[-1]
        if k in _ROWS_IN_LANES:
            t = lambda a: jnp.swapaxes(a.reshape(-1, r, c), 1, 2)
            res = _adamw_sum("adamw_" + k, t(recv[k]), t(wts[k])[0], t(moms[k])[0], t(vels[k])[0])
            out[k] = [jnp.swapaxes(a, 0, 1).reshape(shp) for a in res]
            return res[0]
        res = _adamw_sum("adamw_" + k, recv[k].reshape(-1, r, c), wts[k].reshape(r, c),
                         moms[k].reshape(r, c), vels[k].reshape(r, c))
        out[k] = [a.reshape(shp) for a in res]
        return res[0]

    pending = {}

    def exchange(not_before=(), **grads):
        names = list(grads)
        if len(names) == 1 and names[0] in _TWO_LEVEL:
            k = names[0]
            parts = shard_layout[k](grads[k])
            got = _pair_swap("swap_" + k, [parts], collective_id=next_id[0], after=[last[0]])[0]
            next_id[0] += 1
            pending[k] = (parts, got)
            last[0] = got
            return
        if len(names) == 1 and names[0].endswith(_AFTER):
            k = names[0][:-len(_AFTER)]
            sums = _pair_add("pair_add_" + k, *pending[k])
            if k == "win":
                pending["tail"] = sums
                return
            recv[k] = _chip_exchange("exchange_" + k, [sums], collective_id=next_id[0],
                                     after=[last[0], grads[names[0]]])[0]
            next_id[0] += 1
            last[0] = recv[k]
            return
        got = _exchange_partials("exchange_" + "_".join(names), [shard_layout[k](grads[k]) for k in names],
                                 collective_id=next_id[0], after=[a for a in (last[0], *not_before) if a is not None])
        next_id[0] += 1
        last[0] = got[-1]
        recv.update(zip(names, got))

    loss_part, grad_x, g = _local_step(x[0], positions[0], loss_target[0], w, emit=exchange)
    n_groups = ssm_lambda_re.shape[1]
    two_d = {"lam_re": (n_groups, -1), "lam_im": (n_groups, -1)}
    dense = {k: g[k].reshape(two_d.get(k, (1, -1))) for k in _VECTORS}
    offsets, width = {}, 0
    for k in _PACKED:
        offsets[k] = width
        width += dense[k].shape[1]
    sent = dict(packed=jnp.concatenate([dense[k] for k in _PACKED], axis=1),
                **{k: dense[k] for k in _VECTORS if k not in _PACKED},
                **{k: g[k].reshape(n_groups, -1).astype(BF16) for k in _GHP},
                loss=loss_part)
    names = list(sent)
    got = _all_gather("gather_small_grads", [sent[k] for k in names], collective_id=next_id[0], after=[last[0]],
                      pair_sums=[pending["tail"]])
    gathered = dict(zip(names, got))
    recv["win"] = got[len(names)]
    for k in _BIG:
        if k not in out and k != "win":
            update(k)
    update("win")

    def finish(keys, results):
        for k, res in zip(keys, results):
            out[k] = [a.reshape(wts[k].shape) for a in res]

    view = lambda k, a: a.reshape(dense[k].shape)
    finish(_VECTORS, _adamw_multi("adamw_vectors", [(offsets.get(k, gathered.get(k)), view(k, wts[k]), view(k, moms[k]),
                                                     view(k, vels[k])) for k in _VECTORS], packed=gathered["packed"]))
    summed = _GHP + ["loss"]
    sums = dict(zip(summed, _sum_multi("sum_ssm_bc_loss", [gathered[k] for k in summed])))
    loss = sums["loss"][0, 0]
    ghp = lambda k: sums[k].reshape(g[k].shape)
    t_hp = lambda a: jnp.swapaxes(a, 2, 3)
    bc_keys = ["c_re", "c_im", "b_re", "b_im"]
    items = [(ghp(k)[None, None], wts[k], moms[k], vels[k]) for k in bc_keys[:2]]
    items += [(ghp(t)[None, None], t_hp(wts[k]), t_hp(moms[k]), t_hp(vels[k]))
              for k, t in zip(bc_keys[2:], ("bt_re", "bt_im"))]
    res = _adamw_multi("adamw_ssm_bc", items)
    finish(bc_keys, res[:2] + [tuple(t_hp(a) for a in r) for r in res[2:]])

    grad_x = grad_x.reshape(x.shape)
    return (loss, grad_x, *[out[k][0] for k in _ORDER], *[out[k][1] for k in _ORDER],
            *[out[k][2] for k in _ORDER], *[out[k][3] for k in _ORDER])
```
